```python
import jax, jax.numpy as jnp
from jax import lax
import numpy as np

D_MODEL = 1024
BATCH = 8
SEQ = 4096
DEPTH = 1

HEAD_DIM = 64
MIX_WIDTH = D_MODEL
ATTN_WIDTH = MIX_WIDTH // 2
SGU_WIDTH = MIX_WIDTH - ATTN_WIDTH
N_Q_HEADS = ATTN_WIDTH // HEAD_DIM
N_KV_HEADS = 2
Q_PER_KV = N_Q_HEADS // N_KV_HEADS
KV_WIDTH = N_KV_HEADS * HEAD_DIM
N_SGU_HEADS = 8
SGU_HEAD_DIM = SGU_WIDTH // N_SGU_HEADS
WINDOW = 128
BLOCK = 128
CHUNK = 128
NORM_EPS = 1e-5
NEG_INF = -1e30
SPLIT_SIZES = (ATTN_WIDTH, KV_WIDTH, KV_WIDTH, ATTN_WIDTH, SGU_WIDTH, SGU_WIDTH, SGU_WIDTH)
IN_WIDTH = sum(SPLIT_SIZES)

kernel_name = "hybrid_swa_sink_gmlp_parallel_heads"


def rmsnorm(x, g):
    xf = x.astype(jnp.float32)
    y = xf * lax.rsqrt(jnp.mean(xf * xf, axis=-1, keepdims=True) + NORM_EPS)
    return (y * g.astype(jnp.float32)).astype(x.dtype)


def layernorm(x, g, b):
    xf = x.astype(jnp.float32)
    mu = jnp.mean(xf, axis=-1, keepdims=True)
    xc = xf - mu
    y = xc * lax.rsqrt(jnp.mean(xc * xc, axis=-1, keepdims=True) + NORM_EPS)
    return (y * g.astype(jnp.float32) + b.astype(jnp.float32)).astype(x.dtype)


def banded_sink_attention(q, k, v, sinks):
    B, S = q.shape[0], q.shape[1]
    nb = S // BLOCK
    qb = q.reshape(B, nb, BLOCK, N_KV_HEADS, Q_PER_KV, HEAD_DIM)

    def band(t):
        tb = t.reshape(B, nb, BLOCK, N_KV_HEADS, HEAD_DIM)
        prev = jnp.pad(tb, ((0, 0), (1, 0), (0, 0), (0, 0), (0, 0)))[:, :-1]
        return jnp.concatenate([prev, tb], axis=2)

    kb, vb = band(k), band(v)
    scale = HEAD_DIM ** -0.5
    scores = jnp.einsum('bnqhgd,bnkhd->bnhgqk', qb, kb).astype(jnp.float32) * scale
    qi = jnp.arange(BLOCK)[:, None] + BLOCK
    kj = jnp.arange(2 * BLOCK)[None, :]
    diff = qi - kj
    in_window = (diff >= 0) & (diff < WINDOW)
    key_pos = jnp.arange(nb)[:, None, None] * BLOCK - BLOCK + kj[None]
    valid = in_window[None] & (key_pos >= 0)
    scores = jnp.where(valid[None, :, None, None], scores, NEG_INF)
    sink = sinks.astype(jnp.float32).reshape(N_KV_HEADS, Q_PER_KV)[None, None, :, :, None, None]
    m = jnp.maximum(jnp.max(scores, axis=-1, keepdims=True), sink)
    p = jnp.exp(scores - m)
    probs = p / (jnp.sum(p, axis=-1, keepdims=True) + jnp.exp(sink - m))
    out = jnp.einsum('bnhgqk,bnkhd->bnqhgd', probs.astype(vb.dtype), vb)
    return out.reshape(B, S, ATTN_WIDTH)


def chunked_spatial_gating(u, v, w_s, b_s, ln_g, ln_b):
    B, S = u.shape[0], u.shape[1]
    nc = S // CHUNK
    v = layernorm(v, ln_g, ln_b)
    vc = v.reshape(B, nc, CHUNK, N_SGU_HEADS, SGU_HEAD_DIM)
    causal = jnp.tril(jnp.ones((CHUNK, CHUNK), dtype=bool))
    w = jnp.where(causal[None], w_s, jnp.zeros_like(w_s)).astype(vc.dtype)
    mixed = jnp.einsum('hts,bcshd->bcthd', w, vc) + b_s.T.astype(vc.dtype)[None, None, :, :, None]
    return u * mixed.reshape(B, S, SGU_WIDTH)


def _fwd_setup_inputs(seed: int = 0) -> dict:
    key = jax.random.key(seed)
    ks = jax.random.split(key, 12)
    f32 = jnp.float32
    x = jax.random.normal(ks[0], (BATCH, SEQ, D_MODEL), f32)
    norm_g = 1.0 + 0.02 * jax.random.normal(ks[1], (DEPTH, D_MODEL), f32)
    w_in = jax.random.normal(ks[2], (DEPTH, D_MODEL, IN_WIDTH), f32) * D_MODEL ** -0.5
    b_in = 0.02 * jax.random.normal(ks[3], (DEPTH, IN_WIDTH), f32)
    attn_sinks = 0.5 * jax.random.normal(ks[4], (DEPTH, N_Q_HEADS), f32)
    sgu_ln_g = 1.0 + 0.02 * jax.random.normal(ks[5], (DEPTH, SGU_WIDTH), f32)
    sgu_ln_b = 0.02 * jax.random.normal(ks[6], (DEPTH, SGU_WIDTH), f32)
    sgu_w = jax.random.normal(ks[7], (DEPTH, N_SGU_HEADS, CHUNK, CHUNK), f32) * CHUNK ** -0.5
    sgu_b = 1.0 + 0.02 * jax.random.normal(ks[8], (DEPTH, N_SGU_HEADS, CHUNK), f32)
    w_out = jax.random.normal(ks[9], (DEPTH, MIX_WIDTH, D_MODEL), f32) * MIX_WIDTH ** -0.5
    b_out = 0.02 * jax.random.normal(ks[10], (DEPTH, D_MODEL), f32)
    final_norm_g = 1.0 + 0.02 * jax.random.normal(ks[11], (D_MODEL,), f32)
    return {"x": x, "norm_g": norm_g, "w_in": w_in, "b_in": b_in,
            "attn_sinks": attn_sinks, "sgu_ln_g": sgu_ln_g, "sgu_ln_b": sgu_ln_b,
            "sgu_w": sgu_w, "sgu_b": sgu_b, "w_out": w_out, "b_out": b_out,
            "final_norm_g": final_norm_g}


def _fwd_reference(x, norm_g, w_in, b_in, attn_sinks, sgu_ln_g, sgu_ln_b, sgu_w, sgu_b,
              w_out, b_out, final_norm_g):
    split_points = list(np.cumsum(SPLIT_SIZES)[:-1])
    for l in range(DEPTH):
        h = rmsnorm(x, norm_g[l])
        proj = jnp.einsum('bsd,de->bse', h, w_in[l]) + b_in[l]
        q, k, v, z_a, u_s, v_s, z_s = jnp.split(proj, split_points, axis=-1)
        attn = banded_sink_attention(q, k, v, attn_sinks[l]) * jax.nn.silu(z_a)
        u_s = jax.nn.gelu(u_s, approximate=False)
        v_s = jax.nn.gelu(v_s, approximate=False)
        sgu = chunked_spatial_gating(u_s, v_s, sgu_w[l], sgu_b[l], sgu_ln_g[l], sgu_ln_b[l]) * jax.nn.silu(z_s)
        mixed = jnp.concatenate([attn, sgu], axis=-1)
        x = x + jnp.einsum('bse,ed->bsd', mixed, w_out[l]) + b_out[l]
    return rmsnorm(x, final_norm_g)


import jax as _jax
import jax.numpy as _jnp

TWIN_FORMAT = 'train_step'
FWD_PARAMS = ['x', 'norm_g', 'w_in', 'b_in', 'attn_sinks', 'sgu_ln_g', 'sgu_ln_b', 'sgu_w', 'sgu_b', 'w_out', 'b_out', 'final_norm_g']
TWIN_WEIGHTS = ['norm_g', 'w_in', 'b_in', 'attn_sinks', 'sgu_ln_g', 'sgu_ln_b', 'sgu_w', 'sgu_b', 'w_out', 'b_out', 'final_norm_g']
TWIN_DIFF_INPUT = 'x'
TWIN_INPUTS = ['x', 'norm_g', 'w_in', 'b_in', 'attn_sinks', 'sgu_ln_g', 'sgu_ln_b', 'sgu_w', 'sgu_b', 'w_out', 'b_out', 'final_norm_g', 'loss_target', 'm_norm_g', 'm_w_in', 'm_b_in', 'm_attn_sinks', 'm_sgu_ln_g', 'm_sgu_ln_b', 'm_sgu_w', 'm_sgu_b', 'm_w_out', 'm_b_out', 'm_final_norm_g', 'v_norm_g', 'v_w_in', 'v_b_in', 'v_attn_sinks', 'v_sgu_ln_g', 'v_sgu_ln_b', 'v_sgu_w', 'v_sgu_b', 'v_w_out', 'v_b_out', 'v_final_norm_g']
TWIN_OUTPUTS = ['loss', 'grad_x', 'grad_norm_g', 'grad_w_in', 'grad_b_in', 'grad_attn_sinks', 'grad_sgu_ln_g', 'grad_sgu_ln_b', 'grad_sgu_w', 'grad_sgu_b', 'grad_w_out', 'grad_b_out', 'grad_final_norm_g', 'delta_norm_g', 'delta_w_in', 'delta_b_in', 'delta_attn_sinks', 'delta_sgu_ln_g', 'delta_sgu_ln_b', 'delta_sgu_w', 'delta_sgu_b', 'delta_w_out', 'delta_b_out', 'delta_final_norm_g', 'new_m_norm_g', 'new_m_w_in', 'new_m_b_in', 'new_m_attn_sinks', 'new_m_sgu_ln_g', 'new_m_sgu_ln_b', 'new_m_sgu_w', 'new_m_sgu_b', 'new_m_w_out', 'new_m_b_out', 'new_m_final_norm_g', 'new_v_norm_g', 'new_v_w_in', 'new_v_b_in', 'new_v_attn_sinks', 'new_v_sgu_ln_g', 'new_v_sgu_ln_b', 'new_v_sgu_w', 'new_v_sgu_b', 'new_v_w_out', 'new_v_b_out', 'new_v_final_norm_g']
TWIN_LEAF_KINDS = {'loss': 'loss', 'grad_x': 'grad_x', 'grad_norm_g': 'grad_w', 'grad_w_in': 'grad_w', 'grad_b_in': 'grad_w', 'grad_attn_sinks': 'grad_w', 'grad_sgu_ln_g': 'grad_w', 'grad_sgu_ln_b': 'grad_w', 'grad_sgu_w': 'grad_w', 'grad_sgu_b': 'grad_w', 'grad_w_out': 'grad_w', 'grad_b_out': 'grad_w', 'grad_final_norm_g': 'grad_w', 'delta_norm_g': 'delta_w', 'delta_w_in': 'delta_w', 'delta_b_in': 'delta_w', 'delta_attn_sinks': 'delta_w', 'delta_sgu_ln_g': 'delta_w', 'delta_sgu_ln_b': 'delta_w', 'delta_sgu_w': 'delta_w', 'delta_sgu_b': 'delta_w', 'delta_w_out': 'delta_w', 'delta_b_out': 'delta_w', 'delta_final_norm_g': 'delta_w', 'new_m_norm_g': 'new_m', 'new_m_w_in': 'new_m', 'new_m_b_in': 'new_m', 'new_m_attn_sinks': 'new_m', 'new_m_sgu_ln_g': 'new_m', 'new_m_sgu_ln_b': 'new_m', 'new_m_sgu_w': 'new_m', 'new_m_sgu_b': 'new_m', 'new_m_w_out': 'new_m', 'new_m_b_out': 'new_m', 'new_m_final_norm_g': 'new_m', 'new_v_norm_g': 'new_v', 'new_v_w_in': 'new_v', 'new_v_b_in': 'new_v', 'new_v_attn_sinks': 'new_v', 'new_v_sgu_ln_g': 'new_v', 'new_v_sgu_ln_b': 'new_v', 'new_v_sgu_w': 'new_v', 'new_v_sgu_b': 'new_v', 'new_v_w_out': 'new_v', 'new_v_b_out': 'new_v', 'new_v_final_norm_g': 'new_v'}


def _forward(args):
    return _fwd_reference(*[args[k] for k in FWD_PARAMS])


def _output_shape():
    out = _jax.eval_shape(lambda: _forward(_fwd_setup_inputs(0)))
    return out.shape, out.dtype

N_MICROBATCH = 1
ADAM_LR = 0.001
ADAM_B1 = 0.9
ADAM_B2 = 0.999
ADAM_EPS = 1e-08
ADAM_WD = 0.01
ADAM_STEP = 10
PER_EXAMPLE_BATCH_AXIS = {'x': 0, 'loss_target': 0}
SHARED_INPUTS = []
_WEIGHT_DTYPES = {'norm_g': _jnp.float32, 'w_in': _jnp.float32, 'b_in': _jnp.float32, 'attn_sinks': _jnp.float32, 'sgu_ln_g': _jnp.float32, 'sgu_ln_b': _jnp.float32, 'sgu_w': _jnp.float32, 'sgu_b': _jnp.float32, 'w_out': _jnp.float32, 'b_out': _jnp.float32, 'final_norm_g': _jnp.float32}
MOMENT_SCALE = {'norm_g': 9.428363e-02, 'w_in': 5.660754e-02, 'b_in': 6.819831e-02, 'attn_sinks': 1.719500e-02, 'sgu_ln_g': 5.187110e-02, 'sgu_ln_b': 5.078739e-02, 'sgu_w': 3.231549e-02, 'sgu_b': 4.291258e-02, 'w_out': 5.749093e-02, 'b_out': 1.694945e-01, 'final_norm_g': 3.200414e+01}


def _to_microbatches(a, axis):
    t = _jnp.moveaxis(a, axis, 0)
    t = t.reshape((N_MICROBATCH, t.shape[0] // N_MICROBATCH) + t.shape[1:])
    return _jnp.moveaxis(t, 1, axis + 1)


def setup_inputs(seed: int = 0) -> dict:
    inp = _fwd_setup_inputs(seed)
    key = _jax.random.fold_in(_jax.random.key(seed), 7919)
    shape, _ = _output_shape()
    out = dict(inp)
    out["loss_target"] = _jax.random.normal(_jax.random.fold_in(key, 0), shape, _jnp.float32)
    for i, name in enumerate(TWIN_WEIGHTS):
        w = inp[name].astype(_jnp.float32)
        if MOMENT_SCALE is None:
            s = _jnp.sqrt(_jnp.mean(_jnp.square(w)) + 1e-30)
        else:
            s = MOMENT_SCALE[name]
        km, kv = _jax.random.split(_jax.random.fold_in(key, i + 1))
        out[name] = w
        out["m_" + name] = s * _jax.random.normal(km, w.shape, _jnp.float32)
        out["v_" + name] = (s * s) * _jax.random.uniform(kv, w.shape, _jnp.float32, 0.5, 1.5)
    if N_MICROBATCH > 1:
        for name, axis in PER_EXAMPLE_BATCH_AXIS.items():
            out[name] = _to_microbatches(out[name], axis)
    return {'x': out['x'], 'norm_g': out['norm_g'], 'w_in': out['w_in'], 'b_in': out['b_in'], 'attn_sinks': out['attn_sinks'], 'sgu_ln_g': out['sgu_ln_g'], 'sgu_ln_b': out['sgu_ln_b'], 'sgu_w': out['sgu_w'], 'sgu_b': out['sgu_b'], 'w_out': out['w_out'], 'b_out': out['b_out'], 'final_norm_g': out['final_norm_g'], 'loss_target': out['loss_target'], 'm_norm_g': out['m_norm_g'], 'm_w_in': out['m_w_in'], 'm_b_in': out['m_b_in'], 'm_attn_sinks': out['m_attn_sinks'], 'm_sgu_ln_g': out['m_sgu_ln_g'], 'm_sgu_ln_b': out['m_sgu_ln_b'], 'm_sgu_w': out['m_sgu_w'], 'm_sgu_b': out['m_sgu_b'], 'm_w_out': out['m_w_out'], 'm_b_out': out['m_b_out'], 'm_final_norm_g': out['m_final_norm_g'], 'v_norm_g': out['v_norm_g'], 'v_w_in': out['v_w_in'], 'v_b_in': out['v_b_in'], 'v_attn_sinks': out['v_attn_sinks'], 'v_sgu_ln_g': out['v_sgu_ln_g'], 'v_sgu_ln_b': out['v_sgu_ln_b'], 'v_sgu_w': out['v_sgu_w'], 'v_sgu_b': out['v_sgu_b'], 'v_w_out': out['v_w_out'], 'v_b_out': out['v_b_out'], 'v_final_norm_g': out['v_final_norm_g']}


def _loss(weights, diff, rest, loss_target):
    with _jax.named_scope("forward"):
        args = {**rest, TWIN_DIFF_INPUT: diff, **{k: w.astype(_WEIGHT_DTYPES[k]) for k, w in weights.items()}}
        y = _forward(args)
    with _jax.named_scope("loss_head"):
        err = _jnp.square(y.astype(_jnp.float32) - loss_target)
        return 0.5 * _jnp.sum(_jnp.mean(err, axis=-1)) if err.ndim else 0.5 * err


def _adamw(w, g, m, v):
    m = ADAM_B1 * m + (1.0 - ADAM_B1) * g
    v = ADAM_B2 * v + (1.0 - ADAM_B2) * _jnp.square(g)
    m_hat = m / (1.0 - ADAM_B1 ** ADAM_STEP)
    v_hat = v / (1.0 - ADAM_B2 ** ADAM_STEP)
    delta = -ADAM_LR * (m_hat / (_jnp.sqrt(v_hat) + ADAM_EPS) + ADAM_WD * w)
    return delta, m, v


def reference(x, norm_g, w_in, b_in, attn_sinks, sgu_ln_g, sgu_ln_b, sgu_w, sgu_b, w_out, b_out, final_norm_g, loss_target, m_norm_g, m_w_in, m_b_in, m_attn_sinks, m_sgu_ln_g, m_sgu_ln_b, m_sgu_w, m_sgu_b, m_w_out, m_b_out, m_final_norm_g, v_norm_g, v_w_in, v_b_in, v_attn_sinks, v_sgu_ln_g, v_sgu_ln_b, v_sgu_w, v_sgu_b, v_w_out, v_b_out, v_final_norm_g):
    given = dict(x=x, norm_g=norm_g, w_in=w_in, b_in=b_in, attn_sinks=attn_sinks, sgu_ln_g=sgu_ln_g, sgu_ln_b=sgu_ln_b, sgu_w=sgu_w, sgu_b=sgu_b, w_out=w_out, b_out=b_out, final_norm_g=final_norm_g, loss_target=loss_target, m_norm_g=m_norm_g, m_w_in=m_w_in, m_b_in=m_b_in, m_attn_sinks=m_attn_sinks, m_sgu_ln_g=m_sgu_ln_g, m_sgu_ln_b=m_sgu_ln_b, m_sgu_w=m_sgu_w, m_sgu_b=m_sgu_b, m_w_out=m_w_out, m_b_out=m_b_out, m_final_norm_g=m_final_norm_g, v_norm_g=v_norm_g, v_w_in=v_w_in, v_b_in=v_b_in, v_attn_sinks=v_attn_sinks, v_sgu_ln_g=v_sgu_ln_g, v_sgu_ln_b=v_sgu_ln_b, v_sgu_w=v_sgu_w, v_sgu_b=v_sgu_b, v_w_out=v_w_out, v_b_out=v_b_out, v_final_norm_g=v_final_norm_g)
    weights = {n: given[n] for n in TWIN_WEIGHTS}
    shared = {n: given[n] for n in SHARED_INPUTS}
    per_example = {n: given[n] for n in ['x']}
    grad_fn = _jax.value_and_grad(_loss, argnums=(0, 1))

    def one_microbatch(ex, loss_target):
        ex = dict(ex)
        diff = ex.pop(TWIN_DIFF_INPUT)
        return grad_fn(weights, diff, {**shared, **ex}, loss_target)

    if N_MICROBATCH == 1:
        loss, (grad_w, grad_x) = one_microbatch(per_example, given["loss_target"])
    else:
        def body(carry, xs):
            loss_sum, grad_sum = carry
            l_k, (gw_k, gx_k) = one_microbatch(xs[0], xs[1])
            with _jax.named_scope("update"):
                return (loss_sum + l_k, _jax.tree.map(_jnp.add, grad_sum, gw_k)), gx_k

        init = (_jnp.zeros((), _jnp.float32), _jax.tree.map(_jnp.zeros_like, weights))
        (loss, grad_w), grad_x = _jax.lax.scan(body, init, (per_example, given["loss_target"]))
    with _jax.named_scope("update"):
        delta_w, new_m, new_v = {}, {}, {}
        for n in TWIN_WEIGHTS:
            delta_w[n], new_m[n], new_v[n] = _adamw(weights[n], grad_w[n], given["m_" + n], given["v_" + n])
    return (loss, grad_x, *[grad_w[n] for n in TWIN_WEIGHTS], *[delta_w[n] for n in TWIN_WEIGHTS],
            *[new_m[n] for n in TWIN_WEIGHTS], *[new_v[n] for n in TWIN_WEIGHTS])
```

```python
import functools
import math

import jax
import jax.numpy as jnp
from jax import lax
from jax.experimental import pallas as pl
from jax.experimental.pallas import tpu as pltpu

F32 = jnp.float32
BF16 = jnp.bfloat16
MXU_DTYPE = BF16
COMM_DTYPE = BF16

D_MODEL = 1024
SEQ = 4096
HEAD_DIM = 64
N_Q_HEADS = 8
Q_PER_KV = 4
BLOCK = 128
N_BLOCKS = SEQ // BLOCK
ATTN_W = 512
KV_W = 128
SGU_W = 512
N_SGU_HEADS = 8
IN_W = 2816
NORM_EPS = 1e-5
NEG_INF = -1e30
SCALE = HEAD_DIM ** -0.5
KV0 = ATTN_W
GATE0 = ATTN_W + 2 * KV_W
SGU0 = GATE0 + ATTN_W
ATTN_SECTION = SGU0
SGU_SECTION = IN_W - SGU0

ADAM_LR = 0.001
ADAM_B1 = 0.9
ADAM_B2 = 0.999
ADAM_EPS = 1e-08
ADAM_WD = 0.01
ADAM_STEP = 10

N_DEV = 8
WIN_ROWS = IN_W // N_DEV
WOUT_ROWS = D_MODEL // N_DEV
SGUW_ROWS = N_SGU_HEADS * BLOCK // N_DEV
VEC_ROWS = 16
MESH = pl.DeviceIdType.MESH

LANES = 128
HALF = LANES // 2
TOKEN_TILE = 256
VMEM_LIMIT = 56 * 1024 * 1024

NN = (((1,), (0,)), ((), ()))
NT = (((1,), (1,)), ((), ()))
TN = (((0,), (0,)), ((), ()))


def _dot(a, b, dims=NN):
    return lax.dot_general(a.astype(MXU_DTYPE), b.astype(MXU_DTYPE), dims, preferred_element_type=F32)


def _gelu(x):
    return x * (lax.erf(x * (1.0 / math.sqrt(2.0))) + 1.0) * 0.5


def _gelu_grad(x):
    cdf = (lax.erf(x * (1.0 / math.sqrt(2.0))) + 1.0) * 0.5
    return cdf + x * jnp.exp(-0.5 * x * x) * (1.0 / math.sqrt(2.0 * math.pi))


def _silu_and_grad(z):
    s = jax.nn.sigmoid(z)
    return z * s, s * (1.0 + z * (1.0 - s))


def _params(semantics=None, vmem=None):
    kw = {}
    if semantics is not None:
        kw["dimension_semantics"] = semantics
    if vmem is not None:
        kw["vmem_limit_bytes"] = vmem
    return pltpu.CompilerParams(**kw)


def _full(shape):
    return pl.BlockSpec(shape, lambda *_: (0,) * len(shape))


VMEM_SPEC = pl.BlockSpec(memory_space=pltpu.VMEM)


def _place():
    return lax.axis_index("x"), lax.axis_index("y"), lax.axis_index("c")


def _all_gather_weights(win_t_shard, wout_shard):
    def body(win_ref, wout_ref, gwin_ref, gwout_ref, send_sems, recv_sems):
        x, y, c = _place()
        me, sibling = (x, y, c), (x, y, 1 - c)
        chips = [(1 - x, y), (x, 1 - y), (1 - x, 1 - y)]

        def rows(ref, n_rows, place):
            px, py, pc = place
            start = pl.multiple_of((4 * px + 2 * py + pc) * n_rows, 16)
            return ref.at[pl.ds(start, n_rows), :]

        def copies(k, block, to):
            return [
                pltpu.make_async_remote_copy(
                    src_ref=rows(ref, n, block), dst_ref=rows(ref, n, block),
                    send_sem=send_sems.at[k, a], recv_sem=recv_sems.at[k, a],
                    device_id=to, device_id_type=MESH)
                for a, (ref, n) in enumerate(((gwin_ref, WIN_ROWS), (gwout_ref, WOUT_ROWS)))
            ]

        gwin_ref[pl.ds(pl.multiple_of((4 * x + 2 * y + c) * WIN_ROWS, 16), WIN_ROWS), :] = (
            win_ref[...].astype(COMM_DTYPE))
        gwout_ref[pl.ds(pl.multiple_of((4 * x + 2 * y + c) * WOUT_ROWS, 16), WOUT_ROWS), :] = (
            wout_ref[...].astype(COMM_DTYPE))

        first = copies(0, me, sibling)
        for j, chip in enumerate(chips):
            first += copies(1 + j, me, (*chip, c))
        for cp in first:
            cp.start()
        passed = []
        for j, chip in enumerate(chips):
            for cp in copies(1 + j, (*chip, c), me):
                cp.wait_recv()
            fwd = copies(4 + j, (*chip, c), sibling)
            for cp in fwd:
                cp.start()
            passed += fwd
        for cp in copies(0, sibling, me):
            cp.wait_recv()
        for j, chip in enumerate(chips):
            for cp in copies(4 + j, (*chip, 1 - c), me):
                cp.wait_recv()
        for cp in first + passed:
            cp.wait_send()

    return pl.pallas_call(
        body,
        name="all_gather_weights",
        out_shape=(jax.ShapeDtypeStruct((IN_W, D_MODEL), COMM_DTYPE),
                   jax.ShapeDtypeStruct((D_MODEL, D_MODEL), COMM_DTYPE)),
        in_specs=[VMEM_SPEC, VMEM_SPEC],
        out_specs=(VMEM_SPEC, VMEM_SPEC),
        scratch_shapes=[pltpu.SemaphoreType.DMA((7, 2)), pltpu.SemaphoreType.DMA((7, 2))],
        compiler_params=_params(vmem=VMEM_LIMIT),
    )(win_t_shard, wout_shard)


def _in_proj(x, norm_g, b_in, win_t):
    tm = TOKEN_TILE

    def body(x_ref, g_ref, b_ref, w_ref, h_ref, q_ref, kvx_ref, gate_ref):
        xv = x_ref[...]
        r = lax.rsqrt(jnp.mean(xv * xv, axis=-1, keepdims=True) + NORM_EPS)
        h = ((xv * r) * g_ref[...]).astype(MXU_DTYPE)
        h_ref[...] = h

        def proj(lo, hi):
            return _dot(h, w_ref[lo:hi, :], NT) + b_ref[:, lo:hi]

        q_ref[...] = proj(0, ATTN_W).astype(MXU_DTYPE)
        kv = proj(KV0, GATE0)
        low = lax.broadcasted_iota(jnp.int32, (tm, LANES), 1) < HALF
        for i in range(2):
            t = kv[:, i * LANES:(i + 1) * LANES]
            rot = pltpu.roll(t, HALF, 1)
            variants = (jnp.where(low, t, 0.0), jnp.where(low, 0.0, rot),
                        jnp.where(low, rot, 0.0), jnp.where(low, 0.0, t))
            for j, val in enumerate(variants):
                col = (4 * i + j) * LANES
                kvx_ref[:, col:col + LANES] = val.astype(MXU_DTYPE)
        gate_ref[...] = proj(GATE0, IN_W)

    return pl.pallas_call(
        body,
        name="in_proj",
        grid=(SEQ // tm,),
        in_specs=[pl.BlockSpec((tm, D_MODEL), lambda i: (i, 0)),
                  _full((1, D_MODEL)), _full((1, IN_W)), _full((IN_W, D_MODEL))],
        out_specs=(pl.BlockSpec((tm, D_MODEL), lambda i: (i, 0)),
                   pl.BlockSpec((tm, ATTN_W), lambda i: (i, 0)),
                   pl.BlockSpec((tm, 8 * LANES), lambda i: (i, 0)),
                   pl.BlockSpec((tm, IN_W - GATE0), lambda i: (i, 0))),
        out_shape=(jax.ShapeDtypeStruct((SEQ, D_MODEL), MXU_DTYPE),
                   jax.ShapeDtypeStruct((SEQ, ATTN_W), MXU_DTYPE),
                   jax.ShapeDtypeStruct((SEQ, 8 * LANES), MXU_DTYPE),
                   jax.ShapeDtypeStruct((SEQ, IN_W - GATE0), F32)),
        compiler_params=_params(("arbitrary",), VMEM_LIMIT),
    )(x, norm_g, b_in, win_t)


def _kv_variant(h):
    return (h // Q_PER_KV) * 2 + h % 2


def _attn_fwd(sinks, q, kvx, gates):
    def body(sink_ref, q_ref, kc_ref, kp_ref, za_ref, out_ref, lse_ref, ag_ref):
        n = pl.program_id(0)
        row = lax.broadcasted_iota(jnp.int32, (BLOCK, BLOCK), 0)
        lane = lax.broadcasted_iota(jnp.int32, (BLOCK, BLOCK), 1)
        mask_cur = lane <= row
        mask_prev = jnp.logical_and(lane > row, n > 0)
        lse = jnp.zeros((BLOCK, LANES), F32)
        for pair in range(N_Q_HEADS // 2):
            cols = slice(pair * LANES, (pair + 1) * LANES)
            qp = q_ref[:, cols]
            acc = jnp.zeros((BLOCK, LANES), F32)
            for par in range(2):
                h = 2 * pair + par
                kcol = _kv_variant(h) * LANES
                vcol = kcol + 4 * LANES
                s_cur = jnp.where(mask_cur, _dot(qp, kc_ref[:, kcol:kcol + LANES], NT) * SCALE, NEG_INF)
                s_prev = jnp.where(mask_prev, _dot(qp, kp_ref[:, kcol:kcol + LANES], NT) * SCALE, NEG_INF)
                sink = sink_ref[h]
                m = jnp.maximum(jnp.maximum(jnp.max(s_cur, axis=1, keepdims=True),
                                            jnp.max(s_prev, axis=1, keepdims=True)), sink)
                p_cur = jnp.exp(s_cur - m)
                p_prev = jnp.exp(s_prev - m)
                den = (jnp.sum(p_cur, axis=1, keepdims=True) + jnp.sum(p_prev, axis=1, keepdims=True)
                       + jnp.exp(sink - m))
                inv = 1.0 / den
                acc += (_dot(p_cur * inv, kc_ref[:, vcol:vcol + LANES])
                        + _dot(p_prev * inv, kp_ref[:, vcol:vcol + LANES]))
                lse = jnp.where(lane == h, m + jnp.log(den), lse)
            out_ref[:, cols] = acc
            gate, _ = _silu_and_grad(za_ref[:, cols])
            ag_ref[:, cols] = (acc * gate).astype(MXU_DTYPE)
        lse_ref[...] = lse

    blk = lambda w: pl.BlockSpec((BLOCK, w), lambda n: (n, 0))
    return pl.pallas_call(
        body,
        name="attn_fwd",
        grid=(N_BLOCKS,),
        in_specs=[pl.BlockSpec(memory_space=pltpu.SMEM), blk(ATTN_W), blk(8 * LANES),
                  pl.BlockSpec((BLOCK, 8 * LANES), lambda n: (jnp.maximum(n - 1, 0), 0)),
                  blk(ATTN_W)],
        out_specs=(blk(ATTN_W), blk(LANES), blk(ATTN_W)),
        out_shape=(jax.ShapeDtypeStruct((SEQ, ATTN_W), F32),
                   jax.ShapeDtypeStruct((SEQ, LANES), F32),
                   jax.ShapeDtypeStruct((SEQ, ATTN_W), MXU_DTYPE)),
        compiler_params=_params(("arbitrary",)),
    )(sinks, q, kvx, kvx, gates)


def _sgu_forward_chunk(us, vs, lng, lnb, w_ref, bias_ref):
    u = _gelu(us)
    vg = _gelu(vs)
    mu = jnp.mean(vg, axis=-1, keepdims=True)
    xc = vg - mu
    rstd = lax.rsqrt(jnp.mean(xc * xc, axis=-1, keepdims=True) + NORM_EPS)
    vhat = xc * rstd
    vln = vhat * lng + lnb
    low = lax.broadcasted_iota(jnp.int32, (BLOCK, LANES), 1) < HALF
    tril = (lax.broadcasted_iota(jnp.int32, (BLOCK, BLOCK), 0)
            >= lax.broadcasted_iota(jnp.int32, (BLOCK, BLOCK), 1))
    mixed = []
    for pair in range(N_SGU_HEADS // 2):
        vp = vln[:, pair * LANES:(pair + 1) * LANES]
        w0 = jnp.where(tril, w_ref[2 * pair], 0.0)
        w1 = jnp.where(tril, w_ref[2 * pair + 1], 0.0)
        mixed.append(_dot(w0, jnp.where(low, vp, 0.0)) + _dot(w1, jnp.where(low, 0.0, vp))
                     + bias_ref[:, pair * LANES:(pair + 1) * LANES])
    return u, vhat, rstd, vln, mixed


def _sgu_fwd(gates, ln_g, ln_b, sgu_w, bias_full):
    def body(us_ref, vs_ref, zs_ref, lng_ref, lnb_ref, w_ref, bias_ref, sg_ref):
        u, _, _, _, mixed = _sgu_forward_chunk(us_ref[...], vs_ref[...], lng_ref[...], lnb_ref[...],
                                               w_ref, bias_ref)
        for pair in range(N_SGU_HEADS // 2):
            cols = slice(pair * LANES, (pair + 1) * LANES)
            gate, _ = _silu_and_grad(zs_ref[:, cols])
            sg_ref[:, cols] = (u[:, cols] * mixed[pair] * gate).astype(MXU_DTYPE)

    col = lambda k: pl.BlockSpec((BLOCK, SGU_W), lambda n: (n, k))
    return pl.pallas_call(
        body,
        name="sgu_fwd",
        grid=(N_BLOCKS,),
        in_specs=[col(1), col(2), col(3), _full((1, SGU_W)), _full((1, SGU_W)),
                  _full((N_SGU_HEADS, BLOCK, BLOCK)), _full((BLOCK, SGU_W))],
        out_specs=pl.BlockSpec((BLOCK, SGU_W), lambda n: (n, 0)),
        out_shape=jax.ShapeDtypeStruct((SEQ, SGU_W), MXU_DTYPE),
        compiler_params=_params(("arbitrary",)),
    )(gates, gates, gates, ln_g, ln_b, sgu_w, bias_full)


def _out_proj_loss(ag, sg, x, target, wout, b_out, final_g):
    tm = TOKEN_TILE

    def body(ag_ref, sg_ref, x_ref, t_ref, w_ref, b_ref, gf_ref, gres_ref, dmix_ref, gw_ref, vec_ref):
        @pl.when(pl.program_id(0) == 0)
        def _():
            gw_ref[...] = jnp.zeros_like(gw_ref)
            vec_ref[...] = jnp.zeros_like(vec_ref)

        a = ag_ref[...]
        s = sg_ref[...]
        xo = x_ref[...] + (_dot(a, w_ref[0:ATTN_W, :]) + _dot(s, w_ref[ATTN_W:, :])) + b_ref[...]
        r = lax.rsqrt(jnp.mean(xo * xo, axis=-1, keepdims=True) + NORM_EPS)
        xn = xo * r
        gf = gf_ref[...]
        err = xn * gf - t_ref[...]
        loss = 0.5 * jnp.sum(jnp.mean(err * err, axis=-1, keepdims=True), axis=0, keepdims=True)
        dy = err * (1.0 / D_MODEL)
        dxn = dy * gf
        gres = r * (dxn - xn * jnp.mean(dxn * xn, axis=-1, keepdims=True))
        vec_ref[0:1, :] += jnp.broadcast_to(loss, (1, D_MODEL))
        vec_ref[1:2, :] += jnp.sum(dy * xn, axis=0, keepdims=True)
        vec_ref[2:3, :] += jnp.sum(gres, axis=0, keepdims=True)
        gres_ref[...] = gres
        gb = gres.astype(MXU_DTYPE)
        dmix_ref[:, 0:ATTN_W] = _dot(gb, w_ref[0:ATTN_W, :], NT)
        dmix_ref[:, ATTN_W:] = _dot(gb, w_ref[ATTN_W:, :], NT)
        gw_ref[0:ATTN_W, :] += _dot(a, gb, TN)
        gw_ref[ATTN_W:, :] += _dot(s, gb, TN)

    tile = lambda w: pl.BlockSpec((tm, w), lambda i: (i, 0))
    return pl.pallas_call(
        body,
        name="out_proj_loss",
        grid=(SEQ // tm,),
        in_specs=[tile(ATTN_W), tile(SGU_W), tile(D_MODEL), tile(D_MODEL),
                  _full((D_MODEL, D_MODEL)), _full((1, D_MODEL)), _full((1, D_MODEL))],
        out_specs=(tile(D_MODEL), tile(D_MODEL), _full((D_MODEL, D_MODEL)), _full((8, D_MODEL))),
        out_shape=(jax.ShapeDtypeStruct((SEQ, D_MODEL), F32),
                   jax.ShapeDtypeStruct((SEQ, D_MODEL), F32),
                   jax.ShapeDtypeStruct((D_MODEL, D_MODEL), F32),
                   jax.ShapeDtypeStruct((8, D_MODEL), F32)),
        compiler_params=_params(("arbitrary",), VMEM_LIMIT),
    )(ag, sg, x, target, wout, b_out, final_g)


def _sgu_bwd(dmix, gates, ln_g, ln_b, sgu_w, bias_full):
    last = N_BLOCKS - 1

    def body(d_ref, us_ref, vs_ref, zs_ref, lng_ref, lnb_ref, w_ref, bias_ref,
             dp_ref, gw_ref, gb_ref, gln_ref, gbin_ref, wt_ref, gbias_ref):
        c = pl.program_id(0)
        tril = (lax.broadcasted_iota(jnp.int32, (BLOCK, BLOCK), 0)
                >= lax.broadcasted_iota(jnp.int32, (BLOCK, BLOCK), 1))

        @pl.when(c == 0)
        def _():
            gw_ref[...] = jnp.zeros_like(gw_ref)
            gln_ref[...] = jnp.zeros_like(gln_ref)
            gbin_ref[...] = jnp.zeros_like(gbin_ref)
            gbias_ref[...] = jnp.zeros_like(gbias_ref)
            for hh in range(N_SGU_HEADS):
                wt_ref[hh] = jnp.where(tril, w_ref[hh], 0.0).T.astype(MXU_DTYPE)

        us = us_ref[...]
        vs = vs_ref[...]
        lng = lng_ref[...]
        u, vhat, rstd, vln, mixed = _sgu_forward_chunk(us, vs, lng, lnb_ref[...], w_ref, bias_ref)
        low = lax.broadcasted_iota(jnp.int32, (BLOCK, LANES), 1) < HALF
        du_parts, dzs_parts, dvln_parts = [], [], []
        for pair in range(N_SGU_HEADS // 2):
            cols = slice(pair * LANES, (pair + 1) * LANES)
            dsg = d_ref[:, cols]
            gate, gate_grad = _silu_and_grad(zs_ref[:, cols])
            up = u[:, cols]
            du_parts.append(dsg * mixed[pair] * gate)
            dzs_parts.append(dsg * up * mixed[pair] * gate_grad)
            dmixed = dsg * up * gate
            gbias_ref[:, cols] += dmixed
            dm_lo = jnp.where(low, dmixed, 0.0)
            dm_hi = jnp.where(low, 0.0, dmixed)
            vp = vln[:, cols]
            gw_ref[2 * pair] += _dot(dm_lo, vp, NT)
            gw_ref[2 * pair + 1] += _dot(dm_hi, vp, NT)
            dvln_parts.append(_dot(wt_ref[2 * pair], dm_lo) + _dot(wt_ref[2 * pair + 1], dm_hi))
        dvln = jnp.concatenate(dvln_parts, axis=1)
        gln_ref[0:1, :] += jnp.sum(dvln * vhat, axis=0, keepdims=True)
        gln_ref[1:2, :] += jnp.sum(dvln, axis=0, keepdims=True)
        dvhat = dvln * lng
        dvg = rstd * (dvhat - jnp.mean(dvhat, axis=-1, keepdims=True)
                      - vhat * jnp.mean(dvhat * vhat, axis=-1, keepdims=True))
        dus = jnp.concatenate(du_parts, axis=1) * _gelu_grad(us)
        dvs = dvg * _gelu_grad(vs)
        dzs = jnp.concatenate(dzs_parts, axis=1)
        for k, val in enumerate((dus, dvs, dzs)):
            dp_ref[:, k * SGU_W:(k + 1) * SGU_W] = val.astype(MXU_DTYPE)
            gbin_ref[:, k * SGU_W:(k + 1) * SGU_W] += jnp.sum(val, axis=0, keepdims=True)

        @pl.when(c == last)
        def _():
            for hh in range(N_SGU_HEADS):
                gw_ref[hh] = jnp.where(tril, gw_ref[hh], 0.0)
            head_of_lane = lax.broadcasted_iota(jnp.int32, (N_SGU_HEADS, SGU_W), 1) // HEAD_DIM
            select = (head_of_lane == lax.broadcasted_iota(jnp.int32, (N_SGU_HEADS, SGU_W), 0)).astype(F32)
            gb_ref[...] = lax.dot_general(select, gbias_ref[...], NT, precision=lax.Precision.HIGHEST,
                                          preferred_element_type=F32)

    col = lambda k: pl.BlockSpec((BLOCK, SGU_W), lambda n: (n, k))
    return pl.pallas_call(
        body,
        name="sgu_bwd",
        grid=(N_BLOCKS,),
        in_specs=[col(1), col(1), col(2), col(3), _full((1, SGU_W)), _full((1, SGU_W)),
                  _full((N_SGU_HEADS, BLOCK, BLOCK)), _full((BLOCK, SGU_W))],
        out_specs=(pl.BlockSpec((BLOCK, SGU_SECTION), lambda n: (n, 0)),
                   _full((N_SGU_HEADS, BLOCK, BLOCK)), _full((N_SGU_HEADS, BLOCK)),
                   _full((8, SGU_W)), _full((1, SGU_SECTION))),
        out_shape=(jax.ShapeDtypeStruct((SEQ, SGU_SECTION), MXU_DTYPE),
                   jax.ShapeDtypeStruct((N_SGU_HEADS, BLOCK, BLOCK), F32),
                   jax.ShapeDtypeStruct((N_SGU_HEADS, BLOCK), F32),
                   jax.ShapeDtypeStruct((8, SGU_W), F32),
                   jax.ShapeDtypeStruct((1, SGU_SECTION), F32)),
        scratch_shapes=[pltpu.VMEM((N_SGU_HEADS, BLOCK, BLOCK), MXU_DTYPE),
                        pltpu.VMEM((BLOCK, SGU_W), F32)],
        compiler_params=_params(("arbitrary",)),
    )(dmix, gates, gates, gates, ln_g, ln_b, sgu_w, bias_full)


def _attn_bwd(sinks, dmix, q, kvx, out, lse, gates):
    last = N_BLOCKS - 1

    def body(sink_ref, d_ref, dn_ref, q_ref, qn_ref, kv_ref, o_ref, on_ref, l_ref, ln_ref, za_ref, zan_ref,
             dp_ref, gsink_ref, gbin_ref, carry_ref):
        j = pl.program_id(0)

        @pl.when(j == 0)
        def _():
            gsink_ref[...] = jnp.zeros_like(gsink_ref)
            gbin_ref[...] = jnp.zeros_like(gbin_ref)
            carry_ref[...] = jnp.zeros_like(carry_ref)

        row = lax.broadcasted_iota(jnp.int32, (BLOCK, BLOCK), 0)
        lane = lax.broadcasted_iota(jnp.int32, (BLOCK, BLOCK), 1)
        low = lane < HALF
        mask_cur = lane <= row
        mask_next = jnp.logical_and(lane > row, j < last)
        lse_cur = l_ref[...]
        lse_next = ln_ref[...]
        dk_acc = [jnp.zeros((BLOCK, LANES), F32) for _ in range(4)]
        dv_acc = [jnp.zeros((BLOCK, LANES), F32) for _ in range(4)]
        gsink = jnp.zeros((1, LANES), F32)
        for pair in range(N_Q_HEADS // 2):
            cols = slice(pair * LANES, (pair + 1) * LANES)
            gate, gate_grad = _silu_and_grad(za_ref[:, cols])
            gate_n, _ = _silu_and_grad(zan_ref[:, cols])
            dg = d_ref[:, cols]
            o = o_ref[:, cols]
            dout = dg * gate
            dout_n = dn_ref[:, cols] * gate_n
            dza = dg * o * gate_grad
            prod = dout * o
            prod_n = dout_n * on_ref[:, cols]
            qp = q_ref[:, cols]
            qn = qn_ref[:, cols]
            dq = carry_ref[:, cols]
            dq_next = jnp.zeros((BLOCK, LANES), F32)
            for par in range(2):
                h = 2 * pair + par
                var = _kv_variant(h)
                mine = low if par == 0 else jnp.logical_not(low)
                k = kv_ref[:, var * LANES:(var + 1) * LANES]
                v = kv_ref[:, (var + 4) * LANES:(var + 5) * LANES]
                sink = sink_ref[h]
                lse_h = jnp.sum(jnp.where(lane == h, lse_cur, 0.0), axis=1, keepdims=True)
                lse_hn = jnp.sum(jnp.where(lane == h, lse_next, 0.0), axis=1, keepdims=True)
                delta = jnp.sum(jnp.where(mine, prod, 0.0), axis=1, keepdims=True)
                delta_n = jnp.sum(jnp.where(mine, prod_n, 0.0), axis=1, keepdims=True)
                p = jnp.exp(jnp.where(mask_cur, _dot(qp, k, NT) * SCALE, NEG_INF) - lse_h)
                p_n = jnp.exp(jnp.where(mask_next, _dot(qn, k, NT) * SCALE, NEG_INF) - lse_hn)
                ds = p * (_dot(dout, v, NT) - delta)
                ds_n = p_n * (_dot(dout_n, v, NT) - delta_n)
                dq += _dot(ds, k)
                dq_next += _dot(ds_n, k)
                dk_acc[var] += jnp.where(mine, _dot(ds, qp, TN) + _dot(ds_n, qn, TN), 0.0)
                dv_acc[var] += jnp.where(mine, _dot(p, dout, TN) + _dot(p_n, dout_n, TN), 0.0)
                gs = -jnp.sum(jnp.exp(sink - lse_h) * delta, axis=0, keepdims=True)
                gsink = jnp.where(lane[0:1, :] == h, gs, gsink)
            carry_ref[:, cols] = dq_next
            dq = dq * SCALE
            dp_ref[:, cols] = dq.astype(MXU_DTYPE)
            gbin_ref[:, cols] += jnp.sum(dq, axis=0, keepdims=True)
            zcols = slice(GATE0 + pair * LANES, GATE0 + (pair + 1) * LANES)
            dp_ref[:, zcols] = dza.astype(MXU_DTYPE)
            gbin_ref[:, zcols] += jnp.sum(dza, axis=0, keepdims=True)
        gsink_ref[...] += gsink
        dk = (dk_acc[0] + dk_acc[3] + pltpu.roll(dk_acc[1] + dk_acc[2], HALF, 1)) * SCALE
        dv = dv_acc[0] + dv_acc[3] + pltpu.roll(dv_acc[1] + dv_acc[2], HALF, 1)
        for col, val in ((KV0, dk), (KV0 + KV_W, dv)):
            dp_ref[:, col:col + KV_W] = val.astype(MXU_DTYPE)
            gbin_ref[:, col:col + KV_W] += jnp.sum(val, axis=0, keepdims=True)

    cur = lambda w, k=0: pl.BlockSpec((BLOCK, w), lambda j: (j, k))
    nxt = lambda w, k=0: pl.BlockSpec((BLOCK, w), lambda j: (jnp.minimum(j + 1, last), k))
    return pl.pallas_call(
        body,
        name="attn_bwd",
        grid=(N_BLOCKS,),
        in_specs=[pl.BlockSpec(memory_space=pltpu.SMEM),
                  cur(ATTN_W), nxt(ATTN_W),
                  cur(ATTN_W), nxt(ATTN_W),
                  cur(8 * LANES),
                  cur(ATTN_W), nxt(ATTN_W),
                  cur(LANES), nxt(LANES),
                  cur(ATTN_W), nxt(ATTN_W)],
        out_specs=(pl.BlockSpec((BLOCK, ATTN_SECTION), lambda j: (j, 0)),
                   _full((1, LANES)), _full((1, ATTN_SECTION))),
        out_shape=(jax.ShapeDtypeStruct((SEQ, ATTN_SECTION), MXU_DTYPE),
                   jax.ShapeDtypeStruct((1, LANES), F32),
                   jax.ShapeDtypeStruct((1, ATTN_SECTION), F32)),
        scratch_shapes=[pltpu.VMEM((BLOCK, ATTN_W), F32)],
        compiler_params=_params(("arbitrary",)),
    )(sinks, dmix, dmix, q, q, kvx, out, out, lse, lse, gates, gates)


def _in_proj_bwd(dpa, dps, win_t, x, norm_g, gres):
    tm = TOKEN_TILE

    def body(da_ref, ds_ref, w_ref, x_ref, g_ref, gres_ref, gx_ref, gng_ref):
        @pl.when(pl.program_id(0) == 0)
        def _():
            gng_ref[...] = jnp.zeros_like(gng_ref)

        dh = _dot(da_ref[...], w_ref[0:ATTN_SECTION, :]) + _dot(ds_ref[...], w_ref[ATTN_SECTION:, :])
        xv = x_ref[...]
        r = lax.rsqrt(jnp.mean(xv * xv, axis=-1, keepdims=True) + NORM_EPS)
        xn = xv * r
        gng_ref[...] += jnp.sum(dh * xn, axis=0, keepdims=True)
        dxn = dh * g_ref[...]
        gx_ref[...] = r * (dxn - xn * jnp.mean(dxn * xn, axis=-1, keepdims=True)) + gres_ref[...]

    tile = lambda w: pl.BlockSpec((tm, w), lambda i: (i, 0))
    return pl.pallas_call(
        body,
        name="in_proj_bwd",
        grid=(SEQ // tm,),
        in_specs=[tile(ATTN_SECTION), tile(SGU_SECTION), _full((IN_W, D_MODEL)), tile(D_MODEL),
                  _full((1, D_MODEL)), tile(D_MODEL)],
        out_specs=(tile(D_MODEL), _full((1, D_MODEL))),
        out_shape=(jax.ShapeDtypeStruct((SEQ, D_MODEL), F32), jax.ShapeDtypeStruct((1, D_MODEL), F32)),
        compiler_params=_params(("arbitrary",), VMEM_LIMIT),
    )(dpa, dps, win_t, x, norm_g, gres)


def _win_grad(dpa, dps, h):
    rows = 256
    n_attn = ATTN_SECTION // rows
    n_sgu = SGU_SECTION // rows

    def body(da_ref, ds_ref, h_ref, o_ref):
        i = pl.program_id(0)

        @pl.when(i < n_attn)
        def _():
            o_ref[...] = _dot(da_ref[...], h_ref[...], TN)

        @pl.when(i >= n_attn)
        def _():
            o_ref[...] = _dot(ds_ref[...], h_ref[...], TN)

    return pl.pallas_call(
        body,
        name="win_grad",
        grid=(n_attn + n_sgu,),
        in_specs=[pl.BlockSpec((SEQ, rows), lambda i: (0, jnp.minimum(i, n_attn - 1))),
                  pl.BlockSpec((SEQ, rows), lambda i: (0, jnp.maximum(i - n_attn, 0))),
                  _full((SEQ, D_MODEL))],
        out_specs=pl.BlockSpec((rows, D_MODEL), lambda i: (i, 0)),
        out_shape=jax.ShapeDtypeStruct((IN_W, D_MODEL), F32),
        compiler_params=_params(("arbitrary",), VMEM_LIMIT),
    )(dpa, dps, h)


VEC_NORM_G, VEC_B_IN, VEC_SINKS, VEC_LN_G, VEC_LN_B, VEC_B_OUT, VEC_FINAL_G, VEC_SGU_B = 0, 1, 2, 3, 4, 5, 6, 8


def _reduce_grads(gwin, gwout, gsguw, gng, gbin_a, gbin_s, gsink, gln, gsgub, vec4):
    def body(gwin_ref, gwout_ref, gsguw_ref, gng_ref, gba_ref, gbs_ref, gsink_ref, gln_ref, gsgub_ref, vec4_ref,
             owin_ref, owout_ref, osguw_ref, ovec_ref,
             vec_ref, ra_vec, slots_vec,
             sa_win, ra_win, sb_win, rc_win,
             sa_wout, ra_wout, sb_wout, rc_wout,
             sa_sguw, ra_sguw, sb_sguw, rc_sguw,
             send_sems, recv_sems):
        x, y, c = _place()
        sibling = (x, y, 1 - c)
        relations = [(0, 0), (1, 0), (0, 1), (1, 1)]

        def chip(rel):
            return (x + rel[0] - 2 * x * rel[0], y + rel[1] - 2 * y * rel[1])

        def owner_rows(rel, core, n_rows):
            px, py = chip(rel)
            return pl.ds(pl.multiple_of((4 * px + 2 * py + core) * n_rows, 8), n_rows)

        def slot(ref, i, n_rows):
            return ref.at[pl.ds(i * n_rows, n_rows), :]

        def remote(src, dst, k, to):
            return pltpu.make_async_remote_copy(src_ref=src, dst_ref=dst, send_sem=send_sems.at[k],
                                                recv_sem=recv_sems.at[k], device_id=to, device_id_type=MESH)

        big = [
            (gwin_ref, WIN_ROWS, sa_win, ra_win, sb_win, rc_win, owin_ref),
            (gwout_ref, WOUT_ROWS, sa_wout, ra_wout, sb_wout, rc_wout, owout_ref),
            (gsguw_ref, SGUW_ROWS, sa_sguw, ra_sguw, sb_sguw, rc_sguw, None),
        ]
        n_arrays = len(big) + 1

        vec_ref[...] = jnp.zeros_like(vec_ref)
        vec_ref[VEC_NORM_G:VEC_NORM_G + 1, 0:D_MODEL] = gng_ref[...]
        vec_ref[VEC_B_IN:VEC_B_IN + 1, 0:ATTN_SECTION] = gba_ref[...]
        vec_ref[VEC_B_IN:VEC_B_IN + 1, ATTN_SECTION:IN_W] = gbs_ref[...]
        vec_ref[VEC_SINKS:VEC_SINKS + 1, 0:LANES] = gsink_ref[...]
        vec_ref[VEC_LN_G:VEC_LN_G + 1, 0:SGU_W] = gln_ref[0:1, :]
        vec_ref[VEC_LN_B:VEC_LN_B + 1, 0:SGU_W] = gln_ref[1:2, :]
        vec_ref[VEC_B_OUT:VEC_B_OUT + 1, 0:D_MODEL] = vec4_ref[2:3, :]
        vec_ref[VEC_FINAL_G:VEC_FINAL_G + 1, 0:D_MODEL] = vec4_ref[1:2, :]
        vec_ref[VEC_SGU_B:VEC_SGU_B + N_SGU_HEADS, 0:BLOCK] = gsgub_ref[...]

        level1 = []
        for a, (part, n, sa, ra, _, _, _) in enumerate(big):
            for i, rel in enumerate(relations):
                sa[i * n:(i + 1) * n, :] = part[owner_rows(rel, 1 - c, n), :].astype(sa.dtype)
            level1.append(remote(sa, ra, a, sibling))
        level1.append(remote(vec_ref, ra_vec, len(big), sibling))
        for cp in level1:
            cp.start()

        level2 = []
        for a, (part, n, sa, ra, sb, rc, res) in enumerate(big):
            level1[a].wait_recv()
            for i, rel in enumerate(relations):
                total = part[owner_rows(rel, c, n), :] + ra[i * n:(i + 1) * n, :].astype(F32)
                if i == 0:
                    if res is None:
                        sb_sguw_own = total
                    else:
                        res[...] = total
                else:
                    sb[(i - 1) * n:i * n, :] = total.astype(sb.dtype)
                    cp = remote(slot(sb, i - 1, n), slot(rc, i - 1, n), n_arrays * i + a, (*chip(rel), c))
                    cp.start()
                    level2.append(cp)
        level1[len(big)].wait_recv()
        my_slot = pl.ds(pl.multiple_of((2 * x + y) * VEC_ROWS, 8), VEC_ROWS)
        slots_vec[my_slot, :] = vec_ref[...] + ra_vec[...]
        for i, rel in enumerate(relations[1:], start=1):
            cp = remote(slots_vec.at[my_slot, :], slots_vec.at[my_slot, :], n_arrays * i + len(big), (*chip(rel), c))
            cp.start()
            level2.append(cp)

        def arrived(a, i, ref_slice):
            remote(ref_slice, ref_slice, n_arrays * i + a, sibling).wait_recv()

        for a, (part, n, sa, ra, sb, rc, res) in enumerate(big):
            acc = sb_sguw_own if res is None else res[...]
            for i in range(1, len(relations)):
                arrived(a, i, slot(rc, i - 1, n))
                acc = acc + rc[(i - 1) * n:i * n, :].astype(F32)
            if res is None:
                osguw_ref[owner_rows(relations[0], c, n), :] = acc
            else:
                res[...] = acc
        for i, rel in enumerate(relations[1:], start=1):
            px, py = chip(rel)
            theirs = pl.ds(pl.multiple_of((2 * px + py) * VEC_ROWS, 8), VEC_ROWS)
            arrived(len(big), i, slots_vec.at[theirs, :])
        ovec_ref[...] = ((slots_vec[0:VEC_ROWS, :] + slots_vec[VEC_ROWS:2 * VEC_ROWS, :])
                         + slots_vec[2 * VEC_ROWS:3 * VEC_ROWS, :]) + slots_vec[3 * VEC_ROWS:, :]

        mine = osguw_ref.at[owner_rows(relations[0], c, SGUW_ROWS), :]
        base = n_arrays * len(relations)
        others = [(rel, flip) for flip in (0, 1) for rel in relations if (rel, flip) != ((0, 0), 0)]
        gathers = []
        for k, (rel, flip) in enumerate(others):
            core = c + flip - 2 * c * flip
            cp = remote(mine, mine, base + k, (*chip(rel), core))
            cp.start()
            gathers.append(cp)
        for k, (rel, flip) in enumerate(others):
            core = c + flip - 2 * c * flip
            theirs = osguw_ref.at[owner_rows(rel, core, SGUW_ROWS), :]
            remote(theirs, theirs, base + k, sibling).wait_recv()
        for cp in level1 + level2 + gathers:
            cp.wait_send()

    n_sems = 4 * 4 + 7
    comm = jnp.dtype(COMM_DTYPE)
    return pl.pallas_call(
        body,
        name="reduce_grads",
        out_shape=(jax.ShapeDtypeStruct((WIN_ROWS, D_MODEL), F32),
                   jax.ShapeDtypeStruct((WOUT_ROWS, D_MODEL), F32),
                   jax.ShapeDtypeStruct((N_SGU_HEADS * BLOCK, BLOCK), F32),
                   jax.ShapeDtypeStruct((VEC_ROWS, IN_W), F32)),
        in_specs=[VMEM_SPEC] * 10,
        out_specs=(VMEM_SPEC,) * 4,
        scratch_shapes=[
            pltpu.VMEM((VEC_ROWS, IN_W), F32), pltpu.VMEM((VEC_ROWS, IN_W), F32), pltpu.VMEM((4 * VEC_ROWS, IN_W), F32),
            pltpu.VMEM((4 * WIN_ROWS, D_MODEL), comm), pltpu.VMEM((4 * WIN_ROWS, D_MODEL), comm),
            pltpu.VMEM((3 * WIN_ROWS, D_MODEL), comm), pltpu.VMEM((3 * WIN_ROWS, D_MODEL), comm),
            pltpu.VMEM((4 * WOUT_ROWS, D_MODEL), comm), pltpu.VMEM((4 * WOUT_ROWS, D_MODEL), comm),
            pltpu.VMEM((3 * WOUT_ROWS, D_MODEL), comm), pltpu.VMEM((3 * WOUT_ROWS, D_MODEL), comm),
            pltpu.VMEM((4 * SGUW_ROWS, BLOCK), F32), pltpu.VMEM((4 * SGUW_ROWS, BLOCK), F32),
            pltpu.VMEM((3 * SGUW_ROWS, BLOCK), F32), pltpu.VMEM((3 * SGUW_ROWS, BLOCK), F32),
            pltpu.SemaphoreType.DMA((n_sems,)), pltpu.SemaphoreType.DMA((n_sems,)),
        ],
        compiler_params=_params(vmem=VMEM_LIMIT),
    )(gwin, gwout, gsguw, gng, gbin_a, gbin_s, gsink, gln, gsgub, vec4)


def _adamw(w, g, m, v):
    m = ADAM_B1 * m + (1.0 - ADAM_B1) * g
    v = ADAM_B2 * v + (1.0 - ADAM_B2) * (g * g)
    m_hat = m / (1.0 - ADAM_B1 ** ADAM_STEP)
    v_hat = v / (1.0 - ADAM_B2 ** ADAM_STEP)
    delta = -ADAM_LR * (m_hat / (jnp.sqrt(v_hat) + ADAM_EPS) + ADAM_WD * w)
    return delta, m, v


def _adamw_shard(name, g, w, m, v, block_rows):
    def body(g_ref, w_ref, m_ref, v_ref, d_ref, nm_ref, nv_ref):
        d_ref[...], nm_ref[...], nv_ref[...] = _adamw(w_ref[...], g_ref[...], m_ref[...], v_ref[...])

    rows, cols = w.shape
    spec = pl.BlockSpec((block_rows, cols), lambda i: (i, 0))
    return pl.pallas_call(
        body,
        name=name,
        grid=(rows // block_rows,),
        in_specs=[spec] * 4,
        out_specs=(spec,) * 3,
        out_shape=(jax.ShapeDtypeStruct(w.shape, F32),) * 3,
        compiler_params=_params(("arbitrary",)),
    )(g, w, m, v)


def _adamw_replicated(vec, gsguw, weights, m_state, v_state):
    n = len(SMALL)

    def body(*refs):
        vec_ref, gsguw_ref = refs[0], refs[1]
        w_refs, m_refs, v_refs = (refs[2 + k * n:2 + (k + 1) * n] for k in range(3))
        outs = refs[2 + 3 * n:]
        g_refs, d_refs, nm_refs, nv_refs = (outs[k * n:(k + 1) * n] for k in range(4))
        for i, (_, row, shape) in enumerate(SMALL):
            if row is None:
                g = gsguw_ref[...]
            else:
                g = vec_ref[row:row + shape[0], 0:shape[1]]
            g_refs[i][...] = g
            d_refs[i][...], nm_refs[i][...], nv_refs[i][...] = _adamw(
                w_refs[i][...], g, m_refs[i][...], v_refs[i][...])

    shapes = tuple(jax.ShapeDtypeStruct(shape, F32) for _, _, shape in SMALL)
    outs = pl.pallas_call(
        body,
        name="adamw_replicated",
        in_specs=[VMEM_SPEC] * (2 + 3 * n),
        out_specs=(VMEM_SPEC,) * (4 * n),
        out_shape=shapes * 4,
    )(vec, gsguw, *weights, *m_state, *v_state)
    return tuple(outs[k * n:(k + 1) * n] for k in range(4))


SMALL = (
    ("norm_g", VEC_NORM_G, (1, D_MODEL)),
    ("b_in", VEC_B_IN, (1, IN_W)),
    ("attn_sinks", VEC_SINKS, (1, N_Q_HEADS)),
    ("sgu_ln_g", VEC_LN_G, (1, SGU_W)),
    ("sgu_ln_b", VEC_LN_B, (1, SGU_W)),
    ("sgu_w", None, (N_SGU_HEADS * BLOCK, BLOCK)),
    ("sgu_b", VEC_SGU_B, (N_SGU_HEADS, BLOCK)),
    ("b_out", VEC_B_OUT, (1, D_MODEL)),
    ("final_norm_g", VEC_FINAL_G, (1, D_MODEL)),
)


def _local_grads(x, target, win_t, wout, norm_g, b_in, attn_sinks, sgu_ln_g, sgu_ln_b, sgu_w, sgu_b, b_out,
                 final_g):
    sinks = attn_sinks.reshape(N_Q_HEADS)
    bias_full = jnp.repeat(sgu_b.T, HEAD_DIM, axis=1)
    h, q, kvx, gates = _in_proj(x, norm_g, b_in, win_t)
    out, lse, ag = _attn_fwd(sinks, q, kvx, gates)
    sg = _sgu_fwd(gates, sgu_ln_g, sgu_ln_b, sgu_w, bias_full)
    gres, dmix, gwout, vec4 = _out_proj_loss(ag, sg, x, target, wout, b_out, final_g)
    dps, gsguw, gsgub, gln, gbin_s = _sgu_bwd(dmix, gates, sgu_ln_g, sgu_ln_b, sgu_w, bias_full)
    dpa, gsink, gbin_a = _attn_bwd(sinks, dmix, q, kvx, out, lse, gates)
    grad_x, gng = _in_proj_bwd(dpa, dps, win_t, x, norm_g, gres)
    gwin = _win_grad(dpa, dps, h)
    return grad_x, (gwin, gwout, gsguw.reshape(N_SGU_HEADS * BLOCK, BLOCK), gng, gbin_a, gbin_s, gsink, gln, gsgub,
                    vec4)


def kernel(x, norm_g, w_in, b_in, attn_sinks, sgu_ln_g, sgu_ln_b, sgu_w, sgu_b, w_out, b_out, final_norm_g, loss_target, m_norm_g, m_w_in, m_b_in, m_attn_sinks, m_sgu_ln_g, m_sgu_ln_b, m_sgu_w, m_sgu_b, m_w_out, m_b_out, m_final_norm_g, v_norm_g, v_w_in, v_b_in, v_attn_sinks, v_sgu_ln_g, v_sgu_ln_b, v_sgu_w, v_sgu_b, v_w_out, v_b_out, v_final_norm_g):
    given = dict(norm_g=norm_g, b_in=b_in, attn_sinks=attn_sinks, sgu_ln_g=sgu_ln_g, sgu_ln_b=sgu_ln_b,
                 sgu_w=sgu_w, sgu_b=sgu_b, b_out=b_out, final_norm_g=final_norm_g)
    m_given = dict(norm_g=m_norm_g, b_in=m_b_in, attn_sinks=m_attn_sinks, sgu_ln_g=m_sgu_ln_g,
                   sgu_ln_b=m_sgu_ln_b, sgu_w=m_sgu_w, sgu_b=m_sgu_b, b_out=m_b_out, final_norm_g=m_final_norm_g)
    v_given = dict(norm_g=v_norm_g, b_in=v_b_in, attn_sinks=v_attn_sinks, sgu_ln_g=v_sgu_ln_g,
                   sgu_ln_b=v_sgu_ln_b, sgu_w=v_sgu_w, sgu_b=v_sgu_b, b_out=v_b_out, final_norm_g=v_final_norm_g)

    win_t, wout = _all_gather_weights(w_in[0].T, w_out[0])
    grad_x, partials = _local_grads(
        x[0], loss_target[0], win_t, wout, norm_g, b_in, attn_sinks, sgu_ln_g, sgu_ln_b, sgu_w[0], sgu_b[0],
        b_out, final_norm_g.reshape(1, D_MODEL))
    gwin_t, gwout, gsguw, vec = _reduce_grads(*partials)
    loss = lax.psum(partials[-1][0, 0], ("x", "y", "c"))

    gwin = gwin_t.T
    d_win, nm_win, nv_win = _adamw_shard("adamw_w_in", gwin, w_in[0], m_w_in[0], v_w_in[0], 256)
    d_wout, nm_wout, nv_wout = _adamw_shard("adamw_w_out", gwout, w_out[0], m_w_out[0], v_w_out[0], WOUT_ROWS)
    as_2d = lambda d: [d[name].reshape(shape) for name, _, shape in SMALL]
    small = _adamw_replicated(vec, gsguw, as_2d(given), as_2d(m_given), as_2d(v_given))

    def assemble(big_in, big_out, k):
        vals = {name: small[k][i].reshape(given[name].shape) for i, (name, _, _) in enumerate(SMALL)}
        vals["w_in"] = big_in[None]
        vals["w_out"] = big_out[None]
        order = ("norm_g", "w_in", "b_in", "attn_sinks", "sgu_ln_g", "sgu_ln_b", "sgu_w", "sgu_b", "w_out",
                 "b_out", "final_norm_g")
        return [vals[name] for name in order]

    return (loss, grad_x[None],
            *assemble(gwin, gwout, 0), *assemble(d_win, d_wout, 1),
            *assemble(nm_win, nm_wout, 2), *assemble(nv_win, nv_wout, 3))
```

```python
import functools
import math

import jax
import jax.numpy as jnp
from jax import lax
from jax.experimental import pallas as pl
from jax.experimental.pallas import tpu as pltpu

F32 = jnp.float32
BF16 = jnp.bfloat16
MXU_DTYPE = BF16
COMM_DTYPE = BF16

D_MODEL = 1024
SEQ = 4096
HEAD_DIM = 64
N_Q_HEADS = 8
Q_PER_KV = 4
BLOCK = 128
N_BLOCKS = SEQ // BLOCK
ATTN_W = 512
KV_W = 128
SGU_W = 512
N_SGU_HEADS = 8
IN_W = 2816
NORM_EPS = 1e-5
NEG_INF = -1e30
SCALE = HEAD_DIM ** -0.5
KV0 = ATTN_W
GATE0 = ATTN_W + 2 * KV_W
SGU0 = GATE0 + ATTN_W
ATTN_SECTION = SGU0
SGU_SECTION = IN_W - SGU0

ADAM_LR = 0.001
ADAM_B1 = 0.9
ADAM_B2 = 0.999
ADAM_EPS = 1e-08
ADAM_WD = 0.01
ADAM_STEP = 10

N_DEV = 8
WIN_ROWS = IN_W // N_DEV
WOUT_ROWS = D_MODEL // N_DEV
SGUW_ROWS = N_SGU_HEADS * BLOCK // N_DEV
VEC_ROWS = 16
MESH = pl.DeviceIdType.MESH

LANES = 128
HALF = LANES // 2
N_PAIRS = N_Q_HEADS * HEAD_DIM // LANES
TOKEN_TILE = 256
VMEM_LIMIT = 56 * 1024 * 1024

NN = (((1,), (0,)), ((), ()))
NT = (((1,), (1,)), ((), ()))
TN = (((0,), (0,)), ((), ()))


def _dot(a, b, dims=NN):
    return lax.dot_general(a.astype(MXU_DTYPE), b.astype(MXU_DTYPE), dims, preferred_element_type=F32)


def _gelu(x):
    return x * (lax.erf(x * (1.0 / math.sqrt(2.0))) + 1.0) * 0.5


def _gelu_grad(x):
    cdf = (lax.erf(x * (1.0 / math.sqrt(2.0))) + 1.0) * 0.5
    return cdf + x * jnp.exp(-0.5 * x * x) * (1.0 / math.sqrt(2.0 * math.pi))


def _silu_and_grad(z):
    s = jax.nn.sigmoid(z)
    return z * s, s * (1.0 + z * (1.0 - s))


def _params(semantics=None, vmem=None):
    kw = {}
    if semantics is not None:
        kw["dimension_semantics"] = semantics
    if vmem is not None:
        kw["vmem_limit_bytes"] = vmem
    return pltpu.CompilerParams(**kw)


def _full(shape):
    return pl.BlockSpec(shape, lambda *_: (0,) * len(shape))


VMEM_SPEC = pl.BlockSpec(memory_space=pltpu.VMEM)


def _place():
    return lax.axis_index("x"), lax.axis_index("y"), lax.axis_index("c")


def _all_gather_weights(win_t_shard, wout_shard):
    def body(win_ref, wout_ref, gwin_ref, gwout_ref, send_sems, recv_sems):
        x, y, c = _place()
        me, sibling = (x, y, c), (x, y, 1 - c)
        chips = [(1 - x, y), (x, 1 - y), (1 - x, 1 - y)]

        def rows(ref, n_rows, place):
            px, py, pc = place
            start = pl.multiple_of((4 * px + 2 * py + pc) * n_rows, 16)
            return ref.at[pl.ds(start, n_rows), :]

        def copies(k, block, to):
            return [
                pltpu.make_async_remote_copy(
                    src_ref=rows(ref, n, block), dst_ref=rows(ref, n, block),
                    send_sem=send_sems.at[k, a], recv_sem=recv_sems.at[k, a],
                    device_id=to, device_id_type=MESH)
                for a, (ref, n) in enumerate(((gwin_ref, WIN_ROWS), (gwout_ref, WOUT_ROWS)))
            ]

        gwin_ref[pl.ds(pl.multiple_of((4 * x + 2 * y + c) * WIN_ROWS, 16), WIN_ROWS), :] = (
            win_ref[...].astype(COMM_DTYPE))
        gwout_ref[pl.ds(pl.multiple_of((4 * x + 2 * y + c) * WOUT_ROWS, 16), WOUT_ROWS), :] = (
            wout_ref[...].astype(COMM_DTYPE))

        first = copies(0, me, sibling)
        for j, chip in enumerate(chips):
            first += copies(1 + j, me, (*chip, c))
        for cp in first:
            cp.start()
        passed = []
        for j, chip in enumerate(chips):
            for cp in copies(1 + j, (*chip, c), me):
                cp.wait_recv()
            fwd = copies(4 + j, (*chip, c), sibling)
            for cp in fwd:
                cp.start()
            passed += fwd
        for cp in copies(0, sibling, me):
            cp.wait_recv()
        for j, chip in enumerate(chips):
            for cp in copies(4 + j, (*chip, 1 - c), me):
                cp.wait_recv()
        for cp in first + passed:
            cp.wait_send()

    return pl.pallas_call(
        body,
        name="all_gather_weights",
        out_shape=(jax.ShapeDtypeStruct((IN_W, D_MODEL), COMM_DTYPE),
                   jax.ShapeDtypeStruct((D_MODEL, D_MODEL), COMM_DTYPE)),
        in_specs=[VMEM_SPEC, VMEM_SPEC],
        out_specs=(VMEM_SPEC, VMEM_SPEC),
        scratch_shapes=[pltpu.SemaphoreType.DMA((7, 2)), pltpu.SemaphoreType.DMA((7, 2))],
        compiler_params=_params(vmem=VMEM_LIMIT),
    )(win_t_shard, wout_shard)


def _in_proj(x, norm_g, b_in, win_t):
    tm = TOKEN_TILE

    def body(x_ref, g_ref, b_ref, w_ref, h_ref, q_ref, kvx_ref, gate_ref):
        xv = x_ref[...]
        r = lax.rsqrt(jnp.mean(xv * xv, axis=-1, keepdims=True) + NORM_EPS)
        h = ((xv * r) * g_ref[...]).astype(MXU_DTYPE)
        h_ref[...] = h

        def proj(lo, hi):
            return _dot(h, w_ref[lo:hi, :], NT) + b_ref[:, lo:hi]

        qs = proj(0, ATTN_W) * SCALE
        for pair in range(N_PAIRS):
            q_ref[pair] = qs[:, pair * LANES:(pair + 1) * LANES].astype(MXU_DTYPE)
        kv = proj(KV0, GATE0)
        low = lax.broadcasted_iota(jnp.int32, (tm, LANES), 1) < HALF
        for i in range(2):
            t = kv[:, i * LANES:(i + 1) * LANES]
            rot = pltpu.roll(t, HALF, 1)
            variants = (jnp.where(low, t, 0.0), jnp.where(low, 0.0, rot),
                        jnp.where(low, rot, 0.0), jnp.where(low, 0.0, t))
            for j, val in enumerate(variants):
                col = (4 * i + j) * LANES
                kvx_ref[:, col:col + LANES] = val.astype(MXU_DTYPE)
        gate_ref[...] = proj(GATE0, IN_W)

    return pl.pallas_call(
        body,
        name="in_proj",
        grid=(SEQ // tm,),
        in_specs=[pl.BlockSpec((tm, D_MODEL), lambda i: (i, 0)),
                  _full((1, D_MODEL)), _full((1, IN_W)), _full((IN_W, D_MODEL))],
        out_specs=(pl.BlockSpec((tm, D_MODEL), lambda i: (i, 0)),
                   pl.BlockSpec((N_PAIRS, tm, LANES), lambda i: (0, i, 0)),
                   pl.BlockSpec((tm, 8 * LANES), lambda i: (i, 0)),
                   pl.BlockSpec((tm, IN_W - GATE0), lambda i: (i, 0))),
        out_shape=(jax.ShapeDtypeStruct((SEQ, D_MODEL), MXU_DTYPE),
                   jax.ShapeDtypeStruct((N_PAIRS, SEQ, LANES), MXU_DTYPE),
                   jax.ShapeDtypeStruct((SEQ, 8 * LANES), MXU_DTYPE),
                   jax.ShapeDtypeStruct((SEQ, IN_W - GATE0), F32)),
        compiler_params=_params(("arbitrary",), VMEM_LIMIT),
    )(x, norm_g, b_in, win_t)


def _window_mask(n):
    qi = lax.broadcasted_iota(jnp.int32, (2 * BLOCK, 2 * BLOCK), 0) & (BLOCK - 1)
    p = lax.broadcasted_iota(jnp.int32, (2 * BLOCK, 2 * BLOCK), 1) - BLOCK
    in_window = jnp.logical_and(p <= qi, p > qi - BLOCK)
    return jnp.logical_and(in_window, jnp.logical_or(p >= 0, n > 0))


def _sink_column(sink_ref, g, par):
    return jnp.concatenate([jnp.full((BLOCK, 1), sink_ref[4 * g + par], F32),
                            jnp.full((BLOCK, 1), sink_ref[4 * g + 2 + par], F32)], axis=0)


def _kv_cat(kp_ref, kc_ref, var):
    kcol, vcol = var * LANES, (var + 4) * LANES
    return (jnp.concatenate([kp_ref[:, kcol:kcol + LANES], kc_ref[:, kcol:kcol + LANES]], axis=0),
            jnp.concatenate([kp_ref[:, vcol:vcol + LANES], kc_ref[:, vcol:vcol + LANES]], axis=0))


def _softmax_stats(s, sink):
    m = jnp.maximum(jnp.max(s, axis=1, keepdims=True), sink)
    e = jnp.exp(s - m)
    return e, m, jnp.sum(e, axis=1, keepdims=True) + jnp.exp(sink - m)


def _attn_fwd(sinks, q, kvx, gates):
    def body(sink_ref, q_ref, kc_ref, kp_ref, za_ref, out_ref, ag_ref):
        valid = _window_mask(pl.program_id(0))
        chains = [(g, par) for g in range(2) for par in range(2)]
        kv = [_kv_cat(kp_ref, kc_ref, 2 * g + par) for g, par in chains]
        scores, outs = {}, []

        def issue_scores(i):
            g, _ = chains[i]
            scores[i] = _dot(q_ref[2 * g:2 * g + 2].reshape(2 * BLOCK, LANES), kv[i][0], NT)

        issue_scores(0)
        issue_scores(1)
        for i, (g, par) in enumerate(chains):
            e, _, den = _softmax_stats(jnp.where(valid, scores[i], NEG_INF), _sink_column(sink_ref, g, par))
            if i + 2 < len(chains):
                issue_scores(i + 2)
            outs.append(_dot(e, kv[i][1]) * (1.0 / den))
        for g in range(2):
            acc = outs[2 * g] + outs[2 * g + 1]
            for i in range(2):
                pair = 2 * g + i
                o = acc[i * BLOCK:(i + 1) * BLOCK]
                out_ref[pair] = o
                gate, _ = _silu_and_grad(za_ref[:, pair * LANES:(pair + 1) * LANES])
                ag_ref[:, pair * LANES:(pair + 1) * LANES] = (o * gate).astype(MXU_DTYPE)

    blk = lambda w: pl.BlockSpec((BLOCK, w), lambda n: (n, 0))
    tiles = pl.BlockSpec((N_PAIRS, BLOCK, LANES), lambda n: (0, n, 0))
    return pl.pallas_call(
        body,
        name="attn_fwd",
        grid=(N_BLOCKS,),
        in_specs=[pl.BlockSpec(memory_space=pltpu.SMEM), tiles, blk(8 * LANES),
                  pl.BlockSpec((BLOCK, 8 * LANES), lambda n: (jnp.maximum(n - 1, 0), 0)),
                  blk(ATTN_W)],
        out_specs=(tiles, blk(ATTN_W)),
        out_shape=(jax.ShapeDtypeStruct((N_PAIRS, SEQ, LANES), F32),
                   jax.ShapeDtypeStruct((SEQ, ATTN_W), MXU_DTYPE)),
        compiler_params=_params(("arbitrary",)),
    )(sinks, q, kvx, kvx, gates)


def _sgu_forward_chunk(us, vs, lng, lnb, w_ref, bias_ref):
    u = _gelu(us)
    vg = _gelu(vs)
    mu = jnp.mean(vg, axis=-1, keepdims=True)
    xc = vg - mu
    rstd = lax.rsqrt(jnp.mean(xc * xc, axis=-1, keepdims=True) + NORM_EPS)
    vhat = xc * rstd
    vln = vhat * lng + lnb
    low = lax.broadcasted_iota(jnp.int32, (BLOCK, LANES), 1) < HALF
    tril = (lax.broadcasted_iota(jnp.int32, (BLOCK, BLOCK), 0)
            >= lax.broadcasted_iota(jnp.int32, (BLOCK, BLOCK), 1))
    mixed = []
    for pair in range(N_SGU_HEADS // 2):
        vp = vln[:, pair * LANES:(pair + 1) * LANES]
        w0 = jnp.where(tril, w_ref[2 * pair], 0.0)
        w1 = jnp.where(tril, w_ref[2 * pair + 1], 0.0)
        mixed.append(_dot(w0, jnp.where(low, vp, 0.0)) + _dot(w1, jnp.where(low, 0.0, vp))
                     + bias_ref[:, pair * LANES:(pair + 1) * LANES])
    return u, vhat, rstd, vln, mixed


def _sgu_fwd(gates, ln_g, ln_b, sgu_w, bias_full):
    def body(us_ref, vs_ref, zs_ref, lng_ref, lnb_ref, w_ref, bias_ref, sg_ref):
        u, _, _, _, mixed = _sgu_forward_chunk(us_ref[...], vs_ref[...], lng_ref[...], lnb_ref[...],
                                               w_ref, bias_ref)
        for pair in range(N_SGU_HEADS // 2):
            cols = slice(pair * LANES, (pair + 1) * LANES)
            gate, _ = _silu_and_grad(zs_ref[:, cols])
            sg_ref[:, cols] = (u[:, cols] * mixed[pair] * gate).astype(MXU_DTYPE)

    col = lambda k: pl.BlockSpec((BLOCK, SGU_W), lambda n: (n, k))
    return pl.pallas_call(
        body,
        name="sgu_fwd",
        grid=(N_BLOCKS,),
        in_specs=[col(1), col(2), col(3), _full((1, SGU_W)), _full((1, SGU_W)),
                  _full((N_SGU_HEADS, BLOCK, BLOCK)), _full((BLOCK, SGU_W))],
        out_specs=pl.BlockSpec((BLOCK, SGU_W), lambda n: (n, 0)),
        out_shape=jax.ShapeDtypeStruct((SEQ, SGU_W), MXU_DTYPE),
        compiler_params=_params(("arbitrary",)),
    )(gates, gates, gates, ln_g, ln_b, sgu_w, bias_full)


def _out_proj_loss(ag, sg, x, target, wout, b_out, final_g):
    tm = TOKEN_TILE

    def body(ag_ref, sg_ref, x_ref, t_ref, w_ref, b_ref, gf_ref, gres_ref, dmix_ref, gw_ref, vec_ref):
        @pl.when(pl.program_id(0) == 0)
        def _():
            gw_ref[...] = jnp.zeros_like(gw_ref)
            vec_ref[...] = jnp.zeros_like(vec_ref)

        a = ag_ref[...]
        s = sg_ref[...]
        xo = x_ref[...] + (_dot(a, w_ref[0:ATTN_W, :]) + _dot(s, w_ref[ATTN_W:, :])) + b_ref[...]
        r = lax.rsqrt(jnp.mean(xo * xo, axis=-1, keepdims=True) + NORM_EPS)
        xn = xo * r
        gf = gf_ref[...]
        err = xn * gf - t_ref[...]
        loss = 0.5 * jnp.sum(jnp.mean(err * err, axis=-1, keepdims=True), axis=0, keepdims=True)
        dy = err * (1.0 / D_MODEL)
        dxn = dy * gf
        gres = r * (dxn - xn * jnp.mean(dxn * xn, axis=-1, keepdims=True))
        vec_ref[0:1, :] += jnp.broadcast_to(loss, (1, D_MODEL))
        vec_ref[1:2, :] += jnp.sum(dy * xn, axis=0, keepdims=True)
        vec_ref[2:3, :] += jnp.sum(gres, axis=0, keepdims=True)
        gres_ref[...] = gres
        gb = gres.astype(MXU_DTYPE)
        dmix_ref[:, 0:ATTN_W] = _dot(gb, w_ref[0:ATTN_W, :], NT)
        dmix_ref[:, ATTN_W:] = _dot(gb, w_ref[ATTN_W:, :], NT)
        gw_ref[0:ATTN_W, :] += _dot(a, gb, TN)
        gw_ref[ATTN_W:, :] += _dot(s, gb, TN)

    tile = lambda w: pl.BlockSpec((tm, w), lambda i: (i, 0))
    return pl.pallas_call(
        body,
        name="out_proj_loss",
        grid=(SEQ // tm,),
        in_specs=[tile(ATTN_W), tile(SGU_W), tile(D_MODEL), tile(D_MODEL),
                  _full((D_MODEL, D_MODEL)), _full((1, D_MODEL)), _full((1, D_MODEL))],
        out_specs=(tile(D_MODEL), tile(D_MODEL), _full((D_MODEL, D_MODEL)), _full((8, D_MODEL))),
        out_shape=(jax.ShapeDtypeStruct((SEQ, D_MODEL), F32),
                   jax.ShapeDtypeStruct((SEQ, D_MODEL), F32),
                   jax.ShapeDtypeStruct((D_MODEL, D_MODEL), F32),
                   jax.ShapeDtypeStruct((8, D_MODEL), F32)),
        compiler_params=_params(("arbitrary",), VMEM_LIMIT),
    )(ag, sg, x, target, wout, b_out, final_g)


def _sgu_bwd(dmix, gates, ln_g, ln_b, sgu_w, bias_full):
    last = N_BLOCKS - 1

    def body(d_ref, us_ref, vs_ref, zs_ref, lng_ref, lnb_ref, w_ref, bias_ref,
             dp_ref, gw_ref, gb_ref, gln_ref, gbin_ref, wt_ref, gbias_ref):
        c = pl.program_id(0)
        tril = (lax.broadcasted_iota(jnp.int32, (BLOCK, BLOCK), 0)
                >= lax.broadcasted_iota(jnp.int32, (BLOCK, BLOCK), 1))

        @pl.when(c == 0)
        def _():
            gw_ref[...] = jnp.zeros_like(gw_ref)
            gln_ref[...] = jnp.zeros_like(gln_ref)
            gbin_ref[...] = jnp.zeros_like(gbin_ref)
            gbias_ref[...] = jnp.zeros_like(gbias_ref)
            for hh in range(N_SGU_HEADS):
                wt_ref[hh] = jnp.where(tril, w_ref[hh], 0.0).T.astype(MXU_DTYPE)

        us = us_ref[...]
        vs = vs_ref[...]
        lng = lng_ref[...]
        u, vhat, rstd, vln, mixed = _sgu_forward_chunk(us, vs, lng, lnb_ref[...], w_ref, bias_ref)
        low = lax.broadcasted_iota(jnp.int32, (BLOCK, LANES), 1) < HALF
        du_parts, dzs_parts, dvln_parts = [], [], []
        for pair in range(N_SGU_HEADS // 2):
            cols = slice(pair * LANES, (pair + 1) * LANES)
            dsg = d_ref[:, cols]
            gate, gate_grad = _silu_and_grad(zs_ref[:, cols])
            up = u[:, cols]
            du_parts.append(dsg * mixed[pair] * gate)
            dzs_parts.append(dsg * up * mixed[pair] * gate_grad)
            dmixed = dsg * up * gate
            gbias_ref[:, cols] += dmixed
            dm_lo = jnp.where(low, dmixed, 0.0)
            dm_hi = jnp.where(low, 0.0, dmixed)
            vp = vln[:, cols]
            gw_ref[2 * pair] += _dot(dm_lo, vp, NT)
            gw_ref[2 * pair + 1] += _dot(dm_hi, vp, NT)
            dvln_parts.append(_dot(wt_ref[2 * pair], dm_lo) + _dot(wt_ref[2 * pair + 1], dm_hi))
        dvln = jnp.concatenate(dvln_parts, axis=1)
        gln_ref[0:1, :] += jnp.sum(dvln * vhat, axis=0, keepdims=True)
        gln_ref[1:2, :] += jnp.sum(dvln, axis=0, keepdims=True)
        dvhat = dvln * lng
        dvg = rstd * (dvhat - jnp.mean(dvhat, axis=-1, keepdims=True)
                      - vhat * jnp.mean(dvhat * vhat, axis=-1, keepdims=True))
        dus = jnp.concatenate(du_parts, axis=1) * _gelu_grad(us)
        dvs = dvg * _gelu_grad(vs)
        dzs = jnp.concatenate(dzs_parts, axis=1)
        for k, val in enumerate((dus, dvs, dzs)):
            dp_ref[:, k * SGU_W:(k + 1) * SGU_W] = val.astype(MXU_DTYPE)
            gbin_ref[:, k * SGU_W:(k + 1) * SGU_W] += jnp.sum(val, axis=0, keepdims=True)

        @pl.when(c == last)
        def _():
            for hh in range(N_SGU_HEADS):
                gw_ref[hh] = jnp.where(tril, gw_ref[hh], 0.0)
            head_of_lane = lax.broadcasted_iota(jnp.int32, (N_SGU_HEADS, SGU_W), 1) // HEAD_DIM
            select = (head_of_lane == lax.broadcasted_iota(jnp.int32, (N_SGU_HEADS, SGU_W), 0)).astype(F32)
            gb_ref[...] = lax.dot_general(select, gbias_ref[...], NT, precision=lax.Precision.HIGHEST,
                                          preferred_element_type=F32)

    col = lambda k: pl.BlockSpec((BLOCK, SGU_W), lambda n: (n, k))
    return pl.pallas_call(
        body,
        name="sgu_bwd",
        grid=(N_BLOCKS,),
        in_specs=[col(1), col(1), col(2), col(3), _full((1, SGU_W)), _full((1, SGU_W)),
                  _full((N_SGU_HEADS, BLOCK, BLOCK)), _full((BLOCK, SGU_W))],
        out_specs=(pl.BlockSpec((BLOCK, SGU_SECTION), lambda n: (n, 0)),
                   _full((N_SGU_HEADS, BLOCK, BLOCK)), _full((N_SGU_HEADS, BLOCK)),
                   _full((8, SGU_W)), _full((1, SGU_SECTION))),
        out_shape=(jax.ShapeDtypeStruct((SEQ, SGU_SECTION), MXU_DTYPE),
                   jax.ShapeDtypeStruct((N_SGU_HEADS, BLOCK, BLOCK), F32),
                   jax.ShapeDtypeStruct((N_SGU_HEADS, BLOCK), F32),
                   jax.ShapeDtypeStruct((8, SGU_W), F32),
                   jax.ShapeDtypeStruct((1, SGU_SECTION), F32)),
        scratch_shapes=[pltpu.VMEM((N_SGU_HEADS, BLOCK, BLOCK), MXU_DTYPE),
                        pltpu.VMEM((BLOCK, SGU_W), F32)],
        compiler_params=_params(("arbitrary",)),
    )(dmix, gates, gates, gates, ln_g, ln_b, sgu_w, bias_full)


def _attn_bwd(sinks, dmix, q, kvx, out, gates):
    last = N_BLOCKS - 1

    def body(sink_ref, d_ref, q_ref, kc_ref, kp_ref, o_ref, za_ref, dp_ref, gsink_ref, gbin_ref,
             pend_ref, carry_ref):
        n = pl.program_id(0)

        @pl.when(n == 0)
        def _():
            gsink_ref[...] = jnp.zeros_like(gsink_ref)
            gbin_ref[...] = jnp.zeros_like(gbin_ref)
            carry_ref[...] = jnp.zeros_like(carry_ref)

        @pl.when(n > 0)
        def _():
            dp_ref[:, 0:ATTN_W] = pend_ref[:, 0:ATTN_W]
            dp_ref[:, GATE0:ATTN_SECTION] = pend_ref[:, ATTN_W:]

        @pl.when(n > last)
        def _():
            dp_ref[:, KV0:GATE0] = carry_ref[...].astype(MXU_DTYPE)

        @pl.when(n <= last)
        def _():
            valid = _window_mask(n)
            low = lax.broadcasted_iota(jnp.int32, (2 * BLOCK, LANES), 1) < HALF
            lane_row = lax.broadcasted_iota(jnp.int32, (1, LANES), 1)
            gsink = jnp.zeros((1, LANES), F32)
            chains = [(g, par) for g in range(2) for par in range(2)]
            kv = [_kv_cat(kp_ref, kc_ref, 2 * g + par) for g, par in chains]
            qs, douts, dzas, prods = [], [], [], []
            for g in range(2):
                o = o_ref[2 * g:2 * g + 2].reshape(2 * BLOCK, LANES)
                cols = slice(2 * g * LANES, (2 * g + 2) * LANES)
                stack = lambda ref: jnp.concatenate([ref[:, cols][:, 0:LANES], ref[:, cols][:, LANES:]], axis=0)
                dg = stack(d_ref)
                gate, gate_grad = _silu_and_grad(stack(za_ref))
                qs.append(q_ref[2 * g:2 * g + 2].reshape(2 * BLOCK, LANES))
                douts.append((dg * gate).astype(MXU_DTYPE))
                dzas.append(dg * o * gate_grad)
                prods.append(dg * gate * o)

            first = {}

            def issue_first(i):
                g, _ = chains[i]
                first[i] = (_dot(qs[g], kv[i][0], NT), _dot(douts[g], kv[i][1], NT))

            issue_first(0)
            issue_first(1)
            dqs, dk_parts, dv_parts = [], [], []
            for i, (g, par) in enumerate(chains):
                mine = low if par == 0 else jnp.logical_not(low)
                sink = _sink_column(sink_ref, g, par)
                delta = jnp.sum(jnp.where(mine, prods[g], 0.0), axis=1, keepdims=True)
                e, m, den = _softmax_stats(jnp.where(valid, first[i][0], NEG_INF), sink)
                inv = 1.0 / den
                p = (e * inv)
                ds = (p * (first[i][1] - delta)).astype(MXU_DTYPE)
                p = p.astype(MXU_DTYPE)
                gs = jnp.exp(sink - m) * inv * delta
                for k, h in enumerate((4 * g + par, 4 * g + 2 + par)):
                    total = jnp.sum(gs[k * BLOCK:(k + 1) * BLOCK], axis=0, keepdims=True)
                    gsink = jnp.where(lane_row == h, -total, gsink)
                if i + 2 < len(chains):
                    issue_first(i + 2)
                dqs.append(_dot(ds, kv[i][0]))
                dk_parts.append(jnp.where(mine, _dot(ds, qs[g], TN), 0.0))
                dv_parts.append(jnp.where(mine, _dot(p, douts[g], TN), 0.0))
            for g in range(2):
                dq = (dqs[2 * g] + dqs[2 * g + 1]) * SCALE
                dza = dzas[g]
                for i in range(2):
                    pair = 2 * g + i
                    rows = slice(i * BLOCK, (i + 1) * BLOCK)
                    lanes = slice(pair * LANES, (pair + 1) * LANES)
                    pend_ref[:, lanes] = dq[rows].astype(MXU_DTYPE)
                    gbin_ref[:, lanes] += jnp.sum(dq[rows], axis=0, keepdims=True)
                    zl = slice(ATTN_W + pair * LANES, ATTN_W + (pair + 1) * LANES)
                    pend_ref[:, zl] = dza[rows].astype(MXU_DTYPE)
                    gl = slice(GATE0 + pair * LANES, GATE0 + (pair + 1) * LANES)
                    gbin_ref[:, gl] += jnp.sum(dza[rows], axis=0, keepdims=True)
            gsink_ref[...] += gsink
            for k, parts in enumerate((dk_parts, dv_parts)):
                both = parts[0] + parts[3] + pltpu.roll(parts[1] + parts[2], HALF, 1)
                lanes = slice(k * KV_W, (k + 1) * KV_W)
                done = carry_ref[:, lanes] + both[0:BLOCK]
                dp_ref[:, KV0 + k * KV_W:KV0 + (k + 1) * KV_W] = done.astype(MXU_DTYPE)
                carry_ref[:, lanes] = both[BLOCK:]
                gbin_ref[:, KV0 + k * KV_W:KV0 + (k + 1) * KV_W] += jnp.sum(both, axis=0, keepdims=True)

    at = lambda n: jnp.minimum(n, last)
    blk = lambda w: pl.BlockSpec((BLOCK, w), lambda n: (at(n), 0))
    tiles = pl.BlockSpec((N_PAIRS, BLOCK, LANES), lambda n: (0, at(n), 0))
    return pl.pallas_call(
        body,
        name="attn_bwd",
        grid=(N_BLOCKS + 1,),
        in_specs=[pl.BlockSpec(memory_space=pltpu.SMEM),
                  blk(ATTN_W),
                  tiles,
                  blk(8 * LANES),
                  pl.BlockSpec((BLOCK, 8 * LANES), lambda n: (jnp.maximum(at(n) - 1, 0), 0)),
                  tiles,
                  blk(ATTN_W)],
        out_specs=(pl.BlockSpec((BLOCK, ATTN_SECTION), lambda n: (jnp.maximum(n - 1, 0), 0)),
                   _full((1, LANES)), _full((1, ATTN_SECTION))),
        out_shape=(jax.ShapeDtypeStruct((SEQ, ATTN_SECTION), MXU_DTYPE),
                   jax.ShapeDtypeStruct((1, LANES), F32),
                   jax.ShapeDtypeStruct((1, ATTN_SECTION), F32)),
        scratch_shapes=[pltpu.VMEM((BLOCK, 2 * ATTN_W), MXU_DTYPE), pltpu.VMEM((BLOCK, 2 * KV_W), F32)],
        compiler_params=_params(("arbitrary",)),
    )(sinks, dmix, q, kvx, kvx, out, gates)


def _in_proj_bwd(dpa, dps, win_t, x, norm_g, gres):
    tm = TOKEN_TILE

    def body(da_ref, ds_ref, w_ref, x_ref, g_ref, gres_ref, gx_ref, gng_ref):
        @pl.when(pl.program_id(0) == 0)
        def _():
            gng_ref[...] = jnp.zeros_like(gng_ref)

        dh = _dot(da_ref[...], w_ref[0:ATTN_SECTION, :]) + _dot(ds_ref[...], w_ref[ATTN_SECTION:, :])
        xv = x_ref[...]
        r = lax.rsqrt(jnp.mean(xv * xv, axis=-1, keepdims=True) + NORM_EPS)
        xn = xv * r
        gng_ref[...] += jnp.sum(dh * xn, axis=0, keepdims=True)
        dxn = dh * g_ref[...]
        gx_ref[...] = r * (dxn - xn * jnp.mean(dxn * xn, axis=-1, keepdims=True)) + gres_ref[...]

    tile = lambda w: pl.BlockSpec((tm, w), lambda i: (i, 0))
    return pl.pallas_call(
        body,
        name="in_proj_bwd",
        grid=(SEQ // tm,),
        in_specs=[tile(ATTN_SECTION), tile(SGU_SECTION), _full((IN_W, D_MODEL)), tile(D_MODEL),
                  _full((1, D_MODEL)), tile(D_MODEL)],
        out_specs=(tile(D_MODEL), _full((1, D_MODEL))),
        out_shape=(jax.ShapeDtypeStruct((SEQ, D_MODEL), F32), jax.ShapeDtypeStruct((1, D_MODEL), F32)),
        compiler_params=_params(("arbitrary",), VMEM_LIMIT),
    )(dpa, dps, win_t, x, norm_g, gres)


def _win_grad(dpa, dps, h):
    rows = 256
    n_attn = ATTN_SECTION // rows
    n_sgu = SGU_SECTION // rows

    def body(da_ref, ds_ref, h_ref, o_ref):
        i = pl.program_id(0)

        @pl.when(i < n_attn)
        def _():
            o_ref[...] = _dot(da_ref[...], h_ref[...], TN)

        @pl.when(i >= n_attn)
        def _():
            o_ref[...] = _dot(ds_ref[...], h_ref[...], TN)

    return pl.pallas_call(
        body,
        name="win_grad",
        grid=(n_attn + n_sgu,),
        in_specs=[pl.BlockSpec((SEQ, rows), lambda i: (0, jnp.minimum(i, n_attn - 1))),
                  pl.BlockSpec((SEQ, rows), lambda i: (0, jnp.maximum(i - n_attn, 0))),
                  _full((SEQ, D_MODEL))],
        out_specs=pl.BlockSpec((rows, D_MODEL), lambda i: (i, 0)),
        out_shape=jax.ShapeDtypeStruct((IN_W, D_MODEL), F32),
        compiler_params=_params(("arbitrary",), VMEM_LIMIT),
    )(dpa, dps, h)


VEC_NORM_G, VEC_B_IN, VEC_SINKS, VEC_LN_G, VEC_LN_B, VEC_B_OUT, VEC_FINAL_G, VEC_LOSS, VEC_SGU_B = 0, 1, 2, 3, 4, 5, 6, 7, 8


def _reduce_grads(gwin, gwout, gsguw, gng, gbin_a, gbin_s, gsink, gln, gsgub, vec4):
    def body(gwin_ref, gwout_ref, gsguw_ref, gng_ref, gba_ref, gbs_ref, gsink_ref, gln_ref, gsgub_ref, vec4_ref,
             owin_ref, owout_ref, osguw_ref, ovec_ref,
             vec_ref, ra_vec, slots_vec,
             sa_win, ra_win, sb_win, rc_win,
             sa_wout, ra_wout, sb_wout, rc_wout,
             sa_sguw, ra_sguw, sb_sguw, rc_sguw,
             send_sems, recv_sems):
        x, y, c = _place()
        sibling = (x, y, 1 - c)
        relations = [(0, 0), (1, 0), (0, 1), (1, 1)]

        def chip(rel):
            return (x + rel[0] - 2 * x * rel[0], y + rel[1] - 2 * y * rel[1])

        def owner_rows(rel, core, n_rows):
            px, py = chip(rel)
            return pl.ds(pl.multiple_of((4 * px + 2 * py + core) * n_rows, 8), n_rows)

        def slot(ref, i, n_rows):
            return ref.at[pl.ds(i * n_rows, n_rows), :]

        def remote(src, dst, k, to):
            return pltpu.make_async_remote_copy(src_ref=src, dst_ref=dst, send_sem=send_sems.at[k],
                                                recv_sem=recv_sems.at[k], device_id=to, device_id_type=MESH)

        big = [
            (gwin_ref, WIN_ROWS, sa_win, ra_win, sb_win, rc_win, owin_ref),
            (gwout_ref, WOUT_ROWS, sa_wout, ra_wout, sb_wout, rc_wout, owout_ref),
            (gsguw_ref, SGUW_ROWS, sa_sguw, ra_sguw, sb_sguw, rc_sguw, None),
        ]
        n_arrays = len(big) + 1

        vec_ref[...] = jnp.zeros_like(vec_ref)
        vec_ref[VEC_NORM_G:VEC_NORM_G + 1, 0:D_MODEL] = gng_ref[...]
        vec_ref[VEC_B_IN:VEC_B_IN + 1, 0:ATTN_SECTION] = gba_ref[...]
        vec_ref[VEC_B_IN:VEC_B_IN + 1, ATTN_SECTION:IN_W] = gbs_ref[...]
        vec_ref[VEC_SINKS:VEC_SINKS + 1, 0:LANES] = gsink_ref[...]
        vec_ref[VEC_LN_G:VEC_LN_G + 1, 0:SGU_W] = gln_ref[0:1, :]
        vec_ref[VEC_LN_B:VEC_LN_B + 1, 0:SGU_W] = gln_ref[1:2, :]
        vec_ref[VEC_B_OUT:VEC_B_OUT + 1, 0:D_MODEL] = vec4_ref[2:3, :]
        vec_ref[VEC_FINAL_G:VEC_FINAL_G + 1, 0:D_MODEL] = vec4_ref[1:2, :]
        vec_ref[VEC_LOSS:VEC_LOSS + 1, 0:D_MODEL] = vec4_ref[0:1, :]
        vec_ref[VEC_SGU_B:VEC_SGU_B + N_SGU_HEADS, 0:BLOCK] = gsgub_ref[...]

        level1 = []
        for a, (part, n, sa, ra, _, _, _) in enumerate(big):
            for i, rel in enumerate(relations):
                sa[i * n:(i + 1) * n, :] = part[owner_rows(rel, 1 - c, n), :].astype(sa.dtype)
            level1.append(remote(sa, ra, a, sibling))
        level1.append(remote(vec_ref, ra_vec, len(big), sibling))
        for cp in level1:
            cp.start()

        level2 = []
        for a, (part, n, sa, ra, sb, rc, res) in enumerate(big):
            level1[a].wait_recv()
            for i, rel in enumerate(relations):
                total = part[owner_rows(rel, c, n), :] + ra[i * n:(i + 1) * n, :].astype(F32)
                if i == 0:
                    if res is None:
                        sb_sguw_own = total
                    else:
                        res[...] = total
                else:
                    sb[(i - 1) * n:i * n, :] = total.astype(sb.dtype)
                    cp = remote(slot(sb, i - 1, n), slot(rc, i - 1, n), n_arrays * i + a, (*chip(rel), c))
                    cp.start()
                    level2.append(cp)
        level1[len(big)].wait_recv()
        my_slot = pl.ds(pl.multiple_of((2 * x + y) * VEC_ROWS, 8), VEC_ROWS)
        slots_vec[my_slot, :] = vec_ref[...] + ra_vec[...]
        for i, rel in enumerate(relations[1:], start=1):
            cp = remote(slots_vec.at[my_slot, :], slots_vec.at[my_slot, :], n_arrays * i + len(big), (*chip(rel), c))
            cp.start()
            level2.append(cp)

        def arrived(a, i, ref_slice):
            remote(ref_slice, ref_slice, n_arrays * i + a, sibling).wait_recv()

        for a, (part, n, sa, ra, sb, rc, res) in enumerate(big):
            acc = sb_sguw_own if res is None else res[...]
            for i in range(1, len(relations)):
                arrived(a, i, slot(rc, i - 1, n))
                acc = acc + rc[(i - 1) * n:i * n, :].astype(F32)
            if res is None:
                osguw_ref[owner_rows(relations[0], c, n), :] = acc
            else:
                res[...] = acc
        for i, rel in enumerate(relations[1:], start=1):
            px, py = chip(rel)
            theirs = pl.ds(pl.multiple_of((2 * px + py) * VEC_ROWS, 8), VEC_ROWS)
            arrived(len(big), i, slots_vec.at[theirs, :])
        ovec_ref[...] = ((slots_vec[0:VEC_ROWS, :] + slots_vec[VEC_ROWS:2 * VEC_ROWS, :])
                         + slots_vec[2 * VEC_ROWS:3 * VEC_ROWS, :]) + slots_vec[3 * VEC_ROWS:, :]

        mine = osguw_ref.at[owner_rows(relations[0], c, SGUW_ROWS), :]
        base = n_arrays * len(relations)
        others = [(rel, flip) for flip in (0, 1) for rel in relations if (rel, flip) != ((0, 0), 0)]
        gathers = []
        for k, (rel, flip) in enumerate(others):
            core = c + flip - 2 * c * flip
            cp = remote(mine, mine, base + k, (*chip(rel), core))
            cp.start()
            gathers.append(cp)
        for k, (rel, flip) in enumerate(others):
            core = c + flip - 2 * c * flip
            theirs = osguw_ref.at[owner_rows(rel, core, SGUW_ROWS), :]
            remote(theirs, theirs, base + k, sibling).wait_recv()
        for cp in level1 + level2 + gathers:
            cp.wait_send()

    n_sems = 4 * 4 + 7
    comm = jnp.dtype(COMM_DTYPE)
    return pl.pallas_call(
        body,
        name="reduce_grads",
        out_shape=(jax.ShapeDtypeStruct((WIN_ROWS, D_MODEL), F32),
                   jax.ShapeDtypeStruct((WOUT_ROWS, D_MODEL), F32),
                   jax.ShapeDtypeStruct((N_SGU_HEADS * BLOCK, BLOCK), F32),
                   jax.ShapeDtypeStruct((VEC_ROWS, IN_W), F32)),
        in_specs=[VMEM_SPEC] * 10,
        out_specs=(VMEM_SPEC,) * 4,
        scratch_shapes=[
            pltpu.VMEM((VEC_ROWS, IN_W), F32), pltpu.VMEM((VEC_ROWS, IN_W), F32), pltpu.VMEM((4 * VEC_ROWS, IN_W), F32),
            pltpu.VMEM((4 * WIN_ROWS, D_MODEL), comm), pltpu.VMEM((4 * WIN_ROWS, D_MODEL), comm),
            pltpu.VMEM((3 * WIN_ROWS, D_MODEL), comm), pltpu.VMEM((3 * WIN_ROWS, D_MODEL), comm),
            pltpu.VMEM((4 * WOUT_ROWS, D_MODEL), comm), pltpu.VMEM((4 * WOUT_ROWS, D_MODEL), comm),
            pltpu.VMEM((3 * WOUT_ROWS, D_MODEL), comm), pltpu.VMEM((3 * WOUT_ROWS, D_MODEL), comm),
            pltpu.VMEM((4 * SGUW_ROWS, BLOCK), F32), pltpu.VMEM((4 * SGUW_ROWS, BLOCK), F32),
            pltpu.VMEM((3 * SGUW_ROWS, BLOCK), F32), pltpu.VMEM((3 * SGUW_ROWS, BLOCK), F32),
            pltpu.SemaphoreType.DMA((n_sems,)), pltpu.SemaphoreType.DMA((n_sems,)),
        ],
        compiler_params=_params(vmem=VMEM_LIMIT),
    )(gwin, gwout, gsguw, gng, gbin_a, gbin_s, gsink, gln, gsgub, vec4)


def _adamw(w, g, m, v):
    m = ADAM_B1 * m + (1.0 - ADAM_B1) * g
    v = ADAM_B2 * v + (1.0 - ADAM_B2) * (g * g)
    m_hat = m / (1.0 - ADAM_B1 ** ADAM_STEP)
    v_hat = v / (1.0 - ADAM_B2 ** ADAM_STEP)
    delta = -ADAM_LR * (m_hat / (jnp.sqrt(v_hat) + ADAM_EPS) + ADAM_WD * w)
    return delta, m, v


def _adamw_shard(name, g, w, m, v, block_rows):
    def body(g_ref, w_ref, m_ref, v_ref, d_ref, nm_ref, nv_ref):
        d_ref[...], nm_ref[...], nv_ref[...] = _adamw(w_ref[...], g_ref[...], m_ref[...], v_ref[...])

    rows, cols = w.shape
    spec = pl.BlockSpec((block_rows, cols), lambda i: (i, 0))
    return pl.pallas_call(
        body,
        name=name,
        grid=(rows // block_rows,),
        in_specs=[spec] * 4,
        out_specs=(spec,) * 3,
        out_shape=(jax.ShapeDtypeStruct(w.shape, F32),) * 3,
        compiler_params=_params(("arbitrary",)),
    )(g, w, m, v)


def _adamw_replicated(vec, gsguw, weights, m_state, v_state):
    n = len(SMALL)

    def body(*refs):
        vec_ref, gsguw_ref = refs[0], refs[1]
        w_refs, m_refs, v_refs = (refs[2 + k * n:2 + (k + 1) * n] for k in range(3))
        outs = refs[2 + 3 * n:]
        g_refs, d_refs, nm_refs, nv_refs = (outs[k * n:(k + 1) * n] for k in range(4))
        for i, (_, row, shape) in enumerate(SMALL):
            if row is None:
                g = gsguw_ref[...]
            else:
                g = vec_ref[row:row + shape[0], 0:shape[1]]
            g_refs[i][...] = g
            d_refs[i][...], nm_refs[i][...], nv_refs[i][...] = _adamw(
                w_refs[i][...], g, m_refs[i][...], v_refs[i][...])

    shapes = tuple(jax.ShapeDtypeStruct(shape, F32) for _, _, shape in SMALL)
    outs = pl.pallas_call(
        body,
        name="adamw_replicated",
        in_specs=[VMEM_SPEC] * (2 + 3 * n),
        out_specs=(VMEM_SPEC,) * (4 * n),
        out_shape=shapes * 4,
    )(vec, gsguw, *weights, *m_state, *v_state)
    return tuple(outs[k * n:(k + 1) * n] for k in range(4))


SMALL = (
    ("norm_g", VEC_NORM_G, (1, D_MODEL)),
    ("b_in", VEC_B_IN, (1, IN_W)),
    ("attn_sinks", VEC_SINKS, (1, N_Q_HEADS)),
    ("sgu_ln_g", VEC_LN_G, (1, SGU_W)),
    ("sgu_ln_b", VEC_LN_B, (1, SGU_W)),
    ("sgu_w", None, (N_SGU_HEADS * BLOCK, BLOCK)),
    ("sgu_b", VEC_SGU_B, (N_SGU_HEADS, BLOCK)),
    ("b_out", VEC_B_OUT, (1, D_MODEL)),
    ("final_norm_g", VEC_FINAL_G, (1, D_MODEL)),
)


def _local_grads(x, target, win_t, wout, norm_g, b_in, attn_sinks, sgu_ln_g, sgu_ln_b, sgu_w, sgu_b, b_out,
                 final_g):
    sinks = attn_sinks.reshape(N_Q_HEADS)
    bias_full = jnp.repeat(sgu_b.T, HEAD_DIM, axis=1)
    h, q, kvx, gates = _in_proj(x, norm_g, b_in, win_t)
    out, ag = _attn_fwd(sinks, q, kvx, gates)
    sg = _sgu_fwd(gates, sgu_ln_g, sgu_ln_b, sgu_w, bias_full)
    gres, dmix, gwout, vec4 = _out_proj_loss(ag, sg, x, target, wout, b_out, final_g)
    dps, gsguw, gsgub, gln, gbin_s = _sgu_bwd(dmix, gates, sgu_ln_g, sgu_ln_b, sgu_w, bias_full)
    dpa, gsink, gbin_a = _attn_bwd(sinks, dmix, q, kvx, out, gates)
    grad_x, gng = _in_proj_bwd(dpa, dps, win_t, x, norm_g, gres)
    gwin = _win_grad(dpa, dps, h)
    return grad_x, (gwin, gwout, gsguw.reshape(N_SGU_HEADS * BLOCK, BLOCK), gng, gbin_a, gbin_s, gsink, gln, gsgub,
                    vec4)


def kernel(x, norm_g, w_in, b_in, attn_sinks, sgu_ln_g, sgu_ln_b, sgu_w, sgu_b, w_out, b_out, final_norm_g, loss_target, m_norm_g, m_w_in, m_b_in, m_attn_sinks, m_sgu_ln_g, m_sgu_ln_b, m_sgu_w, m_sgu_b, m_w_out, m_b_out, m_final_norm_g, v_norm_g, v_w_in, v_b_in, v_attn_sinks, v_sgu_ln_g, v_sgu_ln_b, v_sgu_w, v_sgu_b, v_w_out, v_b_out, v_final_norm_g):
    given = dict(norm_g=norm_g, b_in=b_in, attn_sinks=attn_sinks, sgu_ln_g=sgu_ln_g, sgu_ln_b=sgu_ln_b,
                 sgu_w=sgu_w, sgu_b=sgu_b, b_out=b_out, final_norm_g=final_norm_g)
    m_given = dict(norm_g=m_norm_g, b_in=m_b_in, attn_sinks=m_attn_sinks, sgu_ln_g=m_sgu_ln_g,
                   sgu_ln_b=m_sgu_ln_b, sgu_w=m_sgu_w, sgu_b=m_sgu_b, b_out=m_b_out, final_norm_g=m_final_norm_g)
    v_given = dict(norm_g=v_norm_g, b_in=v_b_in, attn_sinks=v_attn_sinks, sgu_ln_g=v_sgu_ln_g,
                   sgu_ln_b=v_sgu_ln_b, sgu_w=v_sgu_w, sgu_b=v_sgu_b, b_out=v_b_out, final_norm_g=v_final_norm_g)

    win_t, wout = _all_gather_weights(w_in[0].T, w_out[0])
    grad_x, partials = _local_grads(
        x[0], loss_target[0], win_t, wout, norm_g, b_in, attn_sinks, sgu_ln_g, sgu_ln_b, sgu_w[0], sgu_b[0],
        b_out, final_norm_g.reshape(1, D_MODEL))
    gwin_t, gwout, gsguw, vec = _reduce_grads(*partials)
    loss = vec[VEC_LOSS, 0]

    gwin = gwin_t.T
    d_win, nm_win, nv_win = _adamw_shard("adamw_w_in", gwin, w_in[0], m_w_in[0], v_w_in[0], 256)
    d_wout, nm_wout, nv_wout = _adamw_shard("adamw_w_out", gwout, w_out[0], m_w_out[0], v_w_out[0], WOUT_ROWS)
    as_2d = lambda d: [d[name].reshape(shape) for name, _, shape in SMALL]
    small = _adamw_replicated(vec, gsguw, as_2d(given), as_2d(m_given), as_2d(v_given))

    def assemble(big_in, big_out, k):
        vals = {name: small[k][i].reshape(given[name].shape) for i, (name, _, _) in enumerate(SMALL)}
        vals["w_in"] = big_in[None]
        vals["w_out"] = big_out[None]
        order = ("norm_g", "w_in", "b_in", "attn_sinks", "sgu_ln_g", "sgu_ln_b", "sgu_w", "sgu_b", "w_out",
                 "b_out", "final_norm_g")
        return [vals[name] for name in order]

    return (loss, grad_x[None],
            *assemble(gwin, gwout, 0), *assemble(d_win, d_wout, 1),
            *assemble(nm_win, nm_wout, 2), *assemble(nv_win, nv_wout, 3))
```

```python
import functools
import math

import jax
import jax.numpy as jnp
from jax import lax
from jax.experimental import pallas as pl
from jax.experimental.pallas import tpu as pltpu

F32 = jnp.float32
BF16 = jnp.bfloat16
MXU_DTYPE = BF16
COMM_DTYPE = BF16

D_MODEL = 1024
SEQ = 4096
HEAD_DIM = 64
N_Q_HEADS = 8
Q_PER_KV = 4
BLOCK = 128
N_BLOCKS = SEQ // BLOCK
ATTN_W = 512
KV_W = 128
SGU_W = 512
N_SGU_HEADS = 8
IN_W = 2816
NORM_EPS = 1e-5
NEG_INF = -1e30
SCALE = HEAD_DIM ** -0.5
KV0 = ATTN_W
GATE0 = ATTN_W + 2 * KV_W
SGU0 = GATE0 + ATTN_W
ATTN_SECTION = SGU0
SGU_SECTION = IN_W - SGU0

ADAM_LR = 0.001
ADAM_B1 = 0.9
ADAM_B2 = 0.999
ADAM_EPS = 1e-08
ADAM_WD = 0.01
ADAM_STEP = 10

N_DEV = 8
WIN_ROWS = IN_W // N_DEV
WOUT_ROWS = D_MODEL // N_DEV
SGUW_ROWS = N_SGU_HEADS * BLOCK // N_DEV
VEC_ROWS = 16
MESH = pl.DeviceIdType.MESH

LANES = 128
HALF = LANES // 2
N_PAIRS = N_Q_HEADS * HEAD_DIM // LANES
TOKEN_TILE = 256
VMEM_LIMIT = 56 * 1024 * 1024

NN = (((1,), (0,)), ((), ()))
NT = (((1,), (1,)), ((), ()))
TN = (((0,), (0,)), ((), ()))


def _dot(a, b, dims=NN):
    return lax.dot_general(a.astype(MXU_DTYPE), b.astype(MXU_DTYPE), dims, preferred_element_type=F32)


def _gelu(x):
    return x * (lax.erf(x * (1.0 / math.sqrt(2.0))) + 1.0) * 0.5


def _gelu_grad(x):
    cdf = (lax.erf(x * (1.0 / math.sqrt(2.0))) + 1.0) * 0.5
    return cdf + x * jnp.exp(-0.5 * x * x) * (1.0 / math.sqrt(2.0 * math.pi))


def _silu_and_grad(z):
    s = jax.nn.sigmoid(z)
    return z * s, s * (1.0 + z * (1.0 - s))


def _params(semantics=None, vmem=None):
    kw = {}
    if semantics is not None:
        kw["dimension_semantics"] = semantics
    if vmem is not None:
        kw["vmem_limit_bytes"] = vmem
    return pltpu.CompilerParams(**kw)


def _full(shape):
    return pl.BlockSpec(shape, lambda *_: (0,) * len(shape))


VMEM_SPEC = pl.BlockSpec(memory_space=pltpu.VMEM)


RELATIONS = ((0, 0), (1, 0), (0, 1), (1, 1))


def _place():
    return lax.axis_index("x"), lax.axis_index("y"), lax.axis_index("c")


def _chip(rel):
    x, y, _ = _place()
    return (1 - x if rel[0] else x, 1 - y if rel[1] else y)


def _block_rows(place, n_rows):
    px, py, pc = place
    return pl.ds(pl.multiple_of((4 * px + 2 * py + pc) * n_rows, 16), n_rows)


class _Copies:
    def __init__(self, send_sems, recv_sems):
        self.send_sems, self.recv_sems = send_sems, recv_sems

    def __call__(self, k, src, dst, to):
        return pltpu.make_async_remote_copy(src_ref=src, dst_ref=dst, send_sem=self.send_sems.at[k],
                                            recv_sem=self.recv_sems.at[k], device_id=to, device_id_type=MESH)


def _gather_plan(copies, sem0, full_ref, n_rows):
    x, y, c = _place()
    me, sibling = (x, y, c), (x, y, 1 - c)
    chips = [_chip(rel) for rel in RELATIONS[1:]]

    def cp(k, block, to):
        rows = full_ref.at[_block_rows(block, n_rows), :]
        return copies(sem0 + k, rows, rows, to)

    first = [cp(0, me, sibling)] + [cp(1 + j, me, (*chip, c)) for j, chip in enumerate(chips)]
    passed = [cp(4 + j, (*chip, c), sibling) for j, chip in enumerate(chips)]

    def start():
        for f in first:
            f.start()

    def forward():
        for j, chip in enumerate(chips):
            cp(1 + j, (*chip, c), me).wait_recv()
            passed[j].start()

    def finish():
        cp(0, sibling, me).wait_recv()
        for j, chip in enumerate(chips):
            cp(4 + j, (*chip, 1 - c), me).wait_recv()
        for f in first + passed:
            f.wait_send()

    return start, forward, finish


GATHER_SEMS = 7


def _reduce_scatter_plan(copies, sem0, part_ref, n_rows, sa, ra, sb, rc, res_ref):
    x, y, c = _place()
    sibling = (x, y, 1 - c)
    n = n_rows
    level1 = copies(sem0, sa, ra, sibling)

    def level2(i):
        slot = pl.ds((i - 1) * n, n)
        return copies(sem0 + i, sb.at[slot, :], rc.at[slot, :], (*_chip(RELATIONS[i]), c))

    def start():
        for i, rel in enumerate(RELATIONS):
            sa[i * n:(i + 1) * n, :] = part_ref[_block_rows((*_chip(rel), 1 - c), n), :].astype(sa.dtype)
        level1.start()

    def exchange():
        level1.wait_recv()
        for i, rel in enumerate(RELATIONS):
            total = part_ref[_block_rows((*_chip(rel), c), n), :] + ra[i * n:(i + 1) * n, :].astype(F32)
            if i == 0:
                res_ref[...] = total
            else:
                sb[(i - 1) * n:i * n, :] = total.astype(sb.dtype)
                level2(i).start()

    def finish():
        acc = res_ref[...]
        for i in range(1, len(RELATIONS)):
            level2(i).wait_recv()
            acc = acc + rc[(i - 1) * n:i * n, :].astype(F32)
        res_ref[...] = acc
        level1.wait_send()
        for i in range(1, len(RELATIONS)):
            level2(i).wait_send()

    return start, exchange, finish


REDUCE_SEMS = 4


def _reduce_scatter_scratch(n_rows, width, dtype):
    return [pltpu.VMEM((4 * n_rows, width), dtype), pltpu.VMEM((4 * n_rows, width), dtype),
            pltpu.VMEM((3 * n_rows, width), dtype), pltpu.VMEM((3 * n_rows, width), dtype)]


def _dma_sems(n):
    return [pltpu.SemaphoreType.DMA((n,)), pltpu.SemaphoreType.DMA((n,))]


def _all_gather_win(win_t_shard):
    def body(win_ref, full_ref, send_sems, recv_sems):
        full_ref[_block_rows(_place(), WIN_ROWS), :] = win_ref[...].astype(COMM_DTYPE)
        start, forward, finish = _gather_plan(_Copies(send_sems, recv_sems), 0, full_ref, WIN_ROWS)
        start()
        forward()
        finish()

    return pl.pallas_call(
        body,
        name="all_gather_win",
        out_shape=jax.ShapeDtypeStruct((IN_W, D_MODEL), COMM_DTYPE),
        in_specs=[VMEM_SPEC],
        out_specs=VMEM_SPEC,
        scratch_shapes=_dma_sems(GATHER_SEMS),
        compiler_params=_params(vmem=VMEM_LIMIT),
    )(win_t_shard)


def _in_proj(x, norm_g, b_in, win_t, wout_shard):
    tm = TOKEN_TILE
    steps = SEQ // tm

    def body(x_ref, g_ref, b_ref, w_ref, wout_ref, h_ref, q_ref, kvx_ref, gate_ref, wfull_ref,
             landing, send_sems, recv_sems):
        step = pl.program_id(0)
        start, forward, finish = _gather_plan(_Copies(send_sems, recv_sems), 0, landing, WOUT_ROWS)

        @pl.when(step == 0)
        def _():
            landing[_block_rows(_place(), WOUT_ROWS), :] = wout_ref[...].astype(COMM_DTYPE)
            start()

        pl.when(step == steps // 2)(forward)

        xv = x_ref[...]
        r = lax.rsqrt(jnp.mean(xv * xv, axis=-1, keepdims=True) + NORM_EPS)
        h = ((xv * r) * g_ref[...]).astype(MXU_DTYPE)
        h_ref[...] = h

        def proj(lo, hi):
            return _dot(h, w_ref[lo:hi, :], NT) + b_ref[:, lo:hi]

        qs = proj(0, ATTN_W) * SCALE
        for pair in range(N_PAIRS):
            q_ref[pair] = qs[:, pair * LANES:(pair + 1) * LANES].astype(MXU_DTYPE)
        kv = proj(KV0, GATE0)
        low = lax.broadcasted_iota(jnp.int32, (tm, LANES), 1) < HALF
        for i in range(2):
            t = kv[:, i * LANES:(i + 1) * LANES]
            rot = pltpu.roll(t, HALF, 1)
            variants = (jnp.where(low, t, 0.0), jnp.where(low, 0.0, rot),
                        jnp.where(low, rot, 0.0), jnp.where(low, 0.0, t))
            for j, val in enumerate(variants):
                col = (4 * i + j) * LANES
                kvx_ref[:, col:col + LANES] = val.astype(MXU_DTYPE)
        gate_ref[...] = proj(GATE0, IN_W)

        @pl.when(step == steps - 1)
        def _():
            finish()
            wfull_ref[...] = landing[...]

    return pl.pallas_call(
        body,
        name="in_proj",
        grid=(steps,),
        in_specs=[pl.BlockSpec((tm, D_MODEL), lambda i: (i, 0)),
                  _full((1, D_MODEL)), _full((1, IN_W)), _full((IN_W, D_MODEL)), VMEM_SPEC],
        out_specs=(pl.BlockSpec((tm, D_MODEL), lambda i: (i, 0)),
                   pl.BlockSpec((N_PAIRS, tm, LANES), lambda i: (0, i, 0)),
                   pl.BlockSpec((tm, 8 * LANES), lambda i: (i, 0)),
                   pl.BlockSpec((tm, IN_W - GATE0), lambda i: (i, 0)),
                   _full((D_MODEL, D_MODEL))),
        out_shape=(jax.ShapeDtypeStruct((SEQ, D_MODEL), MXU_DTYPE),
                   jax.ShapeDtypeStruct((N_PAIRS, SEQ, LANES), MXU_DTYPE),
                   jax.ShapeDtypeStruct((SEQ, 8 * LANES), MXU_DTYPE),
                   jax.ShapeDtypeStruct((SEQ, IN_W - GATE0), F32),
                   jax.ShapeDtypeStruct((D_MODEL, D_MODEL), COMM_DTYPE)),
        scratch_shapes=[pltpu.VMEM((D_MODEL, D_MODEL), COMM_DTYPE)] + _dma_sems(GATHER_SEMS),
        compiler_params=_params(("arbitrary",), VMEM_LIMIT),
    )(x, norm_g, b_in, win_t, wout_shard)


def _window_mask(n):
    qi = lax.broadcasted_iota(jnp.int32, (2 * BLOCK, 2 * BLOCK), 0) & (BLOCK - 1)
    p = lax.broadcasted_iota(jnp.int32, (2 * BLOCK, 2 * BLOCK), 1) - BLOCK
    in_window = jnp.logical_and(p <= qi, p > qi - BLOCK)
    return jnp.logical_and(in_window, jnp.logical_or(p >= 0, n > 0))


def _sink_column(sink_ref, g, par):
    return jnp.concatenate([jnp.full((BLOCK, 1), sink_ref[4 * g + par], F32),
                            jnp.full((BLOCK, 1), sink_ref[4 * g + 2 + par], F32)], axis=0)


def _kv_cat(kp_ref, kc_ref, var):
    kcol, vcol = var * LANES, (var + 4) * LANES
    return (jnp.concatenate([kp_ref[:, kcol:kcol + LANES], kc_ref[:, kcol:kcol + LANES]], axis=0),
            jnp.concatenate([kp_ref[:, vcol:vcol + LANES], kc_ref[:, vcol:vcol + LANES]], axis=0))


def _softmax_stats(s, sink):
    m = jnp.maximum(jnp.max(s, axis=1, keepdims=True), sink)
    e = jnp.exp(s - m)
    return e, m, jnp.sum(e, axis=1, keepdims=True) + jnp.exp(sink - m)


def _attn_fwd(sinks, q, kvx, gates):
    def body(sink_ref, q_ref, kc_ref, kp_ref, za_ref, out_ref, ag_ref):
        valid = _window_mask(pl.program_id(0))
        chains = [(g, par) for g in range(2) for par in range(2)]
        kv = [_kv_cat(kp_ref, kc_ref, 2 * g + par) for g, par in chains]
        scores, outs = {}, []

        def issue_scores(i):
            g, _ = chains[i]
            scores[i] = _dot(q_ref[2 * g:2 * g + 2].reshape(2 * BLOCK, LANES), kv[i][0], NT)

        issue_scores(0)
        issue_scores(1)
        for i, (g, par) in enumerate(chains):
            e, _, den = _softmax_stats(jnp.where(valid, scores[i], NEG_INF), _sink_column(sink_ref, g, par))
            if i + 2 < len(chains):
                issue_scores(i + 2)
            outs.append(_dot(e, kv[i][1]) * (1.0 / den))
        for g in range(2):
            acc = outs[2 * g] + outs[2 * g + 1]
            for i in range(2):
                pair = 2 * g + i
                o = acc[i * BLOCK:(i + 1) * BLOCK]
                out_ref[pair] = o
                gate, _ = _silu_and_grad(za_ref[:, pair * LANES:(pair + 1) * LANES])
                ag_ref[:, pair * LANES:(pair + 1) * LANES] = (o * gate).astype(MXU_DTYPE)

    blk = lambda w: pl.BlockSpec((BLOCK, w), lambda n: (n, 0))
    tiles = pl.BlockSpec((N_PAIRS, BLOCK, LANES), lambda n: (0, n, 0))
    return pl.pallas_call(
        body,
        name="attn_fwd",
        grid=(N_BLOCKS,),
        in_specs=[pl.BlockSpec(memory_space=pltpu.SMEM), tiles, blk(8 * LANES),
                  pl.BlockSpec((BLOCK, 8 * LANES), lambda n: (jnp.maximum(n - 1, 0), 0)),
                  blk(ATTN_W)],
        out_specs=(tiles, blk(ATTN_W)),
        out_shape=(jax.ShapeDtypeStruct((N_PAIRS, SEQ, LANES), F32),
                   jax.ShapeDtypeStruct((SEQ, ATTN_W), MXU_DTYPE)),
        compiler_params=_params(("arbitrary",)),
    )(sinks, q, kvx, kvx, gates)


def _sgu_forward_chunk(us, vs, lng, lnb, w_ref, bias_ref):
    u = _gelu(us)
    vg = _gelu(vs)
    mu = jnp.mean(vg, axis=-1, keepdims=True)
    xc = vg - mu
    rstd = lax.rsqrt(jnp.mean(xc * xc, axis=-1, keepdims=True) + NORM_EPS)
    vhat = xc * rstd
    vln = vhat * lng + lnb
    low = lax.broadcasted_iota(jnp.int32, (BLOCK, LANES), 1) < HALF
    tril = (lax.broadcasted_iota(jnp.int32, (BLOCK, BLOCK), 0)
            >= lax.broadcasted_iota(jnp.int32, (BLOCK, BLOCK), 1))
    mixed = []
    for pair in range(N_SGU_HEADS // 2):
        vp = vln[:, pair * LANES:(pair + 1) * LANES]
        w0 = jnp.where(tril, w_ref[2 * pair], 0.0)
        w1 = jnp.where(tril, w_ref[2 * pair + 1], 0.0)
        mixed.append(_dot(w0, jnp.where(low, vp, 0.0)) + _dot(w1, jnp.where(low, 0.0, vp))
                     + bias_ref[:, pair * LANES:(pair + 1) * LANES])
    return u, vhat, rstd, vln, mixed


def _sgu_fwd(gates, ln_g, ln_b, sgu_w, bias_full):
    def body(us_ref, vs_ref, zs_ref, lng_ref, lnb_ref, w_ref, bias_ref, sg_ref):
        u, _, _, _, mixed = _sgu_forward_chunk(us_ref[...], vs_ref[...], lng_ref[...], lnb_ref[...],
                                               w_ref, bias_ref)
        for pair in range(N_SGU_HEADS // 2):
            cols = slice(pair * LANES, (pair + 1) * LANES)
            gate, _ = _silu_and_grad(zs_ref[:, cols])
            sg_ref[:, cols] = (u[:, cols] * mixed[pair] * gate).astype(MXU_DTYPE)

    col = lambda k: pl.BlockSpec((BLOCK, SGU_W), lambda n: (n, k))
    return pl.pallas_call(
        body,
        name="sgu_fwd",
        grid=(N_BLOCKS,),
        in_specs=[col(1), col(2), col(3), _full((1, SGU_W)), _full((1, SGU_W)),
                  _full((N_SGU_HEADS, BLOCK, BLOCK)), _full((BLOCK, SGU_W))],
        out_specs=pl.BlockSpec((BLOCK, SGU_W), lambda n: (n, 0)),
        out_shape=jax.ShapeDtypeStruct((SEQ, SGU_W), MXU_DTYPE),
        compiler_params=_params(("arbitrary",)),
    )(gates, gates, gates, ln_g, ln_b, sgu_w, bias_full)


def _out_proj_loss(ag, sg, x, target, wout, b_out, final_g):
    tm = TOKEN_TILE

    def body(ag_ref, sg_ref, x_ref, t_ref, w_ref, b_ref, gf_ref, gres_ref, dmix_ref, gw_ref, vec_ref):
        @pl.when(pl.program_id(0) == 0)
        def _():
            gw_ref[...] = jnp.zeros_like(gw_ref)
            vec_ref[...] = jnp.zeros_like(vec_ref)

        a = ag_ref[...]
        s = sg_ref[...]
        xo = x_ref[...] + (_dot(a, w_ref[0:ATTN_W, :]) + _dot(s, w_ref[ATTN_W:, :])) + b_ref[...]
        r = lax.rsqrt(jnp.mean(xo * xo, axis=-1, keepdims=True) + NORM_EPS)
        xn = xo * r
        gf = gf_ref[...]
        err = xn * gf - t_ref[...]
        loss = 0.5 * jnp.sum(jnp.mean(err * err, axis=-1, keepdims=True), axis=0, keepdims=True)
        dy = err * (1.0 / D_MODEL)
        dxn = dy * gf
        gres = r * (dxn - xn * jnp.mean(dxn * xn, axis=-1, keepdims=True))
        vec_ref[0:1, :] += jnp.broadcast_to(loss, (1, D_MODEL))
        vec_ref[1:2, :] += jnp.sum(dy * xn, axis=0, keepdims=True)
        vec_ref[2:3, :] += jnp.sum(gres, axis=0, keepdims=True)
        gres_ref[...] = gres
        gb = gres.astype(MXU_DTYPE)
        dmix_ref[:, 0:ATTN_W] = _dot(gb, w_ref[0:ATTN_W, :], NT)
        dmix_ref[:, ATTN_W:] = _dot(gb, w_ref[ATTN_W:, :], NT)
        gw_ref[0:ATTN_W, :] += _dot(a, gb, TN)
        gw_ref[ATTN_W:, :] += _dot(s, gb, TN)

    tile = lambda w: pl.BlockSpec((tm, w), lambda i: (i, 0))
    return pl.pallas_call(
        body,
        name="out_proj_loss",
        grid=(SEQ // tm,),
        in_specs=[tile(ATTN_W), tile(SGU_W), tile(D_MODEL), tile(D_MODEL),
                  _full((D_MODEL, D_MODEL)), _full((1, D_MODEL)), _full((1, D_MODEL))],
        out_specs=(tile(D_MODEL), tile(D_MODEL), _full((D_MODEL, D_MODEL)), _full((8, D_MODEL))),
        out_shape=(jax.ShapeDtypeStruct((SEQ, D_MODEL), F32),
                   jax.ShapeDtypeStruct((SEQ, D_MODEL), F32),
                   jax.ShapeDtypeStruct((D_MODEL, D_MODEL), F32),
                   jax.ShapeDtypeStruct((8, D_MODEL), F32)),
        compiler_params=_params(("arbitrary",), VMEM_LIMIT),
    )(ag, sg, x, target, wout, b_out, final_g)


def _sgu_bwd(dmix, gates, ln_g, ln_b, sgu_w, bias_full):
    last = N_BLOCKS - 1

    def body(d_ref, us_ref, vs_ref, zs_ref, lng_ref, lnb_ref, w_ref, bias_ref,
             dp_ref, gw_ref, gb_ref, gln_ref, gbin_ref, wt_ref, gbias_ref):
        c = pl.program_id(0)
        tril = (lax.broadcasted_iota(jnp.int32, (BLOCK, BLOCK), 0)
                >= lax.broadcasted_iota(jnp.int32, (BLOCK, BLOCK), 1))

        @pl.when(c == 0)
        def _():
            gw_ref[...] = jnp.zeros_like(gw_ref)
            gln_ref[...] = jnp.zeros_like(gln_ref)
            gbin_ref[...] = jnp.zeros_like(gbin_ref)
            gbias_ref[...] = jnp.zeros_like(gbias_ref)
            for hh in range(N_SGU_HEADS):
                wt_ref[hh] = jnp.where(tril, w_ref[hh], 0.0).T.astype(MXU_DTYPE)

        us = us_ref[...]
        vs = vs_ref[...]
        lng = lng_ref[...]
        u, vhat, rstd, vln, mixed = _sgu_forward_chunk(us, vs, lng, lnb_ref[...], w_ref, bias_ref)
        low = lax.broadcasted_iota(jnp.int32, (BLOCK, LANES), 1) < HALF
        du_parts, dzs_parts, dvln_parts = [], [], []
        for pair in range(N_SGU_HEADS // 2):
            cols = slice(pair * LANES, (pair + 1) * LANES)
            dsg = d_ref[:, cols]
            gate, gate_grad = _silu_and_grad(zs_ref[:, cols])
            up = u[:, cols]
            du_parts.append(dsg * mixed[pair] * gate)
            dzs_parts.append(dsg * up * mixed[pair] * gate_grad)
            dmixed = dsg * up * gate
            gbias_ref[:, cols] += dmixed
            dm_lo = jnp.where(low, dmixed, 0.0)
            dm_hi = jnp.where(low, 0.0, dmixed)
            vp = vln[:, cols]
            gw_ref[2 * pair] += _dot(dm_lo, vp, NT)
            gw_ref[2 * pair + 1] += _dot(dm_hi, vp, NT)
            dvln_parts.append(_dot(wt_ref[2 * pair], dm_lo) + _dot(wt_ref[2 * pair + 1], dm_hi))
        dvln = jnp.concatenate(dvln_parts, axis=1)
        gln_ref[0:1, :] += jnp.sum(dvln * vhat, axis=0, keepdims=True)
        gln_ref[1:2, :] += jnp.sum(dvln, axis=0, keepdims=True)
        dvhat = dvln * lng
        dvg = rstd * (dvhat - jnp.mean(dvhat, axis=-1, keepdims=True)
                      - vhat * jnp.mean(dvhat * vhat, axis=-1, keepdims=True))
        dus = jnp.concatenate(du_parts, axis=1) * _gelu_grad(us)
        dvs = dvg * _gelu_grad(vs)
        dzs = jnp.concatenate(dzs_parts, axis=1)
        for k, val in enumerate((dus, dvs, dzs)):
            dp_ref[:, k * SGU_W:(k + 1) * SGU_W] = val.astype(MXU_DTYPE)
            gbin_ref[:, k * SGU_W:(k + 1) * SGU_W] += jnp.sum(val, axis=0, keepdims=True)

        @pl.when(c == last)
        def _():
            for hh in range(N_SGU_HEADS):
                gw_ref[hh] = jnp.where(tril, gw_ref[hh], 0.0)
            head_of_lane = lax.broadcasted_iota(jnp.int32, (N_SGU_HEADS, SGU_W), 1) // HEAD_DIM
            select = (head_of_lane == lax.broadcasted_iota(jnp.int32, (N_SGU_HEADS, SGU_W), 0)).astype(F32)
            gb_ref[...] = lax.dot_general(select, gbias_ref[...], NT, precision=lax.Precision.HIGHEST,
                                          preferred_element_type=F32)

    col = lambda k: pl.BlockSpec((BLOCK, SGU_W), lambda n: (n, k))
    return pl.pallas_call(
        body,
        name="sgu_bwd",
        grid=(N_BLOCKS,),
        in_specs=[col(1), col(1), col(2), col(3), _full((1, SGU_W)), _full((1, SGU_W)),
                  _full((N_SGU_HEADS, BLOCK, BLOCK)), _full((BLOCK, SGU_W))],
        out_specs=(pl.BlockSpec((BLOCK, SGU_SECTION), lambda n: (n, 0)),
                   _full((N_SGU_HEADS, BLOCK, BLOCK)), _full((N_SGU_HEADS, BLOCK)),
                   _full((8, SGU_W)), _full((1, SGU_SECTION))),
        out_shape=(jax.ShapeDtypeStruct((SEQ, SGU_SECTION), MXU_DTYPE),
                   jax.ShapeDtypeStruct((N_SGU_HEADS, BLOCK, BLOCK), F32),
                   jax.ShapeDtypeStruct((N_SGU_HEADS, BLOCK), F32),
                   jax.ShapeDtypeStruct((8, SGU_W), F32),
                   jax.ShapeDtypeStruct((1, SGU_SECTION), F32)),
        scratch_shapes=[pltpu.VMEM((N_SGU_HEADS, BLOCK, BLOCK), MXU_DTYPE),
                        pltpu.VMEM((BLOCK, SGU_W), F32)],
        compiler_params=_params(("arbitrary",)),
    )(dmix, gates, gates, gates, ln_g, ln_b, sgu_w, bias_full)


def _attn_bwd(sinks, dmix, q, kvx, out, gates):
    last = N_BLOCKS - 1

    def body(sink_ref, d_ref, q_ref, kc_ref, kp_ref, o_ref, za_ref, dp_ref, gsink_ref, gbin_ref,
             pend_ref, carry_ref):
        n = pl.program_id(0)

        @pl.when(n == 0)
        def _():
            gsink_ref[...] = jnp.zeros_like(gsink_ref)
            gbin_ref[...] = jnp.zeros_like(gbin_ref)
            carry_ref[...] = jnp.zeros_like(carry_ref)

        @pl.when(n > 0)
        def _():
            dp_ref[:, 0:ATTN_W] = pend_ref[:, 0:ATTN_W]
            dp_ref[:, GATE0:ATTN_SECTION] = pend_ref[:, ATTN_W:]

        @pl.when(n > last)
        def _():
            dp_ref[:, KV0:GATE0] = carry_ref[...].astype(MXU_DTYPE)

        @pl.when(n <= last)
        def _():
            valid = _window_mask(n)
            low = lax.broadcasted_iota(jnp.int32, (2 * BLOCK, LANES), 1) < HALF
            lane_row = lax.broadcasted_iota(jnp.int32, (1, LANES), 1)
            gsink = jnp.zeros((1, LANES), F32)
            chains = [(g, par) for g in range(2) for par in range(2)]
            kv = [_kv_cat(kp_ref, kc_ref, 2 * g + par) for g, par in chains]
            qs, douts, dzas, prods = [], [], [], []
            for g in range(2):
                o = o_ref[2 * g:2 * g + 2].reshape(2 * BLOCK, LANES)
                cols = slice(2 * g * LANES, (2 * g + 2) * LANES)
                stack = lambda ref: jnp.concatenate([ref[:, cols][:, 0:LANES], ref[:, cols][:, LANES:]], axis=0)
                dg = stack(d_ref)
                gate, gate_grad = _silu_and_grad(stack(za_ref))
                qs.append(q_ref[2 * g:2 * g + 2].reshape(2 * BLOCK, LANES))
                douts.append((dg * gate).astype(MXU_DTYPE))
                dzas.append(dg * o * gate_grad)
                prods.append(dg * gate * o)

            first = {}

            def issue_first(i):
                g, _ = chains[i]
                first[i] = (_dot(qs[g], kv[i][0], NT), _dot(douts[g], kv[i][1], NT))

            issue_first(0)
            issue_first(1)
            dqs, dk_parts, dv_parts = [], [], []
            for i, (g, par) in enumerate(chains):
                mine = low if par == 0 else jnp.logical_not(low)
                sink = _sink_column(sink_ref, g, par)
                delta = jnp.sum(jnp.where(mine, prods[g], 0.0), axis=1, keepdims=True)
                e, m, den = _softmax_stats(jnp.where(valid, first[i][0], NEG_INF), sink)
                inv = 1.0 / den
                p = (e * inv)
                ds = (p * (first[i][1] - delta)).astype(MXU_DTYPE)
                p = p.astype(MXU_DTYPE)
                gs = jnp.exp(sink - m) * inv * delta
                for k, h in enumerate((4 * g + par, 4 * g + 2 + par)):
                    total = jnp.sum(gs[k * BLOCK:(k + 1) * BLOCK], axis=0, keepdims=True)
                    gsink = jnp.where(lane_row == h, -total, gsink)
                if i + 2 < len(chains):
                    issue_first(i + 2)
                dqs.append(_dot(ds, kv[i][0]))
                dk_parts.append(jnp.where(mine, _dot(ds, qs[g], TN), 0.0))
                dv_parts.append(jnp.where(mine, _dot(p, douts[g], TN), 0.0))
            for g in range(2):
                dq = (dqs[2 * g] + dqs[2 * g + 1]) * SCALE
                dza = dzas[g]
                for i in range(2):
                    pair = 2 * g + i
                    rows = slice(i * BLOCK, (i + 1) * BLOCK)
                    lanes = slice(pair * LANES, (pair + 1) * LANES)
                    pend_ref[:, lanes] = dq[rows].astype(MXU_DTYPE)
                    gbin_ref[:, lanes] += jnp.sum(dq[rows], axis=0, keepdims=True)
                    zl = slice(ATTN_W + pair * LANES, ATTN_W + (pair + 1) * LANES)
                    pend_ref[:, zl] = dza[rows].astype(MXU_DTYPE)
                    gl = slice(GATE0 + pair * LANES, GATE0 + (pair + 1) * LANES)
                    gbin_ref[:, gl] += jnp.sum(dza[rows], axis=0, keepdims=True)
            gsink_ref[...] += gsink
            for k, parts in enumerate((dk_parts, dv_parts)):
                both = parts[0] + parts[3] + pltpu.roll(parts[1] + parts[2], HALF, 1)
                lanes = slice(k * KV_W, (k + 1) * KV_W)
                done = carry_ref[:, lanes] + both[0:BLOCK]
                dp_ref[:, KV0 + k * KV_W:KV0 + (k + 1) * KV_W] = done.astype(MXU_DTYPE)
                carry_ref[:, lanes] = both[BLOCK:]
                gbin_ref[:, KV0 + k * KV_W:KV0 + (k + 1) * KV_W] += jnp.sum(both, axis=0, keepdims=True)

    at = lambda n: jnp.minimum(n, last)
    blk = lambda w: pl.BlockSpec((BLOCK, w), lambda n: (at(n), 0))
    tiles = pl.BlockSpec((N_PAIRS, BLOCK, LANES), lambda n: (0, at(n), 0))
    return pl.pallas_call(
        body,
        name="attn_bwd",
        grid=(N_BLOCKS + 1,),
        in_specs=[pl.BlockSpec(memory_space=pltpu.SMEM),
                  blk(ATTN_W),
                  tiles,
                  blk(8 * LANES),
                  pl.BlockSpec((BLOCK, 8 * LANES), lambda n: (jnp.maximum(at(n) - 1, 0), 0)),
                  tiles,
                  blk(ATTN_W)],
        out_specs=(pl.BlockSpec((BLOCK, ATTN_SECTION), lambda n: (jnp.maximum(n - 1, 0), 0)),
                   _full((1, LANES)), _full((1, ATTN_SECTION))),
        out_shape=(jax.ShapeDtypeStruct((SEQ, ATTN_SECTION), MXU_DTYPE),
                   jax.ShapeDtypeStruct((1, LANES), F32),
                   jax.ShapeDtypeStruct((1, ATTN_SECTION), F32)),
        scratch_shapes=[pltpu.VMEM((BLOCK, 2 * ATTN_W), MXU_DTYPE), pltpu.VMEM((BLOCK, 2 * KV_W), F32)],
        compiler_params=_params(("arbitrary",)),
    )(sinks, dmix, q, kvx, kvx, out, gates)


def _in_proj_bwd(dpa, dps, win_t, x, norm_g, gres, gwin):
    tm = TOKEN_TILE
    steps = SEQ // tm

    def body(da_ref, ds_ref, w_ref, x_ref, g_ref, gres_ref, gwin_ref, gx_ref, gng_ref, shard_ref,
             sa, ra, sb, rc, send_sems, recv_sems):
        step = pl.program_id(0)
        start, exchange, finish = _reduce_scatter_plan(_Copies(send_sems, recv_sems), 0, gwin_ref, WIN_ROWS,
                                                       sa, ra, sb, rc, shard_ref)

        @pl.when(step == 0)
        def _():
            gng_ref[...] = jnp.zeros_like(gng_ref)
            start()

        pl.when(step == 3)(exchange)

        dh = _dot(da_ref[...], w_ref[0:ATTN_SECTION, :]) + _dot(ds_ref[...], w_ref[ATTN_SECTION:, :])
        xv = x_ref[...]
        r = lax.rsqrt(jnp.mean(xv * xv, axis=-1, keepdims=True) + NORM_EPS)
        xn = xv * r
        gng_ref[...] += jnp.sum(dh * xn, axis=0, keepdims=True)
        dxn = dh * g_ref[...]
        gx_ref[...] = r * (dxn - xn * jnp.mean(dxn * xn, axis=-1, keepdims=True)) + gres_ref[...]

        pl.when(step == steps - 1)(finish)

    tile = lambda w: pl.BlockSpec((tm, w), lambda i: (i, 0))
    return pl.pallas_call(
        body,
        name="in_proj_bwd",
        grid=(steps,),
        in_specs=[tile(ATTN_SECTION), tile(SGU_SECTION), _full((IN_W, D_MODEL)), tile(D_MODEL),
                  _full((1, D_MODEL)), tile(D_MODEL), VMEM_SPEC],
        out_specs=(tile(D_MODEL), _full((1, D_MODEL)), VMEM_SPEC),
        out_shape=(jax.ShapeDtypeStruct((SEQ, D_MODEL), F32), jax.ShapeDtypeStruct((1, D_MODEL), F32),
                   jax.ShapeDtypeStruct((WIN_ROWS, D_MODEL), F32)),
        scratch_shapes=_reduce_scatter_scratch(WIN_ROWS, D_MODEL, COMM_DTYPE) + _dma_sems(REDUCE_SEMS),
        compiler_params=_params(("arbitrary",), VMEM_LIMIT),
    )(dpa, dps, win_t, x, norm_g, gres, gwin)


def _win_grad(dpa, dps, h, gwout, gsguw):
    rows = 256
    n_attn = ATTN_SECTION // rows
    steps = n_attn + SGU_SECTION // rows

    def body(da_ref, ds_ref, h_ref, gwout_ref, gsguw_ref, o_ref, wout_shard_ref, sguw_full_ref,
             sa_w, ra_w, sb_w, rc_w, sa_s, ra_s, sb_s, rc_s, landing, send_sems, recv_sems):
        step = pl.program_id(0)
        copies = _Copies(send_sems, recv_sems)
        own_sguw = landing.at[_block_rows(_place(), SGUW_ROWS), :]
        plans = [_reduce_scatter_plan(copies, 0, gwout_ref, WOUT_ROWS, sa_w, ra_w, sb_w, rc_w, wout_shard_ref),
                 _reduce_scatter_plan(copies, REDUCE_SEMS, gsguw_ref, SGUW_ROWS, sa_s, ra_s, sb_s, rc_s, own_sguw)]
        gather = _gather_plan(copies, 2 * REDUCE_SEMS, landing, SGUW_ROWS)

        @pl.when(step == 0)
        def _():
            for start, _, _ in plans:
                start()

        @pl.when(step == 2)
        def _():
            for _, exchange, _ in plans:
                exchange()

        @pl.when(step == 6)
        def _():
            for _, _, finish in plans:
                finish()
            gather[0]()

        pl.when(step == 8)(gather[1])

        @pl.when(step < n_attn)
        def _():
            o_ref[...] = _dot(da_ref[...], h_ref[...], TN)

        @pl.when(step >= n_attn)
        def _():
            o_ref[...] = _dot(ds_ref[...], h_ref[...], TN)

        @pl.when(step == steps - 1)
        def _():
            gather[2]()
            sguw_full_ref[...] = landing[...]

    return pl.pallas_call(
        body,
        name="win_grad",
        grid=(steps,),
        in_specs=[pl.BlockSpec((SEQ, rows), lambda i: (0, jnp.minimum(i, n_attn - 1))),
                  pl.BlockSpec((SEQ, rows), lambda i: (0, jnp.maximum(i - n_attn, 0))),
                  _full((SEQ, D_MODEL)), VMEM_SPEC, VMEM_SPEC],
        out_specs=(pl.BlockSpec((rows, D_MODEL), lambda i: (i, 0)), VMEM_SPEC,
                   _full((N_SGU_HEADS * BLOCK, BLOCK))),
        out_shape=(jax.ShapeDtypeStruct((IN_W, D_MODEL), F32),
                   jax.ShapeDtypeStruct((WOUT_ROWS, D_MODEL), F32),
                   jax.ShapeDtypeStruct((N_SGU_HEADS * BLOCK, BLOCK), F32)),
        scratch_shapes=(_reduce_scatter_scratch(WOUT_ROWS, D_MODEL, COMM_DTYPE)
                        + _reduce_scatter_scratch(SGUW_ROWS, BLOCK, F32)
                        + [pltpu.VMEM((N_SGU_HEADS * BLOCK, BLOCK), F32)]
                        + _dma_sems(2 * REDUCE_SEMS + GATHER_SEMS)),
        compiler_params=_params(("arbitrary",), VMEM_LIMIT),
    )(dpa, dps, h, gwout, gsguw)


VEC_NORM_G, VEC_B_IN, VEC_SINKS, VEC_LN_G, VEC_LN_B, VEC_B_OUT, VEC_FINAL_G, VEC_LOSS, VEC_SGU_B = 0, 1, 2, 3, 4, 5, 6, 7, 8


def _adamw(w, g, m, v):
    m = ADAM_B1 * m + (1.0 - ADAM_B1) * g
    v = ADAM_B2 * v + (1.0 - ADAM_B2) * (g * g)
    m_hat = m / (1.0 - ADAM_B1 ** ADAM_STEP)
    v_hat = v / (1.0 - ADAM_B2 ** ADAM_STEP)
    delta = -ADAM_LR * (m_hat / (jnp.sqrt(v_hat) + ADAM_EPS) + ADAM_WD * w)
    return delta, m, v


def _adamw_shard(name, g, w, m, v, block_rows):
    def body(g_ref, w_ref, m_ref, v_ref, d_ref, nm_ref, nv_ref):
        d_ref[...], nm_ref[...], nv_ref[...] = _adamw(w_ref[...], g_ref[...], m_ref[...], v_ref[...])

    rows, cols = w.shape
    spec = pl.BlockSpec((block_rows, cols), lambda i: (i, 0))
    return pl.pallas_call(
        body,
        name=name,
        grid=(rows // block_rows,),
        in_specs=[spec] * 4,
        out_specs=(spec,) * 3,
        out_shape=(jax.ShapeDtypeStruct(w.shape, F32),) * 3,
        compiler_params=_params(("arbitrary",)),
    )(g, w, m, v)


def _adamw_replicated(gsguw, gng, gbin_a, gbin_s, gsink, gln, gsgub, vec4, weights, m_state, v_state):
    n = len(SMALL)

    def body(*refs):
        gsguw_ref, gng_ref, gba_ref, gbs_ref, gsink_ref, gln_ref, gsgub_ref, vec4_ref = refs[:8]
        w_refs, m_refs, v_refs = (refs[8 + k * n:8 + (k + 1) * n] for k in range(3))
        outs = refs[8 + 3 * n:8 + 3 * n + 1 + 4 * n]
        loss_ref = outs[0]
        g_refs, d_refs, nm_refs, nv_refs = (outs[1 + k * n:1 + (k + 1) * n] for k in range(4))
        vec_ref, ra_vec, slots, send_sems, recv_sems = refs[8 + 3 * n + 1 + 4 * n:]
        x, y, c = _place()
        copies = _Copies(send_sems, recv_sems)

        vec_ref[...] = jnp.zeros_like(vec_ref)
        vec_ref[VEC_NORM_G:VEC_NORM_G + 1, 0:D_MODEL] = gng_ref[...]
        vec_ref[VEC_B_IN:VEC_B_IN + 1, 0:ATTN_SECTION] = gba_ref[...]
        vec_ref[VEC_B_IN:VEC_B_IN + 1, ATTN_SECTION:IN_W] = gbs_ref[...]
        vec_ref[VEC_SINKS:VEC_SINKS + 1, 0:LANES] = gsink_ref[...]
        vec_ref[VEC_LN_G:VEC_LN_G + 1, 0:SGU_W] = gln_ref[0:1, :]
        vec_ref[VEC_LN_B:VEC_LN_B + 1, 0:SGU_W] = gln_ref[1:2, :]
        vec_ref[VEC_B_OUT:VEC_B_OUT + 1, 0:D_MODEL] = vec4_ref[2:3, :]
        vec_ref[VEC_FINAL_G:VEC_FINAL_G + 1, 0:D_MODEL] = vec4_ref[1:2, :]
        vec_ref[VEC_LOSS:VEC_LOSS + 1, 0:D_MODEL] = vec4_ref[0:1, :]
        vec_ref[VEC_SGU_B:VEC_SGU_B + N_SGU_HEADS, 0:BLOCK] = gsgub_ref[...]

        to_sibling = copies(0, vec_ref, ra_vec, (x, y, 1 - c))
        to_sibling.start()
        to_sibling.wait_recv()

        def chip_slot(place):
            return slots.at[pl.ds(pl.multiple_of((2 * place[0] + place[1]) * VEC_ROWS, 8), VEC_ROWS), :]

        mine = chip_slot((x, y))
        mine[...] = vec_ref[...] + ra_vec[...]
        to_chips = [copies(i, mine, mine, (*_chip(rel), c)) for i, rel in enumerate(RELATIONS[1:], start=1)]
        for cp in to_chips:
            cp.start()
        for i, rel in enumerate(RELATIONS[1:], start=1):
            theirs = chip_slot(_chip(rel))
            copies(i, theirs, theirs, (x, y, c)).wait_recv()
        vec = ((slots[0:VEC_ROWS, :] + slots[VEC_ROWS:2 * VEC_ROWS, :])
               + slots[2 * VEC_ROWS:3 * VEC_ROWS, :]) + slots[3 * VEC_ROWS:, :]
        to_sibling.wait_send()
        for cp in to_chips:
            cp.wait_send()

        vec_ref[...] = vec
        loss_ref[...] = vec_ref[VEC_LOSS:VEC_LOSS + 1, 0:LANES]
        for i, (_, row, shape) in enumerate(SMALL):
            g = gsguw_ref[...] if row is None else vec_ref[row:row + shape[0], 0:shape[1]]
            g_refs[i][...] = g
            d_refs[i][...], nm_refs[i][...], nv_refs[i][...] = _adamw(
                w_refs[i][...], g, m_refs[i][...], v_refs[i][...])

    shapes = tuple(jax.ShapeDtypeStruct(shape, F32) for _, _, shape in SMALL)
    outs = pl.pallas_call(
        body,
        name="adamw_replicated",
        in_specs=[VMEM_SPEC] * (8 + 3 * n),
        out_specs=(VMEM_SPEC,) * (1 + 4 * n),
        out_shape=(jax.ShapeDtypeStruct((1, LANES), F32),) + shapes * 4,
        scratch_shapes=[pltpu.VMEM((VEC_ROWS, IN_W), F32), pltpu.VMEM((VEC_ROWS, IN_W), F32),
                        pltpu.VMEM((4 * VEC_ROWS, IN_W), F32)] + _dma_sems(4),
    )(gsguw, gng, gbin_a, gbin_s, gsink, gln, gsgub, vec4, *weights, *m_state, *v_state)
    return (outs[0][0, 0],) + tuple(outs[1 + k * n:1 + (k + 1) * n] for k in range(4))


SMALL = (
    ("norm_g", VEC_NORM_G, (1, D_MODEL)),
    ("b_in", VEC_B_IN, (1, IN_W)),
    ("attn_sinks", VEC_SINKS, (1, N_Q_HEADS)),
    ("sgu_ln_g", VEC_LN_G, (1, SGU_W)),
    ("sgu_ln_b", VEC_LN_B, (1, SGU_W)),
    ("sgu_w", None, (N_SGU_HEADS * BLOCK, BLOCK)),
    ("sgu_b", VEC_SGU_B, (N_SGU_HEADS, BLOCK)),
    ("b_out", VEC_B_OUT, (1, D_MODEL)),
    ("final_norm_g", VEC_FINAL_G, (1, D_MODEL)),
)


def _local_grads(x, target, win_t, wout_shard, norm_g, b_in, attn_sinks, sgu_ln_g, sgu_ln_b, sgu_w, sgu_b, b_out,
                 final_g):
    sinks = attn_sinks.reshape(N_Q_HEADS)
    bias_full = jnp.repeat(sgu_b.T, HEAD_DIM, axis=1)
    h, q, kvx, gates, wout = _in_proj(x, norm_g, b_in, win_t, wout_shard)
    out, ag = _attn_fwd(sinks, q, kvx, gates)
    sg = _sgu_fwd(gates, sgu_ln_g, sgu_ln_b, sgu_w, bias_full)
    gres, dmix, gwout, vec4 = _out_proj_loss(ag, sg, x, target, wout, b_out, final_g)
    dps, gsguw, gsgub, gln, gbin_s = _sgu_bwd(dmix, gates, sgu_ln_g, sgu_ln_b, sgu_w, bias_full)
    dpa, gsink, gbin_a = _attn_bwd(sinks, dmix, q, kvx, out, gates)
    gwin, gwout_shard, gsguw_sum = _win_grad(dpa, dps, h, gwout, gsguw.reshape(N_SGU_HEADS * BLOCK, BLOCK))
    grad_x, gng, gwin_shard = _in_proj_bwd(dpa, dps, win_t, x, norm_g, gres, gwin)
    return grad_x, gwin_shard, gwout_shard, gsguw_sum, (gng, gbin_a, gbin_s, gsink, gln, gsgub, vec4)


def kernel(x, norm_g, w_in, b_in, attn_sinks, sgu_ln_g, sgu_ln_b, sgu_w, sgu_b, w_out, b_out, final_norm_g, loss_target, m_norm_g, m_w_in, m_b_in, m_attn_sinks, m_sgu_ln_g, m_sgu_ln_b, m_sgu_w, m_sgu_b, m_w_out, m_b_out, m_final_norm_g, v_norm_g, v_w_in, v_b_in, v_attn_sinks, v_sgu_ln_g, v_sgu_ln_b, v_sgu_w, v_sgu_b, v_w_out, v_b_out, v_final_norm_g):
    given = dict(norm_g=norm_g, b_in=b_in, attn_sinks=attn_sinks, sgu_ln_g=sgu_ln_g, sgu_ln_b=sgu_ln_b,
                 sgu_w=sgu_w, sgu_b=sgu_b, b_out=b_out, final_norm_g=final_norm_g)
    m_given = dict(norm_g=m_norm_g, b_in=m_b_in, attn_sinks=m_attn_sinks, sgu_ln_g=m_sgu_ln_g,
                   sgu_ln_b=m_sgu_ln_b, sgu_w=m_sgu_w, sgu_b=m_sgu_b, b_out=m_b_out, final_norm_g=m_final_norm_g)
    v_given = dict(norm_g=v_norm_g, b_in=v_b_in, attn_sinks=v_attn_sinks, sgu_ln_g=v_sgu_ln_g,
                   sgu_ln_b=v_sgu_ln_b, sgu_w=v_sgu_w, sgu_b=v_sgu_b, b_out=v_b_out, final_norm_g=v_final_norm_g)

    win_t = _all_gather_win(w_in[0].T)
    grad_x, gwin_t, gwout, gsguw, vec_parts = _local_grads(
        x[0], loss_target[0], win_t, w_out[0], norm_g, b_in, attn_sinks, sgu_ln_g, sgu_ln_b, sgu_w[0], sgu_b[0],
        b_out, final_norm_g.reshape(1, D_MODEL))

    gwin = gwin_t.T
    d_win, nm_win, nv_win = _adamw_shard("adamw_w_in", gwin, w_in[0], m_w_in[0], v_w_in[0], 256)
    d_wout, nm_wout, nv_wout = _adamw_shard("adamw_w_out", gwout, w_out[0], m_w_out[0], v_w_out[0], WOUT_ROWS)
    as_2d = lambda d: [d[name].reshape(shape) for name, _, shape in SMALL]
    loss, *small = _adamw_replicated(gsguw, *vec_parts, as_2d(given), as_2d(m_given), as_2d(v_given))

    def assemble(big_in, big_out, k):
        vals = {name: small[k][i].reshape(given[name].shape) for i, (name, _, _) in enumerate(SMALL)}
        vals["w_in"] = big_in[None]
        vals["w_out"] = big_out[None]
        order = ("norm_g", "w_in", "b_in", "attn_sinks", "sgu_ln_g", "sgu_ln_b", "sgu_w", "sgu_b", "w_out",
                 "b_out", "final_norm_g")
        return [vals[name] for name in order]

    return (loss, grad_x[None],
            *assemble(gwin, gwout, 0), *assemble(d_win, d_wout, 1),
            *assemble(nm_win, nm_wout, 2), *assemble(nv_win, nv_wout, 3))
```

```python
import functools
import math

import jax
import jax.numpy as jnp
from jax import lax
from jax.experimental import pallas as pl
from jax.experimental.pallas import tpu as pltpu

F32 = jnp.float32
BF16 = jnp.bfloat16
MXU_DTYPE = BF16
COMM_DTYPE = BF16

D_MODEL = 1024
SEQ = 4096
HEAD_DIM = 64
N_Q_HEADS = 8
Q_PER_KV = 4
BLOCK = 128
N_BLOCKS = SEQ // BLOCK
ATTN_W = 512
KV_W = 128
SGU_W = 512
N_SGU_HEADS = 8
IN_W = 2816
NORM_EPS = 1e-5
NEG_INF = -1e30
SCALE = HEAD_DIM ** -0.5
KV0 = ATTN_W
GATE0 = ATTN_W + 2 * KV_W
SGU0 = GATE0 + ATTN_W
ATTN_SECTION = SGU0
SGU_SECTION = IN_W - SGU0

ADAM_LR = 0.001
ADAM_B1 = 0.9
ADAM_B2 = 0.999
ADAM_EPS = 1e-08
ADAM_WD = 0.01
ADAM_STEP = 10

N_DEV = 8
WIN_ROWS = IN_W // N_DEV
WOUT_ROWS = D_MODEL // N_DEV
SGUW_ROWS = N_SGU_HEADS * BLOCK // N_DEV
VEC_ROWS = 16
MESH = pl.DeviceIdType.MESH

LANES = 128
HALF = LANES // 2
N_PAIRS = N_Q_HEADS * HEAD_DIM // LANES
TOKEN_TILE = 256
VMEM_LIMIT = 56 * 1024 * 1024

NN = (((1,), (0,)), ((), ()))
NT = (((1,), (1,)), ((), ()))
TN = (((0,), (0,)), ((), ()))


def _dot(a, b, dims=NN):
    return lax.dot_general(a.astype(MXU_DTYPE), b.astype(MXU_DTYPE), dims, preferred_element_type=F32)


def _gelu(x):
    return x * (lax.erf(x * (1.0 / math.sqrt(2.0))) + 1.0) * 0.5


def _gelu_grad(x):
    cdf = (lax.erf(x * (1.0 / math.sqrt(2.0))) + 1.0) * 0.5
    return cdf + x * jnp.exp(-0.5 * x * x) * (1.0 / math.sqrt(2.0 * math.pi))


def _silu_and_grad(z):
    s = jax.nn.sigmoid(z)
    return z * s, s * (1.0 + z * (1.0 - s))


def _params(semantics=None, vmem=None):
    kw = {}
    if semantics is not None:
        kw["dimension_semantics"] = semantics
    if vmem is not None:
        kw["vmem_limit_bytes"] = vmem
    return pltpu.CompilerParams(**kw)


def _full(shape):
    return pl.BlockSpec(shape, lambda *_: (0,) * len(shape))


VMEM_SPEC = pl.BlockSpec(memory_space=pltpu.VMEM)


RELATIONS = ((0, 0), (1, 0), (0, 1), (1, 1))


def _place():
    return lax.axis_index("x"), lax.axis_index("y"), lax.axis_index("c")


def _chip(rel):
    x, y, _ = _place()
    return (1 - x if rel[0] else x, 1 - y if rel[1] else y)


def _block_rows(place, n_rows):
    px, py, pc = place
    return pl.ds(pl.multiple_of((4 * px + 2 * py + pc) * n_rows, 16), n_rows)


class _Copies:
    def __init__(self, send_sems, recv_sems):
        self.send_sems, self.recv_sems = send_sems, recv_sems

    def __call__(self, k, src, dst, to):
        return pltpu.make_async_remote_copy(src_ref=src, dst_ref=dst, send_sem=self.send_sems.at[k],
                                            recv_sem=self.recv_sems.at[k], device_id=to, device_id_type=MESH)


def _gather_plan(copies, sem0, full_ref, n_rows):
    x, y, c = _place()
    me, sibling = (x, y, c), (x, y, 1 - c)
    chips = [_chip(rel) for rel in RELATIONS[1:]]

    def cp(k, block, to):
        rows = full_ref.at[_block_rows(block, n_rows), :]
        return copies(sem0 + k, rows, rows, to)

    first = [cp(0, me, sibling)] + [cp(1 + j, me, (*chip, c)) for j, chip in enumerate(chips)]
    passed = [cp(4 + j, (*chip, c), sibling) for j, chip in enumerate(chips)]

    def start():
        for f in first:
            f.start()

    def forward():
        for j, chip in enumerate(chips):
            cp(1 + j, (*chip, c), me).wait_recv()
            passed[j].start()

    def finish():
        cp(0, sibling, me).wait_recv()
        for j, chip in enumerate(chips):
            cp(4 + j, (*chip, 1 - c), me).wait_recv()
        for f in first + passed:
            f.wait_send()

    return start, forward, finish


GATHER_SEMS = 7


def _reduce_scatter_plan(copies, sem0, part_ref, n_rows, sa, ra, sb, rc, res_ref):
    x, y, c = _place()
    sibling = (x, y, 1 - c)
    n = n_rows
    level1 = copies(sem0, sa, ra, sibling)

    def level2(i):
        slot = pl.ds((i - 1) * n, n)
        return copies(sem0 + i, sb.at[slot, :], rc.at[slot, :], (*_chip(RELATIONS[i]), c))

    def start():
        for i, rel in enumerate(RELATIONS):
            sa[i * n:(i + 1) * n, :] = part_ref[_block_rows((*_chip(rel), 1 - c), n), :].astype(sa.dtype)
        level1.start()

    def exchange():
        level1.wait_recv()
        for i, rel in enumerate(RELATIONS):
            total = part_ref[_block_rows((*_chip(rel), c), n), :] + ra[i * n:(i + 1) * n, :].astype(F32)
            if i == 0:
                res_ref[...] = total
            else:
                sb[(i - 1) * n:i * n, :] = total.astype(sb.dtype)
                level2(i).start()

    def finish():
        acc = res_ref[...]
        for i in range(1, len(RELATIONS)):
            level2(i).wait_recv()
            acc = acc + rc[(i - 1) * n:i * n, :].astype(F32)
        res_ref[...] = acc
        level1.wait_send()
        for i in range(1, len(RELATIONS)):
            level2(i).wait_send()

    return start, exchange, finish


REDUCE_SEMS = 4


def _reduce_scatter_scratch(n_rows, width, dtype):
    return [pltpu.VMEM((4 * n_rows, width), dtype), pltpu.VMEM((4 * n_rows, width), dtype),
            pltpu.VMEM((3 * n_rows, width), dtype), pltpu.VMEM((3 * n_rows, width), dtype)]


def _dma_sems(n):
    return [pltpu.SemaphoreType.DMA((n,)), pltpu.SemaphoreType.DMA((n,))]


def _all_gather_win(win_t_shard):
    def body(win_ref, full_ref, send_sems, recv_sems):
        full_ref[_block_rows(_place(), WIN_ROWS), :] = win_ref[...].astype(COMM_DTYPE)
        start, forward, finish = _gather_plan(_Copies(send_sems, recv_sems), 0, full_ref, WIN_ROWS)
        start()
        forward()
        finish()

    return pl.pallas_call(
        body,
        name="all_gather_win",
        out_shape=jax.ShapeDtypeStruct((IN_W, D_MODEL), COMM_DTYPE),
        in_specs=[VMEM_SPEC],
        out_specs=VMEM_SPEC,
        scratch_shapes=_dma_sems(GATHER_SEMS),
        compiler_params=_params(vmem=VMEM_LIMIT),
    )(win_t_shard)


def _in_proj(x, norm_g, b_in, win_t, wout_shard):
    tm = TOKEN_TILE
    steps = SEQ // tm

    def body(x_ref, g_ref, b_ref, w_ref, wout_ref, h_ref, q_ref, kvx_ref, gate_ref, wfull_ref,
             landing, send_sems, recv_sems):
        step = pl.program_id(0)
        start, forward, finish = _gather_plan(_Copies(send_sems, recv_sems), 0, landing, WOUT_ROWS)

        @pl.when(step == 0)
        def _():
            landing[_block_rows(_place(), WOUT_ROWS), :] = wout_ref[...].astype(COMM_DTYPE)
            start()

        pl.when(step == steps // 2)(forward)

        xv = x_ref[...]
        r = lax.rsqrt(jnp.mean(xv * xv, axis=-1, keepdims=True) + NORM_EPS)
        h = ((xv * r) * g_ref[...]).astype(MXU_DTYPE)
        h_ref[...] = h

        def proj(lo, hi):
            return _dot(h, w_ref[lo:hi, :], NT) + b_ref[:, lo:hi]

        qs = proj(0, ATTN_W) * SCALE
        for pair in range(N_PAIRS):
            q_ref[pair] = qs[:, pair * LANES:(pair + 1) * LANES].astype(MXU_DTYPE)
        kv = proj(KV0, GATE0)
        low = lax.broadcasted_iota(jnp.int32, (tm, LANES), 1) < HALF
        for i in range(2):
            t = kv[:, i * LANES:(i + 1) * LANES]
            rot = pltpu.roll(t, HALF, 1)
            variants = (jnp.where(low, t, 0.0), jnp.where(low, 0.0, rot),
                        jnp.where(low, rot, 0.0), jnp.where(low, 0.0, t))
            for j, val in enumerate(variants):
                col = (4 * i + j) * LANES
                kvx_ref[:, col:col + LANES] = val.astype(MXU_DTYPE)
        gate_ref[...] = proj(GATE0, IN_W)

        @pl.when(step == steps - 1)
        def _():
            finish()
            wfull_ref[...] = landing[...]

    return pl.pallas_call(
        body,
        name="in_proj",
        grid=(steps,),
        in_specs=[pl.BlockSpec((tm, D_MODEL), lambda i: (i, 0)),
                  _full((1, D_MODEL)), _full((1, IN_W)), _full((IN_W, D_MODEL)), VMEM_SPEC],
        out_specs=(pl.BlockSpec((tm, D_MODEL), lambda i: (i, 0)),
                   pl.BlockSpec((N_PAIRS, tm, LANES), lambda i: (0, i, 0)),
                   pl.BlockSpec((tm, 8 * LANES), lambda i: (i, 0)),
                   pl.BlockSpec((tm, IN_W - GATE0), lambda i: (i, 0)),
                   _full((D_MODEL, D_MODEL))),
        out_shape=(jax.ShapeDtypeStruct((SEQ, D_MODEL), MXU_DTYPE),
                   jax.ShapeDtypeStruct((N_PAIRS, SEQ, LANES), MXU_DTYPE),
                   jax.ShapeDtypeStruct((SEQ, 8 * LANES), MXU_DTYPE),
                   jax.ShapeDtypeStruct((SEQ, IN_W - GATE0), F32),
                   jax.ShapeDtypeStruct((D_MODEL, D_MODEL), COMM_DTYPE)),
        scratch_shapes=[pltpu.VMEM((D_MODEL, D_MODEL), COMM_DTYPE)] + _dma_sems(GATHER_SEMS),
        compiler_params=_params(("arbitrary",), VMEM_LIMIT),
    )(x, norm_g, b_in, win_t, wout_shard)


def _window_mask(n):
    qi = lax.broadcasted_iota(jnp.int32, (2 * BLOCK, 2 * BLOCK), 0) & (BLOCK - 1)
    p = lax.broadcasted_iota(jnp.int32, (2 * BLOCK, 2 * BLOCK), 1) - BLOCK
    in_window = jnp.logical_and(p <= qi, p > qi - BLOCK)
    return jnp.logical_and(in_window, jnp.logical_or(p >= 0, n > 0))


def _sink_column(sink_ref, g, par):
    return jnp.concatenate([jnp.full((BLOCK, 1), sink_ref[4 * g + par], F32),
                            jnp.full((BLOCK, 1), sink_ref[4 * g + 2 + par], F32)], axis=0)


def _kv_cat(kp_ref, kc_ref, var):
    kcol, vcol = var * LANES, (var + 4) * LANES
    return (jnp.concatenate([kp_ref[:, kcol:kcol + LANES], kc_ref[:, kcol:kcol + LANES]], axis=0),
            jnp.concatenate([kp_ref[:, vcol:vcol + LANES], kc_ref[:, vcol:vcol + LANES]], axis=0))


def _softmax_stats(s, sink):
    m = jnp.maximum(jnp.max(s, axis=1, keepdims=True), sink)
    e = jnp.exp(s - m)
    return e, m, jnp.sum(e, axis=1, keepdims=True) + jnp.exp(sink - m)


def _attn_fwd(sinks, q, kvx, gates):
    def body(sink_ref, q_ref, kc_ref, kp_ref, za_ref, out_ref, ag_ref):
        valid = _window_mask(pl.program_id(0))
        chains = [(g, par) for g in range(2) for par in range(2)]
        kv = [_kv_cat(kp_ref, kc_ref, 2 * g + par) for g, par in chains]
        scores, outs = {}, []

        def issue_scores(i):
            g, _ = chains[i]
            scores[i] = _dot(q_ref[2 * g:2 * g + 2].reshape(2 * BLOCK, LANES), kv[i][0], NT)

        issue_scores(0)
        issue_scores(1)
        for i, (g, par) in enumerate(chains):
            e, _, den = _softmax_stats(jnp.where(valid, scores[i], NEG_INF), _sink_column(sink_ref, g, par))
            if i + 2 < len(chains):
                issue_scores(i + 2)
            outs.append(_dot(e, kv[i][1]) * (1.0 / den))
        for g in range(2):
            acc = outs[2 * g] + outs[2 * g + 1]
            for i in range(2):
                pair = 2 * g + i
                o = acc[i * BLOCK:(i + 1) * BLOCK]
                out_ref[pair] = o
                gate, _ = _silu_and_grad(za_ref[:, pair * LANES:(pair + 1) * LANES])
                ag_ref[:, pair * LANES:(pair + 1) * LANES] = (o * gate).astype(MXU_DTYPE)

    blk = lambda w: pl.BlockSpec((BLOCK, w), lambda n: (n, 0))
    tiles = pl.BlockSpec((N_PAIRS, BLOCK, LANES), lambda n: (0, n, 0))
    return pl.pallas_call(
        body,
        name="attn_fwd",
        grid=(N_BLOCKS,),
        in_specs=[pl.BlockSpec(memory_space=pltpu.SMEM), tiles, blk(8 * LANES),
                  pl.BlockSpec((BLOCK, 8 * LANES), lambda n: (jnp.maximum(n - 1, 0), 0)),
                  blk(ATTN_W)],
        out_specs=(tiles, blk(ATTN_W)),
        out_shape=(jax.ShapeDtypeStruct((N_PAIRS, SEQ, LANES), F32),
                   jax.ShapeDtypeStruct((SEQ, ATTN_W), MXU_DTYPE)),
        compiler_params=_params(("arbitrary",)),
    )(sinks, q, kvx, kvx, gates)


def _sgu_forward_chunk(us, vs, lng, lnb, w_ref, bias_ref):
    u = _gelu(us)
    vg = _gelu(vs)
    mu = jnp.mean(vg, axis=-1, keepdims=True)
    xc = vg - mu
    rstd = lax.rsqrt(jnp.mean(xc * xc, axis=-1, keepdims=True) + NORM_EPS)
    vhat = xc * rstd
    vln = vhat * lng + lnb
    low = lax.broadcasted_iota(jnp.int32, (BLOCK, LANES), 1) < HALF
    tril = (lax.broadcasted_iota(jnp.int32, (BLOCK, BLOCK), 0)
            >= lax.broadcasted_iota(jnp.int32, (BLOCK, BLOCK), 1))
    mixed = []
    for pair in range(N_SGU_HEADS // 2):
        vp = vln[:, pair * LANES:(pair + 1) * LANES]
        w0 = jnp.where(tril, w_ref[2 * pair], 0.0)
        w1 = jnp.where(tril, w_ref[2 * pair + 1], 0.0)
        mixed.append(_dot(w0, jnp.where(low, vp, 0.0)) + _dot(w1, jnp.where(low, 0.0, vp))
                     + bias_ref[:, pair * LANES:(pair + 1) * LANES])
    return u, vhat, rstd, vln, mixed


def _sgu_fwd(gates, ln_g, ln_b, sgu_w, bias_full):
    def body(us_ref, vs_ref, zs_ref, lng_ref, lnb_ref, w_ref, bias_ref, sg_ref):
        u, _, _, _, mixed = _sgu_forward_chunk(us_ref[...], vs_ref[...], lng_ref[...], lnb_ref[...],
                                               w_ref, bias_ref)
        for pair in range(N_SGU_HEADS // 2):
            cols = slice(pair * LANES, (pair + 1) * LANES)
            gate, _ = _silu_and_grad(zs_ref[:, cols])
            sg_ref[:, cols] = (u[:, cols] * mixed[pair] * gate).astype(MXU_DTYPE)

    col = lambda k: pl.BlockSpec((BLOCK, SGU_W), lambda n: (n, k))
    return pl.pallas_call(
        body,
        name="sgu_fwd",
        grid=(N_BLOCKS,),
        in_specs=[col(1), col(2), col(3), _full((1, SGU_W)), _full((1, SGU_W)),
                  _full((N_SGU_HEADS, BLOCK, BLOCK)), _full((BLOCK, SGU_W))],
        out_specs=pl.BlockSpec((BLOCK, SGU_W), lambda n: (n, 0)),
        out_shape=jax.ShapeDtypeStruct((SEQ, SGU_W), MXU_DTYPE),
        compiler_params=_params(("arbitrary",)),
    )(gates, gates, gates, ln_g, ln_b, sgu_w, bias_full)


def _out_proj_loss(ag, sg, x, target, wout, b_out, final_g):
    tm = TOKEN_TILE

    def body(ag_ref, sg_ref, x_ref, t_ref, w_ref, b_ref, gf_ref, gres_ref, dmix_ref, gw_ref, vec_ref):
        @pl.when(pl.program_id(0) == 0)
        def _():
            gw_ref[...] = jnp.zeros_like(gw_ref)
            vec_ref[...] = jnp.zeros_like(vec_ref)

        a = ag_ref[...]
        s = sg_ref[...]
        xo = x_ref[...] + (_dot(a, w_ref[0:ATTN_W, :]) + _dot(s, w_ref[ATTN_W:, :])) + b_ref[...]
        r = lax.rsqrt(jnp.mean(xo * xo, axis=-1, keepdims=True) + NORM_EPS)
        xn = xo * r
        gf = gf_ref[...]
        err = xn * gf - t_ref[...]
        loss = 0.5 * jnp.sum(jnp.mean(err * err, axis=-1, keepdims=True), axis=0, keepdims=True)
        dy = err * (1.0 / D_MODEL)
        dxn = dy * gf
        gres = r * (dxn - xn * jnp.mean(dxn * xn, axis=-1, keepdims=True))
        vec_ref[0:1, :] += jnp.broadcast_to(loss, (1, D_MODEL))
        vec_ref[1:2, :] += jnp.sum(dy * xn, axis=0, keepdims=True)
        vec_ref[2:3, :] += jnp.sum(gres, axis=0, keepdims=True)
        gres_ref[...] = gres
        gb = gres.astype(MXU_DTYPE)
        dmix_ref[:, 0:ATTN_W] = _dot(gb, w_ref[0:ATTN_W, :], NT)
        dmix_ref[:, ATTN_W:] = _dot(gb, w_ref[ATTN_W:, :], NT)
        gw_ref[0:ATTN_W, :] += _dot(a, gb, TN)
        gw_ref[ATTN_W:, :] += _dot(s, gb, TN)

    tile = lambda w: pl.BlockSpec((tm, w), lambda i: (i, 0))
    return pl.pallas_call(
        body,
        name="out_proj_loss",
        grid=(SEQ // tm,),
        in_specs=[tile(ATTN_W), tile(SGU_W), tile(D_MODEL), tile(D_MODEL),
                  _full((D_MODEL, D_MODEL)), _full((1, D_MODEL)), _full((1, D_MODEL))],
        out_specs=(tile(D_MODEL), tile(D_MODEL), _full((D_MODEL, D_MODEL)), _full((8, D_MODEL))),
        out_shape=(jax.ShapeDtypeStruct((SEQ, D_MODEL), F32),
                   jax.ShapeDtypeStruct((SEQ, D_MODEL), F32),
                   jax.ShapeDtypeStruct((D_MODEL, D_MODEL), F32),
                   jax.ShapeDtypeStruct((8, D_MODEL), F32)),
        compiler_params=_params(("arbitrary",), VMEM_LIMIT),
    )(ag, sg, x, target, wout, b_out, final_g)


def _sgu_bwd(dmix, gates, ln_g, ln_b, sgu_w, bias_full):
    last = N_BLOCKS - 1

    def body(d_ref, us_ref, vs_ref, zs_ref, lng_ref, lnb_ref, w_ref, bias_ref,
             dp_ref, gw_ref, gb_ref, gln_ref, gbin_ref, wt_ref, gbias_ref):
        c = pl.program_id(0)
        tril = (lax.broadcasted_iota(jnp.int32, (BLOCK, BLOCK), 0)
                >= lax.broadcasted_iota(jnp.int32, (BLOCK, BLOCK), 1))

        @pl.when(c == 0)
        def _():
            gw_ref[...] = jnp.zeros_like(gw_ref)
            gln_ref[...] = jnp.zeros_like(gln_ref)
            gbin_ref[...] = jnp.zeros_like(gbin_ref)
            gbias_ref[...] = jnp.zeros_like(gbias_ref)
            for hh in range(N_SGU_HEADS):
                wt_ref[hh] = jnp.where(tril, w_ref[hh], 0.0).T.astype(MXU_DTYPE)

        us = us_ref[...]
        vs = vs_ref[...]
        lng = lng_ref[...]
        u, vhat, rstd, vln, mixed = _sgu_forward_chunk(us, vs, lng, lnb_ref[...], w_ref, bias_ref)
        low = lax.broadcasted_iota(jnp.int32, (BLOCK, LANES), 1) < HALF
        du_parts, dzs_parts, dvln_parts = [], [], []
        for pair in range(N_SGU_HEADS // 2):
            cols = slice(pair * LANES, (pair + 1) * LANES)
            dsg = d_ref[:, cols]
            gate, gate_grad = _silu_and_grad(zs_ref[:, cols])
            up = u[:, cols]
            du_parts.append(dsg * mixed[pair] * gate)
            dzs_parts.append(dsg * up * mixed[pair] * gate_grad)
            dmixed = dsg * up * gate
            gbias_ref[:, cols] += dmixed
            dm_lo = jnp.where(low, dmixed, 0.0)
            dm_hi = jnp.where(low, 0.0, dmixed)
            vp = vln[:, cols]
            gw_ref[2 * pair] += _dot(dm_lo, vp, NT)
            gw_ref[2 * pair + 1] += _dot(dm_hi, vp, NT)
            dvln_parts.append(_dot(wt_ref[2 * pair], dm_lo) + _dot(wt_ref[2 * pair + 1], dm_hi))
        dvln = jnp.concatenate(dvln_parts, axis=1)
        gln_ref[0:1, :] += jnp.sum(dvln * vhat, axis=0, keepdims=True)
        gln_ref[1:2, :] += jnp.sum(dvln, axis=0, keepdims=True)
        dvhat = dvln * lng
        dvg = rstd * (dvhat - jnp.mean(dvhat, axis=-1, keepdims=True)
                      - vhat * jnp.mean(dvhat * vhat, axis=-1, keepdims=True))
        dus = jnp.concatenate(du_parts, axis=1) * _gelu_grad(us)
        dvs = dvg * _gelu_grad(vs)
        dzs = jnp.concatenate(dzs_parts, axis=1)
        for k, val in enumerate((dus, dvs, dzs)):
            dp_ref[:, k * SGU_W:(k + 1) * SGU_W] = val.astype(MXU_DTYPE)
            gbin_ref[:, k * SGU_W:(k + 1) * SGU_W] += jnp.sum(val, axis=0, keepdims=True)

        @pl.when(c == last)
        def _():
            for hh in range(N_SGU_HEADS):
                gw_ref[hh] = jnp.where(tril, gw_ref[hh], 0.0)
            head_of_lane = lax.broadcasted_iota(jnp.int32, (N_SGU_HEADS, SGU_W), 1) // HEAD_DIM
            select = (head_of_lane == lax.broadcasted_iota(jnp.int32, (N_SGU_HEADS, SGU_W), 0)).astype(F32)
            gb_ref[...] = lax.dot_general(select, gbias_ref[...], NT, precision=lax.Precision.HIGHEST,
                                          preferred_element_type=F32)

    col = lambda k: pl.BlockSpec((BLOCK, SGU_W), lambda n: (n, k))
    return pl.pallas_call(
        body,
        name="sgu_bwd",
        grid=(N_BLOCKS,),
        in_specs=[col(1), col(1), col(2), col(3), _full((1, SGU_W)), _full((1, SGU_W)),
                  _full((N_SGU_HEADS, BLOCK, BLOCK)), _full((BLOCK, SGU_W))],
        out_specs=(pl.BlockSpec((BLOCK, SGU_SECTION), lambda n: (n, 0)),
                   _full((N_SGU_HEADS, BLOCK, BLOCK)), _full((N_SGU_HEADS, BLOCK)),
                   _full((8, SGU_W)), _full((1, SGU_SECTION))),
        out_shape=(jax.ShapeDtypeStruct((SEQ, SGU_SECTION), MXU_DTYPE),
                   jax.ShapeDtypeStruct((N_SGU_HEADS, BLOCK, BLOCK), F32),
                   jax.ShapeDtypeStruct((N_SGU_HEADS, BLOCK), F32),
                   jax.ShapeDtypeStruct((8, SGU_W), F32),
                   jax.ShapeDtypeStruct((1, SGU_SECTION), F32)),
        scratch_shapes=[pltpu.VMEM((N_SGU_HEADS, BLOCK, BLOCK), MXU_DTYPE),
                        pltpu.VMEM((BLOCK, SGU_W), F32)],
        compiler_params=_params(("arbitrary",)),
    )(dmix, gates, gates, gates, ln_g, ln_b, sgu_w, bias_full)


def _attn_bwd(sinks, dmix, q, kvx, out, gates):
    last = N_BLOCKS - 1

    def body(sink_ref, d_ref, q_ref, kc_ref, kp_ref, o_ref, za_ref, dp_ref, gsink_ref, gbin_ref,
             pend_ref, carry_ref):
        n = pl.program_id(0)

        @pl.when(n == 0)
        def _():
            gsink_ref[...] = jnp.zeros_like(gsink_ref)
            gbin_ref[...] = jnp.zeros_like(gbin_ref)
            carry_ref[...] = jnp.zeros_like(carry_ref)

        @pl.when(n > 0)
        def _():
            dp_ref[:, 0:ATTN_W] = pend_ref[:, 0:ATTN_W]
            dp_ref[:, GATE0:ATTN_SECTION] = pend_ref[:, ATTN_W:]

        @pl.when(n > last)
        def _():
            dp_ref[:, KV0:GATE0] = carry_ref[...].astype(MXU_DTYPE)

        @pl.when(n <= last)
        def _():
            valid = _window_mask(n)
            low = lax.broadcasted_iota(jnp.int32, (2 * BLOCK, LANES), 1) < HALF
            lane_row = lax.broadcasted_iota(jnp.int32, (1, LANES), 1)
            gsink = jnp.zeros((1, LANES), F32)
            chains = [(g, par) for g in range(2) for par in range(2)]
            kv = [_kv_cat(kp_ref, kc_ref, 2 * g + par) for g, par in chains]
            qs, douts, dzas, prods = [], [], [], []
            for g in range(2):
                o = o_ref[2 * g:2 * g + 2].reshape(2 * BLOCK, LANES)
                cols = slice(2 * g * LANES, (2 * g + 2) * LANES)
                stack = lambda ref: jnp.concatenate([ref[:, cols][:, 0:LANES], ref[:, cols][:, LANES:]], axis=0)
                dg = stack(d_ref)
                gate, gate_grad = _silu_and_grad(stack(za_ref))
                qs.append(q_ref[2 * g:2 * g + 2].reshape(2 * BLOCK, LANES))
                douts.append((dg * gate).astype(MXU_DTYPE))
                dzas.append(dg * o * gate_grad)
                prods.append(dg * gate * o)

            first = {}

            def issue_first(i):
                g, _ = chains[i]
                first[i] = (_dot(qs[g], kv[i][0], NT), _dot(douts[g], kv[i][1], NT))

            issue_first(0)
            issue_first(1)
            dqs, dk_parts, dv_parts = [], [], []
            for i, (g, par) in enumerate(chains):
                mine = low if par == 0 else jnp.logical_not(low)
                sink = _sink_column(sink_ref, g, par)
                delta = jnp.sum(jnp.where(mine, prods[g], 0.0), axis=1, keepdims=True)
                e, m, den = _softmax_stats(jnp.where(valid, first[i][0], NEG_INF), sink)
                inv = 1.0 / den
                p = (e * inv)
                ds = (p * (first[i][1] - delta)).astype(MXU_DTYPE)
                p = p.astype(MXU_DTYPE)
                gs = jnp.exp(sink - m) * inv * delta
                for k, h in enumerate((4 * g + par, 4 * g + 2 + par)):
                    total = jnp.sum(gs[k * BLOCK:(k + 1) * BLOCK], axis=0, keepdims=True)
                    gsink = jnp.where(lane_row == h, -total, gsink)
                if i + 2 < len(chains):
                    issue_first(i + 2)
                dqs.append(_dot(ds, kv[i][0]))
                dk_parts.append(jnp.where(mine, _dot(ds, qs[g], TN), 0.0))
                dv_parts.append(jnp.where(mine, _dot(p, douts[g], TN), 0.0))
            for g in range(2):
                dq = (dqs[2 * g] + dqs[2 * g + 1]) * SCALE
                dza = dzas[g]
                for i in range(2):
                    pair = 2 * g + i
                    rows = slice(i * BLOCK, (i + 1) * BLOCK)
                    lanes = slice(pair * LANES, (pair + 1) * LANES)
                    pend_ref[:, lanes] = dq[rows].astype(MXU_DTYPE)
                    gbin_ref[:, lanes] += jnp.sum(dq[rows], axis=0, keepdims=True)
                    zl = slice(ATTN_W + pair * LANES, ATTN_W + (pair + 1) * LANES)
                    pend_ref[:, zl] = dza[rows].astype(MXU_DTYPE)
                    gl = slice(GATE0 + pair * LANES, GATE0 + (pair + 1) * LANES)
                    gbin_ref[:, gl] += jnp.sum(dza[rows], axis=0, keepdims=True)
            gsink_ref[...] += gsink
            for k, parts in enumerate((dk_parts, dv_parts)):
                both = parts[0] + parts[3] + pltpu.roll(parts[1] + parts[2], HALF, 1)
                lanes = slice(k * KV_W, (k + 1) * KV_W)
                done = carry_ref[:, lanes] + both[0:BLOCK]
                dp_ref[:, KV0 + k * KV_W:KV0 + (k + 1) * KV_W] = done.astype(MXU_DTYPE)
                carry_ref[:, lanes] = both[BLOCK:]
                gbin_ref[:, KV0 + k * KV_W:KV0 + (k + 1) * KV_W] += jnp.sum(both, axis=0, keepdims=True)

    at = lambda n: jnp.minimum(n, last)
    blk = lambda w: pl.BlockSpec((BLOCK, w), lambda n: (at(n), 0))
    tiles = pl.BlockSpec((N_PAIRS, BLOCK, LANES), lambda n: (0, at(n), 0))
    return pl.pallas_call(
        body,
        name="attn_bwd",
        grid=(N_BLOCKS + 1,),
        in_specs=[pl.BlockSpec(memory_space=pltpu.SMEM),
                  blk(ATTN_W),
                  tiles,
                  blk(8 * LANES),
                  pl.BlockSpec((BLOCK, 8 * LANES), lambda n: (jnp.maximum(at(n) - 1, 0), 0)),
                  tiles,
                  blk(ATTN_W)],
        out_specs=(pl.BlockSpec((BLOCK, ATTN_SECTION), lambda n: (jnp.maximum(n - 1, 0), 0)),
                   _full((1, LANES)), _full((1, ATTN_SECTION))),
        out_shape=(jax.ShapeDtypeStruct((SEQ, ATTN_SECTION), MXU_DTYPE),
                   jax.ShapeDtypeStruct((1, LANES), F32),
                   jax.ShapeDtypeStruct((1, ATTN_SECTION), F32)),
        scratch_shapes=[pltpu.VMEM((BLOCK, 2 * ATTN_W), MXU_DTYPE), pltpu.VMEM((BLOCK, 2 * KV_W), F32)],
        compiler_params=_params(("arbitrary",)),
    )(sinks, dmix, q, kvx, kvx, out, gates)


def _in_proj_bwd(dpa, dps, win_t, x, norm_g, gres, gwin):
    tm = TOKEN_TILE
    steps = SEQ // tm

    def body(da_ref, ds_ref, w_ref, x_ref, g_ref, gres_ref, gwin_ref, gx_ref, gng_ref, shard_ref,
             sa, ra, sb, rc, send_sems, recv_sems):
        step = pl.program_id(0)
        start, exchange, finish = _reduce_scatter_plan(_Copies(send_sems, recv_sems), 0, gwin_ref, WIN_ROWS,
                                                       sa, ra, sb, rc, shard_ref)

        @pl.when(step == 0)
        def _():
            gng_ref[...] = jnp.zeros_like(gng_ref)
            start()

        pl.when(step == 3)(exchange)

        dh = _dot(da_ref[...], w_ref[0:ATTN_SECTION, :]) + _dot(ds_ref[...], w_ref[ATTN_SECTION:, :])
        xv = x_ref[...]
        r = lax.rsqrt(jnp.mean(xv * xv, axis=-1, keepdims=True) + NORM_EPS)
        xn = xv * r
        gng_ref[...] += jnp.sum(dh * xn, axis=0, keepdims=True)
        dxn = dh * g_ref[...]
        gx_ref[...] = r * (dxn - xn * jnp.mean(dxn * xn, axis=-1, keepdims=True)) + gres_ref[...]

        pl.when(step == steps - 1)(finish)

    tile = lambda w: pl.BlockSpec((tm, w), lambda i: (i, 0))
    return pl.pallas_call(
        body,
        name="in_proj_bwd",
        grid=(steps,),
        in_specs=[tile(ATTN_SECTION), tile(SGU_SECTION), _full((IN_W, D_MODEL)), tile(D_MODEL),
                  _full((1, D_MODEL)), tile(D_MODEL), VMEM_SPEC],
        out_specs=(tile(D_MODEL), _full((1, D_MODEL)), VMEM_SPEC),
        out_shape=(jax.ShapeDtypeStruct((SEQ, D_MODEL), F32), jax.ShapeDtypeStruct((1, D_MODEL), F32),
                   jax.ShapeDtypeStruct((WIN_ROWS, D_MODEL), F32)),
        scratch_shapes=_reduce_scatter_scratch(WIN_ROWS, D_MODEL, COMM_DTYPE) + _dma_sems(REDUCE_SEMS),
        compiler_params=_params(("arbitrary",), VMEM_LIMIT),
    )(dpa, dps, win_t, x, norm_g, gres, gwin)


def _win_grad(dpa, dps, h, gwout, gsguw):
    rows = 256
    n_attn = ATTN_SECTION // rows
    steps = n_attn + SGU_SECTION // rows

    def body(da_ref, ds_ref, h_ref, gwout_ref, gsguw_ref, o_ref, wout_shard_ref, sguw_full_ref,
             sa_w, ra_w, sb_w, rc_w, sa_s, ra_s, sb_s, rc_s, landing, send_sems, recv_sems):
        step = pl.program_id(0)
        copies = _Copies(send_sems, recv_sems)
        own_sguw = landing.at[_block_rows(_place(), SGUW_ROWS), :]
        plans = [_reduce_scatter_plan(copies, 0, gwout_ref, WOUT_ROWS, sa_w, ra_w, sb_w, rc_w, wout_shard_ref),
                 _reduce_scatter_plan(copies, REDUCE_SEMS, gsguw_ref, SGUW_ROWS, sa_s, ra_s, sb_s, rc_s, own_sguw)]
        gather = _gather_plan(copies, 2 * REDUCE_SEMS, landing, SGUW_ROWS)

        @pl.when(step == 0)
        def _():
            for start, _, _ in plans:
                start()

        @pl.when(step == 2)
        def _():
            for _, exchange, _ in plans:
                exchange()

        @pl.when(step == 6)
        def _():
            for _, _, finish in plans:
                finish()
            gather[0]()

        pl.when(step == 8)(gather[1])

        @pl.when(step < n_attn)
        def _():
            o_ref[...] = _dot(da_ref[...], h_ref[...], TN)

        @pl.when(step >= n_attn)
        def _():
            o_ref[...] = _dot(ds_ref[...], h_ref[...], TN)

        @pl.when(step == steps - 1)
        def _():
            gather[2]()
            sguw_full_ref[...] = landing[...]

    return pl.pallas_call(
        body,
        name="win_grad",
        grid=(steps,),
        in_specs=[pl.BlockSpec((SEQ, rows), lambda i: (0, jnp.minimum(i, n_attn - 1))),
                  pl.BlockSpec((SEQ, rows), lambda i: (0, jnp.maximum(i - n_attn, 0))),
                  _full((SEQ, D_MODEL)), VMEM_SPEC, VMEM_SPEC],
        out_specs=(pl.BlockSpec((rows, D_MODEL), lambda i: (i, 0)), VMEM_SPEC,
                   _full((N_SGU_HEADS * BLOCK, BLOCK))),
        out_shape=(jax.ShapeDtypeStruct((IN_W, D_MODEL), F32),
                   jax.ShapeDtypeStruct((WOUT_ROWS, D_MODEL), F32),
                   jax.ShapeDtypeStruct((N_SGU_HEADS * BLOCK, BLOCK), F32)),
        scratch_shapes=(_reduce_scatter_scratch(WOUT_ROWS, D_MODEL, COMM_DTYPE)
                        + _reduce_scatter_scratch(SGUW_ROWS, BLOCK, F32)
                        + [pltpu.VMEM((N_SGU_HEADS * BLOCK, BLOCK), F32)]
                        + _dma_sems(2 * REDUCE_SEMS + GATHER_SEMS)),
        compiler_params=_params(("arbitrary",), VMEM_LIMIT),
    )(dpa, dps, h, gwout, gsguw)


VEC_NORM_G, VEC_B_IN, VEC_SINKS, VEC_LN_G, VEC_LN_B, VEC_B_OUT, VEC_FINAL_G, VEC_LOSS, VEC_SGU_B = 0, 1, 2, 3, 4, 5, 6, 7, 8


def _adamw(w, g, m, v):
    m = ADAM_B1 * m + (1.0 - ADAM_B1) * g
    v = ADAM_B2 * v + (1.0 - ADAM_B2) * (g * g)
    m_hat = m / (1.0 - ADAM_B1 ** ADAM_STEP)
    v_hat = v / (1.0 - ADAM_B2 ** ADAM_STEP)
    delta = -ADAM_LR * (m_hat / (jnp.sqrt(v_hat) + ADAM_EPS) + ADAM_WD * w)
    return delta, m, v


def _adamw_shard(name, g, w, m, v, block_rows):
    def body(g_ref, w_ref, m_ref, v_ref, d_ref, nm_ref, nv_ref):
        d_ref[...], nm_ref[...], nv_ref[...] = _adamw(w_ref[...], g_ref[...], m_ref[...], v_ref[...])

    rows, cols = w.shape
    spec = pl.BlockSpec((block_rows, cols), lambda i: (i, 0))
    return pl.pallas_call(
        body,
        name=name,
        grid=(rows // block_rows,),
        in_specs=[spec] * 4,
        out_specs=(spec,) * 3,
        out_shape=(jax.ShapeDtypeStruct(w.shape, F32),) * 3,
        compiler_params=_params(("arbitrary",)),
    )(g, w, m, v)


def _reduce_vectors(gng, gbin_a, gbin_s, gsink, gln, gsgub, vec4):
    def body(gng_ref, gba_ref, gbs_ref, gsink_ref, gln_ref, gsgub_ref, vec4_ref, out_ref,
             vec_ref, ra_vec, slots, send_sems, recv_sems):
        x, y, c = _place()
        copies = _Copies(send_sems, recv_sems)

        vec_ref[...] = jnp.zeros_like(vec_ref)
        vec_ref[VEC_NORM_G:VEC_NORM_G + 1, 0:D_MODEL] = gng_ref[...]
        vec_ref[VEC_B_IN:VEC_B_IN + 1, 0:ATTN_SECTION] = gba_ref[...]
        vec_ref[VEC_B_IN:VEC_B_IN + 1, ATTN_SECTION:IN_W] = gbs_ref[...]
        vec_ref[VEC_SINKS:VEC_SINKS + 1, 0:LANES] = gsink_ref[...]
        vec_ref[VEC_LN_G:VEC_LN_G + 1, 0:SGU_W] = gln_ref[0:1, :]
        vec_ref[VEC_LN_B:VEC_LN_B + 1, 0:SGU_W] = gln_ref[1:2, :]
        vec_ref[VEC_B_OUT:VEC_B_OUT + 1, 0:D_MODEL] = vec4_ref[2:3, :]
        vec_ref[VEC_FINAL_G:VEC_FINAL_G + 1, 0:D_MODEL] = vec4_ref[1:2, :]
        vec_ref[VEC_LOSS:VEC_LOSS + 1, 0:D_MODEL] = vec4_ref[0:1, :]
        vec_ref[VEC_SGU_B:VEC_SGU_B + N_SGU_HEADS, 0:BLOCK] = gsgub_ref[...]

        to_sibling = copies(0, vec_ref, ra_vec, (x, y, 1 - c))
        to_sibling.start()
        to_sibling.wait_recv()

        def chip_slot(place):
            return slots.at[pl.ds(pl.multiple_of((2 * place[0] + place[1]) * VEC_ROWS, 8), VEC_ROWS), :]

        mine = chip_slot((x, y))
        mine[...] = vec_ref[...] + ra_vec[...]
        to_chips = [copies(i, mine, mine, (*_chip(rel), c)) for i, rel in enumerate(RELATIONS[1:], start=1)]
        for cp in to_chips:
            cp.start()
        for i, rel in enumerate(RELATIONS[1:], start=1):
            theirs = chip_slot(_chip(rel))
            copies(i, theirs, theirs, (x, y, c)).wait_recv()
        out_ref[...] = ((slots[0:VEC_ROWS, :] + slots[VEC_ROWS:2 * VEC_ROWS, :])
                        + slots[2 * VEC_ROWS:3 * VEC_ROWS, :]) + slots[3 * VEC_ROWS:, :]
        to_sibling.wait_send()
        for cp in to_chips:
            cp.wait_send()

    return pl.pallas_call(
        body,
        name="reduce_vectors",
        in_specs=[VMEM_SPEC] * 7,
        out_specs=VMEM_SPEC,
        out_shape=jax.ShapeDtypeStruct((VEC_ROWS, IN_W), F32),
        scratch_shapes=[pltpu.VMEM((VEC_ROWS, IN_W), F32), pltpu.VMEM((VEC_ROWS, IN_W), F32),
                        pltpu.VMEM((4 * VEC_ROWS, IN_W), F32)] + _dma_sems(4),
    )(gng, gbin_a, gbin_s, gsink, gln, gsgub, vec4)


def _adamw_replicated(vec, gsguw, weights, m_state, v_state):
    n = len(SMALL)

    def body(*refs):
        vec_ref, gsguw_ref = refs[0], refs[1]
        w_refs, m_refs, v_refs = (refs[2 + k * n:2 + (k + 1) * n] for k in range(3))
        outs = refs[2 + 3 * n:]
        g_refs, d_refs, nm_refs, nv_refs = (outs[k * n:(k + 1) * n] for k in range(4))
        for i, (_, row, shape) in enumerate(SMALL):
            g = gsguw_ref[...] if row is None else vec_ref[row:row + shape[0], 0:shape[1]]
            g_refs[i][...] = g
            d_refs[i][...], nm_refs[i][...], nv_refs[i][...] = _adamw(
                w_refs[i][...], g, m_refs[i][...], v_refs[i][...])

    shapes = tuple(jax.ShapeDtypeStruct(shape, F32) for _, _, shape in SMALL)
    outs = pl.pallas_call(
        body,
        name="adamw_replicated",
        in_specs=[VMEM_SPEC] * (2 + 3 * n),
        out_specs=(VMEM_SPEC,) * (4 * n),
        out_shape=shapes * 4,
    )(vec, gsguw, *weights, *m_state, *v_state)
    return tuple(outs[k * n:(k + 1) * n] for k in range(4))


SMALL = (
    ("norm_g", VEC_NORM_G, (1, D_MODEL)),
    ("b_in", VEC_B_IN, (1, IN_W)),
    ("attn_sinks", VEC_SINKS, (1, N_Q_HEADS)),
    ("sgu_ln_g", VEC_LN_G, (1, SGU_W)),
    ("sgu_ln_b", VEC_LN_B, (1, SGU_W)),
    ("sgu_w", None, (N_SGU_HEADS * BLOCK, BLOCK)),
    ("sgu_b", VEC_SGU_B, (N_SGU_HEADS, BLOCK)),
    ("b_out", VEC_B_OUT, (1, D_MODEL)),
    ("final_norm_g", VEC_FINAL_G, (1, D_MODEL)),
)


def _local_grads(x, target, win_t, wout_shard, norm_g, b_in, attn_sinks, sgu_ln_g, sgu_ln_b, sgu_w, sgu_b, b_out,
                 final_g):
    sinks = attn_sinks.reshape(N_Q_HEADS)
    bias_full = jnp.repeat(sgu_b.T, HEAD_DIM, axis=1)
    h, q, kvx, gates, wout = _in_proj(x, norm_g, b_in, win_t, wout_shard)
    out, ag = _attn_fwd(sinks, q, kvx, gates)
    sg = _sgu_fwd(gates, sgu_ln_g, sgu_ln_b, sgu_w, bias_full)
    gres, dmix, gwout, vec4 = _out_proj_loss(ag, sg, x, target, wout, b_out, final_g)
    dps, gsguw, gsgub, gln, gbin_s = _sgu_bwd(dmix, gates, sgu_ln_g, sgu_ln_b, sgu_w, bias_full)
    dpa, gsink, gbin_a = _attn_bwd(sinks, dmix, q, kvx, out, gates)
    gwin, gwout_shard, gsguw_sum = _win_grad(dpa, dps, h, gwout, gsguw.reshape(N_SGU_HEADS * BLOCK, BLOCK))
    grad_x, gng, gwin_shard = _in_proj_bwd(dpa, dps, win_t, x, norm_g, gres, gwin)
    return grad_x, gwin_shard, gwout_shard, gsguw_sum, (gng, gbin_a, gbin_s, gsink, gln, gsgub, vec4)


def kernel(x, norm_g, w_in, b_in, attn_sinks, sgu_ln_g, sgu_ln_b, sgu_w, sgu_b, w_out, b_out, final_norm_g, loss_target, m_norm_g, m_w_in, m_b_in, m_attn_sinks, m_sgu_ln_g, m_sgu_ln_b, m_sgu_w, m_sgu_b, m_w_out, m_b_out, m_final_norm_g, v_norm_g, v_w_in, v_b_in, v_attn_sinks, v_sgu_ln_g, v_sgu_ln_b, v_sgu_w, v_sgu_b, v_w_out, v_b_out, v_final_norm_g):
    given = dict(norm_g=norm_g, b_in=b_in, attn_sinks=attn_sinks, sgu_ln_g=sgu_ln_g, sgu_ln_b=sgu_ln_b,
                 sgu_w=sgu_w, sgu_b=sgu_b, b_out=b_out, final_norm_g=final_norm_g)
    m_given = dict(norm_g=m_norm_g, b_in=m_b_in, attn_sinks=m_attn_sinks, sgu_ln_g=m_sgu_ln_g,
                   sgu_ln_b=m_sgu_ln_b, sgu_w=m_sgu_w, sgu_b=m_sgu_b, b_out=m_b_out, final_norm_g=m_final_norm_g)
    v_given = dict(norm_g=v_norm_g, b_in=v_b_in, attn_sinks=v_attn_sinks, sgu_ln_g=v_sgu_ln_g,
                   sgu_ln_b=v_sgu_ln_b, sgu_w=v_sgu_w, sgu_b=v_sgu_b, b_out=v_b_out, final_norm_g=v_final_norm_g)

    win_t = _all_gather_win(w_in[0].T)
    grad_x, gwin_t, gwout, gsguw, vec_parts = _local_grads(
        x[0], loss_target[0], win_t, w_out[0], norm_g, b_in, attn_sinks, sgu_ln_g, sgu_ln_b, sgu_w[0], sgu_b[0],
        b_out, final_norm_g.reshape(1, D_MODEL))

    t = lambda a: a[0].T
    d_win, nm_win, nv_win = _adamw_shard("adamw_w_in", gwin_t, t(w_in), t(m_w_in), t(v_w_in), WIN_ROWS // 2)
    d_wout, nm_wout, nv_wout = _adamw_shard("adamw_w_out", gwout, w_out[0], m_w_out[0], v_w_out[0], WOUT_ROWS)
    as_2d = lambda d: [d[name].reshape(shape) for name, _, shape in SMALL]
    vec = _reduce_vectors(*vec_parts)
    loss = vec[VEC_LOSS, 0]
    small = _adamw_replicated(vec, gsguw, as_2d(given), as_2d(m_given), as_2d(v_given))

    def assemble(big_in, big_out, k):
        vals = {name: small[k][i].reshape(given[name].shape) for i, (name, _, _) in enumerate(SMALL)}
        vals["w_in"] = big_in.T[None]
        vals["w_out"] = big_out[None]
        order = ("norm_g", "w_in", "b_in", "attn_sinks", "sgu_ln_g", "sgu_ln_b", "sgu_w", "sgu_b", "w_out",
                 "b_out", "final_norm_g")
        return [vals[name] for name in order]

    return (loss, grad_x[None],
            *assemble(gwin_t, gwout, 0), *assemble(d_win, d_wout, 1),
            *assemble(nm_win, nm_wout, 2), *assemble(nv_win, nv_wout, 3))
```

```python
import functools
import math

import jax
import jax.numpy as jnp
from jax import lax
from jax.experimental import pallas as pl
from jax.experimental.pallas import tpu as pltpu

F32 = jnp.float32
BF16 = jnp.bfloat16
MXU_DTYPE = BF16
COMM_DTYPE = BF16

D_MODEL = 1024
SEQ = 4096
HEAD_DIM = 64
N_Q_HEADS = 8
Q_PER_KV = 4
BLOCK = 128
N_BLOCKS = SEQ // BLOCK
ATTN_W = 512
KV_W = 128
SGU_W = 512
N_SGU_HEADS = 8
IN_W = 2816
NORM_EPS = 1e-5
NEG_INF = -1e30
SCALE = HEAD_DIM ** -0.5
KV0 = ATTN_W
GATE0 = ATTN_W + 2 * KV_W
SGU0 = GATE0 + ATTN_W
ATTN_SECTION = SGU0
SGU_SECTION = IN_W - SGU0

ADAM_LR = 0.001
ADAM_B1 = 0.9
ADAM_B2 = 0.999
ADAM_EPS = 1e-08
ADAM_WD = 0.01
ADAM_STEP = 10

N_DEV = 8
WIN_ROWS = IN_W // N_DEV
WOUT_ROWS = D_MODEL // N_DEV
SGUW_ROWS = N_SGU_HEADS * BLOCK // N_DEV
VEC_ROWS = 16
MESH = pl.DeviceIdType.MESH

LANES = 128
HALF = LANES // 2
N_PAIRS = N_Q_HEADS * HEAD_DIM // LANES
KVX_W = 12 * LANES
TOKEN_TILE = 256
VMEM_LIMIT = 56 * 1024 * 1024

NN = (((1,), (0,)), ((), ()))
NT = (((1,), (1,)), ((), ()))
TN = (((0,), (0,)), ((), ()))


def _dot(a, b, dims=NN):
    return lax.dot_general(a.astype(MXU_DTYPE), b.astype(MXU_DTYPE), dims, preferred_element_type=F32)


def _gelu(x):
    return x * (lax.erf(x * (1.0 / math.sqrt(2.0))) + 1.0) * 0.5


def _gelu_grad(x):
    cdf = (lax.erf(x * (1.0 / math.sqrt(2.0))) + 1.0) * 0.5
    return cdf + x * jnp.exp(-0.5 * x * x) * (1.0 / math.sqrt(2.0 * math.pi))


def _silu_and_grad(z):
    s = jax.nn.sigmoid(z)
    return z * s, s * (1.0 + z * (1.0 - s))


def _params(semantics=None, vmem=None):
    kw = {}
    if semantics is not None:
        kw["dimension_semantics"] = semantics
    if vmem is not None:
        kw["vmem_limit_bytes"] = vmem
    return pltpu.CompilerParams(**kw)


def _full(shape):
    return pl.BlockSpec(shape, lambda *_: (0,) * len(shape))


VMEM_SPEC = pl.BlockSpec(memory_space=pltpu.VMEM)


RELATIONS = ((0, 0), (1, 0), (0, 1), (1, 1))


def _place():
    return lax.axis_index("x"), lax.axis_index("y"), lax.axis_index("c")


def _chip(rel):
    x, y, _ = _place()
    return (1 - x if rel[0] else x, 1 - y if rel[1] else y)


def _block_rows(place, n_rows):
    px, py, pc = place
    return pl.ds(pl.multiple_of((4 * px + 2 * py + pc) * n_rows, 16), n_rows)


class _Copies:
    def __init__(self, send_sems, recv_sems):
        self.send_sems, self.recv_sems = send_sems, recv_sems

    def __call__(self, k, src, dst, to):
        return pltpu.make_async_remote_copy(src_ref=src, dst_ref=dst, send_sem=self.send_sems.at[k],
                                            recv_sem=self.recv_sems.at[k], device_id=to, device_id_type=MESH)


def _gather_plan(copies, sem0, full_ref, n_rows):
    x, y, c = _place()
    me, sibling = (x, y, c), (x, y, 1 - c)
    chips = [_chip(rel) for rel in RELATIONS[1:]]

    def cp(k, block, to):
        rows = full_ref.at[_block_rows(block, n_rows), :]
        return copies(sem0 + k, rows, rows, to)

    first = [cp(0, me, sibling)] + [cp(1 + j, me, (*chip, c)) for j, chip in enumerate(chips)]
    passed = [cp(4 + j, (*chip, c), sibling) for j, chip in enumerate(chips)]

    def start():
        for f in first:
            f.start()

    def forward():
        for j, chip in enumerate(chips):
            cp(1 + j, (*chip, c), me).wait_recv()
            passed[j].start()

    def finish():
        cp(0, sibling, me).wait_recv()
        for j, chip in enumerate(chips):
            cp(4 + j, (*chip, 1 - c), me).wait_recv()
        for f in first + passed:
            f.wait_send()

    return start, forward, finish


GATHER_SEMS = 7


def _reduce_scatter_plan(copies, sem0, part_ref, n_rows, sa, ra, sb, rc, res_ref):
    x, y, c = _place()
    sibling = (x, y, 1 - c)
    n = n_rows
    level1 = copies(sem0, sa, ra, sibling)

    def level2(i):
        slot = pl.ds((i - 1) * n, n)
        return copies(sem0 + i, sb.at[slot, :], rc.at[slot, :], (*_chip(RELATIONS[i]), c))

    def start():
        for i, rel in enumerate(RELATIONS):
            sa[i * n:(i + 1) * n, :] = part_ref[_block_rows((*_chip(rel), 1 - c), n), :].astype(sa.dtype)
        level1.start()

    def exchange():
        level1.wait_recv()
        for i, rel in enumerate(RELATIONS):
            total = part_ref[_block_rows((*_chip(rel), c), n), :] + ra[i * n:(i + 1) * n, :].astype(F32)
            if i == 0:
                res_ref[...] = total
            else:
                sb[(i - 1) * n:i * n, :] = total.astype(sb.dtype)
                level2(i).start()

    def finish():
        acc = res_ref[...]
        for i in range(1, len(RELATIONS)):
            level2(i).wait_recv()
            acc = acc + rc[(i - 1) * n:i * n, :].astype(F32)
        res_ref[...] = acc
        level1.wait_send()
        for i in range(1, len(RELATIONS)):
            level2(i).wait_send()

    return start, exchange, finish


REDUCE_SEMS = 4


def _reduce_scatter_scratch(n_rows, width, dtype):
    return [pltpu.VMEM((4 * n_rows, width), dtype), pltpu.VMEM((4 * n_rows, width), dtype),
            pltpu.VMEM((3 * n_rows, width), dtype), pltpu.VMEM((3 * n_rows, width), dtype)]


def _dma_sems(n):
    return [pltpu.SemaphoreType.DMA((n,)), pltpu.SemaphoreType.DMA((n,))]


def _all_gather_win(win_t_shard):
    def body(win_ref, full_ref, send_sems, recv_sems):
        full_ref[_block_rows(_place(), WIN_ROWS), :] = win_ref[...].astype(COMM_DTYPE)
        start, forward, finish = _gather_plan(_Copies(send_sems, recv_sems), 0, full_ref, WIN_ROWS)
        start()
        forward()
        finish()

    return pl.pallas_call(
        body,
        name="all_gather_win",
        out_shape=jax.ShapeDtypeStruct((IN_W, D_MODEL), COMM_DTYPE),
        in_specs=[VMEM_SPEC],
        out_specs=VMEM_SPEC,
        scratch_shapes=_dma_sems(GATHER_SEMS),
        compiler_params=_params(vmem=VMEM_LIMIT),
    )(win_t_shard)


def _in_proj(x, norm_g, b_in, win_t, wout_shard):
    tm = TOKEN_TILE
    steps = SEQ // tm

    def body(x_ref, g_ref, b_ref, w_ref, wout_ref, h_ref, q_ref, kvx_ref, gate_ref, wfull_ref,
             landing, send_sems, recv_sems):
        step = pl.program_id(0)
        start, forward, finish = _gather_plan(_Copies(send_sems, recv_sems), 0, landing, WOUT_ROWS)

        @pl.when(step == 0)
        def _():
            landing[_block_rows(_place(), WOUT_ROWS), :] = wout_ref[...].astype(COMM_DTYPE)
            start()

        pl.when(step == steps // 2)(forward)

        xv = x_ref[...]
        r = lax.rsqrt(jnp.mean(xv * xv, axis=-1, keepdims=True) + NORM_EPS)
        h = ((xv * r) * g_ref[...]).astype(MXU_DTYPE)
        h_ref[...] = h

        def proj(lo, hi):
            return _dot(h, w_ref[lo:hi, :], NT) + b_ref[:, lo:hi]

        qs = proj(0, ATTN_W) * SCALE
        for pair in range(N_PAIRS):
            q_ref[pair] = qs[:, pair * LANES:(pair + 1) * LANES].astype(MXU_DTYPE)
        kv = proj(KV0, GATE0)
        low = lax.broadcasted_iota(jnp.int32, (tm, LANES), 1) < HALF
        for i in range(2):
            t = kv[:, i * LANES:(i + 1) * LANES]
            rot = pltpu.roll(t, HALF, 1)
            variants = (jnp.where(low, t, 0.0), jnp.where(low, 0.0, rot),
                        jnp.where(low, rot, 0.0), jnp.where(low, 0.0, t))
            for j, val in enumerate(variants):
                col = (4 * i + j) * LANES
                kvx_ref[:, col:col + LANES] = val.astype(MXU_DTYPE)
                if i == 1:
                    ones_elsewhere = jnp.where(low == (j % 2 == 0), val, 1.0)
                    kvx_ref[:, col + 4 * LANES:col + 5 * LANES] = ones_elsewhere.astype(MXU_DTYPE)
        for k in range(4):
            gate_ref[k] = proj(GATE0 + k * SGU_W, GATE0 + (k + 1) * SGU_W)

        @pl.when(step == steps - 1)
        def _():
            finish()
            wfull_ref[...] = landing[...]

    return pl.pallas_call(
        body,
        name="in_proj",
        grid=(steps,),
        in_specs=[pl.BlockSpec((tm, D_MODEL), lambda i: (i, 0)),
                  _full((1, D_MODEL)), _full((1, IN_W)), _full((IN_W, D_MODEL)), VMEM_SPEC],
        out_specs=(pl.BlockSpec((tm, D_MODEL), lambda i: (i, 0)),
                   pl.BlockSpec((N_PAIRS, tm, LANES), lambda i: (0, i, 0)),
                   pl.BlockSpec((tm, KVX_W), lambda i: (i, 0)),
                   pl.BlockSpec((4, tm, SGU_W), lambda i: (0, i, 0)),
                   _full((D_MODEL, D_MODEL))),
        out_shape=(jax.ShapeDtypeStruct((SEQ, D_MODEL), MXU_DTYPE),
                   jax.ShapeDtypeStruct((N_PAIRS, SEQ, LANES), MXU_DTYPE),
                   jax.ShapeDtypeStruct((SEQ, KVX_W), MXU_DTYPE),
                   jax.ShapeDtypeStruct((4, SEQ, SGU_W), F32),
                   jax.ShapeDtypeStruct((D_MODEL, D_MODEL), COMM_DTYPE)),
        scratch_shapes=[pltpu.VMEM((D_MODEL, D_MODEL), COMM_DTYPE)] + _dma_sems(GATHER_SEMS),
        compiler_params=_params(("arbitrary",), VMEM_LIMIT),
    )(x, norm_g, b_in, win_t, wout_shard)


def _window_mask(n):
    qi = lax.broadcasted_iota(jnp.int32, (2 * BLOCK, 2 * BLOCK), 0) & (BLOCK - 1)
    p = lax.broadcasted_iota(jnp.int32, (2 * BLOCK, 2 * BLOCK), 1) - BLOCK
    in_window = jnp.logical_and(p <= qi, p > qi - BLOCK)
    return jnp.logical_and(in_window, jnp.logical_or(p >= 0, n > 0))


def _sink_column(sink_ref, g, par):
    return jnp.concatenate([jnp.full((BLOCK, 1), sink_ref[4 * g + par], F32),
                            jnp.full((BLOCK, 1), sink_ref[4 * g + 2 + par], F32)], axis=0)


def _kv_cat(kp_ref, kc_ref, var, with_ones):
    kcol, vcol = var * LANES, (var + (8 if with_ones else 4)) * LANES
    return (jnp.concatenate([kp_ref[:, kcol:kcol + LANES], kc_ref[:, kcol:kcol + LANES]], axis=0),
            jnp.concatenate([kp_ref[:, vcol:vcol + LANES], kc_ref[:, vcol:vcol + LANES]], axis=0))


def _softmax_numerator(s, sink):
    m = jnp.maximum(jnp.max(s, axis=1, keepdims=True), sink)
    return jnp.exp(s - m), m


def _attn_fwd(sinks, q, kvx, gates):
    def body(sink_ref, q_ref, kc_ref, kp_ref, za_ref, out_ref, ag_ref):
        valid = _window_mask(pl.program_id(0))
        chains = [(g, par) for g in range(2) for par in range(2)]
        kv = [_kv_cat(kp_ref, kc_ref, 2 * g + par, True) for g, par in chains]
        scores, outs = {}, []

        def issue_scores(i):
            g, _ = chains[i]
            scores[i] = _dot(q_ref[2 * g:2 * g + 2].reshape(2 * BLOCK, LANES), kv[i][0], NT)

        issue_scores(0)
        issue_scores(1)
        low = lax.broadcasted_iota(jnp.int32, (2 * BLOCK, LANES), 1) < HALF
        for i, (g, par) in enumerate(chains):
            sink = _sink_column(sink_ref, g, par)
            e, m = _softmax_numerator(jnp.where(valid, scores[i], NEG_INF), sink)
            if i + 2 < len(chains):
                issue_scores(i + 2)
            o = _dot(e, kv[i][1])
            outs.append(o / (pltpu.roll(o, HALF, 1) + jnp.exp(sink - m)))
        for g in range(2):
            acc = jnp.where(low, outs[2 * g], outs[2 * g + 1])
            for i in range(2):
                pair = 2 * g + i
                o = acc[i * BLOCK:(i + 1) * BLOCK]
                out_ref[pair] = o
                gate, _ = _silu_and_grad(za_ref[:, pair * LANES:(pair + 1) * LANES])
                ag_ref[:, pair * LANES:(pair + 1) * LANES] = (o * gate).astype(MXU_DTYPE)

    blk = lambda w: pl.BlockSpec((BLOCK, w), lambda n: (n, 0))
    tiles = pl.BlockSpec((N_PAIRS, BLOCK, LANES), lambda n: (0, n, 0))
    return pl.pallas_call(
        body,
        name="attn_fwd",
        grid=(N_BLOCKS,),
        in_specs=[pl.BlockSpec(memory_space=pltpu.SMEM), tiles, blk(KVX_W),
                  pl.BlockSpec((BLOCK, KVX_W), lambda n: (jnp.maximum(n - 1, 0), 0)),
                  pl.BlockSpec((None, BLOCK, ATTN_W), lambda n: (0, n, 0))],
        out_specs=(tiles, blk(ATTN_W)),
        out_shape=(jax.ShapeDtypeStruct((N_PAIRS, SEQ, LANES), F32),
                   jax.ShapeDtypeStruct((SEQ, ATTN_W), MXU_DTYPE)),
        compiler_params=_params(("arbitrary",)),
    )(sinks, q, kvx, kvx, gates)


def _sgu_forward_chunk(us, vs, lng, lnb, w_ref, bias_ref):
    u = _gelu(us)
    vg = _gelu(vs)
    mu = jnp.mean(vg, axis=-1, keepdims=True)
    xc = vg - mu
    rstd = lax.rsqrt(jnp.mean(xc * xc, axis=-1, keepdims=True) + NORM_EPS)
    vhat = xc * rstd
    vln = vhat * lng + lnb
    low = lax.broadcasted_iota(jnp.int32, (BLOCK, LANES), 1) < HALF
    tril = (lax.broadcasted_iota(jnp.int32, (BLOCK, BLOCK), 0)
            >= lax.broadcasted_iota(jnp.int32, (BLOCK, BLOCK), 1))
    mixed = []
    for pair in range(N_SGU_HEADS // 2):
        vp = vln[:, pair * LANES:(pair + 1) * LANES]
        w0 = jnp.where(tril, w_ref[2 * pair], 0.0)
        w1 = jnp.where(tril, w_ref[2 * pair + 1], 0.0)
        mixed.append(_dot(w0, jnp.where(low, vp, 0.0)) + _dot(w1, jnp.where(low, 0.0, vp))
                     + bias_ref[:, pair * LANES:(pair + 1) * LANES])
    return u, vhat, rstd, vln, mixed


def _sgu_fwd(gates, ln_g, ln_b, sgu_w, bias_full):
    def body(us_ref, vs_ref, zs_ref, lng_ref, lnb_ref, w_ref, bias_ref, sg_ref):
        u, _, _, _, mixed = _sgu_forward_chunk(us_ref[...], vs_ref[...], lng_ref[...], lnb_ref[...],
                                               w_ref, bias_ref)
        for pair in range(N_SGU_HEADS // 2):
            cols = slice(pair * LANES, (pair + 1) * LANES)
            gate, _ = _silu_and_grad(zs_ref[:, cols])
            sg_ref[:, cols] = (u[:, cols] * mixed[pair] * gate).astype(MXU_DTYPE)

    col = lambda k: pl.BlockSpec((None, BLOCK, SGU_W), lambda n: (k, n, 0))
    return pl.pallas_call(
        body,
        name="sgu_fwd",
        grid=(N_BLOCKS,),
        in_specs=[col(1), col(2), col(3), _full((1, SGU_W)), _full((1, SGU_W)),
                  _full((N_SGU_HEADS, BLOCK, BLOCK)), _full((BLOCK, SGU_W))],
        out_specs=pl.BlockSpec((BLOCK, SGU_W), lambda n: (n, 0)),
        out_shape=jax.ShapeDtypeStruct((SEQ, SGU_W), MXU_DTYPE),
        compiler_params=_params(("arbitrary",)),
    )(gates, gates, gates, ln_g, ln_b, sgu_w, bias_full)


def _out_proj_loss(ag, sg, x, target, wout, b_out, final_g):
    tm = TOKEN_TILE

    def body(ag_ref, sg_ref, x_ref, t_ref, w_ref, b_ref, gf_ref, gres_ref, dmix_ref, gw_ref, vec_ref):
        @pl.when(pl.program_id(0) == 0)
        def _():
            gw_ref[...] = jnp.zeros_like(gw_ref)
            vec_ref[...] = jnp.zeros_like(vec_ref)

        a = ag_ref[...]
        s = sg_ref[...]
        xo = x_ref[...] + (_dot(a, w_ref[0:ATTN_W, :]) + _dot(s, w_ref[ATTN_W:, :])) + b_ref[...]
        r = lax.rsqrt(jnp.mean(xo * xo, axis=-1, keepdims=True) + NORM_EPS)
        xn = xo * r
        gf = gf_ref[...]
        err = xn * gf - t_ref[...]
        loss = 0.5 * jnp.sum(jnp.mean(err * err, axis=-1, keepdims=True), axis=0, keepdims=True)
        dy = err * (1.0 / D_MODEL)
        dxn = dy * gf
        gres = r * (dxn - xn * jnp.mean(dxn * xn, axis=-1, keepdims=True))
        vec_ref[0:1, :] += jnp.broadcast_to(loss, (1, D_MODEL))
        vec_ref[1:2, :] += jnp.sum(dy * xn, axis=0, keepdims=True)
        vec_ref[2:3, :] += jnp.sum(gres, axis=0, keepdims=True)
        gres_ref[...] = gres
        gb = gres.astype(MXU_DTYPE)
        dmix_ref[0] = _dot(gb, w_ref[0:ATTN_W, :], NT)
        dmix_ref[1] = _dot(gb, w_ref[ATTN_W:, :], NT)
        gw_ref[0:ATTN_W, :] += _dot(a, gb, TN)
        gw_ref[ATTN_W:, :] += _dot(s, gb, TN)

    tile = lambda w: pl.BlockSpec((tm, w), lambda i: (i, 0))
    return pl.pallas_call(
        body,
        name="out_proj_loss",
        grid=(SEQ // tm,),
        in_specs=[tile(ATTN_W), tile(SGU_W), tile(D_MODEL), tile(D_MODEL),
                  _full((D_MODEL, D_MODEL)), _full((1, D_MODEL)), _full((1, D_MODEL))],
        out_specs=(tile(D_MODEL), pl.BlockSpec((2, tm, ATTN_W), lambda i: (0, i, 0)), _full((D_MODEL, D_MODEL)),
                   _full((8, D_MODEL))),
        out_shape=(jax.ShapeDtypeStruct((SEQ, D_MODEL), F32),
                   jax.ShapeDtypeStruct((2, SEQ, ATTN_W), F32),
                   jax.ShapeDtypeStruct((D_MODEL, D_MODEL), F32),
                   jax.ShapeDtypeStruct((8, D_MODEL), F32)),
        compiler_params=_params(("arbitrary",), VMEM_LIMIT),
    )(ag, sg, x, target, wout, b_out, final_g)


def _sgu_bwd(dmix, gates, ln_g, ln_b, sgu_w, bias_full):
    last = N_BLOCKS - 1

    def body(d_ref, us_ref, vs_ref, zs_ref, lng_ref, lnb_ref, w_ref, bias_ref,
             dp_ref, gw_ref, gb_ref, gln_ref, gbin_ref, wt_ref, gbias_ref):
        c = pl.program_id(0)
        tril = (lax.broadcasted_iota(jnp.int32, (BLOCK, BLOCK), 0)
                >= lax.broadcasted_iota(jnp.int32, (BLOCK, BLOCK), 1))

        @pl.when(c == 0)
        def _():
            gw_ref[...] = jnp.zeros_like(gw_ref)
            gln_ref[...] = jnp.zeros_like(gln_ref)
            gbin_ref[...] = jnp.zeros_like(gbin_ref)
            gbias_ref[...] = jnp.zeros_like(gbias_ref)
            for hh in range(N_SGU_HEADS):
                wt_ref[hh] = jnp.where(tril, w_ref[hh], 0.0).T.astype(MXU_DTYPE)

        us = us_ref[...]
        vs = vs_ref[...]
        lng = lng_ref[...]
        u, vhat, rstd, vln, mixed = _sgu_forward_chunk(us, vs, lng, lnb_ref[...], w_ref, bias_ref)
        low = lax.broadcasted_iota(jnp.int32, (BLOCK, LANES), 1) < HALF
        du_parts, dzs_parts, dvln_parts = [], [], []
        for pair in range(N_SGU_HEADS // 2):
            cols = slice(pair * LANES, (pair + 1) * LANES)
            dsg = d_ref[:, cols]
            gate, gate_grad = _silu_and_grad(zs_ref[:, cols])
            up = u[:, cols]
            du_parts.append(dsg * mixed[pair] * gate)
            dzs_parts.append(dsg * up * mixed[pair] * gate_grad)
            dmixed = dsg * up * gate
            gbias_ref[:, cols] += dmixed
            dm_lo = jnp.where(low, dmixed, 0.0)
            dm_hi = jnp.where(low, 0.0, dmixed)
            vp = vln[:, cols]
            gw_ref[2 * pair] += _dot(dm_lo, vp, NT)
            gw_ref[2 * pair + 1] += _dot(dm_hi, vp, NT)
            dvln_parts.append(_dot(wt_ref[2 * pair], dm_lo) + _dot(wt_ref[2 * pair + 1], dm_hi))
        dvln = jnp.concatenate(dvln_parts, axis=1)
        gln_ref[0:1, :] += jnp.sum(dvln * vhat, axis=0, keepdims=True)
        gln_ref[1:2, :] += jnp.sum(dvln, axis=0, keepdims=True)
        dvhat = dvln * lng
        dvg = rstd * (dvhat - jnp.mean(dvhat, axis=-1, keepdims=True)
                      - vhat * jnp.mean(dvhat * vhat, axis=-1, keepdims=True))
        dus = jnp.concatenate(du_parts, axis=1) * _gelu_grad(us)
        dvs = dvg * _gelu_grad(vs)
        dzs = jnp.concatenate(dzs_parts, axis=1)
        for k, val in enumerate((dus, dvs, dzs)):
            dp_ref[:, k * SGU_W:(k + 1) * SGU_W] = val.astype(MXU_DTYPE)
            gbin_ref[:, k * SGU_W:(k + 1) * SGU_W] += jnp.sum(val, axis=0, keepdims=True)

        @pl.when(c == last)
        def _():
            for hh in range(N_SGU_HEADS):
                gw_ref[hh] = jnp.where(tril, gw_ref[hh], 0.0)
            head_of_lane = lax.broadcasted_iota(jnp.int32, (N_SGU_HEADS, SGU_W), 1) // HEAD_DIM
            select = (head_of_lane == lax.broadcasted_iota(jnp.int32, (N_SGU_HEADS, SGU_W), 0)).astype(F32)
            gb_ref[...] = lax.dot_general(select, gbias_ref[...], NT, precision=lax.Precision.HIGHEST,
                                          preferred_element_type=F32)

    col = lambda k: pl.BlockSpec((None, BLOCK, SGU_W), lambda n: (k, n, 0))
    return pl.pallas_call(
        body,
        name="sgu_bwd",
        grid=(N_BLOCKS,),
        in_specs=[col(1), col(1), col(2), col(3), _full((1, SGU_W)), _full((1, SGU_W)),
                  _full((N_SGU_HEADS, BLOCK, BLOCK)), _full((BLOCK, SGU_W))],
        out_specs=(pl.BlockSpec((BLOCK, SGU_SECTION), lambda n: (n, 0)),
                   _full((N_SGU_HEADS, BLOCK, BLOCK)), _full((N_SGU_HEADS, BLOCK)),
                   _full((8, SGU_W)), _full((1, SGU_SECTION))),
        out_shape=(jax.ShapeDtypeStruct((SEQ, SGU_SECTION), MXU_DTYPE),
                   jax.ShapeDtypeStruct((N_SGU_HEADS, BLOCK, BLOCK), F32),
                   jax.ShapeDtypeStruct((N_SGU_HEADS, BLOCK), F32),
                   jax.ShapeDtypeStruct((8, SGU_W), F32),
                   jax.ShapeDtypeStruct((1, SGU_SECTION), F32)),
        scratch_shapes=[pltpu.VMEM((N_SGU_HEADS, BLOCK, BLOCK), MXU_DTYPE),
                        pltpu.VMEM((BLOCK, SGU_W), F32)],
        compiler_params=_params(("arbitrary",)),
    )(dmix, gates, gates, gates, ln_g, ln_b, sgu_w, bias_full)


def _attn_bwd(sinks, dmix, q, kvx, out, gates):
    last = N_BLOCKS - 1

    def body(sink_ref, d_ref, q_ref, kc_ref, kp_ref, o_ref, za_ref, dp_ref, gsink_ref, gbin_ref,
             pend_ref, carry_ref):
        n = pl.program_id(0)

        @pl.when(n == 0)
        def _():
            gsink_ref[...] = jnp.zeros_like(gsink_ref)
            gbin_ref[...] = jnp.zeros_like(gbin_ref)
            carry_ref[...] = jnp.zeros_like(carry_ref)

        @pl.when(n > 0)
        def _():
            dp_ref[:, 0:ATTN_W] = pend_ref[:, 0:ATTN_W]
            dp_ref[:, GATE0:ATTN_SECTION] = pend_ref[:, ATTN_W:]

        @pl.when(n > last)
        def _():
            dp_ref[:, KV0:GATE0] = carry_ref[...].astype(MXU_DTYPE)

        @pl.when(n <= last)
        def _():
            valid = _window_mask(n)
            low = lax.broadcasted_iota(jnp.int32, (2 * BLOCK, LANES), 1) < HALF
            lane_row = lax.broadcasted_iota(jnp.int32, (1, LANES), 1)
            gsink = jnp.zeros((1, LANES), F32)
            chains = [(g, par) for g in range(2) for par in range(2)]
            kv = [_kv_cat(kp_ref, kc_ref, 2 * g + par, False) for g, par in chains]
            ones_keys = jnp.ones((2 * BLOCK, LANES), MXU_DTYPE)
            half_of_lane = lax.broadcasted_iota(jnp.int32, (LANES, 2 * LANES), 0) // HALF
            half_of_col = lax.broadcasted_iota(jnp.int32, (LANES, 2 * LANES), 1) // LANES
            sum_halves = (half_of_lane == half_of_col).astype(MXU_DTYPE)
            qs, douts, dzas, deltas = [], [], [], []
            for g in range(2):
                o = o_ref[2 * g:2 * g + 2].reshape(2 * BLOCK, LANES)
                cols = slice(2 * g * LANES, (2 * g + 2) * LANES)
                stack = lambda ref: jnp.concatenate([ref[:, cols][:, 0:LANES], ref[:, cols][:, LANES:]], axis=0)
                dg = stack(d_ref)
                gate, gate_grad = _silu_and_grad(stack(za_ref))
                qs.append(q_ref[2 * g:2 * g + 2].reshape(2 * BLOCK, LANES))
                douts.append((dg * gate).astype(MXU_DTYPE))
                dzas.append(dg * o * gate_grad)
                deltas.append(_dot(dg * gate * o, sum_halves))

            first = {}

            def issue_first(i):
                g, _ = chains[i]
                first[i] = (_dot(qs[g], kv[i][0], NT), _dot(douts[g], kv[i][1], NT))

            issue_first(0)
            issue_first(1)
            dqs, dk_parts, dv_parts = [], [], []
            for i, (g, par) in enumerate(chains):
                mine = low if par == 0 else jnp.logical_not(low)
                sink = _sink_column(sink_ref, g, par)
                delta = deltas[g][:, par * LANES:(par + 1) * LANES]
                e, m = _softmax_numerator(jnp.where(valid, first[i][0], NEG_INF), sink)
                at_sink = jnp.exp(sink - m)
                inv = 1.0 / (_dot(e, ones_keys) + at_sink)
                p = e * jnp.tile(inv, (1, 2))
                ds = (p * (first[i][1] - jnp.tile(delta, (1, 2)))).astype(MXU_DTYPE)
                p = p.astype(MXU_DTYPE)
                gs = at_sink * inv * delta
                for k, h in enumerate((4 * g + par, 4 * g + 2 + par)):
                    total = jnp.sum(gs[k * BLOCK:(k + 1) * BLOCK], axis=0, keepdims=True)
                    gsink = jnp.where(lane_row == h, -total, gsink)
                if i + 2 < len(chains):
                    issue_first(i + 2)
                dqs.append(_dot(ds, kv[i][0]))
                dk_parts.append(jnp.where(mine, _dot(ds, qs[g], TN), 0.0))
                dv_parts.append(jnp.where(mine, _dot(p, douts[g], TN), 0.0))
            for g in range(2):
                dq = (dqs[2 * g] + dqs[2 * g + 1]) * SCALE
                dza = dzas[g]
                for i in range(2):
                    pair = 2 * g + i
                    rows = slice(i * BLOCK, (i + 1) * BLOCK)
                    lanes = slice(pair * LANES, (pair + 1) * LANES)
                    pend_ref[:, lanes] = dq[rows].astype(MXU_DTYPE)
                    gbin_ref[:, lanes] += jnp.sum(dq[rows], axis=0, keepdims=True)
                    zl = slice(ATTN_W + pair * LANES, ATTN_W + (pair + 1) * LANES)
                    pend_ref[:, zl] = dza[rows].astype(MXU_DTYPE)
                    gl = slice(GATE0 + pair * LANES, GATE0 + (pair + 1) * LANES)
                    gbin_ref[:, gl] += jnp.sum(dza[rows], axis=0, keepdims=True)
            gsink_ref[...] += gsink
            for k, parts in enumerate((dk_parts, dv_parts)):
                both = parts[0] + parts[3] + pltpu.roll(parts[1] + parts[2], HALF, 1)
                lanes = slice(k * KV_W, (k + 1) * KV_W)
                done = carry_ref[:, lanes] + both[0:BLOCK]
                dp_ref[:, KV0 + k * KV_W:KV0 + (k + 1) * KV_W] = done.astype(MXU_DTYPE)
                carry_ref[:, lanes] = both[BLOCK:]
                gbin_ref[:, KV0 + k * KV_W:KV0 + (k + 1) * KV_W] += jnp.sum(both, axis=0, keepdims=True)

    at = lambda n: jnp.minimum(n, last)
    blk = lambda w: pl.BlockSpec((BLOCK, w), lambda n: (at(n), 0))
    tiles = pl.BlockSpec((N_PAIRS, BLOCK, LANES), lambda n: (0, at(n), 0))
    return pl.pallas_call(
        body,
        name="attn_bwd",
        grid=(N_BLOCKS + 1,),
        in_specs=[pl.BlockSpec(memory_space=pltpu.SMEM),
                  pl.BlockSpec((None, BLOCK, ATTN_W), lambda n: (0, at(n), 0)),
                  tiles,
                  blk(KVX_W),
                  pl.BlockSpec((BLOCK, KVX_W), lambda n: (jnp.maximum(at(n) - 1, 0), 0)),
                  tiles,
                  pl.BlockSpec((None, BLOCK, ATTN_W), lambda n: (0, at(n), 0))],
        out_specs=(pl.BlockSpec((BLOCK, ATTN_SECTION), lambda n: (jnp.maximum(n - 1, 0), 0)),
                   _full((1, LANES)), _full((1, ATTN_SECTION))),
        out_shape=(jax.ShapeDtypeStruct((SEQ, ATTN_SECTION), MXU_DTYPE),
                   jax.ShapeDtypeStruct((1, LANES), F32),
                   jax.ShapeDtypeStruct((1, ATTN_SECTION), F32)),
        scratch_shapes=[pltpu.VMEM((BLOCK, 2 * ATTN_W), MXU_DTYPE), pltpu.VMEM((BLOCK, 2 * KV_W), F32)],
        compiler_params=_params(("arbitrary",)),
    )(sinks, dmix, q, kvx, kvx, out, gates)


def _in_proj_bwd(dpa, dps, win_t, x, norm_g, gres, gwin):
    tm = TOKEN_TILE
    steps = SEQ // tm

    def body(da_ref, ds_ref, w_ref, x_ref, g_ref, gres_ref, gwin_ref, gx_ref, gng_ref, shard_ref,
             sa, ra, sb, rc, send_sems, recv_sems):
        step = pl.program_id(0)
        start, exchange, finish = _reduce_scatter_plan(_Copies(send_sems, recv_sems), 0, gwin_ref, WIN_ROWS,
                                                       sa, ra, sb, rc, shard_ref)

        @pl.when(step == 0)
        def _():
            gng_ref[...] = jnp.zeros_like(gng_ref)
            start()

        pl.when(step == 3)(exchange)

        dh = _dot(da_ref[...], w_ref[0:ATTN_SECTION, :]) + _dot(ds_ref[...], w_ref[ATTN_SECTION:, :])
        xv = x_ref[...]
        r = lax.rsqrt(jnp.mean(xv * xv, axis=-1, keepdims=True) + NORM_EPS)
        xn = xv * r
        gng_ref[...] += jnp.sum(dh * xn, axis=0, keepdims=True)
        dxn = dh * g_ref[...]
        gx_ref[...] = r * (dxn - xn * jnp.mean(dxn * xn, axis=-1, keepdims=True)) + gres_ref[...]

        pl.when(step == steps - 1)(finish)

    tile = lambda w: pl.BlockSpec((tm, w), lambda i: (i, 0))
    return pl.pallas_call(
        body,
        name="in_proj_bwd",
        grid=(steps,),
        in_specs=[tile(ATTN_SECTION), tile(SGU_SECTION), _full((IN_W, D_MODEL)), tile(D_MODEL),
                  _full((1, D_MODEL)), tile(D_MODEL), VMEM_SPEC],
        out_specs=(tile(D_MODEL), _full((1, D_MODEL)), VMEM_SPEC),
        out_shape=(jax.ShapeDtypeStruct((SEQ, D_MODEL), F32), jax.ShapeDtypeStruct((1, D_MODEL), F32),
                   jax.ShapeDtypeStruct((WIN_ROWS, D_MODEL), F32)),
        scratch_shapes=_reduce_scatter_scratch(WIN_ROWS, D_MODEL, COMM_DTYPE) + _dma_sems(REDUCE_SEMS),
        compiler_params=_params(("arbitrary",), VMEM_LIMIT),
    )(dpa, dps, win_t, x, norm_g, gres, gwin)


def _win_grad(dpa, dps, h, gwout, gsguw):
    rows = 256
    n_attn = ATTN_SECTION // rows
    steps = n_attn + SGU_SECTION // rows

    def body(da_ref, ds_ref, h_ref, gwout_ref, gsguw_ref, o_ref, wout_shard_ref, sguw_full_ref,
             sa_w, ra_w, sb_w, rc_w, sa_s, ra_s, sb_s, rc_s, landing, send_sems, recv_sems):
        step = pl.program_id(0)
        copies = _Copies(send_sems, recv_sems)
        own_sguw = landing.at[_block_rows(_place(), SGUW_ROWS), :]
        plans = [_reduce_scatter_plan(copies, 0, gwout_ref, WOUT_ROWS, sa_w, ra_w, sb_w, rc_w, wout_shard_ref),
                 _reduce_scatter_plan(copies, REDUCE_SEMS, gsguw_ref, SGUW_ROWS, sa_s, ra_s, sb_s, rc_s, own_sguw)]
        gather = _gather_plan(copies, 2 * REDUCE_SEMS, landing, SGUW_ROWS)

        @pl.when(step == 0)
        def _():
            for start, _, _ in plans:
                start()

        @pl.when(step == 2)
        def _():
            for _, exchange, _ in plans:
                exchange()

        @pl.when(step == 6)
        def _():
            for _, _, finish in plans:
                finish()
            gather[0]()

        pl.when(step == 8)(gather[1])

        @pl.when(step < n_attn)
        def _():
            o_ref[...] = _dot(da_ref[...], h_ref[...], TN)

        @pl.when(step >= n_attn)
        def _():
            o_ref[...] = _dot(ds_ref[...], h_ref[...], TN)

        @pl.when(step == steps - 1)
        def _():
            gather[2]()
            sguw_full_ref[...] = landing[...]

    return pl.pallas_call(
        body,
        name="win_grad",
        grid=(steps,),
        in_specs=[pl.BlockSpec((SEQ, rows), lambda i: (0, jnp.minimum(i, n_attn - 1))),
                  pl.BlockSpec((SEQ, rows), lambda i: (0, jnp.maximum(i - n_attn, 0))),
                  _full((SEQ, D_MODEL)), VMEM_SPEC, VMEM_SPEC],
        out_specs=(pl.BlockSpec((rows, D_MODEL), lambda i: (i, 0)), VMEM_SPEC,
                   _full((N_SGU_HEADS * BLOCK, BLOCK))),
        out_shape=(jax.ShapeDtypeStruct((IN_W, D_MODEL), F32),
                   jax.ShapeDtypeStruct((WOUT_ROWS, D_MODEL), F32),
                   jax.ShapeDtypeStruct((N_SGU_HEADS * BLOCK, BLOCK), F32)),
        scratch_shapes=(_reduce_scatter_scratch(WOUT_ROWS, D_MODEL, COMM_DTYPE)
                        + _reduce_scatter_scratch(SGUW_ROWS, BLOCK, F32)
                        + [pltpu.VMEM((N_SGU_HEADS * BLOCK, BLOCK), F32)]
                        + _dma_sems(2 * REDUCE_SEMS + GATHER_SEMS)),
        compiler_params=_params(("arbitrary",), VMEM_LIMIT),
    )(dpa, dps, h, gwout, gsguw)


VEC_NORM_G, VEC_B_IN, VEC_SINKS, VEC_LN_G, VEC_LN_B, VEC_B_OUT, VEC_FINAL_G, VEC_LOSS, VEC_SGU_B = 0, 1, 2, 3, 4, 5, 6, 7, 8


def _adamw(w, g, m, v):
    m = ADAM_B1 * m + (1.0 - ADAM_B1) * g
    v = ADAM_B2 * v + (1.0 - ADAM_B2) * (g * g)
    m_hat = m / (1.0 - ADAM_B1 ** ADAM_STEP)
    v_hat = v / (1.0 - ADAM_B2 ** ADAM_STEP)
    delta = -ADAM_LR * (m_hat / (jnp.sqrt(v_hat) + ADAM_EPS) + ADAM_WD * w)
    return delta, m, v


def _adamw_shard(name, g, w, m, v, block_rows):
    def body(g_ref, w_ref, m_ref, v_ref, d_ref, nm_ref, nv_ref):
        d_ref[...], nm_ref[...], nv_ref[...] = _adamw(w_ref[...], g_ref[...], m_ref[...], v_ref[...])

    rows, cols = w.shape
    spec = pl.BlockSpec((block_rows, cols), lambda i: (i, 0))
    return pl.pallas_call(
        body,
        name=name,
        grid=(rows // block_rows,),
        in_specs=[spec] * 4,
        out_specs=(spec,) * 3,
        out_shape=(jax.ShapeDtypeStruct(w.shape, F32),) * 3,
        compiler_params=_params(("arbitrary",)),
    )(g, w, m, v)


def _reduce_vectors(gng, gbin_a, gbin_s, gsink, gln, gsgub, vec4):
    def body(gng_ref, gba_ref, gbs_ref, gsink_ref, gln_ref, gsgub_ref, vec4_ref, out_ref,
             vec_ref, ra_vec, slots, send_sems, recv_sems):
        x, y, c = _place()
        copies = _Copies(send_sems, recv_sems)

        vec_ref[...] = jnp.zeros_like(vec_ref)
        vec_ref[VEC_NORM_G:VEC_NORM_G + 1, 0:D_MODEL] = gng_ref[...]
        vec_ref[VEC_B_IN:VEC_B_IN + 1, 0:ATTN_SECTION] = gba_ref[...]
        vec_ref[VEC_B_IN:VEC_B_IN + 1, ATTN_SECTION:IN_W] = gbs_ref[...]
        vec_ref[VEC_SINKS:VEC_SINKS + 1, 0:LANES] = gsink_ref[...]
        vec_ref[VEC_LN_G:VEC_LN_G + 1, 0:SGU_W] = gln_ref[0:1, :]
        vec_ref[VEC_LN_B:VEC_LN_B + 1, 0:SGU_W] = gln_ref[1:2, :]
        vec_ref[VEC_B_OUT:VEC_B_OUT + 1, 0:D_MODEL] = vec4_ref[2:3, :]
        vec_ref[VEC_FINAL_G:VEC_FINAL_G + 1, 0:D_MODEL] = vec4_ref[1:2, :]
        vec_ref[VEC_LOSS:VEC_LOSS + 1, 0:D_MODEL] = vec4_ref[0:1, :]
        vec_ref[VEC_SGU_B:VEC_SGU_B + N_SGU_HEADS, 0:BLOCK] = gsgub_ref[...]

        to_sibling = copies(0, vec_ref, ra_vec, (x, y, 1 - c))
        to_sibling.start()
        to_sibling.wait_recv()

        def chip_slot(place):
            return slots.at[pl.ds(pl.multiple_of((2 * place[0] + place[1]) * VEC_ROWS, 8), VEC_ROWS), :]

        mine = chip_slot((x, y))
        mine[...] = vec_ref[...] + ra_vec[...]
        to_chips = [copies(i, mine, mine, (*_chip(rel), c)) for i, rel in enumerate(RELATIONS[1:], start=1)]
        for cp in to_chips:
            cp.start()
        for i, rel in enumerate(RELATIONS[1:], start=1):
            theirs = chip_slot(_chip(rel))
            copies(i, theirs, theirs, (x, y, c)).wait_recv()
        out_ref[...] = ((slots[0:VEC_ROWS, :] + slots[VEC_ROWS:2 * VEC_ROWS, :])
                        + slots[2 * VEC_ROWS:3 * VEC_ROWS, :]) + slots[3 * VEC_ROWS:, :]
        to_sibling.wait_send()
        for cp in to_chips:
            cp.wait_send()

    return pl.pallas_call(
        body,
        name="reduce_vectors",
        in_specs=[VMEM_SPEC] * 7,
        out_specs=VMEM_SPEC,
        out_shape=jax.ShapeDtypeStruct((VEC_ROWS, IN_W), F32),
        scratch_shapes=[pltpu.VMEM((VEC_ROWS, IN_W), F32), pltpu.VMEM((VEC_ROWS, IN_W), F32),
                        pltpu.VMEM((4 * VEC_ROWS, IN_W), F32)] + _dma_sems(4),
    )(gng, gbin_a, gbin_s, gsink, gln, gsgub, vec4)


def _adamw_replicated(vec, gsguw, weights, m_state, v_state):
    n = len(SMALL)

    def body(*refs):
        vec_ref, gsguw_ref = refs[0], refs[1]
        w_refs, m_refs, v_refs = (refs[2 + k * n:2 + (k + 1) * n] for k in range(3))
        outs = refs[2 + 3 * n:]
        g_refs, d_refs, nm_refs, nv_refs = (outs[k * n:(k + 1) * n] for k in range(4))
        for i, (_, row, shape) in enumerate(SMALL):
            g = gsguw_ref[...] if row is None else vec_ref[row:row + shape[0], 0:shape[1]]
            g_refs[i][...] = g
            d_refs[i][...], nm_refs[i][...], nv_refs[i][...] = _adamw(
                w_refs[i][...], g, m_refs[i][...], v_refs[i][...])

    shapes = tuple(jax.ShapeDtypeStruct(shape, F32) for _, _, shape in SMALL)
    outs = pl.pallas_call(
        body,
        name="adamw_replicated",
        in_specs=[VMEM_SPEC] * (2 + 3 * n),
        out_specs=(VMEM_SPEC,) * (4 * n),
        out_shape=shapes * 4,
    )(vec, gsguw, *weights, *m_state, *v_state)
    return tuple(outs[k * n:(k + 1) * n] for k in range(4))


SMALL = (
    ("norm_g", VEC_NORM_G, (1, D_MODEL)),
    ("b_in", VEC_B_IN, (1, IN_W)),
    ("attn_sinks", VEC_SINKS, (1, N_Q_HEADS)),
    ("sgu_ln_g", VEC_LN_G, (1, SGU_W)),
    ("sgu_ln_b", VEC_LN_B, (1, SGU_W)),
    ("sgu_w", None, (N_SGU_HEADS * BLOCK, BLOCK)),
    ("sgu_b", VEC_SGU_B, (N_SGU_HEADS, BLOCK)),
    ("b_out", VEC_B_OUT, (1, D_MODEL)),
    ("final_norm_g", VEC_FINAL_G, (1, D_MODEL)),
)


def _local_grads(x, target, win_t, wout_shard, norm_g, b_in, attn_sinks, sgu_ln_g, sgu_ln_b, sgu_w, sgu_b, b_out,
                 final_g):
    sinks = attn_sinks.reshape(N_Q_HEADS)
    bias_full = jnp.repeat(sgu_b.T, HEAD_DIM, axis=1)
    h, q, kvx, gates, wout = _in_proj(x, norm_g, b_in, win_t, wout_shard)
    out, ag = _attn_fwd(sinks, q, kvx, gates)
    sg = _sgu_fwd(gates, sgu_ln_g, sgu_ln_b, sgu_w, bias_full)
    gres, dmix, gwout, vec4 = _out_proj_loss(ag, sg, x, target, wout, b_out, final_g)
    dps, gsguw, gsgub, gln, gbin_s = _sgu_bwd(dmix, gates, sgu_ln_g, sgu_ln_b, sgu_w, bias_full)
    dpa, gsink, gbin_a = _attn_bwd(sinks, dmix, q, kvx, out, gates)
    gwin, gwout_shard, gsguw_sum = _win_grad(dpa, dps, h, gwout, gsguw.reshape(N_SGU_HEADS * BLOCK, BLOCK))
    grad_x, gng, gwin_shard = _in_proj_bwd(dpa, dps, win_t, x, norm_g, gres, gwin)
    return grad_x, gwin_shard, gwout_shard, gsguw_sum, (gng, gbin_a, gbin_s, gsink, gln, gsgub, vec4)


def kernel(x, norm_g, w_in, b_in, attn_sinks, sgu_ln_g, sgu_ln_b, sgu_w, sgu_b, w_out, b_out, final_norm_g, loss_target, m_norm_g, m_w_in, m_b_in, m_attn_sinks, m_sgu_ln_g, m_sgu_ln_b, m_sgu_w, m_sgu_b, m_w_out, m_b_out, m_final_norm_g, v_norm_g, v_w_in, v_b_in, v_attn_sinks, v_sgu_ln_g, v_sgu_ln_b, v_sgu_w, v_sgu_b, v_w_out, v_b_out, v_final_norm_g):
    given = dict(norm_g=norm_g, b_in=b_in, attn_sinks=attn_sinks, sgu_ln_g=sgu_ln_g, sgu_ln_b=sgu_ln_b,
                 sgu_w=sgu_w, sgu_b=sgu_b, b_out=b_out, final_norm_g=final_norm_g)
    m_given = dict(norm_g=m_norm_g, b_in=m_b_in, attn_sinks=m_attn_sinks, sgu_ln_g=m_sgu_ln_g,
                   sgu_ln_b=m_sgu_ln_b, sgu_w=m_sgu_w, sgu_b=m_sgu_b, b_out=m_b_out, final_norm_g=m_final_norm_g)
    v_given = dict(norm_g=v_norm_g, b_in=v_b_in, attn_sinks=v_attn_sinks, sgu_ln_g=v_sgu_ln_g,
                   sgu_ln_b=v_sgu_ln_b, sgu_w=v_sgu_w, sgu_b=v_sgu_b, b_out=v_b_out, final_norm_g=v_final_norm_g)

    win_t = _all_gather_win(w_in[0].T)
    grad_x, gwin_t, gwout, gsguw, vec_parts = _local_grads(
        x[0], loss_target[0], win_t, w_out[0], norm_g, b_in, attn_sinks, sgu_ln_g, sgu_ln_b, sgu_w[0], sgu_b[0],
        b_out, final_norm_g.reshape(1, D_MODEL))

    t = lambda a: a[0].T
    d_win, nm_win, nv_win = _adamw_shard("adamw_w_in", gwin_t, t(w_in), t(m_w_in), t(v_w_in), WIN_ROWS // 2)
    d_wout, nm_wout, nv_wout = _adamw_shard("adamw_w_out", gwout, w_out[0], m_w_out[0], v_w_out[0], WOUT_ROWS)
    as_2d = lambda d: [d[name].reshape(shape) for name, _, shape in SMALL]
    vec = _reduce_vectors(*vec_parts)
    loss = vec[VEC_LOSS, 0]
    small = _adamw_replicated(vec, gsguw, as_2d(given), as_2d(m_given), as_2d(v_given))

    def assemble(big_in, big_out, k):
        vals = {name: small[k][i].reshape(given[name].shape) for i, (name, _, _) in enumerate(SMALL)}
        vals["w_in"] = big_in.T[None]
        vals["w_out"] = big_out[None]
        order = ("norm_g", "w_in", "b_in", "attn_sinks", "sgu_ln_g", "sgu_ln_b", "sgu_w", "sgu_b", "w_out",
                 "b_out", "final_norm_g")
        return [vals[name] for name in order]

    return (loss, grad_x[None],
            *assemble(gwin_t, gwout, 0), *assemble(d_win, d_wout, 1),
            *assemble(nm_win, nm_wout, 2), *assemble(nv_win, nv_wout, 3))
```

```python
import functools
import math

import jax
import jax.numpy as jnp
from jax import lax
from jax.experimental import pallas as pl
from jax.experimental.pallas import tpu as pltpu

F32 = jnp.float32
BF16 = jnp.bfloat16
MXU_DTYPE = BF16
COMM_DTYPE = BF16

D_MODEL = 1024
SEQ = 4096
HEAD_DIM = 64
N_Q_HEADS = 8
Q_PER_KV = 4
BLOCK = 128
N_BLOCKS = SEQ // BLOCK
ATTN_W = 512
KV_W = 128
SGU_W = 512
N_SGU_HEADS = 8
IN_W = 2816
NORM_EPS = 1e-5
NEG_INF = -1e30
SCALE = HEAD_DIM ** -0.5
KV0 = ATTN_W
GATE0 = ATTN_W + 2 * KV_W
SGU0 = GATE0 + ATTN_W
ATTN_SECTION = SGU0
SGU_SECTION = IN_W - SGU0

ADAM_LR = 0.001
ADAM_B1 = 0.9
ADAM_B2 = 0.999
ADAM_EPS = 1e-08
ADAM_WD = 0.01
ADAM_STEP = 10

N_DEV = 8
WIN_ROWS = IN_W // N_DEV
WOUT_ROWS = D_MODEL // N_DEV
SGUW_ROWS = N_SGU_HEADS * BLOCK // N_DEV
VEC_ROWS = 16
MESH = pl.DeviceIdType.MESH

LANES = 128
HALF = LANES // 2
N_PAIRS = N_Q_HEADS * HEAD_DIM // LANES
KVX_W = 12 * LANES
TOKEN_TILE = 256
SGU_CHUNKS_PER_STEP = 4
VMEM_LIMIT = 56 * 1024 * 1024

NN = (((1,), (0,)), ((), ()))
NT = (((1,), (1,)), ((), ()))
TN = (((0,), (0,)), ((), ()))


def _dot(a, b, dims=NN):
    return lax.dot_general(a.astype(MXU_DTYPE), b.astype(MXU_DTYPE), dims, preferred_element_type=F32)


def _gelu(x):
    return x * (lax.erf(x * (1.0 / math.sqrt(2.0))) + 1.0) * 0.5


def _gelu_grad(x):
    cdf = (lax.erf(x * (1.0 / math.sqrt(2.0))) + 1.0) * 0.5
    return cdf + x * jnp.exp(-0.5 * x * x) * (1.0 / math.sqrt(2.0 * math.pi))


def _silu_and_grad(z):
    s = jax.nn.sigmoid(z)
    return z * s, s * (1.0 + z * (1.0 - s))


def _params(semantics=None, vmem=None):
    kw = {}
    if semantics is not None:
        kw["dimension_semantics"] = semantics
    if vmem is not None:
        kw["vmem_limit_bytes"] = vmem
    return pltpu.CompilerParams(**kw)


def _full(shape):
    return pl.BlockSpec(shape, lambda *_: (0,) * len(shape))


VMEM_SPEC = pl.BlockSpec(memory_space=pltpu.VMEM)


RELATIONS = ((0, 0), (1, 0), (0, 1), (1, 1))


def _place():
    return lax.axis_index("x"), lax.axis_index("y"), lax.axis_index("c")


def _chip(rel):
    x, y, _ = _place()
    return (1 - x if rel[0] else x, 1 - y if rel[1] else y)


def _block_rows(place, n_rows):
    px, py, pc = place
    return pl.ds(pl.multiple_of((4 * px + 2 * py + pc) * n_rows, 16), n_rows)


class _Copies:
    def __init__(self, send_sems, recv_sems):
        self.send_sems, self.recv_sems = send_sems, recv_sems

    def __call__(self, k, src, dst, to):
        return pltpu.make_async_remote_copy(src_ref=src, dst_ref=dst, send_sem=self.send_sems.at[k],
                                            recv_sem=self.recv_sems.at[k], device_id=to, device_id_type=MESH)


def _gather_plan(copies, sem0, full_ref, n_rows):
    x, y, c = _place()
    me, sibling = (x, y, c), (x, y, 1 - c)
    chips = [_chip(rel) for rel in RELATIONS[1:]]

    def cp(k, block, to):
        rows = full_ref.at[_block_rows(block, n_rows), :]
        return copies(sem0 + k, rows, rows, to)

    first = [cp(0, me, sibling)] + [cp(1 + j, me, (*chip, c)) for j, chip in enumerate(chips)]
    passed = [cp(4 + j, (*chip, c), sibling) for j, chip in enumerate(chips)]

    def start():
        for f in first:
            f.start()

    def forward():
        for j, chip in enumerate(chips):
            cp(1 + j, (*chip, c), me).wait_recv()
            passed[j].start()

    def finish():
        cp(0, sibling, me).wait_recv()
        for j, chip in enumerate(chips):
            cp(4 + j, (*chip, 1 - c), me).wait_recv()
        for f in first + passed:
            f.wait_send()

    return start, forward, finish


GATHER_SEMS = 7


def _reduce_scatter_plan(copies, sem0, part_ref, n_rows, sa, ra, sb, rc, res_ref):
    x, y, c = _place()
    sibling = (x, y, 1 - c)
    n = n_rows
    level1 = copies(sem0, sa, ra, sibling)

    def level2(i):
        slot = pl.ds((i - 1) * n, n)
        return copies(sem0 + i, sb.at[slot, :], rc.at[slot, :], (*_chip(RELATIONS[i]), c))

    def start():
        for i, rel in enumerate(RELATIONS):
            sa[i * n:(i + 1) * n, :] = part_ref[_block_rows((*_chip(rel), 1 - c), n), :].astype(sa.dtype)
        level1.start()

    def exchange():
        level1.wait_recv()
        for i, rel in enumerate(RELATIONS):
            total = part_ref[_block_rows((*_chip(rel), c), n), :] + ra[i * n:(i + 1) * n, :].astype(F32)
            if i == 0:
                res_ref[...] = total
            else:
                sb[(i - 1) * n:i * n, :] = total.astype(sb.dtype)
                level2(i).start()

    def finish():
        acc = res_ref[...]
        for i in range(1, len(RELATIONS)):
            level2(i).wait_recv()
            acc = acc + rc[(i - 1) * n:i * n, :].astype(F32)
        res_ref[...] = acc
        level1.wait_send()
        for i in range(1, len(RELATIONS)):
            level2(i).wait_send()

    return start, exchange, finish


REDUCE_SEMS = 4


def _reduce_scatter_scratch(n_rows, width, dtype):
    return [pltpu.VMEM((4 * n_rows, width), dtype), pltpu.VMEM((4 * n_rows, width), dtype),
            pltpu.VMEM((3 * n_rows, width), dtype), pltpu.VMEM((3 * n_rows, width), dtype)]


def _dma_sems(n):
    return [pltpu.SemaphoreType.DMA((n,)), pltpu.SemaphoreType.DMA((n,))]


def _all_gather_win(win_t_shard):
    def body(win_ref, full_ref, send_sems, recv_sems):
        full_ref[_block_rows(_place(), WIN_ROWS), :] = win_ref[...].astype(COMM_DTYPE)
        start, forward, finish = _gather_plan(_Copies(send_sems, recv_sems), 0, full_ref, WIN_ROWS)
        start()
        forward()
        finish()

    return pl.pallas_call(
        body,
        name="all_gather_win",
        out_shape=jax.ShapeDtypeStruct((IN_W, D_MODEL), COMM_DTYPE),
        in_specs=[VMEM_SPEC],
        out_specs=VMEM_SPEC,
        scratch_shapes=_dma_sems(GATHER_SEMS),
        compiler_params=_params(vmem=VMEM_LIMIT),
    )(win_t_shard)


def _in_proj(x, norm_g, b_in, win_t, wout_shard):
    tm = TOKEN_TILE
    steps = SEQ // tm

    def body(x_ref, g_ref, b_ref, w_ref, wout_ref, h_ref, q_ref, kvx_ref, gate_ref, wfull_ref,
             landing, send_sems, recv_sems):
        step = pl.program_id(0)
        start, forward, finish = _gather_plan(_Copies(send_sems, recv_sems), 0, landing, WOUT_ROWS)

        @pl.when(step == 0)
        def _():
            landing[_block_rows(_place(), WOUT_ROWS), :] = wout_ref[...].astype(COMM_DTYPE)
            start()

        pl.when(step == steps // 2)(forward)

        xv = x_ref[...]
        r = lax.rsqrt(jnp.mean(xv * xv, axis=-1, keepdims=True) + NORM_EPS)
        h = ((xv * r) * g_ref[...]).astype(MXU_DTYPE)
        h_ref[...] = h

        def proj(lo, hi):
            return _dot(h, w_ref[lo:hi, :], NT) + b_ref[:, lo:hi]

        qs = proj(0, ATTN_W) * SCALE
        for pair in range(N_PAIRS):
            q_ref[pair] = qs[:, pair * LANES:(pair + 1) * LANES].astype(MXU_DTYPE)
        kv = proj(KV0, GATE0)
        low = lax.broadcasted_iota(jnp.int32, (tm, LANES), 1) < HALF
        for i in range(2):
            t = kv[:, i * LANES:(i + 1) * LANES]
            rot = pltpu.roll(t, HALF, 1)
            variants = (jnp.where(low, t, 0.0), jnp.where(low, 0.0, rot),
                        jnp.where(low, rot, 0.0), jnp.where(low, 0.0, t))
            for j, val in enumerate(variants):
                col = (4 * i + j) * LANES
                kvx_ref[:, col:col + LANES] = val.astype(MXU_DTYPE)
                if i == 1:
                    ones_elsewhere = jnp.where(low == (j % 2 == 0), val, 1.0)
                    kvx_ref[:, col + 4 * LANES:col + 5 * LANES] = ones_elsewhere.astype(MXU_DTYPE)
        for k in range(4):
            gate_ref[k] = proj(GATE0 + k * SGU_W, GATE0 + (k + 1) * SGU_W)

        @pl.when(step == steps - 1)
        def _():
            finish()
            wfull_ref[...] = landing[...]

    return pl.pallas_call(
        body,
        name="in_proj",
        grid=(steps,),
        in_specs=[pl.BlockSpec((tm, D_MODEL), lambda i: (i, 0)),
                  _full((1, D_MODEL)), _full((1, IN_W)), _full((IN_W, D_MODEL)), VMEM_SPEC],
        out_specs=(pl.BlockSpec((tm, D_MODEL), lambda i: (i, 0)),
                   pl.BlockSpec((N_PAIRS, tm, LANES), lambda i: (0, i, 0)),
                   pl.BlockSpec((tm, KVX_W), lambda i: (i, 0)),
                   pl.BlockSpec((4, tm, SGU_W), lambda i: (0, i, 0)),
                   _full((D_MODEL, D_MODEL))),
        out_shape=(jax.ShapeDtypeStruct((SEQ, D_MODEL), MXU_DTYPE),
                   jax.ShapeDtypeStruct((N_PAIRS, SEQ, LANES), MXU_DTYPE),
                   jax.ShapeDtypeStruct((SEQ, KVX_W), MXU_DTYPE),
                   jax.ShapeDtypeStruct((4, SEQ, SGU_W), F32),
                   jax.ShapeDtypeStruct((D_MODEL, D_MODEL), COMM_DTYPE)),
        scratch_shapes=[pltpu.VMEM((D_MODEL, D_MODEL), COMM_DTYPE)] + _dma_sems(GATHER_SEMS),
        compiler_params=_params(("arbitrary",), VMEM_LIMIT),
    )(x, norm_g, b_in, win_t, wout_shard)


def _window_mask(n):
    qi = lax.broadcasted_iota(jnp.int32, (2 * BLOCK, 2 * BLOCK), 0) & (BLOCK - 1)
    p = lax.broadcasted_iota(jnp.int32, (2 * BLOCK, 2 * BLOCK), 1) - BLOCK
    in_window = jnp.logical_and(p <= qi, p > qi - BLOCK)
    return jnp.logical_and(in_window, jnp.logical_or(p >= 0, n > 0))


def _sink_column(sink_ref, g, par):
    return jnp.concatenate([jnp.full((BLOCK, 1), sink_ref[4 * g + par], F32),
                            jnp.full((BLOCK, 1), sink_ref[4 * g + 2 + par], F32)], axis=0)


def _kv_cat(kp_ref, kc_ref, var, with_ones):
    kcol, vcol = var * LANES, (var + (8 if with_ones else 4)) * LANES
    return (jnp.concatenate([kp_ref[:, kcol:kcol + LANES], kc_ref[:, kcol:kcol + LANES]], axis=0),
            jnp.concatenate([kp_ref[:, vcol:vcol + LANES], kc_ref[:, vcol:vcol + LANES]], axis=0))


def _softmax_numerator(s, sink):
    m = jnp.maximum(jnp.max(s, axis=1, keepdims=True), sink)
    return jnp.exp(s - m), m


def _attn_fwd(sinks, q, kvx, gates):
    def body(sink_ref, q_ref, kc_ref, kp_ref, za_ref, out_ref, ag_ref):
        valid = _window_mask(pl.program_id(0))
        chains = [(g, par) for g in range(2) for par in range(2)]
        kv = [_kv_cat(kp_ref, kc_ref, 2 * g + par, True) for g, par in chains]
        scores, outs = {}, []

        def issue_scores(i):
            g, _ = chains[i]
            scores[i] = _dot(q_ref[2 * g:2 * g + 2].reshape(2 * BLOCK, LANES), kv[i][0], NT)

        issue_scores(0)
        issue_scores(1)
        low = lax.broadcasted_iota(jnp.int32, (2 * BLOCK, LANES), 1) < HALF
        for i, (g, par) in enumerate(chains):
            sink = _sink_column(sink_ref, g, par)
            e, m = _softmax_numerator(jnp.where(valid, scores[i], NEG_INF), sink)
            if i + 2 < len(chains):
                issue_scores(i + 2)
            o = _dot(e, kv[i][1])
            outs.append(o / (pltpu.roll(o, HALF, 1) + jnp.exp(sink - m)))
        for g in range(2):
            acc = jnp.where(low, outs[2 * g], outs[2 * g + 1])
            for i in range(2):
                pair = 2 * g + i
                o = acc[i * BLOCK:(i + 1) * BLOCK]
                out_ref[pair] = o
                gate, _ = _silu_and_grad(za_ref[:, pair * LANES:(pair + 1) * LANES])
                ag_ref[:, pair * LANES:(pair + 1) * LANES] = (o * gate).astype(MXU_DTYPE)

    blk = lambda w: pl.BlockSpec((BLOCK, w), lambda n: (n, 0))
    tiles = pl.BlockSpec((N_PAIRS, BLOCK, LANES), lambda n: (0, n, 0))
    return pl.pallas_call(
        body,
        name="attn_fwd",
        grid=(N_BLOCKS,),
        in_specs=[pl.BlockSpec(memory_space=pltpu.SMEM), tiles, blk(KVX_W),
                  pl.BlockSpec((BLOCK, KVX_W), lambda n: (jnp.maximum(n - 1, 0), 0)),
                  pl.BlockSpec((None, BLOCK, ATTN_W), lambda n: (0, n, 0))],
        out_specs=(tiles, blk(ATTN_W)),
        out_shape=(jax.ShapeDtypeStruct((N_PAIRS, SEQ, LANES), F32),
                   jax.ShapeDtypeStruct((SEQ, ATTN_W), MXU_DTYPE)),
        compiler_params=_params(("arbitrary",)),
    )(sinks, q, kvx, kvx, gates)


def _sgu_forward_chunk(us, vs, lng, lnb, w_ref, bias_ref):
    u = _gelu(us)
    vg = _gelu(vs)
    mu = jnp.mean(vg, axis=-1, keepdims=True)
    xc = vg - mu
    rstd = lax.rsqrt(jnp.mean(xc * xc, axis=-1, keepdims=True) + NORM_EPS)
    vhat = xc * rstd
    vln = vhat * lng + lnb
    low = lax.broadcasted_iota(jnp.int32, (BLOCK, LANES), 1) < HALF
    tril = (lax.broadcasted_iota(jnp.int32, (BLOCK, BLOCK), 0)
            >= lax.broadcasted_iota(jnp.int32, (BLOCK, BLOCK), 1))
    mixed = []
    for pair in range(N_SGU_HEADS // 2):
        vp = vln[:, pair * LANES:(pair + 1) * LANES]
        w0 = jnp.where(tril, w_ref[2 * pair], 0.0)
        w1 = jnp.where(tril, w_ref[2 * pair + 1], 0.0)
        mixed.append(_dot(w0, jnp.where(low, vp, 0.0)) + _dot(w1, jnp.where(low, 0.0, vp))
                     + bias_ref[:, pair * LANES:(pair + 1) * LANES])
    return u, vhat, rstd, vln, mixed


def _sgu_fwd(gates, ln_g, ln_b, sgu_w, bias_full):
    rows = SGU_CHUNKS_PER_STEP * BLOCK

    def body(us_ref, vs_ref, zs_ref, lng_ref, lnb_ref, w_ref, bias_ref, sg_ref):
        def chunk(c, carry):
            at = pl.ds(pl.multiple_of(c * BLOCK, BLOCK), BLOCK)
            u, _, _, _, mixed = _sgu_forward_chunk(us_ref[at, :], vs_ref[at, :], lng_ref[...], lnb_ref[...],
                                                   w_ref, bias_ref)
            for pair in range(N_SGU_HEADS // 2):
                cols = slice(pair * LANES, (pair + 1) * LANES)
                gate, _ = _silu_and_grad(zs_ref[at, cols])
                sg_ref[at, cols] = (u[:, cols] * mixed[pair] * gate).astype(MXU_DTYPE)
            return carry

        lax.fori_loop(0, SGU_CHUNKS_PER_STEP, chunk, 0)

    col = lambda k: pl.BlockSpec((None, rows, SGU_W), lambda n: (k, n, 0))
    return pl.pallas_call(
        body,
        name="sgu_fwd",
        grid=(SEQ // rows,),
        in_specs=[col(1), col(2), col(3), _full((1, SGU_W)), _full((1, SGU_W)),
                  _full((N_SGU_HEADS, BLOCK, BLOCK)), _full((BLOCK, SGU_W))],
        out_specs=pl.BlockSpec((rows, SGU_W), lambda n: (n, 0)),
        out_shape=jax.ShapeDtypeStruct((SEQ, SGU_W), MXU_DTYPE),
        compiler_params=_params(("arbitrary",)),
    )(gates, gates, gates, ln_g, ln_b, sgu_w, bias_full)


def _out_proj_loss(ag, sg, x, target, wout, b_out, final_g):
    tm = TOKEN_TILE

    def body(ag_ref, sg_ref, x_ref, t_ref, w_ref, b_ref, gf_ref, gres_ref, dmix_ref, gw_ref, vec_ref):
        @pl.when(pl.program_id(0) == 0)
        def _():
            gw_ref[...] = jnp.zeros_like(gw_ref)
            vec_ref[...] = jnp.zeros_like(vec_ref)

        a = ag_ref[...]
        s = sg_ref[...]
        xo = x_ref[...] + (_dot(a, w_ref[0:ATTN_W, :]) + _dot(s, w_ref[ATTN_W:, :])) + b_ref[...]
        r = lax.rsqrt(jnp.mean(xo * xo, axis=-1, keepdims=True) + NORM_EPS)
        xn = xo * r
        gf = gf_ref[...]
        err = xn * gf - t_ref[...]
        loss = 0.5 * jnp.sum(jnp.mean(err * err, axis=-1, keepdims=True), axis=0, keepdims=True)
        dy = err * (1.0 / D_MODEL)
        dxn = dy * gf
        gres = r * (dxn - xn * jnp.mean(dxn * xn, axis=-1, keepdims=True))
        vec_ref[0:1, :] += jnp.broadcast_to(loss, (1, D_MODEL))
        vec_ref[1:2, :] += jnp.sum(dy * xn, axis=0, keepdims=True)
        vec_ref[2:3, :] += jnp.sum(gres, axis=0, keepdims=True)
        gres_ref[...] = gres
        gb = gres.astype(MXU_DTYPE)
        dmix_ref[0] = _dot(gb, w_ref[0:ATTN_W, :], NT)
        dmix_ref[1] = _dot(gb, w_ref[ATTN_W:, :], NT)
        gw_ref[0:ATTN_W, :] += _dot(a, gb, TN)
        gw_ref[ATTN_W:, :] += _dot(s, gb, TN)

    tile = lambda w: pl.BlockSpec((tm, w), lambda i: (i, 0))
    return pl.pallas_call(
        body,
        name="out_proj_loss",
        grid=(SEQ // tm,),
        in_specs=[tile(ATTN_W), tile(SGU_W), tile(D_MODEL), tile(D_MODEL),
                  _full((D_MODEL, D_MODEL)), _full((1, D_MODEL)), _full((1, D_MODEL))],
        out_specs=(tile(D_MODEL), pl.BlockSpec((2, tm, ATTN_W), lambda i: (0, i, 0)), _full((D_MODEL, D_MODEL)),
                   _full((8, D_MODEL))),
        out_shape=(jax.ShapeDtypeStruct((SEQ, D_MODEL), F32),
                   jax.ShapeDtypeStruct((2, SEQ, ATTN_W), F32),
                   jax.ShapeDtypeStruct((D_MODEL, D_MODEL), F32),
                   jax.ShapeDtypeStruct((8, D_MODEL), F32)),
        compiler_params=_params(("arbitrary",), VMEM_LIMIT),
    )(ag, sg, x, target, wout, b_out, final_g)


def _sgu_bwd(dmix, gates, ln_g, ln_b, sgu_w, bias_full):
    last = N_BLOCKS - 1

    def body(d_ref, us_ref, vs_ref, zs_ref, lng_ref, lnb_ref, w_ref, bias_ref,
             dp_ref, gw_ref, gb_ref, gln_ref, gbin_ref, wt_ref, gbias_ref):
        c = pl.program_id(0)
        tril = (lax.broadcasted_iota(jnp.int32, (BLOCK, BLOCK), 0)
                >= lax.broadcasted_iota(jnp.int32, (BLOCK, BLOCK), 1))

        @pl.when(c == 0)
        def _():
            gw_ref[...] = jnp.zeros_like(gw_ref)
            gln_ref[...] = jnp.zeros_like(gln_ref)
            gbin_ref[...] = jnp.zeros_like(gbin_ref)
            gbias_ref[...] = jnp.zeros_like(gbias_ref)
            for hh in range(N_SGU_HEADS):
                wt_ref[hh] = jnp.where(tril, w_ref[hh], 0.0).T.astype(MXU_DTYPE)

        us = us_ref[...]
        vs = vs_ref[...]
        lng = lng_ref[...]
        u, vhat, rstd, vln, mixed = _sgu_forward_chunk(us, vs, lng, lnb_ref[...], w_ref, bias_ref)
        low = lax.broadcasted_iota(jnp.int32, (BLOCK, LANES), 1) < HALF
        du_parts, dzs_parts, dvln_parts = [], [], []
        for pair in range(N_SGU_HEADS // 2):
            cols = slice(pair * LANES, (pair + 1) * LANES)
            dsg = d_ref[:, cols]
            gate, gate_grad = _silu_and_grad(zs_ref[:, cols])
            up = u[:, cols]
            du_parts.append(dsg * mixed[pair] * gate)
            dzs_parts.append(dsg * up * mixed[pair] * gate_grad)
            dmixed = dsg * up * gate
            gbias_ref[:, cols] += dmixed
            dm_lo = jnp.where(low, dmixed, 0.0)
            dm_hi = jnp.where(low, 0.0, dmixed)
            vp = vln[:, cols]
            gw_ref[2 * pair] += _dot(dm_lo, vp, NT)
            gw_ref[2 * pair + 1] += _dot(dm_hi, vp, NT)
            dvln_parts.append(_dot(wt_ref[2 * pair], dm_lo) + _dot(wt_ref[2 * pair + 1], dm_hi))
        dvln = jnp.concatenate(dvln_parts, axis=1)
        gln_ref[0:1, :] += jnp.sum(dvln * vhat, axis=0, keepdims=True)
        gln_ref[1:2, :] += jnp.sum(dvln, axis=0, keepdims=True)
        dvhat = dvln * lng
        dvg = rstd * (dvhat - jnp.mean(dvhat, axis=-1, keepdims=True)
                      - vhat * jnp.mean(dvhat * vhat, axis=-1, keepdims=True))
        dus = jnp.concatenate(du_parts, axis=1) * _gelu_grad(us)
        dvs = dvg * _gelu_grad(vs)
        dzs = jnp.concatenate(dzs_parts, axis=1)
        for k, val in enumerate((dus, dvs, dzs)):
            dp_ref[:, k * SGU_W:(k + 1) * SGU_W] = val.astype(MXU_DTYPE)
            gbin_ref[:, k * SGU_W:(k + 1) * SGU_W] += jnp.sum(val, axis=0, keepdims=True)

        @pl.when(c == last)
        def _():
            for hh in range(N_SGU_HEADS):
                gw_ref[hh] = jnp.where(tril, gw_ref[hh], 0.0)
            head_of_lane = lax.broadcasted_iota(jnp.int32, (N_SGU_HEADS, SGU_W), 1) // HEAD_DIM
            select = (head_of_lane == lax.broadcasted_iota(jnp.int32, (N_SGU_HEADS, SGU_W), 0)).astype(F32)
            gb_ref[...] = lax.dot_general(select, gbias_ref[...], NT, precision=lax.Precision.HIGHEST,
                                          preferred_element_type=F32)

    col = lambda k: pl.BlockSpec((None, BLOCK, SGU_W), lambda n: (k, n, 0))
    return pl.pallas_call(
        body,
        name="sgu_bwd",
        grid=(N_BLOCKS,),
        in_specs=[col(1), col(1), col(2), col(3), _full((1, SGU_W)), _full((1, SGU_W)),
                  _full((N_SGU_HEADS, BLOCK, BLOCK)), _full((BLOCK, SGU_W))],
        out_specs=(pl.BlockSpec((BLOCK, SGU_SECTION), lambda n: (n, 0)),
                   _full((N_SGU_HEADS, BLOCK, BLOCK)), _full((N_SGU_HEADS, BLOCK)),
                   _full((8, SGU_W)), _full((1, SGU_SECTION))),
        out_shape=(jax.ShapeDtypeStruct((SEQ, SGU_SECTION), MXU_DTYPE),
                   jax.ShapeDtypeStruct((N_SGU_HEADS, BLOCK, BLOCK), F32),
                   jax.ShapeDtypeStruct((N_SGU_HEADS, BLOCK), F32),
                   jax.ShapeDtypeStruct((8, SGU_W), F32),
                   jax.ShapeDtypeStruct((1, SGU_SECTION), F32)),
        scratch_shapes=[pltpu.VMEM((N_SGU_HEADS, BLOCK, BLOCK), MXU_DTYPE),
                        pltpu.VMEM((BLOCK, SGU_W), F32)],
        compiler_params=_params(("arbitrary",)),
    )(dmix, gates, gates, gates, ln_g, ln_b, sgu_w, bias_full)


def _attn_bwd(sinks, dmix, q, kvx, out, gates, gwout, gsguw):
    last = N_BLOCKS - 1

    def body(sink_ref, d_ref, q_ref, kc_ref, kp_ref, o_ref, za_ref, gwout_ref, gsguw_ref,
             dp_ref, gsink_ref, gbin_ref, wout_shard_ref, sguw_full_ref,
             pend_ref, carry_ref, sa_w, ra_w, sb_w, rc_w, sa_s, ra_s, sb_s, rc_s, landing, send_sems, recv_sems):
        n = pl.program_id(0)
        copies = _Copies(send_sems, recv_sems)
        own_sguw = landing.at[_block_rows(_place(), SGUW_ROWS), :]
        plans = [_reduce_scatter_plan(copies, 0, gwout_ref, WOUT_ROWS, sa_w, ra_w, sb_w, rc_w, wout_shard_ref),
                 _reduce_scatter_plan(copies, REDUCE_SEMS, gsguw_ref, SGUW_ROWS, sa_s, ra_s, sb_s, rc_s, own_sguw)]
        gather = _gather_plan(copies, 2 * REDUCE_SEMS, landing, SGUW_ROWS)

        @pl.when(n == 0)
        def _():
            gsink_ref[...] = jnp.zeros_like(gsink_ref)
            gbin_ref[...] = jnp.zeros_like(gbin_ref)
            carry_ref[...] = jnp.zeros_like(carry_ref)
            for start, _, _ in plans:
                start()

        @pl.when(n == 3)
        def _():
            for _, exchange, _ in plans:
                exchange()

        @pl.when(n == 12)
        def _():
            for _, _, finish in plans:
                finish()
            gather[0]()

        pl.when(n == 16)(gather[1])

        @pl.when(n == last + 1)
        def _():
            gather[2]()
            sguw_full_ref[...] = landing[...]

        @pl.when(n > 0)
        def _():
            dp_ref[:, 0:ATTN_W] = pend_ref[:, 0:ATTN_W]
            dp_ref[:, GATE0:ATTN_SECTION] = pend_ref[:, ATTN_W:]

        @pl.when(n > last)
        def _():
            dp_ref[:, KV0:GATE0] = carry_ref[...].astype(MXU_DTYPE)

        @pl.when(n <= last)
        def _():
            valid = _window_mask(n)
            low = lax.broadcasted_iota(jnp.int32, (2 * BLOCK, LANES), 1) < HALF
            lane_row = lax.broadcasted_iota(jnp.int32, (1, LANES), 1)
            gsink = jnp.zeros((1, LANES), F32)
            chains = [(g, par) for g in range(2) for par in range(2)]
            kv = [_kv_cat(kp_ref, kc_ref, 2 * g + par, False) for g, par in chains]
            ones_keys = jnp.ones((2 * BLOCK, LANES), MXU_DTYPE)
            half_of_lane = lax.broadcasted_iota(jnp.int32, (LANES, 2 * LANES), 0) // HALF
            half_of_col = lax.broadcasted_iota(jnp.int32, (LANES, 2 * LANES), 1) // LANES
            sum_halves = (half_of_lane == half_of_col).astype(MXU_DTYPE)
            qs, douts, dzas, deltas = [], [], [], []
            for g in range(2):
                o = o_ref[2 * g:2 * g + 2].reshape(2 * BLOCK, LANES)
                cols = slice(2 * g * LANES, (2 * g + 2) * LANES)
                stack = lambda ref: jnp.concatenate([ref[:, cols][:, 0:LANES], ref[:, cols][:, LANES:]], axis=0)
                dg = stack(d_ref)
                gate, gate_grad = _silu_and_grad(stack(za_ref))
                qs.append(q_ref[2 * g:2 * g + 2].reshape(2 * BLOCK, LANES))
                douts.append((dg * gate).astype(MXU_DTYPE))
                dzas.append(dg * o * gate_grad)
                deltas.append(_dot(dg * gate * o, sum_halves))

            first = {}

            def issue_first(i):
                g, _ = chains[i]
                first[i] = (_dot(qs[g], kv[i][0], NT), _dot(douts[g], kv[i][1], NT))

            issue_first(0)
            issue_first(1)
            dqs, dk_parts, dv_parts = [], [], []
            for i, (g, par) in enumerate(chains):
                mine = low if par == 0 else jnp.logical_not(low)
                sink = _sink_column(sink_ref, g, par)
                delta = deltas[g][:, par * LANES:(par + 1) * LANES]
                e, m = _softmax_numerator(jnp.where(valid, first[i][0], NEG_INF), sink)
                at_sink = jnp.exp(sink - m)
                inv = 1.0 / (_dot(e, ones_keys) + at_sink)
                p = e * jnp.tile(inv, (1, 2))
                ds = (p * (first[i][1] - jnp.tile(delta, (1, 2)))).astype(MXU_DTYPE)
                p = p.astype(MXU_DTYPE)
                gs = at_sink * inv * delta
                for k, h in enumerate((4 * g + par, 4 * g + 2 + par)):
                    total = jnp.sum(gs[k * BLOCK:(k + 1) * BLOCK], axis=0, keepdims=True)
                    gsink = jnp.where(lane_row == h, -total, gsink)
                if i + 2 < len(chains):
                    issue_first(i + 2)
                dqs.append(_dot(ds, kv[i][0]))
                dk_parts.append(jnp.where(mine, _dot(ds, qs[g], TN), 0.0))
                dv_parts.append(jnp.where(mine, _dot(p, douts[g], TN), 0.0))
            for g in range(2):
                dq = (dqs[2 * g] + dqs[2 * g + 1]) * SCALE
                dza = dzas[g]
                for i in range(2):
                    pair = 2 * g + i
                    rows = slice(i * BLOCK, (i + 1) * BLOCK)
                    lanes = slice(pair * LANES, (pair + 1) * LANES)
                    pend_ref[:, lanes] = dq[rows].astype(MXU_DTYPE)
                    gbin_ref[:, lanes] += jnp.sum(dq[rows], axis=0, keepdims=True)
                    zl = slice(ATTN_W + pair * LANES, ATTN_W + (pair + 1) * LANES)
                    pend_ref[:, zl] = dza[rows].astype(MXU_DTYPE)
                    gl = slice(GATE0 + pair * LANES, GATE0 + (pair + 1) * LANES)
                    gbin_ref[:, gl] += jnp.sum(dza[rows], axis=0, keepdims=True)
            gsink_ref[...] += gsink
            for k, parts in enumerate((dk_parts, dv_parts)):
                both = parts[0] + parts[3] + pltpu.roll(parts[1] + parts[2], HALF, 1)
                lanes = slice(k * KV_W, (k + 1) * KV_W)
                done = carry_ref[:, lanes] + both[0:BLOCK]
                dp_ref[:, KV0 + k * KV_W:KV0 + (k + 1) * KV_W] = done.astype(MXU_DTYPE)
                carry_ref[:, lanes] = both[BLOCK:]
                gbin_ref[:, KV0 + k * KV_W:KV0 + (k + 1) * KV_W] += jnp.sum(both, axis=0, keepdims=True)

    at = lambda n: jnp.minimum(n, last)
    blk = lambda w: pl.BlockSpec((BLOCK, w), lambda n: (at(n), 0))
    tiles = pl.BlockSpec((N_PAIRS, BLOCK, LANES), lambda n: (0, at(n), 0))
    return pl.pallas_call(
        body,
        name="attn_bwd",
        grid=(N_BLOCKS + 1,),
        in_specs=[pl.BlockSpec(memory_space=pltpu.SMEM),
                  pl.BlockSpec((None, BLOCK, ATTN_W), lambda n: (0, at(n), 0)),
                  tiles,
                  blk(KVX_W),
                  pl.BlockSpec((BLOCK, KVX_W), lambda n: (jnp.maximum(at(n) - 1, 0), 0)),
                  tiles,
                  pl.BlockSpec((None, BLOCK, ATTN_W), lambda n: (0, at(n), 0)),
                  VMEM_SPEC, VMEM_SPEC],
        out_specs=(pl.BlockSpec((BLOCK, ATTN_SECTION), lambda n: (jnp.maximum(n - 1, 0), 0)),
                   _full((1, LANES)), _full((1, ATTN_SECTION)), VMEM_SPEC, _full((N_SGU_HEADS * BLOCK, BLOCK))),
        out_shape=(jax.ShapeDtypeStruct((SEQ, ATTN_SECTION), MXU_DTYPE),
                   jax.ShapeDtypeStruct((1, LANES), F32),
                   jax.ShapeDtypeStruct((1, ATTN_SECTION), F32),
                   jax.ShapeDtypeStruct((WOUT_ROWS, D_MODEL), F32),
                   jax.ShapeDtypeStruct((N_SGU_HEADS * BLOCK, BLOCK), F32)),
        scratch_shapes=([pltpu.VMEM((BLOCK, 2 * ATTN_W), MXU_DTYPE), pltpu.VMEM((BLOCK, 2 * KV_W), F32)]
                        + _reduce_scatter_scratch(WOUT_ROWS, D_MODEL, COMM_DTYPE)
                        + _reduce_scatter_scratch(SGUW_ROWS, BLOCK, F32)
                        + [pltpu.VMEM((N_SGU_HEADS * BLOCK, BLOCK), F32)]
                        + _dma_sems(2 * REDUCE_SEMS + GATHER_SEMS)),
        compiler_params=_params(("arbitrary",), VMEM_LIMIT),
    )(sinks, dmix, q, kvx, kvx, out, gates, gwout, gsguw)


def _in_proj_bwd(dpa, dps, win_t, x, norm_g, gres, gwin, vec_parts):
    tm = TOKEN_TILE
    steps = SEQ // tm
    n_parts = len(vec_parts)

    def body(da_ref, ds_ref, w_ref, x_ref, g_ref, gres_ref, gwin_ref, *rest):
        part_refs = rest[:n_parts]
        gx_ref, shard_ref, vec_out_ref, gng_ref, sa, ra, sb, rc, vec_ref, ra_vec, slots, send_sems, recv_sems = (
            rest[n_parts:])
        step = pl.program_id(0)
        copies = _Copies(send_sems, recv_sems)
        start, exchange, finish = _reduce_scatter_plan(copies, 0, gwin_ref, WIN_ROWS, sa, ra, sb, rc, shard_ref)

        @pl.when(step == 0)
        def _():
            gng_ref[...] = jnp.zeros_like(gng_ref)
            start()

        pl.when(step == 3)(exchange)

        dh = _dot(da_ref[...], w_ref[0:ATTN_SECTION, :]) + _dot(ds_ref[...], w_ref[ATTN_SECTION:, :])
        xv = x_ref[...]
        r = lax.rsqrt(jnp.mean(xv * xv, axis=-1, keepdims=True) + NORM_EPS)
        xn = xv * r
        gng_ref[...] += jnp.sum(dh * xn, axis=0, keepdims=True)
        dxn = dh * g_ref[...]
        gx_ref[...] = r * (dxn - xn * jnp.mean(dxn * xn, axis=-1, keepdims=True)) + gres_ref[...]

        @pl.when(step == steps - 1)
        def _():
            finish()
            _all_reduce_vectors(copies, REDUCE_SEMS, gng_ref, *part_refs, vec_out_ref, vec_ref, ra_vec, slots)

    tile = lambda w: pl.BlockSpec((tm, w), lambda i: (i, 0))
    grad_x, shard, vec = pl.pallas_call(
        body,
        name="in_proj_bwd",
        grid=(steps,),
        in_specs=[tile(ATTN_SECTION), tile(SGU_SECTION), _full((IN_W, D_MODEL)), tile(D_MODEL),
                  _full((1, D_MODEL)), tile(D_MODEL), VMEM_SPEC] + [VMEM_SPEC] * n_parts,
        out_specs=(tile(D_MODEL), VMEM_SPEC, VMEM_SPEC),
        out_shape=(jax.ShapeDtypeStruct((SEQ, D_MODEL), F32),
                   jax.ShapeDtypeStruct((WIN_ROWS, D_MODEL), F32),
                   jax.ShapeDtypeStruct((VEC_ROWS, IN_W), F32)),
        scratch_shapes=([pltpu.VMEM((1, D_MODEL), F32)] + _reduce_scatter_scratch(WIN_ROWS, D_MODEL, COMM_DTYPE)
                        + _vector_scratch() + _dma_sems(REDUCE_SEMS + VECTOR_SEMS)),
        compiler_params=_params(("arbitrary",), VMEM_LIMIT),
    )(dpa, dps, win_t, x, norm_g, gres, gwin, *vec_parts)
    return grad_x, shard, vec


def _win_grad(dpa, dps, h):
    rows = 256
    n_attn = ATTN_SECTION // rows
    steps = n_attn + SGU_SECTION // rows

    def body(da_ref, ds_ref, h_ref, o_ref):
        step = pl.program_id(0)

        @pl.when(step < n_attn)
        def _():
            o_ref[...] = _dot(da_ref[...], h_ref[...], TN)

        @pl.when(step >= n_attn)
        def _():
            o_ref[...] = _dot(ds_ref[...], h_ref[...], TN)

    return pl.pallas_call(
        body,
        name="win_grad",
        grid=(steps,),
        in_specs=[pl.BlockSpec((SEQ, rows), lambda i: (0, jnp.minimum(i, n_attn - 1))),
                  pl.BlockSpec((SEQ, rows), lambda i: (0, jnp.maximum(i - n_attn, 0))),
                  _full((SEQ, D_MODEL))],
        out_specs=pl.BlockSpec((rows, D_MODEL), lambda i: (i, 0)),
        out_shape=jax.ShapeDtypeStruct((IN_W, D_MODEL), F32),
        compiler_params=_params(("arbitrary",), VMEM_LIMIT),
    )(dpa, dps, h)


VEC_NORM_G, VEC_B_IN, VEC_SINKS, VEC_LN_G, VEC_LN_B, VEC_B_OUT, VEC_FINAL_G, VEC_LOSS, VEC_SGU_B = 0, 1, 2, 3, 4, 5, 6, 7, 8


def _adamw(w, g, m, v):
    m = ADAM_B1 * m + (1.0 - ADAM_B1) * g
    v = ADAM_B2 * v + (1.0 - ADAM_B2) * (g * g)
    m_hat = m / (1.0 - ADAM_B1 ** ADAM_STEP)
    v_hat = v / (1.0 - ADAM_B2 ** ADAM_STEP)
    delta = -ADAM_LR * (m_hat / (jnp.sqrt(v_hat) + ADAM_EPS) + ADAM_WD * w)
    return delta, m, v


def _adamw_shard(name, g, w, m, v, block_rows):
    def body(g_ref, w_ref, m_ref, v_ref, d_ref, nm_ref, nv_ref):
        d_ref[...], nm_ref[...], nv_ref[...] = _adamw(w_ref[...], g_ref[...], m_ref[...], v_ref[...])

    rows, cols = w.shape
    spec = pl.BlockSpec((block_rows, cols), lambda i: (i, 0))
    return pl.pallas_call(
        body,
        name=name,
        grid=(rows // block_rows,),
        in_specs=[spec] * 4,
        out_specs=(spec,) * 3,
        out_shape=(jax.ShapeDtypeStruct(w.shape, F32),) * 3,
        compiler_params=_params(("arbitrary",)),
    )(g, w, m, v)


VECTOR_SEMS = 4


def _vector_scratch():
    return [pltpu.VMEM((VEC_ROWS, IN_W), F32), pltpu.VMEM((VEC_ROWS, IN_W), F32),
            pltpu.VMEM((4 * VEC_ROWS, IN_W), F32)]


def _all_reduce_vectors(copies, sem0, gng_ref, gba_ref, gbs_ref, gsink_ref, gln_ref, gsgub_ref, vec4_ref, out_ref,
                        vec_ref, ra_vec, slots):
    x, y, c = _place()
    vec_ref[...] = jnp.zeros_like(vec_ref)
    vec_ref[VEC_NORM_G:VEC_NORM_G + 1, 0:D_MODEL] = gng_ref[...]
    vec_ref[VEC_B_IN:VEC_B_IN + 1, 0:ATTN_SECTION] = gba_ref[...]
    vec_ref[VEC_B_IN:VEC_B_IN + 1, ATTN_SECTION:IN_W] = gbs_ref[...]
    vec_ref[VEC_SINKS:VEC_SINKS + 1, 0:LANES] = gsink_ref[...]
    vec_ref[VEC_LN_G:VEC_LN_G + 1, 0:SGU_W] = gln_ref[0:1, :]
    vec_ref[VEC_LN_B:VEC_LN_B + 1, 0:SGU_W] = gln_ref[1:2, :]
    vec_ref[VEC_B_OUT:VEC_B_OUT + 1, 0:D_MODEL] = vec4_ref[2:3, :]
    vec_ref[VEC_FINAL_G:VEC_FINAL_G + 1, 0:D_MODEL] = vec4_ref[1:2, :]
    vec_ref[VEC_LOSS:VEC_LOSS + 1, 0:D_MODEL] = vec4_ref[0:1, :]
    vec_ref[VEC_SGU_B:VEC_SGU_B + N_SGU_HEADS, 0:BLOCK] = gsgub_ref[...]

    to_sibling = copies(sem0, vec_ref, ra_vec, (x, y, 1 - c))
    to_sibling.start()
    to_sibling.wait_recv()

    def chip_slot(place):
        return slots.at[pl.ds(pl.multiple_of((2 * place[0] + place[1]) * VEC_ROWS, 8), VEC_ROWS), :]

    mine = chip_slot((x, y))
    mine[...] = vec_ref[...] + ra_vec[...]
    to_chips = [copies(sem0 + i, mine, mine, (*_chip(rel), c)) for i, rel in enumerate(RELATIONS[1:], start=1)]
    for cp in to_chips:
        cp.start()
    for i, rel in enumerate(RELATIONS[1:], start=1):
        theirs = chip_slot(_chip(rel))
        copies(sem0 + i, theirs, theirs, (x, y, c)).wait_recv()
    out_ref[...] = ((slots[0:VEC_ROWS, :] + slots[VEC_ROWS:2 * VEC_ROWS, :])
                    + slots[2 * VEC_ROWS:3 * VEC_ROWS, :]) + slots[3 * VEC_ROWS:, :]
    to_sibling.wait_send()
    for cp in to_chips:
        cp.wait_send()


def _adamw_replicated(vec, gsguw, weights, m_state, v_state):
    n = len(SMALL)

    def body(*refs):
        vec_ref, gsguw_ref = refs[0], refs[1]
        w_refs, m_refs, v_refs = (refs[2 + k * n:2 + (k + 1) * n] for k in range(3))
        outs = refs[2 + 3 * n:]
        g_refs, d_refs, nm_refs, nv_refs = (outs[k * n:(k + 1) * n] for k in range(4))
        for i, (_, row, shape) in enumerate(SMALL):
            g = gsguw_ref[...] if row is None else vec_ref[row:row + shape[0], 0:shape[1]]
            g_refs[i][...] = g
            d_refs[i][...], nm_refs[i][...], nv_refs[i][...] = _adamw(
                w_refs[i][...], g, m_refs[i][...], v_refs[i][...])

    shapes = tuple(jax.ShapeDtypeStruct(shape, F32) for _, _, shape in SMALL)
    outs = pl.pallas_call(
        body,
        name="adamw_replicated",
        in_specs=[VMEM_SPEC] * (2 + 3 * n),
        out_specs=(VMEM_SPEC,) * (4 * n),
        out_shape=shapes * 4,
    )(vec, gsguw, *weights, *m_state, *v_state)
    return tuple(outs[k * n:(k + 1) * n] for k in range(4))


SMALL = (
    ("norm_g", VEC_NORM_G, (1, D_MODEL)),
    ("b_in", VEC_B_IN, (1, IN_W)),
    ("attn_sinks", VEC_SINKS, (1, N_Q_HEADS)),
    ("sgu_ln_g", VEC_LN_G, (1, SGU_W)),
    ("sgu_ln_b", VEC_LN_B, (1, SGU_W)),
    ("sgu_w", None, (N_SGU_HEADS * BLOCK, BLOCK)),
    ("sgu_b", VEC_SGU_B, (N_SGU_HEADS, BLOCK)),
    ("b_out", VEC_B_OUT, (1, D_MODEL)),
    ("final_norm_g", VEC_FINAL_G, (1, D_MODEL)),
)


def _local_grads(x, target, win_t, wout_shard, norm_g, b_in, attn_sinks, sgu_ln_g, sgu_ln_b, sgu_w, sgu_b, b_out,
                 final_g):
    sinks = attn_sinks.reshape(N_Q_HEADS)
    bias_full = jnp.repeat(sgu_b.T, HEAD_DIM, axis=1)
    h, q, kvx, gates, wout = _in_proj(x, norm_g, b_in, win_t, wout_shard)
    out, ag = _attn_fwd(sinks, q, kvx, gates)
    sg = _sgu_fwd(gates, sgu_ln_g, sgu_ln_b, sgu_w, bias_full)
    gres, dmix, gwout, vec4 = _out_proj_loss(ag, sg, x, target, wout, b_out, final_g)
    dps, gsguw, gsgub, gln, gbin_s = _sgu_bwd(dmix, gates, sgu_ln_g, sgu_ln_b, sgu_w, bias_full)
    dpa, gsink, gbin_a, gwout_shard, gsguw_sum = _attn_bwd(
        sinks, dmix, q, kvx, out, gates, gwout, gsguw.reshape(N_SGU_HEADS * BLOCK, BLOCK))
    gwin = _win_grad(dpa, dps, h)
    grad_x, gwin_shard, vec = _in_proj_bwd(dpa, dps, win_t, x, norm_g, gres, gwin,
                                           (gbin_a, gbin_s, gsink, gln, gsgub, vec4))
    return grad_x, gwin_shard, gwout_shard, gsguw_sum, vec


def kernel(x, norm_g, w_in, b_in, attn_sinks, sgu_ln_g, sgu_ln_b, sgu_w, sgu_b, w_out, b_out, final_norm_g, loss_target, m_norm_g, m_w_in, m_b_in, m_attn_sinks, m_sgu_ln_g, m_sgu_ln_b, m_sgu_w, m_sgu_b, m_w_out, m_b_out, m_final_norm_g, v_norm_g, v_w_in, v_b_in, v_attn_sinks, v_sgu_ln_g, v_sgu_ln_b, v_sgu_w, v_sgu_b, v_w_out, v_b_out, v_final_norm_g):
    given = dict(norm_g=norm_g, b_in=b_in, attn_sinks=attn_sinks, sgu_ln_g=sgu_ln_g, sgu_ln_b=sgu_ln_b,
                 sgu_w=sgu_w, sgu_b=sgu_b, b_out=b_out, final_norm_g=final_norm_g)
    m_given = dict(norm_g=m_norm_g, b_in=m_b_in, attn_sinks=m_attn_sinks, sgu_ln_g=m_sgu_ln_g,
                   sgu_ln_b=m_sgu_ln_b, sgu_w=m_sgu_w, sgu_b=m_sgu_b, b_out=m_b_out, final_norm_g=m_final_norm_g)
    v_given = dict(norm_g=v_norm_g, b_in=v_b_in, attn_sinks=v_attn_sinks, sgu_ln_g=v_sgu_ln_g,
                   sgu_ln_b=v_sgu_ln_b, sgu_w=v_sgu_w, sgu_b=v_sgu_b, b_out=v_b_out, final_norm_g=v_final_norm_g)

    win_t = _all_gather_win(w_in[0].T)
    grad_x, gwin_t, gwout, gsguw, vec = _local_grads(
        x[0], loss_target[0], win_t, w_out[0], norm_g, b_in, attn_sinks, sgu_ln_g, sgu_ln_b, sgu_w[0], sgu_b[0],
        b_out, final_norm_g.reshape(1, D_MODEL))

    t = lambda a: a[0].T
    d_win, nm_win, nv_win = _adamw_shard("adamw_w_in", gwin_t, t(w_in), t(m_w_in), t(v_w_in), WIN_ROWS // 2)
    d_wout, nm_wout, nv_wout = _adamw_shard("adamw_w_out", gwout, w_out[0], m_w_out[0], v_w_out[0], WOUT_ROWS)
    as_2d = lambda d: [d[name].reshape(shape) for name, _, shape in SMALL]
    loss = vec[VEC_LOSS, 0]
    small = _adamw_replicated(vec, gsguw, as_2d(given), as_2d(m_given), as_2d(v_given))

    def assemble(big_in, big_out, k):
        vals = {name: small[k][i].reshape(given[name].shape) for i, (name, _, _) in enumerate(SMALL)}
        vals["w_in"] = big_in.T[None]
        vals["w_out"] = big_out[None]
        order = ("norm_g", "w_in", "b_in", "attn_sinks", "sgu_ln_g", "sgu_ln_b", "sgu_w", "sgu_b", "w_out",
                 "b_out", "final_norm_g")
        return [vals[name] for name in order]

    return (loss, grad_x[None],
            *assemble(gwin_t, gwout, 0), *assemble(d_win, d_wout, 1),
            *assemble(nm_win, nm_wout, 2), *assemble(nv_win, nv_wout, 3))
```

```python
import functools
import math

import jax
import jax.numpy as jnp
from jax import lax
from jax.experimental import pallas as pl
from jax.experimental.pallas import tpu as pltpu

F32 = jnp.float32
BF16 = jnp.bfloat16
MXU_DTYPE = BF16
COMM_DTYPE = BF16

D_MODEL = 1024
SEQ = 4096
HEAD_DIM = 64
N_Q_HEADS = 8
Q_PER_KV = 4
BLOCK = 128
N_BLOCKS = SEQ // BLOCK
ATTN_W = 512
KV_W = 128
SGU_W = 512
N_SGU_HEADS = 8
IN_W = 2816
NORM_EPS = 1e-5
NEG_INF = -1e30
SCALE = HEAD_DIM ** -0.5
KV0 = ATTN_W
GATE0 = ATTN_W + 2 * KV_W
SGU0 = GATE0 + ATTN_W
ATTN_SECTION = SGU0
SGU_SECTION = IN_W - SGU0

ADAM_LR = 0.001
ADAM_B1 = 0.9
ADAM_B2 = 0.999
ADAM_EPS = 1e-08
ADAM_WD = 0.01
ADAM_STEP = 10

N_DEV = 8
WIN_ROWS = IN_W // N_DEV
WOUT_ROWS = D_MODEL // N_DEV
SGUW_ROWS = N_SGU_HEADS * BLOCK // N_DEV
VEC_ROWS = 16
MESH = pl.DeviceIdType.MESH

LANES = 128
HALF = LANES // 2
N_PAIRS = N_Q_HEADS * HEAD_DIM // LANES
KVX_W = 12 * LANES
TOKEN_TILE = 256
SGU_CHUNKS_PER_STEP = 4
ATTN_FWD_AHEAD = 4
ATTN_BWD_AHEAD = 3
VMEM_LIMIT = 56 * 1024 * 1024

NN = (((1,), (0,)), ((), ()))
NT = (((1,), (1,)), ((), ()))
TN = (((0,), (0,)), ((), ()))


def _dot(a, b, dims=NN):
    return lax.dot_general(a.astype(MXU_DTYPE), b.astype(MXU_DTYPE), dims, preferred_element_type=F32)


def _gelu(x):
    return x * (lax.erf(x * (1.0 / math.sqrt(2.0))) + 1.0) * 0.5


def _gelu_grad(x):
    cdf = (lax.erf(x * (1.0 / math.sqrt(2.0))) + 1.0) * 0.5
    return cdf + x * jnp.exp(-0.5 * x * x) * (1.0 / math.sqrt(2.0 * math.pi))


def _silu_and_grad(z):
    s = jax.nn.sigmoid(z)
    return z * s, s * (1.0 + z * (1.0 - s))


def _params(semantics=None, vmem=None):
    kw = {}
    if semantics is not None:
        kw["dimension_semantics"] = semantics
    if vmem is not None:
        kw["vmem_limit_bytes"] = vmem
    return pltpu.CompilerParams(**kw)


def _full(shape):
    return pl.BlockSpec(shape, lambda *_: (0,) * len(shape))


VMEM_SPEC = pl.BlockSpec(memory_space=pltpu.VMEM)


RELATIONS = ((0, 0), (1, 0), (0, 1), (1, 1))


def _place():
    return lax.axis_index("x"), lax.axis_index("y"), lax.axis_index("c")


def _chip(rel):
    x, y, _ = _place()
    return (1 - x if rel[0] else x, 1 - y if rel[1] else y)


def _block_rows(place, n_rows):
    px, py, pc = place
    return pl.ds(pl.multiple_of((4 * px + 2 * py + pc) * n_rows, 16), n_rows)


class _Copies:
    def __init__(self, send_sems, recv_sems):
        self.send_sems, self.recv_sems = send_sems, recv_sems

    def __call__(self, k, src, dst, to):
        return pltpu.make_async_remote_copy(src_ref=src, dst_ref=dst, send_sem=self.send_sems.at[k],
                                            recv_sem=self.recv_sems.at[k], device_id=to, device_id_type=MESH)


def _gather_plan(copies, sem0, full_ref, n_rows):
    x, y, c = _place()
    me, sibling = (x, y, c), (x, y, 1 - c)
    chips = [_chip(rel) for rel in RELATIONS[1:]]

    def cp(k, block, to):
        rows = full_ref.at[_block_rows(block, n_rows), :]
        return copies(sem0 + k, rows, rows, to)

    first = [cp(0, me, sibling)] + [cp(1 + j, me, (*chip, c)) for j, chip in enumerate(chips)]
    passed = [cp(4 + j, (*chip, c), sibling) for j, chip in enumerate(chips)]

    def start():
        for f in first:
            f.start()

    def forward():
        for j, chip in enumerate(chips):
            cp(1 + j, (*chip, c), me).wait_recv()
            passed[j].start()

    def finish():
        cp(0, sibling, me).wait_recv()
        for j, chip in enumerate(chips):
            cp(4 + j, (*chip, 1 - c), me).wait_recv()
        for f in first + passed:
            f.wait_send()

    return start, forward, finish


GATHER_SEMS = 7


def _reduce_scatter_plan(copies, sem0, part_ref, n_rows, sa, ra, sb, rc, res_ref):
    x, y, c = _place()
    sibling = (x, y, 1 - c)
    n = n_rows
    level1 = copies(sem0, sa, ra, sibling)

    def level2(i):
        slot = pl.ds((i - 1) * n, n)
        return copies(sem0 + i, sb.at[slot, :], rc.at[slot, :], (*_chip(RELATIONS[i]), c))

    def start():
        for i, rel in enumerate(RELATIONS):
            sa[i * n:(i + 1) * n, :] = part_ref[_block_rows((*_chip(rel), 1 - c), n), :].astype(sa.dtype)
        level1.start()

    def exchange():
        level1.wait_recv()
        for i, rel in enumerate(RELATIONS):
            total = part_ref[_block_rows((*_chip(rel), c), n), :] + ra[i * n:(i + 1) * n, :].astype(F32)
            if i == 0:
                res_ref[...] = total
            else:
                sb[(i - 1) * n:i * n, :] = total.astype(sb.dtype)
                level2(i).start()

    def finish():
        acc = res_ref[...]
        for i in range(1, len(RELATIONS)):
            level2(i).wait_recv()
            acc = acc + rc[(i - 1) * n:i * n, :].astype(F32)
        res_ref[...] = acc
        level1.wait_send()
        for i in range(1, len(RELATIONS)):
            level2(i).wait_send()

    return start, exchange, finish


REDUCE_SEMS = 4


def _reduce_scatter_scratch(n_rows, width, dtype):
    return [pltpu.VMEM((4 * n_rows, width), dtype), pltpu.VMEM((4 * n_rows, width), dtype),
            pltpu.VMEM((3 * n_rows, width), dtype), pltpu.VMEM((3 * n_rows, width), dtype)]


def _dma_sems(n):
    return [pltpu.SemaphoreType.DMA((n,)), pltpu.SemaphoreType.DMA((n,))]


def _all_gather_win(win_t_shard):
    def body(win_ref, full_ref, send_sems, recv_sems):
        full_ref[_block_rows(_place(), WIN_ROWS), :] = win_ref[...].astype(COMM_DTYPE)
        start, forward, finish = _gather_plan(_Copies(send_sems, recv_sems), 0, full_ref, WIN_ROWS)
        start()
        forward()
        finish()

    return pl.pallas_call(
        body,
        name="all_gather_win",
        out_shape=jax.ShapeDtypeStruct((IN_W, D_MODEL), COMM_DTYPE),
        in_specs=[VMEM_SPEC],
        out_specs=VMEM_SPEC,
        scratch_shapes=_dma_sems(GATHER_SEMS),
        compiler_params=_params(vmem=VMEM_LIMIT),
    )(win_t_shard)


def _in_proj(x, norm_g, b_in, win_t, wout_shard):
    tm = TOKEN_TILE
    steps = SEQ // tm

    def body(x_ref, g_ref, b_ref, w_ref, wout_ref, h_ref, q_ref, kvx_ref, gate_ref, wfull_ref,
             landing, send_sems, recv_sems):
        step = pl.program_id(0)
        start, forward, finish = _gather_plan(_Copies(send_sems, recv_sems), 0, landing, WOUT_ROWS)

        @pl.when(step == 0)
        def _():
            landing[_block_rows(_place(), WOUT_ROWS), :] = wout_ref[...].astype(COMM_DTYPE)
            start()

        pl.when(step == steps // 2)(forward)

        xv = x_ref[...]
        r = lax.rsqrt(jnp.mean(xv * xv, axis=-1, keepdims=True) + NORM_EPS)
        h = ((xv * r) * g_ref[...]).astype(MXU_DTYPE)
        h_ref[...] = h

        def proj(lo, hi):
            return _dot(h, w_ref[lo:hi, :], NT) + b_ref[:, lo:hi]

        qs = proj(0, ATTN_W) * SCALE
        for pair in range(N_PAIRS):
            q_ref[pair] = qs[:, pair * LANES:(pair + 1) * LANES].astype(MXU_DTYPE)
        kv = proj(KV0, GATE0)
        low = lax.broadcasted_iota(jnp.int32, (tm, LANES), 1) < HALF
        for i in range(2):
            t = kv[:, i * LANES:(i + 1) * LANES]
            rot = pltpu.roll(t, HALF, 1)
            variants = (jnp.where(low, t, 0.0), jnp.where(low, 0.0, rot),
                        jnp.where(low, rot, 0.0), jnp.where(low, 0.0, t))
            for j, val in enumerate(variants):
                col = (4 * i + j) * LANES
                kvx_ref[:, col:col + LANES] = val.astype(MXU_DTYPE)
                if i == 1:
                    ones_elsewhere = jnp.where(low == (j % 2 == 0), val, 1.0)
                    kvx_ref[:, col + 4 * LANES:col + 5 * LANES] = ones_elsewhere.astype(MXU_DTYPE)
        for k in range(4):
            gate_ref[k] = proj(GATE0 + k * SGU_W, GATE0 + (k + 1) * SGU_W)

        @pl.when(step == steps - 1)
        def _():
            finish()
            wfull_ref[...] = landing[...]

    return pl.pallas_call(
        body,
        name="in_proj",
        grid=(steps,),
        in_specs=[pl.BlockSpec((tm, D_MODEL), lambda i: (i, 0)),
                  _full((1, D_MODEL)), _full((1, IN_W)), _full((IN_W, D_MODEL)), VMEM_SPEC],
        out_specs=(pl.BlockSpec((tm, D_MODEL), lambda i: (i, 0)),
                   pl.BlockSpec((N_PAIRS, tm, LANES), lambda i: (0, i, 0)),
                   pl.BlockSpec((tm, KVX_W), lambda i: (i, 0)),
                   pl.BlockSpec((4, tm, SGU_W), lambda i: (0, i, 0)),
                   _full((D_MODEL, D_MODEL))),
        out_shape=(jax.ShapeDtypeStruct((SEQ, D_MODEL), MXU_DTYPE),
                   jax.ShapeDtypeStruct((N_PAIRS, SEQ, LANES), MXU_DTYPE),
                   jax.ShapeDtypeStruct((SEQ, KVX_W), MXU_DTYPE),
                   jax.ShapeDtypeStruct((4, SEQ, SGU_W), F32),
                   jax.ShapeDtypeStruct((D_MODEL, D_MODEL), COMM_DTYPE)),
        scratch_shapes=[pltpu.VMEM((D_MODEL, D_MODEL), COMM_DTYPE)] + _dma_sems(GATHER_SEMS),
        compiler_params=_params(("arbitrary",), VMEM_LIMIT),
    )(x, norm_g, b_in, win_t, wout_shard)


def _window_mask(n):
    qi = lax.broadcasted_iota(jnp.int32, (2 * BLOCK, 2 * BLOCK), 0) & (BLOCK - 1)
    p = lax.broadcasted_iota(jnp.int32, (2 * BLOCK, 2 * BLOCK), 1) - BLOCK
    in_window = jnp.logical_and(p <= qi, p > qi - BLOCK)
    return jnp.logical_and(in_window, jnp.logical_or(p >= 0, n > 0))


def _sink_column(sink_ref, g, par):
    return jnp.concatenate([jnp.full((BLOCK, 1), sink_ref[4 * g + par], F32),
                            jnp.full((BLOCK, 1), sink_ref[4 * g + 2 + par], F32)], axis=0)


def _kv_cat(kp_ref, kc_ref, var, with_ones):
    kcol, vcol = var * LANES, (var + (8 if with_ones else 4)) * LANES
    return (jnp.concatenate([kp_ref[:, kcol:kcol + LANES], kc_ref[:, kcol:kcol + LANES]], axis=0),
            jnp.concatenate([kp_ref[:, vcol:vcol + LANES], kc_ref[:, vcol:vcol + LANES]], axis=0))


def _softmax_numerator(s, sink):
    m = jnp.maximum(jnp.max(s, axis=1, keepdims=True), sink)
    return jnp.exp(s - m), m


def _attn_fwd(sinks, q, kvx, gates):
    def body(sink_ref, q_ref, kc_ref, kp_ref, za_ref, out_ref, ag_ref):
        valid = _window_mask(pl.program_id(0))[0:BLOCK]
        chains = [(g, par, i) for g in range(2) for par in range(2) for i in range(2)]
        kv = {(g, par): _kv_cat(kp_ref, kc_ref, 2 * g + par, True) for g in range(2) for par in range(2)}
        scores, outs = {}, {}

        def issue_scores(k):
            g, par, i = chains[k]
            scores[k] = _dot(q_ref[2 * g + i], kv[g, par][0], NT)

        ahead = ATTN_FWD_AHEAD
        for k in range(ahead):
            issue_scores(k)
        low = lax.broadcasted_iota(jnp.int32, (BLOCK, LANES), 1) < HALF
        for k, (g, par, i) in enumerate(chains):
            sink = sink_ref[4 * g + 2 * i + par]
            e, m = _softmax_numerator(jnp.where(valid, scores[k], NEG_INF), sink)
            if k + ahead < len(chains):
                issue_scores(k + ahead)
            o = _dot(e, kv[g, par][1])
            outs[g, par, i] = o / (pltpu.roll(o, HALF, 1) + jnp.exp(sink - m))
        for pair in range(N_PAIRS):
            g, i = divmod(pair, 2)
            o = jnp.where(low, outs[g, 0, i], outs[g, 1, i])
            out_ref[pair] = o
            gate, _ = _silu_and_grad(za_ref[:, pair * LANES:(pair + 1) * LANES])
            ag_ref[:, pair * LANES:(pair + 1) * LANES] = (o * gate).astype(MXU_DTYPE)

    blk = lambda w: pl.BlockSpec((BLOCK, w), lambda n: (n, 0))
    tiles = pl.BlockSpec((N_PAIRS, BLOCK, LANES), lambda n: (0, n, 0))
    return pl.pallas_call(
        body,
        name="attn_fwd",
        grid=(N_BLOCKS,),
        in_specs=[pl.BlockSpec(memory_space=pltpu.SMEM), tiles, blk(KVX_W),
                  pl.BlockSpec((BLOCK, KVX_W), lambda n: (jnp.maximum(n - 1, 0), 0)),
                  pl.BlockSpec((None, BLOCK, ATTN_W), lambda n: (0, n, 0))],
        out_specs=(tiles, blk(ATTN_W)),
        out_shape=(jax.ShapeDtypeStruct((N_PAIRS, SEQ, LANES), F32),
                   jax.ShapeDtypeStruct((SEQ, ATTN_W), MXU_DTYPE)),
        compiler_params=_params(("arbitrary",)),
    )(sinks, q, kvx, kvx, gates)


def _sgu_forward_chunk(us, vs, lng, lnb, w_ref, bias_ref):
    u = _gelu(us)
    vg = _gelu(vs)
    mu = jnp.mean(vg, axis=-1, keepdims=True)
    xc = vg - mu
    rstd = lax.rsqrt(jnp.mean(xc * xc, axis=-1, keepdims=True) + NORM_EPS)
    vhat = xc * rstd
    vln = vhat * lng + lnb
    low = lax.broadcasted_iota(jnp.int32, (BLOCK, LANES), 1) < HALF
    tril = (lax.broadcasted_iota(jnp.int32, (BLOCK, BLOCK), 0)
            >= lax.broadcasted_iota(jnp.int32, (BLOCK, BLOCK), 1))
    mixed = []
    for pair in range(N_SGU_HEADS // 2):
        vp = vln[:, pair * LANES:(pair + 1) * LANES]
        w0 = jnp.where(tril, w_ref[2 * pair], 0.0)
        w1 = jnp.where(tril, w_ref[2 * pair + 1], 0.0)
        mixed.append(_dot(w0, jnp.where(low, vp, 0.0)) + _dot(w1, jnp.where(low, 0.0, vp))
                     + bias_ref[:, pair * LANES:(pair + 1) * LANES])
    return u, vhat, rstd, vln, mixed


def _sgu_fwd(gates, ln_g, ln_b, sgu_w, bias_full):
    rows = SGU_CHUNKS_PER_STEP * BLOCK

    def body(us_ref, vs_ref, zs_ref, lng_ref, lnb_ref, w_ref, bias_ref, sg_ref):
        def chunk(c, carry):
            at = pl.ds(pl.multiple_of(c * BLOCK, BLOCK), BLOCK)
            u, _, _, _, mixed = _sgu_forward_chunk(us_ref[at, :], vs_ref[at, :], lng_ref[...], lnb_ref[...],
                                                   w_ref, bias_ref)
            for pair in range(N_SGU_HEADS // 2):
                cols = slice(pair * LANES, (pair + 1) * LANES)
                gate, _ = _silu_and_grad(zs_ref[at, cols])
                sg_ref[at, cols] = (u[:, cols] * mixed[pair] * gate).astype(MXU_DTYPE)
            return carry

        lax.fori_loop(0, SGU_CHUNKS_PER_STEP, chunk, 0)

    col = lambda k: pl.BlockSpec((None, rows, SGU_W), lambda n: (k, n, 0))
    return pl.pallas_call(
        body,
        name="sgu_fwd",
        grid=(SEQ // rows,),
        in_specs=[col(1), col(2), col(3), _full((1, SGU_W)), _full((1, SGU_W)),
                  _full((N_SGU_HEADS, BLOCK, BLOCK)), _full((BLOCK, SGU_W))],
        out_specs=pl.BlockSpec((rows, SGU_W), lambda n: (n, 0)),
        out_shape=jax.ShapeDtypeStruct((SEQ, SGU_W), MXU_DTYPE),
        compiler_params=_params(("arbitrary",)),
    )(gates, gates, gates, ln_g, ln_b, sgu_w, bias_full)


def _out_proj_loss(ag, sg, x, target, wout, b_out, final_g):
    tm = TOKEN_TILE

    def body(ag_ref, sg_ref, x_ref, t_ref, w_ref, b_ref, gf_ref, gres_ref, dmix_ref, gw_ref, vec_ref):
        @pl.when(pl.program_id(0) == 0)
        def _():
            gw_ref[...] = jnp.zeros_like(gw_ref)
            vec_ref[...] = jnp.zeros_like(vec_ref)

        a = ag_ref[...]
        s = sg_ref[...]
        xo = x_ref[...] + (_dot(a, w_ref[0:ATTN_W, :]) + _dot(s, w_ref[ATTN_W:, :])) + b_ref[...]
        r = lax.rsqrt(jnp.mean(xo * xo, axis=-1, keepdims=True) + NORM_EPS)
        xn = xo * r
        gf = gf_ref[...]
        err = xn * gf - t_ref[...]
        loss = 0.5 * jnp.sum(jnp.mean(err * err, axis=-1, keepdims=True), axis=0, keepdims=True)
        dy = err * (1.0 / D_MODEL)
        dxn = dy * gf
        gres = r * (dxn - xn * jnp.mean(dxn * xn, axis=-1, keepdims=True))
        vec_ref[0:1, :] += jnp.broadcast_to(loss, (1, D_MODEL))
        vec_ref[1:2, :] += jnp.sum(dy * xn, axis=0, keepdims=True)
        vec_ref[2:3, :] += jnp.sum(gres, axis=0, keepdims=True)
        gres_ref[...] = gres
        gb = gres.astype(MXU_DTYPE)
        dmix_ref[0] = _dot(gb, w_ref[0:ATTN_W, :], NT)
        dmix_ref[1] = _dot(gb, w_ref[ATTN_W:, :], NT)
        gw_ref[0:ATTN_W, :] += _dot(a, gb, TN)
        gw_ref[ATTN_W:, :] += _dot(s, gb, TN)

    tile = lambda w: pl.BlockSpec((tm, w), lambda i: (i, 0))
    return pl.pallas_call(
        body,
        name="out_proj_loss",
        grid=(SEQ // tm,),
        in_specs=[tile(ATTN_W), tile(SGU_W), tile(D_MODEL), tile(D_MODEL),
                  _full((D_MODEL, D_MODEL)), _full((1, D_MODEL)), _full((1, D_MODEL))],
        out_specs=(tile(D_MODEL), pl.BlockSpec((2, tm, ATTN_W), lambda i: (0, i, 0)), _full((D_MODEL, D_MODEL)),
                   _full((8, D_MODEL))),
        out_shape=(jax.ShapeDtypeStruct((SEQ, D_MODEL), F32),
                   jax.ShapeDtypeStruct((2, SEQ, ATTN_W), F32),
                   jax.ShapeDtypeStruct((D_MODEL, D_MODEL), F32),
                   jax.ShapeDtypeStruct((8, D_MODEL), F32)),
        compiler_params=_params(("arbitrary",), VMEM_LIMIT),
    )(ag, sg, x, target, wout, b_out, final_g)


def _sgu_bwd(dmix, gates, ln_g, ln_b, sgu_w, bias_full):
    last = N_BLOCKS - 1

    def body(d_ref, us_ref, vs_ref, zs_ref, lng_ref, lnb_ref, w_ref, bias_ref,
             dp_ref, gw_ref, gb_ref, gln_ref, gbin_ref, wt_ref, gbias_ref):
        c = pl.program_id(0)
        tril = (lax.broadcasted_iota(jnp.int32, (BLOCK, BLOCK), 0)
                >= lax.broadcasted_iota(jnp.int32, (BLOCK, BLOCK), 1))

        @pl.when(c == 0)
        def _():
            gw_ref[...] = jnp.zeros_like(gw_ref)
            gln_ref[...] = jnp.zeros_like(gln_ref)
            gbin_ref[...] = jnp.zeros_like(gbin_ref)
            gbias_ref[...] = jnp.zeros_like(gbias_ref)
            for hh in range(N_SGU_HEADS):
                wt_ref[hh] = jnp.where(tril, w_ref[hh], 0.0).T.astype(MXU_DTYPE)

        us = us_ref[...]
        vs = vs_ref[...]
        lng = lng_ref[...]
        u, vhat, rstd, vln, mixed = _sgu_forward_chunk(us, vs, lng, lnb_ref[...], w_ref, bias_ref)
        low = lax.broadcasted_iota(jnp.int32, (BLOCK, LANES), 1) < HALF
        du_parts, dzs_parts, dvln_parts = [], [], []
        for pair in range(N_SGU_HEADS // 2):
            cols = slice(pair * LANES, (pair + 1) * LANES)
            dsg = d_ref[:, cols]
            gate, gate_grad = _silu_and_grad(zs_ref[:, cols])
            up = u[:, cols]
            du_parts.append(dsg * mixed[pair] * gate)
            dzs_parts.append(dsg * up * mixed[pair] * gate_grad)
            dmixed = dsg * up * gate
            gbias_ref[:, cols] += dmixed
            dm_lo = jnp.where(low, dmixed, 0.0)
            dm_hi = jnp.where(low, 0.0, dmixed)
            vp = vln[:, cols]
            gw_ref[2 * pair] += _dot(dm_lo, vp, NT)
            gw_ref[2 * pair + 1] += _dot(dm_hi, vp, NT)
            dvln_parts.append(_dot(wt_ref[2 * pair], dm_lo) + _dot(wt_ref[2 * pair + 1], dm_hi))
        dvln = jnp.concatenate(dvln_parts, axis=1)
        gln_ref[0:1, :] += jnp.sum(dvln * vhat, axis=0, keepdims=True)
        gln_ref[1:2, :] += jnp.sum(dvln, axis=0, keepdims=True)
        dvhat = dvln * lng
        dvg = rstd * (dvhat - jnp.mean(dvhat, axis=-1, keepdims=True)
                      - vhat * jnp.mean(dvhat * vhat, axis=-1, keepdims=True))
        dus = jnp.concatenate(du_parts, axis=1) * _gelu_grad(us)
        dvs = dvg * _gelu_grad(vs)
        dzs = jnp.concatenate(dzs_parts, axis=1)
        for k, val in enumerate((dus, dvs, dzs)):
            dp_ref[:, k * SGU_W:(k + 1) * SGU_W] = val.astype(MXU_DTYPE)
            gbin_ref[:, k * SGU_W:(k + 1) * SGU_W] += jnp.sum(val, axis=0, keepdims=True)

        @pl.when(c == last)
        def _():
            for hh in range(N_SGU_HEADS):
                gw_ref[hh] = jnp.where(tril, gw_ref[hh], 0.0)
            head_of_lane = lax.broadcasted_iota(jnp.int32, (N_SGU_HEADS, SGU_W), 1) // HEAD_DIM
            select = (head_of_lane == lax.broadcasted_iota(jnp.int32, (N_SGU_HEADS, SGU_W), 0)).astype(F32)
            gb_ref[...] = lax.dot_general(select, gbias_ref[...], NT, precision=lax.Precision.HIGHEST,
                                          preferred_element_type=F32)

    col = lambda k: pl.BlockSpec((None, BLOCK, SGU_W), lambda n: (k, n, 0))
    return pl.pallas_call(
        body,
        name="sgu_bwd",
        grid=(N_BLOCKS,),
        in_specs=[col(1), col(1), col(2), col(3), _full((1, SGU_W)), _full((1, SGU_W)),
                  _full((N_SGU_HEADS, BLOCK, BLOCK)), _full((BLOCK, SGU_W))],
        out_specs=(pl.BlockSpec((BLOCK, SGU_SECTION), lambda n: (n, 0)),
                   _full((N_SGU_HEADS, BLOCK, BLOCK)), _full((N_SGU_HEADS, BLOCK)),
                   _full((8, SGU_W)), _full((1, SGU_SECTION))),
        out_shape=(jax.ShapeDtypeStruct((SEQ, SGU_SECTION), MXU_DTYPE),
                   jax.ShapeDtypeStruct((N_SGU_HEADS, BLOCK, BLOCK), F32),
                   jax.ShapeDtypeStruct((N_SGU_HEADS, BLOCK), F32),
                   jax.ShapeDtypeStruct((8, SGU_W), F32),
                   jax.ShapeDtypeStruct((1, SGU_SECTION), F32)),
        scratch_shapes=[pltpu.VMEM((N_SGU_HEADS, BLOCK, BLOCK), MXU_DTYPE),
                        pltpu.VMEM((BLOCK, SGU_W), F32)],
        compiler_params=_params(("arbitrary",)),
    )(dmix, gates, gates, gates, ln_g, ln_b, sgu_w, bias_full)


def _attn_bwd(sinks, dmix, q, kvx, out, gates, gwout, gsguw):
    last = N_BLOCKS - 1

    def body(sink_ref, d_ref, q_ref, kc_ref, kp_ref, o_ref, za_ref, gwout_ref, gsguw_ref,
             dp_ref, gsink_ref, gbin_ref, wout_shard_ref, sguw_full_ref,
             pend_ref, carry_ref, sa_w, ra_w, sb_w, rc_w, sa_s, ra_s, sb_s, rc_s, landing, send_sems, recv_sems):
        n = pl.program_id(0)
        copies = _Copies(send_sems, recv_sems)
        own_sguw = landing.at[_block_rows(_place(), SGUW_ROWS), :]
        plans = [_reduce_scatter_plan(copies, 0, gwout_ref, WOUT_ROWS, sa_w, ra_w, sb_w, rc_w, wout_shard_ref),
                 _reduce_scatter_plan(copies, REDUCE_SEMS, gsguw_ref, SGUW_ROWS, sa_s, ra_s, sb_s, rc_s, own_sguw)]
        gather = _gather_plan(copies, 2 * REDUCE_SEMS, landing, SGUW_ROWS)

        @pl.when(n == 0)
        def _():
            gsink_ref[...] = jnp.zeros_like(gsink_ref)
            gbin_ref[...] = jnp.zeros_like(gbin_ref)
            carry_ref[...] = jnp.zeros_like(carry_ref)
            for start, _, _ in plans:
                start()

        @pl.when(n == 3)
        def _():
            for _, exchange, _ in plans:
                exchange()

        @pl.when(n == 12)
        def _():
            for _, _, finish in plans:
                finish()
            gather[0]()

        pl.when(n == 16)(gather[1])

        @pl.when(n == last + 1)
        def _():
            gather[2]()
            sguw_full_ref[...] = landing[...]

        @pl.when(n > 0)
        def _():
            dp_ref[:, 0:ATTN_W] = pend_ref[:, 0:ATTN_W]
            dp_ref[:, GATE0:ATTN_SECTION] = pend_ref[:, ATTN_W:]

        @pl.when(n > last)
        def _():
            dp_ref[:, KV0:GATE0] = carry_ref[...].astype(MXU_DTYPE)

        @pl.when(n <= last)
        def _():
            valid = _window_mask(n)[0:BLOCK]
            low = lax.broadcasted_iota(jnp.int32, (BLOCK, LANES), 1) < HALF
            low_keys = lax.broadcasted_iota(jnp.int32, (2 * BLOCK, LANES), 1) < HALF
            lane_row = lax.broadcasted_iota(jnp.int32, (1, LANES), 1)
            gsink = jnp.zeros((1, LANES), F32)
            chains = [(g, par, i) for g in range(2) for par in range(2) for i in range(2)]
            kv = {(g, par): _kv_cat(kp_ref, kc_ref, 2 * g + par, False) for g in range(2) for par in range(2)}
            ones_keys = jnp.ones((2 * BLOCK, LANES), MXU_DTYPE)
            half_of_lane = lax.broadcasted_iota(jnp.int32, (LANES, 2 * LANES), 0) // HALF
            half_of_col = lax.broadcasted_iota(jnp.int32, (LANES, 2 * LANES), 1) // LANES
            sum_halves = (half_of_lane == half_of_col).astype(MXU_DTYPE)
            douts, deltas = [], []
            for pair in range(N_PAIRS):
                lanes = slice(pair * LANES, (pair + 1) * LANES)
                dg = d_ref[:, lanes]
                gate, gate_grad = _silu_and_grad(za_ref[:, lanes])
                o = o_ref[pair]
                dout = dg * gate
                dza = dg * o * gate_grad
                douts.append(dout.astype(MXU_DTYPE))
                deltas.append(_dot(dout * o, sum_halves))
                zl = slice(ATTN_W + pair * LANES, ATTN_W + (pair + 1) * LANES)
                pend_ref[:, zl] = dza.astype(MXU_DTYPE)
                gl = slice(GATE0 + pair * LANES, GATE0 + (pair + 1) * LANES)
                gbin_ref[:, gl] += jnp.sum(dza, axis=0, keepdims=True)

            first = {}

            def issue_first(k):
                g, par, i = chains[k]
                first[k] = (_dot(q_ref[2 * g + i], kv[g, par][0], NT), _dot(douts[2 * g + i], kv[g, par][1], NT))

            numerators = {}

            def issue_row_sums(k):
                g, par, i = chains[k]
                sink = sink_ref[4 * g + 2 * i + par]
                e, m = _softmax_numerator(jnp.where(valid, first[k][0], NEG_INF), sink)
                numerators[k] = (e, jnp.exp(sink - m), _dot(e, ones_keys))

            ahead = ATTN_BWD_AHEAD
            for k in range(ahead):
                issue_first(k)
            issue_row_sums(0)
            issue_row_sums(1)
            dqs, dk_parts, dv_parts = {}, {}, {}
            operands = {}

            def issue_last(k):
                g, par, i = chains[k]
                ds, ds_t, p_t = operands.pop(k)
                dq = _dot(ds, kv[g, par][0])
                dqs[g, i] = dq if par == 0 else dqs[g, i] + dq
                dk = _dot(ds_t, q_ref[2 * g + i])
                dv = _dot(p_t, douts[2 * g + i])
                dk_parts[g, par] = dk if i == 0 else dk_parts[g, par] + dk
                dv_parts[g, par] = dv if i == 0 else dv_parts[g, par] + dv

            for k, (g, par, i) in enumerate(chains):
                h = 4 * g + 2 * i + par
                delta = deltas[2 * g + i][:, par * LANES:(par + 1) * LANES]
                e, at_sink, row_sum = numerators[k]
                inv = 1.0 / (row_sum + at_sink)
                p = e * jnp.tile(inv, (1, 2))
                ds = p * (first[k][1] - jnp.tile(delta, (1, 2)))
                operands[k] = (ds.astype(MXU_DTYPE), ds.T.astype(MXU_DTYPE), p.T.astype(MXU_DTYPE))
                total = jnp.sum(at_sink * inv * delta, axis=0, keepdims=True)
                gsink = jnp.where(lane_row == h, -total, gsink)
                if k + ahead < len(chains):
                    issue_first(k + ahead)
                if k + 2 < len(chains):
                    issue_row_sums(k + 2)
                if k > 0:
                    issue_last(k - 1)
            issue_last(len(chains) - 1)
            for pair in range(N_PAIRS):
                g, i = divmod(pair, 2)
                dq = dqs[g, i] * SCALE
                lanes = slice(pair * LANES, (pair + 1) * LANES)
                pend_ref[:, lanes] = dq.astype(MXU_DTYPE)
                gbin_ref[:, lanes] += jnp.sum(dq, axis=0, keepdims=True)
            gsink_ref[...] += gsink
            for k, parts in enumerate((dk_parts, dv_parts)):
                masked = {key: jnp.where(low_keys if key[1] == 0 else jnp.logical_not(low_keys), val, 0.0)
                          for key, val in parts.items()}
                both = (masked[0, 0] + masked[1, 1]
                        + pltpu.roll(masked[0, 1] + masked[1, 0], HALF, 1))
                lanes = slice(k * KV_W, (k + 1) * KV_W)
                done = carry_ref[:, lanes] + both[0:BLOCK]
                dp_ref[:, KV0 + k * KV_W:KV0 + (k + 1) * KV_W] = done.astype(MXU_DTYPE)
                carry_ref[:, lanes] = both[BLOCK:]
                gbin_ref[:, KV0 + k * KV_W:KV0 + (k + 1) * KV_W] += jnp.sum(both, axis=0, keepdims=True)

    at = lambda n: jnp.minimum(n, last)
    blk = lambda w: pl.BlockSpec((BLOCK, w), lambda n: (at(n), 0))
    tiles = pl.BlockSpec((N_PAIRS, BLOCK, LANES), lambda n: (0, at(n), 0))
    return pl.pallas_call(
        body,
        name="attn_bwd",
        grid=(N_BLOCKS + 1,),
        in_specs=[pl.BlockSpec(memory_space=pltpu.SMEM),
                  pl.BlockSpec((None, BLOCK, ATTN_W), lambda n: (0, at(n), 0)),
                  tiles,
                  blk(KVX_W),
                  pl.BlockSpec((BLOCK, KVX_W), lambda n: (jnp.maximum(at(n) - 1, 0), 0)),
                  tiles,
                  pl.BlockSpec((None, BLOCK, ATTN_W), lambda n: (0, at(n), 0)),
                  VMEM_SPEC, VMEM_SPEC],
        out_specs=(pl.BlockSpec((BLOCK, ATTN_SECTION), lambda n: (jnp.maximum(n - 1, 0), 0)),
                   _full((1, LANES)), _full((1, ATTN_SECTION)), VMEM_SPEC, _full((N_SGU_HEADS * BLOCK, BLOCK))),
        out_shape=(jax.ShapeDtypeStruct((SEQ, ATTN_SECTION), MXU_DTYPE),
                   jax.ShapeDtypeStruct((1, LANES), F32),
                   jax.ShapeDtypeStruct((1, ATTN_SECTION), F32),
                   jax.ShapeDtypeStruct((WOUT_ROWS, D_MODEL), F32),
                   jax.ShapeDtypeStruct((N_SGU_HEADS * BLOCK, BLOCK), F32)),
        scratch_shapes=([pltpu.VMEM((BLOCK, 2 * ATTN_W), MXU_DTYPE), pltpu.VMEM((BLOCK, 2 * KV_W), F32)]
                        + _reduce_scatter_scratch(WOUT_ROWS, D_MODEL, COMM_DTYPE)
                        + _reduce_scatter_scratch(SGUW_ROWS, BLOCK, F32)
                        + [pltpu.VMEM((N_SGU_HEADS * BLOCK, BLOCK), F32)]
                        + _dma_sems(2 * REDUCE_SEMS + GATHER_SEMS)),
        compiler_params=_params(("arbitrary",), VMEM_LIMIT),
    )(sinks, dmix, q, kvx, kvx, out, gates, gwout, gsguw)


def _in_proj_bwd(dpa, dps, win_t, x, norm_g, gres, gwin, vec_parts):
    tm = TOKEN_TILE
    steps = SEQ // tm
    n_parts = len(vec_parts)

    def body(da_ref, ds_ref, w_ref, x_ref, g_ref, gres_ref, gwin_ref, *rest):
        part_refs = rest[:n_parts]
        gx_ref, shard_ref, vec_out_ref, gng_ref, sa, ra, sb, rc, vec_ref, ra_vec, slots, send_sems, recv_sems = (
            rest[n_parts:])
        step = pl.program_id(0)
        copies = _Copies(send_sems, recv_sems)
        start, exchange, finish = _reduce_scatter_plan(copies, 0, gwin_ref, WIN_ROWS, sa, ra, sb, rc, shard_ref)

        @pl.when(step == 0)
        def _():
            gng_ref[...] = jnp.zeros_like(gng_ref)
            start()

        pl.when(step == 3)(exchange)

        dh = _dot(da_ref[...], w_ref[0:ATTN_SECTION, :]) + _dot(ds_ref[...], w_ref[ATTN_SECTION:, :])
        xv = x_ref[...]
        r = lax.rsqrt(jnp.mean(xv * xv, axis=-1, keepdims=True) + NORM_EPS)
        xn = xv * r
        gng_ref[...] += jnp.sum(dh * xn, axis=0, keepdims=True)
        dxn = dh * g_ref[...]
        gx_ref[...] = r * (dxn - xn * jnp.mean(dxn * xn, axis=-1, keepdims=True)) + gres_ref[...]

        @pl.when(step == steps - 1)
        def _():
            finish()
            _all_reduce_vectors(copies, REDUCE_SEMS, gng_ref, *part_refs, vec_out_ref, vec_ref, ra_vec, slots)

    tile = lambda w: pl.BlockSpec((tm, w), lambda i: (i, 0))
    grad_x, shard, vec = pl.pallas_call(
        body,
        name="in_proj_bwd",
        grid=(steps,),
        in_specs=[tile(ATTN_SECTION), tile(SGU_SECTION), _full((IN_W, D_MODEL)), tile(D_MODEL),
                  _full((1, D_MODEL)), tile(D_MODEL), VMEM_SPEC] + [VMEM_SPEC] * n_parts,
        out_specs=(tile(D_MODEL), VMEM_SPEC, VMEM_SPEC),
        out_shape=(jax.ShapeDtypeStruct((SEQ, D_MODEL), F32),
                   jax.ShapeDtypeStruct((WIN_ROWS, D_MODEL), F32),
                   jax.ShapeDtypeStruct((VEC_ROWS, IN_W), F32)),
        scratch_shapes=([pltpu.VMEM((1, D_MODEL), F32)] + _reduce_scatter_scratch(WIN_ROWS, D_MODEL, COMM_DTYPE)
                        + _vector_scratch() + _dma_sems(REDUCE_SEMS + VECTOR_SEMS)),
        compiler_params=_params(("arbitrary",), VMEM_LIMIT),
    )(dpa, dps, win_t, x, norm_g, gres, gwin, *vec_parts)
    return grad_x, shard, vec


def _win_grad(dpa, dps, h):
    rows = 256
    n_attn = ATTN_SECTION // rows
    steps = n_attn + SGU_SECTION // rows

    def body(da_ref, ds_ref, h_ref, o_ref):
        step = pl.program_id(0)

        @pl.when(step < n_attn)
        def _():
            o_ref[...] = _dot(da_ref[...], h_ref[...], TN)

        @pl.when(step >= n_attn)
        def _():
            o_ref[...] = _dot(ds_ref[...], h_ref[...], TN)

    return pl.pallas_call(
        body,
        name="win_grad",
        grid=(steps,),
        in_specs=[pl.BlockSpec((SEQ, rows), lambda i: (0, jnp.minimum(i, n_attn - 1))),
                  pl.BlockSpec((SEQ, rows), lambda i: (0, jnp.maximum(i - n_attn, 0))),
                  _full((SEQ, D_MODEL))],
        out_specs=pl.BlockSpec((rows, D_MODEL), lambda i: (i, 0)),
        out_shape=jax.ShapeDtypeStruct((IN_W, D_MODEL), F32),
        compiler_params=_params(("arbitrary",), VMEM_LIMIT),
    )(dpa, dps, h)


VEC_NORM_G, VEC_B_IN, VEC_SINKS, VEC_LN_G, VEC_LN_B, VEC_B_OUT, VEC_FINAL_G, VEC_LOSS, VEC_SGU_B = 0, 1, 2, 3, 4, 5, 6, 7, 8


def _adamw(w, g, m, v):
    m = ADAM_B1 * m + (1.0 - ADAM_B1) * g
    v = ADAM_B2 * v + (1.0 - ADAM_B2) * (g * g)
    m_hat = m / (1.0 - ADAM_B1 ** ADAM_STEP)
    v_hat = v / (1.0 - ADAM_B2 ** ADAM_STEP)
    delta = -ADAM_LR * (m_hat / (jnp.sqrt(v_hat) + ADAM_EPS) + ADAM_WD * w)
    return delta, m, v


def _adamw_shard(name, g, w, m, v, block_rows):
    def body(g_ref, w_ref, m_ref, v_ref, d_ref, nm_ref, nv_ref):
        d_ref[...], nm_ref[...], nv_ref[...] = _adamw(w_ref[...], g_ref[...], m_ref[...], v_ref[...])

    rows, cols = w.shape
    spec = pl.BlockSpec((block_rows, cols), lambda i: (i, 0))
    return pl.pallas_call(
        body,
        name=name,
        grid=(rows // block_rows,),
        in_specs=[spec] * 4,
        out_specs=(spec,) * 3,
        out_shape=(jax.ShapeDtypeStruct(w.shape, F32),) * 3,
        compiler_params=_params(("arbitrary",)),
    )(g, w, m, v)


VECTOR_SEMS = 4


def _vector_scratch():
    return [pltpu.VMEM((VEC_ROWS, IN_W), F32), pltpu.VMEM((VEC_ROWS, IN_W), F32),
            pltpu.VMEM((4 * VEC_ROWS, IN_W), F32)]


def _all_reduce_vectors(copies, sem0, gng_ref, gba_ref, gbs_ref, gsink_ref, gln_ref, gsgub_ref, vec4_ref, out_ref,
                        vec_ref, ra_vec, slots):
    x, y, c = _place()
    vec_ref[...] = jnp.zeros_like(vec_ref)
    vec_ref[VEC_NORM_G:VEC_NORM_G + 1, 0:D_MODEL] = gng_ref[...]
    vec_ref[VEC_B_IN:VEC_B_IN + 1, 0:ATTN_SECTION] = gba_ref[...]
    vec_ref[VEC_B_IN:VEC_B_IN + 1, ATTN_SECTION:IN_W] = gbs_ref[...]
    vec_ref[VEC_SINKS:VEC_SINKS + 1, 0:LANES] = gsink_ref[...]
    vec_ref[VEC_LN_G:VEC_LN_G + 1, 0:SGU_W] = gln_ref[0:1, :]
    vec_ref[VEC_LN_B:VEC_LN_B + 1, 0:SGU_W] = gln_ref[1:2, :]
    vec_ref[VEC_B_OUT:VEC_B_OUT + 1, 0:D_MODEL] = vec4_ref[2:3, :]
    vec_ref[VEC_FINAL_G:VEC_FINAL_G + 1, 0:D_MODEL] = vec4_ref[1:2, :]
    vec_ref[VEC_LOSS:VEC_LOSS + 1, 0:D_MODEL] = vec4_ref[0:1, :]
    vec_ref[VEC_SGU_B:VEC_SGU_B + N_SGU_HEADS, 0:BLOCK] = gsgub_ref[...]

    to_sibling = copies(sem0, vec_ref, ra_vec, (x, y, 1 - c))
    to_sibling.start()
    to_sibling.wait_recv()

    def chip_slot(place):
        return slots.at[pl.ds(pl.multiple_of((2 * place[0] + place[1]) * VEC_ROWS, 8), VEC_ROWS), :]

    mine = chip_slot((x, y))
    mine[...] = vec_ref[...] + ra_vec[...]
    to_chips = [copies(sem0 + i, mine, mine, (*_chip(rel), c)) for i, rel in enumerate(RELATIONS[1:], start=1)]
    for cp in to_chips:
        cp.start()
    for i, rel in enumerate(RELATIONS[1:], start=1):
        theirs = chip_slot(_chip(rel))
        copies(sem0 + i, theirs, theirs, (x, y, c)).wait_recv()
    out_ref[...] = ((slots[0:VEC_ROWS, :] + slots[VEC_ROWS:2 * VEC_ROWS, :])
                    + slots[2 * VEC_ROWS:3 * VEC_ROWS, :]) + slots[3 * VEC_ROWS:, :]
    to_sibling.wait_send()
    for cp in to_chips:
        cp.wait_send()


def _adamw_replicated(vec, gsguw, weights, m_state, v_state):
    n = len(SMALL)

    def body(*refs):
        vec_ref, gsguw_ref = refs[0], refs[1]
        w_refs, m_refs, v_refs = (refs[2 + k * n:2 + (k + 1) * n] for k in range(3))
        outs = refs[2 + 3 * n:]
        g_refs, d_refs, nm_refs, nv_refs = (outs[k * n:(k + 1) * n] for k in range(4))
        for i, (_, row, shape) in enumerate(SMALL):
            g = gsguw_ref[...] if row is None else vec_ref[row:row + shape[0], 0:shape[1]]
            g_refs[i][...] = g
            d_refs[i][...], nm_refs[i][...], nv_refs[i][...] = _adamw(
                w_refs[i][...], g, m_refs[i][...], v_refs[i][...])

    shapes = tuple(jax.ShapeDtypeStruct(shape, F32) for _, _, shape in SMALL)
    outs = pl.pallas_call(
        body,
        name="adamw_replicated",
        in_specs=[VMEM_SPEC] * (2 + 3 * n),
        out_specs=(VMEM_SPEC,) * (4 * n),
        out_shape=shapes * 4,
    )(vec, gsguw, *weights, *m_state, *v_state)
    return tuple(outs[k * n:(k + 1) * n] for k in range(4))


SMALL = (
    ("norm_g", VEC_NORM_G, (1, D_MODEL)),
    ("b_in", VEC_B_IN, (1, IN_W)),
    ("attn_sinks", VEC_SINKS, (1, N_Q_HEADS)),
    ("sgu_ln_g", VEC_LN_G, (1, SGU_W)),
    ("sgu_ln_b", VEC_LN_B, (1, SGU_W)),
    ("sgu_w", None, (N_SGU_HEADS * BLOCK, BLOCK)),
    ("sgu_b", VEC_SGU_B, (N_SGU_HEADS, BLOCK)),
    ("b_out", VEC_B_OUT, (1, D_MODEL)),
    ("final_norm_g", VEC_FINAL_G, (1, D_MODEL)),
)


def _local_grads(x, target, win_t, wout_shard, norm_g, b_in, attn_sinks, sgu_ln_g, sgu_ln_b, sgu_w, sgu_b, b_out,
                 final_g):
    sinks = attn_sinks.reshape(N_Q_HEADS)
    bias_full = jnp.repeat(sgu_b.T, HEAD_DIM, axis=1)
    h, q, kvx, gates, wout = _in_proj(x, norm_g, b_in, win_t, wout_shard)
    out, ag = _attn_fwd(sinks, q, kvx, gates)
    sg = _sgu_fwd(gates, sgu_ln_g, sgu_ln_b, sgu_w, bias_full)
    gres, dmix, gwout, vec4 = _out_proj_loss(ag, sg, x, target, wout, b_out, final_g)
    dps, gsguw, gsgub, gln, gbin_s = _sgu_bwd(dmix, gates, sgu_ln_g, sgu_ln_b, sgu_w, bias_full)
    dpa, gsink, gbin_a, gwout_shard, gsguw_sum = _attn_bwd(
        sinks, dmix, q, kvx, out, gates, gwout, gsguw.reshape(N_SGU_HEADS * BLOCK, BLOCK))
    gwin = _win_grad(dpa, dps, h)
    grad_x, gwin_shard, vec = _in_proj_bwd(dpa, dps, win_t, x, norm_g, gres, gwin,
                                           (gbin_a, gbin_s, gsink, gln, gsgub, vec4))
    return grad_x, gwin_shard, gwout_shard, gsguw_sum, vec


def kernel(x, norm_g, w_in, b_in, attn_sinks, sgu_ln_g, sgu_ln_b, sgu_w, sgu_b, w_out, b_out, final_norm_g, loss_target, m_norm_g, m_w_in, m_b_in, m_attn_sinks, m_sgu_ln_g, m_sgu_ln_b, m_sgu_w, m_sgu_b, m_w_out, m_b_out, m_final_norm_g, v_norm_g, v_w_in, v_b_in, v_attn_sinks, v_sgu_ln_g, v_sgu_ln_b, v_sgu_w, v_sgu_b, v_w_out, v_b_out, v_final_norm_g):
    given = dict(norm_g=norm_g, b_in=b_in, attn_sinks=attn_sinks, sgu_ln_g=sgu_ln_g, sgu_ln_b=sgu_ln_b,
                 sgu_w=sgu_w, sgu_b=sgu_b, b_out=b_out, final_norm_g=final_norm_g)
    m_given = dict(norm_g=m_norm_g, b_in=m_b_in, attn_sinks=m_attn_sinks, sgu_ln_g=m_sgu_ln_g,
                   sgu_ln_b=m_sgu_ln_b, sgu_w=m_sgu_w, sgu_b=m_sgu_b, b_out=m_b_out, final_norm_g=m_final_norm_g)
    v_given = dict(norm_g=v_norm_g, b_in=v_b_in, attn_sinks=v_attn_sinks, sgu_ln_g=v_sgu_ln_g,
                   sgu_ln_b=v_sgu_ln_b, sgu_w=v_sgu_w, sgu_b=v_sgu_b, b_out=v_b_out, final_norm_g=v_final_norm_g)

    win_t = _all_gather_win(w_in[0].T)
    grad_x, gwin_t, gwout, gsguw, vec = _local_grads(
        x[0], loss_target[0], win_t, w_out[0], norm_g, b_in, attn_sinks, sgu_ln_g, sgu_ln_b, sgu_w[0], sgu_b[0],
        b_out, final_norm_g.reshape(1, D_MODEL))

    t = lambda a: a[0].T
    d_win, nm_win, nv_win = _adamw_shard("adamw_w_in", gwin_t, t(w_in), t(m_w_in), t(v_w_in), WIN_ROWS // 2)
    d_wout, nm_wout, nv_wout = _adamw_shard("adamw_w_out", gwout, w_out[0], m_w_out[0], v_w_out[0], WOUT_ROWS)
    as_2d = lambda d: [d[name].reshape(shape) for name, _, shape in SMALL]
    loss = vec[VEC_LOSS, 0]
    small = _adamw_replicated(vec, gsguw, as_2d(given), as_2d(m_given), as_2d(v_given))

    def assemble(big_in, big_out, k):
        vals = {name: small[k][i].reshape(given[name].shape) for i, (name, _, _) in enumerate(SMALL)}
        vals["w_in"] = big_in.T[None]
        vals["w_out"] = big_out[None]
        order = ("norm_g", "w_in", "b_in", "attn_sinks", "sgu_ln_g", "sgu_ln_b", "sgu_w", "sgu_b", "w_out",
                 "b_out", "final_norm_g")
        return [vals[name] for name in order]

    return (loss, grad_x[None],
            *assemble(gwin_t, gwout, 0), *assemble(d_win, d_wout, 1),
            *assemble(nm_win, nm_wout, 2), *assemble(nv_win, nv_wout, 3))
```

```python
import functools
import math

import jax
import jax.numpy as jnp
from jax import lax
from jax.experimental import pallas as pl
from jax.experimental.pallas import tpu as pltpu

F32 = jnp.float32
BF16 = jnp.bfloat16
MXU_DTYPE = BF16
COMM_DTYPE = BF16

D_MODEL = 1024
SEQ = 4096
HEAD_DIM = 64
N_Q_HEADS = 8
Q_PER_KV = 4
BLOCK = 128
N_BLOCKS = SEQ // BLOCK
ATTN_W = 512
KV_W = 128
SGU_W = 512
N_SGU_HEADS = 8
IN_W = 2816
NORM_EPS = 1e-5
NEG_INF = -1e30
SCALE = HEAD_DIM ** -0.5
KV0 = ATTN_W
GATE0 = ATTN_W + 2 * KV_W
SGU0 = GATE0 + ATTN_W
ATTN_SECTION = SGU0
SGU_SECTION = IN_W - SGU0

ADAM_LR = 0.001
ADAM_B1 = 0.9
ADAM_B2 = 0.999
ADAM_EPS = 1e-08
ADAM_WD = 0.01
ADAM_STEP = 10

N_DEV = 8
WIN_ROWS = IN_W // N_DEV
WOUT_ROWS = D_MODEL // N_DEV
SGUW_ROWS = N_SGU_HEADS * BLOCK // N_DEV
VEC_ROWS = 16
MESH = pl.DeviceIdType.MESH

LANES = 128
HALF = LANES // 2
N_PAIRS = N_Q_HEADS * HEAD_DIM // LANES
KVX_W = 12 * LANES
TOKEN_TILE = 256
FWD_TOKEN_TILE = 512
SGU_CHUNKS_PER_STEP = 4
ATTN_FWD_AHEAD = 4
ATTN_BWD_AHEAD = 3
VMEM_LIMIT = 56 * 1024 * 1024

NN = (((1,), (0,)), ((), ()))
NT = (((1,), (1,)), ((), ()))
TN = (((0,), (0,)), ((), ()))


def _dot(a, b, dims=NN):
    return lax.dot_general(a.astype(MXU_DTYPE), b.astype(MXU_DTYPE), dims, preferred_element_type=F32)


def _gelu(x):
    return x * (lax.erf(x * (1.0 / math.sqrt(2.0))) + 1.0) * 0.5


def _gelu_grad(x):
    cdf = (lax.erf(x * (1.0 / math.sqrt(2.0))) + 1.0) * 0.5
    return cdf + x * jnp.exp(-0.5 * x * x) * (1.0 / math.sqrt(2.0 * math.pi))


def _silu_and_grad(z):
    s = jax.nn.sigmoid(z)
    return z * s, s * (1.0 + z * (1.0 - s))


def _params(semantics=None, vmem=None):
    kw = {}
    if semantics is not None:
        kw["dimension_semantics"] = semantics
    if vmem is not None:
        kw["vmem_limit_bytes"] = vmem
    return pltpu.CompilerParams(**kw)


def _full(shape):
    return pl.BlockSpec(shape, lambda *_: (0,) * len(shape))


VMEM_SPEC = pl.BlockSpec(memory_space=pltpu.VMEM)


RELATIONS = ((0, 0), (1, 0), (0, 1), (1, 1))


def _place():
    return lax.axis_index("x"), lax.axis_index("y"), lax.axis_index("c")


def _chip(rel):
    x, y, _ = _place()
    return (1 - x if rel[0] else x, 1 - y if rel[1] else y)


def _block_rows(place, n_rows):
    px, py, pc = place
    return pl.ds(pl.multiple_of((4 * px + 2 * py + pc) * n_rows, 16), n_rows)


class _Copies:
    def __init__(self, send_sems, recv_sems):
        self.send_sems, self.recv_sems = send_sems, recv_sems

    def __call__(self, k, src, dst, to):
        return pltpu.make_async_remote_copy(src_ref=src, dst_ref=dst, send_sem=self.send_sems.at[k],
                                            recv_sem=self.recv_sems.at[k], device_id=to, device_id_type=MESH)


def _gather_plan(copies, sem0, full_ref, n_rows):
    x, y, c = _place()
    me, sibling = (x, y, c), (x, y, 1 - c)
    chips = [_chip(rel) for rel in RELATIONS[1:]]

    def cp(k, block, to):
        rows = full_ref.at[_block_rows(block, n_rows), :]
        return copies(sem0 + k, rows, rows, to)

    first = [cp(0, me, sibling)] + [cp(1 + j, me, (*chip, c)) for j, chip in enumerate(chips)]
    passed = [cp(4 + j, (*chip, c), sibling) for j, chip in enumerate(chips)]

    def start():
        for f in first:
            f.start()

    def forward():
        for j, chip in enumerate(chips):
            cp(1 + j, (*chip, c), me).wait_recv()
            passed[j].start()

    def finish():
        cp(0, sibling, me).wait_recv()
        for j, chip in enumerate(chips):
            cp(4 + j, (*chip, 1 - c), me).wait_recv()
        for f in first + passed:
            f.wait_send()

    return start, forward, finish


GATHER_SEMS = 7


def _reduce_scatter_plan(copies, sem0, part_ref, n_rows, sa, ra, sb, rc, res_ref):
    x, y, c = _place()
    sibling = (x, y, 1 - c)
    n = n_rows
    level1 = copies(sem0, sa, ra, sibling)

    def level2(i):
        slot = pl.ds((i - 1) * n, n)
        return copies(sem0 + i, sb.at[slot, :], rc.at[slot, :], (*_chip(RELATIONS[i]), c))

    def start():
        for i, rel in enumerate(RELATIONS):
            sa[i * n:(i + 1) * n, :] = part_ref[_block_rows((*_chip(rel), 1 - c), n), :].astype(sa.dtype)
        level1.start()

    def exchange():
        level1.wait_recv()
        for i, rel in enumerate(RELATIONS):
            total = part_ref[_block_rows((*_chip(rel), c), n), :] + ra[i * n:(i + 1) * n, :].astype(F32)
            if i == 0:
                res_ref[...] = total
            else:
                sb[(i - 1) * n:i * n, :] = total.astype(sb.dtype)
                level2(i).start()

    def finish():
        acc = res_ref[...]
        for i in range(1, len(RELATIONS)):
            level2(i).wait_recv()
            acc = acc + rc[(i - 1) * n:i * n, :].astype(F32)
        res_ref[...] = acc
        level1.wait_send()
        for i in range(1, len(RELATIONS)):
            level2(i).wait_send()

    return start, exchange, finish


REDUCE_SEMS = 4


def _reduce_scatter_scratch(n_rows, width, dtype):
    return [pltpu.VMEM((4 * n_rows, width), dtype), pltpu.VMEM((4 * n_rows, width), dtype),
            pltpu.VMEM((3 * n_rows, width), dtype), pltpu.VMEM((3 * n_rows, width), dtype)]


def _dma_sems(n):
    return [pltpu.SemaphoreType.DMA((n,)), pltpu.SemaphoreType.DMA((n,))]


def _all_gather_win(win_t_shard):
    def body(win_ref, full_ref, send_sems, recv_sems):
        full_ref[_block_rows(_place(), WIN_ROWS), :] = win_ref[...].astype(COMM_DTYPE)
        start, forward, finish = _gather_plan(_Copies(send_sems, recv_sems), 0, full_ref, WIN_ROWS)
        start()
        forward()
        finish()

    return pl.pallas_call(
        body,
        name="all_gather_win",
        out_shape=jax.ShapeDtypeStruct((IN_W, D_MODEL), COMM_DTYPE),
        in_specs=[VMEM_SPEC],
        out_specs=VMEM_SPEC,
        scratch_shapes=_dma_sems(GATHER_SEMS),
        compiler_params=_params(vmem=VMEM_LIMIT),
    )(win_t_shard)


def _in_proj(x, norm_g, b_in, win_t, wout_shard):
    tm = FWD_TOKEN_TILE
    steps = SEQ // tm

    def body(x_ref, g_ref, b_ref, w_ref, wout_ref, h_ref, q_ref, kvx_ref, gate_ref, wfull_ref,
             landing, send_sems, recv_sems):
        step = pl.program_id(0)
        start, forward, finish = _gather_plan(_Copies(send_sems, recv_sems), 0, landing, WOUT_ROWS)

        @pl.when(step == 0)
        def _():
            landing[_block_rows(_place(), WOUT_ROWS), :] = wout_ref[...].astype(COMM_DTYPE)
            start()

        pl.when(step == steps // 2)(forward)

        xv = x_ref[...]
        r = lax.rsqrt(jnp.mean(xv * xv, axis=-1, keepdims=True) + NORM_EPS)
        h = ((xv * r) * g_ref[...]).astype(MXU_DTYPE)
        h_ref[...] = h

        def proj(lo, hi):
            return _dot(h, w_ref[lo:hi, :], NT) + b_ref[:, lo:hi]

        qs = proj(0, ATTN_W) * SCALE
        for pair in range(N_PAIRS):
            q_ref[pair] = qs[:, pair * LANES:(pair + 1) * LANES].astype(MXU_DTYPE)
        kv = proj(KV0, GATE0)
        low = lax.broadcasted_iota(jnp.int32, (tm, LANES), 1) < HALF
        for i in range(2):
            t = kv[:, i * LANES:(i + 1) * LANES]
            rot = pltpu.roll(t, HALF, 1)
            variants = (jnp.where(low, t, 0.0), jnp.where(low, 0.0, rot),
                        jnp.where(low, rot, 0.0), jnp.where(low, 0.0, t))
            for j, val in enumerate(variants):
                col = (4 * i + j) * LANES
                kvx_ref[:, col:col + LANES] = val.astype(MXU_DTYPE)
                if i == 1:
                    ones_elsewhere = jnp.where(low == (j % 2 == 0), val, 1.0)
                    kvx_ref[:, col + 4 * LANES:col + 5 * LANES] = ones_elsewhere.astype(MXU_DTYPE)
        for k in range(4):
            gate_ref[k] = proj(GATE0 + k * SGU_W, GATE0 + (k + 1) * SGU_W)

        @pl.when(step == steps - 1)
        def _():
            finish()
            wfull_ref[...] = landing[...]

    return pl.pallas_call(
        body,
        name="in_proj",
        grid=(steps,),
        in_specs=[pl.BlockSpec((tm, D_MODEL), lambda i: (i, 0)),
                  _full((1, D_MODEL)), _full((1, IN_W)), _full((IN_W, D_MODEL)), VMEM_SPEC],
        out_specs=(pl.BlockSpec((tm, D_MODEL), lambda i: (i, 0)),
                   pl.BlockSpec((N_PAIRS, tm, LANES), lambda i: (0, i, 0)),
                   pl.BlockSpec((tm, KVX_W), lambda i: (i, 0)),
                   pl.BlockSpec((4, tm, SGU_W), lambda i: (0, i, 0)),
                   _full((D_MODEL, D_MODEL))),
        out_shape=(jax.ShapeDtypeStruct((SEQ, D_MODEL), MXU_DTYPE),
                   jax.ShapeDtypeStruct((N_PAIRS, SEQ, LANES), MXU_DTYPE),
                   jax.ShapeDtypeStruct((SEQ, KVX_W), MXU_DTYPE),
                   jax.ShapeDtypeStruct((4, SEQ, SGU_W), F32),
                   jax.ShapeDtypeStruct((D_MODEL, D_MODEL), COMM_DTYPE)),
        scratch_shapes=[pltpu.VMEM((D_MODEL, D_MODEL), COMM_DTYPE)] + _dma_sems(GATHER_SEMS),
        compiler_params=_params(("arbitrary",), VMEM_LIMIT),
    )(x, norm_g, b_in, win_t, wout_shard)


def _window_mask(n):
    qi = lax.broadcasted_iota(jnp.int32, (2 * BLOCK, 2 * BLOCK), 0) & (BLOCK - 1)
    p = lax.broadcasted_iota(jnp.int32, (2 * BLOCK, 2 * BLOCK), 1) - BLOCK
    in_window = jnp.logical_and(p <= qi, p > qi - BLOCK)
    return jnp.logical_and(in_window, jnp.logical_or(p >= 0, n > 0))


def _sink_column(sink_ref, g, par):
    return jnp.concatenate([jnp.full((BLOCK, 1), sink_ref[4 * g + par], F32),
                            jnp.full((BLOCK, 1), sink_ref[4 * g + 2 + par], F32)], axis=0)


def _kv_cat(kp_ref, kc_ref, var, with_ones):
    kcol, vcol = var * LANES, (var + (8 if with_ones else 4)) * LANES
    return (jnp.concatenate([kp_ref[:, kcol:kcol + LANES], kc_ref[:, kcol:kcol + LANES]], axis=0),
            jnp.concatenate([kp_ref[:, vcol:vcol + LANES], kc_ref[:, vcol:vcol + LANES]], axis=0))


def _softmax_numerator(s, sink):
    m = jnp.maximum(jnp.max(s, axis=1, keepdims=True), sink)
    return jnp.exp(s - m), m


def _attn_fwd(sinks, q, kvx, gates):
    def body(sink_ref, q_ref, kc_ref, kp_ref, za_ref, out_ref, ag_ref):
        valid = _window_mask(pl.program_id(0))[0:BLOCK]
        chains = [(g, par, i) for g in range(2) for par in range(2) for i in range(2)]
        kv = {(g, par): _kv_cat(kp_ref, kc_ref, 2 * g + par, True) for g in range(2) for par in range(2)}
        scores, outs = {}, {}

        def issue_scores(k):
            g, par, i = chains[k]
            scores[k] = _dot(q_ref[2 * g + i], kv[g, par][0], NT)

        ahead = ATTN_FWD_AHEAD
        for k in range(ahead):
            issue_scores(k)
        low = lax.broadcasted_iota(jnp.int32, (BLOCK, LANES), 1) < HALF
        for k, (g, par, i) in enumerate(chains):
            sink = sink_ref[4 * g + 2 * i + par]
            e, m = _softmax_numerator(jnp.where(valid, scores[k], NEG_INF), sink)
            if k + ahead < len(chains):
                issue_scores(k + ahead)
            o = _dot(e, kv[g, par][1])
            outs[g, par, i] = o / (pltpu.roll(o, HALF, 1) + jnp.exp(sink - m))
        for pair in range(N_PAIRS):
            g, i = divmod(pair, 2)
            o = jnp.where(low, outs[g, 0, i], outs[g, 1, i])
            out_ref[pair] = o
            gate, _ = _silu_and_grad(za_ref[:, pair * LANES:(pair + 1) * LANES])
            ag_ref[:, pair * LANES:(pair + 1) * LANES] = (o * gate).astype(MXU_DTYPE)

    blk = lambda w: pl.BlockSpec((BLOCK, w), lambda n: (n, 0))
    tiles = pl.BlockSpec((N_PAIRS, BLOCK, LANES), lambda n: (0, n, 0))
    return pl.pallas_call(
        body,
        name="attn_fwd",
        grid=(N_BLOCKS,),
        in_specs=[pl.BlockSpec(memory_space=pltpu.SMEM), tiles, blk(KVX_W),
                  pl.BlockSpec((BLOCK, KVX_W), lambda n: (jnp.maximum(n - 1, 0), 0)),
                  pl.BlockSpec((None, BLOCK, ATTN_W), lambda n: (0, n, 0))],
        out_specs=(tiles, blk(ATTN_W)),
        out_shape=(jax.ShapeDtypeStruct((N_PAIRS, SEQ, LANES), F32),
                   jax.ShapeDtypeStruct((SEQ, ATTN_W), MXU_DTYPE)),
        compiler_params=_params(("arbitrary",)),
    )(sinks, q, kvx, kvx, gates)


def _sgu_forward_chunk(us, vs, lng, lnb, w_ref, bias_ref):
    u = _gelu(us)
    vg = _gelu(vs)
    mu = jnp.mean(vg, axis=-1, keepdims=True)
    xc = vg - mu
    rstd = lax.rsqrt(jnp.mean(xc * xc, axis=-1, keepdims=True) + NORM_EPS)
    vhat = xc * rstd
    vln = vhat * lng + lnb
    low = lax.broadcasted_iota(jnp.int32, (BLOCK, LANES), 1) < HALF
    tril = (lax.broadcasted_iota(jnp.int32, (BLOCK, BLOCK), 0)
            >= lax.broadcasted_iota(jnp.int32, (BLOCK, BLOCK), 1))
    mixed = []
    for pair in range(N_SGU_HEADS // 2):
        vp = vln[:, pair * LANES:(pair + 1) * LANES]
        w0 = jnp.where(tril, w_ref[2 * pair], 0.0)
        w1 = jnp.where(tril, w_ref[2 * pair + 1], 0.0)
        mixed.append(_dot(w0, jnp.where(low, vp, 0.0)) + _dot(w1, jnp.where(low, 0.0, vp))
                     + bias_ref[:, pair * LANES:(pair + 1) * LANES])
    return u, vhat, rstd, vln, mixed


def _sgu_fwd(gates, ln_g, ln_b, sgu_w, bias_full):
    rows = SGU_CHUNKS_PER_STEP * BLOCK

    def body(us_ref, vs_ref, zs_ref, lng_ref, lnb_ref, w_ref, bias_ref, sg_ref):
        def chunk(c, carry):
            at = pl.ds(pl.multiple_of(c * BLOCK, BLOCK), BLOCK)
            u, _, _, _, mixed = _sgu_forward_chunk(us_ref[at, :], vs_ref[at, :], lng_ref[...], lnb_ref[...],
                                                   w_ref, bias_ref)
            for pair in range(N_SGU_HEADS // 2):
                cols = slice(pair * LANES, (pair + 1) * LANES)
                gate, _ = _silu_and_grad(zs_ref[at, cols])
                sg_ref[at, cols] = (u[:, cols] * mixed[pair] * gate).astype(MXU_DTYPE)
            return carry

        lax.fori_loop(0, SGU_CHUNKS_PER_STEP, chunk, 0)

    col = lambda k: pl.BlockSpec((None, rows, SGU_W), lambda n: (k, n, 0))
    return pl.pallas_call(
        body,
        name="sgu_fwd",
        grid=(SEQ // rows,),
        in_specs=[col(1), col(2), col(3), _full((1, SGU_W)), _full((1, SGU_W)),
                  _full((N_SGU_HEADS, BLOCK, BLOCK)), _full((BLOCK, SGU_W))],
        out_specs=pl.BlockSpec((rows, SGU_W), lambda n: (n, 0)),
        out_shape=jax.ShapeDtypeStruct((SEQ, SGU_W), MXU_DTYPE),
        compiler_params=_params(("arbitrary",)),
    )(gates, gates, gates, ln_g, ln_b, sgu_w, bias_full)


def _out_proj_loss(ag, sg, x, target, wout, b_out, final_g):
    tm = FWD_TOKEN_TILE

    def body(ag_ref, sg_ref, x_ref, t_ref, w_ref, b_ref, gf_ref, gres_ref, dmix_ref, gw_ref, vec_ref):
        @pl.when(pl.program_id(0) == 0)
        def _():
            gw_ref[...] = jnp.zeros_like(gw_ref)
            vec_ref[...] = jnp.zeros_like(vec_ref)

        a = ag_ref[...]
        s = sg_ref[...]
        xo = x_ref[...] + (_dot(a, w_ref[0:ATTN_W, :]) + _dot(s, w_ref[ATTN_W:, :])) + b_ref[...]
        r = lax.rsqrt(jnp.mean(xo * xo, axis=-1, keepdims=True) + NORM_EPS)
        xn = xo * r
        gf = gf_ref[...]
        err = xn * gf - t_ref[...]
        loss = 0.5 * jnp.sum(jnp.mean(err * err, axis=-1, keepdims=True), axis=0, keepdims=True)
        dy = err * (1.0 / D_MODEL)
        dxn = dy * gf
        gres = r * (dxn - xn * jnp.mean(dxn * xn, axis=-1, keepdims=True))
        vec_ref[0:1, :] += jnp.broadcast_to(loss, (1, D_MODEL))
        vec_ref[1:2, :] += jnp.sum(dy * xn, axis=0, keepdims=True)
        vec_ref[2:3, :] += jnp.sum(gres, axis=0, keepdims=True)
        gres_ref[...] = gres
        gb = gres.astype(MXU_DTYPE)
        dmix_ref[0] = _dot(gb, w_ref[0:ATTN_W, :], NT)
        dmix_ref[1] = _dot(gb, w_ref[ATTN_W:, :], NT)
        gw_ref[0:ATTN_W, :] += _dot(a, gb, TN)
        gw_ref[ATTN_W:, :] += _dot(s, gb, TN)

    tile = lambda w: pl.BlockSpec((tm, w), lambda i: (i, 0))
    return pl.pallas_call(
        body,
        name="out_proj_loss",
        grid=(SEQ // tm,),
        in_specs=[tile(ATTN_W), tile(SGU_W), tile(D_MODEL), tile(D_MODEL),
                  _full((D_MODEL, D_MODEL)), _full((1, D_MODEL)), _full((1, D_MODEL))],
        out_specs=(tile(D_MODEL), pl.BlockSpec((2, tm, ATTN_W), lambda i: (0, i, 0)), _full((D_MODEL, D_MODEL)),
                   _full((8, D_MODEL))),
        out_shape=(jax.ShapeDtypeStruct((SEQ, D_MODEL), F32),
                   jax.ShapeDtypeStruct((2, SEQ, ATTN_W), F32),
                   jax.ShapeDtypeStruct((D_MODEL, D_MODEL), F32),
                   jax.ShapeDtypeStruct((8, D_MODEL), F32)),
        compiler_params=_params(("arbitrary",), VMEM_LIMIT),
    )(ag, sg, x, target, wout, b_out, final_g)


def _sgu_bwd(dmix, gates, ln_g, ln_b, sgu_w, bias_full):
    last = N_BLOCKS - 1

    def body(d_ref, us_ref, vs_ref, zs_ref, lng_ref, lnb_ref, w_ref, bias_ref,
             dp_ref, gw_ref, gb_ref, gln_ref, gbin_ref, wt_ref, gbias_ref):
        c = pl.program_id(0)
        tril = (lax.broadcasted_iota(jnp.int32, (BLOCK, BLOCK), 0)
                >= lax.broadcasted_iota(jnp.int32, (BLOCK, BLOCK), 1))

        @pl.when(c == 0)
        def _():
            gw_ref[...] = jnp.zeros_like(gw_ref)
            gln_ref[...] = jnp.zeros_like(gln_ref)
            gbin_ref[...] = jnp.zeros_like(gbin_ref)
            gbias_ref[...] = jnp.zeros_like(gbias_ref)
            for hh in range(N_SGU_HEADS):
                wt_ref[hh] = jnp.where(tril, w_ref[hh], 0.0).T.astype(MXU_DTYPE)

        us = us_ref[...]
        vs = vs_ref[...]
        lng = lng_ref[...]
        u, vhat, rstd, vln, mixed = _sgu_forward_chunk(us, vs, lng, lnb_ref[...], w_ref, bias_ref)
        low = lax.broadcasted_iota(jnp.int32, (BLOCK, LANES), 1) < HALF
        du_parts, dzs_parts, dvln_parts = [], [], []
        for pair in range(N_SGU_HEADS // 2):
            cols = slice(pair * LANES, (pair + 1) * LANES)
            dsg = d_ref[:, cols]
            gate, gate_grad = _silu_and_grad(zs_ref[:, cols])
            up = u[:, cols]
            du_parts.append(dsg * mixed[pair] * gate)
            dzs_parts.append(dsg * up * mixed[pair] * gate_grad)
            dmixed = dsg * up * gate
            gbias_ref[:, cols] += dmixed
            dm_lo = jnp.where(low, dmixed, 0.0)
            dm_hi = jnp.where(low, 0.0, dmixed)
            vp = vln[:, cols]
            gw_ref[2 * pair] += _dot(dm_lo, vp, NT)
            gw_ref[2 * pair + 1] += _dot(dm_hi, vp, NT)
            dvln_parts.append(_dot(wt_ref[2 * pair], dm_lo) + _dot(wt_ref[2 * pair + 1], dm_hi))
        dvln = jnp.concatenate(dvln_parts, axis=1)
        gln_ref[0:1, :] += jnp.sum(dvln * vhat, axis=0, keepdims=True)
        gln_ref[1:2, :] += jnp.sum(dvln, axis=0, keepdims=True)
        dvhat = dvln * lng
        dvg = rstd * (dvhat - jnp.mean(dvhat, axis=-1, keepdims=True)
                      - vhat * jnp.mean(dvhat * vhat, axis=-1, keepdims=True))
        dus = jnp.concatenate(du_parts, axis=1) * _gelu_grad(us)
        dvs = dvg * _gelu_grad(vs)
        dzs = jnp.concatenate(dzs_parts, axis=1)
        for k, val in enumerate((dus, dvs, dzs)):
            dp_ref[:, k * SGU_W:(k + 1) * SGU_W] = val.astype(MXU_DTYPE)
            gbin_ref[:, k * SGU_W:(k + 1) * SGU_W] += jnp.sum(val, axis=0, keepdims=True)

        @pl.when(c == last)
        def _():
            for hh in range(N_SGU_HEADS):
                gw_ref[hh] = jnp.where(tril, gw_ref[hh], 0.0)
            head_of_lane = lax.broadcasted_iota(jnp.int32, (N_SGU_HEADS, SGU_W), 1) // HEAD_DIM
            select = (head_of_lane == lax.broadcasted_iota(jnp.int32, (N_SGU_HEADS, SGU_W), 0)).astype(F32)
            gb_ref[...] = lax.dot_general(select, gbias_ref[...], NT, precision=lax.Precision.HIGHEST,
                                          preferred_element_type=F32)

    col = lambda k: pl.BlockSpec((None, BLOCK, SGU_W), lambda n: (k, n, 0))
    return pl.pallas_call(
        body,
        name="sgu_bwd",
        grid=(N_BLOCKS,),
        in_specs=[col(1), col(1), col(2), col(3), _full((1, SGU_W)), _full((1, SGU_W)),
                  _full((N_SGU_HEADS, BLOCK, BLOCK)), _full((BLOCK, SGU_W))],
        out_specs=(pl.BlockSpec((BLOCK, SGU_SECTION), lambda n: (n, 0)),
                   _full((N_SGU_HEADS, BLOCK, BLOCK)), _full((N_SGU_HEADS, BLOCK)),
                   _full((8, SGU_W)), _full((1, SGU_SECTION))),
        out_shape=(jax.ShapeDtypeStruct((SEQ, SGU_SECTION), MXU_DTYPE),
                   jax.ShapeDtypeStruct((N_SGU_HEADS, BLOCK, BLOCK), F32),
                   jax.ShapeDtypeStruct((N_SGU_HEADS, BLOCK), F32),
                   jax.ShapeDtypeStruct((8, SGU_W), F32),
                   jax.ShapeDtypeStruct((1, SGU_SECTION), F32)),
        scratch_shapes=[pltpu.VMEM((N_SGU_HEADS, BLOCK, BLOCK), MXU_DTYPE),
                        pltpu.VMEM((BLOCK, SGU_W), F32)],
        compiler_params=_params(("arbitrary",)),
    )(dmix, gates, gates, gates, ln_g, ln_b, sgu_w, bias_full)


def _attn_bwd(sinks, dmix, q, kvx, out, gates, gwout, gsguw):
    last = N_BLOCKS - 1

    def body(sink_ref, d_ref, q_ref, kc_ref, kp_ref, o_ref, za_ref, gwout_ref, gsguw_ref,
             dp_ref, gsink_ref, gbin_ref, wout_shard_ref, sguw_full_ref,
             pend_ref, carry_ref, sa_w, ra_w, sb_w, rc_w, sa_s, ra_s, sb_s, rc_s, landing, send_sems, recv_sems):
        n = pl.program_id(0)
        copies = _Copies(send_sems, recv_sems)
        own_sguw = landing.at[_block_rows(_place(), SGUW_ROWS), :]
        plans = [_reduce_scatter_plan(copies, 0, gwout_ref, WOUT_ROWS, sa_w, ra_w, sb_w, rc_w, wout_shard_ref),
                 _reduce_scatter_plan(copies, REDUCE_SEMS, gsguw_ref, SGUW_ROWS, sa_s, ra_s, sb_s, rc_s, own_sguw)]
        gather = _gather_plan(copies, 2 * REDUCE_SEMS, landing, SGUW_ROWS)

        @pl.when(n == 0)
        def _():
            gsink_ref[...] = jnp.zeros_like(gsink_ref)
            gbin_ref[...] = jnp.zeros_like(gbin_ref)
            carry_ref[...] = jnp.zeros_like(carry_ref)
            for start, _, _ in plans:
                start()

        @pl.when(n == 3)
        def _():
            for _, exchange, _ in plans:
                exchange()

        @pl.when(n == 12)
        def _():
            for _, _, finish in plans:
                finish()
            gather[0]()

        pl.when(n == 16)(gather[1])

        @pl.when(n == last + 1)
        def _():
            gather[2]()
            sguw_full_ref[...] = landing[...]

        @pl.when(n > 0)
        def _():
            dp_ref[:, 0:ATTN_W] = pend_ref[:, 0:ATTN_W]
            dp_ref[:, GATE0:ATTN_SECTION] = pend_ref[:, ATTN_W:]

        @pl.when(n > last)
        def _():
            dp_ref[:, KV0:GATE0] = carry_ref[...].astype(MXU_DTYPE)

        @pl.when(n <= last)
        def _():
            valid = _window_mask(n)[0:BLOCK]
            low = lax.broadcasted_iota(jnp.int32, (BLOCK, LANES), 1) < HALF
            low_keys = lax.broadcasted_iota(jnp.int32, (2 * BLOCK, LANES), 1) < HALF
            lane_row = lax.broadcasted_iota(jnp.int32, (1, LANES), 1)
            gsink = jnp.zeros((1, LANES), F32)
            chains = [(g, par, i) for g in range(2) for par in range(2) for i in range(2)]
            kv = {(g, par): _kv_cat(kp_ref, kc_ref, 2 * g + par, False) for g in range(2) for par in range(2)}
            ones_keys = jnp.ones((2 * BLOCK, LANES), MXU_DTYPE)
            half_of_lane = lax.broadcasted_iota(jnp.int32, (LANES, 2 * LANES), 0) // HALF
            half_of_col = lax.broadcasted_iota(jnp.int32, (LANES, 2 * LANES), 1) // LANES
            sum_halves = (half_of_lane == half_of_col).astype(MXU_DTYPE)
            douts, deltas = [], []
            for pair in range(N_PAIRS):
                lanes = slice(pair * LANES, (pair + 1) * LANES)
                dg = d_ref[:, lanes]
                gate, gate_grad = _silu_and_grad(za_ref[:, lanes])
                o = o_ref[pair]
                dout = dg * gate
                dza = dg * o * gate_grad
                douts.append(dout.astype(MXU_DTYPE))
                deltas.append(_dot(dout * o, sum_halves))
                zl = slice(ATTN_W + pair * LANES, ATTN_W + (pair + 1) * LANES)
                pend_ref[:, zl] = dza.astype(MXU_DTYPE)
                gl = slice(GATE0 + pair * LANES, GATE0 + (pair + 1) * LANES)
                gbin_ref[:, gl] += jnp.sum(dza, axis=0, keepdims=True)

            first = {}

            def issue_first(k):
                g, par, i = chains[k]
                first[k] = (_dot(q_ref[2 * g + i], kv[g, par][0], NT), _dot(douts[2 * g + i], kv[g, par][1], NT))

            numerators = {}

            def issue_row_sums(k):
                g, par, i = chains[k]
                sink = sink_ref[4 * g + 2 * i + par]
                e, m = _softmax_numerator(jnp.where(valid, first[k][0], NEG_INF), sink)
                numerators[k] = (e, jnp.exp(sink - m), _dot(e, ones_keys))

            ahead = ATTN_BWD_AHEAD
            for k in range(ahead):
                issue_first(k)
            issue_row_sums(0)
            issue_row_sums(1)
            dqs, dk_parts, dv_parts = {}, {}, {}
            operands = {}

            def issue_last(k):
                g, par, i = chains[k]
                ds, ds_t, p_t = operands.pop(k)
                dq = _dot(ds, kv[g, par][0])
                dqs[g, i] = dq if par == 0 else dqs[g, i] + dq
                dk = _dot(ds_t, q_ref[2 * g + i])
                dv = _dot(p_t, douts[2 * g + i])
                dk_parts[g, par] = dk if i == 0 else dk_parts[g, par] + dk
                dv_parts[g, par] = dv if i == 0 else dv_parts[g, par] + dv

            for k, (g, par, i) in enumerate(chains):
                h = 4 * g + 2 * i + par
                delta = deltas[2 * g + i][:, par * LANES:(par + 1) * LANES]
                e, at_sink, row_sum = numerators[k]
                inv = 1.0 / (row_sum + at_sink)
                p = e * jnp.tile(inv, (1, 2))
                ds = p * (first[k][1] - jnp.tile(delta, (1, 2)))
                operands[k] = (ds.astype(MXU_DTYPE), ds.T.astype(MXU_DTYPE), p.T.astype(MXU_DTYPE))
                total = jnp.sum(at_sink * inv * delta, axis=0, keepdims=True)
                gsink = jnp.where(lane_row == h, -total, gsink)
                if k + ahead < len(chains):
                    issue_first(k + ahead)
                if k + 2 < len(chains):
                    issue_row_sums(k + 2)
                if k > 0:
                    issue_last(k - 1)
            issue_last(len(chains) - 1)
            for pair in range(N_PAIRS):
                g, i = divmod(pair, 2)
                dq = dqs[g, i] * SCALE
                lanes = slice(pair * LANES, (pair + 1) * LANES)
                pend_ref[:, lanes] = dq.astype(MXU_DTYPE)
                gbin_ref[:, lanes] += jnp.sum(dq, axis=0, keepdims=True)
            gsink_ref[...] += gsink
            for k, parts in enumerate((dk_parts, dv_parts)):
                masked = {key: jnp.where(low_keys if key[1] == 0 else jnp.logical_not(low_keys), val, 0.0)
                          for key, val in parts.items()}
                both = (masked[0, 0] + masked[1, 1]
                        + pltpu.roll(masked[0, 1] + masked[1, 0], HALF, 1))
                lanes = slice(k * KV_W, (k + 1) * KV_W)
                done = carry_ref[:, lanes] + both[0:BLOCK]
                dp_ref[:, KV0 + k * KV_W:KV0 + (k + 1) * KV_W] = done.astype(MXU_DTYPE)
                carry_ref[:, lanes] = both[BLOCK:]
                gbin_ref[:, KV0 + k * KV_W:KV0 + (k + 1) * KV_W] += jnp.sum(both, axis=0, keepdims=True)

    at = lambda n: jnp.minimum(n, last)
    blk = lambda w: pl.BlockSpec((BLOCK, w), lambda n: (at(n), 0))
    tiles = pl.BlockSpec((N_PAIRS, BLOCK, LANES), lambda n: (0, at(n), 0))
    return pl.pallas_call(
        body,
        name="attn_bwd",
        grid=(N_BLOCKS + 1,),
        in_specs=[pl.BlockSpec(memory_space=pltpu.SMEM),
                  pl.BlockSpec((None, BLOCK, ATTN_W), lambda n: (0, at(n), 0)),
                  tiles,
                  blk(KVX_W),
                  pl.BlockSpec((BLOCK, KVX_W), lambda n: (jnp.maximum(at(n) - 1, 0), 0)),
                  tiles,
                  pl.BlockSpec((None, BLOCK, ATTN_W), lambda n: (0, at(n), 0)),
                  VMEM_SPEC, VMEM_SPEC],
        out_specs=(pl.BlockSpec((BLOCK, ATTN_SECTION), lambda n: (jnp.maximum(n - 1, 0), 0)),
                   _full((1, LANES)), _full((1, ATTN_SECTION)), VMEM_SPEC, _full((N_SGU_HEADS * BLOCK, BLOCK))),
        out_shape=(jax.ShapeDtypeStruct((SEQ, ATTN_SECTION), MXU_DTYPE),
                   jax.ShapeDtypeStruct((1, LANES), F32),
                   jax.ShapeDtypeStruct((1, ATTN_SECTION), F32),
                   jax.ShapeDtypeStruct((WOUT_ROWS, D_MODEL), F32),
                   jax.ShapeDtypeStruct((N_SGU_HEADS * BLOCK, BLOCK), F32)),
        scratch_shapes=([pltpu.VMEM((BLOCK, 2 * ATTN_W), MXU_DTYPE), pltpu.VMEM((BLOCK, 2 * KV_W), F32)]
                        + _reduce_scatter_scratch(WOUT_ROWS, D_MODEL, COMM_DTYPE)
                        + _reduce_scatter_scratch(SGUW_ROWS, BLOCK, F32)
                        + [pltpu.VMEM((N_SGU_HEADS * BLOCK, BLOCK), F32)]
                        + _dma_sems(2 * REDUCE_SEMS + GATHER_SEMS)),
        compiler_params=_params(("arbitrary",), VMEM_LIMIT),
    )(sinks, dmix, q, kvx, kvx, out, gates, gwout, gsguw)


def _in_proj_bwd(dpa, dps, win_t, x, norm_g, gres, gwin, vec_parts):
    tm = TOKEN_TILE
    steps = SEQ // tm
    n_parts = len(vec_parts)

    def body(da_ref, ds_ref, w_ref, x_ref, g_ref, gres_ref, gwin_ref, *rest):
        part_refs = rest[:n_parts]
        gx_ref, shard_ref, vec_out_ref, gng_ref, sa, ra, sb, rc, vec_ref, ra_vec, slots, send_sems, recv_sems = (
            rest[n_parts:])
        step = pl.program_id(0)
        copies = _Copies(send_sems, recv_sems)
        start, exchange, finish = _reduce_scatter_plan(copies, 0, gwin_ref, WIN_ROWS, sa, ra, sb, rc, shard_ref)

        @pl.when(step == 0)
        def _():
            gng_ref[...] = jnp.zeros_like(gng_ref)
            start()

        pl.when(step == 2)(exchange)

        dh = _dot(da_ref[...], w_ref[0:ATTN_SECTION, :]) + _dot(ds_ref[...], w_ref[ATTN_SECTION:, :])
        xv = x_ref[...]
        r = lax.rsqrt(jnp.mean(xv * xv, axis=-1, keepdims=True) + NORM_EPS)
        xn = xv * r
        gng_ref[...] += jnp.sum(dh * xn, axis=0, keepdims=True)
        dxn = dh * g_ref[...]
        gx_ref[...] = r * (dxn - xn * jnp.mean(dxn * xn, axis=-1, keepdims=True)) + gres_ref[...]

        @pl.when(step == steps - 1)
        def _():
            finish()
            _all_reduce_vectors(copies, REDUCE_SEMS, gng_ref, *part_refs, vec_out_ref, vec_ref, ra_vec, slots)

    tile = lambda w: pl.BlockSpec((tm, w), lambda i: (i, 0))
    grad_x, shard, vec = pl.pallas_call(
        body,
        name="in_proj_bwd",
        grid=(steps,),
        in_specs=[tile(ATTN_SECTION), tile(SGU_SECTION), _full((IN_W, D_MODEL)), tile(D_MODEL),
                  _full((1, D_MODEL)), tile(D_MODEL), VMEM_SPEC] + [VMEM_SPEC] * n_parts,
        out_specs=(tile(D_MODEL), VMEM_SPEC, VMEM_SPEC),
        out_shape=(jax.ShapeDtypeStruct((SEQ, D_MODEL), F32),
                   jax.ShapeDtypeStruct((WIN_ROWS, D_MODEL), F32),
                   jax.ShapeDtypeStruct((VEC_ROWS, IN_W), F32)),
        scratch_shapes=([pltpu.VMEM((1, D_MODEL), F32)] + _reduce_scatter_scratch(WIN_ROWS, D_MODEL, COMM_DTYPE)
                        + _vector_scratch() + _dma_sems(REDUCE_SEMS + VECTOR_SEMS)),
        compiler_params=_params(("arbitrary",), VMEM_LIMIT),
    )(dpa, dps, win_t, x, norm_g, gres, gwin, *vec_parts)
    return grad_x, shard, vec


def _win_grad(dpa, dps, h):
    rows = 256
    n_attn = ATTN_SECTION // rows
    steps = n_attn + SGU_SECTION // rows

    def body(da_ref, ds_ref, h_ref, o_ref):
        step = pl.program_id(0)

        @pl.when(step < n_attn)
        def _():
            o_ref[...] = _dot(da_ref[...], h_ref[...], TN)

        @pl.when(step >= n_attn)
        def _():
            o_ref[...] = _dot(ds_ref[...], h_ref[...], TN)

    return pl.pallas_call(
        body,
        name="win_grad",
        grid=(steps,),
        in_specs=[pl.BlockSpec((SEQ, rows), lambda i: (0, jnp.minimum(i, n_attn - 1))),
                  pl.BlockSpec((SEQ, rows), lambda i: (0, jnp.maximum(i - n_attn, 0))),
                  _full((SEQ, D_MODEL))],
        out_specs=pl.BlockSpec((rows, D_MODEL), lambda i: (i, 0)),
        out_shape=jax.ShapeDtypeStruct((IN_W, D_MODEL), F32),
        compiler_params=_params(("arbitrary",), VMEM_LIMIT),
    )(dpa, dps, h)


VEC_NORM_G, VEC_B_IN, VEC_SINKS, VEC_LN_G, VEC_LN_B, VEC_B_OUT, VEC_FINAL_G, VEC_LOSS, VEC_SGU_B = 0, 1, 2, 3, 4, 5, 6, 7, 8


def _adamw(w, g, m, v):
    m = ADAM_B1 * m + (1.0 - ADAM_B1) * g
    v = ADAM_B2 * v + (1.0 - ADAM_B2) * (g * g)
    m_hat = m / (1.0 - ADAM_B1 ** ADAM_STEP)
    v_hat = v / (1.0 - ADAM_B2 ** ADAM_STEP)
    delta = -ADAM_LR * (m_hat / (jnp.sqrt(v_hat) + ADAM_EPS) + ADAM_WD * w)
    return delta, m, v


def _adamw_shard(name, g, w, m, v, block_rows):
    def body(g_ref, w_ref, m_ref, v_ref, d_ref, nm_ref, nv_ref):
        d_ref[...], nm_ref[...], nv_ref[...] = _adamw(w_ref[...], g_ref[...], m_ref[...], v_ref[...])

    rows, cols = w.shape
    spec = pl.BlockSpec((block_rows, cols), lambda i: (i, 0))
    return pl.pallas_call(
        body,
        name=name,
        grid=(rows // block_rows,),
        in_specs=[spec] * 4,
        out_specs=(spec,) * 3,
        out_shape=(jax.ShapeDtypeStruct(w.shape, F32),) * 3,
        compiler_params=_params(("arbitrary",)),
    )(g, w, m, v)


VECTOR_SEMS = 4


def _vector_scratch():
    return [pltpu.VMEM((VEC_ROWS, IN_W), F32), pltpu.VMEM((VEC_ROWS, IN_W), F32),
            pltpu.VMEM((4 * VEC_ROWS, IN_W), F32)]


def _all_reduce_vectors(copies, sem0, gng_ref, gba_ref, gbs_ref, gsink_ref, gln_ref, gsgub_ref, vec4_ref, out_ref,
                        vec_ref, ra_vec, slots):
    x, y, c = _place()
    vec_ref[...] = jnp.zeros_like(vec_ref)
    vec_ref[VEC_NORM_G:VEC_NORM_G + 1, 0:D_MODEL] = gng_ref[...]
    vec_ref[VEC_B_IN:VEC_B_IN + 1, 0:ATTN_SECTION] = gba_ref[...]
    vec_ref[VEC_B_IN:VEC_B_IN + 1, ATTN_SECTION:IN_W] = gbs_ref[...]
    vec_ref[VEC_SINKS:VEC_SINKS + 1, 0:LANES] = gsink_ref[...]
    vec_ref[VEC_LN_G:VEC_LN_G + 1, 0:SGU_W] = gln_ref[0:1, :]
    vec_ref[VEC_LN_B:VEC_LN_B + 1, 0:SGU_W] = gln_ref[1:2, :]
    vec_ref[VEC_B_OUT:VEC_B_OUT + 1, 0:D_MODEL] = vec4_ref[2:3, :]
    vec_ref[VEC_FINAL_G:VEC_FINAL_G + 1, 0:D_MODEL] = vec4_ref[1:2, :]
    vec_ref[VEC_LOSS:VEC_LOSS + 1, 0:D_MODEL] = vec4_ref[0:1, :]
    vec_ref[VEC_SGU_B:VEC_SGU_B + N_SGU_HEADS, 0:BLOCK] = gsgub_ref[...]

    to_sibling = copies(sem0, vec_ref, ra_vec, (x, y, 1 - c))
    to_sibling.start()
    to_sibling.wait_recv()

    def chip_slot(place):
        return slots.at[pl.ds(pl.multiple_of((2 * place[0] + place[1]) * VEC_ROWS, 8), VEC_ROWS), :]

    mine = chip_slot((x, y))
    mine[...] = vec_ref[...] + ra_vec[...]
    to_chips = [copies(sem0 + i, mine, mine, (*_chip(rel), c)) for i, rel in enumerate(RELATIONS[1:], start=1)]
    for cp in to_chips:
        cp.start()
    for i, rel in enumerate(RELATIONS[1:], start=1):
        theirs = chip_slot(_chip(rel))
        copies(sem0 + i, theirs, theirs, (x, y, c)).wait_recv()
    out_ref[...] = ((slots[0:VEC_ROWS, :] + slots[VEC_ROWS:2 * VEC_ROWS, :])
                    + slots[2 * VEC_ROWS:3 * VEC_ROWS, :]) + slots[3 * VEC_ROWS:, :]
    to_sibling.wait_send()
    for cp in to_chips:
        cp.wait_send()


def _adamw_replicated(vec, gsguw, weights, m_state, v_state):
    n = len(SMALL)

    def body(*refs):
        vec_ref, gsguw_ref = refs[0], refs[1]
        w_refs, m_refs, v_refs = (refs[2 + k * n:2 + (k + 1) * n] for k in range(3))
        outs = refs[2 + 3 * n:]
        g_refs, d_refs, nm_refs, nv_refs = (outs[k * n:(k + 1) * n] for k in range(4))
        for i, (_, row, shape) in enumerate(SMALL):
            g = gsguw_ref[...] if row is None else vec_ref[row:row + shape[0], 0:shape[1]]
            g_refs[i][...] = g
            d_refs[i][...], nm_refs[i][...], nv_refs[i][...] = _adamw(
                w_refs[i][...], g, m_refs[i][...], v_refs[i][...])

    shapes = tuple(jax.ShapeDtypeStruct(shape, F32) for _, _, shape in SMALL)
    outs = pl.pallas_call(
        body,
        name="adamw_replicated",
        in_specs=[VMEM_SPEC] * (2 + 3 * n),
        out_specs=(VMEM_SPEC,) * (4 * n),
        out_shape=shapes * 4,
    )(vec, gsguw, *weights, *m_state, *v_state)
    return tuple(outs[k * n:(k + 1) * n] for k in range(4))


SMALL = (
    ("norm_g", VEC_NORM_G, (1, D_MODEL)),
    ("b_in", VEC_B_IN, (1, IN_W)),
    ("attn_sinks", VEC_SINKS, (1, N_Q_HEADS)),
    ("sgu_ln_g", VEC_LN_G, (1, SGU_W)),
    ("sgu_ln_b", VEC_LN_B, (1, SGU_W)),
    ("sgu_w", None, (N_SGU_HEADS * BLOCK, BLOCK)),
    ("sgu_b", VEC_SGU_B, (N_SGU_HEADS, BLOCK)),
    ("b_out", VEC_B_OUT, (1, D_MODEL)),
    ("final_norm_g", VEC_FINAL_G, (1, D_MODEL)),
)


def _local_grads(x, target, win_t, wout_shard, norm_g, b_in, attn_sinks, sgu_ln_g, sgu_ln_b, sgu_w, sgu_b, b_out,
                 final_g):
    sinks = attn_sinks.reshape(N_Q_HEADS)
    bias_full = jnp.repeat(sgu_b.T, HEAD_DIM, axis=1)
    h, q, kvx, gates, wout = _in_proj(x, norm_g, b_in, win_t, wout_shard)
    out, ag = _attn_fwd(sinks, q, kvx, gates)
    sg = _sgu_fwd(gates, sgu_ln_g, sgu_ln_b, sgu_w, bias_full)
    gres, dmix, gwout, vec4 = _out_proj_loss(ag, sg, x, target, wout, b_out, final_g)
    dps, gsguw, gsgub, gln, gbin_s = _sgu_bwd(dmix, gates, sgu_ln_g, sgu_ln_b, sgu_w, bias_full)
    dpa, gsink, gbin_a, gwout_shard, gsguw_sum = _attn_bwd(
        sinks, dmix, q, kvx, out, gates, gwout, gsguw.reshape(N_SGU_HEADS * BLOCK, BLOCK))
    gwin = _win_grad(dpa, dps, h)
    grad_x, gwin_shard, vec = _in_proj_bwd(dpa, dps, win_t, x, norm_g, gres, gwin,
                                           (gbin_a, gbin_s, gsink, gln, gsgub, vec4))
    return grad_x, gwin_shard, gwout_shard, gsguw_sum, vec


def kernel(x, norm_g, w_in, b_in, attn_sinks, sgu_ln_g, sgu_ln_b, sgu_w, sgu_b, w_out, b_out, final_norm_g, loss_target, m_norm_g, m_w_in, m_b_in, m_attn_sinks, m_sgu_ln_g, m_sgu_ln_b, m_sgu_w, m_sgu_b, m_w_out, m_b_out, m_final_norm_g, v_norm_g, v_w_in, v_b_in, v_attn_sinks, v_sgu_ln_g, v_sgu_ln_b, v_sgu_w, v_sgu_b, v_w_out, v_b_out, v_final_norm_g):
    given = dict(norm_g=norm_g, b_in=b_in, attn_sinks=attn_sinks, sgu_ln_g=sgu_ln_g, sgu_ln_b=sgu_ln_b,
                 sgu_w=sgu_w, sgu_b=sgu_b, b_out=b_out, final_norm_g=final_norm_g)
    m_given = dict(norm_g=m_norm_g, b_in=m_b_in, attn_sinks=m_attn_sinks, sgu_ln_g=m_sgu_ln_g,
                   sgu_ln_b=m_sgu_ln_b, sgu_w=m_sgu_w, sgu_b=m_sgu_b, b_out=m_b_out, final_norm_g=m_final_norm_g)
    v_given = dict(norm_g=v_norm_g, b_in=v_b_in, attn_sinks=v_attn_sinks, sgu_ln_g=v_sgu_ln_g,
                   sgu_ln_b=v_sgu_ln_b, sgu_w=v_sgu_w, sgu_b=v_sgu_b, b_out=v_b_out, final_norm_g=v_final_norm_g)

    win_t = _all_gather_win(w_in[0].T)
    grad_x, gwin_t, gwout, gsguw, vec = _local_grads(
        x[0], loss_target[0], win_t, w_out[0], norm_g, b_in, attn_sinks, sgu_ln_g, sgu_ln_b, sgu_w[0], sgu_b[0],
        b_out, final_norm_g.reshape(1, D_MODEL))

    t = lambda a: a[0].T
    d_win, nm_win, nv_win = _adamw_shard("adamw_w_in", gwin_t, t(w_in), t(m_w_in), t(v_w_in), WIN_ROWS // 2)
    d_wout, nm_wout, nv_wout = _adamw_shard("adamw_w_out", gwout, w_out[0], m_w_out[0], v_w_out[0], WOUT_ROWS)
    as_2d = lambda d: [d[name].reshape(shape) for name, _, shape in SMALL]
    loss = vec[VEC_LOSS, 0]
    small = _adamw_replicated(vec, gsguw, as_2d(given), as_2d(m_given), as_2d(v_given))

    def assemble(big_in, big_out, k):
        vals = {name: small[k][i].reshape(given[name].shape) for i, (name, _, _) in enumerate(SMALL)}
        vals["w_in"] = big_in.T[None]
        vals["w_out"] = big_out[None]
        order = ("norm_g", "w_in", "b_in", "attn_sinks", "sgu_ln_g", "sgu_ln_b", "sgu_w", "sgu_b", "w_out",
                 "b_out", "final_norm_g")
        return [vals[name] for name in order]

    return (loss, grad_x[None],
            *assemble(gwin_t, gwout, 0), *assemble(d_win, d_wout, 1),
            *assemble(nm_win, nm_wout, 2), *assemble(nv_win, nv_wout, 3))
```

```python
import functools
import math

import jax
import jax.numpy as jnp
from jax import lax
from jax.experimental import pallas as pl
from jax.experimental.pallas import tpu as pltpu

F32 = jnp.float32
BF16 = jnp.bfloat16
MXU_DTYPE = BF16
COMM_DTYPE = BF16

D_MODEL = 1024
SEQ = 4096
HEAD_DIM = 64
N_Q_HEADS = 8
Q_PER_KV = 4
BLOCK = 128
N_BLOCKS = SEQ // BLOCK
ATTN_W = 512
KV_W = 128
SGU_W = 512
N_SGU_HEADS = 8
IN_W = 2816
NORM_EPS = 1e-5
NEG_INF = -1e30
SCALE = HEAD_DIM ** -0.5
KV0 = ATTN_W
GATE0 = ATTN_W + 2 * KV_W
SGU0 = GATE0 + ATTN_W
ATTN_SECTION = SGU0
SGU_SECTION = IN_W - SGU0

ADAM_LR = 0.001
ADAM_B1 = 0.9
ADAM_B2 = 0.999
ADAM_EPS = 1e-08
ADAM_WD = 0.01
ADAM_STEP = 10

N_DEV = 8
WIN_ROWS = IN_W // N_DEV
WOUT_ROWS = D_MODEL // N_DEV
SGUW_ROWS = N_SGU_HEADS * BLOCK // N_DEV
VEC_ROWS = 16
MESH = pl.DeviceIdType.MESH

LANES = 128
HALF = LANES // 2
N_PAIRS = N_Q_HEADS * HEAD_DIM // LANES
KVX_W = 12 * LANES
TOKEN_TILE = 256
FWD_TOKEN_TILE = 512
SGU_CHUNKS_PER_STEP = 4
ATTN_FWD_AHEAD = 4
ATTN_BWD_AHEAD = 3
VMEM_LIMIT = 56 * 1024 * 1024

NN = (((1,), (0,)), ((), ()))
NT = (((1,), (1,)), ((), ()))
TN = (((0,), (0,)), ((), ()))


def _dot(a, b, dims=NN):
    return lax.dot_general(a.astype(MXU_DTYPE), b.astype(MXU_DTYPE), dims, preferred_element_type=F32)


def _gelu(x):
    return x * (lax.erf(x * (1.0 / math.sqrt(2.0))) + 1.0) * 0.5


def _gelu_grad(x):
    cdf = (lax.erf(x * (1.0 / math.sqrt(2.0))) + 1.0) * 0.5
    return cdf + x * jnp.exp(-0.5 * x * x) * (1.0 / math.sqrt(2.0 * math.pi))


def _silu_and_grad(z):
    s = jax.nn.sigmoid(z)
    return z * s, s * (1.0 + z * (1.0 - s))


def _params(semantics=None, vmem=None):
    kw = {}
    if semantics is not None:
        kw["dimension_semantics"] = semantics
    if vmem is not None:
        kw["vmem_limit_bytes"] = vmem
    return pltpu.CompilerParams(**kw)


def _full(shape):
    return pl.BlockSpec(shape, lambda *_: (0,) * len(shape))


VMEM_SPEC = pl.BlockSpec(memory_space=pltpu.VMEM)


RELATIONS = ((0, 0), (1, 0), (0, 1), (1, 1))


def _place():
    return lax.axis_index("x"), lax.axis_index("y"), lax.axis_index("c")


def _chip(rel):
    x, y, _ = _place()
    return (1 - x if rel[0] else x, 1 - y if rel[1] else y)


def _block_rows(place, n_rows):
    px, py, pc = place
    return pl.ds(pl.multiple_of((4 * px + 2 * py + pc) * n_rows, 16), n_rows)


class _Copies:
    def __init__(self, send_sems, recv_sems):
        self.send_sems, self.recv_sems = send_sems, recv_sems

    def __call__(self, k, src, dst, to):
        return pltpu.make_async_remote_copy(src_ref=src, dst_ref=dst, send_sem=self.send_sems.at[k],
                                            recv_sem=self.recv_sems.at[k], device_id=to, device_id_type=MESH)


def _gather_plan(copies, sem0, full_ref, n_rows):
    x, y, c = _place()
    me, sibling = (x, y, c), (x, y, 1 - c)
    chips = [_chip(rel) for rel in RELATIONS[1:]]

    def cp(k, block, to):
        rows = full_ref.at[_block_rows(block, n_rows), :]
        return copies(sem0 + k, rows, rows, to)

    first = [cp(0, me, sibling)] + [cp(1 + j, me, (*chip, c)) for j, chip in enumerate(chips)]
    passed = [cp(4 + j, (*chip, c), sibling) for j, chip in enumerate(chips)]

    def start():
        for f in first:
            f.start()

    def forward():
        for j, chip in enumerate(chips):
            cp(1 + j, (*chip, c), me).wait_recv()
            passed[j].start()

    def finish():
        cp(0, sibling, me).wait_recv()
        for j, chip in enumerate(chips):
            cp(4 + j, (*chip, 1 - c), me).wait_recv()
        for f in first + passed:
            f.wait_send()

    return start, forward, finish


GATHER_SEMS = 7


def _reduce_scatter_plan(copies, sem0, part_ref, n_rows, sa, ra, sb, rc, res_ref):
    x, y, c = _place()
    sibling = (x, y, 1 - c)
    n = n_rows
    level1 = copies(sem0, sa, ra, sibling)

    def level2(i):
        slot = pl.ds((i - 1) * n, n)
        return copies(sem0 + i, sb.at[slot, :], rc.at[slot, :], (*_chip(RELATIONS[i]), c))

    def start():
        for i, rel in enumerate(RELATIONS):
            sa[i * n:(i + 1) * n, :] = part_ref[_block_rows((*_chip(rel), 1 - c), n), :].astype(sa.dtype)
        level1.start()

    def exchange():
        level1.wait_recv()
        for i, rel in enumerate(RELATIONS):
            total = part_ref[_block_rows((*_chip(rel), c), n), :] + ra[i * n:(i + 1) * n, :].astype(F32)
            if i == 0:
                res_ref[...] = total
            else:
                sb[(i - 1) * n:i * n, :] = total.astype(sb.dtype)
                level2(i).start()

    def finish():
        acc = res_ref[...]
        for i in range(1, len(RELATIONS)):
            level2(i).wait_recv()
            acc = acc + rc[(i - 1) * n:i * n, :].astype(F32)
        res_ref[...] = acc
        level1.wait_send()
        for i in range(1, len(RELATIONS)):
            level2(i).wait_send()

    return start, exchange, finish


REDUCE_SEMS = 4


def _reduce_scatter_scratch(n_rows, width, dtype):
    return [pltpu.VMEM((4 * n_rows, width), dtype), pltpu.VMEM((4 * n_rows, width), dtype),
            pltpu.VMEM((3 * n_rows, width), dtype), pltpu.VMEM((3 * n_rows, width), dtype)]


def _dma_sems(n):
    return [pltpu.SemaphoreType.DMA((n,)), pltpu.SemaphoreType.DMA((n,))]


def _all_gather_win(win_t_shard):
    def body(win_ref, full_ref, send_sems, recv_sems):
        full_ref[_block_rows(_place(), WIN_ROWS), :] = win_ref[...].astype(COMM_DTYPE)
        start, forward, finish = _gather_plan(_Copies(send_sems, recv_sems), 0, full_ref, WIN_ROWS)
        start()
        forward()
        finish()

    return pl.pallas_call(
        body,
        name="all_gather_win",
        out_shape=jax.ShapeDtypeStruct((IN_W, D_MODEL), COMM_DTYPE),
        in_specs=[VMEM_SPEC],
        out_specs=VMEM_SPEC,
        scratch_shapes=_dma_sems(GATHER_SEMS),
        compiler_params=_params(vmem=VMEM_LIMIT),
    )(win_t_shard)


def _in_proj(x, norm_g, b_in, win_t, wout_shard):
    tm = FWD_TOKEN_TILE
    steps = SEQ // tm

    def body(x_ref, g_ref, b_ref, w_ref, wout_ref, h_ref, q_ref, kvx_ref, gate_ref, wfull_ref,
             landing, send_sems, recv_sems):
        step = pl.program_id(0)
        start, forward, finish = _gather_plan(_Copies(send_sems, recv_sems), 0, landing, WOUT_ROWS)

        @pl.when(step == 0)
        def _():
            landing[_block_rows(_place(), WOUT_ROWS), :] = wout_ref[...].astype(COMM_DTYPE)
            start()

        pl.when(step == steps // 2)(forward)

        xv = x_ref[...]
        r = lax.rsqrt(jnp.mean(xv * xv, axis=-1, keepdims=True) + NORM_EPS)
        h = ((xv * r) * g_ref[...]).astype(MXU_DTYPE)
        h_ref[...] = h

        def proj(lo, hi):
            return _dot(h, w_ref[lo:hi, :], NT) + b_ref[:, lo:hi]

        qs = proj(0, ATTN_W) * SCALE
        for pair in range(N_PAIRS):
            q_ref[pair] = qs[:, pair * LANES:(pair + 1) * LANES].astype(MXU_DTYPE)
        kv = proj(KV0, GATE0)
        low = lax.broadcasted_iota(jnp.int32, (tm, LANES), 1) < HALF
        for i in range(2):
            t = kv[:, i * LANES:(i + 1) * LANES]
            rot = pltpu.roll(t, HALF, 1)
            variants = (jnp.where(low, t, 0.0), jnp.where(low, 0.0, rot),
                        jnp.where(low, rot, 0.0), jnp.where(low, 0.0, t))
            for j, val in enumerate(variants):
                col = (4 * i + j) * LANES
                kvx_ref[:, col:col + LANES] = val.astype(MXU_DTYPE)
                if i == 1:
                    ones_elsewhere = jnp.where(low == (j % 2 == 0), val, 1.0)
                    kvx_ref[:, col + 4 * LANES:col + 5 * LANES] = ones_elsewhere.astype(MXU_DTYPE)
        for k in range(4):
            gate_ref[k] = proj(GATE0 + k * SGU_W, GATE0 + (k + 1) * SGU_W)

        @pl.when(step == steps - 1)
        def _():
            finish()
            wfull_ref[...] = landing[...]

    return pl.pallas_call(
        body,
        name="in_proj",
        grid=(steps,),
        in_specs=[pl.BlockSpec((tm, D_MODEL), lambda i: (i, 0)),
                  _full((1, D_MODEL)), _full((1, IN_W)), _full((IN_W, D_MODEL)), VMEM_SPEC],
        out_specs=(pl.BlockSpec((tm, D_MODEL), lambda i: (i, 0)),
                   pl.BlockSpec((N_PAIRS, tm, LANES), lambda i: (0, i, 0)),
                   pl.BlockSpec((tm, KVX_W), lambda i: (i, 0)),
                   pl.BlockSpec((4, tm, SGU_W), lambda i: (0, i, 0)),
                   _full((D_MODEL, D_MODEL))),
        out_shape=(jax.ShapeDtypeStruct((SEQ, D_MODEL), MXU_DTYPE),
                   jax.ShapeDtypeStruct((N_PAIRS, SEQ, LANES), MXU_DTYPE),
                   jax.ShapeDtypeStruct((SEQ, KVX_W), MXU_DTYPE),
                   jax.ShapeDtypeStruct((4, SEQ, SGU_W), F32),
                   jax.ShapeDtypeStruct((D_MODEL, D_MODEL), COMM_DTYPE)),
        scratch_shapes=[pltpu.VMEM((D_MODEL, D_MODEL), COMM_DTYPE)] + _dma_sems(GATHER_SEMS),
        compiler_params=_params(("arbitrary",), VMEM_LIMIT),
    )(x, norm_g, b_in, win_t, wout_shard)


def _window_mask(n):
    qi = lax.broadcasted_iota(jnp.int32, (2 * BLOCK, 2 * BLOCK), 0) & (BLOCK - 1)
    p = lax.broadcasted_iota(jnp.int32, (2 * BLOCK, 2 * BLOCK), 1) - BLOCK
    in_window = jnp.logical_and(p <= qi, p > qi - BLOCK)
    return jnp.logical_and(in_window, jnp.logical_or(p >= 0, n > 0))


def _sink_column(sink_ref, g, par):
    return jnp.concatenate([jnp.full((BLOCK, 1), sink_ref[4 * g + par], F32),
                            jnp.full((BLOCK, 1), sink_ref[4 * g + 2 + par], F32)], axis=0)


def _kv_cat(kp_ref, kc_ref, var, with_ones):
    kcol, vcol = var * LANES, (var + (8 if with_ones else 4)) * LANES
    return (jnp.concatenate([kp_ref[:, kcol:kcol + LANES], kc_ref[:, kcol:kcol + LANES]], axis=0),
            jnp.concatenate([kp_ref[:, vcol:vcol + LANES], kc_ref[:, vcol:vcol + LANES]], axis=0))


def _softmax_numerator(s, sink):
    m = jnp.maximum(jnp.max(s, axis=1, keepdims=True), sink)
    return jnp.exp(s - m), m


def _attn_fwd(sinks, q, kvx, gates):
    def body(sink_ref, q_ref, kc_ref, za_ref, out_ref, ag_ref, kp_ref):
        @pl.when(pl.program_id(0) == 0)
        def _():
            kp_ref[...] = jnp.zeros_like(kp_ref)

        valid = _window_mask(pl.program_id(0))[0:BLOCK]
        chains = [(g, par, i) for g in range(2) for par in range(2) for i in range(2)]
        kv = {(g, par): _kv_cat(kp_ref, kc_ref, 2 * g + par, True) for g in range(2) for par in range(2)}
        scores, outs = {}, {}

        def issue_scores(k):
            g, par, i = chains[k]
            scores[k] = _dot(q_ref[2 * g + i], kv[g, par][0], NT)

        ahead = ATTN_FWD_AHEAD
        for k in range(ahead):
            issue_scores(k)
        low = lax.broadcasted_iota(jnp.int32, (BLOCK, LANES), 1) < HALF
        for k, (g, par, i) in enumerate(chains):
            sink = sink_ref[4 * g + 2 * i + par]
            e, m = _softmax_numerator(jnp.where(valid, scores[k], NEG_INF), sink)
            if k + ahead < len(chains):
                issue_scores(k + ahead)
            o = _dot(e, kv[g, par][1])
            outs[g, par, i] = o / (pltpu.roll(o, HALF, 1) + jnp.exp(sink - m))
        for pair in range(N_PAIRS):
            g, i = divmod(pair, 2)
            o = jnp.where(low, outs[g, 0, i], outs[g, 1, i])
            out_ref[pair] = o
            gate, _ = _silu_and_grad(za_ref[:, pair * LANES:(pair + 1) * LANES])
            ag_ref[:, pair * LANES:(pair + 1) * LANES] = (o * gate).astype(MXU_DTYPE)
        kp_ref[...] = kc_ref[...]

    blk = lambda w: pl.BlockSpec((BLOCK, w), lambda n: (n, 0))
    tiles = pl.BlockSpec((N_PAIRS, BLOCK, LANES), lambda n: (0, n, 0))
    return pl.pallas_call(
        body,
        name="attn_fwd",
        grid=(N_BLOCKS,),
        in_specs=[pl.BlockSpec(memory_space=pltpu.SMEM), tiles, blk(KVX_W),
                  pl.BlockSpec((None, BLOCK, ATTN_W), lambda n: (0, n, 0))],
        out_specs=(tiles, blk(ATTN_W)),
        out_shape=(jax.ShapeDtypeStruct((N_PAIRS, SEQ, LANES), F32),
                   jax.ShapeDtypeStruct((SEQ, ATTN_W), MXU_DTYPE)),
        scratch_shapes=[pltpu.VMEM((BLOCK, KVX_W), MXU_DTYPE)],
        compiler_params=_params(("arbitrary",)),
    )(sinks, q, kvx, gates)


def _sgu_forward_chunk(us, vs, lng, lnb, w_ref, bias_ref):
    u = _gelu(us)
    vg = _gelu(vs)
    mu = jnp.mean(vg, axis=-1, keepdims=True)
    xc = vg - mu
    rstd = lax.rsqrt(jnp.mean(xc * xc, axis=-1, keepdims=True) + NORM_EPS)
    vhat = xc * rstd
    vln = vhat * lng + lnb
    low = lax.broadcasted_iota(jnp.int32, (BLOCK, LANES), 1) < HALF
    tril = (lax.broadcasted_iota(jnp.int32, (BLOCK, BLOCK), 0)
            >= lax.broadcasted_iota(jnp.int32, (BLOCK, BLOCK), 1))
    mixed = []
    for pair in range(N_SGU_HEADS // 2):
        vp = vln[:, pair * LANES:(pair + 1) * LANES]
        w0 = jnp.where(tril, w_ref[2 * pair], 0.0)
        w1 = jnp.where(tril, w_ref[2 * pair + 1], 0.0)
        mixed.append(_dot(w0, jnp.where(low, vp, 0.0)) + _dot(w1, jnp.where(low, 0.0, vp))
                     + bias_ref[:, pair * LANES:(pair + 1) * LANES])
    return u, vhat, rstd, vln, mixed


def _sgu_fwd(gates, ln_g, ln_b, sgu_w, bias_full):
    rows = SGU_CHUNKS_PER_STEP * BLOCK

    def body(us_ref, vs_ref, zs_ref, lng_ref, lnb_ref, w_ref, bias_ref, sg_ref):
        def chunk(c, carry):
            at = pl.ds(pl.multiple_of(c * BLOCK, BLOCK), BLOCK)
            u, _, _, _, mixed = _sgu_forward_chunk(us_ref[at, :], vs_ref[at, :], lng_ref[...], lnb_ref[...],
                                                   w_ref, bias_ref)
            for pair in range(N_SGU_HEADS // 2):
                cols = slice(pair * LANES, (pair + 1) * LANES)
                gate, _ = _silu_and_grad(zs_ref[at, cols])
                sg_ref[at, cols] = (u[:, cols] * mixed[pair] * gate).astype(MXU_DTYPE)
            return carry

        lax.fori_loop(0, SGU_CHUNKS_PER_STEP, chunk, 0)

    col = lambda k: pl.BlockSpec((None, rows, SGU_W), lambda n: (k, n, 0))
    return pl.pallas_call(
        body,
        name="sgu_fwd",
        grid=(SEQ // rows,),
        in_specs=[col(1), col(2), col(3), _full((1, SGU_W)), _full((1, SGU_W)),
                  _full((N_SGU_HEADS, BLOCK, BLOCK)), _full((BLOCK, SGU_W))],
        out_specs=pl.BlockSpec((rows, SGU_W), lambda n: (n, 0)),
        out_shape=jax.ShapeDtypeStruct((SEQ, SGU_W), MXU_DTYPE),
        compiler_params=_params(("arbitrary",)),
    )(gates, gates, gates, ln_g, ln_b, sgu_w, bias_full)


def _out_proj_loss(ag, sg, x, target, wout, b_out, final_g):
    tm = FWD_TOKEN_TILE

    def body(ag_ref, sg_ref, x_ref, t_ref, w_ref, b_ref, gf_ref, gres_ref, dmix_ref, gw_ref, vec_ref):
        @pl.when(pl.program_id(0) == 0)
        def _():
            gw_ref[...] = jnp.zeros_like(gw_ref)
            vec_ref[...] = jnp.zeros_like(vec_ref)

        a = ag_ref[...]
        s = sg_ref[...]
        xo = x_ref[...] + (_dot(a, w_ref[0:ATTN_W, :]) + _dot(s, w_ref[ATTN_W:, :])) + b_ref[...]
        r = lax.rsqrt(jnp.mean(xo * xo, axis=-1, keepdims=True) + NORM_EPS)
        xn = xo * r
        gf = gf_ref[...]
        err = xn * gf - t_ref[...]
        loss = 0.5 * jnp.sum(jnp.mean(err * err, axis=-1, keepdims=True), axis=0, keepdims=True)
        dy = err * (1.0 / D_MODEL)
        dxn = dy * gf
        gres = r * (dxn - xn * jnp.mean(dxn * xn, axis=-1, keepdims=True))
        vec_ref[0:1, :] += jnp.broadcast_to(loss, (1, D_MODEL))
        vec_ref[1:2, :] += jnp.sum(dy * xn, axis=0, keepdims=True)
        vec_ref[2:3, :] += jnp.sum(gres, axis=0, keepdims=True)
        gres_ref[...] = gres
        gb = gres.astype(MXU_DTYPE)
        dmix_ref[0] = _dot(gb, w_ref[0:ATTN_W, :], NT)
        dmix_ref[1] = _dot(gb, w_ref[ATTN_W:, :], NT)
        gw_ref[0:ATTN_W, :] += _dot(a, gb, TN)
        gw_ref[ATTN_W:, :] += _dot(s, gb, TN)

    tile = lambda w: pl.BlockSpec((tm, w), lambda i: (i, 0))
    return pl.pallas_call(
        body,
        name="out_proj_loss",
        grid=(SEQ // tm,),
        in_specs=[tile(ATTN_W), tile(SGU_W), tile(D_MODEL), tile(D_MODEL),
                  _full((D_MODEL, D_MODEL)), _full((1, D_MODEL)), _full((1, D_MODEL))],
        out_specs=(tile(D_MODEL), pl.BlockSpec((2, tm, ATTN_W), lambda i: (0, i, 0)), _full((D_MODEL, D_MODEL)),
                   _full((8, D_MODEL))),
        out_shape=(jax.ShapeDtypeStruct((SEQ, D_MODEL), F32),
                   jax.ShapeDtypeStruct((2, SEQ, ATTN_W), F32),
                   jax.ShapeDtypeStruct((D_MODEL, D_MODEL), F32),
                   jax.ShapeDtypeStruct((8, D_MODEL), F32)),
        compiler_params=_params(("arbitrary",), VMEM_LIMIT),
    )(ag, sg, x, target, wout, b_out, final_g)


def _sgu_bwd(dmix, gates, ln_g, ln_b, sgu_w, bias_full):
    last = N_BLOCKS - 1

    def body(d_ref, us_ref, vs_ref, zs_ref, lng_ref, lnb_ref, w_ref, bias_ref,
             dp_ref, gw_ref, gb_ref, gln_ref, gbin_ref, wt_ref, gbias_ref):
        c = pl.program_id(0)
        tril = (lax.broadcasted_iota(jnp.int32, (BLOCK, BLOCK), 0)
                >= lax.broadcasted_iota(jnp.int32, (BLOCK, BLOCK), 1))

        @pl.when(c == 0)
        def _():
            gw_ref[...] = jnp.zeros_like(gw_ref)
            gln_ref[...] = jnp.zeros_like(gln_ref)
            gbin_ref[...] = jnp.zeros_like(gbin_ref)
            gbias_ref[...] = jnp.zeros_like(gbias_ref)
            for hh in range(N_SGU_HEADS):
                wt_ref[hh] = jnp.where(tril, w_ref[hh], 0.0).T.astype(MXU_DTYPE)

        us = us_ref[...]
        vs = vs_ref[...]
        lng = lng_ref[...]
        u, vhat, rstd, vln, mixed = _sgu_forward_chunk(us, vs, lng, lnb_ref[...], w_ref, bias_ref)
        low = lax.broadcasted_iota(jnp.int32, (BLOCK, LANES), 1) < HALF
        du_parts, dzs_parts, dvln_parts = [], [], []
        for pair in range(N_SGU_HEADS // 2):
            cols = slice(pair * LANES, (pair + 1) * LANES)
            dsg = d_ref[:, cols]
            gate, gate_grad = _silu_and_grad(zs_ref[:, cols])
            up = u[:, cols]
            du_parts.append(dsg * mixed[pair] * gate)
            dzs_parts.append(dsg * up * mixed[pair] * gate_grad)
            dmixed = dsg * up * gate
            gbias_ref[:, cols] += dmixed
            dm_lo = jnp.where(low, dmixed, 0.0)
            dm_hi = jnp.where(low, 0.0, dmixed)
            vp = vln[:, cols]
            gw_ref[2 * pair] += _dot(dm_lo, vp, NT)
            gw_ref[2 * pair + 1] += _dot(dm_hi, vp, NT)
            dvln_parts.append(_dot(wt_ref[2 * pair], dm_lo) + _dot(wt_ref[2 * pair + 1], dm_hi))
        dvln = jnp.concatenate(dvln_parts, axis=1)
        gln_ref[0:1, :] += jnp.sum(dvln * vhat, axis=0, keepdims=True)
        gln_ref[1:2, :] += jnp.sum(dvln, axis=0, keepdims=True)
        dvhat = dvln * lng
        dvg = rstd * (dvhat - jnp.mean(dvhat, axis=-1, keepdims=True)
                      - vhat * jnp.mean(dvhat * vhat, axis=-1, keepdims=True))
        dus = jnp.concatenate(du_parts, axis=1) * _gelu_grad(us)
        dvs = dvg * _gelu_grad(vs)
        dzs = jnp.concatenate(dzs_parts, axis=1)
        for k, val in enumerate((dus, dvs, dzs)):
            dp_ref[:, k * SGU_W:(k + 1) * SGU_W] = val.astype(MXU_DTYPE)
            gbin_ref[:, k * SGU_W:(k + 1) * SGU_W] += jnp.sum(val, axis=0, keepdims=True)

        @pl.when(c == last)
        def _():
            for hh in range(N_SGU_HEADS):
                gw_ref[hh] = jnp.where(tril, gw_ref[hh], 0.0)
            head_of_lane = lax.broadcasted_iota(jnp.int32, (N_SGU_HEADS, SGU_W), 1) // HEAD_DIM
            select = (head_of_lane == lax.broadcasted_iota(jnp.int32, (N_SGU_HEADS, SGU_W), 0)).astype(F32)
            gb_ref[...] = lax.dot_general(select, gbias_ref[...], NT, precision=lax.Precision.HIGHEST,
                                          preferred_element_type=F32)

    col = lambda k: pl.BlockSpec((None, BLOCK, SGU_W), lambda n: (k, n, 0))
    return pl.pallas_call(
        body,
        name="sgu_bwd",
        grid=(N_BLOCKS,),
        in_specs=[col(1), col(1), col(2), col(3), _full((1, SGU_W)), _full((1, SGU_W)),
                  _full((N_SGU_HEADS, BLOCK, BLOCK)), _full((BLOCK, SGU_W))],
        out_specs=(pl.BlockSpec((BLOCK, SGU_SECTION), lambda n: (n, 0)),
                   _full((N_SGU_HEADS, BLOCK, BLOCK)), _full((N_SGU_HEADS, BLOCK)),
                   _full((8, SGU_W)), _full((1, SGU_SECTION))),
        out_shape=(jax.ShapeDtypeStruct((SEQ, SGU_SECTION), MXU_DTYPE),
                   jax.ShapeDtypeStruct((N_SGU_HEADS, BLOCK, BLOCK), F32),
                   jax.ShapeDtypeStruct((N_SGU_HEADS, BLOCK), F32),
                   jax.ShapeDtypeStruct((8, SGU_W), F32),
                   jax.ShapeDtypeStruct((1, SGU_SECTION), F32)),
        scratch_shapes=[pltpu.VMEM((N_SGU_HEADS, BLOCK, BLOCK), MXU_DTYPE),
                        pltpu.VMEM((BLOCK, SGU_W), F32)],
        compiler_params=_params(("arbitrary",)),
    )(dmix, gates, gates, gates, ln_g, ln_b, sgu_w, bias_full)


def _attn_bwd(sinks, dmix, q, kvx, out, gates, gwout, gsguw):
    last = N_BLOCKS - 1

    def body(sink_ref, d_ref, q_ref, kc_ref, o_ref, za_ref, gwout_ref, gsguw_ref,
             dp_ref, gsink_ref, gbin_ref, wout_shard_ref, sguw_full_ref,
             kp_ref, pend_ref, carry_ref, sa_w, ra_w, sb_w, rc_w, sa_s, ra_s, sb_s, rc_s, landing, send_sems, recv_sems):
        n = pl.program_id(0)
        copies = _Copies(send_sems, recv_sems)
        own_sguw = landing.at[_block_rows(_place(), SGUW_ROWS), :]
        plans = [_reduce_scatter_plan(copies, 0, gwout_ref, WOUT_ROWS, sa_w, ra_w, sb_w, rc_w, wout_shard_ref),
                 _reduce_scatter_plan(copies, REDUCE_SEMS, gsguw_ref, SGUW_ROWS, sa_s, ra_s, sb_s, rc_s, own_sguw)]
        gather = _gather_plan(copies, 2 * REDUCE_SEMS, landing, SGUW_ROWS)

        @pl.when(n == 0)
        def _():
            gsink_ref[...] = jnp.zeros_like(gsink_ref)
            gbin_ref[...] = jnp.zeros_like(gbin_ref)
            carry_ref[...] = jnp.zeros_like(carry_ref)
            kp_ref[...] = jnp.zeros_like(kp_ref)
            for start, _, _ in plans:
                start()

        @pl.when(n == 3)
        def _():
            for _, exchange, _ in plans:
                exchange()

        @pl.when(n == 12)
        def _():
            for _, _, finish in plans:
                finish()
            gather[0]()

        pl.when(n == 16)(gather[1])

        @pl.when(n == last + 1)
        def _():
            gather[2]()
            sguw_full_ref[...] = landing[...]

        @pl.when(n > 0)
        def _():
            dp_ref[:, 0:ATTN_W] = pend_ref[:, 0:ATTN_W]
            dp_ref[:, GATE0:ATTN_SECTION] = pend_ref[:, ATTN_W:]

        @pl.when(n > last)
        def _():
            dp_ref[:, KV0:GATE0] = carry_ref[...].astype(MXU_DTYPE)

        @pl.when(n <= last)
        def _():
            valid = _window_mask(n)[0:BLOCK]
            low = lax.broadcasted_iota(jnp.int32, (BLOCK, LANES), 1) < HALF
            low_keys = lax.broadcasted_iota(jnp.int32, (2 * BLOCK, LANES), 1) < HALF
            lane_row = lax.broadcasted_iota(jnp.int32, (1, LANES), 1)
            gsink = jnp.zeros((1, LANES), F32)
            chains = [(g, par, i) for g in range(2) for par in range(2) for i in range(2)]
            kv = {(g, par): _kv_cat(kp_ref, kc_ref, 2 * g + par, False) for g in range(2) for par in range(2)}
            ones_keys = jnp.ones((2 * BLOCK, LANES), MXU_DTYPE)
            half_of_lane = lax.broadcasted_iota(jnp.int32, (LANES, 2 * LANES), 0) // HALF
            half_of_col = lax.broadcasted_iota(jnp.int32, (LANES, 2 * LANES), 1) // LANES
            sum_halves = (half_of_lane == half_of_col).astype(MXU_DTYPE)
            douts, deltas = [], []
            for pair in range(N_PAIRS):
                lanes = slice(pair * LANES, (pair + 1) * LANES)
                dg = d_ref[:, lanes]
                gate, gate_grad = _silu_and_grad(za_ref[:, lanes])
                o = o_ref[pair]
                dout = dg * gate
                dza = dg * o * gate_grad
                douts.append(dout.astype(MXU_DTYPE))
                deltas.append(_dot(dout * o, sum_halves))
                zl = slice(ATTN_W + pair * LANES, ATTN_W + (pair + 1) * LANES)
                pend_ref[:, zl] = dza.astype(MXU_DTYPE)
                gl = slice(GATE0 + pair * LANES, GATE0 + (pair + 1) * LANES)
                gbin_ref[:, gl] += jnp.sum(dza, axis=0, keepdims=True)

            first = {}

            def issue_first(k):
                g, par, i = chains[k]
                first[k] = (_dot(q_ref[2 * g + i], kv[g, par][0], NT), _dot(douts[2 * g + i], kv[g, par][1], NT))

            numerators = {}

            def issue_row_sums(k):
                g, par, i = chains[k]
                sink = sink_ref[4 * g + 2 * i + par]
                e, m = _softmax_numerator(jnp.where(valid, first[k][0], NEG_INF), sink)
                numerators[k] = (e, jnp.exp(sink - m), _dot(e, ones_keys))

            ahead = ATTN_BWD_AHEAD
            for k in range(ahead):
                issue_first(k)
            issue_row_sums(0)
            issue_row_sums(1)
            dqs, dk_parts, dv_parts = {}, {}, {}
            operands = {}

            def issue_last(k):
                g, par, i = chains[k]
                ds, ds_t, p_t = operands.pop(k)
                dq = _dot(ds, kv[g, par][0])
                dqs[g, i] = dq if par == 0 else dqs[g, i] + dq
                dk = _dot(ds_t, q_ref[2 * g + i])
                dv = _dot(p_t, douts[2 * g + i])
                dk_parts[g, par] = dk if i == 0 else dk_parts[g, par] + dk
                dv_parts[g, par] = dv if i == 0 else dv_parts[g, par] + dv

            for k, (g, par, i) in enumerate(chains):
                h = 4 * g + 2 * i + par
                delta = deltas[2 * g + i][:, par * LANES:(par + 1) * LANES]
                e, at_sink, row_sum = numerators[k]
                inv = 1.0 / (row_sum + at_sink)
                p = e * jnp.tile(inv, (1, 2))
                ds = p * (first[k][1] - jnp.tile(delta, (1, 2)))
                operands[k] = (ds.astype(MXU_DTYPE), ds.T.astype(MXU_DTYPE), p.T.astype(MXU_DTYPE))
                total = jnp.sum(at_sink * inv * delta, axis=0, keepdims=True)
                gsink = jnp.where(lane_row == h, -total, gsink)
                if k + ahead < len(chains):
                    issue_first(k + ahead)
                if k + 2 < len(chains):
                    issue_row_sums(k + 2)
                if k > 0:
                    issue_last(k - 1)
            issue_last(len(chains) - 1)
            for pair in range(N_PAIRS):
                g, i = divmod(pair, 2)
                dq = dqs[g, i] * SCALE
                lanes = slice(pair * LANES, (pair + 1) * LANES)
                pend_ref[:, lanes] = dq.astype(MXU_DTYPE)
                gbin_ref[:, lanes] += jnp.sum(dq, axis=0, keepdims=True)
            gsink_ref[...] += gsink
            for k, parts in enumerate((dk_parts, dv_parts)):
                masked = {key: jnp.where(low_keys if key[1] == 0 else jnp.logical_not(low_keys), val, 0.0)
                          for key, val in parts.items()}
                both = (masked[0, 0] + masked[1, 1]
                        + pltpu.roll(masked[0, 1] + masked[1, 0], HALF, 1))
                lanes = slice(k * KV_W, (k + 1) * KV_W)
                done = carry_ref[:, lanes] + both[0:BLOCK]
                dp_ref[:, KV0 + k * KV_W:KV0 + (k + 1) * KV_W] = done.astype(MXU_DTYPE)
                carry_ref[:, lanes] = both[BLOCK:]
                gbin_ref[:, KV0 + k * KV_W:KV0 + (k + 1) * KV_W] += jnp.sum(both, axis=0, keepdims=True)
            kp_ref[...] = kc_ref[...]

    at = lambda n: jnp.minimum(n, last)
    blk = lambda w: pl.BlockSpec((BLOCK, w), lambda n: (at(n), 0))
    tiles = pl.BlockSpec((N_PAIRS, BLOCK, LANES), lambda n: (0, at(n), 0))
    return pl.pallas_call(
        body,
        name="attn_bwd",
        grid=(N_BLOCKS + 1,),
        in_specs=[pl.BlockSpec(memory_space=pltpu.SMEM),
                  pl.BlockSpec((None, BLOCK, ATTN_W), lambda n: (0, at(n), 0)),
                  tiles,
                  blk(KVX_W),
                  tiles,
                  pl.BlockSpec((None, BLOCK, ATTN_W), lambda n: (0, at(n), 0)),
                  VMEM_SPEC, VMEM_SPEC],
        out_specs=(pl.BlockSpec((BLOCK, ATTN_SECTION), lambda n: (jnp.maximum(n - 1, 0), 0)),
                   _full((1, LANES)), _full((1, ATTN_SECTION)), VMEM_SPEC, _full((N_SGU_HEADS * BLOCK, BLOCK))),
        out_shape=(jax.ShapeDtypeStruct((SEQ, ATTN_SECTION), MXU_DTYPE),
                   jax.ShapeDtypeStruct((1, LANES), F32),
                   jax.ShapeDtypeStruct((1, ATTN_SECTION), F32),
                   jax.ShapeDtypeStruct((WOUT_ROWS, D_MODEL), F32),
                   jax.ShapeDtypeStruct((N_SGU_HEADS * BLOCK, BLOCK), F32)),
        scratch_shapes=([pltpu.VMEM((BLOCK, KVX_W), MXU_DTYPE),
                         pltpu.VMEM((BLOCK, 2 * ATTN_W), MXU_DTYPE), pltpu.VMEM((BLOCK, 2 * KV_W), F32)]
                        + _reduce_scatter_scratch(WOUT_ROWS, D_MODEL, COMM_DTYPE)
                        + _reduce_scatter_scratch(SGUW_ROWS, BLOCK, F32)
                        + [pltpu.VMEM((N_SGU_HEADS * BLOCK, BLOCK), F32)]
                        + _dma_sems(2 * REDUCE_SEMS + GATHER_SEMS)),
        compiler_params=_params(("arbitrary",), VMEM_LIMIT),
    )(sinks, dmix, q, kvx, out, gates, gwout, gsguw)


def _in_proj_bwd(dpa, dps, win_t, x, norm_g, gres, gwin, vec_parts):
    tm = TOKEN_TILE
    steps = SEQ // tm
    n_parts = len(vec_parts)

    def body(da_ref, ds_ref, w_ref, x_ref, g_ref, gres_ref, gwin_ref, *rest):
        part_refs = rest[:n_parts]
        gx_ref, shard_ref, vec_out_ref, gng_ref, sa, ra, sb, rc, vec_ref, ra_vec, slots, send_sems, recv_sems = (
            rest[n_parts:])
        step = pl.program_id(0)
        copies = _Copies(send_sems, recv_sems)
        start, exchange, finish = _reduce_scatter_plan(copies, 0, gwin_ref, WIN_ROWS, sa, ra, sb, rc, shard_ref)

        @pl.when(step == 0)
        def _():
            gng_ref[...] = jnp.zeros_like(gng_ref)
            start()

        pl.when(step == 2)(exchange)

        dh = _dot(da_ref[...], w_ref[0:ATTN_SECTION, :]) + _dot(ds_ref[...], w_ref[ATTN_SECTION:, :])
        xv = x_ref[...]
        r = lax.rsqrt(jnp.mean(xv * xv, axis=-1, keepdims=True) + NORM_EPS)
        xn = xv * r
        gng_ref[...] += jnp.sum(dh * xn, axis=0, keepdims=True)
        dxn = dh * g_ref[...]
        gx_ref[...] = r * (dxn - xn * jnp.mean(dxn * xn, axis=-1, keepdims=True)) + gres_ref[...]

        @pl.when(step == steps - 1)
        def _():
            finish()
            _all_reduce_vectors(copies, REDUCE_SEMS, gng_ref, *part_refs, vec_out_ref, vec_ref, ra_vec, slots)

    tile = lambda w: pl.BlockSpec((tm, w), lambda i: (i, 0))
    grad_x, shard, vec = pl.pallas_call(
        body,
        name="in_proj_bwd",
        grid=(steps,),
        in_specs=[tile(ATTN_SECTION), tile(SGU_SECTION), _full((IN_W, D_MODEL)), tile(D_MODEL),
                  _full((1, D_MODEL)), tile(D_MODEL), VMEM_SPEC] + [VMEM_SPEC] * n_parts,
        out_specs=(tile(D_MODEL), VMEM_SPEC, VMEM_SPEC),
        out_shape=(jax.ShapeDtypeStruct((SEQ, D_MODEL), F32),
                   jax.ShapeDtypeStruct((WIN_ROWS, D_MODEL), F32),
                   jax.ShapeDtypeStruct((VEC_ROWS, IN_W), F32)),
        scratch_shapes=([pltpu.VMEM((1, D_MODEL), F32)] + _reduce_scatter_scratch(WIN_ROWS, D_MODEL, COMM_DTYPE)
                        + _vector_scratch() + _dma_sems(REDUCE_SEMS + VECTOR_SEMS)),
        compiler_params=_params(("arbitrary",), VMEM_LIMIT),
    )(dpa, dps, win_t, x, norm_g, gres, gwin, *vec_parts)
    return grad_x, shard, vec


def _win_grad(dpa, dps, h):
    rows = 256
    n_attn = ATTN_SECTION // rows
    steps = n_attn + SGU_SECTION // rows

    def body(da_ref, ds_ref, h_ref, o_ref):
        step = pl.program_id(0)

        @pl.when(step < n_attn)
        def _():
            o_ref[...] = _dot(da_ref[...], h_ref[...], TN)

        @pl.when(step >= n_attn)
        def _():
            o_ref[...] = _dot(ds_ref[...], h_ref[...], TN)

    return pl.pallas_call(
        body,
        name="win_grad",
        grid=(steps,),
        in_specs=[pl.BlockSpec((SEQ, rows), lambda i: (0, jnp.minimum(i, n_attn - 1))),
                  pl.BlockSpec((SEQ, rows), lambda i: (0, jnp.maximum(i - n_attn, 0))),
                  _full((SEQ, D_MODEL))],
        out_specs=pl.BlockSpec((rows, D_MODEL), lambda i: (i, 0)),
        out_shape=jax.ShapeDtypeStruct((IN_W, D_MODEL), F32),
        compiler_params=_params(("arbitrary",), VMEM_LIMIT),
    )(dpa, dps, h)


VEC_NORM_G, VEC_B_IN, VEC_SINKS, VEC_LN_G, VEC_LN_B, VEC_B_OUT, VEC_FINAL_G, VEC_LOSS, VEC_SGU_B = 0, 1, 2, 3, 4, 5, 6, 7, 8


def _adamw(w, g, m, v):
    m = ADAM_B1 * m + (1.0 - ADAM_B1) * g
    v = ADAM_B2 * v + (1.0 - ADAM_B2) * (g * g)
    m_hat = m / (1.0 - ADAM_B1 ** ADAM_STEP)
    v_hat = v / (1.0 - ADAM_B2 ** ADAM_STEP)
    delta = -ADAM_LR * (m_hat / (jnp.sqrt(v_hat) + ADAM_EPS) + ADAM_WD * w)
    return delta, m, v


def _adamw_shard(name, g, w, m, v, block_rows):
    def body(g_ref, w_ref, m_ref, v_ref, d_ref, nm_ref, nv_ref):
        d_ref[...], nm_ref[...], nv_ref[...] = _adamw(w_ref[...], g_ref[...], m_ref[...], v_ref[...])

    rows, cols = w.shape
    spec = pl.BlockSpec((block_rows, cols), lambda i: (i, 0))
    return pl.pallas_call(
        body,
        name=name,
        grid=(rows // block_rows,),
        in_specs=[spec] * 4,
        out_specs=(spec,) * 3,
        out_shape=(jax.ShapeDtypeStruct(w.shape, F32),) * 3,
        compiler_params=_params(("arbitrary",)),
    )(g, w, m, v)


VECTOR_SEMS = 4


def _vector_scratch():
    return [pltpu.VMEM((VEC_ROWS, IN_W), F32), pltpu.VMEM((VEC_ROWS, IN_W), F32),
            pltpu.VMEM((4 * VEC_ROWS, IN_W), F32)]


def _all_reduce_vectors(copies, sem0, gng_ref, gba_ref, gbs_ref, gsink_ref, gln_ref, gsgub_ref, vec4_ref, out_ref,
                        vec_ref, ra_vec, slots):
    x, y, c = _place()
    vec_ref[...] = jnp.zeros_like(vec_ref)
    vec_ref[VEC_NORM_G:VEC_NORM_G + 1, 0:D_MODEL] = gng_ref[...]
    vec_ref[VEC_B_IN:VEC_B_IN + 1, 0:ATTN_SECTION] = gba_ref[...]
    vec_ref[VEC_B_IN:VEC_B_IN + 1, ATTN_SECTION:IN_W] = gbs_ref[...]
    vec_ref[VEC_SINKS:VEC_SINKS + 1, 0:LANES] = gsink_ref[...]
    vec_ref[VEC_LN_G:VEC_LN_G + 1, 0:SGU_W] = gln_ref[0:1, :]
    vec_ref[VEC_LN_B:VEC_LN_B + 1, 0:SGU_W] = gln_ref[1:2, :]
    vec_ref[VEC_B_OUT:VEC_B_OUT + 1, 0:D_MODEL] = vec4_ref[2:3, :]
    vec_ref[VEC_FINAL_G:VEC_FINAL_G + 1, 0:D_MODEL] = vec4_ref[1:2, :]
    vec_ref[VEC_LOSS:VEC_LOSS + 1, 0:D_MODEL] = vec4_ref[0:1, :]
    vec_ref[VEC_SGU_B:VEC_SGU_B + N_SGU_HEADS, 0:BLOCK] = gsgub_ref[...]

    to_sibling = copies(sem0, vec_ref, ra_vec, (x, y, 1 - c))
    to_sibling.start()
    to_sibling.wait_recv()

    def chip_slot(place):
        return slots.at[pl.ds(pl.multiple_of((2 * place[0] + place[1]) * VEC_ROWS, 8), VEC_ROWS), :]

    mine = chip_slot((x, y))
    mine[...] = vec_ref[...] + ra_vec[...]
    to_chips = [copies(sem0 + i, mine, mine, (*_chip(rel), c)) for i, rel in enumerate(RELATIONS[1:], start=1)]
    for cp in to_chips:
        cp.start()
    for i, rel in enumerate(RELATIONS[1:], start=1):
        theirs = chip_slot(_chip(rel))
        copies(sem0 + i, theirs, theirs, (x, y, c)).wait_recv()
    out_ref[...] = ((slots[0:VEC_ROWS, :] + slots[VEC_ROWS:2 * VEC_ROWS, :])
                    + slots[2 * VEC_ROWS:3 * VEC_ROWS, :]) + slots[3 * VEC_ROWS:, :]
    to_sibling.wait_send()
    for cp in to_chips:
        cp.wait_send()


def _adamw_replicated(vec, gsguw, weights, m_state, v_state):
    n = len(SMALL)

    def body(*refs):
        vec_ref, gsguw_ref = refs[0], refs[1]
        w_refs, m_refs, v_refs = (refs[2 + k * n:2 + (k + 1) * n] for k in range(3))
        outs = refs[2 + 3 * n:]
        g_refs, d_refs, nm_refs, nv_refs = (outs[k * n:(k + 1) * n] for k in range(4))
        for i, (_, row, shape) in enumerate(SMALL):
            g = gsguw_ref[...] if row is None else vec_ref[row:row + shape[0], 0:shape[1]]
            g_refs[i][...] = g
            d_refs[i][...], nm_refs[i][...], nv_refs[i][...] = _adamw(
                w_refs[i][...], g, m_refs[i][...], v_refs[i][...])

    shapes = tuple(jax.ShapeDtypeStruct(shape, F32) for _, _, shape in SMALL)
    outs = pl.pallas_call(
        body,
        name="adamw_replicated",
        in_specs=[VMEM_SPEC] * (2 + 3 * n),
        out_specs=(VMEM_SPEC,) * (4 * n),
        out_shape=shapes * 4,
    )(vec, gsguw, *weights, *m_state, *v_state)
    return tuple(outs[k * n:(k + 1) * n] for k in range(4))


SMALL = (
    ("norm_g", VEC_NORM_G, (1, D_MODEL)),
    ("b_in", VEC_B_IN, (1, IN_W)),
    ("attn_sinks", VEC_SINKS, (1, N_Q_HEADS)),
    ("sgu_ln_g", VEC_LN_G, (1, SGU_W)),
    ("sgu_ln_b", VEC_LN_B, (1, SGU_W)),
    ("sgu_w", None, (N_SGU_HEADS * BLOCK, BLOCK)),
    ("sgu_b", VEC_SGU_B, (N_SGU_HEADS, BLOCK)),
    ("b_out", VEC_B_OUT, (1, D_MODEL)),
    ("final_norm_g", VEC_FINAL_G, (1, D_MODEL)),
)


def _local_grads(x, target, win_t, wout_shard, norm_g, b_in, attn_sinks, sgu_ln_g, sgu_ln_b, sgu_w, sgu_b, b_out,
                 final_g):
    sinks = attn_sinks.reshape(N_Q_HEADS)
    bias_full = jnp.repeat(sgu_b.T, HEAD_DIM, axis=1)
    h, q, kvx, gates, wout = _in_proj(x, norm_g, b_in, win_t, wout_shard)
    out, ag = _attn_fwd(sinks, q, kvx, gates)
    sg = _sgu_fwd(gates, sgu_ln_g, sgu_ln_b, sgu_w, bias_full)
    gres, dmix, gwout, vec4 = _out_proj_loss(ag, sg, x, target, wout, b_out, final_g)
    dps, gsguw, gsgub, gln, gbin_s = _sgu_bwd(dmix, gates, sgu_ln_g, sgu_ln_b, sgu_w, bias_full)
    dpa, gsink, gbin_a, gwout_shard, gsguw_sum = _attn_bwd(
        sinks, dmix, q, kvx, out, gates, gwout, gsguw.reshape(N_SGU_HEADS * BLOCK, BLOCK))
    gwin = _win_grad(dpa, dps, h)
    grad_x, gwin_shard, vec = _in_proj_bwd(dpa, dps, win_t, x, norm_g, gres, gwin,
                                           (gbin_a, gbin_s, gsink, gln, gsgub, vec4))
    return grad_x, gwin_shard, gwout_shard, gsguw_sum, vec


def kernel(x, norm_g, w_in, b_in, attn_sinks, sgu_ln_g, sgu_ln_b, sgu_w, sgu_b, w_out, b_out, final_norm_g, loss_target, m_norm_g, m_w_in, m_b_in, m_attn_sinks, m_sgu_ln_g, m_sgu_ln_b, m_sgu_w, m_sgu_b, m_w_out, m_b_out, m_final_norm_g, v_norm_g, v_w_in, v_b_in, v_attn_sinks, v_sgu_ln_g, v_sgu_ln_b, v_sgu_w, v_sgu_b, v_w_out, v_b_out, v_final_norm_g):
    given = dict(norm_g=norm_g, b_in=b_in, attn_sinks=attn_sinks, sgu_ln_g=sgu_ln_g, sgu_ln_b=sgu_ln_b,
                 sgu_w=sgu_w, sgu_b=sgu_b, b_out=b_out, final_norm_g=final_norm_g)
    m_given = dict(norm_g=m_norm_g, b_in=m_b_in, attn_sinks=m_attn_sinks, sgu_ln_g=m_sgu_ln_g,
                   sgu_ln_b=m_sgu_ln_b, sgu_w=m_sgu_w, sgu_b=m_sgu_b, b_out=m_b_out, final_norm_g=m_final_norm_g)
    v_given = dict(norm_g=v_norm_g, b_in=v_b_in, attn_sinks=v_attn_sinks, sgu_ln_g=v_sgu_ln_g,
                   sgu_ln_b=v_sgu_ln_b, sgu_w=v_sgu_w, sgu_b=v_sgu_b, b_out=v_b_out, final_norm_g=v_final_norm_g)

    win_t = _all_gather_win(w_in[0].T)
    grad_x, gwin_t, gwout, gsguw, vec = _local_grads(
        x[0], loss_target[0], win_t, w_out[0], norm_g, b_in, attn_sinks, sgu_ln_g, sgu_ln_b, sgu_w[0], sgu_b[0],
        b_out, final_norm_g.reshape(1, D_MODEL))

    t = lambda a: a[0].T
    d_win, nm_win, nv_win = _adamw_shard("adamw_w_in", gwin_t, t(w_in), t(m_w_in), t(v_w_in), WIN_ROWS // 2)
    d_wout, nm_wout, nv_wout = _adamw_shard("adamw_w_out", gwout, w_out[0], m_w_out[0], v_w_out[0], WOUT_ROWS)
    as_2d = lambda d: [d[name].reshape(shape) for name, _, shape in SMALL]
    loss = vec[VEC_LOSS, 0]
    small = _adamw_replicated(vec, gsguw, as_2d(given), as_2d(m_given), as_2d(v_given))

    def assemble(big_in, big_out, k):
        vals = {name: small[k][i].reshape(given[name].shape) for i, (name, _, _) in enumerate(SMALL)}
        vals["w_in"] = big_in.T[None]
        vals["w_out"] = big_out[None]
        order = ("norm_g", "w_in", "b_in", "attn_sinks", "sgu_ln_g", "sgu_ln_b", "sgu_w", "sgu_b", "w_out",
                 "b_out", "final_norm_g")
        return [vals[name] for name in order]

    return (loss, grad_x[None],
            *assemble(gwin_t, gwout, 0), *assemble(d_win, d_wout, 1),
            *assemble(nm_win, nm_wout, 2), *assemble(nv_win, nv_wout, 3))
```

```python
import functools
import math

import jax
import jax.numpy as jnp
from jax import lax
from jax.experimental import pallas as pl
from jax.experimental.pallas import tpu as pltpu

F32 = jnp.float32
BF16 = jnp.bfloat16
MXU_DTYPE = BF16
COMM_DTYPE = BF16

D_MODEL = 1024
SEQ = 4096
HEAD_DIM = 64
N_Q_HEADS = 8
Q_PER_KV = 4
BLOCK = 128
N_BLOCKS = SEQ // BLOCK
ATTN_W = 512
KV_W = 128
SGU_W = 512
N_SGU_HEADS = 8
IN_W = 2816
NORM_EPS = 1e-5
NEG_INF = -1e30
SCALE = HEAD_DIM ** -0.5
KV0 = ATTN_W
GATE0 = ATTN_W + 2 * KV_W
SGU0 = GATE0 + ATTN_W
ATTN_SECTION = SGU0
SGU_SECTION = IN_W - SGU0

ADAM_LR = 0.001
ADAM_B1 = 0.9
ADAM_B2 = 0.999
ADAM_EPS = 1e-08
ADAM_WD = 0.01
ADAM_STEP = 10

N_DEV = 8
WIN_ROWS = IN_W // N_DEV
WOUT_ROWS = D_MODEL // N_DEV
SGUW_ROWS = N_SGU_HEADS * BLOCK // N_DEV
VEC_ROWS = 16
MESH = pl.DeviceIdType.MESH

LANES = 128
HALF = LANES // 2
N_PAIRS = N_Q_HEADS * HEAD_DIM // LANES
KVX_W = 12 * LANES
TOKEN_TILE = 256
FWD_TOKEN_TILE = 512
ATTN_FWD_AHEAD = 4
ATTN_BWD_AHEAD = 3
VMEM_LIMIT = 56 * 1024 * 1024

NN = (((1,), (0,)), ((), ()))
NT = (((1,), (1,)), ((), ()))
TN = (((0,), (0,)), ((), ()))


def _dot(a, b, dims=NN):
    return lax.dot_general(a.astype(MXU_DTYPE), b.astype(MXU_DTYPE), dims, preferred_element_type=F32)


def _gelu(x):
    return x * (lax.erf(x * (1.0 / math.sqrt(2.0))) + 1.0) * 0.5


def _gelu_grad(x):
    cdf = (lax.erf(x * (1.0 / math.sqrt(2.0))) + 1.0) * 0.5
    return cdf + x * jnp.exp(-0.5 * x * x) * (1.0 / math.sqrt(2.0 * math.pi))


def _silu_and_grad(z):
    s = jax.nn.sigmoid(z)
    return z * s, s * (1.0 + z * (1.0 - s))


def _params(semantics=None, vmem=None):
    kw = {}
    if semantics is not None:
        kw["dimension_semantics"] = semantics
    if vmem is not None:
        kw["vmem_limit_bytes"] = vmem
    return pltpu.CompilerParams(**kw)


def _full(shape):
    return pl.BlockSpec(shape, lambda *_: (0,) * len(shape))


VMEM_SPEC = pl.BlockSpec(memory_space=pltpu.VMEM)


RELATIONS = ((0, 0), (1, 0), (0, 1), (1, 1))


def _place():
    return lax.axis_index("x"), lax.axis_index("y"), lax.axis_index("c")


def _chip(rel):
    x, y, _ = _place()
    return (1 - x if rel[0] else x, 1 - y if rel[1] else y)


def _block_rows(place, n_rows):
    px, py, pc = place
    return pl.ds(pl.multiple_of((4 * px + 2 * py + pc) * n_rows, 16), n_rows)


class _Copies:
    def __init__(self, send_sems, recv_sems):
        self.send_sems, self.recv_sems = send_sems, recv_sems

    def __call__(self, k, src, dst, to):
        return pltpu.make_async_remote_copy(src_ref=src, dst_ref=dst, send_sem=self.send_sems.at[k],
                                            recv_sem=self.recv_sems.at[k], device_id=to, device_id_type=MESH)


def _gather_plan(copies, sem0, full_ref, n_rows):
    x, y, c = _place()
    me, sibling = (x, y, c), (x, y, 1 - c)
    chips = [_chip(rel) for rel in RELATIONS[1:]]

    def cp(k, block, to):
        rows = full_ref.at[_block_rows(block, n_rows), :]
        return copies(sem0 + k, rows, rows, to)

    first = [cp(0, me, sibling)] + [cp(1 + j, me, (*chip, c)) for j, chip in enumerate(chips)]
    passed = [cp(4 + j, (*chip, c), sibling) for j, chip in enumerate(chips)]

    def start():
        for f in first:
            f.start()

    def forward():
        for j, chip in enumerate(chips):
            cp(1 + j, (*chip, c), me).wait_recv()
            passed[j].start()

    def finish():
        cp(0, sibling, me).wait_recv()
        for j, chip in enumerate(chips):
            cp(4 + j, (*chip, 1 - c), me).wait_recv()
        for f in first + passed:
            f.wait_send()

    return start, forward, finish


GATHER_SEMS = 7


def _reduce_scatter_plan(copies, sem0, part_ref, n_rows, sa, ra, sb, rc, res_ref):
    x, y, c = _place()
    sibling = (x, y, 1 - c)
    n = n_rows
    level1 = copies(sem0, sa, ra, sibling)

    def level2(i):
        slot = pl.ds((i - 1) * n, n)
        return copies(sem0 + i, sb.at[slot, :], rc.at[slot, :], (*_chip(RELATIONS[i]), c))

    def start():
        for i, rel in enumerate(RELATIONS):
            sa[i * n:(i + 1) * n, :] = part_ref[_block_rows((*_chip(rel), 1 - c), n), :].astype(sa.dtype)
        level1.start()

    def exchange():
        level1.wait_recv()
        for i, rel in enumerate(RELATIONS):
            total = part_ref[_block_rows((*_chip(rel), c), n), :] + ra[i * n:(i + 1) * n, :].astype(F32)
            if i == 0:
                res_ref[...] = total
            else:
                sb[(i - 1) * n:i * n, :] = total.astype(sb.dtype)
                level2(i).start()

    def finish():
        acc = res_ref[...]
        for i in range(1, len(RELATIONS)):
            level2(i).wait_recv()
            acc = acc + rc[(i - 1) * n:i * n, :].astype(F32)
        res_ref[...] = acc
        level1.wait_send()
        for i in range(1, len(RELATIONS)):
            level2(i).wait_send()

    return start, exchange, finish


REDUCE_SEMS = 4


def _reduce_scatter_scratch(n_rows, width, dtype):
    return [pltpu.VMEM((4 * n_rows, width), dtype), pltpu.VMEM((4 * n_rows, width), dtype),
            pltpu.VMEM((3 * n_rows, width), dtype), pltpu.VMEM((3 * n_rows, width), dtype)]


def _dma_sems(n):
    return [pltpu.SemaphoreType.DMA((n,)), pltpu.SemaphoreType.DMA((n,))]


def _all_gather_win(win_t_shard):
    def body(win_ref, full_ref, send_sems, recv_sems):
        full_ref[_block_rows(_place(), WIN_ROWS), :] = win_ref[...].astype(COMM_DTYPE)
        start, forward, finish = _gather_plan(_Copies(send_sems, recv_sems), 0, full_ref, WIN_ROWS)
        start()
        forward()
        finish()

    return pl.pallas_call(
        body,
        name="all_gather_win",
        out_shape=jax.ShapeDtypeStruct((IN_W, D_MODEL), COMM_DTYPE),
        in_specs=[VMEM_SPEC],
        out_specs=VMEM_SPEC,
        scratch_shapes=_dma_sems(GATHER_SEMS),
        compiler_params=_params(vmem=VMEM_LIMIT),
    )(win_t_shard)


def _in_proj(x, norm_g, b_in, win_t, wout_shard):
    tm = FWD_TOKEN_TILE
    steps = SEQ // tm

    def body(x_ref, g_ref, b_ref, w_ref, wout_ref, h_ref, q_ref, kvx_ref, gate_ref, wfull_ref,
             landing, send_sems, recv_sems):
        step = pl.program_id(0)
        start, forward, finish = _gather_plan(_Copies(send_sems, recv_sems), 0, landing, WOUT_ROWS)

        @pl.when(step == 0)
        def _():
            landing[_block_rows(_place(), WOUT_ROWS), :] = wout_ref[...].astype(COMM_DTYPE)
            start()

        pl.when(step == steps // 2)(forward)

        xv = x_ref[...]
        r = lax.rsqrt(jnp.mean(xv * xv, axis=-1, keepdims=True) + NORM_EPS)
        h = ((xv * r) * g_ref[...]).astype(MXU_DTYPE)
        h_ref[...] = h

        def proj(lo, hi):
            return _dot(h, w_ref[lo:hi, :], NT) + b_ref[:, lo:hi]

        qs = proj(0, ATTN_W) * SCALE
        for pair in range(N_PAIRS):
            q_ref[pair] = qs[:, pair * LANES:(pair + 1) * LANES].astype(MXU_DTYPE)
        kv = proj(KV0, GATE0)
        low = lax.broadcasted_iota(jnp.int32, (tm, LANES), 1) < HALF
        for i in range(2):
            t = kv[:, i * LANES:(i + 1) * LANES]
            rot = pltpu.roll(t, HALF, 1)
            variants = (jnp.where(low, t, 0.0), jnp.where(low, 0.0, rot),
                        jnp.where(low, rot, 0.0), jnp.where(low, 0.0, t))
            for j, val in enumerate(variants):
                col = (4 * i + j) * LANES
                kvx_ref[:, col:col + LANES] = val.astype(MXU_DTYPE)
                if i == 1:
                    ones_elsewhere = jnp.where(low == (j % 2 == 0), val, 1.0)
                    kvx_ref[:, col + 4 * LANES:col + 5 * LANES] = ones_elsewhere.astype(MXU_DTYPE)
        for k in range(4):
            gate_ref[k] = proj(GATE0 + k * SGU_W, GATE0 + (k + 1) * SGU_W)

        @pl.when(step == steps - 1)
        def _():
            finish()
            wfull_ref[...] = landing[...]

    return pl.pallas_call(
        body,
        name="in_proj",
        grid=(steps,),
        in_specs=[pl.BlockSpec((tm, D_MODEL), lambda i: (i, 0)),
                  _full((1, D_MODEL)), _full((1, IN_W)), _full((IN_W, D_MODEL)), VMEM_SPEC],
        out_specs=(pl.BlockSpec((tm, D_MODEL), lambda i: (i, 0)),
                   pl.BlockSpec((N_PAIRS, tm, LANES), lambda i: (0, i, 0)),
                   pl.BlockSpec((tm, KVX_W), lambda i: (i, 0)),
                   pl.BlockSpec((4, tm, SGU_W), lambda i: (0, i, 0)),
                   _full((D_MODEL, D_MODEL))),
        out_shape=(jax.ShapeDtypeStruct((SEQ, D_MODEL), MXU_DTYPE),
                   jax.ShapeDtypeStruct((N_PAIRS, SEQ, LANES), MXU_DTYPE),
                   jax.ShapeDtypeStruct((SEQ, KVX_W), MXU_DTYPE),
                   jax.ShapeDtypeStruct((4, SEQ, SGU_W), F32),
                   jax.ShapeDtypeStruct((D_MODEL, D_MODEL), COMM_DTYPE)),
        scratch_shapes=[pltpu.VMEM((D_MODEL, D_MODEL), COMM_DTYPE)] + _dma_sems(GATHER_SEMS),
        compiler_params=_params(("arbitrary",), VMEM_LIMIT),
    )(x, norm_g, b_in, win_t, wout_shard)


def _window_mask(n):
    qi = lax.broadcasted_iota(jnp.int32, (2 * BLOCK, 2 * BLOCK), 0) & (BLOCK - 1)
    p = lax.broadcasted_iota(jnp.int32, (2 * BLOCK, 2 * BLOCK), 1) - BLOCK
    in_window = jnp.logical_and(p <= qi, p > qi - BLOCK)
    return jnp.logical_and(in_window, jnp.logical_or(p >= 0, n > 0))


def _sink_column(sink_ref, g, par):
    return jnp.concatenate([jnp.full((BLOCK, 1), sink_ref[4 * g + par], F32),
                            jnp.full((BLOCK, 1), sink_ref[4 * g + 2 + par], F32)], axis=0)


def _kv_cat(kp_ref, kc_ref, var, with_ones):
    kcol, vcol = var * LANES, (var + (8 if with_ones else 4)) * LANES
    return (jnp.concatenate([kp_ref[:, kcol:kcol + LANES], kc_ref[:, kcol:kcol + LANES]], axis=0),
            jnp.concatenate([kp_ref[:, vcol:vcol + LANES], kc_ref[:, vcol:vcol + LANES]], axis=0))


def _softmax_numerator(s, sink):
    m = jnp.maximum(jnp.max(s, axis=1, keepdims=True), sink)
    return jnp.exp(s - m), m


def _mixers_fwd(sinks, q, kvx, gates, ln_g, ln_b, sgu_w, bias_full):
    def body(sink_ref, q_ref, kc_ref, za_ref, us_ref, vs_ref, zs_ref, lng_ref, lnb_ref, w_ref, bias_ref,
             out_ref, ag_ref, sg_ref, kp_ref):
        @pl.when(pl.program_id(0) == 0)
        def _():
            kp_ref[...] = jnp.zeros_like(kp_ref)

        u, _, _, vln = _sgu_activations(us_ref[...], vs_ref[...], lng_ref[...], lnb_ref[...])

        valid = _window_mask(pl.program_id(0))[0:BLOCK]
        chains = [(g, par, i) for g in range(2) for par in range(2) for i in range(2)]
        kv = {(g, par): _kv_cat(kp_ref, kc_ref, 2 * g + par, True) for g in range(2) for par in range(2)}
        scores, outs = {}, {}

        def issue_scores(k):
            g, par, i = chains[k]
            scores[k] = _dot(q_ref[2 * g + i], kv[g, par][0], NT)

        ahead = ATTN_FWD_AHEAD
        for k in range(ahead):
            issue_scores(k)
        low = lax.broadcasted_iota(jnp.int32, (BLOCK, LANES), 1) < HALF
        for k, (g, par, i) in enumerate(chains):
            sink = sink_ref[4 * g + 2 * i + par]
            e, m = _softmax_numerator(jnp.where(valid, scores[k], NEG_INF), sink)
            if k + ahead < len(chains):
                issue_scores(k + ahead)
            o = _dot(e, kv[g, par][1])
            outs[g, par, i] = o / (pltpu.roll(o, HALF, 1) + jnp.exp(sink - m))
        for pair in range(N_PAIRS):
            g, i = divmod(pair, 2)
            o = jnp.where(low, outs[g, 0, i], outs[g, 1, i])
            out_ref[pair] = o
            gate, _ = _silu_and_grad(za_ref[:, pair * LANES:(pair + 1) * LANES])
            ag_ref[:, pair * LANES:(pair + 1) * LANES] = (o * gate).astype(MXU_DTYPE)
        kp_ref[...] = kc_ref[...]
        mixed = _sgu_mix(vln, w_ref, bias_ref)
        for pair in range(N_SGU_HEADS // 2):
            cols = slice(pair * LANES, (pair + 1) * LANES)
            gate, _ = _silu_and_grad(zs_ref[:, cols])
            sg_ref[:, cols] = (u[:, cols] * mixed[pair] * gate).astype(MXU_DTYPE)

    blk = lambda w: pl.BlockSpec((BLOCK, w), lambda n: (n, 0))
    tiles = pl.BlockSpec((N_PAIRS, BLOCK, LANES), lambda n: (0, n, 0))
    gate = lambda k: pl.BlockSpec((None, BLOCK, SGU_W), lambda n: (k, n, 0))
    return pl.pallas_call(
        body,
        name="mixers_fwd",
        grid=(N_BLOCKS,),
        in_specs=[pl.BlockSpec(memory_space=pltpu.SMEM), tiles, blk(KVX_W), gate(0), gate(1), gate(2), gate(3),
                  _full((1, SGU_W)), _full((1, SGU_W)), _full((N_SGU_HEADS, BLOCK, BLOCK)), _full((BLOCK, SGU_W))],
        out_specs=(tiles, blk(ATTN_W), blk(SGU_W)),
        out_shape=(jax.ShapeDtypeStruct((N_PAIRS, SEQ, LANES), F32),
                   jax.ShapeDtypeStruct((SEQ, ATTN_W), MXU_DTYPE),
                   jax.ShapeDtypeStruct((SEQ, SGU_W), MXU_DTYPE)),
        scratch_shapes=[pltpu.VMEM((BLOCK, KVX_W), MXU_DTYPE)],
        compiler_params=_params(("arbitrary",)),
    )(sinks, q, kvx, gates, gates, gates, gates, ln_g, ln_b, sgu_w, bias_full)


def _sgu_activations(us, vs, lng, lnb):
    u = _gelu(us)
    vg = _gelu(vs)
    mu = jnp.mean(vg, axis=-1, keepdims=True)
    xc = vg - mu
    rstd = lax.rsqrt(jnp.mean(xc * xc, axis=-1, keepdims=True) + NORM_EPS)
    vhat = xc * rstd
    return u, vhat, rstd, vhat * lng + lnb


def _sgu_mix(vln, w_ref, bias_ref):
    low = lax.broadcasted_iota(jnp.int32, (BLOCK, LANES), 1) < HALF
    tril = (lax.broadcasted_iota(jnp.int32, (BLOCK, BLOCK), 0)
            >= lax.broadcasted_iota(jnp.int32, (BLOCK, BLOCK), 1))
    mixed = []
    for pair in range(N_SGU_HEADS // 2):
        vp = vln[:, pair * LANES:(pair + 1) * LANES]
        w0 = jnp.where(tril, w_ref[2 * pair], 0.0)
        w1 = jnp.where(tril, w_ref[2 * pair + 1], 0.0)
        mixed.append(_dot(w0, jnp.where(low, vp, 0.0)) + _dot(w1, jnp.where(low, 0.0, vp))
                     + bias_ref[:, pair * LANES:(pair + 1) * LANES])
    return mixed


def _out_proj_loss(ag, sg, x, target, wout, b_out, final_g):
    tm = FWD_TOKEN_TILE

    def body(ag_ref, sg_ref, x_ref, t_ref, w_ref, b_ref, gf_ref, gres_ref, dmix_ref, gw_ref, vec_ref):
        @pl.when(pl.program_id(0) == 0)
        def _():
            gw_ref[...] = jnp.zeros_like(gw_ref)
            vec_ref[...] = jnp.zeros_like(vec_ref)

        a = ag_ref[...]
        s = sg_ref[...]
        xo = x_ref[...] + (_dot(a, w_ref[0:ATTN_W, :]) + _dot(s, w_ref[ATTN_W:, :])) + b_ref[...]
        r = lax.rsqrt(jnp.mean(xo * xo, axis=-1, keepdims=True) + NORM_EPS)
        xn = xo * r
        gf = gf_ref[...]
        err = xn * gf - t_ref[...]
        loss = 0.5 * jnp.sum(jnp.mean(err * err, axis=-1, keepdims=True), axis=0, keepdims=True)
        dy = err * (1.0 / D_MODEL)
        dxn = dy * gf
        gres = r * (dxn - xn * jnp.mean(dxn * xn, axis=-1, keepdims=True))
        vec_ref[0:1, :] += jnp.broadcast_to(loss, (1, D_MODEL))
        vec_ref[1:2, :] += jnp.sum(dy * xn, axis=0, keepdims=True)
        vec_ref[2:3, :] += jnp.sum(gres, axis=0, keepdims=True)
        gres_ref[...] = gres
        gb = gres.astype(MXU_DTYPE)
        dmix_ref[0] = _dot(gb, w_ref[0:ATTN_W, :], NT)
        dmix_ref[1] = _dot(gb, w_ref[ATTN_W:, :], NT)
        gw_ref[0:ATTN_W, :] += _dot(a, gb, TN)
        gw_ref[ATTN_W:, :] += _dot(s, gb, TN)

    tile = lambda w: pl.BlockSpec((tm, w), lambda i: (i, 0))
    return pl.pallas_call(
        body,
        name="out_proj_loss",
        grid=(SEQ // tm,),
        in_specs=[tile(ATTN_W), tile(SGU_W), tile(D_MODEL), tile(D_MODEL),
                  _full((D_MODEL, D_MODEL)), _full((1, D_MODEL)), _full((1, D_MODEL))],
        out_specs=(tile(D_MODEL), pl.BlockSpec((2, tm, ATTN_W), lambda i: (0, i, 0)), _full((D_MODEL, D_MODEL)),
                   _full((8, D_MODEL))),
        out_shape=(jax.ShapeDtypeStruct((SEQ, D_MODEL), F32),
                   jax.ShapeDtypeStruct((2, SEQ, ATTN_W), F32),
                   jax.ShapeDtypeStruct((D_MODEL, D_MODEL), F32),
                   jax.ShapeDtypeStruct((8, D_MODEL), F32)),
        compiler_params=_params(("arbitrary",), VMEM_LIMIT),
    )(ag, sg, x, target, wout, b_out, final_g)


def _mixers_bwd(sinks, dmix, q, kvx, out, gates, ln_g, ln_b, sgu_w, bias_full, gwout):
    last = N_BLOCKS - 1

    def body(sink_ref, d_ref, q_ref, kc_ref, o_ref, za_ref, dsg_ref, us_ref, vs_ref, zs_ref, lng_ref, lnb_ref, w_ref,
             bias_ref, gwout_ref,
             dp_ref, gsink_ref, gbin_ref, dps_ref, gw_ref, gb_ref, gln_ref, gbins_ref, wout_shard_ref,
             kp_ref, pend_ref, carry_ref, wt_ref, gbias_ref, sa_w, ra_w, sb_w, rc_w, send_sems, recv_sems):
        n = pl.program_id(0)
        start, exchange, finish = _reduce_scatter_plan(_Copies(send_sems, recv_sems), 0, gwout_ref, WOUT_ROWS,
                                                       sa_w, ra_w, sb_w, rc_w, wout_shard_ref)
        tril = (lax.broadcasted_iota(jnp.int32, (BLOCK, BLOCK), 0)
                >= lax.broadcasted_iota(jnp.int32, (BLOCK, BLOCK), 1))

        @pl.when(n == 0)
        def _():
            gsink_ref[...] = jnp.zeros_like(gsink_ref)
            gbin_ref[...] = jnp.zeros_like(gbin_ref)
            carry_ref[...] = jnp.zeros_like(carry_ref)
            kp_ref[...] = jnp.zeros_like(kp_ref)
            gw_ref[...] = jnp.zeros_like(gw_ref)
            gln_ref[...] = jnp.zeros_like(gln_ref)
            gbins_ref[...] = jnp.zeros_like(gbins_ref)
            gbias_ref[...] = jnp.zeros_like(gbias_ref)
            for hh in range(N_SGU_HEADS):
                wt_ref[hh] = jnp.where(tril, w_ref[hh], 0.0).T.astype(MXU_DTYPE)
            start()

        pl.when(n == 3)(exchange)
        pl.when(n == 12)(finish)

        @pl.when(n > 0)
        def _():
            dp_ref[:, 0:ATTN_W] = pend_ref[:, 0:ATTN_W]
            dp_ref[:, GATE0:ATTN_SECTION] = pend_ref[:, ATTN_W:]

        @pl.when(n > last)
        def _():
            dp_ref[:, KV0:GATE0] = carry_ref[...].astype(MXU_DTYPE)

        @pl.when(n <= last)
        def _():
            us = us_ref[...]
            vs = vs_ref[...]
            lng = lng_ref[...]
            u, vhat, rstd, vln = _sgu_activations(us, vs, lng, lnb_ref[...])
            low_sgu = lax.broadcasted_iota(jnp.int32, (BLOCK, LANES), 1) < HALF
            sgu = {}

            def sgu_gates():
                mixed = _sgu_mix(vln, w_ref, bias_ref)
                sgu["du"], sgu["dzs"], sgu["dm"] = [], [], []
                for pair in range(N_SGU_HEADS // 2):
                    cols = slice(pair * LANES, (pair + 1) * LANES)
                    dsg = dsg_ref[:, cols]
                    gate, gate_grad = _silu_and_grad(zs_ref[:, cols])
                    up = u[:, cols]
                    sgu["du"].append(dsg * mixed[pair] * gate)
                    sgu["dzs"].append(dsg * up * mixed[pair] * gate_grad)
                    dmixed = dsg * up * gate
                    gbias_ref[:, cols] += dmixed
                    sgu["dm"].append((jnp.where(low_sgu, dmixed, 0.0).astype(MXU_DTYPE),
                                      jnp.where(low_sgu, 0.0, dmixed).astype(MXU_DTYPE)))

            def sgu_grads():
                dvln_parts = []
                for pair in range(N_SGU_HEADS // 2):
                    dm_lo, dm_hi = sgu["dm"][pair]
                    vp = vln[:, pair * LANES:(pair + 1) * LANES]
                    gw_ref[2 * pair] += _dot(dm_lo, vp, NT)
                    gw_ref[2 * pair + 1] += _dot(dm_hi, vp, NT)
                    dvln_parts.append(_dot(wt_ref[2 * pair], dm_lo) + _dot(wt_ref[2 * pair + 1], dm_hi))
                dvln = jnp.concatenate(dvln_parts, axis=1)
                gln_ref[0:1, :] += jnp.sum(dvln * vhat, axis=0, keepdims=True)
                gln_ref[1:2, :] += jnp.sum(dvln, axis=0, keepdims=True)
                dvhat = dvln * lng
                dvg = rstd * (dvhat - jnp.mean(dvhat, axis=-1, keepdims=True)
                              - vhat * jnp.mean(dvhat * vhat, axis=-1, keepdims=True))
                dus = jnp.concatenate(sgu["du"], axis=1) * _gelu_grad(us)
                dvs = dvg * _gelu_grad(vs)
                dzs = jnp.concatenate(sgu["dzs"], axis=1)
                for k, val in enumerate((dus, dvs, dzs)):
                    dps_ref[:, k * SGU_W:(k + 1) * SGU_W] = val.astype(MXU_DTYPE)
                    gbins_ref[:, k * SGU_W:(k + 1) * SGU_W] += jnp.sum(val, axis=0, keepdims=True)

            valid = _window_mask(n)[0:BLOCK]
            low = lax.broadcasted_iota(jnp.int32, (BLOCK, LANES), 1) < HALF
            low_keys = lax.broadcasted_iota(jnp.int32, (2 * BLOCK, LANES), 1) < HALF
            lane_row = lax.broadcasted_iota(jnp.int32, (1, LANES), 1)
            gsink = jnp.zeros((1, LANES), F32)
            chains = [(g, par, i) for g in range(2) for par in range(2) for i in range(2)]
            kv = {(g, par): _kv_cat(kp_ref, kc_ref, 2 * g + par, False) for g in range(2) for par in range(2)}
            ones_keys = jnp.ones((2 * BLOCK, LANES), MXU_DTYPE)
            half_of_lane = lax.broadcasted_iota(jnp.int32, (LANES, 2 * LANES), 0) // HALF
            half_of_col = lax.broadcasted_iota(jnp.int32, (LANES, 2 * LANES), 1) // LANES
            sum_halves = (half_of_lane == half_of_col).astype(MXU_DTYPE)
            douts, deltas = [], []
            for pair in range(N_PAIRS):
                lanes = slice(pair * LANES, (pair + 1) * LANES)
                dg = d_ref[:, lanes]
                gate, gate_grad = _silu_and_grad(za_ref[:, lanes])
                o = o_ref[pair]
                dout = dg * gate
                dza = dg * o * gate_grad
                douts.append(dout.astype(MXU_DTYPE))
                deltas.append(_dot(dout * o, sum_halves))
                zl = slice(ATTN_W + pair * LANES, ATTN_W + (pair + 1) * LANES)
                pend_ref[:, zl] = dza.astype(MXU_DTYPE)
                gl = slice(GATE0 + pair * LANES, GATE0 + (pair + 1) * LANES)
                gbin_ref[:, gl] += jnp.sum(dza, axis=0, keepdims=True)

            first = {}

            def issue_first(k):
                g, par, i = chains[k]
                first[k] = (_dot(q_ref[2 * g + i], kv[g, par][0], NT), _dot(douts[2 * g + i], kv[g, par][1], NT))

            numerators = {}

            def issue_row_sums(k):
                g, par, i = chains[k]
                sink = sink_ref[4 * g + 2 * i + par]
                e, m = _softmax_numerator(jnp.where(valid, first[k][0], NEG_INF), sink)
                numerators[k] = (e, jnp.exp(sink - m), _dot(e, ones_keys))

            ahead = ATTN_BWD_AHEAD
            for k in range(ahead):
                issue_first(k)
            issue_row_sums(0)
            issue_row_sums(1)
            dqs, dk_parts, dv_parts = {}, {}, {}
            operands = {}

            def issue_last(k):
                g, par, i = chains[k]
                ds, ds_t, p_t = operands.pop(k)
                dq = _dot(ds, kv[g, par][0])
                dqs[g, i] = dq if par == 0 else dqs[g, i] + dq
                dk = _dot(ds_t, q_ref[2 * g + i])
                dv = _dot(p_t, douts[2 * g + i])
                dk_parts[g, par] = dk if i == 0 else dk_parts[g, par] + dk
                dv_parts[g, par] = dv if i == 0 else dv_parts[g, par] + dv

            for k, (g, par, i) in enumerate(chains):
                h = 4 * g + 2 * i + par
                delta = deltas[2 * g + i][:, par * LANES:(par + 1) * LANES]
                e, at_sink, row_sum = numerators[k]
                inv = 1.0 / (row_sum + at_sink)
                p = e * jnp.tile(inv, (1, 2))
                ds = p * (first[k][1] - jnp.tile(delta, (1, 2)))
                operands[k] = (ds.astype(MXU_DTYPE), ds.T.astype(MXU_DTYPE), p.T.astype(MXU_DTYPE))
                total = jnp.sum(at_sink * inv * delta, axis=0, keepdims=True)
                gsink = jnp.where(lane_row == h, -total, gsink)
                if k + ahead < len(chains):
                    issue_first(k + ahead)
                if k + 2 < len(chains):
                    issue_row_sums(k + 2)
                if k > 0:
                    issue_last(k - 1)
                if k == len(chains) // 2 - 1:
                    sgu_gates()
            issue_last(len(chains) - 1)
            sgu_grads()
            for pair in range(N_PAIRS):
                g, i = divmod(pair, 2)
                dq = dqs[g, i] * SCALE
                lanes = slice(pair * LANES, (pair + 1) * LANES)
                pend_ref[:, lanes] = dq.astype(MXU_DTYPE)
                gbin_ref[:, lanes] += jnp.sum(dq, axis=0, keepdims=True)
            gsink_ref[...] += gsink
            for k, parts in enumerate((dk_parts, dv_parts)):
                masked = {key: jnp.where(low_keys if key[1] == 0 else jnp.logical_not(low_keys), val, 0.0)
                          for key, val in parts.items()}
                both = (masked[0, 0] + masked[1, 1]
                        + pltpu.roll(masked[0, 1] + masked[1, 0], HALF, 1))
                lanes = slice(k * KV_W, (k + 1) * KV_W)
                done = carry_ref[:, lanes] + both[0:BLOCK]
                dp_ref[:, KV0 + k * KV_W:KV0 + (k + 1) * KV_W] = done.astype(MXU_DTYPE)
                carry_ref[:, lanes] = both[BLOCK:]
                gbin_ref[:, KV0 + k * KV_W:KV0 + (k + 1) * KV_W] += jnp.sum(both, axis=0, keepdims=True)
            kp_ref[...] = kc_ref[...]

        @pl.when(n == last)
        def _():
            for hh in range(N_SGU_HEADS):
                gw_ref[hh] = jnp.where(tril, gw_ref[hh], 0.0)
            head_of_lane = lax.broadcasted_iota(jnp.int32, (N_SGU_HEADS, SGU_W), 1) // HEAD_DIM
            select = (head_of_lane == lax.broadcasted_iota(jnp.int32, (N_SGU_HEADS, SGU_W), 0)).astype(F32)
            gb_ref[...] = lax.dot_general(select, gbias_ref[...], NT, precision=lax.Precision.HIGHEST,
                                          preferred_element_type=F32)

    at = lambda n: jnp.minimum(n, last)
    blk = lambda w: pl.BlockSpec((BLOCK, w), lambda n: (at(n), 0))
    tiles = pl.BlockSpec((N_PAIRS, BLOCK, LANES), lambda n: (0, at(n), 0))
    section = lambda k: pl.BlockSpec((None, BLOCK, SGU_W), lambda n: (k, at(n), 0))
    return pl.pallas_call(
        body,
        name="mixers_bwd",
        grid=(N_BLOCKS + 1,),
        in_specs=[pl.BlockSpec(memory_space=pltpu.SMEM),
                  section(0),
                  tiles,
                  blk(KVX_W),
                  tiles,
                  section(0),
                  section(1),
                  section(1), section(2), section(3),
                  _full((1, SGU_W)), _full((1, SGU_W)), _full((N_SGU_HEADS, BLOCK, BLOCK)), _full((BLOCK, SGU_W)),
                  VMEM_SPEC],
        out_specs=(pl.BlockSpec((BLOCK, ATTN_SECTION), lambda n: (jnp.maximum(n - 1, 0), 0)),
                   _full((1, LANES)), _full((1, ATTN_SECTION)),
                   pl.BlockSpec((BLOCK, SGU_SECTION), lambda n: (at(n), 0)),
                   _full((N_SGU_HEADS, BLOCK, BLOCK)), _full((N_SGU_HEADS, BLOCK)),
                   _full((8, SGU_W)), _full((1, SGU_SECTION)), VMEM_SPEC),
        out_shape=(jax.ShapeDtypeStruct((SEQ, ATTN_SECTION), MXU_DTYPE),
                   jax.ShapeDtypeStruct((1, LANES), F32),
                   jax.ShapeDtypeStruct((1, ATTN_SECTION), F32),
                   jax.ShapeDtypeStruct((SEQ, SGU_SECTION), MXU_DTYPE),
                   jax.ShapeDtypeStruct((N_SGU_HEADS, BLOCK, BLOCK), F32),
                   jax.ShapeDtypeStruct((N_SGU_HEADS, BLOCK), F32),
                   jax.ShapeDtypeStruct((8, SGU_W), F32),
                   jax.ShapeDtypeStruct((1, SGU_SECTION), F32),
                   jax.ShapeDtypeStruct((WOUT_ROWS, D_MODEL), F32)),
        scratch_shapes=([pltpu.VMEM((BLOCK, KVX_W), MXU_DTYPE),
                         pltpu.VMEM((BLOCK, 2 * ATTN_W), MXU_DTYPE), pltpu.VMEM((BLOCK, 2 * KV_W), F32),
                         pltpu.VMEM((N_SGU_HEADS, BLOCK, BLOCK), MXU_DTYPE), pltpu.VMEM((BLOCK, SGU_W), F32)]
                        + _reduce_scatter_scratch(WOUT_ROWS, D_MODEL, COMM_DTYPE) + _dma_sems(REDUCE_SEMS)),
        compiler_params=_params(("arbitrary",), VMEM_LIMIT),
    )(sinks, dmix, q, kvx, out, gates, dmix, gates, gates, gates, ln_g, ln_b, sgu_w, bias_full, gwout)


def _in_proj_bwd(dpa, dps, win_t, x, norm_g, gres, gwin, gsguw, vec_parts):
    tm = TOKEN_TILE
    steps = SEQ // tm
    n_parts = len(vec_parts)

    def body(da_ref, ds_ref, w_ref, x_ref, g_ref, gres_ref, gwin_ref, gsguw_ref, *rest):
        part_refs = rest[:n_parts]
        (gx_ref, shard_ref, sguw_full_ref, vec_out_ref, gng_ref, sa, ra, sb, rc, sa_s, ra_s, sb_s, rc_s, landing,
         vec_ref, ra_vec, slots, send_sems, recv_sems) = rest[n_parts:]
        step = pl.program_id(0)
        copies = _Copies(send_sems, recv_sems)
        start, exchange, finish = _reduce_scatter_plan(copies, 0, gwin_ref, WIN_ROWS, sa, ra, sb, rc, shard_ref)
        own_sguw = landing.at[_block_rows(_place(), SGUW_ROWS), :]
        start_s, exchange_s, finish_s = _reduce_scatter_plan(copies, REDUCE_SEMS, gsguw_ref, SGUW_ROWS,
                                                             sa_s, ra_s, sb_s, rc_s, own_sguw)
        gather = _gather_plan(copies, 2 * REDUCE_SEMS, landing, SGUW_ROWS)

        @pl.when(step == 0)
        def _():
            gng_ref[...] = jnp.zeros_like(gng_ref)
            start()
            start_s()

        @pl.when(step == 2)
        def _():
            exchange()
            exchange_s()

        @pl.when(step == 8)
        def _():
            finish_s()
            gather[0]()

        pl.when(step == 11)(gather[1])

        dh = _dot(da_ref[...], w_ref[0:ATTN_SECTION, :]) + _dot(ds_ref[...], w_ref[ATTN_SECTION:, :])
        xv = x_ref[...]
        r = lax.rsqrt(jnp.mean(xv * xv, axis=-1, keepdims=True) + NORM_EPS)
        xn = xv * r
        gng_ref[...] += jnp.sum(dh * xn, axis=0, keepdims=True)
        dxn = dh * g_ref[...]
        gx_ref[...] = r * (dxn - xn * jnp.mean(dxn * xn, axis=-1, keepdims=True)) + gres_ref[...]

        @pl.when(step == steps - 1)
        def _():
            finish()
            gather[2]()
            sguw_full_ref[...] = landing[...]
            _all_reduce_vectors(copies, 2 * REDUCE_SEMS + GATHER_SEMS, gng_ref, *part_refs, vec_out_ref,
                                vec_ref, ra_vec, slots)

    tile = lambda w: pl.BlockSpec((tm, w), lambda i: (i, 0))
    return pl.pallas_call(
        body,
        name="in_proj_bwd",
        grid=(steps,),
        in_specs=[tile(ATTN_SECTION), tile(SGU_SECTION), _full((IN_W, D_MODEL)), tile(D_MODEL),
                  _full((1, D_MODEL)), tile(D_MODEL), VMEM_SPEC, VMEM_SPEC] + [VMEM_SPEC] * n_parts,
        out_specs=(tile(D_MODEL), VMEM_SPEC, _full((N_SGU_HEADS * BLOCK, BLOCK)), VMEM_SPEC),
        out_shape=(jax.ShapeDtypeStruct((SEQ, D_MODEL), F32),
                   jax.ShapeDtypeStruct((WIN_ROWS, D_MODEL), F32),
                   jax.ShapeDtypeStruct((N_SGU_HEADS * BLOCK, BLOCK), F32),
                   jax.ShapeDtypeStruct((VEC_ROWS, IN_W), F32)),
        scratch_shapes=([pltpu.VMEM((1, D_MODEL), F32)] + _reduce_scatter_scratch(WIN_ROWS, D_MODEL, COMM_DTYPE)
                        + _reduce_scatter_scratch(SGUW_ROWS, BLOCK, F32)
                        + [pltpu.VMEM((N_SGU_HEADS * BLOCK, BLOCK), F32)]
                        + _vector_scratch() + _dma_sems(2 * REDUCE_SEMS + GATHER_SEMS + VECTOR_SEMS)),
        compiler_params=_params(("arbitrary",), VMEM_LIMIT),
    )(dpa, dps, win_t, x, norm_g, gres, gwin, gsguw, *vec_parts)


def _win_grad(dpa, dps, h):
    rows = 256
    n_attn = ATTN_SECTION // rows
    steps = n_attn + SGU_SECTION // rows

    def body(da_ref, ds_ref, h_ref, o_ref):
        step = pl.program_id(0)

        @pl.when(step < n_attn)
        def _():
            o_ref[...] = _dot(da_ref[...], h_ref[...], TN)

        @pl.when(step >= n_attn)
        def _():
            o_ref[...] = _dot(ds_ref[...], h_ref[...], TN)

    return pl.pallas_call(
        body,
        name="win_grad",
        grid=(steps,),
        in_specs=[pl.BlockSpec((SEQ, rows), lambda i: (0, jnp.minimum(i, n_attn - 1))),
                  pl.BlockSpec((SEQ, rows), lambda i: (0, jnp.maximum(i - n_attn, 0))),
                  _full((SEQ, D_MODEL))],
        out_specs=pl.BlockSpec((rows, D_MODEL), lambda i: (i, 0)),
        out_shape=jax.ShapeDtypeStruct((IN_W, D_MODEL), F32),
        compiler_params=_params(("arbitrary",), VMEM_LIMIT),
    )(dpa, dps, h)


VEC_NORM_G, VEC_B_IN, VEC_SINKS, VEC_LN_G, VEC_LN_B, VEC_B_OUT, VEC_FINAL_G, VEC_LOSS, VEC_SGU_B = 0, 1, 2, 3, 4, 5, 6, 7, 8


def _adamw(w, g, m, v):
    m = ADAM_B1 * m + (1.0 - ADAM_B1) * g
    v = ADAM_B2 * v + (1.0 - ADAM_B2) * (g * g)
    m_hat = m / (1.0 - ADAM_B1 ** ADAM_STEP)
    v_hat = v / (1.0 - ADAM_B2 ** ADAM_STEP)
    delta = -ADAM_LR * (m_hat / (jnp.sqrt(v_hat) + ADAM_EPS) + ADAM_WD * w)
    return delta, m, v


def _adamw_shard(name, g, w, m, v, block_rows):
    def body(g_ref, w_ref, m_ref, v_ref, d_ref, nm_ref, nv_ref):
        d_ref[...], nm_ref[...], nv_ref[...] = _adamw(w_ref[...], g_ref[...], m_ref[...], v_ref[...])

    rows, cols = w.shape
    spec = pl.BlockSpec((block_rows, cols), lambda i: (i, 0))
    return pl.pallas_call(
        body,
        name=name,
        grid=(rows // block_rows,),
        in_specs=[spec] * 4,
        out_specs=(spec,) * 3,
        out_shape=(jax.ShapeDtypeStruct(w.shape, F32),) * 3,
        compiler_params=_params(("arbitrary",)),
    )(g, w, m, v)


VECTOR_SEMS = 4


def _vector_scratch():
    return [pltpu.VMEM((VEC_ROWS, IN_W), F32), pltpu.VMEM((VEC_ROWS, IN_W), F32),
            pltpu.VMEM((4 * VEC_ROWS, IN_W), F32)]


def _all_reduce_vectors(copies, sem0, gng_ref, gba_ref, gbs_ref, gsink_ref, gln_ref, gsgub_ref, vec4_ref, out_ref,
                        vec_ref, ra_vec, slots):
    x, y, c = _place()
    vec_ref[...] = jnp.zeros_like(vec_ref)
    vec_ref[VEC_NORM_G:VEC_NORM_G + 1, 0:D_MODEL] = gng_ref[...]
    vec_ref[VEC_B_IN:VEC_B_IN + 1, 0:ATTN_SECTION] = gba_ref[...]
    vec_ref[VEC_B_IN:VEC_B_IN + 1, ATTN_SECTION:IN_W] = gbs_ref[...]
    vec_ref[VEC_SINKS:VEC_SINKS + 1, 0:LANES] = gsink_ref[...]
    vec_ref[VEC_LN_G:VEC_LN_G + 1, 0:SGU_W] = gln_ref[0:1, :]
    vec_ref[VEC_LN_B:VEC_LN_B + 1, 0:SGU_W] = gln_ref[1:2, :]
    vec_ref[VEC_B_OUT:VEC_B_OUT + 1, 0:D_MODEL] = vec4_ref[2:3, :]
    vec_ref[VEC_FINAL_G:VEC_FINAL_G + 1, 0:D_MODEL] = vec4_ref[1:2, :]
    vec_ref[VEC_LOSS:VEC_LOSS + 1, 0:D_MODEL] = vec4_ref[0:1, :]
    vec_ref[VEC_SGU_B:VEC_SGU_B + N_SGU_HEADS, 0:BLOCK] = gsgub_ref[...]

    to_sibling = copies(sem0, vec_ref, ra_vec, (x, y, 1 - c))
    to_sibling.start()
    to_sibling.wait_recv()

    def chip_slot(place):
        return slots.at[pl.ds(pl.multiple_of((2 * place[0] + place[1]) * VEC_ROWS, 8), VEC_ROWS), :]

    mine = chip_slot((x, y))
    mine[...] = vec_ref[...] + ra_vec[...]
    to_chips = [copies(sem0 + i, mine, mine, (*_chip(rel), c)) for i, rel in enumerate(RELATIONS[1:], start=1)]
    for cp in to_chips:
        cp.start()
    for i, rel in enumerate(RELATIONS[1:], start=1):
        theirs = chip_slot(_chip(rel))
        copies(sem0 + i, theirs, theirs, (x, y, c)).wait_recv()
    out_ref[...] = ((slots[0:VEC_ROWS, :] + slots[VEC_ROWS:2 * VEC_ROWS, :])
                    + slots[2 * VEC_ROWS:3 * VEC_ROWS, :]) + slots[3 * VEC_ROWS:, :]
    to_sibling.wait_send()
    for cp in to_chips:
        cp.wait_send()


def _adamw_replicated(vec, gsguw, weights, m_state, v_state):
    n = len(SMALL)

    def body(*refs):
        vec_ref, gsguw_ref = refs[0], refs[1]
        w_refs, m_refs, v_refs = (refs[2 + k * n:2 + (k + 1) * n] for k in range(3))
        outs = refs[2 + 3 * n:]
        g_refs, d_refs, nm_refs, nv_refs = (outs[k * n:(k + 1) * n] for k in range(4))
        for i, (_, row, shape) in enumerate(SMALL):
            g = gsguw_ref[...] if row is None else vec_ref[row:row + shape[0], 0:shape[1]]
            g_refs[i][...] = g
            d_refs[i][...], nm_refs[i][...], nv_refs[i][...] = _adamw(
                w_refs[i][...], g, m_refs[i][...], v_refs[i][...])

    shapes = tuple(jax.ShapeDtypeStruct(shape, F32) for _, _, shape in SMALL)
    outs = pl.pallas_call(
        body,
        name="adamw_replicated",
        in_specs=[VMEM_SPEC] * (2 + 3 * n),
        out_specs=(VMEM_SPEC,) * (4 * n),
        out_shape=shapes * 4,
    )(vec, gsguw, *weights, *m_state, *v_state)
    return tuple(outs[k * n:(k + 1) * n] for k in range(4))


SMALL = (
    ("norm_g", VEC_NORM_G, (1, D_MODEL)),
    ("b_in", VEC_B_IN, (1, IN_W)),
    ("attn_sinks", VEC_SINKS, (1, N_Q_HEADS)),
    ("sgu_ln_g", VEC_LN_G, (1, SGU_W)),
    ("sgu_ln_b", VEC_LN_B, (1, SGU_W)),
    ("sgu_w", None, (N_SGU_HEADS * BLOCK, BLOCK)),
    ("sgu_b", VEC_SGU_B, (N_SGU_HEADS, BLOCK)),
    ("b_out", VEC_B_OUT, (1, D_MODEL)),
    ("final_norm_g", VEC_FINAL_G, (1, D_MODEL)),
)


def _local_grads(x, target, win_t, wout_shard, norm_g, b_in, attn_sinks, sgu_ln_g, sgu_ln_b, sgu_w, sgu_b, b_out,
                 final_g):
    sinks = attn_sinks.reshape(N_Q_HEADS)
    bias_full = jnp.repeat(sgu_b.T, HEAD_DIM, axis=1)
    h, q, kvx, gates, wout = _in_proj(x, norm_g, b_in, win_t, wout_shard)
    out, ag, sg = _mixers_fwd(sinks, q, kvx, gates, sgu_ln_g, sgu_ln_b, sgu_w, bias_full)
    gres, dmix, gwout, vec4 = _out_proj_loss(ag, sg, x, target, wout, b_out, final_g)
    dpa, gsink, gbin_a, dps, gsguw, gsgub, gln, gbin_s, gwout_shard = _mixers_bwd(
        sinks, dmix, q, kvx, out, gates, sgu_ln_g, sgu_ln_b, sgu_w, bias_full, gwout)
    gwin = _win_grad(dpa, dps, h)
    grad_x, gwin_shard, gsguw_sum, vec = _in_proj_bwd(
        dpa, dps, win_t, x, norm_g, gres, gwin, gsguw.reshape(N_SGU_HEADS * BLOCK, BLOCK),
        (gbin_a, gbin_s, gsink, gln, gsgub, vec4))
    return grad_x, gwin_shard, gwout_shard, gsguw_sum, vec


def kernel(x, norm_g, w_in, b_in, attn_sinks, sgu_ln_g, sgu_ln_b, sgu_w, sgu_b, w_out, b_out, final_norm_g, loss_target, m_norm_g, m_w_in, m_b_in, m_attn_sinks, m_sgu_ln_g, m_sgu_ln_b, m_sgu_w, m_sgu_b, m_w_out, m_b_out, m_final_norm_g, v_norm_g, v_w_in, v_b_in, v_attn_sinks, v_sgu_ln_g, v_sgu_ln_b, v_sgu_w, v_sgu_b, v_w_out, v_b_out, v_final_norm_g):
    given = dict(norm_g=norm_g, b_in=b_in, attn_sinks=attn_sinks, sgu_ln_g=sgu_ln_g, sgu_ln_b=sgu_ln_b,
                 sgu_w=sgu_w, sgu_b=sgu_b, b_out=b_out, final_norm_g=final_norm_g)
    m_given = dict(norm_g=m_norm_g, b_in=m_b_in, attn_sinks=m_attn_sinks, sgu_ln_g=m_sgu_ln_g,
                   sgu_ln_b=m_sgu_ln_b, sgu_w=m_sgu_w, sgu_b=m_sgu_b, b_out=m_b_out, final_norm_g=m_final_norm_g)
    v_given = dict(norm_g=v_norm_g, b_in=v_b_in, attn_sinks=v_attn_sinks, sgu_ln_g=v_sgu_ln_g,
                   sgu_ln_b=v_sgu_ln_b, sgu_w=v_sgu_w, sgu_b=v_sgu_b, b_out=v_b_out, final_norm_g=v_final_norm_g)

    win_t = _all_gather_win(w_in[0].T)
    grad_x, gwin_t, gwout, gsguw, vec = _local_grads(
        x[0], loss_target[0], win_t, w_out[0], norm_g, b_in, attn_sinks, sgu_ln_g, sgu_ln_b, sgu_w[0], sgu_b[0],
        b_out, final_norm_g.reshape(1, D_MODEL))

    t = lambda a: a[0].T
    d_win, nm_win, nv_win = _adamw_shard("adamw_w_in", gwin_t, t(w_in), t(m_w_in), t(v_w_in), WIN_ROWS // 2)
    d_wout, nm_wout, nv_wout = _adamw_shard("adamw_w_out", gwout, w_out[0], m_w_out[0], v_w_out[0], WOUT_ROWS)
    as_2d = lambda d: [d[name].reshape(shape) for name, _, shape in SMALL]
    loss = vec[VEC_LOSS, 0]
    small = _adamw_replicated(vec, gsguw, as_2d(given), as_2d(m_given), as_2d(v_given))

    def assemble(big_in, big_out, k):
        vals = {name: small[k][i].reshape(given[name].shape) for i, (name, _, _) in enumerate(SMALL)}
        vals["w_in"] = big_in.T[None]
        vals["w_out"] = big_out[None]
        order = ("norm_g", "w_in", "b_in", "attn_sinks", "sgu_ln_g", "sgu_ln_b", "sgu_w", "sgu_b", "w_out",
                 "b_out", "final_norm_g")
        return [vals[name] for name in order]

    return (loss, grad_x[None],
            *assemble(gwin_t, gwout, 0), *assemble(d_win, d_wout, 1),
            *assemble(nm_win, nm_wout, 2), *assemble(nv_win, nv_wout, 3))
```

```python
import functools
import math

import jax
import jax.numpy as jnp
from jax import lax
from jax.experimental import pallas as pl
from jax.experimental.pallas import tpu as pltpu

F32 = jnp.float32
BF16 = jnp.bfloat16
MXU_DTYPE = BF16
COMM_DTYPE = BF16

D_MODEL = 1024
SEQ = 4096
HEAD_DIM = 64
N_Q_HEADS = 8
Q_PER_KV = 4
BLOCK = 128
N_BLOCKS = SEQ // BLOCK
ATTN_W = 512
KV_W = 128
SGU_W = 512
N_SGU_HEADS = 8
IN_W = 2816
NORM_EPS = 1e-5
NEG_INF = -1e30
SCALE = HEAD_DIM ** -0.5
KV0 = ATTN_W
GATE0 = ATTN_W + 2 * KV_W
SGU0 = GATE0 + ATTN_W
ATTN_SECTION = SGU0
SGU_SECTION = IN_W - SGU0

ADAM_LR = 0.001
ADAM_B1 = 0.9
ADAM_B2 = 0.999
ADAM_EPS = 1e-08
ADAM_WD = 0.01
ADAM_STEP = 10

N_DEV = 8
WIN_ROWS = IN_W // N_DEV
WOUT_ROWS = D_MODEL // N_DEV
SGUW_ROWS = N_SGU_HEADS * BLOCK // N_DEV
VEC_ROWS = 16
MESH = pl.DeviceIdType.MESH

LANES = 128
HALF = LANES // 2
N_PAIRS = N_Q_HEADS * HEAD_DIM // LANES
KVX_W = 12 * LANES
TOKEN_TILE = 256
FWD_TOKEN_TILE = 512
ATTN_FWD_AHEAD = 4
ATTN_BWD_AHEAD = 3
VMEM_LIMIT = 56 * 1024 * 1024

NN = (((1,), (0,)), ((), ()))
NT = (((1,), (1,)), ((), ()))
TN = (((0,), (0,)), ((), ()))


def _dot(a, b, dims=NN):
    return lax.dot_general(a.astype(MXU_DTYPE), b.astype(MXU_DTYPE), dims, preferred_element_type=F32)


def _gelu(x):
    return x * (lax.erf(x * (1.0 / math.sqrt(2.0))) + 1.0) * 0.5


def _gelu_grad(x):
    cdf = (lax.erf(x * (1.0 / math.sqrt(2.0))) + 1.0) * 0.5
    return cdf + x * jnp.exp(-0.5 * x * x) * (1.0 / math.sqrt(2.0 * math.pi))


def _silu_and_grad(z):
    s = jax.nn.sigmoid(z)
    return z * s, s * (1.0 + z * (1.0 - s))


def _params(semantics=None, vmem=None):
    kw = {}
    if semantics is not None:
        kw["dimension_semantics"] = semantics
    if vmem is not None:
        kw["vmem_limit_bytes"] = vmem
    return pltpu.CompilerParams(**kw)


def _full(shape):
    return pl.BlockSpec(shape, lambda *_: (0,) * len(shape))


VMEM_SPEC = pl.BlockSpec(memory_space=pltpu.VMEM)


RELATIONS = ((0, 0), (1, 0), (0, 1), (1, 1))


def _place():
    return lax.axis_index("x"), lax.axis_index("y"), lax.axis_index("c")


def _chip(rel):
    x, y, _ = _place()
    return (1 - x if rel[0] else x, 1 - y if rel[1] else y)


def _block_rows(place, n_rows):
    px, py, pc = place
    return pl.ds(pl.multiple_of((4 * px + 2 * py + pc) * n_rows, 16), n_rows)


class _Copies:
    def __init__(self, send_sems, recv_sems):
        self.send_sems, self.recv_sems = send_sems, recv_sems

    def __call__(self, k, src, dst, to):
        return pltpu.make_async_remote_copy(src_ref=src, dst_ref=dst, send_sem=self.send_sems.at[k],
                                            recv_sem=self.recv_sems.at[k], device_id=to, device_id_type=MESH)


def _gather_plan(copies, sem0, full_ref, n_rows):
    x, y, c = _place()
    me, sibling = (x, y, c), (x, y, 1 - c)
    chips = [_chip(rel) for rel in RELATIONS[1:]]

    def cp(k, block, to):
        rows = full_ref.at[_block_rows(block, n_rows), :]
        return copies(sem0 + k, rows, rows, to)

    first = [cp(0, me, sibling)] + [cp(1 + j, me, (*chip, c)) for j, chip in enumerate(chips)]
    passed = [cp(4 + j, (*chip, c), sibling) for j, chip in enumerate(chips)]

    def start():
        for f in first:
            f.start()

    def forward():
        for j, chip in enumerate(chips):
            cp(1 + j, (*chip, c), me).wait_recv()
            passed[j].start()

    def finish():
        cp(0, sibling, me).wait_recv()
        for j, chip in enumerate(chips):
            cp(4 + j, (*chip, 1 - c), me).wait_recv()
        for f in first + passed:
            f.wait_send()

    return start, forward, finish


GATHER_SEMS = 7


def _reduce_scatter_plan(copies, sem0, part_ref, n_rows, sa, ra, sb, rc, res_ref):
    x, y, c = _place()
    sibling = (x, y, 1 - c)
    n = n_rows
    level1 = copies(sem0, sa, ra, sibling)

    def level2(i):
        slot = pl.ds((i - 1) * n, n)
        return copies(sem0 + i, sb.at[slot, :], rc.at[slot, :], (*_chip(RELATIONS[i]), c))

    def start():
        for i, rel in enumerate(RELATIONS):
            sa[i * n:(i + 1) * n, :] = part_ref[_block_rows((*_chip(rel), 1 - c), n), :].astype(sa.dtype)
        level1.start()

    def exchange():
        level1.wait_recv()
        for i, rel in enumerate(RELATIONS):
            total = part_ref[_block_rows((*_chip(rel), c), n), :] + ra[i * n:(i + 1) * n, :].astype(F32)
            if i == 0:
                res_ref[...] = total
            else:
                sb[(i - 1) * n:i * n, :] = total.astype(sb.dtype)
                level2(i).start()

    def finish():
        acc = res_ref[...]
        for i in range(1, len(RELATIONS)):
            level2(i).wait_recv()
            acc = acc + rc[(i - 1) * n:i * n, :].astype(F32)
        res_ref[...] = acc
        level1.wait_send()
        for i in range(1, len(RELATIONS)):
            level2(i).wait_send()

    return start, exchange, finish


REDUCE_SEMS = 4


def _reduce_scatter_scratch(n_rows, width, dtype):
    return [pltpu.VMEM((4 * n_rows, width), dtype), pltpu.VMEM((4 * n_rows, width), dtype),
            pltpu.VMEM((3 * n_rows, width), dtype), pltpu.VMEM((3 * n_rows, width), dtype)]


def _dma_sems(n):
    return [pltpu.SemaphoreType.DMA((n,)), pltpu.SemaphoreType.DMA((n,))]


def _all_gather_win(win_t_shard):
    def body(win_ref, full_ref, send_sems, recv_sems):
        full_ref[_block_rows(_place(), WIN_ROWS), :] = win_ref[...].astype(COMM_DTYPE)
        start, forward, finish = _gather_plan(_Copies(send_sems, recv_sems), 0, full_ref, WIN_ROWS)
        start()
        forward()
        finish()

    return pl.pallas_call(
        body,
        name="all_gather_win",
        out_shape=jax.ShapeDtypeStruct((IN_W, D_MODEL), COMM_DTYPE),
        in_specs=[VMEM_SPEC],
        out_specs=VMEM_SPEC,
        scratch_shapes=_dma_sems(GATHER_SEMS),
        compiler_params=_params(vmem=VMEM_LIMIT),
    )(win_t_shard)


def _in_proj(x, norm_g, b_in, win_t, wout_shard):
    tm = FWD_TOKEN_TILE
    steps = SEQ // tm

    def body(x_ref, g_ref, b_ref, w_ref, wout_ref, h_ref, q_ref, kvx_ref, gate_ref, wfull_ref,
             landing, send_sems, recv_sems):
        step = pl.program_id(0)
        start, forward, finish = _gather_plan(_Copies(send_sems, recv_sems), 0, landing, WOUT_ROWS)

        @pl.when(step == 0)
        def _():
            landing[_block_rows(_place(), WOUT_ROWS), :] = wout_ref[...].astype(COMM_DTYPE)
            start()

        pl.when(step == steps // 2)(forward)

        xv = x_ref[...]
        r = lax.rsqrt(jnp.mean(xv * xv, axis=-1, keepdims=True) + NORM_EPS)
        h = ((xv * r) * g_ref[...]).astype(MXU_DTYPE)
        h_ref[...] = h

        def proj(lo, hi):
            return _dot(h, w_ref[lo:hi, :], NT) + b_ref[:, lo:hi]

        qs = proj(0, ATTN_W) * SCALE
        for pair in range(N_PAIRS):
            q_ref[pair] = qs[:, pair * LANES:(pair + 1) * LANES].astype(MXU_DTYPE)
        kv = proj(KV0, GATE0)
        low = lax.broadcasted_iota(jnp.int32, (tm, LANES), 1) < HALF
        for i in range(2):
            t = kv[:, i * LANES:(i + 1) * LANES]
            rot = pltpu.roll(t, HALF, 1)
            variants = (jnp.where(low, t, 0.0), jnp.where(low, 0.0, rot),
                        jnp.where(low, rot, 0.0), jnp.where(low, 0.0, t))
            for j, val in enumerate(variants):
                col = (4 * i + j) * LANES
                kvx_ref[:, col:col + LANES] = val.astype(MXU_DTYPE)
                if i == 1:
                    ones_elsewhere = jnp.where(low == (j % 2 == 0), val, 1.0)
                    kvx_ref[:, col + 4 * LANES:col + 5 * LANES] = ones_elsewhere.astype(MXU_DTYPE)
        for k in range(4):
            gate_ref[k] = proj(GATE0 + k * SGU_W, GATE0 + (k + 1) * SGU_W)

        @pl.when(step == steps - 1)
        def _():
            finish()
            wfull_ref[...] = landing[...]

    return pl.pallas_call(
        body,
        name="in_proj",
        grid=(steps,),
        in_specs=[pl.BlockSpec((tm, D_MODEL), lambda i: (i, 0)),
                  _full((1, D_MODEL)), _full((1, IN_W)), _full((IN_W, D_MODEL)), VMEM_SPEC],
        out_specs=(pl.BlockSpec((tm, D_MODEL), lambda i: (i, 0)),
                   pl.BlockSpec((N_PAIRS, tm, LANES), lambda i: (0, i, 0)),
                   pl.BlockSpec((tm, KVX_W), lambda i: (i, 0)),
                   pl.BlockSpec((4, tm, SGU_W), lambda i: (0, i, 0)),
                   _full((D_MODEL, D_MODEL))),
        out_shape=(jax.ShapeDtypeStruct((SEQ, D_MODEL), MXU_DTYPE),
                   jax.ShapeDtypeStruct((N_PAIRS, SEQ, LANES), MXU_DTYPE),
                   jax.ShapeDtypeStruct((SEQ, KVX_W), MXU_DTYPE),
                   jax.ShapeDtypeStruct((4, SEQ, SGU_W), F32),
                   jax.ShapeDtypeStruct((D_MODEL, D_MODEL), COMM_DTYPE)),
        scratch_shapes=[pltpu.VMEM((D_MODEL, D_MODEL), COMM_DTYPE)] + _dma_sems(GATHER_SEMS),
        compiler_params=_params(("arbitrary",), VMEM_LIMIT),
    )(x, norm_g, b_in, win_t, wout_shard)


def _window_mask(n):
    qi = lax.broadcasted_iota(jnp.int32, (2 * BLOCK, 2 * BLOCK), 0) & (BLOCK - 1)
    p = lax.broadcasted_iota(jnp.int32, (2 * BLOCK, 2 * BLOCK), 1) - BLOCK
    in_window = jnp.logical_and(p <= qi, p > qi - BLOCK)
    return jnp.logical_and(in_window, jnp.logical_or(p >= 0, n > 0))


def _sink_column(sink_ref, g, par):
    return jnp.concatenate([jnp.full((BLOCK, 1), sink_ref[4 * g + par], F32),
                            jnp.full((BLOCK, 1), sink_ref[4 * g + 2 + par], F32)], axis=0)


def _kv_cat(kp_ref, kc_ref, var, with_ones):
    kcol, vcol = var * LANES, (var + (8 if with_ones else 4)) * LANES
    return (jnp.concatenate([kp_ref[:, kcol:kcol + LANES], kc_ref[:, kcol:kcol + LANES]], axis=0),
            jnp.concatenate([kp_ref[:, vcol:vcol + LANES], kc_ref[:, vcol:vcol + LANES]], axis=0))


def _softmax_numerator(s, sink):
    m = jnp.maximum(jnp.max(s, axis=1, keepdims=True), sink)
    return jnp.exp(s - m), m


def _mixers_fwd(sinks, q, kvx, gates, ln_g, ln_b, sgu_w, bias_full):
    def body(sink_ref, q_ref, kc_ref, za_ref, us_ref, vs_ref, zs_ref, lng_ref, lnb_ref, w_ref, bias_ref,
             out_ref, ag_ref, sg_ref, kp_ref):
        @pl.when(pl.program_id(0) == 0)
        def _():
            kp_ref[...] = jnp.zeros_like(kp_ref)

        u, _, _, vln = _sgu_activations(us_ref[...], vs_ref[...], lng_ref[...], lnb_ref[...])

        valid = _window_mask(pl.program_id(0))[0:BLOCK]
        chains = [(g, par, i) for g in range(2) for par in range(2) for i in range(2)]
        kv = {(g, par): _kv_cat(kp_ref, kc_ref, 2 * g + par, True) for g in range(2) for par in range(2)}
        scores, outs = {}, {}

        def issue_scores(k):
            g, par, i = chains[k]
            scores[k] = _dot(q_ref[2 * g + i], kv[g, par][0], NT)

        ahead = ATTN_FWD_AHEAD
        for k in range(ahead):
            issue_scores(k)
        low = lax.broadcasted_iota(jnp.int32, (BLOCK, LANES), 1) < HALF
        for k, (g, par, i) in enumerate(chains):
            sink = sink_ref[4 * g + 2 * i + par]
            e, m = _softmax_numerator(jnp.where(valid, scores[k], NEG_INF), sink)
            if k + ahead < len(chains):
                issue_scores(k + ahead)
            o = _dot(e, kv[g, par][1])
            outs[g, par, i] = o / (pltpu.roll(o, HALF, 1) + jnp.exp(sink - m))
        for pair in range(N_PAIRS):
            g, i = divmod(pair, 2)
            o = jnp.where(low, outs[g, 0, i], outs[g, 1, i])
            out_ref[pair] = o
            gate, _ = _silu_and_grad(za_ref[:, pair * LANES:(pair + 1) * LANES])
            ag_ref[:, pair * LANES:(pair + 1) * LANES] = (o * gate).astype(MXU_DTYPE)
        kp_ref[...] = kc_ref[...]
        mixed = _sgu_mix(vln, w_ref, bias_ref)
        for pair in range(N_SGU_HEADS // 2):
            cols = slice(pair * LANES, (pair + 1) * LANES)
            gate, _ = _silu_and_grad(zs_ref[:, cols])
            sg_ref[:, cols] = (u[:, cols] * mixed[pair] * gate).astype(MXU_DTYPE)

    blk = lambda w: pl.BlockSpec((BLOCK, w), lambda n: (n, 0))
    tiles = pl.BlockSpec((N_PAIRS, BLOCK, LANES), lambda n: (0, n, 0))
    gate = lambda k: pl.BlockSpec((None, BLOCK, SGU_W), lambda n: (k, n, 0))
    return pl.pallas_call(
        body,
        name="mixers_fwd",
        grid=(N_BLOCKS,),
        in_specs=[pl.BlockSpec(memory_space=pltpu.SMEM), tiles, blk(KVX_W), gate(0), gate(1), gate(2), gate(3),
                  _full((1, SGU_W)), _full((1, SGU_W)), _full((N_SGU_HEADS, BLOCK, BLOCK)), _full((BLOCK, SGU_W))],
        out_specs=(tiles, blk(ATTN_W), blk(SGU_W)),
        out_shape=(jax.ShapeDtypeStruct((N_PAIRS, SEQ, LANES), F32),
                   jax.ShapeDtypeStruct((SEQ, ATTN_W), MXU_DTYPE),
                   jax.ShapeDtypeStruct((SEQ, SGU_W), MXU_DTYPE)),
        scratch_shapes=[pltpu.VMEM((BLOCK, KVX_W), MXU_DTYPE)],
        compiler_params=_params(("arbitrary",)),
    )(sinks, q, kvx, gates, gates, gates, gates, ln_g, ln_b, sgu_w, bias_full)


def _sgu_activations(us, vs, lng, lnb):
    u = _gelu(us)
    vg = _gelu(vs)
    mu = jnp.mean(vg, axis=-1, keepdims=True)
    xc = vg - mu
    rstd = lax.rsqrt(jnp.mean(xc * xc, axis=-1, keepdims=True) + NORM_EPS)
    vhat = xc * rstd
    return u, vhat, rstd, vhat * lng + lnb


def _sgu_mix(vln, w_ref, bias_ref):
    low = lax.broadcasted_iota(jnp.int32, (BLOCK, LANES), 1) < HALF
    tril = (lax.broadcasted_iota(jnp.int32, (BLOCK, BLOCK), 0)
            >= lax.broadcasted_iota(jnp.int32, (BLOCK, BLOCK), 1))
    mixed = []
    for pair in range(N_SGU_HEADS // 2):
        vp = vln[:, pair * LANES:(pair + 1) * LANES]
        w0 = jnp.where(tril, w_ref[2 * pair], 0.0)
        w1 = jnp.where(tril, w_ref[2 * pair + 1], 0.0)
        mixed.append(_dot(w0, jnp.where(low, vp, 0.0)) + _dot(w1, jnp.where(low, 0.0, vp))
                     + bias_ref[:, pair * LANES:(pair + 1) * LANES])
    return mixed


def _out_proj_loss(ag, sg, x, target, wout, b_out, final_g):
    tm = FWD_TOKEN_TILE

    def body(ag_ref, sg_ref, x_ref, t_ref, w_ref, b_ref, gf_ref, gres_ref, dmix_ref, gw_ref, vec_ref):
        @pl.when(pl.program_id(0) == 0)
        def _():
            gw_ref[...] = jnp.zeros_like(gw_ref)
            vec_ref[...] = jnp.zeros_like(vec_ref)

        a = ag_ref[...]
        s = sg_ref[...]
        xo = x_ref[...] + (_dot(a, w_ref[0:ATTN_W, :]) + _dot(s, w_ref[ATTN_W:, :])) + b_ref[...]
        r = lax.rsqrt(jnp.mean(xo * xo, axis=-1, keepdims=True) + NORM_EPS)
        xn = xo * r
        gf = gf_ref[...]
        err = xn * gf - t_ref[...]
        loss = 0.5 * jnp.sum(jnp.mean(err * err, axis=-1, keepdims=True), axis=0, keepdims=True)
        dy = err * (1.0 / D_MODEL)
        dxn = dy * gf
        gres = r * (dxn - xn * jnp.mean(dxn * xn, axis=-1, keepdims=True))
        vec_ref[0:1, :] += jnp.broadcast_to(loss, (1, D_MODEL))
        vec_ref[1:2, :] += jnp.sum(dy * xn, axis=0, keepdims=True)
        vec_ref[2:3, :] += jnp.sum(gres, axis=0, keepdims=True)
        gres_ref[...] = gres
        gb = gres.astype(MXU_DTYPE)
        dmix_ref[0] = _dot(gb, w_ref[0:ATTN_W, :], NT)
        dmix_ref[1] = _dot(gb, w_ref[ATTN_W:, :], NT)
        gw_ref[0:ATTN_W, :] += _dot(a, gb, TN)
        gw_ref[ATTN_W:, :] += _dot(s, gb, TN)

    tile = lambda w: pl.BlockSpec((tm, w), lambda i: (i, 0))
    return pl.pallas_call(
        body,
        name="out_proj_loss",
        grid=(SEQ // tm,),
        in_specs=[tile(ATTN_W), tile(SGU_W), tile(D_MODEL), tile(D_MODEL),
                  _full((D_MODEL, D_MODEL)), _full((1, D_MODEL)), _full((1, D_MODEL))],
        out_specs=(tile(D_MODEL), pl.BlockSpec((2, tm, ATTN_W), lambda i: (0, i, 0)), _full((D_MODEL, D_MODEL)),
                   _full((8, D_MODEL))),
        out_shape=(jax.ShapeDtypeStruct((SEQ, D_MODEL), F32),
                   jax.ShapeDtypeStruct((2, SEQ, ATTN_W), F32),
                   jax.ShapeDtypeStruct((D_MODEL, D_MODEL), F32),
                   jax.ShapeDtypeStruct((8, D_MODEL), F32)),
        compiler_params=_params(("arbitrary",), VMEM_LIMIT),
    )(ag, sg, x, target, wout, b_out, final_g)


def _mixers_bwd(sinks, dmix, q, kvx, out, gates, ln_g, ln_b, sgu_w, bias_full, gwout):
    last = N_BLOCKS - 1

    def body(sink_ref, d_ref, q_ref, kc_ref, o_ref, za_ref, dsg_ref, us_ref, vs_ref, zs_ref, lng_ref, lnb_ref, w_ref,
             bias_ref, gwout_ref,
             dp_ref, gsink_ref, gbin_ref, dps_ref, gw_ref, gb_ref, gln_ref, gbins_ref, wout_shard_ref,
             kp_ref, pend_ref, carry_ref, wt_ref, gbias_ref, sa_w, ra_w, sb_w, rc_w, send_sems, recv_sems):
        n = pl.program_id(0)
        start, exchange, finish = _reduce_scatter_plan(_Copies(send_sems, recv_sems), 0, gwout_ref, WOUT_ROWS,
                                                       sa_w, ra_w, sb_w, rc_w, wout_shard_ref)
        tril = (lax.broadcasted_iota(jnp.int32, (BLOCK, BLOCK), 0)
                >= lax.broadcasted_iota(jnp.int32, (BLOCK, BLOCK), 1))

        @pl.when(n == 0)
        def _():
            gsink_ref[...] = jnp.zeros_like(gsink_ref)
            gbin_ref[...] = jnp.zeros_like(gbin_ref)
            carry_ref[...] = jnp.zeros_like(carry_ref)
            kp_ref[...] = jnp.zeros_like(kp_ref)
            gw_ref[...] = jnp.zeros_like(gw_ref)
            gln_ref[...] = jnp.zeros_like(gln_ref)
            gbins_ref[...] = jnp.zeros_like(gbins_ref)
            gbias_ref[...] = jnp.zeros_like(gbias_ref)
            for hh in range(N_SGU_HEADS):
                wt_ref[hh] = jnp.where(tril, w_ref[hh], 0.0).T.astype(MXU_DTYPE)
            start()

        pl.when(n == 3)(exchange)
        pl.when(n == 12)(finish)

        @pl.when(n > 0)
        def _():
            dp_ref[:, 0:ATTN_W] = pend_ref[:, 0:ATTN_W]
            dp_ref[:, GATE0:ATTN_SECTION] = pend_ref[:, ATTN_W:]

        @pl.when(n > last)
        def _():
            dp_ref[:, KV0:GATE0] = carry_ref[...].astype(MXU_DTYPE)

        @pl.when(n <= last)
        def _():
            us = us_ref[...]
            vs = vs_ref[...]
            lng = lng_ref[...]
            u, vhat, rstd, vln = _sgu_activations(us, vs, lng, lnb_ref[...])
            low_sgu = lax.broadcasted_iota(jnp.int32, (BLOCK, LANES), 1) < HALF
            sgu = {}

            def sgu_gates():
                mixed = _sgu_mix(vln, w_ref, bias_ref)
                sgu["du"], sgu["dzs"], sgu["dm"] = [], [], []
                for pair in range(N_SGU_HEADS // 2):
                    cols = slice(pair * LANES, (pair + 1) * LANES)
                    dsg = dsg_ref[:, cols]
                    gate, gate_grad = _silu_and_grad(zs_ref[:, cols])
                    up = u[:, cols]
                    sgu["du"].append(dsg * mixed[pair] * gate)
                    sgu["dzs"].append(dsg * up * mixed[pair] * gate_grad)
                    dmixed = dsg * up * gate
                    gbias_ref[:, cols] += dmixed
                    sgu["dm"].append((jnp.where(low_sgu, dmixed, 0.0).astype(MXU_DTYPE),
                                      jnp.where(low_sgu, 0.0, dmixed).astype(MXU_DTYPE)))

            def sgu_grads():
                dvln_parts = []
                for pair in range(N_SGU_HEADS // 2):
                    dm_lo, dm_hi = sgu["dm"][pair]
                    vp = vln[:, pair * LANES:(pair + 1) * LANES]
                    gw_ref[2 * pair] += _dot(dm_lo, vp, NT)
                    gw_ref[2 * pair + 1] += _dot(dm_hi, vp, NT)
                    dvln_parts.append(_dot(wt_ref[2 * pair], dm_lo) + _dot(wt_ref[2 * pair + 1], dm_hi))
                dvln = jnp.concatenate(dvln_parts, axis=1)
                gln_ref[0:1, :] += jnp.sum(dvln * vhat, axis=0, keepdims=True)
                gln_ref[1:2, :] += jnp.sum(dvln, axis=0, keepdims=True)
                dvhat = dvln * lng
                dvg = rstd * (dvhat - jnp.mean(dvhat, axis=-1, keepdims=True)
                              - vhat * jnp.mean(dvhat * vhat, axis=-1, keepdims=True))
                dus = jnp.concatenate(sgu["du"], axis=1) * _gelu_grad(us)
                dvs = dvg * _gelu_grad(vs)
                dzs = jnp.concatenate(sgu["dzs"], axis=1)
                for k, val in enumerate((dus, dvs, dzs)):
                    dps_ref[:, k * SGU_W:(k + 1) * SGU_W] = val.astype(MXU_DTYPE)
                    gbins_ref[:, k * SGU_W:(k + 1) * SGU_W] += jnp.sum(val, axis=0, keepdims=True)

            valid = _window_mask(n)[0:BLOCK]
            low = lax.broadcasted_iota(jnp.int32, (BLOCK, LANES), 1) < HALF
            low_keys = lax.broadcasted_iota(jnp.int32, (2 * BLOCK, LANES), 1) < HALF
            lane_row = lax.broadcasted_iota(jnp.int32, (1, LANES), 1)
            gsink = jnp.zeros((1, LANES), F32)
            chains = [(g, par, i) for g in range(2) for par in range(2) for i in range(2)]
            kv = {(g, par): _kv_cat(kp_ref, kc_ref, 2 * g + par, False) for g in range(2) for par in range(2)}
            ones_keys = jnp.ones((2 * BLOCK, LANES), MXU_DTYPE)
            half_of_lane = lax.broadcasted_iota(jnp.int32, (LANES, 2 * LANES), 0) // HALF
            half_of_col = lax.broadcasted_iota(jnp.int32, (LANES, 2 * LANES), 1) // LANES
            sum_halves = (half_of_lane == half_of_col).astype(MXU_DTYPE)
            douts, deltas = [], []
            for pair in range(N_PAIRS):
                lanes = slice(pair * LANES, (pair + 1) * LANES)
                dg = d_ref[:, lanes]
                gate, gate_grad = _silu_and_grad(za_ref[:, lanes])
                o = o_ref[pair]
                dout = dg * gate
                dza = dg * o * gate_grad
                douts.append(dout.astype(MXU_DTYPE))
                deltas.append(_dot(dout * o, sum_halves))
                zl = slice(ATTN_W + pair * LANES, ATTN_W + (pair + 1) * LANES)
                pend_ref[:, zl] = dza.astype(MXU_DTYPE)
                gl = slice(GATE0 + pair * LANES, GATE0 + (pair + 1) * LANES)
                gbin_ref[:, gl] += jnp.sum(dza, axis=0, keepdims=True)

            first = {}

            def issue_first(k):
                g, par, i = chains[k]
                first[k] = (_dot(q_ref[2 * g + i], kv[g, par][0], NT), _dot(douts[2 * g + i], kv[g, par][1], NT))

            numerators = {}

            def issue_row_sums(k):
                g, par, i = chains[k]
                sink = sink_ref[4 * g + 2 * i + par]
                e, m = _softmax_numerator(jnp.where(valid, first[k][0], NEG_INF), sink)
                numerators[k] = (e, jnp.exp(sink - m), _dot(e, ones_keys))

            ahead = ATTN_BWD_AHEAD
            for k in range(ahead):
                issue_first(k)
            issue_row_sums(0)
            issue_row_sums(1)
            dqs, dk_parts, dv_parts = {}, {}, {}
            operands = {}

            def issue_last(k):
                g, par, i = chains[k]
                ds, ds_t, p_t = operands.pop(k)
                dq = _dot(ds, kv[g, par][0])
                dqs[g, i] = dq if par == 0 else dqs[g, i] + dq
                dk = _dot(ds_t, q_ref[2 * g + i])
                dv = _dot(p_t, douts[2 * g + i])
                dk_parts[g, par] = dk if i == 0 else dk_parts[g, par] + dk
                dv_parts[g, par] = dv if i == 0 else dv_parts[g, par] + dv

            for k, (g, par, i) in enumerate(chains):
                h = 4 * g + 2 * i + par
                delta = deltas[2 * g + i][:, par * LANES:(par + 1) * LANES]
                e, at_sink, row_sum = numerators[k]
                inv = 1.0 / (row_sum + at_sink)
                p = e * jnp.tile(inv, (1, 2))
                ds = p * (first[k][1] - jnp.tile(delta, (1, 2)))
                operands[k] = (ds.astype(MXU_DTYPE), ds.T.astype(MXU_DTYPE), p.T.astype(MXU_DTYPE))
                total = jnp.sum(at_sink * inv * delta, axis=0, keepdims=True)
                gsink = jnp.where(lane_row == h, -total, gsink)
                if k + ahead < len(chains):
                    issue_first(k + ahead)
                if k + 2 < len(chains):
                    issue_row_sums(k + 2)
                if k > 0:
                    issue_last(k - 1)
                if k == len(chains) // 2 - 1:
                    sgu_gates()
            issue_last(len(chains) - 1)
            sgu_grads()
            for pair in range(N_PAIRS):
                g, i = divmod(pair, 2)
                dq = dqs[g, i] * SCALE
                lanes = slice(pair * LANES, (pair + 1) * LANES)
                pend_ref[:, lanes] = dq.astype(MXU_DTYPE)
                gbin_ref[:, lanes] += jnp.sum(dq, axis=0, keepdims=True)
            gsink_ref[...] += gsink
            for k, parts in enumerate((dk_parts, dv_parts)):
                masked = {key: jnp.where(low_keys if key[1] == 0 else jnp.logical_not(low_keys), val, 0.0)
                          for key, val in parts.items()}
                both = (masked[0, 0] + masked[1, 1]
                        + pltpu.roll(masked[0, 1] + masked[1, 0], HALF, 1))
                lanes = slice(k * KV_W, (k + 1) * KV_W)
                done = carry_ref[:, lanes] + both[0:BLOCK]
                dp_ref[:, KV0 + k * KV_W:KV0 + (k + 1) * KV_W] = done.astype(MXU_DTYPE)
                carry_ref[:, lanes] = both[BLOCK:]
                gbin_ref[:, KV0 + k * KV_W:KV0 + (k + 1) * KV_W] += jnp.sum(both, axis=0, keepdims=True)
            kp_ref[...] = kc_ref[...]

        @pl.when(n == last)
        def _():
            for hh in range(N_SGU_HEADS):
                gw_ref[hh] = jnp.where(tril, gw_ref[hh], 0.0)
            head_of_lane = lax.broadcasted_iota(jnp.int32, (N_SGU_HEADS, SGU_W), 1) // HEAD_DIM
            select = (head_of_lane == lax.broadcasted_iota(jnp.int32, (N_SGU_HEADS, SGU_W), 0)).astype(F32)
            gb_ref[...] = lax.dot_general(select, gbias_ref[...], NT, precision=lax.Precision.HIGHEST,
                                          preferred_element_type=F32)

    at = lambda n: jnp.minimum(n, last)
    blk = lambda w: pl.BlockSpec((BLOCK, w), lambda n: (at(n), 0))
    tiles = pl.BlockSpec((N_PAIRS, BLOCK, LANES), lambda n: (0, at(n), 0))
    section = lambda k: pl.BlockSpec((None, BLOCK, SGU_W), lambda n: (k, at(n), 0))
    return pl.pallas_call(
        body,
        name="mixers_bwd",
        grid=(N_BLOCKS + 1,),
        in_specs=[pl.BlockSpec(memory_space=pltpu.SMEM),
                  section(0),
                  tiles,
                  blk(KVX_W),
                  tiles,
                  section(0),
                  section(1),
                  section(1), section(2), section(3),
                  _full((1, SGU_W)), _full((1, SGU_W)), _full((N_SGU_HEADS, BLOCK, BLOCK)), _full((BLOCK, SGU_W)),
                  VMEM_SPEC],
        out_specs=(pl.BlockSpec((BLOCK, ATTN_SECTION), lambda n: (jnp.maximum(n - 1, 0), 0)),
                   _full((1, LANES)), _full((1, ATTN_SECTION)),
                   pl.BlockSpec((BLOCK, SGU_SECTION), lambda n: (at(n), 0)),
                   _full((N_SGU_HEADS, BLOCK, BLOCK)), _full((N_SGU_HEADS, BLOCK)),
                   _full((8, SGU_W)), _full((1, SGU_SECTION)), VMEM_SPEC),
        out_shape=(jax.ShapeDtypeStruct((SEQ, ATTN_SECTION), MXU_DTYPE),
                   jax.ShapeDtypeStruct((1, LANES), F32),
                   jax.ShapeDtypeStruct((1, ATTN_SECTION), F32),
                   jax.ShapeDtypeStruct((SEQ, SGU_SECTION), MXU_DTYPE),
                   jax.ShapeDtypeStruct((N_SGU_HEADS, BLOCK, BLOCK), F32),
                   jax.ShapeDtypeStruct((N_SGU_HEADS, BLOCK), F32),
                   jax.ShapeDtypeStruct((8, SGU_W), F32),
                   jax.ShapeDtypeStruct((1, SGU_SECTION), F32),
                   jax.ShapeDtypeStruct((WOUT_ROWS, D_MODEL), F32)),
        scratch_shapes=([pltpu.VMEM((BLOCK, KVX_W), MXU_DTYPE),
                         pltpu.VMEM((BLOCK, 2 * ATTN_W), MXU_DTYPE), pltpu.VMEM((BLOCK, 2 * KV_W), F32),
                         pltpu.VMEM((N_SGU_HEADS, BLOCK, BLOCK), MXU_DTYPE), pltpu.VMEM((BLOCK, SGU_W), F32)]
                        + _reduce_scatter_scratch(WOUT_ROWS, D_MODEL, COMM_DTYPE) + _dma_sems(REDUCE_SEMS)),
        compiler_params=_params(("arbitrary",), VMEM_LIMIT),
    )(sinks, dmix, q, kvx, out, gates, dmix, gates, gates, gates, ln_g, ln_b, sgu_w, bias_full, gwout)


def _in_proj_bwd(dpa, dps, win_t, x, norm_g, gres, gwin, vec_parts):
    tm = TOKEN_TILE
    steps = SEQ // tm
    n_parts = len(vec_parts)

    def body(da_ref, ds_ref, w_ref, x_ref, g_ref, gres_ref, gwin_ref, *rest):
        part_refs = rest[:n_parts]
        gx_ref, shard_ref, vec_out_ref, gng_ref, sa, ra, sb, rc, vec_ref, ra_vec, slots, send_sems, recv_sems = (
            rest[n_parts:])
        step = pl.program_id(0)
        copies = _Copies(send_sems, recv_sems)
        start, exchange, finish = _reduce_scatter_plan(copies, 0, gwin_ref, WIN_ROWS, sa, ra, sb, rc, shard_ref)

        @pl.when(step == 0)
        def _():
            gng_ref[...] = jnp.zeros_like(gng_ref)
            start()

        pl.when(step == 2)(exchange)

        dh = _dot(da_ref[...], w_ref[0:ATTN_SECTION, :]) + _dot(ds_ref[...], w_ref[ATTN_SECTION:, :])
        xv = x_ref[...]
        r = lax.rsqrt(jnp.mean(xv * xv, axis=-1, keepdims=True) + NORM_EPS)
        xn = xv * r
        gng_ref[...] += jnp.sum(dh * xn, axis=0, keepdims=True)
        dxn = dh * g_ref[...]
        gx_ref[...] = r * (dxn - xn * jnp.mean(dxn * xn, axis=-1, keepdims=True)) + gres_ref[...]

        @pl.when(step == steps - 1)
        def _():
            finish()
            _all_reduce_vectors(copies, REDUCE_SEMS, gng_ref, *part_refs, vec_out_ref, vec_ref, ra_vec, slots)

    tile = lambda w: pl.BlockSpec((tm, w), lambda i: (i, 0))
    return pl.pallas_call(
        body,
        name="in_proj_bwd",
        grid=(steps,),
        in_specs=[tile(ATTN_SECTION), tile(SGU_SECTION), _full((IN_W, D_MODEL)), tile(D_MODEL),
                  _full((1, D_MODEL)), tile(D_MODEL), VMEM_SPEC] + [VMEM_SPEC] * n_parts,
        out_specs=(tile(D_MODEL), VMEM_SPEC, VMEM_SPEC),
        out_shape=(jax.ShapeDtypeStruct((SEQ, D_MODEL), F32),
                   jax.ShapeDtypeStruct((WIN_ROWS, D_MODEL), F32),
                   jax.ShapeDtypeStruct((VEC_ROWS, IN_W), F32)),
        scratch_shapes=([pltpu.VMEM((1, D_MODEL), F32)] + _reduce_scatter_scratch(WIN_ROWS, D_MODEL, COMM_DTYPE)
                        + _vector_scratch() + _dma_sems(REDUCE_SEMS + VECTOR_SEMS)),
        compiler_params=_params(("arbitrary",), VMEM_LIMIT),
    )(dpa, dps, win_t, x, norm_g, gres, gwin, *vec_parts)


def _win_grad(dpa, dps, h, gsguw):
    rows = 256
    n_attn = ATTN_SECTION // rows
    steps = n_attn + SGU_SECTION // rows

    def body(da_ref, ds_ref, h_ref, gsguw_ref, o_ref, sguw_full_ref, sa, ra, sb, rc, landing, send_sems, recv_sems):
        step = pl.program_id(0)
        copies = _Copies(send_sems, recv_sems)
        own_sguw = landing.at[_block_rows(_place(), SGUW_ROWS), :]
        start, exchange, finish = _reduce_scatter_plan(copies, 0, gsguw_ref, SGUW_ROWS, sa, ra, sb, rc, own_sguw)
        gather = _gather_plan(copies, REDUCE_SEMS, landing, SGUW_ROWS)

        pl.when(step == 0)(start)
        pl.when(step == 2)(exchange)

        @pl.when(step == 5)
        def _():
            finish()
            gather[0]()

        pl.when(step == 7)(gather[1])

        @pl.when(step < n_attn)
        def _():
            o_ref[...] = _dot(da_ref[...], h_ref[...], TN)

        @pl.when(step >= n_attn)
        def _():
            o_ref[...] = _dot(ds_ref[...], h_ref[...], TN)

        @pl.when(step == steps - 1)
        def _():
            gather[2]()
            sguw_full_ref[...] = landing[...]

    return pl.pallas_call(
        body,
        name="win_grad",
        grid=(steps,),
        in_specs=[pl.BlockSpec((SEQ, rows), lambda i: (0, jnp.minimum(i, n_attn - 1))),
                  pl.BlockSpec((SEQ, rows), lambda i: (0, jnp.maximum(i - n_attn, 0))),
                  _full((SEQ, D_MODEL)), VMEM_SPEC],
        out_specs=(pl.BlockSpec((rows, D_MODEL), lambda i: (i, 0)), _full((N_SGU_HEADS * BLOCK, BLOCK))),
        out_shape=(jax.ShapeDtypeStruct((IN_W, D_MODEL), F32),
                   jax.ShapeDtypeStruct((N_SGU_HEADS * BLOCK, BLOCK), F32)),
        scratch_shapes=(_reduce_scatter_scratch(SGUW_ROWS, BLOCK, F32)
                        + [pltpu.VMEM((N_SGU_HEADS * BLOCK, BLOCK), F32)]
                        + _dma_sems(REDUCE_SEMS + GATHER_SEMS)),
        compiler_params=_params(("arbitrary",), VMEM_LIMIT),
    )(dpa, dps, h, gsguw)


VEC_NORM_G, VEC_B_IN, VEC_SINKS, VEC_LN_G, VEC_LN_B, VEC_B_OUT, VEC_FINAL_G, VEC_LOSS, VEC_SGU_B = 0, 1, 2, 3, 4, 5, 6, 7, 8


def _adamw(w, g, m, v):
    m = ADAM_B1 * m + (1.0 - ADAM_B1) * g
    v = ADAM_B2 * v + (1.0 - ADAM_B2) * (g * g)
    m_hat = m / (1.0 - ADAM_B1 ** ADAM_STEP)
    v_hat = v / (1.0 - ADAM_B2 ** ADAM_STEP)
    delta = -ADAM_LR * (m_hat / (jnp.sqrt(v_hat) + ADAM_EPS) + ADAM_WD * w)
    return delta, m, v


def _adamw_shard(name, g, w, m, v, block_rows):
    def body(g_ref, w_ref, m_ref, v_ref, d_ref, nm_ref, nv_ref):
        d_ref[...], nm_ref[...], nv_ref[...] = _adamw(w_ref[...], g_ref[...], m_ref[...], v_ref[...])

    rows, cols = w.shape
    spec = pl.BlockSpec((block_rows, cols), lambda i: (i, 0))
    return pl.pallas_call(
        body,
        name=name,
        grid=(rows // block_rows,),
        in_specs=[spec] * 4,
        out_specs=(spec,) * 3,
        out_shape=(jax.ShapeDtypeStruct(w.shape, F32),) * 3,
        compiler_params=_params(("arbitrary",)),
    )(g, w, m, v)


VECTOR_SEMS = 4


def _vector_scratch():
    return [pltpu.VMEM((VEC_ROWS, IN_W), F32), pltpu.VMEM((VEC_ROWS, IN_W), F32),
            pltpu.VMEM((4 * VEC_ROWS, IN_W), F32)]


def _all_reduce_vectors(copies, sem0, gng_ref, gba_ref, gbs_ref, gsink_ref, gln_ref, gsgub_ref, vec4_ref, out_ref,
                        vec_ref, ra_vec, slots):
    x, y, c = _place()
    vec_ref[...] = jnp.zeros_like(vec_ref)
    vec_ref[VEC_NORM_G:VEC_NORM_G + 1, 0:D_MODEL] = gng_ref[...]
    vec_ref[VEC_B_IN:VEC_B_IN + 1, 0:ATTN_SECTION] = gba_ref[...]
    vec_ref[VEC_B_IN:VEC_B_IN + 1, ATTN_SECTION:IN_W] = gbs_ref[...]
    vec_ref[VEC_SINKS:VEC_SINKS + 1, 0:LANES] = gsink_ref[...]
    vec_ref[VEC_LN_G:VEC_LN_G + 1, 0:SGU_W] = gln_ref[0:1, :]
    vec_ref[VEC_LN_B:VEC_LN_B + 1, 0:SGU_W] = gln_ref[1:2, :]
    vec_ref[VEC_B_OUT:VEC_B_OUT + 1, 0:D_MODEL] = vec4_ref[2:3, :]
    vec_ref[VEC_FINAL_G:VEC_FINAL_G + 1, 0:D_MODEL] = vec4_ref[1:2, :]
    vec_ref[VEC_LOSS:VEC_LOSS + 1, 0:D_MODEL] = vec4_ref[0:1, :]
    vec_ref[VEC_SGU_B:VEC_SGU_B + N_SGU_HEADS, 0:BLOCK] = gsgub_ref[...]

    to_sibling = copies(sem0, vec_ref, ra_vec, (x, y, 1 - c))
    to_sibling.start()
    to_sibling.wait_recv()

    def chip_slot(place):
        return slots.at[pl.ds(pl.multiple_of((2 * place[0] + place[1]) * VEC_ROWS, 8), VEC_ROWS), :]

    mine = chip_slot((x, y))
    mine[...] = vec_ref[...] + ra_vec[...]
    to_chips = [copies(sem0 + i, mine, mine, (*_chip(rel), c)) for i, rel in enumerate(RELATIONS[1:], start=1)]
    for cp in to_chips:
        cp.start()
    for i, rel in enumerate(RELATIONS[1:], start=1):
        theirs = chip_slot(_chip(rel))
        copies(sem0 + i, theirs, theirs, (x, y, c)).wait_recv()
    out_ref[...] = ((slots[0:VEC_ROWS, :] + slots[VEC_ROWS:2 * VEC_ROWS, :])
                    + slots[2 * VEC_ROWS:3 * VEC_ROWS, :]) + slots[3 * VEC_ROWS:, :]
    to_sibling.wait_send()
    for cp in to_chips:
        cp.wait_send()


def _adamw_replicated(vec, gsguw, weights, m_state, v_state):
    n = len(SMALL)

    def body(*refs):
        vec_ref, gsguw_ref = refs[0], refs[1]
        w_refs, m_refs, v_refs = (refs[2 + k * n:2 + (k + 1) * n] for k in range(3))
        outs = refs[2 + 3 * n:]
        g_refs, d_refs, nm_refs, nv_refs = (outs[k * n:(k + 1) * n] for k in range(4))
        for i, (_, row, shape) in enumerate(SMALL):
            g = gsguw_ref[...] if row is None else vec_ref[row:row + shape[0], 0:shape[1]]
            g_refs[i][...] = g
            d_refs[i][...], nm_refs[i][...], nv_refs[i][...] = _adamw(
                w_refs[i][...], g, m_refs[i][...], v_refs[i][...])

    shapes = tuple(jax.ShapeDtypeStruct(shape, F32) for _, _, shape in SMALL)
    outs = pl.pallas_call(
        body,
        name="adamw_replicated",
        in_specs=[VMEM_SPEC] * (2 + 3 * n),
        out_specs=(VMEM_SPEC,) * (4 * n),
        out_shape=shapes * 4,
    )(vec, gsguw, *weights, *m_state, *v_state)
    return tuple(outs[k * n:(k + 1) * n] for k in range(4))


SMALL = (
    ("norm_g", VEC_NORM_G, (1, D_MODEL)),
    ("b_in", VEC_B_IN, (1, IN_W)),
    ("attn_sinks", VEC_SINKS, (1, N_Q_HEADS)),
    ("sgu_ln_g", VEC_LN_G, (1, SGU_W)),
    ("sgu_ln_b", VEC_LN_B, (1, SGU_W)),
    ("sgu_w", None, (N_SGU_HEADS * BLOCK, BLOCK)),
    ("sgu_b", VEC_SGU_B, (N_SGU_HEADS, BLOCK)),
    ("b_out", VEC_B_OUT, (1, D_MODEL)),
    ("final_norm_g", VEC_FINAL_G, (1, D_MODEL)),
)


def _local_grads(x, target, win_t, wout_shard, norm_g, b_in, attn_sinks, sgu_ln_g, sgu_ln_b, sgu_w, sgu_b, b_out,
                 final_g):
    sinks = attn_sinks.reshape(N_Q_HEADS)
    bias_full = jnp.repeat(sgu_b.T, HEAD_DIM, axis=1)
    h, q, kvx, gates, wout = _in_proj(x, norm_g, b_in, win_t, wout_shard)
    out, ag, sg = _mixers_fwd(sinks, q, kvx, gates, sgu_ln_g, sgu_ln_b, sgu_w, bias_full)
    gres, dmix, gwout, vec4 = _out_proj_loss(ag, sg, x, target, wout, b_out, final_g)
    dpa, gsink, gbin_a, dps, gsguw, gsgub, gln, gbin_s, gwout_shard = _mixers_bwd(
        sinks, dmix, q, kvx, out, gates, sgu_ln_g, sgu_ln_b, sgu_w, bias_full, gwout)
    gwin, gsguw_sum = _win_grad(dpa, dps, h, gsguw.reshape(N_SGU_HEADS * BLOCK, BLOCK))
    grad_x, gwin_shard, vec = _in_proj_bwd(dpa, dps, win_t, x, norm_g, gres, gwin,
                                           (gbin_a, gbin_s, gsink, gln, gsgub, vec4))
    return grad_x, gwin_shard, gwout_shard, gsguw_sum, vec


def kernel(x, norm_g, w_in, b_in, attn_sinks, sgu_ln_g, sgu_ln_b, sgu_w, sgu_b, w_out, b_out, final_norm_g, loss_target, m_norm_g, m_w_in, m_b_in, m_attn_sinks, m_sgu_ln_g, m_sgu_ln_b, m_sgu_w, m_sgu_b, m_w_out, m_b_out, m_final_norm_g, v_norm_g, v_w_in, v_b_in, v_attn_sinks, v_sgu_ln_g, v_sgu_ln_b, v_sgu_w, v_sgu_b, v_w_out, v_b_out, v_final_norm_g):
    given = dict(norm_g=norm_g, b_in=b_in, attn_sinks=attn_sinks, sgu_ln_g=sgu_ln_g, sgu_ln_b=sgu_ln_b,
                 sgu_w=sgu_w, sgu_b=sgu_b, b_out=b_out, final_norm_g=final_norm_g)
    m_given = dict(norm_g=m_norm_g, b_in=m_b_in, attn_sinks=m_attn_sinks, sgu_ln_g=m_sgu_ln_g,
                   sgu_ln_b=m_sgu_ln_b, sgu_w=m_sgu_w, sgu_b=m_sgu_b, b_out=m_b_out, final_norm_g=m_final_norm_g)
    v_given = dict(norm_g=v_norm_g, b_in=v_b_in, attn_sinks=v_attn_sinks, sgu_ln_g=v_sgu_ln_g,
                   sgu_ln_b=v_sgu_ln_b, sgu_w=v_sgu_w, sgu_b=v_sgu_b, b_out=v_b_out, final_norm_g=v_final_norm_g)

    win_t = _all_gather_win(w_in[0].T)
    grad_x, gwin_t, gwout, gsguw, vec = _local_grads(
        x[0], loss_target[0], win_t, w_out[0], norm_g, b_in, attn_sinks, sgu_ln_g, sgu_ln_b, sgu_w[0], sgu_b[0],
        b_out, final_norm_g.reshape(1, D_MODEL))

    t = lambda a: a[0].T
    d_win, nm_win, nv_win = _adamw_shard("adamw_w_in", gwin_t, t(w_in), t(m_w_in), t(v_w_in), WIN_ROWS // 2)
    d_wout, nm_wout, nv_wout = _adamw_shard("adamw_w_out", gwout, w_out[0], m_w_out[0], v_w_out[0], WOUT_ROWS)
    as_2d = lambda d: [d[name].reshape(shape) for name, _, shape in SMALL]
    loss = vec[VEC_LOSS, 0]
    small = _adamw_replicated(vec, gsguw, as_2d(given), as_2d(m_given), as_2d(v_given))

    def assemble(big_in, big_out, k):
        vals = {name: small[k][i].reshape(given[name].shape) for i, (name, _, _) in enumerate(SMALL)}
        vals["w_in"] = big_in.T[None]
        vals["w_out"] = big_out[None]
        order = ("norm_g", "w_in", "b_in", "attn_sinks", "sgu_ln_g", "sgu_ln_b", "sgu_w", "sgu_b", "w_out",
                 "b_out", "final_norm_g")
        return [vals[name] for name in order]

    return (loss, grad_x[None],
            *assemble(gwin_t, gwout, 0), *assemble(d_win, d_wout, 1),
            *assemble(nm_win, nm_wout, 2), *assemble(nv_win, nv_wout, 3))
```

```python
import functools
import math

import jax
import jax.numpy as jnp
from jax import lax
from jax.experimental import pallas as pl
from jax.experimental.pallas import tpu as pltpu

F32 = jnp.float32
BF16 = jnp.bfloat16
MXU_DTYPE = BF16
COMM_DTYPE = BF16

D_MODEL = 1024
SEQ = 4096
HEAD_DIM = 64
N_Q_HEADS = 8
Q_PER_KV = 4
BLOCK = 128
N_BLOCKS = SEQ // BLOCK
ATTN_W = 512
KV_W = 128
SGU_W = 512
N_SGU_HEADS = 8
IN_W = 2816
NORM_EPS = 1e-5
NEG_INF = -1e30
SCALE = HEAD_DIM ** -0.5
KV0 = ATTN_W
GATE0 = ATTN_W + 2 * KV_W
SGU0 = GATE0 + ATTN_W
ATTN_SECTION = SGU0
SGU_SECTION = IN_W - SGU0

ADAM_LR = 0.001
ADAM_B1 = 0.9
ADAM_B2 = 0.999
ADAM_EPS = 1e-08
ADAM_WD = 0.01
ADAM_STEP = 10

N_DEV = 8
WIN_ROWS = IN_W // N_DEV
WOUT_ROWS = D_MODEL // N_DEV
SGUW_ROWS = N_SGU_HEADS * BLOCK // N_DEV
VEC_ROWS = 16
MESH = pl.DeviceIdType.MESH

LANES = 128
HALF = LANES // 2
N_PAIRS = N_Q_HEADS * HEAD_DIM // LANES
KVX_W = 12 * LANES
TOKEN_TILE = 256
FWD_TOKEN_TILE = 512
ATTN_FWD_AHEAD = 4
FWD_BLOCKS_PER_STEP = 4
SGU_MIX_AFTER_CHAIN = 0
SGU_GATES_AFTER_CHAIN = 1
SGU_GRADS_AFTER_CHAIN = 5
ATTN_BWD_AHEAD = 3
VMEM_LIMIT = 56 * 1024 * 1024

NN = (((1,), (0,)), ((), ()))
NT = (((1,), (1,)), ((), ()))
TN = (((0,), (0,)), ((), ()))


def _dot(a, b, dims=NN):
    return lax.dot_general(a.astype(MXU_DTYPE), b.astype(MXU_DTYPE), dims, preferred_element_type=F32)


def _gelu(x):
    return x * (lax.erf(x * (1.0 / math.sqrt(2.0))) + 1.0) * 0.5


def _gelu_grad(x):
    cdf = (lax.erf(x * (1.0 / math.sqrt(2.0))) + 1.0) * 0.5
    return cdf + x * jnp.exp(-0.5 * x * x) * (1.0 / math.sqrt(2.0 * math.pi))


def _silu_and_grad(z):
    s = jax.nn.sigmoid(z)
    return z * s, s * (1.0 + z * (1.0 - s))


def _params(semantics=None, vmem=None):
    kw = {}
    if semantics is not None:
        kw["dimension_semantics"] = semantics
    if vmem is not None:
        kw["vmem_limit_bytes"] = vmem
    return pltpu.CompilerParams(**kw)


def _full(shape):
    return pl.BlockSpec(shape, lambda *_: (0,) * len(shape))


VMEM_SPEC = pl.BlockSpec(memory_space=pltpu.VMEM)


RELATIONS = ((0, 0), (1, 0), (0, 1), (1, 1))


def _place():
    return lax.axis_index("x"), lax.axis_index("y"), lax.axis_index("c")


def _chip(rel):
    x, y, _ = _place()
    return (1 - x if rel[0] else x, 1 - y if rel[1] else y)


def _block_rows(place, n_rows):
    px, py, pc = place
    return pl.ds(pl.multiple_of((4 * px + 2 * py + pc) * n_rows, 16), n_rows)


class _Copies:
    def __init__(self, send_sems, recv_sems):
        self.send_sems, self.recv_sems = send_sems, recv_sems

    def __call__(self, k, src, dst, to):
        return pltpu.make_async_remote_copy(src_ref=src, dst_ref=dst, send_sem=self.send_sems.at[k],
                                            recv_sem=self.recv_sems.at[k], device_id=to, device_id_type=MESH)


def _gather_plan(copies, sem0, full_ref, n_rows):
    x, y, c = _place()
    me, sibling = (x, y, c), (x, y, 1 - c)
    chips = [_chip(rel) for rel in RELATIONS[1:]]

    def cp(k, block, to):
        rows = full_ref.at[_block_rows(block, n_rows), :]
        return copies(sem0 + k, rows, rows, to)

    first = [cp(0, me, sibling)] + [cp(1 + j, me, (*chip, c)) for j, chip in enumerate(chips)]
    passed = [cp(4 + j, (*chip, c), sibling) for j, chip in enumerate(chips)]

    def start():
        for f in first:
            f.start()

    def forward():
        for j, chip in enumerate(chips):
            cp(1 + j, (*chip, c), me).wait_recv()
            passed[j].start()

    def finish():
        cp(0, sibling, me).wait_recv()
        for j, chip in enumerate(chips):
            cp(4 + j, (*chip, 1 - c), me).wait_recv()
        for f in first + passed:
            f.wait_send()

    return start, forward, finish


GATHER_SEMS = 7


def _reduce_scatter_plan(copies, sem0, part_ref, n_rows, sa, ra, sb, rc, res_ref):
    x, y, c = _place()
    sibling = (x, y, 1 - c)
    n = n_rows
    level1 = copies(sem0, sa, ra, sibling)

    def level2(i):
        slot = pl.ds((i - 1) * n, n)
        return copies(sem0 + i, sb.at[slot, :], rc.at[slot, :], (*_chip(RELATIONS[i]), c))

    def start():
        for i, rel in enumerate(RELATIONS):
            sa[i * n:(i + 1) * n, :] = part_ref[_block_rows((*_chip(rel), 1 - c), n), :].astype(sa.dtype)
        level1.start()

    def exchange():
        level1.wait_recv()
        for i, rel in enumerate(RELATIONS):
            total = part_ref[_block_rows((*_chip(rel), c), n), :] + ra[i * n:(i + 1) * n, :].astype(F32)
            if i == 0:
                res_ref[...] = total
            else:
                sb[(i - 1) * n:i * n, :] = total.astype(sb.dtype)
                level2(i).start()

    def finish():
        acc = res_ref[...]
        for i in range(1, len(RELATIONS)):
            level2(i).wait_recv()
            acc = acc + rc[(i - 1) * n:i * n, :].astype(F32)
        res_ref[...] = acc
        level1.wait_send()
        for i in range(1, len(RELATIONS)):
            level2(i).wait_send()

    return start, exchange, finish


REDUCE_SEMS = 4


def _reduce_scatter_scratch(n_rows, width, dtype):
    return [pltpu.VMEM((4 * n_rows, width), dtype), pltpu.VMEM((4 * n_rows, width), dtype),
            pltpu.VMEM((3 * n_rows, width), dtype), pltpu.VMEM((3 * n_rows, width), dtype)]


def _dma_sems(n):
    return [pltpu.SemaphoreType.DMA((n,)), pltpu.SemaphoreType.DMA((n,))]


def _all_gather_win(win_t_shard):
    def body(win_ref, full_ref, send_sems, recv_sems):
        full_ref[_block_rows(_place(), WIN_ROWS), :] = win_ref[...].astype(COMM_DTYPE)
        start, forward, finish = _gather_plan(_Copies(send_sems, recv_sems), 0, full_ref, WIN_ROWS)
        start()
        forward()
        finish()

    return pl.pallas_call(
        body,
        name="all_gather_win",
        out_shape=jax.ShapeDtypeStruct((IN_W, D_MODEL), COMM_DTYPE),
        in_specs=[VMEM_SPEC],
        out_specs=VMEM_SPEC,
        scratch_shapes=_dma_sems(GATHER_SEMS),
        compiler_params=_params(vmem=VMEM_LIMIT),
    )(win_t_shard)


def _in_proj(x, norm_g, b_in, win_t, wout_shard):
    tm = FWD_TOKEN_TILE
    steps = SEQ // tm

    def body(x_ref, g_ref, b_ref, w_ref, wout_ref, h_ref, q_ref, kvx_ref, gate_ref, wfull_ref,
             landing, send_sems, recv_sems):
        step = pl.program_id(0)
        start, forward, finish = _gather_plan(_Copies(send_sems, recv_sems), 0, landing, WOUT_ROWS)

        @pl.when(step == 0)
        def _():
            landing[_block_rows(_place(), WOUT_ROWS), :] = wout_ref[...].astype(COMM_DTYPE)
            start()

        pl.when(step == steps // 2)(forward)

        xv = x_ref[...]
        r = lax.rsqrt(jnp.mean(xv * xv, axis=-1, keepdims=True) + NORM_EPS)
        h = ((xv * r) * g_ref[...]).astype(MXU_DTYPE)
        h_ref[...] = h

        def proj(lo, hi):
            return _dot(h, w_ref[lo:hi, :], NT) + b_ref[:, lo:hi]

        qs = proj(0, ATTN_W) * SCALE
        for pair in range(N_PAIRS):
            q_ref[pair] = qs[:, pair * LANES:(pair + 1) * LANES].astype(MXU_DTYPE)
        kv = proj(KV0, GATE0)
        low = lax.broadcasted_iota(jnp.int32, (tm, LANES), 1) < HALF
        for i in range(2):
            t = kv[:, i * LANES:(i + 1) * LANES]
            rot = pltpu.roll(t, HALF, 1)
            variants = (jnp.where(low, t, 0.0), jnp.where(low, 0.0, rot),
                        jnp.where(low, rot, 0.0), jnp.where(low, 0.0, t))
            for j, val in enumerate(variants):
                col = (4 * i + j) * LANES
                kvx_ref[:, col:col + LANES] = val.astype(MXU_DTYPE)
                if i == 1:
                    ones_elsewhere = jnp.where(low == (j % 2 == 0), val, 1.0)
                    kvx_ref[:, col + 4 * LANES:col + 5 * LANES] = ones_elsewhere.astype(MXU_DTYPE)
        for k in range(4):
            gate_ref[k] = proj(GATE0 + k * SGU_W, GATE0 + (k + 1) * SGU_W)

        @pl.when(step == steps - 1)
        def _():
            finish()
            wfull_ref[...] = landing[...]

    return pl.pallas_call(
        body,
        name="in_proj",
        grid=(steps,),
        in_specs=[pl.BlockSpec((tm, D_MODEL), lambda i: (i, 0)),
                  _full((1, D_MODEL)), _full((1, IN_W)), _full((IN_W, D_MODEL)), VMEM_SPEC],
        out_specs=(pl.BlockSpec((tm, D_MODEL), lambda i: (i, 0)),
                   pl.BlockSpec((N_PAIRS, tm, LANES), lambda i: (0, i, 0)),
                   pl.BlockSpec((tm, KVX_W), lambda i: (i, 0)),
                   pl.BlockSpec((4, tm, SGU_W), lambda i: (0, i, 0)),
                   _full((D_MODEL, D_MODEL))),
        out_shape=(jax.ShapeDtypeStruct((SEQ, D_MODEL), MXU_DTYPE),
                   jax.ShapeDtypeStruct((N_PAIRS, SEQ, LANES), MXU_DTYPE),
                   jax.ShapeDtypeStruct((SEQ, KVX_W), MXU_DTYPE),
                   jax.ShapeDtypeStruct((4, SEQ, SGU_W), F32),
                   jax.ShapeDtypeStruct((D_MODEL, D_MODEL), COMM_DTYPE)),
        scratch_shapes=[pltpu.VMEM((D_MODEL, D_MODEL), COMM_DTYPE)] + _dma_sems(GATHER_SEMS),
        compiler_params=_params(("arbitrary",), VMEM_LIMIT),
    )(x, norm_g, b_in, win_t, wout_shard)


def _window_mask(n):
    qi = lax.broadcasted_iota(jnp.int32, (2 * BLOCK, 2 * BLOCK), 0) & (BLOCK - 1)
    p = lax.broadcasted_iota(jnp.int32, (2 * BLOCK, 2 * BLOCK), 1) - BLOCK
    in_window = jnp.logical_and(p <= qi, p > qi - BLOCK)
    return jnp.logical_and(in_window, jnp.logical_or(p >= 0, n > 0))


def _sink_column(sink_ref, g, par):
    return jnp.concatenate([jnp.full((BLOCK, 1), sink_ref[4 * g + par], F32),
                            jnp.full((BLOCK, 1), sink_ref[4 * g + 2 + par], F32)], axis=0)


def _kv_cat(kp_ref, kc_ref, var, with_ones):
    kcol, vcol = var * LANES, (var + (8 if with_ones else 4)) * LANES
    return (jnp.concatenate([kp_ref[:, kcol:kcol + LANES], kc_ref[:, kcol:kcol + LANES]], axis=0),
            jnp.concatenate([kp_ref[:, vcol:vcol + LANES], kc_ref[:, vcol:vcol + LANES]], axis=0))


def _softmax_numerator(s, sink):
    m = jnp.maximum(jnp.max(s, axis=1, keepdims=True), sink)
    return jnp.exp(s - m), m


def _mixers_fwd(sinks, q, kvx, gates, ln_g, ln_b, sgu_w, bias_full):
    per_step = FWD_BLOCKS_PER_STEP
    rows_per_step = per_step * BLOCK

    def body(sink_ref, q_ref, kc_ref, za_ref, us_ref, vs_ref, zs_ref, lng_ref, lnb_ref, w_ref, bias_ref,
             out_ref, ag_ref, sg_ref, kp_ref):
        @pl.when(pl.program_id(0) == 0)
        def _():
            kp_ref[...] = jnp.zeros_like(kp_ref)

        def one_block(b, carry):
            rows = pl.ds(pl.multiple_of(b * BLOCK, BLOCK), BLOCK)
            kc = kc_ref.at[rows, :]
            u, _, _, vln = _sgu_activations(us_ref[rows, :], vs_ref[rows, :], lng_ref[...], lnb_ref[...])

            valid = _window_mask(pl.program_id(0) * per_step + b)[0:BLOCK]
            chains = [(g, par, i) for g in range(2) for par in range(2) for i in range(2)]
            kv = {(g, par): _kv_cat(kp_ref, kc, 2 * g + par, True) for g in range(2) for par in range(2)}
            scores, outs = {}, {}

            def issue_scores(k):
                g, par, i = chains[k]
                scores[k] = _dot(q_ref[2 * g + i, rows, :], kv[g, par][0], NT)

            ahead = ATTN_FWD_AHEAD
            for k in range(ahead):
                issue_scores(k)
            low = lax.broadcasted_iota(jnp.int32, (BLOCK, LANES), 1) < HALF
            for k, (g, par, i) in enumerate(chains):
                sink = sink_ref[4 * g + 2 * i + par]
                e, m = _softmax_numerator(jnp.where(valid, scores[k], NEG_INF), sink)
                if k + ahead < len(chains):
                    issue_scores(k + ahead)
                o = _dot(e, kv[g, par][1])
                outs[g, par, i] = o / (pltpu.roll(o, HALF, 1) + jnp.exp(sink - m))
                if k == SGU_MIX_AFTER_CHAIN:
                    mixed = _sgu_mix(vln, w_ref, bias_ref)
            for pair in range(N_PAIRS):
                g, i = divmod(pair, 2)
                lanes = slice(pair * LANES, (pair + 1) * LANES)
                o = jnp.where(low, outs[g, 0, i], outs[g, 1, i])
                out_ref[pair, rows, :] = o
                gate, _ = _silu_and_grad(za_ref[rows, lanes])
                ag_ref[rows, lanes] = (o * gate).astype(MXU_DTYPE)
            kp_ref[...] = kc[...]
            for pair in range(N_SGU_HEADS // 2):
                cols = slice(pair * LANES, (pair + 1) * LANES)
                gate, _ = _silu_and_grad(zs_ref[rows, cols])
                sg_ref[rows, cols] = (u[:, cols] * mixed[pair] * gate).astype(MXU_DTYPE)
            return carry

        lax.fori_loop(0, per_step, one_block, 0)

    blk = lambda w: pl.BlockSpec((rows_per_step, w), lambda n: (n, 0))
    tiles = pl.BlockSpec((N_PAIRS, rows_per_step, LANES), lambda n: (0, n, 0))
    gate = lambda k: pl.BlockSpec((None, rows_per_step, SGU_W), lambda n: (k, n, 0))
    return pl.pallas_call(
        body,
        name="mixers_fwd",
        grid=(N_BLOCKS // per_step,),
        in_specs=[pl.BlockSpec(memory_space=pltpu.SMEM), tiles, blk(KVX_W), gate(0), gate(1), gate(2), gate(3),
                  _full((1, SGU_W)), _full((1, SGU_W)), _full((N_SGU_HEADS, BLOCK, BLOCK)), _full((BLOCK, SGU_W))],
        out_specs=(tiles, blk(ATTN_W), blk(SGU_W)),
        out_shape=(jax.ShapeDtypeStruct((N_PAIRS, SEQ, LANES), F32),
                   jax.ShapeDtypeStruct((SEQ, ATTN_W), MXU_DTYPE),
                   jax.ShapeDtypeStruct((SEQ, SGU_W), MXU_DTYPE)),
        scratch_shapes=[pltpu.VMEM((BLOCK, KVX_W), MXU_DTYPE)],
        compiler_params=_params(("arbitrary",)),
    )(sinks, q, kvx, gates, gates, gates, gates, ln_g, ln_b, sgu_w, bias_full)


def _sgu_activations(us, vs, lng, lnb):
    u = _gelu(us)
    vg = _gelu(vs)
    mu = jnp.mean(vg, axis=-1, keepdims=True)
    xc = vg - mu
    rstd = lax.rsqrt(jnp.mean(xc * xc, axis=-1, keepdims=True) + NORM_EPS)
    vhat = xc * rstd
    return u, vhat, rstd, vhat * lng + lnb


def _sgu_mix(vln, w_ref, bias_ref):
    low = lax.broadcasted_iota(jnp.int32, (BLOCK, LANES), 1) < HALF
    tril = (lax.broadcasted_iota(jnp.int32, (BLOCK, BLOCK), 0)
            >= lax.broadcasted_iota(jnp.int32, (BLOCK, BLOCK), 1))
    mixed = []
    for pair in range(N_SGU_HEADS // 2):
        vp = vln[:, pair * LANES:(pair + 1) * LANES]
        w0 = jnp.where(tril, w_ref[2 * pair], 0.0)
        w1 = jnp.where(tril, w_ref[2 * pair + 1], 0.0)
        mixed.append(_dot(w0, jnp.where(low, vp, 0.0)) + _dot(w1, jnp.where(low, 0.0, vp))
                     + bias_ref[:, pair * LANES:(pair + 1) * LANES])
    return mixed


def _out_proj_loss(ag, sg, x, target, wout, b_out, final_g):
    tm = FWD_TOKEN_TILE

    def body(ag_ref, sg_ref, x_ref, t_ref, w_ref, b_ref, gf_ref, gres_ref, dmix_ref, gw_ref, vec_ref):
        @pl.when(pl.program_id(0) == 0)
        def _():
            gw_ref[...] = jnp.zeros_like(gw_ref)
            vec_ref[...] = jnp.zeros_like(vec_ref)

        a = ag_ref[...]
        s = sg_ref[...]
        xo = x_ref[...] + (_dot(a, w_ref[0:ATTN_W, :]) + _dot(s, w_ref[ATTN_W:, :])) + b_ref[...]
        r = lax.rsqrt(jnp.mean(xo * xo, axis=-1, keepdims=True) + NORM_EPS)
        xn = xo * r
        gf = gf_ref[...]
        err = xn * gf - t_ref[...]
        loss = 0.5 * jnp.sum(jnp.mean(err * err, axis=-1, keepdims=True), axis=0, keepdims=True)
        dy = err * (1.0 / D_MODEL)
        dxn = dy * gf
        gres = r * (dxn - xn * jnp.mean(dxn * xn, axis=-1, keepdims=True))
        vec_ref[0:1, :] += jnp.broadcast_to(loss, (1, D_MODEL))
        vec_ref[1:2, :] += jnp.sum(dy * xn, axis=0, keepdims=True)
        vec_ref[2:3, :] += jnp.sum(gres, axis=0, keepdims=True)
        gres_ref[...] = gres
        gb = gres.astype(MXU_DTYPE)
        dmix_ref[0] = _dot(gb, w_ref[0:ATTN_W, :], NT)
        dmix_ref[1] = _dot(gb, w_ref[ATTN_W:, :], NT)
        gw_ref[0:ATTN_W, :] += _dot(a, gb, TN)
        gw_ref[ATTN_W:, :] += _dot(s, gb, TN)

    tile = lambda w: pl.BlockSpec((tm, w), lambda i: (i, 0))
    return pl.pallas_call(
        body,
        name="out_proj_loss",
        grid=(SEQ // tm,),
        in_specs=[tile(ATTN_W), tile(SGU_W), tile(D_MODEL), tile(D_MODEL),
                  _full((D_MODEL, D_MODEL)), _full((1, D_MODEL)), _full((1, D_MODEL))],
        out_specs=(tile(D_MODEL), pl.BlockSpec((2, tm, ATTN_W), lambda i: (0, i, 0)), _full((D_MODEL, D_MODEL)),
                   _full((8, D_MODEL))),
        out_shape=(jax.ShapeDtypeStruct((SEQ, D_MODEL), F32),
                   jax.ShapeDtypeStruct((2, SEQ, ATTN_W), F32),
                   jax.ShapeDtypeStruct((D_MODEL, D_MODEL), F32),
                   jax.ShapeDtypeStruct((8, D_MODEL), F32)),
        compiler_params=_params(("arbitrary",), VMEM_LIMIT),
    )(ag, sg, x, target, wout, b_out, final_g)


def _mixers_bwd(sinks, dmix, q, kvx, out, gates, ln_g, ln_b, sgu_w, bias_full, gwout):
    last = N_BLOCKS - 1

    def body(sink_ref, d_ref, q_ref, kc_ref, o_ref, za_ref, dsg_ref, us_ref, vs_ref, zs_ref, lng_ref, lnb_ref, w_ref,
             bias_ref, gwout_ref,
             dp_ref, gsink_ref, gbin_ref, dps_ref, gw_ref, gb_ref, gln_ref, gbins_ref, wout_shard_ref,
             kp_ref, pend_ref, carry_ref, wt_ref, gbias_ref, sa_w, ra_w, sb_w, rc_w, send_sems, recv_sems):
        n = pl.program_id(0)
        start, exchange, finish = _reduce_scatter_plan(_Copies(send_sems, recv_sems), 0, gwout_ref, WOUT_ROWS,
                                                       sa_w, ra_w, sb_w, rc_w, wout_shard_ref)
        tril = (lax.broadcasted_iota(jnp.int32, (BLOCK, BLOCK), 0)
                >= lax.broadcasted_iota(jnp.int32, (BLOCK, BLOCK), 1))

        @pl.when(n == 0)
        def _():
            gsink_ref[...] = jnp.zeros_like(gsink_ref)
            gbin_ref[...] = jnp.zeros_like(gbin_ref)
            carry_ref[...] = jnp.zeros_like(carry_ref)
            kp_ref[...] = jnp.zeros_like(kp_ref)
            gw_ref[...] = jnp.zeros_like(gw_ref)
            gln_ref[...] = jnp.zeros_like(gln_ref)
            gbins_ref[...] = jnp.zeros_like(gbins_ref)
            gbias_ref[...] = jnp.zeros_like(gbias_ref)
            for hh in range(N_SGU_HEADS):
                wt_ref[hh] = jnp.where(tril, w_ref[hh], 0.0).T.astype(MXU_DTYPE)
            start()

        pl.when(n == 3)(exchange)
        pl.when(n == 12)(finish)

        @pl.when(n > 0)
        def _():
            dp_ref[:, 0:ATTN_W] = pend_ref[:, 0:ATTN_W]
            dp_ref[:, GATE0:ATTN_SECTION] = pend_ref[:, ATTN_W:]

        @pl.when(n > last)
        def _():
            dp_ref[:, KV0:GATE0] = carry_ref[...].astype(MXU_DTYPE)

        @pl.when(n <= last)
        def _():
            us = us_ref[...]
            vs = vs_ref[...]
            lng = lng_ref[...]
            u, vhat, rstd, vln = _sgu_activations(us, vs, lng, lnb_ref[...])
            low_sgu = lax.broadcasted_iota(jnp.int32, (BLOCK, LANES), 1) < HALF
            sgu = {}

            def sgu_gates():
                mixed = _sgu_mix(vln, w_ref, bias_ref)
                sgu["du"], sgu["dzs"], sgu["dm"] = [], [], []
                for pair in range(N_SGU_HEADS // 2):
                    cols = slice(pair * LANES, (pair + 1) * LANES)
                    dsg = dsg_ref[:, cols]
                    gate, gate_grad = _silu_and_grad(zs_ref[:, cols])
                    up = u[:, cols]
                    sgu["du"].append(dsg * mixed[pair] * gate)
                    sgu["dzs"].append(dsg * up * mixed[pair] * gate_grad)
                    dmixed = dsg * up * gate
                    gbias_ref[:, cols] += dmixed
                    sgu["dm"].append((jnp.where(low_sgu, dmixed, 0.0).astype(MXU_DTYPE),
                                      jnp.where(low_sgu, 0.0, dmixed).astype(MXU_DTYPE)))

            def sgu_grads():
                dvln_parts = []
                for pair in range(N_SGU_HEADS // 2):
                    dm_lo, dm_hi = sgu["dm"][pair]
                    vp = vln[:, pair * LANES:(pair + 1) * LANES]
                    gw_ref[2 * pair] += _dot(dm_lo, vp, NT)
                    gw_ref[2 * pair + 1] += _dot(dm_hi, vp, NT)
                    dvln_parts.append(_dot(wt_ref[2 * pair], dm_lo) + _dot(wt_ref[2 * pair + 1], dm_hi))
                dvln = jnp.concatenate(dvln_parts, axis=1)
                gln_ref[0:1, :] += jnp.sum(dvln * vhat, axis=0, keepdims=True)
                gln_ref[1:2, :] += jnp.sum(dvln, axis=0, keepdims=True)
                dvhat = dvln * lng
                dvg = rstd * (dvhat - jnp.mean(dvhat, axis=-1, keepdims=True)
                              - vhat * jnp.mean(dvhat * vhat, axis=-1, keepdims=True))
                dus = jnp.concatenate(sgu["du"], axis=1) * _gelu_grad(us)
                dvs = dvg * _gelu_grad(vs)
                dzs = jnp.concatenate(sgu["dzs"], axis=1)
                for k, val in enumerate((dus, dvs, dzs)):
                    dps_ref[:, k * SGU_W:(k + 1) * SGU_W] = val.astype(MXU_DTYPE)
                    gbins_ref[:, k * SGU_W:(k + 1) * SGU_W] += jnp.sum(val, axis=0, keepdims=True)

            valid = _window_mask(n)[0:BLOCK]
            low = lax.broadcasted_iota(jnp.int32, (BLOCK, LANES), 1) < HALF
            low_keys = lax.broadcasted_iota(jnp.int32, (2 * BLOCK, LANES), 1) < HALF
            lane_row = lax.broadcasted_iota(jnp.int32, (1, LANES), 1)
            gsink = jnp.zeros((1, LANES), F32)
            chains = [(g, par, i) for g in range(2) for par in range(2) for i in range(2)]
            kv = {(g, par): _kv_cat(kp_ref, kc_ref, 2 * g + par, False) for g in range(2) for par in range(2)}
            ones_keys = jnp.ones((2 * BLOCK, LANES), MXU_DTYPE)
            half_of_lane = lax.broadcasted_iota(jnp.int32, (LANES, 2 * LANES), 0) // HALF
            half_of_col = lax.broadcasted_iota(jnp.int32, (LANES, 2 * LANES), 1) // LANES
            sum_halves = (half_of_lane == half_of_col).astype(MXU_DTYPE)
            douts, deltas = [], []
            for pair in range(N_PAIRS):
                lanes = slice(pair * LANES, (pair + 1) * LANES)
                dg = d_ref[:, lanes]
                gate, gate_grad = _silu_and_grad(za_ref[:, lanes])
                o = o_ref[pair]
                dout = dg * gate
                dza = dg * o * gate_grad
                douts.append(dout.astype(MXU_DTYPE))
                deltas.append(_dot(dout * o, sum_halves))
                zl = slice(ATTN_W + pair * LANES, ATTN_W + (pair + 1) * LANES)
                pend_ref[:, zl] = dza.astype(MXU_DTYPE)
                gl = slice(GATE0 + pair * LANES, GATE0 + (pair + 1) * LANES)
                gbin_ref[:, gl] += jnp.sum(dza, axis=0, keepdims=True)

            first = {}

            def issue_first(k):
                g, par, i = chains[k]
                first[k] = (_dot(q_ref[2 * g + i], kv[g, par][0], NT), _dot(douts[2 * g + i], kv[g, par][1], NT))

            numerators = {}

            def issue_row_sums(k):
                g, par, i = chains[k]
                sink = sink_ref[4 * g + 2 * i + par]
                e, m = _softmax_numerator(jnp.where(valid, first[k][0], NEG_INF), sink)
                numerators[k] = (e, jnp.exp(sink - m), _dot(e, ones_keys))

            ahead = ATTN_BWD_AHEAD
            for k in range(ahead):
                issue_first(k)
            issue_row_sums(0)
            issue_row_sums(1)
            dqs, dk_parts, dv_parts = {}, {}, {}
            operands = {}

            def issue_last(k):
                g, par, i = chains[k]
                ds, ds_t, p_t = operands.pop(k)
                dq = _dot(ds, kv[g, par][0])
                dqs[g, i] = dq if par == 0 else dqs[g, i] + dq
                dk = _dot(ds_t, q_ref[2 * g + i])
                dv = _dot(p_t, douts[2 * g + i])
                dk_parts[g, par] = dk if i == 0 else dk_parts[g, par] + dk
                dv_parts[g, par] = dv if i == 0 else dv_parts[g, par] + dv

            for k, (g, par, i) in enumerate(chains):
                h = 4 * g + 2 * i + par
                delta = deltas[2 * g + i][:, par * LANES:(par + 1) * LANES]
                e, at_sink, row_sum = numerators[k]
                inv = 1.0 / (row_sum + at_sink)
                p = e * jnp.tile(inv, (1, 2))
                ds = p * (first[k][1] - jnp.tile(delta, (1, 2)))
                operands[k] = (ds.astype(MXU_DTYPE), ds.T.astype(MXU_DTYPE), p.T.astype(MXU_DTYPE))
                total = jnp.sum(at_sink * inv * delta, axis=0, keepdims=True)
                gsink = jnp.where(lane_row == h, -total, gsink)
                if k + ahead < len(chains):
                    issue_first(k + ahead)
                if k + 2 < len(chains):
                    issue_row_sums(k + 2)
                if k > 0:
                    issue_last(k - 1)
                if k == SGU_GATES_AFTER_CHAIN:
                    sgu_gates()
                if k == SGU_GRADS_AFTER_CHAIN:
                    sgu_grads()
            issue_last(len(chains) - 1)
            for pair in range(N_PAIRS):
                g, i = divmod(pair, 2)
                dq = dqs[g, i] * SCALE
                lanes = slice(pair * LANES, (pair + 1) * LANES)
                pend_ref[:, lanes] = dq.astype(MXU_DTYPE)
                gbin_ref[:, lanes] += jnp.sum(dq, axis=0, keepdims=True)
            gsink_ref[...] += gsink
            for k, parts in enumerate((dk_parts, dv_parts)):
                masked = {key: jnp.where(low_keys if key[1] == 0 else jnp.logical_not(low_keys), val, 0.0)
                          for key, val in parts.items()}
                both = (masked[0, 0] + masked[1, 1]
                        + pltpu.roll(masked[0, 1] + masked[1, 0], HALF, 1))
                lanes = slice(k * KV_W, (k + 1) * KV_W)
                done = carry_ref[:, lanes] + both[0:BLOCK]
                dp_ref[:, KV0 + k * KV_W:KV0 + (k + 1) * KV_W] = done.astype(MXU_DTYPE)
                carry_ref[:, lanes] = both[BLOCK:]
                gbin_ref[:, KV0 + k * KV_W:KV0 + (k + 1) * KV_W] += jnp.sum(both, axis=0, keepdims=True)
            kp_ref[...] = kc_ref[...]

        @pl.when(n == last)
        def _():
            for hh in range(N_SGU_HEADS):
                gw_ref[hh] = jnp.where(tril, gw_ref[hh], 0.0)
            head_of_lane = lax.broadcasted_iota(jnp.int32, (N_SGU_HEADS, SGU_W), 1) // HEAD_DIM
            select = (head_of_lane == lax.broadcasted_iota(jnp.int32, (N_SGU_HEADS, SGU_W), 0)).astype(F32)
            gb_ref[...] = lax.dot_general(select, gbias_ref[...], NT, precision=lax.Precision.HIGHEST,
                                          preferred_element_type=F32)

    at = lambda n: jnp.minimum(n, last)
    blk = lambda w: pl.BlockSpec((BLOCK, w), lambda n: (at(n), 0))
    tiles = pl.BlockSpec((N_PAIRS, BLOCK, LANES), lambda n: (0, at(n), 0))
    section = lambda k: pl.BlockSpec((None, BLOCK, SGU_W), lambda n: (k, at(n), 0))
    return pl.pallas_call(
        body,
        name="mixers_bwd",
        grid=(N_BLOCKS + 1,),
        in_specs=[pl.BlockSpec(memory_space=pltpu.SMEM),
                  section(0),
                  tiles,
                  blk(KVX_W),
                  tiles,
                  section(0),
                  section(1),
                  section(1), section(2), section(3),
                  _full((1, SGU_W)), _full((1, SGU_W)), _full((N_SGU_HEADS, BLOCK, BLOCK)), _full((BLOCK, SGU_W)),
                  VMEM_SPEC],
        out_specs=(pl.BlockSpec((BLOCK, ATTN_SECTION), lambda n: (jnp.maximum(n - 1, 0), 0)),
                   _full((1, LANES)), _full((1, ATTN_SECTION)),
                   pl.BlockSpec((BLOCK, SGU_SECTION), lambda n: (at(n), 0)),
                   _full((N_SGU_HEADS, BLOCK, BLOCK)), _full((N_SGU_HEADS, BLOCK)),
                   _full((8, SGU_W)), _full((1, SGU_SECTION)), VMEM_SPEC),
        out_shape=(jax.ShapeDtypeStruct((SEQ, ATTN_SECTION), MXU_DTYPE),
                   jax.ShapeDtypeStruct((1, LANES), F32),
                   jax.ShapeDtypeStruct((1, ATTN_SECTION), F32),
                   jax.ShapeDtypeStruct((SEQ, SGU_SECTION), MXU_DTYPE),
                   jax.ShapeDtypeStruct((N_SGU_HEADS, BLOCK, BLOCK), F32),
                   jax.ShapeDtypeStruct((N_SGU_HEADS, BLOCK), F32),
                   jax.ShapeDtypeStruct((8, SGU_W), F32),
                   jax.ShapeDtypeStruct((1, SGU_SECTION), F32),
                   jax.ShapeDtypeStruct((WOUT_ROWS, D_MODEL), F32)),
        scratch_shapes=([pltpu.VMEM((BLOCK, KVX_W), MXU_DTYPE),
                         pltpu.VMEM((BLOCK, 2 * ATTN_W), MXU_DTYPE), pltpu.VMEM((BLOCK, 2 * KV_W), F32),
                         pltpu.VMEM((N_SGU_HEADS, BLOCK, BLOCK), MXU_DTYPE), pltpu.VMEM((BLOCK, SGU_W), F32)]
                        + _reduce_scatter_scratch(WOUT_ROWS, D_MODEL, COMM_DTYPE) + _dma_sems(REDUCE_SEMS)),
        compiler_params=_params(("arbitrary",), VMEM_LIMIT),
    )(sinks, dmix, q, kvx, out, gates, dmix, gates, gates, gates, ln_g, ln_b, sgu_w, bias_full, gwout)


def _in_proj_bwd(dpa, dps, win_t, x, norm_g, gres, gwin, vec_parts):
    tm = TOKEN_TILE
    steps = SEQ // tm
    n_parts = len(vec_parts)

    def body(da_ref, ds_ref, w_ref, x_ref, g_ref, gres_ref, gwin_ref, *rest):
        part_refs = rest[:n_parts]
        gx_ref, shard_ref, vec_out_ref, gng_ref, sa, ra, sb, rc, vec_ref, ra_vec, slots, send_sems, recv_sems = (
            rest[n_parts:])
        step = pl.program_id(0)
        copies = _Copies(send_sems, recv_sems)
        start, exchange, finish = _reduce_scatter_plan(copies, 0, gwin_ref, WIN_ROWS, sa, ra, sb, rc, shard_ref)

        @pl.when(step == 0)
        def _():
            gng_ref[...] = jnp.zeros_like(gng_ref)
            start()

        pl.when(step == 2)(exchange)

        dh = _dot(da_ref[...], w_ref[0:ATTN_SECTION, :]) + _dot(ds_ref[...], w_ref[ATTN_SECTION:, :])
        xv = x_ref[...]
        r = lax.rsqrt(jnp.mean(xv * xv, axis=-1, keepdims=True) + NORM_EPS)
        xn = xv * r
        gng_ref[...] += jnp.sum(dh * xn, axis=0, keepdims=True)
        dxn = dh * g_ref[...]
        gx_ref[...] = r * (dxn - xn * jnp.mean(dxn * xn, axis=-1, keepdims=True)) + gres_ref[...]

        @pl.when(step == steps - 1)
        def _():
            finish()
            _all_reduce_vectors(copies, REDUCE_SEMS, gng_ref, *part_refs, vec_out_ref, vec_ref, ra_vec, slots)

    tile = lambda w: pl.BlockSpec((tm, w), lambda i: (i, 0))
    return pl.pallas_call(
        body,
        name="in_proj_bwd",
        grid=(steps,),
        in_specs=[tile(ATTN_SECTION), tile(SGU_SECTION), _full((IN_W, D_MODEL)), tile(D_MODEL),
                  _full((1, D_MODEL)), tile(D_MODEL), VMEM_SPEC] + [VMEM_SPEC] * n_parts,
        out_specs=(tile(D_MODEL), VMEM_SPEC, VMEM_SPEC),
        out_shape=(jax.ShapeDtypeStruct((SEQ, D_MODEL), F32),
                   jax.ShapeDtypeStruct((WIN_ROWS, D_MODEL), F32),
                   jax.ShapeDtypeStruct((VEC_ROWS, IN_W), F32)),
        scratch_shapes=([pltpu.VMEM((1, D_MODEL), F32)] + _reduce_scatter_scratch(WIN_ROWS, D_MODEL, COMM_DTYPE)
                        + _vector_scratch() + _dma_sems(REDUCE_SEMS + VECTOR_SEMS)),
        compiler_params=_params(("arbitrary",), VMEM_LIMIT),
    )(dpa, dps, win_t, x, norm_g, gres, gwin, *vec_parts)


def _win_grad(dpa, dps, h, gsguw):
    rows = 256
    n_attn = ATTN_SECTION // rows
    steps = n_attn + SGU_SECTION // rows

    def body(da_ref, ds_ref, h_ref, gsguw_ref, o_ref, sguw_full_ref, sa, ra, sb, rc, landing, send_sems, recv_sems):
        step = pl.program_id(0)
        copies = _Copies(send_sems, recv_sems)
        own_sguw = landing.at[_block_rows(_place(), SGUW_ROWS), :]
        start, exchange, finish = _reduce_scatter_plan(copies, 0, gsguw_ref, SGUW_ROWS, sa, ra, sb, rc, own_sguw)
        gather = _gather_plan(copies, REDUCE_SEMS, landing, SGUW_ROWS)

        pl.when(step == 0)(start)
        pl.when(step == 2)(exchange)

        @pl.when(step == 5)
        def _():
            finish()
            gather[0]()

        pl.when(step == 7)(gather[1])

        @pl.when(step < n_attn)
        def _():
            o_ref[...] = _dot(da_ref[...], h_ref[...], TN)

        @pl.when(step >= n_attn)
        def _():
            o_ref[...] = _dot(ds_ref[...], h_ref[...], TN)

        @pl.when(step == steps - 1)
        def _():
            gather[2]()
            sguw_full_ref[...] = landing[...]

    return pl.pallas_call(
        body,
        name="win_grad",
        grid=(steps,),
        in_specs=[pl.BlockSpec((SEQ, rows), lambda i: (0, jnp.minimum(i, n_attn - 1))),
                  pl.BlockSpec((SEQ, rows), lambda i: (0, jnp.maximum(i - n_attn, 0))),
                  _full((SEQ, D_MODEL)), VMEM_SPEC],
        out_specs=(pl.BlockSpec((rows, D_MODEL), lambda i: (i, 0)), _full((N_SGU_HEADS * BLOCK, BLOCK))),
        out_shape=(jax.ShapeDtypeStruct((IN_W, D_MODEL), F32),
                   jax.ShapeDtypeStruct((N_SGU_HEADS * BLOCK, BLOCK), F32)),
        scratch_shapes=(_reduce_scatter_scratch(SGUW_ROWS, BLOCK, F32)
                        + [pltpu.VMEM((N_SGU_HEADS * BLOCK, BLOCK), F32)]
                        + _dma_sems(REDUCE_SEMS + GATHER_SEMS)),
        compiler_params=_params(("arbitrary",), VMEM_LIMIT),
    )(dpa, dps, h, gsguw)


VEC_NORM_G, VEC_B_IN, VEC_SINKS, VEC_LN_G, VEC_LN_B, VEC_B_OUT, VEC_FINAL_G, VEC_LOSS, VEC_SGU_B = 0, 1, 2, 3, 4, 5, 6, 7, 8


def _adamw(w, g, m, v):
    m = ADAM_B1 * m + (1.0 - ADAM_B1) * g
    v = ADAM_B2 * v + (1.0 - ADAM_B2) * (g * g)
    m_hat = m / (1.0 - ADAM_B1 ** ADAM_STEP)
    v_hat = v / (1.0 - ADAM_B2 ** ADAM_STEP)
    delta = -ADAM_LR * (m_hat / (jnp.sqrt(v_hat) + ADAM_EPS) + ADAM_WD * w)
    return delta, m, v


def _adamw_shard(name, g, w, m, v, block_rows):
    def body(g_ref, w_ref, m_ref, v_ref, d_ref, nm_ref, nv_ref):
        d_ref[...], nm_ref[...], nv_ref[...] = _adamw(w_ref[...], g_ref[...], m_ref[...], v_ref[...])

    rows, cols = w.shape
    spec = pl.BlockSpec((block_rows, cols), lambda i: (i, 0))
    return pl.pallas_call(
        body,
        name=name,
        grid=(rows // block_rows,),
        in_specs=[spec] * 4,
        out_specs=(spec,) * 3,
        out_shape=(jax.ShapeDtypeStruct(w.shape, F32),) * 3,
        compiler_params=_params(("arbitrary",)),
    )(g, w, m, v)


VECTOR_SEMS = 4


def _vector_scratch():
    return [pltpu.VMEM((VEC_ROWS, IN_W), F32), pltpu.VMEM((VEC_ROWS, IN_W), F32),
            pltpu.VMEM((4 * VEC_ROWS, IN_W), F32)]


def _all_reduce_vectors(copies, sem0, gng_ref, gba_ref, gbs_ref, gsink_ref, gln_ref, gsgub_ref, vec4_ref, out_ref,
                        vec_ref, ra_vec, slots):
    x, y, c = _place()
    vec_ref[...] = jnp.zeros_like(vec_ref)
    vec_ref[VEC_NORM_G:VEC_NORM_G + 1, 0:D_MODEL] = gng_ref[...]
    vec_ref[VEC_B_IN:VEC_B_IN + 1, 0:ATTN_SECTION] = gba_ref[...]
    vec_ref[VEC_B_IN:VEC_B_IN + 1, ATTN_SECTION:IN_W] = gbs_ref[...]
    vec_ref[VEC_SINKS:VEC_SINKS + 1, 0:LANES] = gsink_ref[...]
    vec_ref[VEC_LN_G:VEC_LN_G + 1, 0:SGU_W] = gln_ref[0:1, :]
    vec_ref[VEC_LN_B:VEC_LN_B + 1, 0:SGU_W] = gln_ref[1:2, :]
    vec_ref[VEC_B_OUT:VEC_B_OUT + 1, 0:D_MODEL] = vec4_ref[2:3, :]
    vec_ref[VEC_FINAL_G:VEC_FINAL_G + 1, 0:D_MODEL] = vec4_ref[1:2, :]
    vec_ref[VEC_LOSS:VEC_LOSS + 1, 0:D_MODEL] = vec4_ref[0:1, :]
    vec_ref[VEC_SGU_B:VEC_SGU_B + N_SGU_HEADS, 0:BLOCK] = gsgub_ref[...]

    to_sibling = copies(sem0, vec_ref, ra_vec, (x, y, 1 - c))
    to_sibling.start()
    to_sibling.wait_recv()

    def chip_slot(place):
        return slots.at[pl.ds(pl.multiple_of((2 * place[0] + place[1]) * VEC_ROWS, 8), VEC_ROWS), :]

    mine = chip_slot((x, y))
    mine[...] = vec_ref[...] + ra_vec[...]
    to_chips = [copies(sem0 + i, mine, mine, (*_chip(rel), c)) for i, rel in enumerate(RELATIONS[1:], start=1)]
    for cp in to_chips:
        cp.start()
    for i, rel in enumerate(RELATIONS[1:], start=1):
        theirs = chip_slot(_chip(rel))
        copies(sem0 + i, theirs, theirs, (x, y, c)).wait_recv()
    out_ref[...] = ((slots[0:VEC_ROWS, :] + slots[VEC_ROWS:2 * VEC_ROWS, :])
                    + slots[2 * VEC_ROWS:3 * VEC_ROWS, :]) + slots[3 * VEC_ROWS:, :]
    to_sibling.wait_send()
    for cp in to_chips:
        cp.wait_send()


def _adamw_replicated(vec, gsguw, weights, m_state, v_state):
    n = len(SMALL)

    def body(*refs):
        vec_ref, gsguw_ref = refs[0], refs[1]
        w_refs, m_refs, v_refs = (refs[2 + k * n:2 + (k + 1) * n] for k in range(3))
        outs = refs[2 + 3 * n:]
        g_refs, d_refs, nm_refs, nv_refs = (outs[k * n:(k + 1) * n] for k in range(4))
        for i, (_, row, shape) in enumerate(SMALL):
            g = gsguw_ref[...] if row is None else vec_ref[row:row + shape[0], 0:shape[1]]
            g_refs[i][...] = g
            d_refs[i][...], nm_refs[i][...], nv_refs[i][...] = _adamw(
                w_refs[i][...], g, m_refs[i][...], v_refs[i][...])

    shapes = tuple(jax.ShapeDtypeStruct(shape, F32) for _, _, shape in SMALL)
    outs = pl.pallas_call(
        body,
        name="adamw_replicated",
        in_specs=[VMEM_SPEC] * (2 + 3 * n),
        out_specs=(VMEM_SPEC,) * (4 * n),
        out_shape=shapes * 4,
    )(vec, gsguw, *weights, *m_state, *v_state)
    return tuple(outs[k * n:(k + 1) * n] for k in range(4))


SMALL = (
    ("norm_g", VEC_NORM_G, (1, D_MODEL)),
    ("b_in", VEC_B_IN, (1, IN_W)),
    ("attn_sinks", VEC_SINKS, (1, N_Q_HEADS)),
    ("sgu_ln_g", VEC_LN_G, (1, SGU_W)),
    ("sgu_ln_b", VEC_LN_B, (1, SGU_W)),
    ("sgu_w", None, (N_SGU_HEADS * BLOCK, BLOCK)),
    ("sgu_b", VEC_SGU_B, (N_SGU_HEADS, BLOCK)),
    ("b_out", VEC_B_OUT, (1, D_MODEL)),
    ("final_norm_g", VEC_FINAL_G, (1, D_MODEL)),
)


def _local_grads(x, target, win_t, wout_shard, norm_g, b_in, attn_sinks, sgu_ln_g, sgu_ln_b, sgu_w, sgu_b, b_out,
                 final_g):
    sinks = attn_sinks.reshape(N_Q_HEADS)
    bias_full = jnp.repeat(sgu_b.T, HEAD_DIM, axis=1)
    h, q, kvx, gates, wout = _in_proj(x, norm_g, b_in, win_t, wout_shard)
    out, ag, sg = _mixers_fwd(sinks, q, kvx, gates, sgu_ln_g, sgu_ln_b, sgu_w, bias_full)
    gres, dmix, gwout, vec4 = _out_proj_loss(ag, sg, x, target, wout, b_out, final_g)
    dpa, gsink, gbin_a, dps, gsguw, gsgub, gln, gbin_s, gwout_shard = _mixers_bwd(
        sinks, dmix, q, kvx, out, gates, sgu_ln_g, sgu_ln_b, sgu_w, bias_full, gwout)
    gwin, gsguw_sum = _win_grad(dpa, dps, h, gsguw.reshape(N_SGU_HEADS * BLOCK, BLOCK))
    grad_x, gwin_shard, vec = _in_proj_bwd(dpa, dps, win_t, x, norm_g, gres, gwin,
                                           (gbin_a, gbin_s, gsink, gln, gsgub, vec4))
    return grad_x, gwin_shard, gwout_shard, gsguw_sum, vec


def kernel(x, norm_g, w_in, b_in, attn_sinks, sgu_ln_g, sgu_ln_b, sgu_w, sgu_b, w_out, b_out, final_norm_g, loss_target, m_norm_g, m_w_in, m_b_in, m_attn_sinks, m_sgu_ln_g, m_sgu_ln_b, m_sgu_w, m_sgu_b, m_w_out, m_b_out, m_final_norm_g, v_norm_g, v_w_in, v_b_in, v_attn_sinks, v_sgu_ln_g, v_sgu_ln_b, v_sgu_w, v_sgu_b, v_w_out, v_b_out, v_final_norm_g):
    given = dict(norm_g=norm_g, b_in=b_in, attn_sinks=attn_sinks, sgu_ln_g=sgu_ln_g, sgu_ln_b=sgu_ln_b,
                 sgu_w=sgu_w, sgu_b=sgu_b, b_out=b_out, final_norm_g=final_norm_g)
    m_given = dict(norm_g=m_norm_g, b_in=m_b_in, attn_sinks=m_attn_sinks, sgu_ln_g=m_sgu_ln_g,
                   sgu_ln_b=m_sgu_ln_b, sgu_w=m_sgu_w, sgu_b=m_sgu_b, b_out=m_b_out, final_norm_g=m_final_norm_g)
    v_given = dict(norm_g=v_norm_g, b_in=v_b_in, attn_sinks=v_attn_sinks, sgu_ln_g=v_sgu_ln_g,
                   sgu_ln_b=v_sgu_ln_b, sgu_w=v_sgu_w, sgu_b=v_sgu_b, b_out=v_b_out, final_norm_g=v_final_norm_g)

    win_t = _all_gather_win(w_in[0].T)
    grad_x, gwin_t, gwout, gsguw, vec = _local_grads(
        x[0], loss_target[0], win_t, w_out[0], norm_g, b_in, attn_sinks, sgu_ln_g, sgu_ln_b, sgu_w[0], sgu_b[0],
        b_out, final_norm_g.reshape(1, D_MODEL))

    t = lambda a: a[0].T
    d_win, nm_win, nv_win = _adamw_shard("adamw_w_in", gwin_t, t(w_in), t(m_w_in), t(v_w_in), WIN_ROWS // 2)
    d_wout, nm_wout, nv_wout = _adamw_shard("adamw_w_out", gwout, w_out[0], m_w_out[0], v_w_out[0], WOUT_ROWS)
    as_2d = lambda d: [d[name].reshape(shape) for name, _, shape in SMALL]
    loss = vec[VEC_LOSS, 0]
    small = _adamw_replicated(vec, gsguw, as_2d(given), as_2d(m_given), as_2d(v_given))

    def assemble(big_in, big_out, k):
        vals = {name: small[k][i].reshape(given[name].shape) for i, (name, _, _) in enumerate(SMALL)}
        vals["w_in"] = big_in.T[None]
        vals["w_out"] = big_out[None]
        order = ("norm_g", "w_in", "b_in", "attn_sinks", "sgu_ln_g", "sgu_ln_b", "sgu_w", "sgu_b", "w_out",
                 "b_out", "final_norm_g")
        return [vals[name] for name in order]

    return (loss, grad_x[None],
            *assemble(gwin_t, gwout, 0), *assemble(d_win, d_wout, 1),
            *assemble(nm_win, nm_wout, 2), *assemble(nv_win, nv_wout, 3))
```

```python
import functools
import math

import jax
import jax.numpy as jnp
from jax import lax
from jax.experimental import pallas as pl
from jax.experimental.pallas import tpu as pltpu

F32 = jnp.float32
BF16 = jnp.bfloat16
MXU_DTYPE = BF16
COMM_DTYPE = BF16

D_MODEL = 1024
SEQ = 4096
HEAD_DIM = 64
N_Q_HEADS = 8
Q_PER_KV = 4
BLOCK = 128
N_BLOCKS = SEQ // BLOCK
ATTN_W = 512
KV_W = 128
SGU_W = 512
N_SGU_HEADS = 8
IN_W = 2816
NORM_EPS = 1e-5
NEG_INF = -1e30
SCALE = HEAD_DIM ** -0.5
KV0 = ATTN_W
GATE0 = ATTN_W + 2 * KV_W
SGU0 = GATE0 + ATTN_W
ATTN_SECTION = SGU0
SGU_SECTION = IN_W - SGU0

ADAM_LR = 0.001
ADAM_B1 = 0.9
ADAM_B2 = 0.999
ADAM_EPS = 1e-08
ADAM_WD = 0.01
ADAM_STEP = 10

N_DEV = 8
WIN_ROWS = IN_W // N_DEV
WOUT_ROWS = D_MODEL // N_DEV
SGUW_ROWS = N_SGU_HEADS * BLOCK // N_DEV
VEC_ROWS = 16
MESH = pl.DeviceIdType.MESH

LANES = 128
HALF = LANES // 2
N_PAIRS = N_Q_HEADS * HEAD_DIM // LANES
KVX_W = 12 * LANES
TOKEN_TILE = 256
FWD_TOKEN_TILE = 512
ATTN_FWD_AHEAD = 4
FWD_BLOCKS_PER_STEP = 4
SGU_MIX_AFTER_CHAIN = 0
SGU_GATES_AFTER_CHAIN = 1
SGU_GRADS_AFTER_CHAIN = 5
ATTN_BWD_AHEAD = 3
VMEM_LIMIT = 56 * 1024 * 1024

NN = (((1,), (0,)), ((), ()))
NT = (((1,), (1,)), ((), ()))
TN = (((0,), (0,)), ((), ()))


def _dot(a, b, dims=NN):
    return lax.dot_general(a.astype(MXU_DTYPE), b.astype(MXU_DTYPE), dims, preferred_element_type=F32)


def _gelu(x):
    return x * (lax.erf(x * (1.0 / math.sqrt(2.0))) + 1.0) * 0.5


def _gelu_grad(x):
    cdf = (lax.erf(x * (1.0 / math.sqrt(2.0))) + 1.0) * 0.5
    return cdf + x * jnp.exp(-0.5 * x * x) * (1.0 / math.sqrt(2.0 * math.pi))


def _silu_and_grad(z):
    s = jax.nn.sigmoid(z)
    return z * s, s * (1.0 + z * (1.0 - s))


def _params(semantics=None, vmem=None):
    kw = {}
    if semantics is not None:
        kw["dimension_semantics"] = semantics
    if vmem is not None:
        kw["vmem_limit_bytes"] = vmem
    return pltpu.CompilerParams(**kw)


def _full(shape):
    return pl.BlockSpec(shape, lambda *_: (0,) * len(shape))


VMEM_SPEC = pl.BlockSpec(memory_space=pltpu.VMEM)


RELATIONS = ((0, 0), (1, 0), (0, 1), (1, 1))


def _place():
    return lax.axis_index("x"), lax.axis_index("y"), lax.axis_index("c")


def _chip(rel):
    x, y, _ = _place()
    return (1 - x if rel[0] else x, 1 - y if rel[1] else y)


def _block_rows(place, n_rows):
    px, py, pc = place
    return pl.ds(pl.multiple_of((4 * px + 2 * py + pc) * n_rows, 16), n_rows)


class _Copies:
    def __init__(self, send_sems, recv_sems):
        self.send_sems, self.recv_sems = send_sems, recv_sems

    def __call__(self, k, src, dst, to):
        return pltpu.make_async_remote_copy(src_ref=src, dst_ref=dst, send_sem=self.send_sems.at[k],
                                            recv_sem=self.recv_sems.at[k], device_id=to, device_id_type=MESH)


def _gather_plan(copies, sem0, full_ref, n_rows):
    x, y, c = _place()
    me, sibling = (x, y, c), (x, y, 1 - c)
    chips = [_chip(rel) for rel in RELATIONS[1:]]

    def cp(k, block, to):
        rows = full_ref.at[_block_rows(block, n_rows), :]
        return copies(sem0 + k, rows, rows, to)

    first = [cp(0, me, sibling)] + [cp(1 + j, me, (*chip, c)) for j, chip in enumerate(chips)]
    passed = [cp(4 + j, (*chip, c), sibling) for j, chip in enumerate(chips)]

    def start():
        for f in first:
            f.start()

    def forward():
        for j, chip in enumerate(chips):
            cp(1 + j, (*chip, c), me).wait_recv()
            passed[j].start()

    def finish():
        cp(0, sibling, me).wait_recv()
        for j, chip in enumerate(chips):
            cp(4 + j, (*chip, 1 - c), me).wait_recv()
        for f in first + passed:
            f.wait_send()

    return start, forward, finish


GATHER_SEMS = 7


def _reduce_scatter_plan(copies, sem0, part_ref, n_rows, sa, ra, sb, rc, res_ref):
    x, y, c = _place()
    sibling = (x, y, 1 - c)
    n = n_rows
    level1 = copies(sem0, sa, ra, sibling)

    def level2(i):
        slot = pl.ds((i - 1) * n, n)
        return copies(sem0 + i, sb.at[slot, :], rc.at[slot, :], (*_chip(RELATIONS[i]), c))

    def start():
        for i, rel in enumerate(RELATIONS):
            sa[i * n:(i + 1) * n, :] = part_ref[_block_rows((*_chip(rel), 1 - c), n), :].astype(sa.dtype)
        level1.start()

    def exchange():
        level1.wait_recv()
        for i, rel in enumerate(RELATIONS):
            total = part_ref[_block_rows((*_chip(rel), c), n), :] + ra[i * n:(i + 1) * n, :].astype(F32)
            if i == 0:
                res_ref[...] = total
            else:
                sb[(i - 1) * n:i * n, :] = total.astype(sb.dtype)
                level2(i).start()

    def finish():
        acc = res_ref[...]
        for i in range(1, len(RELATIONS)):
            level2(i).wait_recv()
            acc = acc + rc[(i - 1) * n:i * n, :].astype(F32)
        res_ref[...] = acc
        level1.wait_send()
        for i in range(1, len(RELATIONS)):
            level2(i).wait_send()

    return start, exchange, finish


REDUCE_SEMS = 4


def _reduce_scatter_scratch(n_rows, width, dtype):
    return [pltpu.VMEM((4 * n_rows, width), dtype), pltpu.VMEM((4 * n_rows, width), dtype),
            pltpu.VMEM((3 * n_rows, width), dtype), pltpu.VMEM((3 * n_rows, width), dtype)]


def _dma_sems(n):
    return [pltpu.SemaphoreType.DMA((n,)), pltpu.SemaphoreType.DMA((n,))]


def _all_gather_win(win_t_shard):
    def body(win_ref, full_ref, send_sems, recv_sems):
        full_ref[_block_rows(_place(), WIN_ROWS), :] = win_ref[...].astype(COMM_DTYPE)
        start, forward, finish = _gather_plan(_Copies(send_sems, recv_sems), 0, full_ref, WIN_ROWS)
        start()
        forward()
        finish()

    return pl.pallas_call(
        body,
        name="all_gather_win",
        out_shape=jax.ShapeDtypeStruct((IN_W, D_MODEL), COMM_DTYPE),
        in_specs=[VMEM_SPEC],
        out_specs=VMEM_SPEC,
        scratch_shapes=_dma_sems(GATHER_SEMS),
        compiler_params=_params(vmem=VMEM_LIMIT),
    )(win_t_shard)


def _in_proj(x, norm_g, b_in, win_t, wout_shard):
    tm = FWD_TOKEN_TILE
    steps = SEQ // tm

    def body(x_ref, g_ref, b_ref, w_ref, wout_ref, h_ref, q_ref, kvx_ref, gate_ref, wfull_ref,
             landing, send_sems, recv_sems):
        step = pl.program_id(0)
        start, forward, finish = _gather_plan(_Copies(send_sems, recv_sems), 0, landing, WOUT_ROWS)

        @pl.when(step == 0)
        def _():
            landing[_block_rows(_place(), WOUT_ROWS), :] = wout_ref[...].astype(COMM_DTYPE)
            start()

        pl.when(step == steps // 2)(forward)

        xv = x_ref[...]
        r = lax.rsqrt(jnp.mean(xv * xv, axis=-1, keepdims=True) + NORM_EPS)
        h = ((xv * r) * g_ref[...]).astype(MXU_DTYPE)
        h_ref[...] = h

        def proj(lo, hi):
            return _dot(h, w_ref[lo:hi, :], NT) + b_ref[:, lo:hi]

        qs = proj(0, ATTN_W) * SCALE
        for pair in range(N_PAIRS):
            q_ref[pair] = qs[:, pair * LANES:(pair + 1) * LANES].astype(MXU_DTYPE)
        kv = proj(KV0, GATE0)
        low = lax.broadcasted_iota(jnp.int32, (tm, LANES), 1) < HALF
        for i in range(2):
            t = kv[:, i * LANES:(i + 1) * LANES]
            rot = pltpu.roll(t, HALF, 1)
            variants = (jnp.where(low, t, 0.0), jnp.where(low, 0.0, rot),
                        jnp.where(low, rot, 0.0), jnp.where(low, 0.0, t))
            for j, val in enumerate(variants):
                col = (4 * i + j) * LANES
                kvx_ref[:, col:col + LANES] = val.astype(MXU_DTYPE)
                if i == 1:
                    ones_elsewhere = jnp.where(low == (j % 2 == 0), val, 1.0)
                    kvx_ref[:, col + 4 * LANES:col + 5 * LANES] = ones_elsewhere.astype(MXU_DTYPE)
        for k in range(4):
            gate_ref[k] = proj(GATE0 + k * SGU_W, GATE0 + (k + 1) * SGU_W)

        @pl.when(step == steps - 1)
        def _():
            finish()
            wfull_ref[...] = landing[...]

    return pl.pallas_call(
        body,
        name="in_proj",
        grid=(steps,),
        in_specs=[pl.BlockSpec((tm, D_MODEL), lambda i: (i, 0)),
                  _full((1, D_MODEL)), _full((1, IN_W)), _full((IN_W, D_MODEL)), VMEM_SPEC],
        out_specs=(pl.BlockSpec((tm, D_MODEL), lambda i: (i, 0)),
                   pl.BlockSpec((N_PAIRS, tm, LANES), lambda i: (0, i, 0)),
                   pl.BlockSpec((tm, KVX_W), lambda i: (i, 0)),
                   pl.BlockSpec((4, tm, SGU_W), lambda i: (0, i, 0)),
                   _full((D_MODEL, D_MODEL))),
        out_shape=(jax.ShapeDtypeStruct((SEQ, D_MODEL), MXU_DTYPE),
                   jax.ShapeDtypeStruct((N_PAIRS, SEQ, LANES), MXU_DTYPE),
                   jax.ShapeDtypeStruct((SEQ, KVX_W), MXU_DTYPE),
                   jax.ShapeDtypeStruct((4, SEQ, SGU_W), F32),
                   jax.ShapeDtypeStruct((D_MODEL, D_MODEL), COMM_DTYPE)),
        scratch_shapes=[pltpu.VMEM((D_MODEL, D_MODEL), COMM_DTYPE)] + _dma_sems(GATHER_SEMS),
        compiler_params=_params(("arbitrary",), VMEM_LIMIT),
    )(x, norm_g, b_in, win_t, wout_shard)


def _window_mask(n):
    qi = lax.broadcasted_iota(jnp.int32, (2 * BLOCK, 2 * BLOCK), 0) & (BLOCK - 1)
    p = lax.broadcasted_iota(jnp.int32, (2 * BLOCK, 2 * BLOCK), 1) - BLOCK
    in_window = jnp.logical_and(p <= qi, p > qi - BLOCK)
    return jnp.logical_and(in_window, jnp.logical_or(p >= 0, n > 0))


def _sink_column(sink_ref, g, par):
    return jnp.concatenate([jnp.full((BLOCK, 1), sink_ref[4 * g + par], F32),
                            jnp.full((BLOCK, 1), sink_ref[4 * g + 2 + par], F32)], axis=0)


def _kv_cat(kp_ref, kc_ref, var, with_ones):
    kcol, vcol = var * LANES, (var + (8 if with_ones else 4)) * LANES
    return (jnp.concatenate([kp_ref[:, kcol:kcol + LANES], kc_ref[:, kcol:kcol + LANES]], axis=0),
            jnp.concatenate([kp_ref[:, vcol:vcol + LANES], kc_ref[:, vcol:vcol + LANES]], axis=0))


def _softmax_numerator(s, sink):
    m = jnp.maximum(jnp.max(s, axis=1, keepdims=True), sink)
    return jnp.exp(s - m), m


def _mixers_fwd(sinks, q, kvx, gates, ln_g, ln_b, sgu_w, bias_full):
    per_step = FWD_BLOCKS_PER_STEP
    rows_per_step = per_step * BLOCK

    def body(sink_ref, q_ref, kc_ref, za_ref, us_ref, vs_ref, zs_ref, lng_ref, lnb_ref, w_ref, bias_ref,
             out_ref, ag_ref, sg_ref, kp_ref, wm_ref):
        @pl.when(pl.program_id(0) == 0)
        def _():
            kp_ref[...] = jnp.zeros_like(kp_ref)
            _mask_sgu_weights(w_ref, wm_ref)

        def one_block(b, carry):
            rows = pl.ds(pl.multiple_of(b * BLOCK, BLOCK), BLOCK)
            kc = kc_ref.at[rows, :]
            u, _, _, vln = _sgu_activations(us_ref[rows, :], vs_ref[rows, :], lng_ref[...], lnb_ref[...])

            valid = _window_mask(pl.program_id(0) * per_step + b)[0:BLOCK]
            chains = [(g, par, i) for g in range(2) for par in range(2) for i in range(2)]
            kv = {(g, par): _kv_cat(kp_ref, kc, 2 * g + par, True) for g in range(2) for par in range(2)}
            scores, outs = {}, {}

            def issue_scores(k):
                g, par, i = chains[k]
                scores[k] = _dot(q_ref[2 * g + i, rows, :], kv[g, par][0], NT)

            ahead = ATTN_FWD_AHEAD
            for k in range(ahead):
                issue_scores(k)
            low = lax.broadcasted_iota(jnp.int32, (BLOCK, LANES), 1) < HALF
            for k, (g, par, i) in enumerate(chains):
                sink = sink_ref[4 * g + 2 * i + par]
                e, m = _softmax_numerator(jnp.where(valid, scores[k], NEG_INF), sink)
                if k + ahead < len(chains):
                    issue_scores(k + ahead)
                o = _dot(e, kv[g, par][1])
                outs[g, par, i] = o / (pltpu.roll(o, HALF, 1) + jnp.exp(sink - m))
                if k == SGU_MIX_AFTER_CHAIN:
                    mixed = _sgu_mix(vln, wm_ref, bias_ref)
            for pair in range(N_PAIRS):
                g, i = divmod(pair, 2)
                lanes = slice(pair * LANES, (pair + 1) * LANES)
                o = jnp.where(low, outs[g, 0, i], outs[g, 1, i])
                out_ref[pair, rows, :] = o
                gate, _ = _silu_and_grad(za_ref[rows, lanes])
                ag_ref[rows, lanes] = (o * gate).astype(MXU_DTYPE)
            kp_ref[...] = kc[...]
            for pair in range(N_SGU_HEADS // 2):
                cols = slice(pair * LANES, (pair + 1) * LANES)
                gate, _ = _silu_and_grad(zs_ref[rows, cols])
                sg_ref[rows, cols] = (u[:, cols] * mixed[pair] * gate).astype(MXU_DTYPE)
            return carry

        lax.fori_loop(0, per_step, one_block, 0)

    blk = lambda w: pl.BlockSpec((rows_per_step, w), lambda n: (n, 0))
    tiles = pl.BlockSpec((N_PAIRS, rows_per_step, LANES), lambda n: (0, n, 0))
    gate = lambda k: pl.BlockSpec((None, rows_per_step, SGU_W), lambda n: (k, n, 0))
    return pl.pallas_call(
        body,
        name="mixers_fwd",
        grid=(N_BLOCKS // per_step,),
        in_specs=[pl.BlockSpec(memory_space=pltpu.SMEM), tiles, blk(KVX_W), gate(0), gate(1), gate(2), gate(3),
                  _full((1, SGU_W)), _full((1, SGU_W)), _full((N_SGU_HEADS, BLOCK, BLOCK)), _full((BLOCK, SGU_W))],
        out_specs=(tiles, blk(ATTN_W), blk(SGU_W)),
        out_shape=(jax.ShapeDtypeStruct((N_PAIRS, SEQ, LANES), F32),
                   jax.ShapeDtypeStruct((SEQ, ATTN_W), MXU_DTYPE),
                   jax.ShapeDtypeStruct((SEQ, SGU_W), MXU_DTYPE)),
        scratch_shapes=[pltpu.VMEM((BLOCK, KVX_W), MXU_DTYPE), pltpu.VMEM((N_SGU_HEADS, BLOCK, BLOCK), MXU_DTYPE)],
        compiler_params=_params(("arbitrary",)),
    )(sinks, q, kvx, gates, gates, gates, gates, ln_g, ln_b, sgu_w, bias_full)


def _sgu_activations(us, vs, lng, lnb):
    u = _gelu(us)
    vg = _gelu(vs)
    mu = jnp.mean(vg, axis=-1, keepdims=True)
    xc = vg - mu
    rstd = lax.rsqrt(jnp.mean(xc * xc, axis=-1, keepdims=True) + NORM_EPS)
    vhat = xc * rstd
    return u, vhat, rstd, vhat * lng + lnb


def _mask_sgu_weights(w_ref, masked_ref, transposed_ref=None):
    tril = (lax.broadcasted_iota(jnp.int32, (BLOCK, BLOCK), 0)
            >= lax.broadcasted_iota(jnp.int32, (BLOCK, BLOCK), 1))
    for hh in range(N_SGU_HEADS):
        w = jnp.where(tril, w_ref[hh], 0.0)
        masked_ref[hh] = w.astype(MXU_DTYPE)
        if transposed_ref is not None:
            transposed_ref[hh] = w.T.astype(MXU_DTYPE)


def _sgu_mix(vln, masked_w_ref, bias_ref):
    low = lax.broadcasted_iota(jnp.int32, (BLOCK, LANES), 1) < HALF
    mixed = []
    for pair in range(N_SGU_HEADS // 2):
        vp = vln[:, pair * LANES:(pair + 1) * LANES]
        mixed.append(_dot(masked_w_ref[2 * pair], jnp.where(low, vp, 0.0))
                     + _dot(masked_w_ref[2 * pair + 1], jnp.where(low, 0.0, vp))
                     + bias_ref[:, pair * LANES:(pair + 1) * LANES])
    return mixed


def _out_proj_loss(ag, sg, x, target, wout, b_out, final_g):
    tm = FWD_TOKEN_TILE

    def body(ag_ref, sg_ref, x_ref, t_ref, w_ref, b_ref, gf_ref, gres_ref, dmix_ref, gw_ref, vec_ref):
        @pl.when(pl.program_id(0) == 0)
        def _():
            gw_ref[...] = jnp.zeros_like(gw_ref)
            vec_ref[...] = jnp.zeros_like(vec_ref)

        a = ag_ref[...]
        s = sg_ref[...]
        xo = x_ref[...] + (_dot(a, w_ref[0:ATTN_W, :]) + _dot(s, w_ref[ATTN_W:, :])) + b_ref[...]
        r = lax.rsqrt(jnp.mean(xo * xo, axis=-1, keepdims=True) + NORM_EPS)
        xn = xo * r
        gf = gf_ref[...]
        err = xn * gf - t_ref[...]
        loss = 0.5 * jnp.sum(jnp.mean(err * err, axis=-1, keepdims=True), axis=0, keepdims=True)
        dy = err * (1.0 / D_MODEL)
        dxn = dy * gf
        gres = r * (dxn - xn * jnp.mean(dxn * xn, axis=-1, keepdims=True))
        vec_ref[0:1, :] += jnp.broadcast_to(loss, (1, D_MODEL))
        vec_ref[1:2, :] += jnp.sum(dy * xn, axis=0, keepdims=True)
        vec_ref[2:3, :] += jnp.sum(gres, axis=0, keepdims=True)
        gres_ref[...] = gres
        gb = gres.astype(MXU_DTYPE)
        dmix_ref[0] = _dot(gb, w_ref[0:ATTN_W, :], NT)
        dmix_ref[1] = _dot(gb, w_ref[ATTN_W:, :], NT)
        gw_ref[0:ATTN_W, :] += _dot(a, gb, TN)
        gw_ref[ATTN_W:, :] += _dot(s, gb, TN)

    tile = lambda w: pl.BlockSpec((tm, w), lambda i: (i, 0))
    return pl.pallas_call(
        body,
        name="out_proj_loss",
        grid=(SEQ // tm,),
        in_specs=[tile(ATTN_W), tile(SGU_W), tile(D_MODEL), tile(D_MODEL),
                  _full((D_MODEL, D_MODEL)), _full((1, D_MODEL)), _full((1, D_MODEL))],
        out_specs=(tile(D_MODEL), pl.BlockSpec((2, tm, ATTN_W), lambda i: (0, i, 0)), _full((D_MODEL, D_MODEL)),
                   _full((8, D_MODEL))),
        out_shape=(jax.ShapeDtypeStruct((SEQ, D_MODEL), F32),
                   jax.ShapeDtypeStruct((2, SEQ, ATTN_W), F32),
                   jax.ShapeDtypeStruct((D_MODEL, D_MODEL), F32),
                   jax.ShapeDtypeStruct((8, D_MODEL), F32)),
        compiler_params=_params(("arbitrary",), VMEM_LIMIT),
    )(ag, sg, x, target, wout, b_out, final_g)


def _mixers_bwd(sinks, dmix, q, kvx, out, gates, ln_g, ln_b, sgu_w, bias_full, gwout):
    last = N_BLOCKS - 1

    def body(sink_ref, d_ref, q_ref, kc_ref, o_ref, za_ref, dsg_ref, us_ref, vs_ref, zs_ref, lng_ref, lnb_ref, w_ref,
             bias_ref, gwout_ref,
             dp_ref, gsink_ref, gbin_ref, dps_ref, gw_ref, gb_ref, gln_ref, gbins_ref, wout_shard_ref,
             kp_ref, pend_ref, carry_ref, wm_ref, wt_ref, gbias_ref, sa_w, ra_w, sb_w, rc_w, send_sems, recv_sems):
        n = pl.program_id(0)
        start, exchange, finish = _reduce_scatter_plan(_Copies(send_sems, recv_sems), 0, gwout_ref, WOUT_ROWS,
                                                       sa_w, ra_w, sb_w, rc_w, wout_shard_ref)
        tril = (lax.broadcasted_iota(jnp.int32, (BLOCK, BLOCK), 0)
                >= lax.broadcasted_iota(jnp.int32, (BLOCK, BLOCK), 1))

        @pl.when(n == 0)
        def _():
            gsink_ref[...] = jnp.zeros_like(gsink_ref)
            gbin_ref[...] = jnp.zeros_like(gbin_ref)
            carry_ref[...] = jnp.zeros_like(carry_ref)
            kp_ref[...] = jnp.zeros_like(kp_ref)
            gw_ref[...] = jnp.zeros_like(gw_ref)
            gln_ref[...] = jnp.zeros_like(gln_ref)
            gbins_ref[...] = jnp.zeros_like(gbins_ref)
            gbias_ref[...] = jnp.zeros_like(gbias_ref)
            _mask_sgu_weights(w_ref, wm_ref, wt_ref)
            start()

        pl.when(n == 3)(exchange)
        pl.when(n == 12)(finish)

        @pl.when(n > 0)
        def _():
            dp_ref[:, 0:ATTN_W] = pend_ref[:, 0:ATTN_W]
            dp_ref[:, GATE0:ATTN_SECTION] = pend_ref[:, ATTN_W:]

        @pl.when(n > last)
        def _():
            dp_ref[:, KV0:GATE0] = carry_ref[...].astype(MXU_DTYPE)

        @pl.when(n <= last)
        def _():
            us = us_ref[...]
            vs = vs_ref[...]
            lng = lng_ref[...]
            u, vhat, rstd, vln = _sgu_activations(us, vs, lng, lnb_ref[...])
            low_sgu = lax.broadcasted_iota(jnp.int32, (BLOCK, LANES), 1) < HALF
            sgu = {}

            def sgu_gates():
                mixed = _sgu_mix(vln, wm_ref, bias_ref)
                sgu["du"], sgu["dzs"], sgu["dm"] = [], [], []
                for pair in range(N_SGU_HEADS // 2):
                    cols = slice(pair * LANES, (pair + 1) * LANES)
                    dsg = dsg_ref[:, cols]
                    gate, gate_grad = _silu_and_grad(zs_ref[:, cols])
                    up = u[:, cols]
                    sgu["du"].append(dsg * mixed[pair] * gate)
                    sgu["dzs"].append(dsg * up * mixed[pair] * gate_grad)
                    dmixed = dsg * up * gate
                    gbias_ref[:, cols] += dmixed
                    sgu["dm"].append((jnp.where(low_sgu, dmixed, 0.0).astype(MXU_DTYPE),
                                      jnp.where(low_sgu, 0.0, dmixed).astype(MXU_DTYPE)))

            def sgu_grads():
                dvln_parts = []
                for pair in range(N_SGU_HEADS // 2):
                    dm_lo, dm_hi = sgu["dm"][pair]
                    vp = vln[:, pair * LANES:(pair + 1) * LANES]
                    gw_ref[2 * pair] += _dot(dm_lo, vp, NT)
                    gw_ref[2 * pair + 1] += _dot(dm_hi, vp, NT)
                    dvln_parts.append(_dot(wt_ref[2 * pair], dm_lo) + _dot(wt_ref[2 * pair + 1], dm_hi))
                dvln = jnp.concatenate(dvln_parts, axis=1)
                gln_ref[0:1, :] += jnp.sum(dvln * vhat, axis=0, keepdims=True)
                gln_ref[1:2, :] += jnp.sum(dvln, axis=0, keepdims=True)
                dvhat = dvln * lng
                dvg = rstd * (dvhat - jnp.mean(dvhat, axis=-1, keepdims=True)
                              - vhat * jnp.mean(dvhat * vhat, axis=-1, keepdims=True))
                dus = jnp.concatenate(sgu["du"], axis=1) * _gelu_grad(us)
                dvs = dvg * _gelu_grad(vs)
                dzs = jnp.concatenate(sgu["dzs"], axis=1)
                for k, val in enumerate((dus, dvs, dzs)):
                    dps_ref[:, k * SGU_W:(k + 1) * SGU_W] = val.astype(MXU_DTYPE)
                    gbins_ref[:, k * SGU_W:(k + 1) * SGU_W] += jnp.sum(val, axis=0, keepdims=True)

            valid = _window_mask(n)[0:BLOCK]
            low = lax.broadcasted_iota(jnp.int32, (BLOCK, LANES), 1) < HALF
            low_keys = lax.broadcasted_iota(jnp.int32, (2 * BLOCK, LANES), 1) < HALF
            lane_row = lax.broadcasted_iota(jnp.int32, (1, LANES), 1)
            gsink = jnp.zeros((1, LANES), F32)
            chains = [(g, par, i) for g in range(2) for par in range(2) for i in range(2)]
            kv = {(g, par): _kv_cat(kp_ref, kc_ref, 2 * g + par, False) for g in range(2) for par in range(2)}
            ones_keys = jnp.ones((2 * BLOCK, LANES), MXU_DTYPE)
            half_of_lane = lax.broadcasted_iota(jnp.int32, (LANES, 2 * LANES), 0) // HALF
            half_of_col = lax.broadcasted_iota(jnp.int32, (LANES, 2 * LANES), 1) // LANES
            sum_halves = (half_of_lane == half_of_col).astype(MXU_DTYPE)
            douts, deltas = [], []
            for pair in range(N_PAIRS):
                lanes = slice(pair * LANES, (pair + 1) * LANES)
                dg = d_ref[:, lanes]
                gate, gate_grad = _silu_and_grad(za_ref[:, lanes])
                o = o_ref[pair]
                dout = dg * gate
                dza = dg * o * gate_grad
                douts.append(dout.astype(MXU_DTYPE))
                deltas.append(_dot(dout * o, sum_halves))
                zl = slice(ATTN_W + pair * LANES, ATTN_W + (pair + 1) * LANES)
                pend_ref[:, zl] = dza.astype(MXU_DTYPE)
                gl = slice(GATE0 + pair * LANES, GATE0 + (pair + 1) * LANES)
                gbin_ref[:, gl] += jnp.sum(dza, axis=0, keepdims=True)

            first = {}

            def issue_first(k):
                g, par, i = chains[k]
                first[k] = (_dot(q_ref[2 * g + i], kv[g, par][0], NT), _dot(douts[2 * g + i], kv[g, par][1], NT))

            numerators = {}

            def issue_row_sums(k):
                g, par, i = chains[k]
                sink = sink_ref[4 * g + 2 * i + par]
                e, m = _softmax_numerator(jnp.where(valid, first[k][0], NEG_INF), sink)
                numerators[k] = (e, jnp.exp(sink - m), _dot(e, ones_keys))

            ahead = ATTN_BWD_AHEAD
            for k in range(ahead):
                issue_first(k)
            issue_row_sums(0)
            issue_row_sums(1)
            dqs, dk_parts, dv_parts = {}, {}, {}
            operands = {}

            def issue_last(k):
                g, par, i = chains[k]
                ds, ds_t, p_t = operands.pop(k)
                dq = _dot(ds, kv[g, par][0])
                dqs[g, i] = dq if par == 0 else dqs[g, i] + dq
                dk = _dot(ds_t, q_ref[2 * g + i])
                dv = _dot(p_t, douts[2 * g + i])
                dk_parts[g, par] = dk if i == 0 else dk_parts[g, par] + dk
                dv_parts[g, par] = dv if i == 0 else dv_parts[g, par] + dv

            for k, (g, par, i) in enumerate(chains):
                h = 4 * g + 2 * i + par
                delta = deltas[2 * g + i][:, par * LANES:(par + 1) * LANES]
                e, at_sink, row_sum = numerators[k]
                inv = 1.0 / (row_sum + at_sink)
                p = e * jnp.tile(inv, (1, 2))
                ds = p * (first[k][1] - jnp.tile(delta, (1, 2)))
                ds = ds.astype(MXU_DTYPE)
                operands[k] = (ds, ds.T, p.astype(MXU_DTYPE).T)
                total = jnp.sum(at_sink * inv * delta, axis=0, keepdims=True)
                gsink = jnp.where(lane_row == h, -total, gsink)
                if k + ahead < len(chains):
                    issue_first(k + ahead)
                if k + 2 < len(chains):
                    issue_row_sums(k + 2)
                if k > 0:
                    issue_last(k - 1)
                if k == SGU_GATES_AFTER_CHAIN:
                    sgu_gates()
                if k == SGU_GRADS_AFTER_CHAIN:
                    sgu_grads()
            issue_last(len(chains) - 1)
            for pair in range(N_PAIRS):
                g, i = divmod(pair, 2)
                dq = dqs[g, i] * SCALE
                lanes = slice(pair * LANES, (pair + 1) * LANES)
                pend_ref[:, lanes] = dq.astype(MXU_DTYPE)
                gbin_ref[:, lanes] += jnp.sum(dq, axis=0, keepdims=True)
            gsink_ref[...] += gsink
            for k, parts in enumerate((dk_parts, dv_parts)):
                masked = {key: jnp.where(low_keys if key[1] == 0 else jnp.logical_not(low_keys), val, 0.0)
                          for key, val in parts.items()}
                both = (masked[0, 0] + masked[1, 1]
                        + pltpu.roll(masked[0, 1] + masked[1, 0], HALF, 1))
                lanes = slice(k * KV_W, (k + 1) * KV_W)
                done = carry_ref[:, lanes] + both[0:BLOCK]
                dp_ref[:, KV0 + k * KV_W:KV0 + (k + 1) * KV_W] = done.astype(MXU_DTYPE)
                carry_ref[:, lanes] = both[BLOCK:]
                gbin_ref[:, KV0 + k * KV_W:KV0 + (k + 1) * KV_W] += jnp.sum(both, axis=0, keepdims=True)
            kp_ref[...] = kc_ref[...]

        @pl.when(n == last)
        def _():
            for hh in range(N_SGU_HEADS):
                gw_ref[hh] = jnp.where(tril, gw_ref[hh], 0.0)
            head_of_lane = lax.broadcasted_iota(jnp.int32, (N_SGU_HEADS, SGU_W), 1) // HEAD_DIM
            select = (head_of_lane == lax.broadcasted_iota(jnp.int32, (N_SGU_HEADS, SGU_W), 0)).astype(F32)
            gb_ref[...] = lax.dot_general(select, gbias_ref[...], NT, precision=lax.Precision.HIGHEST,
                                          preferred_element_type=F32)

    at = lambda n: jnp.minimum(n, last)
    blk = lambda w: pl.BlockSpec((BLOCK, w), lambda n: (at(n), 0))
    tiles = pl.BlockSpec((N_PAIRS, BLOCK, LANES), lambda n: (0, at(n), 0))
    section = lambda k: pl.BlockSpec((None, BLOCK, SGU_W), lambda n: (k, at(n), 0))
    return pl.pallas_call(
        body,
        name="mixers_bwd",
        grid=(N_BLOCKS + 1,),
        in_specs=[pl.BlockSpec(memory_space=pltpu.SMEM),
                  section(0),
                  tiles,
                  blk(KVX_W),
                  tiles,
                  section(0),
                  section(1),
                  section(1), section(2), section(3),
                  _full((1, SGU_W)), _full((1, SGU_W)), _full((N_SGU_HEADS, BLOCK, BLOCK)), _full((BLOCK, SGU_W)),
                  VMEM_SPEC],
        out_specs=(pl.BlockSpec((BLOCK, ATTN_SECTION), lambda n: (jnp.maximum(n - 1, 0), 0)),
                   _full((1, LANES)), _full((1, ATTN_SECTION)),
                   pl.BlockSpec((BLOCK, SGU_SECTION), lambda n: (at(n), 0)),
                   _full((N_SGU_HEADS, BLOCK, BLOCK)), _full((N_SGU_HEADS, BLOCK)),
                   _full((8, SGU_W)), _full((1, SGU_SECTION)), VMEM_SPEC),
        out_shape=(jax.ShapeDtypeStruct((SEQ, ATTN_SECTION), MXU_DTYPE),
                   jax.ShapeDtypeStruct((1, LANES), F32),
                   jax.ShapeDtypeStruct((1, ATTN_SECTION), F32),
                   jax.ShapeDtypeStruct((SEQ, SGU_SECTION), MXU_DTYPE),
                   jax.ShapeDtypeStruct((N_SGU_HEADS, BLOCK, BLOCK), F32),
                   jax.ShapeDtypeStruct((N_SGU_HEADS, BLOCK), F32),
                   jax.ShapeDtypeStruct((8, SGU_W), F32),
                   jax.ShapeDtypeStruct((1, SGU_SECTION), F32),
                   jax.ShapeDtypeStruct((WOUT_ROWS, D_MODEL), F32)),
        scratch_shapes=([pltpu.VMEM((BLOCK, KVX_W), MXU_DTYPE),
                         pltpu.VMEM((BLOCK, 2 * ATTN_W), MXU_DTYPE), pltpu.VMEM((BLOCK, 2 * KV_W), F32),
                         pltpu.VMEM((N_SGU_HEADS, BLOCK, BLOCK), MXU_DTYPE),
                         pltpu.VMEM((N_SGU_HEADS, BLOCK, BLOCK), MXU_DTYPE), pltpu.VMEM((BLOCK, SGU_W), F32)]
                        + _reduce_scatter_scratch(WOUT_ROWS, D_MODEL, COMM_DTYPE) + _dma_sems(REDUCE_SEMS)),
        compiler_params=_params(("arbitrary",), VMEM_LIMIT),
    )(sinks, dmix, q, kvx, out, gates, dmix, gates, gates, gates, ln_g, ln_b, sgu_w, bias_full, gwout)


def _in_proj_bwd(dpa, dps, win_t, x, norm_g, gres, gwin, vec_parts):
    tm = TOKEN_TILE
    steps = SEQ // tm
    n_parts = len(vec_parts)

    def body(da_ref, ds_ref, w_ref, x_ref, g_ref, gres_ref, gwin_ref, *rest):
        part_refs = rest[:n_parts]
        gx_ref, shard_ref, vec_out_ref, gng_ref, sa, ra, sb, rc, vec_ref, ra_vec, slots, send_sems, recv_sems = (
            rest[n_parts:])
        step = pl.program_id(0)
        copies = _Copies(send_sems, recv_sems)
        start, exchange, finish = _reduce_scatter_plan(copies, 0, gwin_ref, WIN_ROWS, sa, ra, sb, rc, shard_ref)

        @pl.when(step == 0)
        def _():
            gng_ref[...] = jnp.zeros_like(gng_ref)
            start()

        pl.when(step == 2)(exchange)

        dh = _dot(da_ref[...], w_ref[0:ATTN_SECTION, :]) + _dot(ds_ref[...], w_ref[ATTN_SECTION:, :])
        xv = x_ref[...]
        r = lax.rsqrt(jnp.mean(xv * xv, axis=-1, keepdims=True) + NORM_EPS)
        xn = xv * r
        gng_ref[...] += jnp.sum(dh * xn, axis=0, keepdims=True)
        dxn = dh * g_ref[...]
        gx_ref[...] = r * (dxn - xn * jnp.mean(dxn * xn, axis=-1, keepdims=True)) + gres_ref[...]

        @pl.when(step == steps - 1)
        def _():
            finish()
            _all_reduce_vectors(copies, REDUCE_SEMS, gng_ref, *part_refs, vec_out_ref, vec_ref, ra_vec, slots)

    tile = lambda w: pl.BlockSpec((tm, w), lambda i: (i, 0))
    return pl.pallas_call(
        body,
        name="in_proj_bwd",
        grid=(steps,),
        in_specs=[tile(ATTN_SECTION), tile(SGU_SECTION), _full((IN_W, D_MODEL)), tile(D_MODEL),
                  _full((1, D_MODEL)), tile(D_MODEL), VMEM_SPEC] + [VMEM_SPEC] * n_parts,
        out_specs=(tile(D_MODEL), VMEM_SPEC, VMEM_SPEC),
        out_shape=(jax.ShapeDtypeStruct((SEQ, D_MODEL), F32),
                   jax.ShapeDtypeStruct((WIN_ROWS, D_MODEL), F32),
                   jax.ShapeDtypeStruct((VEC_ROWS, IN_W), F32)),
        scratch_shapes=([pltpu.VMEM((1, D_MODEL), F32)] + _reduce_scatter_scratch(WIN_ROWS, D_MODEL, COMM_DTYPE)
                        + _vector_scratch() + _dma_sems(REDUCE_SEMS + VECTOR_SEMS)),
        compiler_params=_params(("arbitrary",), VMEM_LIMIT),
    )(dpa, dps, win_t, x, norm_g, gres, gwin, *vec_parts)


def _win_grad(dpa, dps, h, gsguw):
    rows = 256
    n_attn = ATTN_SECTION // rows
    steps = n_attn + SGU_SECTION // rows

    def body(da_ref, ds_ref, h_ref, gsguw_ref, o_ref, sguw_full_ref, sa, ra, sb, rc, landing, send_sems, recv_sems):
        step = pl.program_id(0)
        copies = _Copies(send_sems, recv_sems)
        own_sguw = landing.at[_block_rows(_place(), SGUW_ROWS), :]
        start, exchange, finish = _reduce_scatter_plan(copies, 0, gsguw_ref, SGUW_ROWS, sa, ra, sb, rc, own_sguw)
        gather = _gather_plan(copies, REDUCE_SEMS, landing, SGUW_ROWS)

        pl.when(step == 0)(start)
        pl.when(step == 2)(exchange)

        @pl.when(step == 5)
        def _():
            finish()
            gather[0]()

        pl.when(step == 7)(gather[1])

        @pl.when(step < n_attn)
        def _():
            o_ref[...] = _dot(da_ref[...], h_ref[...], TN)

        @pl.when(step >= n_attn)
        def _():
            o_ref[...] = _dot(ds_ref[...], h_ref[...], TN)

        @pl.when(step == steps - 1)
        def _():
            gather[2]()
            sguw_full_ref[...] = landing[...]

    return pl.pallas_call(
        body,
        name="win_grad",
        grid=(steps,),
        in_specs=[pl.BlockSpec((SEQ, rows), lambda i: (0, jnp.minimum(i, n_attn - 1))),
                  pl.BlockSpec((SEQ, rows), lambda i: (0, jnp.maximum(i - n_attn, 0))),
                  _full((SEQ, D_MODEL)), VMEM_SPEC],
        out_specs=(pl.BlockSpec((rows, D_MODEL), lambda i: (i, 0)), _full((N_SGU_HEADS * BLOCK, BLOCK))),
        out_shape=(jax.ShapeDtypeStruct((IN_W, D_MODEL), F32),
                   jax.ShapeDtypeStruct((N_SGU_HEADS * BLOCK, BLOCK), F32)),
        scratch_shapes=(_reduce_scatter_scratch(SGUW_ROWS, BLOCK, F32)
                        + [pltpu.VMEM((N_SGU_HEADS * BLOCK, BLOCK), F32)]
                        + _dma_sems(REDUCE_SEMS + GATHER_SEMS)),
        compiler_params=_params(("arbitrary",), VMEM_LIMIT),
    )(dpa, dps, h, gsguw)


VEC_NORM_G, VEC_B_IN, VEC_SINKS, VEC_LN_G, VEC_LN_B, VEC_B_OUT, VEC_FINAL_G, VEC_LOSS, VEC_SGU_B = 0, 1, 2, 3, 4, 5, 6, 7, 8


def _adamw(w, g, m, v):
    m = ADAM_B1 * m + (1.0 - ADAM_B1) * g
    v = ADAM_B2 * v + (1.0 - ADAM_B2) * (g * g)
    m_hat = m / (1.0 - ADAM_B1 ** ADAM_STEP)
    v_hat = v / (1.0 - ADAM_B2 ** ADAM_STEP)
    delta = -ADAM_LR * (m_hat / (jnp.sqrt(v_hat) + ADAM_EPS) + ADAM_WD * w)
    return delta, m, v


def _adamw_shard(name, g, w, m, v, block_rows):
    def body(g_ref, w_ref, m_ref, v_ref, d_ref, nm_ref, nv_ref):
        d_ref[...], nm_ref[...], nv_ref[...] = _adamw(w_ref[...], g_ref[...], m_ref[...], v_ref[...])

    rows, cols = w.shape
    spec = pl.BlockSpec((block_rows, cols), lambda i: (i, 0))
    return pl.pallas_call(
        body,
        name=name,
        grid=(rows // block_rows,),
        in_specs=[spec] * 4,
        out_specs=(spec,) * 3,
        out_shape=(jax.ShapeDtypeStruct(w.shape, F32),) * 3,
        compiler_params=_params(("arbitrary",)),
    )(g, w, m, v)


VECTOR_SEMS = 4


def _vector_scratch():
    return [pltpu.VMEM((VEC_ROWS, IN_W), F32), pltpu.VMEM((VEC_ROWS, IN_W), F32),
            pltpu.VMEM((4 * VEC_ROWS, IN_W), F32)]


def _all_reduce_vectors(copies, sem0, gng_ref, gba_ref, gbs_ref, gsink_ref, gln_ref, gsgub_ref, vec4_ref, out_ref,
                        vec_ref, ra_vec, slots):
    x, y, c = _place()
    vec_ref[...] = jnp.zeros_like(vec_ref)
    vec_ref[VEC_NORM_G:VEC_NORM_G + 1, 0:D_MODEL] = gng_ref[...]
    vec_ref[VEC_B_IN:VEC_B_IN + 1, 0:ATTN_SECTION] = gba_ref[...]
    vec_ref[VEC_B_IN:VEC_B_IN + 1, ATTN_SECTION:IN_W] = gbs_ref[...]
    vec_ref[VEC_SINKS:VEC_SINKS + 1, 0:LANES] = gsink_ref[...]
    vec_ref[VEC_LN_G:VEC_LN_G + 1, 0:SGU_W] = gln_ref[0:1, :]
    vec_ref[VEC_LN_B:VEC_LN_B + 1, 0:SGU_W] = gln_ref[1:2, :]
    vec_ref[VEC_B_OUT:VEC_B_OUT + 1, 0:D_MODEL] = vec4_ref[2:3, :]
    vec_ref[VEC_FINAL_G:VEC_FINAL_G + 1, 0:D_MODEL] = vec4_ref[1:2, :]
    vec_ref[VEC_LOSS:VEC_LOSS + 1, 0:D_MODEL] = vec4_ref[0:1, :]
    vec_ref[VEC_SGU_B:VEC_SGU_B + N_SGU_HEADS, 0:BLOCK] = gsgub_ref[...]

    to_sibling = copies(sem0, vec_ref, ra_vec, (x, y, 1 - c))
    to_sibling.start()
    to_sibling.wait_recv()

    def chip_slot(place):
        return slots.at[pl.ds(pl.multiple_of((2 * place[0] + place[1]) * VEC_ROWS, 8), VEC_ROWS), :]

    mine = chip_slot((x, y))
    mine[...] = vec_ref[...] + ra_vec[...]
    to_chips = [copies(sem0 + i, mine, mine, (*_chip(rel), c)) for i, rel in enumerate(RELATIONS[1:], start=1)]
    for cp in to_chips:
        cp.start()
    for i, rel in enumerate(RELATIONS[1:], start=1):
        theirs = chip_slot(_chip(rel))
        copies(sem0 + i, theirs, theirs, (x, y, c)).wait_recv()
    out_ref[...] = ((slots[0:VEC_ROWS, :] + slots[VEC_ROWS:2 * VEC_ROWS, :])
                    + slots[2 * VEC_ROWS:3 * VEC_ROWS, :]) + slots[3 * VEC_ROWS:, :]
    to_sibling.wait_send()
    for cp in to_chips:
        cp.wait_send()


def _adamw_replicated(vec, gsguw, weights, m_state, v_state):
    n = len(SMALL)

    def body(*refs):
        vec_ref, gsguw_ref = refs[0], refs[1]
        w_refs, m_refs, v_refs = (refs[2 + k * n:2 + (k + 1) * n] for k in range(3))
        outs = refs[2 + 3 * n:]
        g_refs, d_refs, nm_refs, nv_refs = (outs[k * n:(k + 1) * n] for k in range(4))
        for i, (_, row, shape) in enumerate(SMALL):
            g = gsguw_ref[...] if row is None else vec_ref[row:row + shape[0], 0:shape[1]]
            g_refs[i][...] = g
            d_refs[i][...], nm_refs[i][...], nv_refs[i][...] = _adamw(
                w_refs[i][...], g, m_refs[i][...], v_refs[i][...])

    shapes = tuple(jax.ShapeDtypeStruct(shape, F32) for _, _, shape in SMALL)
    outs = pl.pallas_call(
        body,
        name="adamw_replicated",
        in_specs=[VMEM_SPEC] * (2 + 3 * n),
        out_specs=(VMEM_SPEC,) * (4 * n),
        out_shape=shapes * 4,
    )(vec, gsguw, *weights, *m_state, *v_state)
    return tuple(outs[k * n:(k + 1) * n] for k in range(4))


SMALL = (
    ("norm_g", VEC_NORM_G, (1, D_MODEL)),
    ("b_in", VEC_B_IN, (1, IN_W)),
    ("attn_sinks", VEC_SINKS, (1, N_Q_HEADS)),
    ("sgu_ln_g", VEC_LN_G, (1, SGU_W)),
    ("sgu_ln_b", VEC_LN_B, (1, SGU_W)),
    ("sgu_w", None, (N_SGU_HEADS * BLOCK, BLOCK)),
    ("sgu_b", VEC_SGU_B, (N_SGU_HEADS, BLOCK)),
    ("b_out", VEC_B_OUT, (1, D_MODEL)),
    ("final_norm_g", VEC_FINAL_G, (1, D_MODEL)),
)


def _local_grads(x, target, win_t, wout_shard, norm_g, b_in, attn_sinks, sgu_ln_g, sgu_ln_b, sgu_w, sgu_b, b_out,
                 final_g):
    sinks = attn_sinks.reshape(N_Q_HEADS)
    bias_full = jnp.repeat(sgu_b.T, HEAD_DIM, axis=1)
    h, q, kvx, gates, wout = _in_proj(x, norm_g, b_in, win_t, wout_shard)
    out, ag, sg = _mixers_fwd(sinks, q, kvx, gates, sgu_ln_g, sgu_ln_b, sgu_w, bias_full)
    gres, dmix, gwout, vec4 = _out_proj_loss(ag, sg, x, target, wout, b_out, final_g)
    dpa, gsink, gbin_a, dps, gsguw, gsgub, gln, gbin_s, gwout_shard = _mixers_bwd(
        sinks, dmix, q, kvx, out, gates, sgu_ln_g, sgu_ln_b, sgu_w, bias_full, gwout)
    gwin, gsguw_sum = _win_grad(dpa, dps, h, gsguw.reshape(N_SGU_HEADS * BLOCK, BLOCK))
    grad_x, gwin_shard, vec = _in_proj_bwd(dpa, dps, win_t, x, norm_g, gres, gwin,
                                           (gbin_a, gbin_s, gsink, gln, gsgub, vec4))
    return grad_x, gwin_shard, gwout_shard, gsguw_sum, vec


def kernel(x, norm_g, w_in, b_in, attn_sinks, sgu_ln_g, sgu_ln_b, sgu_w, sgu_b, w_out, b_out, final_norm_g, loss_target, m_norm_g, m_w_in, m_b_in, m_attn_sinks, m_sgu_ln_g, m_sgu_ln_b, m_sgu_w, m_sgu_b, m_w_out, m_b_out, m_final_norm_g, v_norm_g, v_w_in, v_b_in, v_attn_sinks, v_sgu_ln_g, v_sgu_ln_b, v_sgu_w, v_sgu_b, v_w_out, v_b_out, v_final_norm_g):
    given = dict(norm_g=norm_g, b_in=b_in, attn_sinks=attn_sinks, sgu_ln_g=sgu_ln_g, sgu_ln_b=sgu_ln_b,
                 sgu_w=sgu_w, sgu_b=sgu_b, b_out=b_out, final_norm_g=final_norm_g)
    m_given = dict(norm_g=m_norm_g, b_in=m_b_in, attn_sinks=m_attn_sinks, sgu_ln_g=m_sgu_ln_g,
                   sgu_ln_b=m_sgu_ln_b, sgu_w=m_sgu_w, sgu_b=m_sgu_b, b_out=m_b_out, final_norm_g=m_final_norm_g)
    v_given = dict(norm_g=v_norm_g, b_in=v_b_in, attn_sinks=v_attn_sinks, sgu_ln_g=v_sgu_ln_g,
                   sgu_ln_b=v_sgu_ln_b, sgu_w=v_sgu_w, sgu_b=v_sgu_b, b_out=v_b_out, final_norm_g=v_final_norm_g)

    win_t = _all_gather_win(w_in[0].T)
    grad_x, gwin_t, gwout, gsguw, vec = _local_grads(
        x[0], loss_target[0], win_t, w_out[0], norm_g, b_in, attn_sinks, sgu_ln_g, sgu_ln_b, sgu_w[0], sgu_b[0],
        b_out, final_norm_g.reshape(1, D_MODEL))

    t = lambda a: a[0].T
    d_win, nm_win, nv_win = _adamw_shard("adamw_w_in", gwin_t, t(w_in), t(m_w_in), t(v_w_in), WIN_ROWS // 2)
    d_wout, nm_wout, nv_wout = _adamw_shard("adamw_w_out", gwout, w_out[0], m_w_out[0], v_w_out[0], WOUT_ROWS)
    as_2d = lambda d: [d[name].reshape(shape) for name, _, shape in SMALL]
    loss = vec[VEC_LOSS, 0]
    small = _adamw_replicated(vec, gsguw, as_2d(given), as_2d(m_given), as_2d(v_given))

    def assemble(big_in, big_out, k):
        vals = {name: small[k][i].reshape(given[name].shape) for i, (name, _, _) in enumerate(SMALL)}
        vals["w_in"] = big_in.T[None]
        vals["w_out"] = big_out[None]
        order = ("norm_g", "w_in", "b_in", "attn_sinks", "sgu_ln_g", "sgu_ln_b", "sgu_w", "sgu_b", "w_out",
                 "b_out", "final_norm_g")
        return [vals[name] for name in order]

    return (loss, grad_x[None],
            *assemble(gwin_t, gwout, 0), *assemble(d_win, d_wout, 1),
            *assemble(nm_win, nm_wout, 2), *assemble(nv_win, nv_wout, 3))
```

```python
import functools
import math

import jax
import jax.numpy as jnp
from jax import lax
from jax.experimental import pallas as pl
from jax.experimental.pallas import tpu as pltpu

F32 = jnp.float32
BF16 = jnp.bfloat16
MXU_DTYPE = BF16
COMM_DTYPE = BF16

D_MODEL = 1024
SEQ = 4096
HEAD_DIM = 64
N_Q_HEADS = 8
Q_PER_KV = 4
BLOCK = 128
N_BLOCKS = SEQ // BLOCK
ATTN_W = 512
KV_W = 128
SGU_W = 512
N_SGU_HEADS = 8
IN_W = 2816
NORM_EPS = 1e-5
NEG_INF = -1e30
SCALE = HEAD_DIM ** -0.5
KV0 = ATTN_W
GATE0 = ATTN_W + 2 * KV_W
SGU0 = GATE0 + ATTN_W
ATTN_SECTION = SGU0
SGU_SECTION = IN_W - SGU0

ADAM_LR = 0.001
ADAM_B1 = 0.9
ADAM_B2 = 0.999
ADAM_EPS = 1e-08
ADAM_WD = 0.01
ADAM_STEP = 10

N_DEV = 8
WIN_ROWS = IN_W // N_DEV
WOUT_ROWS = D_MODEL // N_DEV
SGUW_ROWS = N_SGU_HEADS * BLOCK // N_DEV
VEC_ROWS = 16
MESH = pl.DeviceIdType.MESH

LANES = 128
HALF = LANES // 2
N_PAIRS = N_Q_HEADS * HEAD_DIM // LANES
KVX_W = 12 * LANES
TOKEN_TILE = 256
FWD_TOKEN_TILE = 512
ATTN_FWD_AHEAD = 4
FWD_BLOCKS_PER_STEP = 4
SGU_MIX_AFTER_CHAIN = 0
SGU_GATES_AFTER_CHAIN = 1
SGU_GRADS_AFTER_CHAIN = 5
ATTN_BWD_AHEAD = 3
VMEM_LIMIT = 56 * 1024 * 1024

NN = (((1,), (0,)), ((), ()))
NT = (((1,), (1,)), ((), ()))
TN = (((0,), (0,)), ((), ()))


def _dot(a, b, dims=NN):
    return lax.dot_general(a.astype(MXU_DTYPE), b.astype(MXU_DTYPE), dims, preferred_element_type=F32)


def _gelu(x):
    return x * (lax.erf(x * (1.0 / math.sqrt(2.0))) + 1.0) * 0.5


def _gelu_grad(x):
    cdf = (lax.erf(x * (1.0 / math.sqrt(2.0))) + 1.0) * 0.5
    return cdf + x * jnp.exp(-0.5 * x * x) * (1.0 / math.sqrt(2.0 * math.pi))


def _silu_and_grad(z):
    s = jax.nn.sigmoid(z)
    return z * s, s * (1.0 + z * (1.0 - s))


def _params(semantics=None, vmem=None):
    kw = {}
    if semantics is not None:
        kw["dimension_semantics"] = semantics
    if vmem is not None:
        kw["vmem_limit_bytes"] = vmem
    return pltpu.CompilerParams(**kw)


def _full(shape):
    return pl.BlockSpec(shape, lambda *_: (0,) * len(shape))


VMEM_SPEC = pl.BlockSpec(memory_space=pltpu.VMEM)


RELATIONS = ((0, 0), (1, 0), (0, 1), (1, 1))


def _place():
    return lax.axis_index("x"), lax.axis_index("y"), lax.axis_index("c")


def _chip(rel):
    x, y, _ = _place()
    return (1 - x if rel[0] else x, 1 - y if rel[1] else y)


def _block_rows(place, n_rows):
    px, py, pc = place
    return pl.ds(pl.multiple_of((4 * px + 2 * py + pc) * n_rows, 16), n_rows)


class _Copies:
    def __init__(self, send_sems, recv_sems):
        self.send_sems, self.recv_sems = send_sems, recv_sems

    def __call__(self, k, src, dst, to):
        return pltpu.make_async_remote_copy(src_ref=src, dst_ref=dst, send_sem=self.send_sems.at[k],
                                            recv_sem=self.recv_sems.at[k], device_id=to, device_id_type=MESH)


def _gather_plan(copies, sem0, full_ref, n_rows):
    x, y, c = _place()
    me, sibling = (x, y, c), (x, y, 1 - c)
    chips = [_chip(rel) for rel in RELATIONS[1:]]

    def cp(k, block, to):
        rows = full_ref.at[_block_rows(block, n_rows), :]
        return copies(sem0 + k, rows, rows, to)

    first = [cp(0, me, sibling)] + [cp(1 + j, me, (*chip, c)) for j, chip in enumerate(chips)]
    passed = [cp(4 + j, (*chip, c), sibling) for j, chip in enumerate(chips)]

    def start():
        for f in first:
            f.start()

    def forward():
        for j, chip in enumerate(chips):
            cp(1 + j, (*chip, c), me).wait_recv()
            passed[j].start()

    def finish():
        cp(0, sibling, me).wait_recv()
        for j, chip in enumerate(chips):
            cp(4 + j, (*chip, 1 - c), me).wait_recv()
        for f in first + passed:
            f.wait_send()

    return start, forward, finish


GATHER_SEMS = 7


def _reduce_scatter_plan(copies, sem0, part_ref, n_rows, sa, ra, sb, rc, res_ref):
    x, y, c = _place()
    sibling = (x, y, 1 - c)
    n = n_rows
    level1 = copies(sem0, sa, ra, sibling)

    def level2(i):
        slot = pl.ds((i - 1) * n, n)
        return copies(sem0 + i, sb.at[slot, :], rc.at[slot, :], (*_chip(RELATIONS[i]), c))

    def start():
        for i, rel in enumerate(RELATIONS):
            sa[i * n:(i + 1) * n, :] = part_ref[_block_rows((*_chip(rel), 1 - c), n), :].astype(sa.dtype)
        level1.start()

    def exchange():
        level1.wait_recv()
        for i, rel in enumerate(RELATIONS):
            total = part_ref[_block_rows((*_chip(rel), c), n), :] + ra[i * n:(i + 1) * n, :].astype(F32)
            if i == 0:
                res_ref[...] = total
            else:
                sb[(i - 1) * n:i * n, :] = total.astype(sb.dtype)
                level2(i).start()

    def finish():
        acc = res_ref[...]
        for i in range(1, len(RELATIONS)):
            level2(i).wait_recv()
            acc = acc + rc[(i - 1) * n:i * n, :].astype(F32)
        res_ref[...] = acc
        level1.wait_send()
        for i in range(1, len(RELATIONS)):
            level2(i).wait_send()

    return start, exchange, finish


REDUCE_SEMS = 4


def _reduce_scatter_scratch(n_rows, width, dtype):
    return [pltpu.VMEM((4 * n_rows, width), dtype), pltpu.VMEM((4 * n_rows, width), dtype),
            pltpu.VMEM((3 * n_rows, width), dtype), pltpu.VMEM((3 * n_rows, width), dtype)]


def _dma_sems(n):
    return [pltpu.SemaphoreType.DMA((n,)), pltpu.SemaphoreType.DMA((n,))]


def _all_gather_win(win_t_shard, x, norm_g):
    tm = FWD_TOKEN_TILE
    steps = SEQ // tm

    def body(win_ref, x_ref, g_ref, full_ref, h_ref, landing, send_sems, recv_sems):
        step = pl.program_id(0)
        start, forward, finish = _gather_plan(_Copies(send_sems, recv_sems), 0, landing, WIN_ROWS)

        @pl.when(step == 0)
        def _():
            landing[_block_rows(_place(), WIN_ROWS), :] = win_ref[...].astype(COMM_DTYPE)
            start()

        xv = x_ref[...]
        r = lax.rsqrt(jnp.mean(xv * xv, axis=-1, keepdims=True) + NORM_EPS)
        h_ref[...] = ((xv * r) * g_ref[...]).astype(MXU_DTYPE)

        @pl.when(step == steps - 1)
        def _():
            forward()
            finish()
            full_ref[...] = landing[...]

    return pl.pallas_call(
        body,
        name="all_gather_win",
        grid=(steps,),
        in_specs=[VMEM_SPEC, pl.BlockSpec((tm, D_MODEL), lambda i: (i, 0)), _full((1, D_MODEL))],
        out_specs=(_full((IN_W, D_MODEL)), pl.BlockSpec((tm, D_MODEL), lambda i: (i, 0))),
        out_shape=(jax.ShapeDtypeStruct((IN_W, D_MODEL), COMM_DTYPE),
                   jax.ShapeDtypeStruct((SEQ, D_MODEL), MXU_DTYPE)),
        scratch_shapes=[pltpu.VMEM((IN_W, D_MODEL), COMM_DTYPE)] + _dma_sems(GATHER_SEMS),
        compiler_params=_params(("arbitrary",), VMEM_LIMIT),
    )(win_t_shard, x, norm_g)


def _in_proj(h, b_in, win_t, wout_shard):
    tm = FWD_TOKEN_TILE
    steps = SEQ // tm

    def body(h_ref, b_ref, w_ref, wout_ref, q_ref, kvx_ref, gate_ref, wfull_ref, landing, send_sems, recv_sems):
        step = pl.program_id(0)
        start, forward, finish = _gather_plan(_Copies(send_sems, recv_sems), 0, landing, WOUT_ROWS)

        @pl.when(step == 0)
        def _():
            landing[_block_rows(_place(), WOUT_ROWS), :] = wout_ref[...].astype(COMM_DTYPE)
            start()

        pl.when(step == steps // 2)(forward)

        h = h_ref[...]

        def proj(lo, hi):
            return _dot(h, w_ref[lo:hi, :], NT) + b_ref[:, lo:hi]

        qs = proj(0, ATTN_W) * SCALE
        for pair in range(N_PAIRS):
            q_ref[pair] = qs[:, pair * LANES:(pair + 1) * LANES].astype(MXU_DTYPE)
        kv = proj(KV0, GATE0)
        low = lax.broadcasted_iota(jnp.int32, (tm, LANES), 1) < HALF
        for i in range(2):
            t = kv[:, i * LANES:(i + 1) * LANES]
            rot = pltpu.roll(t, HALF, 1)
            variants = (jnp.where(low, t, 0.0), jnp.where(low, 0.0, rot),
                        jnp.where(low, rot, 0.0), jnp.where(low, 0.0, t))
            for j, val in enumerate(variants):
                col = (4 * i + j) * LANES
                kvx_ref[:, col:col + LANES] = val.astype(MXU_DTYPE)
                if i == 1:
                    ones_elsewhere = jnp.where(low == (j % 2 == 0), val, 1.0)
                    kvx_ref[:, col + 4 * LANES:col + 5 * LANES] = ones_elsewhere.astype(MXU_DTYPE)
        for k in range(4):
            gate_ref[k] = proj(GATE0 + k * SGU_W, GATE0 + (k + 1) * SGU_W)

        @pl.when(step == steps - 1)
        def _():
            finish()
            wfull_ref[...] = landing[...]

    return pl.pallas_call(
        body,
        name="in_proj",
        grid=(steps,),
        in_specs=[pl.BlockSpec((tm, D_MODEL), lambda i: (i, 0)),
                  _full((1, IN_W)), _full((IN_W, D_MODEL)), VMEM_SPEC],
        out_specs=(pl.BlockSpec((N_PAIRS, tm, LANES), lambda i: (0, i, 0)),
                   pl.BlockSpec((tm, KVX_W), lambda i: (i, 0)),
                   pl.BlockSpec((4, tm, SGU_W), lambda i: (0, i, 0)),
                   _full((D_MODEL, D_MODEL))),
        out_shape=(jax.ShapeDtypeStruct((N_PAIRS, SEQ, LANES), MXU_DTYPE),
                   jax.ShapeDtypeStruct((SEQ, KVX_W), MXU_DTYPE),
                   jax.ShapeDtypeStruct((4, SEQ, SGU_W), F32),
                   jax.ShapeDtypeStruct((D_MODEL, D_MODEL), COMM_DTYPE)),
        scratch_shapes=[pltpu.VMEM((D_MODEL, D_MODEL), COMM_DTYPE)] + _dma_sems(GATHER_SEMS),
        compiler_params=_params(("arbitrary",), VMEM_LIMIT),
    )(h, b_in, win_t, wout_shard)


def _window_mask(n):
    qi = lax.broadcasted_iota(jnp.int32, (2 * BLOCK, 2 * BLOCK), 0) & (BLOCK - 1)
    p = lax.broadcasted_iota(jnp.int32, (2 * BLOCK, 2 * BLOCK), 1) - BLOCK
    in_window = jnp.logical_and(p <= qi, p > qi - BLOCK)
    return jnp.logical_and(in_window, jnp.logical_or(p >= 0, n > 0))


def _sink_column(sink_ref, g, par):
    return jnp.concatenate([jnp.full((BLOCK, 1), sink_ref[4 * g + par], F32),
                            jnp.full((BLOCK, 1), sink_ref[4 * g + 2 + par], F32)], axis=0)


def _kv_cat(kp_ref, kc_ref, var, with_ones):
    kcol, vcol = var * LANES, (var + (8 if with_ones else 4)) * LANES
    return (jnp.concatenate([kp_ref[:, kcol:kcol + LANES], kc_ref[:, kcol:kcol + LANES]], axis=0),
            jnp.concatenate([kp_ref[:, vcol:vcol + LANES], kc_ref[:, vcol:vcol + LANES]], axis=0))


def _softmax_numerator(s, sink):
    m = jnp.maximum(jnp.max(s, axis=1, keepdims=True), sink)
    return jnp.exp(s - m), m


def _mixers_fwd(sinks, q, kvx, gates, ln_g, ln_b, sgu_w, bias_full):
    per_step = FWD_BLOCKS_PER_STEP
    rows_per_step = per_step * BLOCK

    def body(sink_ref, q_ref, kc_ref, za_ref, us_ref, vs_ref, zs_ref, lng_ref, lnb_ref, w_ref, bias_ref,
             out_ref, ag_ref, sg_ref, kp_ref, wm_ref):
        @pl.when(pl.program_id(0) == 0)
        def _():
            kp_ref[...] = jnp.zeros_like(kp_ref)
            _mask_sgu_weights(w_ref, wm_ref)

        def one_block(b, carry):
            rows = pl.ds(pl.multiple_of(b * BLOCK, BLOCK), BLOCK)
            kc = kc_ref.at[rows, :]
            u, _, _, vln = _sgu_activations(us_ref[rows, :], vs_ref[rows, :], lng_ref[...], lnb_ref[...])

            valid = _window_mask(pl.program_id(0) * per_step + b)[0:BLOCK]
            chains = [(g, par, i) for g in range(2) for par in range(2) for i in range(2)]
            kv = {(g, par): _kv_cat(kp_ref, kc, 2 * g + par, True) for g in range(2) for par in range(2)}
            scores, outs = {}, {}

            def issue_scores(k):
                g, par, i = chains[k]
                scores[k] = _dot(q_ref[2 * g + i, rows, :], kv[g, par][0], NT)

            ahead = ATTN_FWD_AHEAD
            for k in range(ahead):
                issue_scores(k)
            low = lax.broadcasted_iota(jnp.int32, (BLOCK, LANES), 1) < HALF
            for k, (g, par, i) in enumerate(chains):
                sink = sink_ref[4 * g + 2 * i + par]
                e, m = _softmax_numerator(jnp.where(valid, scores[k], NEG_INF), sink)
                if k + ahead < len(chains):
                    issue_scores(k + ahead)
                o = _dot(e, kv[g, par][1])
                outs[g, par, i] = o / (pltpu.roll(o, HALF, 1) + jnp.exp(sink - m))
                if k == SGU_MIX_AFTER_CHAIN:
                    mixed = _sgu_mix(vln, wm_ref, bias_ref)
            for pair in range(N_PAIRS):
                g, i = divmod(pair, 2)
                lanes = slice(pair * LANES, (pair + 1) * LANES)
                o = jnp.where(low, outs[g, 0, i], outs[g, 1, i])
                out_ref[pair, rows, :] = o
                gate, _ = _silu_and_grad(za_ref[rows, lanes])
                ag_ref[rows, lanes] = (o * gate).astype(MXU_DTYPE)
            kp_ref[...] = kc[...]
            for pair in range(N_SGU_HEADS // 2):
                cols = slice(pair * LANES, (pair + 1) * LANES)
                gate, _ = _silu_and_grad(zs_ref[rows, cols])
                sg_ref[rows, cols] = (u[:, cols] * mixed[pair] * gate).astype(MXU_DTYPE)
            return carry

        lax.fori_loop(0, per_step, one_block, 0)

    blk = lambda w: pl.BlockSpec((rows_per_step, w), lambda n: (n, 0))
    tiles = pl.BlockSpec((N_PAIRS, rows_per_step, LANES), lambda n: (0, n, 0))
    gate = lambda k: pl.BlockSpec((None, rows_per_step, SGU_W), lambda n: (k, n, 0))
    return pl.pallas_call(
        body,
        name="mixers_fwd",
        grid=(N_BLOCKS // per_step,),
        in_specs=[pl.BlockSpec(memory_space=pltpu.SMEM), tiles, blk(KVX_W), gate(0), gate(1), gate(2), gate(3),
                  _full((1, SGU_W)), _full((1, SGU_W)), _full((N_SGU_HEADS, BLOCK, BLOCK)), _full((BLOCK, SGU_W))],
        out_specs=(tiles, blk(ATTN_W), blk(SGU_W)),
        out_shape=(jax.ShapeDtypeStruct((N_PAIRS, SEQ, LANES), F32),
                   jax.ShapeDtypeStruct((SEQ, ATTN_W), MXU_DTYPE),
                   jax.ShapeDtypeStruct((SEQ, SGU_W), MXU_DTYPE)),
        scratch_shapes=[pltpu.VMEM((BLOCK, KVX_W), MXU_DTYPE), pltpu.VMEM((N_SGU_HEADS, BLOCK, BLOCK), MXU_DTYPE)],
        compiler_params=_params(("arbitrary",)),
    )(sinks, q, kvx, gates, gates, gates, gates, ln_g, ln_b, sgu_w, bias_full)


def _sgu_activations(us, vs, lng, lnb):
    u = _gelu(us)
    vg = _gelu(vs)
    mu = jnp.mean(vg, axis=-1, keepdims=True)
    xc = vg - mu
    rstd = lax.rsqrt(jnp.mean(xc * xc, axis=-1, keepdims=True) + NORM_EPS)
    vhat = xc * rstd
    return u, vhat, rstd, vhat * lng + lnb


def _mask_sgu_weights(w_ref, masked_ref, transposed_ref=None):
    tril = (lax.broadcasted_iota(jnp.int32, (BLOCK, BLOCK), 0)
            >= lax.broadcasted_iota(jnp.int32, (BLOCK, BLOCK), 1))
    for hh in range(N_SGU_HEADS):
        w = jnp.where(tril, w_ref[hh], 0.0)
        masked_ref[hh] = w.astype(MXU_DTYPE)
        if transposed_ref is not None:
            transposed_ref[hh] = w.T.astype(MXU_DTYPE)


def _sgu_mix(vln, masked_w_ref, bias_ref):
    low = lax.broadcasted_iota(jnp.int32, (BLOCK, LANES), 1) < HALF
    mixed = []
    for pair in range(N_SGU_HEADS // 2):
        vp = vln[:, pair * LANES:(pair + 1) * LANES]
        mixed.append(_dot(masked_w_ref[2 * pair], jnp.where(low, vp, 0.0))
                     + _dot(masked_w_ref[2 * pair + 1], jnp.where(low, 0.0, vp))
                     + bias_ref[:, pair * LANES:(pair + 1) * LANES])
    return mixed


def _out_proj_loss(ag, sg, x, target, wout, b_out, final_g):
    tm = FWD_TOKEN_TILE

    def body(ag_ref, sg_ref, x_ref, t_ref, w_ref, b_ref, gf_ref, gres_ref, dmix_ref, gw_ref, vec_ref):
        @pl.when(pl.program_id(0) == 0)
        def _():
            gw_ref[...] = jnp.zeros_like(gw_ref)
            vec_ref[...] = jnp.zeros_like(vec_ref)

        a = ag_ref[...]
        s = sg_ref[...]
        xo = x_ref[...] + (_dot(a, w_ref[0:ATTN_W, :]) + _dot(s, w_ref[ATTN_W:, :])) + b_ref[...]
        r = lax.rsqrt(jnp.mean(xo * xo, axis=-1, keepdims=True) + NORM_EPS)
        xn = xo * r
        gf = gf_ref[...]
        err = xn * gf - t_ref[...]
        loss = 0.5 * jnp.sum(jnp.mean(err * err, axis=-1, keepdims=True), axis=0, keepdims=True)
        dy = err * (1.0 / D_MODEL)
        dxn = dy * gf
        gres = r * (dxn - xn * jnp.mean(dxn * xn, axis=-1, keepdims=True))
        vec_ref[0:1, :] += jnp.broadcast_to(loss, (1, D_MODEL))
        vec_ref[1:2, :] += jnp.sum(dy * xn, axis=0, keepdims=True)
        vec_ref[2:3, :] += jnp.sum(gres, axis=0, keepdims=True)
        gres_ref[...] = gres
        gb = gres.astype(MXU_DTYPE)
        dmix_ref[0] = _dot(gb, w_ref[0:ATTN_W, :], NT)
        dmix_ref[1] = _dot(gb, w_ref[ATTN_W:, :], NT)
        gw_ref[0:ATTN_W, :] += _dot(a, gb, TN)
        gw_ref[ATTN_W:, :] += _dot(s, gb, TN)

    tile = lambda w: pl.BlockSpec((tm, w), lambda i: (i, 0))
    return pl.pallas_call(
        body,
        name="out_proj_loss",
        grid=(SEQ // tm,),
        in_specs=[tile(ATTN_W), tile(SGU_W), tile(D_MODEL), tile(D_MODEL),
                  _full((D_MODEL, D_MODEL)), _full((1, D_MODEL)), _full((1, D_MODEL))],
        out_specs=(tile(D_MODEL), pl.BlockSpec((2, tm, ATTN_W), lambda i: (0, i, 0)), _full((D_MODEL, D_MODEL)),
                   _full((8, D_MODEL))),
        out_shape=(jax.ShapeDtypeStruct((SEQ, D_MODEL), F32),
                   jax.ShapeDtypeStruct((2, SEQ, ATTN_W), F32),
                   jax.ShapeDtypeStruct((D_MODEL, D_MODEL), F32),
                   jax.ShapeDtypeStruct((8, D_MODEL), F32)),
        compiler_params=_params(("arbitrary",), VMEM_LIMIT),
    )(ag, sg, x, target, wout, b_out, final_g)


def _mixers_bwd(sinks, dmix, q, kvx, out, gates, ln_g, ln_b, sgu_w, bias_full, gwout):
    last = N_BLOCKS - 1

    def body(sink_ref, d_ref, q_ref, kc_ref, o_ref, za_ref, dsg_ref, us_ref, vs_ref, zs_ref, lng_ref, lnb_ref, w_ref,
             bias_ref, gwout_ref,
             dp_ref, gsink_ref, gbin_ref, dps_ref, gw_ref, gb_ref, gln_ref, gbins_ref, wout_shard_ref,
             kp_ref, pend_ref, carry_ref, wm_ref, wt_ref, gbias_ref, sa_w, ra_w, sb_w, rc_w, send_sems, recv_sems):
        n = pl.program_id(0)
        start, exchange, finish = _reduce_scatter_plan(_Copies(send_sems, recv_sems), 0, gwout_ref, WOUT_ROWS,
                                                       sa_w, ra_w, sb_w, rc_w, wout_shard_ref)
        tril = (lax.broadcasted_iota(jnp.int32, (BLOCK, BLOCK), 0)
                >= lax.broadcasted_iota(jnp.int32, (BLOCK, BLOCK), 1))

        @pl.when(n == 0)
        def _():
            gsink_ref[...] = jnp.zeros_like(gsink_ref)
            gbin_ref[...] = jnp.zeros_like(gbin_ref)
            carry_ref[...] = jnp.zeros_like(carry_ref)
            kp_ref[...] = jnp.zeros_like(kp_ref)
            gw_ref[...] = jnp.zeros_like(gw_ref)
            gln_ref[...] = jnp.zeros_like(gln_ref)
            gbins_ref[...] = jnp.zeros_like(gbins_ref)
            gbias_ref[...] = jnp.zeros_like(gbias_ref)
            _mask_sgu_weights(w_ref, wm_ref, wt_ref)
            start()

        pl.when(n == 3)(exchange)
        pl.when(n == 12)(finish)

        @pl.when(n > 0)
        def _():
            dp_ref[:, 0:ATTN_W] = pend_ref[:, 0:ATTN_W]
            dp_ref[:, GATE0:ATTN_SECTION] = pend_ref[:, ATTN_W:]

        @pl.when(n > last)
        def _():
            dp_ref[:, KV0:GATE0] = carry_ref[...].astype(MXU_DTYPE)

        @pl.when(n <= last)
        def _():
            us = us_ref[...]
            vs = vs_ref[...]
            lng = lng_ref[...]
            u, vhat, rstd, vln = _sgu_activations(us, vs, lng, lnb_ref[...])
            low_sgu = lax.broadcasted_iota(jnp.int32, (BLOCK, LANES), 1) < HALF
            sgu = {}

            def sgu_gates():
                mixed = _sgu_mix(vln, wm_ref, bias_ref)
                sgu["du"], sgu["dzs"], sgu["dm"] = [], [], []
                for pair in range(N_SGU_HEADS // 2):
                    cols = slice(pair * LANES, (pair + 1) * LANES)
                    dsg = dsg_ref[:, cols]
                    gate, gate_grad = _silu_and_grad(zs_ref[:, cols])
                    up = u[:, cols]
                    sgu["du"].append(dsg * mixed[pair] * gate)
                    sgu["dzs"].append(dsg * up * mixed[pair] * gate_grad)
                    dmixed = dsg * up * gate
                    gbias_ref[:, cols] += dmixed
                    sgu["dm"].append((jnp.where(low_sgu, dmixed, 0.0).astype(MXU_DTYPE),
                                      jnp.where(low_sgu, 0.0, dmixed).astype(MXU_DTYPE)))

            def sgu_grads():
                dvln_parts = []
                for pair in range(N_SGU_HEADS // 2):
                    dm_lo, dm_hi = sgu["dm"][pair]
                    vp = vln[:, pair * LANES:(pair + 1) * LANES]
                    gw_ref[2 * pair] += _dot(dm_lo, vp, NT)
                    gw_ref[2 * pair + 1] += _dot(dm_hi, vp, NT)
                    dvln_parts.append(_dot(wt_ref[2 * pair], dm_lo) + _dot(wt_ref[2 * pair + 1], dm_hi))
                dvln = jnp.concatenate(dvln_parts, axis=1)
                gln_ref[0:1, :] += jnp.sum(dvln * vhat, axis=0, keepdims=True)
                gln_ref[1:2, :] += jnp.sum(dvln, axis=0, keepdims=True)
                dvhat = dvln * lng
                dvg = rstd * (dvhat - jnp.mean(dvhat, axis=-1, keepdims=True)
                              - vhat * jnp.mean(dvhat * vhat, axis=-1, keepdims=True))
                dus = jnp.concatenate(sgu["du"], axis=1) * _gelu_grad(us)
                dvs = dvg * _gelu_grad(vs)
                dzs = jnp.concatenate(sgu["dzs"], axis=1)
                for k, val in enumerate((dus, dvs, dzs)):
                    dps_ref[:, k * SGU_W:(k + 1) * SGU_W] = val.astype(MXU_DTYPE)
                    gbins_ref[:, k * SGU_W:(k + 1) * SGU_W] += jnp.sum(val, axis=0, keepdims=True)

            valid = _window_mask(n)[0:BLOCK]
            low = lax.broadcasted_iota(jnp.int32, (BLOCK, LANES), 1) < HALF
            low_keys = lax.broadcasted_iota(jnp.int32, (2 * BLOCK, LANES), 1) < HALF
            lane_row = lax.broadcasted_iota(jnp.int32, (1, LANES), 1)
            gsink = jnp.zeros((1, LANES), F32)
            chains = [(g, par, i) for g in range(2) for par in range(2) for i in range(2)]
            kv = {(g, par): _kv_cat(kp_ref, kc_ref, 2 * g + par, False) for g in range(2) for par in range(2)}
            ones_keys = jnp.ones((2 * BLOCK, LANES), MXU_DTYPE)
            half_of_lane = lax.broadcasted_iota(jnp.int32, (LANES, 2 * LANES), 0) // HALF
            half_of_col = lax.broadcasted_iota(jnp.int32, (LANES, 2 * LANES), 1) // LANES
            sum_halves = (half_of_lane == half_of_col).astype(MXU_DTYPE)
            douts, deltas = [], []
            for pair in range(N_PAIRS):
                lanes = slice(pair * LANES, (pair + 1) * LANES)
                dg = d_ref[:, lanes]
                gate, gate_grad = _silu_and_grad(za_ref[:, lanes])
                o = o_ref[pair]
                dout = dg * gate
                dza = dg * o * gate_grad
                douts.append(dout.astype(MXU_DTYPE))
                deltas.append(_dot(dout * o, sum_halves))
                zl = slice(ATTN_W + pair * LANES, ATTN_W + (pair + 1) * LANES)
                pend_ref[:, zl] = dza.astype(MXU_DTYPE)
                gl = slice(GATE0 + pair * LANES, GATE0 + (pair + 1) * LANES)
                gbin_ref[:, gl] += jnp.sum(dza, axis=0, keepdims=True)

            first = {}

            def issue_first(k):
                g, par, i = chains[k]
                first[k] = (_dot(q_ref[2 * g + i], kv[g, par][0], NT), _dot(douts[2 * g + i], kv[g, par][1], NT))

            numerators = {}

            def issue_row_sums(k):
                g, par, i = chains[k]
                sink = sink_ref[4 * g + 2 * i + par]
                e, m = _softmax_numerator(jnp.where(valid, first[k][0], NEG_INF), sink)
                numerators[k] = (e, jnp.exp(sink - m), _dot(e, ones_keys))

            ahead = ATTN_BWD_AHEAD
            for k in range(ahead):
                issue_first(k)
            issue_row_sums(0)
            issue_row_sums(1)
            dqs, dk_parts, dv_parts = {}, {}, {}
            operands = {}

            def issue_last(k):
                g, par, i = chains[k]
                ds, ds_t, p_t = operands.pop(k)
                dq = _dot(ds, kv[g, par][0])
                dqs[g, i] = dq if par == 0 else dqs[g, i] + dq
                dk = _dot(ds_t, q_ref[2 * g + i])
                dv = _dot(p_t, douts[2 * g + i])
                dk_parts[g, par] = dk if i == 0 else dk_parts[g, par] + dk
                dv_parts[g, par] = dv if i == 0 else dv_parts[g, par] + dv

            for k, (g, par, i) in enumerate(chains):
                h = 4 * g + 2 * i + par
                delta = deltas[2 * g + i][:, par * LANES:(par + 1) * LANES]
                e, at_sink, row_sum = numerators[k]
                inv = 1.0 / (row_sum + at_sink)
                p = e * jnp.tile(inv, (1, 2))
                ds = p * (first[k][1] - jnp.tile(delta, (1, 2)))
                ds = ds.astype(MXU_DTYPE)
                operands[k] = (ds, ds.T, p.astype(MXU_DTYPE).T)
                total = jnp.sum(at_sink * inv * delta, axis=0, keepdims=True)
                gsink = jnp.where(lane_row == h, -total, gsink)
                if k + ahead < len(chains):
                    issue_first(k + ahead)
                if k + 2 < len(chains):
                    issue_row_sums(k + 2)
                if k > 0:
                    issue_last(k - 1)
                if k == SGU_GATES_AFTER_CHAIN:
                    sgu_gates()
                if k == SGU_GRADS_AFTER_CHAIN:
                    sgu_grads()
            issue_last(len(chains) - 1)
            for pair in range(N_PAIRS):
                g, i = divmod(pair, 2)
                dq = dqs[g, i] * SCALE
                lanes = slice(pair * LANES, (pair + 1) * LANES)
                pend_ref[:, lanes] = dq.astype(MXU_DTYPE)
                gbin_ref[:, lanes] += jnp.sum(dq, axis=0, keepdims=True)
            gsink_ref[...] += gsink
            for k, parts in enumerate((dk_parts, dv_parts)):
                masked = {key: jnp.where(low_keys if key[1] == 0 else jnp.logical_not(low_keys), val, 0.0)
                          for key, val in parts.items()}
                both = (masked[0, 0] + masked[1, 1]
                        + pltpu.roll(masked[0, 1] + masked[1, 0], HALF, 1))
                lanes = slice(k * KV_W, (k + 1) * KV_W)
                done = carry_ref[:, lanes] + both[0:BLOCK]
                dp_ref[:, KV0 + k * KV_W:KV0 + (k + 1) * KV_W] = done.astype(MXU_DTYPE)
                carry_ref[:, lanes] = both[BLOCK:]
                gbin_ref[:, KV0 + k * KV_W:KV0 + (k + 1) * KV_W] += jnp.sum(both, axis=0, keepdims=True)
            kp_ref[...] = kc_ref[...]

        @pl.when(n == last)
        def _():
            for hh in range(N_SGU_HEADS):
                gw_ref[hh] = jnp.where(tril, gw_ref[hh], 0.0)
            head_of_lane = lax.broadcasted_iota(jnp.int32, (N_SGU_HEADS, SGU_W), 1) // HEAD_DIM
            select = (head_of_lane == lax.broadcasted_iota(jnp.int32, (N_SGU_HEADS, SGU_W), 0)).astype(F32)
            gb_ref[...] = lax.dot_general(select, gbias_ref[...], NT, precision=lax.Precision.HIGHEST,
                                          preferred_element_type=F32)

    at = lambda n: jnp.minimum(n, last)
    blk = lambda w: pl.BlockSpec((BLOCK, w), lambda n: (at(n), 0))
    tiles = pl.BlockSpec((N_PAIRS, BLOCK, LANES), lambda n: (0, at(n), 0))
    section = lambda k: pl.BlockSpec((None, BLOCK, SGU_W), lambda n: (k, at(n), 0))
    return pl.pallas_call(
        body,
        name="mixers_bwd",
        grid=(N_BLOCKS + 1,),
        in_specs=[pl.BlockSpec(memory_space=pltpu.SMEM),
                  section(0),
                  tiles,
                  blk(KVX_W),
                  tiles,
                  section(0),
                  section(1),
                  section(1), section(2), section(3),
                  _full((1, SGU_W)), _full((1, SGU_W)), _full((N_SGU_HEADS, BLOCK, BLOCK)), _full((BLOCK, SGU_W)),
                  VMEM_SPEC],
        out_specs=(pl.BlockSpec((BLOCK, ATTN_SECTION), lambda n: (jnp.maximum(n - 1, 0), 0)),
                   _full((1, LANES)), _full((1, ATTN_SECTION)),
                   pl.BlockSpec((BLOCK, SGU_SECTION), lambda n: (at(n), 0)),
                   _full((N_SGU_HEADS, BLOCK, BLOCK)), _full((N_SGU_HEADS, BLOCK)),
                   _full((8, SGU_W)), _full((1, SGU_SECTION)), VMEM_SPEC),
        out_shape=(jax.ShapeDtypeStruct((SEQ, ATTN_SECTION), MXU_DTYPE),
                   jax.ShapeDtypeStruct((1, LANES), F32),
                   jax.ShapeDtypeStruct((1, ATTN_SECTION), F32),
                   jax.ShapeDtypeStruct((SEQ, SGU_SECTION), MXU_DTYPE),
                   jax.ShapeDtypeStruct((N_SGU_HEADS, BLOCK, BLOCK), F32),
                   jax.ShapeDtypeStruct((N_SGU_HEADS, BLOCK), F32),
                   jax.ShapeDtypeStruct((8, SGU_W), F32),
                   jax.ShapeDtypeStruct((1, SGU_SECTION), F32),
                   jax.ShapeDtypeStruct((WOUT_ROWS, D_MODEL), F32)),
        scratch_shapes=([pltpu.VMEM((BLOCK, KVX_W), MXU_DTYPE),
                         pltpu.VMEM((BLOCK, 2 * ATTN_W), MXU_DTYPE), pltpu.VMEM((BLOCK, 2 * KV_W), F32),
                         pltpu.VMEM((N_SGU_HEADS, BLOCK, BLOCK), MXU_DTYPE),
                         pltpu.VMEM((N_SGU_HEADS, BLOCK, BLOCK), MXU_DTYPE), pltpu.VMEM((BLOCK, SGU_W), F32)]
                        + _reduce_scatter_scratch(WOUT_ROWS, D_MODEL, COMM_DTYPE) + _dma_sems(REDUCE_SEMS)),
        compiler_params=_params(("arbitrary",), VMEM_LIMIT),
    )(sinks, dmix, q, kvx, out, gates, dmix, gates, gates, gates, ln_g, ln_b, sgu_w, bias_full, gwout)


def _in_proj_bwd(dpa, dps, win_t, x, norm_g, gres, gwin, vec_parts):
    tm = TOKEN_TILE
    steps = SEQ // tm
    n_parts = len(vec_parts)

    def body(da_ref, ds_ref, w_ref, x_ref, g_ref, gres_ref, gwin_ref, *rest):
        part_refs = rest[:n_parts]
        gx_ref, shard_ref, vec_out_ref, gng_ref, sa, ra, sb, rc, vec_ref, ra_vec, slots, send_sems, recv_sems = (
            rest[n_parts:])
        step = pl.program_id(0)
        copies = _Copies(send_sems, recv_sems)
        start, exchange, finish = _reduce_scatter_plan(copies, 0, gwin_ref, WIN_ROWS, sa, ra, sb, rc, shard_ref)

        @pl.when(step == 0)
        def _():
            gng_ref[...] = jnp.zeros_like(gng_ref)
            start()

        pl.when(step == 2)(exchange)

        dh = _dot(da_ref[...], w_ref[0:ATTN_SECTION, :]) + _dot(ds_ref[...], w_ref[ATTN_SECTION:, :])
        xv = x_ref[...]
        r = lax.rsqrt(jnp.mean(xv * xv, axis=-1, keepdims=True) + NORM_EPS)
        xn = xv * r
        gng_ref[...] += jnp.sum(dh * xn, axis=0, keepdims=True)
        dxn = dh * g_ref[...]
        gx_ref[...] = r * (dxn - xn * jnp.mean(dxn * xn, axis=-1, keepdims=True)) + gres_ref[...]

        @pl.when(step == steps - 1)
        def _():
            finish()
            _all_reduce_vectors(copies, REDUCE_SEMS, gng_ref, *part_refs, vec_out_ref, vec_ref, ra_vec, slots)

    tile = lambda w: pl.BlockSpec((tm, w), lambda i: (i, 0))
    return pl.pallas_call(
        body,
        name="in_proj_bwd",
        grid=(steps,),
        in_specs=[tile(ATTN_SECTION), tile(SGU_SECTION), _full((IN_W, D_MODEL)), tile(D_MODEL),
                  _full((1, D_MODEL)), tile(D_MODEL), VMEM_SPEC] + [VMEM_SPEC] * n_parts,
        out_specs=(tile(D_MODEL), VMEM_SPEC, VMEM_SPEC),
        out_shape=(jax.ShapeDtypeStruct((SEQ, D_MODEL), F32),
                   jax.ShapeDtypeStruct((WIN_ROWS, D_MODEL), F32),
                   jax.ShapeDtypeStruct((VEC_ROWS, IN_W), F32)),
        scratch_shapes=([pltpu.VMEM((1, D_MODEL), F32)] + _reduce_scatter_scratch(WIN_ROWS, D_MODEL, COMM_DTYPE)
                        + _vector_scratch() + _dma_sems(REDUCE_SEMS + VECTOR_SEMS)),
        compiler_params=_params(("arbitrary",), VMEM_LIMIT),
    )(dpa, dps, win_t, x, norm_g, gres, gwin, *vec_parts)


def _win_grad(dpa, dps, h, gsguw):
    rows = 256
    n_attn = ATTN_SECTION // rows
    steps = n_attn + SGU_SECTION // rows

    def body(da_ref, ds_ref, h_ref, gsguw_ref, o_ref, sguw_full_ref, sa, ra, sb, rc, landing, send_sems, recv_sems):
        step = pl.program_id(0)
        copies = _Copies(send_sems, recv_sems)
        own_sguw = landing.at[_block_rows(_place(), SGUW_ROWS), :]
        start, exchange, finish = _reduce_scatter_plan(copies, 0, gsguw_ref, SGUW_ROWS, sa, ra, sb, rc, own_sguw)
        gather = _gather_plan(copies, REDUCE_SEMS, landing, SGUW_ROWS)

        pl.when(step == 0)(start)
        pl.when(step == 2)(exchange)

        @pl.when(step == 5)
        def _():
            finish()
            gather[0]()

        pl.when(step == 7)(gather[1])

        @pl.when(step < n_attn)
        def _():
            o_ref[...] = _dot(da_ref[...], h_ref[...], TN)

        @pl.when(step >= n_attn)
        def _():
            o_ref[...] = _dot(ds_ref[...], h_ref[...], TN)

        @pl.when(step == steps - 1)
        def _():
            gather[2]()
            sguw_full_ref[...] = landing[...]

    return pl.pallas_call(
        body,
        name="win_grad",
        grid=(steps,),
        in_specs=[pl.BlockSpec((SEQ, rows), lambda i: (0, jnp.minimum(i, n_attn - 1))),
                  pl.BlockSpec((SEQ, rows), lambda i: (0, jnp.maximum(i - n_attn, 0))),
                  _full((SEQ, D_MODEL)), VMEM_SPEC],
        out_specs=(pl.BlockSpec((rows, D_MODEL), lambda i: (i, 0)), _full((N_SGU_HEADS * BLOCK, BLOCK))),
        out_shape=(jax.ShapeDtypeStruct((IN_W, D_MODEL), F32),
                   jax.ShapeDtypeStruct((N_SGU_HEADS * BLOCK, BLOCK), F32)),
        scratch_shapes=(_reduce_scatter_scratch(SGUW_ROWS, BLOCK, F32)
                        + [pltpu.VMEM((N_SGU_HEADS * BLOCK, BLOCK), F32)]
                        + _dma_sems(REDUCE_SEMS + GATHER_SEMS)),
        compiler_params=_params(("arbitrary",), VMEM_LIMIT),
    )(dpa, dps, h, gsguw)


VEC_NORM_G, VEC_B_IN, VEC_SINKS, VEC_LN_G, VEC_LN_B, VEC_B_OUT, VEC_FINAL_G, VEC_LOSS, VEC_SGU_B = 0, 1, 2, 3, 4, 5, 6, 7, 8


def _adamw(w, g, m, v):
    m = ADAM_B1 * m + (1.0 - ADAM_B1) * g
    v = ADAM_B2 * v + (1.0 - ADAM_B2) * (g * g)
    m_hat = m / (1.0 - ADAM_B1 ** ADAM_STEP)
    v_hat = v / (1.0 - ADAM_B2 ** ADAM_STEP)
    delta = -ADAM_LR * (m_hat / (jnp.sqrt(v_hat) + ADAM_EPS) + ADAM_WD * w)
    return delta, m, v


def _adamw_shard(name, g, w, m, v, block_rows):
    def body(g_ref, w_ref, m_ref, v_ref, d_ref, nm_ref, nv_ref):
        d_ref[...], nm_ref[...], nv_ref[...] = _adamw(w_ref[...], g_ref[...], m_ref[...], v_ref[...])

    rows, cols = w.shape
    spec = pl.BlockSpec((block_rows, cols), lambda i: (i, 0))
    return pl.pallas_call(
        body,
        name=name,
        grid=(rows // block_rows,),
        in_specs=[spec] * 4,
        out_specs=(spec,) * 3,
        out_shape=(jax.ShapeDtypeStruct(w.shape, F32),) * 3,
        compiler_params=_params(("arbitrary",)),
    )(g, w, m, v)


VECTOR_SEMS = 4


def _vector_scratch():
    return [pltpu.VMEM((VEC_ROWS, IN_W), F32), pltpu.VMEM((VEC_ROWS, IN_W), F32),
            pltpu.VMEM((4 * VEC_ROWS, IN_W), F32)]


def _all_reduce_vectors(copies, sem0, gng_ref, gba_ref, gbs_ref, gsink_ref, gln_ref, gsgub_ref, vec4_ref, out_ref,
                        vec_ref, ra_vec, slots):
    x, y, c = _place()
    vec_ref[...] = jnp.zeros_like(vec_ref)
    vec_ref[VEC_NORM_G:VEC_NORM_G + 1, 0:D_MODEL] = gng_ref[...]
    vec_ref[VEC_B_IN:VEC_B_IN + 1, 0:ATTN_SECTION] = gba_ref[...]
    vec_ref[VEC_B_IN:VEC_B_IN + 1, ATTN_SECTION:IN_W] = gbs_ref[...]
    vec_ref[VEC_SINKS:VEC_SINKS + 1, 0:LANES] = gsink_ref[...]
    vec_ref[VEC_LN_G:VEC_LN_G + 1, 0:SGU_W] = gln_ref[0:1, :]
    vec_ref[VEC_LN_B:VEC_LN_B + 1, 0:SGU_W] = gln_ref[1:2, :]
    vec_ref[VEC_B_OUT:VEC_B_OUT + 1, 0:D_MODEL] = vec4_ref[2:3, :]
    vec_ref[VEC_FINAL_G:VEC_FINAL_G + 1, 0:D_MODEL] = vec4_ref[1:2, :]
    vec_ref[VEC_LOSS:VEC_LOSS + 1, 0:D_MODEL] = vec4_ref[0:1, :]
    vec_ref[VEC_SGU_B:VEC_SGU_B + N_SGU_HEADS, 0:BLOCK] = gsgub_ref[...]

    to_sibling = copies(sem0, vec_ref, ra_vec, (x, y, 1 - c))
    to_sibling.start()
    to_sibling.wait_recv()

    def chip_slot(place):
        return slots.at[pl.ds(pl.multiple_of((2 * place[0] + place[1]) * VEC_ROWS, 8), VEC_ROWS), :]

    mine = chip_slot((x, y))
    mine[...] = vec_ref[...] + ra_vec[...]
    to_chips = [copies(sem0 + i, mine, mine, (*_chip(rel), c)) for i, rel in enumerate(RELATIONS[1:], start=1)]
    for cp in to_chips:
        cp.start()
    for i, rel in enumerate(RELATIONS[1:], start=1):
        theirs = chip_slot(_chip(rel))
        copies(sem0 + i, theirs, theirs, (x, y, c)).wait_recv()
    out_ref[...] = ((slots[0:VEC_ROWS, :] + slots[VEC_ROWS:2 * VEC_ROWS, :])
                    + slots[2 * VEC_ROWS:3 * VEC_ROWS, :]) + slots[3 * VEC_ROWS:, :]
    to_sibling.wait_send()
    for cp in to_chips:
        cp.wait_send()


def _adamw_replicated(vec, gsguw, weights, m_state, v_state):
    n = len(SMALL)

    def body(*refs):
        vec_ref, gsguw_ref = refs[0], refs[1]
        w_refs, m_refs, v_refs = (refs[2 + k * n:2 + (k + 1) * n] for k in range(3))
        outs = refs[2 + 3 * n:]
        g_refs, d_refs, nm_refs, nv_refs = (outs[k * n:(k + 1) * n] for k in range(4))
        for i, (_, row, shape) in enumerate(SMALL):
            g = gsguw_ref[...] if row is None else vec_ref[row:row + shape[0], 0:shape[1]]
            g_refs[i][...] = g
            d_refs[i][...], nm_refs[i][...], nv_refs[i][...] = _adamw(
                w_refs[i][...], g, m_refs[i][...], v_refs[i][...])

    shapes = tuple(jax.ShapeDtypeStruct(shape, F32) for _, _, shape in SMALL)
    outs = pl.pallas_call(
        body,
        name="adamw_replicated",
        in_specs=[VMEM_SPEC] * (2 + 3 * n),
        out_specs=(VMEM_SPEC,) * (4 * n),
        out_shape=shapes * 4,
    )(vec, gsguw, *weights, *m_state, *v_state)
    return tuple(outs[k * n:(k + 1) * n] for k in range(4))


SMALL = (
    ("norm_g", VEC_NORM_G, (1, D_MODEL)),
    ("b_in", VEC_B_IN, (1, IN_W)),
    ("attn_sinks", VEC_SINKS, (1, N_Q_HEADS)),
    ("sgu_ln_g", VEC_LN_G, (1, SGU_W)),
    ("sgu_ln_b", VEC_LN_B, (1, SGU_W)),
    ("sgu_w", None, (N_SGU_HEADS * BLOCK, BLOCK)),
    ("sgu_b", VEC_SGU_B, (N_SGU_HEADS, BLOCK)),
    ("b_out", VEC_B_OUT, (1, D_MODEL)),
    ("final_norm_g", VEC_FINAL_G, (1, D_MODEL)),
)


def _local_grads(x, target, h, win_t, wout_shard, norm_g, b_in, attn_sinks, sgu_ln_g, sgu_ln_b, sgu_w, sgu_b, b_out,
                 final_g):
    sinks = attn_sinks.reshape(N_Q_HEADS)
    bias_full = jnp.repeat(sgu_b.T, HEAD_DIM, axis=1)
    q, kvx, gates, wout = _in_proj(h, b_in, win_t, wout_shard)
    out, ag, sg = _mixers_fwd(sinks, q, kvx, gates, sgu_ln_g, sgu_ln_b, sgu_w, bias_full)
    gres, dmix, gwout, vec4 = _out_proj_loss(ag, sg, x, target, wout, b_out, final_g)
    dpa, gsink, gbin_a, dps, gsguw, gsgub, gln, gbin_s, gwout_shard = _mixers_bwd(
        sinks, dmix, q, kvx, out, gates, sgu_ln_g, sgu_ln_b, sgu_w, bias_full, gwout)
    gwin, gsguw_sum = _win_grad(dpa, dps, h, gsguw.reshape(N_SGU_HEADS * BLOCK, BLOCK))
    grad_x, gwin_shard, vec = _in_proj_bwd(dpa, dps, win_t, x, norm_g, gres, gwin,
                                           (gbin_a, gbin_s, gsink, gln, gsgub, vec4))
    return grad_x, gwin_shard, gwout_shard, gsguw_sum, vec


def kernel(x, norm_g, w_in, b_in, attn_sinks, sgu_ln_g, sgu_ln_b, sgu_w, sgu_b, w_out, b_out, final_norm_g, loss_target, m_norm_g, m_w_in, m_b_in, m_attn_sinks, m_sgu_ln_g, m_sgu_ln_b, m_sgu_w, m_sgu_b, m_w_out, m_b_out, m_final_norm_g, v_norm_g, v_w_in, v_b_in, v_attn_sinks, v_sgu_ln_g, v_sgu_ln_b, v_sgu_w, v_sgu_b, v_w_out, v_b_out, v_final_norm_g):
    given = dict(norm_g=norm_g, b_in=b_in, attn_sinks=attn_sinks, sgu_ln_g=sgu_ln_g, sgu_ln_b=sgu_ln_b,
                 sgu_w=sgu_w, sgu_b=sgu_b, b_out=b_out, final_norm_g=final_norm_g)
    m_given = dict(norm_g=m_norm_g, b_in=m_b_in, attn_sinks=m_attn_sinks, sgu_ln_g=m_sgu_ln_g,
                   sgu_ln_b=m_sgu_ln_b, sgu_w=m_sgu_w, sgu_b=m_sgu_b, b_out=m_b_out, final_norm_g=m_final_norm_g)
    v_given = dict(norm_g=v_norm_g, b_in=v_b_in, attn_sinks=v_attn_sinks, sgu_ln_g=v_sgu_ln_g,
                   sgu_ln_b=v_sgu_ln_b, sgu_w=v_sgu_w, sgu_b=v_sgu_b, b_out=v_b_out, final_norm_g=v_final_norm_g)

    win_t, h = _all_gather_win(w_in[0].T, x[0], norm_g)
    grad_x, gwin_t, gwout, gsguw, vec = _local_grads(
        x[0], loss_target[0], h, win_t, w_out[0], norm_g, b_in, attn_sinks, sgu_ln_g, sgu_ln_b, sgu_w[0], sgu_b[0],
        b_out, final_norm_g.reshape(1, D_MODEL))

    t = lambda a: a[0].T
    d_win, nm_win, nv_win = _adamw_shard("adamw_w_in", gwin_t, t(w_in), t(m_w_in), t(v_w_in), WIN_ROWS // 2)
    d_wout, nm_wout, nv_wout = _adamw_shard("adamw_w_out", gwout, w_out[0], m_w_out[0], v_w_out[0], WOUT_ROWS)
    as_2d = lambda d: [d[name].reshape(shape) for name, _, shape in SMALL]
    loss = vec[VEC_LOSS, 0]
    small = _adamw_replicated(vec, gsguw, as_2d(given), as_2d(m_given), as_2d(v_given))

    def assemble(big_in, big_out, k):
        vals = {name: small[k][i].reshape(given[name].shape) for i, (name, _, _) in enumerate(SMALL)}
        vals["w_in"] = big_in.T[None]
        vals["w_out"] = big_out[None]
        order = ("norm_g", "w_in", "b_in", "attn_sinks", "sgu_ln_g", "sgu_ln_b", "sgu_w", "sgu_b", "w_out",
                 "b_out", "final_norm_g")
        return [vals[name] for name in order]

    return (loss, grad_x[None],
            *assemble(gwin_t, gwout, 0), *assemble(d_win, d_wout, 1),
            *assemble(nm_win, nm_wout, 2), *assemble(nv_win, nv_wout, 3))
```

```python
import functools
import math

import jax
import jax.numpy as jnp
from jax import lax
from jax.experimental import pallas as pl
from jax.experimental.pallas import tpu as pltpu

F32 = jnp.float32
BF16 = jnp.bfloat16
MXU_DTYPE = BF16
COMM_DTYPE = BF16

D_MODEL = 1024
SEQ = 4096
HEAD_DIM = 64
N_Q_HEADS = 8
Q_PER_KV = 4
BLOCK = 128
N_BLOCKS = SEQ // BLOCK
ATTN_W = 512
KV_W = 128
SGU_W = 512
N_SGU_HEADS = 8
IN_W = 2816
NORM_EPS = 1e-5
NEG_INF = -1e30
SCALE = HEAD_DIM ** -0.5
KV0 = ATTN_W
GATE0 = ATTN_W + 2 * KV_W
SGU0 = GATE0 + ATTN_W
ATTN_SECTION = SGU0
SGU_SECTION = IN_W - SGU0

ADAM_LR = 0.001
ADAM_B1 = 0.9
ADAM_B2 = 0.999
ADAM_EPS = 1e-08
ADAM_WD = 0.01
ADAM_STEP = 10

N_DEV = 8
WIN_ROWS = IN_W // N_DEV
WOUT_ROWS = D_MODEL // N_DEV
SGUW_ROWS = N_SGU_HEADS * BLOCK // N_DEV
VEC_ROWS = 16
MESH = pl.DeviceIdType.MESH

LANES = 128
HALF = LANES // 2
N_PAIRS = N_Q_HEADS * HEAD_DIM // LANES
KVX_W = 12 * LANES
TOKEN_TILE = 256
FWD_TOKEN_TILE = 512
ATTN_FWD_AHEAD = 4
FWD_BLOCKS_PER_STEP = 4
SGU_MIX_AFTER_CHAIN = 0
SGU_GATES_AFTER_CHAIN = 1
SGU_GRADS_AFTER_CHAIN = 5
ATTN_BWD_AHEAD = 3
VMEM_LIMIT = 56 * 1024 * 1024

NN = (((1,), (0,)), ((), ()))
NT = (((1,), (1,)), ((), ()))
TN = (((0,), (0,)), ((), ()))


def _dot(a, b, dims=NN):
    return lax.dot_general(a.astype(MXU_DTYPE), b.astype(MXU_DTYPE), dims, preferred_element_type=F32)


def _gelu(x):
    return x * (lax.erf(x * (1.0 / math.sqrt(2.0))) + 1.0) * 0.5


def _gelu_grad(x):
    cdf = (lax.erf(x * (1.0 / math.sqrt(2.0))) + 1.0) * 0.5
    return cdf + x * jnp.exp(-0.5 * x * x) * (1.0 / math.sqrt(2.0 * math.pi))


def _silu_and_grad(z):
    s = jax.nn.sigmoid(z)
    return z * s, s * (1.0 + z * (1.0 - s))


def _params(semantics=None, vmem=None):
    kw = {}
    if semantics is not None:
        kw["dimension_semantics"] = semantics
    if vmem is not None:
        kw["vmem_limit_bytes"] = vmem
    return pltpu.CompilerParams(**kw)


def _full(shape):
    return pl.BlockSpec(shape, lambda *_: (0,) * len(shape))


VMEM_SPEC = pl.BlockSpec(memory_space=pltpu.VMEM)


RELATIONS = ((0, 0), (1, 0), (0, 1), (1, 1))


def _place():
    return lax.axis_index("x"), lax.axis_index("y"), lax.axis_index("c")


def _chip(rel):
    x, y, _ = _place()
    return (1 - x if rel[0] else x, 1 - y if rel[1] else y)


def _block_rows(place, n_rows):
    px, py, pc = place
    return pl.ds(pl.multiple_of((4 * px + 2 * py + pc) * n_rows, 16), n_rows)


class _Copies:
    def __init__(self, send_sems, recv_sems):
        self.send_sems, self.recv_sems = send_sems, recv_sems

    def __call__(self, k, src, dst, to):
        return pltpu.make_async_remote_copy(src_ref=src, dst_ref=dst, send_sem=self.send_sems.at[k],
                                            recv_sem=self.recv_sems.at[k], device_id=to, device_id_type=MESH)


def _gather_plan(copies, sem0, full_ref, n_rows):
    x, y, c = _place()
    me, sibling = (x, y, c), (x, y, 1 - c)
    chips = [_chip(rel) for rel in RELATIONS[1:]]

    def cp(k, block, to):
        rows = full_ref.at[_block_rows(block, n_rows), :]
        return copies(sem0 + k, rows, rows, to)

    first = [cp(0, me, sibling)] + [cp(1 + j, me, (*chip, c)) for j, chip in enumerate(chips)]
    passed = [cp(4 + j, (*chip, c), sibling) for j, chip in enumerate(chips)]

    def start():
        for f in first:
            f.start()

    def forward():
        for j, chip in enumerate(chips):
            cp(1 + j, (*chip, c), me).wait_recv()
            passed[j].start()

    def finish():
        cp(0, sibling, me).wait_recv()
        for j, chip in enumerate(chips):
            cp(4 + j, (*chip, 1 - c), me).wait_recv()
        for f in first + passed:
            f.wait_send()

    return start, forward, finish


GATHER_SEMS = 7


def _reduce_scatter_plan(copies, sem0, part_ref, n_rows, sa, ra, sb, rc, res_ref):
    x, y, c = _place()
    sibling = (x, y, 1 - c)
    n = n_rows
    level1 = copies(sem0, sa, ra, sibling)

    def level2(i):
        slot = pl.ds((i - 1) * n, n)
        return copies(sem0 + i, sb.at[slot, :], rc.at[slot, :], (*_chip(RELATIONS[i]), c))

    def start():
        for i, rel in enumerate(RELATIONS):
            sa[i * n:(i + 1) * n, :] = part_ref[_block_rows((*_chip(rel), 1 - c), n), :].astype(sa.dtype)
        level1.start()

    def exchange():
        level1.wait_recv()
        for i, rel in enumerate(RELATIONS):
            total = part_ref[_block_rows((*_chip(rel), c), n), :] + ra[i * n:(i + 1) * n, :].astype(F32)
            if i == 0:
                res_ref[...] = total
            else:
                sb[(i - 1) * n:i * n, :] = total.astype(sb.dtype)
                level2(i).start()

    def finish():
        acc = res_ref[...]
        for i in range(1, len(RELATIONS)):
            level2(i).wait_recv()
            acc = acc + rc[(i - 1) * n:i * n, :].astype(F32)
        res_ref[...] = acc
        level1.wait_send()
        for i in range(1, len(RELATIONS)):
            level2(i).wait_send()

    return start, exchange, finish


REDUCE_SEMS = 4


def _reduce_scatter_scratch(n_rows, width, dtype):
    return [pltpu.VMEM((4 * n_rows, width), dtype), pltpu.VMEM((4 * n_rows, width), dtype),
            pltpu.VMEM((3 * n_rows, width), dtype), pltpu.VMEM((3 * n_rows, width), dtype)]


def _dma_sems(n):
    return [pltpu.SemaphoreType.DMA((n,)), pltpu.SemaphoreType.DMA((n,))]


def _all_gather_win(win_t_shard, x, norm_g):
    tm = FWD_TOKEN_TILE
    steps = SEQ // tm

    def body(win_ref, x_ref, g_ref, full_ref, h_ref, landing, send_sems, recv_sems):
        step = pl.program_id(0)
        start, forward, finish = _gather_plan(_Copies(send_sems, recv_sems), 0, landing, WIN_ROWS)

        @pl.when(step == 0)
        def _():
            landing[_block_rows(_place(), WIN_ROWS), :] = win_ref[...].astype(COMM_DTYPE)
            start()

        xv = x_ref[...]
        r = lax.rsqrt(jnp.mean(xv * xv, axis=-1, keepdims=True) + NORM_EPS)
        h_ref[...] = ((xv * r) * g_ref[...]).astype(MXU_DTYPE)

        @pl.when(step == steps - 1)
        def _():
            forward()
            finish()
            full_ref[...] = landing[...]

    return pl.pallas_call(
        body,
        name="all_gather_win",
        grid=(steps,),
        in_specs=[VMEM_SPEC, pl.BlockSpec((tm, D_MODEL), lambda i: (i, 0)), _full((1, D_MODEL))],
        out_specs=(_full((IN_W, D_MODEL)), pl.BlockSpec((tm, D_MODEL), lambda i: (i, 0))),
        out_shape=(jax.ShapeDtypeStruct((IN_W, D_MODEL), COMM_DTYPE),
                   jax.ShapeDtypeStruct((SEQ, D_MODEL), MXU_DTYPE)),
        scratch_shapes=[pltpu.VMEM((IN_W, D_MODEL), COMM_DTYPE)] + _dma_sems(GATHER_SEMS),
        compiler_params=_params(("arbitrary",), VMEM_LIMIT),
    )(win_t_shard, x, norm_g)


def _in_proj(h, b_in, win_t, wout_shard):
    tm = FWD_TOKEN_TILE
    steps = SEQ // tm

    def body(h_ref, b_ref, w_ref, wout_ref, q_ref, kvx_ref, gate_ref, wfull_ref, landing, send_sems, recv_sems):
        step = pl.program_id(0)
        start, forward, finish = _gather_plan(_Copies(send_sems, recv_sems), 0, landing, WOUT_ROWS)

        @pl.when(step == 0)
        def _():
            landing[_block_rows(_place(), WOUT_ROWS), :] = wout_ref[...].astype(COMM_DTYPE)
            start()

        pl.when(step == steps // 2)(forward)

        h = h_ref[...]

        def proj(lo, hi):
            return _dot(h, w_ref[lo:hi, :], NT) + b_ref[:, lo:hi]

        qs = proj(0, ATTN_W) * SCALE
        for pair in range(N_PAIRS):
            q_ref[pair] = qs[:, pair * LANES:(pair + 1) * LANES].astype(MXU_DTYPE)
        kv = proj(KV0, GATE0)
        low = lax.broadcasted_iota(jnp.int32, (tm, LANES), 1) < HALF
        for i in range(2):
            t = kv[:, i * LANES:(i + 1) * LANES]
            rot = pltpu.roll(t, HALF, 1)
            variants = (jnp.where(low, t, 0.0), jnp.where(low, 0.0, rot),
                        jnp.where(low, rot, 0.0), jnp.where(low, 0.0, t))
            for j, val in enumerate(variants):
                col = (4 * i + j) * LANES
                kvx_ref[:, col:col + LANES] = val.astype(MXU_DTYPE)
                if i == 1:
                    ones_elsewhere = jnp.where(low == (j % 2 == 0), val, 1.0)
                    kvx_ref[:, col + 4 * LANES:col + 5 * LANES] = ones_elsewhere.astype(MXU_DTYPE)
        for k in range(4):
            gate_ref[k] = proj(GATE0 + k * SGU_W, GATE0 + (k + 1) * SGU_W)

        @pl.when(step == steps - 1)
        def _():
            finish()
            wfull_ref[...] = landing[...]

    return pl.pallas_call(
        body,
        name="in_proj",
        grid=(steps,),
        in_specs=[pl.BlockSpec((tm, D_MODEL), lambda i: (i, 0)),
                  _full((1, IN_W)), _full((IN_W, D_MODEL)), VMEM_SPEC],
        out_specs=(pl.BlockSpec((N_PAIRS, tm, LANES), lambda i: (0, i, 0)),
                   pl.BlockSpec((tm, KVX_W), lambda i: (i, 0)),
                   pl.BlockSpec((4, tm, SGU_W), lambda i: (0, i, 0)),
                   _full((D_MODEL, D_MODEL))),
        out_shape=(jax.ShapeDtypeStruct((N_PAIRS, SEQ, LANES), MXU_DTYPE),
                   jax.ShapeDtypeStruct((SEQ, KVX_W), MXU_DTYPE),
                   jax.ShapeDtypeStruct((4, SEQ, SGU_W), F32),
                   jax.ShapeDtypeStruct((D_MODEL, D_MODEL), COMM_DTYPE)),
        scratch_shapes=[pltpu.VMEM((D_MODEL, D_MODEL), COMM_DTYPE)] + _dma_sems(GATHER_SEMS),
        compiler_params=_params(("arbitrary",), VMEM_LIMIT),
    )(h, b_in, win_t, wout_shard)


def _window_mask(n):
    qi = lax.broadcasted_iota(jnp.int32, (2 * BLOCK, 2 * BLOCK), 0) & (BLOCK - 1)
    p = lax.broadcasted_iota(jnp.int32, (2 * BLOCK, 2 * BLOCK), 1) - BLOCK
    in_window = jnp.logical_and(p <= qi, p > qi - BLOCK)
    return jnp.logical_and(in_window, jnp.logical_or(p >= 0, n > 0))


def _sink_column(sink_ref, g, par):
    return jnp.concatenate([jnp.full((BLOCK, 1), sink_ref[4 * g + par], F32),
                            jnp.full((BLOCK, 1), sink_ref[4 * g + 2 + par], F32)], axis=0)


def _kv_cat(kp_ref, kc_ref, var, with_ones):
    kcol, vcol = var * LANES, (var + (8 if with_ones else 4)) * LANES
    return (jnp.concatenate([kp_ref[:, kcol:kcol + LANES], kc_ref[:, kcol:kcol + LANES]], axis=0),
            jnp.concatenate([kp_ref[:, vcol:vcol + LANES], kc_ref[:, vcol:vcol + LANES]], axis=0))


def _softmax_numerator(s, sink):
    m = jnp.maximum(jnp.max(s, axis=1, keepdims=True), sink)
    return jnp.exp(s - m), m


def _mixers_fwd(sinks, q, kvx, gates, ln_g, ln_b, sgu_w, bias_full):
    per_step = FWD_BLOCKS_PER_STEP
    rows_per_step = per_step * BLOCK

    def body(sink_ref, q_ref, kc_ref, za_ref, us_ref, vs_ref, zs_ref, lng_ref, lnb_ref, w_ref, bias_ref,
             out_ref, ag_ref, sg_ref, kp_ref, wm_ref):
        @pl.when(pl.program_id(0) == 0)
        def _():
            kp_ref[...] = jnp.zeros_like(kp_ref)
            _mask_sgu_weights(w_ref, wm_ref)

        def one_block(b, carry):
            rows = pl.ds(pl.multiple_of(b * BLOCK, BLOCK), BLOCK)
            kc = kc_ref.at[rows, :]
            u, _, _, vln = _sgu_activations(us_ref[rows, :], vs_ref[rows, :], lng_ref[...], lnb_ref[...])

            valid = _window_mask(pl.program_id(0) * per_step + b)[0:BLOCK]
            chains = [(g, par, i) for g in range(2) for par in range(2) for i in range(2)]
            kv = {(g, par): _kv_cat(kp_ref, kc, 2 * g + par, True) for g in range(2) for par in range(2)}
            scores, outs = {}, {}

            def issue_scores(k):
                g, par, i = chains[k]
                scores[k] = _dot(q_ref[2 * g + i, rows, :], kv[g, par][0], NT)

            ahead = ATTN_FWD_AHEAD
            for k in range(ahead):
                issue_scores(k)
            low = lax.broadcasted_iota(jnp.int32, (BLOCK, LANES), 1) < HALF
            for k, (g, par, i) in enumerate(chains):
                sink = sink_ref[4 * g + 2 * i + par]
                e, m = _softmax_numerator(jnp.where(valid, scores[k], NEG_INF), sink)
                if k + ahead < len(chains):
                    issue_scores(k + ahead)
                o = _dot(e, kv[g, par][1])
                outs[g, par, i] = o / (pltpu.roll(o, HALF, 1) + jnp.exp(sink - m))
                if k == SGU_MIX_AFTER_CHAIN:
                    mixed = _sgu_mix(vln, wm_ref, bias_ref)
            for pair in range(N_PAIRS):
                g, i = divmod(pair, 2)
                lanes = slice(pair * LANES, (pair + 1) * LANES)
                o = jnp.where(low, outs[g, 0, i], outs[g, 1, i])
                out_ref[pair, rows, :] = o
                gate, _ = _silu_and_grad(za_ref[rows, lanes])
                ag_ref[rows, lanes] = (o * gate).astype(MXU_DTYPE)
            kp_ref[...] = kc[...]
            for pair in range(N_SGU_HEADS // 2):
                cols = slice(pair * LANES, (pair + 1) * LANES)
                gate, _ = _silu_and_grad(zs_ref[rows, cols])
                sg_ref[rows, cols] = (u[:, cols] * mixed[pair] * gate).astype(MXU_DTYPE)
            return carry

        lax.fori_loop(0, per_step, one_block, 0)

    blk = lambda w: pl.BlockSpec((rows_per_step, w), lambda n: (n, 0))
    tiles = pl.BlockSpec((N_PAIRS, rows_per_step, LANES), lambda n: (0, n, 0))
    gate = lambda k: pl.BlockSpec((None, rows_per_step, SGU_W), lambda n: (k, n, 0))
    return pl.pallas_call(
        body,
        name="mixers_fwd",
        grid=(N_BLOCKS // per_step,),
        in_specs=[pl.BlockSpec(memory_space=pltpu.SMEM), tiles, blk(KVX_W), gate(0), gate(1), gate(2), gate(3),
                  _full((1, SGU_W)), _full((1, SGU_W)), _full((N_SGU_HEADS, BLOCK, BLOCK)), _full((BLOCK, SGU_W))],
        out_specs=(tiles, blk(ATTN_W), blk(SGU_W)),
        out_shape=(jax.ShapeDtypeStruct((N_PAIRS, SEQ, LANES), F32),
                   jax.ShapeDtypeStruct((SEQ, ATTN_W), MXU_DTYPE),
                   jax.ShapeDtypeStruct((SEQ, SGU_W), MXU_DTYPE)),
        scratch_shapes=[pltpu.VMEM((BLOCK, KVX_W), MXU_DTYPE), pltpu.VMEM((N_SGU_HEADS, BLOCK, BLOCK), MXU_DTYPE)],
        compiler_params=_params(("arbitrary",)),
    )(sinks, q, kvx, gates, gates, gates, gates, ln_g, ln_b, sgu_w, bias_full)


def _sgu_activations(us, vs, lng, lnb):
    u = _gelu(us)
    vg = _gelu(vs)
    mu = jnp.mean(vg, axis=-1, keepdims=True)
    xc = vg - mu
    rstd = lax.rsqrt(jnp.mean(xc * xc, axis=-1, keepdims=True) + NORM_EPS)
    vhat = xc * rstd
    return u, vhat, rstd, vhat * lng + lnb


def _mask_sgu_weights(w_ref, masked_ref, transposed_ref=None):
    tril = (lax.broadcasted_iota(jnp.int32, (BLOCK, BLOCK), 0)
            >= lax.broadcasted_iota(jnp.int32, (BLOCK, BLOCK), 1))
    for hh in range(N_SGU_HEADS):
        w = jnp.where(tril, w_ref[hh], 0.0)
        masked_ref[hh] = w.astype(MXU_DTYPE)
        if transposed_ref is not None:
            transposed_ref[hh] = w.T.astype(MXU_DTYPE)


def _sgu_mix(vln, masked_w_ref, bias_ref):
    low = lax.broadcasted_iota(jnp.int32, (BLOCK, LANES), 1) < HALF
    mixed = []
    for pair in range(N_SGU_HEADS // 2):
        vp = vln[:, pair * LANES:(pair + 1) * LANES]
        mixed.append(_dot(masked_w_ref[2 * pair], jnp.where(low, vp, 0.0))
                     + _dot(masked_w_ref[2 * pair + 1], jnp.where(low, 0.0, vp))
                     + bias_ref[:, pair * LANES:(pair + 1) * LANES])
    return mixed


def _out_proj_loss(ag, sg, x, target, wout, b_out, final_g):
    tm = FWD_TOKEN_TILE

    def body(ag_ref, sg_ref, x_ref, t_ref, w_ref, b_ref, gf_ref, gres_ref, dmix_ref, gw_ref, vec_ref, gb_ref):
        @pl.when(pl.program_id(0) == 0)
        def _():
            gw_ref[...] = jnp.zeros_like(gw_ref)
            vec_ref[...] = jnp.zeros_like(vec_ref)

        gf = gf_ref[...]
        halves = [slice(k * tm // 2, (k + 1) * tm // 2) for k in range(2)]
        projected = [_dot(ag_ref[rows, :], w_ref[0:ATTN_W, :]) + _dot(sg_ref[rows, :], w_ref[ATTN_W:, :])
                     for rows in halves]
        for rows, mm in zip(halves, projected):
            xo = x_ref[rows, :] + mm + b_ref[...]
            r = lax.rsqrt(jnp.mean(xo * xo, axis=-1, keepdims=True) + NORM_EPS)
            xn = xo * r
            err = xn * gf - t_ref[rows, :]
            loss = 0.5 * jnp.sum(jnp.mean(err * err, axis=-1, keepdims=True), axis=0, keepdims=True)
            dy = err * (1.0 / D_MODEL)
            dxn = dy * gf
            gres = r * (dxn - xn * jnp.mean(dxn * xn, axis=-1, keepdims=True))
            vec_ref[0:1, :] += jnp.broadcast_to(loss, (1, D_MODEL))
            vec_ref[1:2, :] += jnp.sum(dy * xn, axis=0, keepdims=True)
            vec_ref[2:3, :] += jnp.sum(gres, axis=0, keepdims=True)
            gres_ref[rows, :] = gres
            gb = gres.astype(MXU_DTYPE)
            gb_ref[rows, :] = gb
            dmix_ref[0, rows, :] = _dot(gb, w_ref[0:ATTN_W, :], NT)
            dmix_ref[1, rows, :] = _dot(gb, w_ref[ATTN_W:, :], NT)
        gw_ref[0:ATTN_W, :] += _dot(ag_ref[...], gb_ref[...], TN)
        gw_ref[ATTN_W:, :] += _dot(sg_ref[...], gb_ref[...], TN)

    tile = lambda w: pl.BlockSpec((tm, w), lambda i: (i, 0))
    return pl.pallas_call(
        body,
        name="out_proj_loss",
        grid=(SEQ // tm,),
        in_specs=[tile(ATTN_W), tile(SGU_W), tile(D_MODEL), tile(D_MODEL),
                  _full((D_MODEL, D_MODEL)), _full((1, D_MODEL)), _full((1, D_MODEL))],
        out_specs=(tile(D_MODEL), pl.BlockSpec((2, tm, ATTN_W), lambda i: (0, i, 0)), _full((D_MODEL, D_MODEL)),
                   _full((8, D_MODEL))),
        out_shape=(jax.ShapeDtypeStruct((SEQ, D_MODEL), F32),
                   jax.ShapeDtypeStruct((2, SEQ, ATTN_W), F32),
                   jax.ShapeDtypeStruct((D_MODEL, D_MODEL), F32),
                   jax.ShapeDtypeStruct((8, D_MODEL), F32)),
        scratch_shapes=[pltpu.VMEM((tm, D_MODEL), MXU_DTYPE)],
        compiler_params=_params(("arbitrary",), VMEM_LIMIT),
    )(ag, sg, x, target, wout, b_out, final_g)


def _mixers_bwd(sinks, dmix, q, kvx, out, gates, ln_g, ln_b, sgu_w, bias_full, gwout):
    last = N_BLOCKS - 1

    def body(sink_ref, d_ref, q_ref, kc_ref, o_ref, za_ref, dsg_ref, us_ref, vs_ref, zs_ref, lng_ref, lnb_ref, w_ref,
             bias_ref, gwout_ref,
             dp_ref, gsink_ref, gbin_ref, dps_ref, gw_ref, gb_ref, gln_ref, gbins_ref, wout_shard_ref,
             kp_ref, pend_ref, carry_ref, wm_ref, wt_ref, gbias_ref, sa_w, ra_w, sb_w, rc_w, send_sems, recv_sems):
        n = pl.program_id(0)
        start, exchange, finish = _reduce_scatter_plan(_Copies(send_sems, recv_sems), 0, gwout_ref, WOUT_ROWS,
                                                       sa_w, ra_w, sb_w, rc_w, wout_shard_ref)
        tril = (lax.broadcasted_iota(jnp.int32, (BLOCK, BLOCK), 0)
                >= lax.broadcasted_iota(jnp.int32, (BLOCK, BLOCK), 1))

        @pl.when(n == 0)
        def _():
            gsink_ref[...] = jnp.zeros_like(gsink_ref)
            gbin_ref[...] = jnp.zeros_like(gbin_ref)
            carry_ref[...] = jnp.zeros_like(carry_ref)
            kp_ref[...] = jnp.zeros_like(kp_ref)
            gw_ref[...] = jnp.zeros_like(gw_ref)
            gln_ref[...] = jnp.zeros_like(gln_ref)
            gbins_ref[...] = jnp.zeros_like(gbins_ref)
            gbias_ref[...] = jnp.zeros_like(gbias_ref)
            _mask_sgu_weights(w_ref, wm_ref, wt_ref)
            start()

        pl.when(n == 3)(exchange)
        pl.when(n == 12)(finish)

        @pl.when(n > 0)
        def _():
            dp_ref[:, 0:ATTN_W] = pend_ref[:, 0:ATTN_W]
            dp_ref[:, GATE0:ATTN_SECTION] = pend_ref[:, ATTN_W:]

        @pl.when(n > last)
        def _():
            dp_ref[:, KV0:GATE0] = carry_ref[...].astype(MXU_DTYPE)

        @pl.when(n <= last)
        def _():
            us = us_ref[...]
            vs = vs_ref[...]
            lng = lng_ref[...]
            u, vhat, rstd, vln = _sgu_activations(us, vs, lng, lnb_ref[...])
            low_sgu = lax.broadcasted_iota(jnp.int32, (BLOCK, LANES), 1) < HALF
            sgu = {}

            def sgu_gates():
                mixed = _sgu_mix(vln, wm_ref, bias_ref)
                sgu["du"], sgu["dzs"], sgu["dm"] = [], [], []
                for pair in range(N_SGU_HEADS // 2):
                    cols = slice(pair * LANES, (pair + 1) * LANES)
                    dsg = dsg_ref[:, cols]
                    gate, gate_grad = _silu_and_grad(zs_ref[:, cols])
                    up = u[:, cols]
                    sgu["du"].append(dsg * mixed[pair] * gate)
                    sgu["dzs"].append(dsg * up * mixed[pair] * gate_grad)
                    dmixed = dsg * up * gate
                    gbias_ref[:, cols] += dmixed
                    sgu["dm"].append((jnp.where(low_sgu, dmixed, 0.0).astype(MXU_DTYPE),
                                      jnp.where(low_sgu, 0.0, dmixed).astype(MXU_DTYPE)))

            def sgu_grads():
                dvln_parts = []
                for pair in range(N_SGU_HEADS // 2):
                    dm_lo, dm_hi = sgu["dm"][pair]
                    vp = vln[:, pair * LANES:(pair + 1) * LANES]
                    gw_ref[2 * pair] += _dot(dm_lo, vp, NT)
                    gw_ref[2 * pair + 1] += _dot(dm_hi, vp, NT)
                    dvln_parts.append(_dot(wt_ref[2 * pair], dm_lo) + _dot(wt_ref[2 * pair + 1], dm_hi))
                dvln = jnp.concatenate(dvln_parts, axis=1)
                gln_ref[0:1, :] += jnp.sum(dvln * vhat, axis=0, keepdims=True)
                gln_ref[1:2, :] += jnp.sum(dvln, axis=0, keepdims=True)
                dvhat = dvln * lng
                dvg = rstd * (dvhat - jnp.mean(dvhat, axis=-1, keepdims=True)
                              - vhat * jnp.mean(dvhat * vhat, axis=-1, keepdims=True))
                dus = jnp.concatenate(sgu["du"], axis=1) * _gelu_grad(us)
                dvs = dvg * _gelu_grad(vs)
                dzs = jnp.concatenate(sgu["dzs"], axis=1)
                for k, val in enumerate((dus, dvs, dzs)):
                    dps_ref[:, k * SGU_W:(k + 1) * SGU_W] = val.astype(MXU_DTYPE)
                    gbins_ref[:, k * SGU_W:(k + 1) * SGU_W] += jnp.sum(val, axis=0, keepdims=True)

            valid = _window_mask(n)[0:BLOCK]
            low = lax.broadcasted_iota(jnp.int32, (BLOCK, LANES), 1) < HALF
            low_keys = lax.broadcasted_iota(jnp.int32, (2 * BLOCK, LANES), 1) < HALF
            lane_row = lax.broadcasted_iota(jnp.int32, (1, LANES), 1)
            gsink = jnp.zeros((1, LANES), F32)
            chains = [(g, par, i) for g in range(2) for par in range(2) for i in range(2)]
            kv = {(g, par): _kv_cat(kp_ref, kc_ref, 2 * g + par, False) for g in range(2) for par in range(2)}
            ones_keys = jnp.ones((2 * BLOCK, LANES), MXU_DTYPE)
            half_of_lane = lax.broadcasted_iota(jnp.int32, (LANES, 2 * LANES), 0) // HALF
            half_of_col = lax.broadcasted_iota(jnp.int32, (LANES, 2 * LANES), 1) // LANES
            sum_halves = (half_of_lane == half_of_col).astype(MXU_DTYPE)
            douts, deltas = [], []
            for pair in range(N_PAIRS):
                lanes = slice(pair * LANES, (pair + 1) * LANES)
                dg = d_ref[:, lanes]
                gate, gate_grad = _silu_and_grad(za_ref[:, lanes])
                o = o_ref[pair]
                dout = dg * gate
                dza = dg * o * gate_grad
                douts.append(dout.astype(MXU_DTYPE))
                deltas.append(_dot(dout * o, sum_halves))
                zl = slice(ATTN_W + pair * LANES, ATTN_W + (pair + 1) * LANES)
                pend_ref[:, zl] = dza.astype(MXU_DTYPE)
                gl = slice(GATE0 + pair * LANES, GATE0 + (pair + 1) * LANES)
                gbin_ref[:, gl] += jnp.sum(dza, axis=0, keepdims=True)

            first = {}

            def issue_first(k):
                g, par, i = chains[k]
                first[k] = (_dot(q_ref[2 * g + i], kv[g, par][0], NT), _dot(douts[2 * g + i], kv[g, par][1], NT))

            numerators = {}

            def issue_row_sums(k):
                g, par, i = chains[k]
                sink = sink_ref[4 * g + 2 * i + par]
                e, m = _softmax_numerator(jnp.where(valid, first[k][0], NEG_INF), sink)
                numerators[k] = (e, jnp.exp(sink - m), _dot(e, ones_keys))

            ahead = ATTN_BWD_AHEAD
            for k in range(ahead):
                issue_first(k)
            issue_row_sums(0)
            issue_row_sums(1)
            dqs, dk_parts, dv_parts = {}, {}, {}
            operands = {}

            def issue_last(k):
                g, par, i = chains[k]
                ds, ds_t, p_t = operands.pop(k)
                dq = _dot(ds, kv[g, par][0])
                dqs[g, i] = dq if par == 0 else dqs[g, i] + dq
                dk = _dot(ds_t, q_ref[2 * g + i])
                dv = _dot(p_t, douts[2 * g + i])
                dk_parts[g, par] = dk if i == 0 else dk_parts[g, par] + dk
                dv_parts[g, par] = dv if i == 0 else dv_parts[g, par] + dv

            for k, (g, par, i) in enumerate(chains):
                h = 4 * g + 2 * i + par
                delta = deltas[2 * g + i][:, par * LANES:(par + 1) * LANES]
                e, at_sink, row_sum = numerators[k]
                inv = 1.0 / (row_sum + at_sink)
                p = e * jnp.tile(inv, (1, 2))
                ds = p * (first[k][1] - jnp.tile(delta, (1, 2)))
                ds = ds.astype(MXU_DTYPE)
                operands[k] = (ds, ds.T, p.astype(MXU_DTYPE).T)
                total = jnp.sum(at_sink * inv * delta, axis=0, keepdims=True)
                gsink = jnp.where(lane_row == h, -total, gsink)
                if k + ahead < len(chains):
                    issue_first(k + ahead)
                if k + 2 < len(chains):
                    issue_row_sums(k + 2)
                if k > 0:
                    issue_last(k - 1)
                if k == SGU_GATES_AFTER_CHAIN:
                    sgu_gates()
                if k == SGU_GRADS_AFTER_CHAIN:
                    sgu_grads()
            issue_last(len(chains) - 1)
            for pair in range(N_PAIRS):
                g, i = divmod(pair, 2)
                dq = dqs[g, i] * SCALE
                lanes = slice(pair * LANES, (pair + 1) * LANES)
                pend_ref[:, lanes] = dq.astype(MXU_DTYPE)
                gbin_ref[:, lanes] += jnp.sum(dq, axis=0, keepdims=True)
            gsink_ref[...] += gsink
            for k, parts in enumerate((dk_parts, dv_parts)):
                masked = {key: jnp.where(low_keys if key[1] == 0 else jnp.logical_not(low_keys), val, 0.0)
                          for key, val in parts.items()}
                both = (masked[0, 0] + masked[1, 1]
                        + pltpu.roll(masked[0, 1] + masked[1, 0], HALF, 1))
                lanes = slice(k * KV_W, (k + 1) * KV_W)
                done = carry_ref[:, lanes] + both[0:BLOCK]
                dp_ref[:, KV0 + k * KV_W:KV0 + (k + 1) * KV_W] = done.astype(MXU_DTYPE)
                carry_ref[:, lanes] = both[BLOCK:]
                gbin_ref[:, KV0 + k * KV_W:KV0 + (k + 1) * KV_W] += jnp.sum(both, axis=0, keepdims=True)
            kp_ref[...] = kc_ref[...]

        @pl.when(n == last)
        def _():
            for hh in range(N_SGU_HEADS):
                gw_ref[hh] = jnp.where(tril, gw_ref[hh], 0.0)
            head_of_lane = lax.broadcasted_iota(jnp.int32, (N_SGU_HEADS, SGU_W), 1) // HEAD_DIM
            select = (head_of_lane == lax.broadcasted_iota(jnp.int32, (N_SGU_HEADS, SGU_W), 0)).astype(F32)
            gb_ref[...] = lax.dot_general(select, gbias_ref[...], NT, precision=lax.Precision.HIGHEST,
                                          preferred_element_type=F32)

    at = lambda n: jnp.minimum(n, last)
    blk = lambda w: pl.BlockSpec((BLOCK, w), lambda n: (at(n), 0))
    tiles = pl.BlockSpec((N_PAIRS, BLOCK, LANES), lambda n: (0, at(n), 0))
    section = lambda k: pl.BlockSpec((None, BLOCK, SGU_W), lambda n: (k, at(n), 0))
    return pl.pallas_call(
        body,
        name="mixers_bwd",
        grid=(N_BLOCKS + 1,),
        in_specs=[pl.BlockSpec(memory_space=pltpu.SMEM),
                  section(0),
                  tiles,
                  blk(KVX_W),
                  tiles,
                  section(0),
                  section(1),
                  section(1), section(2), section(3),
                  _full((1, SGU_W)), _full((1, SGU_W)), _full((N_SGU_HEADS, BLOCK, BLOCK)), _full((BLOCK, SGU_W)),
                  VMEM_SPEC],
        out_specs=(pl.BlockSpec((BLOCK, ATTN_SECTION), lambda n: (jnp.maximum(n - 1, 0), 0)),
                   _full((1, LANES)), _full((1, ATTN_SECTION)),
                   pl.BlockSpec((BLOCK, SGU_SECTION), lambda n: (at(n), 0)),
                   _full((N_SGU_HEADS, BLOCK, BLOCK)), _full((N_SGU_HEADS, BLOCK)),
                   _full((8, SGU_W)), _full((1, SGU_SECTION)), VMEM_SPEC),
        out_shape=(jax.ShapeDtypeStruct((SEQ, ATTN_SECTION), MXU_DTYPE),
                   jax.ShapeDtypeStruct((1, LANES), F32),
                   jax.ShapeDtypeStruct((1, ATTN_SECTION), F32),
                   jax.ShapeDtypeStruct((SEQ, SGU_SECTION), MXU_DTYPE),
                   jax.ShapeDtypeStruct((N_SGU_HEADS, BLOCK, BLOCK), F32),
                   jax.ShapeDtypeStruct((N_SGU_HEADS, BLOCK), F32),
                   jax.ShapeDtypeStruct((8, SGU_W), F32),
                   jax.ShapeDtypeStruct((1, SGU_SECTION), F32),
                   jax.ShapeDtypeStruct((WOUT_ROWS, D_MODEL), F32)),
        scratch_shapes=([pltpu.VMEM((BLOCK, KVX_W), MXU_DTYPE),
                         pltpu.VMEM((BLOCK, 2 * ATTN_W), MXU_DTYPE), pltpu.VMEM((BLOCK, 2 * KV_W), F32),
                         pltpu.VMEM((N_SGU_HEADS, BLOCK, BLOCK), MXU_DTYPE),
                         pltpu.VMEM((N_SGU_HEADS, BLOCK, BLOCK), MXU_DTYPE), pltpu.VMEM((BLOCK, SGU_W), F32)]
                        + _reduce_scatter_scratch(WOUT_ROWS, D_MODEL, COMM_DTYPE) + _dma_sems(REDUCE_SEMS)),
        compiler_params=_params(("arbitrary",), VMEM_LIMIT),
    )(sinks, dmix, q, kvx, out, gates, dmix, gates, gates, gates, ln_g, ln_b, sgu_w, bias_full, gwout)


def _in_proj_bwd(dpa, dps, win_t, x, norm_g, gres, gwin, vec_parts):
    tm = TOKEN_TILE
    steps = SEQ // tm
    n_parts = len(vec_parts)

    def body(da_ref, ds_ref, w_ref, x_ref, g_ref, gres_ref, gwin_ref, *rest):
        part_refs = rest[:n_parts]
        gx_ref, shard_ref, vec_out_ref, gng_ref, sa, ra, sb, rc, vec_ref, ra_vec, slots, send_sems, recv_sems = (
            rest[n_parts:])
        step = pl.program_id(0)
        copies = _Copies(send_sems, recv_sems)
        start, exchange, finish = _reduce_scatter_plan(copies, 0, gwin_ref, WIN_ROWS, sa, ra, sb, rc, shard_ref)

        @pl.when(step == 0)
        def _():
            gng_ref[...] = jnp.zeros_like(gng_ref)
            start()

        pl.when(step == 2)(exchange)

        dh = _dot(da_ref[...], w_ref[0:ATTN_SECTION, :]) + _dot(ds_ref[...], w_ref[ATTN_SECTION:, :])
        xv = x_ref[...]
        r = lax.rsqrt(jnp.mean(xv * xv, axis=-1, keepdims=True) + NORM_EPS)
        xn = xv * r
        gng_ref[...] += jnp.sum(dh * xn, axis=0, keepdims=True)
        dxn = dh * g_ref[...]
        gx_ref[...] = r * (dxn - xn * jnp.mean(dxn * xn, axis=-1, keepdims=True)) + gres_ref[...]

        @pl.when(step == steps - 1)
        def _():
            finish()
            _all_reduce_vectors(copies, REDUCE_SEMS, gng_ref, *part_refs, vec_out_ref, vec_ref, ra_vec, slots)

    tile = lambda w: pl.BlockSpec((tm, w), lambda i: (i, 0))
    return pl.pallas_call(
        body,
        name="in_proj_bwd",
        grid=(steps,),
        in_specs=[tile(ATTN_SECTION), tile(SGU_SECTION), _full((IN_W, D_MODEL)), tile(D_MODEL),
                  _full((1, D_MODEL)), tile(D_MODEL), VMEM_SPEC] + [VMEM_SPEC] * n_parts,
        out_specs=(tile(D_MODEL), VMEM_SPEC, VMEM_SPEC),
        out_shape=(jax.ShapeDtypeStruct((SEQ, D_MODEL), F32),
                   jax.ShapeDtypeStruct((WIN_ROWS, D_MODEL), F32),
                   jax.ShapeDtypeStruct((VEC_ROWS, IN_W), F32)),
        scratch_shapes=([pltpu.VMEM((1, D_MODEL), F32)] + _reduce_scatter_scratch(WIN_ROWS, D_MODEL, COMM_DTYPE)
                        + _vector_scratch() + _dma_sems(REDUCE_SEMS + VECTOR_SEMS)),
        compiler_params=_params(("arbitrary",), VMEM_LIMIT),
    )(dpa, dps, win_t, x, norm_g, gres, gwin, *vec_parts)


def _win_grad(dpa, dps, h, gsguw):
    rows = 256
    n_attn = ATTN_SECTION // rows
    steps = n_attn + SGU_SECTION // rows

    def body(da_ref, ds_ref, h_ref, gsguw_ref, o_ref, sguw_full_ref, sa, ra, sb, rc, landing, send_sems, recv_sems):
        step = pl.program_id(0)
        copies = _Copies(send_sems, recv_sems)
        own_sguw = landing.at[_block_rows(_place(), SGUW_ROWS), :]
        start, exchange, finish = _reduce_scatter_plan(copies, 0, gsguw_ref, SGUW_ROWS, sa, ra, sb, rc, own_sguw)
        gather = _gather_plan(copies, REDUCE_SEMS, landing, SGUW_ROWS)

        pl.when(step == 0)(start)
        pl.when(step == 2)(exchange)

        @pl.when(step == 5)
        def _():
            finish()
            gather[0]()

        pl.when(step == 7)(gather[1])

        @pl.when(step < n_attn)
        def _():
            o_ref[...] = _dot(da_ref[...], h_ref[...], TN)

        @pl.when(step >= n_attn)
        def _():
            o_ref[...] = _dot(ds_ref[...], h_ref[...], TN)

        @pl.when(step == steps - 1)
        def _():
            gather[2]()
            sguw_full_ref[...] = landing[...]

    return pl.pallas_call(
        body,
        name="win_grad",
        grid=(steps,),
        in_specs=[pl.BlockSpec((SEQ, rows), lambda i: (0, jnp.minimum(i, n_attn - 1))),
                  pl.BlockSpec((SEQ, rows), lambda i: (0, jnp.maximum(i - n_attn, 0))),
                  _full((SEQ, D_MODEL)), VMEM_SPEC],
        out_specs=(pl.BlockSpec((rows, D_MODEL), lambda i: (i, 0)), _full((N_SGU_HEADS * BLOCK, BLOCK))),
        out_shape=(jax.ShapeDtypeStruct((IN_W, D_MODEL), F32),
                   jax.ShapeDtypeStruct((N_SGU_HEADS * BLOCK, BLOCK), F32)),
        scratch_shapes=(_reduce_scatter_scratch(SGUW_ROWS, BLOCK, F32)
                        + [pltpu.VMEM((N_SGU_HEADS * BLOCK, BLOCK), F32)]
                        + _dma_sems(REDUCE_SEMS + GATHER_SEMS)),
        compiler_params=_params(("arbitrary",), VMEM_LIMIT),
    )(dpa, dps, h, gsguw)


VEC_NORM_G, VEC_B_IN, VEC_SINKS, VEC_LN_G, VEC_LN_B, VEC_B_OUT, VEC_FINAL_G, VEC_LOSS, VEC_SGU_B = 0, 1, 2, 3, 4, 5, 6, 7, 8


def _adamw(w, g, m, v):
    m = ADAM_B1 * m + (1.0 - ADAM_B1) * g
    v = ADAM_B2 * v + (1.0 - ADAM_B2) * (g * g)
    m_hat = m / (1.0 - ADAM_B1 ** ADAM_STEP)
    v_hat = v / (1.0 - ADAM_B2 ** ADAM_STEP)
    delta = -ADAM_LR * (m_hat / (jnp.sqrt(v_hat) + ADAM_EPS) + ADAM_WD * w)
    return delta, m, v


def _adamw_shard(name, g, w, m, v, block_rows):
    def body(g_ref, w_ref, m_ref, v_ref, d_ref, nm_ref, nv_ref):
        d_ref[...], nm_ref[...], nv_ref[...] = _adamw(w_ref[...], g_ref[...], m_ref[...], v_ref[...])

    rows, cols = w.shape
    spec = pl.BlockSpec((block_rows, cols), lambda i: (i, 0))
    return pl.pallas_call(
        body,
        name=name,
        grid=(rows // block_rows,),
        in_specs=[spec] * 4,
        out_specs=(spec,) * 3,
        out_shape=(jax.ShapeDtypeStruct(w.shape, F32),) * 3,
        compiler_params=_params(("arbitrary",)),
    )(g, w, m, v)


VECTOR_SEMS = 4


def _vector_scratch():
    return [pltpu.VMEM((VEC_ROWS, IN_W), F32), pltpu.VMEM((VEC_ROWS, IN_W), F32),
            pltpu.VMEM((4 * VEC_ROWS, IN_W), F32)]


def _all_reduce_vectors(copies, sem0, gng_ref, gba_ref, gbs_ref, gsink_ref, gln_ref, gsgub_ref, vec4_ref, out_ref,
                        vec_ref, ra_vec, slots):
    x, y, c = _place()
    vec_ref[...] = jnp.zeros_like(vec_ref)
    vec_ref[VEC_NORM_G:VEC_NORM_G + 1, 0:D_MODEL] = gng_ref[...]
    vec_ref[VEC_B_IN:VEC_B_IN + 1, 0:ATTN_SECTION] = gba_ref[...]
    vec_ref[VEC_B_IN:VEC_B_IN + 1, ATTN_SECTION:IN_W] = gbs_ref[...]
    vec_ref[VEC_SINKS:VEC_SINKS + 1, 0:LANES] = gsink_ref[...]
    vec_ref[VEC_LN_G:VEC_LN_G + 1, 0:SGU_W] = gln_ref[0:1, :]
    vec_ref[VEC_LN_B:VEC_LN_B + 1, 0:SGU_W] = gln_ref[1:2, :]
    vec_ref[VEC_B_OUT:VEC_B_OUT + 1, 0:D_MODEL] = vec4_ref[2:3, :]
    vec_ref[VEC_FINAL_G:VEC_FINAL_G + 1, 0:D_MODEL] = vec4_ref[1:2, :]
    vec_ref[VEC_LOSS:VEC_LOSS + 1, 0:D_MODEL] = vec4_ref[0:1, :]
    vec_ref[VEC_SGU_B:VEC_SGU_B + N_SGU_HEADS, 0:BLOCK] = gsgub_ref[...]

    to_sibling = copies(sem0, vec_ref, ra_vec, (x, y, 1 - c))
    to_sibling.start()
    to_sibling.wait_recv()

    def chip_slot(place):
        return slots.at[pl.ds(pl.multiple_of((2 * place[0] + place[1]) * VEC_ROWS, 8), VEC_ROWS), :]

    mine = chip_slot((x, y))
    mine[...] = vec_ref[...] + ra_vec[...]
    to_chips = [copies(sem0 + i, mine, mine, (*_chip(rel), c)) for i, rel in enumerate(RELATIONS[1:], start=1)]
    for cp in to_chips:
        cp.start()
    for i, rel in enumerate(RELATIONS[1:], start=1):
        theirs = chip_slot(_chip(rel))
        copies(sem0 + i, theirs, theirs, (x, y, c)).wait_recv()
    out_ref[...] = ((slots[0:VEC_ROWS, :] + slots[VEC_ROWS:2 * VEC_ROWS, :])
                    + slots[2 * VEC_ROWS:3 * VEC_ROWS, :]) + slots[3 * VEC_ROWS:, :]
    to_sibling.wait_send()
    for cp in to_chips:
        cp.wait_send()


def _adamw_replicated(vec, gsguw, weights, m_state, v_state):
    n = len(SMALL)

    def body(*refs):
        vec_ref, gsguw_ref = refs[0], refs[1]
        w_refs, m_refs, v_refs = (refs[2 + k * n:2 + (k + 1) * n] for k in range(3))
        outs = refs[2 + 3 * n:]
        g_refs, d_refs, nm_refs, nv_refs = (outs[k * n:(k + 1) * n] for k in range(4))
        for i, (_, row, shape) in enumerate(SMALL):
            g = gsguw_ref[...] if row is None else vec_ref[row:row + shape[0], 0:shape[1]]
            g_refs[i][...] = g
            d_refs[i][...], nm_refs[i][...], nv_refs[i][...] = _adamw(
                w_refs[i][...], g, m_refs[i][...], v_refs[i][...])

    shapes = tuple(jax.ShapeDtypeStruct(shape, F32) for _, _, shape in SMALL)
    outs = pl.pallas_call(
        body,
        name="adamw_replicated",
        in_specs=[VMEM_SPEC] * (2 + 3 * n),
        out_specs=(VMEM_SPEC,) * (4 * n),
        out_shape=shapes * 4,
    )(vec, gsguw, *weights, *m_state, *v_state)
    return tuple(outs[k * n:(k + 1) * n] for k in range(4))


SMALL = (
    ("norm_g", VEC_NORM_G, (1, D_MODEL)),
    ("b_in", VEC_B_IN, (1, IN_W)),
    ("attn_sinks", VEC_SINKS, (1, N_Q_HEADS)),
    ("sgu_ln_g", VEC_LN_G, (1, SGU_W)),
    ("sgu_ln_b", VEC_LN_B, (1, SGU_W)),
    ("sgu_w", None, (N_SGU_HEADS * BLOCK, BLOCK)),
    ("sgu_b", VEC_SGU_B, (N_SGU_HEADS, BLOCK)),
    ("b_out", VEC_B_OUT, (1, D_MODEL)),
    ("final_norm_g", VEC_FINAL_G, (1, D_MODEL)),
)


def _local_grads(x, target, h, win_t, wout_shard, norm_g, b_in, attn_sinks, sgu_ln_g, sgu_ln_b, sgu_w, sgu_b, b_out,
                 final_g):
    sinks = attn_sinks.reshape(N_Q_HEADS)
    bias_full = jnp.repeat(sgu_b.T, HEAD_DIM, axis=1)
    q, kvx, gates, wout = _in_proj(h, b_in, win_t, wout_shard)
    out, ag, sg = _mixers_fwd(sinks, q, kvx, gates, sgu_ln_g, sgu_ln_b, sgu_w, bias_full)
    gres, dmix, gwout, vec4 = _out_proj_loss(ag, sg, x, target, wout, b_out, final_g)
    dpa, gsink, gbin_a, dps, gsguw, gsgub, gln, gbin_s, gwout_shard = _mixers_bwd(
        sinks, dmix, q, kvx, out, gates, sgu_ln_g, sgu_ln_b, sgu_w, bias_full, gwout)
    gwin, gsguw_sum = _win_grad(dpa, dps, h, gsguw.reshape(N_SGU_HEADS * BLOCK, BLOCK))
    grad_x, gwin_shard, vec = _in_proj_bwd(dpa, dps, win_t, x, norm_g, gres, gwin,
                                           (gbin_a, gbin_s, gsink, gln, gsgub, vec4))
    return grad_x, gwin_shard, gwout_shard, gsguw_sum, vec


def kernel(x, norm_g, w_in, b_in, attn_sinks, sgu_ln_g, sgu_ln_b, sgu_w, sgu_b, w_out, b_out, final_norm_g, loss_target, m_norm_g, m_w_in, m_b_in, m_attn_sinks, m_sgu_ln_g, m_sgu_ln_b, m_sgu_w, m_sgu_b, m_w_out, m_b_out, m_final_norm_g, v_norm_g, v_w_in, v_b_in, v_attn_sinks, v_sgu_ln_g, v_sgu_ln_b, v_sgu_w, v_sgu_b, v_w_out, v_b_out, v_final_norm_g):
    given = dict(norm_g=norm_g, b_in=b_in, attn_sinks=attn_sinks, sgu_ln_g=sgu_ln_g, sgu_ln_b=sgu_ln_b,
                 sgu_w=sgu_w, sgu_b=sgu_b, b_out=b_out, final_norm_g=final_norm_g)
    m_given = dict(norm_g=m_norm_g, b_in=m_b_in, attn_sinks=m_attn_sinks, sgu_ln_g=m_sgu_ln_g,
                   sgu_ln_b=m_sgu_ln_b, sgu_w=m_sgu_w, sgu_b=m_sgu_b, b_out=m_b_out, final_norm_g=m_final_norm_g)
    v_given = dict(norm_g=v_norm_g, b_in=v_b_in, attn_sinks=v_attn_sinks, sgu_ln_g=v_sgu_ln_g,
                   sgu_ln_b=v_sgu_ln_b, sgu_w=v_sgu_w, sgu_b=v_sgu_b, b_out=v_b_out, final_norm_g=v_final_norm_g)

    win_t, h = _all_gather_win(w_in[0].T, x[0], norm_g)
    grad_x, gwin_t, gwout, gsguw, vec = _local_grads(
        x[0], loss_target[0], h, win_t, w_out[0], norm_g, b_in, attn_sinks, sgu_ln_g, sgu_ln_b, sgu_w[0], sgu_b[0],
        b_out, final_norm_g.reshape(1, D_MODEL))

    t = lambda a: a[0].T
    d_win, nm_win, nv_win = _adamw_shard("adamw_w_in", gwin_t, t(w_in), t(m_w_in), t(v_w_in), WIN_ROWS // 2)
    d_wout, nm_wout, nv_wout = _adamw_shard("adamw_w_out", gwout, w_out[0], m_w_out[0], v_w_out[0], WOUT_ROWS)
    as_2d = lambda d: [d[name].reshape(shape) for name, _, shape in SMALL]
    loss = vec[VEC_LOSS, 0]
    small = _adamw_replicated(vec, gsguw, as_2d(given), as_2d(m_given), as_2d(v_given))

    def assemble(big_in, big_out, k):
        vals = {name: small[k][i].reshape(given[name].shape) for i, (name, _, _) in enumerate(SMALL)}
        vals["w_in"] = big_in.T[None]
        vals["w_out"] = big_out[None]
        order = ("norm_g", "w_in", "b_in", "attn_sinks", "sgu_ln_g", "sgu_ln_b", "sgu_w", "sgu_b", "w_out",
                 "b_out", "final_norm_g")
        return [vals[name] for name in order]

    return (loss, grad_x[None],
            *assemble(gwin_t, gwout, 0), *assemble(d_win, d_wout, 1),
            *assemble(nm_win, nm_wout, 2), *assemble(nv_win, nv_wout, 3))
```

```python
import functools
import math

import jax
import jax.numpy as jnp
from jax import lax
from jax.experimental import pallas as pl
from jax.experimental.pallas import tpu as pltpu

F32 = jnp.float32
BF16 = jnp.bfloat16
MXU_DTYPE = BF16
COMM_DTYPE = BF16

D_MODEL = 1024
SEQ = 4096
HEAD_DIM = 64
N_Q_HEADS = 8
Q_PER_KV = 4
BLOCK = 128
N_BLOCKS = SEQ // BLOCK
ATTN_W = 512
KV_W = 128
SGU_W = 512
N_SGU_HEADS = 8
IN_W = 2816
NORM_EPS = 1e-5
NEG_INF = -1e30
SCALE = HEAD_DIM ** -0.5
KV0 = ATTN_W
GATE0 = ATTN_W + 2 * KV_W
SGU0 = GATE0 + ATTN_W
ATTN_SECTION = SGU0
SGU_SECTION = IN_W - SGU0

ADAM_LR = 0.001
ADAM_B1 = 0.9
ADAM_B2 = 0.999
ADAM_EPS = 1e-08
ADAM_WD = 0.01
ADAM_STEP = 10

N_DEV = 8
WIN_ROWS = IN_W // N_DEV
WOUT_ROWS = D_MODEL // N_DEV
SGUW_ROWS = N_SGU_HEADS * BLOCK // N_DEV
VEC_ROWS = 16
MESH = pl.DeviceIdType.MESH

LANES = 128
HALF = LANES // 2
N_PAIRS = N_Q_HEADS * HEAD_DIM // LANES
KVX_W = 12 * LANES
TOKEN_TILE = 256
FWD_TOKEN_TILE = 512
ATTN_FWD_AHEAD = 4
FUSED_BLOCKS = 2
SGU_MIX_AFTER_CHAIN = 0
SGU_GATES_AFTER_CHAIN = 1
SGU_GRADS_AFTER_CHAIN = 5
ATTN_BWD_AHEAD = 3
VMEM_LIMIT = 56 * 1024 * 1024

NN = (((1,), (0,)), ((), ()))
NT = (((1,), (1,)), ((), ()))
TN = (((0,), (0,)), ((), ()))


def _dot(a, b, dims=NN):
    return lax.dot_general(a.astype(MXU_DTYPE), b.astype(MXU_DTYPE), dims, preferred_element_type=F32)


def _gelu(x):
    return x * (lax.erf(x * (1.0 / math.sqrt(2.0))) + 1.0) * 0.5


def _gelu_grad(x):
    cdf = (lax.erf(x * (1.0 / math.sqrt(2.0))) + 1.0) * 0.5
    return cdf + x * jnp.exp(-0.5 * x * x) * (1.0 / math.sqrt(2.0 * math.pi))


def _silu_and_grad(z):
    s = jax.nn.sigmoid(z)
    return z * s, s * (1.0 + z * (1.0 - s))


def _params(semantics=None, vmem=None):
    kw = {}
    if semantics is not None:
        kw["dimension_semantics"] = semantics
    if vmem is not None:
        kw["vmem_limit_bytes"] = vmem
    return pltpu.CompilerParams(**kw)


def _full(shape):
    return pl.BlockSpec(shape, lambda *_: (0,) * len(shape))


VMEM_SPEC = pl.BlockSpec(memory_space=pltpu.VMEM)


RELATIONS = ((0, 0), (1, 0), (0, 1), (1, 1))


def _place():
    return lax.axis_index("x"), lax.axis_index("y"), lax.axis_index("c")


def _chip(rel):
    x, y, _ = _place()
    return (1 - x if rel[0] else x, 1 - y if rel[1] else y)


def _block_rows(place, n_rows):
    px, py, pc = place
    return pl.ds(pl.multiple_of((4 * px + 2 * py + pc) * n_rows, 16), n_rows)


class _Copies:
    def __init__(self, send_sems, recv_sems):
        self.send_sems, self.recv_sems = send_sems, recv_sems

    def __call__(self, k, src, dst, to):
        return pltpu.make_async_remote_copy(src_ref=src, dst_ref=dst, send_sem=self.send_sems.at[k],
                                            recv_sem=self.recv_sems.at[k], device_id=to, device_id_type=MESH)


def _gather_plan(copies, sem0, full_ref, n_rows):
    x, y, c = _place()
    me, sibling = (x, y, c), (x, y, 1 - c)
    chips = [_chip(rel) for rel in RELATIONS[1:]]

    def cp(k, block, to):
        rows = full_ref.at[_block_rows(block, n_rows), :]
        return copies(sem0 + k, rows, rows, to)

    first = [cp(0, me, sibling)] + [cp(1 + j, me, (*chip, c)) for j, chip in enumerate(chips)]
    passed = [cp(4 + j, (*chip, c), sibling) for j, chip in enumerate(chips)]

    def start():
        for f in first:
            f.start()

    def forward():
        for j, chip in enumerate(chips):
            cp(1 + j, (*chip, c), me).wait_recv()
            passed[j].start()

    def finish():
        cp(0, sibling, me).wait_recv()
        for j, chip in enumerate(chips):
            cp(4 + j, (*chip, 1 - c), me).wait_recv()
        for f in first + passed:
            f.wait_send()

    return start, forward, finish


GATHER_SEMS = 7


def _reduce_scatter_plan(copies, sem0, part_ref, n_rows, sa, ra, sb, rc, res_ref):
    x, y, c = _place()
    sibling = (x, y, 1 - c)
    n = n_rows
    level1 = copies(sem0, sa, ra, sibling)

    def level2(i):
        slot = pl.ds((i - 1) * n, n)
        return copies(sem0 + i, sb.at[slot, :], rc.at[slot, :], (*_chip(RELATIONS[i]), c))

    def start():
        for i, rel in enumerate(RELATIONS):
            sa[i * n:(i + 1) * n, :] = part_ref[_block_rows((*_chip(rel), 1 - c), n), :].astype(sa.dtype)
        level1.start()

    def exchange():
        level1.wait_recv()
        for i, rel in enumerate(RELATIONS):
            total = part_ref[_block_rows((*_chip(rel), c), n), :] + ra[i * n:(i + 1) * n, :].astype(F32)
            if i == 0:
                res_ref[...] = total
            else:
                sb[(i - 1) * n:i * n, :] = total.astype(sb.dtype)
                level2(i).start()

    def finish():
        acc = res_ref[...]
        for i in range(1, len(RELATIONS)):
            level2(i).wait_recv()
            acc = acc + rc[(i - 1) * n:i * n, :].astype(F32)
        res_ref[...] = acc
        level1.wait_send()
        for i in range(1, len(RELATIONS)):
            level2(i).wait_send()

    return start, exchange, finish


REDUCE_SEMS = 4


def _reduce_scatter_scratch(n_rows, width, dtype):
    return [pltpu.VMEM((4 * n_rows, width), dtype), pltpu.VMEM((4 * n_rows, width), dtype),
            pltpu.VMEM((3 * n_rows, width), dtype), pltpu.VMEM((3 * n_rows, width), dtype)]


def _dma_sems(n):
    return [pltpu.SemaphoreType.DMA((n,)), pltpu.SemaphoreType.DMA((n,))]


def _all_gather_win(win_t_shard, x, norm_g):
    tm = FWD_TOKEN_TILE
    steps = SEQ // tm

    def body(win_ref, x_ref, g_ref, full_ref, h_ref, landing, send_sems, recv_sems):
        step = pl.program_id(0)
        start, forward, finish = _gather_plan(_Copies(send_sems, recv_sems), 0, landing, WIN_ROWS)

        @pl.when(step == 0)
        def _():
            landing[_block_rows(_place(), WIN_ROWS), :] = win_ref[...].astype(COMM_DTYPE)
            start()

        xv = x_ref[...]
        r = lax.rsqrt(jnp.mean(xv * xv, axis=-1, keepdims=True) + NORM_EPS)
        h_ref[...] = ((xv * r) * g_ref[...]).astype(MXU_DTYPE)

        @pl.when(step == steps - 1)
        def _():
            forward()
            finish()
            full_ref[...] = landing[...]

    return pl.pallas_call(
        body,
        name="all_gather_win",
        grid=(steps,),
        in_specs=[VMEM_SPEC, pl.BlockSpec((tm, D_MODEL), lambda i: (i, 0)), _full((1, D_MODEL))],
        out_specs=(_full((IN_W, D_MODEL)), pl.BlockSpec((tm, D_MODEL), lambda i: (i, 0))),
        out_shape=(jax.ShapeDtypeStruct((IN_W, D_MODEL), COMM_DTYPE),
                   jax.ShapeDtypeStruct((SEQ, D_MODEL), MXU_DTYPE)),
        scratch_shapes=[pltpu.VMEM((IN_W, D_MODEL), COMM_DTYPE)] + _dma_sems(GATHER_SEMS),
        compiler_params=_params(("arbitrary",), VMEM_LIMIT),
    )(win_t_shard, x, norm_g)


def _in_proj(h, b_in, win_t, wout_shard):
    tm = FWD_TOKEN_TILE
    steps = SEQ // tm

    def body(h_ref, b_ref, w_ref, wout_ref, q_ref, kvx_ref, gate_ref, wfull_ref, landing, send_sems, recv_sems):
        step = pl.program_id(0)
        start, forward, finish = _gather_plan(_Copies(send_sems, recv_sems), 0, landing, WOUT_ROWS)

        @pl.when(step == 0)
        def _():
            landing[_block_rows(_place(), WOUT_ROWS), :] = wout_ref[...].astype(COMM_DTYPE)
            start()

        pl.when(step == steps // 2)(forward)

        h = h_ref[...]

        def proj(lo, hi):
            return _dot(h, w_ref[lo:hi, :], NT) + b_ref[:, lo:hi]

        qs = proj(0, ATTN_W) * SCALE
        for pair in range(N_PAIRS):
            q_ref[pair] = qs[:, pair * LANES:(pair + 1) * LANES].astype(MXU_DTYPE)
        kv = proj(KV0, GATE0)
        low = lax.broadcasted_iota(jnp.int32, (tm, LANES), 1) < HALF
        for i in range(2):
            t = kv[:, i * LANES:(i + 1) * LANES]
            rot = pltpu.roll(t, HALF, 1)
            variants = (jnp.where(low, t, 0.0), jnp.where(low, 0.0, rot),
                        jnp.where(low, rot, 0.0), jnp.where(low, 0.0, t))
            for j, val in enumerate(variants):
                col = (4 * i + j) * LANES
                kvx_ref[:, col:col + LANES] = val.astype(MXU_DTYPE)
                if i == 1:
                    ones_elsewhere = jnp.where(low == (j % 2 == 0), val, 1.0)
                    kvx_ref[:, col + 4 * LANES:col + 5 * LANES] = ones_elsewhere.astype(MXU_DTYPE)
        for k in range(4):
            gate_ref[k] = proj(GATE0 + k * SGU_W, GATE0 + (k + 1) * SGU_W)

        @pl.when(step == steps - 1)
        def _():
            finish()
            wfull_ref[...] = landing[...]

    return pl.pallas_call(
        body,
        name="in_proj",
        grid=(steps,),
        in_specs=[pl.BlockSpec((tm, D_MODEL), lambda i: (i, 0)),
                  _full((1, IN_W)), _full((IN_W, D_MODEL)), VMEM_SPEC],
        out_specs=(pl.BlockSpec((N_PAIRS, tm, LANES), lambda i: (0, i, 0)),
                   pl.BlockSpec((tm, KVX_W), lambda i: (i, 0)),
                   pl.BlockSpec((4, tm, SGU_W), lambda i: (0, i, 0)),
                   _full((D_MODEL, D_MODEL))),
        out_shape=(jax.ShapeDtypeStruct((N_PAIRS, SEQ, LANES), MXU_DTYPE),
                   jax.ShapeDtypeStruct((SEQ, KVX_W), MXU_DTYPE),
                   jax.ShapeDtypeStruct((4, SEQ, SGU_W), F32),
                   jax.ShapeDtypeStruct((D_MODEL, D_MODEL), COMM_DTYPE)),
        scratch_shapes=[pltpu.VMEM((D_MODEL, D_MODEL), COMM_DTYPE)] + _dma_sems(GATHER_SEMS),
        compiler_params=_params(("arbitrary",), VMEM_LIMIT),
    )(h, b_in, win_t, wout_shard)


def _window_mask(n):
    qi = lax.broadcasted_iota(jnp.int32, (2 * BLOCK, 2 * BLOCK), 0) & (BLOCK - 1)
    p = lax.broadcasted_iota(jnp.int32, (2 * BLOCK, 2 * BLOCK), 1) - BLOCK
    in_window = jnp.logical_and(p <= qi, p > qi - BLOCK)
    return jnp.logical_and(in_window, jnp.logical_or(p >= 0, n > 0))


def _sink_column(sink_ref, g, par):
    return jnp.concatenate([jnp.full((BLOCK, 1), sink_ref[4 * g + par], F32),
                            jnp.full((BLOCK, 1), sink_ref[4 * g + 2 + par], F32)], axis=0)


def _kv_cat(kp_ref, kc_ref, var, with_ones):
    kcol, vcol = var * LANES, (var + (8 if with_ones else 4)) * LANES
    return (jnp.concatenate([kp_ref[:, kcol:kcol + LANES], kc_ref[:, kcol:kcol + LANES]], axis=0),
            jnp.concatenate([kp_ref[:, vcol:vcol + LANES], kc_ref[:, vcol:vcol + LANES]], axis=0))


def _softmax_numerator(s, sink):
    m = jnp.maximum(jnp.max(s, axis=1, keepdims=True), sink)
    return jnp.exp(s - m), m


def _mixers_out_proj(sinks, q, kvx, gates, ln_g, ln_b, sgu_w, bias_full, x, target, wout, b_out, final_g):
    tm = FUSED_BLOCKS * BLOCK
    n_tiles = SEQ // tm

    def body(sink_ref, q_ref, kc_ref, za_ref, us_ref, vs_ref, zs_ref, lng_ref, lnb_ref, w_ref, bias_ref,
             x_ref, t_ref, wout_ref, b_ref, gf_ref,
             out_ref, gres_ref, dmix_ref, gw_ref, vec_ref,
             kp_ref, wm_ref, mixed_next, mixed_cur, out_stage, gb_ref):
        step = pl.program_id(0)

        @pl.when(step == 0)
        def _():
            kp_ref[...] = jnp.zeros_like(kp_ref)
            _mask_sgu_weights(w_ref, wm_ref)
            gw_ref[...] = jnp.zeros_like(gw_ref)
            vec_ref[...] = jnp.zeros_like(vec_ref)
            mixed_cur[...] = jnp.zeros_like(mixed_cur)

        def mixers_block(b):
            rows = slice(b * BLOCK, (b + 1) * BLOCK)
            kc = kc_ref.at[rows, :]
            u, _, _, vln = _sgu_activations(us_ref[rows, :], vs_ref[rows, :], lng_ref[...], lnb_ref[...])

            valid = _window_mask(step * FUSED_BLOCKS + b)[0:BLOCK]
            chains = [(g, par, i) for g in range(2) for par in range(2) for i in range(2)]
            kv = {(g, par): _kv_cat(kp_ref, kc, 2 * g + par, True) for g in range(2) for par in range(2)}
            scores, outs = {}, {}

            def issue_scores(k):
                g, par, i = chains[k]
                scores[k] = _dot(q_ref[2 * g + i, rows, :], kv[g, par][0], NT)

            ahead = ATTN_FWD_AHEAD
            for k in range(ahead):
                issue_scores(k)
            low = lax.broadcasted_iota(jnp.int32, (BLOCK, LANES), 1) < HALF
            for k, (g, par, i) in enumerate(chains):
                sink = sink_ref[4 * g + 2 * i + par]
                e, m = _softmax_numerator(jnp.where(valid, scores[k], NEG_INF), sink)
                if k + ahead < len(chains):
                    issue_scores(k + ahead)
                o = _dot(e, kv[g, par][1])
                outs[g, par, i] = o / (pltpu.roll(o, HALF, 1) + jnp.exp(sink - m))
                if k == SGU_MIX_AFTER_CHAIN:
                    mixed = _sgu_mix(vln, wm_ref, bias_ref)
            for pair in range(N_PAIRS):
                g, i = divmod(pair, 2)
                lanes = slice(pair * LANES, (pair + 1) * LANES)
                o = jnp.where(low, outs[g, 0, i], outs[g, 1, i])
                out_stage[pair, rows, :] = o
                gate, _ = _silu_and_grad(za_ref[rows, lanes])
                mixed_next[rows, lanes] = (o * gate).astype(MXU_DTYPE)
            kp_ref[...] = kc[...]
            for pair in range(N_SGU_HEADS // 2):
                cols = slice(pair * LANES, (pair + 1) * LANES)
                gate, _ = _silu_and_grad(zs_ref[rows, cols])
                mixed_next[rows, ATTN_W + pair * LANES:ATTN_W + (pair + 1) * LANES] = (
                    u[:, cols] * mixed[pair] * gate).astype(MXU_DTYPE)

        live = (step > 0).astype(F32)
        projected = _dot(mixed_cur[...], wout_ref[...])
        mixers_block(0)
        xo = x_ref[...] + projected + b_ref[...]
        r = lax.rsqrt(jnp.mean(xo * xo, axis=-1, keepdims=True) + NORM_EPS)
        xn = xo * r
        gf = gf_ref[...]
        err = xn * gf - t_ref[...]
        loss = 0.5 * jnp.sum(jnp.mean(err * err, axis=-1, keepdims=True), axis=0, keepdims=True)
        dy = err * (1.0 / D_MODEL)
        dxn = dy * gf
        gres = r * (dxn - xn * jnp.mean(dxn * xn, axis=-1, keepdims=True))
        vec_ref[0:1, :] += jnp.broadcast_to(loss * live, (1, D_MODEL))
        vec_ref[1:2, :] += jnp.sum(dy * xn, axis=0, keepdims=True) * live
        vec_ref[2:3, :] += jnp.sum(gres, axis=0, keepdims=True) * live
        gres_ref[...] = gres
        gb = gres.astype(MXU_DTYPE)
        gb_ref[...] = gb
        dmix_ref[0] = _dot(gb, wout_ref[0:ATTN_W, :], NT)
        dmix_ref[1] = _dot(gb, wout_ref[ATTN_W:, :], NT)
        for b in range(1, FUSED_BLOCKS):
            mixers_block(b)
        gw_ref[...] += _dot(mixed_cur[...], gb_ref[...], TN)

        @pl.when(step < n_tiles)
        def _():
            out_ref[...] = out_stage[...]

        mixed_cur[...] = mixed_next[...]

    ahead_tile = lambda i: jnp.minimum(i, n_tiles - 1)
    behind_tile = lambda i: jnp.maximum(i - 1, 0)
    blk = lambda w: pl.BlockSpec((tm, w), lambda i: (ahead_tile(i), 0))
    tiles = pl.BlockSpec((N_PAIRS, tm, LANES), lambda i: (0, ahead_tile(i), 0))
    gate = lambda k: pl.BlockSpec((None, tm, SGU_W), lambda i: (k, ahead_tile(i), 0))
    behind = lambda w: pl.BlockSpec((tm, w), lambda i: (behind_tile(i), 0))
    return pl.pallas_call(
        body,
        name="mixers_out_proj",
        grid=(n_tiles + 1,),
        in_specs=[pl.BlockSpec(memory_space=pltpu.SMEM), tiles, blk(KVX_W), gate(0), gate(1), gate(2), gate(3),
                  _full((1, SGU_W)), _full((1, SGU_W)), _full((N_SGU_HEADS, BLOCK, BLOCK)), _full((BLOCK, SGU_W)),
                  behind(D_MODEL), behind(D_MODEL), _full((D_MODEL, D_MODEL)), _full((1, D_MODEL)),
                  _full((1, D_MODEL))],
        out_specs=(tiles, behind(D_MODEL), pl.BlockSpec((2, tm, ATTN_W), lambda i: (0, behind_tile(i), 0)),
                   _full((D_MODEL, D_MODEL)), _full((8, D_MODEL))),
        out_shape=(jax.ShapeDtypeStruct((N_PAIRS, SEQ, LANES), F32),
                   jax.ShapeDtypeStruct((SEQ, D_MODEL), F32),
                   jax.ShapeDtypeStruct((2, SEQ, ATTN_W), F32),
                   jax.ShapeDtypeStruct((D_MODEL, D_MODEL), F32),
                   jax.ShapeDtypeStruct((8, D_MODEL), F32)),
        scratch_shapes=[pltpu.VMEM((BLOCK, KVX_W), MXU_DTYPE), pltpu.VMEM((N_SGU_HEADS, BLOCK, BLOCK), MXU_DTYPE),
                        pltpu.VMEM((tm, D_MODEL), MXU_DTYPE), pltpu.VMEM((tm, D_MODEL), MXU_DTYPE),
                        pltpu.VMEM((N_PAIRS, tm, LANES), F32), pltpu.VMEM((tm, D_MODEL), MXU_DTYPE)],
        compiler_params=_params(("arbitrary",), VMEM_LIMIT),
    )(sinks, q, kvx, gates, gates, gates, gates, ln_g, ln_b, sgu_w, bias_full, x, target, wout, b_out, final_g)


def _sgu_activations(us, vs, lng, lnb):
    u = _gelu(us)
    vg = _gelu(vs)
    mu = jnp.mean(vg, axis=-1, keepdims=True)
    xc = vg - mu
    rstd = lax.rsqrt(jnp.mean(xc * xc, axis=-1, keepdims=True) + NORM_EPS)
    vhat = xc * rstd
    return u, vhat, rstd, vhat * lng + lnb


def _mask_sgu_weights(w_ref, masked_ref, transposed_ref=None):
    tril = (lax.broadcasted_iota(jnp.int32, (BLOCK, BLOCK), 0)
            >= lax.broadcasted_iota(jnp.int32, (BLOCK, BLOCK), 1))
    for hh in range(N_SGU_HEADS):
        w = jnp.where(tril, w_ref[hh], 0.0)
        masked_ref[hh] = w.astype(MXU_DTYPE)
        if transposed_ref is not None:
            transposed_ref[hh] = w.T.astype(MXU_DTYPE)


def _sgu_mix(vln, masked_w_ref, bias_ref):
    low = lax.broadcasted_iota(jnp.int32, (BLOCK, LANES), 1) < HALF
    mixed = []
    for pair in range(N_SGU_HEADS // 2):
        vp = vln[:, pair * LANES:(pair + 1) * LANES]
        mixed.append(_dot(masked_w_ref[2 * pair], jnp.where(low, vp, 0.0))
                     + _dot(masked_w_ref[2 * pair + 1], jnp.where(low, 0.0, vp))
                     + bias_ref[:, pair * LANES:(pair + 1) * LANES])
    return mixed


def _mixers_bwd(sinks, dmix, q, kvx, out, gates, ln_g, ln_b, sgu_w, bias_full, gwout):
    last = N_BLOCKS - 1

    def body(sink_ref, d_ref, q_ref, kc_ref, o_ref, za_ref, dsg_ref, us_ref, vs_ref, zs_ref, lng_ref, lnb_ref, w_ref,
             bias_ref, gwout_ref,
             dp_ref, gsink_ref, gbin_ref, dps_ref, gw_ref, gb_ref, gln_ref, gbins_ref, wout_shard_ref,
             kp_ref, pend_ref, carry_ref, wm_ref, wt_ref, gbias_ref, sa_w, ra_w, sb_w, rc_w, send_sems, recv_sems):
        n = pl.program_id(0)
        start, exchange, finish = _reduce_scatter_plan(_Copies(send_sems, recv_sems), 0, gwout_ref, WOUT_ROWS,
                                                       sa_w, ra_w, sb_w, rc_w, wout_shard_ref)
        tril = (lax.broadcasted_iota(jnp.int32, (BLOCK, BLOCK), 0)
                >= lax.broadcasted_iota(jnp.int32, (BLOCK, BLOCK), 1))

        @pl.when(n == 0)
        def _():
            gsink_ref[...] = jnp.zeros_like(gsink_ref)
            gbin_ref[...] = jnp.zeros_like(gbin_ref)
            carry_ref[...] = jnp.zeros_like(carry_ref)
            kp_ref[...] = jnp.zeros_like(kp_ref)
            gw_ref[...] = jnp.zeros_like(gw_ref)
            gln_ref[...] = jnp.zeros_like(gln_ref)
            gbins_ref[...] = jnp.zeros_like(gbins_ref)
            gbias_ref[...] = jnp.zeros_like(gbias_ref)
            _mask_sgu_weights(w_ref, wm_ref, wt_ref)
            start()

        pl.when(n == 3)(exchange)
        pl.when(n == 12)(finish)

        @pl.when(n > 0)
        def _():
            dp_ref[:, 0:ATTN_W] = pend_ref[:, 0:ATTN_W]
            dp_ref[:, GATE0:ATTN_SECTION] = pend_ref[:, ATTN_W:]

        @pl.when(n > last)
        def _():
            dp_ref[:, KV0:GATE0] = carry_ref[...].astype(MXU_DTYPE)

        @pl.when(n <= last)
        def _():
            us = us_ref[...]
            vs = vs_ref[...]
            lng = lng_ref[...]
            u, vhat, rstd, vln = _sgu_activations(us, vs, lng, lnb_ref[...])
            low_sgu = lax.broadcasted_iota(jnp.int32, (BLOCK, LANES), 1) < HALF
            sgu = {}

            def sgu_gates():
                mixed = _sgu_mix(vln, wm_ref, bias_ref)
                sgu["du"], sgu["dzs"], sgu["dm"] = [], [], []
                for pair in range(N_SGU_HEADS // 2):
                    cols = slice(pair * LANES, (pair + 1) * LANES)
                    dsg = dsg_ref[:, cols]
                    gate, gate_grad = _silu_and_grad(zs_ref[:, cols])
                    up = u[:, cols]
                    sgu["du"].append(dsg * mixed[pair] * gate)
                    sgu["dzs"].append(dsg * up * mixed[pair] * gate_grad)
                    dmixed = dsg * up * gate
                    gbias_ref[:, cols] += dmixed
                    sgu["dm"].append((jnp.where(low_sgu, dmixed, 0.0).astype(MXU_DTYPE),
                                      jnp.where(low_sgu, 0.0, dmixed).astype(MXU_DTYPE)))

            def sgu_grads():
                dvln_parts = []
                for pair in range(N_SGU_HEADS // 2):
                    dm_lo, dm_hi = sgu["dm"][pair]
                    vp = vln[:, pair * LANES:(pair + 1) * LANES]
                    gw_ref[2 * pair] += _dot(dm_lo, vp, NT)
                    gw_ref[2 * pair + 1] += _dot(dm_hi, vp, NT)
                    dvln_parts.append(_dot(wt_ref[2 * pair], dm_lo) + _dot(wt_ref[2 * pair + 1], dm_hi))
                dvln = jnp.concatenate(dvln_parts, axis=1)
                gln_ref[0:1, :] += jnp.sum(dvln * vhat, axis=0, keepdims=True)
                gln_ref[1:2, :] += jnp.sum(dvln, axis=0, keepdims=True)
                dvhat = dvln * lng
                dvg = rstd * (dvhat - jnp.mean(dvhat, axis=-1, keepdims=True)
                              - vhat * jnp.mean(dvhat * vhat, axis=-1, keepdims=True))
                dus = jnp.concatenate(sgu["du"], axis=1) * _gelu_grad(us)
                dvs = dvg * _gelu_grad(vs)
                dzs = jnp.concatenate(sgu["dzs"], axis=1)
                for k, val in enumerate((dus, dvs, dzs)):
                    dps_ref[:, k * SGU_W:(k + 1) * SGU_W] = val.astype(MXU_DTYPE)
                    gbins_ref[:, k * SGU_W:(k + 1) * SGU_W] += jnp.sum(val, axis=0, keepdims=True)

            valid = _window_mask(n)[0:BLOCK]
            low = lax.broadcasted_iota(jnp.int32, (BLOCK, LANES), 1) < HALF
            low_keys = lax.broadcasted_iota(jnp.int32, (2 * BLOCK, LANES), 1) < HALF
            lane_row = lax.broadcasted_iota(jnp.int32, (1, LANES), 1)
            gsink = jnp.zeros((1, LANES), F32)
            chains = [(g, par, i) for g in range(2) for par in range(2) for i in range(2)]
            kv = {(g, par): _kv_cat(kp_ref, kc_ref, 2 * g + par, False) for g in range(2) for par in range(2)}
            ones_keys = jnp.ones((2 * BLOCK, LANES), MXU_DTYPE)
            half_of_lane = lax.broadcasted_iota(jnp.int32, (LANES, 2 * LANES), 0) // HALF
            half_of_col = lax.broadcasted_iota(jnp.int32, (LANES, 2 * LANES), 1) // LANES
            sum_halves = (half_of_lane == half_of_col).astype(MXU_DTYPE)
            douts, deltas = [], []
            for pair in range(N_PAIRS):
                lanes = slice(pair * LANES, (pair + 1) * LANES)
                dg = d_ref[:, lanes]
                gate, gate_grad = _silu_and_grad(za_ref[:, lanes])
                o = o_ref[pair]
                dout = dg * gate
                dza = dg * o * gate_grad
                douts.append(dout.astype(MXU_DTYPE))
                deltas.append(_dot(dout * o, sum_halves))
                zl = slice(ATTN_W + pair * LANES, ATTN_W + (pair + 1) * LANES)
                pend_ref[:, zl] = dza.astype(MXU_DTYPE)
                gl = slice(GATE0 + pair * LANES, GATE0 + (pair + 1) * LANES)
                gbin_ref[:, gl] += jnp.sum(dza, axis=0, keepdims=True)

            first = {}

            def issue_first(k):
                g, par, i = chains[k]
                first[k] = (_dot(q_ref[2 * g + i], kv[g, par][0], NT), _dot(douts[2 * g + i], kv[g, par][1], NT))

            numerators = {}

            def issue_row_sums(k):
                g, par, i = chains[k]
                sink = sink_ref[4 * g + 2 * i + par]
                e, m = _softmax_numerator(jnp.where(valid, first[k][0], NEG_INF), sink)
                numerators[k] = (e, jnp.exp(sink - m), _dot(e, ones_keys))

            ahead = ATTN_BWD_AHEAD
            for k in range(ahead):
                issue_first(k)
            issue_row_sums(0)
            issue_row_sums(1)
            dqs, dk_parts, dv_parts = {}, {}, {}
            operands = {}

            def issue_last(k):
                g, par, i = chains[k]
                ds, ds_t, p_t = operands.pop(k)
                dq = _dot(ds, kv[g, par][0])
                dqs[g, i] = dq if par == 0 else dqs[g, i] + dq
                dk = _dot(ds_t, q_ref[2 * g + i])
                dv = _dot(p_t, douts[2 * g + i])
                dk_parts[g, par] = dk if i == 0 else dk_parts[g, par] + dk
                dv_parts[g, par] = dv if i == 0 else dv_parts[g, par] + dv

            for k, (g, par, i) in enumerate(chains):
                h = 4 * g + 2 * i + par
                delta = deltas[2 * g + i][:, par * LANES:(par + 1) * LANES]
                e, at_sink, row_sum = numerators[k]
                inv = 1.0 / (row_sum + at_sink)
                p = e * jnp.tile(inv, (1, 2))
                ds = p * (first[k][1] - jnp.tile(delta, (1, 2)))
                ds = ds.astype(MXU_DTYPE)
                operands[k] = (ds, ds.T, p.astype(MXU_DTYPE).T)
                total = jnp.sum(at_sink * inv * delta, axis=0, keepdims=True)
                gsink = jnp.where(lane_row == h, -total, gsink)
                if k + ahead < len(chains):
                    issue_first(k + ahead)
                if k + 2 < len(chains):
                    issue_row_sums(k + 2)
                if k > 0:
                    issue_last(k - 1)
                if k == SGU_GATES_AFTER_CHAIN:
                    sgu_gates()
                if k == SGU_GRADS_AFTER_CHAIN:
                    sgu_grads()
            issue_last(len(chains) - 1)
            for pair in range(N_PAIRS):
                g, i = divmod(pair, 2)
                dq = dqs[g, i] * SCALE
                lanes = slice(pair * LANES, (pair + 1) * LANES)
                pend_ref[:, lanes] = dq.astype(MXU_DTYPE)
                gbin_ref[:, lanes] += jnp.sum(dq, axis=0, keepdims=True)
            gsink_ref[...] += gsink
            for k, parts in enumerate((dk_parts, dv_parts)):
                masked = {key: jnp.where(low_keys if key[1] == 0 else jnp.logical_not(low_keys), val, 0.0)
                          for key, val in parts.items()}
                both = (masked[0, 0] + masked[1, 1]
                        + pltpu.roll(masked[0, 1] + masked[1, 0], HALF, 1))
                lanes = slice(k * KV_W, (k + 1) * KV_W)
                done = carry_ref[:, lanes] + both[0:BLOCK]
                dp_ref[:, KV0 + k * KV_W:KV0 + (k + 1) * KV_W] = done.astype(MXU_DTYPE)
                carry_ref[:, lanes] = both[BLOCK:]
                gbin_ref[:, KV0 + k * KV_W:KV0 + (k + 1) * KV_W] += jnp.sum(both, axis=0, keepdims=True)
            kp_ref[...] = kc_ref[...]

        @pl.when(n == last)
        def _():
            for hh in range(N_SGU_HEADS):
                gw_ref[hh] = jnp.where(tril, gw_ref[hh], 0.0)
            head_of_lane = lax.broadcasted_iota(jnp.int32, (N_SGU_HEADS, SGU_W), 1) // HEAD_DIM
            select = (head_of_lane == lax.broadcasted_iota(jnp.int32, (N_SGU_HEADS, SGU_W), 0)).astype(F32)
            gb_ref[...] = lax.dot_general(select, gbias_ref[...], NT, precision=lax.Precision.HIGHEST,
                                          preferred_element_type=F32)

    at = lambda n: jnp.minimum(n, last)
    blk = lambda w: pl.BlockSpec((BLOCK, w), lambda n: (at(n), 0))
    tiles = pl.BlockSpec((N_PAIRS, BLOCK, LANES), lambda n: (0, at(n), 0))
    section = lambda k: pl.BlockSpec((None, BLOCK, SGU_W), lambda n: (k, at(n), 0))
    return pl.pallas_call(
        body,
        name="mixers_bwd",
        grid=(N_BLOCKS + 1,),
        in_specs=[pl.BlockSpec(memory_space=pltpu.SMEM),
                  section(0),
                  tiles,
                  blk(KVX_W),
                  tiles,
                  section(0),
                  section(1),
                  section(1), section(2), section(3),
                  _full((1, SGU_W)), _full((1, SGU_W)), _full((N_SGU_HEADS, BLOCK, BLOCK)), _full((BLOCK, SGU_W)),
                  VMEM_SPEC],
        out_specs=(pl.BlockSpec((BLOCK, ATTN_SECTION), lambda n: (jnp.maximum(n - 1, 0), 0)),
                   _full((1, LANES)), _full((1, ATTN_SECTION)),
                   pl.BlockSpec((BLOCK, SGU_SECTION), lambda n: (at(n), 0)),
                   _full((N_SGU_HEADS, BLOCK, BLOCK)), _full((N_SGU_HEADS, BLOCK)),
                   _full((8, SGU_W)), _full((1, SGU_SECTION)), VMEM_SPEC),
        out_shape=(jax.ShapeDtypeStruct((SEQ, ATTN_SECTION), MXU_DTYPE),
                   jax.ShapeDtypeStruct((1, LANES), F32),
                   jax.ShapeDtypeStruct((1, ATTN_SECTION), F32),
                   jax.ShapeDtypeStruct((SEQ, SGU_SECTION), MXU_DTYPE),
                   jax.ShapeDtypeStruct((N_SGU_HEADS, BLOCK, BLOCK), F32),
                   jax.ShapeDtypeStruct((N_SGU_HEADS, BLOCK), F32),
                   jax.ShapeDtypeStruct((8, SGU_W), F32),
                   jax.ShapeDtypeStruct((1, SGU_SECTION), F32),
                   jax.ShapeDtypeStruct((WOUT_ROWS, D_MODEL), F32)),
        scratch_shapes=([pltpu.VMEM((BLOCK, KVX_W), MXU_DTYPE),
                         pltpu.VMEM((BLOCK, 2 * ATTN_W), MXU_DTYPE), pltpu.VMEM((BLOCK, 2 * KV_W), F32),
                         pltpu.VMEM((N_SGU_HEADS, BLOCK, BLOCK), MXU_DTYPE),
                         pltpu.VMEM((N_SGU_HEADS, BLOCK, BLOCK), MXU_DTYPE), pltpu.VMEM((BLOCK, SGU_W), F32)]
                        + _reduce_scatter_scratch(WOUT_ROWS, D_MODEL, COMM_DTYPE) + _dma_sems(REDUCE_SEMS)),
        compiler_params=_params(("arbitrary",), VMEM_LIMIT),
    )(sinks, dmix, q, kvx, out, gates, dmix, gates, gates, gates, ln_g, ln_b, sgu_w, bias_full, gwout)


def _in_proj_bwd(dpa, dps, win_t, x, norm_g, gres, gwin, vec_parts):
    tm = TOKEN_TILE
    steps = SEQ // tm
    n_parts = len(vec_parts)

    def body(da_ref, ds_ref, w_ref, x_ref, g_ref, gres_ref, gwin_ref, *rest):
        part_refs = rest[:n_parts]
        gx_ref, shard_ref, vec_out_ref, gng_ref, sa, ra, sb, rc, vec_ref, ra_vec, slots, send_sems, recv_sems = (
            rest[n_parts:])
        step = pl.program_id(0)
        copies = _Copies(send_sems, recv_sems)
        start, exchange, finish = _reduce_scatter_plan(copies, 0, gwin_ref, WIN_ROWS, sa, ra, sb, rc, shard_ref)

        @pl.when(step == 0)
        def _():
            gng_ref[...] = jnp.zeros_like(gng_ref)
            start()

        pl.when(step == 2)(exchange)

        dh = _dot(da_ref[...], w_ref[0:ATTN_SECTION, :]) + _dot(ds_ref[...], w_ref[ATTN_SECTION:, :])
        xv = x_ref[...]
        r = lax.rsqrt(jnp.mean(xv * xv, axis=-1, keepdims=True) + NORM_EPS)
        xn = xv * r
        gng_ref[...] += jnp.sum(dh * xn, axis=0, keepdims=True)
        dxn = dh * g_ref[...]
        gx_ref[...] = r * (dxn - xn * jnp.mean(dxn * xn, axis=-1, keepdims=True)) + gres_ref[...]

        @pl.when(step == steps - 1)
        def _():
            finish()
            _all_reduce_vectors(copies, REDUCE_SEMS, gng_ref, *part_refs, vec_out_ref, vec_ref, ra_vec, slots)

    tile = lambda w: pl.BlockSpec((tm, w), lambda i: (i, 0))
    return pl.pallas_call(
        body,
        name="in_proj_bwd",
        grid=(steps,),
        in_specs=[tile(ATTN_SECTION), tile(SGU_SECTION), _full((IN_W, D_MODEL)), tile(D_MODEL),
                  _full((1, D_MODEL)), tile(D_MODEL), VMEM_SPEC] + [VMEM_SPEC] * n_parts,
        out_specs=(tile(D_MODEL), VMEM_SPEC, VMEM_SPEC),
        out_shape=(jax.ShapeDtypeStruct((SEQ, D_MODEL), F32),
                   jax.ShapeDtypeStruct((WIN_ROWS, D_MODEL), F32),
                   jax.ShapeDtypeStruct((VEC_ROWS, IN_W), F32)),
        scratch_shapes=([pltpu.VMEM((1, D_MODEL), F32)] + _reduce_scatter_scratch(WIN_ROWS, D_MODEL, COMM_DTYPE)
                        + _vector_scratch() + _dma_sems(REDUCE_SEMS + VECTOR_SEMS)),
        compiler_params=_params(("arbitrary",), VMEM_LIMIT),
    )(dpa, dps, win_t, x, norm_g, gres, gwin, *vec_parts)


def _win_grad(dpa, dps, h, gsguw):
    rows = 256
    n_attn = ATTN_SECTION // rows
    steps = n_attn + SGU_SECTION // rows

    def body(da_ref, ds_ref, h_ref, gsguw_ref, o_ref, sguw_full_ref, sa, ra, sb, rc, landing, send_sems, recv_sems):
        step = pl.program_id(0)
        copies = _Copies(send_sems, recv_sems)
        own_sguw = landing.at[_block_rows(_place(), SGUW_ROWS), :]
        start, exchange, finish = _reduce_scatter_plan(copies, 0, gsguw_ref, SGUW_ROWS, sa, ra, sb, rc, own_sguw)
        gather = _gather_plan(copies, REDUCE_SEMS, landing, SGUW_ROWS)

        pl.when(step == 0)(start)
        pl.when(step == 2)(exchange)

        @pl.when(step == 5)
        def _():
            finish()
            gather[0]()

        pl.when(step == 7)(gather[1])

        @pl.when(step < n_attn)
        def _():
            o_ref[...] = _dot(da_ref[...], h_ref[...], TN)

        @pl.when(step >= n_attn)
        def _():
            o_ref[...] = _dot(ds_ref[...], h_ref[...], TN)

        @pl.when(step == steps - 1)
        def _():
            gather[2]()
            sguw_full_ref[...] = landing[...]

    return pl.pallas_call(
        body,
        name="win_grad",
        grid=(steps,),
        in_specs=[pl.BlockSpec((SEQ, rows), lambda i: (0, jnp.minimum(i, n_attn - 1))),
                  pl.BlockSpec((SEQ, rows), lambda i: (0, jnp.maximum(i - n_attn, 0))),
                  _full((SEQ, D_MODEL)), VMEM_SPEC],
        out_specs=(pl.BlockSpec((rows, D_MODEL), lambda i: (i, 0)), _full((N_SGU_HEADS * BLOCK, BLOCK))),
        out_shape=(jax.ShapeDtypeStruct((IN_W, D_MODEL), F32),
                   jax.ShapeDtypeStruct((N_SGU_HEADS * BLOCK, BLOCK), F32)),
        scratch_shapes=(_reduce_scatter_scratch(SGUW_ROWS, BLOCK, F32)
                        + [pltpu.VMEM((N_SGU_HEADS * BLOCK, BLOCK), F32)]
                        + _dma_sems(REDUCE_SEMS + GATHER_SEMS)),
        compiler_params=_params(("arbitrary",), VMEM_LIMIT),
    )(dpa, dps, h, gsguw)


VEC_NORM_G, VEC_B_IN, VEC_SINKS, VEC_LN_G, VEC_LN_B, VEC_B_OUT, VEC_FINAL_G, VEC_LOSS, VEC_SGU_B = 0, 1, 2, 3, 4, 5, 6, 7, 8


def _adamw(w, g, m, v):
    m = ADAM_B1 * m + (1.0 - ADAM_B1) * g
    v = ADAM_B2 * v + (1.0 - ADAM_B2) * (g * g)
    m_hat = m / (1.0 - ADAM_B1 ** ADAM_STEP)
    v_hat = v / (1.0 - ADAM_B2 ** ADAM_STEP)
    delta = -ADAM_LR * (m_hat / (jnp.sqrt(v_hat) + ADAM_EPS) + ADAM_WD * w)
    return delta, m, v


def _adamw_shard(name, g, w, m, v, block_rows):
    def body(g_ref, w_ref, m_ref, v_ref, d_ref, nm_ref, nv_ref):
        d_ref[...], nm_ref[...], nv_ref[...] = _adamw(w_ref[...], g_ref[...], m_ref[...], v_ref[...])

    rows, cols = w.shape
    spec = pl.BlockSpec((block_rows, cols), lambda i: (i, 0))
    return pl.pallas_call(
        body,
        name=name,
        grid=(rows // block_rows,),
        in_specs=[spec] * 4,
        out_specs=(spec,) * 3,
        out_shape=(jax.ShapeDtypeStruct(w.shape, F32),) * 3,
        compiler_params=_params(("arbitrary",)),
    )(g, w, m, v)


VECTOR_SEMS = 4


def _vector_scratch():
    return [pltpu.VMEM((VEC_ROWS, IN_W), F32), pltpu.VMEM((VEC_ROWS, IN_W), F32),
            pltpu.VMEM((4 * VEC_ROWS, IN_W), F32)]


def _all_reduce_vectors(copies, sem0, gng_ref, gba_ref, gbs_ref, gsink_ref, gln_ref, gsgub_ref, vec4_ref, out_ref,
                        vec_ref, ra_vec, slots):
    x, y, c = _place()
    vec_ref[...] = jnp.zeros_like(vec_ref)
    vec_ref[VEC_NORM_G:VEC_NORM_G + 1, 0:D_MODEL] = gng_ref[...]
    vec_ref[VEC_B_IN:VEC_B_IN + 1, 0:ATTN_SECTION] = gba_ref[...]
    vec_ref[VEC_B_IN:VEC_B_IN + 1, ATTN_SECTION:IN_W] = gbs_ref[...]
    vec_ref[VEC_SINKS:VEC_SINKS + 1, 0:LANES] = gsink_ref[...]
    vec_ref[VEC_LN_G:VEC_LN_G + 1, 0:SGU_W] = gln_ref[0:1, :]
    vec_ref[VEC_LN_B:VEC_LN_B + 1, 0:SGU_W] = gln_ref[1:2, :]
    vec_ref[VEC_B_OUT:VEC_B_OUT + 1, 0:D_MODEL] = vec4_ref[2:3, :]
    vec_ref[VEC_FINAL_G:VEC_FINAL_G + 1, 0:D_MODEL] = vec4_ref[1:2, :]
    vec_ref[VEC_LOSS:VEC_LOSS + 1, 0:D_MODEL] = vec4_ref[0:1, :]
    vec_ref[VEC_SGU_B:VEC_SGU_B + N_SGU_HEADS, 0:BLOCK] = gsgub_ref[...]

    to_sibling = copies(sem0, vec_ref, ra_vec, (x, y, 1 - c))
    to_sibling.start()
    to_sibling.wait_recv()

    def chip_slot(place):
        return slots.at[pl.ds(pl.multiple_of((2 * place[0] + place[1]) * VEC_ROWS, 8), VEC_ROWS), :]

    mine = chip_slot((x, y))
    mine[...] = vec_ref[...] + ra_vec[...]
    to_chips = [copies(sem0 + i, mine, mine, (*_chip(rel), c)) for i, rel in enumerate(RELATIONS[1:], start=1)]
    for cp in to_chips:
        cp.start()
    for i, rel in enumerate(RELATIONS[1:], start=1):
        theirs = chip_slot(_chip(rel))
        copies(sem0 + i, theirs, theirs, (x, y, c)).wait_recv()
    out_ref[...] = ((slots[0:VEC_ROWS, :] + slots[VEC_ROWS:2 * VEC_ROWS, :])
                    + slots[2 * VEC_ROWS:3 * VEC_ROWS, :]) + slots[3 * VEC_ROWS:, :]
    to_sibling.wait_send()
    for cp in to_chips:
        cp.wait_send()


def _adamw_replicated(vec, gsguw, weights, m_state, v_state):
    n = len(SMALL)

    def body(*refs):
        vec_ref, gsguw_ref = refs[0], refs[1]
        w_refs, m_refs, v_refs = (refs[2 + k * n:2 + (k + 1) * n] for k in range(3))
        outs = refs[2 + 3 * n:]
        g_refs, d_refs, nm_refs, nv_refs = (outs[k * n:(k + 1) * n] for k in range(4))
        for i, (_, row, shape) in enumerate(SMALL):
            g = gsguw_ref[...] if row is None else vec_ref[row:row + shape[0], 0:shape[1]]
            g_refs[i][...] = g
            d_refs[i][...], nm_refs[i][...], nv_refs[i][...] = _adamw(
                w_refs[i][...], g, m_refs[i][...], v_refs[i][...])

    shapes = tuple(jax.ShapeDtypeStruct(shape, F32) for _, _, shape in SMALL)
    outs = pl.pallas_call(
        body,
        name="adamw_replicated",
        in_specs=[VMEM_SPEC] * (2 + 3 * n),
        out_specs=(VMEM_SPEC,) * (4 * n),
        out_shape=shapes * 4,
    )(vec, gsguw, *weights, *m_state, *v_state)
    return tuple(outs[k * n:(k + 1) * n] for k in range(4))


SMALL = (
    ("norm_g", VEC_NORM_G, (1, D_MODEL)),
    ("b_in", VEC_B_IN, (1, IN_W)),
    ("attn_sinks", VEC_SINKS, (1, N_Q_HEADS)),
    ("sgu_ln_g", VEC_LN_G, (1, SGU_W)),
    ("sgu_ln_b", VEC_LN_B, (1, SGU_W)),
    ("sgu_w", None, (N_SGU_HEADS * BLOCK, BLOCK)),
    ("sgu_b", VEC_SGU_B, (N_SGU_HEADS, BLOCK)),
    ("b_out", VEC_B_OUT, (1, D_MODEL)),
    ("final_norm_g", VEC_FINAL_G, (1, D_MODEL)),
)


def _local_grads(x, target, h, win_t, wout_shard, norm_g, b_in, attn_sinks, sgu_ln_g, sgu_ln_b, sgu_w, sgu_b, b_out,
                 final_g):
    sinks = attn_sinks.reshape(N_Q_HEADS)
    bias_full = jnp.repeat(sgu_b.T, HEAD_DIM, axis=1)
    q, kvx, gates, wout = _in_proj(h, b_in, win_t, wout_shard)
    out, gres, dmix, gwout, vec4 = _mixers_out_proj(sinks, q, kvx, gates, sgu_ln_g, sgu_ln_b, sgu_w, bias_full,
                                                    x, target, wout, b_out, final_g)
    dpa, gsink, gbin_a, dps, gsguw, gsgub, gln, gbin_s, gwout_shard = _mixers_bwd(
        sinks, dmix, q, kvx, out, gates, sgu_ln_g, sgu_ln_b, sgu_w, bias_full, gwout)
    gwin, gsguw_sum = _win_grad(dpa, dps, h, gsguw.reshape(N_SGU_HEADS * BLOCK, BLOCK))
    grad_x, gwin_shard, vec = _in_proj_bwd(dpa, dps, win_t, x, norm_g, gres, gwin,
                                           (gbin_a, gbin_s, gsink, gln, gsgub, vec4))
    return grad_x, gwin_shard, gwout_shard, gsguw_sum, vec


def kernel(x, norm_g, w_in, b_in, attn_sinks, sgu_ln_g, sgu_ln_b, sgu_w, sgu_b, w_out, b_out, final_norm_g, loss_target, m_norm_g, m_w_in, m_b_in, m_attn_sinks, m_sgu_ln_g, m_sgu_ln_b, m_sgu_w, m_sgu_b, m_w_out, m_b_out, m_final_norm_g, v_norm_g, v_w_in, v_b_in, v_attn_sinks, v_sgu_ln_g, v_sgu_ln_b, v_sgu_w, v_sgu_b, v_w_out, v_b_out, v_final_norm_g):
    given = dict(norm_g=norm_g, b_in=b_in, attn_sinks=attn_sinks, sgu_ln_g=sgu_ln_g, sgu_ln_b=sgu_ln_b,
                 sgu_w=sgu_w, sgu_b=sgu_b, b_out=b_out, final_norm_g=final_norm_g)
    m_given = dict(norm_g=m_norm_g, b_in=m_b_in, attn_sinks=m_attn_sinks, sgu_ln_g=m_sgu_ln_g,
                   sgu_ln_b=m_sgu_ln_b, sgu_w=m_sgu_w, sgu_b=m_sgu_b, b_out=m_b_out, final_norm_g=m_final_norm_g)
    v_given = dict(norm_g=v_norm_g, b_in=v_b_in, attn_sinks=v_attn_sinks, sgu_ln_g=v_sgu_ln_g,
                   sgu_ln_b=v_sgu_ln_b, sgu_w=v_sgu_w, sgu_b=v_sgu_b, b_out=v_b_out, final_norm_g=v_final_norm_g)

    win_t, h = _all_gather_win(w_in[0].T, x[0], norm_g)
    grad_x, gwin_t, gwout, gsguw, vec = _local_grads(
        x[0], loss_target[0], h, win_t, w_out[0], norm_g, b_in, attn_sinks, sgu_ln_g, sgu_ln_b, sgu_w[0], sgu_b[0],
        b_out, final_norm_g.reshape(1, D_MODEL))

    t = lambda a: a[0].T
    d_win, nm_win, nv_win = _adamw_shard("adamw_w_in", gwin_t, t(w_in), t(m_w_in), t(v_w_in), WIN_ROWS // 2)
    d_wout, nm_wout, nv_wout = _adamw_shard("adamw_w_out", gwout, w_out[0], m_w_out[0], v_w_out[0], WOUT_ROWS)
    as_2d = lambda d: [d[name].reshape(shape) for name, _, shape in SMALL]
    loss = vec[VEC_LOSS, 0]
    small = _adamw_replicated(vec, gsguw, as_2d(given), as_2d(m_given), as_2d(v_given))

    def assemble(big_in, big_out, k):
        vals = {name: small[k][i].reshape(given[name].shape) for i, (name, _, _) in enumerate(SMALL)}
        vals["w_in"] = big_in.T[None]
        vals["w_out"] = big_out[None]
        order = ("norm_g", "w_in", "b_in", "attn_sinks", "sgu_ln_g", "sgu_ln_b", "sgu_w", "sgu_b", "w_out",
                 "b_out", "final_norm_g")
        return [vals[name] for name in order]

    return (loss, grad_x[None],
            *assemble(gwin_t, gwout, 0), *assemble(d_win, d_wout, 1),
            *assemble(nm_win, nm_wout, 2), *assemble(nv_win, nv_wout, 3))
```

```python
import functools
import math

import jax
import jax.numpy as jnp
from jax import lax
from jax.experimental import pallas as pl
from jax.experimental.pallas import tpu as pltpu

F32 = jnp.float32
BF16 = jnp.bfloat16
MXU_DTYPE = BF16
COMM_DTYPE = BF16

D_MODEL = 1024
SEQ = 4096
HEAD_DIM = 64
N_Q_HEADS = 8
Q_PER_KV = 4
BLOCK = 128
N_BLOCKS = SEQ // BLOCK
ATTN_W = 512
KV_W = 128
SGU_W = 512
N_SGU_HEADS = 8
IN_W = 2816
NORM_EPS = 1e-5
NEG_INF = -1e30
SCALE = HEAD_DIM ** -0.5
KV0 = ATTN_W
GATE0 = ATTN_W + 2 * KV_W
SGU0 = GATE0 + ATTN_W
ATTN_SECTION = SGU0
SGU_SECTION = IN_W - SGU0

ADAM_LR = 0.001
ADAM_B1 = 0.9
ADAM_B2 = 0.999
ADAM_EPS = 1e-08
ADAM_WD = 0.01
ADAM_STEP = 10

N_DEV = 8
WIN_ROWS = IN_W // N_DEV
WOUT_ROWS = D_MODEL // N_DEV
SGUW_ROWS = N_SGU_HEADS * BLOCK // N_DEV
VEC_ROWS = 16
MESH = pl.DeviceIdType.MESH

LANES = 128
HALF = LANES // 2
N_PAIRS = N_Q_HEADS * HEAD_DIM // LANES
KVX_W = 12 * LANES
TOKEN_TILE = 256
FWD_TOKEN_TILE = 512
ATTN_FWD_AHEAD = 4
FUSED_BLOCKS = 2
SGU_MIX_AFTER_CHAIN = 0
SGU_GATES_AFTER_CHAIN = 1
SGU_GRADS_AFTER_CHAIN = 5
ATTN_BWD_AHEAD = 3
VMEM_LIMIT = 56 * 1024 * 1024

NN = (((1,), (0,)), ((), ()))
NT = (((1,), (1,)), ((), ()))
TN = (((0,), (0,)), ((), ()))


def _dot(a, b, dims=NN):
    return lax.dot_general(a.astype(MXU_DTYPE), b.astype(MXU_DTYPE), dims, preferred_element_type=F32)


def _gelu(x):
    return x * (lax.erf(x * (1.0 / math.sqrt(2.0))) + 1.0) * 0.5


def _gelu_grad(x):
    cdf = (lax.erf(x * (1.0 / math.sqrt(2.0))) + 1.0) * 0.5
    return cdf + x * jnp.exp(-0.5 * x * x) * (1.0 / math.sqrt(2.0 * math.pi))


def _silu_and_grad(z):
    s = jax.nn.sigmoid(z)
    return z * s, s * (1.0 + z * (1.0 - s))


def _params(semantics=None, vmem=None):
    kw = {}
    if semantics is not None:
        kw["dimension_semantics"] = semantics
    if vmem is not None:
        kw["vmem_limit_bytes"] = vmem
    return pltpu.CompilerParams(**kw)


def _full(shape):
    return pl.BlockSpec(shape, lambda *_: (0,) * len(shape))


VMEM_SPEC = pl.BlockSpec(memory_space=pltpu.VMEM)


RELATIONS = ((0, 0), (1, 0), (0, 1), (1, 1))


def _place():
    return lax.axis_index("x"), lax.axis_index("y"), lax.axis_index("c")


def _chip(rel):
    x, y, _ = _place()
    return (1 - x if rel[0] else x, 1 - y if rel[1] else y)


def _block_rows(place, n_rows):
    px, py, pc = place
    return pl.ds(pl.multiple_of((4 * px + 2 * py + pc) * n_rows, 16), n_rows)


class _Copies:
    def __init__(self, send_sems, recv_sems):
        self.send_sems, self.recv_sems = send_sems, recv_sems

    def __call__(self, k, src, dst, to):
        return pltpu.make_async_remote_copy(src_ref=src, dst_ref=dst, send_sem=self.send_sems.at[k],
                                            recv_sem=self.recv_sems.at[k], device_id=to, device_id_type=MESH)


def _gather_plan(copies, sem0, full_ref, n_rows):
    x, y, c = _place()
    me, sibling = (x, y, c), (x, y, 1 - c)
    chips = [_chip(rel) for rel in RELATIONS[1:]]

    def cp(k, block, to):
        rows = full_ref.at[_block_rows(block, n_rows), :]
        return copies(sem0 + k, rows, rows, to)

    first = [cp(0, me, sibling)] + [cp(1 + j, me, (*chip, c)) for j, chip in enumerate(chips)]
    passed = [cp(4 + j, (*chip, c), sibling) for j, chip in enumerate(chips)]

    def start():
        for f in first:
            f.start()

    def forward():
        for j, chip in enumerate(chips):
            cp(1 + j, (*chip, c), me).wait_recv()
            passed[j].start()

    def finish():
        cp(0, sibling, me).wait_recv()
        for j, chip in enumerate(chips):
            cp(4 + j, (*chip, 1 - c), me).wait_recv()
        for f in first + passed:
            f.wait_send()

    return start, forward, finish


GATHER_SEMS = 7


def _reduce_scatter_plan(copies, sem0, part_ref, n_rows, sa, ra, sb, rc, res_ref):
    x, y, c = _place()
    sibling = (x, y, 1 - c)
    n = n_rows
    level1 = copies(sem0, sa, ra, sibling)

    def level2(i):
        slot = pl.ds((i - 1) * n, n)
        return copies(sem0 + i, sb.at[slot, :], rc.at[slot, :], (*_chip(RELATIONS[i]), c))

    def start():
        for i, rel in enumerate(RELATIONS):
            sa[i * n:(i + 1) * n, :] = part_ref[_block_rows((*_chip(rel), 1 - c), n), :].astype(sa.dtype)
        level1.start()

    def exchange():
        level1.wait_recv()
        for i, rel in enumerate(RELATIONS):
            total = part_ref[_block_rows((*_chip(rel), c), n), :] + ra[i * n:(i + 1) * n, :].astype(F32)
            if i == 0:
                res_ref[...] = total
            else:
                sb[(i - 1) * n:i * n, :] = total.astype(sb.dtype)
                level2(i).start()

    def finish():
        acc = res_ref[...]
        for i in range(1, len(RELATIONS)):
            level2(i).wait_recv()
            acc = acc + rc[(i - 1) * n:i * n, :].astype(F32)
        res_ref[...] = acc
        level1.wait_send()
        for i in range(1, len(RELATIONS)):
            level2(i).wait_send()

    return start, exchange, finish


REDUCE_SEMS = 4


def _reduce_scatter_scratch(n_rows, width, dtype):
    return [pltpu.VMEM((4 * n_rows, width), dtype), pltpu.VMEM((4 * n_rows, width), dtype),
            pltpu.VMEM((3 * n_rows, width), dtype), pltpu.VMEM((3 * n_rows, width), dtype)]


def _dma_sems(n):
    return [pltpu.SemaphoreType.DMA((n,)), pltpu.SemaphoreType.DMA((n,))]


def _all_gather_win(win_t_shard, x, norm_g):
    tm = FWD_TOKEN_TILE
    steps = SEQ // tm

    def body(win_ref, x_ref, g_ref, full_ref, h_ref, landing, send_sems, recv_sems):
        step = pl.program_id(0)
        start, forward, finish = _gather_plan(_Copies(send_sems, recv_sems), 0, landing, WIN_ROWS)

        @pl.when(step == 0)
        def _():
            landing[_block_rows(_place(), WIN_ROWS), :] = win_ref[...].astype(COMM_DTYPE)
            start()

        xv = x_ref[...]
        r = lax.rsqrt(jnp.mean(xv * xv, axis=-1, keepdims=True) + NORM_EPS)
        h_ref[...] = ((xv * r) * g_ref[...]).astype(MXU_DTYPE)

        @pl.when(step == steps - 1)
        def _():
            forward()
            finish()
            full_ref[...] = landing[...]

    return pl.pallas_call(
        body,
        name="all_gather_win",
        grid=(steps,),
        in_specs=[VMEM_SPEC, pl.BlockSpec((tm, D_MODEL), lambda i: (i, 0)), _full((1, D_MODEL))],
        out_specs=(_full((IN_W, D_MODEL)), pl.BlockSpec((tm, D_MODEL), lambda i: (i, 0))),
        out_shape=(jax.ShapeDtypeStruct((IN_W, D_MODEL), COMM_DTYPE),
                   jax.ShapeDtypeStruct((SEQ, D_MODEL), MXU_DTYPE)),
        scratch_shapes=[pltpu.VMEM((IN_W, D_MODEL), COMM_DTYPE)] + _dma_sems(GATHER_SEMS),
        compiler_params=_params(("arbitrary",), VMEM_LIMIT),
    )(win_t_shard, x, norm_g)


def _in_proj(h, b_in, win_t, wout_shard):
    tm = FWD_TOKEN_TILE
    steps = SEQ // tm

    def body(h_ref, b_ref, w_ref, wout_ref, q_ref, kvx_ref, gate_ref, wfull_ref, landing, send_sems, recv_sems):
        step = pl.program_id(0)
        start, forward, finish = _gather_plan(_Copies(send_sems, recv_sems), 0, landing, WOUT_ROWS)

        @pl.when(step == 0)
        def _():
            landing[_block_rows(_place(), WOUT_ROWS), :] = wout_ref[...].astype(COMM_DTYPE)
            start()

        pl.when(step == steps // 2)(forward)

        h = h_ref[...]

        def proj(lo, hi):
            return _dot(h, w_ref[lo:hi, :], NT) + b_ref[:, lo:hi]

        qs = proj(0, ATTN_W) * SCALE
        for pair in range(N_PAIRS):
            q_ref[pair] = qs[:, pair * LANES:(pair + 1) * LANES].astype(MXU_DTYPE)
        kv = proj(KV0, GATE0)
        low = lax.broadcasted_iota(jnp.int32, (tm, LANES), 1) < HALF
        for i in range(2):
            t = kv[:, i * LANES:(i + 1) * LANES]
            rot = pltpu.roll(t, HALF, 1)
            variants = (jnp.where(low, t, 0.0), jnp.where(low, 0.0, rot),
                        jnp.where(low, rot, 0.0), jnp.where(low, 0.0, t))
            for j, val in enumerate(variants):
                col = (4 * i + j) * LANES
                kvx_ref[:, col:col + LANES] = val.astype(MXU_DTYPE)
                if i == 1:
                    ones_elsewhere = jnp.where(low == (j % 2 == 0), val, 1.0)
                    kvx_ref[:, col + 4 * LANES:col + 5 * LANES] = ones_elsewhere.astype(MXU_DTYPE)
        for k in range(4):
            gate_ref[k] = proj(GATE0 + k * SGU_W, GATE0 + (k + 1) * SGU_W)

        @pl.when(step == steps - 1)
        def _():
            finish()
            wfull_ref[...] = landing[...]

    return pl.pallas_call(
        body,
        name="in_proj",
        grid=(steps,),
        in_specs=[pl.BlockSpec((tm, D_MODEL), lambda i: (i, 0)),
                  _full((1, IN_W)), _full((IN_W, D_MODEL)), VMEM_SPEC],
        out_specs=(pl.BlockSpec((N_PAIRS, tm, LANES), lambda i: (0, i, 0)),
                   pl.BlockSpec((tm, KVX_W), lambda i: (i, 0)),
                   pl.BlockSpec((4, tm, SGU_W), lambda i: (0, i, 0)),
                   _full((D_MODEL, D_MODEL))),
        out_shape=(jax.ShapeDtypeStruct((N_PAIRS, SEQ, LANES), MXU_DTYPE),
                   jax.ShapeDtypeStruct((SEQ, KVX_W), MXU_DTYPE),
                   jax.ShapeDtypeStruct((4, SEQ, SGU_W), F32),
                   jax.ShapeDtypeStruct((D_MODEL, D_MODEL), COMM_DTYPE)),
        scratch_shapes=[pltpu.VMEM((D_MODEL, D_MODEL), COMM_DTYPE)] + _dma_sems(GATHER_SEMS),
        compiler_params=_params(("arbitrary",), VMEM_LIMIT),
    )(h, b_in, win_t, wout_shard)


def _window_mask(n):
    qi = lax.broadcasted_iota(jnp.int32, (2 * BLOCK, 2 * BLOCK), 0) & (BLOCK - 1)
    p = lax.broadcasted_iota(jnp.int32, (2 * BLOCK, 2 * BLOCK), 1) - BLOCK
    in_window = jnp.logical_and(p <= qi, p > qi - BLOCK)
    return jnp.logical_and(in_window, jnp.logical_or(p >= 0, n > 0))


def _sink_column(sink_ref, g, par):
    return jnp.concatenate([jnp.full((BLOCK, 1), sink_ref[4 * g + par], F32),
                            jnp.full((BLOCK, 1), sink_ref[4 * g + 2 + par], F32)], axis=0)


def _kv_cat(kp_ref, kc_ref, var, with_ones):
    kcol, vcol = var * LANES, (var + (8 if with_ones else 4)) * LANES
    return (jnp.concatenate([kp_ref[:, kcol:kcol + LANES], kc_ref[:, kcol:kcol + LANES]], axis=0),
            jnp.concatenate([kp_ref[:, vcol:vcol + LANES], kc_ref[:, vcol:vcol + LANES]], axis=0))


def _softmax_numerator(s, sink):
    m = jnp.maximum(jnp.max(s, axis=1, keepdims=True), sink)
    return jnp.exp(s - m), m


def _mixers_out_proj(sinks, q, kvx, gates, ln_g, ln_b, sgu_w, bias_full, x, target, wout, b_out, final_g):
    tm = FUSED_BLOCKS * BLOCK
    n_tiles = SEQ // tm

    def body(sink_ref, q_ref, kc_ref, za_ref, us_ref, vs_ref, zs_ref, lng_ref, lnb_ref, w_ref, bias_ref,
             x_ref, t_ref, wout_ref, b_ref, gf_ref,
             out_ref, gres_ref, dmix_ref, gw_ref, vec_ref,
             kp_ref, wm_ref, mixed_next, mixed_cur, out_stage, gb_ref):
        step = pl.program_id(0)

        @pl.when(step == 0)
        def _():
            kp_ref[...] = jnp.zeros_like(kp_ref)
            _mask_sgu_weights(w_ref, wm_ref)
            gw_ref[...] = jnp.zeros_like(gw_ref)
            vec_ref[...] = jnp.zeros_like(vec_ref)
            mixed_cur[...] = jnp.zeros_like(mixed_cur)

        def mixers_block(b, after_chain=()):
            rows = slice(b * BLOCK, (b + 1) * BLOCK)
            kc = kc_ref.at[rows, :]
            u, _, _, vln = _sgu_activations(us_ref[rows, :], vs_ref[rows, :], lng_ref[...], lnb_ref[...])

            valid = _window_mask(step * FUSED_BLOCKS + b)[0:BLOCK]
            chains = [(g, par, i) for g in range(2) for par in range(2) for i in range(2)]
            kv = {(g, par): _kv_cat(kp_ref, kc, 2 * g + par, True) for g in range(2) for par in range(2)}
            scores, outs = {}, {}

            def issue_scores(k):
                g, par, i = chains[k]
                scores[k] = _dot(q_ref[2 * g + i, rows, :], kv[g, par][0], NT)

            ahead = ATTN_FWD_AHEAD
            for k in range(ahead):
                issue_scores(k)
            low = lax.broadcasted_iota(jnp.int32, (BLOCK, LANES), 1) < HALF
            for k, (g, par, i) in enumerate(chains):
                sink = sink_ref[4 * g + 2 * i + par]
                e, m = _softmax_numerator(jnp.where(valid, scores[k], NEG_INF), sink)
                if k + ahead < len(chains):
                    issue_scores(k + ahead)
                o = _dot(e, kv[g, par][1])
                outs[g, par, i] = o / (pltpu.roll(o, HALF, 1) + jnp.exp(sink - m))
                if k == SGU_MIX_AFTER_CHAIN:
                    mixed = _sgu_mix(vln, wm_ref, bias_ref)
                if k % 2 == 0 and k // 2 < len(after_chain):
                    after_chain[k // 2]()
            for pair in range(N_PAIRS):
                g, i = divmod(pair, 2)
                lanes = slice(pair * LANES, (pair + 1) * LANES)
                o = jnp.where(low, outs[g, 0, i], outs[g, 1, i])
                out_stage[pair, rows, :] = o
                gate, _ = _silu_and_grad(za_ref[rows, lanes])
                mixed_next[rows, lanes] = (o * gate).astype(MXU_DTYPE)
            kp_ref[...] = kc[...]
            for pair in range(N_SGU_HEADS // 2):
                cols = slice(pair * LANES, (pair + 1) * LANES)
                gate, _ = _silu_and_grad(zs_ref[rows, cols])
                mixed_next[rows, ATTN_W + pair * LANES:ATTN_W + (pair + 1) * LANES] = (
                    u[:, cols] * mixed[pair] * gate).astype(MXU_DTYPE)

        live = (step > 0).astype(F32)
        quarter = D_MODEL // 4
        columns = [None] * 4

        def project(j):
            def piece():
                columns[j] = _dot(mixed_cur[...], wout_ref[:, j * quarter:(j + 1) * quarter])
            return piece

        mixers_block(0, [project(j) for j in range(4)])
        xo = x_ref[...] + jnp.concatenate(columns, axis=1) + b_ref[...]
        r = lax.rsqrt(jnp.mean(xo * xo, axis=-1, keepdims=True) + NORM_EPS)
        xn = xo * r
        gf = gf_ref[...]
        err = xn * gf - t_ref[...]
        loss = 0.5 * jnp.sum(jnp.mean(err * err, axis=-1, keepdims=True), axis=0, keepdims=True)
        dy = err * (1.0 / D_MODEL)
        dxn = dy * gf
        gres = r * (dxn - xn * jnp.mean(dxn * xn, axis=-1, keepdims=True))
        vec_ref[0:1, :] += jnp.broadcast_to(loss * live, (1, D_MODEL))
        vec_ref[1:2, :] += jnp.sum(dy * xn, axis=0, keepdims=True) * live
        vec_ref[2:3, :] += jnp.sum(gres, axis=0, keepdims=True) * live
        gres_ref[...] = gres
        gb_ref[...] = gres.astype(MXU_DTYPE)

        def branch_grad(k):
            def piece():
                dmix_ref[k] = _dot(gb_ref[...], wout_ref[k * ATTN_W:(k + 1) * ATTN_W, :], NT)
            return piece

        def weight_grad(k):
            def piece():
                rows = slice(k * ATTN_W, (k + 1) * ATTN_W)
                gw_ref[rows, :] += _dot(mixed_cur[:, rows], gb_ref[...], TN)
            return piece

        assert FUSED_BLOCKS == 2
        mixers_block(1, [branch_grad(0), branch_grad(1), weight_grad(0), weight_grad(1)])

        @pl.when(step < n_tiles)
        def _():
            out_ref[...] = out_stage[...]

        mixed_cur[...] = mixed_next[...]

    ahead_tile = lambda i: jnp.minimum(i, n_tiles - 1)
    behind_tile = lambda i: jnp.maximum(i - 1, 0)
    blk = lambda w: pl.BlockSpec((tm, w), lambda i: (ahead_tile(i), 0))
    tiles = pl.BlockSpec((N_PAIRS, tm, LANES), lambda i: (0, ahead_tile(i), 0))
    gate = lambda k: pl.BlockSpec((None, tm, SGU_W), lambda i: (k, ahead_tile(i), 0))
    behind = lambda w: pl.BlockSpec((tm, w), lambda i: (behind_tile(i), 0))
    return pl.pallas_call(
        body,
        name="mixers_out_proj",
        grid=(n_tiles + 1,),
        in_specs=[pl.BlockSpec(memory_space=pltpu.SMEM), tiles, blk(KVX_W), gate(0), gate(1), gate(2), gate(3),
                  _full((1, SGU_W)), _full((1, SGU_W)), _full((N_SGU_HEADS, BLOCK, BLOCK)), _full((BLOCK, SGU_W)),
                  behind(D_MODEL), behind(D_MODEL), _full((D_MODEL, D_MODEL)), _full((1, D_MODEL)),
                  _full((1, D_MODEL))],
        out_specs=(tiles, behind(D_MODEL), pl.BlockSpec((2, tm, ATTN_W), lambda i: (0, behind_tile(i), 0)),
                   _full((D_MODEL, D_MODEL)), _full((8, D_MODEL))),
        out_shape=(jax.ShapeDtypeStruct((N_PAIRS, SEQ, LANES), F32),
                   jax.ShapeDtypeStruct((SEQ, D_MODEL), F32),
                   jax.ShapeDtypeStruct((2, SEQ, ATTN_W), F32),
                   jax.ShapeDtypeStruct((D_MODEL, D_MODEL), F32),
                   jax.ShapeDtypeStruct((8, D_MODEL), F32)),
        scratch_shapes=[pltpu.VMEM((BLOCK, KVX_W), MXU_DTYPE), pltpu.VMEM((N_SGU_HEADS, BLOCK, BLOCK), MXU_DTYPE),
                        pltpu.VMEM((tm, D_MODEL), MXU_DTYPE), pltpu.VMEM((tm, D_MODEL), MXU_DTYPE),
                        pltpu.VMEM((N_PAIRS, tm, LANES), F32), pltpu.VMEM((tm, D_MODEL), MXU_DTYPE)],
        compiler_params=_params(("arbitrary",), VMEM_LIMIT),
    )(sinks, q, kvx, gates, gates, gates, gates, ln_g, ln_b, sgu_w, bias_full, x, target, wout, b_out, final_g)


def _sgu_activations(us, vs, lng, lnb):
    u = _gelu(us)
    vg = _gelu(vs)
    mu = jnp.mean(vg, axis=-1, keepdims=True)
    xc = vg - mu
    rstd = lax.rsqrt(jnp.mean(xc * xc, axis=-1, keepdims=True) + NORM_EPS)
    vhat = xc * rstd
    return u, vhat, rstd, vhat * lng + lnb


def _mask_sgu_weights(w_ref, masked_ref, transposed_ref=None):
    tril = (lax.broadcasted_iota(jnp.int32, (BLOCK, BLOCK), 0)
            >= lax.broadcasted_iota(jnp.int32, (BLOCK, BLOCK), 1))
    for hh in range(N_SGU_HEADS):
        w = jnp.where(tril, w_ref[hh], 0.0)
        masked_ref[hh] = w.astype(MXU_DTYPE)
        if transposed_ref is not None:
            transposed_ref[hh] = w.T.astype(MXU_DTYPE)


def _sgu_mix(vln, masked_w_ref, bias_ref):
    low = lax.broadcasted_iota(jnp.int32, (BLOCK, LANES), 1) < HALF
    mixed = []
    for pair in range(N_SGU_HEADS // 2):
        vp = vln[:, pair * LANES:(pair + 1) * LANES]
        mixed.append(_dot(masked_w_ref[2 * pair], jnp.where(low, vp, 0.0))
                     + _dot(masked_w_ref[2 * pair + 1], jnp.where(low, 0.0, vp))
                     + bias_ref[:, pair * LANES:(pair + 1) * LANES])
    return mixed


def _mixers_bwd(sinks, dmix, q, kvx, out, gates, ln_g, ln_b, sgu_w, bias_full, gwout):
    last = N_BLOCKS - 1

    def body(sink_ref, d_ref, q_ref, kc_ref, o_ref, za_ref, dsg_ref, us_ref, vs_ref, zs_ref, lng_ref, lnb_ref, w_ref,
             bias_ref, gwout_ref,
             dp_ref, gsink_ref, gbin_ref, dps_ref, gw_ref, gb_ref, gln_ref, gbins_ref, wout_shard_ref,
             kp_ref, pend_ref, carry_ref, wm_ref, wt_ref, gbias_ref, sa_w, ra_w, sb_w, rc_w, send_sems, recv_sems):
        n = pl.program_id(0)
        start, exchange, finish = _reduce_scatter_plan(_Copies(send_sems, recv_sems), 0, gwout_ref, WOUT_ROWS,
                                                       sa_w, ra_w, sb_w, rc_w, wout_shard_ref)
        tril = (lax.broadcasted_iota(jnp.int32, (BLOCK, BLOCK), 0)
                >= lax.broadcasted_iota(jnp.int32, (BLOCK, BLOCK), 1))

        @pl.when(n == 0)
        def _():
            gsink_ref[...] = jnp.zeros_like(gsink_ref)
            gbin_ref[...] = jnp.zeros_like(gbin_ref)
            carry_ref[...] = jnp.zeros_like(carry_ref)
            kp_ref[...] = jnp.zeros_like(kp_ref)
            gw_ref[...] = jnp.zeros_like(gw_ref)
            gln_ref[...] = jnp.zeros_like(gln_ref)
            gbins_ref[...] = jnp.zeros_like(gbins_ref)
            gbias_ref[...] = jnp.zeros_like(gbias_ref)
            _mask_sgu_weights(w_ref, wm_ref, wt_ref)
            start()

        pl.when(n == 3)(exchange)
        pl.when(n == 12)(finish)

        @pl.when(n > 0)
        def _():
            dp_ref[:, 0:ATTN_W] = pend_ref[:, 0:ATTN_W]
            dp_ref[:, GATE0:ATTN_SECTION] = pend_ref[:, ATTN_W:]

        @pl.when(n > last)
        def _():
            dp_ref[:, KV0:GATE0] = carry_ref[...].astype(MXU_DTYPE)

        @pl.when(n <= last)
        def _():
            us = us_ref[...]
            vs = vs_ref[...]
            lng = lng_ref[...]
            u, vhat, rstd, vln = _sgu_activations(us, vs, lng, lnb_ref[...])
            low_sgu = lax.broadcasted_iota(jnp.int32, (BLOCK, LANES), 1) < HALF
            sgu = {}

            def sgu_gates():
                mixed = _sgu_mix(vln, wm_ref, bias_ref)
                sgu["du"], sgu["dzs"], sgu["dm"] = [], [], []
                for pair in range(N_SGU_HEADS // 2):
                    cols = slice(pair * LANES, (pair + 1) * LANES)
                    dsg = dsg_ref[:, cols]
                    gate, gate_grad = _silu_and_grad(zs_ref[:, cols])
                    up = u[:, cols]
                    sgu["du"].append(dsg * mixed[pair] * gate)
                    sgu["dzs"].append(dsg * up * mixed[pair] * gate_grad)
                    dmixed = dsg * up * gate
                    gbias_ref[:, cols] += dmixed
                    sgu["dm"].append((jnp.where(low_sgu, dmixed, 0.0).astype(MXU_DTYPE),
                                      jnp.where(low_sgu, 0.0, dmixed).astype(MXU_DTYPE)))

            def sgu_grads():
                dvln_parts = []
                for pair in range(N_SGU_HEADS // 2):
                    dm_lo, dm_hi = sgu["dm"][pair]
                    vp = vln[:, pair * LANES:(pair + 1) * LANES]
                    gw_ref[2 * pair] += _dot(dm_lo, vp, NT)
                    gw_ref[2 * pair + 1] += _dot(dm_hi, vp, NT)
                    dvln_parts.append(_dot(wt_ref[2 * pair], dm_lo) + _dot(wt_ref[2 * pair + 1], dm_hi))
                dvln = jnp.concatenate(dvln_parts, axis=1)
                gln_ref[0:1, :] += jnp.sum(dvln * vhat, axis=0, keepdims=True)
                gln_ref[1:2, :] += jnp.sum(dvln, axis=0, keepdims=True)
                dvhat = dvln * lng
                dvg = rstd * (dvhat - jnp.mean(dvhat, axis=-1, keepdims=True)
                              - vhat * jnp.mean(dvhat * vhat, axis=-1, keepdims=True))
                dus = jnp.concatenate(sgu["du"], axis=1) * _gelu_grad(us)
                dvs = dvg * _gelu_grad(vs)
                dzs = jnp.concatenate(sgu["dzs"], axis=1)
                for k, val in enumerate((dus, dvs, dzs)):
                    dps_ref[:, k * SGU_W:(k + 1) * SGU_W] = val.astype(MXU_DTYPE)
                    gbins_ref[:, k * SGU_W:(k + 1) * SGU_W] += jnp.sum(val, axis=0, keepdims=True)

            valid = _window_mask(n)[0:BLOCK]
            low = lax.broadcasted_iota(jnp.int32, (BLOCK, LANES), 1) < HALF
            low_keys = lax.broadcasted_iota(jnp.int32, (2 * BLOCK, LANES), 1) < HALF
            lane_row = lax.broadcasted_iota(jnp.int32, (1, LANES), 1)
            gsink = jnp.zeros((1, LANES), F32)
            chains = [(g, par, i) for g in range(2) for par in range(2) for i in range(2)]
            kv = {(g, par): _kv_cat(kp_ref, kc_ref, 2 * g + par, False) for g in range(2) for par in range(2)}
            ones_keys = jnp.ones((2 * BLOCK, LANES), MXU_DTYPE)
            half_of_lane = lax.broadcasted_iota(jnp.int32, (LANES, 2 * LANES), 0) // HALF
            half_of_col = lax.broadcasted_iota(jnp.int32, (LANES, 2 * LANES), 1) // LANES
            sum_halves = (half_of_lane == half_of_col).astype(MXU_DTYPE)
            douts, deltas = [], []
            for pair in range(N_PAIRS):
                lanes = slice(pair * LANES, (pair + 1) * LANES)
                dg = d_ref[:, lanes]
                gate, gate_grad = _silu_and_grad(za_ref[:, lanes])
                o = o_ref[pair]
                dout = dg * gate
                dza = dg * o * gate_grad
                douts.append(dout.astype(MXU_DTYPE))
                deltas.append(_dot(dout * o, sum_halves))
                zl = slice(ATTN_W + pair * LANES, ATTN_W + (pair + 1) * LANES)
                pend_ref[:, zl] = dza.astype(MXU_DTYPE)
                gl = slice(GATE0 + pair * LANES, GATE0 + (pair + 1) * LANES)
                gbin_ref[:, gl] += jnp.sum(dza, axis=0, keepdims=True)

            first = {}

            def issue_first(k):
                g, par, i = chains[k]
                first[k] = (_dot(q_ref[2 * g + i], kv[g, par][0], NT), _dot(douts[2 * g + i], kv[g, par][1], NT))

            numerators = {}

            def issue_row_sums(k):
                g, par, i = chains[k]
                sink = sink_ref[4 * g + 2 * i + par]
                e, m = _softmax_numerator(jnp.where(valid, first[k][0], NEG_INF), sink)
                numerators[k] = (e, jnp.exp(sink - m), _dot(e, ones_keys))

            ahead = ATTN_BWD_AHEAD
            for k in range(ahead):
                issue_first(k)
            issue_row_sums(0)
            issue_row_sums(1)
            dqs, dk_parts, dv_parts = {}, {}, {}
            operands = {}

            def issue_last(k):
                g, par, i = chains[k]
                ds, ds_t, p_t = operands.pop(k)
                dq = _dot(ds, kv[g, par][0])
                dqs[g, i] = dq if par == 0 else dqs[g, i] + dq
                dk = _dot(ds_t, q_ref[2 * g + i])
                dv = _dot(p_t, douts[2 * g + i])
                dk_parts[g, par] = dk if i == 0 else dk_parts[g, par] + dk
                dv_parts[g, par] = dv if i == 0 else dv_parts[g, par] + dv

            for k, (g, par, i) in enumerate(chains):
                h = 4 * g + 2 * i + par
                delta = deltas[2 * g + i][:, par * LANES:(par + 1) * LANES]
                e, at_sink, row_sum = numerators[k]
                inv = 1.0 / (row_sum + at_sink)
                p = e * jnp.tile(inv, (1, 2))
                ds = p * (first[k][1] - jnp.tile(delta, (1, 2)))
                ds = ds.astype(MXU_DTYPE)
                operands[k] = (ds, ds.T, p.astype(MXU_DTYPE).T)
                total = jnp.sum(at_sink * inv * delta, axis=0, keepdims=True)
                gsink = jnp.where(lane_row == h, -total, gsink)
                if k + ahead < len(chains):
                    issue_first(k + ahead)
                if k + 2 < len(chains):
                    issue_row_sums(k + 2)
                if k > 0:
                    issue_last(k - 1)
                if k == SGU_GATES_AFTER_CHAIN:
                    sgu_gates()
                if k == SGU_GRADS_AFTER_CHAIN:
                    sgu_grads()
            issue_last(len(chains) - 1)
            for pair in range(N_PAIRS):
                g, i = divmod(pair, 2)
                dq = dqs[g, i] * SCALE
                lanes = slice(pair * LANES, (pair + 1) * LANES)
                pend_ref[:, lanes] = dq.astype(MXU_DTYPE)
                gbin_ref[:, lanes] += jnp.sum(dq, axis=0, keepdims=True)
            gsink_ref[...] += gsink
            for k, parts in enumerate((dk_parts, dv_parts)):
                masked = {key: jnp.where(low_keys if key[1] == 0 else jnp.logical_not(low_keys), val, 0.0)
                          for key, val in parts.items()}
                both = (masked[0, 0] + masked[1, 1]
                        + pltpu.roll(masked[0, 1] + masked[1, 0], HALF, 1))
                lanes = slice(k * KV_W, (k + 1) * KV_W)
                done = carry_ref[:, lanes] + both[0:BLOCK]
                dp_ref[:, KV0 + k * KV_W:KV0 + (k + 1) * KV_W] = done.astype(MXU_DTYPE)
                carry_ref[:, lanes] = both[BLOCK:]
                gbin_ref[:, KV0 + k * KV_W:KV0 + (k + 1) * KV_W] += jnp.sum(both, axis=0, keepdims=True)
            kp_ref[...] = kc_ref[...]

        @pl.when(n == last)
        def _():
            for hh in range(N_SGU_HEADS):
                gw_ref[hh] = jnp.where(tril, gw_ref[hh], 0.0)
            head_of_lane = lax.broadcasted_iota(jnp.int32, (N_SGU_HEADS, SGU_W), 1) // HEAD_DIM
            select = (head_of_lane == lax.broadcasted_iota(jnp.int32, (N_SGU_HEADS, SGU_W), 0)).astype(F32)
            gb_ref[...] = lax.dot_general(select, gbias_ref[...], NT, precision=lax.Precision.HIGHEST,
                                          preferred_element_type=F32)

    at = lambda n: jnp.minimum(n, last)
    blk = lambda w: pl.BlockSpec((BLOCK, w), lambda n: (at(n), 0))
    tiles = pl.BlockSpec((N_PAIRS, BLOCK, LANES), lambda n: (0, at(n), 0))
    section = lambda k: pl.BlockSpec((None, BLOCK, SGU_W), lambda n: (k, at(n), 0))
    return pl.pallas_call(
        body,
        name="mixers_bwd",
        grid=(N_BLOCKS + 1,),
        in_specs=[pl.BlockSpec(memory_space=pltpu.SMEM),
                  section(0),
                  tiles,
                  blk(KVX_W),
                  tiles,
                  section(0),
                  section(1),
                  section(1), section(2), section(3),
                  _full((1, SGU_W)), _full((1, SGU_W)), _full((N_SGU_HEADS, BLOCK, BLOCK)), _full((BLOCK, SGU_W)),
                  VMEM_SPEC],
        out_specs=(pl.BlockSpec((BLOCK, ATTN_SECTION), lambda n: (jnp.maximum(n - 1, 0), 0)),
                   _full((1, LANES)), _full((1, ATTN_SECTION)),
                   pl.BlockSpec((BLOCK, SGU_SECTION), lambda n: (at(n), 0)),
                   _full((N_SGU_HEADS, BLOCK, BLOCK)), _full((N_SGU_HEADS, BLOCK)),
                   _full((8, SGU_W)), _full((1, SGU_SECTION)), VMEM_SPEC),
        out_shape=(jax.ShapeDtypeStruct((SEQ, ATTN_SECTION), MXU_DTYPE),
                   jax.ShapeDtypeStruct((1, LANES), F32),
                   jax.ShapeDtypeStruct((1, ATTN_SECTION), F32),
                   jax.ShapeDtypeStruct((SEQ, SGU_SECTION), MXU_DTYPE),
                   jax.ShapeDtypeStruct((N_SGU_HEADS, BLOCK, BLOCK), F32),
                   jax.ShapeDtypeStruct((N_SGU_HEADS, BLOCK), F32),
                   jax.ShapeDtypeStruct((8, SGU_W), F32),
                   jax.ShapeDtypeStruct((1, SGU_SECTION), F32),
                   jax.ShapeDtypeStruct((WOUT_ROWS, D_MODEL), F32)),
        scratch_shapes=([pltpu.VMEM((BLOCK, KVX_W), MXU_DTYPE),
                         pltpu.VMEM((BLOCK, 2 * ATTN_W), MXU_DTYPE), pltpu.VMEM((BLOCK, 2 * KV_W), F32),
                         pltpu.VMEM((N_SGU_HEADS, BLOCK, BLOCK), MXU_DTYPE),
                         pltpu.VMEM((N_SGU_HEADS, BLOCK, BLOCK), MXU_DTYPE), pltpu.VMEM((BLOCK, SGU_W), F32)]
                        + _reduce_scatter_scratch(WOUT_ROWS, D_MODEL, COMM_DTYPE) + _dma_sems(REDUCE_SEMS)),
        compiler_params=_params(("arbitrary",), VMEM_LIMIT),
    )(sinks, dmix, q, kvx, out, gates, dmix, gates, gates, gates, ln_g, ln_b, sgu_w, bias_full, gwout)


def _in_proj_bwd(dpa, dps, win_t, x, norm_g, gres, gwin, vec_parts):
    tm = TOKEN_TILE
    steps = SEQ // tm
    n_parts = len(vec_parts)

    def body(da_ref, ds_ref, w_ref, x_ref, g_ref, gres_ref, gwin_ref, *rest):
        part_refs = rest[:n_parts]
        gx_ref, shard_ref, vec_out_ref, gng_ref, sa, ra, sb, rc, vec_ref, ra_vec, slots, send_sems, recv_sems = (
            rest[n_parts:])
        step = pl.program_id(0)
        copies = _Copies(send_sems, recv_sems)
        start, exchange, finish = _reduce_scatter_plan(copies, 0, gwin_ref, WIN_ROWS, sa, ra, sb, rc, shard_ref)

        @pl.when(step == 0)
        def _():
            gng_ref[...] = jnp.zeros_like(gng_ref)
            start()

        pl.when(step == 2)(exchange)

        dh = _dot(da_ref[...], w_ref[0:ATTN_SECTION, :]) + _dot(ds_ref[...], w_ref[ATTN_SECTION:, :])
        xv = x_ref[...]
        r = lax.rsqrt(jnp.mean(xv * xv, axis=-1, keepdims=True) + NORM_EPS)
        xn = xv * r
        gng_ref[...] += jnp.sum(dh * xn, axis=0, keepdims=True)
        dxn = dh * g_ref[...]
        gx_ref[...] = r * (dxn - xn * jnp.mean(dxn * xn, axis=-1, keepdims=True)) + gres_ref[...]

        @pl.when(step == steps - 1)
        def _():
            finish()
            _all_reduce_vectors(copies, REDUCE_SEMS, gng_ref, *part_refs, vec_out_ref, vec_ref, ra_vec, slots)

    tile = lambda w: pl.BlockSpec((tm, w), lambda i: (i, 0))
    return pl.pallas_call(
        body,
        name="in_proj_bwd",
        grid=(steps,),
        in_specs=[tile(ATTN_SECTION), tile(SGU_SECTION), _full((IN_W, D_MODEL)), tile(D_MODEL),
                  _full((1, D_MODEL)), tile(D_MODEL), VMEM_SPEC] + [VMEM_SPEC] * n_parts,
        out_specs=(tile(D_MODEL), VMEM_SPEC, VMEM_SPEC),
        out_shape=(jax.ShapeDtypeStruct((SEQ, D_MODEL), F32),
                   jax.ShapeDtypeStruct((WIN_ROWS, D_MODEL), F32),
                   jax.ShapeDtypeStruct((VEC_ROWS, IN_W), F32)),
        scratch_shapes=([pltpu.VMEM((1, D_MODEL), F32)] + _reduce_scatter_scratch(WIN_ROWS, D_MODEL, COMM_DTYPE)
                        + _vector_scratch() + _dma_sems(REDUCE_SEMS + VECTOR_SEMS)),
        compiler_params=_params(("arbitrary",), VMEM_LIMIT),
    )(dpa, dps, win_t, x, norm_g, gres, gwin, *vec_parts)


def _win_grad(dpa, dps, h, gsguw):
    rows = 256
    n_attn = ATTN_SECTION // rows
    steps = n_attn + SGU_SECTION // rows

    def body(da_ref, ds_ref, h_ref, gsguw_ref, o_ref, sguw_full_ref, sa, ra, sb, rc, landing, send_sems, recv_sems):
        step = pl.program_id(0)
        copies = _Copies(send_sems, recv_sems)
        own_sguw = landing.at[_block_rows(_place(), SGUW_ROWS), :]
        start, exchange, finish = _reduce_scatter_plan(copies, 0, gsguw_ref, SGUW_ROWS, sa, ra, sb, rc, own_sguw)
        gather = _gather_plan(copies, REDUCE_SEMS, landing, SGUW_ROWS)

        pl.when(step == 0)(start)
        pl.when(step == 2)(exchange)

        @pl.when(step == 5)
        def _():
            finish()
            gather[0]()

        pl.when(step == 7)(gather[1])

        @pl.when(step < n_attn)
        def _():
            o_ref[...] = _dot(da_ref[...], h_ref[...], TN)

        @pl.when(step >= n_attn)
        def _():
            o_ref[...] = _dot(ds_ref[...], h_ref[...], TN)

        @pl.when(step == steps - 1)
        def _():
            gather[2]()
            sguw_full_ref[...] = landing[...]

    return pl.pallas_call(
        body,
        name="win_grad",
        grid=(steps,),
        in_specs=[pl.BlockSpec((SEQ, rows), lambda i: (0, jnp.minimum(i, n_attn - 1))),
                  pl.BlockSpec((SEQ, rows), lambda i: (0, jnp.maximum(i - n_attn, 0))),
                  _full((SEQ, D_MODEL)), VMEM_SPEC],
        out_specs=(pl.BlockSpec((rows, D_MODEL), lambda i: (i, 0)), _full((N_SGU_HEADS * BLOCK, BLOCK))),
        out_shape=(jax.ShapeDtypeStruct((IN_W, D_MODEL), F32),
                   jax.ShapeDtypeStruct((N_SGU_HEADS * BLOCK, BLOCK), F32)),
        scratch_shapes=(_reduce_scatter_scratch(SGUW_ROWS, BLOCK, F32)
                        + [pltpu.VMEM((N_SGU_HEADS * BLOCK, BLOCK), F32)]
                        + _dma_sems(REDUCE_SEMS + GATHER_SEMS)),
        compiler_params=_params(("arbitrary",), VMEM_LIMIT),
    )(dpa, dps, h, gsguw)


VEC_NORM_G, VEC_B_IN, VEC_SINKS, VEC_LN_G, VEC_LN_B, VEC_B_OUT, VEC_FINAL_G, VEC_LOSS, VEC_SGU_B = 0, 1, 2, 3, 4, 5, 6, 7, 8


def _adamw(w, g, m, v):
    m = ADAM_B1 * m + (1.0 - ADAM_B1) * g
    v = ADAM_B2 * v + (1.0 - ADAM_B2) * (g * g)
    m_hat = m / (1.0 - ADAM_B1 ** ADAM_STEP)
    v_hat = v / (1.0 - ADAM_B2 ** ADAM_STEP)
    delta = -ADAM_LR * (m_hat / (jnp.sqrt(v_hat) + ADAM_EPS) + ADAM_WD * w)
    return delta, m, v


def _adamw_shard(name, g, w, m, v, block_rows):
    def body(g_ref, w_ref, m_ref, v_ref, d_ref, nm_ref, nv_ref):
        d_ref[...], nm_ref[...], nv_ref[...] = _adamw(w_ref[...], g_ref[...], m_ref[...], v_ref[...])

    rows, cols = w.shape
    spec = pl.BlockSpec((block_rows, cols), lambda i: (i, 0))
    return pl.pallas_call(
        body,
        name=name,
        grid=(rows // block_rows,),
        in_specs=[spec] * 4,
        out_specs=(spec,) * 3,
        out_shape=(jax.ShapeDtypeStruct(w.shape, F32),) * 3,
        compiler_params=_params(("arbitrary",)),
    )(g, w, m, v)


VECTOR_SEMS = 4


def _vector_scratch():
    return [pltpu.VMEM((VEC_ROWS, IN_W), F32), pltpu.VMEM((VEC_ROWS, IN_W), F32),
            pltpu.VMEM((4 * VEC_ROWS, IN_W), F32)]


def _all_reduce_vectors(copies, sem0, gng_ref, gba_ref, gbs_ref, gsink_ref, gln_ref, gsgub_ref, vec4_ref, out_ref,
                        vec_ref, ra_vec, slots):
    x, y, c = _place()
    vec_ref[...] = jnp.zeros_like(vec_ref)
    vec_ref[VEC_NORM_G:VEC_NORM_G + 1, 0:D_MODEL] = gng_ref[...]
    vec_ref[VEC_B_IN:VEC_B_IN + 1, 0:ATTN_SECTION] = gba_ref[...]
    vec_ref[VEC_B_IN:VEC_B_IN + 1, ATTN_SECTION:IN_W] = gbs_ref[...]
    vec_ref[VEC_SINKS:VEC_SINKS + 1, 0:LANES] = gsink_ref[...]
    vec_ref[VEC_LN_G:VEC_LN_G + 1, 0:SGU_W] = gln_ref[0:1, :]
    vec_ref[VEC_LN_B:VEC_LN_B + 1, 0:SGU_W] = gln_ref[1:2, :]
    vec_ref[VEC_B_OUT:VEC_B_OUT + 1, 0:D_MODEL] = vec4_ref[2:3, :]
    vec_ref[VEC_FINAL_G:VEC_FINAL_G + 1, 0:D_MODEL] = vec4_ref[1:2, :]
    vec_ref[VEC_LOSS:VEC_LOSS + 1, 0:D_MODEL] = vec4_ref[0:1, :]
    vec_ref[VEC_SGU_B:VEC_SGU_B + N_SGU_HEADS, 0:BLOCK] = gsgub_ref[...]

    to_sibling = copies(sem0, vec_ref, ra_vec, (x, y, 1 - c))
    to_sibling.start()
    to_sibling.wait_recv()

    def chip_slot(place):
        return slots.at[pl.ds(pl.multiple_of((2 * place[0] + place[1]) * VEC_ROWS, 8), VEC_ROWS), :]

    mine = chip_slot((x, y))
    mine[...] = vec_ref[...] + ra_vec[...]
    to_chips = [copies(sem0 + i, mine, mine, (*_chip(rel), c)) for i, rel in enumerate(RELATIONS[1:], start=1)]
    for cp in to_chips:
        cp.start()
    for i, rel in enumerate(RELATIONS[1:], start=1):
        theirs = chip_slot(_chip(rel))
        copies(sem0 + i, theirs, theirs, (x, y, c)).wait_recv()
    out_ref[...] = ((slots[0:VEC_ROWS, :] + slots[VEC_ROWS:2 * VEC_ROWS, :])
                    + slots[2 * VEC_ROWS:3 * VEC_ROWS, :]) + slots[3 * VEC_ROWS:, :]
    to_sibling.wait_send()
    for cp in to_chips:
        cp.wait_send()


def _adamw_replicated(vec, gsguw, weights, m_state, v_state):
    n = len(SMALL)

    def body(*refs):
        vec_ref, gsguw_ref = refs[0], refs[1]
        w_refs, m_refs, v_refs = (refs[2 + k * n:2 + (k + 1) * n] for k in range(3))
        outs = refs[2 + 3 * n:]
        g_refs, d_refs, nm_refs, nv_refs = (outs[k * n:(k + 1) * n] for k in range(4))
        for i, (_, row, shape) in enumerate(SMALL):
            g = gsguw_ref[...] if row is None else vec_ref[row:row + shape[0], 0:shape[1]]
            g_refs[i][...] = g
            d_refs[i][...], nm_refs[i][...], nv_refs[i][...] = _adamw(
                w_refs[i][...], g, m_refs[i][...], v_refs[i][...])

    shapes = tuple(jax.ShapeDtypeStruct(shape, F32) for _, _, shape in SMALL)
    outs = pl.pallas_call(
        body,
        name="adamw_replicated",
        in_specs=[VMEM_SPEC] * (2 + 3 * n),
        out_specs=(VMEM_SPEC,) * (4 * n),
        out_shape=shapes * 4,
    )(vec, gsguw, *weights, *m_state, *v_state)
    return tuple(outs[k * n:(k + 1) * n] for k in range(4))


SMALL = (
    ("norm_g", VEC_NORM_G, (1, D_MODEL)),
    ("b_in", VEC_B_IN, (1, IN_W)),
    ("attn_sinks", VEC_SINKS, (1, N_Q_HEADS)),
    ("sgu_ln_g", VEC_LN_G, (1, SGU_W)),
    ("sgu_ln_b", VEC_LN_B, (1, SGU_W)),
    ("sgu_w", None, (N_SGU_HEADS * BLOCK, BLOCK)),
    ("sgu_b", VEC_SGU_B, (N_SGU_HEADS, BLOCK)),
    ("b_out", VEC_B_OUT, (1, D_MODEL)),
    ("final_norm_g", VEC_FINAL_G, (1, D_MODEL)),
)


def _local_grads(x, target, h, win_t, wout_shard, norm_g, b_in, attn_sinks, sgu_ln_g, sgu_ln_b, sgu_w, sgu_b, b_out,
                 final_g):
    sinks = attn_sinks.reshape(N_Q_HEADS)
    bias_full = jnp.repeat(sgu_b.T, HEAD_DIM, axis=1)
    q, kvx, gates, wout = _in_proj(h, b_in, win_t, wout_shard)
    out, gres, dmix, gwout, vec4 = _mixers_out_proj(sinks, q, kvx, gates, sgu_ln_g, sgu_ln_b, sgu_w, bias_full,
                                                    x, target, wout, b_out, final_g)
    dpa, gsink, gbin_a, dps, gsguw, gsgub, gln, gbin_s, gwout_shard = _mixers_bwd(
        sinks, dmix, q, kvx, out, gates, sgu_ln_g, sgu_ln_b, sgu_w, bias_full, gwout)
    gwin, gsguw_sum = _win_grad(dpa, dps, h, gsguw.reshape(N_SGU_HEADS * BLOCK, BLOCK))
    grad_x, gwin_shard, vec = _in_proj_bwd(dpa, dps, win_t, x, norm_g, gres, gwin,
                                           (gbin_a, gbin_s, gsink, gln, gsgub, vec4))
    return grad_x, gwin_shard, gwout_shard, gsguw_sum, vec


def kernel(x, norm_g, w_in, b_in, attn_sinks, sgu_ln_g, sgu_ln_b, sgu_w, sgu_b, w_out, b_out, final_norm_g, loss_target, m_norm_g, m_w_in, m_b_in, m_attn_sinks, m_sgu_ln_g, m_sgu_ln_b, m_sgu_w, m_sgu_b, m_w_out, m_b_out, m_final_norm_g, v_norm_g, v_w_in, v_b_in, v_attn_sinks, v_sgu_ln_g, v_sgu_ln_b, v_sgu_w, v_sgu_b, v_w_out, v_b_out, v_final_norm_g):
    given = dict(norm_g=norm_g, b_in=b_in, attn_sinks=attn_sinks, sgu_ln_g=sgu_ln_g, sgu_ln_b=sgu_ln_b,
                 sgu_w=sgu_w, sgu_b=sgu_b, b_out=b_out, final_norm_g=final_norm_g)
    m_given = dict(norm_g=m_norm_g, b_in=m_b_in, attn_sinks=m_attn_sinks, sgu_ln_g=m_sgu_ln_g,
                   sgu_ln_b=m_sgu_ln_b, sgu_w=m_sgu_w, sgu_b=m_sgu_b, b_out=m_b_out, final_norm_g=m_final_norm_g)
    v_given = dict(norm_g=v_norm_g, b_in=v_b_in, attn_sinks=v_attn_sinks, sgu_ln_g=v_sgu_ln_g,
                   sgu_ln_b=v_sgu_ln_b, sgu_w=v_sgu_w, sgu_b=v_sgu_b, b_out=v_b_out, final_norm_g=v_final_norm_g)

    win_t, h = _all_gather_win(w_in[0].T, x[0], norm_g)
    grad_x, gwin_t, gwout, gsguw, vec = _local_grads(
        x[0], loss_target[0], h, win_t, w_out[0], norm_g, b_in, attn_sinks, sgu_ln_g, sgu_ln_b, sgu_w[0], sgu_b[0],
        b_out, final_norm_g.reshape(1, D_MODEL))

    t = lambda a: a[0].T
    d_win, nm_win, nv_win = _adamw_shard("adamw_w_in", gwin_t, t(w_in), t(m_w_in), t(v_w_in), WIN_ROWS // 2)
    d_wout, nm_wout, nv_wout = _adamw_shard("adamw_w_out", gwout, w_out[0], m_w_out[0], v_w_out[0], WOUT_ROWS)
    as_2d = lambda d: [d[name].reshape(shape) for name, _, shape in SMALL]
    loss = vec[VEC_LOSS, 0]
    small = _adamw_replicated(vec, gsguw, as_2d(given), as_2d(m_given), as_2d(v_given))

    def assemble(big_in, big_out, k):
        vals = {name: small[k][i].reshape(given[name].shape) for i, (name, _, _) in enumerate(SMALL)}
        vals["w_in"] = big_in.T[None]
        vals["w_out"] = big_out[None]
        order = ("norm_g", "w_in", "b_in", "attn_sinks", "sgu_ln_g", "sgu_ln_b", "sgu_w", "sgu_b", "w_out",
                 "b_out", "final_norm_g")
        return [vals[name] for name in order]

    return (loss, grad_x[None],
            *assemble(gwin_t, gwout, 0), *assemble(d_win, d_wout, 1),
            *assemble(nm_win, nm_wout, 2), *assemble(nv_win, nv_wout, 3))
```

```python
import functools
import math

import jax
import jax.numpy as jnp
from jax import lax
from jax.experimental import pallas as pl
from jax.experimental.pallas import tpu as pltpu

F32 = jnp.float32
BF16 = jnp.bfloat16
MXU_DTYPE = BF16
COMM_DTYPE = BF16

D_MODEL = 1024
SEQ = 4096
HEAD_DIM = 64
N_Q_HEADS = 8
Q_PER_KV = 4
BLOCK = 128
N_BLOCKS = SEQ // BLOCK
ATTN_W = 512
KV_W = 128
SGU_W = 512
N_SGU_HEADS = 8
IN_W = 2816
NORM_EPS = 1e-5
NEG_INF = -1e30
SCALE = HEAD_DIM ** -0.5
KV0 = ATTN_W
GATE0 = ATTN_W + 2 * KV_W
SGU0 = GATE0 + ATTN_W
ATTN_SECTION = SGU0
SGU_SECTION = IN_W - SGU0

ADAM_LR = 0.001
ADAM_B1 = 0.9
ADAM_B2 = 0.999
ADAM_EPS = 1e-08
ADAM_WD = 0.01
ADAM_STEP = 10

N_DEV = 8
WIN_ROWS = IN_W // N_DEV
WOUT_ROWS = D_MODEL // N_DEV
SGUW_ROWS = N_SGU_HEADS * BLOCK // N_DEV
VEC_ROWS = 16
MESH = pl.DeviceIdType.MESH

LANES = 128
HALF = LANES // 2
N_PAIRS = N_Q_HEADS * HEAD_DIM // LANES
KVX_W = 12 * LANES
TOKEN_TILE = 256
FWD_TOKEN_TILE = 512
ATTN_FWD_AHEAD = 4
FUSED_BLOCKS = 4
SGU_MIX_AFTER_CHAIN = 0
SGU_GATES_AFTER_CHAIN = 1
SGU_GRADS_AFTER_CHAIN = 5
ATTN_BWD_AHEAD = 3
VMEM_LIMIT = 56 * 1024 * 1024

NN = (((1,), (0,)), ((), ()))
NT = (((1,), (1,)), ((), ()))
TN = (((0,), (0,)), ((), ()))


def _dot(a, b, dims=NN):
    return lax.dot_general(a.astype(MXU_DTYPE), b.astype(MXU_DTYPE), dims, preferred_element_type=F32)


def _gelu(x):
    return x * (lax.erf(x * (1.0 / math.sqrt(2.0))) + 1.0) * 0.5


def _gelu_grad(x):
    cdf = (lax.erf(x * (1.0 / math.sqrt(2.0))) + 1.0) * 0.5
    return cdf + x * jnp.exp(-0.5 * x * x) * (1.0 / math.sqrt(2.0 * math.pi))


def _silu_and_grad(z):
    s = jax.nn.sigmoid(z)
    return z * s, s * (1.0 + z * (1.0 - s))


def _params(semantics=None, vmem=None):
    kw = {}
    if semantics is not None:
        kw["dimension_semantics"] = semantics
    if vmem is not None:
        kw["vmem_limit_bytes"] = vmem
    return pltpu.CompilerParams(**kw)


def _full(shape):
    return pl.BlockSpec(shape, lambda *_: (0,) * len(shape))


VMEM_SPEC = pl.BlockSpec(memory_space=pltpu.VMEM)


RELATIONS = ((0, 0), (1, 0), (0, 1), (1, 1))


def _place():
    return lax.axis_index("x"), lax.axis_index("y"), lax.axis_index("c")


def _chip(rel):
    x, y, _ = _place()
    return (1 - x if rel[0] else x, 1 - y if rel[1] else y)


def _block_rows(place, n_rows):
    px, py, pc = place
    return pl.ds(pl.multiple_of((4 * px + 2 * py + pc) * n_rows, 16), n_rows)


class _Copies:
    def __init__(self, send_sems, recv_sems):
        self.send_sems, self.recv_sems = send_sems, recv_sems

    def __call__(self, k, src, dst, to):
        return pltpu.make_async_remote_copy(src_ref=src, dst_ref=dst, send_sem=self.send_sems.at[k],
                                            recv_sem=self.recv_sems.at[k], device_id=to, device_id_type=MESH)


def _gather_plan(copies, sem0, full_ref, n_rows):
    x, y, c = _place()
    me, sibling = (x, y, c), (x, y, 1 - c)
    chips = [_chip(rel) for rel in RELATIONS[1:]]

    def cp(k, block, to):
        rows = full_ref.at[_block_rows(block, n_rows), :]
        return copies(sem0 + k, rows, rows, to)

    first = [cp(0, me, sibling)] + [cp(1 + j, me, (*chip, c)) for j, chip in enumerate(chips)]
    passed = [cp(4 + j, (*chip, c), sibling) for j, chip in enumerate(chips)]

    def start():
        for f in first:
            f.start()

    def forward():
        for j, chip in enumerate(chips):
            cp(1 + j, (*chip, c), me).wait_recv()
            passed[j].start()

    def finish():
        cp(0, sibling, me).wait_recv()
        for j, chip in enumerate(chips):
            cp(4 + j, (*chip, 1 - c), me).wait_recv()
        for f in first + passed:
            f.wait_send()

    return start, forward, finish


GATHER_SEMS = 7


def _reduce_scatter_plan(copies, sem0, part_ref, n_rows, sa, ra, sb, rc, res_ref):
    x, y, c = _place()
    sibling = (x, y, 1 - c)
    n = n_rows
    level1 = copies(sem0, sa, ra, sibling)

    def level2(i):
        slot = pl.ds((i - 1) * n, n)
        return copies(sem0 + i, sb.at[slot, :], rc.at[slot, :], (*_chip(RELATIONS[i]), c))

    def start():
        for i, rel in enumerate(RELATIONS):
            sa[i * n:(i + 1) * n, :] = part_ref[_block_rows((*_chip(rel), 1 - c), n), :].astype(sa.dtype)
        level1.start()

    def exchange():
        level1.wait_recv()
        for i, rel in enumerate(RELATIONS):
            total = part_ref[_block_rows((*_chip(rel), c), n), :] + ra[i * n:(i + 1) * n, :].astype(F32)
            if i == 0:
                res_ref[...] = total
            else:
                sb[(i - 1) * n:i * n, :] = total.astype(sb.dtype)
                level2(i).start()

    def finish():
        acc = res_ref[...]
        for i in range(1, len(RELATIONS)):
            level2(i).wait_recv()
            acc = acc + rc[(i - 1) * n:i * n, :].astype(F32)
        res_ref[...] = acc
        level1.wait_send()
        for i in range(1, len(RELATIONS)):
            level2(i).wait_send()

    return start, exchange, finish


REDUCE_SEMS = 4


def _reduce_scatter_scratch(n_rows, width, dtype):
    return [pltpu.VMEM((4 * n_rows, width), dtype), pltpu.VMEM((4 * n_rows, width), dtype),
            pltpu.VMEM((3 * n_rows, width), dtype), pltpu.VMEM((3 * n_rows, width), dtype)]


def _dma_sems(n):
    return [pltpu.SemaphoreType.DMA((n,)), pltpu.SemaphoreType.DMA((n,))]


def _all_gather_win(win_t_shard, x, norm_g):
    tm = FWD_TOKEN_TILE
    steps = SEQ // tm

    def body(win_ref, x_ref, g_ref, full_ref, h_ref, landing, send_sems, recv_sems):
        step = pl.program_id(0)
        start, forward, finish = _gather_plan(_Copies(send_sems, recv_sems), 0, landing, WIN_ROWS)

        @pl.when(step == 0)
        def _():
            landing[_block_rows(_place(), WIN_ROWS), :] = win_ref[...].astype(COMM_DTYPE)
            start()

        xv = x_ref[...]
        r = lax.rsqrt(jnp.mean(xv * xv, axis=-1, keepdims=True) + NORM_EPS)
        h_ref[...] = ((xv * r) * g_ref[...]).astype(MXU_DTYPE)

        @pl.when(step == steps - 1)
        def _():
            forward()
            finish()
            full_ref[...] = landing[...]

    return pl.pallas_call(
        body,
        name="all_gather_win",
        grid=(steps,),
        in_specs=[VMEM_SPEC, pl.BlockSpec((tm, D_MODEL), lambda i: (i, 0)), _full((1, D_MODEL))],
        out_specs=(_full((IN_W, D_MODEL)), pl.BlockSpec((tm, D_MODEL), lambda i: (i, 0))),
        out_shape=(jax.ShapeDtypeStruct((IN_W, D_MODEL), COMM_DTYPE),
                   jax.ShapeDtypeStruct((SEQ, D_MODEL), MXU_DTYPE)),
        scratch_shapes=[pltpu.VMEM((IN_W, D_MODEL), COMM_DTYPE)] + _dma_sems(GATHER_SEMS),
        compiler_params=_params(("arbitrary",), VMEM_LIMIT),
    )(win_t_shard, x, norm_g)


def _in_proj(h, b_in, win_t, wout_shard):
    tm = FWD_TOKEN_TILE
    steps = SEQ // tm

    def body(h_ref, b_ref, w_ref, wout_ref, q_ref, kvx_ref, gate_ref, wfull_ref, landing, send_sems, recv_sems):
        step = pl.program_id(0)
        start, forward, finish = _gather_plan(_Copies(send_sems, recv_sems), 0, landing, WOUT_ROWS)

        @pl.when(step == 0)
        def _():
            landing[_block_rows(_place(), WOUT_ROWS), :] = wout_ref[...].astype(COMM_DTYPE)
            start()

        pl.when(step == steps // 2)(forward)

        h = h_ref[...]

        def proj(lo, hi):
            return _dot(h, w_ref[lo:hi, :], NT) + b_ref[:, lo:hi]

        qs = proj(0, ATTN_W) * SCALE
        for pair in range(N_PAIRS):
            q_ref[pair] = qs[:, pair * LANES:(pair + 1) * LANES].astype(MXU_DTYPE)
        kv = proj(KV0, GATE0)
        low = lax.broadcasted_iota(jnp.int32, (tm, LANES), 1) < HALF
        for i in range(2):
            t = kv[:, i * LANES:(i + 1) * LANES]
            rot = pltpu.roll(t, HALF, 1)
            variants = (jnp.where(low, t, 0.0), jnp.where(low, 0.0, rot),
                        jnp.where(low, rot, 0.0), jnp.where(low, 0.0, t))
            for j, val in enumerate(variants):
                col = (4 * i + j) * LANES
                kvx_ref[:, col:col + LANES] = val.astype(MXU_DTYPE)
                if i == 1:
                    ones_elsewhere = jnp.where(low == (j % 2 == 0), val, 1.0)
                    kvx_ref[:, col + 4 * LANES:col + 5 * LANES] = ones_elsewhere.astype(MXU_DTYPE)
        for k in range(4):
            gate_ref[k] = proj(GATE0 + k * SGU_W, GATE0 + (k + 1) * SGU_W)

        @pl.when(step == steps - 1)
        def _():
            finish()
            wfull_ref[...] = landing[...]

    return pl.pallas_call(
        body,
        name="in_proj",
        grid=(steps,),
        in_specs=[pl.BlockSpec((tm, D_MODEL), lambda i: (i, 0)),
                  _full((1, IN_W)), _full((IN_W, D_MODEL)), VMEM_SPEC],
        out_specs=(pl.BlockSpec((N_PAIRS, tm, LANES), lambda i: (0, i, 0)),
                   pl.BlockSpec((tm, KVX_W), lambda i: (i, 0)),
                   pl.BlockSpec((4, tm, SGU_W), lambda i: (0, i, 0)),
                   _full((D_MODEL, D_MODEL))),
        out_shape=(jax.ShapeDtypeStruct((N_PAIRS, SEQ, LANES), MXU_DTYPE),
                   jax.ShapeDtypeStruct((SEQ, KVX_W), MXU_DTYPE),
                   jax.ShapeDtypeStruct((4, SEQ, SGU_W), F32),
                   jax.ShapeDtypeStruct((D_MODEL, D_MODEL), COMM_DTYPE)),
        scratch_shapes=[pltpu.VMEM((D_MODEL, D_MODEL), COMM_DTYPE)] + _dma_sems(GATHER_SEMS),
        compiler_params=_params(("arbitrary",), VMEM_LIMIT),
    )(h, b_in, win_t, wout_shard)


def _window_mask(n):
    qi = lax.broadcasted_iota(jnp.int32, (2 * BLOCK, 2 * BLOCK), 0) & (BLOCK - 1)
    p = lax.broadcasted_iota(jnp.int32, (2 * BLOCK, 2 * BLOCK), 1) - BLOCK
    in_window = jnp.logical_and(p <= qi, p > qi - BLOCK)
    return jnp.logical_and(in_window, jnp.logical_or(p >= 0, n > 0))


def _sink_column(sink_ref, g, par):
    return jnp.concatenate([jnp.full((BLOCK, 1), sink_ref[4 * g + par], F32),
                            jnp.full((BLOCK, 1), sink_ref[4 * g + 2 + par], F32)], axis=0)


def _kv_cat(kp_ref, kc_ref, var, with_ones):
    kcol, vcol = var * LANES, (var + (8 if with_ones else 4)) * LANES
    return (jnp.concatenate([kp_ref[:, kcol:kcol + LANES], kc_ref[:, kcol:kcol + LANES]], axis=0),
            jnp.concatenate([kp_ref[:, vcol:vcol + LANES], kc_ref[:, vcol:vcol + LANES]], axis=0))


def _softmax_numerator(s, sink):
    m = jnp.maximum(jnp.max(s, axis=1, keepdims=True), sink)
    return jnp.exp(s - m), m


def _mixers_out_proj(sinks, q, kvx, gates, ln_g, ln_b, sgu_w, bias_full, x, target, wout, b_out, final_g):
    tm = FUSED_BLOCKS * BLOCK
    n_tiles = SEQ // tm

    def body(sink_ref, q_ref, kc_ref, za_ref, us_ref, vs_ref, zs_ref, lng_ref, lnb_ref, w_ref, bias_ref,
             x_ref, t_ref, wout_ref, b_ref, gf_ref,
             out_ref, gres_ref, dmix_ref, gw_ref, vec_ref,
             kp_ref, wm_ref, mixed_next, mixed_cur, out_stage, gb_ref):
        step = pl.program_id(0)

        @pl.when(step == 0)
        def _():
            kp_ref[...] = jnp.zeros_like(kp_ref)
            _mask_sgu_weights(w_ref, wm_ref)
            gw_ref[...] = jnp.zeros_like(gw_ref)
            vec_ref[...] = jnp.zeros_like(vec_ref)
            mixed_cur[...] = jnp.zeros_like(mixed_cur)

        def mixers_block(b, after_chain=()):
            rows = slice(b * BLOCK, (b + 1) * BLOCK)
            kc = kc_ref.at[rows, :]
            u, _, _, vln = _sgu_activations(us_ref[rows, :], vs_ref[rows, :], lng_ref[...], lnb_ref[...])

            valid = _window_mask(step * FUSED_BLOCKS + b)[0:BLOCK]
            chains = [(g, par, i) for g in range(2) for par in range(2) for i in range(2)]
            kv = {(g, par): _kv_cat(kp_ref, kc, 2 * g + par, True) for g in range(2) for par in range(2)}
            scores, outs = {}, {}

            def issue_scores(k):
                g, par, i = chains[k]
                scores[k] = _dot(q_ref[2 * g + i, rows, :], kv[g, par][0], NT)

            ahead = ATTN_FWD_AHEAD
            for k in range(ahead):
                issue_scores(k)
            low = lax.broadcasted_iota(jnp.int32, (BLOCK, LANES), 1) < HALF
            for k, (g, par, i) in enumerate(chains):
                sink = sink_ref[4 * g + 2 * i + par]
                e, m = _softmax_numerator(jnp.where(valid, scores[k], NEG_INF), sink)
                if k + ahead < len(chains):
                    issue_scores(k + ahead)
                o = _dot(e, kv[g, par][1])
                outs[g, par, i] = o / (pltpu.roll(o, HALF, 1) + jnp.exp(sink - m))
                if k == SGU_MIX_AFTER_CHAIN:
                    mixed = _sgu_mix(vln, wm_ref, bias_ref)
                if k % 2 == 0 and k // 2 < len(after_chain):
                    after_chain[k // 2]()
            for pair in range(N_PAIRS):
                g, i = divmod(pair, 2)
                lanes = slice(pair * LANES, (pair + 1) * LANES)
                o = jnp.where(low, outs[g, 0, i], outs[g, 1, i])
                out_stage[pair, rows, :] = o
                gate, _ = _silu_and_grad(za_ref[rows, lanes])
                mixed_next[rows, lanes] = (o * gate).astype(MXU_DTYPE)
            kp_ref[...] = kc[...]
            for pair in range(N_SGU_HEADS // 2):
                cols = slice(pair * LANES, (pair + 1) * LANES)
                gate, _ = _silu_and_grad(zs_ref[rows, cols])
                mixed_next[rows, ATTN_W + pair * LANES:ATTN_W + (pair + 1) * LANES] = (
                    u[:, cols] * mixed[pair] * gate).astype(MXU_DTYPE)

        live = (step > 0).astype(F32)
        quarter = D_MODEL // 4
        columns = [None] * 4

        def project(j):
            def piece():
                columns[j] = _dot(mixed_cur[...], wout_ref[:, j * quarter:(j + 1) * quarter])
            return piece

        half_blocks = FUSED_BLOCKS // 2
        per_block = 4 // half_blocks
        for b in range(half_blocks):
            mixers_block(b, [project(j) for j in range(b * per_block, (b + 1) * per_block)])
        xo = x_ref[...] + jnp.concatenate(columns, axis=1) + b_ref[...]
        r = lax.rsqrt(jnp.mean(xo * xo, axis=-1, keepdims=True) + NORM_EPS)
        xn = xo * r
        gf = gf_ref[...]
        err = xn * gf - t_ref[...]
        loss = 0.5 * jnp.sum(jnp.mean(err * err, axis=-1, keepdims=True), axis=0, keepdims=True)
        dy = err * (1.0 / D_MODEL)
        dxn = dy * gf
        gres = r * (dxn - xn * jnp.mean(dxn * xn, axis=-1, keepdims=True))
        vec_ref[0:1, :] += jnp.broadcast_to(loss * live, (1, D_MODEL))
        vec_ref[1:2, :] += jnp.sum(dy * xn, axis=0, keepdims=True) * live
        vec_ref[2:3, :] += jnp.sum(gres, axis=0, keepdims=True) * live
        gres_ref[...] = gres
        gb_ref[...] = gres.astype(MXU_DTYPE)

        def branch_grad(k):
            def piece():
                dmix_ref[k] = _dot(gb_ref[...], wout_ref[k * ATTN_W:(k + 1) * ATTN_W, :], NT)
            return piece

        def weight_grad(k):
            def piece():
                rows = slice(k * ATTN_W, (k + 1) * ATTN_W)
                gw_ref[rows, :] += _dot(mixed_cur[:, rows], gb_ref[...], TN)
            return piece

        backward = [branch_grad(0), branch_grad(1), weight_grad(0), weight_grad(1)]
        for b in range(half_blocks):
            mixers_block(half_blocks + b, backward[b * per_block:(b + 1) * per_block])

        @pl.when(step < n_tiles)
        def _():
            out_ref[...] = out_stage[...]

        mixed_cur[...] = mixed_next[...]

    ahead_tile = lambda i: jnp.minimum(i, n_tiles - 1)
    behind_tile = lambda i: jnp.maximum(i - 1, 0)
    blk = lambda w: pl.BlockSpec((tm, w), lambda i: (ahead_tile(i), 0))
    tiles = pl.BlockSpec((N_PAIRS, tm, LANES), lambda i: (0, ahead_tile(i), 0))
    gate = lambda k: pl.BlockSpec((None, tm, SGU_W), lambda i: (k, ahead_tile(i), 0))
    behind = lambda w: pl.BlockSpec((tm, w), lambda i: (behind_tile(i), 0))
    return pl.pallas_call(
        body,
        name="mixers_out_proj",
        grid=(n_tiles + 1,),
        in_specs=[pl.BlockSpec(memory_space=pltpu.SMEM), tiles, blk(KVX_W), gate(0), gate(1), gate(2), gate(3),
                  _full((1, SGU_W)), _full((1, SGU_W)), _full((N_SGU_HEADS, BLOCK, BLOCK)), _full((BLOCK, SGU_W)),
                  behind(D_MODEL), behind(D_MODEL), _full((D_MODEL, D_MODEL)), _full((1, D_MODEL)),
                  _full((1, D_MODEL))],
        out_specs=(tiles, behind(D_MODEL), pl.BlockSpec((2, tm, ATTN_W), lambda i: (0, behind_tile(i), 0)),
                   _full((D_MODEL, D_MODEL)), _full((8, D_MODEL))),
        out_shape=(jax.ShapeDtypeStruct((N_PAIRS, SEQ, LANES), F32),
                   jax.ShapeDtypeStruct((SEQ, D_MODEL), F32),
                   jax.ShapeDtypeStruct((2, SEQ, ATTN_W), F32),
                   jax.ShapeDtypeStruct((D_MODEL, D_MODEL), F32),
                   jax.ShapeDtypeStruct((8, D_MODEL), F32)),
        scratch_shapes=[pltpu.VMEM((BLOCK, KVX_W), MXU_DTYPE), pltpu.VMEM((N_SGU_HEADS, BLOCK, BLOCK), MXU_DTYPE),
                        pltpu.VMEM((tm, D_MODEL), MXU_DTYPE), pltpu.VMEM((tm, D_MODEL), MXU_DTYPE),
                        pltpu.VMEM((N_PAIRS, tm, LANES), F32), pltpu.VMEM((tm, D_MODEL), MXU_DTYPE)],
        compiler_params=_params(("arbitrary",), VMEM_LIMIT),
    )(sinks, q, kvx, gates, gates, gates, gates, ln_g, ln_b, sgu_w, bias_full, x, target, wout, b_out, final_g)


def _sgu_activations(us, vs, lng, lnb):
    u = _gelu(us)
    vg = _gelu(vs)
    mu = jnp.mean(vg, axis=-1, keepdims=True)
    xc = vg - mu
    rstd = lax.rsqrt(jnp.mean(xc * xc, axis=-1, keepdims=True) + NORM_EPS)
    vhat = xc * rstd
    return u, vhat, rstd, vhat * lng + lnb


def _mask_sgu_weights(w_ref, masked_ref, transposed_ref=None):
    tril = (lax.broadcasted_iota(jnp.int32, (BLOCK, BLOCK), 0)
            >= lax.broadcasted_iota(jnp.int32, (BLOCK, BLOCK), 1))
    for hh in range(N_SGU_HEADS):
        w = jnp.where(tril, w_ref[hh], 0.0)
        masked_ref[hh] = w.astype(MXU_DTYPE)
        if transposed_ref is not None:
            transposed_ref[hh] = w.T.astype(MXU_DTYPE)


def _sgu_mix(vln, masked_w_ref, bias_ref):
    low = lax.broadcasted_iota(jnp.int32, (BLOCK, LANES), 1) < HALF
    mixed = []
    for pair in range(N_SGU_HEADS // 2):
        vp = vln[:, pair * LANES:(pair + 1) * LANES]
        mixed.append(_dot(masked_w_ref[2 * pair], jnp.where(low, vp, 0.0))
                     + _dot(masked_w_ref[2 * pair + 1], jnp.where(low, 0.0, vp))
                     + bias_ref[:, pair * LANES:(pair + 1) * LANES])
    return mixed


def _mixers_bwd(sinks, dmix, q, kvx, out, gates, ln_g, ln_b, sgu_w, bias_full, gwout):
    last = N_BLOCKS - 1

    def body(sink_ref, d_ref, q_ref, kc_ref, o_ref, za_ref, dsg_ref, us_ref, vs_ref, zs_ref, lng_ref, lnb_ref, w_ref,
             bias_ref, gwout_ref,
             dp_ref, gsink_ref, gbin_ref, dps_ref, gw_ref, gb_ref, gln_ref, gbins_ref, wout_shard_ref,
             kp_ref, pend_ref, carry_ref, wm_ref, wt_ref, gbias_ref, sa_w, ra_w, sb_w, rc_w, send_sems, recv_sems):
        n = pl.program_id(0)
        start, exchange, finish = _reduce_scatter_plan(_Copies(send_sems, recv_sems), 0, gwout_ref, WOUT_ROWS,
                                                       sa_w, ra_w, sb_w, rc_w, wout_shard_ref)
        tril = (lax.broadcasted_iota(jnp.int32, (BLOCK, BLOCK), 0)
                >= lax.broadcasted_iota(jnp.int32, (BLOCK, BLOCK), 1))

        @pl.when(n == 0)
        def _():
            gsink_ref[...] = jnp.zeros_like(gsink_ref)
            gbin_ref[...] = jnp.zeros_like(gbin_ref)
            carry_ref[...] = jnp.zeros_like(carry_ref)
            kp_ref[...] = jnp.zeros_like(kp_ref)
            gw_ref[...] = jnp.zeros_like(gw_ref)
            gln_ref[...] = jnp.zeros_like(gln_ref)
            gbins_ref[...] = jnp.zeros_like(gbins_ref)
            gbias_ref[...] = jnp.zeros_like(gbias_ref)
            _mask_sgu_weights(w_ref, wm_ref, wt_ref)
            start()

        pl.when(n == 3)(exchange)
        pl.when(n == 12)(finish)

        @pl.when(n > 0)
        def _():
            dp_ref[:, 0:ATTN_W] = pend_ref[:, 0:ATTN_W]
            dp_ref[:, GATE0:ATTN_SECTION] = pend_ref[:, ATTN_W:]

        @pl.when(n > last)
        def _():
            dp_ref[:, KV0:GATE0] = carry_ref[...].astype(MXU_DTYPE)

        @pl.when(n <= last)
        def _():
            us = us_ref[...]
            vs = vs_ref[...]
            lng = lng_ref[...]
            u, vhat, rstd, vln = _sgu_activations(us, vs, lng, lnb_ref[...])
            low_sgu = lax.broadcasted_iota(jnp.int32, (BLOCK, LANES), 1) < HALF
            sgu = {}

            def sgu_gates():
                mixed = _sgu_mix(vln, wm_ref, bias_ref)
                sgu["du"], sgu["dzs"], sgu["dm"] = [], [], []
                for pair in range(N_SGU_HEADS // 2):
                    cols = slice(pair * LANES, (pair + 1) * LANES)
                    dsg = dsg_ref[:, cols]
                    gate, gate_grad = _silu_and_grad(zs_ref[:, cols])
                    up = u[:, cols]
                    sgu["du"].append(dsg * mixed[pair] * gate)
                    sgu["dzs"].append(dsg * up * mixed[pair] * gate_grad)
                    dmixed = dsg * up * gate
                    gbias_ref[:, cols] += dmixed
                    sgu["dm"].append((jnp.where(low_sgu, dmixed, 0.0).astype(MXU_DTYPE),
                                      jnp.where(low_sgu, 0.0, dmixed).astype(MXU_DTYPE)))

            def sgu_grads():
                dvln_parts = []
                for pair in range(N_SGU_HEADS // 2):
                    dm_lo, dm_hi = sgu["dm"][pair]
                    vp = vln[:, pair * LANES:(pair + 1) * LANES]
                    gw_ref[2 * pair] += _dot(dm_lo, vp, NT)
                    gw_ref[2 * pair + 1] += _dot(dm_hi, vp, NT)
                    dvln_parts.append(_dot(wt_ref[2 * pair], dm_lo) + _dot(wt_ref[2 * pair + 1], dm_hi))
                dvln = jnp.concatenate(dvln_parts, axis=1)
                gln_ref[0:1, :] += jnp.sum(dvln * vhat, axis=0, keepdims=True)
                gln_ref[1:2, :] += jnp.sum(dvln, axis=0, keepdims=True)
                dvhat = dvln * lng
                dvg = rstd * (dvhat - jnp.mean(dvhat, axis=-1, keepdims=True)
                              - vhat * jnp.mean(dvhat * vhat, axis=-1, keepdims=True))
                dus = jnp.concatenate(sgu["du"], axis=1) * _gelu_grad(us)
                dvs = dvg * _gelu_grad(vs)
                dzs = jnp.concatenate(sgu["dzs"], axis=1)
                for k, val in enumerate((dus, dvs, dzs)):
                    dps_ref[:, k * SGU_W:(k + 1) * SGU_W] = val.astype(MXU_DTYPE)
                    gbins_ref[:, k * SGU_W:(k + 1) * SGU_W] += jnp.sum(val, axis=0, keepdims=True)

            valid = _window_mask(n)[0:BLOCK]
            low = lax.broadcasted_iota(jnp.int32, (BLOCK, LANES), 1) < HALF
            low_keys = lax.broadcasted_iota(jnp.int32, (2 * BLOCK, LANES), 1) < HALF
            lane_row = lax.broadcasted_iota(jnp.int32, (1, LANES), 1)
            gsink = jnp.zeros((1, LANES), F32)
            chains = [(g, par, i) for g in range(2) for par in range(2) for i in range(2)]
            kv = {(g, par): _kv_cat(kp_ref, kc_ref, 2 * g + par, False) for g in range(2) for par in range(2)}
            ones_keys = jnp.ones((2 * BLOCK, LANES), MXU_DTYPE)
            half_of_lane = lax.broadcasted_iota(jnp.int32, (LANES, 2 * LANES), 0) // HALF
            half_of_col = lax.broadcasted_iota(jnp.int32, (LANES, 2 * LANES), 1) // LANES
            sum_halves = (half_of_lane == half_of_col).astype(MXU_DTYPE)
            douts, deltas = [], []
            for pair in range(N_PAIRS):
                lanes = slice(pair * LANES, (pair + 1) * LANES)
                dg = d_ref[:, lanes]
                gate, gate_grad = _silu_and_grad(za_ref[:, lanes])
                o = o_ref[pair]
                dout = dg * gate
                dza = dg * o * gate_grad
                douts.append(dout.astype(MXU_DTYPE))
                deltas.append(_dot(dout * o, sum_halves))
                zl = slice(ATTN_W + pair * LANES, ATTN_W + (pair + 1) * LANES)
                pend_ref[:, zl] = dza.astype(MXU_DTYPE)
                gl = slice(GATE0 + pair * LANES, GATE0 + (pair + 1) * LANES)
                gbin_ref[:, gl] += jnp.sum(dza, axis=0, keepdims=True)

            first = {}

            def issue_first(k):
                g, par, i = chains[k]
                first[k] = (_dot(q_ref[2 * g + i], kv[g, par][0], NT), _dot(douts[2 * g + i], kv[g, par][1], NT))

            numerators = {}

            def issue_row_sums(k):
                g, par, i = chains[k]
                sink = sink_ref[4 * g + 2 * i + par]
                e, m = _softmax_numerator(jnp.where(valid, first[k][0], NEG_INF), sink)
                numerators[k] = (e, jnp.exp(sink - m), _dot(e, ones_keys))

            ahead = ATTN_BWD_AHEAD
            for k in range(ahead):
                issue_first(k)
            issue_row_sums(0)
            issue_row_sums(1)
            dqs, dk_parts, dv_parts = {}, {}, {}
            operands = {}

            def issue_last(k):
                g, par, i = chains[k]
                ds, ds_t, p_t = operands.pop(k)
                dq = _dot(ds, kv[g, par][0])
                dqs[g, i] = dq if par == 0 else dqs[g, i] + dq
                dk = _dot(ds_t, q_ref[2 * g + i])
                dv = _dot(p_t, douts[2 * g + i])
                dk_parts[g, par] = dk if i == 0 else dk_parts[g, par] + dk
                dv_parts[g, par] = dv if i == 0 else dv_parts[g, par] + dv

            for k, (g, par, i) in enumerate(chains):
                h = 4 * g + 2 * i + par
                delta = deltas[2 * g + i][:, par * LANES:(par + 1) * LANES]
                e, at_sink, row_sum = numerators[k]
                inv = 1.0 / (row_sum + at_sink)
                p = e * jnp.tile(inv, (1, 2))
                ds = p * (first[k][1] - jnp.tile(delta, (1, 2)))
                ds = ds.astype(MXU_DTYPE)
                operands[k] = (ds, ds.T, p.astype(MXU_DTYPE).T)
                total = jnp.sum(at_sink * inv * delta, axis=0, keepdims=True)
                gsink = jnp.where(lane_row == h, -total, gsink)
                if k + ahead < len(chains):
                    issue_first(k + ahead)
                if k + 2 < len(chains):
                    issue_row_sums(k + 2)
                if k > 0:
                    issue_last(k - 1)
                if k == SGU_GATES_AFTER_CHAIN:
                    sgu_gates()
                if k == SGU_GRADS_AFTER_CHAIN:
                    sgu_grads()
            issue_last(len(chains) - 1)
            for pair in range(N_PAIRS):
                g, i = divmod(pair, 2)
                dq = dqs[g, i] * SCALE
                lanes = slice(pair * LANES, (pair + 1) * LANES)
                pend_ref[:, lanes] = dq.astype(MXU_DTYPE)
                gbin_ref[:, lanes] += jnp.sum(dq, axis=0, keepdims=True)
            gsink_ref[...] += gsink
            for k, parts in enumerate((dk_parts, dv_parts)):
                masked = {key: jnp.where(low_keys if key[1] == 0 else jnp.logical_not(low_keys), val, 0.0)
                          for key, val in parts.items()}
                both = (masked[0, 0] + masked[1, 1]
                        + pltpu.roll(masked[0, 1] + masked[1, 0], HALF, 1))
                lanes = slice(k * KV_W, (k + 1) * KV_W)
                done = carry_ref[:, lanes] + both[0:BLOCK]
                dp_ref[:, KV0 + k * KV_W:KV0 + (k + 1) * KV_W] = done.astype(MXU_DTYPE)
                carry_ref[:, lanes] = both[BLOCK:]
                gbin_ref[:, KV0 + k * KV_W:KV0 + (k + 1) * KV_W] += jnp.sum(both, axis=0, keepdims=True)
            kp_ref[...] = kc_ref[...]

        @pl.when(n == last)
        def _():
            for hh in range(N_SGU_HEADS):
                gw_ref[hh] = jnp.where(tril, gw_ref[hh], 0.0)
            head_of_lane = lax.broadcasted_iota(jnp.int32, (N_SGU_HEADS, SGU_W), 1) // HEAD_DIM
            select = (head_of_lane == lax.broadcasted_iota(jnp.int32, (N_SGU_HEADS, SGU_W), 0)).astype(F32)
            gb_ref[...] = lax.dot_general(select, gbias_ref[...], NT, precision=lax.Precision.HIGHEST,
                                          preferred_element_type=F32)

    at = lambda n: jnp.minimum(n, last)
    blk = lambda w: pl.BlockSpec((BLOCK, w), lambda n: (at(n), 0))
    tiles = pl.BlockSpec((N_PAIRS, BLOCK, LANES), lambda n: (0, at(n), 0))
    section = lambda k: pl.BlockSpec((None, BLOCK, SGU_W), lambda n: (k, at(n), 0))
    return pl.pallas_call(
        body,
        name="mixers_bwd",
        grid=(N_BLOCKS + 1,),
        in_specs=[pl.BlockSpec(memory_space=pltpu.SMEM),
                  section(0),
                  tiles,
                  blk(KVX_W),
                  tiles,
                  section(0),
                  section(1),
                  section(1), section(2), section(3),
                  _full((1, SGU_W)), _full((1, SGU_W)), _full((N_SGU_HEADS, BLOCK, BLOCK)), _full((BLOCK, SGU_W)),
                  VMEM_SPEC],
        out_specs=(pl.BlockSpec((BLOCK, ATTN_SECTION), lambda n: (jnp.maximum(n - 1, 0), 0)),
                   _full((1, LANES)), _full((1, ATTN_SECTION)),
                   pl.BlockSpec((BLOCK, SGU_SECTION), lambda n: (at(n), 0)),
                   _full((N_SGU_HEADS, BLOCK, BLOCK)), _full((N_SGU_HEADS, BLOCK)),
                   _full((8, SGU_W)), _full((1, SGU_SECTION)), VMEM_SPEC),
        out_shape=(jax.ShapeDtypeStruct((SEQ, ATTN_SECTION), MXU_DTYPE),
                   jax.ShapeDtypeStruct((1, LANES), F32),
                   jax.ShapeDtypeStruct((1, ATTN_SECTION), F32),
                   jax.ShapeDtypeStruct((SEQ, SGU_SECTION), MXU_DTYPE),
                   jax.ShapeDtypeStruct((N_SGU_HEADS, BLOCK, BLOCK), F32),
                   jax.ShapeDtypeStruct((N_SGU_HEADS, BLOCK), F32),
                   jax.ShapeDtypeStruct((8, SGU_W), F32),
                   jax.ShapeDtypeStruct((1, SGU_SECTION), F32),
                   jax.ShapeDtypeStruct((WOUT_ROWS, D_MODEL), F32)),
        scratch_shapes=([pltpu.VMEM((BLOCK, KVX_W), MXU_DTYPE),
                         pltpu.VMEM((BLOCK, 2 * ATTN_W), MXU_DTYPE), pltpu.VMEM((BLOCK, 2 * KV_W), F32),
                         pltpu.VMEM((N_SGU_HEADS, BLOCK, BLOCK), MXU_DTYPE),
                         pltpu.VMEM((N_SGU_HEADS, BLOCK, BLOCK), MXU_DTYPE), pltpu.VMEM((BLOCK, SGU_W), F32)]
                        + _reduce_scatter_scratch(WOUT_ROWS, D_MODEL, COMM_DTYPE) + _dma_sems(REDUCE_SEMS)),
        compiler_params=_params(("arbitrary",), VMEM_LIMIT),
    )(sinks, dmix, q, kvx, out, gates, dmix, gates, gates, gates, ln_g, ln_b, sgu_w, bias_full, gwout)


def _in_proj_bwd(dpa, dps, win_t, x, norm_g, gres, gwin, vec_parts):
    tm = TOKEN_TILE
    steps = SEQ // tm
    n_parts = len(vec_parts)

    def body(da_ref, ds_ref, w_ref, x_ref, g_ref, gres_ref, gwin_ref, *rest):
        part_refs = rest[:n_parts]
        gx_ref, shard_ref, vec_out_ref, gng_ref, sa, ra, sb, rc, vec_ref, ra_vec, slots, send_sems, recv_sems = (
            rest[n_parts:])
        step = pl.program_id(0)
        copies = _Copies(send_sems, recv_sems)
        start, exchange, finish = _reduce_scatter_plan(copies, 0, gwin_ref, WIN_ROWS, sa, ra, sb, rc, shard_ref)

        @pl.when(step == 0)
        def _():
            gng_ref[...] = jnp.zeros_like(gng_ref)
            start()

        pl.when(step == 2)(exchange)

        dh = _dot(da_ref[...], w_ref[0:ATTN_SECTION, :]) + _dot(ds_ref[...], w_ref[ATTN_SECTION:, :])
        xv = x_ref[...]
        r = lax.rsqrt(jnp.mean(xv * xv, axis=-1, keepdims=True) + NORM_EPS)
        xn = xv * r
        gng_ref[...] += jnp.sum(dh * xn, axis=0, keepdims=True)
        dxn = dh * g_ref[...]
        gx_ref[...] = r * (dxn - xn * jnp.mean(dxn * xn, axis=-1, keepdims=True)) + gres_ref[...]

        @pl.when(step == steps - 1)
        def _():
            finish()
            _all_reduce_vectors(copies, REDUCE_SEMS, gng_ref, *part_refs, vec_out_ref, vec_ref, ra_vec, slots)

    tile = lambda w: pl.BlockSpec((tm, w), lambda i: (i, 0))
    return pl.pallas_call(
        body,
        name="in_proj_bwd",
        grid=(steps,),
        in_specs=[tile(ATTN_SECTION), tile(SGU_SECTION), _full((IN_W, D_MODEL)), tile(D_MODEL),
                  _full((1, D_MODEL)), tile(D_MODEL), VMEM_SPEC] + [VMEM_SPEC] * n_parts,
        out_specs=(tile(D_MODEL), VMEM_SPEC, VMEM_SPEC),
        out_shape=(jax.ShapeDtypeStruct((SEQ, D_MODEL), F32),
                   jax.ShapeDtypeStruct((WIN_ROWS, D_MODEL), F32),
                   jax.ShapeDtypeStruct((VEC_ROWS, IN_W), F32)),
        scratch_shapes=([pltpu.VMEM((1, D_MODEL), F32)] + _reduce_scatter_scratch(WIN_ROWS, D_MODEL, COMM_DTYPE)
                        + _vector_scratch() + _dma_sems(REDUCE_SEMS + VECTOR_SEMS)),
        compiler_params=_params(("arbitrary",), VMEM_LIMIT),
    )(dpa, dps, win_t, x, norm_g, gres, gwin, *vec_parts)


def _win_grad(dpa, dps, h, gsguw):
    rows = 256
    n_attn = ATTN_SECTION // rows
    steps = n_attn + SGU_SECTION // rows

    def body(da_ref, ds_ref, h_ref, gsguw_ref, o_ref, sguw_full_ref, sa, ra, sb, rc, landing, send_sems, recv_sems):
        step = pl.program_id(0)
        copies = _Copies(send_sems, recv_sems)
        own_sguw = landing.at[_block_rows(_place(), SGUW_ROWS), :]
        start, exchange, finish = _reduce_scatter_plan(copies, 0, gsguw_ref, SGUW_ROWS, sa, ra, sb, rc, own_sguw)
        gather = _gather_plan(copies, REDUCE_SEMS, landing, SGUW_ROWS)

        pl.when(step == 0)(start)
        pl.when(step == 2)(exchange)

        @pl.when(step == 5)
        def _():
            finish()
            gather[0]()

        pl.when(step == 7)(gather[1])

        @pl.when(step < n_attn)
        def _():
            o_ref[...] = _dot(da_ref[...], h_ref[...], TN)

        @pl.when(step >= n_attn)
        def _():
            o_ref[...] = _dot(ds_ref[...], h_ref[...], TN)

        @pl.when(step == steps - 1)
        def _():
            gather[2]()
            sguw_full_ref[...] = landing[...]

    return pl.pallas_call(
        body,
        name="win_grad",
        grid=(steps,),
        in_specs=[pl.BlockSpec((SEQ, rows), lambda i: (0, jnp.minimum(i, n_attn - 1))),
                  pl.BlockSpec((SEQ, rows), lambda i: (0, jnp.maximum(i - n_attn, 0))),
                  _full((SEQ, D_MODEL)), VMEM_SPEC],
        out_specs=(pl.BlockSpec((rows, D_MODEL), lambda i: (i, 0)), _full((N_SGU_HEADS * BLOCK, BLOCK))),
        out_shape=(jax.ShapeDtypeStruct((IN_W, D_MODEL), F32),
                   jax.ShapeDtypeStruct((N_SGU_HEADS * BLOCK, BLOCK), F32)),
        scratch_shapes=(_reduce_scatter_scratch(SGUW_ROWS, BLOCK, F32)
                        + [pltpu.VMEM((N_SGU_HEADS * BLOCK, BLOCK), F32)]
                        + _dma_sems(REDUCE_SEMS + GATHER_SEMS)),
        compiler_params=_params(("arbitrary",), VMEM_LIMIT),
    )(dpa, dps, h, gsguw)


VEC_NORM_G, VEC_B_IN, VEC_SINKS, VEC_LN_G, VEC_LN_B, VEC_B_OUT, VEC_FINAL_G, VEC_LOSS, VEC_SGU_B = 0, 1, 2, 3, 4, 5, 6, 7, 8


def _adamw(w, g, m, v):
    m = ADAM_B1 * m + (1.0 - ADAM_B1) * g
    v = ADAM_B2 * v + (1.0 - ADAM_B2) * (g * g)
    m_hat = m / (1.0 - ADAM_B1 ** ADAM_STEP)
    v_hat = v / (1.0 - ADAM_B2 ** ADAM_STEP)
    delta = -ADAM_LR * (m_hat / (jnp.sqrt(v_hat) + ADAM_EPS) + ADAM_WD * w)
    return delta, m, v


def _adamw_shard(name, g, w, m, v, block_rows):
    def body(g_ref, w_ref, m_ref, v_ref, d_ref, nm_ref, nv_ref):
        d_ref[...], nm_ref[...], nv_ref[...] = _adamw(w_ref[...], g_ref[...], m_ref[...], v_ref[...])

    rows, cols = w.shape
    spec = pl.BlockSpec((block_rows, cols), lambda i: (i, 0))
    return pl.pallas_call(
        body,
        name=name,
        grid=(rows // block_rows,),
        in_specs=[spec] * 4,
        out_specs=(spec,) * 3,
        out_shape=(jax.ShapeDtypeStruct(w.shape, F32),) * 3,
        compiler_params=_params(("arbitrary",)),
    )(g, w, m, v)


VECTOR_SEMS = 4


def _vector_scratch():
    return [pltpu.VMEM((VEC_ROWS, IN_W), F32), pltpu.VMEM((VEC_ROWS, IN_W), F32),
            pltpu.VMEM((4 * VEC_ROWS, IN_W), F32)]


def _all_reduce_vectors(copies, sem0, gng_ref, gba_ref, gbs_ref, gsink_ref, gln_ref, gsgub_ref, vec4_ref, out_ref,
                        vec_ref, ra_vec, slots):
    x, y, c = _place()
    vec_ref[...] = jnp.zeros_like(vec_ref)
    vec_ref[VEC_NORM_G:VEC_NORM_G + 1, 0:D_MODEL] = gng_ref[...]
    vec_ref[VEC_B_IN:VEC_B_IN + 1, 0:ATTN_SECTION] = gba_ref[...]
    vec_ref[VEC_B_IN:VEC_B_IN + 1, ATTN_SECTION:IN_W] = gbs_ref[...]
    vec_ref[VEC_SINKS:VEC_SINKS + 1, 0:LANES] = gsink_ref[...]
    vec_ref[VEC_LN_G:VEC_LN_G + 1, 0:SGU_W] = gln_ref[0:1, :]
    vec_ref[VEC_LN_B:VEC_LN_B + 1, 0:SGU_W] = gln_ref[1:2, :]
    vec_ref[VEC_B_OUT:VEC_B_OUT + 1, 0:D_MODEL] = vec4_ref[2:3, :]
    vec_ref[VEC_FINAL_G:VEC_FINAL_G + 1, 0:D_MODEL] = vec4_ref[1:2, :]
    vec_ref[VEC_LOSS:VEC_LOSS + 1, 0:D_MODEL] = vec4_ref[0:1, :]
    vec_ref[VEC_SGU_B:VEC_SGU_B + N_SGU_HEADS, 0:BLOCK] = gsgub_ref[...]

    to_sibling = copies(sem0, vec_ref, ra_vec, (x, y, 1 - c))
    to_sibling.start()
    to_sibling.wait_recv()

    def chip_slot(place):
        return slots.at[pl.ds(pl.multiple_of((2 * place[0] + place[1]) * VEC_ROWS, 8), VEC_ROWS), :]

    mine = chip_slot((x, y))
    mine[...] = vec_ref[...] + ra_vec[...]
    to_chips = [copies(sem0 + i, mine, mine, (*_chip(rel), c)) for i, rel in enumerate(RELATIONS[1:], start=1)]
    for cp in to_chips:
        cp.start()
    for i, rel in enumerate(RELATIONS[1:], start=1):
        theirs = chip_slot(_chip(rel))
        copies(sem0 + i, theirs, theirs, (x, y, c)).wait_recv()
    out_ref[...] = ((slots[0:VEC_ROWS, :] + slots[VEC_ROWS:2 * VEC_ROWS, :])
                    + slots[2 * VEC_ROWS:3 * VEC_ROWS, :]) + slots[3 * VEC_ROWS:, :]
    to_sibling.wait_send()
    for cp in to_chips:
        cp.wait_send()


def _adamw_replicated(vec, gsguw, weights, m_state, v_state):
    n = len(SMALL)

    def body(*refs):
        vec_ref, gsguw_ref = refs[0], refs[1]
        w_refs, m_refs, v_refs = (refs[2 + k * n:2 + (k + 1) * n] for k in range(3))
        outs = refs[2 + 3 * n:]
        g_refs, d_refs, nm_refs, nv_refs = (outs[k * n:(k + 1) * n] for k in range(4))
        for i, (_, row, shape) in enumerate(SMALL):
            g = gsguw_ref[...] if row is None else vec_ref[row:row + shape[0], 0:shape[1]]
            g_refs[i][...] = g
            d_refs[i][...], nm_refs[i][...], nv_refs[i][...] = _adamw(
                w_refs[i][...], g, m_refs[i][...], v_refs[i][...])

    shapes = tuple(jax.ShapeDtypeStruct(shape, F32) for _, _, shape in SMALL)
    outs = pl.pallas_call(
        body,
        name="adamw_replicated",
        in_specs=[VMEM_SPEC] * (2 + 3 * n),
        out_specs=(VMEM_SPEC,) * (4 * n),
        out_shape=shapes * 4,
    )(vec, gsguw, *weights, *m_state, *v_state)
    return tuple(outs[k * n:(k + 1) * n] for k in range(4))


SMALL = (
    ("norm_g", VEC_NORM_G, (1, D_MODEL)),
    ("b_in", VEC_B_IN, (1, IN_W)),
    ("attn_sinks", VEC_SINKS, (1, N_Q_HEADS)),
    ("sgu_ln_g", VEC_LN_G, (1, SGU_W)),
    ("sgu_ln_b", VEC_LN_B, (1, SGU_W)),
    ("sgu_w", None, (N_SGU_HEADS * BLOCK, BLOCK)),
    ("sgu_b", VEC_SGU_B, (N_SGU_HEADS, BLOCK)),
    ("b_out", VEC_B_OUT, (1, D_MODEL)),
    ("final_norm_g", VEC_FINAL_G, (1, D_MODEL)),
)


def _local_grads(x, target, h, win_t, wout_shard, norm_g, b_in, attn_sinks, sgu_ln_g, sgu_ln_b, sgu_w, sgu_b, b_out,
                 final_g):
    sinks = attn_sinks.reshape(N_Q_HEADS)
    bias_full = jnp.repeat(sgu_b.T, HEAD_DIM, axis=1)
    q, kvx, gates, wout = _in_proj(h, b_in, win_t, wout_shard)
    out, gres, dmix, gwout, vec4 = _mixers_out_proj(sinks, q, kvx, gates, sgu_ln_g, sgu_ln_b, sgu_w, bias_full,
                                                    x, target, wout, b_out, final_g)
    dpa, gsink, gbin_a, dps, gsguw, gsgub, gln, gbin_s, gwout_shard = _mixers_bwd(
        sinks, dmix, q, kvx, out, gates, sgu_ln_g, sgu_ln_b, sgu_w, bias_full, gwout)
    gwin, gsguw_sum = _win_grad(dpa, dps, h, gsguw.reshape(N_SGU_HEADS * BLOCK, BLOCK))
    grad_x, gwin_shard, vec = _in_proj_bwd(dpa, dps, win_t, x, norm_g, gres, gwin,
                                           (gbin_a, gbin_s, gsink, gln, gsgub, vec4))
    return grad_x, gwin_shard, gwout_shard, gsguw_sum, vec


def kernel(x, norm_g, w_in, b_in, attn_sinks, sgu_ln_g, sgu_ln_b, sgu_w, sgu_b, w_out, b_out, final_norm_g, loss_target, m_norm_g, m_w_in, m_b_in, m_attn_sinks, m_sgu_ln_g, m_sgu_ln_b, m_sgu_w, m_sgu_b, m_w_out, m_b_out, m_final_norm_g, v_norm_g, v_w_in, v_b_in, v_attn_sinks, v_sgu_ln_g, v_sgu_ln_b, v_sgu_w, v_sgu_b, v_w_out, v_b_out, v_final_norm_g):
    given = dict(norm_g=norm_g, b_in=b_in, attn_sinks=attn_sinks, sgu_ln_g=sgu_ln_g, sgu_ln_b=sgu_ln_b,
                 sgu_w=sgu_w, sgu_b=sgu_b, b_out=b_out, final_norm_g=final_norm_g)
    m_given = dict(norm_g=m_norm_g, b_in=m_b_in, attn_sinks=m_attn_sinks, sgu_ln_g=m_sgu_ln_g,
                   sgu_ln_b=m_sgu_ln_b, sgu_w=m_sgu_w, sgu_b=m_sgu_b, b_out=m_b_out, final_norm_g=m_final_norm_g)
    v_given = dict(norm_g=v_norm_g, b_in=v_b_in, attn_sinks=v_attn_sinks, sgu_ln_g=v_sgu_ln_g,
                   sgu_ln_b=v_sgu_ln_b, sgu_w=v_sgu_w, sgu_b=v_sgu_b, b_out=v_b_out, final_norm_g=v_final_norm_g)

    win_t, h = _all_gather_win(w_in[0].T, x[0], norm_g)
    grad_x, gwin_t, gwout, gsguw, vec = _local_grads(
        x[0], loss_target[0], h, win_t, w_out[0], norm_g, b_in, attn_sinks, sgu_ln_g, sgu_ln_b, sgu_w[0], sgu_b[0],
        b_out, final_norm_g.reshape(1, D_MODEL))

    t = lambda a: a[0].T
    d_win, nm_win, nv_win = _adamw_shard("adamw_w_in", gwin_t, t(w_in), t(m_w_in), t(v_w_in), WIN_ROWS // 2)
    d_wout, nm_wout, nv_wout = _adamw_shard("adamw_w_out", gwout, w_out[0], m_w_out[0], v_w_out[0], WOUT_ROWS)
    as_2d = lambda d: [d[name].reshape(shape) for name, _, shape in SMALL]
    loss = vec[VEC_LOSS, 0]
    small = _adamw_replicated(vec, gsguw, as_2d(given), as_2d(m_given), as_2d(v_given))

    def assemble(big_in, big_out, k):
        vals = {name: small[k][i].reshape(given[name].shape) for i, (name, _, _) in enumerate(SMALL)}
        vals["w_in"] = big_in.T[None]
        vals["w_out"] = big_out[None]
        order = ("norm_g", "w_in", "b_in", "attn_sinks", "sgu_ln_g", "sgu_ln_b", "sgu_w", "sgu_b", "w_out",
                 "b_out", "final_norm_g")
        return [vals[name] for name in order]

    return (loss, grad_x[None],
            *assemble(gwin_t, gwout, 0), *assemble(d_win, d_wout, 1),
            *assemble(nm_win, nm_wout, 2), *assemble(nv_win, nv_wout, 3))
```

```python
import functools
import math

import jax
import jax.numpy as jnp
from jax import lax
from jax.experimental import pallas as pl
from jax.experimental.pallas import tpu as pltpu

F32 = jnp.float32
BF16 = jnp.bfloat16
MXU_DTYPE = BF16
COMM_DTYPE = BF16

D_MODEL = 1024
SEQ = 4096
HEAD_DIM = 64
N_Q_HEADS = 8
Q_PER_KV = 4
BLOCK = 128
N_BLOCKS = SEQ // BLOCK
ATTN_W = 512
KV_W = 128
SGU_W = 512
N_SGU_HEADS = 8
IN_W = 2816
NORM_EPS = 1e-5
NEG_INF = -1e30
SCALE = HEAD_DIM ** -0.5
KV0 = ATTN_W
GATE0 = ATTN_W + 2 * KV_W
SGU0 = GATE0 + ATTN_W
ATTN_SECTION = SGU0
SGU_SECTION = IN_W - SGU0

ADAM_LR = 0.001
ADAM_B1 = 0.9
ADAM_B2 = 0.999
ADAM_EPS = 1e-08
ADAM_WD = 0.01
ADAM_STEP = 10

N_DEV = 8
WIN_ROWS = IN_W // N_DEV
WOUT_ROWS = D_MODEL // N_DEV
SGUW_ROWS = N_SGU_HEADS * BLOCK // N_DEV
VEC_ROWS = 16
MESH = pl.DeviceIdType.MESH

LANES = 128
HALF = LANES // 2
N_PAIRS = N_Q_HEADS * HEAD_DIM // LANES
KVX_W = 12 * LANES
TOKEN_TILE = 256
FWD_TOKEN_TILE = 512
ATTN_FWD_AHEAD = 4
FUSED_BLOCKS = 2
SGU_MIX_AFTER_CHAIN = 0
SGU_GATES_AFTER_CHAIN = 1
SGU_GRADS_AFTER_CHAIN = 5
ATTN_BWD_AHEAD = 3
VMEM_LIMIT = 56 * 1024 * 1024

NN = (((1,), (0,)), ((), ()))
NT = (((1,), (1,)), ((), ()))
TN = (((0,), (0,)), ((), ()))


def _dot(a, b, dims=NN):
    return lax.dot_general(a.astype(MXU_DTYPE), b.astype(MXU_DTYPE), dims, preferred_element_type=F32)


def _gelu(x):
    return x * (lax.erf(x * (1.0 / math.sqrt(2.0))) + 1.0) * 0.5


def _gelu_grad(x):
    cdf = (lax.erf(x * (1.0 / math.sqrt(2.0))) + 1.0) * 0.5
    return cdf + x * jnp.exp(-0.5 * x * x) * (1.0 / math.sqrt(2.0 * math.pi))


def _silu_and_grad(z):
    s = jax.nn.sigmoid(z)
    return z * s, s * (1.0 + z * (1.0 - s))


def _params(semantics=None, vmem=None):
    kw = {}
    if semantics is not None:
        kw["dimension_semantics"] = semantics
    if vmem is not None:
        kw["vmem_limit_bytes"] = vmem
    return pltpu.CompilerParams(**kw)


def _full(shape):
    return pl.BlockSpec(shape, lambda *_: (0,) * len(shape))


VMEM_SPEC = pl.BlockSpec(memory_space=pltpu.VMEM)


RELATIONS = ((0, 0), (1, 0), (0, 1), (1, 1))


def _place():
    return lax.axis_index("x"), lax.axis_index("y"), lax.axis_index("c")


def _chip(rel):
    x, y, _ = _place()
    return (1 - x if rel[0] else x, 1 - y if rel[1] else y)


def _block_rows(place, n_rows):
    px, py, pc = place
    return pl.ds(pl.multiple_of((4 * px + 2 * py + pc) * n_rows, 16), n_rows)


class _Copies:
    def __init__(self, send_sems, recv_sems):
        self.send_sems, self.recv_sems = send_sems, recv_sems

    def __call__(self, k, src, dst, to):
        return pltpu.make_async_remote_copy(src_ref=src, dst_ref=dst, send_sem=self.send_sems.at[k],
                                            recv_sem=self.recv_sems.at[k], device_id=to, device_id_type=MESH)


def _gather_plan(copies, sem0, full_ref, n_rows):
    x, y, c = _place()
    me, sibling = (x, y, c), (x, y, 1 - c)
    chips = [_chip(rel) for rel in RELATIONS[1:]]

    def cp(k, block, to):
        rows = full_ref.at[_block_rows(block, n_rows), :]
        return copies(sem0 + k, rows, rows, to)

    first = [cp(0, me, sibling)] + [cp(1 + j, me, (*chip, c)) for j, chip in enumerate(chips)]
    passed = [cp(4 + j, (*chip, c), sibling) for j, chip in enumerate(chips)]

    def start():
        for f in first:
            f.start()

    def forward():
        for j, chip in enumerate(chips):
            cp(1 + j, (*chip, c), me).wait_recv()
            passed[j].start()

    def finish():
        cp(0, sibling, me).wait_recv()
        for j, chip in enumerate(chips):
            cp(4 + j, (*chip, 1 - c), me).wait_recv()
        for f in first + passed:
            f.wait_send()

    return start, forward, finish


GATHER_SEMS = 7


def _reduce_scatter_plan(copies, sem0, part_ref, n_rows, sa, ra, sb, rc, res_ref):
    x, y, c = _place()
    sibling = (x, y, 1 - c)
    n = n_rows
    level1 = copies(sem0, sa, ra, sibling)

    def level2(i):
        slot = pl.ds((i - 1) * n, n)
        return copies(sem0 + i, sb.at[slot, :], rc.at[slot, :], (*_chip(RELATIONS[i]), c))

    def start():
        for i, rel in enumerate(RELATIONS):
            sa[i * n:(i + 1) * n, :] = part_ref[_block_rows((*_chip(rel), 1 - c), n), :].astype(sa.dtype)
        level1.start()

    def exchange():
        level1.wait_recv()
        for i, rel in enumerate(RELATIONS):
            total = part_ref[_block_rows((*_chip(rel), c), n), :] + ra[i * n:(i + 1) * n, :].astype(F32)
            if i == 0:
                res_ref[...] = total
            else:
                sb[(i - 1) * n:i * n, :] = total.astype(sb.dtype)
                level2(i).start()

    def finish():
        acc = res_ref[...]
        for i in range(1, len(RELATIONS)):
            level2(i).wait_recv()
            acc = acc + rc[(i - 1) * n:i * n, :].astype(F32)
        res_ref[...] = acc
        level1.wait_send()
        for i in range(1, len(RELATIONS)):
            level2(i).wait_send()

    return start, exchange, finish


REDUCE_SEMS = 4


def _reduce_scatter_scratch(n_rows, width, dtype):
    return [pltpu.VMEM((4 * n_rows, width), dtype), pltpu.VMEM((4 * n_rows, width), dtype),
            pltpu.VMEM((3 * n_rows, width), dtype), pltpu.VMEM((3 * n_rows, width), dtype)]


def _dma_sems(n):
    return [pltpu.SemaphoreType.DMA((n,)), pltpu.SemaphoreType.DMA((n,))]


def _all_gather_win(win_t_shard, x, norm_g):
    tm = FWD_TOKEN_TILE
    steps = SEQ // tm

    def body(win_ref, x_ref, g_ref, full_ref, h_ref, landing, send_sems, recv_sems):
        step = pl.program_id(0)
        start, forward, finish = _gather_plan(_Copies(send_sems, recv_sems), 0, landing, WIN_ROWS)

        @pl.when(step == 0)
        def _():
            landing[_block_rows(_place(), WIN_ROWS), :] = win_ref[...].astype(COMM_DTYPE)
            start()

        xv = x_ref[...]
        r = lax.rsqrt(jnp.mean(xv * xv, axis=-1, keepdims=True) + NORM_EPS)
        h_ref[...] = ((xv * r) * g_ref[...]).astype(MXU_DTYPE)

        @pl.when(step == steps - 1)
        def _():
            forward()
            finish()
            full_ref[...] = landing[...]

    return pl.pallas_call(
        body,
        name="all_gather_win",
        grid=(steps,),
        in_specs=[VMEM_SPEC, pl.BlockSpec((tm, D_MODEL), lambda i: (i, 0)), _full((1, D_MODEL))],
        out_specs=(_full((IN_W, D_MODEL)), pl.BlockSpec((tm, D_MODEL), lambda i: (i, 0))),
        out_shape=(jax.ShapeDtypeStruct((IN_W, D_MODEL), COMM_DTYPE),
                   jax.ShapeDtypeStruct((SEQ, D_MODEL), MXU_DTYPE)),
        scratch_shapes=[pltpu.VMEM((IN_W, D_MODEL), COMM_DTYPE)] + _dma_sems(GATHER_SEMS),
        compiler_params=_params(("arbitrary",), VMEM_LIMIT),
    )(win_t_shard, x, norm_g)


def _in_proj(h, b_in, win_t, wout_shard):
    tm = FWD_TOKEN_TILE
    steps = SEQ // tm

    def body(h_ref, b_ref, w_ref, wout_ref, q_ref, kvx_ref, gate_ref, wfull_ref, landing, send_sems, recv_sems):
        step = pl.program_id(0)
        start, forward, finish = _gather_plan(_Copies(send_sems, recv_sems), 0, landing, WOUT_ROWS)

        @pl.when(step == 0)
        def _():
            landing[_block_rows(_place(), WOUT_ROWS), :] = wout_ref[...].astype(COMM_DTYPE)
            start()

        pl.when(step == steps // 2)(forward)

        h = h_ref[...]

        def proj(lo, hi):
            return _dot(h, w_ref[lo:hi, :], NT) + b_ref[:, lo:hi]

        qs = proj(0, ATTN_W) * SCALE
        for pair in range(N_PAIRS):
            q_ref[pair] = qs[:, pair * LANES:(pair + 1) * LANES].astype(MXU_DTYPE)
        kv = proj(KV0, GATE0)
        low = lax.broadcasted_iota(jnp.int32, (tm, LANES), 1) < HALF
        for i in range(2):
            t = kv[:, i * LANES:(i + 1) * LANES]
            rot = pltpu.roll(t, HALF, 1)
            variants = (jnp.where(low, t, 0.0), jnp.where(low, 0.0, rot),
                        jnp.where(low, rot, 0.0), jnp.where(low, 0.0, t))
            for j, val in enumerate(variants):
                col = (4 * i + j) * LANES
                kvx_ref[:, col:col + LANES] = val.astype(MXU_DTYPE)
                if i == 1:
                    ones_elsewhere = jnp.where(low == (j % 2 == 0), val, 1.0)
                    kvx_ref[:, col + 4 * LANES:col + 5 * LANES] = ones_elsewhere.astype(MXU_DTYPE)
        for k in range(4):
            gate_ref[k] = proj(GATE0 + k * SGU_W, GATE0 + (k + 1) * SGU_W)

        @pl.when(step == steps - 1)
        def _():
            finish()
            wfull_ref[...] = landing[...]

    return pl.pallas_call(
        body,
        name="in_proj",
        grid=(steps,),
        in_specs=[pl.BlockSpec((tm, D_MODEL), lambda i: (i, 0)),
                  _full((1, IN_W)), _full((IN_W, D_MODEL)), VMEM_SPEC],
        out_specs=(pl.BlockSpec((N_PAIRS, tm, LANES), lambda i: (0, i, 0)),
                   pl.BlockSpec((tm, KVX_W), lambda i: (i, 0)),
                   pl.BlockSpec((4, tm, SGU_W), lambda i: (0, i, 0)),
                   _full((D_MODEL, D_MODEL))),
        out_shape=(jax.ShapeDtypeStruct((N_PAIRS, SEQ, LANES), MXU_DTYPE),
                   jax.ShapeDtypeStruct((SEQ, KVX_W), MXU_DTYPE),
                   jax.ShapeDtypeStruct((4, SEQ, SGU_W), F32),
                   jax.ShapeDtypeStruct((D_MODEL, D_MODEL), COMM_DTYPE)),
        scratch_shapes=[pltpu.VMEM((D_MODEL, D_MODEL), COMM_DTYPE)] + _dma_sems(GATHER_SEMS),
        compiler_params=_params(("arbitrary",), VMEM_LIMIT),
    )(h, b_in, win_t, wout_shard)


def _window_mask(n):
    qi = lax.broadcasted_iota(jnp.int32, (2 * BLOCK, 2 * BLOCK), 0) & (BLOCK - 1)
    p = lax.broadcasted_iota(jnp.int32, (2 * BLOCK, 2 * BLOCK), 1) - BLOCK
    in_window = jnp.logical_and(p <= qi, p > qi - BLOCK)
    return jnp.logical_and(in_window, jnp.logical_or(p >= 0, n > 0))


def _sink_column(sink_ref, g, par):
    return jnp.concatenate([jnp.full((BLOCK, 1), sink_ref[4 * g + par], F32),
                            jnp.full((BLOCK, 1), sink_ref[4 * g + 2 + par], F32)], axis=0)


def _kv_cat(kp_ref, kc_ref, var, with_ones):
    kcol, vcol = var * LANES, (var + (8 if with_ones else 4)) * LANES
    return (jnp.concatenate([kp_ref[:, kcol:kcol + LANES], kc_ref[:, kcol:kcol + LANES]], axis=0),
            jnp.concatenate([kp_ref[:, vcol:vcol + LANES], kc_ref[:, vcol:vcol + LANES]], axis=0))


def _softmax_numerator(s, sink):
    m = jnp.maximum(jnp.max(s, axis=1, keepdims=True), sink)
    return jnp.exp(s - m), m


def _mixers_out_proj(sinks, q, kvx, gates, ln_g, ln_b, sgu_w, bias_full, x, target, wout, b_out, final_g):
    tm = FUSED_BLOCKS * BLOCK
    n_tiles = SEQ // tm

    def body(sink_ref, q_ref, kc_ref, za_ref, us_ref, vs_ref, zs_ref, lng_ref, lnb_ref, w_ref, bias_ref,
             x_ref, t_ref, wout_ref, b_ref, gf_ref,
             out_ref, gres_ref, dmix_ref, gw_ref, vec_ref,
             kp_ref, wm_ref, mixed_next, mixed_cur, out_stage, gb_ref):
        step = pl.program_id(0)

        @pl.when(step == 0)
        def _():
            kp_ref[...] = jnp.zeros_like(kp_ref)
            _mask_sgu_weights(w_ref, wm_ref)
            gw_ref[...] = jnp.zeros_like(gw_ref)
            vec_ref[...] = jnp.zeros_like(vec_ref)
            mixed_cur[...] = jnp.zeros_like(mixed_cur)

        def mixers_block(b, after_chain=()):
            rows = slice(b * BLOCK, (b + 1) * BLOCK)
            kc = kc_ref.at[rows, :]
            u, _, _, vln = _sgu_activations(us_ref[rows, :], vs_ref[rows, :], lng_ref[...], lnb_ref[...])

            valid = _window_mask(step * FUSED_BLOCKS + b)[0:BLOCK]
            chains = [(g, par, i) for g in range(2) for par in range(2) for i in range(2)]
            kv = {(g, par): _kv_cat(kp_ref, kc, 2 * g + par, True) for g in range(2) for par in range(2)}
            scores, outs = {}, {}

            def issue_scores(k):
                g, par, i = chains[k]
                scores[k] = _dot(q_ref[2 * g + i, rows, :], kv[g, par][0], NT)

            ahead = ATTN_FWD_AHEAD
            for k in range(ahead):
                issue_scores(k)
            low = lax.broadcasted_iota(jnp.int32, (BLOCK, LANES), 1) < HALF
            for k, (g, par, i) in enumerate(chains):
                sink = sink_ref[4 * g + 2 * i + par]
                e, m = _softmax_numerator(jnp.where(valid, scores[k], NEG_INF), sink)
                if k + ahead < len(chains):
                    issue_scores(k + ahead)
                o = _dot(e, kv[g, par][1])
                outs[g, par, i] = o / (pltpu.roll(o, HALF, 1) + jnp.exp(sink - m))
                if k == SGU_MIX_AFTER_CHAIN:
                    mixed = _sgu_mix(vln, wm_ref, bias_ref)
                if k % 2 == 0 and k // 2 < len(after_chain):
                    after_chain[k // 2]()
            for pair in range(N_PAIRS):
                g, i = divmod(pair, 2)
                lanes = slice(pair * LANES, (pair + 1) * LANES)
                o = jnp.where(low, outs[g, 0, i], outs[g, 1, i])
                out_stage[pair, rows, :] = o
                gate, _ = _silu_and_grad(za_ref[rows, lanes])
                mixed_next[rows, lanes] = (o * gate).astype(MXU_DTYPE)
            kp_ref[...] = kc[...]
            for pair in range(N_SGU_HEADS // 2):
                cols = slice(pair * LANES, (pair + 1) * LANES)
                gate, _ = _silu_and_grad(zs_ref[rows, cols])
                mixed_next[rows, ATTN_W + pair * LANES:ATTN_W + (pair + 1) * LANES] = (
                    u[:, cols] * mixed[pair] * gate).astype(MXU_DTYPE)

        live = (step > 0).astype(F32)
        quarter = D_MODEL // 4
        columns = [None] * 4

        def project(j):
            def piece():
                columns[j] = _dot(mixed_cur[...], wout_ref[:, j * quarter:(j + 1) * quarter])
            return piece

        half_blocks = FUSED_BLOCKS // 2
        per_block = 4 // half_blocks
        for b in range(half_blocks):
            mixers_block(b, [project(j) for j in range(b * per_block, (b + 1) * per_block)])
        xo = x_ref[...] + jnp.concatenate(columns, axis=1) + b_ref[...]
        r = lax.rsqrt(jnp.mean(xo * xo, axis=-1, keepdims=True) + NORM_EPS)
        xn = xo * r
        gf = gf_ref[...]
        err = xn * gf - t_ref[...]
        loss = 0.5 * jnp.sum(jnp.mean(err * err, axis=-1, keepdims=True), axis=0, keepdims=True)
        dy = err * (1.0 / D_MODEL)
        dxn = dy * gf
        gres = r * (dxn - xn * jnp.mean(dxn * xn, axis=-1, keepdims=True))
        vec_ref[0:1, :] += jnp.broadcast_to(loss * live, (1, D_MODEL))
        vec_ref[1:2, :] += jnp.sum(dy * xn, axis=0, keepdims=True) * live
        vec_ref[2:3, :] += jnp.sum(gres, axis=0, keepdims=True) * live
        gres_ref[...] = gres
        gb_ref[...] = gres.astype(MXU_DTYPE)

        def branch_grad(k):
            def piece():
                dmix_ref[k] = _dot(gb_ref[...], wout_ref[k * ATTN_W:(k + 1) * ATTN_W, :], NT)
            return piece

        def weight_grad(k):
            def piece():
                rows = slice(k * ATTN_W, (k + 1) * ATTN_W)
                gw_ref[rows, :] += _dot(mixed_cur[:, rows], gb_ref[...], TN)
            return piece

        backward = [branch_grad(0), branch_grad(1), weight_grad(0), weight_grad(1)]
        for b in range(half_blocks):
            mixers_block(half_blocks + b, backward[b * per_block:(b + 1) * per_block])

        @pl.when(step < n_tiles)
        def _():
            out_ref[...] = out_stage[...]

        mixed_cur[...] = mixed_next[...]

    ahead_tile = lambda i: jnp.minimum(i, n_tiles - 1)
    behind_tile = lambda i: jnp.maximum(i - 1, 0)
    blk = lambda w: pl.BlockSpec((tm, w), lambda i: (ahead_tile(i), 0))
    tiles = pl.BlockSpec((N_PAIRS, tm, LANES), lambda i: (0, ahead_tile(i), 0))
    gate = lambda k: pl.BlockSpec((None, tm, SGU_W), lambda i: (k, ahead_tile(i), 0))
    behind = lambda w: pl.BlockSpec((tm, w), lambda i: (behind_tile(i), 0))
    return pl.pallas_call(
        body,
        name="mixers_out_proj",
        grid=(n_tiles + 1,),
        in_specs=[pl.BlockSpec(memory_space=pltpu.SMEM), tiles, blk(KVX_W), gate(0), gate(1), gate(2), gate(3),
                  _full((1, SGU_W)), _full((1, SGU_W)), _full((N_SGU_HEADS, BLOCK, BLOCK)), _full((BLOCK, SGU_W)),
                  behind(D_MODEL), behind(D_MODEL), _full((D_MODEL, D_MODEL)), _full((1, D_MODEL)),
                  _full((1, D_MODEL))],
        out_specs=(tiles, behind(D_MODEL), pl.BlockSpec((2, tm, ATTN_W), lambda i: (0, behind_tile(i), 0)),
                   _full((D_MODEL, D_MODEL)), _full((8, D_MODEL))),
        out_shape=(jax.ShapeDtypeStruct((N_PAIRS, SEQ, LANES), F32),
                   jax.ShapeDtypeStruct((SEQ, D_MODEL), F32),
                   jax.ShapeDtypeStruct((2, SEQ, ATTN_W), F32),
                   jax.ShapeDtypeStruct((D_MODEL, D_MODEL), F32),
                   jax.ShapeDtypeStruct((8, D_MODEL), F32)),
        scratch_shapes=[pltpu.VMEM((BLOCK, KVX_W), MXU_DTYPE), pltpu.VMEM((N_SGU_HEADS, BLOCK, BLOCK), MXU_DTYPE),
                        pltpu.VMEM((tm, D_MODEL), MXU_DTYPE), pltpu.VMEM((tm, D_MODEL), MXU_DTYPE),
                        pltpu.VMEM((N_PAIRS, tm, LANES), F32), pltpu.VMEM((tm, D_MODEL), MXU_DTYPE)],
        compiler_params=_params(("arbitrary",), VMEM_LIMIT),
    )(sinks, q, kvx, gates, gates, gates, gates, ln_g, ln_b, sgu_w, bias_full, x, target, wout, b_out, final_g)


def _sgu_activations(us, vs, lng, lnb):
    u = _gelu(us)
    vg = _gelu(vs)
    mu = jnp.mean(vg, axis=-1, keepdims=True)
    xc = vg - mu
    rstd = lax.rsqrt(jnp.mean(xc * xc, axis=-1, keepdims=True) + NORM_EPS)
    vhat = xc * rstd
    return u, vhat, rstd, vhat * lng + lnb


def _mask_sgu_weights(w_ref, masked_ref, transposed_ref=None):
    tril = (lax.broadcasted_iota(jnp.int32, (BLOCK, BLOCK), 0)
            >= lax.broadcasted_iota(jnp.int32, (BLOCK, BLOCK), 1))
    for hh in range(N_SGU_HEADS):
        w = jnp.where(tril, w_ref[hh], 0.0)
        masked_ref[hh] = w.astype(MXU_DTYPE)
        if transposed_ref is not None:
            transposed_ref[hh] = w.T.astype(MXU_DTYPE)


def _sgu_mix(vln, masked_w_ref, bias_ref):
    low = lax.broadcasted_iota(jnp.int32, (BLOCK, LANES), 1) < HALF
    mixed = []
    for pair in range(N_SGU_HEADS // 2):
        vp = vln[:, pair * LANES:(pair + 1) * LANES]
        mixed.append(_dot(masked_w_ref[2 * pair], jnp.where(low, vp, 0.0))
                     + _dot(masked_w_ref[2 * pair + 1], jnp.where(low, 0.0, vp))
                     + bias_ref[:, pair * LANES:(pair + 1) * LANES])
    return mixed


def _mixers_bwd(sinks, dmix, q, kvx, out, gates, ln_g, ln_b, sgu_w, bias_full, gwout):
    last = N_BLOCKS - 1

    def body(sink_ref, d_ref, q_ref, kc_ref, o_ref, za_ref, dsg_ref, us_ref, vs_ref, zs_ref, lng_ref, lnb_ref, w_ref,
             bias_ref, gwout_ref,
             dp_ref, gsink_ref, gbin_ref, dps_ref, gw_ref, gb_ref, gln_ref, gbins_ref, wout_shard_ref,
             kp_ref, pend_ref, carry_ref, wm_ref, wt_ref, gbias_ref, sa_w, ra_w, sb_w, rc_w, send_sems, recv_sems):
        n = pl.program_id(0)
        start, exchange, finish = _reduce_scatter_plan(_Copies(send_sems, recv_sems), 0, gwout_ref, WOUT_ROWS,
                                                       sa_w, ra_w, sb_w, rc_w, wout_shard_ref)
        tril = (lax.broadcasted_iota(jnp.int32, (BLOCK, BLOCK), 0)
                >= lax.broadcasted_iota(jnp.int32, (BLOCK, BLOCK), 1))

        @pl.when(n == 0)
        def _():
            gsink_ref[...] = jnp.zeros_like(gsink_ref)
            gbin_ref[...] = jnp.zeros_like(gbin_ref)
            carry_ref[...] = jnp.zeros_like(carry_ref)
            kp_ref[...] = jnp.zeros_like(kp_ref)
            gw_ref[...] = jnp.zeros_like(gw_ref)
            gln_ref[...] = jnp.zeros_like(gln_ref)
            gbins_ref[...] = jnp.zeros_like(gbins_ref)
            gbias_ref[...] = jnp.zeros_like(gbias_ref)
            _mask_sgu_weights(w_ref, wm_ref, wt_ref)
            start()

        pl.when(n == 3)(exchange)
        pl.when(n == 12)(finish)

        @pl.when(n > 0)
        def _():
            dp_ref[:, 0:ATTN_W] = pend_ref[:, 0:ATTN_W]
            dp_ref[:, GATE0:ATTN_SECTION] = pend_ref[:, ATTN_W:]

        @pl.when(n > last)
        def _():
            dp_ref[:, KV0:GATE0] = carry_ref[...].astype(MXU_DTYPE)

        @pl.when(n <= last)
        def _():
            us = us_ref[...]
            vs = vs_ref[...]
            lng = lng_ref[...]
            u, vhat, rstd, vln = _sgu_activations(us, vs, lng, lnb_ref[...])
            low_sgu = lax.broadcasted_iota(jnp.int32, (BLOCK, LANES), 1) < HALF
            sgu = {}

            def sgu_gates():
                mixed = _sgu_mix(vln, wm_ref, bias_ref)
                sgu["du"], sgu["dzs"], sgu["dm"] = [], [], []
                for pair in range(N_SGU_HEADS // 2):
                    cols = slice(pair * LANES, (pair + 1) * LANES)
                    dsg = dsg_ref[:, cols]
                    gate, gate_grad = _silu_and_grad(zs_ref[:, cols])
                    up = u[:, cols]
                    sgu["du"].append(dsg * mixed[pair] * gate)
                    sgu["dzs"].append(dsg * up * mixed[pair] * gate_grad)
                    dmixed = dsg * up * gate
                    gbias_ref[:, cols] += dmixed
                    sgu["dm"].append((jnp.where(low_sgu, dmixed, 0.0).astype(MXU_DTYPE),
                                      jnp.where(low_sgu, 0.0, dmixed).astype(MXU_DTYPE)))

            def sgu_grads():
                dvln_parts = []
                for pair in range(N_SGU_HEADS // 2):
                    dm_lo, dm_hi = sgu["dm"][pair]
                    vp = vln[:, pair * LANES:(pair + 1) * LANES]
                    gw_ref[2 * pair] += _dot(dm_lo, vp, NT)
                    gw_ref[2 * pair + 1] += _dot(dm_hi, vp, NT)
                    dvln_parts.append(_dot(wt_ref[2 * pair], dm_lo) + _dot(wt_ref[2 * pair + 1], dm_hi))
                dvln = jnp.concatenate(dvln_parts, axis=1)
                gln_ref[0:1, :] += jnp.sum(dvln * vhat, axis=0, keepdims=True)
                gln_ref[1:2, :] += jnp.sum(dvln, axis=0, keepdims=True)
                dvhat = dvln * lng
                dvg = rstd * (dvhat - jnp.mean(dvhat, axis=-1, keepdims=True)
                              - vhat * jnp.mean(dvhat * vhat, axis=-1, keepdims=True))
                dus = jnp.concatenate(sgu["du"], axis=1) * _gelu_grad(us)
                dvs = dvg * _gelu_grad(vs)
                dzs = jnp.concatenate(sgu["dzs"], axis=1)
                for k, val in enumerate((dus, dvs, dzs)):
                    dps_ref[:, k * SGU_W:(k + 1) * SGU_W] = val.astype(MXU_DTYPE)
                    gbins_ref[:, k * SGU_W:(k + 1) * SGU_W] += jnp.sum(val, axis=0, keepdims=True)

            valid = _window_mask(n)[0:BLOCK]
            low = lax.broadcasted_iota(jnp.int32, (BLOCK, LANES), 1) < HALF
            low_keys = lax.broadcasted_iota(jnp.int32, (2 * BLOCK, LANES), 1) < HALF
            lane_row = lax.broadcasted_iota(jnp.int32, (1, LANES), 1)
            gsink = jnp.zeros((1, LANES), F32)
            chains = [(g, par, i) for g in range(2) for par in range(2) for i in range(2)]
            kv = {(g, par): _kv_cat(kp_ref, kc_ref, 2 * g + par, False) for g in range(2) for par in range(2)}
            ones_keys = jnp.ones((2 * BLOCK, LANES), MXU_DTYPE)
            half_of_lane = lax.broadcasted_iota(jnp.int32, (LANES, 2 * LANES), 0) // HALF
            half_of_col = lax.broadcasted_iota(jnp.int32, (LANES, 2 * LANES), 1) // LANES
            sum_halves = (half_of_lane == half_of_col).astype(MXU_DTYPE)
            douts, deltas = [], []
            for pair in range(N_PAIRS):
                lanes = slice(pair * LANES, (pair + 1) * LANES)
                dg = d_ref[:, lanes]
                gate, gate_grad = _silu_and_grad(za_ref[:, lanes])
                o = o_ref[pair]
                dout = dg * gate
                dza = dg * o * gate_grad
                douts.append(dout.astype(MXU_DTYPE))
                deltas.append(_dot(dout * o, sum_halves))
                zl = slice(ATTN_W + pair * LANES, ATTN_W + (pair + 1) * LANES)
                pend_ref[:, zl] = dza.astype(MXU_DTYPE)
                gl = slice(GATE0 + pair * LANES, GATE0 + (pair + 1) * LANES)
                gbin_ref[:, gl] += jnp.sum(dza, axis=0, keepdims=True)

            first = {}

            def issue_first(k):
                g, par, i = chains[k]
                first[k] = (_dot(q_ref[2 * g + i], kv[g, par][0], NT), _dot(douts[2 * g + i], kv[g, par][1], NT))

            numerators = {}

            def issue_row_sums(k):
                g, par, i = chains[k]
                sink = sink_ref[4 * g + 2 * i + par]
                e, m = _softmax_numerator(jnp.where(valid, first[k][0], NEG_INF), sink)
                numerators[k] = (e, jnp.exp(sink - m), _dot(e, ones_keys))

            ahead = ATTN_BWD_AHEAD
            for k in range(ahead):
                issue_first(k)
            issue_row_sums(0)
            issue_row_sums(1)
            dqs, dk_parts, dv_parts = {}, {}, {}
            operands = {}

            def issue_last(k):
                g, par, i = chains[k]
                ds, ds_t, p_t = operands.pop(k)
                dq = _dot(ds, kv[g, par][0])
                dqs[g, i] = dq if par == 0 else dqs[g, i] + dq
                dk = _dot(ds_t, q_ref[2 * g + i])
                dv = _dot(p_t, douts[2 * g + i])
                dk_parts[g, par] = dk if i == 0 else dk_parts[g, par] + dk
                dv_parts[g, par] = dv if i == 0 else dv_parts[g, par] + dv

            for k, (g, par, i) in enumerate(chains):
                h = 4 * g + 2 * i + par
                delta = deltas[2 * g + i][:, par * LANES:(par + 1) * LANES]
                e, at_sink, row_sum = numerators[k]
                inv = 1.0 / (row_sum + at_sink)
                p = e * jnp.tile(inv, (1, 2))
                ds = p * (first[k][1] - jnp.tile(delta, (1, 2)))
                ds = ds.astype(MXU_DTYPE)
                operands[k] = (ds, ds.T, p.astype(MXU_DTYPE).T)
                total = jnp.sum(at_sink * inv * delta, axis=0, keepdims=True)
                gsink = jnp.where(lane_row == h, -total, gsink)
                if k + ahead < len(chains):
                    issue_first(k + ahead)
                if k + 2 < len(chains):
                    issue_row_sums(k + 2)
                if k > 0:
                    issue_last(k - 1)
                if k == SGU_GATES_AFTER_CHAIN:
                    sgu_gates()
                if k == SGU_GRADS_AFTER_CHAIN:
                    sgu_grads()
            issue_last(len(chains) - 1)
            for pair in range(N_PAIRS):
                g, i = divmod(pair, 2)
                dq = dqs[g, i] * SCALE
                lanes = slice(pair * LANES, (pair + 1) * LANES)
                pend_ref[:, lanes] = dq.astype(MXU_DTYPE)
                gbin_ref[:, lanes] += jnp.sum(dq, axis=0, keepdims=True)
            gsink_ref[...] += gsink
            for k, parts in enumerate((dk_parts, dv_parts)):
                masked = {key: jnp.where(low_keys if key[1] == 0 else jnp.logical_not(low_keys), val, 0.0)
                          for key, val in parts.items()}
                both = (masked[0, 0] + masked[1, 1]
                        + pltpu.roll(masked[0, 1] + masked[1, 0], HALF, 1))
                lanes = slice(k * KV_W, (k + 1) * KV_W)
                done = carry_ref[:, lanes] + both[0:BLOCK]
                dp_ref[:, KV0 + k * KV_W:KV0 + (k + 1) * KV_W] = done.astype(MXU_DTYPE)
                carry_ref[:, lanes] = both[BLOCK:]
                gbin_ref[:, KV0 + k * KV_W:KV0 + (k + 1) * KV_W] += jnp.sum(both, axis=0, keepdims=True)
            kp_ref[...] = kc_ref[...]

        @pl.when(n == last)
        def _():
            for hh in range(N_SGU_HEADS):
                gw_ref[hh] = jnp.where(tril, gw_ref[hh], 0.0)
            head_of_lane = lax.broadcasted_iota(jnp.int32, (N_SGU_HEADS, SGU_W), 1) // HEAD_DIM
            select = (head_of_lane == lax.broadcasted_iota(jnp.int32, (N_SGU_HEADS, SGU_W), 0)).astype(F32)
            gb_ref[...] = lax.dot_general(select, gbias_ref[...], NT, precision=lax.Precision.HIGHEST,
                                          preferred_element_type=F32)

    at = lambda n: jnp.minimum(n, last)
    blk = lambda w: pl.BlockSpec((BLOCK, w), lambda n: (at(n), 0))
    tiles = pl.BlockSpec((N_PAIRS, BLOCK, LANES), lambda n: (0, at(n), 0))
    section = lambda k: pl.BlockSpec((None, BLOCK, SGU_W), lambda n: (k, at(n), 0))
    return pl.pallas_call(
        body,
        name="mixers_bwd",
        grid=(N_BLOCKS + 1,),
        in_specs=[pl.BlockSpec(memory_space=pltpu.SMEM),
                  section(0),
                  tiles,
                  blk(KVX_W),
                  tiles,
                  section(0),
                  section(1),
                  section(1), section(2), section(3),
                  _full((1, SGU_W)), _full((1, SGU_W)), _full((N_SGU_HEADS, BLOCK, BLOCK)), _full((BLOCK, SGU_W)),
                  VMEM_SPEC],
        out_specs=(pl.BlockSpec((BLOCK, ATTN_SECTION), lambda n: (jnp.maximum(n - 1, 0), 0)),
                   _full((1, LANES)), _full((1, ATTN_SECTION)),
                   pl.BlockSpec((BLOCK, SGU_SECTION), lambda n: (at(n), 0)),
                   _full((N_SGU_HEADS, BLOCK, BLOCK)), _full((N_SGU_HEADS, BLOCK)),
                   _full((8, SGU_W)), _full((1, SGU_SECTION)), VMEM_SPEC),
        out_shape=(jax.ShapeDtypeStruct((SEQ, ATTN_SECTION), MXU_DTYPE),
                   jax.ShapeDtypeStruct((1, LANES), F32),
                   jax.ShapeDtypeStruct((1, ATTN_SECTION), F32),
                   jax.ShapeDtypeStruct((SEQ, SGU_SECTION), MXU_DTYPE),
                   jax.ShapeDtypeStruct((N_SGU_HEADS, BLOCK, BLOCK), F32),
                   jax.ShapeDtypeStruct((N_SGU_HEADS, BLOCK), F32),
                   jax.ShapeDtypeStruct((8, SGU_W), F32),
                   jax.ShapeDtypeStruct((1, SGU_SECTION), F32),
                   jax.ShapeDtypeStruct((WOUT_ROWS, D_MODEL), F32)),
        scratch_shapes=([pltpu.VMEM((BLOCK, KVX_W), MXU_DTYPE),
                         pltpu.VMEM((BLOCK, 2 * ATTN_W), MXU_DTYPE), pltpu.VMEM((BLOCK, 2 * KV_W), F32),
                         pltpu.VMEM((N_SGU_HEADS, BLOCK, BLOCK), MXU_DTYPE),
                         pltpu.VMEM((N_SGU_HEADS, BLOCK, BLOCK), MXU_DTYPE), pltpu.VMEM((BLOCK, SGU_W), F32)]
                        + _reduce_scatter_scratch(WOUT_ROWS, D_MODEL, COMM_DTYPE) + _dma_sems(REDUCE_SEMS)),
        compiler_params=_params(("arbitrary",), VMEM_LIMIT),
    )(sinks, dmix, q, kvx, out, gates, dmix, gates, gates, gates, ln_g, ln_b, sgu_w, bias_full, gwout)


def _in_proj_bwd(dpa, dps, win_t, x, norm_g, gres):
    tm = TOKEN_TILE

    def body(da_ref, ds_ref, w_ref, x_ref, g_ref, gres_ref, gx_ref, gng_ref):
        @pl.when(pl.program_id(0) == 0)
        def _():
            gng_ref[...] = jnp.zeros_like(gng_ref)

        dh = _dot(da_ref[...], w_ref[0:ATTN_SECTION, :]) + _dot(ds_ref[...], w_ref[ATTN_SECTION:, :])
        xv = x_ref[...]
        r = lax.rsqrt(jnp.mean(xv * xv, axis=-1, keepdims=True) + NORM_EPS)
        xn = xv * r
        gng_ref[...] += jnp.sum(dh * xn, axis=0, keepdims=True)
        dxn = dh * g_ref[...]
        gx_ref[...] = r * (dxn - xn * jnp.mean(dxn * xn, axis=-1, keepdims=True)) + gres_ref[...]

    tile = lambda w: pl.BlockSpec((tm, w), lambda i: (i, 0))
    return pl.pallas_call(
        body,
        name="in_proj_bwd",
        grid=(SEQ // tm,),
        in_specs=[tile(ATTN_SECTION), tile(SGU_SECTION), _full((IN_W, D_MODEL)), tile(D_MODEL),
                  _full((1, D_MODEL)), tile(D_MODEL)],
        out_specs=(tile(D_MODEL), _full((1, D_MODEL))),
        out_shape=(jax.ShapeDtypeStruct((SEQ, D_MODEL), F32), jax.ShapeDtypeStruct((1, D_MODEL), F32)),
        compiler_params=_params(("arbitrary",), VMEM_LIMIT),
    )(dpa, dps, win_t, x, norm_g, gres)


WIN_GRAD_ROWS = 256


def _win_grad_segments():
    segments = []
    for step in range(IN_W // WIN_GRAD_ROWS):
        for owner in range(N_DEV):
            lo = max(step * WIN_GRAD_ROWS, owner * WIN_ROWS)
            hi = min((step + 1) * WIN_GRAD_ROWS, (owner + 1) * WIN_ROWS)
            if lo < hi:
                segments.append((len(segments), step, owner, lo, hi - lo))
    return segments


def _win_grad(dpa, dps, h, gsguw, vec_parts):
    rows = WIN_GRAD_ROWS
    n_attn = ATTN_SECTION // rows
    steps = IN_W // rows
    segments = _win_grad_segments()
    n_seg = len(segments)
    per_owner = max(sum(1 for seg in segments if seg[2] == p) for p in range(N_DEV))
    n_parts = len(vec_parts)
    class_rows = (N_DEV // 2) * WIN_ROWS

    def body(da_ref, ds_ref, h_ref, gsguw_ref, *rest):
        part_refs = rest[:n_parts]
        (shard_ref, sguw_full_ref, vec_out_ref,
         chunks, sa, ra, sb, rc, sa_s, ra_s, sb_s, rc_s, landing, vec_ref, ra_vec, slots,
         send_sems, recv_sems, send1, recv1, send2, recv2) = rest[n_parts:]
        step = pl.program_id(0)
        x, y, c = _place()
        copies = _Copies(send_sems, recv_sems)
        own_sguw = landing.at[_block_rows((x, y, c), SGUW_ROWS), :]
        start_s, exchange_s, finish_s = _reduce_scatter_plan(copies, 0, gsguw_ref, SGUW_ROWS,
                                                             sa_s, ra_s, sb_s, rc_s, own_sguw)
        gather = _gather_plan(copies, REDUCE_SEMS, landing, SGUW_ROWS)

        def place_of(owner):
            return owner // 4, (owner // 2) % 2, owner % 2

        def class_rows_of(owner, first, n):
            return pl.ds((owner // 2) * WIN_ROWS + first - owner * WIN_ROWS, n)

        def to_sibling(seg):
            sid, _, owner, first, n = seg
            at = class_rows_of(owner, first, n)
            return pltpu.make_async_remote_copy(src_ref=sa.at[at, :], dst_ref=ra.at[at, :], send_sem=send1.at[sid],
                                                recv_sem=recv1.at[sid], device_id=(x, y, 1 - c), device_id_type=MESH)

        def to_owner(seg):
            sid, _, owner, first, n = seg
            px, py, pc = place_of(owner)
            slot = (x + px - 2 * x * px) + 2 * (y + py - 2 * y * py) - 1
            nth = sum(1 for other in segments if other[2] == owner and other[0] < sid)
            dst = rc.at[pl.ds(pl.multiple_of(slot * WIN_ROWS, 16) + first - owner * WIN_ROWS, n), :]
            return pltpu.make_async_remote_copy(src_ref=sb.at[class_rows_of(owner, first, n), :], dst_ref=dst,
                                                send_sem=send2.at[sid], recv_sem=recv2.at[slot * per_owner + nth],
                                                device_id=(px, py, pc), device_id_type=MESH)

        def give(seg):
            sid, at_step, owner, first, n = seg

            @pl.when(c != owner % 2)
            def _():
                local = pl.ds(first - at_step * rows, n)
                sa[class_rows_of(owner, first, n), :] = chunks[at_step % 2, local, :].astype(sa.dtype)
                to_sibling(seg).start()

        def keep(seg):
            sid, at_step, owner, first, n = seg
            px, py, pc = place_of(owner)

            @pl.when(c == pc)
            def _():
                to_sibling(seg).wait_recv()
                local = pl.ds(first - at_step * rows, n)
                total = chunks[at_step % 2, local, :] + ra[class_rows_of(owner, first, n), :].astype(F32)
                mine = jnp.logical_and(x == px, y == py)

                @pl.when(mine)
                def _():
                    shard_ref[pl.ds(first - owner * WIN_ROWS, n), :] = total

                @pl.when(jnp.logical_not(mine))
                def _():
                    sb[class_rows_of(owner, first, n), :] = total.astype(sb.dtype)
                    to_owner(seg).start()

        pl.when(step == 0)(start_s)
        pl.when(step == 2)(exchange_s)

        @pl.when(step == 5)
        def _():
            finish_s()
            gather[0]()

        pl.when(step == 7)(gather[1])

        @pl.when(step < n_attn)
        def _():
            chunks[step % 2] = _dot(da_ref[...], h_ref[...], TN)

        @pl.when(step >= n_attn)
        def _():
            chunks[step % 2] = _dot(ds_ref[...], h_ref[...], TN)

        for s_ in range(steps):
            @pl.when(step == s_)
            def _():
                for seg in segments:
                    if seg[1] == s_:
                        give(seg)
                    if seg[1] == s_ - 1:
                        keep(seg)

        @pl.when(step == steps - 1)
        def _():
            for seg in segments:
                if seg[1] == steps - 1:
                    keep(seg)
            for owner in range(N_DEV):
                px, py, pc = place_of(owner)

                @pl.when(jnp.logical_and(jnp.logical_and(x == px, y == py), c == pc))
                def _():
                    mine = [seg for seg in segments if seg[2] == owner]
                    for slot in range(3):
                        for nth, (sid, _, _, first, n) in enumerate(mine):
                            landed = rc.at[pl.ds(slot * WIN_ROWS + first - owner * WIN_ROWS, n), :]
                            pltpu.make_async_remote_copy(
                                src_ref=landed, dst_ref=landed, send_sem=send2.at[sid],
                                recv_sem=recv2.at[slot * per_owner + nth], device_id=(x, y, c),
                                device_id_type=MESH).wait_recv()
                    acc = shard_ref[...]
                    for slot in range(3):
                        acc = acc + rc[slot * WIN_ROWS:(slot + 1) * WIN_ROWS, :].astype(F32)
                    shard_ref[...] = acc
            for seg in segments:
                owner = seg[2]
                px, py, pc = place_of(owner)

                @pl.when(c != pc)
                def _():
                    to_sibling(seg).wait_send()

                @pl.when(jnp.logical_and(c == pc, jnp.logical_not(jnp.logical_and(x == px, y == py))))
                def _():
                    to_owner(seg).wait_send()
            gather[2]()
            sguw_full_ref[...] = landing[...]
            _all_reduce_vectors(copies, REDUCE_SEMS + GATHER_SEMS, *part_refs, vec_out_ref, vec_ref, ra_vec, slots)

    return pl.pallas_call(
        body,
        name="win_grad",
        grid=(steps,),
        in_specs=[pl.BlockSpec((SEQ, rows), lambda i: (0, jnp.minimum(i, n_attn - 1))),
                  pl.BlockSpec((SEQ, rows), lambda i: (0, jnp.maximum(i - n_attn, 0))),
                  _full((SEQ, D_MODEL)), VMEM_SPEC] + [VMEM_SPEC] * n_parts,
        out_specs=(VMEM_SPEC, _full((N_SGU_HEADS * BLOCK, BLOCK)), VMEM_SPEC),
        out_shape=(jax.ShapeDtypeStruct((WIN_ROWS, D_MODEL), F32),
                   jax.ShapeDtypeStruct((N_SGU_HEADS * BLOCK, BLOCK), F32),
                   jax.ShapeDtypeStruct((VEC_ROWS, IN_W), F32)),
        scratch_shapes=([pltpu.VMEM((2, rows, D_MODEL), F32),
                         pltpu.VMEM((class_rows, D_MODEL), COMM_DTYPE), pltpu.VMEM((class_rows, D_MODEL), COMM_DTYPE),
                         pltpu.VMEM((class_rows, D_MODEL), COMM_DTYPE), pltpu.VMEM((3 * WIN_ROWS, D_MODEL), COMM_DTYPE)]
                        + _reduce_scatter_scratch(SGUW_ROWS, BLOCK, F32)
                        + [pltpu.VMEM((N_SGU_HEADS * BLOCK, BLOCK), F32)]
                        + _vector_scratch() + _dma_sems(REDUCE_SEMS + GATHER_SEMS + VECTOR_SEMS)
                        + _dma_sems(n_seg) + [pltpu.SemaphoreType.DMA((n_seg,)),
                                              pltpu.SemaphoreType.DMA((3 * per_owner,))]),
        compiler_params=_params(("arbitrary",), VMEM_LIMIT),
    )(dpa, dps, h, gsguw, *vec_parts)


VEC_NORM_G, VEC_B_IN, VEC_SINKS, VEC_LN_G, VEC_LN_B, VEC_B_OUT, VEC_FINAL_G, VEC_LOSS, VEC_SGU_B = 0, 1, 2, 3, 4, 5, 6, 7, 8


def _adamw(w, g, m, v):
    m = ADAM_B1 * m + (1.0 - ADAM_B1) * g
    v = ADAM_B2 * v + (1.0 - ADAM_B2) * (g * g)
    m_hat = m / (1.0 - ADAM_B1 ** ADAM_STEP)
    v_hat = v / (1.0 - ADAM_B2 ** ADAM_STEP)
    delta = -ADAM_LR * (m_hat / (jnp.sqrt(v_hat) + ADAM_EPS) + ADAM_WD * w)
    return delta, m, v


def _adamw_shard(name, g, w, m, v, block_rows):
    def body(g_ref, w_ref, m_ref, v_ref, d_ref, nm_ref, nv_ref):
        d_ref[...], nm_ref[...], nv_ref[...] = _adamw(w_ref[...], g_ref[...], m_ref[...], v_ref[...])

    rows, cols = w.shape
    spec = pl.BlockSpec((block_rows, cols), lambda i: (i, 0))
    return pl.pallas_call(
        body,
        name=name,
        grid=(rows // block_rows,),
        in_specs=[spec] * 4,
        out_specs=(spec,) * 3,
        out_shape=(jax.ShapeDtypeStruct(w.shape, F32),) * 3,
        compiler_params=_params(("arbitrary",)),
    )(g, w, m, v)


VECTOR_SEMS = 4


def _vector_scratch():
    return [pltpu.VMEM((VEC_ROWS, IN_W), F32), pltpu.VMEM((VEC_ROWS, IN_W), F32),
            pltpu.VMEM((4 * VEC_ROWS, IN_W), F32)]


def _all_reduce_vectors(copies, sem0, gng_ref, gba_ref, gbs_ref, gsink_ref, gln_ref, gsgub_ref, vec4_ref, out_ref,
                        vec_ref, ra_vec, slots):
    x, y, c = _place()
    vec_ref[...] = jnp.zeros_like(vec_ref)
    vec_ref[VEC_NORM_G:VEC_NORM_G + 1, 0:D_MODEL] = gng_ref[...]
    vec_ref[VEC_B_IN:VEC_B_IN + 1, 0:ATTN_SECTION] = gba_ref[...]
    vec_ref[VEC_B_IN:VEC_B_IN + 1, ATTN_SECTION:IN_W] = gbs_ref[...]
    vec_ref[VEC_SINKS:VEC_SINKS + 1, 0:LANES] = gsink_ref[...]
    vec_ref[VEC_LN_G:VEC_LN_G + 1, 0:SGU_W] = gln_ref[0:1, :]
    vec_ref[VEC_LN_B:VEC_LN_B + 1, 0:SGU_W] = gln_ref[1:2, :]
    vec_ref[VEC_B_OUT:VEC_B_OUT + 1, 0:D_MODEL] = vec4_ref[2:3, :]
    vec_ref[VEC_FINAL_G:VEC_FINAL_G + 1, 0:D_MODEL] = vec4_ref[1:2, :]
    vec_ref[VEC_LOSS:VEC_LOSS + 1, 0:D_MODEL] = vec4_ref[0:1, :]
    vec_ref[VEC_SGU_B:VEC_SGU_B + N_SGU_HEADS, 0:BLOCK] = gsgub_ref[...]

    to_sibling = copies(sem0, vec_ref, ra_vec, (x, y, 1 - c))
    to_sibling.start()
    to_sibling.wait_recv()

    def chip_slot(place):
        return slots.at[pl.ds(pl.multiple_of((2 * place[0] + place[1]) * VEC_ROWS, 8), VEC_ROWS), :]

    mine = chip_slot((x, y))
    mine[...] = vec_ref[...] + ra_vec[...]
    to_chips = [copies(sem0 + i, mine, mine, (*_chip(rel), c)) for i, rel in enumerate(RELATIONS[1:], start=1)]
    for cp in to_chips:
        cp.start()
    for i, rel in enumerate(RELATIONS[1:], start=1):
        theirs = chip_slot(_chip(rel))
        copies(sem0 + i, theirs, theirs, (x, y, c)).wait_recv()
    out_ref[...] = ((slots[0:VEC_ROWS, :] + slots[VEC_ROWS:2 * VEC_ROWS, :])
                    + slots[2 * VEC_ROWS:3 * VEC_ROWS, :]) + slots[3 * VEC_ROWS:, :]
    to_sibling.wait_send()
    for cp in to_chips:
        cp.wait_send()


def _adamw_replicated(vec, gsguw, weights, m_state, v_state):
    n = len(SMALL)

    def body(*refs):
        vec_ref, gsguw_ref = refs[0], refs[1]
        w_refs, m_refs, v_refs = (refs[2 + k * n:2 + (k + 1) * n] for k in range(3))
        outs = refs[2 + 3 * n:]
        g_refs, d_refs, nm_refs, nv_refs = (outs[k * n:(k + 1) * n] for k in range(4))
        for i, (_, row, shape) in enumerate(SMALL):
            g = gsguw_ref[...] if row is None else vec_ref[row:row + shape[0], 0:shape[1]]
            g_refs[i][...] = g
            d_refs[i][...], nm_refs[i][...], nv_refs[i][...] = _adamw(
                w_refs[i][...], g, m_refs[i][...], v_refs[i][...])

    shapes = tuple(jax.ShapeDtypeStruct(shape, F32) for _, _, shape in SMALL)
    outs = pl.pallas_call(
        body,
        name="adamw_replicated",
        in_specs=[VMEM_SPEC] * (2 + 3 * n),
        out_specs=(VMEM_SPEC,) * (4 * n),
        out_shape=shapes * 4,
    )(vec, gsguw, *weights, *m_state, *v_state)
    return tuple(outs[k * n:(k + 1) * n] for k in range(4))


SMALL = (
    ("norm_g", VEC_NORM_G, (1, D_MODEL)),
    ("b_in", VEC_B_IN, (1, IN_W)),
    ("attn_sinks", VEC_SINKS, (1, N_Q_HEADS)),
    ("sgu_ln_g", VEC_LN_G, (1, SGU_W)),
    ("sgu_ln_b", VEC_LN_B, (1, SGU_W)),
    ("sgu_w", None, (N_SGU_HEADS * BLOCK, BLOCK)),
    ("sgu_b", VEC_SGU_B, (N_SGU_HEADS, BLOCK)),
    ("b_out", VEC_B_OUT, (1, D_MODEL)),
    ("final_norm_g", VEC_FINAL_G, (1, D_MODEL)),
)


def _local_grads(x, target, h, win_t, wout_shard, norm_g, b_in, attn_sinks, sgu_ln_g, sgu_ln_b, sgu_w, sgu_b, b_out,
                 final_g):
    sinks = attn_sinks.reshape(N_Q_HEADS)
    bias_full = jnp.repeat(sgu_b.T, HEAD_DIM, axis=1)
    q, kvx, gates, wout = _in_proj(h, b_in, win_t, wout_shard)
    out, gres, dmix, gwout, vec4 = _mixers_out_proj(sinks, q, kvx, gates, sgu_ln_g, sgu_ln_b, sgu_w, bias_full,
                                                    x, target, wout, b_out, final_g)
    dpa, gsink, gbin_a, dps, gsguw, gsgub, gln, gbin_s, gwout_shard = _mixers_bwd(
        sinks, dmix, q, kvx, out, gates, sgu_ln_g, sgu_ln_b, sgu_w, bias_full, gwout)
    grad_x, gng = _in_proj_bwd(dpa, dps, win_t, x, norm_g, gres)
    gwin_shard, gsguw_sum, vec = _win_grad(dpa, dps, h, gsguw.reshape(N_SGU_HEADS * BLOCK, BLOCK),
                                           (gng, gbin_a, gbin_s, gsink, gln, gsgub, vec4))
    return grad_x, gwin_shard, gwout_shard, gsguw_sum, vec


def kernel(x, norm_g, w_in, b_in, attn_sinks, sgu_ln_g, sgu_ln_b, sgu_w, sgu_b, w_out, b_out, final_norm_g, loss_target, m_norm_g, m_w_in, m_b_in, m_attn_sinks, m_sgu_ln_g, m_sgu_ln_b, m_sgu_w, m_sgu_b, m_w_out, m_b_out, m_final_norm_g, v_norm_g, v_w_in, v_b_in, v_attn_sinks, v_sgu_ln_g, v_sgu_ln_b, v_sgu_w, v_sgu_b, v_w_out, v_b_out, v_final_norm_g):
    given = dict(norm_g=norm_g, b_in=b_in, attn_sinks=attn_sinks, sgu_ln_g=sgu_ln_g, sgu_ln_b=sgu_ln_b,
                 sgu_w=sgu_w, sgu_b=sgu_b, b_out=b_out, final_norm_g=final_norm_g)
    m_given = dict(norm_g=m_norm_g, b_in=m_b_in, attn_sinks=m_attn_sinks, sgu_ln_g=m_sgu_ln_g,
                   sgu_ln_b=m_sgu_ln_b, sgu_w=m_sgu_w, sgu_b=m_sgu_b, b_out=m_b_out, final_norm_g=m_final_norm_g)
    v_given = dict(norm_g=v_norm_g, b_in=v_b_in, attn_sinks=v_attn_sinks, sgu_ln_g=v_sgu_ln_g,
                   sgu_ln_b=v_sgu_ln_b, sgu_w=v_sgu_w, sgu_b=v_sgu_b, b_out=v_b_out, final_norm_g=v_final_norm_g)

    win_t, h = _all_gather_win(w_in[0].T, x[0], norm_g)
    grad_x, gwin_t, gwout, gsguw, vec = _local_grads(
        x[0], loss_target[0], h, win_t, w_out[0], norm_g, b_in, attn_sinks, sgu_ln_g, sgu_ln_b, sgu_w[0], sgu_b[0],
        b_out, final_norm_g.reshape(1, D_MODEL))

    t = lambda a: a[0].T
    d_win, nm_win, nv_win = _adamw_shard("adamw_w_in", gwin_t, t(w_in), t(m_w_in), t(v_w_in), WIN_ROWS // 2)
    d_wout, nm_wout, nv_wout = _adamw_shard("adamw_w_out", gwout, w_out[0], m_w_out[0], v_w_out[0], WOUT_ROWS)
    as_2d = lambda d: [d[name].reshape(shape) for name, _, shape in SMALL]
    loss = vec[VEC_LOSS, 0]
    small = _adamw_replicated(vec, gsguw, as_2d(given), as_2d(m_given), as_2d(v_given))

    def assemble(big_in, big_out, k):
        vals = {name: small[k][i].reshape(given[name].shape) for i, (name, _, _) in enumerate(SMALL)}
        vals["w_in"] = big_in.T[None]
        vals["w_out"] = big_out[None]
        order = ("norm_g", "w_in", "b_in", "attn_sinks", "sgu_ln_g", "sgu_ln_b", "sgu_w", "sgu_b", "w_out",
                 "b_out", "final_norm_g")
        return [vals[name] for name in order]

    return (loss, grad_x[None],
            *assemble(gwin_t, gwout, 0), *assemble(d_win, d_wout, 1),
            *assemble(nm_win, nm_wout, 2), *assemble(nv_win, nv_wout, 3))
```

```python
import functools
import math

import jax
import jax.numpy as jnp
from jax import lax
from jax.experimental import pallas as pl
from jax.experimental.pallas import tpu as pltpu

F32 = jnp.float32
BF16 = jnp.bfloat16
MXU_DTYPE = BF16
COMM_DTYPE = BF16

D_MODEL = 1024
SEQ = 4096
HEAD_DIM = 64
N_Q_HEADS = 8
Q_PER_KV = 4
BLOCK = 128
N_BLOCKS = SEQ // BLOCK
ATTN_W = 512
KV_W = 128
SGU_W = 512
N_SGU_HEADS = 8
IN_W = 2816
NORM_EPS = 1e-5
NEG_INF = -1e30
SCALE = HEAD_DIM ** -0.5
KV0 = ATTN_W
GATE0 = ATTN_W + 2 * KV_W
SGU0 = GATE0 + ATTN_W
ATTN_SECTION = SGU0
SGU_SECTION = IN_W - SGU0

ADAM_LR = 0.001
ADAM_B1 = 0.9
ADAM_B2 = 0.999
ADAM_EPS = 1e-08
ADAM_WD = 0.01
ADAM_STEP = 10

N_DEV = 8
WIN_ROWS = IN_W // N_DEV
WOUT_ROWS = D_MODEL // N_DEV
SGUW_ROWS = N_SGU_HEADS * BLOCK // N_DEV
VEC_ROWS = 16
MESH = pl.DeviceIdType.MESH

LANES = 128
HALF = LANES // 2
N_PAIRS = N_Q_HEADS * HEAD_DIM // LANES
KVX_W = 12 * LANES
TOKEN_TILE = 256
FWD_TOKEN_TILE = 512
ATTN_FWD_AHEAD = 4
FUSED_BLOCKS = 2
SGU_MIX_AFTER_CHAIN = 0
SGU_GATES_AFTER_CHAIN = 1
SGU_GRADS_AFTER_CHAIN = 5
ATTN_BWD_AHEAD = 3
VMEM_LIMIT = 56 * 1024 * 1024

NN = (((1,), (0,)), ((), ()))
NT = (((1,), (1,)), ((), ()))
TN = (((0,), (0,)), ((), ()))


def _dot(a, b, dims=NN):
    return lax.dot_general(a.astype(MXU_DTYPE), b.astype(MXU_DTYPE), dims, preferred_element_type=F32)


def _gelu(x):
    return x * (lax.erf(x * (1.0 / math.sqrt(2.0))) + 1.0) * 0.5


def _gelu_grad(x):
    cdf = (lax.erf(x * (1.0 / math.sqrt(2.0))) + 1.0) * 0.5
    return cdf + x * jnp.exp(-0.5 * x * x) * (1.0 / math.sqrt(2.0 * math.pi))


def _silu_and_grad(z):
    s = jax.nn.sigmoid(z)
    return z * s, s * (1.0 + z * (1.0 - s))


def _params(semantics=None, vmem=None):
    kw = {}
    if semantics is not None:
        kw["dimension_semantics"] = semantics
    if vmem is not None:
        kw["vmem_limit_bytes"] = vmem
    return pltpu.CompilerParams(**kw)


def _full(shape):
    return pl.BlockSpec(shape, lambda *_: (0,) * len(shape))


VMEM_SPEC = pl.BlockSpec(memory_space=pltpu.VMEM)


RELATIONS = ((0, 0), (1, 0), (0, 1), (1, 1))


def _place():
    return lax.axis_index("x"), lax.axis_index("y"), lax.axis_index("c")


def _chip(rel):
    x, y, _ = _place()
    return (1 - x if rel[0] else x, 1 - y if rel[1] else y)


def _block_rows(place, n_rows):
    px, py, pc = place
    return pl.ds(pl.multiple_of((4 * px + 2 * py + pc) * n_rows, 16), n_rows)


class _Copies:
    def __init__(self, send_sems, recv_sems):
        self.send_sems, self.recv_sems = send_sems, recv_sems

    def __call__(self, k, src, dst, to):
        return pltpu.make_async_remote_copy(src_ref=src, dst_ref=dst, send_sem=self.send_sems.at[k],
                                            recv_sem=self.recv_sems.at[k], device_id=to, device_id_type=MESH)


def _gather_plan(copies, sem0, full_ref, n_rows):
    x, y, c = _place()
    me, sibling = (x, y, c), (x, y, 1 - c)
    chips = [_chip(rel) for rel in RELATIONS[1:]]

    def cp(k, block, to):
        rows = full_ref.at[_block_rows(block, n_rows), :]
        return copies(sem0 + k, rows, rows, to)

    first = [cp(0, me, sibling)] + [cp(1 + j, me, (*chip, c)) for j, chip in enumerate(chips)]
    passed = [cp(4 + j, (*chip, c), sibling) for j, chip in enumerate(chips)]

    def start():
        for f in first:
            f.start()

    def forward():
        for j, chip in enumerate(chips):
            cp(1 + j, (*chip, c), me).wait_recv()
            passed[j].start()

    def finish():
        cp(0, sibling, me).wait_recv()
        for j, chip in enumerate(chips):
            cp(4 + j, (*chip, 1 - c), me).wait_recv()
        for f in first + passed:
            f.wait_send()

    return start, forward, finish


GATHER_SEMS = 7


def _reduce_scatter_plan(copies, sem0, part_ref, n_rows, sa, ra, sb, rc, res_ref):
    x, y, c = _place()
    sibling = (x, y, 1 - c)
    n = n_rows
    level1 = copies(sem0, sa, ra, sibling)

    def level2(i):
        slot = pl.ds((i - 1) * n, n)
        return copies(sem0 + i, sb.at[slot, :], rc.at[slot, :], (*_chip(RELATIONS[i]), c))

    def start():
        for i, rel in enumerate(RELATIONS):
            sa[i * n:(i + 1) * n, :] = part_ref[_block_rows((*_chip(rel), 1 - c), n), :].astype(sa.dtype)
        level1.start()

    def exchange():
        level1.wait_recv()
        for i, rel in enumerate(RELATIONS):
            total = part_ref[_block_rows((*_chip(rel), c), n), :] + ra[i * n:(i + 1) * n, :].astype(F32)
            if i == 0:
                res_ref[...] = total
            else:
                sb[(i - 1) * n:i * n, :] = total.astype(sb.dtype)
                level2(i).start()

    def finish():
        acc = res_ref[...]
        for i in range(1, len(RELATIONS)):
            level2(i).wait_recv()
            acc = acc + rc[(i - 1) * n:i * n, :].astype(F32)
        res_ref[...] = acc
        level1.wait_send()
        for i in range(1, len(RELATIONS)):
            level2(i).wait_send()

    return start, exchange, finish


REDUCE_SEMS = 4


def _reduce_scatter_scratch(n_rows, width, dtype):
    return [pltpu.VMEM((4 * n_rows, width), dtype), pltpu.VMEM((4 * n_rows, width), dtype),
            pltpu.VMEM((3 * n_rows, width), dtype), pltpu.VMEM((3 * n_rows, width), dtype)]


def _dma_sems(n):
    return [pltpu.SemaphoreType.DMA((n,)), pltpu.SemaphoreType.DMA((n,))]


def _all_gather_win(win_t_shard, x, norm_g):
    tm = FWD_TOKEN_TILE
    steps = SEQ // tm

    def body(win_ref, x_ref, g_ref, full_ref, h_ref, landing, send_sems, recv_sems):
        step = pl.program_id(0)
        start, forward, finish = _gather_plan(_Copies(send_sems, recv_sems), 0, landing, WIN_ROWS)

        @pl.when(step == 0)
        def _():
            landing[_block_rows(_place(), WIN_ROWS), :] = win_ref[...].astype(COMM_DTYPE)
            start()

        xv = x_ref[...]
        r = lax.rsqrt(jnp.mean(xv * xv, axis=-1, keepdims=True) + NORM_EPS)
        h_ref[...] = ((xv * r) * g_ref[...]).astype(MXU_DTYPE)

        @pl.when(step == steps - 1)
        def _():
            forward()
            finish()
            full_ref[...] = landing[...]

    return pl.pallas_call(
        body,
        name="all_gather_win",
        grid=(steps,),
        in_specs=[VMEM_SPEC, pl.BlockSpec((tm, D_MODEL), lambda i: (i, 0)), _full((1, D_MODEL))],
        out_specs=(_full((IN_W, D_MODEL)), pl.BlockSpec((tm, D_MODEL), lambda i: (i, 0))),
        out_shape=(jax.ShapeDtypeStruct((IN_W, D_MODEL), COMM_DTYPE),
                   jax.ShapeDtypeStruct((SEQ, D_MODEL), MXU_DTYPE)),
        scratch_shapes=[pltpu.VMEM((IN_W, D_MODEL), COMM_DTYPE)] + _dma_sems(GATHER_SEMS),
        compiler_params=_params(("arbitrary",), VMEM_LIMIT),
    )(win_t_shard, x, norm_g)


def _in_proj(h, b_in, win_t, wout_shard):
    tm = FWD_TOKEN_TILE
    steps = SEQ // tm

    def body(h_ref, b_ref, w_ref, wout_ref, q_ref, kvx_ref, gate_ref, wfull_ref, landing, send_sems, recv_sems):
        step = pl.program_id(0)
        start, forward, finish = _gather_plan(_Copies(send_sems, recv_sems), 0, landing, WOUT_ROWS)

        @pl.when(step == 0)
        def _():
            landing[_block_rows(_place(), WOUT_ROWS), :] = wout_ref[...].astype(COMM_DTYPE)
            start()

        pl.when(step == steps // 2)(forward)

        h = h_ref[...]

        def proj(lo, hi):
            return _dot(h, w_ref[lo:hi, :], NT) + b_ref[:, lo:hi]

        qs = proj(0, ATTN_W) * SCALE
        for pair in range(N_PAIRS):
            q_ref[pair] = qs[:, pair * LANES:(pair + 1) * LANES].astype(MXU_DTYPE)
        kv = proj(KV0, GATE0)
        low = lax.broadcasted_iota(jnp.int32, (tm, LANES), 1) < HALF
        for i in range(2):
            t = kv[:, i * LANES:(i + 1) * LANES]
            rot = pltpu.roll(t, HALF, 1)
            variants = (jnp.where(low, t, 0.0), jnp.where(low, 0.0, rot),
                        jnp.where(low, rot, 0.0), jnp.where(low, 0.0, t))
            for j, val in enumerate(variants):
                col = (4 * i + j) * LANES
                kvx_ref[:, col:col + LANES] = val.astype(MXU_DTYPE)
                if i == 1:
                    ones_elsewhere = jnp.where(low == (j % 2 == 0), val, 1.0)
                    kvx_ref[:, col + 4 * LANES:col + 5 * LANES] = ones_elsewhere.astype(MXU_DTYPE)
        for k in range(4):
            gate_ref[k] = proj(GATE0 + k * SGU_W, GATE0 + (k + 1) * SGU_W)

        @pl.when(step == steps - 1)
        def _():
            finish()
            wfull_ref[...] = landing[...]

    return pl.pallas_call(
        body,
        name="in_proj",
        grid=(steps,),
        in_specs=[pl.BlockSpec((tm, D_MODEL), lambda i: (i, 0)),
                  _full((1, IN_W)), _full((IN_W, D_MODEL)), VMEM_SPEC],
        out_specs=(pl.BlockSpec((N_PAIRS, tm, LANES), lambda i: (0, i, 0)),
                   pl.BlockSpec((tm, KVX_W), lambda i: (i, 0)),
                   pl.BlockSpec((4, tm, SGU_W), lambda i: (0, i, 0)),
                   _full((D_MODEL, D_MODEL))),
        out_shape=(jax.ShapeDtypeStruct((N_PAIRS, SEQ, LANES), MXU_DTYPE),
                   jax.ShapeDtypeStruct((SEQ, KVX_W), MXU_DTYPE),
                   jax.ShapeDtypeStruct((4, SEQ, SGU_W), F32),
                   jax.ShapeDtypeStruct((D_MODEL, D_MODEL), COMM_DTYPE)),
        scratch_shapes=[pltpu.VMEM((D_MODEL, D_MODEL), COMM_DTYPE)] + _dma_sems(GATHER_SEMS),
        compiler_params=_params(("arbitrary",), VMEM_LIMIT),
    )(h, b_in, win_t, wout_shard)


def _window_mask(n):
    qi = lax.broadcasted_iota(jnp.int32, (2 * BLOCK, 2 * BLOCK), 0) & (BLOCK - 1)
    p = lax.broadcasted_iota(jnp.int32, (2 * BLOCK, 2 * BLOCK), 1) - BLOCK
    in_window = jnp.logical_and(p <= qi, p > qi - BLOCK)
    return jnp.logical_and(in_window, jnp.logical_or(p >= 0, n > 0))


def _sink_column(sink_ref, g, par):
    return jnp.concatenate([jnp.full((BLOCK, 1), sink_ref[4 * g + par], F32),
                            jnp.full((BLOCK, 1), sink_ref[4 * g + 2 + par], F32)], axis=0)


def _kv_cat(kp_ref, kc_ref, var, with_ones):
    kcol, vcol = var * LANES, (var + (8 if with_ones else 4)) * LANES
    return (jnp.concatenate([kp_ref[:, kcol:kcol + LANES], kc_ref[:, kcol:kcol + LANES]], axis=0),
            jnp.concatenate([kp_ref[:, vcol:vcol + LANES], kc_ref[:, vcol:vcol + LANES]], axis=0))


def _softmax_numerator(s, sink):
    m = jnp.maximum(jnp.max(s, axis=1, keepdims=True), sink)
    return jnp.exp(s - m), m


def _mixers_out_proj(sinks, q, kvx, gates, ln_g, ln_b, sgu_w, bias_full, x, target, wout, b_out, final_g):
    tm = FUSED_BLOCKS * BLOCK
    n_tiles = SEQ // tm

    def body(sink_ref, q_ref, kc_ref, za_ref, us_ref, vs_ref, zs_ref, lng_ref, lnb_ref, w_ref, bias_ref,
             x_ref, t_ref, wout_ref, b_ref, gf_ref,
             out_ref, gres_ref, dmix_ref, gw_ref, vec_ref,
             kp_ref, wm_ref, mixed_next, mixed_cur, out_stage, gb_ref):
        step = pl.program_id(0)

        @pl.when(step == 0)
        def _():
            kp_ref[...] = jnp.zeros_like(kp_ref)
            _mask_sgu_weights(w_ref, wm_ref)
            gw_ref[...] = jnp.zeros_like(gw_ref)
            vec_ref[...] = jnp.zeros_like(vec_ref)
            mixed_cur[...] = jnp.zeros_like(mixed_cur)

        def mixers_block(b, after_chain=()):
            rows = slice(b * BLOCK, (b + 1) * BLOCK)
            kc = kc_ref.at[rows, :]
            u, _, _, vln = _sgu_activations(us_ref[rows, :], vs_ref[rows, :], lng_ref[...], lnb_ref[...])

            valid = _window_mask(step * FUSED_BLOCKS + b)[0:BLOCK]
            chains = [(g, par, i) for g in range(2) for par in range(2) for i in range(2)]
            kv = {(g, par): _kv_cat(kp_ref, kc, 2 * g + par, True) for g in range(2) for par in range(2)}
            scores, outs = {}, {}

            def issue_scores(k):
                g, par, i = chains[k]
                scores[k] = _dot(q_ref[2 * g + i, rows, :], kv[g, par][0], NT)

            ahead = ATTN_FWD_AHEAD
            for k in range(ahead):
                issue_scores(k)
            low = lax.broadcasted_iota(jnp.int32, (BLOCK, LANES), 1) < HALF
            for k, (g, par, i) in enumerate(chains):
                sink = sink_ref[4 * g + 2 * i + par]
                e, m = _softmax_numerator(jnp.where(valid, scores[k], NEG_INF), sink)
                if k + ahead < len(chains):
                    issue_scores(k + ahead)
                o = _dot(e, kv[g, par][1])
                outs[g, par, i] = o / (pltpu.roll(o, HALF, 1) + jnp.exp(sink - m))
                if k == SGU_MIX_AFTER_CHAIN:
                    mixed = _sgu_mix(vln, wm_ref, bias_ref)
                if k % 2 == 0 and k // 2 < len(after_chain):
                    after_chain[k // 2]()
            for pair in range(N_PAIRS):
                g, i = divmod(pair, 2)
                lanes = slice(pair * LANES, (pair + 1) * LANES)
                o = jnp.where(low, outs[g, 0, i], outs[g, 1, i])
                out_stage[pair, rows, :] = o
                gate, _ = _silu_and_grad(za_ref[rows, lanes])
                mixed_next[rows, lanes] = (o * gate).astype(MXU_DTYPE)
            kp_ref[...] = kc[...]
            for pair in range(N_SGU_HEADS // 2):
                cols = slice(pair * LANES, (pair + 1) * LANES)
                gate, _ = _silu_and_grad(zs_ref[rows, cols])
                mixed_next[rows, ATTN_W + pair * LANES:ATTN_W + (pair + 1) * LANES] = (
                    u[:, cols] * mixed[pair] * gate).astype(MXU_DTYPE)

        live = (step > 0).astype(F32)
        quarter = D_MODEL // 4
        columns = [None] * 4

        def project(j):
            def piece():
                columns[j] = _dot(mixed_cur[...], wout_ref[:, j * quarter:(j + 1) * quarter])
            return piece

        half_blocks = FUSED_BLOCKS // 2
        per_block = 4 // half_blocks
        for b in range(half_blocks):
            mixers_block(b, [project(j) for j in range(b * per_block, (b + 1) * per_block)])
        xo = x_ref[...] + jnp.concatenate(columns, axis=1) + b_ref[...]
        r = lax.rsqrt(jnp.mean(xo * xo, axis=-1, keepdims=True) + NORM_EPS)
        xn = xo * r
        gf = gf_ref[...]
        err = xn * gf - t_ref[...]
        loss = 0.5 * jnp.sum(jnp.mean(err * err, axis=-1, keepdims=True), axis=0, keepdims=True)
        dy = err * (1.0 / D_MODEL)
        dxn = dy * gf
        gres = r * (dxn - xn * jnp.mean(dxn * xn, axis=-1, keepdims=True))
        vec_ref[0:1, :] += jnp.broadcast_to(loss * live, (1, D_MODEL))
        vec_ref[1:2, :] += jnp.sum(dy * xn, axis=0, keepdims=True) * live
        vec_ref[2:3, :] += jnp.sum(gres, axis=0, keepdims=True) * live
        gres_ref[...] = gres
        gb_ref[...] = gres.astype(MXU_DTYPE)

        def branch_grad(k):
            def piece():
                dmix_ref[k] = _dot(gb_ref[...], wout_ref[k * ATTN_W:(k + 1) * ATTN_W, :], NT)
            return piece

        def weight_grad(k):
            def piece():
                rows = slice(k * ATTN_W, (k + 1) * ATTN_W)
                gw_ref[rows, :] += _dot(mixed_cur[:, rows], gb_ref[...], TN)
            return piece

        backward = [branch_grad(0), branch_grad(1), weight_grad(0), weight_grad(1)]
        for b in range(half_blocks):
            mixers_block(half_blocks + b, backward[b * per_block:(b + 1) * per_block])

        @pl.when(step < n_tiles)
        def _():
            out_ref[...] = out_stage[...]

        mixed_cur[...] = mixed_next[...]

    ahead_tile = lambda i: jnp.minimum(i, n_tiles - 1)
    behind_tile = lambda i: jnp.maximum(i - 1, 0)
    blk = lambda w: pl.BlockSpec((tm, w), lambda i: (ahead_tile(i), 0))
    tiles = pl.BlockSpec((N_PAIRS, tm, LANES), lambda i: (0, ahead_tile(i), 0))
    gate = lambda k: pl.BlockSpec((None, tm, SGU_W), lambda i: (k, ahead_tile(i), 0))
    behind = lambda w: pl.BlockSpec((tm, w), lambda i: (behind_tile(i), 0))
    return pl.pallas_call(
        body,
        name="mixers_out_proj",
        grid=(n_tiles + 1,),
        in_specs=[pl.BlockSpec(memory_space=pltpu.SMEM), tiles, blk(KVX_W), gate(0), gate(1), gate(2), gate(3),
                  _full((1, SGU_W)), _full((1, SGU_W)), _full((N_SGU_HEADS, BLOCK, BLOCK)), _full((BLOCK, SGU_W)),
                  behind(D_MODEL), behind(D_MODEL), _full((D_MODEL, D_MODEL)), _full((1, D_MODEL)),
                  _full((1, D_MODEL))],
        out_specs=(tiles, behind(D_MODEL), pl.BlockSpec((2, tm, ATTN_W), lambda i: (0, behind_tile(i), 0)),
                   _full((D_MODEL, D_MODEL)), _full((8, D_MODEL))),
        out_shape=(jax.ShapeDtypeStruct((N_PAIRS, SEQ, LANES), F32),
                   jax.ShapeDtypeStruct((SEQ, D_MODEL), F32),
                   jax.ShapeDtypeStruct((2, SEQ, ATTN_W), F32),
                   jax.ShapeDtypeStruct((D_MODEL, D_MODEL), F32),
                   jax.ShapeDtypeStruct((8, D_MODEL), F32)),
        scratch_shapes=[pltpu.VMEM((BLOCK, KVX_W), MXU_DTYPE), pltpu.VMEM((N_SGU_HEADS, BLOCK, BLOCK), MXU_DTYPE),
                        pltpu.VMEM((tm, D_MODEL), MXU_DTYPE), pltpu.VMEM((tm, D_MODEL), MXU_DTYPE),
                        pltpu.VMEM((N_PAIRS, tm, LANES), F32), pltpu.VMEM((tm, D_MODEL), MXU_DTYPE)],
        compiler_params=_params(("arbitrary",), VMEM_LIMIT),
    )(sinks, q, kvx, gates, gates, gates, gates, ln_g, ln_b, sgu_w, bias_full, x, target, wout, b_out, final_g)


def _sgu_activations(us, vs, lng, lnb):
    u = _gelu(us)
    vg = _gelu(vs)
    mu = jnp.mean(vg, axis=-1, keepdims=True)
    xc = vg - mu
    rstd = lax.rsqrt(jnp.mean(xc * xc, axis=-1, keepdims=True) + NORM_EPS)
    vhat = xc * rstd
    return u, vhat, rstd, vhat * lng + lnb


def _mask_sgu_weights(w_ref, masked_ref, transposed_ref=None):
    tril = (lax.broadcasted_iota(jnp.int32, (BLOCK, BLOCK), 0)
            >= lax.broadcasted_iota(jnp.int32, (BLOCK, BLOCK), 1))
    for hh in range(N_SGU_HEADS):
        w = jnp.where(tril, w_ref[hh], 0.0)
        masked_ref[hh] = w.astype(MXU_DTYPE)
        if transposed_ref is not None:
            transposed_ref[hh] = w.T.astype(MXU_DTYPE)


def _sgu_mix(vln, masked_w_ref, bias_ref):
    low = lax.broadcasted_iota(jnp.int32, (BLOCK, LANES), 1) < HALF
    mixed = []
    for pair in range(N_SGU_HEADS // 2):
        vp = vln[:, pair * LANES:(pair + 1) * LANES]
        mixed.append(_dot(masked_w_ref[2 * pair], jnp.where(low, vp, 0.0))
                     + _dot(masked_w_ref[2 * pair + 1], jnp.where(low, 0.0, vp))
                     + bias_ref[:, pair * LANES:(pair + 1) * LANES])
    return mixed


def _mixers_bwd(sinks, dmix, q, kvx, out, gates, ln_g, ln_b, sgu_w, bias_full, gwout):
    last = N_BLOCKS - 1

    def body(sink_ref, d_ref, q_ref, kc_ref, o_ref, za_ref, dsg_ref, us_ref, vs_ref, zs_ref, lng_ref, lnb_ref, w_ref,
             bias_ref, gwout_ref,
             dp_ref, gsink_ref, gbin_ref, dps_ref, gw_ref, gb_ref, gln_ref, gbins_ref, wout_shard_ref,
             kp_ref, pend_ref, carry_ref, wm_ref, wt_ref, gbias_ref, sa_w, ra_w, sb_w, rc_w, send_sems, recv_sems):
        n = pl.program_id(0)
        start, exchange, finish = _reduce_scatter_plan(_Copies(send_sems, recv_sems), 0, gwout_ref, WOUT_ROWS,
                                                       sa_w, ra_w, sb_w, rc_w, wout_shard_ref)
        tril = (lax.broadcasted_iota(jnp.int32, (BLOCK, BLOCK), 0)
                >= lax.broadcasted_iota(jnp.int32, (BLOCK, BLOCK), 1))

        @pl.when(n == 0)
        def _():
            gsink_ref[...] = jnp.zeros_like(gsink_ref)
            gbin_ref[...] = jnp.zeros_like(gbin_ref)
            carry_ref[...] = jnp.zeros_like(carry_ref)
            kp_ref[...] = jnp.zeros_like(kp_ref)
            gw_ref[...] = jnp.zeros_like(gw_ref)
            gln_ref[...] = jnp.zeros_like(gln_ref)
            gbins_ref[...] = jnp.zeros_like(gbins_ref)
            gbias_ref[...] = jnp.zeros_like(gbias_ref)
            _mask_sgu_weights(w_ref, wm_ref, wt_ref)
            start()

        pl.when(n == 3)(exchange)
        pl.when(n == 12)(finish)

        @pl.when(n > 0)
        def _():
            dp_ref[:, 0:ATTN_W] = pend_ref[:, 0:ATTN_W]
            dp_ref[:, GATE0:ATTN_SECTION] = pend_ref[:, ATTN_W:]

        @pl.when(n > last)
        def _():
            dp_ref[:, KV0:GATE0] = carry_ref[...].astype(MXU_DTYPE)

        @pl.when(n <= last)
        def _():
            us = us_ref[...]
            vs = vs_ref[...]
            lng = lng_ref[...]
            u, vhat, rstd, vln = _sgu_activations(us, vs, lng, lnb_ref[...])
            low_sgu = lax.broadcasted_iota(jnp.int32, (BLOCK, LANES), 1) < HALF
            sgu = {}

            def sgu_gates():
                mixed = _sgu_mix(vln, wm_ref, bias_ref)
                sgu["du"], sgu["dzs"], sgu["dm"] = [], [], []
                for pair in range(N_SGU_HEADS // 2):
                    cols = slice(pair * LANES, (pair + 1) * LANES)
                    dsg = dsg_ref[:, cols]
                    gate, gate_grad = _silu_and_grad(zs_ref[:, cols])
                    up = u[:, cols]
                    sgu["du"].append(dsg * mixed[pair] * gate)
                    sgu["dzs"].append(dsg * up * mixed[pair] * gate_grad)
                    dmixed = dsg * up * gate
                    gbias_ref[:, cols] += dmixed
                    sgu["dm"].append((jnp.where(low_sgu, dmixed, 0.0).astype(MXU_DTYPE),
                                      jnp.where(low_sgu, 0.0, dmixed).astype(MXU_DTYPE)))

            def sgu_grads():
                dvln_parts = []
                for pair in range(N_SGU_HEADS // 2):
                    dm_lo, dm_hi = sgu["dm"][pair]
                    vp = vln[:, pair * LANES:(pair + 1) * LANES]
                    gw_ref[2 * pair] += _dot(dm_lo, vp, NT)
                    gw_ref[2 * pair + 1] += _dot(dm_hi, vp, NT)
                    dvln_parts.append(_dot(wt_ref[2 * pair], dm_lo) + _dot(wt_ref[2 * pair + 1], dm_hi))
                dvln = jnp.concatenate(dvln_parts, axis=1)
                gln_ref[0:1, :] += jnp.sum(dvln * vhat, axis=0, keepdims=True)
                gln_ref[1:2, :] += jnp.sum(dvln, axis=0, keepdims=True)
                dvhat = dvln * lng
                dvg = rstd * (dvhat - jnp.mean(dvhat, axis=-1, keepdims=True)
                              - vhat * jnp.mean(dvhat * vhat, axis=-1, keepdims=True))
                dus = jnp.concatenate(sgu["du"], axis=1) * _gelu_grad(us)
                dvs = dvg * _gelu_grad(vs)
                dzs = jnp.concatenate(sgu["dzs"], axis=1)
                for k, val in enumerate((dus, dvs, dzs)):
                    dps_ref[:, k * SGU_W:(k + 1) * SGU_W] = val.astype(MXU_DTYPE)
                    gbins_ref[:, k * SGU_W:(k + 1) * SGU_W] += jnp.sum(val, axis=0, keepdims=True)

            valid = _window_mask(n)[0:BLOCK]
            low = lax.broadcasted_iota(jnp.int32, (BLOCK, LANES), 1) < HALF
            low_keys = lax.broadcasted_iota(jnp.int32, (2 * BLOCK, LANES), 1) < HALF
            lane_row = lax.broadcasted_iota(jnp.int32, (1, LANES), 1)
            gsink = jnp.zeros((1, LANES), F32)
            chains = [(g, par, i) for g in range(2) for par in range(2) for i in range(2)]
            kv = {(g, par): _kv_cat(kp_ref, kc_ref, 2 * g + par, False) for g in range(2) for par in range(2)}
            ones_keys = jnp.ones((2 * BLOCK, LANES), MXU_DTYPE)
            half_of_lane = lax.broadcasted_iota(jnp.int32, (LANES, 2 * LANES), 0) // HALF
            half_of_col = lax.broadcasted_iota(jnp.int32, (LANES, 2 * LANES), 1) // LANES
            sum_halves = (half_of_lane == half_of_col).astype(MXU_DTYPE)
            douts, deltas = [], []
            for pair in range(N_PAIRS):
                lanes = slice(pair * LANES, (pair + 1) * LANES)
                dg = d_ref[:, lanes]
                gate, gate_grad = _silu_and_grad(za_ref[:, lanes])
                o = o_ref[pair]
                dout = dg * gate
                dza = dg * o * gate_grad
                douts.append(dout.astype(MXU_DTYPE))
                deltas.append(_dot(dout * o, sum_halves))
                zl = slice(ATTN_W + pair * LANES, ATTN_W + (pair + 1) * LANES)
                pend_ref[:, zl] = dza.astype(MXU_DTYPE)
                gl = slice(GATE0 + pair * LANES, GATE0 + (pair + 1) * LANES)
                gbin_ref[:, gl] += jnp.sum(dza, axis=0, keepdims=True)

            first = {}

            def issue_first(k):
                g, par, i = chains[k]
                first[k] = (_dot(q_ref[2 * g + i], kv[g, par][0], NT), _dot(douts[2 * g + i], kv[g, par][1], NT))

            numerators = {}

            def issue_row_sums(k):
                g, par, i = chains[k]
                sink = sink_ref[4 * g + 2 * i + par]
                e, m = _softmax_numerator(jnp.where(valid, first[k][0], NEG_INF), sink)
                numerators[k] = (e, jnp.exp(sink - m), _dot(e, ones_keys))

            ahead = ATTN_BWD_AHEAD
            for k in range(ahead):
                issue_first(k)
            issue_row_sums(0)
            issue_row_sums(1)
            dqs, dk_parts, dv_parts = {}, {}, {}
            operands = {}

            def issue_last(k):
                g, par, i = chains[k]
                ds, ds_t, p_t = operands.pop(k)
                dq = _dot(ds, kv[g, par][0])
                dqs[g, i] = dq if par == 0 else dqs[g, i] + dq
                dk = _dot(ds_t, q_ref[2 * g + i])
                dv = _dot(p_t, douts[2 * g + i])
                dk_parts[g, par] = dk if i == 0 else dk_parts[g, par] + dk
                dv_parts[g, par] = dv if i == 0 else dv_parts[g, par] + dv

            for k, (g, par, i) in enumerate(chains):
                h = 4 * g + 2 * i + par
                delta = deltas[2 * g + i][:, par * LANES:(par + 1) * LANES]
                e, at_sink, row_sum = numerators[k]
                inv = 1.0 / (row_sum + at_sink)
                p = e * jnp.tile(inv, (1, 2))
                ds = p * (first[k][1] - jnp.tile(delta, (1, 2)))
                ds = ds.astype(MXU_DTYPE)
                operands[k] = (ds, ds.T, p.astype(MXU_DTYPE).T)
                total = jnp.sum(at_sink * inv * delta, axis=0, keepdims=True)
                gsink = jnp.where(lane_row == h, -total, gsink)
                if k + ahead < len(chains):
                    issue_first(k + ahead)
                if k + 2 < len(chains):
                    issue_row_sums(k + 2)
                if k > 0:
                    issue_last(k - 1)
                if k == SGU_GATES_AFTER_CHAIN:
                    sgu_gates()
                if k == SGU_GRADS_AFTER_CHAIN:
                    sgu_grads()
            issue_last(len(chains) - 1)
            for pair in range(N_PAIRS):
                g, i = divmod(pair, 2)
                dq = dqs[g, i] * SCALE
                lanes = slice(pair * LANES, (pair + 1) * LANES)
                pend_ref[:, lanes] = dq.astype(MXU_DTYPE)
                gbin_ref[:, lanes] += jnp.sum(dq, axis=0, keepdims=True)
            gsink_ref[...] += gsink
            for k, parts in enumerate((dk_parts, dv_parts)):
                masked = {key: jnp.where(low_keys if key[1] == 0 else jnp.logical_not(low_keys), val, 0.0)
                          for key, val in parts.items()}
                both = (masked[0, 0] + masked[1, 1]
                        + pltpu.roll(masked[0, 1] + masked[1, 0], HALF, 1))
                lanes = slice(k * KV_W, (k + 1) * KV_W)
                done = carry_ref[:, lanes] + both[0:BLOCK]
                dp_ref[:, KV0 + k * KV_W:KV0 + (k + 1) * KV_W] = done.astype(MXU_DTYPE)
                carry_ref[:, lanes] = both[BLOCK:]
                gbin_ref[:, KV0 + k * KV_W:KV0 + (k + 1) * KV_W] += jnp.sum(both, axis=0, keepdims=True)
            kp_ref[...] = kc_ref[...]

        @pl.when(n == last)
        def _():
            for hh in range(N_SGU_HEADS):
                gw_ref[hh] = jnp.where(tril, gw_ref[hh], 0.0)
            head_of_lane = lax.broadcasted_iota(jnp.int32, (N_SGU_HEADS, SGU_W), 1) // HEAD_DIM
            select = (head_of_lane == lax.broadcasted_iota(jnp.int32, (N_SGU_HEADS, SGU_W), 0)).astype(F32)
            gb_ref[...] = lax.dot_general(select, gbias_ref[...], NT, precision=lax.Precision.HIGHEST,
                                          preferred_element_type=F32)

    at = lambda n: jnp.minimum(n, last)
    blk = lambda w: pl.BlockSpec((BLOCK, w), lambda n: (at(n), 0))
    tiles = pl.BlockSpec((N_PAIRS, BLOCK, LANES), lambda n: (0, at(n), 0))
    section = lambda k: pl.BlockSpec((None, BLOCK, SGU_W), lambda n: (k, at(n), 0))
    return pl.pallas_call(
        body,
        name="mixers_bwd",
        grid=(N_BLOCKS + 1,),
        in_specs=[pl.BlockSpec(memory_space=pltpu.SMEM),
                  section(0),
                  tiles,
                  blk(KVX_W),
                  tiles,
                  section(0),
                  section(1),
                  section(1), section(2), section(3),
                  _full((1, SGU_W)), _full((1, SGU_W)), _full((N_SGU_HEADS, BLOCK, BLOCK)), _full((BLOCK, SGU_W)),
                  VMEM_SPEC],
        out_specs=(pl.BlockSpec((BLOCK, ATTN_SECTION), lambda n: (jnp.maximum(n - 1, 0), 0)),
                   _full((1, LANES)), _full((1, ATTN_SECTION)),
                   pl.BlockSpec((BLOCK, SGU_SECTION), lambda n: (at(n), 0)),
                   _full((N_SGU_HEADS, BLOCK, BLOCK)), _full((N_SGU_HEADS, BLOCK)),
                   _full((8, SGU_W)), _full((1, SGU_SECTION)), VMEM_SPEC),
        out_shape=(jax.ShapeDtypeStruct((SEQ, ATTN_SECTION), MXU_DTYPE),
                   jax.ShapeDtypeStruct((1, LANES), F32),
                   jax.ShapeDtypeStruct((1, ATTN_SECTION), F32),
                   jax.ShapeDtypeStruct((SEQ, SGU_SECTION), MXU_DTYPE),
                   jax.ShapeDtypeStruct((N_SGU_HEADS, BLOCK, BLOCK), F32),
                   jax.ShapeDtypeStruct((N_SGU_HEADS, BLOCK), F32),
                   jax.ShapeDtypeStruct((8, SGU_W), F32),
                   jax.ShapeDtypeStruct((1, SGU_SECTION), F32),
                   jax.ShapeDtypeStruct((WOUT_ROWS, D_MODEL), F32)),
        scratch_shapes=([pltpu.VMEM((BLOCK, KVX_W), MXU_DTYPE),
                         pltpu.VMEM((BLOCK, 2 * ATTN_W), MXU_DTYPE), pltpu.VMEM((BLOCK, 2 * KV_W), F32),
                         pltpu.VMEM((N_SGU_HEADS, BLOCK, BLOCK), MXU_DTYPE),
                         pltpu.VMEM((N_SGU_HEADS, BLOCK, BLOCK), MXU_DTYPE), pltpu.VMEM((BLOCK, SGU_W), F32)]
                        + _reduce_scatter_scratch(WOUT_ROWS, D_MODEL, COMM_DTYPE) + _dma_sems(REDUCE_SEMS)),
        compiler_params=_params(("arbitrary",), VMEM_LIMIT),
    )(sinks, dmix, q, kvx, out, gates, dmix, gates, gates, gates, ln_g, ln_b, sgu_w, bias_full, gwout)


def _in_proj_bwd(dpa, dps, win_t, x, norm_g, gres):
    tm = TOKEN_TILE

    def body(da_ref, ds_ref, w_ref, x_ref, g_ref, gres_ref, gx_ref, gng_ref):
        @pl.when(pl.program_id(0) == 0)
        def _():
            gng_ref[...] = jnp.zeros_like(gng_ref)

        dh = _dot(da_ref[...], w_ref[0:ATTN_SECTION, :]) + _dot(ds_ref[...], w_ref[ATTN_SECTION:, :])
        xv = x_ref[...]
        r = lax.rsqrt(jnp.mean(xv * xv, axis=-1, keepdims=True) + NORM_EPS)
        xn = xv * r
        gng_ref[...] += jnp.sum(dh * xn, axis=0, keepdims=True)
        dxn = dh * g_ref[...]
        gx_ref[...] = r * (dxn - xn * jnp.mean(dxn * xn, axis=-1, keepdims=True)) + gres_ref[...]

    tile = lambda w: pl.BlockSpec((tm, w), lambda i: (i, 0))
    return pl.pallas_call(
        body,
        name="in_proj_bwd",
        grid=(SEQ // tm,),
        in_specs=[tile(ATTN_SECTION), tile(SGU_SECTION), _full((IN_W, D_MODEL)), tile(D_MODEL),
                  _full((1, D_MODEL)), tile(D_MODEL)],
        out_specs=(tile(D_MODEL), _full((1, D_MODEL))),
        out_shape=(jax.ShapeDtypeStruct((SEQ, D_MODEL), F32), jax.ShapeDtypeStruct((1, D_MODEL), F32)),
        compiler_params=_params(("arbitrary",), VMEM_LIMIT),
    )(dpa, dps, win_t, x, norm_g, gres)


WIN_GRAD_ROWS = 256
WIN_GRAD_STRIDE = 3


def _win_grad_segments():
    segments = []
    steps = IN_W // WIN_GRAD_ROWS
    for step in range(steps):
        chunk = step * WIN_GRAD_STRIDE % steps
        for owner in range(N_DEV):
            lo = max(chunk * WIN_GRAD_ROWS, owner * WIN_ROWS)
            hi = min((chunk + 1) * WIN_GRAD_ROWS, (owner + 1) * WIN_ROWS)
            if lo < hi:
                segments.append((len(segments), step, owner, lo, hi - lo))
    return segments


def _win_grad(dpa, dps, h, gsguw, vec_parts):
    rows = WIN_GRAD_ROWS
    n_attn = ATTN_SECTION // rows
    steps = IN_W // rows
    segments = _win_grad_segments()
    n_seg = len(segments)
    per_owner = max(sum(1 for seg in segments if seg[2] == p) for p in range(N_DEV))
    n_parts = len(vec_parts)
    class_rows = (N_DEV // 2) * WIN_ROWS

    def chunk_of(step):
        return lax.rem(step * WIN_GRAD_STRIDE, steps)

    def body(da_ref, ds_ref, h_ref, gsguw_ref, *rest):
        part_refs = rest[:n_parts]
        (shard_ref, sguw_full_ref, vec_out_ref,
         chunks, sa, ra, sb, rc, sa_s, ra_s, sb_s, rc_s, landing, vec_ref, ra_vec, slots,
         send_sems, recv_sems, send1, recv1, send2, recv2) = rest[n_parts:]
        step = pl.program_id(0)
        x, y, c = _place()
        copies = _Copies(send_sems, recv_sems)
        own_sguw = landing.at[_block_rows((x, y, c), SGUW_ROWS), :]
        start_s, exchange_s, finish_s = _reduce_scatter_plan(copies, 0, gsguw_ref, SGUW_ROWS,
                                                             sa_s, ra_s, sb_s, rc_s, own_sguw)
        gather = _gather_plan(copies, REDUCE_SEMS, landing, SGUW_ROWS)

        def place_of(owner):
            return owner // 4, (owner // 2) % 2, owner % 2

        def class_rows_of(owner, first, n):
            return pl.ds((owner // 2) * WIN_ROWS + first - owner * WIN_ROWS, n)

        def to_sibling(seg):
            sid, _, owner, first, n = seg
            at = class_rows_of(owner, first, n)
            return pltpu.make_async_remote_copy(src_ref=sa.at[at, :], dst_ref=ra.at[at, :], send_sem=send1.at[sid],
                                                recv_sem=recv1.at[sid], device_id=(x, y, 1 - c), device_id_type=MESH)

        def to_owner(seg):
            sid, _, owner, first, n = seg
            px, py, pc = place_of(owner)
            slot = (x + px - 2 * x * px) + 2 * (y + py - 2 * y * py) - 1
            nth = sum(1 for other in segments if other[2] == owner and other[0] < sid)
            dst = rc.at[pl.ds(pl.multiple_of(slot * WIN_ROWS, 16) + first - owner * WIN_ROWS, n), :]
            return pltpu.make_async_remote_copy(src_ref=sb.at[class_rows_of(owner, first, n), :], dst_ref=dst,
                                                send_sem=send2.at[sid], recv_sem=recv2.at[slot * per_owner + nth],
                                                device_id=(px, py, pc), device_id_type=MESH)

        def give(seg):
            sid, at_step, owner, first, n = seg

            @pl.when(c != owner % 2)
            def _():
                local = pl.ds(first % rows, n)
                sa[class_rows_of(owner, first, n), :] = chunks[at_step % 2, local, :].astype(sa.dtype)
                to_sibling(seg).start()

        def keep(seg):
            sid, at_step, owner, first, n = seg
            px, py, pc = place_of(owner)

            @pl.when(c == pc)
            def _():
                to_sibling(seg).wait_recv()
                local = pl.ds(first % rows, n)
                total = chunks[at_step % 2, local, :] + ra[class_rows_of(owner, first, n), :].astype(F32)
                mine = jnp.logical_and(x == px, y == py)

                @pl.when(mine)
                def _():
                    shard_ref[pl.ds(first - owner * WIN_ROWS, n), :] = total

                @pl.when(jnp.logical_not(mine))
                def _():
                    sb[class_rows_of(owner, first, n), :] = total.astype(sb.dtype)
                    to_owner(seg).start()

        pl.when(step == 0)(start_s)
        pl.when(step == 2)(exchange_s)

        @pl.when(step == 5)
        def _():
            finish_s()
            gather[0]()

        pl.when(step == 7)(gather[1])

        @pl.when(chunk_of(step) < n_attn)
        def _():
            chunks[step % 2] = _dot(da_ref[...], h_ref[...], TN)

        @pl.when(chunk_of(step) >= n_attn)
        def _():
            chunks[step % 2] = _dot(ds_ref[...], h_ref[...], TN)

        for s_ in range(steps):
            @pl.when(step == s_)
            def _():
                for seg in segments:
                    if seg[1] == s_:
                        give(seg)
                    if seg[1] == s_ - 1:
                        keep(seg)

        @pl.when(step == steps - 1)
        def _():
            for seg in segments:
                if seg[1] == steps - 1:
                    keep(seg)
            for owner in range(N_DEV):
                px, py, pc = place_of(owner)

                @pl.when(jnp.logical_and(jnp.logical_and(x == px, y == py), c == pc))
                def _():
                    mine = [seg for seg in segments if seg[2] == owner]
                    for slot in range(3):
                        for nth, (sid, _, _, first, n) in enumerate(mine):
                            landed = rc.at[pl.ds(slot * WIN_ROWS + first - owner * WIN_ROWS, n), :]
                            pltpu.make_async_remote_copy(
                                src_ref=landed, dst_ref=landed, send_sem=send2.at[sid],
                                recv_sem=recv2.at[slot * per_owner + nth], device_id=(x, y, c),
                                device_id_type=MESH).wait_recv()
                    acc = shard_ref[...]
                    for slot in range(3):
                        acc = acc + rc[slot * WIN_ROWS:(slot + 1) * WIN_ROWS, :].astype(F32)
                    shard_ref[...] = acc
            for seg in segments:
                owner = seg[2]
                px, py, pc = place_of(owner)

                @pl.when(c != pc)
                def _():
                    to_sibling(seg).wait_send()

                @pl.when(jnp.logical_and(c == pc, jnp.logical_not(jnp.logical_and(x == px, y == py))))
                def _():
                    to_owner(seg).wait_send()
            gather[2]()
            sguw_full_ref[...] = landing[...]
            _all_reduce_vectors(copies, REDUCE_SEMS + GATHER_SEMS, *part_refs, vec_out_ref, vec_ref, ra_vec, slots)

    return pl.pallas_call(
        body,
        name="win_grad",
        grid=(steps,),
        in_specs=[pl.BlockSpec((SEQ, rows), lambda i: (0, jnp.minimum(chunk_of(i), n_attn - 1))),
                  pl.BlockSpec((SEQ, rows), lambda i: (0, jnp.maximum(chunk_of(i) - n_attn, 0))),
                  _full((SEQ, D_MODEL)), VMEM_SPEC] + [VMEM_SPEC] * n_parts,
        out_specs=(VMEM_SPEC, _full((N_SGU_HEADS * BLOCK, BLOCK)), VMEM_SPEC),
        out_shape=(jax.ShapeDtypeStruct((WIN_ROWS, D_MODEL), F32),
                   jax.ShapeDtypeStruct((N_SGU_HEADS * BLOCK, BLOCK), F32),
                   jax.ShapeDtypeStruct((VEC_ROWS, IN_W), F32)),
        scratch_shapes=([pltpu.VMEM((2, rows, D_MODEL), F32),
                         pltpu.VMEM((class_rows, D_MODEL), COMM_DTYPE), pltpu.VMEM((class_rows, D_MODEL), COMM_DTYPE),
                         pltpu.VMEM((class_rows, D_MODEL), COMM_DTYPE), pltpu.VMEM((3 * WIN_ROWS, D_MODEL), COMM_DTYPE)]
                        + _reduce_scatter_scratch(SGUW_ROWS, BLOCK, F32)
                        + [pltpu.VMEM((N_SGU_HEADS * BLOCK, BLOCK), F32)]
                        + _vector_scratch() + _dma_sems(REDUCE_SEMS + GATHER_SEMS + VECTOR_SEMS)
                        + _dma_sems(n_seg) + [pltpu.SemaphoreType.DMA((n_seg,)),
                                              pltpu.SemaphoreType.DMA((3 * per_owner,))]),
        compiler_params=_params(("arbitrary",), VMEM_LIMIT),
    )(dpa, dps, h, gsguw, *vec_parts)


VEC_NORM_G, VEC_B_IN, VEC_SINKS, VEC_LN_G, VEC_LN_B, VEC_B_OUT, VEC_FINAL_G, VEC_LOSS, VEC_SGU_B = 0, 1, 2, 3, 4, 5, 6, 7, 8


def _adamw(w, g, m, v):
    m = ADAM_B1 * m + (1.0 - ADAM_B1) * g
    v = ADAM_B2 * v + (1.0 - ADAM_B2) * (g * g)
    m_hat = m / (1.0 - ADAM_B1 ** ADAM_STEP)
    v_hat = v / (1.0 - ADAM_B2 ** ADAM_STEP)
    delta = -ADAM_LR * (m_hat / (jnp.sqrt(v_hat) + ADAM_EPS) + ADAM_WD * w)
    return delta, m, v


def _adamw_shard(name, g, w, m, v, block_rows):
    def body(g_ref, w_ref, m_ref, v_ref, d_ref, nm_ref, nv_ref):
        d_ref[...], nm_ref[...], nv_ref[...] = _adamw(w_ref[...], g_ref[...], m_ref[...], v_ref[...])

    rows, cols = w.shape
    spec = pl.BlockSpec((block_rows, cols), lambda i: (i, 0))
    return pl.pallas_call(
        body,
        name=name,
        grid=(rows // block_rows,),
        in_specs=[spec] * 4,
        out_specs=(spec,) * 3,
        out_shape=(jax.ShapeDtypeStruct(w.shape, F32),) * 3,
        compiler_params=_params(("arbitrary",)),
    )(g, w, m, v)


VECTOR_SEMS = 4


def _vector_scratch():
    return [pltpu.VMEM((VEC_ROWS, IN_W), F32), pltpu.VMEM((VEC_ROWS, IN_W), F32),
            pltpu.VMEM((4 * VEC_ROWS, IN_W), F32)]


def _all_reduce_vectors(copies, sem0, gng_ref, gba_ref, gbs_ref, gsink_ref, gln_ref, gsgub_ref, vec4_ref, out_ref,
                        vec_ref, ra_vec, slots):
    x, y, c = _place()
    vec_ref[...] = jnp.zeros_like(vec_ref)
    vec_ref[VEC_NORM_G:VEC_NORM_G + 1, 0:D_MODEL] = gng_ref[...]
    vec_ref[VEC_B_IN:VEC_B_IN + 1, 0:ATTN_SECTION] = gba_ref[...]
    vec_ref[VEC_B_IN:VEC_B_IN + 1, ATTN_SECTION:IN_W] = gbs_ref[...]
    vec_ref[VEC_SINKS:VEC_SINKS + 1, 0:LANES] = gsink_ref[...]
    vec_ref[VEC_LN_G:VEC_LN_G + 1, 0:SGU_W] = gln_ref[0:1, :]
    vec_ref[VEC_LN_B:VEC_LN_B + 1, 0:SGU_W] = gln_ref[1:2, :]
    vec_ref[VEC_B_OUT:VEC_B_OUT + 1, 0:D_MODEL] = vec4_ref[2:3, :]
    vec_ref[VEC_FINAL_G:VEC_FINAL_G + 1, 0:D_MODEL] = vec4_ref[1:2, :]
    vec_ref[VEC_LOSS:VEC_LOSS + 1, 0:D_MODEL] = vec4_ref[0:1, :]
    vec_ref[VEC_SGU_B:VEC_SGU_B + N_SGU_HEADS, 0:BLOCK] = gsgub_ref[...]

    to_sibling = copies(sem0, vec_ref, ra_vec, (x, y, 1 - c))
    to_sibling.start()
    to_sibling.wait_recv()

    def chip_slot(place):
        return slots.at[pl.ds(pl.multiple_of((2 * place[0] + place[1]) * VEC_ROWS, 8), VEC_ROWS), :]

    mine = chip_slot((x, y))
    mine[...] = vec_ref[...] + ra_vec[...]
    to_chips = [copies(sem0 + i, mine, mine, (*_chip(rel), c)) for i, rel in enumerate(RELATIONS[1:], start=1)]
    for cp in to_chips:
        cp.start()
    for i, rel in enumerate(RELATIONS[1:], start=1):
        theirs = chip_slot(_chip(rel))
        copies(sem0 + i, theirs, theirs, (x, y, c)).wait_recv()
    out_ref[...] = ((slots[0:VEC_ROWS, :] + slots[VEC_ROWS:2 * VEC_ROWS, :])
                    + slots[2 * VEC_ROWS:3 * VEC_ROWS, :]) + slots[3 * VEC_ROWS:, :]
    to_sibling.wait_send()
    for cp in to_chips:
        cp.wait_send()


def _adamw_replicated(vec, gsguw, weights, m_state, v_state):
    n = len(SMALL)

    def body(*refs):
        vec_ref, gsguw_ref = refs[0], refs[1]
        w_refs, m_refs, v_refs = (refs[2 + k * n:2 + (k + 1) * n] for k in range(3))
        outs = refs[2 + 3 * n:]
        g_refs, d_refs, nm_refs, nv_refs = (outs[k * n:(k + 1) * n] for k in range(4))
        for i, (_, row, shape) in enumerate(SMALL):
            g = gsguw_ref[...] if row is None else vec_ref[row:row + shape[0], 0:shape[1]]
            g_refs[i][...] = g
            d_refs[i][...], nm_refs[i][...], nv_refs[i][...] = _adamw(
                w_refs[i][...], g, m_refs[i][...], v_refs[i][...])

    shapes = tuple(jax.ShapeDtypeStruct(shape, F32) for _, _, shape in SMALL)
    outs = pl.pallas_call(
        body,
        name="adamw_replicated",
        in_specs=[VMEM_SPEC] * (2 + 3 * n),
        out_specs=(VMEM_SPEC,) * (4 * n),
        out_shape=shapes * 4,
    )(vec, gsguw, *weights, *m_state, *v_state)
    return tuple(outs[k * n:(k + 1) * n] for k in range(4))


SMALL = (
    ("norm_g", VEC_NORM_G, (1, D_MODEL)),
    ("b_in", VEC_B_IN, (1, IN_W)),
    ("attn_sinks", VEC_SINKS, (1, N_Q_HEADS)),
    ("sgu_ln_g", VEC_LN_G, (1, SGU_W)),
    ("sgu_ln_b", VEC_LN_B, (1, SGU_W)),
    ("sgu_w", None, (N_SGU_HEADS * BLOCK, BLOCK)),
    ("sgu_b", VEC_SGU_B, (N_SGU_HEADS, BLOCK)),
    ("b_out", VEC_B_OUT, (1, D_MODEL)),
    ("final_norm_g", VEC_FINAL_G, (1, D_MODEL)),
)


def _local_grads(x, target, h, win_t, wout_shard, norm_g, b_in, attn_sinks, sgu_ln_g, sgu_ln_b, sgu_w, sgu_b, b_out,
                 final_g):
    sinks = attn_sinks.reshape(N_Q_HEADS)
    bias_full = jnp.repeat(sgu_b.T, HEAD_DIM, axis=1)
    q, kvx, gates, wout = _in_proj(h, b_in, win_t, wout_shard)
    out, gres, dmix, gwout, vec4 = _mixers_out_proj(sinks, q, kvx, gates, sgu_ln_g, sgu_ln_b, sgu_w, bias_full,
                                                    x, target, wout, b_out, final_g)
    dpa, gsink, gbin_a, dps, gsguw, gsgub, gln, gbin_s, gwout_shard = _mixers_bwd(
        sinks, dmix, q, kvx, out, gates, sgu_ln_g, sgu_ln_b, sgu_w, bias_full, gwout)
    grad_x, gng = _in_proj_bwd(dpa, dps, win_t, x, norm_g, gres)
    gwin_shard, gsguw_sum, vec = _win_grad(dpa, dps, h, gsguw.reshape(N_SGU_HEADS * BLOCK, BLOCK),
                                           (gng, gbin_a, gbin_s, gsink, gln, gsgub, vec4))
    return grad_x, gwin_shard, gwout_shard, gsguw_sum, vec


def kernel(x, norm_g, w_in, b_in, attn_sinks, sgu_ln_g, sgu_ln_b, sgu_w, sgu_b, w_out, b_out, final_norm_g, loss_target, m_norm_g, m_w_in, m_b_in, m_attn_sinks, m_sgu_ln_g, m_sgu_ln_b, m_sgu_w, m_sgu_b, m_w_out, m_b_out, m_final_norm_g, v_norm_g, v_w_in, v_b_in, v_attn_sinks, v_sgu_ln_g, v_sgu_ln_b, v_sgu_w, v_sgu_b, v_w_out, v_b_out, v_final_norm_g):
    given = dict(norm_g=norm_g, b_in=b_in, attn_sinks=attn_sinks, sgu_ln_g=sgu_ln_g, sgu_ln_b=sgu_ln_b,
                 sgu_w=sgu_w, sgu_b=sgu_b, b_out=b_out, final_norm_g=final_norm_g)
    m_given = dict(norm_g=m_norm_g, b_in=m_b_in, attn_sinks=m_attn_sinks, sgu_ln_g=m_sgu_ln_g,
                   sgu_ln_b=m_sgu_ln_b, sgu_w=m_sgu_w, sgu_b=m_sgu_b, b_out=m_b_out, final_norm_g=m_final_norm_g)
    v_given = dict(norm_g=v_norm_g, b_in=v_b_in, attn_sinks=v_attn_sinks, sgu_ln_g=v_sgu_ln_g,
                   sgu_ln_b=v_sgu_ln_b, sgu_w=v_sgu_w, sgu_b=v_sgu_b, b_out=v_b_out, final_norm_g=v_final_norm_g)

    win_t, h = _all_gather_win(w_in[0].T, x[0], norm_g)
    grad_x, gwin_t, gwout, gsguw, vec = _local_grads(
        x[0], loss_target[0], h, win_t, w_out[0], norm_g, b_in, attn_sinks, sgu_ln_g, sgu_ln_b, sgu_w[0], sgu_b[0],
        b_out, final_norm_g.reshape(1, D_MODEL))

    t = lambda a: a[0].T
    d_win, nm_win, nv_win = _adamw_shard("adamw_w_in", gwin_t, t(w_in), t(m_w_in), t(v_w_in), WIN_ROWS // 2)
    d_wout, nm_wout, nv_wout = _adamw_shard("adamw_w_out", gwout, w_out[0], m_w_out[0], v_w_out[0], WOUT_ROWS)
    as_2d = lambda d: [d[name].reshape(shape) for name, _, shape in SMALL]
    loss = vec[VEC_LOSS, 0]
    small = _adamw_replicated(vec, gsguw, as_2d(given), as_2d(m_given), as_2d(v_given))

    def assemble(big_in, big_out, k):
        vals = {name: small[k][i].reshape(given[name].shape) for i, (name, _, _) in enumerate(SMALL)}
        vals["w_in"] = big_in.T[None]
        vals["w_out"] = big_out[None]
        order = ("norm_g", "w_in", "b_in", "attn_sinks", "sgu_ln_g", "sgu_ln_b", "sgu_w", "sgu_b", "w_out",
                 "b_out", "final_norm_g")
        return [vals[name] for name in order]

    return (loss, grad_x[None],
            *assemble(gwin_t, gwout, 0), *assemble(d_win, d_wout, 1),
            *assemble(nm_win, nm_wout, 2), *assemble(nv_win, nv_wout, 3))
```

```python
import functools
import math

import jax
import jax.numpy as jnp
from jax import lax
from jax.experimental import pallas as pl
from jax.experimental.pallas import tpu as pltpu

F32 = jnp.float32
BF16 = jnp.bfloat16
MXU_DTYPE = BF16
COMM_DTYPE = BF16

D_MODEL = 1024
SEQ = 4096
HEAD_DIM = 64
N_Q_HEADS = 8
Q_PER_KV = 4
BLOCK = 128
N_BLOCKS = SEQ // BLOCK
ATTN_W = 512
KV_W = 128
SGU_W = 512
N_SGU_HEADS = 8
IN_W = 2816
NORM_EPS = 1e-5
NEG_INF = -1e30
SCALE = HEAD_DIM ** -0.5
KV0 = ATTN_W
GATE0 = ATTN_W + 2 * KV_W
SGU0 = GATE0 + ATTN_W
ATTN_SECTION = SGU0
SGU_SECTION = IN_W - SGU0

ADAM_LR = 0.001
ADAM_B1 = 0.9
ADAM_B2 = 0.999
ADAM_EPS = 1e-08
ADAM_WD = 0.01
ADAM_STEP = 10

N_DEV = 8
WIN_ROWS = IN_W // N_DEV
WOUT_ROWS = D_MODEL // N_DEV
SGUW_ROWS = N_SGU_HEADS * BLOCK // N_DEV
VEC_ROWS = 16
MESH = pl.DeviceIdType.MESH

LANES = 128
HALF = LANES // 2
N_PAIRS = N_Q_HEADS * HEAD_DIM // LANES
KVX_W = 12 * LANES
TOKEN_TILE = 256
FWD_TOKEN_TILE = 512
ATTN_FWD_AHEAD = 4
FUSED_BLOCKS = 2
SGU_MIX_AFTER_CHAIN = 0
SGU_GATES_AFTER_CHAIN = 1
SGU_GRADS_AFTER_CHAIN = 5
ATTN_BWD_AHEAD = 3
VMEM_LIMIT = 56 * 1024 * 1024

NN = (((1,), (0,)), ((), ()))
NT = (((1,), (1,)), ((), ()))
TN = (((0,), (0,)), ((), ()))


def _dot(a, b, dims=NN):
    return lax.dot_general(a.astype(MXU_DTYPE), b.astype(MXU_DTYPE), dims, preferred_element_type=F32)


def _gelu(x):
    return x * (lax.erf(x * (1.0 / math.sqrt(2.0))) + 1.0) * 0.5


def _gelu_grad(x):
    cdf = (lax.erf(x * (1.0 / math.sqrt(2.0))) + 1.0) * 0.5
    return cdf + x * jnp.exp(-0.5 * x * x) * (1.0 / math.sqrt(2.0 * math.pi))


def _silu_and_grad(z):
    s = jax.nn.sigmoid(z)
    return z * s, s * (1.0 + z * (1.0 - s))


def _params(semantics=None, vmem=None):
    kw = {}
    if semantics is not None:
        kw["dimension_semantics"] = semantics
    if vmem is not None:
        kw["vmem_limit_bytes"] = vmem
    return pltpu.CompilerParams(**kw)


def _full(shape):
    return pl.BlockSpec(shape, lambda *_: (0,) * len(shape))


VMEM_SPEC = pl.BlockSpec(memory_space=pltpu.VMEM)


RELATIONS = ((0, 0), (1, 0), (0, 1), (1, 1))


def _place():
    return lax.axis_index("x"), lax.axis_index("y"), lax.axis_index("c")


def _chip(rel):
    x, y, _ = _place()
    return (1 - x if rel[0] else x, 1 - y if rel[1] else y)


def _block_rows(place, n_rows):
    px, py, pc = place
    return pl.ds(pl.multiple_of((4 * px + 2 * py + pc) * n_rows, 16), n_rows)


class _Copies:
    def __init__(self, send_sems, recv_sems):
        self.send_sems, self.recv_sems = send_sems, recv_sems

    def __call__(self, k, src, dst, to):
        return pltpu.make_async_remote_copy(src_ref=src, dst_ref=dst, send_sem=self.send_sems.at[k],
                                            recv_sem=self.recv_sems.at[k], device_id=to, device_id_type=MESH)


def _gather_plan(copies, sem0, full_ref, n_rows):
    x, y, c = _place()
    me, sibling = (x, y, c), (x, y, 1 - c)
    chips = [_chip(rel) for rel in RELATIONS[1:]]

    def cp(k, block, to):
        rows = full_ref.at[_block_rows(block, n_rows), :]
        return copies(sem0 + k, rows, rows, to)

    first = [cp(0, me, sibling)] + [cp(1 + j, me, (*chip, c)) for j, chip in enumerate(chips)]
    passed = [cp(4 + j, (*chip, c), sibling) for j, chip in enumerate(chips)]

    def start():
        for f in first:
            f.start()

    def forward():
        for j, chip in enumerate(chips):
            cp(1 + j, (*chip, c), me).wait_recv()
            passed[j].start()

    def finish():
        cp(0, sibling, me).wait_recv()
        for j, chip in enumerate(chips):
            cp(4 + j, (*chip, 1 - c), me).wait_recv()
        for f in first + passed:
            f.wait_send()

    return start, forward, finish


GATHER_SEMS = 7


def _reduce_scatter_plan(copies, sem0, part_ref, n_rows, sa, ra, sb, rc, res_ref):
    x, y, c = _place()
    sibling = (x, y, 1 - c)
    n = n_rows
    level1 = copies(sem0, sa, ra, sibling)

    def level2(i):
        slot = pl.ds((i - 1) * n, n)
        return copies(sem0 + i, sb.at[slot, :], rc.at[slot, :], (*_chip(RELATIONS[i]), c))

    def start():
        for i, rel in enumerate(RELATIONS):
            sa[i * n:(i + 1) * n, :] = part_ref[_block_rows((*_chip(rel), 1 - c), n), :].astype(sa.dtype)
        level1.start()

    def exchange():
        level1.wait_recv()
        for i, rel in enumerate(RELATIONS):
            total = part_ref[_block_rows((*_chip(rel), c), n), :] + ra[i * n:(i + 1) * n, :].astype(F32)
            if i == 0:
                res_ref[...] = total
            else:
                sb[(i - 1) * n:i * n, :] = total.astype(sb.dtype)
                level2(i).start()

    def finish():
        acc = res_ref[...]
        for i in range(1, len(RELATIONS)):
            level2(i).wait_recv()
            acc = acc + rc[(i - 1) * n:i * n, :].astype(F32)
        res_ref[...] = acc
        level1.wait_send()
        for i in range(1, len(RELATIONS)):
            level2(i).wait_send()

    return start, exchange, finish


REDUCE_SEMS = 4


def _reduce_scatter_scratch(n_rows, width, dtype):
    return [pltpu.VMEM((4 * n_rows, width), dtype), pltpu.VMEM((4 * n_rows, width), dtype),
            pltpu.VMEM((3 * n_rows, width), dtype), pltpu.VMEM((3 * n_rows, width), dtype)]


def _dma_sems(n):
    return [pltpu.SemaphoreType.DMA((n,)), pltpu.SemaphoreType.DMA((n,))]


def _all_gather_win(win_t_shard, x, norm_g):
    tm = FWD_TOKEN_TILE
    steps = SEQ // tm

    def body(win_ref, x_ref, g_ref, full_ref, h_ref, landing, send_sems, recv_sems):
        step = pl.program_id(0)
        start, forward, finish = _gather_plan(_Copies(send_sems, recv_sems), 0, landing, WIN_ROWS)

        @pl.when(step == 0)
        def _():
            landing[_block_rows(_place(), WIN_ROWS), :] = win_ref[...].astype(COMM_DTYPE)
            start()

        xv = x_ref[...]
        r = lax.rsqrt(jnp.mean(xv * xv, axis=-1, keepdims=True) + NORM_EPS)
        h_ref[...] = ((xv * r) * g_ref[...]).astype(MXU_DTYPE)

        @pl.when(step == steps - 1)
        def _():
            forward()
            finish()
            full_ref[...] = landing[...]

    return pl.pallas_call(
        body,
        name="all_gather_win",
        grid=(steps,),
        in_specs=[VMEM_SPEC, pl.BlockSpec((tm, D_MODEL), lambda i: (i, 0)), _full((1, D_MODEL))],
        out_specs=(_full((IN_W, D_MODEL)), pl.BlockSpec((tm, D_MODEL), lambda i: (i, 0))),
        out_shape=(jax.ShapeDtypeStruct((IN_W, D_MODEL), COMM_DTYPE),
                   jax.ShapeDtypeStruct((SEQ, D_MODEL), MXU_DTYPE)),
        scratch_shapes=[pltpu.VMEM((IN_W, D_MODEL), COMM_DTYPE)] + _dma_sems(GATHER_SEMS),
        compiler_params=_params(("arbitrary",), VMEM_LIMIT),
    )(win_t_shard, x, norm_g)


def _in_proj(h, b_in, win_t, wout_shard):
    tm = FWD_TOKEN_TILE
    steps = SEQ // tm

    def body(h_ref, b_ref, w_ref, wout_ref, q_ref, kvx_ref, gate_ref, wfull_ref, landing, send_sems, recv_sems):
        step = pl.program_id(0)
        start, forward, finish = _gather_plan(_Copies(send_sems, recv_sems), 0, landing, WOUT_ROWS)

        @pl.when(step == 0)
        def _():
            landing[_block_rows(_place(), WOUT_ROWS), :] = wout_ref[...].astype(COMM_DTYPE)
            start()

        pl.when(step == steps // 2)(forward)

        h = h_ref[...]

        def proj(lo, hi):
            return _dot(h, w_ref[lo:hi, :], NT) + b_ref[:, lo:hi]

        qs = proj(0, ATTN_W) * SCALE
        for pair in range(N_PAIRS):
            q_ref[pair] = qs[:, pair * LANES:(pair + 1) * LANES].astype(MXU_DTYPE)
        kv = proj(KV0, GATE0)
        low = lax.broadcasted_iota(jnp.int32, (tm, LANES), 1) < HALF
        for i in range(2):
            t = kv[:, i * LANES:(i + 1) * LANES]
            rot = pltpu.roll(t, HALF, 1)
            variants = (jnp.where(low, t, 0.0), jnp.where(low, 0.0, rot),
                        jnp.where(low, rot, 0.0), jnp.where(low, 0.0, t))
            for j, val in enumerate(variants):
                col = (4 * i + j) * LANES
                kvx_ref[:, col:col + LANES] = val.astype(MXU_DTYPE)
                if i == 1:
                    ones_elsewhere = jnp.where(low == (j % 2 == 0), val, 1.0)
                    kvx_ref[:, col + 4 * LANES:col + 5 * LANES] = ones_elsewhere.astype(MXU_DTYPE)
        for k in range(4):
            gate_ref[k] = proj(GATE0 + k * SGU_W, GATE0 + (k + 1) * SGU_W)

        @pl.when(step == steps - 1)
        def _():
            finish()
            wfull_ref[...] = landing[...]

    return pl.pallas_call(
        body,
        name="in_proj",
        grid=(steps,),
        in_specs=[pl.BlockSpec((tm, D_MODEL), lambda i: (i, 0)),
                  _full((1, IN_W)), _full((IN_W, D_MODEL)), VMEM_SPEC],
        out_specs=(pl.BlockSpec((N_PAIRS, tm, LANES), lambda i: (0, i, 0)),
                   pl.BlockSpec((tm, KVX_W), lambda i: (i, 0)),
                   pl.BlockSpec((4, tm, SGU_W), lambda i: (0, i, 0)),
                   _full((D_MODEL, D_MODEL))),
        out_shape=(jax.ShapeDtypeStruct((N_PAIRS, SEQ, LANES), MXU_DTYPE),
                   jax.ShapeDtypeStruct((SEQ, KVX_W), MXU_DTYPE),
                   jax.ShapeDtypeStruct((4, SEQ, SGU_W), F32),
                   jax.ShapeDtypeStruct((D_MODEL, D_MODEL), COMM_DTYPE)),
        scratch_shapes=[pltpu.VMEM((D_MODEL, D_MODEL), COMM_DTYPE)] + _dma_sems(GATHER_SEMS),
        compiler_params=_params(("arbitrary",), VMEM_LIMIT),
    )(h, b_in, win_t, wout_shard)


def _window_mask(n):
    qi = lax.broadcasted_iota(jnp.int32, (2 * BLOCK, 2 * BLOCK), 0) & (BLOCK - 1)
    p = lax.broadcasted_iota(jnp.int32, (2 * BLOCK, 2 * BLOCK), 1) - BLOCK
    in_window = jnp.logical_and(p <= qi, p > qi - BLOCK)
    return jnp.logical_and(in_window, jnp.logical_or(p >= 0, n > 0))


def _sink_column(sink_ref, g, par):
    return jnp.concatenate([jnp.full((BLOCK, 1), sink_ref[4 * g + par], F32),
                            jnp.full((BLOCK, 1), sink_ref[4 * g + 2 + par], F32)], axis=0)


def _kv_cat(kp_ref, kc_ref, var, with_ones):
    kcol, vcol = var * LANES, (var + (8 if with_ones else 4)) * LANES
    return (jnp.concatenate([kp_ref[:, kcol:kcol + LANES], kc_ref[:, kcol:kcol + LANES]], axis=0),
            jnp.concatenate([kp_ref[:, vcol:vcol + LANES], kc_ref[:, vcol:vcol + LANES]], axis=0))


def _softmax_numerator(s, sink):
    m = jnp.maximum(jnp.max(s, axis=1, keepdims=True), sink)
    return jnp.exp(s - m), m


def _mixers_out_proj(sinks, q, kvx, gates, ln_g, ln_b, sgu_w, bias_full, x, target, wout, b_out, final_g):
    tm = FUSED_BLOCKS * BLOCK
    n_tiles = SEQ // tm

    def body(sink_ref, q_ref, kc_ref, za_ref, us_ref, vs_ref, zs_ref, lng_ref, lnb_ref, w_ref, bias_ref,
             x_ref, t_ref, wout_ref, b_ref, gf_ref,
             out_ref, gres_ref, dmix_ref, gw_ref, vec_ref,
             kp_ref, wm_ref, mixed_next, mixed_cur, out_stage, gb_ref):
        step = pl.program_id(0)

        @pl.when(step == 0)
        def _():
            kp_ref[...] = jnp.zeros_like(kp_ref)
            _mask_sgu_weights(w_ref, wm_ref)
            gw_ref[...] = jnp.zeros_like(gw_ref)
            vec_ref[...] = jnp.zeros_like(vec_ref)
            mixed_cur[...] = jnp.zeros_like(mixed_cur)

        def mixers_block(b, after_chain=()):
            rows = slice(b * BLOCK, (b + 1) * BLOCK)
            kc = kc_ref.at[rows, :]
            u, _, _, vln = _sgu_activations(us_ref[rows, :], vs_ref[rows, :], lng_ref[...], lnb_ref[...])

            valid = _window_mask(step * FUSED_BLOCKS + b)[0:BLOCK]
            chains = [(g, par, i) for g in range(2) for par in range(2) for i in range(2)]
            kv = {(g, par): _kv_cat(kp_ref, kc, 2 * g + par, True) for g in range(2) for par in range(2)}
            scores, outs = {}, {}

            def issue_scores(k):
                g, par, i = chains[k]
                scores[k] = _dot(q_ref[2 * g + i, rows, :], kv[g, par][0], NT)

            ahead = ATTN_FWD_AHEAD
            for k in range(ahead):
                issue_scores(k)
            low = lax.broadcasted_iota(jnp.int32, (BLOCK, LANES), 1) < HALF
            for k, (g, par, i) in enumerate(chains):
                sink = sink_ref[4 * g + 2 * i + par]
                e, m = _softmax_numerator(jnp.where(valid, scores[k], NEG_INF), sink)
                if k + ahead < len(chains):
                    issue_scores(k + ahead)
                o = _dot(e, kv[g, par][1])
                outs[g, par, i] = o / (pltpu.roll(o, HALF, 1) + jnp.exp(sink - m))
                if k == SGU_MIX_AFTER_CHAIN:
                    mixed = _sgu_mix(vln, wm_ref, bias_ref)
                if k % 2 == 0 and k // 2 < len(after_chain):
                    after_chain[k // 2]()
            for pair in range(N_PAIRS):
                g, i = divmod(pair, 2)
                lanes = slice(pair * LANES, (pair + 1) * LANES)
                o = jnp.where(low, outs[g, 0, i], outs[g, 1, i])
                out_stage[pair, rows, :] = o
                gate, _ = _silu_and_grad(za_ref[rows, lanes])
                mixed_next[rows, lanes] = (o * gate).astype(MXU_DTYPE)
            kp_ref[...] = kc[...]
            for pair in range(N_SGU_HEADS // 2):
                cols = slice(pair * LANES, (pair + 1) * LANES)
                gate, _ = _silu_and_grad(zs_ref[rows, cols])
                mixed_next[rows, ATTN_W + pair * LANES:ATTN_W + (pair + 1) * LANES] = (
                    u[:, cols] * mixed[pair] * gate).astype(MXU_DTYPE)

        live = (step > 0).astype(F32)
        quarter = D_MODEL // 4
        columns = [None] * 4

        def project(j):
            def piece():
                columns[j] = _dot(mixed_cur[...], wout_ref[:, j * quarter:(j + 1) * quarter])
            return piece

        half_blocks = FUSED_BLOCKS // 2
        per_block = 4 // half_blocks
        for b in range(half_blocks):
            mixers_block(b, [project(j) for j in range(b * per_block, (b + 1) * per_block)])
        xo = x_ref[...] + jnp.concatenate(columns, axis=1) + b_ref[...]
        r = lax.rsqrt(jnp.mean(xo * xo, axis=-1, keepdims=True) + NORM_EPS)
        xn = xo * r
        gf = gf_ref[...]
        err = xn * gf - t_ref[...]
        loss = 0.5 * jnp.sum(jnp.mean(err * err, axis=-1, keepdims=True), axis=0, keepdims=True)
        dy = err * (1.0 / D_MODEL)
        dxn = dy * gf
        gres = r * (dxn - xn * jnp.mean(dxn * xn, axis=-1, keepdims=True))
        vec_ref[0:1, :] += jnp.broadcast_to(loss * live, (1, D_MODEL))
        vec_ref[1:2, :] += jnp.sum(dy * xn, axis=0, keepdims=True) * live
        vec_ref[2:3, :] += jnp.sum(gres, axis=0, keepdims=True) * live
        gres_ref[...] = gres
        gb_ref[...] = gres.astype(MXU_DTYPE)

        def branch_grad(k):
            def piece():
                dmix_ref[k] = _dot(gb_ref[...], wout_ref[k * ATTN_W:(k + 1) * ATTN_W, :], NT)
            return piece

        def weight_grad(k):
            def piece():
                rows = slice(k * ATTN_W, (k + 1) * ATTN_W)
                gw_ref[rows, :] += _dot(mixed_cur[:, rows], gb_ref[...], TN)
            return piece

        backward = [branch_grad(0), branch_grad(1), weight_grad(0), weight_grad(1)]
        for b in range(half_blocks):
            mixers_block(half_blocks + b, backward[b * per_block:(b + 1) * per_block])

        @pl.when(step < n_tiles)
        def _():
            out_ref[...] = out_stage[...]

        mixed_cur[...] = mixed_next[...]

    ahead_tile = lambda i: jnp.minimum(i, n_tiles - 1)
    behind_tile = lambda i: jnp.maximum(i - 1, 0)
    blk = lambda w: pl.BlockSpec((tm, w), lambda i: (ahead_tile(i), 0))
    tiles = pl.BlockSpec((N_PAIRS, tm, LANES), lambda i: (0, ahead_tile(i), 0))
    gate = lambda k: pl.BlockSpec((None, tm, SGU_W), lambda i: (k, ahead_tile(i), 0))
    behind = lambda w: pl.BlockSpec((tm, w), lambda i: (behind_tile(i), 0))
    return pl.pallas_call(
        body,
        name="mixers_out_proj",
        grid=(n_tiles + 1,),
        in_specs=[pl.BlockSpec(memory_space=pltpu.SMEM), tiles, blk(KVX_W), gate(0), gate(1), gate(2), gate(3),
                  _full((1, SGU_W)), _full((1, SGU_W)), _full((N_SGU_HEADS, BLOCK, BLOCK)), _full((BLOCK, SGU_W)),
                  behind(D_MODEL), behind(D_MODEL), _full((D_MODEL, D_MODEL)), _full((1, D_MODEL)),
                  _full((1, D_MODEL))],
        out_specs=(tiles, behind(D_MODEL), pl.BlockSpec((2, tm, ATTN_W), lambda i: (0, behind_tile(i), 0)),
                   _full((D_MODEL, D_MODEL)), _full((8, D_MODEL))),
        out_shape=(jax.ShapeDtypeStruct((N_PAIRS, SEQ, LANES), F32),
                   jax.ShapeDtypeStruct((SEQ, D_MODEL), F32),
                   jax.ShapeDtypeStruct((2, SEQ, ATTN_W), F32),
                   jax.ShapeDtypeStruct((D_MODEL, D_MODEL), F32),
                   jax.ShapeDtypeStruct((8, D_MODEL), F32)),
        scratch_shapes=[pltpu.VMEM((BLOCK, KVX_W), MXU_DTYPE), pltpu.VMEM((N_SGU_HEADS, BLOCK, BLOCK), MXU_DTYPE),
                        pltpu.VMEM((tm, D_MODEL), MXU_DTYPE), pltpu.VMEM((tm, D_MODEL), MXU_DTYPE),
                        pltpu.VMEM((N_PAIRS, tm, LANES), F32), pltpu.VMEM((tm, D_MODEL), MXU_DTYPE)],
        compiler_params=_params(("arbitrary",), VMEM_LIMIT),
    )(sinks, q, kvx, gates, gates, gates, gates, ln_g, ln_b, sgu_w, bias_full, x, target, wout, b_out, final_g)


def _sgu_activations(us, vs, lng, lnb):
    u = _gelu(us)
    vg = _gelu(vs)
    mu = jnp.mean(vg, axis=-1, keepdims=True)
    xc = vg - mu
    rstd = lax.rsqrt(jnp.mean(xc * xc, axis=-1, keepdims=True) + NORM_EPS)
    vhat = xc * rstd
    return u, vhat, rstd, vhat * lng + lnb


def _mask_sgu_weights(w_ref, masked_ref, transposed_ref=None):
    tril = (lax.broadcasted_iota(jnp.int32, (BLOCK, BLOCK), 0)
            >= lax.broadcasted_iota(jnp.int32, (BLOCK, BLOCK), 1))
    for hh in range(N_SGU_HEADS):
        w = jnp.where(tril, w_ref[hh], 0.0)
        masked_ref[hh] = w.astype(MXU_DTYPE)
        if transposed_ref is not None:
            transposed_ref[hh] = w.T.astype(MXU_DTYPE)


def _sgu_mix(vln, masked_w_ref, bias_ref):
    low = lax.broadcasted_iota(jnp.int32, (BLOCK, LANES), 1) < HALF
    mixed = []
    for pair in range(N_SGU_HEADS // 2):
        vp = vln[:, pair * LANES:(pair + 1) * LANES]
        mixed.append(_dot(masked_w_ref[2 * pair], jnp.where(low, vp, 0.0))
                     + _dot(masked_w_ref[2 * pair + 1], jnp.where(low, 0.0, vp))
                     + bias_ref[:, pair * LANES:(pair + 1) * LANES])
    return mixed


def _mixers_bwd(sinks, dmix, q, kvx, out, gates, ln_g, ln_b, sgu_w, bias_full, gwout):
    last = N_BLOCKS - 1

    def body(sink_ref, d_ref, q_ref, kc_ref, o_ref, za_ref, dsg_ref, us_ref, vs_ref, zs_ref, lng_ref, lnb_ref, w_ref,
             bias_ref, gwout_ref,
             dp_ref, gsink_ref, gbin_ref, dps_ref, gw_ref, gb_ref, gln_ref, gbins_ref, wout_shard_ref,
             kp_ref, pend_ref, carry_ref, wm_ref, wt_ref, gbias_ref, sa_w, ra_w, sb_w, rc_w, send_sems, recv_sems):
        n = pl.program_id(0)
        start, exchange, finish = _reduce_scatter_plan(_Copies(send_sems, recv_sems), 0, gwout_ref, WOUT_ROWS,
                                                       sa_w, ra_w, sb_w, rc_w, wout_shard_ref)
        tril = (lax.broadcasted_iota(jnp.int32, (BLOCK, BLOCK), 0)
                >= lax.broadcasted_iota(jnp.int32, (BLOCK, BLOCK), 1))

        @pl.when(n == 0)
        def _():
            gsink_ref[...] = jnp.zeros_like(gsink_ref)
            gbin_ref[...] = jnp.zeros_like(gbin_ref)
            carry_ref[...] = jnp.zeros_like(carry_ref)
            kp_ref[...] = jnp.zeros_like(kp_ref)
            gw_ref[...] = jnp.zeros_like(gw_ref)
            gln_ref[...] = jnp.zeros_like(gln_ref)
            gbins_ref[...] = jnp.zeros_like(gbins_ref)
            gbias_ref[...] = jnp.zeros_like(gbias_ref)
            _mask_sgu_weights(w_ref, wm_ref, wt_ref)
            start()

        pl.when(n == 3)(exchange)
        pl.when(n == 12)(finish)

        @pl.when(n > 0)
        def _():
            dp_ref[:, 0:ATTN_W] = pend_ref[:, 0:ATTN_W]
            dp_ref[:, GATE0:ATTN_SECTION] = pend_ref[:, ATTN_W:]

        @pl.when(n > last)
        def _():
            dp_ref[:, KV0:GATE0] = carry_ref[...].astype(MXU_DTYPE)

        @pl.when(n <= last)
        def _():
            us = us_ref[...]
            vs = vs_ref[...]
            lng = lng_ref[...]
            u, vhat, rstd, vln = _sgu_activations(us, vs, lng, lnb_ref[...])
            low_sgu = lax.broadcasted_iota(jnp.int32, (BLOCK, LANES), 1) < HALF
            sgu = {}

            def sgu_gates():
                mixed = _sgu_mix(vln, wm_ref, bias_ref)
                sgu["du"], sgu["dzs"], sgu["dm"] = [], [], []
                for pair in range(N_SGU_HEADS // 2):
                    cols = slice(pair * LANES, (pair + 1) * LANES)
                    dsg = dsg_ref[:, cols]
                    gate, gate_grad = _silu_and_grad(zs_ref[:, cols])
                    up = u[:, cols]
                    sgu["du"].append(dsg * mixed[pair] * gate)
                    sgu["dzs"].append(dsg * up * mixed[pair] * gate_grad)
                    dmixed = dsg * up * gate
                    gbias_ref[:, cols] += dmixed
                    sgu["dm"].append((jnp.where(low_sgu, dmixed, 0.0).astype(MXU_DTYPE),
                                      jnp.where(low_sgu, 0.0, dmixed).astype(MXU_DTYPE)))

            def sgu_grads():
                dvln_parts = []
                for pair in range(N_SGU_HEADS // 2):
                    dm_lo, dm_hi = sgu["dm"][pair]
                    vp = vln[:, pair * LANES:(pair + 1) * LANES]
                    gw_ref[2 * pair] += _dot(dm_lo, vp, NT)
                    gw_ref[2 * pair + 1] += _dot(dm_hi, vp, NT)
                    dvln_parts.append(_dot(wt_ref[2 * pair], dm_lo) + _dot(wt_ref[2 * pair + 1], dm_hi))
                dvln = jnp.concatenate(dvln_parts, axis=1)
                gln_ref[0:1, :] += jnp.sum(dvln * vhat, axis=0, keepdims=True)
                gln_ref[1:2, :] += jnp.sum(dvln, axis=0, keepdims=True)
                dvhat = dvln * lng
                dvg = rstd * (dvhat - jnp.mean(dvhat, axis=-1, keepdims=True)
                              - vhat * jnp.mean(dvhat * vhat, axis=-1, keepdims=True))
                dus = jnp.concatenate(sgu["du"], axis=1) * _gelu_grad(us)
                dvs = dvg * _gelu_grad(vs)
                dzs = jnp.concatenate(sgu["dzs"], axis=1)
                for k, val in enumerate((dus, dvs, dzs)):
                    dps_ref[:, k * SGU_W:(k + 1) * SGU_W] = val.astype(MXU_DTYPE)
                    gbins_ref[:, k * SGU_W:(k + 1) * SGU_W] += jnp.sum(val, axis=0, keepdims=True)

            valid = _window_mask(n)[0:BLOCK]
            low = lax.broadcasted_iota(jnp.int32, (BLOCK, LANES), 1) < HALF
            low_keys = lax.broadcasted_iota(jnp.int32, (2 * BLOCK, LANES), 1) < HALF
            lane_row = lax.broadcasted_iota(jnp.int32, (1, LANES), 1)
            gsink = jnp.zeros((1, LANES), F32)
            chains = [(g, par, i) for g in range(2) for par in range(2) for i in range(2)]
            kv = {(g, par): _kv_cat(kp_ref, kc_ref, 2 * g + par, False) for g in range(2) for par in range(2)}
            ones_keys = jnp.ones((2 * BLOCK, LANES), MXU_DTYPE)
            half_of_lane = lax.broadcasted_iota(jnp.int32, (LANES, 2 * LANES), 0) // HALF
            half_of_col = lax.broadcasted_iota(jnp.int32, (LANES, 2 * LANES), 1) // LANES
            sum_halves = (half_of_lane == half_of_col).astype(MXU_DTYPE)
            douts, deltas = [], []
            for pair in range(N_PAIRS):
                lanes = slice(pair * LANES, (pair + 1) * LANES)
                dg = d_ref[:, lanes]
                gate, gate_grad = _silu_and_grad(za_ref[:, lanes])
                o = o_ref[pair]
                dout = dg * gate
                dza = dg * o * gate_grad
                douts.append(dout.astype(MXU_DTYPE))
                deltas.append(_dot(dout * o, sum_halves))
                zl = slice(ATTN_W + pair * LANES, ATTN_W + (pair + 1) * LANES)
                pend_ref[:, zl] = dza.astype(MXU_DTYPE)
                gl = slice(GATE0 + pair * LANES, GATE0 + (pair + 1) * LANES)
                gbin_ref[:, gl] += jnp.sum(dza, axis=0, keepdims=True)

            first = {}

            def issue_first(k):
                g, par, i = chains[k]
                first[k] = (_dot(q_ref[2 * g + i], kv[g, par][0], NT), _dot(douts[2 * g + i], kv[g, par][1], NT))

            numerators = {}

            def issue_row_sums(k):
                g, par, i = chains[k]
                sink = sink_ref[4 * g + 2 * i + par]
                e, m = _softmax_numerator(jnp.where(valid, first[k][0], NEG_INF), sink)
                numerators[k] = (e, jnp.exp(sink - m), _dot(e, ones_keys))

            ahead = ATTN_BWD_AHEAD
            for k in range(ahead):
                issue_first(k)
            issue_row_sums(0)
            issue_row_sums(1)
            dqs, dk_parts, dv_parts = {}, {}, {}
            operands = {}

            def issue_last(k):
                g, par, i = chains[k]
                ds, ds_t, p_t = operands.pop(k)
                dq = _dot(ds, kv[g, par][0])
                dqs[g, i] = dq if par == 0 else dqs[g, i] + dq
                dk = _dot(ds_t, q_ref[2 * g + i])
                dv = _dot(p_t, douts[2 * g + i])
                dk_parts[g, par] = dk if i == 0 else dk_parts[g, par] + dk
                dv_parts[g, par] = dv if i == 0 else dv_parts[g, par] + dv

            for k, (g, par, i) in enumerate(chains):
                h = 4 * g + 2 * i + par
                delta = deltas[2 * g + i][:, par * LANES:(par + 1) * LANES]
                e, at_sink, row_sum = numerators[k]
                inv = 1.0 / (row_sum + at_sink)
                p = e * jnp.tile(inv, (1, 2))
                ds = p * (first[k][1] - jnp.tile(delta, (1, 2)))
                ds = ds.astype(MXU_DTYPE)
                operands[k] = (ds, ds.T, p.astype(MXU_DTYPE).T)
                total = jnp.sum(at_sink * inv * delta, axis=0, keepdims=True)
                gsink = jnp.where(lane_row == h, -total, gsink)
                if k + ahead < len(chains):
                    issue_first(k + ahead)
                if k + 2 < len(chains):
                    issue_row_sums(k + 2)
                if k > 0:
                    issue_last(k - 1)
                if k == SGU_GATES_AFTER_CHAIN:
                    sgu_gates()
                if k == SGU_GRADS_AFTER_CHAIN:
                    sgu_grads()
            issue_last(len(chains) - 1)
            for pair in range(N_PAIRS):
                g, i = divmod(pair, 2)
                dq = dqs[g, i] * SCALE
                lanes = slice(pair * LANES, (pair + 1) * LANES)
                pend_ref[:, lanes] = dq.astype(MXU_DTYPE)
                gbin_ref[:, lanes] += jnp.sum(dq, axis=0, keepdims=True)
            gsink_ref[...] += gsink
            for k, parts in enumerate((dk_parts, dv_parts)):
                masked = {key: jnp.where(low_keys if key[1] == 0 else jnp.logical_not(low_keys), val, 0.0)
                          for key, val in parts.items()}
                both = (masked[0, 0] + masked[1, 1]
                        + pltpu.roll(masked[0, 1] + masked[1, 0], HALF, 1))
                lanes = slice(k * KV_W, (k + 1) * KV_W)
                done = carry_ref[:, lanes] + both[0:BLOCK]
                dp_ref[:, KV0 + k * KV_W:KV0 + (k + 1) * KV_W] = done.astype(MXU_DTYPE)
                carry_ref[:, lanes] = both[BLOCK:]
                gbin_ref[:, KV0 + k * KV_W:KV0 + (k + 1) * KV_W] += jnp.sum(both, axis=0, keepdims=True)
            kp_ref[...] = kc_ref[...]

        @pl.when(n == last)
        def _():
            for hh in range(N_SGU_HEADS):
                gw_ref[hh] = jnp.where(tril, gw_ref[hh], 0.0)
            head_of_lane = lax.broadcasted_iota(jnp.int32, (N_SGU_HEADS, SGU_W), 1) // HEAD_DIM
            select = (head_of_lane == lax.broadcasted_iota(jnp.int32, (N_SGU_HEADS, SGU_W), 0)).astype(F32)
            gb_ref[...] = lax.dot_general(select, gbias_ref[...], NT, precision=lax.Precision.HIGHEST,
                                          preferred_element_type=F32)

    at = lambda n: jnp.minimum(n, last)
    blk = lambda w: pl.BlockSpec((BLOCK, w), lambda n: (at(n), 0))
    tiles = pl.BlockSpec((N_PAIRS, BLOCK, LANES), lambda n: (0, at(n), 0))
    section = lambda k: pl.BlockSpec((None, BLOCK, SGU_W), lambda n: (k, at(n), 0))
    return pl.pallas_call(
        body,
        name="mixers_bwd",
        grid=(N_BLOCKS + 1,),
        in_specs=[pl.BlockSpec(memory_space=pltpu.SMEM),
                  section(0),
                  tiles,
                  blk(KVX_W),
                  tiles,
                  section(0),
                  section(1),
                  section(1), section(2), section(3),
                  _full((1, SGU_W)), _full((1, SGU_W)), _full((N_SGU_HEADS, BLOCK, BLOCK)), _full((BLOCK, SGU_W)),
                  VMEM_SPEC],
        out_specs=(pl.BlockSpec((BLOCK, ATTN_SECTION), lambda n: (jnp.maximum(n - 1, 0), 0)),
                   _full((1, LANES)), _full((1, ATTN_SECTION)),
                   pl.BlockSpec((BLOCK, SGU_SECTION), lambda n: (at(n), 0)),
                   _full((N_SGU_HEADS, BLOCK, BLOCK)), _full((N_SGU_HEADS, BLOCK)),
                   _full((8, SGU_W)), _full((1, SGU_SECTION)), VMEM_SPEC),
        out_shape=(jax.ShapeDtypeStruct((SEQ, ATTN_SECTION), MXU_DTYPE),
                   jax.ShapeDtypeStruct((1, LANES), F32),
                   jax.ShapeDtypeStruct((1, ATTN_SECTION), F32),
                   jax.ShapeDtypeStruct((SEQ, SGU_SECTION), MXU_DTYPE),
                   jax.ShapeDtypeStruct((N_SGU_HEADS, BLOCK, BLOCK), F32),
                   jax.ShapeDtypeStruct((N_SGU_HEADS, BLOCK), F32),
                   jax.ShapeDtypeStruct((8, SGU_W), F32),
                   jax.ShapeDtypeStruct((1, SGU_SECTION), F32),
                   jax.ShapeDtypeStruct((WOUT_ROWS, D_MODEL), F32)),
        scratch_shapes=([pltpu.VMEM((BLOCK, KVX_W), MXU_DTYPE),
                         pltpu.VMEM((BLOCK, 2 * ATTN_W), MXU_DTYPE), pltpu.VMEM((BLOCK, 2 * KV_W), F32),
                         pltpu.VMEM((N_SGU_HEADS, BLOCK, BLOCK), MXU_DTYPE),
                         pltpu.VMEM((N_SGU_HEADS, BLOCK, BLOCK), MXU_DTYPE), pltpu.VMEM((BLOCK, SGU_W), F32)]
                        + _reduce_scatter_scratch(WOUT_ROWS, D_MODEL, COMM_DTYPE) + _dma_sems(REDUCE_SEMS)),
        compiler_params=_params(("arbitrary",), VMEM_LIMIT),
    )(sinks, dmix, q, kvx, out, gates, dmix, gates, gates, gates, ln_g, ln_b, sgu_w, bias_full, gwout)


def _in_proj_bwd(dpa, dps, win_t, x, norm_g, gres):
    tm = TOKEN_TILE

    def body(da_ref, ds_ref, w_ref, x_ref, g_ref, gres_ref, gx_ref, gng_ref):
        @pl.when(pl.program_id(0) == 0)
        def _():
            gng_ref[...] = jnp.zeros_like(gng_ref)

        dh = _dot(da_ref[...], w_ref[0:ATTN_SECTION, :]) + _dot(ds_ref[...], w_ref[ATTN_SECTION:, :])
        xv = x_ref[...]
        r = lax.rsqrt(jnp.mean(xv * xv, axis=-1, keepdims=True) + NORM_EPS)
        xn = xv * r
        gng_ref[...] += jnp.sum(dh * xn, axis=0, keepdims=True)
        dxn = dh * g_ref[...]
        gx_ref[...] = r * (dxn - xn * jnp.mean(dxn * xn, axis=-1, keepdims=True)) + gres_ref[...]

    tile = lambda w: pl.BlockSpec((tm, w), lambda i: (i, 0))
    return pl.pallas_call(
        body,
        name="in_proj_bwd",
        grid=(SEQ // tm,),
        in_specs=[tile(ATTN_SECTION), tile(SGU_SECTION), _full((IN_W, D_MODEL)), tile(D_MODEL),
                  _full((1, D_MODEL)), tile(D_MODEL)],
        out_specs=(tile(D_MODEL), _full((1, D_MODEL))),
        out_shape=(jax.ShapeDtypeStruct((SEQ, D_MODEL), F32), jax.ShapeDtypeStruct((1, D_MODEL), F32)),
        compiler_params=_params(("arbitrary",), VMEM_LIMIT),
    )(dpa, dps, win_t, x, norm_g, gres)


WIN_GRAD_ROWS = 256


def _win_grad_segments():
    segments = []
    for chunk in range(IN_W // WIN_GRAD_ROWS):
        for owner in range(N_DEV):
            lo = max(chunk * WIN_GRAD_ROWS, owner * WIN_ROWS)
            hi = min((chunk + 1) * WIN_GRAD_ROWS, (owner + 1) * WIN_ROWS)
            if lo < hi:
                segments.append((len(segments), chunk, owner, lo, hi - lo))
    return segments


def _win_grad_orders():
    n_chips = N_DEV // 2
    chunks = range(IN_W // WIN_GRAD_ROWS)
    of_chip = [[k for k in chunks if (k * WIN_GRAD_ROWS + WIN_GRAD_ROWS // 2) // (2 * WIN_ROWS) == chip]
               for chip in range(n_chips)]
    orders = []
    for chip in range(n_chips):
        order = [of_chip[chip ^ t][r] for r in range(max(map(len, of_chip))) for t in range(1, n_chips)
                 if r < len(of_chip[chip ^ t])]
        orders.append(order + of_chip[chip])
    return orders


def _win_grad(dpa, dps, h, gsguw, vec_parts):
    rows = WIN_GRAD_ROWS
    n_attn = ATTN_SECTION // rows
    steps = IN_W // rows
    segments = _win_grad_segments()
    n_seg = len(segments)
    per_owner = max(sum(1 for seg in segments if seg[2] == p) for p in range(N_DEV))
    n_parts = len(vec_parts)
    class_rows = (N_DEV // 2) * WIN_ROWS

    orders = _win_grad_orders()
    x, y, _ = _place()
    order = jnp.asarray(orders, jnp.int32)[2 * x + y]

    def body(order_ref, da_ref, ds_ref, h_ref, gsguw_ref, *rest):
        part_refs = rest[:n_parts]
        (shard_ref, sguw_full_ref, vec_out_ref,
         chunks, sa, ra, sb, rc, sa_s, ra_s, sb_s, rc_s, landing, vec_ref, ra_vec, slots,
         send_sems, recv_sems, send1, recv1, send2, recv2) = rest[n_parts:]
        step = pl.program_id(0)
        chunk = order_ref[step]
        last_chunk = order_ref[jnp.maximum(step - 1, 0)]
        x, y, c = _place()
        copies = _Copies(send_sems, recv_sems)
        own_sguw = landing.at[_block_rows((x, y, c), SGUW_ROWS), :]
        start_s, exchange_s, finish_s = _reduce_scatter_plan(copies, 0, gsguw_ref, SGUW_ROWS,
                                                             sa_s, ra_s, sb_s, rc_s, own_sguw)
        gather = _gather_plan(copies, REDUCE_SEMS, landing, SGUW_ROWS)

        def place_of(owner):
            return owner // 4, (owner // 2) % 2, owner % 2

        def class_rows_of(owner, first, n):
            return pl.ds((owner // 2) * WIN_ROWS + first - owner * WIN_ROWS, n)

        def to_sibling(seg):
            sid, _, owner, first, n = seg
            at = class_rows_of(owner, first, n)
            return pltpu.make_async_remote_copy(src_ref=sa.at[at, :], dst_ref=ra.at[at, :], send_sem=send1.at[sid],
                                                recv_sem=recv1.at[sid], device_id=(x, y, 1 - c), device_id_type=MESH)

        def to_owner(seg):
            sid, _, owner, first, n = seg
            px, py, pc = place_of(owner)
            slot = (x + px - 2 * x * px) + 2 * (y + py - 2 * y * py) - 1
            nth = sum(1 for other in segments if other[2] == owner and other[0] < sid)
            dst = rc.at[pl.ds(pl.multiple_of(slot * WIN_ROWS, 16) + first - owner * WIN_ROWS, n), :]
            return pltpu.make_async_remote_copy(src_ref=sb.at[class_rows_of(owner, first, n), :], dst_ref=dst,
                                                send_sem=send2.at[sid], recv_sem=recv2.at[slot * per_owner + nth],
                                                device_id=(px, py, pc), device_id_type=MESH)

        def give(seg, at_step):
            sid, _, owner, first, n = seg

            @pl.when(c != owner % 2)
            def _():
                local = pl.ds(first % rows, n)
                sa[class_rows_of(owner, first, n), :] = chunks[at_step % 2, local, :].astype(sa.dtype)
                to_sibling(seg).start()

        def keep(seg, at_step):
            sid, _, owner, first, n = seg
            px, py, pc = place_of(owner)

            @pl.when(c == pc)
            def _():
                to_sibling(seg).wait_recv()
                local = pl.ds(first % rows, n)
                total = chunks[at_step % 2, local, :] + ra[class_rows_of(owner, first, n), :].astype(F32)
                mine = jnp.logical_and(x == px, y == py)

                @pl.when(mine)
                def _():
                    shard_ref[pl.ds(first - owner * WIN_ROWS, n), :] = total

                @pl.when(jnp.logical_not(mine))
                def _():
                    sb[class_rows_of(owner, first, n), :] = total.astype(sb.dtype)
                    to_owner(seg).start()

        pl.when(step == 0)(start_s)
        pl.when(step == 2)(exchange_s)

        @pl.when(step == 5)
        def _():
            finish_s()
            gather[0]()

        pl.when(step == 7)(gather[1])

        @pl.when(chunk < n_attn)
        def _():
            chunks[step % 2] = _dot(da_ref[...], h_ref[...], TN)

        @pl.when(chunk >= n_attn)
        def _():
            chunks[step % 2] = _dot(ds_ref[...], h_ref[...], TN)

        for k in range(steps):
            @pl.when(jnp.logical_and(step > 0, last_chunk == k))
            def _():
                for seg in segments:
                    if seg[1] == k:
                        keep(seg, step - 1)

            @pl.when(chunk == k)
            def _():
                for seg in segments:
                    if seg[1] == k:
                        give(seg, step)

        for k in sorted({chip_order[-1] for chip_order in orders}):
            @pl.when(jnp.logical_and(step == steps - 1, chunk == k))
            def _():
                for seg in segments:
                    if seg[1] == k:
                        keep(seg, step)

        @pl.when(step == steps - 1)
        def _():
            for owner in range(N_DEV):
                px, py, pc = place_of(owner)

                @pl.when(jnp.logical_and(jnp.logical_and(x == px, y == py), c == pc))
                def _():
                    mine = [seg for seg in segments if seg[2] == owner]
                    for slot in range(3):
                        for nth, (sid, _, _, first, n) in enumerate(mine):
                            landed = rc.at[pl.ds(slot * WIN_ROWS + first - owner * WIN_ROWS, n), :]
                            pltpu.make_async_remote_copy(
                                src_ref=landed, dst_ref=landed, send_sem=send2.at[sid],
                                recv_sem=recv2.at[slot * per_owner + nth], device_id=(x, y, c),
                                device_id_type=MESH).wait_recv()
                    acc = shard_ref[...]
                    for slot in range(3):
                        acc = acc + rc[slot * WIN_ROWS:(slot + 1) * WIN_ROWS, :].astype(F32)
                    shard_ref[...] = acc
            for seg in segments:
                owner = seg[2]
                px, py, pc = place_of(owner)

                @pl.when(c != pc)
                def _():
                    to_sibling(seg).wait_send()

                @pl.when(jnp.logical_and(c == pc, jnp.logical_not(jnp.logical_and(x == px, y == py))))
                def _():
                    to_owner(seg).wait_send()
            gather[2]()
            sguw_full_ref[...] = landing[...]
            _all_reduce_vectors(copies, REDUCE_SEMS + GATHER_SEMS, *part_refs, vec_out_ref, vec_ref, ra_vec, slots)

    grid_spec = pltpu.PrefetchScalarGridSpec(
        num_scalar_prefetch=1,
        grid=(steps,),
        in_specs=[pl.BlockSpec((SEQ, rows), lambda i, order_ref: (0, jnp.minimum(order_ref[i], n_attn - 1))),
                  pl.BlockSpec((SEQ, rows), lambda i, order_ref: (0, jnp.maximum(order_ref[i] - n_attn, 0))),
                  _full((SEQ, D_MODEL)), VMEM_SPEC] + [VMEM_SPEC] * n_parts,
        out_specs=(VMEM_SPEC, _full((N_SGU_HEADS * BLOCK, BLOCK)), VMEM_SPEC),
        scratch_shapes=([pltpu.VMEM((2, rows, D_MODEL), F32),
                         pltpu.VMEM((class_rows, D_MODEL), COMM_DTYPE), pltpu.VMEM((class_rows, D_MODEL), COMM_DTYPE),
                         pltpu.VMEM((class_rows, D_MODEL), COMM_DTYPE), pltpu.VMEM((3 * WIN_ROWS, D_MODEL), COMM_DTYPE)]
                        + _reduce_scatter_scratch(SGUW_ROWS, BLOCK, F32)
                        + [pltpu.VMEM((N_SGU_HEADS * BLOCK, BLOCK), F32)]
                        + _vector_scratch() + _dma_sems(REDUCE_SEMS + GATHER_SEMS + VECTOR_SEMS)
                        + _dma_sems(n_seg) + [pltpu.SemaphoreType.DMA((n_seg,)),
                                              pltpu.SemaphoreType.DMA((3 * per_owner,))]))
    return pl.pallas_call(
        body,
        name="win_grad",
        grid_spec=grid_spec,
        out_shape=(jax.ShapeDtypeStruct((WIN_ROWS, D_MODEL), F32),
                   jax.ShapeDtypeStruct((N_SGU_HEADS * BLOCK, BLOCK), F32),
                   jax.ShapeDtypeStruct((VEC_ROWS, IN_W), F32)),
        compiler_params=_params(("arbitrary",), VMEM_LIMIT),
    )(order, dpa, dps, h, gsguw, *vec_parts)


VEC_NORM_G, VEC_B_IN, VEC_SINKS, VEC_LN_G, VEC_LN_B, VEC_B_OUT, VEC_FINAL_G, VEC_LOSS, VEC_SGU_B = 0, 1, 2, 3, 4, 5, 6, 7, 8


def _adamw(w, g, m, v):
    m = ADAM_B1 * m + (1.0 - ADAM_B1) * g
    v = ADAM_B2 * v + (1.0 - ADAM_B2) * (g * g)
    m_hat = m / (1.0 - ADAM_B1 ** ADAM_STEP)
    v_hat = v / (1.0 - ADAM_B2 ** ADAM_STEP)
    delta = -ADAM_LR * (m_hat / (jnp.sqrt(v_hat) + ADAM_EPS) + ADAM_WD * w)
    return delta, m, v


def _adamw_shard(name, g, w, m, v, block_rows):
    def body(g_ref, w_ref, m_ref, v_ref, d_ref, nm_ref, nv_ref):
        d_ref[...], nm_ref[...], nv_ref[...] = _adamw(w_ref[...], g_ref[...], m_ref[...], v_ref[...])

    rows, cols = w.shape
    spec = pl.BlockSpec((block_rows, cols), lambda i: (i, 0))
    return pl.pallas_call(
        body,
        name=name,
        grid=(rows // block_rows,),
        in_specs=[spec] * 4,
        out_specs=(spec,) * 3,
        out_shape=(jax.ShapeDtypeStruct(w.shape, F32),) * 3,
        compiler_params=_params(("arbitrary",)),
    )(g, w, m, v)


VECTOR_SEMS = 4


def _vector_scratch():
    return [pltpu.VMEM((VEC_ROWS, IN_W), F32), pltpu.VMEM((VEC_ROWS, IN_W), F32),
            pltpu.VMEM((4 * VEC_ROWS, IN_W), F32)]


def _all_reduce_vectors(copies, sem0, gng_ref, gba_ref, gbs_ref, gsink_ref, gln_ref, gsgub_ref, vec4_ref, out_ref,
                        vec_ref, ra_vec, slots):
    x, y, c = _place()
    vec_ref[...] = jnp.zeros_like(vec_ref)
    vec_ref[VEC_NORM_G:VEC_NORM_G + 1, 0:D_MODEL] = gng_ref[...]
    vec_ref[VEC_B_IN:VEC_B_IN + 1, 0:ATTN_SECTION] = gba_ref[...]
    vec_ref[VEC_B_IN:VEC_B_IN + 1, ATTN_SECTION:IN_W] = gbs_ref[...]
    vec_ref[VEC_SINKS:VEC_SINKS + 1, 0:LANES] = gsink_ref[...]
    vec_ref[VEC_LN_G:VEC_LN_G + 1, 0:SGU_W] = gln_ref[0:1, :]
    vec_ref[VEC_LN_B:VEC_LN_B + 1, 0:SGU_W] = gln_ref[1:2, :]
    vec_ref[VEC_B_OUT:VEC_B_OUT + 1, 0:D_MODEL] = vec4_ref[2:3, :]
    vec_ref[VEC_FINAL_G:VEC_FINAL_G + 1, 0:D_MODEL] = vec4_ref[1:2, :]
    vec_ref[VEC_LOSS:VEC_LOSS + 1, 0:D_MODEL] = vec4_ref[0:1, :]
    vec_ref[VEC_SGU_B:VEC_SGU_B + N_SGU_HEADS, 0:BLOCK] = gsgub_ref[...]

    to_sibling = copies(sem0, vec_ref, ra_vec, (x, y, 1 - c))
    to_sibling.start()
    to_sibling.wait_recv()

    def chip_slot(place):
        return slots.at[pl.ds(pl.multiple_of((2 * place[0] + place[1]) * VEC_ROWS, 8), VEC_ROWS), :]

    mine = chip_slot((x, y))
    mine[...] = vec_ref[...] + ra_vec[...]
    to_chips = [copies(sem0 + i, mine, mine, (*_chip(rel), c)) for i, rel in enumerate(RELATIONS[1:], start=1)]
    for cp in to_chips:
        cp.start()
    for i, rel in enumerate(RELATIONS[1:], start=1):
        theirs = chip_slot(_chip(rel))
        copies(sem0 + i, theirs, theirs, (x, y, c)).wait_recv()
    out_ref[...] = ((slots[0:VEC_ROWS, :] + slots[VEC_ROWS:2 * VEC_ROWS, :])
                    + slots[2 * VEC_ROWS:3 * VEC_ROWS, :]) + slots[3 * VEC_ROWS:, :]
    to_sibling.wait_send()
    for cp in to_chips:
        cp.wait_send()


def _adamw_replicated(vec, gsguw, weights, m_state, v_state):
    n = len(SMALL)

    def body(*refs):
        vec_ref, gsguw_ref = refs[0], refs[1]
        w_refs, m_refs, v_refs = (refs[2 + k * n:2 + (k + 1) * n] for k in range(3))
        outs = refs[2 + 3 * n:]
        g_refs, d_refs, nm_refs, nv_refs = (outs[k * n:(k + 1) * n] for k in range(4))
        for i, (_, row, shape) in enumerate(SMALL):
            g = gsguw_ref[...] if row is None else vec_ref[row:row + shape[0], 0:shape[1]]
            g_refs[i][...] = g
            d_refs[i][...], nm_refs[i][...], nv_refs[i][...] = _adamw(
                w_refs[i][...], g, m_refs[i][...], v_refs[i][...])

    shapes = tuple(jax.ShapeDtypeStruct(shape, F32) for _, _, shape in SMALL)
    outs = pl.pallas_call(
        body,
        name="adamw_replicated",
        in_specs=[VMEM_SPEC] * (2 + 3 * n),
        out_specs=(VMEM_SPEC,) * (4 * n),
        out_shape=shapes * 4,
    )(vec, gsguw, *weights, *m_state, *v_state)
    return tuple(outs[k * n:(k + 1) * n] for k in range(4))


SMALL = (
    ("norm_g", VEC_NORM_G, (1, D_MODEL)),
    ("b_in", VEC_B_IN, (1, IN_W)),
    ("attn_sinks", VEC_SINKS, (1, N_Q_HEADS)),
    ("sgu_ln_g", VEC_LN_G, (1, SGU_W)),
    ("sgu_ln_b", VEC_LN_B, (1, SGU_W)),
    ("sgu_w", None, (N_SGU_HEADS * BLOCK, BLOCK)),
    ("sgu_b", VEC_SGU_B, (N_SGU_HEADS, BLOCK)),
    ("b_out", VEC_B_OUT, (1, D_MODEL)),
    ("final_norm_g", VEC_FINAL_G, (1, D_MODEL)),
)


def _local_grads(x, target, h, win_t, wout_shard, norm_g, b_in, attn_sinks, sgu_ln_g, sgu_ln_b, sgu_w, sgu_b, b_out,
                 final_g):
    sinks = attn_sinks.reshape(N_Q_HEADS)
    bias_full = jnp.repeat(sgu_b.T, HEAD_DIM, axis=1)
    q, kvx, gates, wout = _in_proj(h, b_in, win_t, wout_shard)
    out, gres, dmix, gwout, vec4 = _mixers_out_proj(sinks, q, kvx, gates, sgu_ln_g, sgu_ln_b, sgu_w, bias_full,
                                                    x, target, wout, b_out, final_g)
    dpa, gsink, gbin_a, dps, gsguw, gsgub, gln, gbin_s, gwout_shard = _mixers_bwd(
        sinks, dmix, q, kvx, out, gates, sgu_ln_g, sgu_ln_b, sgu_w, bias_full, gwout)
    grad_x, gng = _in_proj_bwd(dpa, dps, win_t, x, norm_g, gres)
    gwin_shard, gsguw_sum, vec = _win_grad(dpa, dps, h, gsguw.reshape(N_SGU_HEADS * BLOCK, BLOCK),
                                           (gng, gbin_a, gbin_s, gsink, gln, gsgub, vec4))
    return grad_x, gwin_shard, gwout_shard, gsguw_sum, vec


def kernel(x, norm_g, w_in, b_in, attn_sinks, sgu_ln_g, sgu_ln_b, sgu_w, sgu_b, w_out, b_out, final_norm_g, loss_target, m_norm_g, m_w_in, m_b_in, m_attn_sinks, m_sgu_ln_g, m_sgu_ln_b, m_sgu_w, m_sgu_b, m_w_out, m_b_out, m_final_norm_g, v_norm_g, v_w_in, v_b_in, v_attn_sinks, v_sgu_ln_g, v_sgu_ln_b, v_sgu_w, v_sgu_b, v_w_out, v_b_out, v_final_norm_g):
    given = dict(norm_g=norm_g, b_in=b_in, attn_sinks=attn_sinks, sgu_ln_g=sgu_ln_g, sgu_ln_b=sgu_ln_b,
                 sgu_w=sgu_w, sgu_b=sgu_b, b_out=b_out, final_norm_g=final_norm_g)
    m_given = dict(norm_g=m_norm_g, b_in=m_b_in, attn_sinks=m_attn_sinks, sgu_ln_g=m_sgu_ln_g,
                   sgu_ln_b=m_sgu_ln_b, sgu_w=m_sgu_w, sgu_b=m_sgu_b, b_out=m_b_out, final_norm_g=m_final_norm_g)
    v_given = dict(norm_g=v_norm_g, b_in=v_b_in, attn_sinks=v_attn_sinks, sgu_ln_g=v_sgu_ln_g,
                   sgu_ln_b=v_sgu_ln_b, sgu_w=v_sgu_w, sgu_b=v_sgu_b, b_out=v_b_out, final_norm_g=v_final_norm_g)

    win_t, h = _all_gather_win(w_in[0].T, x[0], norm_g)
    grad_x, gwin_t, gwout, gsguw, vec = _local_grads(
        x[0], loss_target[0], h, win_t, w_out[0], norm_g, b_in, attn_sinks, sgu_ln_g, sgu_ln_b, sgu_w[0], sgu_b[0],
        b_out, final_norm_g.reshape(1, D_MODEL))

    t = lambda a: a[0].T
    d_win, nm_win, nv_win = _adamw_shard("adamw_w_in", gwin_t, t(w_in), t(m_w_in), t(v_w_in), WIN_ROWS // 2)
    d_wout, nm_wout, nv_wout = _adamw_shard("adamw_w_out", gwout, w_out[0], m_w_out[0], v_w_out[0], WOUT_ROWS)
    as_2d = lambda d: [d[name].reshape(shape) for name, _, shape in SMALL]
    loss = vec[VEC_LOSS, 0]
    small = _adamw_replicated(vec, gsguw, as_2d(given), as_2d(m_given), as_2d(v_given))

    def assemble(big_in, big_out, k):
        vals = {name: small[k][i].reshape(given[name].shape) for i, (name, _, _) in enumerate(SMALL)}
        vals["w_in"] = big_in.T[None]
        vals["w_out"] = big_out[None]
        order = ("norm_g", "w_in", "b_in", "attn_sinks", "sgu_ln_g", "sgu_ln_b", "sgu_w", "sgu_b", "w_out",
                 "b_out", "final_norm_g")
        return [vals[name] for name in order]

    return (loss, grad_x[None],
            *assemble(gwin_t, gwout, 0), *assemble(d_win, d_wout, 1),
            *assemble(nm_win, nm_wout, 2), *assemble(nv_win, nv_wout, 3))
```

```python
import functools
import math

import jax
import jax.numpy as jnp
from jax import lax
from jax.experimental import pallas as pl
from jax.experimental.pallas import tpu as pltpu

F32 = jnp.float32
BF16 = jnp.bfloat16
MXU_DTYPE = BF16
COMM_DTYPE = BF16

D_MODEL = 1024
SEQ = 4096
HEAD_DIM = 64
N_Q_HEADS = 8
Q_PER_KV = 4
BLOCK = 128
N_BLOCKS = SEQ // BLOCK
ATTN_W = 512
KV_W = 128
SGU_W = 512
N_SGU_HEADS = 8
IN_W = 2816
NORM_EPS = 1e-5
NEG_INF = -1e30
SCALE = HEAD_DIM ** -0.5
KV0 = ATTN_W
GATE0 = ATTN_W + 2 * KV_W
SGU0 = GATE0 + ATTN_W
ATTN_SECTION = SGU0
SGU_SECTION = IN_W - SGU0

ADAM_LR = 0.001
ADAM_B1 = 0.9
ADAM_B2 = 0.999
ADAM_EPS = 1e-08
ADAM_WD = 0.01
ADAM_STEP = 10

N_DEV = 8
WIN_ROWS = IN_W // N_DEV
WOUT_ROWS = D_MODEL // N_DEV
SGUW_ROWS = N_SGU_HEADS * BLOCK // N_DEV
VEC_ROWS = 16
MESH = pl.DeviceIdType.MESH

LANES = 128
HALF = LANES // 2
N_PAIRS = N_Q_HEADS * HEAD_DIM // LANES
KVX_W = 12 * LANES
TOKEN_TILE = 256
FWD_TOKEN_TILE = 512
ATTN_FWD_AHEAD = 4
FUSED_BLOCKS = 2
SGU_MIX_AFTER_CHAIN = 0
SGU_GATES_AFTER_CHAIN = 1
SGU_GRADS_AFTER_CHAIN = 5
ATTN_BWD_AHEAD = 3
VMEM_LIMIT = 56 * 1024 * 1024

NN = (((1,), (0,)), ((), ()))
NT = (((1,), (1,)), ((), ()))
TN = (((0,), (0,)), ((), ()))


def _dot(a, b, dims=NN):
    return lax.dot_general(a.astype(MXU_DTYPE), b.astype(MXU_DTYPE), dims, preferred_element_type=F32)


def _gelu(x):
    return x * (lax.erf(x * (1.0 / math.sqrt(2.0))) + 1.0) * 0.5


def _gelu_grad(x):
    cdf = (lax.erf(x * (1.0 / math.sqrt(2.0))) + 1.0) * 0.5
    return cdf + x * jnp.exp(-0.5 * x * x) * (1.0 / math.sqrt(2.0 * math.pi))


def _silu_and_grad(z):
    s = jax.nn.sigmoid(z)
    return z * s, s * (1.0 + z * (1.0 - s))


def _params(semantics=None, vmem=None):
    kw = {}
    if semantics is not None:
        kw["dimension_semantics"] = semantics
    if vmem is not None:
        kw["vmem_limit_bytes"] = vmem
    return pltpu.CompilerParams(**kw)


def _full(shape):
    return pl.BlockSpec(shape, lambda *_: (0,) * len(shape))


VMEM_SPEC = pl.BlockSpec(memory_space=pltpu.VMEM)


RELATIONS = ((0, 0), (1, 0), (0, 1), (1, 1))


def _place():
    return lax.axis_index("x"), lax.axis_index("y"), lax.axis_index("c")


def _chip(rel):
    x, y, _ = _place()
    return (1 - x if rel[0] else x, 1 - y if rel[1] else y)


def _block_rows(place, n_rows):
    px, py, pc = place
    return pl.ds(pl.multiple_of((4 * px + 2 * py + pc) * n_rows, 16), n_rows)


class _Copies:
    def __init__(self, send_sems, recv_sems):
        self.send_sems, self.recv_sems = send_sems, recv_sems

    def __call__(self, k, src, dst, to):
        return pltpu.make_async_remote_copy(src_ref=src, dst_ref=dst, send_sem=self.send_sems.at[k],
                                            recv_sem=self.recv_sems.at[k], device_id=to, device_id_type=MESH)


def _gather_plan(copies, sem0, full_ref, n_rows):
    x, y, c = _place()
    me, sibling = (x, y, c), (x, y, 1 - c)
    chips = [_chip(rel) for rel in RELATIONS[1:]]

    def cp(k, block, to):
        rows = full_ref.at[_block_rows(block, n_rows), :]
        return copies(sem0 + k, rows, rows, to)

    first = [cp(0, me, sibling)] + [cp(1 + j, me, (*chip, c)) for j, chip in enumerate(chips)]
    passed = [cp(4 + j, (*chip, c), sibling) for j, chip in enumerate(chips)]

    def start():
        for f in first:
            f.start()

    def forward():
        for j, chip in enumerate(chips):
            cp(1 + j, (*chip, c), me).wait_recv()
            passed[j].start()

    def finish():
        cp(0, sibling, me).wait_recv()
        for j, chip in enumerate(chips):
            cp(4 + j, (*chip, 1 - c), me).wait_recv()
        for f in first + passed:
            f.wait_send()

    return start, forward, finish


GATHER_SEMS = 7


def _reduce_scatter_plan(copies, sem0, part_ref, n_rows, sa, ra, sb, rc, res_ref):
    x, y, c = _place()
    sibling = (x, y, 1 - c)
    n = n_rows
    level1 = copies(sem0, sa, ra, sibling)

    def level2(i):
        slot = pl.ds((i - 1) * n, n)
        return copies(sem0 + i, sb.at[slot, :], rc.at[slot, :], (*_chip(RELATIONS[i]), c))

    def start():
        for i, rel in enumerate(RELATIONS):
            sa[i * n:(i + 1) * n, :] = part_ref[_block_rows((*_chip(rel), 1 - c), n), :].astype(sa.dtype)
        level1.start()

    def exchange():
        level1.wait_recv()
        for i, rel in enumerate(RELATIONS):
            total = part_ref[_block_rows((*_chip(rel), c), n), :] + ra[i * n:(i + 1) * n, :].astype(F32)
            if i == 0:
                res_ref[...] = total
            else:
                sb[(i - 1) * n:i * n, :] = total.astype(sb.dtype)
                level2(i).start()

    def finish():
        acc = res_ref[...]
        for i in range(1, len(RELATIONS)):
            level2(i).wait_recv()
            acc = acc + rc[(i - 1) * n:i * n, :].astype(F32)
        res_ref[...] = acc
        level1.wait_send()
        for i in range(1, len(RELATIONS)):
            level2(i).wait_send()

    return start, exchange, finish


REDUCE_SEMS = 4


def _reduce_scatter_scratch(n_rows, width, dtype):
    return [pltpu.VMEM((4 * n_rows, width), dtype), pltpu.VMEM((4 * n_rows, width), dtype),
            pltpu.VMEM((3 * n_rows, width), dtype), pltpu.VMEM((3 * n_rows, width), dtype)]


def _dma_sems(n):
    return [pltpu.SemaphoreType.DMA((n,)), pltpu.SemaphoreType.DMA((n,))]


def _all_gather_win(win_t_shard, x, norm_g):
    tm = FWD_TOKEN_TILE
    steps = SEQ // tm

    def body(win_ref, x_ref, g_ref, full_ref, h_ref, landing, send_sems, recv_sems):
        step = pl.program_id(0)
        start, forward, finish = _gather_plan(_Copies(send_sems, recv_sems), 0, landing, WIN_ROWS)

        @pl.when(step == 0)
        def _():
            landing[_block_rows(_place(), WIN_ROWS), :] = win_ref[...].astype(COMM_DTYPE)
            start()

        xv = x_ref[...]
        r = lax.rsqrt(jnp.mean(xv * xv, axis=-1, keepdims=True) + NORM_EPS)
        h_ref[...] = ((xv * r) * g_ref[...]).astype(MXU_DTYPE)

        @pl.when(step == steps - 1)
        def _():
            forward()
            finish()
            full_ref[...] = landing[...]

    return pl.pallas_call(
        body,
        name="all_gather_win",
        grid=(steps,),
        in_specs=[VMEM_SPEC, pl.BlockSpec((tm, D_MODEL), lambda i: (i, 0)), _full((1, D_MODEL))],
        out_specs=(_full((IN_W, D_MODEL)), pl.BlockSpec((tm, D_MODEL), lambda i: (i, 0))),
        out_shape=(jax.ShapeDtypeStruct((IN_W, D_MODEL), COMM_DTYPE),
                   jax.ShapeDtypeStruct((SEQ, D_MODEL), MXU_DTYPE)),
        scratch_shapes=[pltpu.VMEM((IN_W, D_MODEL), COMM_DTYPE)] + _dma_sems(GATHER_SEMS),
        compiler_params=_params(("arbitrary",), VMEM_LIMIT),
    )(win_t_shard, x, norm_g)


def _in_proj(h, b_in, win_t, wout_shard):
    tm = FWD_TOKEN_TILE
    steps = SEQ // tm

    def body(h_ref, b_ref, w_ref, wout_ref, q_ref, kvx_ref, gate_ref, wfull_ref, landing, send_sems, recv_sems):
        step = pl.program_id(0)
        start, forward, finish = _gather_plan(_Copies(send_sems, recv_sems), 0, landing, WOUT_ROWS)

        @pl.when(step == 0)
        def _():
            landing[_block_rows(_place(), WOUT_ROWS), :] = wout_ref[...].astype(COMM_DTYPE)
            start()

        pl.when(step == steps // 2)(forward)

        h = h_ref[...]

        def proj(lo, hi):
            return _dot(h, w_ref[lo:hi, :], NT) + b_ref[:, lo:hi]

        qs = proj(0, ATTN_W) * SCALE
        for pair in range(N_PAIRS):
            q_ref[pair] = qs[:, pair * LANES:(pair + 1) * LANES].astype(MXU_DTYPE)
        kv = proj(KV0, GATE0)
        low = lax.broadcasted_iota(jnp.int32, (tm, LANES), 1) < HALF
        for i in range(2):
            t = kv[:, i * LANES:(i + 1) * LANES]
            rot = pltpu.roll(t, HALF, 1)
            variants = (jnp.where(low, t, 0.0), jnp.where(low, 0.0, rot),
                        jnp.where(low, rot, 0.0), jnp.where(low, 0.0, t))
            for j, val in enumerate(variants):
                col = (4 * i + j) * LANES
                kvx_ref[:, col:col + LANES] = val.astype(MXU_DTYPE)
                if i == 1:
                    ones_elsewhere = jnp.where(low == (j % 2 == 0), val, 1.0)
                    kvx_ref[:, col + 4 * LANES:col + 5 * LANES] = ones_elsewhere.astype(MXU_DTYPE)
        for k in range(4):
            gate_ref[k] = proj(GATE0 + k * SGU_W, GATE0 + (k + 1) * SGU_W)

        @pl.when(step == steps - 1)
        def _():
            finish()
            wfull_ref[...] = landing[...]

    return pl.pallas_call(
        body,
        name="in_proj",
        grid=(steps,),
        in_specs=[pl.BlockSpec((tm, D_MODEL), lambda i: (i, 0)),
                  _full((1, IN_W)), _full((IN_W, D_MODEL)), VMEM_SPEC],
        out_specs=(pl.BlockSpec((N_PAIRS, tm, LANES), lambda i: (0, i, 0)),
                   pl.BlockSpec((tm, KVX_W), lambda i: (i, 0)),
                   pl.BlockSpec((4, tm, SGU_W), lambda i: (0, i, 0)),
                   _full((D_MODEL, D_MODEL))),
        out_shape=(jax.ShapeDtypeStruct((N_PAIRS, SEQ, LANES), MXU_DTYPE),
                   jax.ShapeDtypeStruct((SEQ, KVX_W), MXU_DTYPE),
                   jax.ShapeDtypeStruct((4, SEQ, SGU_W), F32),
                   jax.ShapeDtypeStruct((D_MODEL, D_MODEL), COMM_DTYPE)),
        scratch_shapes=[pltpu.VMEM((D_MODEL, D_MODEL), COMM_DTYPE)] + _dma_sems(GATHER_SEMS),
        compiler_params=_params(("arbitrary",), VMEM_LIMIT),
    )(h, b_in, win_t, wout_shard)


def _window_mask(n):
    qi = lax.broadcasted_iota(jnp.int32, (2 * BLOCK, 2 * BLOCK), 0) & (BLOCK - 1)
    p = lax.broadcasted_iota(jnp.int32, (2 * BLOCK, 2 * BLOCK), 1) - BLOCK
    in_window = jnp.logical_and(p <= qi, p > qi - BLOCK)
    return jnp.logical_and(in_window, jnp.logical_or(p >= 0, n > 0))


def _sink_column(sink_ref, g, par):
    return jnp.concatenate([jnp.full((BLOCK, 1), sink_ref[4 * g + par], F32),
                            jnp.full((BLOCK, 1), sink_ref[4 * g + 2 + par], F32)], axis=0)


def _kv_cat(kp_ref, kc_ref, var, with_ones):
    kcol, vcol = var * LANES, (var + (8 if with_ones else 4)) * LANES
    return (jnp.concatenate([kp_ref[:, kcol:kcol + LANES], kc_ref[:, kcol:kcol + LANES]], axis=0),
            jnp.concatenate([kp_ref[:, vcol:vcol + LANES], kc_ref[:, vcol:vcol + LANES]], axis=0))


def _softmax_numerator(s, sink):
    m = jnp.maximum(jnp.max(s, axis=1, keepdims=True), sink)
    return jnp.exp(s - m), m


def _mixers_out_proj(sinks, q, kvx, gates, ln_g, ln_b, sgu_w, bias_full, x, target, wout, b_out, final_g):
    tm = FUSED_BLOCKS * BLOCK
    n_tiles = SEQ // tm

    def body(sink_ref, q_ref, kc_ref, za_ref, us_ref, vs_ref, zs_ref, lng_ref, lnb_ref, w_ref, bias_ref,
             x_ref, t_ref, wout_ref, b_ref, gf_ref,
             out_ref, gres_ref, dmix_ref, gw_ref, vec_ref,
             kp_ref, wm_ref, mixed_next, mixed_cur, out_stage, gb_ref):
        step = pl.program_id(0)

        @pl.when(step == 0)
        def _():
            kp_ref[...] = jnp.zeros_like(kp_ref)
            _mask_sgu_weights(w_ref, wm_ref)
            gw_ref[...] = jnp.zeros_like(gw_ref)
            vec_ref[...] = jnp.zeros_like(vec_ref)
            mixed_cur[...] = jnp.zeros_like(mixed_cur)

        def mixers_block(b, after_chain=()):
            rows = slice(b * BLOCK, (b + 1) * BLOCK)
            kc = kc_ref.at[rows, :]
            u, _, _, vln = _sgu_activations(us_ref[rows, :], vs_ref[rows, :], lng_ref[...], lnb_ref[...])

            valid = _window_mask(step * FUSED_BLOCKS + b)[0:BLOCK]
            chains = [(g, par, i) for g in range(2) for par in range(2) for i in range(2)]
            kv = {(g, par): _kv_cat(kp_ref, kc, 2 * g + par, True) for g in range(2) for par in range(2)}
            scores, outs = {}, {}

            def issue_scores(k):
                g, par, i = chains[k]
                scores[k] = _dot(q_ref[2 * g + i, rows, :], kv[g, par][0], NT)

            ahead = ATTN_FWD_AHEAD
            for k in range(ahead):
                issue_scores(k)
            low = lax.broadcasted_iota(jnp.int32, (BLOCK, LANES), 1) < HALF
            for k, (g, par, i) in enumerate(chains):
                sink = sink_ref[4 * g + 2 * i + par]
                e, m = _softmax_numerator(jnp.where(valid, scores[k], NEG_INF), sink)
                if k + ahead < len(chains):
                    issue_scores(k + ahead)
                o = _dot(e, kv[g, par][1])
                outs[g, par, i] = o / (pltpu.roll(o, HALF, 1) + jnp.exp(sink - m))
                if k == SGU_MIX_AFTER_CHAIN:
                    mixed = _sgu_mix(vln, wm_ref, bias_ref)
                if k % 2 == 0 and k // 2 < len(after_chain):
                    after_chain[k // 2]()
            for pair in range(N_PAIRS):
                g, i = divmod(pair, 2)
                lanes = slice(pair * LANES, (pair + 1) * LANES)
                o = jnp.where(low, outs[g, 0, i], outs[g, 1, i])
                out_stage[pair, rows, :] = o
                gate, _ = _silu_and_grad(za_ref[rows, lanes])
                mixed_next[rows, lanes] = (o * gate).astype(MXU_DTYPE)
            kp_ref[...] = kc[...]
            for pair in range(N_SGU_HEADS // 2):
                cols = slice(pair * LANES, (pair + 1) * LANES)
                gate, _ = _silu_and_grad(zs_ref[rows, cols])
                mixed_next[rows, ATTN_W + pair * LANES:ATTN_W + (pair + 1) * LANES] = (
                    u[:, cols] * mixed[pair] * gate).astype(MXU_DTYPE)

        live = (step > 0).astype(F32)
        quarter = D_MODEL // 4
        columns = [None] * 4

        def project(j):
            def piece():
                columns[j] = _dot(mixed_cur[...], wout_ref[:, j * quarter:(j + 1) * quarter])
            return piece

        half_blocks = FUSED_BLOCKS // 2
        per_block = 4 // half_blocks
        for b in range(half_blocks):
            mixers_block(b, [project(j) for j in range(b * per_block, (b + 1) * per_block)])
        xo = x_ref[...] + jnp.concatenate(columns, axis=1) + b_ref[...]
        r = lax.rsqrt(jnp.mean(xo * xo, axis=-1, keepdims=True) + NORM_EPS)
        xn = xo * r
        gf = gf_ref[...]
        err = xn * gf - t_ref[...]
        loss = 0.5 * jnp.sum(jnp.mean(err * err, axis=-1, keepdims=True), axis=0, keepdims=True)
        dy = err * (1.0 / D_MODEL)
        dxn = dy * gf
        gres = r * (dxn - xn * jnp.mean(dxn * xn, axis=-1, keepdims=True))
        vec_ref[0:1, :] += jnp.broadcast_to(loss * live, (1, D_MODEL))
        vec_ref[1:2, :] += jnp.sum(dy * xn, axis=0, keepdims=True) * live
        vec_ref[2:3, :] += jnp.sum(gres, axis=0, keepdims=True) * live
        gres_ref[...] = gres
        gb_ref[...] = gres.astype(MXU_DTYPE)

        def branch_grad(k):
            def piece():
                dmix_ref[k] = _dot(gb_ref[...], wout_ref[k * ATTN_W:(k + 1) * ATTN_W, :], NT)
            return piece

        def weight_grad(k):
            def piece():
                rows = slice(k * ATTN_W, (k + 1) * ATTN_W)
                gw_ref[rows, :] += _dot(mixed_cur[:, rows], gb_ref[...], TN)
            return piece

        backward = [branch_grad(0), branch_grad(1), weight_grad(0), weight_grad(1)]
        for b in range(half_blocks):
            mixers_block(half_blocks + b, backward[b * per_block:(b + 1) * per_block])

        @pl.when(step < n_tiles)
        def _():
            out_ref[...] = out_stage[...]

        mixed_cur[...] = mixed_next[...]

    ahead_tile = lambda i: jnp.minimum(i, n_tiles - 1)
    behind_tile = lambda i: jnp.maximum(i - 1, 0)
    blk = lambda w: pl.BlockSpec((tm, w), lambda i: (ahead_tile(i), 0))
    tiles = pl.BlockSpec((N_PAIRS, tm, LANES), lambda i: (0, ahead_tile(i), 0))
    gate = lambda k: pl.BlockSpec((None, tm, SGU_W), lambda i: (k, ahead_tile(i), 0))
    behind = lambda w: pl.BlockSpec((tm, w), lambda i: (behind_tile(i), 0))
    return pl.pallas_call(
        body,
        name="mixers_out_proj",
        grid=(n_tiles + 1,),
        in_specs=[pl.BlockSpec(memory_space=pltpu.SMEM), tiles, blk(KVX_W), gate(0), gate(1), gate(2), gate(3),
                  _full((1, SGU_W)), _full((1, SGU_W)), _full((N_SGU_HEADS, BLOCK, BLOCK)), _full((BLOCK, SGU_W)),
                  behind(D_MODEL), behind(D_MODEL), _full((D_MODEL, D_MODEL)), _full((1, D_MODEL)),
                  _full((1, D_MODEL))],
        out_specs=(tiles, behind(D_MODEL), pl.BlockSpec((2, tm, ATTN_W), lambda i: (0, behind_tile(i), 0)),
                   _full((D_MODEL, D_MODEL)), _full((8, D_MODEL))),
        out_shape=(jax.ShapeDtypeStruct((N_PAIRS, SEQ, LANES), F32),
                   jax.ShapeDtypeStruct((SEQ, D_MODEL), F32),
                   jax.ShapeDtypeStruct((2, SEQ, ATTN_W), F32),
                   jax.ShapeDtypeStruct((D_MODEL, D_MODEL), F32),
                   jax.ShapeDtypeStruct((8, D_MODEL), F32)),
        scratch_shapes=[pltpu.VMEM((BLOCK, KVX_W), MXU_DTYPE), pltpu.VMEM((N_SGU_HEADS, BLOCK, BLOCK), MXU_DTYPE),
                        pltpu.VMEM((tm, D_MODEL), MXU_DTYPE), pltpu.VMEM((tm, D_MODEL), MXU_DTYPE),
                        pltpu.VMEM((N_PAIRS, tm, LANES), F32), pltpu.VMEM((tm, D_MODEL), MXU_DTYPE)],
        compiler_params=_params(("arbitrary",), VMEM_LIMIT),
    )(sinks, q, kvx, gates, gates, gates, gates, ln_g, ln_b, sgu_w, bias_full, x, target, wout, b_out, final_g)


def _sgu_activations(us, vs, lng, lnb):
    u = _gelu(us)
    vg = _gelu(vs)
    mu = jnp.mean(vg, axis=-1, keepdims=True)
    xc = vg - mu
    rstd = lax.rsqrt(jnp.mean(xc * xc, axis=-1, keepdims=True) + NORM_EPS)
    vhat = xc * rstd
    return u, vhat, rstd, vhat * lng + lnb


def _mask_sgu_weights(w_ref, masked_ref, transposed_ref=None):
    tril = (lax.broadcasted_iota(jnp.int32, (BLOCK, BLOCK), 0)
            >= lax.broadcasted_iota(jnp.int32, (BLOCK, BLOCK), 1))
    for hh in range(N_SGU_HEADS):
        w = jnp.where(tril, w_ref[hh], 0.0)
        masked_ref[hh] = w.astype(MXU_DTYPE)
        if transposed_ref is not None:
            transposed_ref[hh] = w.T.astype(MXU_DTYPE)


def _sgu_mix(vln, masked_w_ref, bias_ref):
    low = lax.broadcasted_iota(jnp.int32, (BLOCK, LANES), 1) < HALF
    mixed = []
    for pair in range(N_SGU_HEADS // 2):
        vp = vln[:, pair * LANES:(pair + 1) * LANES]
        mixed.append(_dot(masked_w_ref[2 * pair], jnp.where(low, vp, 0.0))
                     + _dot(masked_w_ref[2 * pair + 1], jnp.where(low, 0.0, vp))
                     + bias_ref[:, pair * LANES:(pair + 1) * LANES])
    return mixed


def _mixers_bwd(sinks, dmix, q, kvx, out, gates, ln_g, ln_b, sgu_w, bias_full, gwout):
    last = N_BLOCKS - 1

    def body(sink_ref, d_ref, q_ref, kc_ref, o_ref, za_ref, dsg_ref, us_ref, vs_ref, zs_ref, lng_ref, lnb_ref, w_ref,
             bias_ref, gwout_ref,
             dp_ref, gsink_ref, gbin_ref, dps_ref, gw_ref, gb_ref, gln_ref, gbins_ref, wout_shard_ref,
             kp_ref, pend_ref, carry_ref, wm_ref, wt_ref, gbias_ref, sa_w, ra_w, sb_w, rc_w, send_sems, recv_sems):
        n = pl.program_id(0)
        start, exchange, finish = _reduce_scatter_plan(_Copies(send_sems, recv_sems), 0, gwout_ref, WOUT_ROWS,
                                                       sa_w, ra_w, sb_w, rc_w, wout_shard_ref)
        tril = (lax.broadcasted_iota(jnp.int32, (BLOCK, BLOCK), 0)
                >= lax.broadcasted_iota(jnp.int32, (BLOCK, BLOCK), 1))

        @pl.when(n == 0)
        def _():
            gsink_ref[...] = jnp.zeros_like(gsink_ref)
            gbin_ref[...] = jnp.zeros_like(gbin_ref)
            carry_ref[...] = jnp.zeros_like(carry_ref)
            kp_ref[...] = jnp.zeros_like(kp_ref)
            gw_ref[...] = jnp.zeros_like(gw_ref)
            gln_ref[...] = jnp.zeros_like(gln_ref)
            gbins_ref[...] = jnp.zeros_like(gbins_ref)
            gbias_ref[...] = jnp.zeros_like(gbias_ref)
            _mask_sgu_weights(w_ref, wm_ref, wt_ref)
            start()

        pl.when(n == 3)(exchange)
        pl.when(n == 12)(finish)

        @pl.when(n > 0)
        def _():
            dp_ref[:, 0:ATTN_W] = pend_ref[:, 0:ATTN_W]
            dp_ref[:, GATE0:ATTN_SECTION] = pend_ref[:, ATTN_W:]

        @pl.when(n > last)
        def _():
            dp_ref[:, KV0:GATE0] = carry_ref[...].astype(MXU_DTYPE)

        @pl.when(n <= last)
        def _():
            us = us_ref[...]
            vs = vs_ref[...]
            lng = lng_ref[...]
            u, vhat, rstd, vln = _sgu_activations(us, vs, lng, lnb_ref[...])
            low_sgu = lax.broadcasted_iota(jnp.int32, (BLOCK, LANES), 1) < HALF
            sgu = {}

            def sgu_gates():
                mixed = _sgu_mix(vln, wm_ref, bias_ref)
                sgu["du"], sgu["dzs"], sgu["dm"] = [], [], []
                for pair in range(N_SGU_HEADS // 2):
                    cols = slice(pair * LANES, (pair + 1) * LANES)
                    dsg = dsg_ref[:, cols]
                    gate, gate_grad = _silu_and_grad(zs_ref[:, cols])
                    up = u[:, cols]
                    sgu["du"].append(dsg * mixed[pair] * gate)
                    sgu["dzs"].append(dsg * up * mixed[pair] * gate_grad)
                    dmixed = dsg * up * gate
                    gbias_ref[:, cols] += dmixed
                    sgu["dm"].append((jnp.where(low_sgu, dmixed, 0.0).astype(MXU_DTYPE),
                                      jnp.where(low_sgu, 0.0, dmixed).astype(MXU_DTYPE)))

            def sgu_grads():
                dvln_parts = []
                for pair in range(N_SGU_HEADS // 2):
                    dm_lo, dm_hi = sgu["dm"][pair]
                    vp = vln[:, pair * LANES:(pair + 1) * LANES]
                    gw_ref[2 * pair] += _dot(dm_lo, vp, NT)
                    gw_ref[2 * pair + 1] += _dot(dm_hi, vp, NT)
                    dvln_parts.append(_dot(wt_ref[2 * pair], dm_lo) + _dot(wt_ref[2 * pair + 1], dm_hi))
                dvln = jnp.concatenate(dvln_parts, axis=1)
                gln_ref[0:1, :] += jnp.sum(dvln * vhat, axis=0, keepdims=True)
                gln_ref[1:2, :] += jnp.sum(dvln, axis=0, keepdims=True)
                dvhat = dvln * lng
                dvg = rstd * (dvhat - jnp.mean(dvhat, axis=-1, keepdims=True)
                              - vhat * jnp.mean(dvhat * vhat, axis=-1, keepdims=True))
                dus = jnp.concatenate(sgu["du"], axis=1) * _gelu_grad(us)
                dvs = dvg * _gelu_grad(vs)
                dzs = jnp.concatenate(sgu["dzs"], axis=1)
                for k, val in enumerate((dus, dvs, dzs)):
                    dps_ref[:, k * SGU_W:(k + 1) * SGU_W] = val.astype(MXU_DTYPE)
                    gbins_ref[:, k * SGU_W:(k + 1) * SGU_W] += jnp.sum(val, axis=0, keepdims=True)

            valid = _window_mask(n)[0:BLOCK]
            low = lax.broadcasted_iota(jnp.int32, (BLOCK, LANES), 1) < HALF
            low_keys = lax.broadcasted_iota(jnp.int32, (2 * BLOCK, LANES), 1) < HALF
            lane_row = lax.broadcasted_iota(jnp.int32, (1, LANES), 1)
            gsink = jnp.zeros((1, LANES), F32)
            chains = [(g, par, i) for g in range(2) for par in range(2) for i in range(2)]
            kv = {(g, par): _kv_cat(kp_ref, kc_ref, 2 * g + par, False) for g in range(2) for par in range(2)}
            ones_keys = jnp.ones((2 * BLOCK, LANES), MXU_DTYPE)
            half_of_lane = lax.broadcasted_iota(jnp.int32, (LANES, 2 * LANES), 0) // HALF
            half_of_col = lax.broadcasted_iota(jnp.int32, (LANES, 2 * LANES), 1) // LANES
            sum_halves = (half_of_lane == half_of_col).astype(MXU_DTYPE)
            douts, deltas = [], []
            for pair in range(N_PAIRS):
                lanes = slice(pair * LANES, (pair + 1) * LANES)
                dg = d_ref[:, lanes]
                gate, gate_grad = _silu_and_grad(za_ref[:, lanes])
                o = o_ref[pair]
                dout = dg * gate
                dza = dg * o * gate_grad
                douts.append(dout.astype(MXU_DTYPE))
                deltas.append(_dot(dout * o, sum_halves))
                zl = slice(ATTN_W + pair * LANES, ATTN_W + (pair + 1) * LANES)
                pend_ref[:, zl] = dza.astype(MXU_DTYPE)
                gl = slice(GATE0 + pair * LANES, GATE0 + (pair + 1) * LANES)
                gbin_ref[:, gl] += jnp.sum(dza, axis=0, keepdims=True)

            first = {}

            def issue_first(k):
                g, par, i = chains[k]
                first[k] = (_dot(q_ref[2 * g + i], kv[g, par][0], NT), _dot(douts[2 * g + i], kv[g, par][1], NT))

            numerators = {}

            def issue_row_sums(k):
                g, par, i = chains[k]
                sink = sink_ref[4 * g + 2 * i + par]
                e, m = _softmax_numerator(jnp.where(valid, first[k][0], NEG_INF), sink)
                numerators[k] = (e, jnp.exp(sink - m), _dot(e, ones_keys))

            ahead = ATTN_BWD_AHEAD
            for k in range(ahead):
                issue_first(k)
            issue_row_sums(0)
            issue_row_sums(1)
            dqs, dk_parts, dv_parts = {}, {}, {}
            operands = {}

            def issue_last(k):
                g, par, i = chains[k]
                ds, ds_t, p_t = operands.pop(k)
                dq = _dot(ds, kv[g, par][0])
                dqs[g, i] = dq if par == 0 else dqs[g, i] + dq
                dk = _dot(ds_t, q_ref[2 * g + i])
                dv = _dot(p_t, douts[2 * g + i])
                dk_parts[g, par] = dk if i == 0 else dk_parts[g, par] + dk
                dv_parts[g, par] = dv if i == 0 else dv_parts[g, par] + dv

            for k, (g, par, i) in enumerate(chains):
                h = 4 * g + 2 * i + par
                delta = deltas[2 * g + i][:, par * LANES:(par + 1) * LANES]
                e, at_sink, row_sum = numerators[k]
                inv = 1.0 / (row_sum + at_sink)
                p = e * jnp.tile(inv, (1, 2))
                ds = p * (first[k][1] - jnp.tile(delta, (1, 2)))
                ds = ds.astype(MXU_DTYPE)
                operands[k] = (ds, ds.T, p.astype(MXU_DTYPE).T)
                total = jnp.sum(at_sink * inv * delta, axis=0, keepdims=True)
                gsink = jnp.where(lane_row == h, -total, gsink)
                if k + ahead < len(chains):
                    issue_first(k + ahead)
                if k + 2 < len(chains):
                    issue_row_sums(k + 2)
                if k > 0:
                    issue_last(k - 1)
                if k == SGU_GATES_AFTER_CHAIN:
                    sgu_gates()
                if k == SGU_GRADS_AFTER_CHAIN:
                    sgu_grads()
            issue_last(len(chains) - 1)
            for pair in range(N_PAIRS):
                g, i = divmod(pair, 2)
                dq = dqs[g, i] * SCALE
                lanes = slice(pair * LANES, (pair + 1) * LANES)
                pend_ref[:, lanes] = dq.astype(MXU_DTYPE)
                gbin_ref[:, lanes] += jnp.sum(dq, axis=0, keepdims=True)
            gsink_ref[...] += gsink
            for k, parts in enumerate((dk_parts, dv_parts)):
                masked = {key: jnp.where(low_keys if key[1] == 0 else jnp.logical_not(low_keys), val, 0.0)
                          for key, val in parts.items()}
                both = (masked[0, 0] + masked[1, 1]
                        + pltpu.roll(masked[0, 1] + masked[1, 0], HALF, 1))
                lanes = slice(k * KV_W, (k + 1) * KV_W)
                done = carry_ref[:, lanes] + both[0:BLOCK]
                dp_ref[:, KV0 + k * KV_W:KV0 + (k + 1) * KV_W] = done.astype(MXU_DTYPE)
                carry_ref[:, lanes] = both[BLOCK:]
                gbin_ref[:, KV0 + k * KV_W:KV0 + (k + 1) * KV_W] += jnp.sum(both, axis=0, keepdims=True)
            kp_ref[...] = kc_ref[...]

        @pl.when(n == last)
        def _():
            for hh in range(N_SGU_HEADS):
                gw_ref[hh] = jnp.where(tril, gw_ref[hh], 0.0)
            head_of_lane = lax.broadcasted_iota(jnp.int32, (N_SGU_HEADS, SGU_W), 1) // HEAD_DIM
            select = (head_of_lane == lax.broadcasted_iota(jnp.int32, (N_SGU_HEADS, SGU_W), 0)).astype(F32)
            gb_ref[...] = lax.dot_general(select, gbias_ref[...], NT, precision=lax.Precision.HIGHEST,
                                          preferred_element_type=F32)

    at = lambda n: jnp.minimum(n, last)
    blk = lambda w: pl.BlockSpec((BLOCK, w), lambda n: (at(n), 0))
    tiles = pl.BlockSpec((N_PAIRS, BLOCK, LANES), lambda n: (0, at(n), 0))
    section = lambda k: pl.BlockSpec((None, BLOCK, SGU_W), lambda n: (k, at(n), 0))
    return pl.pallas_call(
        body,
        name="mixers_bwd",
        grid=(N_BLOCKS + 1,),
        in_specs=[pl.BlockSpec(memory_space=pltpu.SMEM),
                  section(0),
                  tiles,
                  blk(KVX_W),
                  tiles,
                  section(0),
                  section(1),
                  section(1), section(2), section(3),
                  _full((1, SGU_W)), _full((1, SGU_W)), _full((N_SGU_HEADS, BLOCK, BLOCK)), _full((BLOCK, SGU_W)),
                  VMEM_SPEC],
        out_specs=(pl.BlockSpec((BLOCK, ATTN_SECTION), lambda n: (jnp.maximum(n - 1, 0), 0)),
                   _full((1, LANES)), _full((1, ATTN_SECTION)),
                   pl.BlockSpec((BLOCK, SGU_SECTION), lambda n: (at(n), 0)),
                   _full((N_SGU_HEADS, BLOCK, BLOCK)), _full((N_SGU_HEADS, BLOCK)),
                   _full((8, SGU_W)), _full((1, SGU_SECTION)), VMEM_SPEC),
        out_shape=(jax.ShapeDtypeStruct((SEQ, ATTN_SECTION), MXU_DTYPE),
                   jax.ShapeDtypeStruct((1, LANES), F32),
                   jax.ShapeDtypeStruct((1, ATTN_SECTION), F32),
                   jax.ShapeDtypeStruct((SEQ, SGU_SECTION), MXU_DTYPE),
                   jax.ShapeDtypeStruct((N_SGU_HEADS, BLOCK, BLOCK), F32),
                   jax.ShapeDtypeStruct((N_SGU_HEADS, BLOCK), F32),
                   jax.ShapeDtypeStruct((8, SGU_W), F32),
                   jax.ShapeDtypeStruct((1, SGU_SECTION), F32),
                   jax.ShapeDtypeStruct((WOUT_ROWS, D_MODEL), F32)),
        scratch_shapes=([pltpu.VMEM((BLOCK, KVX_W), MXU_DTYPE),
                         pltpu.VMEM((BLOCK, 2 * ATTN_W), MXU_DTYPE), pltpu.VMEM((BLOCK, 2 * KV_W), F32),
                         pltpu.VMEM((N_SGU_HEADS, BLOCK, BLOCK), MXU_DTYPE),
                         pltpu.VMEM((N_SGU_HEADS, BLOCK, BLOCK), MXU_DTYPE), pltpu.VMEM((BLOCK, SGU_W), F32)]
                        + _reduce_scatter_scratch(WOUT_ROWS, D_MODEL, COMM_DTYPE) + _dma_sems(REDUCE_SEMS)),
        compiler_params=_params(("arbitrary",), VMEM_LIMIT),
    )(sinks, dmix, q, kvx, out, gates, dmix, gates, gates, gates, ln_g, ln_b, sgu_w, bias_full, gwout)


WIN_GRAD_ROWS = 256


def _win_grad_segments():
    segments = []
    for chunk in range(IN_W // WIN_GRAD_ROWS):
        for owner in range(N_DEV):
            lo = max(chunk * WIN_GRAD_ROWS, owner * WIN_ROWS)
            hi = min((chunk + 1) * WIN_GRAD_ROWS, (owner + 1) * WIN_ROWS)
            if lo < hi:
                segments.append((len(segments), chunk, owner, lo, hi - lo))
    return segments


def _win_grad_orders():
    n_chips = N_DEV // 2
    chunks = range(IN_W // WIN_GRAD_ROWS)
    of_chip = [[k for k in chunks if (k * WIN_GRAD_ROWS + WIN_GRAD_ROWS // 2) // (2 * WIN_ROWS) == chip]
               for chip in range(n_chips)]
    orders = []
    for chip in range(n_chips):
        order = [of_chip[chip ^ t][r] for r in range(max(map(len, of_chip))) for t in range(1, n_chips)
                 if r < len(of_chip[chip ^ t])]
        orders.append(order + of_chip[chip])
    return orders


def _win_grad_in_proj_bwd(dpa, dps, h, gsguw, win_t, x, norm_g, gres, vec_parts):
    rows = WIN_GRAD_ROWS
    tm = TOKEN_TILE
    n_attn = ATTN_SECTION // rows
    steps = IN_W // rows
    all_steps = steps + SEQ // tm
    segments = _win_grad_segments()
    n_seg = len(segments)
    per_owner = max(sum(1 for seg in segments if seg[2] == p) for p in range(N_DEV))
    n_parts = len(vec_parts)
    class_rows = (N_DEV // 2) * WIN_ROWS

    orders = _win_grad_orders()
    chip_x, chip_y, _ = _place()
    order = jnp.asarray(orders, jnp.int32)[2 * chip_x + chip_y]

    def body(order_ref, da_ref, ds_ref, h_hbm, gsguw_ref, da_tile_ref, ds_tile_ref, w_hbm, x_ref, g_ref, gres_ref,
             *rest):
        part_refs = rest[:n_parts]
        (gx_ref, shard_ref, sguw_full_ref, vec_out_ref,
         h_ref, w_ref, load_sems, gng_ref,
         chunks, sa, ra, sb, rc, sa_s, ra_s, sb_s, rc_s, landing, vec_ref, ra_vec, slots,
         send_sems, recv_sems, send1, recv1, send2, recv2) = rest[n_parts:]
        step = pl.program_id(0)
        first_phase = step < steps
        chunk = order_ref[jnp.minimum(step, steps - 1)]
        last_chunk = order_ref[jnp.clip(step - 1, 0, steps - 1)]
        load_h = pltpu.make_async_copy(h_hbm, h_ref, load_sems.at[0])
        load_w = pltpu.make_async_copy(w_hbm, w_ref, load_sems.at[1])

        @pl.when(step == 0)
        def _():
            load_h.start()
            load_w.start()
            gng_ref[...] = jnp.zeros_like(gng_ref)
            load_h.wait()
        x, y, c = _place()
        copies = _Copies(send_sems, recv_sems)
        own_sguw = landing.at[_block_rows((x, y, c), SGUW_ROWS), :]
        start_s, exchange_s, finish_s = _reduce_scatter_plan(copies, 0, gsguw_ref, SGUW_ROWS,
                                                             sa_s, ra_s, sb_s, rc_s, own_sguw)
        gather = _gather_plan(copies, REDUCE_SEMS, landing, SGUW_ROWS)

        def place_of(owner):
            return owner // 4, (owner // 2) % 2, owner % 2

        def class_rows_of(owner, first, n):
            return pl.ds((owner // 2) * WIN_ROWS + first - owner * WIN_ROWS, n)

        def to_sibling(seg):
            sid, _, owner, first, n = seg
            at = class_rows_of(owner, first, n)
            return pltpu.make_async_remote_copy(src_ref=sa.at[at, :], dst_ref=ra.at[at, :], send_sem=send1.at[sid],
                                                recv_sem=recv1.at[sid], device_id=(x, y, 1 - c), device_id_type=MESH)

        def to_owner(seg):
            sid, _, owner, first, n = seg
            px, py, pc = place_of(owner)
            slot = (x + px - 2 * x * px) + 2 * (y + py - 2 * y * py) - 1
            nth = sum(1 for other in segments if other[2] == owner and other[0] < sid)
            dst = rc.at[pl.ds(pl.multiple_of(slot * WIN_ROWS, 16) + first - owner * WIN_ROWS, n), :]
            return pltpu.make_async_remote_copy(src_ref=sb.at[class_rows_of(owner, first, n), :], dst_ref=dst,
                                                send_sem=send2.at[sid], recv_sem=recv2.at[slot * per_owner + nth],
                                                device_id=(px, py, pc), device_id_type=MESH)

        def give(seg, at_step):
            sid, _, owner, first, n = seg

            @pl.when(c != owner % 2)
            def _():
                local = pl.ds(first % rows, n)
                sa[class_rows_of(owner, first, n), :] = chunks[at_step % 2, local, :].astype(sa.dtype)
                to_sibling(seg).start()

        def keep(seg, at_step):
            sid, _, owner, first, n = seg
            px, py, pc = place_of(owner)

            @pl.when(c == pc)
            def _():
                to_sibling(seg).wait_recv()
                local = pl.ds(first % rows, n)
                total = chunks[at_step % 2, local, :] + ra[class_rows_of(owner, first, n), :].astype(F32)
                mine = jnp.logical_and(x == px, y == py)

                @pl.when(mine)
                def _():
                    shard_ref[pl.ds(first - owner * WIN_ROWS, n), :] = total

                @pl.when(jnp.logical_not(mine))
                def _():
                    sb[class_rows_of(owner, first, n), :] = total.astype(sb.dtype)
                    to_owner(seg).start()

        pl.when(step == 0)(start_s)
        pl.when(step == 2)(exchange_s)

        @pl.when(step == 5)
        def _():
            finish_s()
            gather[0]()

        pl.when(step == 7)(gather[1])

        @pl.when(jnp.logical_and(first_phase, chunk < n_attn))
        def _():
            chunks[step % 2] = _dot(da_ref[...], h_ref[...], TN)

        @pl.when(jnp.logical_and(first_phase, chunk >= n_attn))
        def _():
            chunks[step % 2] = _dot(ds_ref[...], h_ref[...], TN)

        for k in range(steps):
            @pl.when(jnp.logical_and(jnp.logical_and(step > 0, step <= steps), last_chunk == k))
            def _():
                for seg in segments:
                    if seg[1] == k:
                        keep(seg, step - 1)

            @pl.when(jnp.logical_and(first_phase, chunk == k))
            def _():
                for seg in segments:
                    if seg[1] == k:
                        give(seg, step)

        pl.when(step == steps)(load_w.wait)

        @pl.when(step >= steps)
        def _():
            dh = (_dot(da_tile_ref[...], w_ref[0:ATTN_SECTION, :])
                  + _dot(ds_tile_ref[...], w_ref[ATTN_SECTION:, :]))
            xv = x_ref[...]
            r = lax.rsqrt(jnp.mean(xv * xv, axis=-1, keepdims=True) + NORM_EPS)
            xn = xv * r
            gng_ref[...] += jnp.sum(dh * xn, axis=0, keepdims=True)
            dxn = dh * g_ref[...]
            gx_ref[...] = r * (dxn - xn * jnp.mean(dxn * xn, axis=-1, keepdims=True)) + gres_ref[...]

        @pl.when(step == all_steps - 1)
        def _():
            for owner in range(N_DEV):
                px, py, pc = place_of(owner)

                @pl.when(jnp.logical_and(jnp.logical_and(x == px, y == py), c == pc))
                def _():
                    mine = [seg for seg in segments if seg[2] == owner]
                    for slot in range(3):
                        for nth, (sid, _, _, first, n) in enumerate(mine):
                            landed = rc.at[pl.ds(slot * WIN_ROWS + first - owner * WIN_ROWS, n), :]
                            pltpu.make_async_remote_copy(
                                src_ref=landed, dst_ref=landed, send_sem=send2.at[sid],
                                recv_sem=recv2.at[slot * per_owner + nth], device_id=(x, y, c),
                                device_id_type=MESH).wait_recv()
                    acc = shard_ref[...]
                    for slot in range(3):
                        acc = acc + rc[slot * WIN_ROWS:(slot + 1) * WIN_ROWS, :].astype(F32)
                    shard_ref[...] = acc
            for seg in segments:
                owner = seg[2]
                px, py, pc = place_of(owner)

                @pl.when(c != pc)
                def _():
                    to_sibling(seg).wait_send()

                @pl.when(jnp.logical_and(c == pc, jnp.logical_not(jnp.logical_and(x == px, y == py))))
                def _():
                    to_owner(seg).wait_send()
            gather[2]()
            sguw_full_ref[...] = landing[...]
            _all_reduce_vectors(copies, REDUCE_SEMS + GATHER_SEMS, gng_ref, *part_refs, vec_out_ref, vec_ref, ra_vec,
                                slots)

    chunk_at = lambda i, order_ref: order_ref[jnp.minimum(i, steps - 1)]
    tile = lambda w: pl.BlockSpec((tm, w), lambda i, order_ref: (jnp.maximum(i - steps, 0), 0))
    grid_spec = pltpu.PrefetchScalarGridSpec(
        num_scalar_prefetch=1,
        grid=(all_steps,),
        in_specs=[pl.BlockSpec((SEQ, rows), lambda i, order_ref: (0, jnp.minimum(chunk_at(i, order_ref), n_attn - 1))),
                  pl.BlockSpec((SEQ, rows), lambda i, order_ref: (0, jnp.maximum(chunk_at(i, order_ref) - n_attn, 0))),
                  pl.BlockSpec(memory_space=pl.ANY), VMEM_SPEC,
                  tile(ATTN_SECTION), tile(SGU_SECTION), pl.BlockSpec(memory_space=pl.ANY), tile(D_MODEL),
                  _full((1, D_MODEL)), tile(D_MODEL)] + [VMEM_SPEC] * n_parts,
        out_specs=(tile(D_MODEL), VMEM_SPEC, _full((N_SGU_HEADS * BLOCK, BLOCK)), VMEM_SPEC),
        scratch_shapes=([pltpu.VMEM((SEQ, D_MODEL), h.dtype), pltpu.VMEM((IN_W, D_MODEL), win_t.dtype),
                         pltpu.SemaphoreType.DMA((2,)), pltpu.VMEM((1, D_MODEL), F32),
                         pltpu.VMEM((2, rows, D_MODEL), F32),
                         pltpu.VMEM((class_rows, D_MODEL), COMM_DTYPE), pltpu.VMEM((class_rows, D_MODEL), COMM_DTYPE),
                         pltpu.VMEM((class_rows, D_MODEL), COMM_DTYPE), pltpu.VMEM((3 * WIN_ROWS, D_MODEL), COMM_DTYPE)]
                        + _reduce_scatter_scratch(SGUW_ROWS, BLOCK, F32)
                        + [pltpu.VMEM((N_SGU_HEADS * BLOCK, BLOCK), F32)]
                        + _vector_scratch() + _dma_sems(REDUCE_SEMS + GATHER_SEMS + VECTOR_SEMS)
                        + _dma_sems(n_seg) + [pltpu.SemaphoreType.DMA((n_seg,)),
                                              pltpu.SemaphoreType.DMA((3 * per_owner,))]))
    return pl.pallas_call(
        body,
        name="win_grad_in_proj_bwd",
        grid_spec=grid_spec,
        out_shape=(jax.ShapeDtypeStruct((SEQ, D_MODEL), F32),
                   jax.ShapeDtypeStruct((WIN_ROWS, D_MODEL), F32),
                   jax.ShapeDtypeStruct((N_SGU_HEADS * BLOCK, BLOCK), F32),
                   jax.ShapeDtypeStruct((VEC_ROWS, IN_W), F32)),
        compiler_params=_params(("arbitrary",), VMEM_LIMIT),
    )(order, dpa, dps, h, gsguw, dpa, dps, win_t, x, norm_g, gres, *vec_parts)


VEC_NORM_G, VEC_B_IN, VEC_SINKS, VEC_LN_G, VEC_LN_B, VEC_B_OUT, VEC_FINAL_G, VEC_LOSS, VEC_SGU_B = 0, 1, 2, 3, 4, 5, 6, 7, 8


def _adamw(w, g, m, v):
    m = ADAM_B1 * m + (1.0 - ADAM_B1) * g
    v = ADAM_B2 * v + (1.0 - ADAM_B2) * (g * g)
    m_hat = m / (1.0 - ADAM_B1 ** ADAM_STEP)
    v_hat = v / (1.0 - ADAM_B2 ** ADAM_STEP)
    delta = -ADAM_LR * (m_hat / (jnp.sqrt(v_hat) + ADAM_EPS) + ADAM_WD * w)
    return delta, m, v


def _adamw_shard(name, g, w, m, v, block_rows):
    def body(g_ref, w_ref, m_ref, v_ref, d_ref, nm_ref, nv_ref):
        d_ref[...], nm_ref[...], nv_ref[...] = _adamw(w_ref[...], g_ref[...], m_ref[...], v_ref[...])

    rows, cols = w.shape
    spec = pl.BlockSpec((block_rows, cols), lambda i: (i, 0))
    return pl.pallas_call(
        body,
        name=name,
        grid=(rows // block_rows,),
        in_specs=[spec] * 4,
        out_specs=(spec,) * 3,
        out_shape=(jax.ShapeDtypeStruct(w.shape, F32),) * 3,
        compiler_params=_params(("arbitrary",)),
    )(g, w, m, v)


VECTOR_SEMS = 4


def _vector_scratch():
    return [pltpu.VMEM((VEC_ROWS, IN_W), F32), pltpu.VMEM((VEC_ROWS, IN_W), F32),
            pltpu.VMEM((4 * VEC_ROWS, IN_W), F32)]


def _all_reduce_vectors(copies, sem0, gng_ref, gba_ref, gbs_ref, gsink_ref, gln_ref, gsgub_ref, vec4_ref, out_ref,
                        vec_ref, ra_vec, slots):
    x, y, c = _place()
    vec_ref[...] = jnp.zeros_like(vec_ref)
    vec_ref[VEC_NORM_G:VEC_NORM_G + 1, 0:D_MODEL] = gng_ref[...]
    vec_ref[VEC_B_IN:VEC_B_IN + 1, 0:ATTN_SECTION] = gba_ref[...]
    vec_ref[VEC_B_IN:VEC_B_IN + 1, ATTN_SECTION:IN_W] = gbs_ref[...]
    vec_ref[VEC_SINKS:VEC_SINKS + 1, 0:LANES] = gsink_ref[...]
    vec_ref[VEC_LN_G:VEC_LN_G + 1, 0:SGU_W] = gln_ref[0:1, :]
    vec_ref[VEC_LN_B:VEC_LN_B + 1, 0:SGU_W] = gln_ref[1:2, :]
    vec_ref[VEC_B_OUT:VEC_B_OUT + 1, 0:D_MODEL] = vec4_ref[2:3, :]
    vec_ref[VEC_FINAL_G:VEC_FINAL_G + 1, 0:D_MODEL] = vec4_ref[1:2, :]
    vec_ref[VEC_LOSS:VEC_LOSS + 1, 0:D_MODEL] = vec4_ref[0:1, :]
    vec_ref[VEC_SGU_B:VEC_SGU_B + N_SGU_HEADS, 0:BLOCK] = gsgub_ref[...]

    to_sibling = copies(sem0, vec_ref, ra_vec, (x, y, 1 - c))
    to_sibling.start()
    to_sibling.wait_recv()

    def chip_slot(place):
        return slots.at[pl.ds(pl.multiple_of((2 * place[0] + place[1]) * VEC_ROWS, 8), VEC_ROWS), :]

    mine = chip_slot((x, y))
    mine[...] = vec_ref[...] + ra_vec[...]
    to_chips = [copies(sem0 + i, mine, mine, (*_chip(rel), c)) for i, rel in enumerate(RELATIONS[1:], start=1)]
    for cp in to_chips:
        cp.start()
    for i, rel in enumerate(RELATIONS[1:], start=1):
        theirs = chip_slot(_chip(rel))
        copies(sem0 + i, theirs, theirs, (x, y, c)).wait_recv()
    out_ref[...] = ((slots[0:VEC_ROWS, :] + slots[VEC_ROWS:2 * VEC_ROWS, :])
                    + slots[2 * VEC_ROWS:3 * VEC_ROWS, :]) + slots[3 * VEC_ROWS:, :]
    to_sibling.wait_send()
    for cp in to_chips:
        cp.wait_send()


def _adamw_replicated(vec, gsguw, weights, m_state, v_state):
    n = len(SMALL)

    def body(*refs):
        vec_ref, gsguw_ref = refs[0], refs[1]
        w_refs, m_refs, v_refs = (refs[2 + k * n:2 + (k + 1) * n] for k in range(3))
        outs = refs[2 + 3 * n:]
        g_refs, d_refs, nm_refs, nv_refs = (outs[k * n:(k + 1) * n] for k in range(4))
        for i, (_, row, shape) in enumerate(SMALL):
            g = gsguw_ref[...] if row is None else vec_ref[row:row + shape[0], 0:shape[1]]
            g_refs[i][...] = g
            d_refs[i][...], nm_refs[i][...], nv_refs[i][...] = _adamw(
                w_refs[i][...], g, m_refs[i][...], v_refs[i][...])

    shapes = tuple(jax.ShapeDtypeStruct(shape, F32) for _, _, shape in SMALL)
    outs = pl.pallas_call(
        body,
        name="adamw_replicated",
        in_specs=[VMEM_SPEC] * (2 + 3 * n),
        out_specs=(VMEM_SPEC,) * (4 * n),
        out_shape=shapes * 4,
    )(vec, gsguw, *weights, *m_state, *v_state)
    return tuple(outs[k * n:(k + 1) * n] for k in range(4))


SMALL = (
    ("norm_g", VEC_NORM_G, (1, D_MODEL)),
    ("b_in", VEC_B_IN, (1, IN_W)),
    ("attn_sinks", VEC_SINKS, (1, N_Q_HEADS)),
    ("sgu_ln_g", VEC_LN_G, (1, SGU_W)),
    ("sgu_ln_b", VEC_LN_B, (1, SGU_W)),
    ("sgu_w", None, (N_SGU_HEADS * BLOCK, BLOCK)),
    ("sgu_b", VEC_SGU_B, (N_SGU_HEADS, BLOCK)),
    ("b_out", VEC_B_OUT, (1, D_MODEL)),
    ("final_norm_g", VEC_FINAL_G, (1, D_MODEL)),
)


def _local_grads(x, target, h, win_t, wout_shard, norm_g, b_in, attn_sinks, sgu_ln_g, sgu_ln_b, sgu_w, sgu_b, b_out,
                 final_g):
    sinks = attn_sinks.reshape(N_Q_HEADS)
    bias_full = jnp.repeat(sgu_b.T, HEAD_DIM, axis=1)
    q, kvx, gates, wout = _in_proj(h, b_in, win_t, wout_shard)
    out, gres, dmix, gwout, vec4 = _mixers_out_proj(sinks, q, kvx, gates, sgu_ln_g, sgu_ln_b, sgu_w, bias_full,
                                                    x, target, wout, b_out, final_g)
    dpa, gsink, gbin_a, dps, gsguw, gsgub, gln, gbin_s, gwout_shard = _mixers_bwd(
        sinks, dmix, q, kvx, out, gates, sgu_ln_g, sgu_ln_b, sgu_w, bias_full, gwout)
    grad_x, gwin_shard, gsguw_sum, vec = _win_grad_in_proj_bwd(
        dpa, dps, h, gsguw.reshape(N_SGU_HEADS * BLOCK, BLOCK), win_t, x, norm_g, gres,
        (gbin_a, gbin_s, gsink, gln, gsgub, vec4))
    return grad_x, gwin_shard, gwout_shard, gsguw_sum, vec


def kernel(x, norm_g, w_in, b_in, attn_sinks, sgu_ln_g, sgu_ln_b, sgu_w, sgu_b, w_out, b_out, final_norm_g, loss_target, m_norm_g, m_w_in, m_b_in, m_attn_sinks, m_sgu_ln_g, m_sgu_ln_b, m_sgu_w, m_sgu_b, m_w_out, m_b_out, m_final_norm_g, v_norm_g, v_w_in, v_b_in, v_attn_sinks, v_sgu_ln_g, v_sgu_ln_b, v_sgu_w, v_sgu_b, v_w_out, v_b_out, v_final_norm_g):
    given = dict(norm_g=norm_g, b_in=b_in, attn_sinks=attn_sinks, sgu_ln_g=sgu_ln_g, sgu_ln_b=sgu_ln_b,
                 sgu_w=sgu_w, sgu_b=sgu_b, b_out=b_out, final_norm_g=final_norm_g)
    m_given = dict(norm_g=m_norm_g, b_in=m_b_in, attn_sinks=m_attn_sinks, sgu_ln_g=m_sgu_ln_g,
                   sgu_ln_b=m_sgu_ln_b, sgu_w=m_sgu_w, sgu_b=m_sgu_b, b_out=m_b_out, final_norm_g=m_final_norm_g)
    v_given = dict(norm_g=v_norm_g, b_in=v_b_in, attn_sinks=v_attn_sinks, sgu_ln_g=v_sgu_ln_g,
                   sgu_ln_b=v_sgu_ln_b, sgu_w=v_sgu_w, sgu_b=v_sgu_b, b_out=v_b_out, final_norm_g=v_final_norm_g)

    win_t, h = _all_gather_win(w_in[0].T, x[0], norm_g)
    grad_x, gwin_t, gwout, gsguw, vec = _local_grads(
        x[0], loss_target[0], h, win_t, w_out[0], norm_g, b_in, attn_sinks, sgu_ln_g, sgu_ln_b, sgu_w[0], sgu_b[0],
        b_out, final_norm_g.reshape(1, D_MODEL))

    t = lambda a: a[0].T
    d_win, nm_win, nv_win = _adamw_shard("adamw_w_in", gwin_t, t(w_in), t(m_w_in), t(v_w_in), WIN_ROWS // 2)
    d_wout, nm_wout, nv_wout = _adamw_shard("adamw_w_out", gwout, w_out[0], m_w_out[0], v_w_out[0], WOUT_ROWS)
    as_2d = lambda d: [d[name].reshape(shape) for name, _, shape in SMALL]
    loss = vec[VEC_LOSS, 0]
    small = _adamw_replicated(vec, gsguw, as_2d(given), as_2d(m_given), as_2d(v_given))

    def assemble(big_in, big_out, k):
        vals = {name: small[k][i].reshape(given[name].shape) for i, (name, _, _) in enumerate(SMALL)}
        vals["w_in"] = big_in.T[None]
        vals["w_out"] = big_out[None]
        order = ("norm_g", "w_in", "b_in", "attn_sinks", "sgu_ln_g", "sgu_ln_b", "sgu_w", "sgu_b", "w_out",
                 "b_out", "final_norm_g")
        return [vals[name] for name in order]

    return (loss, grad_x[None],
            *assemble(gwin_t, gwout, 0), *assemble(d_win, d_wout, 1),
            *assemble(nm_win, nm_wout, 2), *assemble(nv_win, nv_wout, 3))
```

```python
import functools
import math

import jax
import jax.numpy as jnp
from jax import lax
from jax.experimental import pallas as pl
from jax.experimental.pallas import tpu as pltpu

F32 = jnp.float32
BF16 = jnp.bfloat16
MXU_DTYPE = BF16
COMM_DTYPE = BF16

D_MODEL = 1024
SEQ = 4096
HEAD_DIM = 64
N_Q_HEADS = 8
Q_PER_KV = 4
BLOCK = 128
N_BLOCKS = SEQ // BLOCK
ATTN_W = 512
KV_W = 128
SGU_W = 512
N_SGU_HEADS = 8
IN_W = 2816
NORM_EPS = 1e-5
NEG_INF = -1e30
SCALE = HEAD_DIM ** -0.5
KV0 = ATTN_W
GATE0 = ATTN_W + 2 * KV_W
SGU0 = GATE0 + ATTN_W
ATTN_SECTION = SGU0
SGU_SECTION = IN_W - SGU0

ADAM_LR = 0.001
ADAM_B1 = 0.9
ADAM_B2 = 0.999
ADAM_EPS = 1e-08
ADAM_WD = 0.01
ADAM_STEP = 10

N_DEV = 8
WIN_ROWS = IN_W // N_DEV
WOUT_ROWS = D_MODEL // N_DEV
SGUW_ROWS = N_SGU_HEADS * BLOCK // N_DEV
VEC_ROWS = 16
MESH = pl.DeviceIdType.MESH

LANES = 128
HALF = LANES // 2
N_PAIRS = N_Q_HEADS * HEAD_DIM // LANES
KVX_W = 12 * LANES
TOKEN_TILE = 256
FWD_TOKEN_TILE = 512
ATTN_FWD_AHEAD = 4
FUSED_BLOCKS = 2
SGU_MIX_AFTER_CHAIN = 0
SGU_GATES_AFTER_CHAIN = 1
SGU_GRADS_AFTER_CHAIN = 5
ATTN_BWD_AHEAD = 3
VMEM_LIMIT = 56 * 1024 * 1024

NN = (((1,), (0,)), ((), ()))
NT = (((1,), (1,)), ((), ()))
TN = (((0,), (0,)), ((), ()))


def _dot(a, b, dims=NN):
    return lax.dot_general(a.astype(MXU_DTYPE), b.astype(MXU_DTYPE), dims, preferred_element_type=F32)


def _gelu(x):
    return x * (lax.erf(x * (1.0 / math.sqrt(2.0))) + 1.0) * 0.5


def _gelu_grad(x):
    cdf = (lax.erf(x * (1.0 / math.sqrt(2.0))) + 1.0) * 0.5
    return cdf + x * jnp.exp(-0.5 * x * x) * (1.0 / math.sqrt(2.0 * math.pi))


def _silu_and_grad(z):
    s = jax.nn.sigmoid(z)
    return z * s, s * (1.0 + z * (1.0 - s))


def _params(semantics=None, vmem=None):
    kw = {}
    if semantics is not None:
        kw["dimension_semantics"] = semantics
    if vmem is not None:
        kw["vmem_limit_bytes"] = vmem
    return pltpu.CompilerParams(**kw)


def _full(shape):
    return pl.BlockSpec(shape, lambda *_: (0,) * len(shape))


VMEM_SPEC = pl.BlockSpec(memory_space=pltpu.VMEM)


RELATIONS = ((0, 0), (1, 0), (0, 1), (1, 1))


def _place():
    return lax.axis_index("x"), lax.axis_index("y"), lax.axis_index("c")


def _chip(rel):
    x, y, _ = _place()
    return (1 - x if rel[0] else x, 1 - y if rel[1] else y)


def _block_rows(place, n_rows):
    px, py, pc = place
    return pl.ds(pl.multiple_of((4 * px + 2 * py + pc) * n_rows, 16), n_rows)


class _Copies:
    def __init__(self, send_sems, recv_sems):
        self.send_sems, self.recv_sems = send_sems, recv_sems

    def __call__(self, k, src, dst, to):
        return pltpu.make_async_remote_copy(src_ref=src, dst_ref=dst, send_sem=self.send_sems.at[k],
                                            recv_sem=self.recv_sems.at[k], device_id=to, device_id_type=MESH)


def _gather_plan(copies, sem0, full_ref, n_rows):
    x, y, c = _place()
    me, sibling = (x, y, c), (x, y, 1 - c)
    chips = [_chip(rel) for rel in RELATIONS[1:]]

    def cp(k, block, to):
        rows = full_ref.at[_block_rows(block, n_rows), :]
        return copies(sem0 + k, rows, rows, to)

    first = [cp(0, me, sibling)] + [cp(1 + j, me, (*chip, c)) for j, chip in enumerate(chips)]
    passed = [cp(4 + j, (*chip, c), sibling) for j, chip in enumerate(chips)]

    def start():
        for f in first:
            f.start()

    def forward():
        for j, chip in enumerate(chips):
            cp(1 + j, (*chip, c), me).wait_recv()
            passed[j].start()

    def finish():
        cp(0, sibling, me).wait_recv()
        for j, chip in enumerate(chips):
            cp(4 + j, (*chip, 1 - c), me).wait_recv()
        for f in first + passed:
            f.wait_send()

    return start, forward, finish


GATHER_SEMS = 7


def _reduce_scatter_plan(copies, sem0, part_ref, n_rows, sa, ra, sb, rc, res_ref):
    x, y, c = _place()
    sibling = (x, y, 1 - c)
    n = n_rows
    level1 = copies(sem0, sa, ra, sibling)

    def level2(i):
        slot = pl.ds((i - 1) * n, n)
        return copies(sem0 + i, sb.at[slot, :], rc.at[slot, :], (*_chip(RELATIONS[i]), c))

    def start():
        for i, rel in enumerate(RELATIONS):
            sa[i * n:(i + 1) * n, :] = part_ref[_block_rows((*_chip(rel), 1 - c), n), :].astype(sa.dtype)
        level1.start()

    def exchange():
        level1.wait_recv()
        for i, rel in enumerate(RELATIONS):
            total = part_ref[_block_rows((*_chip(rel), c), n), :] + ra[i * n:(i + 1) * n, :].astype(F32)
            if i == 0:
                res_ref[...] = total
            else:
                sb[(i - 1) * n:i * n, :] = total.astype(sb.dtype)
                level2(i).start()

    def finish():
        acc = res_ref[...]
        for i in range(1, len(RELATIONS)):
            level2(i).wait_recv()
            acc = acc + rc[(i - 1) * n:i * n, :].astype(F32)
        res_ref[...] = acc
        level1.wait_send()
        for i in range(1, len(RELATIONS)):
            level2(i).wait_send()

    return start, exchange, finish


REDUCE_SEMS = 4


def _reduce_scatter_scratch(n_rows, width, dtype):
    return [pltpu.VMEM((4 * n_rows, width), dtype), pltpu.VMEM((4 * n_rows, width), dtype),
            pltpu.VMEM((3 * n_rows, width), dtype), pltpu.VMEM((3 * n_rows, width), dtype)]


def _dma_sems(n):
    return [pltpu.SemaphoreType.DMA((n,)), pltpu.SemaphoreType.DMA((n,))]


def _all_gather_win(win_t_shard, x, norm_g):
    tm = FWD_TOKEN_TILE
    steps = SEQ // tm

    def body(win_ref, x_ref, g_ref, full_ref, h_ref, landing, send_sems, recv_sems):
        step = pl.program_id(0)
        start, forward, finish = _gather_plan(_Copies(send_sems, recv_sems), 0, landing, WIN_ROWS)

        @pl.when(step == 0)
        def _():
            landing[_block_rows(_place(), WIN_ROWS), :] = win_ref[...].astype(COMM_DTYPE)
            start()

        xv = x_ref[...]
        r = lax.rsqrt(jnp.mean(xv * xv, axis=-1, keepdims=True) + NORM_EPS)
        h_ref[...] = ((xv * r) * g_ref[...]).astype(MXU_DTYPE)

        @pl.when(step == steps - 1)
        def _():
            forward()
            finish()
            full_ref[...] = landing[...]

    return pl.pallas_call(
        body,
        name="all_gather_win",
        grid=(steps,),
        in_specs=[VMEM_SPEC, pl.BlockSpec((tm, D_MODEL), lambda i: (i, 0)), _full((1, D_MODEL))],
        out_specs=(_full((IN_W, D_MODEL)), pl.BlockSpec((tm, D_MODEL), lambda i: (i, 0))),
        out_shape=(jax.ShapeDtypeStruct((IN_W, D_MODEL), COMM_DTYPE),
                   jax.ShapeDtypeStruct((SEQ, D_MODEL), MXU_DTYPE)),
        scratch_shapes=[pltpu.VMEM((IN_W, D_MODEL), COMM_DTYPE)] + _dma_sems(GATHER_SEMS),
        compiler_params=_params(("arbitrary",), VMEM_LIMIT),
    )(win_t_shard, x, norm_g)


def _in_proj(h, b_in, win_t, wout_shard):
    tm = FWD_TOKEN_TILE
    steps = SEQ // tm

    def body(h_ref, b_ref, w_ref, wout_ref, q_ref, kvx_ref, gate_ref, wfull_ref, landing, send_sems, recv_sems):
        step = pl.program_id(0)
        start, forward, finish = _gather_plan(_Copies(send_sems, recv_sems), 0, landing, WOUT_ROWS)

        @pl.when(step == 0)
        def _():
            landing[_block_rows(_place(), WOUT_ROWS), :] = wout_ref[...].astype(COMM_DTYPE)
            start()

        pl.when(step == steps // 2)(forward)

        h = h_ref[...]

        def proj(lo, hi):
            return _dot(h, w_ref[lo:hi, :], NT) + b_ref[:, lo:hi]

        qs = proj(0, ATTN_W) * SCALE
        for pair in range(N_PAIRS):
            q_ref[pair] = qs[:, pair * LANES:(pair + 1) * LANES].astype(MXU_DTYPE)
        kv = proj(KV0, GATE0)
        low = lax.broadcasted_iota(jnp.int32, (tm, LANES), 1) < HALF
        for i in range(2):
            t = kv[:, i * LANES:(i + 1) * LANES]
            rot = pltpu.roll(t, HALF, 1)
            variants = (jnp.where(low, t, 0.0), jnp.where(low, 0.0, rot),
                        jnp.where(low, rot, 0.0), jnp.where(low, 0.0, t))
            for j, val in enumerate(variants):
                col = (4 * i + j) * LANES
                kvx_ref[:, col:col + LANES] = val.astype(MXU_DTYPE)
                if i == 1:
                    ones_elsewhere = jnp.where(low == (j % 2 == 0), val, 1.0)
                    kvx_ref[:, col + 4 * LANES:col + 5 * LANES] = ones_elsewhere.astype(MXU_DTYPE)
        for k in range(4):
            gate_ref[k] = proj(GATE0 + k * SGU_W, GATE0 + (k + 1) * SGU_W)

        @pl.when(step == steps - 1)
        def _():
            finish()
            wfull_ref[...] = landing[...]

    return pl.pallas_call(
        body,
        name="in_proj",
        grid=(steps,),
        in_specs=[pl.BlockSpec((tm, D_MODEL), lambda i: (i, 0)),
                  _full((1, IN_W)), _full((IN_W, D_MODEL)), VMEM_SPEC],
        out_specs=(pl.BlockSpec((N_PAIRS, tm, LANES), lambda i: (0, i, 0)),
                   pl.BlockSpec((tm, KVX_W), lambda i: (i, 0)),
                   pl.BlockSpec((4, tm, SGU_W), lambda i: (0, i, 0)),
                   _full((D_MODEL, D_MODEL))),
        out_shape=(jax.ShapeDtypeStruct((N_PAIRS, SEQ, LANES), MXU_DTYPE),
                   jax.ShapeDtypeStruct((SEQ, KVX_W), MXU_DTYPE),
                   jax.ShapeDtypeStruct((4, SEQ, SGU_W), F32),
                   jax.ShapeDtypeStruct((D_MODEL, D_MODEL), COMM_DTYPE)),
        scratch_shapes=[pltpu.VMEM((D_MODEL, D_MODEL), COMM_DTYPE)] + _dma_sems(GATHER_SEMS),
        compiler_params=_params(("arbitrary",), VMEM_LIMIT),
    )(h, b_in, win_t, wout_shard)


def _window_mask(n):
    qi = lax.broadcasted_iota(jnp.int32, (2 * BLOCK, 2 * BLOCK), 0) & (BLOCK - 1)
    p = lax.broadcasted_iota(jnp.int32, (2 * BLOCK, 2 * BLOCK), 1) - BLOCK
    in_window = jnp.logical_and(p <= qi, p > qi - BLOCK)
    return jnp.logical_and(in_window, jnp.logical_or(p >= 0, n > 0))


def _sink_column(sink_ref, g, par):
    return jnp.concatenate([jnp.full((BLOCK, 1), sink_ref[4 * g + par], F32),
                            jnp.full((BLOCK, 1), sink_ref[4 * g + 2 + par], F32)], axis=0)


def _kv_cat(kp_ref, kc_ref, var, with_ones):
    kcol, vcol = var * LANES, (var + (8 if with_ones else 4)) * LANES
    return (jnp.concatenate([kp_ref[:, kcol:kcol + LANES], kc_ref[:, kcol:kcol + LANES]], axis=0),
            jnp.concatenate([kp_ref[:, vcol:vcol + LANES], kc_ref[:, vcol:vcol + LANES]], axis=0))


def _softmax_numerator(s, sink):
    m = jnp.maximum(jnp.max(s, axis=1, keepdims=True), sink)
    return jnp.exp(s - m), m


def _mixers_out_proj(sinks, q, kvx, gates, ln_g, ln_b, sgu_w, bias_full, x, target, wout, b_out, final_g):
    tm = FUSED_BLOCKS * BLOCK
    n_tiles = SEQ // tm

    def body(sink_ref, q_ref, kc_ref, za_ref, us_ref, vs_ref, zs_ref, lng_ref, lnb_ref, w_ref, bias_ref,
             x_ref, t_ref, wout_ref, b_ref, gf_ref,
             out_ref, gres_ref, dmix_ref, gw_ref, vec_ref,
             kp_ref, wm_ref, mixed_next, mixed_cur, out_stage, gb_ref):
        step = pl.program_id(0)

        @pl.when(step == 0)
        def _():
            kp_ref[...] = jnp.zeros_like(kp_ref)
            _mask_sgu_weights(w_ref, wm_ref)
            gw_ref[...] = jnp.zeros_like(gw_ref)
            vec_ref[...] = jnp.zeros_like(vec_ref)
            mixed_cur[...] = jnp.zeros_like(mixed_cur)

        def mixers_block(b, after_chain=()):
            rows = slice(b * BLOCK, (b + 1) * BLOCK)
            kc = kc_ref.at[rows, :]
            u, _, _, vln = _sgu_activations(us_ref[rows, :], vs_ref[rows, :], lng_ref[...], lnb_ref[...])

            valid = _window_mask(step * FUSED_BLOCKS + b)[0:BLOCK]
            chains = [(g, par, i) for g in range(2) for par in range(2) for i in range(2)]
            kv = {(g, par): _kv_cat(kp_ref, kc, 2 * g + par, True) for g in range(2) for par in range(2)}
            scores, outs = {}, {}

            def issue_scores(k):
                g, par, i = chains[k]
                scores[k] = _dot(q_ref[2 * g + i, rows, :], kv[g, par][0], NT)

            ahead = ATTN_FWD_AHEAD
            for k in range(ahead):
                issue_scores(k)
            low = lax.broadcasted_iota(jnp.int32, (BLOCK, LANES), 1) < HALF
            for k, (g, par, i) in enumerate(chains):
                sink = sink_ref[4 * g + 2 * i + par]
                e, m = _softmax_numerator(jnp.where(valid, scores[k], NEG_INF), sink)
                if k + ahead < len(chains):
                    issue_scores(k + ahead)
                o = _dot(e, kv[g, par][1])
                outs[g, par, i] = o / (pltpu.roll(o, HALF, 1) + jnp.exp(sink - m))
                if k == SGU_MIX_AFTER_CHAIN:
                    mixed = _sgu_mix(vln, wm_ref, bias_ref)
                if k % 2 == 0 and k // 2 < len(after_chain):
                    after_chain[k // 2]()
            for pair in range(N_PAIRS):
                g, i = divmod(pair, 2)
                lanes = slice(pair * LANES, (pair + 1) * LANES)
                o = jnp.where(low, outs[g, 0, i], outs[g, 1, i])
                out_stage[pair, rows, :] = o
                gate, _ = _silu_and_grad(za_ref[rows, lanes])
                mixed_next[rows, lanes] = (o * gate).astype(MXU_DTYPE)
            kp_ref[...] = kc[...]
            for pair in range(N_SGU_HEADS // 2):
                cols = slice(pair * LANES, (pair + 1) * LANES)
                gate, _ = _silu_and_grad(zs_ref[rows, cols])
                mixed_next[rows, ATTN_W + pair * LANES:ATTN_W + (pair + 1) * LANES] = (
                    u[:, cols] * mixed[pair] * gate).astype(MXU_DTYPE)

        live = (step > 0).astype(F32)
        quarter = D_MODEL // 4
        columns = [None] * 4

        def project(j):
            def piece():
                columns[j] = _dot(mixed_cur[...], wout_ref[:, j * quarter:(j + 1) * quarter])
            return piece

        half_blocks = FUSED_BLOCKS // 2
        per_block = 4 // half_blocks
        for b in range(half_blocks):
            mixers_block(b, [project(j) for j in range(b * per_block, (b + 1) * per_block)])
        xo = x_ref[...] + jnp.concatenate(columns, axis=1) + b_ref[...]
        r = lax.rsqrt(jnp.mean(xo * xo, axis=-1, keepdims=True) + NORM_EPS)
        xn = xo * r
        gf = gf_ref[...]
        err = xn * gf - t_ref[...]
        loss = 0.5 * jnp.sum(jnp.mean(err * err, axis=-1, keepdims=True), axis=0, keepdims=True)
        dy = err * (1.0 / D_MODEL)
        dxn = dy * gf
        gres = r * (dxn - xn * jnp.mean(dxn * xn, axis=-1, keepdims=True))
        vec_ref[0:1, :] += jnp.broadcast_to(loss * live, (1, D_MODEL))
        vec_ref[1:2, :] += jnp.sum(dy * xn, axis=0, keepdims=True) * live
        vec_ref[2:3, :] += jnp.sum(gres, axis=0, keepdims=True) * live
        gres_ref[...] = gres
        gb_ref[...] = gres.astype(MXU_DTYPE)

        def branch_grad(k):
            def piece():
                dmix_ref[k] = _dot(gb_ref[...], wout_ref[k * ATTN_W:(k + 1) * ATTN_W, :], NT)
            return piece

        def weight_grad(k):
            def piece():
                rows = slice(k * ATTN_W, (k + 1) * ATTN_W)
                gw_ref[rows, :] += _dot(mixed_cur[:, rows], gb_ref[...], TN)
            return piece

        backward = [branch_grad(0), branch_grad(1), weight_grad(0), weight_grad(1)]
        for b in range(half_blocks):
            mixers_block(half_blocks + b, backward[b * per_block:(b + 1) * per_block])

        @pl.when(step < n_tiles)
        def _():
            out_ref[...] = out_stage[...]

        mixed_cur[...] = mixed_next[...]

    ahead_tile = lambda i: jnp.minimum(i, n_tiles - 1)
    behind_tile = lambda i: jnp.maximum(i - 1, 0)
    blk = lambda w: pl.BlockSpec((tm, w), lambda i: (ahead_tile(i), 0))
    tiles = pl.BlockSpec((N_PAIRS, tm, LANES), lambda i: (0, ahead_tile(i), 0))
    gate = lambda k: pl.BlockSpec((None, tm, SGU_W), lambda i: (k, ahead_tile(i), 0))
    behind = lambda w: pl.BlockSpec((tm, w), lambda i: (behind_tile(i), 0))
    return pl.pallas_call(
        body,
        name="mixers_out_proj",
        grid=(n_tiles + 1,),
        in_specs=[pl.BlockSpec(memory_space=pltpu.SMEM), tiles, blk(KVX_W), gate(0), gate(1), gate(2), gate(3),
                  _full((1, SGU_W)), _full((1, SGU_W)), _full((N_SGU_HEADS, BLOCK, BLOCK)), _full((BLOCK, SGU_W)),
                  behind(D_MODEL), behind(D_MODEL), _full((D_MODEL, D_MODEL)), _full((1, D_MODEL)),
                  _full((1, D_MODEL))],
        out_specs=(tiles, behind(D_MODEL), pl.BlockSpec((2, tm, ATTN_W), lambda i: (0, behind_tile(i), 0)),
                   _full((D_MODEL, D_MODEL)), _full((8, D_MODEL))),
        out_shape=(jax.ShapeDtypeStruct((N_PAIRS, SEQ, LANES), F32),
                   jax.ShapeDtypeStruct((SEQ, D_MODEL), F32),
                   jax.ShapeDtypeStruct((2, SEQ, ATTN_W), F32),
                   jax.ShapeDtypeStruct((D_MODEL, D_MODEL), F32),
                   jax.ShapeDtypeStruct((8, D_MODEL), F32)),
        scratch_shapes=[pltpu.VMEM((BLOCK, KVX_W), MXU_DTYPE), pltpu.VMEM((N_SGU_HEADS, BLOCK, BLOCK), MXU_DTYPE),
                        pltpu.VMEM((tm, D_MODEL), MXU_DTYPE), pltpu.VMEM((tm, D_MODEL), MXU_DTYPE),
                        pltpu.VMEM((N_PAIRS, tm, LANES), F32), pltpu.VMEM((tm, D_MODEL), MXU_DTYPE)],
        compiler_params=_params(("arbitrary",), VMEM_LIMIT),
    )(sinks, q, kvx, gates, gates, gates, gates, ln_g, ln_b, sgu_w, bias_full, x, target, wout, b_out, final_g)


def _sgu_activations(us, vs, lng, lnb):
    u = _gelu(us)
    vg = _gelu(vs)
    mu = jnp.mean(vg, axis=-1, keepdims=True)
    xc = vg - mu
    rstd = lax.rsqrt(jnp.mean(xc * xc, axis=-1, keepdims=True) + NORM_EPS)
    vhat = xc * rstd
    return u, vhat, rstd, vhat * lng + lnb


def _mask_sgu_weights(w_ref, masked_ref, transposed_ref=None):
    tril = (lax.broadcasted_iota(jnp.int32, (BLOCK, BLOCK), 0)
            >= lax.broadcasted_iota(jnp.int32, (BLOCK, BLOCK), 1))
    for hh in range(N_SGU_HEADS):
        w = jnp.where(tril, w_ref[hh], 0.0)
        masked_ref[hh] = w.astype(MXU_DTYPE)
        if transposed_ref is not None:
            transposed_ref[hh] = w.T.astype(MXU_DTYPE)


def _sgu_mix(vln, masked_w_ref, bias_ref):
    low = lax.broadcasted_iota(jnp.int32, (BLOCK, LANES), 1) < HALF
    mixed = []
    for pair in range(N_SGU_HEADS // 2):
        vp = vln[:, pair * LANES:(pair + 1) * LANES]
        mixed.append(_dot(masked_w_ref[2 * pair], jnp.where(low, vp, 0.0))
                     + _dot(masked_w_ref[2 * pair + 1], jnp.where(low, 0.0, vp))
                     + bias_ref[:, pair * LANES:(pair + 1) * LANES])
    return mixed


def _mixers_bwd(sinks, dmix, q, kvx, out, gates, ln_g, ln_b, sgu_w, bias_full, gwout):
    last = N_BLOCKS - 1

    def body(sink_ref, d_ref, q_ref, kc_ref, o_ref, za_ref, dsg_ref, us_ref, vs_ref, zs_ref, lng_ref, lnb_ref, w_ref,
             bias_ref, gwout_ref,
             dp_ref, gsink_ref, gbin_ref, dps_ref, gw_ref, gb_ref, gln_ref, gbins_ref, wout_shard_ref,
             kp_ref, pend_ref, carry_ref, wm_ref, wt_ref, gbias_ref, sa_w, ra_w, sb_w, rc_w, send_sems, recv_sems):
        n = pl.program_id(0)
        start, exchange, finish = _reduce_scatter_plan(_Copies(send_sems, recv_sems), 0, gwout_ref, WOUT_ROWS,
                                                       sa_w, ra_w, sb_w, rc_w, wout_shard_ref)
        tril = (lax.broadcasted_iota(jnp.int32, (BLOCK, BLOCK), 0)
                >= lax.broadcasted_iota(jnp.int32, (BLOCK, BLOCK), 1))

        @pl.when(n == 0)
        def _():
            gsink_ref[...] = jnp.zeros_like(gsink_ref)
            gbin_ref[...] = jnp.zeros_like(gbin_ref)
            carry_ref[...] = jnp.zeros_like(carry_ref)
            kp_ref[...] = jnp.zeros_like(kp_ref)
            gw_ref[...] = jnp.zeros_like(gw_ref)
            gln_ref[...] = jnp.zeros_like(gln_ref)
            gbins_ref[...] = jnp.zeros_like(gbins_ref)
            gbias_ref[...] = jnp.zeros_like(gbias_ref)
            _mask_sgu_weights(w_ref, wm_ref, wt_ref)
            start()

        pl.when(n == 3)(exchange)
        pl.when(n == 12)(finish)

        @pl.when(n > 0)
        def _():
            dp_ref[:, 0:ATTN_W] = pend_ref[:, 0:ATTN_W]
            dp_ref[:, GATE0:ATTN_SECTION] = pend_ref[:, ATTN_W:]

        @pl.when(n > last)
        def _():
            dp_ref[:, KV0:GATE0] = carry_ref[...].astype(MXU_DTYPE)

        @pl.when(n <= last)
        def _():
            us = us_ref[...]
            vs = vs_ref[...]
            lng = lng_ref[...]
            u, vhat, rstd, vln = _sgu_activations(us, vs, lng, lnb_ref[...])
            low_sgu = lax.broadcasted_iota(jnp.int32, (BLOCK, LANES), 1) < HALF
            sgu = {}

            def sgu_gates():
                mixed = _sgu_mix(vln, wm_ref, bias_ref)
                sgu["du"], sgu["dzs"], sgu["dm"] = [], [], []
                for pair in range(N_SGU_HEADS // 2):
                    cols = slice(pair * LANES, (pair + 1) * LANES)
                    dsg = dsg_ref[:, cols]
                    gate, gate_grad = _silu_and_grad(zs_ref[:, cols])
                    up = u[:, cols]
                    sgu["du"].append(dsg * mixed[pair] * gate)
                    sgu["dzs"].append(dsg * up * mixed[pair] * gate_grad)
                    dmixed = dsg * up * gate
                    gbias_ref[:, cols] += dmixed
                    sgu["dm"].append((jnp.where(low_sgu, dmixed, 0.0).astype(MXU_DTYPE),
                                      jnp.where(low_sgu, 0.0, dmixed).astype(MXU_DTYPE)))

            def sgu_grads():
                dvln_parts = []
                for pair in range(N_SGU_HEADS // 2):
                    dm_lo, dm_hi = sgu["dm"][pair]
                    vp = vln[:, pair * LANES:(pair + 1) * LANES]
                    gw_ref[2 * pair] += _dot(dm_lo, vp, NT)
                    gw_ref[2 * pair + 1] += _dot(dm_hi, vp, NT)
                    dvln_parts.append(_dot(wt_ref[2 * pair], dm_lo) + _dot(wt_ref[2 * pair + 1], dm_hi))
                dvln = jnp.concatenate(dvln_parts, axis=1)
                gln_ref[0:1, :] += jnp.sum(dvln * vhat, axis=0, keepdims=True)
                gln_ref[1:2, :] += jnp.sum(dvln, axis=0, keepdims=True)
                dvhat = dvln * lng
                dvg = rstd * (dvhat - jnp.mean(dvhat, axis=-1, keepdims=True)
                              - vhat * jnp.mean(dvhat * vhat, axis=-1, keepdims=True))
                dus = jnp.concatenate(sgu["du"], axis=1) * _gelu_grad(us)
                dvs = dvg * _gelu_grad(vs)
                dzs = jnp.concatenate(sgu["dzs"], axis=1)
                for k, val in enumerate((dus, dvs, dzs)):
                    dps_ref[:, k * SGU_W:(k + 1) * SGU_W] = val.astype(MXU_DTYPE)
                    gbins_ref[:, k * SGU_W:(k + 1) * SGU_W] += jnp.sum(val, axis=0, keepdims=True)

            valid = _window_mask(n)[0:BLOCK]
            low = lax.broadcasted_iota(jnp.int32, (BLOCK, LANES), 1) < HALF
            low_keys = lax.broadcasted_iota(jnp.int32, (2 * BLOCK, LANES), 1) < HALF
            lane_row = lax.broadcasted_iota(jnp.int32, (1, LANES), 1)
            gsink = jnp.zeros((1, LANES), F32)
            chains = [(g, par, i) for g in range(2) for par in range(2) for i in range(2)]
            kv = {(g, par): _kv_cat(kp_ref, kc_ref, 2 * g + par, False) for g in range(2) for par in range(2)}
            ones_keys = jnp.ones((2 * BLOCK, LANES), MXU_DTYPE)
            half_of_lane = lax.broadcasted_iota(jnp.int32, (LANES, 2 * LANES), 0) // HALF
            half_of_col = lax.broadcasted_iota(jnp.int32, (LANES, 2 * LANES), 1) // LANES
            sum_halves = (half_of_lane == half_of_col).astype(MXU_DTYPE)
            douts, deltas = [], []
            for pair in range(N_PAIRS):
                lanes = slice(pair * LANES, (pair + 1) * LANES)
                dg = d_ref[:, lanes]
                gate, gate_grad = _silu_and_grad(za_ref[:, lanes])
                o = o_ref[pair]
                dout = dg * gate
                dza = dg * o * gate_grad
                douts.append(dout.astype(MXU_DTYPE))
                deltas.append(_dot(dout * o, sum_halves))
                zl = slice(ATTN_W + pair * LANES, ATTN_W + (pair + 1) * LANES)
                pend_ref[:, zl] = dza.astype(MXU_DTYPE)
                gl = slice(GATE0 + pair * LANES, GATE0 + (pair + 1) * LANES)
                gbin_ref[:, gl] += jnp.sum(dza, axis=0, keepdims=True)

            first = {}

            def issue_first(k):
                g, par, i = chains[k]
                first[k] = (_dot(q_ref[2 * g + i], kv[g, par][0], NT), _dot(douts[2 * g + i], kv[g, par][1], NT))

            numerators = {}

            def issue_row_sums(k):
                g, par, i = chains[k]
                sink = sink_ref[4 * g + 2 * i + par]
                e, m = _softmax_numerator(jnp.where(valid, first[k][0], NEG_INF), sink)
                numerators[k] = (e, jnp.exp(sink - m), _dot(e, ones_keys))

            ahead = ATTN_BWD_AHEAD
            for k in range(ahead):
                issue_first(k)
            issue_row_sums(0)
            issue_row_sums(1)
            dqs, dk_parts, dv_parts = {}, {}, {}
            operands = {}

            def issue_last(k):
                g, par, i = chains[k]
                ds, ds_t, p_t = operands.pop(k)
                dq = _dot(ds, kv[g, par][0])
                dqs[g, i] = dq if par == 0 else dqs[g, i] + dq
                dk = _dot(ds_t, q_ref[2 * g + i])
                dv = _dot(p_t, douts[2 * g + i])
                dk_parts[g, par] = dk if i == 0 else dk_parts[g, par] + dk
                dv_parts[g, par] = dv if i == 0 else dv_parts[g, par] + dv

            for k, (g, par, i) in enumerate(chains):
                h = 4 * g + 2 * i + par
                delta = deltas[2 * g + i][:, par * LANES:(par + 1) * LANES]
                e, at_sink, row_sum = numerators[k]
                inv = 1.0 / (row_sum + at_sink)
                p = e * jnp.tile(inv, (1, 2))
                ds = p * (first[k][1] - jnp.tile(delta, (1, 2)))
                ds = ds.astype(MXU_DTYPE)
                operands[k] = (ds, ds.T, p.astype(MXU_DTYPE).T)
                total = jnp.sum(at_sink * inv * delta, axis=0, keepdims=True)
                gsink = jnp.where(lane_row == h, -total, gsink)
                if k + ahead < len(chains):
                    issue_first(k + ahead)
                if k + 2 < len(chains):
                    issue_row_sums(k + 2)
                if k > 0:
                    issue_last(k - 1)
                if k == SGU_GATES_AFTER_CHAIN:
                    sgu_gates()
                if k == SGU_GRADS_AFTER_CHAIN:
                    sgu_grads()
            issue_last(len(chains) - 1)
            for pair in range(N_PAIRS):
                g, i = divmod(pair, 2)
                dq = dqs[g, i] * SCALE
                lanes = slice(pair * LANES, (pair + 1) * LANES)
                pend_ref[:, lanes] = dq.astype(MXU_DTYPE)
                gbin_ref[:, lanes] += jnp.sum(dq, axis=0, keepdims=True)
            gsink_ref[...] += gsink
            for k, parts in enumerate((dk_parts, dv_parts)):
                masked = {key: jnp.where(low_keys if key[1] == 0 else jnp.logical_not(low_keys), val, 0.0)
                          for key, val in parts.items()}
                both = (masked[0, 0] + masked[1, 1]
                        + pltpu.roll(masked[0, 1] + masked[1, 0], HALF, 1))
                lanes = slice(k * KV_W, (k + 1) * KV_W)
                done = carry_ref[:, lanes] + both[0:BLOCK]
                dp_ref[:, KV0 + k * KV_W:KV0 + (k + 1) * KV_W] = done.astype(MXU_DTYPE)
                carry_ref[:, lanes] = both[BLOCK:]
                gbin_ref[:, KV0 + k * KV_W:KV0 + (k + 1) * KV_W] += jnp.sum(both, axis=0, keepdims=True)
            kp_ref[...] = kc_ref[...]

        @pl.when(n == last)
        def _():
            for hh in range(N_SGU_HEADS):
                gw_ref[hh] = jnp.where(tril, gw_ref[hh], 0.0)
            head_of_lane = lax.broadcasted_iota(jnp.int32, (N_SGU_HEADS, SGU_W), 1) // HEAD_DIM
            select = (head_of_lane == lax.broadcasted_iota(jnp.int32, (N_SGU_HEADS, SGU_W), 0)).astype(F32)
            gb_ref[...] = lax.dot_general(select, gbias_ref[...], NT, precision=lax.Precision.HIGHEST,
                                          preferred_element_type=F32)

    at = lambda n: jnp.minimum(n, last)
    blk = lambda w: pl.BlockSpec((BLOCK, w), lambda n: (at(n), 0))
    tiles = pl.BlockSpec((N_PAIRS, BLOCK, LANES), lambda n: (0, at(n), 0))
    section = lambda k: pl.BlockSpec((None, BLOCK, SGU_W), lambda n: (k, at(n), 0))
    return pl.pallas_call(
        body,
        name="mixers_bwd",
        grid=(N_BLOCKS + 1,),
        in_specs=[pl.BlockSpec(memory_space=pltpu.SMEM),
                  section(0),
                  tiles,
                  blk(KVX_W),
                  tiles,
                  section(0),
                  section(1),
                  section(1), section(2), section(3),
                  _full((1, SGU_W)), _full((1, SGU_W)), _full((N_SGU_HEADS, BLOCK, BLOCK)), _full((BLOCK, SGU_W)),
                  VMEM_SPEC],
        out_specs=(pl.BlockSpec((BLOCK, ATTN_SECTION), lambda n: (jnp.maximum(n - 1, 0), 0)),
                   _full((1, LANES)), _full((1, ATTN_SECTION)),
                   pl.BlockSpec((BLOCK, SGU_SECTION), lambda n: (at(n), 0)),
                   _full((N_SGU_HEADS, BLOCK, BLOCK)), _full((N_SGU_HEADS, BLOCK)),
                   _full((8, SGU_W)), _full((1, SGU_SECTION)), VMEM_SPEC),
        out_shape=(jax.ShapeDtypeStruct((SEQ, ATTN_SECTION), MXU_DTYPE),
                   jax.ShapeDtypeStruct((1, LANES), F32),
                   jax.ShapeDtypeStruct((1, ATTN_SECTION), F32),
                   jax.ShapeDtypeStruct((SEQ, SGU_SECTION), MXU_DTYPE),
                   jax.ShapeDtypeStruct((N_SGU_HEADS, BLOCK, BLOCK), F32),
                   jax.ShapeDtypeStruct((N_SGU_HEADS, BLOCK), F32),
                   jax.ShapeDtypeStruct((8, SGU_W), F32),
                   jax.ShapeDtypeStruct((1, SGU_SECTION), F32),
                   jax.ShapeDtypeStruct((WOUT_ROWS, D_MODEL), F32)),
        scratch_shapes=([pltpu.VMEM((BLOCK, KVX_W), MXU_DTYPE),
                         pltpu.VMEM((BLOCK, 2 * ATTN_W), MXU_DTYPE), pltpu.VMEM((BLOCK, 2 * KV_W), F32),
                         pltpu.VMEM((N_SGU_HEADS, BLOCK, BLOCK), MXU_DTYPE),
                         pltpu.VMEM((N_SGU_HEADS, BLOCK, BLOCK), MXU_DTYPE), pltpu.VMEM((BLOCK, SGU_W), F32)]
                        + _reduce_scatter_scratch(WOUT_ROWS, D_MODEL, COMM_DTYPE) + _dma_sems(REDUCE_SEMS)),
        compiler_params=_params(("arbitrary",), VMEM_LIMIT),
    )(sinks, dmix, q, kvx, out, gates, dmix, gates, gates, gates, ln_g, ln_b, sgu_w, bias_full, gwout)


WIN_GRAD_ROWS = 256


def _win_grad_segments():
    segments = []
    for chunk in range(IN_W // WIN_GRAD_ROWS):
        for owner in range(N_DEV):
            lo = max(chunk * WIN_GRAD_ROWS, owner * WIN_ROWS)
            hi = min((chunk + 1) * WIN_GRAD_ROWS, (owner + 1) * WIN_ROWS)
            if lo < hi:
                segments.append((len(segments), chunk, owner, lo, hi - lo))
    return segments


def _win_grad_orders():
    n_chips = N_DEV // 2
    chunks = range(IN_W // WIN_GRAD_ROWS)
    of_chip = [[k for k in chunks if (k * WIN_GRAD_ROWS + WIN_GRAD_ROWS // 2) // (2 * WIN_ROWS) == chip]
               for chip in range(n_chips)]
    orders = []
    for chip in range(n_chips):
        order = [of_chip[chip ^ t][r] for r in range(max(map(len, of_chip))) for t in range(1, n_chips)
                 if r < len(of_chip[chip ^ t])]
        orders.append(order + of_chip[chip])
    return orders


def _win_grad_in_proj_bwd(dpa, dps, h, gsguw, win_t, x, norm_g, gres, vec_parts):
    rows = WIN_GRAD_ROWS
    tm = TOKEN_TILE
    n_attn = ATTN_SECTION // rows
    steps = IN_W // rows
    all_steps = steps + SEQ // tm
    segments = _win_grad_segments()
    n_seg = len(segments)
    per_owner = max(sum(1 for seg in segments if seg[2] == p) for p in range(N_DEV))
    n_parts = len(vec_parts)
    class_rows = (N_DEV // 2) * WIN_ROWS

    orders = _win_grad_orders()
    chip_x, chip_y, _ = _place()
    order = jnp.asarray(orders, jnp.int32)[2 * chip_x + chip_y]

    def body(order_ref, da_ref, ds_ref, h_hbm, gsguw_ref, da_tile_ref, ds_tile_ref, w_hbm, x_ref, g_ref, gres_ref,
             *rest):
        part_refs = rest[:n_parts]
        (gx_ref, shard_ref, sguw_full_ref, vec_out_ref,
         h_ref, w_ref, load_sems, gng_ref,
         chunks, sa, ra, sb, rc, sa_s, ra_s, sb_s, rc_s, landing, vec_ref, ra_vec, slots,
         send_sems, recv_sems, send1, recv1, send2, recv2) = rest[n_parts:]
        step = pl.program_id(0)
        first_phase = step < steps
        chunk = order_ref[jnp.minimum(step, steps - 1)]
        last_chunk = order_ref[jnp.clip(step - 1, 0, steps - 1)]
        load_h = pltpu.make_async_copy(h_hbm, h_ref, load_sems.at[0])
        load_w = pltpu.make_async_copy(w_hbm, w_ref, load_sems.at[1])

        @pl.when(step == 0)
        def _():
            load_h.start()
            load_w.start()
            gng_ref[...] = jnp.zeros_like(gng_ref)
            load_h.wait()
        x, y, c = _place()
        copies = _Copies(send_sems, recv_sems)
        own_sguw = landing.at[_block_rows((x, y, c), SGUW_ROWS), :]
        start_s, exchange_s, finish_s = _reduce_scatter_plan(copies, 0, gsguw_ref, SGUW_ROWS,
                                                             sa_s, ra_s, sb_s, rc_s, own_sguw)
        gather = _gather_plan(copies, REDUCE_SEMS, landing, SGUW_ROWS)

        def place_of(owner):
            return owner // 4, (owner // 2) % 2, owner % 2

        def class_rows_of(owner, first, n):
            return pl.ds((owner // 2) * WIN_ROWS + first - owner * WIN_ROWS, n)

        def to_sibling(seg):
            sid, _, owner, first, n = seg
            at = class_rows_of(owner, first, n)
            return pltpu.make_async_remote_copy(src_ref=sa.at[at, :], dst_ref=ra.at[at, :], send_sem=send1.at[sid],
                                                recv_sem=recv1.at[sid], device_id=(x, y, 1 - c), device_id_type=MESH)

        def to_owner(seg):
            sid, _, owner, first, n = seg
            px, py, pc = place_of(owner)
            slot = (x + px - 2 * x * px) + 2 * (y + py - 2 * y * py) - 1
            nth = sum(1 for other in segments if other[2] == owner and other[0] < sid)
            dst = rc.at[pl.ds(pl.multiple_of(slot * WIN_ROWS, 16) + first - owner * WIN_ROWS, n), :]
            return pltpu.make_async_remote_copy(src_ref=sb.at[class_rows_of(owner, first, n), :], dst_ref=dst,
                                                send_sem=send2.at[sid], recv_sem=recv2.at[slot * per_owner + nth],
                                                device_id=(px, py, pc), device_id_type=MESH)

        def give(seg, at_step):
            sid, _, owner, first, n = seg

            @pl.when(c != owner % 2)
            def _():
                local = pl.ds(first % rows, n)
                sa[class_rows_of(owner, first, n), :] = chunks[at_step % 2, local, :].astype(sa.dtype)
                to_sibling(seg).start()

        def keep(seg, at_step):
            sid, _, owner, first, n = seg
            px, py, pc = place_of(owner)

            @pl.when(c == pc)
            def _():
                to_sibling(seg).wait_recv()
                local = pl.ds(first % rows, n)
                total = chunks[at_step % 2, local, :] + ra[class_rows_of(owner, first, n), :].astype(F32)
                mine = jnp.logical_and(x == px, y == py)

                @pl.when(mine)
                def _():
                    shard_ref[pl.ds(first - owner * WIN_ROWS, n), :] = total

                @pl.when(jnp.logical_not(mine))
                def _():
                    sb[class_rows_of(owner, first, n), :] = total.astype(sb.dtype)

        def send(seg):
            px, py, pc = place_of(seg[2])

            @pl.when(jnp.logical_and(c == pc, jnp.logical_not(jnp.logical_and(x == px, y == py))))
            def _():
                to_owner(seg).start()

        sent = jnp.clip(step // 2 - 1, 0, steps - 1)
        sending = jnp.logical_and(jnp.logical_and(step % 2 == 0, step >= 2), step // 2 - 1 < steps)
        sent_chunk = order_ref[sent]

        pl.when(step == 0)(start_s)
        pl.when(step == 2)(exchange_s)

        @pl.when(step == 5)
        def _():
            finish_s()
            gather[0]()

        pl.when(step == 7)(gather[1])

        @pl.when(jnp.logical_and(first_phase, chunk < n_attn))
        def _():
            chunks[step % 2] = _dot(da_ref[...], h_ref[...], TN)

        @pl.when(jnp.logical_and(first_phase, chunk >= n_attn))
        def _():
            chunks[step % 2] = _dot(ds_ref[...], h_ref[...], TN)

        for k in range(steps):
            @pl.when(jnp.logical_and(first_phase, chunk == k))
            def _():
                for seg in segments:
                    if seg[1] == k:
                        give(seg, step)

            @pl.when(jnp.logical_and(jnp.logical_and(step > 0, step <= steps), last_chunk == k))
            def _():
                for seg in segments:
                    if seg[1] == k:
                        keep(seg, step - 1)

            @pl.when(jnp.logical_and(sending, sent_chunk == k))
            def _():
                for seg in segments:
                    if seg[1] == k:
                        send(seg)

        pl.when(step == steps)(load_w.wait)

        @pl.when(step >= steps)
        def _():
            dh = (_dot(da_tile_ref[...], w_ref[0:ATTN_SECTION, :])
                  + _dot(ds_tile_ref[...], w_ref[ATTN_SECTION:, :]))
            xv = x_ref[...]
            r = lax.rsqrt(jnp.mean(xv * xv, axis=-1, keepdims=True) + NORM_EPS)
            xn = xv * r
            gng_ref[...] += jnp.sum(dh * xn, axis=0, keepdims=True)
            dxn = dh * g_ref[...]
            gx_ref[...] = r * (dxn - xn * jnp.mean(dxn * xn, axis=-1, keepdims=True)) + gres_ref[...]

        @pl.when(step == all_steps - 1)
        def _():
            for owner in range(N_DEV):
                px, py, pc = place_of(owner)

                @pl.when(jnp.logical_and(jnp.logical_and(x == px, y == py), c == pc))
                def _():
                    mine = [seg for seg in segments if seg[2] == owner]
                    for slot in range(3):
                        for nth, (sid, _, _, first, n) in enumerate(mine):
                            landed = rc.at[pl.ds(slot * WIN_ROWS + first - owner * WIN_ROWS, n), :]
                            pltpu.make_async_remote_copy(
                                src_ref=landed, dst_ref=landed, send_sem=send2.at[sid],
                                recv_sem=recv2.at[slot * per_owner + nth], device_id=(x, y, c),
                                device_id_type=MESH).wait_recv()
                    acc = shard_ref[...]
                    for slot in range(3):
                        acc = acc + rc[slot * WIN_ROWS:(slot + 1) * WIN_ROWS, :].astype(F32)
                    shard_ref[...] = acc
            for seg in segments:
                owner = seg[2]
                px, py, pc = place_of(owner)

                @pl.when(c != pc)
                def _():
                    to_sibling(seg).wait_send()

                @pl.when(jnp.logical_and(c == pc, jnp.logical_not(jnp.logical_and(x == px, y == py))))
                def _():
                    to_owner(seg).wait_send()
            gather[2]()
            sguw_full_ref[...] = landing[...]
            _all_reduce_vectors(copies, REDUCE_SEMS + GATHER_SEMS, gng_ref, *part_refs, vec_out_ref, vec_ref, ra_vec,
                                slots)

    chunk_at = lambda i, order_ref: order_ref[jnp.minimum(i, steps - 1)]
    tile = lambda w: pl.BlockSpec((tm, w), lambda i, order_ref: (jnp.maximum(i - steps, 0), 0))
    grid_spec = pltpu.PrefetchScalarGridSpec(
        num_scalar_prefetch=1,
        grid=(all_steps,),
        in_specs=[pl.BlockSpec((SEQ, rows), lambda i, order_ref: (0, jnp.minimum(chunk_at(i, order_ref), n_attn - 1))),
                  pl.BlockSpec((SEQ, rows), lambda i, order_ref: (0, jnp.maximum(chunk_at(i, order_ref) - n_attn, 0))),
                  pl.BlockSpec(memory_space=pl.ANY), VMEM_SPEC,
                  tile(ATTN_SECTION), tile(SGU_SECTION), pl.BlockSpec(memory_space=pl.ANY), tile(D_MODEL),
                  _full((1, D_MODEL)), tile(D_MODEL)] + [VMEM_SPEC] * n_parts,
        out_specs=(tile(D_MODEL), VMEM_SPEC, _full((N_SGU_HEADS * BLOCK, BLOCK)), VMEM_SPEC),
        scratch_shapes=([pltpu.VMEM((SEQ, D_MODEL), h.dtype), pltpu.VMEM((IN_W, D_MODEL), win_t.dtype),
                         pltpu.SemaphoreType.DMA((2,)), pltpu.VMEM((1, D_MODEL), F32),
                         pltpu.VMEM((2, rows, D_MODEL), F32),
                         pltpu.VMEM((class_rows, D_MODEL), COMM_DTYPE), pltpu.VMEM((class_rows, D_MODEL), COMM_DTYPE),
                         pltpu.VMEM((class_rows, D_MODEL), COMM_DTYPE), pltpu.VMEM((3 * WIN_ROWS, D_MODEL), COMM_DTYPE)]
                        + _reduce_scatter_scratch(SGUW_ROWS, BLOCK, F32)
                        + [pltpu.VMEM((N_SGU_HEADS * BLOCK, BLOCK), F32)]
                        + _vector_scratch() + _dma_sems(REDUCE_SEMS + GATHER_SEMS + VECTOR_SEMS)
                        + _dma_sems(n_seg) + [pltpu.SemaphoreType.DMA((n_seg,)),
                                              pltpu.SemaphoreType.DMA((3 * per_owner,))]))
    return pl.pallas_call(
        body,
        name="win_grad_in_proj_bwd",
        grid_spec=grid_spec,
        out_shape=(jax.ShapeDtypeStruct((SEQ, D_MODEL), F32),
                   jax.ShapeDtypeStruct((WIN_ROWS, D_MODEL), F32),
                   jax.ShapeDtypeStruct((N_SGU_HEADS * BLOCK, BLOCK), F32),
                   jax.ShapeDtypeStruct((VEC_ROWS, IN_W), F32)),
        compiler_params=_params(("arbitrary",), VMEM_LIMIT),
    )(order, dpa, dps, h, gsguw, dpa, dps, win_t, x, norm_g, gres, *vec_parts)


VEC_NORM_G, VEC_B_IN, VEC_SINKS, VEC_LN_G, VEC_LN_B, VEC_B_OUT, VEC_FINAL_G, VEC_LOSS, VEC_SGU_B = 0, 1, 2, 3, 4, 5, 6, 7, 8


def _adamw(w, g, m, v):
    m = ADAM_B1 * m + (1.0 - ADAM_B1) * g
    v = ADAM_B2 * v + (1.0 - ADAM_B2) * (g * g)
    m_hat = m / (1.0 - ADAM_B1 ** ADAM_STEP)
    v_hat = v / (1.0 - ADAM_B2 ** ADAM_STEP)
    delta = -ADAM_LR * (m_hat / (jnp.sqrt(v_hat) + ADAM_EPS) + ADAM_WD * w)
    return delta, m, v


def _adamw_shard(name, g, w, m, v, block_rows):
    def body(g_ref, w_ref, m_ref, v_ref, d_ref, nm_ref, nv_ref):
        d_ref[...], nm_ref[...], nv_ref[...] = _adamw(w_ref[...], g_ref[...], m_ref[...], v_ref[...])

    rows, cols = w.shape
    spec = pl.BlockSpec((block_rows, cols), lambda i: (i, 0))
    return pl.pallas_call(
        body,
        name=name,
        grid=(rows // block_rows,),
        in_specs=[spec] * 4,
        out_specs=(spec,) * 3,
        out_shape=(jax.ShapeDtypeStruct(w.shape, F32),) * 3,
        compiler_params=_params(("arbitrary",)),
    )(g, w, m, v)


VECTOR_SEMS = 4


def _vector_scratch():
    return [pltpu.VMEM((VEC_ROWS, IN_W), F32), pltpu.VMEM((VEC_ROWS, IN_W), F32),
            pltpu.VMEM((4 * VEC_ROWS, IN_W), F32)]


def _all_reduce_vectors(copies, sem0, gng_ref, gba_ref, gbs_ref, gsink_ref, gln_ref, gsgub_ref, vec4_ref, out_ref,
                        vec_ref, ra_vec, slots):
    x, y, c = _place()
    vec_ref[...] = jnp.zeros_like(vec_ref)
    vec_ref[VEC_NORM_G:VEC_NORM_G + 1, 0:D_MODEL] = gng_ref[...]
    vec_ref[VEC_B_IN:VEC_B_IN + 1, 0:ATTN_SECTION] = gba_ref[...]
    vec_ref[VEC_B_IN:VEC_B_IN + 1, ATTN_SECTION:IN_W] = gbs_ref[...]
    vec_ref[VEC_SINKS:VEC_SINKS + 1, 0:LANES] = gsink_ref[...]
    vec_ref[VEC_LN_G:VEC_LN_G + 1, 0:SGU_W] = gln_ref[0:1, :]
    vec_ref[VEC_LN_B:VEC_LN_B + 1, 0:SGU_W] = gln_ref[1:2, :]
    vec_ref[VEC_B_OUT:VEC_B_OUT + 1, 0:D_MODEL] = vec4_ref[2:3, :]
    vec_ref[VEC_FINAL_G:VEC_FINAL_G + 1, 0:D_MODEL] = vec4_ref[1:2, :]
    vec_ref[VEC_LOSS:VEC_LOSS + 1, 0:D_MODEL] = vec4_ref[0:1, :]
    vec_ref[VEC_SGU_B:VEC_SGU_B + N_SGU_HEADS, 0:BLOCK] = gsgub_ref[...]

    to_sibling = copies(sem0, vec_ref, ra_vec, (x, y, 1 - c))
    to_sibling.start()
    to_sibling.wait_recv()

    def chip_slot(place):
        return slots.at[pl.ds(pl.multiple_of((2 * place[0] + place[1]) * VEC_ROWS, 8), VEC_ROWS), :]

    mine = chip_slot((x, y))
    mine[...] = vec_ref[...] + ra_vec[...]
    to_chips = [copies(sem0 + i, mine, mine, (*_chip(rel), c)) for i, rel in enumerate(RELATIONS[1:], start=1)]
    for cp in to_chips:
        cp.start()
    for i, rel in enumerate(RELATIONS[1:], start=1):
        theirs = chip_slot(_chip(rel))
        copies(sem0 + i, theirs, theirs, (x, y, c)).wait_recv()
    out_ref[...] = ((slots[0:VEC_ROWS, :] + slots[VEC_ROWS:2 * VEC_ROWS, :])
                    + slots[2 * VEC_ROWS:3 * VEC_ROWS, :]) + slots[3 * VEC_ROWS:, :]
    to_sibling.wait_send()
    for cp in to_chips:
        cp.wait_send()


def _adamw_replicated(vec, gsguw, weights, m_state, v_state):
    n = len(SMALL)

    def body(*refs):
        vec_ref, gsguw_ref = refs[0], refs[1]
        w_refs, m_refs, v_refs = (refs[2 + k * n:2 + (k + 1) * n] for k in range(3))
        outs = refs[2 + 3 * n:]
        g_refs, d_refs, nm_refs, nv_refs = (outs[k * n:(k + 1) * n] for k in range(4))
        for i, (_, row, shape) in enumerate(SMALL):
            g = gsguw_ref[...] if row is None else vec_ref[row:row + shape[0], 0:shape[1]]
            g_refs[i][...] = g
            d_refs[i][...], nm_refs[i][...], nv_refs[i][...] = _adamw(
                w_refs[i][...], g, m_refs[i][...], v_refs[i][...])

    shapes = tuple(jax.ShapeDtypeStruct(shape, F32) for _, _, shape in SMALL)
    outs = pl.pallas_call(
        body,
        name="adamw_replicated",
        in_specs=[VMEM_SPEC] * (2 + 3 * n),
        out_specs=(VMEM_SPEC,) * (4 * n),
        out_shape=shapes * 4,
    )(vec, gsguw, *weights, *m_state, *v_state)
    return tuple(outs[k * n:(k + 1) * n] for k in range(4))


SMALL = (
    ("norm_g", VEC_NORM_G, (1, D_MODEL)),
    ("b_in", VEC_B_IN, (1, IN_W)),
    ("attn_sinks", VEC_SINKS, (1, N_Q_HEADS)),
    ("sgu_ln_g", VEC_LN_G, (1, SGU_W)),
    ("sgu_ln_b", VEC_LN_B, (1, SGU_W)),
    ("sgu_w", None, (N_SGU_HEADS * BLOCK, BLOCK)),
    ("sgu_b", VEC_SGU_B, (N_SGU_HEADS, BLOCK)),
    ("b_out", VEC_B_OUT, (1, D_MODEL)),
    ("final_norm_g", VEC_FINAL_G, (1, D_MODEL)),
)


def _local_grads(x, target, h, win_t, wout_shard, norm_g, b_in, attn_sinks, sgu_ln_g, sgu_ln_b, sgu_w, sgu_b, b_out,
                 final_g):
    sinks = attn_sinks.reshape(N_Q_HEADS)
    bias_full = jnp.repeat(sgu_b.T, HEAD_DIM, axis=1)
    q, kvx, gates, wout = _in_proj(h, b_in, win_t, wout_shard)
    out, gres, dmix, gwout, vec4 = _mixers_out_proj(sinks, q, kvx, gates, sgu_ln_g, sgu_ln_b, sgu_w, bias_full,
                                                    x, target, wout, b_out, final_g)
    dpa, gsink, gbin_a, dps, gsguw, gsgub, gln, gbin_s, gwout_shard = _mixers_bwd(
        sinks, dmix, q, kvx, out, gates, sgu_ln_g, sgu_ln_b, sgu_w, bias_full, gwout)
    grad_x, gwin_shard, gsguw_sum, vec = _win_grad_in_proj_bwd(
        dpa, dps, h, gsguw.reshape(N_SGU_HEADS * BLOCK, BLOCK), win_t, x, norm_g, gres,
        (gbin_a, gbin_s, gsink, gln, gsgub, vec4))
    return grad_x, gwin_shard, gwout_shard, gsguw_sum, vec


def kernel(x, norm_g, w_in, b_in, attn_sinks, sgu_ln_g, sgu_ln_b, sgu_w, sgu_b, w_out, b_out, final_norm_g, loss_target, m_norm_g, m_w_in, m_b_in, m_attn_sinks, m_sgu_ln_g, m_sgu_ln_b, m_sgu_w, m_sgu_b, m_w_out, m_b_out, m_final_norm_g, v_norm_g, v_w_in, v_b_in, v_attn_sinks, v_sgu_ln_g, v_sgu_ln_b, v_sgu_w, v_sgu_b, v_w_out, v_b_out, v_final_norm_g):
    given = dict(norm_g=norm_g, b_in=b_in, attn_sinks=attn_sinks, sgu_ln_g=sgu_ln_g, sgu_ln_b=sgu_ln_b,
                 sgu_w=sgu_w, sgu_b=sgu_b, b_out=b_out, final_norm_g=final_norm_g)
    m_given = dict(norm_g=m_norm_g, b_in=m_b_in, attn_sinks=m_attn_sinks, sgu_ln_g=m_sgu_ln_g,
                   sgu_ln_b=m_sgu_ln_b, sgu_w=m_sgu_w, sgu_b=m_sgu_b, b_out=m_b_out, final_norm_g=m_final_norm_g)
    v_given = dict(norm_g=v_norm_g, b_in=v_b_in, attn_sinks=v_attn_sinks, sgu_ln_g=v_sgu_ln_g,
                   sgu_ln_b=v_sgu_ln_b, sgu_w=v_sgu_w, sgu_b=v_sgu_b, b_out=v_b_out, final_norm_g=v_final_norm_g)

    win_t, h = _all_gather_win(w_in[0].T, x[0], norm_g)
    grad_x, gwin_t, gwout, gsguw, vec = _local_grads(
        x[0], loss_target[0], h, win_t, w_out[0], norm_g, b_in, attn_sinks, sgu_ln_g, sgu_ln_b, sgu_w[0], sgu_b[0],
        b_out, final_norm_g.reshape(1, D_MODEL))

    t = lambda a: a[0].T
    d_win, nm_win, nv_win = _adamw_shard("adamw_w_in", gwin_t, t(w_in), t(m_w_in), t(v_w_in), WIN_ROWS // 2)
    d_wout, nm_wout, nv_wout = _adamw_shard("adamw_w_out", gwout, w_out[0], m_w_out[0], v_w_out[0], WOUT_ROWS)
    as_2d = lambda d: [d[name].reshape(shape) for name, _, shape in SMALL]
    loss = vec[VEC_LOSS, 0]
    small = _adamw_replicated(vec, gsguw, as_2d(given), as_2d(m_given), as_2d(v_given))

    def assemble(big_in, big_out, k):
        vals = {name: small[k][i].reshape(given[name].shape) for i, (name, _, _) in enumerate(SMALL)}
        vals["w_in"] = big_in.T[None]
        vals["w_out"] = big_out[None]
        order = ("norm_g", "w_in", "b_in", "attn_sinks", "sgu_ln_g", "sgu_ln_b", "sgu_w", "sgu_b", "w_out",
                 "b_out", "final_norm_g")
        return [vals[name] for name in order]

    return (loss, grad_x[None],
            *assemble(gwin_t, gwout, 0), *assemble(d_win, d_wout, 1),
            *assemble(nm_win, nm_wout, 2), *assemble(nv_win, nv_wout, 3))
```

```python
import functools
import math

import jax
import jax.numpy as jnp
from jax import lax
from jax.experimental import pallas as pl
from jax.experimental.pallas import tpu as pltpu

F32 = jnp.float32
BF16 = jnp.bfloat16
MXU_DTYPE = BF16
COMM_DTYPE = BF16

D_MODEL = 1024
SEQ = 4096
HEAD_DIM = 64
N_Q_HEADS = 8
Q_PER_KV = 4
BLOCK = 128
N_BLOCKS = SEQ // BLOCK
ATTN_W = 512
KV_W = 128
SGU_W = 512
N_SGU_HEADS = 8
IN_W = 2816
NORM_EPS = 1e-5
NEG_INF = -1e30
SCALE = HEAD_DIM ** -0.5
KV0 = ATTN_W
GATE0 = ATTN_W + 2 * KV_W
SGU0 = GATE0 + ATTN_W
ATTN_SECTION = SGU0
SGU_SECTION = IN_W - SGU0

ADAM_LR = 0.001
ADAM_B1 = 0.9
ADAM_B2 = 0.999
ADAM_EPS = 1e-08
ADAM_WD = 0.01
ADAM_STEP = 10

N_DEV = 8
WIN_ROWS = IN_W // N_DEV
WOUT_ROWS = D_MODEL // N_DEV
SGUW_ROWS = N_SGU_HEADS * BLOCK // N_DEV
VEC_ROWS = 16
MESH = pl.DeviceIdType.MESH

LANES = 128
HALF = LANES // 2
N_PAIRS = N_Q_HEADS * HEAD_DIM // LANES
KVX_W = 12 * LANES
TOKEN_TILE = 256
FWD_TOKEN_TILE = 512
ATTN_FWD_AHEAD = 4
FUSED_BLOCKS = 2
SGU_MIX_AFTER_CHAIN = 0
SGU_GATES_AFTER_CHAIN = 1
SGU_GRADS_AFTER_CHAIN = 5
ATTN_BWD_AHEAD = 3
VMEM_LIMIT = 56 * 1024 * 1024

NN = (((1,), (0,)), ((), ()))
NT = (((1,), (1,)), ((), ()))
TN = (((0,), (0,)), ((), ()))


def _dot(a, b, dims=NN):
    return lax.dot_general(a.astype(MXU_DTYPE), b.astype(MXU_DTYPE), dims, preferred_element_type=F32)


def _gelu(x):
    return x * (lax.erf(x * (1.0 / math.sqrt(2.0))) + 1.0) * 0.5


def _gelu_grad(x):
    cdf = (lax.erf(x * (1.0 / math.sqrt(2.0))) + 1.0) * 0.5
    return cdf + x * jnp.exp(-0.5 * x * x) * (1.0 / math.sqrt(2.0 * math.pi))


def _silu_and_grad(z):
    s = jax.nn.sigmoid(z)
    return z * s, s * (1.0 + z * (1.0 - s))


def _params(semantics=None, vmem=None):
    kw = {}
    if semantics is not None:
        kw["dimension_semantics"] = semantics
    if vmem is not None:
        kw["vmem_limit_bytes"] = vmem
    return pltpu.CompilerParams(**kw)


def _full(shape):
    return pl.BlockSpec(shape, lambda *_: (0,) * len(shape))


VMEM_SPEC = pl.BlockSpec(memory_space=pltpu.VMEM)


RELATIONS = ((0, 0), (1, 0), (0, 1), (1, 1))


def _place():
    return lax.axis_index("x"), lax.axis_index("y"), lax.axis_index("c")


def _chip(rel):
    x, y, _ = _place()
    return (1 - x if rel[0] else x, 1 - y if rel[1] else y)


def _block_rows(place, n_rows):
    px, py, pc = place
    return pl.ds(pl.multiple_of((4 * px + 2 * py + pc) * n_rows, 16), n_rows)


class _Copies:
    def __init__(self, send_sems, recv_sems):
        self.send_sems, self.recv_sems = send_sems, recv_sems

    def __call__(self, k, src, dst, to):
        return pltpu.make_async_remote_copy(src_ref=src, dst_ref=dst, send_sem=self.send_sems.at[k],
                                            recv_sem=self.recv_sems.at[k], device_id=to, device_id_type=MESH)


def _gather_plan(copies, sem0, full_ref, n_rows):
    x, y, c = _place()
    me, sibling = (x, y, c), (x, y, 1 - c)
    chips = [_chip(rel) for rel in RELATIONS[1:]]

    def cp(k, block, to):
        rows = full_ref.at[_block_rows(block, n_rows), :]
        return copies(sem0 + k, rows, rows, to)

    first = [cp(0, me, sibling)] + [cp(1 + j, me, (*chip, c)) for j, chip in enumerate(chips)]
    passed = [cp(4 + j, (*chip, c), sibling) for j, chip in enumerate(chips)]

    def start():
        for f in first:
            f.start()

    def forward():
        for j, chip in enumerate(chips):
            cp(1 + j, (*chip, c), me).wait_recv()
            passed[j].start()

    def finish():
        cp(0, sibling, me).wait_recv()
        for j, chip in enumerate(chips):
            cp(4 + j, (*chip, 1 - c), me).wait_recv()
        for f in first + passed:
            f.wait_send()

    return start, forward, finish


GATHER_SEMS = 7


def _reduce_scatter_plan(copies, sem0, part_ref, n_rows, sa, ra, sb, rc, res_ref, pieces=1):
    x, y, c = _place()
    sibling = (x, y, 1 - c)
    n = n_rows
    piece_rows = n // pieces
    assert piece_rows * pieces == n and piece_rows % 16 == 0
    level1 = copies(sem0, sa, ra, sibling)

    def level2(i, j):
        first = (i - 1) * n + j * piece_rows
        slot = pl.ds(first if isinstance(first, int) else pl.multiple_of(first, 16), piece_rows)
        return copies(sem0 + 1 + (i - 1) * pieces + j, sb.at[slot, :], rc.at[slot, :], (*_chip(RELATIONS[i]), c))

    def start():
        for i, rel in enumerate(RELATIONS):
            sa[i * n:(i + 1) * n, :] = part_ref[_block_rows((*_chip(rel), 1 - c), n), :].astype(sa.dtype)
        level1.start()

    def add():
        level1.wait_recv()
        for i, rel in enumerate(RELATIONS):
            total = part_ref[_block_rows((*_chip(rel), c), n), :] + ra[i * n:(i + 1) * n, :].astype(F32)
            if i == 0:
                res_ref[...] = total
            else:
                sb[(i - 1) * n:i * n, :] = total.astype(sb.dtype)

    def send(j):
        for i in range(1, len(RELATIONS)):
            level2(i, j).start()

    def exchange():
        add()
        send(0)

    def finish():
        acc = res_ref[...]
        for i in range(1, len(RELATIONS)):
            for j in range(pieces):
                level2(i, j).wait_recv()
            acc = acc + rc[(i - 1) * n:i * n, :].astype(F32)
        res_ref[...] = acc
        level1.wait_send()
        for i in range(1, len(RELATIONS)):
            for j in range(pieces):
                level2(i, j).wait_send()

    return (start, exchange, finish) if pieces == 1 else (start, add, send, finish)


REDUCE_SEMS = 4


def _reduce_scatter_scratch(n_rows, width, dtype):
    return [pltpu.VMEM((4 * n_rows, width), dtype), pltpu.VMEM((4 * n_rows, width), dtype),
            pltpu.VMEM((3 * n_rows, width), dtype), pltpu.VMEM((3 * n_rows, width), dtype)]


def _dma_sems(n):
    return [pltpu.SemaphoreType.DMA((n,)), pltpu.SemaphoreType.DMA((n,))]


def _all_gather_win(win_t_shard, x, norm_g):
    tm = FWD_TOKEN_TILE
    steps = SEQ // tm

    def body(win_ref, x_ref, g_ref, full_ref, h_ref, landing, send_sems, recv_sems):
        step = pl.program_id(0)
        start, forward, finish = _gather_plan(_Copies(send_sems, recv_sems), 0, landing, WIN_ROWS)

        @pl.when(step == 0)
        def _():
            landing[_block_rows(_place(), WIN_ROWS), :] = win_ref[...].astype(COMM_DTYPE)
            start()

        xv = x_ref[...]
        r = lax.rsqrt(jnp.mean(xv * xv, axis=-1, keepdims=True) + NORM_EPS)
        h_ref[...] = ((xv * r) * g_ref[...]).astype(MXU_DTYPE)

        @pl.when(step == steps - 1)
        def _():
            forward()
            finish()
            full_ref[...] = landing[...]

    return pl.pallas_call(
        body,
        name="all_gather_win",
        grid=(steps,),
        in_specs=[VMEM_SPEC, pl.BlockSpec((tm, D_MODEL), lambda i: (i, 0)), _full((1, D_MODEL))],
        out_specs=(_full((IN_W, D_MODEL)), pl.BlockSpec((tm, D_MODEL), lambda i: (i, 0))),
        out_shape=(jax.ShapeDtypeStruct((IN_W, D_MODEL), COMM_DTYPE),
                   jax.ShapeDtypeStruct((SEQ, D_MODEL), MXU_DTYPE)),
        scratch_shapes=[pltpu.VMEM((IN_W, D_MODEL), COMM_DTYPE)] + _dma_sems(GATHER_SEMS),
        compiler_params=_params(("arbitrary",), VMEM_LIMIT),
    )(win_t_shard, x, norm_g)


def _in_proj(h, b_in, win_t, wout_shard):
    tm = FWD_TOKEN_TILE
    steps = SEQ // tm

    def body(h_ref, b_ref, w_ref, wout_ref, q_ref, kvx_ref, gate_ref, wfull_ref, landing, send_sems, recv_sems):
        step = pl.program_id(0)
        start, forward, finish = _gather_plan(_Copies(send_sems, recv_sems), 0, landing, WOUT_ROWS)

        @pl.when(step == 0)
        def _():
            landing[_block_rows(_place(), WOUT_ROWS), :] = wout_ref[...].astype(COMM_DTYPE)
            start()

        pl.when(step == steps // 2)(forward)

        h = h_ref[...]

        def proj(lo, hi):
            return _dot(h, w_ref[lo:hi, :], NT) + b_ref[:, lo:hi]

        qs = proj(0, ATTN_W) * SCALE
        for pair in range(N_PAIRS):
            q_ref[pair] = qs[:, pair * LANES:(pair + 1) * LANES].astype(MXU_DTYPE)
        kv = proj(KV0, GATE0)
        low = lax.broadcasted_iota(jnp.int32, (tm, LANES), 1) < HALF
        for i in range(2):
            t = kv[:, i * LANES:(i + 1) * LANES]
            rot = pltpu.roll(t, HALF, 1)
            variants = (jnp.where(low, t, 0.0), jnp.where(low, 0.0, rot),
                        jnp.where(low, rot, 0.0), jnp.where(low, 0.0, t))
            for j, val in enumerate(variants):
                col = (4 * i + j) * LANES
                kvx_ref[:, col:col + LANES] = val.astype(MXU_DTYPE)
                if i == 1:
                    ones_elsewhere = jnp.where(low == (j % 2 == 0), val, 1.0)
                    kvx_ref[:, col + 4 * LANES:col + 5 * LANES] = ones_elsewhere.astype(MXU_DTYPE)
        for k in range(4):
            gate_ref[k] = proj(GATE0 + k * SGU_W, GATE0 + (k + 1) * SGU_W)

        @pl.when(step == steps - 1)
        def _():
            finish()
            wfull_ref[...] = landing[...]

    return pl.pallas_call(
        body,
        name="in_proj",
        grid=(steps,),
        in_specs=[pl.BlockSpec((tm, D_MODEL), lambda i: (i, 0)),
                  _full((1, IN_W)), _full((IN_W, D_MODEL)), VMEM_SPEC],
        out_specs=(pl.BlockSpec((N_PAIRS, tm, LANES), lambda i: (0, i, 0)),
                   pl.BlockSpec((tm, KVX_W), lambda i: (i, 0)),
                   pl.BlockSpec((4, tm, SGU_W), lambda i: (0, i, 0)),
                   _full((D_MODEL, D_MODEL))),
        out_shape=(jax.ShapeDtypeStruct((N_PAIRS, SEQ, LANES), MXU_DTYPE),
                   jax.ShapeDtypeStruct((SEQ, KVX_W), MXU_DTYPE),
                   jax.ShapeDtypeStruct((4, SEQ, SGU_W), F32),
                   jax.ShapeDtypeStruct((D_MODEL, D_MODEL), COMM_DTYPE)),
        scratch_shapes=[pltpu.VMEM((D_MODEL, D_MODEL), COMM_DTYPE)] + _dma_sems(GATHER_SEMS),
        compiler_params=_params(("arbitrary",), VMEM_LIMIT),
    )(h, b_in, win_t, wout_shard)


def _window_mask(n):
    qi = lax.broadcasted_iota(jnp.int32, (2 * BLOCK, 2 * BLOCK), 0) & (BLOCK - 1)
    p = lax.broadcasted_iota(jnp.int32, (2 * BLOCK, 2 * BLOCK), 1) - BLOCK
    in_window = jnp.logical_and(p <= qi, p > qi - BLOCK)
    return jnp.logical_and(in_window, jnp.logical_or(p >= 0, n > 0))


def _sink_column(sink_ref, g, par):
    return jnp.concatenate([jnp.full((BLOCK, 1), sink_ref[4 * g + par], F32),
                            jnp.full((BLOCK, 1), sink_ref[4 * g + 2 + par], F32)], axis=0)


def _kv_cat(kp_ref, kc_ref, var, with_ones):
    kcol, vcol = var * LANES, (var + (8 if with_ones else 4)) * LANES
    return (jnp.concatenate([kp_ref[:, kcol:kcol + LANES], kc_ref[:, kcol:kcol + LANES]], axis=0),
            jnp.concatenate([kp_ref[:, vcol:vcol + LANES], kc_ref[:, vcol:vcol + LANES]], axis=0))


def _softmax_numerator(s, sink):
    m = jnp.maximum(jnp.max(s, axis=1, keepdims=True), sink)
    return jnp.exp(s - m), m


def _mixers_out_proj(sinks, q, kvx, gates, ln_g, ln_b, sgu_w, bias_full, x, target, wout, b_out, final_g):
    tm = FUSED_BLOCKS * BLOCK
    n_tiles = SEQ // tm

    def body(sink_ref, q_ref, kc_ref, za_ref, us_ref, vs_ref, zs_ref, lng_ref, lnb_ref, w_ref, bias_ref,
             x_ref, t_ref, wout_ref, b_ref, gf_ref,
             out_ref, gres_ref, dmix_ref, gw_ref, vec_ref,
             kp_ref, wm_ref, mixed_next, mixed_cur, out_stage, gb_ref):
        step = pl.program_id(0)

        @pl.when(step == 0)
        def _():
            kp_ref[...] = jnp.zeros_like(kp_ref)
            _mask_sgu_weights(w_ref, wm_ref)
            gw_ref[...] = jnp.zeros_like(gw_ref)
            vec_ref[...] = jnp.zeros_like(vec_ref)
            mixed_cur[...] = jnp.zeros_like(mixed_cur)

        def mixers_block(b, after_chain=()):
            rows = slice(b * BLOCK, (b + 1) * BLOCK)
            kc = kc_ref.at[rows, :]
            u, _, _, vln = _sgu_activations(us_ref[rows, :], vs_ref[rows, :], lng_ref[...], lnb_ref[...])

            valid = _window_mask(step * FUSED_BLOCKS + b)[0:BLOCK]
            chains = [(g, par, i) for g in range(2) for par in range(2) for i in range(2)]
            kv = {(g, par): _kv_cat(kp_ref, kc, 2 * g + par, True) for g in range(2) for par in range(2)}
            scores, outs = {}, {}

            def issue_scores(k):
                g, par, i = chains[k]
                scores[k] = _dot(q_ref[2 * g + i, rows, :], kv[g, par][0], NT)

            ahead = ATTN_FWD_AHEAD
            for k in range(ahead):
                issue_scores(k)
            low = lax.broadcasted_iota(jnp.int32, (BLOCK, LANES), 1) < HALF
            for k, (g, par, i) in enumerate(chains):
                sink = sink_ref[4 * g + 2 * i + par]
                e, m = _softmax_numerator(jnp.where(valid, scores[k], NEG_INF), sink)
                if k + ahead < len(chains):
                    issue_scores(k + ahead)
                o = _dot(e, kv[g, par][1])
                outs[g, par, i] = o / (pltpu.roll(o, HALF, 1) + jnp.exp(sink - m))
                if k == SGU_MIX_AFTER_CHAIN:
                    mixed = _sgu_mix(vln, wm_ref, bias_ref)
                if k % 2 == 0 and k // 2 < len(after_chain):
                    after_chain[k // 2]()
            for pair in range(N_PAIRS):
                g, i = divmod(pair, 2)
                lanes = slice(pair * LANES, (pair + 1) * LANES)
                o = jnp.where(low, outs[g, 0, i], outs[g, 1, i])
                out_stage[pair, rows, :] = o
                gate, _ = _silu_and_grad(za_ref[rows, lanes])
                mixed_next[rows, lanes] = (o * gate).astype(MXU_DTYPE)
            kp_ref[...] = kc[...]
            for pair in range(N_SGU_HEADS // 2):
                cols = slice(pair * LANES, (pair + 1) * LANES)
                gate, _ = _silu_and_grad(zs_ref[rows, cols])
                mixed_next[rows, ATTN_W + pair * LANES:ATTN_W + (pair + 1) * LANES] = (
                    u[:, cols] * mixed[pair] * gate).astype(MXU_DTYPE)

        live = (step > 0).astype(F32)
        quarter = D_MODEL // 4
        columns = [None] * 4

        def project(j):
            def piece():
                columns[j] = _dot(mixed_cur[...], wout_ref[:, j * quarter:(j + 1) * quarter])
            return piece

        half_blocks = FUSED_BLOCKS // 2
        per_block = 4 // half_blocks
        for b in range(half_blocks):
            mixers_block(b, [project(j) for j in range(b * per_block, (b + 1) * per_block)])
        xo = x_ref[...] + jnp.concatenate(columns, axis=1) + b_ref[...]
        r = lax.rsqrt(jnp.mean(xo * xo, axis=-1, keepdims=True) + NORM_EPS)
        xn = xo * r
        gf = gf_ref[...]
        err = xn * gf - t_ref[...]
        loss = 0.5 * jnp.sum(jnp.mean(err * err, axis=-1, keepdims=True), axis=0, keepdims=True)
        dy = err * (1.0 / D_MODEL)
        dxn = dy * gf
        gres = r * (dxn - xn * jnp.mean(dxn * xn, axis=-1, keepdims=True))
        vec_ref[0:1, :] += jnp.broadcast_to(loss * live, (1, D_MODEL))
        vec_ref[1:2, :] += jnp.sum(dy * xn, axis=0, keepdims=True) * live
        vec_ref[2:3, :] += jnp.sum(gres, axis=0, keepdims=True) * live
        gres_ref[...] = gres
        gb_ref[...] = gres.astype(MXU_DTYPE)

        def branch_grad(k):
            def piece():
                dmix_ref[k] = _dot(gb_ref[...], wout_ref[k * ATTN_W:(k + 1) * ATTN_W, :], NT)
            return piece

        def weight_grad(k):
            def piece():
                rows = slice(k * ATTN_W, (k + 1) * ATTN_W)
                gw_ref[rows, :] += _dot(mixed_cur[:, rows], gb_ref[...], TN)
            return piece

        backward = [branch_grad(0), branch_grad(1), weight_grad(0), weight_grad(1)]
        for b in range(half_blocks):
            mixers_block(half_blocks + b, backward[b * per_block:(b + 1) * per_block])

        @pl.when(step < n_tiles)
        def _():
            out_ref[...] = out_stage[...]

        mixed_cur[...] = mixed_next[...]

    ahead_tile = lambda i: jnp.minimum(i, n_tiles - 1)
    behind_tile = lambda i: jnp.maximum(i - 1, 0)
    blk = lambda w: pl.BlockSpec((tm, w), lambda i: (ahead_tile(i), 0))
    tiles = pl.BlockSpec((N_PAIRS, tm, LANES), lambda i: (0, ahead_tile(i), 0))
    gate = lambda k: pl.BlockSpec((None, tm, SGU_W), lambda i: (k, ahead_tile(i), 0))
    behind = lambda w: pl.BlockSpec((tm, w), lambda i: (behind_tile(i), 0))
    return pl.pallas_call(
        body,
        name="mixers_out_proj",
        grid=(n_tiles + 1,),
        in_specs=[pl.BlockSpec(memory_space=pltpu.SMEM), tiles, blk(KVX_W), gate(0), gate(1), gate(2), gate(3),
                  _full((1, SGU_W)), _full((1, SGU_W)), _full((N_SGU_HEADS, BLOCK, BLOCK)), _full((BLOCK, SGU_W)),
                  behind(D_MODEL), behind(D_MODEL), _full((D_MODEL, D_MODEL)), _full((1, D_MODEL)),
                  _full((1, D_MODEL))],
        out_specs=(tiles, behind(D_MODEL), pl.BlockSpec((2, tm, ATTN_W), lambda i: (0, behind_tile(i), 0)),
                   _full((D_MODEL, D_MODEL)), _full((8, D_MODEL))),
        out_shape=(jax.ShapeDtypeStruct((N_PAIRS, SEQ, LANES), F32),
                   jax.ShapeDtypeStruct((SEQ, D_MODEL), F32),
                   jax.ShapeDtypeStruct((2, SEQ, ATTN_W), F32),
                   jax.ShapeDtypeStruct((D_MODEL, D_MODEL), F32),
                   jax.ShapeDtypeStruct((8, D_MODEL), F32)),
        scratch_shapes=[pltpu.VMEM((BLOCK, KVX_W), MXU_DTYPE), pltpu.VMEM((N_SGU_HEADS, BLOCK, BLOCK), MXU_DTYPE),
                        pltpu.VMEM((tm, D_MODEL), MXU_DTYPE), pltpu.VMEM((tm, D_MODEL), MXU_DTYPE),
                        pltpu.VMEM((N_PAIRS, tm, LANES), F32), pltpu.VMEM((tm, D_MODEL), MXU_DTYPE)],
        compiler_params=_params(("arbitrary",), VMEM_LIMIT),
    )(sinks, q, kvx, gates, gates, gates, gates, ln_g, ln_b, sgu_w, bias_full, x, target, wout, b_out, final_g)


def _sgu_activations(us, vs, lng, lnb):
    u = _gelu(us)
    vg = _gelu(vs)
    mu = jnp.mean(vg, axis=-1, keepdims=True)
    xc = vg - mu
    rstd = lax.rsqrt(jnp.mean(xc * xc, axis=-1, keepdims=True) + NORM_EPS)
    vhat = xc * rstd
    return u, vhat, rstd, vhat * lng + lnb


def _mask_sgu_weights(w_ref, masked_ref, transposed_ref=None):
    tril = (lax.broadcasted_iota(jnp.int32, (BLOCK, BLOCK), 0)
            >= lax.broadcasted_iota(jnp.int32, (BLOCK, BLOCK), 1))
    for hh in range(N_SGU_HEADS):
        w = jnp.where(tril, w_ref[hh], 0.0)
        masked_ref[hh] = w.astype(MXU_DTYPE)
        if transposed_ref is not None:
            transposed_ref[hh] = w.T.astype(MXU_DTYPE)


def _sgu_mix(vln, masked_w_ref, bias_ref):
    low = lax.broadcasted_iota(jnp.int32, (BLOCK, LANES), 1) < HALF
    mixed = []
    for pair in range(N_SGU_HEADS // 2):
        vp = vln[:, pair * LANES:(pair + 1) * LANES]
        mixed.append(_dot(masked_w_ref[2 * pair], jnp.where(low, vp, 0.0))
                     + _dot(masked_w_ref[2 * pair + 1], jnp.where(low, 0.0, vp))
                     + bias_ref[:, pair * LANES:(pair + 1) * LANES])
    return mixed


def _mixers_bwd(sinks, dmix, q, kvx, out, gates, ln_g, ln_b, sgu_w, bias_full, gwout):
    last = N_BLOCKS - 1

    def body(sink_ref, d_ref, q_ref, kc_ref, o_ref, za_ref, dsg_ref, us_ref, vs_ref, zs_ref, lng_ref, lnb_ref, w_ref,
             bias_ref, gwout_ref,
             dp_ref, gsink_ref, gbin_ref, dps_ref, gw_ref, gb_ref, gln_ref, gbins_ref, wout_shard_ref,
             kp_ref, pend_ref, carry_ref, wm_ref, wt_ref, gbias_ref, sa_w, ra_w, sb_w, rc_w, send_sems, recv_sems):
        n = pl.program_id(0)
        start, exchange, finish = _reduce_scatter_plan(_Copies(send_sems, recv_sems), 0, gwout_ref, WOUT_ROWS,
                                                       sa_w, ra_w, sb_w, rc_w, wout_shard_ref)
        tril = (lax.broadcasted_iota(jnp.int32, (BLOCK, BLOCK), 0)
                >= lax.broadcasted_iota(jnp.int32, (BLOCK, BLOCK), 1))

        @pl.when(n == 0)
        def _():
            gsink_ref[...] = jnp.zeros_like(gsink_ref)
            gbin_ref[...] = jnp.zeros_like(gbin_ref)
            carry_ref[...] = jnp.zeros_like(carry_ref)
            kp_ref[...] = jnp.zeros_like(kp_ref)
            gw_ref[...] = jnp.zeros_like(gw_ref)
            gln_ref[...] = jnp.zeros_like(gln_ref)
            gbins_ref[...] = jnp.zeros_like(gbins_ref)
            gbias_ref[...] = jnp.zeros_like(gbias_ref)
            _mask_sgu_weights(w_ref, wm_ref, wt_ref)
            start()

        pl.when(n == 3)(exchange)
        pl.when(n == 12)(finish)

        @pl.when(n > 0)
        def _():
            dp_ref[:, 0:ATTN_W] = pend_ref[:, 0:ATTN_W]
            dp_ref[:, GATE0:ATTN_SECTION] = pend_ref[:, ATTN_W:]

        @pl.when(n > last)
        def _():
            dp_ref[:, KV0:GATE0] = carry_ref[...].astype(MXU_DTYPE)

        @pl.when(n <= last)
        def _():
            us = us_ref[...]
            vs = vs_ref[...]
            lng = lng_ref[...]
            u, vhat, rstd, vln = _sgu_activations(us, vs, lng, lnb_ref[...])
            low_sgu = lax.broadcasted_iota(jnp.int32, (BLOCK, LANES), 1) < HALF
            sgu = {}

            def sgu_gates():
                mixed = _sgu_mix(vln, wm_ref, bias_ref)
                sgu["du"], sgu["dzs"], sgu["dm"] = [], [], []
                for pair in range(N_SGU_HEADS // 2):
                    cols = slice(pair * LANES, (pair + 1) * LANES)
                    dsg = dsg_ref[:, cols]
                    gate, gate_grad = _silu_and_grad(zs_ref[:, cols])
                    up = u[:, cols]
                    sgu["du"].append(dsg * mixed[pair] * gate)
                    sgu["dzs"].append(dsg * up * mixed[pair] * gate_grad)
                    dmixed = dsg * up * gate
                    gbias_ref[:, cols] += dmixed
                    sgu["dm"].append((jnp.where(low_sgu, dmixed, 0.0).astype(MXU_DTYPE),
                                      jnp.where(low_sgu, 0.0, dmixed).astype(MXU_DTYPE)))

            def sgu_grads():
                dvln_parts = []
                for pair in range(N_SGU_HEADS // 2):
                    dm_lo, dm_hi = sgu["dm"][pair]
                    vp = vln[:, pair * LANES:(pair + 1) * LANES]
                    gw_ref[2 * pair] += _dot(dm_lo, vp, NT)
                    gw_ref[2 * pair + 1] += _dot(dm_hi, vp, NT)
                    dvln_parts.append(_dot(wt_ref[2 * pair], dm_lo) + _dot(wt_ref[2 * pair + 1], dm_hi))
                dvln = jnp.concatenate(dvln_parts, axis=1)
                gln_ref[0:1, :] += jnp.sum(dvln * vhat, axis=0, keepdims=True)
                gln_ref[1:2, :] += jnp.sum(dvln, axis=0, keepdims=True)
                dvhat = dvln * lng
                dvg = rstd * (dvhat - jnp.mean(dvhat, axis=-1, keepdims=True)
                              - vhat * jnp.mean(dvhat * vhat, axis=-1, keepdims=True))
                dus = jnp.concatenate(sgu["du"], axis=1) * _gelu_grad(us)
                dvs = dvg * _gelu_grad(vs)
                dzs = jnp.concatenate(sgu["dzs"], axis=1)
                for k, val in enumerate((dus, dvs, dzs)):
                    dps_ref[:, k * SGU_W:(k + 1) * SGU_W] = val.astype(MXU_DTYPE)
                    gbins_ref[:, k * SGU_W:(k + 1) * SGU_W] += jnp.sum(val, axis=0, keepdims=True)

            valid = _window_mask(n)[0:BLOCK]
            low = lax.broadcasted_iota(jnp.int32, (BLOCK, LANES), 1) < HALF
            low_keys = lax.broadcasted_iota(jnp.int32, (2 * BLOCK, LANES), 1) < HALF
            lane_row = lax.broadcasted_iota(jnp.int32, (1, LANES), 1)
            gsink = jnp.zeros((1, LANES), F32)
            chains = [(g, par, i) for g in range(2) for par in range(2) for i in range(2)]
            kv = {(g, par): _kv_cat(kp_ref, kc_ref, 2 * g + par, False) for g in range(2) for par in range(2)}
            ones_keys = jnp.ones((2 * BLOCK, LANES), MXU_DTYPE)
            half_of_lane = lax.broadcasted_iota(jnp.int32, (LANES, 2 * LANES), 0) // HALF
            half_of_col = lax.broadcasted_iota(jnp.int32, (LANES, 2 * LANES), 1) // LANES
            sum_halves = (half_of_lane == half_of_col).astype(MXU_DTYPE)
            douts, deltas = [], []
            for pair in range(N_PAIRS):
                lanes = slice(pair * LANES, (pair + 1) * LANES)
                dg = d_ref[:, lanes]
                gate, gate_grad = _silu_and_grad(za_ref[:, lanes])
                o = o_ref[pair]
                dout = dg * gate
                dza = dg * o * gate_grad
                douts.append(dout.astype(MXU_DTYPE))
                deltas.append(_dot(dout * o, sum_halves))
                zl = slice(ATTN_W + pair * LANES, ATTN_W + (pair + 1) * LANES)
                pend_ref[:, zl] = dza.astype(MXU_DTYPE)
                gl = slice(GATE0 + pair * LANES, GATE0 + (pair + 1) * LANES)
                gbin_ref[:, gl] += jnp.sum(dza, axis=0, keepdims=True)

            first = {}

            def issue_first(k):
                g, par, i = chains[k]
                first[k] = (_dot(q_ref[2 * g + i], kv[g, par][0], NT), _dot(douts[2 * g + i], kv[g, par][1], NT))

            numerators = {}

            def issue_row_sums(k):
                g, par, i = chains[k]
                sink = sink_ref[4 * g + 2 * i + par]
                e, m = _softmax_numerator(jnp.where(valid, first[k][0], NEG_INF), sink)
                numerators[k] = (e, jnp.exp(sink - m), _dot(e, ones_keys))

            ahead = ATTN_BWD_AHEAD
            for k in range(ahead):
                issue_first(k)
            issue_row_sums(0)
            issue_row_sums(1)
            dqs, dk_parts, dv_parts = {}, {}, {}
            operands = {}

            def issue_last(k):
                g, par, i = chains[k]
                ds, ds_t, p_t = operands.pop(k)
                dq = _dot(ds, kv[g, par][0])
                dqs[g, i] = dq if par == 0 else dqs[g, i] + dq
                dk = _dot(ds_t, q_ref[2 * g + i])
                dv = _dot(p_t, douts[2 * g + i])
                dk_parts[g, par] = dk if i == 0 else dk_parts[g, par] + dk
                dv_parts[g, par] = dv if i == 0 else dv_parts[g, par] + dv

            for k, (g, par, i) in enumerate(chains):
                h = 4 * g + 2 * i + par
                delta = deltas[2 * g + i][:, par * LANES:(par + 1) * LANES]
                e, at_sink, row_sum = numerators[k]
                inv = 1.0 / (row_sum + at_sink)
                p = e * jnp.tile(inv, (1, 2))
                ds = p * (first[k][1] - jnp.tile(delta, (1, 2)))
                ds = ds.astype(MXU_DTYPE)
                operands[k] = (ds, ds.T, p.astype(MXU_DTYPE).T)
                total = jnp.sum(at_sink * inv * delta, axis=0, keepdims=True)
                gsink = jnp.where(lane_row == h, -total, gsink)
                if k + ahead < len(chains):
                    issue_first(k + ahead)
                if k + 2 < len(chains):
                    issue_row_sums(k + 2)
                if k > 0:
                    issue_last(k - 1)
                if k == SGU_GATES_AFTER_CHAIN:
                    sgu_gates()
                if k == SGU_GRADS_AFTER_CHAIN:
                    sgu_grads()
            issue_last(len(chains) - 1)
            for pair in range(N_PAIRS):
                g, i = divmod(pair, 2)
                dq = dqs[g, i] * SCALE
                lanes = slice(pair * LANES, (pair + 1) * LANES)
                pend_ref[:, lanes] = dq.astype(MXU_DTYPE)
                gbin_ref[:, lanes] += jnp.sum(dq, axis=0, keepdims=True)
            gsink_ref[...] += gsink
            for k, parts in enumerate((dk_parts, dv_parts)):
                masked = {key: jnp.where(low_keys if key[1] == 0 else jnp.logical_not(low_keys), val, 0.0)
                          for key, val in parts.items()}
                both = (masked[0, 0] + masked[1, 1]
                        + pltpu.roll(masked[0, 1] + masked[1, 0], HALF, 1))
                lanes = slice(k * KV_W, (k + 1) * KV_W)
                done = carry_ref[:, lanes] + both[0:BLOCK]
                dp_ref[:, KV0 + k * KV_W:KV0 + (k + 1) * KV_W] = done.astype(MXU_DTYPE)
                carry_ref[:, lanes] = both[BLOCK:]
                gbin_ref[:, KV0 + k * KV_W:KV0 + (k + 1) * KV_W] += jnp.sum(both, axis=0, keepdims=True)
            kp_ref[...] = kc_ref[...]

        @pl.when(n == last)
        def _():
            for hh in range(N_SGU_HEADS):
                gw_ref[hh] = jnp.where(tril, gw_ref[hh], 0.0)
            head_of_lane = lax.broadcasted_iota(jnp.int32, (N_SGU_HEADS, SGU_W), 1) // HEAD_DIM
            select = (head_of_lane == lax.broadcasted_iota(jnp.int32, (N_SGU_HEADS, SGU_W), 0)).astype(F32)
            gb_ref[...] = lax.dot_general(select, gbias_ref[...], NT, precision=lax.Precision.HIGHEST,
                                          preferred_element_type=F32)

    at = lambda n: jnp.minimum(n, last)
    blk = lambda w: pl.BlockSpec((BLOCK, w), lambda n: (at(n), 0))
    tiles = pl.BlockSpec((N_PAIRS, BLOCK, LANES), lambda n: (0, at(n), 0))
    section = lambda k: pl.BlockSpec((None, BLOCK, SGU_W), lambda n: (k, at(n), 0))
    return pl.pallas_call(
        body,
        name="mixers_bwd",
        grid=(N_BLOCKS + 1,),
        in_specs=[pl.BlockSpec(memory_space=pltpu.SMEM),
                  section(0),
                  tiles,
                  blk(KVX_W),
                  tiles,
                  section(0),
                  section(1),
                  section(1), section(2), section(3),
                  _full((1, SGU_W)), _full((1, SGU_W)), _full((N_SGU_HEADS, BLOCK, BLOCK)), _full((BLOCK, SGU_W)),
                  VMEM_SPEC],
        out_specs=(pl.BlockSpec((BLOCK, ATTN_SECTION), lambda n: (jnp.maximum(n - 1, 0), 0)),
                   _full((1, LANES)), _full((1, ATTN_SECTION)),
                   pl.BlockSpec((BLOCK, SGU_SECTION), lambda n: (at(n), 0)),
                   _full((N_SGU_HEADS, BLOCK, BLOCK)), _full((N_SGU_HEADS, BLOCK)),
                   _full((8, SGU_W)), _full((1, SGU_SECTION)), VMEM_SPEC),
        out_shape=(jax.ShapeDtypeStruct((SEQ, ATTN_SECTION), MXU_DTYPE),
                   jax.ShapeDtypeStruct((1, LANES), F32),
                   jax.ShapeDtypeStruct((1, ATTN_SECTION), F32),
                   jax.ShapeDtypeStruct((SEQ, SGU_SECTION), MXU_DTYPE),
                   jax.ShapeDtypeStruct((N_SGU_HEADS, BLOCK, BLOCK), F32),
                   jax.ShapeDtypeStruct((N_SGU_HEADS, BLOCK), F32),
                   jax.ShapeDtypeStruct((8, SGU_W), F32),
                   jax.ShapeDtypeStruct((1, SGU_SECTION), F32),
                   jax.ShapeDtypeStruct((WOUT_ROWS, D_MODEL), F32)),
        scratch_shapes=([pltpu.VMEM((BLOCK, KVX_W), MXU_DTYPE),
                         pltpu.VMEM((BLOCK, 2 * ATTN_W), MXU_DTYPE), pltpu.VMEM((BLOCK, 2 * KV_W), F32),
                         pltpu.VMEM((N_SGU_HEADS, BLOCK, BLOCK), MXU_DTYPE),
                         pltpu.VMEM((N_SGU_HEADS, BLOCK, BLOCK), MXU_DTYPE), pltpu.VMEM((BLOCK, SGU_W), F32)]
                        + _reduce_scatter_scratch(WOUT_ROWS, D_MODEL, COMM_DTYPE) + _dma_sems(REDUCE_SEMS)),
        compiler_params=_params(("arbitrary",), VMEM_LIMIT),
    )(sinks, dmix, q, kvx, out, gates, dmix, gates, gates, gates, ln_g, ln_b, sgu_w, bias_full, gwout)


WIN_REDUCE_PIECES = 11


def _in_proj_bwd(dpa, dps, win_t, x, norm_g, gres, gwin, vec_parts):
    tm = TOKEN_TILE
    steps = SEQ // tm
    n_parts = len(vec_parts)
    pieces = WIN_REDUCE_PIECES
    reduce_sems = 1 + 3 * pieces
    first_send = 2
    assert first_send + pieces < steps

    def body(da_ref, ds_ref, w_ref, x_ref, g_ref, gres_ref, gwin_ref, *rest):
        part_refs = rest[:n_parts]
        gx_ref, shard_ref, vec_out_ref, gng_ref, sa, ra, sb, rc, vec_ref, ra_vec, slots, send_sems, recv_sems = (
            rest[n_parts:])
        step = pl.program_id(0)
        copies = _Copies(send_sems, recv_sems)
        start, add, send, finish = _reduce_scatter_plan(copies, 0, gwin_ref, WIN_ROWS, sa, ra, sb, rc, shard_ref,
                                                        pieces=pieces)

        @pl.when(step == 0)
        def _():
            gng_ref[...] = jnp.zeros_like(gng_ref)
            start()

        pl.when(step == first_send)(add)

        @pl.when(jnp.logical_and(step >= first_send, step < first_send + pieces))
        def _():
            send(step - first_send)

        dh = _dot(da_ref[...], w_ref[0:ATTN_SECTION, :]) + _dot(ds_ref[...], w_ref[ATTN_SECTION:, :])
        xv = x_ref[...]
        r = lax.rsqrt(jnp.mean(xv * xv, axis=-1, keepdims=True) + NORM_EPS)
        xn = xv * r
        gng_ref[...] += jnp.sum(dh * xn, axis=0, keepdims=True)
        dxn = dh * g_ref[...]
        gx_ref[...] = r * (dxn - xn * jnp.mean(dxn * xn, axis=-1, keepdims=True)) + gres_ref[...]

        @pl.when(step == steps - 1)
        def _():
            finish()
            _all_reduce_vectors(copies, reduce_sems, gng_ref, *part_refs, vec_out_ref, vec_ref, ra_vec, slots)

    tile = lambda w: pl.BlockSpec((tm, w), lambda i: (i, 0))
    return pl.pallas_call(
        body,
        name="in_proj_bwd",
        grid=(steps,),
        in_specs=[tile(ATTN_SECTION), tile(SGU_SECTION), _full((IN_W, D_MODEL)), tile(D_MODEL),
                  _full((1, D_MODEL)), tile(D_MODEL), VMEM_SPEC] + [VMEM_SPEC] * n_parts,
        out_specs=(tile(D_MODEL), VMEM_SPEC, VMEM_SPEC),
        out_shape=(jax.ShapeDtypeStruct((SEQ, D_MODEL), F32),
                   jax.ShapeDtypeStruct((WIN_ROWS, D_MODEL), F32),
                   jax.ShapeDtypeStruct((VEC_ROWS, IN_W), F32)),
        scratch_shapes=([pltpu.VMEM((1, D_MODEL), F32)] + _reduce_scatter_scratch(WIN_ROWS, D_MODEL, COMM_DTYPE)
                        + _vector_scratch() + _dma_sems(reduce_sems + VECTOR_SEMS)),
        compiler_params=_params(("arbitrary",), VMEM_LIMIT),
    )(dpa, dps, win_t, x, norm_g, gres, gwin, *vec_parts)


def _win_grad(dpa, dps, h, gsguw):
    rows = 256
    n_attn = ATTN_SECTION // rows
    steps = n_attn + SGU_SECTION // rows

    def body(da_ref, ds_ref, h_ref, gsguw_ref, o_ref, sguw_full_ref, sa, ra, sb, rc, landing, send_sems, recv_sems):
        step = pl.program_id(0)
        copies = _Copies(send_sems, recv_sems)
        own_sguw = landing.at[_block_rows(_place(), SGUW_ROWS), :]
        start, exchange, finish = _reduce_scatter_plan(copies, 0, gsguw_ref, SGUW_ROWS, sa, ra, sb, rc, own_sguw)
        gather = _gather_plan(copies, REDUCE_SEMS, landing, SGUW_ROWS)

        pl.when(step == 0)(start)
        pl.when(step == 2)(exchange)

        @pl.when(step == 5)
        def _():
            finish()
            gather[0]()

        pl.when(step == 7)(gather[1])

        @pl.when(step < n_attn)
        def _():
            o_ref[...] = _dot(da_ref[...], h_ref[...], TN)

        @pl.when(step >= n_attn)
        def _():
            o_ref[...] = _dot(ds_ref[...], h_ref[...], TN)

        @pl.when(step == steps - 1)
        def _():
            gather[2]()
            sguw_full_ref[...] = landing[...]

    return pl.pallas_call(
        body,
        name="win_grad",
        grid=(steps,),
        in_specs=[pl.BlockSpec((SEQ, rows), lambda i: (0, jnp.minimum(i, n_attn - 1))),
                  pl.BlockSpec((SEQ, rows), lambda i: (0, jnp.maximum(i - n_attn, 0))),
                  _full((SEQ, D_MODEL)), VMEM_SPEC],
        out_specs=(pl.BlockSpec((rows, D_MODEL), lambda i: (i, 0)), _full((N_SGU_HEADS * BLOCK, BLOCK))),
        out_shape=(jax.ShapeDtypeStruct((IN_W, D_MODEL), F32),
                   jax.ShapeDtypeStruct((N_SGU_HEADS * BLOCK, BLOCK), F32)),
        scratch_shapes=(_reduce_scatter_scratch(SGUW_ROWS, BLOCK, F32)
                        + [pltpu.VMEM((N_SGU_HEADS * BLOCK, BLOCK), F32)]
                        + _dma_sems(REDUCE_SEMS + GATHER_SEMS)),
        compiler_params=_params(("arbitrary",), VMEM_LIMIT),
    )(dpa, dps, h, gsguw)


VEC_NORM_G, VEC_B_IN, VEC_SINKS, VEC_LN_G, VEC_LN_B, VEC_B_OUT, VEC_FINAL_G, VEC_LOSS, VEC_SGU_B = 0, 1, 2, 3, 4, 5, 6, 7, 8


def _adamw(w, g, m, v):
    m = ADAM_B1 * m + (1.0 - ADAM_B1) * g
    v = ADAM_B2 * v + (1.0 - ADAM_B2) * (g * g)
    m_hat = m / (1.0 - ADAM_B1 ** ADAM_STEP)
    v_hat = v / (1.0 - ADAM_B2 ** ADAM_STEP)
    delta = -ADAM_LR * (m_hat / (jnp.sqrt(v_hat) + ADAM_EPS) + ADAM_WD * w)
    return delta, m, v


def _adamw_shard(name, g, w, m, v, block_rows):
    def body(g_ref, w_ref, m_ref, v_ref, d_ref, nm_ref, nv_ref):
        d_ref[...], nm_ref[...], nv_ref[...] = _adamw(w_ref[...], g_ref[...], m_ref[...], v_ref[...])

    rows, cols = w.shape
    spec = pl.BlockSpec((block_rows, cols), lambda i: (i, 0))
    return pl.pallas_call(
        body,
        name=name,
        grid=(rows // block_rows,),
        in_specs=[spec] * 4,
        out_specs=(spec,) * 3,
        out_shape=(jax.ShapeDtypeStruct(w.shape, F32),) * 3,
        compiler_params=_params(("arbitrary",)),
    )(g, w, m, v)


VECTOR_SEMS = 4


def _vector_scratch():
    return [pltpu.VMEM((VEC_ROWS, IN_W), F32), pltpu.VMEM((VEC_ROWS, IN_W), F32),
            pltpu.VMEM((4 * VEC_ROWS, IN_W), F32)]


def _all_reduce_vectors(copies, sem0, gng_ref, gba_ref, gbs_ref, gsink_ref, gln_ref, gsgub_ref, vec4_ref, out_ref,
                        vec_ref, ra_vec, slots):
    x, y, c = _place()
    vec_ref[...] = jnp.zeros_like(vec_ref)
    vec_ref[VEC_NORM_G:VEC_NORM_G + 1, 0:D_MODEL] = gng_ref[...]
    vec_ref[VEC_B_IN:VEC_B_IN + 1, 0:ATTN_SECTION] = gba_ref[...]
    vec_ref[VEC_B_IN:VEC_B_IN + 1, ATTN_SECTION:IN_W] = gbs_ref[...]
    vec_ref[VEC_SINKS:VEC_SINKS + 1, 0:LANES] = gsink_ref[...]
    vec_ref[VEC_LN_G:VEC_LN_G + 1, 0:SGU_W] = gln_ref[0:1, :]
    vec_ref[VEC_LN_B:VEC_LN_B + 1, 0:SGU_W] = gln_ref[1:2, :]
    vec_ref[VEC_B_OUT:VEC_B_OUT + 1, 0:D_MODEL] = vec4_ref[2:3, :]
    vec_ref[VEC_FINAL_G:VEC_FINAL_G + 1, 0:D_MODEL] = vec4_ref[1:2, :]
    vec_ref[VEC_LOSS:VEC_LOSS + 1, 0:D_MODEL] = vec4_ref[0:1, :]
    vec_ref[VEC_SGU_B:VEC_SGU_B + N_SGU_HEADS, 0:BLOCK] = gsgub_ref[...]

    to_sibling = copies(sem0, vec_ref, ra_vec, (x, y, 1 - c))
    to_sibling.start()
    to_sibling.wait_recv()

    def chip_slot(place):
        return slots.at[pl.ds(pl.multiple_of((2 * place[0] + place[1]) * VEC_ROWS, 8), VEC_ROWS), :]

    mine = chip_slot((x, y))
    mine[...] = vec_ref[...] + ra_vec[...]
    to_chips = [copies(sem0 + i, mine, mine, (*_chip(rel), c)) for i, rel in enumerate(RELATIONS[1:], start=1)]
    for cp in to_chips:
        cp.start()
    for i, rel in enumerate(RELATIONS[1:], start=1):
        theirs = chip_slot(_chip(rel))
        copies(sem0 + i, theirs, theirs, (x, y, c)).wait_recv()
    out_ref[...] = ((slots[0:VEC_ROWS, :] + slots[VEC_ROWS:2 * VEC_ROWS, :])
                    + slots[2 * VEC_ROWS:3 * VEC_ROWS, :]) + slots[3 * VEC_ROWS:, :]
    to_sibling.wait_send()
    for cp in to_chips:
        cp.wait_send()


def _adamw_replicated(vec, gsguw, weights, m_state, v_state):
    n = len(SMALL)

    def body(*refs):
        vec_ref, gsguw_ref = refs[0], refs[1]
        w_refs, m_refs, v_refs = (refs[2 + k * n:2 + (k + 1) * n] for k in range(3))
        outs = refs[2 + 3 * n:]
        g_refs, d_refs, nm_refs, nv_refs = (outs[k * n:(k + 1) * n] for k in range(4))
        for i, (_, row, shape) in enumerate(SMALL):
            g = gsguw_ref[...] if row is None else vec_ref[row:row + shape[0], 0:shape[1]]
            g_refs[i][...] = g
            d_refs[i][...], nm_refs[i][...], nv_refs[i][...] = _adamw(
                w_refs[i][...], g, m_refs[i][...], v_refs[i][...])

    shapes = tuple(jax.ShapeDtypeStruct(shape, F32) for _, _, shape in SMALL)
    outs = pl.pallas_call(
        body,
        name="adamw_replicated",
        in_specs=[VMEM_SPEC] * (2 + 3 * n),
        out_specs=(VMEM_SPEC,) * (4 * n),
        out_shape=shapes * 4,
    )(vec, gsguw, *weights, *m_state, *v_state)
    return tuple(outs[k * n:(k + 1) * n] for k in range(4))


SMALL = (
    ("norm_g", VEC_NORM_G, (1, D_MODEL)),
    ("b_in", VEC_B_IN, (1, IN_W)),
    ("attn_sinks", VEC_SINKS, (1, N_Q_HEADS)),
    ("sgu_ln_g", VEC_LN_G, (1, SGU_W)),
    ("sgu_ln_b", VEC_LN_B, (1, SGU_W)),
    ("sgu_w", None, (N_SGU_HEADS * BLOCK, BLOCK)),
    ("sgu_b", VEC_SGU_B, (N_SGU_HEADS, BLOCK)),
    ("b_out", VEC_B_OUT, (1, D_MODEL)),
    ("final_norm_g", VEC_FINAL_G, (1, D_MODEL)),
)


def _local_grads(x, target, h, win_t, wout_shard, norm_g, b_in, attn_sinks, sgu_ln_g, sgu_ln_b, sgu_w, sgu_b, b_out,
                 final_g):
    sinks = attn_sinks.reshape(N_Q_HEADS)
    bias_full = jnp.repeat(sgu_b.T, HEAD_DIM, axis=1)
    q, kvx, gates, wout = _in_proj(h, b_in, win_t, wout_shard)
    out, gres, dmix, gwout, vec4 = _mixers_out_proj(sinks, q, kvx, gates, sgu_ln_g, sgu_ln_b, sgu_w, bias_full,
                                                    x, target, wout, b_out, final_g)
    dpa, gsink, gbin_a, dps, gsguw, gsgub, gln, gbin_s, gwout_shard = _mixers_bwd(
        sinks, dmix, q, kvx, out, gates, sgu_ln_g, sgu_ln_b, sgu_w, bias_full, gwout)
    gwin, gsguw_sum = _win_grad(dpa, dps, h, gsguw.reshape(N_SGU_HEADS * BLOCK, BLOCK))
    grad_x, gwin_shard, vec = _in_proj_bwd(dpa, dps, win_t, x, norm_g, gres, gwin,
                                           (gbin_a, gbin_s, gsink, gln, gsgub, vec4))
    return grad_x, gwin_shard, gwout_shard, gsguw_sum, vec


def kernel(x, norm_g, w_in, b_in, attn_sinks, sgu_ln_g, sgu_ln_b, sgu_w, sgu_b, w_out, b_out, final_norm_g, loss_target, m_norm_g, m_w_in, m_b_in, m_attn_sinks, m_sgu_ln_g, m_sgu_ln_b, m_sgu_w, m_sgu_b, m_w_out, m_b_out, m_final_norm_g, v_norm_g, v_w_in, v_b_in, v_attn_sinks, v_sgu_ln_g, v_sgu_ln_b, v_sgu_w, v_sgu_b, v_w_out, v_b_out, v_final_norm_g):
    given = dict(norm_g=norm_g, b_in=b_in, attn_sinks=attn_sinks, sgu_ln_g=sgu_ln_g, sgu_ln_b=sgu_ln_b,
                 sgu_w=sgu_w, sgu_b=sgu_b, b_out=b_out, final_norm_g=final_norm_g)
    m_given = dict(norm_g=m_norm_g, b_in=m_b_in, attn_sinks=m_attn_sinks, sgu_ln_g=m_sgu_ln_g,
                   sgu_ln_b=m_sgu_ln_b, sgu_w=m_sgu_w, sgu_b=m_sgu_b, b_out=m_b_out, final_norm_g=m_final_norm_g)
    v_given = dict(norm_g=v_norm_g, b_in=v_b_in, attn_sinks=v_attn_sinks, sgu_ln_g=v_sgu_ln_g,
                   sgu_ln_b=v_sgu_ln_b, sgu_w=v_sgu_w, sgu_b=v_sgu_b, b_out=v_b_out, final_norm_g=v_final_norm_g)

    win_t, h = _all_gather_win(w_in[0].T, x[0], norm_g)
    grad_x, gwin_t, gwout, gsguw, vec = _local_grads(
        x[0], loss_target[0], h, win_t, w_out[0], norm_g, b_in, attn_sinks, sgu_ln_g, sgu_ln_b, sgu_w[0], sgu_b[0],
        b_out, final_norm_g.reshape(1, D_MODEL))

    t = lambda a: a[0].T
    d_win, nm_win, nv_win = _adamw_shard("adamw_w_in", gwin_t, t(w_in), t(m_w_in), t(v_w_in), WIN_ROWS // 2)
    d_wout, nm_wout, nv_wout = _adamw_shard("adamw_w_out", gwout, w_out[0], m_w_out[0], v_w_out[0], WOUT_ROWS)
    as_2d = lambda d: [d[name].reshape(shape) for name, _, shape in SMALL]
    loss = vec[VEC_LOSS, 0]
    small = _adamw_replicated(vec, gsguw, as_2d(given), as_2d(m_given), as_2d(v_given))

    def assemble(big_in, big_out, k):
        vals = {name: small[k][i].reshape(given[name].shape) for i, (name, _, _) in enumerate(SMALL)}
        vals["w_in"] = big_in.T[None]
        vals["w_out"] = big_out[None]
        order = ("norm_g", "w_in", "b_in", "attn_sinks", "sgu_ln_g", "sgu_ln_b", "sgu_w", "sgu_b", "w_out",
                 "b_out", "final_norm_g")
        return [vals[name] for name in order]

    return (loss, grad_x[None],
            *assemble(gwin_t, gwout, 0), *assemble(d_win, d_wout, 1),
            *assemble(nm_win, nm_wout, 2), *assemble(nv_win, nv_wout, 3))
```

```python
import functools
import math

import jax
import jax.numpy as jnp
from jax import lax
from jax.experimental import pallas as pl
from jax.experimental.pallas import tpu as pltpu

F32 = jnp.float32
BF16 = jnp.bfloat16
MXU_DTYPE = BF16
COMM_DTYPE = BF16

D_MODEL = 1024
SEQ = 4096
HEAD_DIM = 64
N_Q_HEADS = 8
Q_PER_KV = 4
BLOCK = 128
N_BLOCKS = SEQ // BLOCK
ATTN_W = 512
KV_W = 128
SGU_W = 512
N_SGU_HEADS = 8
IN_W = 2816
NORM_EPS = 1e-5
NEG_INF = -1e30
SCALE = HEAD_DIM ** -0.5
KV0 = ATTN_W
GATE0 = ATTN_W + 2 * KV_W
SGU0 = GATE0 + ATTN_W
ATTN_SECTION = SGU0
SGU_SECTION = IN_W - SGU0

ADAM_LR = 0.001
ADAM_B1 = 0.9
ADAM_B2 = 0.999
ADAM_EPS = 1e-08
ADAM_WD = 0.01
ADAM_STEP = 10

N_DEV = 8
WIN_ROWS = IN_W // N_DEV
WOUT_ROWS = D_MODEL // N_DEV
SGUW_ROWS = N_SGU_HEADS * BLOCK // N_DEV
VEC_ROWS = 16
MESH = pl.DeviceIdType.MESH

LANES = 128
HALF = LANES // 2
N_PAIRS = N_Q_HEADS * HEAD_DIM // LANES
KVX_W = 12 * LANES
TOKEN_TILE = 256
FWD_TOKEN_TILE = 512
ATTN_FWD_AHEAD = 4
FUSED_BLOCKS = 2
SGU_MIX_AFTER_CHAIN = 0
SGU_GATES_AFTER_CHAIN = 1
SGU_GRADS_AFTER_CHAIN = 5
ATTN_BWD_AHEAD = 3
VMEM_LIMIT = 56 * 1024 * 1024

NN = (((1,), (0,)), ((), ()))
NT = (((1,), (1,)), ((), ()))
TN = (((0,), (0,)), ((), ()))


def _dot(a, b, dims=NN):
    return lax.dot_general(a.astype(MXU_DTYPE), b.astype(MXU_DTYPE), dims, preferred_element_type=F32)


def _gelu(x):
    return x * (lax.erf(x * (1.0 / math.sqrt(2.0))) + 1.0) * 0.5


def _gelu_grad(x):
    cdf = (lax.erf(x * (1.0 / math.sqrt(2.0))) + 1.0) * 0.5
    return cdf + x * jnp.exp(-0.5 * x * x) * (1.0 / math.sqrt(2.0 * math.pi))


def _silu_and_grad(z):
    s = jax.nn.sigmoid(z)
    return z * s, s * (1.0 + z * (1.0 - s))


def _params(semantics=None, vmem=None):
    kw = {}
    if semantics is not None:
        kw["dimension_semantics"] = semantics
    if vmem is not None:
        kw["vmem_limit_bytes"] = vmem
    return pltpu.CompilerParams(**kw)


def _full(shape):
    return pl.BlockSpec(shape, lambda *_: (0,) * len(shape))


VMEM_SPEC = pl.BlockSpec(memory_space=pltpu.VMEM)


RELATIONS = ((0, 0), (1, 0), (0, 1), (1, 1))


def _place():
    return lax.axis_index("x"), lax.axis_index("y"), lax.axis_index("c")


def _chip(rel):
    x, y, _ = _place()
    return (1 - x if rel[0] else x, 1 - y if rel[1] else y)


def _block_rows(place, n_rows):
    px, py, pc = place
    return pl.ds(pl.multiple_of((4 * px + 2 * py + pc) * n_rows, 16), n_rows)


class _Copies:
    def __init__(self, send_sems, recv_sems):
        self.send_sems, self.recv_sems = send_sems, recv_sems

    def __call__(self, k, src, dst, to):
        return pltpu.make_async_remote_copy(src_ref=src, dst_ref=dst, send_sem=self.send_sems.at[k],
                                            recv_sem=self.recv_sems.at[k], device_id=to, device_id_type=MESH)


def _gather_plan(copies, sem0, full_ref, n_rows):
    x, y, c = _place()
    me, sibling = (x, y, c), (x, y, 1 - c)
    chips = [_chip(rel) for rel in RELATIONS[1:]]

    def cp(k, block, to):
        rows = full_ref.at[_block_rows(block, n_rows), :]
        return copies(sem0 + k, rows, rows, to)

    first = [cp(0, me, sibling)] + [cp(1 + j, me, (*chip, c)) for j, chip in enumerate(chips)]
    passed = [cp(4 + j, (*chip, c), sibling) for j, chip in enumerate(chips)]

    def start():
        for f in first:
            f.start()

    def forward():
        for j, chip in enumerate(chips):
            cp(1 + j, (*chip, c), me).wait_recv()
            passed[j].start()

    def finish():
        cp(0, sibling, me).wait_recv()
        for j, chip in enumerate(chips):
            cp(4 + j, (*chip, 1 - c), me).wait_recv()
        for f in first + passed:
            f.wait_send()

    return start, forward, finish


GATHER_SEMS = 7


def _reduce_scatter_plan(copies, sem0, part_ref, n_rows, sa, ra, sb, rc, res_ref):
    x, y, c = _place()
    sibling = (x, y, 1 - c)
    n = n_rows
    level1 = copies(sem0, sa, ra, sibling)

    def level2(i):
        slot = pl.ds((i - 1) * n, n)
        return copies(sem0 + i, sb.at[slot, :], rc.at[slot, :], (*_chip(RELATIONS[i]), c))

    def start():
        for i, rel in enumerate(RELATIONS):
            sa[i * n:(i + 1) * n, :] = part_ref[_block_rows((*_chip(rel), 1 - c), n), :].astype(sa.dtype)
        level1.start()

    def exchange():
        level1.wait_recv()
        for i, rel in enumerate(RELATIONS):
            total = part_ref[_block_rows((*_chip(rel), c), n), :] + ra[i * n:(i + 1) * n, :].astype(F32)
            if i == 0:
                res_ref[...] = total
            else:
                sb[(i - 1) * n:i * n, :] = total.astype(sb.dtype)
                level2(i).start()

    def finish():
        acc = res_ref[...]
        for i in range(1, len(RELATIONS)):
            level2(i).wait_recv()
            acc = acc + rc[(i - 1) * n:i * n, :].astype(F32)
        res_ref[...] = acc
        level1.wait_send()
        for i in range(1, len(RELATIONS)):
            level2(i).wait_send()

    return start, exchange, finish


REDUCE_SEMS = 4


def _owner_sums_plan(copies, sem0, own_ref, sb, rc, res_ref):
    _, _, c = _place()
    n = own_ref.shape[0]

    def level2(i):
        slot = pl.ds((i - 1) * n, n)
        return copies(sem0 + i - 1, sb.at[slot, :], rc.at[slot, :], (*_chip(RELATIONS[i]), c))

    def send():
        for i in range(1, len(RELATIONS)):
            level2(i).start()

    def finish():
        acc = own_ref[...]
        for i in range(1, len(RELATIONS)):
            level2(i).wait_recv()
            acc = acc + rc[(i - 1) * n:i * n, :].astype(F32)
        res_ref[...] = acc
        for i in range(1, len(RELATIONS)):
            level2(i).wait_send()

    return send, finish


OWNER_SEMS = 3


def _reduce_scatter_scratch(n_rows, width, dtype):
    return [pltpu.VMEM((4 * n_rows, width), dtype), pltpu.VMEM((4 * n_rows, width), dtype),
            pltpu.VMEM((3 * n_rows, width), dtype), pltpu.VMEM((3 * n_rows, width), dtype)]


def _dma_sems(n):
    return [pltpu.SemaphoreType.DMA((n,)), pltpu.SemaphoreType.DMA((n,))]


def _all_gather_win(win_t_shard, x, norm_g):
    tm = FWD_TOKEN_TILE
    steps = SEQ // tm

    def body(win_ref, x_ref, g_ref, full_ref, h_ref, landing, send_sems, recv_sems):
        step = pl.program_id(0)
        start, forward, finish = _gather_plan(_Copies(send_sems, recv_sems), 0, landing, WIN_ROWS)

        @pl.when(step == 0)
        def _():
            landing[_block_rows(_place(), WIN_ROWS), :] = win_ref[...].astype(COMM_DTYPE)
            start()

        xv = x_ref[...]
        r = lax.rsqrt(jnp.mean(xv * xv, axis=-1, keepdims=True) + NORM_EPS)
        h_ref[...] = ((xv * r) * g_ref[...]).astype(MXU_DTYPE)

        @pl.when(step == steps - 1)
        def _():
            forward()
            finish()
            full_ref[...] = landing[...]

    return pl.pallas_call(
        body,
        name="all_gather_win",
        grid=(steps,),
        in_specs=[VMEM_SPEC, pl.BlockSpec((tm, D_MODEL), lambda i: (i, 0)), _full((1, D_MODEL))],
        out_specs=(_full((IN_W, D_MODEL)), pl.BlockSpec((tm, D_MODEL), lambda i: (i, 0))),
        out_shape=(jax.ShapeDtypeStruct((IN_W, D_MODEL), COMM_DTYPE),
                   jax.ShapeDtypeStruct((SEQ, D_MODEL), MXU_DTYPE)),
        scratch_shapes=[pltpu.VMEM((IN_W, D_MODEL), COMM_DTYPE)] + _dma_sems(GATHER_SEMS),
        compiler_params=_params(("arbitrary",), VMEM_LIMIT),
    )(win_t_shard, x, norm_g)


def _in_proj(h, b_in, win_t, wout_shard):
    tm = FWD_TOKEN_TILE
    steps = SEQ // tm

    def body(h_ref, b_ref, w_ref, wout_ref, q_ref, kvx_ref, gate_ref, wfull_ref, landing, send_sems, recv_sems):
        step = pl.program_id(0)
        start, forward, finish = _gather_plan(_Copies(send_sems, recv_sems), 0, landing, WOUT_ROWS)

        @pl.when(step == 0)
        def _():
            landing[_block_rows(_place(), WOUT_ROWS), :] = wout_ref[...].astype(COMM_DTYPE)
            start()

        pl.when(step == steps // 2)(forward)

        h = h_ref[...]

        def proj(lo, hi):
            return _dot(h, w_ref[lo:hi, :], NT) + b_ref[:, lo:hi]

        qs = proj(0, ATTN_W) * SCALE
        for pair in range(N_PAIRS):
            q_ref[pair] = qs[:, pair * LANES:(pair + 1) * LANES].astype(MXU_DTYPE)
        kv = proj(KV0, GATE0)
        low = lax.broadcasted_iota(jnp.int32, (tm, LANES), 1) < HALF
        for i in range(2):
            t = kv[:, i * LANES:(i + 1) * LANES]
            rot = pltpu.roll(t, HALF, 1)
            variants = (jnp.where(low, t, 0.0), jnp.where(low, 0.0, rot),
                        jnp.where(low, rot, 0.0), jnp.where(low, 0.0, t))
            for j, val in enumerate(variants):
                col = (4 * i + j) * LANES
                kvx_ref[:, col:col + LANES] = val.astype(MXU_DTYPE)
                if i == 1:
                    ones_elsewhere = jnp.where(low == (j % 2 == 0), val, 1.0)
                    kvx_ref[:, col + 4 * LANES:col + 5 * LANES] = ones_elsewhere.astype(MXU_DTYPE)
        for k in range(4):
            gate_ref[k] = proj(GATE0 + k * SGU_W, GATE0 + (k + 1) * SGU_W)

        @pl.when(step == steps - 1)
        def _():
            finish()
            wfull_ref[...] = landing[...]

    return pl.pallas_call(
        body,
        name="in_proj",
        grid=(steps,),
        in_specs=[pl.BlockSpec((tm, D_MODEL), lambda i: (i, 0)),
                  _full((1, IN_W)), _full((IN_W, D_MODEL)), VMEM_SPEC],
        out_specs=(pl.BlockSpec((N_PAIRS, tm, LANES), lambda i: (0, i, 0)),
                   pl.BlockSpec((tm, KVX_W), lambda i: (i, 0)),
                   pl.BlockSpec((4, tm, SGU_W), lambda i: (0, i, 0)),
                   _full((D_MODEL, D_MODEL))),
        out_shape=(jax.ShapeDtypeStruct((N_PAIRS, SEQ, LANES), MXU_DTYPE),
                   jax.ShapeDtypeStruct((SEQ, KVX_W), MXU_DTYPE),
                   jax.ShapeDtypeStruct((4, SEQ, SGU_W), F32),
                   jax.ShapeDtypeStruct((D_MODEL, D_MODEL), COMM_DTYPE)),
        scratch_shapes=[pltpu.VMEM((D_MODEL, D_MODEL), COMM_DTYPE)] + _dma_sems(GATHER_SEMS),
        compiler_params=_params(("arbitrary",), VMEM_LIMIT),
    )(h, b_in, win_t, wout_shard)


def _window_mask(n):
    qi = lax.broadcasted_iota(jnp.int32, (2 * BLOCK, 2 * BLOCK), 0) & (BLOCK - 1)
    p = lax.broadcasted_iota(jnp.int32, (2 * BLOCK, 2 * BLOCK), 1) - BLOCK
    in_window = jnp.logical_and(p <= qi, p > qi - BLOCK)
    return jnp.logical_and(in_window, jnp.logical_or(p >= 0, n > 0))


def _sink_column(sink_ref, g, par):
    return jnp.concatenate([jnp.full((BLOCK, 1), sink_ref[4 * g + par], F32),
                            jnp.full((BLOCK, 1), sink_ref[4 * g + 2 + par], F32)], axis=0)


def _kv_cat(kp_ref, kc_ref, var, with_ones):
    kcol, vcol = var * LANES, (var + (8 if with_ones else 4)) * LANES
    return (jnp.concatenate([kp_ref[:, kcol:kcol + LANES], kc_ref[:, kcol:kcol + LANES]], axis=0),
            jnp.concatenate([kp_ref[:, vcol:vcol + LANES], kc_ref[:, vcol:vcol + LANES]], axis=0))


def _softmax_numerator(s, sink):
    m = jnp.maximum(jnp.max(s, axis=1, keepdims=True), sink)
    return jnp.exp(s - m), m


def _mixers_out_proj(sinks, q, kvx, gates, ln_g, ln_b, sgu_w, bias_full, x, target, wout, b_out, final_g):
    tm = FUSED_BLOCKS * BLOCK
    n_tiles = SEQ // tm

    def body(sink_ref, q_ref, kc_ref, za_ref, us_ref, vs_ref, zs_ref, lng_ref, lnb_ref, w_ref, bias_ref,
             x_ref, t_ref, wout_ref, b_ref, gf_ref,
             out_ref, gres_ref, dmix_ref, gw_ref, vec_ref,
             kp_ref, wm_ref, mixed_next, mixed_cur, out_stage, gb_ref):
        step = pl.program_id(0)

        @pl.when(step == 0)
        def _():
            kp_ref[...] = jnp.zeros_like(kp_ref)
            _mask_sgu_weights(w_ref, wm_ref)
            gw_ref[...] = jnp.zeros_like(gw_ref)
            vec_ref[...] = jnp.zeros_like(vec_ref)
            mixed_cur[...] = jnp.zeros_like(mixed_cur)

        def mixers_block(b, after_chain=()):
            rows = slice(b * BLOCK, (b + 1) * BLOCK)
            kc = kc_ref.at[rows, :]
            u, _, _, vln = _sgu_activations(us_ref[rows, :], vs_ref[rows, :], lng_ref[...], lnb_ref[...])

            valid = _window_mask(step * FUSED_BLOCKS + b)[0:BLOCK]
            chains = [(g, par, i) for g in range(2) for par in range(2) for i in range(2)]
            kv = {(g, par): _kv_cat(kp_ref, kc, 2 * g + par, True) for g in range(2) for par in range(2)}
            scores, outs = {}, {}

            def issue_scores(k):
                g, par, i = chains[k]
                scores[k] = _dot(q_ref[2 * g + i, rows, :], kv[g, par][0], NT)

            ahead = ATTN_FWD_AHEAD
            for k in range(ahead):
                issue_scores(k)
            low = lax.broadcasted_iota(jnp.int32, (BLOCK, LANES), 1) < HALF
            for k, (g, par, i) in enumerate(chains):
                sink = sink_ref[4 * g + 2 * i + par]
                e, m = _softmax_numerator(jnp.where(valid, scores[k], NEG_INF), sink)
                if k + ahead < len(chains):
                    issue_scores(k + ahead)
                o = _dot(e, kv[g, par][1])
                outs[g, par, i] = o / (pltpu.roll(o, HALF, 1) + jnp.exp(sink - m))
                if k == SGU_MIX_AFTER_CHAIN:
                    mixed = _sgu_mix(vln, wm_ref, bias_ref)
                if k % 2 == 0 and k // 2 < len(after_chain):
                    after_chain[k // 2]()
            for pair in range(N_PAIRS):
                g, i = divmod(pair, 2)
                lanes = slice(pair * LANES, (pair + 1) * LANES)
                o = jnp.where(low, outs[g, 0, i], outs[g, 1, i])
                out_stage[pair, rows, :] = o
                gate, _ = _silu_and_grad(za_ref[rows, lanes])
                mixed_next[rows, lanes] = (o * gate).astype(MXU_DTYPE)
            kp_ref[...] = kc[...]
            for pair in range(N_SGU_HEADS // 2):
                cols = slice(pair * LANES, (pair + 1) * LANES)
                gate, _ = _silu_and_grad(zs_ref[rows, cols])
                mixed_next[rows, ATTN_W + pair * LANES:ATTN_W + (pair + 1) * LANES] = (
                    u[:, cols] * mixed[pair] * gate).astype(MXU_DTYPE)

        live = (step > 0).astype(F32)
        quarter = D_MODEL // 4
        columns = [None] * 4

        def project(j):
            def piece():
                columns[j] = _dot(mixed_cur[...], wout_ref[:, j * quarter:(j + 1) * quarter])
            return piece

        half_blocks = FUSED_BLOCKS // 2
        per_block = 4 // half_blocks
        for b in range(half_blocks):
            mixers_block(b, [project(j) for j in range(b * per_block, (b + 1) * per_block)])
        xo = x_ref[...] + jnp.concatenate(columns, axis=1) + b_ref[...]
        r = lax.rsqrt(jnp.mean(xo * xo, axis=-1, keepdims=True) + NORM_EPS)
        xn = xo * r
        gf = gf_ref[...]
        err = xn * gf - t_ref[...]
        loss = 0.5 * jnp.sum(jnp.mean(err * err, axis=-1, keepdims=True), axis=0, keepdims=True)
        dy = err * (1.0 / D_MODEL)
        dxn = dy * gf
        gres = r * (dxn - xn * jnp.mean(dxn * xn, axis=-1, keepdims=True))
        vec_ref[0:1, :] += jnp.broadcast_to(loss * live, (1, D_MODEL))
        vec_ref[1:2, :] += jnp.sum(dy * xn, axis=0, keepdims=True) * live
        vec_ref[2:3, :] += jnp.sum(gres, axis=0, keepdims=True) * live
        gres_ref[...] = gres
        gb_ref[...] = gres.astype(MXU_DTYPE)

        def branch_grad(k):
            def piece():
                dmix_ref[k] = _dot(gb_ref[...], wout_ref[k * ATTN_W:(k + 1) * ATTN_W, :], NT)
            return piece

        def weight_grad(k):
            def piece():
                rows = slice(k * ATTN_W, (k + 1) * ATTN_W)
                gw_ref[rows, :] += _dot(mixed_cur[:, rows], gb_ref[...], TN)
            return piece

        backward = [branch_grad(0), branch_grad(1), weight_grad(0), weight_grad(1)]
        for b in range(half_blocks):
            mixers_block(half_blocks + b, backward[b * per_block:(b + 1) * per_block])

        @pl.when(step < n_tiles)
        def _():
            out_ref[...] = out_stage[...]

        mixed_cur[...] = mixed_next[...]

    ahead_tile = lambda i: jnp.minimum(i, n_tiles - 1)
    behind_tile = lambda i: jnp.maximum(i - 1, 0)
    blk = lambda w: pl.BlockSpec((tm, w), lambda i: (ahead_tile(i), 0))
    tiles = pl.BlockSpec((N_PAIRS, tm, LANES), lambda i: (0, ahead_tile(i), 0))
    gate = lambda k: pl.BlockSpec((None, tm, SGU_W), lambda i: (k, ahead_tile(i), 0))
    behind = lambda w: pl.BlockSpec((tm, w), lambda i: (behind_tile(i), 0))
    return pl.pallas_call(
        body,
        name="mixers_out_proj",
        grid=(n_tiles + 1,),
        in_specs=[pl.BlockSpec(memory_space=pltpu.SMEM), tiles, blk(KVX_W), gate(0), gate(1), gate(2), gate(3),
                  _full((1, SGU_W)), _full((1, SGU_W)), _full((N_SGU_HEADS, BLOCK, BLOCK)), _full((BLOCK, SGU_W)),
                  behind(D_MODEL), behind(D_MODEL), _full((D_MODEL, D_MODEL)), _full((1, D_MODEL)),
                  _full((1, D_MODEL))],
        out_specs=(tiles, behind(D_MODEL), pl.BlockSpec((2, tm, ATTN_W), lambda i: (0, behind_tile(i), 0)),
                   _full((D_MODEL, D_MODEL)), _full((8, D_MODEL))),
        out_shape=(jax.ShapeDtypeStruct((N_PAIRS, SEQ, LANES), F32),
                   jax.ShapeDtypeStruct((SEQ, D_MODEL), F32),
                   jax.ShapeDtypeStruct((2, SEQ, ATTN_W), F32),
                   jax.ShapeDtypeStruct((D_MODEL, D_MODEL), F32),
                   jax.ShapeDtypeStruct((8, D_MODEL), F32)),
        scratch_shapes=[pltpu.VMEM((BLOCK, KVX_W), MXU_DTYPE), pltpu.VMEM((N_SGU_HEADS, BLOCK, BLOCK), MXU_DTYPE),
                        pltpu.VMEM((tm, D_MODEL), MXU_DTYPE), pltpu.VMEM((tm, D_MODEL), MXU_DTYPE),
                        pltpu.VMEM((N_PAIRS, tm, LANES), F32), pltpu.VMEM((tm, D_MODEL), MXU_DTYPE)],
        compiler_params=_params(("arbitrary",), VMEM_LIMIT),
    )(sinks, q, kvx, gates, gates, gates, gates, ln_g, ln_b, sgu_w, bias_full, x, target, wout, b_out, final_g)


def _sgu_activations(us, vs, lng, lnb):
    u = _gelu(us)
    vg = _gelu(vs)
    mu = jnp.mean(vg, axis=-1, keepdims=True)
    xc = vg - mu
    rstd = lax.rsqrt(jnp.mean(xc * xc, axis=-1, keepdims=True) + NORM_EPS)
    vhat = xc * rstd
    return u, vhat, rstd, vhat * lng + lnb


def _mask_sgu_weights(w_ref, masked_ref, transposed_ref=None):
    tril = (lax.broadcasted_iota(jnp.int32, (BLOCK, BLOCK), 0)
            >= lax.broadcasted_iota(jnp.int32, (BLOCK, BLOCK), 1))
    for hh in range(N_SGU_HEADS):
        w = jnp.where(tril, w_ref[hh], 0.0)
        masked_ref[hh] = w.astype(MXU_DTYPE)
        if transposed_ref is not None:
            transposed_ref[hh] = w.T.astype(MXU_DTYPE)


def _sgu_mix(vln, masked_w_ref, bias_ref):
    low = lax.broadcasted_iota(jnp.int32, (BLOCK, LANES), 1) < HALF
    mixed = []
    for pair in range(N_SGU_HEADS // 2):
        vp = vln[:, pair * LANES:(pair + 1) * LANES]
        mixed.append(_dot(masked_w_ref[2 * pair], jnp.where(low, vp, 0.0))
                     + _dot(masked_w_ref[2 * pair + 1], jnp.where(low, 0.0, vp))
                     + bias_ref[:, pair * LANES:(pair + 1) * LANES])
    return mixed


def _mixers_bwd(sinks, dmix, q, kvx, out, gates, ln_g, ln_b, sgu_w, bias_full, gwout):
    last = N_BLOCKS - 1

    def body(sink_ref, d_ref, q_ref, kc_ref, o_ref, za_ref, dsg_ref, us_ref, vs_ref, zs_ref, lng_ref, lnb_ref, w_ref,
             bias_ref, gwout_ref,
             dp_ref, gsink_ref, gbin_ref, dps_ref, gw_ref, gb_ref, gln_ref, gbins_ref, wout_shard_ref,
             kp_ref, pend_ref, carry_ref, wm_ref, wt_ref, gbias_ref, sa_w, ra_w, sb_w, rc_w, send_sems, recv_sems):
        n = pl.program_id(0)
        start, exchange, finish = _reduce_scatter_plan(_Copies(send_sems, recv_sems), 0, gwout_ref, WOUT_ROWS,
                                                       sa_w, ra_w, sb_w, rc_w, wout_shard_ref)
        tril = (lax.broadcasted_iota(jnp.int32, (BLOCK, BLOCK), 0)
                >= lax.broadcasted_iota(jnp.int32, (BLOCK, BLOCK), 1))

        @pl.when(n == 0)
        def _():
            gsink_ref[...] = jnp.zeros_like(gsink_ref)
            gbin_ref[...] = jnp.zeros_like(gbin_ref)
            carry_ref[...] = jnp.zeros_like(carry_ref)
            kp_ref[...] = jnp.zeros_like(kp_ref)
            gw_ref[...] = jnp.zeros_like(gw_ref)
            gln_ref[...] = jnp.zeros_like(gln_ref)
            gbins_ref[...] = jnp.zeros_like(gbins_ref)
            gbias_ref[...] = jnp.zeros_like(gbias_ref)
            _mask_sgu_weights(w_ref, wm_ref, wt_ref)
            start()

        pl.when(n == 3)(exchange)
        pl.when(n == 12)(finish)

        @pl.when(n > 0)
        def _():
            dp_ref[:, 0:ATTN_W] = pend_ref[:, 0:ATTN_W]
            dp_ref[:, GATE0:ATTN_SECTION] = pend_ref[:, ATTN_W:]

        @pl.when(n > last)
        def _():
            dp_ref[:, KV0:GATE0] = carry_ref[...].astype(MXU_DTYPE)

        @pl.when(n <= last)
        def _():
            us = us_ref[...]
            vs = vs_ref[...]
            lng = lng_ref[...]
            u, vhat, rstd, vln = _sgu_activations(us, vs, lng, lnb_ref[...])
            low_sgu = lax.broadcasted_iota(jnp.int32, (BLOCK, LANES), 1) < HALF
            sgu = {}

            def sgu_gates():
                mixed = _sgu_mix(vln, wm_ref, bias_ref)
                sgu["du"], sgu["dzs"], sgu["dm"] = [], [], []
                for pair in range(N_SGU_HEADS // 2):
                    cols = slice(pair * LANES, (pair + 1) * LANES)
                    dsg = dsg_ref[:, cols]
                    gate, gate_grad = _silu_and_grad(zs_ref[:, cols])
                    up = u[:, cols]
                    sgu["du"].append(dsg * mixed[pair] * gate)
                    sgu["dzs"].append(dsg * up * mixed[pair] * gate_grad)
                    dmixed = dsg * up * gate
                    gbias_ref[:, cols] += dmixed
                    sgu["dm"].append((jnp.where(low_sgu, dmixed, 0.0).astype(MXU_DTYPE),
                                      jnp.where(low_sgu, 0.0, dmixed).astype(MXU_DTYPE)))

            def sgu_grads():
                dvln_parts = []
                for pair in range(N_SGU_HEADS // 2):
                    dm_lo, dm_hi = sgu["dm"][pair]
                    vp = vln[:, pair * LANES:(pair + 1) * LANES]
                    gw_ref[2 * pair] += _dot(dm_lo, vp, NT)
                    gw_ref[2 * pair + 1] += _dot(dm_hi, vp, NT)
                    dvln_parts.append(_dot(wt_ref[2 * pair], dm_lo) + _dot(wt_ref[2 * pair + 1], dm_hi))
                dvln = jnp.concatenate(dvln_parts, axis=1)
                gln_ref[0:1, :] += jnp.sum(dvln * vhat, axis=0, keepdims=True)
                gln_ref[1:2, :] += jnp.sum(dvln, axis=0, keepdims=True)
                dvhat = dvln * lng
                dvg = rstd * (dvhat - jnp.mean(dvhat, axis=-1, keepdims=True)
                              - vhat * jnp.mean(dvhat * vhat, axis=-1, keepdims=True))
                dus = jnp.concatenate(sgu["du"], axis=1) * _gelu_grad(us)
                dvs = dvg * _gelu_grad(vs)
                dzs = jnp.concatenate(sgu["dzs"], axis=1)
                for k, val in enumerate((dus, dvs, dzs)):
                    dps_ref[:, k * SGU_W:(k + 1) * SGU_W] = val.astype(MXU_DTYPE)
                    gbins_ref[:, k * SGU_W:(k + 1) * SGU_W] += jnp.sum(val, axis=0, keepdims=True)

            valid = _window_mask(n)[0:BLOCK]
            low = lax.broadcasted_iota(jnp.int32, (BLOCK, LANES), 1) < HALF
            low_keys = lax.broadcasted_iota(jnp.int32, (2 * BLOCK, LANES), 1) < HALF
            lane_row = lax.broadcasted_iota(jnp.int32, (1, LANES), 1)
            gsink = jnp.zeros((1, LANES), F32)
            chains = [(g, par, i) for g in range(2) for par in range(2) for i in range(2)]
            kv = {(g, par): _kv_cat(kp_ref, kc_ref, 2 * g + par, False) for g in range(2) for par in range(2)}
            ones_keys = jnp.ones((2 * BLOCK, LANES), MXU_DTYPE)
            half_of_lane = lax.broadcasted_iota(jnp.int32, (LANES, 2 * LANES), 0) // HALF
            half_of_col = lax.broadcasted_iota(jnp.int32, (LANES, 2 * LANES), 1) // LANES
            sum_halves = (half_of_lane == half_of_col).astype(MXU_DTYPE)
            douts, deltas = [], []
            for pair in range(N_PAIRS):
                lanes = slice(pair * LANES, (pair + 1) * LANES)
                dg = d_ref[:, lanes]
                gate, gate_grad = _silu_and_grad(za_ref[:, lanes])
                o = o_ref[pair]
                dout = dg * gate
                dza = dg * o * gate_grad
                douts.append(dout.astype(MXU_DTYPE))
                deltas.append(_dot(dout * o, sum_halves))
                zl = slice(ATTN_W + pair * LANES, ATTN_W + (pair + 1) * LANES)
                pend_ref[:, zl] = dza.astype(MXU_DTYPE)
                gl = slice(GATE0 + pair * LANES, GATE0 + (pair + 1) * LANES)
                gbin_ref[:, gl] += jnp.sum(dza, axis=0, keepdims=True)

            first = {}

            def issue_first(k):
                g, par, i = chains[k]
                first[k] = (_dot(q_ref[2 * g + i], kv[g, par][0], NT), _dot(douts[2 * g + i], kv[g, par][1], NT))

            numerators = {}

            def issue_row_sums(k):
                g, par, i = chains[k]
                sink = sink_ref[4 * g + 2 * i + par]
                e, m = _softmax_numerator(jnp.where(valid, first[k][0], NEG_INF), sink)
                numerators[k] = (e, jnp.exp(sink - m), _dot(e, ones_keys))

            ahead = ATTN_BWD_AHEAD
            for k in range(ahead):
                issue_first(k)
            issue_row_sums(0)
            issue_row_sums(1)
            dqs, dk_parts, dv_parts = {}, {}, {}
            operands = {}

            def issue_last(k):
                g, par, i = chains[k]
                ds, ds_t, p_t = operands.pop(k)
                dq = _dot(ds, kv[g, par][0])
                dqs[g, i] = dq if par == 0 else dqs[g, i] + dq
                dk = _dot(ds_t, q_ref[2 * g + i])
                dv = _dot(p_t, douts[2 * g + i])
                dk_parts[g, par] = dk if i == 0 else dk_parts[g, par] + dk
                dv_parts[g, par] = dv if i == 0 else dv_parts[g, par] + dv

            for k, (g, par, i) in enumerate(chains):
                h = 4 * g + 2 * i + par
                delta = deltas[2 * g + i][:, par * LANES:(par + 1) * LANES]
                e, at_sink, row_sum = numerators[k]
                inv = 1.0 / (row_sum + at_sink)
                p = e * jnp.tile(inv, (1, 2))
                ds = p * (first[k][1] - jnp.tile(delta, (1, 2)))
                ds = ds.astype(MXU_DTYPE)
                operands[k] = (ds, ds.T, p.astype(MXU_DTYPE).T)
                total = jnp.sum(at_sink * inv * delta, axis=0, keepdims=True)
                gsink = jnp.where(lane_row == h, -total, gsink)
                if k + ahead < len(chains):
                    issue_first(k + ahead)
                if k + 2 < len(chains):
                    issue_row_sums(k + 2)
                if k > 0:
                    issue_last(k - 1)
                if k == SGU_GATES_AFTER_CHAIN:
                    sgu_gates()
                if k == SGU_GRADS_AFTER_CHAIN:
                    sgu_grads()
            issue_last(len(chains) - 1)
            for pair in range(N_PAIRS):
                g, i = divmod(pair, 2)
                dq = dqs[g, i] * SCALE
                lanes = slice(pair * LANES, (pair + 1) * LANES)
                pend_ref[:, lanes] = dq.astype(MXU_DTYPE)
                gbin_ref[:, lanes] += jnp.sum(dq, axis=0, keepdims=True)
            gsink_ref[...] += gsink
            for k, parts in enumerate((dk_parts, dv_parts)):
                masked = {key: jnp.where(low_keys if key[1] == 0 else jnp.logical_not(low_keys), val, 0.0)
                          for key, val in parts.items()}
                both = (masked[0, 0] + masked[1, 1]
                        + pltpu.roll(masked[0, 1] + masked[1, 0], HALF, 1))
                lanes = slice(k * KV_W, (k + 1) * KV_W)
                done = carry_ref[:, lanes] + both[0:BLOCK]
                dp_ref[:, KV0 + k * KV_W:KV0 + (k + 1) * KV_W] = done.astype(MXU_DTYPE)
                carry_ref[:, lanes] = both[BLOCK:]
                gbin_ref[:, KV0 + k * KV_W:KV0 + (k + 1) * KV_W] += jnp.sum(both, axis=0, keepdims=True)
            kp_ref[...] = kc_ref[...]

        @pl.when(n == last)
        def _():
            for hh in range(N_SGU_HEADS):
                gw_ref[hh] = jnp.where(tril, gw_ref[hh], 0.0)
            head_of_lane = lax.broadcasted_iota(jnp.int32, (N_SGU_HEADS, SGU_W), 1) // HEAD_DIM
            select = (head_of_lane == lax.broadcasted_iota(jnp.int32, (N_SGU_HEADS, SGU_W), 0)).astype(F32)
            gb_ref[...] = lax.dot_general(select, gbias_ref[...], NT, precision=lax.Precision.HIGHEST,
                                          preferred_element_type=F32)

    at = lambda n: jnp.minimum(n, last)
    blk = lambda w: pl.BlockSpec((BLOCK, w), lambda n: (at(n), 0))
    tiles = pl.BlockSpec((N_PAIRS, BLOCK, LANES), lambda n: (0, at(n), 0))
    section = lambda k: pl.BlockSpec((None, BLOCK, SGU_W), lambda n: (k, at(n), 0))
    return pl.pallas_call(
        body,
        name="mixers_bwd",
        grid=(N_BLOCKS + 1,),
        in_specs=[pl.BlockSpec(memory_space=pltpu.SMEM),
                  section(0),
                  tiles,
                  blk(KVX_W),
                  tiles,
                  section(0),
                  section(1),
                  section(1), section(2), section(3),
                  _full((1, SGU_W)), _full((1, SGU_W)), _full((N_SGU_HEADS, BLOCK, BLOCK)), _full((BLOCK, SGU_W)),
                  VMEM_SPEC],
        out_specs=(pl.BlockSpec((BLOCK, ATTN_SECTION), lambda n: (jnp.maximum(n - 1, 0), 0)),
                   _full((1, LANES)), _full((1, ATTN_SECTION)),
                   pl.BlockSpec((BLOCK, SGU_SECTION), lambda n: (at(n), 0)),
                   _full((N_SGU_HEADS, BLOCK, BLOCK)), _full((N_SGU_HEADS, BLOCK)),
                   _full((8, SGU_W)), _full((1, SGU_SECTION)), VMEM_SPEC),
        out_shape=(jax.ShapeDtypeStruct((SEQ, ATTN_SECTION), MXU_DTYPE),
                   jax.ShapeDtypeStruct((1, LANES), F32),
                   jax.ShapeDtypeStruct((1, ATTN_SECTION), F32),
                   jax.ShapeDtypeStruct((SEQ, SGU_SECTION), MXU_DTYPE),
                   jax.ShapeDtypeStruct((N_SGU_HEADS, BLOCK, BLOCK), F32),
                   jax.ShapeDtypeStruct((N_SGU_HEADS, BLOCK), F32),
                   jax.ShapeDtypeStruct((8, SGU_W), F32),
                   jax.ShapeDtypeStruct((1, SGU_SECTION), F32),
                   jax.ShapeDtypeStruct((WOUT_ROWS, D_MODEL), F32)),
        scratch_shapes=([pltpu.VMEM((BLOCK, KVX_W), MXU_DTYPE),
                         pltpu.VMEM((BLOCK, 2 * ATTN_W), MXU_DTYPE), pltpu.VMEM((BLOCK, 2 * KV_W), F32),
                         pltpu.VMEM((N_SGU_HEADS, BLOCK, BLOCK), MXU_DTYPE),
                         pltpu.VMEM((N_SGU_HEADS, BLOCK, BLOCK), MXU_DTYPE), pltpu.VMEM((BLOCK, SGU_W), F32)]
                        + _reduce_scatter_scratch(WOUT_ROWS, D_MODEL, COMM_DTYPE) + _dma_sems(REDUCE_SEMS)),
        compiler_params=_params(("arbitrary",), VMEM_LIMIT),
    )(sinks, dmix, q, kvx, out, gates, dmix, gates, gates, gates, ln_g, ln_b, sgu_w, bias_full, gwout)


def _in_proj_bwd(dpa, dps, win_t, x, norm_g, gres, gwin_own, gwin_others, vec_parts):
    tm = TOKEN_TILE
    steps = SEQ // tm
    n_parts = len(vec_parts)

    def body(da_ref, ds_ref, w_ref, x_ref, g_ref, gres_ref, own_ref, others_ref, *rest):
        part_refs = rest[:n_parts]
        gx_ref, shard_ref, vec_out_ref, gng_ref, rc, vec_ref, ra_vec, slots, send_sems, recv_sems = rest[n_parts:]
        step = pl.program_id(0)
        copies = _Copies(send_sems, recv_sems)
        send, finish = _owner_sums_plan(copies, 0, own_ref, others_ref, rc, shard_ref)

        @pl.when(step == 0)
        def _():
            gng_ref[...] = jnp.zeros_like(gng_ref)
            send()

        dh = _dot(da_ref[...], w_ref[0:ATTN_SECTION, :]) + _dot(ds_ref[...], w_ref[ATTN_SECTION:, :])
        xv = x_ref[...]
        r = lax.rsqrt(jnp.mean(xv * xv, axis=-1, keepdims=True) + NORM_EPS)
        xn = xv * r
        gng_ref[...] += jnp.sum(dh * xn, axis=0, keepdims=True)
        dxn = dh * g_ref[...]
        gx_ref[...] = r * (dxn - xn * jnp.mean(dxn * xn, axis=-1, keepdims=True)) + gres_ref[...]

        @pl.when(step == steps - 1)
        def _():
            finish()
            _all_reduce_vectors(copies, OWNER_SEMS, gng_ref, *part_refs, vec_out_ref, vec_ref, ra_vec, slots)

    tile = lambda w: pl.BlockSpec((tm, w), lambda i: (i, 0))
    return pl.pallas_call(
        body,
        name="in_proj_bwd",
        grid=(steps,),
        in_specs=[tile(ATTN_SECTION), tile(SGU_SECTION), _full((IN_W, D_MODEL)), tile(D_MODEL),
                  _full((1, D_MODEL)), tile(D_MODEL), VMEM_SPEC, VMEM_SPEC] + [VMEM_SPEC] * n_parts,
        out_specs=(tile(D_MODEL), VMEM_SPEC, VMEM_SPEC),
        out_shape=(jax.ShapeDtypeStruct((SEQ, D_MODEL), F32),
                   jax.ShapeDtypeStruct((WIN_ROWS, D_MODEL), F32),
                   jax.ShapeDtypeStruct((VEC_ROWS, IN_W), F32)),
        scratch_shapes=([pltpu.VMEM((1, D_MODEL), F32), pltpu.VMEM((3 * WIN_ROWS, D_MODEL), COMM_DTYPE)]
                        + _vector_scratch() + _dma_sems(OWNER_SEMS + VECTOR_SEMS)),
        compiler_params=_params(("arbitrary",), VMEM_LIMIT),
    )(dpa, dps, win_t, x, norm_g, gres, gwin_own, gwin_others, *vec_parts)


def _win_grad_pieces(rows):
    pieces = []
    for step in range(IN_W // rows):
        for owner in range(N_DEV):
            lo, hi = max(step * rows, owner * WIN_ROWS), min((step + 1) * rows, (owner + 1) * WIN_ROWS)
            if lo < hi:
                pieces.append((len(pieces), step, owner, lo, hi - lo))
    return pieces


def _win_grad(dpa, dps, h, gsguw):
    rows = 256
    n_attn = ATTN_SECTION // rows
    steps = IN_W // rows
    pieces = _win_grad_pieces(rows)
    class_rows = (N_DEV // 2) * WIN_ROWS

    def body(da_ref, ds_ref, h_ref, gsguw_ref, own_ref, others_ref, sguw_full_ref,
             chunks, sa, ra, sa_s, ra_s, sb_s, rc_s, landing, send_sems, recv_sems, give_sems, take_sems):
        step = pl.program_id(0)
        x, y, c = _place()
        copies = _Copies(send_sems, recv_sems)
        own_sguw = landing.at[_block_rows((x, y, c), SGUW_ROWS), :]
        start, exchange, finish = _reduce_scatter_plan(copies, 0, gsguw_ref, SGUW_ROWS, sa_s, ra_s, sb_s, rc_s,
                                                       own_sguw)
        gather = _gather_plan(copies, REDUCE_SEMS, landing, SGUW_ROWS)

        def class_rows_of(owner, first, n):
            return pl.ds((owner // 2) * WIN_ROWS + first - owner * WIN_ROWS, n)

        def to_sibling(piece):
            k, _, owner, first, n = piece
            at = class_rows_of(owner, first, n)
            return pltpu.make_async_remote_copy(src_ref=sa.at[at, :], dst_ref=ra.at[at, :], send_sem=give_sems.at[k],
                                                recv_sem=take_sems.at[k], device_id=(x, y, 1 - c), device_id_type=MESH)

        def give(piece):
            k, at_step, owner, first, n = piece

            @pl.when(c != owner % 2)
            def _():
                sa[class_rows_of(owner, first, n), :] = chunks[at_step % 2, pl.ds(first % rows, n), :].astype(sa.dtype)
                to_sibling(piece).start()

        def keep(piece):
            k, at_step, owner, first, n = piece
            px, py = owner // 4, (owner // 2) % 2

            @pl.when(c == owner % 2)
            def _():
                to_sibling(piece).wait_recv()
                total = (chunks[at_step % 2, pl.ds(first % rows, n), :]
                         + ra[class_rows_of(owner, first, n), :].astype(F32))
                relation = (x + px - 2 * x * px) + 2 * (y + py - 2 * y * py)

                @pl.when(relation == 0)
                def _():
                    own_ref[pl.ds(first - owner * WIN_ROWS, n), :] = total

                @pl.when(relation != 0)
                def _():
                    at = pl.multiple_of((relation - 1) * WIN_ROWS + first - owner * WIN_ROWS, 16)
                    others_ref[pl.ds(at, n), :] = total.astype(others_ref.dtype)

        pl.when(step == 0)(start)
        pl.when(step == 2)(exchange)

        @pl.when(step == 5)
        def _():
            finish()
            gather[0]()

        pl.when(step == 7)(gather[1])

        @pl.when(step < n_attn)
        def _():
            chunks[step % 2] = _dot(da_ref[...], h_ref[...], TN)

        @pl.when(step >= n_attn)
        def _():
            chunks[step % 2] = _dot(ds_ref[...], h_ref[...], TN)

        for at_step in range(steps):
            @pl.when(step == at_step)
            def _():
                for piece in pieces:
                    if piece[1] == at_step:
                        give(piece)
                    if piece[1] == at_step - 1:
                        keep(piece)

        @pl.when(step == steps - 1)
        def _():
            for piece in pieces:
                if piece[1] == steps - 1:
                    keep(piece)
            for piece in pieces:
                pl.when(c != piece[2] % 2)(to_sibling(piece).wait_send)
            gather[2]()
            sguw_full_ref[...] = landing[...]

    return pl.pallas_call(
        body,
        name="win_grad",
        grid=(steps,),
        in_specs=[pl.BlockSpec((SEQ, rows), lambda i: (0, jnp.minimum(i, n_attn - 1))),
                  pl.BlockSpec((SEQ, rows), lambda i: (0, jnp.maximum(i - n_attn, 0))),
                  _full((SEQ, D_MODEL)), VMEM_SPEC],
        out_specs=(VMEM_SPEC, VMEM_SPEC, _full((N_SGU_HEADS * BLOCK, BLOCK))),
        out_shape=(jax.ShapeDtypeStruct((WIN_ROWS, D_MODEL), F32),
                   jax.ShapeDtypeStruct((3 * WIN_ROWS, D_MODEL), COMM_DTYPE),
                   jax.ShapeDtypeStruct((N_SGU_HEADS * BLOCK, BLOCK), F32)),
        scratch_shapes=([pltpu.VMEM((2, rows, D_MODEL), F32),
                         pltpu.VMEM((class_rows, D_MODEL), COMM_DTYPE), pltpu.VMEM((class_rows, D_MODEL), COMM_DTYPE)]
                        + _reduce_scatter_scratch(SGUW_ROWS, BLOCK, F32)
                        + [pltpu.VMEM((N_SGU_HEADS * BLOCK, BLOCK), F32)]
                        + _dma_sems(REDUCE_SEMS + GATHER_SEMS) + _dma_sems(len(pieces))),
        compiler_params=_params(("arbitrary",), VMEM_LIMIT),
    )(dpa, dps, h, gsguw)


VEC_NORM_G, VEC_B_IN, VEC_SINKS, VEC_LN_G, VEC_LN_B, VEC_B_OUT, VEC_FINAL_G, VEC_LOSS, VEC_SGU_B = 0, 1, 2, 3, 4, 5, 6, 7, 8


def _adamw(w, g, m, v):
    m = ADAM_B1 * m + (1.0 - ADAM_B1) * g
    v = ADAM_B2 * v + (1.0 - ADAM_B2) * (g * g)
    m_hat = m / (1.0 - ADAM_B1 ** ADAM_STEP)
    v_hat = v / (1.0 - ADAM_B2 ** ADAM_STEP)
    delta = -ADAM_LR * (m_hat / (jnp.sqrt(v_hat) + ADAM_EPS) + ADAM_WD * w)
    return delta, m, v


def _adamw_shard(name, g, w, m, v, block_rows):
    def body(g_ref, w_ref, m_ref, v_ref, d_ref, nm_ref, nv_ref):
        d_ref[...], nm_ref[...], nv_ref[...] = _adamw(w_ref[...], g_ref[...], m_ref[...], v_ref[...])

    rows, cols = w.shape
    spec = pl.BlockSpec((block_rows, cols), lambda i: (i, 0))
    return pl.pallas_call(
        body,
        name=name,
        grid=(rows // block_rows,),
        in_specs=[spec] * 4,
        out_specs=(spec,) * 3,
        out_shape=(jax.ShapeDtypeStruct(w.shape, F32),) * 3,
        compiler_params=_params(("arbitrary",)),
    )(g, w, m, v)


VECTOR_SEMS = 4


def _vector_scratch():
    return [pltpu.VMEM((VEC_ROWS, IN_W), F32), pltpu.VMEM((VEC_ROWS, IN_W), F32),
            pltpu.VMEM((4 * VEC_ROWS, IN_W), F32)]


def _all_reduce_vectors(copies, sem0, gng_ref, gba_ref, gbs_ref, gsink_ref, gln_ref, gsgub_ref, vec4_ref, out_ref,
                        vec_ref, ra_vec, slots):
    x, y, c = _place()
    vec_ref[...] = jnp.zeros_like(vec_ref)
    vec_ref[VEC_NORM_G:VEC_NORM_G + 1, 0:D_MODEL] = gng_ref[...]
    vec_ref[VEC_B_IN:VEC_B_IN + 1, 0:ATTN_SECTION] = gba_ref[...]
    vec_ref[VEC_B_IN:VEC_B_IN + 1, ATTN_SECTION:IN_W] = gbs_ref[...]
    vec_ref[VEC_SINKS:VEC_SINKS + 1, 0:LANES] = gsink_ref[...]
    vec_ref[VEC_LN_G:VEC_LN_G + 1, 0:SGU_W] = gln_ref[0:1, :]
    vec_ref[VEC_LN_B:VEC_LN_B + 1, 0:SGU_W] = gln_ref[1:2, :]
    vec_ref[VEC_B_OUT:VEC_B_OUT + 1, 0:D_MODEL] = vec4_ref[2:3, :]
    vec_ref[VEC_FINAL_G:VEC_FINAL_G + 1, 0:D_MODEL] = vec4_ref[1:2, :]
    vec_ref[VEC_LOSS:VEC_LOSS + 1, 0:D_MODEL] = vec4_ref[0:1, :]
    vec_ref[VEC_SGU_B:VEC_SGU_B + N_SGU_HEADS, 0:BLOCK] = gsgub_ref[...]

    to_sibling = copies(sem0, vec_ref, ra_vec, (x, y, 1 - c))
    to_sibling.start()
    to_sibling.wait_recv()

    def chip_slot(place):
        return slots.at[pl.ds(pl.multiple_of((2 * place[0] + place[1]) * VEC_ROWS, 8), VEC_ROWS), :]

    mine = chip_slot((x, y))
    mine[...] = vec_ref[...] + ra_vec[...]
    to_chips = [copies(sem0 + i, mine, mine, (*_chip(rel), c)) for i, rel in enumerate(RELATIONS[1:], start=1)]
    for cp in to_chips:
        cp.start()
    for i, rel in enumerate(RELATIONS[1:], start=1):
        theirs = chip_slot(_chip(rel))
        copies(sem0 + i, theirs, theirs, (x, y, c)).wait_recv()
    out_ref[...] = ((slots[0:VEC_ROWS, :] + slots[VEC_ROWS:2 * VEC_ROWS, :])
                    + slots[2 * VEC_ROWS:3 * VEC_ROWS, :]) + slots[3 * VEC_ROWS:, :]
    to_sibling.wait_send()
    for cp in to_chips:
        cp.wait_send()


def _adamw_replicated(vec, gsguw, weights, m_state, v_state):
    n = len(SMALL)

    def body(*refs):
        vec_ref, gsguw_ref = refs[0], refs[1]
        w_refs, m_refs, v_refs = (refs[2 + k * n:2 + (k + 1) * n] for k in range(3))
        outs = refs[2 + 3 * n:]
        g_refs, d_refs, nm_refs, nv_refs = (outs[k * n:(k + 1) * n] for k in range(4))
        for i, (_, row, shape) in enumerate(SMALL):
            g = gsguw_ref[...] if row is None else vec_ref[row:row + shape[0], 0:shape[1]]
            g_refs[i][...] = g
            d_refs[i][...], nm_refs[i][...], nv_refs[i][...] = _adamw(
                w_refs[i][...], g, m_refs[i][...], v_refs[i][...])

    shapes = tuple(jax.ShapeDtypeStruct(shape, F32) for _, _, shape in SMALL)
    outs = pl.pallas_call(
        body,
        name="adamw_replicated",
        in_specs=[VMEM_SPEC] * (2 + 3 * n),
        out_specs=(VMEM_SPEC,) * (4 * n),
        out_shape=shapes * 4,
    )(vec, gsguw, *weights, *m_state, *v_state)
    return tuple(outs[k * n:(k + 1) * n] for k in range(4))


SMALL = (
    ("norm_g", VEC_NORM_G, (1, D_MODEL)),
    ("b_in", VEC_B_IN, (1, IN_W)),
    ("attn_sinks", VEC_SINKS, (1, N_Q_HEADS)),
    ("sgu_ln_g", VEC_LN_G, (1, SGU_W)),
    ("sgu_ln_b", VEC_LN_B, (1, SGU_W)),
    ("sgu_w", None, (N_SGU_HEADS * BLOCK, BLOCK)),
    ("sgu_b", VEC_SGU_B, (N_SGU_HEADS, BLOCK)),
    ("b_out", VEC_B_OUT, (1, D_MODEL)),
    ("final_norm_g", VEC_FINAL_G, (1, D_MODEL)),
)


def _local_grads(x, target, h, win_t, wout_shard, norm_g, b_in, attn_sinks, sgu_ln_g, sgu_ln_b, sgu_w, sgu_b, b_out,
                 final_g):
    sinks = attn_sinks.reshape(N_Q_HEADS)
    bias_full = jnp.repeat(sgu_b.T, HEAD_DIM, axis=1)
    q, kvx, gates, wout = _in_proj(h, b_in, win_t, wout_shard)
    out, gres, dmix, gwout, vec4 = _mixers_out_proj(sinks, q, kvx, gates, sgu_ln_g, sgu_ln_b, sgu_w, bias_full,
                                                    x, target, wout, b_out, final_g)
    dpa, gsink, gbin_a, dps, gsguw, gsgub, gln, gbin_s, gwout_shard = _mixers_bwd(
        sinks, dmix, q, kvx, out, gates, sgu_ln_g, sgu_ln_b, sgu_w, bias_full, gwout)
    gwin_own, gwin_others, gsguw_sum = _win_grad(dpa, dps, h, gsguw.reshape(N_SGU_HEADS * BLOCK, BLOCK))
    grad_x, gwin_shard, vec = _in_proj_bwd(dpa, dps, win_t, x, norm_g, gres, gwin_own, gwin_others,
                                           (gbin_a, gbin_s, gsink, gln, gsgub, vec4))
    return grad_x, gwin_shard, gwout_shard, gsguw_sum, vec


def kernel(x, norm_g, w_in, b_in, attn_sinks, sgu_ln_g, sgu_ln_b, sgu_w, sgu_b, w_out, b_out, final_norm_g, loss_target, m_norm_g, m_w_in, m_b_in, m_attn_sinks, m_sgu_ln_g, m_sgu_ln_b, m_sgu_w, m_sgu_b, m_w_out, m_b_out, m_final_norm_g, v_norm_g, v_w_in, v_b_in, v_attn_sinks, v_sgu_ln_g, v_sgu_ln_b, v_sgu_w, v_sgu_b, v_w_out, v_b_out, v_final_norm_g):
    given = dict(norm_g=norm_g, b_in=b_in, attn_sinks=attn_sinks, sgu_ln_g=sgu_ln_g, sgu_ln_b=sgu_ln_b,
                 sgu_w=sgu_w, sgu_b=sgu_b, b_out=b_out, final_norm_g=final_norm_g)
    m_given = dict(norm_g=m_norm_g, b_in=m_b_in, attn_sinks=m_attn_sinks, sgu_ln_g=m_sgu_ln_g,
                   sgu_ln_b=m_sgu_ln_b, sgu_w=m_sgu_w, sgu_b=m_sgu_b, b_out=m_b_out, final_norm_g=m_final_norm_g)
    v_given = dict(norm_g=v_norm_g, b_in=v_b_in, attn_sinks=v_attn_sinks, sgu_ln_g=v_sgu_ln_g,
                   sgu_ln_b=v_sgu_ln_b, sgu_w=v_sgu_w, sgu_b=v_sgu_b, b_out=v_b_out, final_norm_g=v_final_norm_g)

    win_t, h = _all_gather_win(w_in[0].T, x[0], norm_g)
    grad_x, gwin_t, gwout, gsguw, vec = _local_grads(
        x[0], loss_target[0], h, win_t, w_out[0], norm_g, b_in, attn_sinks, sgu_ln_g, sgu_ln_b, sgu_w[0], sgu_b[0],
        b_out, final_norm_g.reshape(1, D_MODEL))

    t = lambda a: a[0].T
    d_win, nm_win, nv_win = _adamw_shard("adamw_w_in", gwin_t, t(w_in), t(m_w_in), t(v_w_in), WIN_ROWS // 2)
    d_wout, nm_wout, nv_wout = _adamw_shard("adamw_w_out", gwout, w_out[0], m_w_out[0], v_w_out[0], WOUT_ROWS)
    as_2d = lambda d: [d[name].reshape(shape) for name, _, shape in SMALL]
    loss = vec[VEC_LOSS, 0]
    small = _adamw_replicated(vec, gsguw, as_2d(given), as_2d(m_given), as_2d(v_given))

    def assemble(big_in, big_out, k):
        vals = {name: small[k][i].reshape(given[name].shape) for i, (name, _, _) in enumerate(SMALL)}
        vals["w_in"] = big_in.T[None]
        vals["w_out"] = big_out[None]
        order = ("norm_g", "w_in", "b_in", "attn_sinks", "sgu_ln_g", "sgu_ln_b", "sgu_w", "sgu_b", "w_out",
                 "b_out", "final_norm_g")
        return [vals[name] for name in order]

    return (loss, grad_x[None],
            *assemble(gwin_t, gwout, 0), *assemble(d_win, d_wout, 1),
            *assemble(nm_win, nm_wout, 2), *assemble(nv_win, nv_wout, 3))
```

```python
import functools
import math

import jax
import jax.numpy as jnp
from jax import lax
from jax.experimental import pallas as pl
from jax.experimental.pallas import tpu as pltpu

F32 = jnp.float32
BF16 = jnp.bfloat16
MXU_DTYPE = BF16
COMM_DTYPE = BF16

D_MODEL = 1024
SEQ = 4096
HEAD_DIM = 64
N_Q_HEADS = 8
Q_PER_KV = 4
BLOCK = 128
N_BLOCKS = SEQ // BLOCK
ATTN_W = 512
KV_W = 128
SGU_W = 512
N_SGU_HEADS = 8
IN_W = 2816
NORM_EPS = 1e-5
NEG_INF = -1e30
SCALE = HEAD_DIM ** -0.5
KV0 = ATTN_W
GATE0 = ATTN_W + 2 * KV_W
SGU0 = GATE0 + ATTN_W
ATTN_SECTION = SGU0
SGU_SECTION = IN_W - SGU0

ADAM_LR = 0.001
ADAM_B1 = 0.9
ADAM_B2 = 0.999
ADAM_EPS = 1e-08
ADAM_WD = 0.01
ADAM_STEP = 10

N_DEV = 8
WIN_ROWS = IN_W // N_DEV
WOUT_ROWS = D_MODEL // N_DEV
SGUW_ROWS = N_SGU_HEADS * BLOCK // N_DEV
VEC_ROWS = 16
MESH = pl.DeviceIdType.MESH

LANES = 128
HALF = LANES // 2
N_PAIRS = N_Q_HEADS * HEAD_DIM // LANES
KVX_W = 12 * LANES
TOKEN_TILE = 256
FWD_TOKEN_TILE = 512
ATTN_FWD_AHEAD = 4
FUSED_BLOCKS = 2
SGU_MIX_AFTER_CHAIN = 0
SGU_GATES_AFTER_CHAIN = 1
SGU_GRADS_AFTER_CHAIN = 5
ATTN_BWD_AHEAD = 3
VMEM_LIMIT = 56 * 1024 * 1024

NN = (((1,), (0,)), ((), ()))
NT = (((1,), (1,)), ((), ()))
TN = (((0,), (0,)), ((), ()))


def _dot(a, b, dims=NN):
    return lax.dot_general(a.astype(MXU_DTYPE), b.astype(MXU_DTYPE), dims, preferred_element_type=F32)


def _gelu(x):
    return x * (lax.erf(x * (1.0 / math.sqrt(2.0))) + 1.0) * 0.5


def _gelu_grad(x):
    cdf = (lax.erf(x * (1.0 / math.sqrt(2.0))) + 1.0) * 0.5
    return cdf + x * jnp.exp(-0.5 * x * x) * (1.0 / math.sqrt(2.0 * math.pi))


def _silu_and_grad(z):
    s = jax.nn.sigmoid(z)
    return z * s, s * (1.0 + z * (1.0 - s))


def _params(semantics=None, vmem=None):
    kw = {}
    if semantics is not None:
        kw["dimension_semantics"] = semantics
    if vmem is not None:
        kw["vmem_limit_bytes"] = vmem
    return pltpu.CompilerParams(**kw)


def _full(shape):
    return pl.BlockSpec(shape, lambda *_: (0,) * len(shape))


VMEM_SPEC = pl.BlockSpec(memory_space=pltpu.VMEM)


RELATIONS = ((0, 0), (1, 0), (0, 1), (1, 1))


def _place():
    return lax.axis_index("x"), lax.axis_index("y"), lax.axis_index("c")


def _chip(rel):
    x, y, _ = _place()
    return (1 - x if rel[0] else x, 1 - y if rel[1] else y)


def _block_rows(place, n_rows):
    px, py, pc = place
    return pl.ds(pl.multiple_of((4 * px + 2 * py + pc) * n_rows, 16), n_rows)


class _Copies:
    def __init__(self, send_sems, recv_sems):
        self.send_sems, self.recv_sems = send_sems, recv_sems

    def __call__(self, k, src, dst, to):
        return pltpu.make_async_remote_copy(src_ref=src, dst_ref=dst, send_sem=self.send_sems.at[k],
                                            recv_sem=self.recv_sems.at[k], device_id=to, device_id_type=MESH)


def _gather_plan(copies, sem0, full_ref, n_rows):
    x, y, c = _place()
    me, sibling = (x, y, c), (x, y, 1 - c)
    chips = [_chip(rel) for rel in RELATIONS[1:]]

    def cp(k, block, to):
        rows = full_ref.at[_block_rows(block, n_rows), :]
        return copies(sem0 + k, rows, rows, to)

    first = [cp(0, me, sibling)] + [cp(1 + j, me, (*chip, c)) for j, chip in enumerate(chips)]
    passed = [cp(4 + j, (*chip, c), sibling) for j, chip in enumerate(chips)]

    def start():
        for f in first:
            f.start()

    def forward():
        for j, chip in enumerate(chips):
            cp(1 + j, (*chip, c), me).wait_recv()
            passed[j].start()

    def finish():
        cp(0, sibling, me).wait_recv()
        for j, chip in enumerate(chips):
            cp(4 + j, (*chip, 1 - c), me).wait_recv()
        for f in first + passed:
            f.wait_send()

    return start, forward, finish


GATHER_SEMS = 7


def _reduce_scatter_plan(copies, sem0, part_ref, n_rows, sa, ra, sb, rc, res_ref):
    x, y, c = _place()
    sibling = (x, y, 1 - c)
    n = n_rows
    level1 = copies(sem0, sa, ra, sibling)

    def level2(i):
        slot = pl.ds((i - 1) * n, n)
        return copies(sem0 + i, sb.at[slot, :], rc.at[slot, :], (*_chip(RELATIONS[i]), c))

    def start():
        for i, rel in enumerate(RELATIONS):
            sa[i * n:(i + 1) * n, :] = part_ref[_block_rows((*_chip(rel), 1 - c), n), :].astype(sa.dtype)
        level1.start()

    def exchange():
        level1.wait_recv()
        for i, rel in enumerate(RELATIONS):
            total = part_ref[_block_rows((*_chip(rel), c), n), :] + ra[i * n:(i + 1) * n, :].astype(F32)
            if i == 0:
                res_ref[...] = total
            else:
                sb[(i - 1) * n:i * n, :] = total.astype(sb.dtype)
                level2(i).start()

    def finish():
        acc = res_ref[...]
        for i in range(1, len(RELATIONS)):
            level2(i).wait_recv()
            acc = acc + rc[(i - 1) * n:i * n, :].astype(F32)
        res_ref[...] = acc
        level1.wait_send()
        for i in range(1, len(RELATIONS)):
            level2(i).wait_send()

    return start, exchange, finish


REDUCE_SEMS = 4


def _owner_sums_plan(copies, sem0, own_ref, sb, rc, res_ref):
    _, _, c = _place()
    n = own_ref.shape[0]

    def level2(i):
        slot = pl.ds((i - 1) * n, n)
        return copies(sem0 + i - 1, sb.at[slot, :], rc.at[slot, :], (*_chip(RELATIONS[i]), c))

    def send():
        for i in range(1, len(RELATIONS)):
            level2(i).start()

    def finish():
        acc = own_ref[...]
        for i in range(1, len(RELATIONS)):
            level2(i).wait_recv()
            acc = acc + rc[(i - 1) * n:i * n, :].astype(F32)
        res_ref[...] = acc
        for i in range(1, len(RELATIONS)):
            level2(i).wait_send()

    return send, finish


OWNER_SEMS = 3


def _reduce_scatter_scratch(n_rows, width, dtype):
    return [pltpu.VMEM((4 * n_rows, width), dtype), pltpu.VMEM((4 * n_rows, width), dtype),
            pltpu.VMEM((3 * n_rows, width), dtype), pltpu.VMEM((3 * n_rows, width), dtype)]


def _dma_sems(n):
    return [pltpu.SemaphoreType.DMA((n,)), pltpu.SemaphoreType.DMA((n,))]


def _all_gather_win(win_t_shard, x, norm_g):
    tm = FWD_TOKEN_TILE
    steps = SEQ // tm

    def body(win_ref, x_ref, g_ref, full_ref, h_ref, landing, send_sems, recv_sems):
        step = pl.program_id(0)
        start, forward, finish = _gather_plan(_Copies(send_sems, recv_sems), 0, landing, WIN_ROWS)

        @pl.when(step == 0)
        def _():
            landing[_block_rows(_place(), WIN_ROWS), :] = win_ref[...].astype(COMM_DTYPE)
            start()

        xv = x_ref[...]
        r = lax.rsqrt(jnp.mean(xv * xv, axis=-1, keepdims=True) + NORM_EPS)
        h_ref[...] = ((xv * r) * g_ref[...]).astype(MXU_DTYPE)

        @pl.when(step == steps - 1)
        def _():
            forward()
            finish()
            full_ref[...] = landing[...]

    return pl.pallas_call(
        body,
        name="all_gather_win",
        grid=(steps,),
        in_specs=[VMEM_SPEC, pl.BlockSpec((tm, D_MODEL), lambda i: (i, 0)), _full((1, D_MODEL))],
        out_specs=(_full((IN_W, D_MODEL)), pl.BlockSpec((tm, D_MODEL), lambda i: (i, 0))),
        out_shape=(jax.ShapeDtypeStruct((IN_W, D_MODEL), COMM_DTYPE),
                   jax.ShapeDtypeStruct((SEQ, D_MODEL), MXU_DTYPE)),
        scratch_shapes=[pltpu.VMEM((IN_W, D_MODEL), COMM_DTYPE)] + _dma_sems(GATHER_SEMS),
        compiler_params=_params(("arbitrary",), VMEM_LIMIT),
    )(win_t_shard, x, norm_g)


def _in_proj(h, b_in, win_t, wout_shard):
    tm = FWD_TOKEN_TILE
    steps = SEQ // tm

    def body(h_ref, b_ref, w_ref, wout_ref, q_ref, kvx_ref, gate_ref, wfull_ref, landing, send_sems, recv_sems):
        step = pl.program_id(0)
        start, forward, finish = _gather_plan(_Copies(send_sems, recv_sems), 0, landing, WOUT_ROWS)

        @pl.when(step == 0)
        def _():
            landing[_block_rows(_place(), WOUT_ROWS), :] = wout_ref[...].astype(COMM_DTYPE)
            start()

        pl.when(step == steps // 2)(forward)

        h = h_ref[...]

        def proj(lo, hi):
            return _dot(h, w_ref[lo:hi, :], NT) + b_ref[:, lo:hi]

        qs = proj(0, ATTN_W) * SCALE
        for pair in range(N_PAIRS):
            q_ref[pair] = qs[:, pair * LANES:(pair + 1) * LANES].astype(MXU_DTYPE)
        kv = proj(KV0, GATE0)
        low = lax.broadcasted_iota(jnp.int32, (tm, LANES), 1) < HALF
        for i in range(2):
            t = kv[:, i * LANES:(i + 1) * LANES]
            rot = pltpu.roll(t, HALF, 1)
            variants = (jnp.where(low, t, 0.0), jnp.where(low, 0.0, rot),
                        jnp.where(low, rot, 0.0), jnp.where(low, 0.0, t))
            for j, val in enumerate(variants):
                col = (4 * i + j) * LANES
                kvx_ref[:, col:col + LANES] = val.astype(MXU_DTYPE)
                if i == 1:
                    ones_elsewhere = jnp.where(low == (j % 2 == 0), val, 1.0)
                    kvx_ref[:, col + 4 * LANES:col + 5 * LANES] = ones_elsewhere.astype(MXU_DTYPE)
        for k in range(4):
            gate_ref[k] = proj(GATE0 + k * SGU_W, GATE0 + (k + 1) * SGU_W)

        @pl.when(step == steps - 1)
        def _():
            finish()
            wfull_ref[...] = landing[...]

    return pl.pallas_call(
        body,
        name="in_proj",
        grid=(steps,),
        in_specs=[pl.BlockSpec((tm, D_MODEL), lambda i: (i, 0)),
                  _full((1, IN_W)), _full((IN_W, D_MODEL)), VMEM_SPEC],
        out_specs=(pl.BlockSpec((N_PAIRS, tm, LANES), lambda i: (0, i, 0)),
                   pl.BlockSpec((tm, KVX_W), lambda i: (i, 0)),
                   pl.BlockSpec((4, tm, SGU_W), lambda i: (0, i, 0)),
                   _full((D_MODEL, D_MODEL))),
        out_shape=(jax.ShapeDtypeStruct((N_PAIRS, SEQ, LANES), MXU_DTYPE),
                   jax.ShapeDtypeStruct((SEQ, KVX_W), MXU_DTYPE),
                   jax.ShapeDtypeStruct((4, SEQ, SGU_W), F32),
                   jax.ShapeDtypeStruct((D_MODEL, D_MODEL), COMM_DTYPE)),
        scratch_shapes=[pltpu.VMEM((D_MODEL, D_MODEL), COMM_DTYPE)] + _dma_sems(GATHER_SEMS),
        compiler_params=_params(("arbitrary",), VMEM_LIMIT),
    )(h, b_in, win_t, wout_shard)


def _window_mask(n):
    qi = lax.broadcasted_iota(jnp.int32, (2 * BLOCK, 2 * BLOCK), 0) & (BLOCK - 1)
    p = lax.broadcasted_iota(jnp.int32, (2 * BLOCK, 2 * BLOCK), 1) - BLOCK
    in_window = jnp.logical_and(p <= qi, p > qi - BLOCK)
    return jnp.logical_and(in_window, jnp.logical_or(p >= 0, n > 0))


def _sink_column(sink_ref, g, par):
    return jnp.concatenate([jnp.full((BLOCK, 1), sink_ref[4 * g + par], F32),
                            jnp.full((BLOCK, 1), sink_ref[4 * g + 2 + par], F32)], axis=0)


def _kv_cat(kp_ref, kc_ref, var, with_ones):
    kcol, vcol = var * LANES, (var + (8 if with_ones else 4)) * LANES
    return (jnp.concatenate([kp_ref[:, kcol:kcol + LANES], kc_ref[:, kcol:kcol + LANES]], axis=0),
            jnp.concatenate([kp_ref[:, vcol:vcol + LANES], kc_ref[:, vcol:vcol + LANES]], axis=0))


def _softmax_numerator(s, sink):
    m = jnp.maximum(jnp.max(s, axis=1, keepdims=True), sink)
    return jnp.exp(s - m), m


def _mixers_out_proj(sinks, q, kvx, gates, ln_g, ln_b, sgu_w, bias_full, x, target, wout, b_out, final_g):
    tm = FUSED_BLOCKS * BLOCK
    n_tiles = SEQ // tm

    def body(sink_ref, q_ref, kc_ref, za_ref, us_ref, vs_ref, zs_ref, lng_ref, lnb_ref, w_ref, bias_ref,
             x_ref, t_ref, wout_ref, b_ref, gf_ref,
             out_ref, gres_ref, dmix_ref, gw_ref, vec_ref,
             kp_ref, wm_ref, mixed_next, mixed_cur, out_stage, gb_ref):
        step = pl.program_id(0)

        @pl.when(step == 0)
        def _():
            kp_ref[...] = jnp.zeros_like(kp_ref)
            _mask_sgu_weights(w_ref, wm_ref)
            gw_ref[...] = jnp.zeros_like(gw_ref)
            vec_ref[...] = jnp.zeros_like(vec_ref)
            mixed_cur[...] = jnp.zeros_like(mixed_cur)

        def mixers_block(b, after_chain=()):
            rows = slice(b * BLOCK, (b + 1) * BLOCK)
            kc = kc_ref.at[rows, :]
            u, _, _, vln = _sgu_activations(us_ref[rows, :], vs_ref[rows, :], lng_ref[...], lnb_ref[...])

            valid = _window_mask(step * FUSED_BLOCKS + b)[0:BLOCK]
            chains = [(g, par, i) for g in range(2) for par in range(2) for i in range(2)]
            kv = {(g, par): _kv_cat(kp_ref, kc, 2 * g + par, True) for g in range(2) for par in range(2)}
            scores, outs = {}, {}

            def issue_scores(k):
                g, par, i = chains[k]
                scores[k] = _dot(q_ref[2 * g + i, rows, :], kv[g, par][0], NT)

            ahead = ATTN_FWD_AHEAD
            for k in range(ahead):
                issue_scores(k)
            low = lax.broadcasted_iota(jnp.int32, (BLOCK, LANES), 1) < HALF
            for k, (g, par, i) in enumerate(chains):
                sink = sink_ref[4 * g + 2 * i + par]
                e, m = _softmax_numerator(jnp.where(valid, scores[k], NEG_INF), sink)
                if k + ahead < len(chains):
                    issue_scores(k + ahead)
                o = _dot(e, kv[g, par][1])
                outs[g, par, i] = o / (pltpu.roll(o, HALF, 1) + jnp.exp(sink - m))
                if k == SGU_MIX_AFTER_CHAIN:
                    mixed = _sgu_mix(vln, wm_ref, bias_ref)
                if k % 2 == 0 and k // 2 < len(after_chain):
                    after_chain[k // 2]()
            for pair in range(N_PAIRS):
                g, i = divmod(pair, 2)
                lanes = slice(pair * LANES, (pair + 1) * LANES)
                o = jnp.where(low, outs[g, 0, i], outs[g, 1, i])
                out_stage[pair, rows, :] = o
                gate, _ = _silu_and_grad(za_ref[rows, lanes])
                mixed_next[rows, lanes] = (o * gate).astype(MXU_DTYPE)
            kp_ref[...] = kc[...]
            for pair in range(N_SGU_HEADS // 2):
                cols = slice(pair * LANES, (pair + 1) * LANES)
                gate, _ = _silu_and_grad(zs_ref[rows, cols])
                mixed_next[rows, ATTN_W + pair * LANES:ATTN_W + (pair + 1) * LANES] = (
                    u[:, cols] * mixed[pair] * gate).astype(MXU_DTYPE)

        live = (step > 0).astype(F32)
        quarter = D_MODEL // 4
        columns = [None] * 4

        def project(j):
            def piece():
                columns[j] = _dot(mixed_cur[...], wout_ref[:, j * quarter:(j + 1) * quarter])
            return piece

        half_blocks = FUSED_BLOCKS // 2
        per_block = 4 // half_blocks
        for b in range(half_blocks):
            mixers_block(b, [project(j) for j in range(b * per_block, (b + 1) * per_block)])
        xo = x_ref[...] + jnp.concatenate(columns, axis=1) + b_ref[...]
        r = lax.rsqrt(jnp.mean(xo * xo, axis=-1, keepdims=True) + NORM_EPS)
        xn = xo * r
        gf = gf_ref[...]
        err = xn * gf - t_ref[...]
        loss = 0.5 * jnp.sum(jnp.mean(err * err, axis=-1, keepdims=True), axis=0, keepdims=True)
        dy = err * (1.0 / D_MODEL)
        dxn = dy * gf
        gres = r * (dxn - xn * jnp.mean(dxn * xn, axis=-1, keepdims=True))
        vec_ref[0:1, :] += jnp.broadcast_to(loss * live, (1, D_MODEL))
        vec_ref[1:2, :] += jnp.sum(dy * xn, axis=0, keepdims=True) * live
        vec_ref[2:3, :] += jnp.sum(gres, axis=0, keepdims=True) * live
        gres_ref[...] = gres
        gb_ref[...] = gres.astype(MXU_DTYPE)

        def branch_grad(k):
            def piece():
                dmix_ref[k] = _dot(gb_ref[...], wout_ref[k * ATTN_W:(k + 1) * ATTN_W, :], NT)
            return piece

        def weight_grad(k):
            def piece():
                rows = slice(k * ATTN_W, (k + 1) * ATTN_W)
                gw_ref[rows, :] += _dot(mixed_cur[:, rows], gb_ref[...], TN)
            return piece

        backward = [branch_grad(0), branch_grad(1), weight_grad(0), weight_grad(1)]
        for b in range(half_blocks):
            mixers_block(half_blocks + b, backward[b * per_block:(b + 1) * per_block])

        @pl.when(step < n_tiles)
        def _():
            out_ref[...] = out_stage[...]

        mixed_cur[...] = mixed_next[...]

    ahead_tile = lambda i: jnp.minimum(i, n_tiles - 1)
    behind_tile = lambda i: jnp.maximum(i - 1, 0)
    blk = lambda w: pl.BlockSpec((tm, w), lambda i: (ahead_tile(i), 0))
    tiles = pl.BlockSpec((N_PAIRS, tm, LANES), lambda i: (0, ahead_tile(i), 0))
    gate = lambda k: pl.BlockSpec((None, tm, SGU_W), lambda i: (k, ahead_tile(i), 0))
    behind = lambda w: pl.BlockSpec((tm, w), lambda i: (behind_tile(i), 0))
    return pl.pallas_call(
        body,
        name="mixers_out_proj",
        grid=(n_tiles + 1,),
        in_specs=[pl.BlockSpec(memory_space=pltpu.SMEM), tiles, blk(KVX_W), gate(0), gate(1), gate(2), gate(3),
                  _full((1, SGU_W)), _full((1, SGU_W)), _full((N_SGU_HEADS, BLOCK, BLOCK)), _full((BLOCK, SGU_W)),
                  behind(D_MODEL), behind(D_MODEL), _full((D_MODEL, D_MODEL)), _full((1, D_MODEL)),
                  _full((1, D_MODEL))],
        out_specs=(tiles, behind(D_MODEL), pl.BlockSpec((2, tm, ATTN_W), lambda i: (0, behind_tile(i), 0)),
                   _full((D_MODEL, D_MODEL)), _full((8, D_MODEL))),
        out_shape=(jax.ShapeDtypeStruct((N_PAIRS, SEQ, LANES), F32),
                   jax.ShapeDtypeStruct((SEQ, D_MODEL), F32),
                   jax.ShapeDtypeStruct((2, SEQ, ATTN_W), F32),
                   jax.ShapeDtypeStruct((D_MODEL, D_MODEL), F32),
                   jax.ShapeDtypeStruct((8, D_MODEL), F32)),
        scratch_shapes=[pltpu.VMEM((BLOCK, KVX_W), MXU_DTYPE), pltpu.VMEM((N_SGU_HEADS, BLOCK, BLOCK), MXU_DTYPE),
                        pltpu.VMEM((tm, D_MODEL), MXU_DTYPE), pltpu.VMEM((tm, D_MODEL), MXU_DTYPE),
                        pltpu.VMEM((N_PAIRS, tm, LANES), F32), pltpu.VMEM((tm, D_MODEL), MXU_DTYPE)],
        compiler_params=_params(("arbitrary",), VMEM_LIMIT),
    )(sinks, q, kvx, gates, gates, gates, gates, ln_g, ln_b, sgu_w, bias_full, x, target, wout, b_out, final_g)


def _sgu_activations(us, vs, lng, lnb):
    u = _gelu(us)
    vg = _gelu(vs)
    mu = jnp.mean(vg, axis=-1, keepdims=True)
    xc = vg - mu
    rstd = lax.rsqrt(jnp.mean(xc * xc, axis=-1, keepdims=True) + NORM_EPS)
    vhat = xc * rstd
    return u, vhat, rstd, vhat * lng + lnb


def _mask_sgu_weights(w_ref, masked_ref, transposed_ref=None):
    tril = (lax.broadcasted_iota(jnp.int32, (BLOCK, BLOCK), 0)
            >= lax.broadcasted_iota(jnp.int32, (BLOCK, BLOCK), 1))
    for hh in range(N_SGU_HEADS):
        w = jnp.where(tril, w_ref[hh], 0.0)
        masked_ref[hh] = w.astype(MXU_DTYPE)
        if transposed_ref is not None:
            transposed_ref[hh] = w.T.astype(MXU_DTYPE)


def _sgu_mix(vln, masked_w_ref, bias_ref):
    low = lax.broadcasted_iota(jnp.int32, (BLOCK, LANES), 1) < HALF
    mixed = []
    for pair in range(N_SGU_HEADS // 2):
        vp = vln[:, pair * LANES:(pair + 1) * LANES]
        mixed.append(_dot(masked_w_ref[2 * pair], jnp.where(low, vp, 0.0))
                     + _dot(masked_w_ref[2 * pair + 1], jnp.where(low, 0.0, vp))
                     + bias_ref[:, pair * LANES:(pair + 1) * LANES])
    return mixed


def _mixers_bwd(sinks, dmix, q, kvx, out, gates, ln_g, ln_b, sgu_w, bias_full, gwout):
    last = N_BLOCKS - 1

    def body(sink_ref, d_ref, q_ref, kc_ref, o_ref, za_ref, dsg_ref, us_ref, vs_ref, zs_ref, lng_ref, lnb_ref, w_ref,
             bias_ref, gwout_ref,
             dp_ref, gsink_ref, gbin_ref, dps_ref, gw_ref, gb_ref, gln_ref, gbins_ref, wout_shard_ref,
             kp_ref, pend_ref, carry_ref, wm_ref, wt_ref, gbias_ref, sa_w, ra_w, sb_w, rc_w, send_sems, recv_sems):
        n = pl.program_id(0)
        start, exchange, finish = _reduce_scatter_plan(_Copies(send_sems, recv_sems), 0, gwout_ref, WOUT_ROWS,
                                                       sa_w, ra_w, sb_w, rc_w, wout_shard_ref)
        tril = (lax.broadcasted_iota(jnp.int32, (BLOCK, BLOCK), 0)
                >= lax.broadcasted_iota(jnp.int32, (BLOCK, BLOCK), 1))

        @pl.when(n == 0)
        def _():
            gsink_ref[...] = jnp.zeros_like(gsink_ref)
            gbin_ref[...] = jnp.zeros_like(gbin_ref)
            carry_ref[...] = jnp.zeros_like(carry_ref)
            kp_ref[...] = jnp.zeros_like(kp_ref)
            gw_ref[...] = jnp.zeros_like(gw_ref)
            gln_ref[...] = jnp.zeros_like(gln_ref)
            gbins_ref[...] = jnp.zeros_like(gbins_ref)
            gbias_ref[...] = jnp.zeros_like(gbias_ref)
            _mask_sgu_weights(w_ref, wm_ref, wt_ref)
            start()

        pl.when(n == 3)(exchange)
        pl.when(n == 12)(finish)

        @pl.when(n > 0)
        def _():
            dp_ref[:, 0:ATTN_W] = pend_ref[:, 0:ATTN_W]
            dp_ref[:, GATE0:ATTN_SECTION] = pend_ref[:, ATTN_W:]

        @pl.when(n > last)
        def _():
            dp_ref[:, KV0:GATE0] = carry_ref[...].astype(MXU_DTYPE)

        @pl.when(n <= last)
        def _():
            us = us_ref[...]
            vs = vs_ref[...]
            lng = lng_ref[...]
            u, vhat, rstd, vln = _sgu_activations(us, vs, lng, lnb_ref[...])
            low_sgu = lax.broadcasted_iota(jnp.int32, (BLOCK, LANES), 1) < HALF
            sgu = {}

            def sgu_gates():
                mixed = _sgu_mix(vln, wm_ref, bias_ref)
                sgu["du"], sgu["dzs"], sgu["dm"] = [], [], []
                for pair in range(N_SGU_HEADS // 2):
                    cols = slice(pair * LANES, (pair + 1) * LANES)
                    dsg = dsg_ref[:, cols]
                    gate, gate_grad = _silu_and_grad(zs_ref[:, cols])
                    up = u[:, cols]
                    sgu["du"].append(dsg * mixed[pair] * gate)
                    sgu["dzs"].append(dsg * up * mixed[pair] * gate_grad)
                    dmixed = dsg * up * gate
                    gbias_ref[:, cols] += dmixed
                    sgu["dm"].append((jnp.where(low_sgu, dmixed, 0.0).astype(MXU_DTYPE),
                                      jnp.where(low_sgu, 0.0, dmixed).astype(MXU_DTYPE)))

            def sgu_grads():
                dvln_parts = []
                for pair in range(N_SGU_HEADS // 2):
                    dm_lo, dm_hi = sgu["dm"][pair]
                    vp = vln[:, pair * LANES:(pair + 1) * LANES]
                    gw_ref[2 * pair] += _dot(dm_lo, vp, NT)
                    gw_ref[2 * pair + 1] += _dot(dm_hi, vp, NT)
                    dvln_parts.append(_dot(wt_ref[2 * pair], dm_lo) + _dot(wt_ref[2 * pair + 1], dm_hi))
                dvln = jnp.concatenate(dvln_parts, axis=1)
                gln_ref[0:1, :] += jnp.sum(dvln * vhat, axis=0, keepdims=True)
                gln_ref[1:2, :] += jnp.sum(dvln, axis=0, keepdims=True)
                dvhat = dvln * lng
                dvg = rstd * (dvhat - jnp.mean(dvhat, axis=-1, keepdims=True)
                              - vhat * jnp.mean(dvhat * vhat, axis=-1, keepdims=True))
                dus = jnp.concatenate(sgu["du"], axis=1) * _gelu_grad(us)
                dvs = dvg * _gelu_grad(vs)
                dzs = jnp.concatenate(sgu["dzs"], axis=1)
                for k, val in enumerate((dus, dvs, dzs)):
                    dps_ref[:, k * SGU_W:(k + 1) * SGU_W] = val.astype(MXU_DTYPE)
                    gbins_ref[:, k * SGU_W:(k + 1) * SGU_W] += jnp.sum(val, axis=0, keepdims=True)

            valid = _window_mask(n)[0:BLOCK]
            low = lax.broadcasted_iota(jnp.int32, (BLOCK, LANES), 1) < HALF
            low_keys = lax.broadcasted_iota(jnp.int32, (2 * BLOCK, LANES), 1) < HALF
            lane_row = lax.broadcasted_iota(jnp.int32, (1, LANES), 1)
            gsink = jnp.zeros((1, LANES), F32)
            chains = [(g, par, i) for g in range(2) for par in range(2) for i in range(2)]
            kv = {(g, par): _kv_cat(kp_ref, kc_ref, 2 * g + par, False) for g in range(2) for par in range(2)}
            ones_keys = jnp.ones((2 * BLOCK, LANES), MXU_DTYPE)
            half_of_lane = lax.broadcasted_iota(jnp.int32, (LANES, 2 * LANES), 0) // HALF
            half_of_col = lax.broadcasted_iota(jnp.int32, (LANES, 2 * LANES), 1) // LANES
            sum_halves = (half_of_lane == half_of_col).astype(MXU_DTYPE)
            douts, deltas = [], []
            for pair in range(N_PAIRS):
                lanes = slice(pair * LANES, (pair + 1) * LANES)
                dg = d_ref[:, lanes]
                gate, gate_grad = _silu_and_grad(za_ref[:, lanes])
                o = o_ref[pair]
                dout = dg * gate
                dza = dg * o * gate_grad
                douts.append(dout.astype(MXU_DTYPE))
                deltas.append(_dot(dout * o, sum_halves))
                zl = slice(ATTN_W + pair * LANES, ATTN_W + (pair + 1) * LANES)
                pend_ref[:, zl] = dza.astype(MXU_DTYPE)
                gl = slice(GATE0 + pair * LANES, GATE0 + (pair + 1) * LANES)
                gbin_ref[:, gl] += jnp.sum(dza, axis=0, keepdims=True)

            first = {}

            def issue_first(k):
                g, par, i = chains[k]
                first[k] = (_dot(q_ref[2 * g + i], kv[g, par][0], NT), _dot(douts[2 * g + i], kv[g, par][1], NT))

            numerators = {}

            def issue_row_sums(k):
                g, par, i = chains[k]
                sink = sink_ref[4 * g + 2 * i + par]
                e, m = _softmax_numerator(jnp.where(valid, first[k][0], NEG_INF), sink)
                numerators[k] = (e, jnp.exp(sink - m), _dot(e, ones_keys))

            ahead = ATTN_BWD_AHEAD
            for k in range(ahead):
                issue_first(k)
            issue_row_sums(0)
            issue_row_sums(1)
            dqs, dk_parts, dv_parts = {}, {}, {}
            operands = {}

            def issue_last(k):
                g, par, i = chains[k]
                ds, ds_t, p_t = operands.pop(k)
                dq = _dot(ds, kv[g, par][0])
                dqs[g, i] = dq if par == 0 else dqs[g, i] + dq
                dk = _dot(ds_t, q_ref[2 * g + i])
                dv = _dot(p_t, douts[2 * g + i])
                dk_parts[g, par] = dk if i == 0 else dk_parts[g, par] + dk
                dv_parts[g, par] = dv if i == 0 else dv_parts[g, par] + dv

            for k, (g, par, i) in enumerate(chains):
                h = 4 * g + 2 * i + par
                delta = deltas[2 * g + i][:, par * LANES:(par + 1) * LANES]
                e, at_sink, row_sum = numerators[k]
                inv = 1.0 / (row_sum + at_sink)
                p = e * jnp.tile(inv, (1, 2))
                ds = p * (first[k][1] - jnp.tile(delta, (1, 2)))
                ds = ds.astype(MXU_DTYPE)
                operands[k] = (ds, ds.T, p.astype(MXU_DTYPE).T)
                total = jnp.sum(at_sink * inv * delta, axis=0, keepdims=True)
                gsink = jnp.where(lane_row == h, -total, gsink)
                if k + ahead < len(chains):
                    issue_first(k + ahead)
                if k + 2 < len(chains):
                    issue_row_sums(k + 2)
                if k > 0:
                    issue_last(k - 1)
                if k == SGU_GATES_AFTER_CHAIN:
                    sgu_gates()
                if k == SGU_GRADS_AFTER_CHAIN:
                    sgu_grads()
            issue_last(len(chains) - 1)
            for pair in range(N_PAIRS):
                g, i = divmod(pair, 2)
                dq = dqs[g, i] * SCALE
                lanes = slice(pair * LANES, (pair + 1) * LANES)
                pend_ref[:, lanes] = dq.astype(MXU_DTYPE)
                gbin_ref[:, lanes] += jnp.sum(dq, axis=0, keepdims=True)
            gsink_ref[...] += gsink
            for k, parts in enumerate((dk_parts, dv_parts)):
                masked = {key: jnp.where(low_keys if key[1] == 0 else jnp.logical_not(low_keys), val, 0.0)
                          for key, val in parts.items()}
                both = (masked[0, 0] + masked[1, 1]
                        + pltpu.roll(masked[0, 1] + masked[1, 0], HALF, 1))
                lanes = slice(k * KV_W, (k + 1) * KV_W)
                done = carry_ref[:, lanes] + both[0:BLOCK]
                dp_ref[:, KV0 + k * KV_W:KV0 + (k + 1) * KV_W] = done.astype(MXU_DTYPE)
                carry_ref[:, lanes] = both[BLOCK:]
                gbin_ref[:, KV0 + k * KV_W:KV0 + (k + 1) * KV_W] += jnp.sum(both, axis=0, keepdims=True)
            kp_ref[...] = kc_ref[...]

        @pl.when(n == last)
        def _():
            for hh in range(N_SGU_HEADS):
                gw_ref[hh] = jnp.where(tril, gw_ref[hh], 0.0)
            head_of_lane = lax.broadcasted_iota(jnp.int32, (N_SGU_HEADS, SGU_W), 1) // HEAD_DIM
            select = (head_of_lane == lax.broadcasted_iota(jnp.int32, (N_SGU_HEADS, SGU_W), 0)).astype(F32)
            gb_ref[...] = lax.dot_general(select, gbias_ref[...], NT, precision=lax.Precision.HIGHEST,
                                          preferred_element_type=F32)

    at = lambda n: jnp.minimum(n, last)
    blk = lambda w: pl.BlockSpec((BLOCK, w), lambda n: (at(n), 0))
    tiles = pl.BlockSpec((N_PAIRS, BLOCK, LANES), lambda n: (0, at(n), 0))
    section = lambda k: pl.BlockSpec((None, BLOCK, SGU_W), lambda n: (k, at(n), 0))
    return pl.pallas_call(
        body,
        name="mixers_bwd",
        grid=(N_BLOCKS + 1,),
        in_specs=[pl.BlockSpec(memory_space=pltpu.SMEM),
                  section(0),
                  tiles,
                  blk(KVX_W),
                  tiles,
                  section(0),
                  section(1),
                  section(1), section(2), section(3),
                  _full((1, SGU_W)), _full((1, SGU_W)), _full((N_SGU_HEADS, BLOCK, BLOCK)), _full((BLOCK, SGU_W)),
                  VMEM_SPEC],
        out_specs=(pl.BlockSpec((BLOCK, ATTN_SECTION), lambda n: (jnp.maximum(n - 1, 0), 0)),
                   _full((1, LANES)), _full((1, ATTN_SECTION)),
                   pl.BlockSpec((BLOCK, SGU_SECTION), lambda n: (at(n), 0)),
                   _full((N_SGU_HEADS, BLOCK, BLOCK)), _full((N_SGU_HEADS, BLOCK)),
                   _full((8, SGU_W)), _full((1, SGU_SECTION)), VMEM_SPEC),
        out_shape=(jax.ShapeDtypeStruct((SEQ, ATTN_SECTION), MXU_DTYPE),
                   jax.ShapeDtypeStruct((1, LANES), F32),
                   jax.ShapeDtypeStruct((1, ATTN_SECTION), F32),
                   jax.ShapeDtypeStruct((SEQ, SGU_SECTION), MXU_DTYPE),
                   jax.ShapeDtypeStruct((N_SGU_HEADS, BLOCK, BLOCK), F32),
                   jax.ShapeDtypeStruct((N_SGU_HEADS, BLOCK), F32),
                   jax.ShapeDtypeStruct((8, SGU_W), F32),
                   jax.ShapeDtypeStruct((1, SGU_SECTION), F32),
                   jax.ShapeDtypeStruct((WOUT_ROWS, D_MODEL), F32)),
        scratch_shapes=([pltpu.VMEM((BLOCK, KVX_W), MXU_DTYPE),
                         pltpu.VMEM((BLOCK, 2 * ATTN_W), MXU_DTYPE), pltpu.VMEM((BLOCK, 2 * KV_W), F32),
                         pltpu.VMEM((N_SGU_HEADS, BLOCK, BLOCK), MXU_DTYPE),
                         pltpu.VMEM((N_SGU_HEADS, BLOCK, BLOCK), MXU_DTYPE), pltpu.VMEM((BLOCK, SGU_W), F32)]
                        + _reduce_scatter_scratch(WOUT_ROWS, D_MODEL, COMM_DTYPE) + _dma_sems(REDUCE_SEMS)),
        compiler_params=_params(("arbitrary",), VMEM_LIMIT),
    )(sinks, dmix, q, kvx, out, gates, dmix, gates, gates, gates, ln_g, ln_b, sgu_w, bias_full, gwout)


def _in_proj_bwd(dpa, dps, win_t, x, norm_g, gres, gwin_own, gwin_others, vec_parts):
    tm = TOKEN_TILE
    steps = SEQ // tm
    n_parts = len(vec_parts)

    def body(da_ref, ds_ref, w_ref, x_ref, g_ref, gres_ref, own_ref, others_ref, *rest):
        part_refs = rest[:n_parts]
        gx_ref, shard_ref, vec_out_ref, gng_ref, rc, vec_ref, ra_vec, slots, send_sems, recv_sems = rest[n_parts:]
        step = pl.program_id(0)
        copies = _Copies(send_sems, recv_sems)
        send, finish = _owner_sums_plan(copies, 0, own_ref, others_ref, rc, shard_ref)

        @pl.when(step == 0)
        def _():
            gng_ref[...] = jnp.zeros_like(gng_ref)
            send()

        dh = _dot(da_ref[...], w_ref[0:ATTN_SECTION, :]) + _dot(ds_ref[...], w_ref[ATTN_SECTION:, :])
        xv = x_ref[...]
        r = lax.rsqrt(jnp.mean(xv * xv, axis=-1, keepdims=True) + NORM_EPS)
        xn = xv * r
        gng_ref[...] += jnp.sum(dh * xn, axis=0, keepdims=True)
        dxn = dh * g_ref[...]
        gx_ref[...] = r * (dxn - xn * jnp.mean(dxn * xn, axis=-1, keepdims=True)) + gres_ref[...]

        @pl.when(step == steps - 1)
        def _():
            finish()
            _all_reduce_vectors(copies, OWNER_SEMS, gng_ref, *part_refs, vec_out_ref, vec_ref, ra_vec, slots)

    tile = lambda w: pl.BlockSpec((tm, w), lambda i: (i, 0))
    return pl.pallas_call(
        body,
        name="in_proj_bwd",
        grid=(steps,),
        in_specs=[tile(ATTN_SECTION), tile(SGU_SECTION), _full((IN_W, D_MODEL)), tile(D_MODEL),
                  _full((1, D_MODEL)), tile(D_MODEL), VMEM_SPEC, VMEM_SPEC] + [VMEM_SPEC] * n_parts,
        out_specs=(tile(D_MODEL), VMEM_SPEC, VMEM_SPEC),
        out_shape=(jax.ShapeDtypeStruct((SEQ, D_MODEL), F32),
                   jax.ShapeDtypeStruct((WIN_ROWS, D_MODEL), F32),
                   jax.ShapeDtypeStruct((VEC_ROWS, IN_W), F32)),
        scratch_shapes=([pltpu.VMEM((1, D_MODEL), F32), pltpu.VMEM((3 * WIN_ROWS, D_MODEL), COMM_DTYPE)]
                        + _vector_scratch() + _dma_sems(OWNER_SEMS + VECTOR_SEMS)),
        input_output_aliases={5: 0},
        compiler_params=_params(("arbitrary",), VMEM_LIMIT),
    )(dpa, dps, win_t, x, norm_g, gres, gwin_own, gwin_others, *vec_parts)


def _win_grad_pieces(rows):
    pieces = []
    for step in range(IN_W // rows):
        for owner in range(N_DEV):
            lo, hi = max(step * rows, owner * WIN_ROWS), min((step + 1) * rows, (owner + 1) * WIN_ROWS)
            if lo < hi:
                pieces.append((len(pieces), step, owner, lo, hi - lo))
    return pieces


def _win_grad(dpa, dps, h, gsguw):
    rows = 256
    n_attn = ATTN_SECTION // rows
    steps = IN_W // rows
    pieces = _win_grad_pieces(rows)
    class_rows = (N_DEV // 2) * WIN_ROWS

    def body(da_ref, ds_ref, h_ref, gsguw_ref, own_ref, others_ref, sguw_full_ref,
             chunks, sa, ra, sa_s, ra_s, sb_s, rc_s, landing, send_sems, recv_sems, give_sems, take_sems):
        step = pl.program_id(0)
        x, y, c = _place()
        copies = _Copies(send_sems, recv_sems)
        own_sguw = landing.at[_block_rows((x, y, c), SGUW_ROWS), :]
        start, exchange, finish = _reduce_scatter_plan(copies, 0, gsguw_ref, SGUW_ROWS, sa_s, ra_s, sb_s, rc_s,
                                                       own_sguw)
        gather = _gather_plan(copies, REDUCE_SEMS, landing, SGUW_ROWS)

        def class_rows_of(owner, first, n):
            return pl.ds((owner // 2) * WIN_ROWS + first - owner * WIN_ROWS, n)

        def to_sibling(piece):
            k, _, owner, first, n = piece
            at = class_rows_of(owner, first, n)
            return pltpu.make_async_remote_copy(src_ref=sa.at[at, :], dst_ref=ra.at[at, :], send_sem=give_sems.at[k],
                                                recv_sem=take_sems.at[k], device_id=(x, y, 1 - c), device_id_type=MESH)

        def give(piece):
            k, at_step, owner, first, n = piece

            @pl.when(c != owner % 2)
            def _():
                sa[class_rows_of(owner, first, n), :] = chunks[at_step % 2, pl.ds(first % rows, n), :].astype(sa.dtype)
                to_sibling(piece).start()

        def keep(piece):
            k, at_step, owner, first, n = piece
            px, py = owner // 4, (owner // 2) % 2

            @pl.when(c == owner % 2)
            def _():
                to_sibling(piece).wait_recv()
                total = (chunks[at_step % 2, pl.ds(first % rows, n), :]
                         + ra[class_rows_of(owner, first, n), :].astype(F32))
                relation = (x + px - 2 * x * px) + 2 * (y + py - 2 * y * py)

                @pl.when(relation == 0)
                def _():
                    own_ref[pl.ds(first - owner * WIN_ROWS, n), :] = total

                @pl.when(relation != 0)
                def _():
                    at = pl.multiple_of((relation - 1) * WIN_ROWS + first - owner * WIN_ROWS, 16)
                    others_ref[pl.ds(at, n), :] = total.astype(others_ref.dtype)

        pl.when(step == 0)(start)
        pl.when(step == 2)(exchange)

        @pl.when(step == 5)
        def _():
            finish()
            gather[0]()

        pl.when(step == 7)(gather[1])

        @pl.when(step < n_attn)
        def _():
            chunks[step % 2] = _dot(da_ref[...], h_ref[...], TN)

        @pl.when(step >= n_attn)
        def _():
            chunks[step % 2] = _dot(ds_ref[...], h_ref[...], TN)

        for at_step in range(steps):
            @pl.when(step == at_step)
            def _():
                for piece in pieces:
                    if piece[1] == at_step:
                        give(piece)
                    if piece[1] == at_step - 1:
                        keep(piece)

        @pl.when(step == steps - 1)
        def _():
            for piece in pieces:
                if piece[1] == steps - 1:
                    keep(piece)
            for piece in pieces:
                pl.when(c != piece[2] % 2)(to_sibling(piece).wait_send)
            gather[2]()
            sguw_full_ref[...] = landing[...]

    return pl.pallas_call(
        body,
        name="win_grad",
        grid=(steps,),
        in_specs=[pl.BlockSpec((SEQ, rows), lambda i: (0, jnp.minimum(i, n_attn - 1))),
                  pl.BlockSpec((SEQ, rows), lambda i: (0, jnp.maximum(i - n_attn, 0))),
                  _full((SEQ, D_MODEL)), VMEM_SPEC],
        out_specs=(VMEM_SPEC, VMEM_SPEC, _full((N_SGU_HEADS * BLOCK, BLOCK))),
        out_shape=(jax.ShapeDtypeStruct((WIN_ROWS, D_MODEL), F32),
                   jax.ShapeDtypeStruct((3 * WIN_ROWS, D_MODEL), COMM_DTYPE),
                   jax.ShapeDtypeStruct((N_SGU_HEADS * BLOCK, BLOCK), F32)),
        scratch_shapes=([pltpu.VMEM((2, rows, D_MODEL), F32),
                         pltpu.VMEM((class_rows, D_MODEL), COMM_DTYPE), pltpu.VMEM((class_rows, D_MODEL), COMM_DTYPE)]
                        + _reduce_scatter_scratch(SGUW_ROWS, BLOCK, F32)
                        + [pltpu.VMEM((N_SGU_HEADS * BLOCK, BLOCK), F32)]
                        + _dma_sems(REDUCE_SEMS + GATHER_SEMS) + _dma_sems(len(pieces))),
        compiler_params=_params(("arbitrary",), VMEM_LIMIT),
    )(dpa, dps, h, gsguw)


VEC_NORM_G, VEC_B_IN, VEC_SINKS, VEC_LN_G, VEC_LN_B, VEC_B_OUT, VEC_FINAL_G, VEC_LOSS, VEC_SGU_B = 0, 1, 2, 3, 4, 5, 6, 7, 8


def _adamw(w, g, m, v):
    m = ADAM_B1 * m + (1.0 - ADAM_B1) * g
    v = ADAM_B2 * v + (1.0 - ADAM_B2) * (g * g)
    m_hat = m / (1.0 - ADAM_B1 ** ADAM_STEP)
    v_hat = v / (1.0 - ADAM_B2 ** ADAM_STEP)
    delta = -ADAM_LR * (m_hat / (jnp.sqrt(v_hat) + ADAM_EPS) + ADAM_WD * w)
    return delta, m, v


def _adamw_shard(name, g, w, m, v, block_rows):
    def body(g_ref, w_ref, m_ref, v_ref, d_ref, nm_ref, nv_ref):
        d_ref[...], nm_ref[...], nv_ref[...] = _adamw(w_ref[...], g_ref[...], m_ref[...], v_ref[...])

    rows, cols = w.shape
    spec = pl.BlockSpec((block_rows, cols), lambda i: (i, 0))
    return pl.pallas_call(
        body,
        name=name,
        grid=(rows // block_rows,),
        in_specs=[spec] * 4,
        out_specs=(spec,) * 3,
        out_shape=(jax.ShapeDtypeStruct(w.shape, F32),) * 3,
        compiler_params=_params(("arbitrary",)),
    )(g, w, m, v)


VECTOR_SEMS = 4


def _vector_scratch():
    return [pltpu.VMEM((VEC_ROWS, IN_W), F32), pltpu.VMEM((VEC_ROWS, IN_W), F32),
            pltpu.VMEM((4 * VEC_ROWS, IN_W), F32)]


def _all_reduce_vectors(copies, sem0, gng_ref, gba_ref, gbs_ref, gsink_ref, gln_ref, gsgub_ref, vec4_ref, out_ref,
                        vec_ref, ra_vec, slots):
    x, y, c = _place()
    vec_ref[...] = jnp.zeros_like(vec_ref)
    vec_ref[VEC_NORM_G:VEC_NORM_G + 1, 0:D_MODEL] = gng_ref[...]
    vec_ref[VEC_B_IN:VEC_B_IN + 1, 0:ATTN_SECTION] = gba_ref[...]
    vec_ref[VEC_B_IN:VEC_B_IN + 1, ATTN_SECTION:IN_W] = gbs_ref[...]
    vec_ref[VEC_SINKS:VEC_SINKS + 1, 0:LANES] = gsink_ref[...]
    vec_ref[VEC_LN_G:VEC_LN_G + 1, 0:SGU_W] = gln_ref[0:1, :]
    vec_ref[VEC_LN_B:VEC_LN_B + 1, 0:SGU_W] = gln_ref[1:2, :]
    vec_ref[VEC_B_OUT:VEC_B_OUT + 1, 0:D_MODEL] = vec4_ref[2:3, :]
    vec_ref[VEC_FINAL_G:VEC_FINAL_G + 1, 0:D_MODEL] = vec4_ref[1:2, :]
    vec_ref[VEC_LOSS:VEC_LOSS + 1, 0:D_MODEL] = vec4_ref[0:1, :]
    vec_ref[VEC_SGU_B:VEC_SGU_B + N_SGU_HEADS, 0:BLOCK] = gsgub_ref[...]

    to_sibling = copies(sem0, vec_ref, ra_vec, (x, y, 1 - c))
    to_sibling.start()
    to_sibling.wait_recv()

    def chip_slot(place):
        return slots.at[pl.ds(pl.multiple_of((2 * place[0] + place[1]) * VEC_ROWS, 8), VEC_ROWS), :]

    mine = chip_slot((x, y))
    mine[...] = vec_ref[...] + ra_vec[...]
    to_chips = [copies(sem0 + i, mine, mine, (*_chip(rel), c)) for i, rel in enumerate(RELATIONS[1:], start=1)]
    for cp in to_chips:
        cp.start()
    for i, rel in enumerate(RELATIONS[1:], start=1):
        theirs = chip_slot(_chip(rel))
        copies(sem0 + i, theirs, theirs, (x, y, c)).wait_recv()
    out_ref[...] = ((slots[0:VEC_ROWS, :] + slots[VEC_ROWS:2 * VEC_ROWS, :])
                    + slots[2 * VEC_ROWS:3 * VEC_ROWS, :]) + slots[3 * VEC_ROWS:, :]
    to_sibling.wait_send()
    for cp in to_chips:
        cp.wait_send()


def _adamw_replicated(vec, gsguw, weights, m_state, v_state):
    n = len(SMALL)

    def body(*refs):
        vec_ref, gsguw_ref = refs[0], refs[1]
        w_refs, m_refs, v_refs = (refs[2 + k * n:2 + (k + 1) * n] for k in range(3))
        outs = refs[2 + 3 * n:]
        g_refs, d_refs, nm_refs, nv_refs = (outs[k * n:(k + 1) * n] for k in range(4))
        for i, (_, row, shape) in enumerate(SMALL):
            g = gsguw_ref[...] if row is None else vec_ref[row:row + shape[0], 0:shape[1]]
            g_refs[i][...] = g
            d_refs[i][...], nm_refs[i][...], nv_refs[i][...] = _adamw(
                w_refs[i][...], g, m_refs[i][...], v_refs[i][...])

    shapes = tuple(jax.ShapeDtypeStruct(shape, F32) for _, _, shape in SMALL)
    outs = pl.pallas_call(
        body,
        name="adamw_replicated",
        in_specs=[VMEM_SPEC] * (2 + 3 * n),
        out_specs=(VMEM_SPEC,) * (4 * n),
        out_shape=shapes * 4,
    )(vec, gsguw, *weights, *m_state, *v_state)
    return tuple(outs[k * n:(k + 1) * n] for k in range(4))


SMALL = (
    ("norm_g", VEC_NORM_G, (1, D_MODEL)),
    ("b_in", VEC_B_IN, (1, IN_W)),
    ("attn_sinks", VEC_SINKS, (1, N_Q_HEADS)),
    ("sgu_ln_g", VEC_LN_G, (1, SGU_W)),
    ("sgu_ln_b", VEC_LN_B, (1, SGU_W)),
    ("sgu_w", None, (N_SGU_HEADS * BLOCK, BLOCK)),
    ("sgu_b", VEC_SGU_B, (N_SGU_HEADS, BLOCK)),
    ("b_out", VEC_B_OUT, (1, D_MODEL)),
    ("final_norm_g", VEC_FINAL_G, (1, D_MODEL)),
)


def _local_grads(x, target, h, win_t, wout_shard, norm_g, b_in, attn_sinks, sgu_ln_g, sgu_ln_b, sgu_w, sgu_b, b_out,
                 final_g):
    sinks = attn_sinks.reshape(N_Q_HEADS)
    bias_full = jnp.repeat(sgu_b.T, HEAD_DIM, axis=1)
    q, kvx, gates, wout = _in_proj(h, b_in, win_t, wout_shard)
    out, gres, dmix, gwout, vec4 = _mixers_out_proj(sinks, q, kvx, gates, sgu_ln_g, sgu_ln_b, sgu_w, bias_full,
                                                    x, target, wout, b_out, final_g)
    dpa, gsink, gbin_a, dps, gsguw, gsgub, gln, gbin_s, gwout_shard = _mixers_bwd(
        sinks, dmix, q, kvx, out, gates, sgu_ln_g, sgu_ln_b, sgu_w, bias_full, gwout)
    gwin_own, gwin_others, gsguw_sum = _win_grad(dpa, dps, h, gsguw.reshape(N_SGU_HEADS * BLOCK, BLOCK))
    grad_x, gwin_shard, vec = _in_proj_bwd(dpa, dps, win_t, x, norm_g, gres, gwin_own, gwin_others,
                                           (gbin_a, gbin_s, gsink, gln, gsgub, vec4))
    return grad_x, gwin_shard, gwout_shard, gsguw_sum, vec


def kernel(x, norm_g, w_in, b_in, attn_sinks, sgu_ln_g, sgu_ln_b, sgu_w, sgu_b, w_out, b_out, final_norm_g, loss_target, m_norm_g, m_w_in, m_b_in, m_attn_sinks, m_sgu_ln_g, m_sgu_ln_b, m_sgu_w, m_sgu_b, m_w_out, m_b_out, m_final_norm_g, v_norm_g, v_w_in, v_b_in, v_attn_sinks, v_sgu_ln_g, v_sgu_ln_b, v_sgu_w, v_sgu_b, v_w_out, v_b_out, v_final_norm_g):
    given = dict(norm_g=norm_g, b_in=b_in, attn_sinks=attn_sinks, sgu_ln_g=sgu_ln_g, sgu_ln_b=sgu_ln_b,
                 sgu_w=sgu_w, sgu_b=sgu_b, b_out=b_out, final_norm_g=final_norm_g)
    m_given = dict(norm_g=m_norm_g, b_in=m_b_in, attn_sinks=m_attn_sinks, sgu_ln_g=m_sgu_ln_g,
                   sgu_ln_b=m_sgu_ln_b, sgu_w=m_sgu_w, sgu_b=m_sgu_b, b_out=m_b_out, final_norm_g=m_final_norm_g)
    v_given = dict(norm_g=v_norm_g, b_in=v_b_in, attn_sinks=v_attn_sinks, sgu_ln_g=v_sgu_ln_g,
                   sgu_ln_b=v_sgu_ln_b, sgu_w=v_sgu_w, sgu_b=v_sgu_b, b_out=v_b_out, final_norm_g=v_final_norm_g)

    win_t, h = _all_gather_win(w_in[0].T, x[0], norm_g)
    grad_x, gwin_t, gwout, gsguw, vec = _local_grads(
        x[0], loss_target[0], h, win_t, w_out[0], norm_g, b_in, attn_sinks, sgu_ln_g, sgu_ln_b, sgu_w[0], sgu_b[0],
        b_out, final_norm_g.reshape(1, D_MODEL))

    t = lambda a: a[0].T
    d_win, nm_win, nv_win = _adamw_shard("adamw_w_in", gwin_t, t(w_in), t(m_w_in), t(v_w_in), WIN_ROWS // 2)
    d_wout, nm_wout, nv_wout = _adamw_shard("adamw_w_out", gwout, w_out[0], m_w_out[0], v_w_out[0], WOUT_ROWS)
    as_2d = lambda d: [d[name].reshape(shape) for name, _, shape in SMALL]
    loss = vec[VEC_LOSS, 0]
    small = _adamw_replicated(vec, gsguw, as_2d(given), as_2d(m_given), as_2d(v_given))

    def assemble(big_in, big_out, k):
        vals = {name: small[k][i].reshape(given[name].shape) for i, (name, _, _) in enumerate(SMALL)}
        vals["w_in"] = big_in.T[None]
        vals["w_out"] = big_out[None]
        order = ("norm_g", "w_in", "b_in", "attn_sinks", "sgu_ln_g", "sgu_ln_b", "sgu_w", "sgu_b", "w_out",
                 "b_out", "final_norm_g")
        return [vals[name] for name in order]

    return (loss, grad_x[None],
            *assemble(gwin_t, gwout, 0), *assemble(d_win, d_wout, 1),
            *assemble(nm_win, nm_wout, 2), *assemble(nv_win, nv_wout, 3))
```

```python
import functools
import math

import jax
import jax.numpy as jnp
from jax import lax
from jax.experimental import pallas as pl
from jax.experimental.pallas import tpu as pltpu

F32 = jnp.float32
BF16 = jnp.bfloat16
MXU_DTYPE = BF16
COMM_DTYPE = BF16

D_MODEL = 1024
SEQ = 4096
HEAD_DIM = 64
N_Q_HEADS = 8
Q_PER_KV = 4
BLOCK = 128
N_BLOCKS = SEQ // BLOCK
ATTN_W = 512
KV_W = 128
SGU_W = 512
N_SGU_HEADS = 8
IN_W = 2816
NORM_EPS = 1e-5
NEG_INF = -1e30
SCALE = HEAD_DIM ** -0.5
KV0 = ATTN_W
GATE0 = ATTN_W + 2 * KV_W
SGU0 = GATE0 + ATTN_W
ATTN_SECTION = SGU0
SGU_SECTION = IN_W - SGU0

ADAM_LR = 0.001
ADAM_B1 = 0.9
ADAM_B2 = 0.999
ADAM_EPS = 1e-08
ADAM_WD = 0.01
ADAM_STEP = 10

N_DEV = 8
WIN_ROWS = IN_W // N_DEV
WOUT_ROWS = D_MODEL // N_DEV
SGUW_ROWS = N_SGU_HEADS * BLOCK // N_DEV
VEC_ROWS = 16
MESH = pl.DeviceIdType.MESH

LANES = 128
HALF = LANES // 2
N_PAIRS = N_Q_HEADS * HEAD_DIM // LANES
KVX_W = 12 * LANES
TOKEN_TILE = 256
FWD_TOKEN_TILE = 512
ATTN_FWD_AHEAD = 4
FUSED_BLOCKS = 2
SGU_MIX_AFTER_CHAIN = 0
SGU_GATES_AFTER_CHAIN = 1
SGU_GRADS_AFTER_CHAIN = 5
ATTN_BWD_AHEAD = 3
VMEM_LIMIT = 56 * 1024 * 1024

NN = (((1,), (0,)), ((), ()))
NT = (((1,), (1,)), ((), ()))
TN = (((0,), (0,)), ((), ()))


def _dot(a, b, dims=NN):
    return lax.dot_general(a.astype(MXU_DTYPE), b.astype(MXU_DTYPE), dims, preferred_element_type=F32)


def _gelu(x):
    return x * (lax.erf(x * (1.0 / math.sqrt(2.0))) + 1.0) * 0.5


def _gelu_grad(x):
    cdf = (lax.erf(x * (1.0 / math.sqrt(2.0))) + 1.0) * 0.5
    return cdf + x * jnp.exp(-0.5 * x * x) * (1.0 / math.sqrt(2.0 * math.pi))


def _silu_and_grad(z):
    s = jax.nn.sigmoid(z)
    return z * s, s * (1.0 + z * (1.0 - s))


def _params(semantics=None, vmem=None):
    kw = {}
    if semantics is not None:
        kw["dimension_semantics"] = semantics
    if vmem is not None:
        kw["vmem_limit_bytes"] = vmem
    return pltpu.CompilerParams(**kw)


def _full(shape):
    return pl.BlockSpec(shape, lambda *_: (0,) * len(shape))


VMEM_SPEC = pl.BlockSpec(memory_space=pltpu.VMEM)


RELATIONS = ((0, 0), (1, 0), (0, 1), (1, 1))


def _place():
    return lax.axis_index("x"), lax.axis_index("y"), lax.axis_index("c")


def _chip(rel):
    x, y, _ = _place()
    return (1 - x if rel[0] else x, 1 - y if rel[1] else y)


def _block_rows(place, n_rows):
    px, py, pc = place
    return pl.ds(pl.multiple_of((4 * px + 2 * py + pc) * n_rows, 16), n_rows)


class _Copies:
    def __init__(self, send_sems, recv_sems):
        self.send_sems, self.recv_sems = send_sems, recv_sems

    def __call__(self, k, src, dst, to):
        return pltpu.make_async_remote_copy(src_ref=src, dst_ref=dst, send_sem=self.send_sems.at[k],
                                            recv_sem=self.recv_sems.at[k], device_id=to, device_id_type=MESH)


def _gather_plan(copies, sem0, full_ref, n_rows):
    x, y, c = _place()
    me, sibling = (x, y, c), (x, y, 1 - c)
    chips = [_chip(rel) for rel in RELATIONS[1:]]

    def cp(k, block, to):
        rows = full_ref.at[_block_rows(block, n_rows), :]
        return copies(sem0 + k, rows, rows, to)

    first = [cp(0, me, sibling)] + [cp(1 + j, me, (*chip, c)) for j, chip in enumerate(chips)]
    passed = [cp(4 + j, (*chip, c), sibling) for j, chip in enumerate(chips)]

    def start():
        for f in first:
            f.start()

    def forward():
        for j, chip in enumerate(chips):
            cp(1 + j, (*chip, c), me).wait_recv()
            passed[j].start()

    def finish():
        cp(0, sibling, me).wait_recv()
        for j, chip in enumerate(chips):
            cp(4 + j, (*chip, 1 - c), me).wait_recv()
        for f in first + passed:
            f.wait_send()

    return start, forward, finish


GATHER_SEMS = 7


def _reduce_scatter_plan(copies, sem0, part_ref, n_rows, sa, ra, sb, rc, res_ref):
    x, y, c = _place()
    sibling = (x, y, 1 - c)
    n = n_rows
    level1 = copies(sem0, sa, ra, sibling)

    def level2(i):
        slot = pl.ds((i - 1) * n, n)
        return copies(sem0 + i, sb.at[slot, :], rc.at[slot, :], (*_chip(RELATIONS[i]), c))

    def start():
        for i, rel in enumerate(RELATIONS):
            sa[i * n:(i + 1) * n, :] = part_ref[_block_rows((*_chip(rel), 1 - c), n), :].astype(sa.dtype)
        level1.start()

    def exchange():
        level1.wait_recv()
        for i, rel in enumerate(RELATIONS):
            total = part_ref[_block_rows((*_chip(rel), c), n), :] + ra[i * n:(i + 1) * n, :].astype(F32)
            if i == 0:
                res_ref[...] = total
            else:
                sb[(i - 1) * n:i * n, :] = total.astype(sb.dtype)
                level2(i).start()

    def finish():
        acc = res_ref[...]
        for i in range(1, len(RELATIONS)):
            level2(i).wait_recv()
            acc = acc + rc[(i - 1) * n:i * n, :].astype(F32)
        res_ref[...] = acc
        level1.wait_send()
        for i in range(1, len(RELATIONS)):
            level2(i).wait_send()

    return start, exchange, finish


REDUCE_SEMS = 4


def _owner_sums_plan(copies, sem0, own_ref, sb, rc, res_ref):
    _, _, c = _place()
    n = own_ref.shape[0]

    def level2(i):
        slot = pl.ds((i - 1) * n, n)
        return copies(sem0 + i - 1, sb.at[slot, :], rc.at[slot, :], (*_chip(RELATIONS[i]), c))

    def send():
        for i in range(1, len(RELATIONS)):
            level2(i).start()

    def finish():
        acc = own_ref[...]
        for i in range(1, len(RELATIONS)):
            level2(i).wait_recv()
            acc = acc + rc[(i - 1) * n:i * n, :].astype(F32)
        res_ref[...] = acc
        for i in range(1, len(RELATIONS)):
            level2(i).wait_send()

    return send, finish


OWNER_SEMS = 3


def _reduce_scatter_scratch(n_rows, width, dtype):
    return [pltpu.VMEM((4 * n_rows, width), dtype), pltpu.VMEM((4 * n_rows, width), dtype),
            pltpu.VMEM((3 * n_rows, width), dtype), pltpu.VMEM((3 * n_rows, width), dtype)]


def _dma_sems(n):
    return [pltpu.SemaphoreType.DMA((n,)), pltpu.SemaphoreType.DMA((n,))]


def _all_gather_win(win_t_shard, x, norm_g):
    tm = FWD_TOKEN_TILE
    steps = SEQ // tm

    def body(win_ref, x_ref, g_ref, full_ref, h_ref, landing, send_sems, recv_sems):
        step = pl.program_id(0)
        start, forward, finish = _gather_plan(_Copies(send_sems, recv_sems), 0, landing, WIN_ROWS)

        @pl.when(step == 0)
        def _():
            landing[_block_rows(_place(), WIN_ROWS), :] = win_ref[...].astype(COMM_DTYPE)
            start()

        xv = x_ref[...]
        r = lax.rsqrt(jnp.mean(xv * xv, axis=-1, keepdims=True) + NORM_EPS)
        h_ref[...] = ((xv * r) * g_ref[...]).astype(MXU_DTYPE)

        @pl.when(step == steps - 1)
        def _():
            forward()
            finish()
            full_ref[...] = landing[...]

    return pl.pallas_call(
        body,
        name="all_gather_win",
        grid=(steps,),
        in_specs=[VMEM_SPEC, pl.BlockSpec((tm, D_MODEL), lambda i: (i, 0)), _full((1, D_MODEL))],
        out_specs=(_full((IN_W, D_MODEL)), pl.BlockSpec((tm, D_MODEL), lambda i: (i, 0))),
        out_shape=(jax.ShapeDtypeStruct((IN_W, D_MODEL), COMM_DTYPE),
                   jax.ShapeDtypeStruct((SEQ, D_MODEL), MXU_DTYPE)),
        scratch_shapes=[pltpu.VMEM((IN_W, D_MODEL), COMM_DTYPE)] + _dma_sems(GATHER_SEMS),
        compiler_params=_params(("arbitrary",), VMEM_LIMIT),
    )(win_t_shard, x, norm_g)


def _in_proj(h, b_in, win_t, wout_shard):
    tm = FWD_TOKEN_TILE
    steps = SEQ // tm

    def body(h_ref, b_ref, w_ref, wout_ref, q_ref, kvx_ref, gate_ref, wfull_ref, landing, send_sems, recv_sems):
        step = pl.program_id(0)
        start, forward, finish = _gather_plan(_Copies(send_sems, recv_sems), 0, landing, WOUT_ROWS)

        @pl.when(step == 0)
        def _():
            landing[_block_rows(_place(), WOUT_ROWS), :] = wout_ref[...].astype(COMM_DTYPE)
            start()

        pl.when(step == steps // 2)(forward)

        h = h_ref[...]

        def proj(lo, hi):
            return _dot(h, w_ref[lo:hi, :], NT) + b_ref[:, lo:hi]

        qs = proj(0, ATTN_W) * SCALE
        for pair in range(N_PAIRS):
            q_ref[pair] = qs[:, pair * LANES:(pair + 1) * LANES].astype(MXU_DTYPE)
        kv = proj(KV0, GATE0)
        low = lax.broadcasted_iota(jnp.int32, (tm, LANES), 1) < HALF
        for i in range(2):
            t = kv[:, i * LANES:(i + 1) * LANES]
            rot = pltpu.roll(t, HALF, 1)
            variants = (jnp.where(low, t, 0.0), jnp.where(low, 0.0, rot),
                        jnp.where(low, rot, 0.0), jnp.where(low, 0.0, t))
            for j, val in enumerate(variants):
                col = (4 * i + j) * LANES
                kvx_ref[:, col:col + LANES] = val.astype(MXU_DTYPE)
                if i == 1:
                    ones_elsewhere = jnp.where(low == (j % 2 == 0), val, 1.0)
                    kvx_ref[:, col + 4 * LANES:col + 5 * LANES] = ones_elsewhere.astype(MXU_DTYPE)
        for k in range(4):
            gate_ref[k] = proj(GATE0 + k * SGU_W, GATE0 + (k + 1) * SGU_W)

        @pl.when(step == steps - 1)
        def _():
            finish()
            wfull_ref[...] = landing[...]

    return pl.pallas_call(
        body,
        name="in_proj",
        grid=(steps,),
        in_specs=[pl.BlockSpec((tm, D_MODEL), lambda i: (i, 0)),
                  _full((1, IN_W)), _full((IN_W, D_MODEL)), VMEM_SPEC],
        out_specs=(pl.BlockSpec((N_PAIRS, tm, LANES), lambda i: (0, i, 0)),
                   pl.BlockSpec((tm, KVX_W), lambda i: (i, 0)),
                   pl.BlockSpec((4, tm, SGU_W), lambda i: (0, i, 0)),
                   _full((D_MODEL, D_MODEL))),
        out_shape=(jax.ShapeDtypeStruct((N_PAIRS, SEQ, LANES), MXU_DTYPE),
                   jax.ShapeDtypeStruct((SEQ, KVX_W), MXU_DTYPE),
                   jax.ShapeDtypeStruct((4, SEQ, SGU_W), F32),
                   jax.ShapeDtypeStruct((D_MODEL, D_MODEL), COMM_DTYPE)),
        scratch_shapes=[pltpu.VMEM((D_MODEL, D_MODEL), COMM_DTYPE)] + _dma_sems(GATHER_SEMS),
        compiler_params=_params(("arbitrary",), VMEM_LIMIT),
    )(h, b_in, win_t, wout_shard)


def _window_mask(n):
    qi = lax.broadcasted_iota(jnp.int32, (2 * BLOCK, 2 * BLOCK), 0) & (BLOCK - 1)
    p = lax.broadcasted_iota(jnp.int32, (2 * BLOCK, 2 * BLOCK), 1) - BLOCK
    in_window = jnp.logical_and(p <= qi, p > qi - BLOCK)
    return jnp.logical_and(in_window, jnp.logical_or(p >= 0, n > 0))


def _sink_column(sink_ref, g, par):
    return jnp.concatenate([jnp.full((BLOCK, 1), sink_ref[4 * g + par], F32),
                            jnp.full((BLOCK, 1), sink_ref[4 * g + 2 + par], F32)], axis=0)


def _kv_cat(kp_ref, kc_ref, var, with_ones):
    kcol, vcol = var * LANES, (var + (8 if with_ones else 4)) * LANES
    return (jnp.concatenate([kp_ref[:, kcol:kcol + LANES], kc_ref[:, kcol:kcol + LANES]], axis=0),
            jnp.concatenate([kp_ref[:, vcol:vcol + LANES], kc_ref[:, vcol:vcol + LANES]], axis=0))


def _softmax_numerator(s, sink):
    m = jnp.maximum(jnp.max(s, axis=1, keepdims=True), sink)
    return jnp.exp(s - m), m


def _mixers_out_proj(sinks, q, kvx, gates, ln_g, ln_b, sgu_w, bias_full, x, target, wout, b_out, final_g):
    tm = FUSED_BLOCKS * BLOCK
    n_tiles = SEQ // tm

    def body(sink_ref, q_ref, kc_ref, za_ref, us_ref, vs_ref, zs_ref, lng_ref, lnb_ref, w_ref, bias_ref,
             x_ref, t_ref, wout_ref, b_ref, gf_ref,
             out_ref, gres_ref, dmix_ref, gw_ref, vec_ref,
             kp_ref, wm_ref, mixed_next, mixed_cur, out_stage, gb_ref):
        step = pl.program_id(0)

        @pl.when(step == 0)
        def _():
            kp_ref[...] = jnp.zeros_like(kp_ref)
            _mask_sgu_weights(w_ref, wm_ref)
            gw_ref[...] = jnp.zeros_like(gw_ref)
            vec_ref[...] = jnp.zeros_like(vec_ref)
            mixed_cur[...] = jnp.zeros_like(mixed_cur)

        def mixers_block(b, after_chain=()):
            rows = slice(b * BLOCK, (b + 1) * BLOCK)
            kc = kc_ref.at[rows, :]
            u, _, _, vln = _sgu_activations(us_ref[rows, :], vs_ref[rows, :], lng_ref[...], lnb_ref[...])

            valid = _window_mask(step * FUSED_BLOCKS + b)[0:BLOCK]
            chains = [(g, par, i) for g in range(2) for par in range(2) for i in range(2)]
            kv = {(g, par): _kv_cat(kp_ref, kc, 2 * g + par, True) for g in range(2) for par in range(2)}
            scores, outs = {}, {}

            def issue_scores(k):
                g, par, i = chains[k]
                scores[k] = _dot(q_ref[2 * g + i, rows, :], kv[g, par][0], NT)

            ahead = ATTN_FWD_AHEAD
            for k in range(ahead):
                issue_scores(k)
            low = lax.broadcasted_iota(jnp.int32, (BLOCK, LANES), 1) < HALF
            for k, (g, par, i) in enumerate(chains):
                sink = sink_ref[4 * g + 2 * i + par]
                e, m = _softmax_numerator(jnp.where(valid, scores[k], NEG_INF), sink)
                if k + ahead < len(chains):
                    issue_scores(k + ahead)
                o = _dot(e, kv[g, par][1])
                outs[g, par, i] = o / (pltpu.roll(o, HALF, 1) + jnp.exp(sink - m))
                if k == SGU_MIX_AFTER_CHAIN:
                    mixed = _sgu_mix(vln, wm_ref, bias_ref)
                if k % 2 == 0 and k // 2 < len(after_chain):
                    after_chain[k // 2]()
            for pair in range(N_PAIRS):
                g, i = divmod(pair, 2)
                lanes = slice(pair * LANES, (pair + 1) * LANES)
                o = jnp.where(low, outs[g, 0, i], outs[g, 1, i])
                out_stage[pair, rows, :] = o
                gate, _ = _silu_and_grad(za_ref[rows, lanes])
                mixed_next[rows, lanes] = (o * gate).astype(MXU_DTYPE)
            kp_ref[...] = kc[...]
            for pair in range(N_SGU_HEADS // 2):
                cols = slice(pair * LANES, (pair + 1) * LANES)
                gate, _ = _silu_and_grad(zs_ref[rows, cols])
                mixed_next[rows, ATTN_W + pair * LANES:ATTN_W + (pair + 1) * LANES] = (
                    u[:, cols] * mixed[pair] * gate).astype(MXU_DTYPE)

        live = (step > 0).astype(F32)
        quarter = D_MODEL // 4
        columns = [None] * 4

        def project(j):
            def piece():
                columns[j] = _dot(mixed_cur[...], wout_ref[:, j * quarter:(j + 1) * quarter])
            return piece

        half_blocks = FUSED_BLOCKS // 2
        per_block = 4 // half_blocks
        for b in range(half_blocks):
            mixers_block(b, [project(j) for j in range(b * per_block, (b + 1) * per_block)])
        xo = x_ref[...] + jnp.concatenate(columns, axis=1) + b_ref[...]
        r = lax.rsqrt(jnp.mean(xo * xo, axis=-1, keepdims=True) + NORM_EPS)
        xn = xo * r
        gf = gf_ref[...]
        err = xn * gf - t_ref[...]
        loss = 0.5 * jnp.sum(jnp.mean(err * err, axis=-1, keepdims=True), axis=0, keepdims=True)
        dy = err * (1.0 / D_MODEL)
        dxn = dy * gf
        gres = r * (dxn - xn * jnp.mean(dxn * xn, axis=-1, keepdims=True))
        vec_ref[0:1, :] += jnp.broadcast_to(loss * live, (1, D_MODEL))
        vec_ref[1:2, :] += jnp.sum(dy * xn, axis=0, keepdims=True) * live
        vec_ref[2:3, :] += jnp.sum(gres, axis=0, keepdims=True) * live
        gres_ref[...] = gres
        gb_ref[...] = gres.astype(MXU_DTYPE)

        def branch_grad(k):
            def piece():
                dmix_ref[k] = _dot(gb_ref[...], wout_ref[k * ATTN_W:(k + 1) * ATTN_W, :], NT)
            return piece

        def weight_grad(k):
            def piece():
                rows = slice(k * ATTN_W, (k + 1) * ATTN_W)
                gw_ref[rows, :] += _dot(mixed_cur[:, rows], gb_ref[...], TN)
            return piece

        backward = [branch_grad(0), branch_grad(1), weight_grad(0), weight_grad(1)]
        for b in range(half_blocks):
            mixers_block(half_blocks + b, backward[b * per_block:(b + 1) * per_block])

        @pl.when(step < n_tiles)
        def _():
            out_ref[...] = out_stage[...]

        mixed_cur[...] = mixed_next[...]

    ahead_tile = lambda i: jnp.minimum(i, n_tiles - 1)
    behind_tile = lambda i: jnp.maximum(i - 1, 0)
    blk = lambda w: pl.BlockSpec((tm, w), lambda i: (ahead_tile(i), 0))
    tiles = pl.BlockSpec((N_PAIRS, tm, LANES), lambda i: (0, ahead_tile(i), 0))
    gate = lambda k: pl.BlockSpec((None, tm, SGU_W), lambda i: (k, ahead_tile(i), 0))
    behind = lambda w: pl.BlockSpec((tm, w), lambda i: (behind_tile(i), 0))
    return pl.pallas_call(
        body,
        name="mixers_out_proj",
        grid=(n_tiles + 1,),
        in_specs=[pl.BlockSpec(memory_space=pltpu.SMEM), tiles, blk(KVX_W), gate(0), gate(1), gate(2), gate(3),
                  _full((1, SGU_W)), _full((1, SGU_W)), _full((N_SGU_HEADS, BLOCK, BLOCK)), _full((BLOCK, SGU_W)),
                  behind(D_MODEL), behind(D_MODEL), _full((D_MODEL, D_MODEL)), _full((1, D_MODEL)),
                  _full((1, D_MODEL))],
        out_specs=(tiles, behind(D_MODEL), pl.BlockSpec((2, tm, ATTN_W), lambda i: (0, behind_tile(i), 0)),
                   _full((D_MODEL, D_MODEL)), _full((8, D_MODEL))),
        out_shape=(jax.ShapeDtypeStruct((N_PAIRS, SEQ, LANES), F32),
                   jax.ShapeDtypeStruct((SEQ, D_MODEL), F32),
                   jax.ShapeDtypeStruct((2, SEQ, ATTN_W), F32),
                   jax.ShapeDtypeStruct((D_MODEL, D_MODEL), F32),
                   jax.ShapeDtypeStruct((8, D_MODEL), F32)),
        scratch_shapes=[pltpu.VMEM((BLOCK, KVX_W), MXU_DTYPE), pltpu.VMEM((N_SGU_HEADS, BLOCK, BLOCK), MXU_DTYPE),
                        pltpu.VMEM((tm, D_MODEL), MXU_DTYPE), pltpu.VMEM((tm, D_MODEL), MXU_DTYPE),
                        pltpu.VMEM((N_PAIRS, tm, LANES), F32), pltpu.VMEM((tm, D_MODEL), MXU_DTYPE)],
        compiler_params=_params(("arbitrary",), VMEM_LIMIT),
    )(sinks, q, kvx, gates, gates, gates, gates, ln_g, ln_b, sgu_w, bias_full, x, target, wout, b_out, final_g)


def _sgu_activations(us, vs, lng, lnb):
    u = _gelu(us)
    vg = _gelu(vs)
    mu = jnp.mean(vg, axis=-1, keepdims=True)
    xc = vg - mu
    rstd = lax.rsqrt(jnp.mean(xc * xc, axis=-1, keepdims=True) + NORM_EPS)
    vhat = xc * rstd
    return u, vhat, rstd, vhat * lng + lnb


def _mask_sgu_weights(w_ref, masked_ref, transposed_ref=None):
    tril = (lax.broadcasted_iota(jnp.int32, (BLOCK, BLOCK), 0)
            >= lax.broadcasted_iota(jnp.int32, (BLOCK, BLOCK), 1))
    for hh in range(N_SGU_HEADS):
        w = jnp.where(tril, w_ref[hh], 0.0)
        masked_ref[hh] = w.astype(MXU_DTYPE)
        if transposed_ref is not None:
            transposed_ref[hh] = w.T.astype(MXU_DTYPE)


def _sgu_mix(vln, masked_w_ref, bias_ref):
    low = lax.broadcasted_iota(jnp.int32, (BLOCK, LANES), 1) < HALF
    mixed = []
    for pair in range(N_SGU_HEADS // 2):
        vp = vln[:, pair * LANES:(pair + 1) * LANES]
        mixed.append(_dot(masked_w_ref[2 * pair], jnp.where(low, vp, 0.0))
                     + _dot(masked_w_ref[2 * pair + 1], jnp.where(low, 0.0, vp))
                     + bias_ref[:, pair * LANES:(pair + 1) * LANES])
    return mixed


def _mixers_bwd(sinks, dmix, q, kvx, out, gates, ln_g, ln_b, sgu_w, bias_full, gwout):
    last = N_BLOCKS - 1

    def body(sink_ref, d_ref, q_ref, kc_ref, o_ref, za_ref, dsg_ref, us_ref, vs_ref, zs_ref, lng_ref, lnb_ref, w_ref,
             bias_ref, gwout_ref,
             dp_ref, gsink_ref, gbin_ref, dps_ref, gw_ref, gb_ref, gln_ref, gbins_ref, wout_shard_ref,
             kp_ref, pend_ref, carry_ref, wm_ref, wt_ref, gbias_ref, sa_w, ra_w, sb_w, rc_w, send_sems, recv_sems):
        n = pl.program_id(0)
        start, exchange, finish = _reduce_scatter_plan(_Copies(send_sems, recv_sems), 0, gwout_ref, WOUT_ROWS,
                                                       sa_w, ra_w, sb_w, rc_w, wout_shard_ref)
        tril = (lax.broadcasted_iota(jnp.int32, (BLOCK, BLOCK), 0)
                >= lax.broadcasted_iota(jnp.int32, (BLOCK, BLOCK), 1))

        @pl.when(n == 0)
        def _():
            gsink_ref[...] = jnp.zeros_like(gsink_ref)
            gbin_ref[...] = jnp.zeros_like(gbin_ref)
            carry_ref[...] = jnp.zeros_like(carry_ref)
            kp_ref[...] = jnp.zeros_like(kp_ref)
            gw_ref[...] = jnp.zeros_like(gw_ref)
            gln_ref[...] = jnp.zeros_like(gln_ref)
            gbins_ref[...] = jnp.zeros_like(gbins_ref)
            gbias_ref[...] = jnp.zeros_like(gbias_ref)
            _mask_sgu_weights(w_ref, wm_ref, wt_ref)
            start()

        pl.when(n == 3)(exchange)
        pl.when(n == 12)(finish)

        @pl.when(n > 0)
        def _():
            dp_ref[:, 0:ATTN_W] = pend_ref[:, 0:ATTN_W]
            dp_ref[:, GATE0:ATTN_SECTION] = pend_ref[:, ATTN_W:]

        @pl.when(n > last)
        def _():
            dp_ref[:, KV0:GATE0] = carry_ref[...].astype(MXU_DTYPE)

        @pl.when(n <= last)
        def _():
            us = us_ref[...]
            vs = vs_ref[...]
            lng = lng_ref[...]
            u, vhat, rstd, vln = _sgu_activations(us, vs, lng, lnb_ref[...])
            low_sgu = lax.broadcasted_iota(jnp.int32, (BLOCK, LANES), 1) < HALF
            sgu = {}

            def sgu_gates():
                mixed = _sgu_mix(vln, wm_ref, bias_ref)
                sgu["du"], sgu["dzs"], sgu["dm"] = [], [], []
                for pair in range(N_SGU_HEADS // 2):
                    cols = slice(pair * LANES, (pair + 1) * LANES)
                    dsg = dsg_ref[:, cols]
                    gate, gate_grad = _silu_and_grad(zs_ref[:, cols])
                    up = u[:, cols]
                    sgu["du"].append(dsg * mixed[pair] * gate)
                    sgu["dzs"].append(dsg * up * mixed[pair] * gate_grad)
                    dmixed = dsg * up * gate
                    gbias_ref[:, cols] += dmixed
                    sgu["dm"].append((jnp.where(low_sgu, dmixed, 0.0).astype(MXU_DTYPE),
                                      jnp.where(low_sgu, 0.0, dmixed).astype(MXU_DTYPE)))

            def sgu_grads():
                dvln_parts = []
                for pair in range(N_SGU_HEADS // 2):
                    dm_lo, dm_hi = sgu["dm"][pair]
                    vp = vln[:, pair * LANES:(pair + 1) * LANES]
                    gw_ref[2 * pair] += _dot(dm_lo, vp, NT)
                    gw_ref[2 * pair + 1] += _dot(dm_hi, vp, NT)
                    dvln_parts.append(_dot(wt_ref[2 * pair], dm_lo) + _dot(wt_ref[2 * pair + 1], dm_hi))
                dvln = jnp.concatenate(dvln_parts, axis=1)
                gln_ref[0:1, :] += jnp.sum(dvln * vhat, axis=0, keepdims=True)
                gln_ref[1:2, :] += jnp.sum(dvln, axis=0, keepdims=True)
                dvhat = dvln * lng
                dvg = rstd * (dvhat - jnp.mean(dvhat, axis=-1, keepdims=True)
                              - vhat * jnp.mean(dvhat * vhat, axis=-1, keepdims=True))
                dus = jnp.concatenate(sgu["du"], axis=1) * _gelu_grad(us)
                dvs = dvg * _gelu_grad(vs)
                dzs = jnp.concatenate(sgu["dzs"], axis=1)
                for k, val in enumerate((dus, dvs, dzs)):
                    dps_ref[:, k * SGU_W:(k + 1) * SGU_W] = val.astype(MXU_DTYPE)
                    gbins_ref[:, k * SGU_W:(k + 1) * SGU_W] += jnp.sum(val, axis=0, keepdims=True)

            valid = _window_mask(n)[0:BLOCK]
            low = lax.broadcasted_iota(jnp.int32, (BLOCK, LANES), 1) < HALF
            low_keys = lax.broadcasted_iota(jnp.int32, (2 * BLOCK, LANES), 1) < HALF
            lane_row = lax.broadcasted_iota(jnp.int32, (1, LANES), 1)
            gsink = jnp.zeros((1, LANES), F32)
            chains = [(g, par, i) for g in range(2) for par in range(2) for i in range(2)]
            kv = {(g, par): _kv_cat(kp_ref, kc_ref, 2 * g + par, False) for g in range(2) for par in range(2)}
            ones_keys = jnp.ones((2 * BLOCK, LANES), MXU_DTYPE)
            half_of_lane = lax.broadcasted_iota(jnp.int32, (LANES, 2 * LANES), 0) // HALF
            half_of_col = lax.broadcasted_iota(jnp.int32, (LANES, 2 * LANES), 1) // LANES
            sum_halves = (half_of_lane == half_of_col).astype(MXU_DTYPE)
            douts, deltas = [], []
            for pair in range(N_PAIRS):
                lanes = slice(pair * LANES, (pair + 1) * LANES)
                dg = d_ref[:, lanes]
                gate, gate_grad = _silu_and_grad(za_ref[:, lanes])
                o = o_ref[pair]
                dout = dg * gate
                dza = dg * o * gate_grad
                douts.append(dout.astype(MXU_DTYPE))
                deltas.append(_dot(dout * o, sum_halves))
                zl = slice(ATTN_W + pair * LANES, ATTN_W + (pair + 1) * LANES)
                pend_ref[:, zl] = dza.astype(MXU_DTYPE)
                gl = slice(GATE0 + pair * LANES, GATE0 + (pair + 1) * LANES)
                gbin_ref[:, gl] += jnp.sum(dza, axis=0, keepdims=True)

            first = {}

            def issue_first(k):
                g, par, i = chains[k]
                first[k] = (_dot(q_ref[2 * g + i], kv[g, par][0], NT), _dot(douts[2 * g + i], kv[g, par][1], NT))

            numerators = {}

            def issue_row_sums(k):
                g, par, i = chains[k]
                sink = sink_ref[4 * g + 2 * i + par]
                e, m = _softmax_numerator(jnp.where(valid, first[k][0], NEG_INF), sink)
                numerators[k] = (e, jnp.exp(sink - m), _dot(e, ones_keys))

            ahead = ATTN_BWD_AHEAD
            for k in range(ahead):
                issue_first(k)
            issue_row_sums(0)
            issue_row_sums(1)
            dqs, dk_parts, dv_parts = {}, {}, {}
            operands = {}

            def issue_last(k):
                g, par, i = chains[k]
                ds, ds_t, p_t = operands.pop(k)
                dq = _dot(ds, kv[g, par][0])
                dqs[g, i] = dq if par == 0 else dqs[g, i] + dq
                dk = _dot(ds_t, q_ref[2 * g + i])
                dv = _dot(p_t, douts[2 * g + i])
                dk_parts[g, par] = dk if i == 0 else dk_parts[g, par] + dk
                dv_parts[g, par] = dv if i == 0 else dv_parts[g, par] + dv

            for k, (g, par, i) in enumerate(chains):
                h = 4 * g + 2 * i + par
                delta = deltas[2 * g + i][:, par * LANES:(par + 1) * LANES]
                e, at_sink, row_sum = numerators[k]
                inv = 1.0 / (row_sum + at_sink)
                p = e * jnp.tile(inv, (1, 2))
                ds = p * (first[k][1] - jnp.tile(delta, (1, 2)))
                ds = ds.astype(MXU_DTYPE)
                operands[k] = (ds, ds.T, p.astype(MXU_DTYPE).T)
                total = jnp.sum(at_sink * inv * delta, axis=0, keepdims=True)
                gsink = jnp.where(lane_row == h, -total, gsink)
                if k + ahead < len(chains):
                    issue_first(k + ahead)
                if k + 2 < len(chains):
                    issue_row_sums(k + 2)
                if k > 0:
                    issue_last(k - 1)
                if k == SGU_GATES_AFTER_CHAIN:
                    sgu_gates()
                if k == SGU_GRADS_AFTER_CHAIN:
                    sgu_grads()
            issue_last(len(chains) - 1)
            for pair in range(N_PAIRS):
                g, i = divmod(pair, 2)
                dq = dqs[g, i] * SCALE
                lanes = slice(pair * LANES, (pair + 1) * LANES)
                pend_ref[:, lanes] = dq.astype(MXU_DTYPE)
                gbin_ref[:, lanes] += jnp.sum(dq, axis=0, keepdims=True)
            gsink_ref[...] += gsink
            for k, parts in enumerate((dk_parts, dv_parts)):
                masked = {key: jnp.where(low_keys if key[1] == 0 else jnp.logical_not(low_keys), val, 0.0)
                          for key, val in parts.items()}
                both = (masked[0, 0] + masked[1, 1]
                        + pltpu.roll(masked[0, 1] + masked[1, 0], HALF, 1))
                lanes = slice(k * KV_W, (k + 1) * KV_W)
                done = carry_ref[:, lanes] + both[0:BLOCK]
                dp_ref[:, KV0 + k * KV_W:KV0 + (k + 1) * KV_W] = done.astype(MXU_DTYPE)
                carry_ref[:, lanes] = both[BLOCK:]
                gbin_ref[:, KV0 + k * KV_W:KV0 + (k + 1) * KV_W] += jnp.sum(both, axis=0, keepdims=True)
            kp_ref[...] = kc_ref[...]

        @pl.when(n == last)
        def _():
            for hh in range(N_SGU_HEADS):
                gw_ref[hh] = jnp.where(tril, gw_ref[hh], 0.0)
            head_of_lane = lax.broadcasted_iota(jnp.int32, (N_SGU_HEADS, SGU_W), 1) // HEAD_DIM
            select = (head_of_lane == lax.broadcasted_iota(jnp.int32, (N_SGU_HEADS, SGU_W), 0)).astype(F32)
            gb_ref[...] = lax.dot_general(select, gbias_ref[...], NT, precision=lax.Precision.HIGHEST,
                                          preferred_element_type=F32)

    at = lambda n: jnp.minimum(n, last)
    blk = lambda w: pl.BlockSpec((BLOCK, w), lambda n: (at(n), 0))
    tiles = pl.BlockSpec((N_PAIRS, BLOCK, LANES), lambda n: (0, at(n), 0))
    section = lambda k: pl.BlockSpec((None, BLOCK, SGU_W), lambda n: (k, at(n), 0))
    return pl.pallas_call(
        body,
        name="mixers_bwd",
        grid=(N_BLOCKS + 1,),
        in_specs=[pl.BlockSpec(memory_space=pltpu.SMEM),
                  section(0),
                  tiles,
                  blk(KVX_W),
                  tiles,
                  section(0),
                  section(1),
                  section(1), section(2), section(3),
                  _full((1, SGU_W)), _full((1, SGU_W)), _full((N_SGU_HEADS, BLOCK, BLOCK)), _full((BLOCK, SGU_W)),
                  VMEM_SPEC],
        out_specs=(pl.BlockSpec((BLOCK, ATTN_SECTION), lambda n: (jnp.maximum(n - 1, 0), 0)),
                   _full((1, LANES)), _full((1, ATTN_SECTION)),
                   pl.BlockSpec((BLOCK, SGU_SECTION), lambda n: (at(n), 0)),
                   _full((N_SGU_HEADS, BLOCK, BLOCK)), _full((N_SGU_HEADS, BLOCK)),
                   _full((8, SGU_W)), _full((1, SGU_SECTION)), VMEM_SPEC),
        out_shape=(jax.ShapeDtypeStruct((SEQ, ATTN_SECTION), MXU_DTYPE),
                   jax.ShapeDtypeStruct((1, LANES), F32),
                   jax.ShapeDtypeStruct((1, ATTN_SECTION), F32),
                   jax.ShapeDtypeStruct((SEQ, SGU_SECTION), MXU_DTYPE),
                   jax.ShapeDtypeStruct((N_SGU_HEADS, BLOCK, BLOCK), F32),
                   jax.ShapeDtypeStruct((N_SGU_HEADS, BLOCK), F32),
                   jax.ShapeDtypeStruct((8, SGU_W), F32),
                   jax.ShapeDtypeStruct((1, SGU_SECTION), F32),
                   jax.ShapeDtypeStruct((WOUT_ROWS, D_MODEL), F32)),
        scratch_shapes=([pltpu.VMEM((BLOCK, KVX_W), MXU_DTYPE),
                         pltpu.VMEM((BLOCK, 2 * ATTN_W), MXU_DTYPE), pltpu.VMEM((BLOCK, 2 * KV_W), F32),
                         pltpu.VMEM((N_SGU_HEADS, BLOCK, BLOCK), MXU_DTYPE),
                         pltpu.VMEM((N_SGU_HEADS, BLOCK, BLOCK), MXU_DTYPE), pltpu.VMEM((BLOCK, SGU_W), F32)]
                        + _reduce_scatter_scratch(WOUT_ROWS, D_MODEL, COMM_DTYPE) + _dma_sems(REDUCE_SEMS)),
        compiler_params=_params(("arbitrary",), VMEM_LIMIT),
    )(sinks, dmix, q, kvx, out, gates, dmix, gates, gates, gates, ln_g, ln_b, sgu_w, bias_full, gwout)


def _in_proj_bwd(dpa, dps, win_t, x, norm_g, gres, gwin_own, gwin_others, vec_parts):
    tm = TOKEN_TILE
    steps = SEQ // tm
    n_parts = len(vec_parts)

    def body(da_ref, ds_ref, w_hbm, x_ref, g_ref, gres_ref, own_ref, others_ref, *rest):
        part_refs = rest[:n_parts]
        (gx_ref, shard_ref, vec_out_ref, gng_ref, w_ref, w_sem, rc, vec_ref, ra_vec, slots,
         send_sems, recv_sems) = rest[n_parts:]
        step = pl.program_id(0)
        copies = _Copies(send_sems, recv_sems)
        send, finish = _owner_sums_plan(copies, 0, own_ref, others_ref, rc, shard_ref)

        @pl.when(step == 0)
        def _():
            gng_ref[...] = jnp.zeros_like(gng_ref)
            send()
            load_w = pltpu.make_async_copy(w_hbm, w_ref, w_sem.at[0])
            load_w.start()
            load_w.wait()

        dh = _dot(da_ref[...], w_ref[0:ATTN_SECTION, :]) + _dot(ds_ref[...], w_ref[ATTN_SECTION:, :])
        xv = x_ref[...]
        r = lax.rsqrt(jnp.mean(xv * xv, axis=-1, keepdims=True) + NORM_EPS)
        xn = xv * r
        gng_ref[...] += jnp.sum(dh * xn, axis=0, keepdims=True)
        dxn = dh * g_ref[...]
        gx_ref[...] = r * (dxn - xn * jnp.mean(dxn * xn, axis=-1, keepdims=True)) + gres_ref[...]

        @pl.when(step == steps - 1)
        def _():
            finish()
            _all_reduce_vectors(copies, OWNER_SEMS, gng_ref, *part_refs, vec_out_ref, vec_ref, ra_vec, slots)

    tile = lambda w: pl.BlockSpec((tm, w), lambda i: (i, 0))
    return pl.pallas_call(
        body,
        name="in_proj_bwd",
        grid=(steps,),
        in_specs=[tile(ATTN_SECTION), tile(SGU_SECTION), pl.BlockSpec(memory_space=pl.ANY), tile(D_MODEL),
                  _full((1, D_MODEL)), tile(D_MODEL), VMEM_SPEC, VMEM_SPEC] + [VMEM_SPEC] * n_parts,
        out_specs=(tile(D_MODEL), VMEM_SPEC, VMEM_SPEC),
        out_shape=(jax.ShapeDtypeStruct((SEQ, D_MODEL), F32),
                   jax.ShapeDtypeStruct((WIN_ROWS, D_MODEL), F32),
                   jax.ShapeDtypeStruct((VEC_ROWS, IN_W), F32)),
        scratch_shapes=([pltpu.VMEM((1, D_MODEL), F32), pltpu.VMEM((IN_W, D_MODEL), win_t.dtype),
                         pltpu.SemaphoreType.DMA((1,)), pltpu.VMEM((3 * WIN_ROWS, D_MODEL), COMM_DTYPE)]
                        + _vector_scratch() + _dma_sems(OWNER_SEMS + VECTOR_SEMS)),
        input_output_aliases={5: 0},
        compiler_params=_params(("arbitrary",), VMEM_LIMIT),
    )(dpa, dps, win_t, x, norm_g, gres, gwin_own, gwin_others, *vec_parts)


def _win_grad_pieces(rows):
    pieces = []
    for step in range(IN_W // rows):
        for owner in range(N_DEV):
            lo, hi = max(step * rows, owner * WIN_ROWS), min((step + 1) * rows, (owner + 1) * WIN_ROWS)
            if lo < hi:
                pieces.append((len(pieces), step, owner, lo, hi - lo))
    return pieces


def _win_grad(dpa, dps, h, gsguw):
    rows = 256
    n_attn = ATTN_SECTION // rows
    steps = IN_W // rows
    pieces = _win_grad_pieces(rows)
    class_rows = (N_DEV // 2) * WIN_ROWS

    def body(da_ref, ds_ref, h_ref, gsguw_ref, own_ref, others_ref, sguw_full_ref,
             chunks, sa, ra, sa_s, ra_s, sb_s, rc_s, landing, send_sems, recv_sems, give_sems, take_sems):
        step = pl.program_id(0)
        x, y, c = _place()
        copies = _Copies(send_sems, recv_sems)
        own_sguw = landing.at[_block_rows((x, y, c), SGUW_ROWS), :]
        start, exchange, finish = _reduce_scatter_plan(copies, 0, gsguw_ref, SGUW_ROWS, sa_s, ra_s, sb_s, rc_s,
                                                       own_sguw)
        gather = _gather_plan(copies, REDUCE_SEMS, landing, SGUW_ROWS)

        def class_rows_of(owner, first, n):
            return pl.ds((owner // 2) * WIN_ROWS + first - owner * WIN_ROWS, n)

        def to_sibling(piece):
            k, _, owner, first, n = piece
            at = class_rows_of(owner, first, n)
            return pltpu.make_async_remote_copy(src_ref=sa.at[at, :], dst_ref=ra.at[at, :], send_sem=give_sems.at[k],
                                                recv_sem=take_sems.at[k], device_id=(x, y, 1 - c), device_id_type=MESH)

        def give(piece):
            k, at_step, owner, first, n = piece

            @pl.when(c != owner % 2)
            def _():
                sa[class_rows_of(owner, first, n), :] = chunks[at_step % 2, pl.ds(first % rows, n), :].astype(sa.dtype)
                to_sibling(piece).start()

        def keep(piece):
            k, at_step, owner, first, n = piece
            px, py = owner // 4, (owner // 2) % 2

            @pl.when(c == owner % 2)
            def _():
                to_sibling(piece).wait_recv()
                total = (chunks[at_step % 2, pl.ds(first % rows, n), :]
                         + ra[class_rows_of(owner, first, n), :].astype(F32))
                relation = (x + px - 2 * x * px) + 2 * (y + py - 2 * y * py)

                @pl.when(relation == 0)
                def _():
                    own_ref[pl.ds(first - owner * WIN_ROWS, n), :] = total

                @pl.when(relation != 0)
                def _():
                    at = pl.multiple_of((relation - 1) * WIN_ROWS + first - owner * WIN_ROWS, 16)
                    others_ref[pl.ds(at, n), :] = total.astype(others_ref.dtype)

        pl.when(step == 0)(start)
        pl.when(step == 2)(exchange)

        @pl.when(step == 5)
        def _():
            finish()
            gather[0]()

        pl.when(step == 7)(gather[1])

        @pl.when(step < n_attn)
        def _():
            chunks[step % 2] = _dot(da_ref[...], h_ref[...], TN)

        @pl.when(step >= n_attn)
        def _():
            chunks[step % 2] = _dot(ds_ref[...], h_ref[...], TN)

        for at_step in range(steps):
            @pl.when(step == at_step)
            def _():
                for piece in pieces:
                    if piece[1] == at_step:
                        give(piece)
                    if piece[1] == at_step - 1:
                        keep(piece)

        @pl.when(step == steps - 1)
        def _():
            for piece in pieces:
                if piece[1] == steps - 1:
                    keep(piece)
            for piece in pieces:
                pl.when(c != piece[2] % 2)(to_sibling(piece).wait_send)
            gather[2]()
            sguw_full_ref[...] = landing[...]

    return pl.pallas_call(
        body,
        name="win_grad",
        grid=(steps,),
        in_specs=[pl.BlockSpec((SEQ, rows), lambda i: (0, jnp.minimum(i, n_attn - 1))),
                  pl.BlockSpec((SEQ, rows), lambda i: (0, jnp.maximum(i - n_attn, 0))),
                  _full((SEQ, D_MODEL)), VMEM_SPEC],
        out_specs=(VMEM_SPEC, VMEM_SPEC, _full((N_SGU_HEADS * BLOCK, BLOCK))),
        out_shape=(jax.ShapeDtypeStruct((WIN_ROWS, D_MODEL), F32),
                   jax.ShapeDtypeStruct((3 * WIN_ROWS, D_MODEL), COMM_DTYPE),
                   jax.ShapeDtypeStruct((N_SGU_HEADS * BLOCK, BLOCK), F32)),
        scratch_shapes=([pltpu.VMEM((2, rows, D_MODEL), F32),
                         pltpu.VMEM((class_rows, D_MODEL), COMM_DTYPE), pltpu.VMEM((class_rows, D_MODEL), COMM_DTYPE)]
                        + _reduce_scatter_scratch(SGUW_ROWS, BLOCK, F32)
                        + [pltpu.VMEM((N_SGU_HEADS * BLOCK, BLOCK), F32)]
                        + _dma_sems(REDUCE_SEMS + GATHER_SEMS) + _dma_sems(len(pieces))),
        compiler_params=_params(("arbitrary",), VMEM_LIMIT),
    )(dpa, dps, h, gsguw)


VEC_NORM_G, VEC_B_IN, VEC_SINKS, VEC_LN_G, VEC_LN_B, VEC_B_OUT, VEC_FINAL_G, VEC_LOSS, VEC_SGU_B = 0, 1, 2, 3, 4, 5, 6, 7, 8


def _adamw(w, g, m, v):
    m = ADAM_B1 * m + (1.0 - ADAM_B1) * g
    v = ADAM_B2 * v + (1.0 - ADAM_B2) * (g * g)
    m_hat = m / (1.0 - ADAM_B1 ** ADAM_STEP)
    v_hat = v / (1.0 - ADAM_B2 ** ADAM_STEP)
    delta = -ADAM_LR * (m_hat / (jnp.sqrt(v_hat) + ADAM_EPS) + ADAM_WD * w)
    return delta, m, v


def _adamw_shard(name, g, w, m, v, block_rows):
    def body(g_ref, w_ref, m_ref, v_ref, d_ref, nm_ref, nv_ref):
        d_ref[...], nm_ref[...], nv_ref[...] = _adamw(w_ref[...], g_ref[...], m_ref[...], v_ref[...])

    rows, cols = w.shape
    spec = pl.BlockSpec((block_rows, cols), lambda i: (i, 0))
    return pl.pallas_call(
        body,
        name=name,
        grid=(rows // block_rows,),
        in_specs=[spec] * 4,
        out_specs=(spec,) * 3,
        out_shape=(jax.ShapeDtypeStruct(w.shape, F32),) * 3,
        compiler_params=_params(("arbitrary",)),
    )(g, w, m, v)


VECTOR_SEMS = 4


def _vector_scratch():
    return [pltpu.VMEM((VEC_ROWS, IN_W), F32), pltpu.VMEM((VEC_ROWS, IN_W), F32),
            pltpu.VMEM((4 * VEC_ROWS, IN_W), F32)]


def _all_reduce_vectors(copies, sem0, gng_ref, gba_ref, gbs_ref, gsink_ref, gln_ref, gsgub_ref, vec4_ref, out_ref,
                        vec_ref, ra_vec, slots):
    x, y, c = _place()
    vec_ref[...] = jnp.zeros_like(vec_ref)
    vec_ref[VEC_NORM_G:VEC_NORM_G + 1, 0:D_MODEL] = gng_ref[...]
    vec_ref[VEC_B_IN:VEC_B_IN + 1, 0:ATTN_SECTION] = gba_ref[...]
    vec_ref[VEC_B_IN:VEC_B_IN + 1, ATTN_SECTION:IN_W] = gbs_ref[...]
    vec_ref[VEC_SINKS:VEC_SINKS + 1, 0:LANES] = gsink_ref[...]
    vec_ref[VEC_LN_G:VEC_LN_G + 1, 0:SGU_W] = gln_ref[0:1, :]
    vec_ref[VEC_LN_B:VEC_LN_B + 1, 0:SGU_W] = gln_ref[1:2, :]
    vec_ref[VEC_B_OUT:VEC_B_OUT + 1, 0:D_MODEL] = vec4_ref[2:3, :]
    vec_ref[VEC_FINAL_G:VEC_FINAL_G + 1, 0:D_MODEL] = vec4_ref[1:2, :]
    vec_ref[VEC_LOSS:VEC_LOSS + 1, 0:D_MODEL] = vec4_ref[0:1, :]
    vec_ref[VEC_SGU_B:VEC_SGU_B + N_SGU_HEADS, 0:BLOCK] = gsgub_ref[...]

    to_sibling = copies(sem0, vec_ref, ra_vec, (x, y, 1 - c))
    to_sibling.start()
    to_sibling.wait_recv()

    def chip_slot(place):
        return slots.at[pl.ds(pl.multiple_of((2 * place[0] + place[1]) * VEC_ROWS, 8), VEC_ROWS), :]

    mine = chip_slot((x, y))
    mine[...] = vec_ref[...] + ra_vec[...]
    to_chips = [copies(sem0 + i, mine, mine, (*_chip(rel), c)) for i, rel in enumerate(RELATIONS[1:], start=1)]
    for cp in to_chips:
        cp.start()
    for i, rel in enumerate(RELATIONS[1:], start=1):
        theirs = chip_slot(_chip(rel))
        copies(sem0 + i, theirs, theirs, (x, y, c)).wait_recv()
    out_ref[...] = ((slots[0:VEC_ROWS, :] + slots[VEC_ROWS:2 * VEC_ROWS, :])
                    + slots[2 * VEC_ROWS:3 * VEC_ROWS, :]) + slots[3 * VEC_ROWS:, :]
    to_sibling.wait_send()
    for cp in to_chips:
        cp.wait_send()


def _adamw_replicated(vec, gsguw, weights, m_state, v_state):
    n = len(SMALL)

    def body(*refs):
        vec_ref, gsguw_ref = refs[0], refs[1]
        w_refs, m_refs, v_refs = (refs[2 + k * n:2 + (k + 1) * n] for k in range(3))
        outs = refs[2 + 3 * n:]
        g_refs, d_refs, nm_refs, nv_refs = (outs[k * n:(k + 1) * n] for k in range(4))
        for i, (_, row, shape) in enumerate(SMALL):
            g = gsguw_ref[...] if row is None else vec_ref[row:row + shape[0], 0:shape[1]]
            g_refs[i][...] = g
            d_refs[i][...], nm_refs[i][...], nv_refs[i][...] = _adamw(
                w_refs[i][...], g, m_refs[i][...], v_refs[i][...])

    shapes = tuple(jax.ShapeDtypeStruct(shape, F32) for _, _, shape in SMALL)
    outs = pl.pallas_call(
        body,
        name="adamw_replicated",
        in_specs=[VMEM_SPEC] * (2 + 3 * n),
        out_specs=(VMEM_SPEC,) * (4 * n),
        out_shape=shapes * 4,
    )(vec, gsguw, *weights, *m_state, *v_state)
    return tuple(outs[k * n:(k + 1) * n] for k in range(4))


SMALL = (
    ("norm_g", VEC_NORM_G, (1, D_MODEL)),
    ("b_in", VEC_B_IN, (1, IN_W)),
    ("attn_sinks", VEC_SINKS, (1, N_Q_HEADS)),
    ("sgu_ln_g", VEC_LN_G, (1, SGU_W)),
    ("sgu_ln_b", VEC_LN_B, (1, SGU_W)),
    ("sgu_w", None, (N_SGU_HEADS * BLOCK, BLOCK)),
    ("sgu_b", VEC_SGU_B, (N_SGU_HEADS, BLOCK)),
    ("b_out", VEC_B_OUT, (1, D_MODEL)),
    ("final_norm_g", VEC_FINAL_G, (1, D_MODEL)),
)


def _local_grads(x, target, h, win_t, wout_shard, norm_g, b_in, attn_sinks, sgu_ln_g, sgu_ln_b, sgu_w, sgu_b, b_out,
                 final_g):
    sinks = attn_sinks.reshape(N_Q_HEADS)
    bias_full = jnp.repeat(sgu_b.T, HEAD_DIM, axis=1)
    q, kvx, gates, wout = _in_proj(h, b_in, win_t, wout_shard)
    out, gres, dmix, gwout, vec4 = _mixers_out_proj(sinks, q, kvx, gates, sgu_ln_g, sgu_ln_b, sgu_w, bias_full,
                                                    x, target, wout, b_out, final_g)
    dpa, gsink, gbin_a, dps, gsguw, gsgub, gln, gbin_s, gwout_shard = _mixers_bwd(
        sinks, dmix, q, kvx, out, gates, sgu_ln_g, sgu_ln_b, sgu_w, bias_full, gwout)
    gwin_own, gwin_others, gsguw_sum = _win_grad(dpa, dps, h, gsguw.reshape(N_SGU_HEADS * BLOCK, BLOCK))
    grad_x, gwin_shard, vec = _in_proj_bwd(dpa, dps, win_t, x, norm_g, gres, gwin_own, gwin_others,
                                           (gbin_a, gbin_s, gsink, gln, gsgub, vec4))
    return grad_x, gwin_shard, gwout_shard, gsguw_sum, vec


def kernel(x, norm_g, w_in, b_in, attn_sinks, sgu_ln_g, sgu_ln_b, sgu_w, sgu_b, w_out, b_out, final_norm_g, loss_target, m_norm_g, m_w_in, m_b_in, m_attn_sinks, m_sgu_ln_g, m_sgu_ln_b, m_sgu_w, m_sgu_b, m_w_out, m_b_out, m_final_norm_g, v_norm_g, v_w_in, v_b_in, v_attn_sinks, v_sgu_ln_g, v_sgu_ln_b, v_sgu_w, v_sgu_b, v_w_out, v_b_out, v_final_norm_g):
    given = dict(norm_g=norm_g, b_in=b_in, attn_sinks=attn_sinks, sgu_ln_g=sgu_ln_g, sgu_ln_b=sgu_ln_b,
                 sgu_w=sgu_w, sgu_b=sgu_b, b_out=b_out, final_norm_g=final_norm_g)
    m_given = dict(norm_g=m_norm_g, b_in=m_b_in, attn_sinks=m_attn_sinks, sgu_ln_g=m_sgu_ln_g,
                   sgu_ln_b=m_sgu_ln_b, sgu_w=m_sgu_w, sgu_b=m_sgu_b, b_out=m_b_out, final_norm_g=m_final_norm_g)
    v_given = dict(norm_g=v_norm_g, b_in=v_b_in, attn_sinks=v_attn_sinks, sgu_ln_g=v_sgu_ln_g,
                   sgu_ln_b=v_sgu_ln_b, sgu_w=v_sgu_w, sgu_b=v_sgu_b, b_out=v_b_out, final_norm_g=v_final_norm_g)

    win_t, h = _all_gather_win(w_in[0].T, x[0], norm_g)
    grad_x, gwin_t, gwout, gsguw, vec = _local_grads(
        x[0], loss_target[0], h, win_t, w_out[0], norm_g, b_in, attn_sinks, sgu_ln_g, sgu_ln_b, sgu_w[0], sgu_b[0],
        b_out, final_norm_g.reshape(1, D_MODEL))

    t = lambda a: a[0].T
    d_win, nm_win, nv_win = _adamw_shard("adamw_w_in", gwin_t, t(w_in), t(m_w_in), t(v_w_in), WIN_ROWS // 2)
    d_wout, nm_wout, nv_wout = _adamw_shard("adamw_w_out", gwout, w_out[0], m_w_out[0], v_w_out[0], WOUT_ROWS)
    as_2d = lambda d: [d[name].reshape(shape) for name, _, shape in SMALL]
    loss = vec[VEC_LOSS, 0]
    small = _adamw_replicated(vec, gsguw, as_2d(given), as_2d(m_given), as_2d(v_given))

    def assemble(big_in, big_out, k):
        vals = {name: small[k][i].reshape(given[name].shape) for i, (name, _, _) in enumerate(SMALL)}
        vals["w_in"] = big_in.T[None]
        vals["w_out"] = big_out[None]
        order = ("norm_g", "w_in", "b_in", "attn_sinks", "sgu_ln_g", "sgu_ln_b", "sgu_w", "sgu_b", "w_out",
                 "b_out", "final_norm_g")
        return [vals[name] for name in order]

    return (loss, grad_x[None],
            *assemble(gwin_t, gwout, 0), *assemble(d_win, d_wout, 1),
            *assemble(nm_win, nm_wout, 2), *assemble(nv_win, nv_wout, 3))
```

```python
import functools
import math

import jax
import jax.numpy as jnp
from jax import lax
from jax.experimental import pallas as pl
from jax.experimental.pallas import tpu as pltpu

F32 = jnp.float32
BF16 = jnp.bfloat16
MXU_DTYPE = BF16
COMM_DTYPE = BF16

D_MODEL = 1024
SEQ = 4096
HEAD_DIM = 64
N_Q_HEADS = 8
Q_PER_KV = 4
BLOCK = 128
N_BLOCKS = SEQ // BLOCK
ATTN_W = 512
KV_W = 128
SGU_W = 512
N_SGU_HEADS = 8
IN_W = 2816
NORM_EPS = 1e-5
NEG_INF = -1e30
SCALE = HEAD_DIM ** -0.5
KV0 = ATTN_W
GATE0 = ATTN_W + 2 * KV_W
SGU0 = GATE0 + ATTN_W
ATTN_SECTION = SGU0
SGU_SECTION = IN_W - SGU0

ADAM_LR = 0.001
ADAM_B1 = 0.9
ADAM_B2 = 0.999
ADAM_EPS = 1e-08
ADAM_WD = 0.01
ADAM_STEP = 10

N_DEV = 8
WIN_ROWS = IN_W // N_DEV
WOUT_ROWS = D_MODEL // N_DEV
SGUW_ROWS = N_SGU_HEADS * BLOCK // N_DEV
VEC_ROWS = 16
MESH = pl.DeviceIdType.MESH

LANES = 128
HALF = LANES // 2
N_PAIRS = N_Q_HEADS * HEAD_DIM // LANES
KVX_W = 12 * LANES
TOKEN_TILE = 512
FWD_TOKEN_TILE = 512
ATTN_FWD_AHEAD = 4
FUSED_BLOCKS = 2
SGU_MIX_AFTER_CHAIN = 0
SGU_GATES_AFTER_CHAIN = 1
SGU_GRADS_AFTER_CHAIN = 5
ATTN_BWD_AHEAD = 3
VMEM_LIMIT = 56 * 1024 * 1024

NN = (((1,), (0,)), ((), ()))
NT = (((1,), (1,)), ((), ()))
TN = (((0,), (0,)), ((), ()))


def _dot(a, b, dims=NN):
    return lax.dot_general(a.astype(MXU_DTYPE), b.astype(MXU_DTYPE), dims, preferred_element_type=F32)


def _gelu(x):
    return x * (lax.erf(x * (1.0 / math.sqrt(2.0))) + 1.0) * 0.5


def _gelu_grad(x):
    cdf = (lax.erf(x * (1.0 / math.sqrt(2.0))) + 1.0) * 0.5
    return cdf + x * jnp.exp(-0.5 * x * x) * (1.0 / math.sqrt(2.0 * math.pi))


def _silu_and_grad(z):
    s = jax.nn.sigmoid(z)
    return z * s, s * (1.0 + z * (1.0 - s))


def _params(semantics=None, vmem=None):
    kw = {}
    if semantics is not None:
        kw["dimension_semantics"] = semantics
    if vmem is not None:
        kw["vmem_limit_bytes"] = vmem
    return pltpu.CompilerParams(**kw)


def _full(shape):
    return pl.BlockSpec(shape, lambda *_: (0,) * len(shape))


VMEM_SPEC = pl.BlockSpec(memory_space=pltpu.VMEM)


RELATIONS = ((0, 0), (1, 0), (0, 1), (1, 1))


def _place():
    return lax.axis_index("x"), lax.axis_index("y"), lax.axis_index("c")


def _chip(rel):
    x, y, _ = _place()
    return (1 - x if rel[0] else x, 1 - y if rel[1] else y)


def _block_rows(place, n_rows):
    px, py, pc = place
    return pl.ds(pl.multiple_of((4 * px + 2 * py + pc) * n_rows, 16), n_rows)


class _Copies:
    def __init__(self, send_sems, recv_sems):
        self.send_sems, self.recv_sems = send_sems, recv_sems

    def __call__(self, k, src, dst, to):
        return pltpu.make_async_remote_copy(src_ref=src, dst_ref=dst, send_sem=self.send_sems.at[k],
                                            recv_sem=self.recv_sems.at[k], device_id=to, device_id_type=MESH)


def _gather_plan(copies, sem0, full_ref, n_rows):
    x, y, c = _place()
    me, sibling = (x, y, c), (x, y, 1 - c)
    chips = [_chip(rel) for rel in RELATIONS[1:]]

    def cp(k, block, to):
        rows = full_ref.at[_block_rows(block, n_rows), :]
        return copies(sem0 + k, rows, rows, to)

    first = [cp(0, me, sibling)] + [cp(1 + j, me, (*chip, c)) for j, chip in enumerate(chips)]
    passed = [cp(4 + j, (*chip, c), sibling) for j, chip in enumerate(chips)]

    def start():
        for f in first:
            f.start()

    def forward():
        for j, chip in enumerate(chips):
            cp(1 + j, (*chip, c), me).wait_recv()
            passed[j].start()

    def finish():
        cp(0, sibling, me).wait_recv()
        for j, chip in enumerate(chips):
            cp(4 + j, (*chip, 1 - c), me).wait_recv()
        for f in first + passed:
            f.wait_send()

    return start, forward, finish


GATHER_SEMS = 7


def _reduce_scatter_plan(copies, sem0, part_ref, n_rows, sa, ra, sb, rc, res_ref):
    x, y, c = _place()
    sibling = (x, y, 1 - c)
    n = n_rows
    level1 = copies(sem0, sa, ra, sibling)

    def level2(i):
        slot = pl.ds((i - 1) * n, n)
        return copies(sem0 + i, sb.at[slot, :], rc.at[slot, :], (*_chip(RELATIONS[i]), c))

    def start():
        for i, rel in enumerate(RELATIONS):
            sa[i * n:(i + 1) * n, :] = part_ref[_block_rows((*_chip(rel), 1 - c), n), :].astype(sa.dtype)
        level1.start()

    def exchange():
        level1.wait_recv()
        for i, rel in enumerate(RELATIONS):
            total = part_ref[_block_rows((*_chip(rel), c), n), :] + ra[i * n:(i + 1) * n, :].astype(F32)
            if i == 0:
                res_ref[...] = total
            else:
                sb[(i - 1) * n:i * n, :] = total.astype(sb.dtype)
                level2(i).start()

    def finish():
        acc = res_ref[...]
        for i in range(1, len(RELATIONS)):
            level2(i).wait_recv()
            acc = acc + rc[(i - 1) * n:i * n, :].astype(F32)
        res_ref[...] = acc
        level1.wait_send()
        for i in range(1, len(RELATIONS)):
            level2(i).wait_send()

    return start, exchange, finish


REDUCE_SEMS = 4


def _owner_sums_plan(copies, sem0, own_ref, sb, rc, res_ref):
    _, _, c = _place()
    n = own_ref.shape[0]

    def level2(i):
        slot = pl.ds((i - 1) * n, n)
        return copies(sem0 + i - 1, sb.at[slot, :], rc.at[slot, :], (*_chip(RELATIONS[i]), c))

    def send():
        for i in range(1, len(RELATIONS)):
            level2(i).start()

    def finish():
        acc = own_ref[...]
        for i in range(1, len(RELATIONS)):
            level2(i).wait_recv()
            acc = acc + rc[(i - 1) * n:i * n, :].astype(F32)
        res_ref[...] = acc
        for i in range(1, len(RELATIONS)):
            level2(i).wait_send()

    return send, finish


OWNER_SEMS = 3


def _reduce_scatter_scratch(n_rows, width, dtype):
    return [pltpu.VMEM((4 * n_rows, width), dtype), pltpu.VMEM((4 * n_rows, width), dtype),
            pltpu.VMEM((3 * n_rows, width), dtype), pltpu.VMEM((3 * n_rows, width), dtype)]


def _dma_sems(n):
    return [pltpu.SemaphoreType.DMA((n,)), pltpu.SemaphoreType.DMA((n,))]


def _all_gather_win(win_t_shard, x, norm_g):
    tm = FWD_TOKEN_TILE
    steps = SEQ // tm

    def body(win_ref, x_ref, g_ref, full_ref, h_ref, landing, send_sems, recv_sems):
        step = pl.program_id(0)
        start, forward, finish = _gather_plan(_Copies(send_sems, recv_sems), 0, landing, WIN_ROWS)

        @pl.when(step == 0)
        def _():
            landing[_block_rows(_place(), WIN_ROWS), :] = win_ref[...].astype(COMM_DTYPE)
            start()

        xv = x_ref[...]
        r = lax.rsqrt(jnp.mean(xv * xv, axis=-1, keepdims=True) + NORM_EPS)
        h_ref[...] = ((xv * r) * g_ref[...]).astype(MXU_DTYPE)

        @pl.when(step == steps - 1)
        def _():
            forward()
            finish()
            full_ref[...] = landing[...]

    return pl.pallas_call(
        body,
        name="all_gather_win",
        grid=(steps,),
        in_specs=[VMEM_SPEC, pl.BlockSpec((tm, D_MODEL), lambda i: (i, 0)), _full((1, D_MODEL))],
        out_specs=(_full((IN_W, D_MODEL)), pl.BlockSpec((tm, D_MODEL), lambda i: (i, 0))),
        out_shape=(jax.ShapeDtypeStruct((IN_W, D_MODEL), COMM_DTYPE),
                   jax.ShapeDtypeStruct((SEQ, D_MODEL), MXU_DTYPE)),
        scratch_shapes=[pltpu.VMEM((IN_W, D_MODEL), COMM_DTYPE)] + _dma_sems(GATHER_SEMS),
        compiler_params=_params(("arbitrary",), VMEM_LIMIT),
    )(win_t_shard, x, norm_g)


def _in_proj(h, b_in, win_t, wout_shard):
    tm = FWD_TOKEN_TILE
    steps = SEQ // tm

    def body(h_ref, b_ref, w_ref, wout_ref, q_ref, kvx_ref, gate_ref, wfull_ref, landing, send_sems, recv_sems):
        step = pl.program_id(0)
        start, forward, finish = _gather_plan(_Copies(send_sems, recv_sems), 0, landing, WOUT_ROWS)

        @pl.when(step == 0)
        def _():
            landing[_block_rows(_place(), WOUT_ROWS), :] = wout_ref[...].astype(COMM_DTYPE)
            start()

        pl.when(step == steps // 2)(forward)

        h = h_ref[...]

        def proj(lo, hi):
            return _dot(h, w_ref[lo:hi, :], NT) + b_ref[:, lo:hi]

        qs = proj(0, ATTN_W) * SCALE
        for pair in range(N_PAIRS):
            q_ref[pair] = qs[:, pair * LANES:(pair + 1) * LANES].astype(MXU_DTYPE)
        kv = proj(KV0, GATE0)
        low = lax.broadcasted_iota(jnp.int32, (tm, LANES), 1) < HALF
        for i in range(2):
            t = kv[:, i * LANES:(i + 1) * LANES]
            rot = pltpu.roll(t, HALF, 1)
            variants = (jnp.where(low, t, 0.0), jnp.where(low, 0.0, rot),
                        jnp.where(low, rot, 0.0), jnp.where(low, 0.0, t))
            for j, val in enumerate(variants):
                col = (4 * i + j) * LANES
                kvx_ref[:, col:col + LANES] = val.astype(MXU_DTYPE)
                if i == 1:
                    ones_elsewhere = jnp.where(low == (j % 2 == 0), val, 1.0)
                    kvx_ref[:, col + 4 * LANES:col + 5 * LANES] = ones_elsewhere.astype(MXU_DTYPE)
        for k in range(4):
            gate_ref[k] = proj(GATE0 + k * SGU_W, GATE0 + (k + 1) * SGU_W)

        @pl.when(step == steps - 1)
        def _():
            finish()
            wfull_ref[...] = landing[...]

    return pl.pallas_call(
        body,
        name="in_proj",
        grid=(steps,),
        in_specs=[pl.BlockSpec((tm, D_MODEL), lambda i: (i, 0)),
                  _full((1, IN_W)), _full((IN_W, D_MODEL)), VMEM_SPEC],
        out_specs=(pl.BlockSpec((N_PAIRS, tm, LANES), lambda i: (0, i, 0)),
                   pl.BlockSpec((tm, KVX_W), lambda i: (i, 0)),
                   pl.BlockSpec((4, tm, SGU_W), lambda i: (0, i, 0)),
                   _full((D_MODEL, D_MODEL))),
        out_shape=(jax.ShapeDtypeStruct((N_PAIRS, SEQ, LANES), MXU_DTYPE),
                   jax.ShapeDtypeStruct((SEQ, KVX_W), MXU_DTYPE),
                   jax.ShapeDtypeStruct((4, SEQ, SGU_W), F32),
                   jax.ShapeDtypeStruct((D_MODEL, D_MODEL), COMM_DTYPE)),
        scratch_shapes=[pltpu.VMEM((D_MODEL, D_MODEL), COMM_DTYPE)] + _dma_sems(GATHER_SEMS),
        compiler_params=_params(("arbitrary",), VMEM_LIMIT),
    )(h, b_in, win_t, wout_shard)


def _window_mask(n):
    qi = lax.broadcasted_iota(jnp.int32, (2 * BLOCK, 2 * BLOCK), 0) & (BLOCK - 1)
    p = lax.broadcasted_iota(jnp.int32, (2 * BLOCK, 2 * BLOCK), 1) - BLOCK
    in_window = jnp.logical_and(p <= qi, p > qi - BLOCK)
    return jnp.logical_and(in_window, jnp.logical_or(p >= 0, n > 0))


def _sink_column(sink_ref, g, par):
    return jnp.concatenate([jnp.full((BLOCK, 1), sink_ref[4 * g + par], F32),
                            jnp.full((BLOCK, 1), sink_ref[4 * g + 2 + par], F32)], axis=0)


def _kv_cat(kp_ref, kc_ref, var, with_ones):
    kcol, vcol = var * LANES, (var + (8 if with_ones else 4)) * LANES
    return (jnp.concatenate([kp_ref[:, kcol:kcol + LANES], kc_ref[:, kcol:kcol + LANES]], axis=0),
            jnp.concatenate([kp_ref[:, vcol:vcol + LANES], kc_ref[:, vcol:vcol + LANES]], axis=0))


def _softmax_numerator(s, sink):
    m = jnp.maximum(jnp.max(s, axis=1, keepdims=True), sink)
    return jnp.exp(s - m), m


def _mixers_out_proj(sinks, q, kvx, gates, ln_g, ln_b, sgu_w, bias_full, x, target, wout, b_out, final_g):
    tm = FUSED_BLOCKS * BLOCK
    n_tiles = SEQ // tm

    def body(sink_ref, q_ref, kc_ref, za_ref, us_ref, vs_ref, zs_ref, lng_ref, lnb_ref, w_ref, bias_ref,
             x_ref, t_ref, wout_ref, b_ref, gf_ref,
             out_ref, gres_ref, dmix_ref, gw_ref, vec_ref,
             kp_ref, wm_ref, mixed_next, mixed_cur, out_stage, gb_ref):
        step = pl.program_id(0)

        @pl.when(step == 0)
        def _():
            kp_ref[...] = jnp.zeros_like(kp_ref)
            _mask_sgu_weights(w_ref, wm_ref)
            gw_ref[...] = jnp.zeros_like(gw_ref)
            vec_ref[...] = jnp.zeros_like(vec_ref)
            mixed_cur[...] = jnp.zeros_like(mixed_cur)

        def mixers_block(b, after_chain=()):
            rows = slice(b * BLOCK, (b + 1) * BLOCK)
            kc = kc_ref.at[rows, :]
            u, _, _, vln = _sgu_activations(us_ref[rows, :], vs_ref[rows, :], lng_ref[...], lnb_ref[...])

            valid = _window_mask(step * FUSED_BLOCKS + b)[0:BLOCK]
            chains = [(g, par, i) for g in range(2) for par in range(2) for i in range(2)]
            kv = {(g, par): _kv_cat(kp_ref, kc, 2 * g + par, True) for g in range(2) for par in range(2)}
            scores, outs = {}, {}

            def issue_scores(k):
                g, par, i = chains[k]
                scores[k] = _dot(q_ref[2 * g + i, rows, :], kv[g, par][0], NT)

            ahead = ATTN_FWD_AHEAD
            for k in range(ahead):
                issue_scores(k)
            low = lax.broadcasted_iota(jnp.int32, (BLOCK, LANES), 1) < HALF
            for k, (g, par, i) in enumerate(chains):
                sink = sink_ref[4 * g + 2 * i + par]
                e, m = _softmax_numerator(jnp.where(valid, scores[k], NEG_INF), sink)
                if k + ahead < len(chains):
                    issue_scores(k + ahead)
                o = _dot(e, kv[g, par][1])
                outs[g, par, i] = o / (pltpu.roll(o, HALF, 1) + jnp.exp(sink - m))
                if k == SGU_MIX_AFTER_CHAIN:
                    mixed = _sgu_mix(vln, wm_ref, bias_ref)
                if k % 2 == 0 and k // 2 < len(after_chain):
                    after_chain[k // 2]()
            for pair in range(N_PAIRS):
                g, i = divmod(pair, 2)
                lanes = slice(pair * LANES, (pair + 1) * LANES)
                o = jnp.where(low, outs[g, 0, i], outs[g, 1, i])
                out_stage[pair, rows, :] = o
                gate, _ = _silu_and_grad(za_ref[rows, lanes])
                mixed_next[rows, lanes] = (o * gate).astype(MXU_DTYPE)
            kp_ref[...] = kc[...]
            for pair in range(N_SGU_HEADS // 2):
                cols = slice(pair * LANES, (pair + 1) * LANES)
                gate, _ = _silu_and_grad(zs_ref[rows, cols])
                mixed_next[rows, ATTN_W + pair * LANES:ATTN_W + (pair + 1) * LANES] = (
                    u[:, cols] * mixed[pair] * gate).astype(MXU_DTYPE)

        live = (step > 0).astype(F32)
        quarter = D_MODEL // 4
        columns = [None] * 4

        def project(j):
            def piece():
                columns[j] = _dot(mixed_cur[...], wout_ref[:, j * quarter:(j + 1) * quarter])
            return piece

        half_blocks = FUSED_BLOCKS // 2
        per_block = 4 // half_blocks
        for b in range(half_blocks):
            mixers_block(b, [project(j) for j in range(b * per_block, (b + 1) * per_block)])
        xo = x_ref[...] + jnp.concatenate(columns, axis=1) + b_ref[...]
        r = lax.rsqrt(jnp.mean(xo * xo, axis=-1, keepdims=True) + NORM_EPS)
        xn = xo * r
        gf = gf_ref[...]
        err = xn * gf - t_ref[...]
        loss = 0.5 * jnp.sum(jnp.mean(err * err, axis=-1, keepdims=True), axis=0, keepdims=True)
        dy = err * (1.0 / D_MODEL)
        dxn = dy * gf
        gres = r * (dxn - xn * jnp.mean(dxn * xn, axis=-1, keepdims=True))
        vec_ref[0:1, :] += jnp.broadcast_to(loss * live, (1, D_MODEL))
        vec_ref[1:2, :] += jnp.sum(dy * xn, axis=0, keepdims=True) * live
        vec_ref[2:3, :] += jnp.sum(gres, axis=0, keepdims=True) * live
        gres_ref[...] = gres
        gb_ref[...] = gres.astype(MXU_DTYPE)

        def branch_grad(k):
            def piece():
                dmix_ref[k] = _dot(gb_ref[...], wout_ref[k * ATTN_W:(k + 1) * ATTN_W, :], NT)
            return piece

        def weight_grad(k):
            def piece():
                rows = slice(k * ATTN_W, (k + 1) * ATTN_W)
                gw_ref[rows, :] += _dot(mixed_cur[:, rows], gb_ref[...], TN)
            return piece

        backward = [branch_grad(0), branch_grad(1), weight_grad(0), weight_grad(1)]
        for b in range(half_blocks):
            mixers_block(half_blocks + b, backward[b * per_block:(b + 1) * per_block])

        @pl.when(step < n_tiles)
        def _():
            out_ref[...] = out_stage[...]

        mixed_cur[...] = mixed_next[...]

    ahead_tile = lambda i: jnp.minimum(i, n_tiles - 1)
    behind_tile = lambda i: jnp.maximum(i - 1, 0)
    blk = lambda w: pl.BlockSpec((tm, w), lambda i: (ahead_tile(i), 0))
    tiles = pl.BlockSpec((N_PAIRS, tm, LANES), lambda i: (0, ahead_tile(i), 0))
    gate = lambda k: pl.BlockSpec((None, tm, SGU_W), lambda i: (k, ahead_tile(i), 0))
    behind = lambda w: pl.BlockSpec((tm, w), lambda i: (behind_tile(i), 0))
    return pl.pallas_call(
        body,
        name="mixers_out_proj",
        grid=(n_tiles + 1,),
        in_specs=[pl.BlockSpec(memory_space=pltpu.SMEM), tiles, blk(KVX_W), gate(0), gate(1), gate(2), gate(3),
                  _full((1, SGU_W)), _full((1, SGU_W)), _full((N_SGU_HEADS, BLOCK, BLOCK)), _full((BLOCK, SGU_W)),
                  behind(D_MODEL), behind(D_MODEL), _full((D_MODEL, D_MODEL)), _full((1, D_MODEL)),
                  _full((1, D_MODEL))],
        out_specs=(tiles, behind(D_MODEL), pl.BlockSpec((2, tm, ATTN_W), lambda i: (0, behind_tile(i), 0)),
                   _full((D_MODEL, D_MODEL)), _full((8, D_MODEL))),
        out_shape=(jax.ShapeDtypeStruct((N_PAIRS, SEQ, LANES), F32),
                   jax.ShapeDtypeStruct((SEQ, D_MODEL), F32),
                   jax.ShapeDtypeStruct((2, SEQ, ATTN_W), F32),
                   jax.ShapeDtypeStruct((D_MODEL, D_MODEL), F32),
                   jax.ShapeDtypeStruct((8, D_MODEL), F32)),
        scratch_shapes=[pltpu.VMEM((BLOCK, KVX_W), MXU_DTYPE), pltpu.VMEM((N_SGU_HEADS, BLOCK, BLOCK), MXU_DTYPE),
                        pltpu.VMEM((tm, D_MODEL), MXU_DTYPE), pltpu.VMEM((tm, D_MODEL), MXU_DTYPE),
                        pltpu.VMEM((N_PAIRS, tm, LANES), F32), pltpu.VMEM((tm, D_MODEL), MXU_DTYPE)],
        compiler_params=_params(("arbitrary",), VMEM_LIMIT),
    )(sinks, q, kvx, gates, gates, gates, gates, ln_g, ln_b, sgu_w, bias_full, x, target, wout, b_out, final_g)


def _sgu_activations(us, vs, lng, lnb):
    u = _gelu(us)
    vg = _gelu(vs)
    mu = jnp.mean(vg, axis=-1, keepdims=True)
    xc = vg - mu
    rstd = lax.rsqrt(jnp.mean(xc * xc, axis=-1, keepdims=True) + NORM_EPS)
    vhat = xc * rstd
    return u, vhat, rstd, vhat * lng + lnb


def _mask_sgu_weights(w_ref, masked_ref, transposed_ref=None):
    tril = (lax.broadcasted_iota(jnp.int32, (BLOCK, BLOCK), 0)
            >= lax.broadcasted_iota(jnp.int32, (BLOCK, BLOCK), 1))
    for hh in range(N_SGU_HEADS):
        w = jnp.where(tril, w_ref[hh], 0.0)
        masked_ref[hh] = w.astype(MXU_DTYPE)
        if transposed_ref is not None:
            transposed_ref[hh] = w.T.astype(MXU_DTYPE)


def _sgu_mix(vln, masked_w_ref, bias_ref):
    low = lax.broadcasted_iota(jnp.int32, (BLOCK, LANES), 1) < HALF
    mixed = []
    for pair in range(N_SGU_HEADS // 2):
        vp = vln[:, pair * LANES:(pair + 1) * LANES]
        mixed.append(_dot(masked_w_ref[2 * pair], jnp.where(low, vp, 0.0))
                     + _dot(masked_w_ref[2 * pair + 1], jnp.where(low, 0.0, vp))
                     + bias_ref[:, pair * LANES:(pair + 1) * LANES])
    return mixed


def _mixers_bwd(sinks, dmix, q, kvx, out, gates, ln_g, ln_b, sgu_w, bias_full, gwout):
    last = N_BLOCKS - 1

    def body(sink_ref, d_ref, q_ref, kc_ref, o_ref, za_ref, dsg_ref, us_ref, vs_ref, zs_ref, lng_ref, lnb_ref, w_ref,
             bias_ref, gwout_ref,
             dp_ref, gsink_ref, gbin_ref, dps_ref, gw_ref, gb_ref, gln_ref, gbins_ref, wout_shard_ref,
             kp_ref, pend_ref, carry_ref, wm_ref, wt_ref, gbias_ref, sa_w, ra_w, sb_w, rc_w, send_sems, recv_sems):
        n = pl.program_id(0)
        start, exchange, finish = _reduce_scatter_plan(_Copies(send_sems, recv_sems), 0, gwout_ref, WOUT_ROWS,
                                                       sa_w, ra_w, sb_w, rc_w, wout_shard_ref)
        tril = (lax.broadcasted_iota(jnp.int32, (BLOCK, BLOCK), 0)
                >= lax.broadcasted_iota(jnp.int32, (BLOCK, BLOCK), 1))

        @pl.when(n == 0)
        def _():
            gsink_ref[...] = jnp.zeros_like(gsink_ref)
            gbin_ref[...] = jnp.zeros_like(gbin_ref)
            carry_ref[...] = jnp.zeros_like(carry_ref)
            kp_ref[...] = jnp.zeros_like(kp_ref)
            gw_ref[...] = jnp.zeros_like(gw_ref)
            gln_ref[...] = jnp.zeros_like(gln_ref)
            gbins_ref[...] = jnp.zeros_like(gbins_ref)
            gbias_ref[...] = jnp.zeros_like(gbias_ref)
            _mask_sgu_weights(w_ref, wm_ref, wt_ref)
            start()

        pl.when(n == 3)(exchange)
        pl.when(n == 12)(finish)

        @pl.when(n > 0)
        def _():
            dp_ref[:, 0:ATTN_W] = pend_ref[:, 0:ATTN_W]
            dp_ref[:, GATE0:ATTN_SECTION] = pend_ref[:, ATTN_W:]

        @pl.when(n > last)
        def _():
            dp_ref[:, KV0:GATE0] = carry_ref[...].astype(MXU_DTYPE)

        @pl.when(n <= last)
        def _():
            us = us_ref[...]
            vs = vs_ref[...]
            lng = lng_ref[...]
            u, vhat, rstd, vln = _sgu_activations(us, vs, lng, lnb_ref[...])
            low_sgu = lax.broadcasted_iota(jnp.int32, (BLOCK, LANES), 1) < HALF
            sgu = {}

            def sgu_gates():
                mixed = _sgu_mix(vln, wm_ref, bias_ref)
                sgu["du"], sgu["dzs"], sgu["dm"] = [], [], []
                for pair in range(N_SGU_HEADS // 2):
                    cols = slice(pair * LANES, (pair + 1) * LANES)
                    dsg = dsg_ref[:, cols]
                    gate, gate_grad = _silu_and_grad(zs_ref[:, cols])
                    up = u[:, cols]
                    sgu["du"].append(dsg * mixed[pair] * gate)
                    sgu["dzs"].append(dsg * up * mixed[pair] * gate_grad)
                    dmixed = dsg * up * gate
                    gbias_ref[:, cols] += dmixed
                    sgu["dm"].append((jnp.where(low_sgu, dmixed, 0.0).astype(MXU_DTYPE),
                                      jnp.where(low_sgu, 0.0, dmixed).astype(MXU_DTYPE)))

            def sgu_grads():
                dvln_parts = []
                for pair in range(N_SGU_HEADS // 2):
                    dm_lo, dm_hi = sgu["dm"][pair]
                    vp = vln[:, pair * LANES:(pair + 1) * LANES]
                    gw_ref[2 * pair] += _dot(dm_lo, vp, NT)
                    gw_ref[2 * pair + 1] += _dot(dm_hi, vp, NT)
                    dvln_parts.append(_dot(wt_ref[2 * pair], dm_lo) + _dot(wt_ref[2 * pair + 1], dm_hi))
                dvln = jnp.concatenate(dvln_parts, axis=1)
                gln_ref[0:1, :] += jnp.sum(dvln * vhat, axis=0, keepdims=True)
                gln_ref[1:2, :] += jnp.sum(dvln, axis=0, keepdims=True)
                dvhat = dvln * lng
                dvg = rstd * (dvhat - jnp.mean(dvhat, axis=-1, keepdims=True)
                              - vhat * jnp.mean(dvhat * vhat, axis=-1, keepdims=True))
                dus = jnp.concatenate(sgu["du"], axis=1) * _gelu_grad(us)
                dvs = dvg * _gelu_grad(vs)
                dzs = jnp.concatenate(sgu["dzs"], axis=1)
                for k, val in enumerate((dus, dvs, dzs)):
                    dps_ref[:, k * SGU_W:(k + 1) * SGU_W] = val.astype(MXU_DTYPE)
                    gbins_ref[:, k * SGU_W:(k + 1) * SGU_W] += jnp.sum(val, axis=0, keepdims=True)

            valid = _window_mask(n)[0:BLOCK]
            low = lax.broadcasted_iota(jnp.int32, (BLOCK, LANES), 1) < HALF
            low_keys = lax.broadcasted_iota(jnp.int32, (2 * BLOCK, LANES), 1) < HALF
            lane_row = lax.broadcasted_iota(jnp.int32, (1, LANES), 1)
            gsink = jnp.zeros((1, LANES), F32)
            chains = [(g, par, i) for g in range(2) for par in range(2) for i in range(2)]
            kv = {(g, par): _kv_cat(kp_ref, kc_ref, 2 * g + par, False) for g in range(2) for par in range(2)}
            ones_keys = jnp.ones((2 * BLOCK, LANES), MXU_DTYPE)
            half_of_lane = lax.broadcasted_iota(jnp.int32, (LANES, 2 * LANES), 0) // HALF
            half_of_col = lax.broadcasted_iota(jnp.int32, (LANES, 2 * LANES), 1) // LANES
            sum_halves = (half_of_lane == half_of_col).astype(MXU_DTYPE)
            douts, deltas = [], []
            for pair in range(N_PAIRS):
                lanes = slice(pair * LANES, (pair + 1) * LANES)
                dg = d_ref[:, lanes]
                gate, gate_grad = _silu_and_grad(za_ref[:, lanes])
                o = o_ref[pair]
                dout = dg * gate
                dza = dg * o * gate_grad
                douts.append(dout.astype(MXU_DTYPE))
                deltas.append(_dot(dout * o, sum_halves))
                zl = slice(ATTN_W + pair * LANES, ATTN_W + (pair + 1) * LANES)
                pend_ref[:, zl] = dza.astype(MXU_DTYPE)
                gl = slice(GATE0 + pair * LANES, GATE0 + (pair + 1) * LANES)
                gbin_ref[:, gl] += jnp.sum(dza, axis=0, keepdims=True)

            first = {}

            def issue_first(k):
                g, par, i = chains[k]
                first[k] = (_dot(q_ref[2 * g + i], kv[g, par][0], NT), _dot(douts[2 * g + i], kv[g, par][1], NT))

            numerators = {}

            def issue_row_sums(k):
                g, par, i = chains[k]
                sink = sink_ref[4 * g + 2 * i + par]
                e, m = _softmax_numerator(jnp.where(valid, first[k][0], NEG_INF), sink)
                numerators[k] = (e, jnp.exp(sink - m), _dot(e, ones_keys))

            ahead = ATTN_BWD_AHEAD
            for k in range(ahead):
                issue_first(k)
            issue_row_sums(0)
            issue_row_sums(1)
            dqs, dk_parts, dv_parts = {}, {}, {}
            operands = {}

            def issue_last(k):
                g, par, i = chains[k]
                ds, ds_t, p_t = operands.pop(k)
                dq = _dot(ds, kv[g, par][0])
                dqs[g, i] = dq if par == 0 else dqs[g, i] + dq
                dk = _dot(ds_t, q_ref[2 * g + i])
                dv = _dot(p_t, douts[2 * g + i])
                dk_parts[g, par] = dk if i == 0 else dk_parts[g, par] + dk
                dv_parts[g, par] = dv if i == 0 else dv_parts[g, par] + dv

            for k, (g, par, i) in enumerate(chains):
                h = 4 * g + 2 * i + par
                delta = deltas[2 * g + i][:, par * LANES:(par + 1) * LANES]
                e, at_sink, row_sum = numerators[k]
                inv = 1.0 / (row_sum + at_sink)
                p = e * jnp.tile(inv, (1, 2))
                ds = p * (first[k][1] - jnp.tile(delta, (1, 2)))
                ds = ds.astype(MXU_DTYPE)
                operands[k] = (ds, ds.T, p.astype(MXU_DTYPE).T)
                total = jnp.sum(at_sink * inv * delta, axis=0, keepdims=True)
                gsink = jnp.where(lane_row == h, -total, gsink)
                if k + ahead < len(chains):
                    issue_first(k + ahead)
                if k + 2 < len(chains):
                    issue_row_sums(k + 2)
                if k > 0:
                    issue_last(k - 1)
                if k == SGU_GATES_AFTER_CHAIN:
                    sgu_gates()
                if k == SGU_GRADS_AFTER_CHAIN:
                    sgu_grads()
            issue_last(len(chains) - 1)
            for pair in range(N_PAIRS):
                g, i = divmod(pair, 2)
                dq = dqs[g, i] * SCALE
                lanes = slice(pair * LANES, (pair + 1) * LANES)
                pend_ref[:, lanes] = dq.astype(MXU_DTYPE)
                gbin_ref[:, lanes] += jnp.sum(dq, axis=0, keepdims=True)
            gsink_ref[...] += gsink
            for k, parts in enumerate((dk_parts, dv_parts)):
                masked = {key: jnp.where(low_keys if key[1] == 0 else jnp.logical_not(low_keys), val, 0.0)
                          for key, val in parts.items()}
                both = (masked[0, 0] + masked[1, 1]
                        + pltpu.roll(masked[0, 1] + masked[1, 0], HALF, 1))
                lanes = slice(k * KV_W, (k + 1) * KV_W)
                done = carry_ref[:, lanes] + both[0:BLOCK]
                dp_ref[:, KV0 + k * KV_W:KV0 + (k + 1) * KV_W] = done.astype(MXU_DTYPE)
                carry_ref[:, lanes] = both[BLOCK:]
                gbin_ref[:, KV0 + k * KV_W:KV0 + (k + 1) * KV_W] += jnp.sum(both, axis=0, keepdims=True)
            kp_ref[...] = kc_ref[...]

        @pl.when(n == last)
        def _():
            for hh in range(N_SGU_HEADS):
                gw_ref[hh] = jnp.where(tril, gw_ref[hh], 0.0)
            head_of_lane = lax.broadcasted_iota(jnp.int32, (N_SGU_HEADS, SGU_W), 1) // HEAD_DIM
            select = (head_of_lane == lax.broadcasted_iota(jnp.int32, (N_SGU_HEADS, SGU_W), 0)).astype(F32)
            gb_ref[...] = lax.dot_general(select, gbias_ref[...], NT, precision=lax.Precision.HIGHEST,
                                          preferred_element_type=F32)

    at = lambda n: jnp.minimum(n, last)
    blk = lambda w: pl.BlockSpec((BLOCK, w), lambda n: (at(n), 0))
    tiles = pl.BlockSpec((N_PAIRS, BLOCK, LANES), lambda n: (0, at(n), 0))
    section = lambda k: pl.BlockSpec((None, BLOCK, SGU_W), lambda n: (k, at(n), 0))
    return pl.pallas_call(
        body,
        name="mixers_bwd",
        grid=(N_BLOCKS + 1,),
        in_specs=[pl.BlockSpec(memory_space=pltpu.SMEM),
                  section(0),
                  tiles,
                  blk(KVX_W),
                  tiles,
                  section(0),
                  section(1),
                  section(1), section(2), section(3),
                  _full((1, SGU_W)), _full((1, SGU_W)), _full((N_SGU_HEADS, BLOCK, BLOCK)), _full((BLOCK, SGU_W)),
                  VMEM_SPEC],
        out_specs=(pl.BlockSpec((BLOCK, ATTN_SECTION), lambda n: (jnp.maximum(n - 1, 0), 0)),
                   _full((1, LANES)), _full((1, ATTN_SECTION)),
                   pl.BlockSpec((BLOCK, SGU_SECTION), lambda n: (at(n), 0)),
                   _full((N_SGU_HEADS, BLOCK, BLOCK)), _full((N_SGU_HEADS, BLOCK)),
                   _full((8, SGU_W)), _full((1, SGU_SECTION)), VMEM_SPEC),
        out_shape=(jax.ShapeDtypeStruct((SEQ, ATTN_SECTION), MXU_DTYPE),
                   jax.ShapeDtypeStruct((1, LANES), F32),
                   jax.ShapeDtypeStruct((1, ATTN_SECTION), F32),
                   jax.ShapeDtypeStruct((SEQ, SGU_SECTION), MXU_DTYPE),
                   jax.ShapeDtypeStruct((N_SGU_HEADS, BLOCK, BLOCK), F32),
                   jax.ShapeDtypeStruct((N_SGU_HEADS, BLOCK), F32),
                   jax.ShapeDtypeStruct((8, SGU_W), F32),
                   jax.ShapeDtypeStruct((1, SGU_SECTION), F32),
                   jax.ShapeDtypeStruct((WOUT_ROWS, D_MODEL), F32)),
        scratch_shapes=([pltpu.VMEM((BLOCK, KVX_W), MXU_DTYPE),
                         pltpu.VMEM((BLOCK, 2 * ATTN_W), MXU_DTYPE), pltpu.VMEM((BLOCK, 2 * KV_W), F32),
                         pltpu.VMEM((N_SGU_HEADS, BLOCK, BLOCK), MXU_DTYPE),
                         pltpu.VMEM((N_SGU_HEADS, BLOCK, BLOCK), MXU_DTYPE), pltpu.VMEM((BLOCK, SGU_W), F32)]
                        + _reduce_scatter_scratch(WOUT_ROWS, D_MODEL, COMM_DTYPE) + _dma_sems(REDUCE_SEMS)),
        compiler_params=_params(("arbitrary",), VMEM_LIMIT),
    )(sinks, dmix, q, kvx, out, gates, dmix, gates, gates, gates, ln_g, ln_b, sgu_w, bias_full, gwout)


def _in_proj_bwd(dpa, dps, win_t, x, norm_g, gres, gwin_own, gwin_others, vec_parts):
    tm = TOKEN_TILE
    steps = SEQ // tm
    n_parts = len(vec_parts)

    def body(da_ref, ds_ref, w_ref, x_ref, g_ref, gres_ref, own_ref, others_ref, *rest):
        part_refs = rest[:n_parts]
        gx_ref, shard_ref, vec_out_ref, gng_ref, rc, vec_ref, ra_vec, slots, send_sems, recv_sems = rest[n_parts:]
        step = pl.program_id(0)
        copies = _Copies(send_sems, recv_sems)
        send, finish = _owner_sums_plan(copies, 0, own_ref, others_ref, rc, shard_ref)

        @pl.when(step == 0)
        def _():
            gng_ref[...] = jnp.zeros_like(gng_ref)
            send()

        dh = _dot(da_ref[...], w_ref[0:ATTN_SECTION, :]) + _dot(ds_ref[...], w_ref[ATTN_SECTION:, :])
        xv = x_ref[...]
        r = lax.rsqrt(jnp.mean(xv * xv, axis=-1, keepdims=True) + NORM_EPS)
        xn = xv * r
        gng_ref[...] += jnp.sum(dh * xn, axis=0, keepdims=True)
        dxn = dh * g_ref[...]
        gx_ref[...] = r * (dxn - xn * jnp.mean(dxn * xn, axis=-1, keepdims=True)) + gres_ref[...]

        @pl.when(step == steps - 1)
        def _():
            finish()
            _all_reduce_vectors(copies, OWNER_SEMS, gng_ref, *part_refs, vec_out_ref, vec_ref, ra_vec, slots)

    tile = lambda w: pl.BlockSpec((tm, w), lambda i: (i, 0))
    return pl.pallas_call(
        body,
        name="in_proj_bwd",
        grid=(steps,),
        in_specs=[tile(ATTN_SECTION), tile(SGU_SECTION), _full((IN_W, D_MODEL)), tile(D_MODEL),
                  _full((1, D_MODEL)), tile(D_MODEL), VMEM_SPEC, VMEM_SPEC] + [VMEM_SPEC] * n_parts,
        out_specs=(tile(D_MODEL), VMEM_SPEC, VMEM_SPEC),
        out_shape=(jax.ShapeDtypeStruct((SEQ, D_MODEL), F32),
                   jax.ShapeDtypeStruct((WIN_ROWS, D_MODEL), F32),
                   jax.ShapeDtypeStruct((VEC_ROWS, IN_W), F32)),
        scratch_shapes=([pltpu.VMEM((1, D_MODEL), F32), pltpu.VMEM((3 * WIN_ROWS, D_MODEL), COMM_DTYPE)]
                        + _vector_scratch() + _dma_sems(OWNER_SEMS + VECTOR_SEMS)),
        input_output_aliases={5: 0},
        compiler_params=_params(("arbitrary",), VMEM_LIMIT),
    )(dpa, dps, win_t, x, norm_g, gres, gwin_own, gwin_others, *vec_parts)


def _win_grad_pieces(rows):
    pieces = []
    for step in range(IN_W // rows):
        for owner in range(N_DEV):
            lo, hi = max(step * rows, owner * WIN_ROWS), min((step + 1) * rows, (owner + 1) * WIN_ROWS)
            if lo < hi:
                pieces.append((len(pieces), step, owner, lo, hi - lo))
    return pieces


def _win_grad(dpa, dps, h, gsguw):
    rows = 256
    n_attn = ATTN_SECTION // rows
    steps = IN_W // rows
    pieces = _win_grad_pieces(rows)
    class_rows = (N_DEV // 2) * WIN_ROWS

    def body(da_ref, ds_ref, h_ref, gsguw_ref, own_ref, others_ref, sguw_full_ref,
             chunks, sa, ra, sa_s, ra_s, sb_s, rc_s, landing, send_sems, recv_sems, give_sems, take_sems):
        step = pl.program_id(0)
        x, y, c = _place()
        copies = _Copies(send_sems, recv_sems)
        own_sguw = landing.at[_block_rows((x, y, c), SGUW_ROWS), :]
        start, exchange, finish = _reduce_scatter_plan(copies, 0, gsguw_ref, SGUW_ROWS, sa_s, ra_s, sb_s, rc_s,
                                                       own_sguw)
        gather = _gather_plan(copies, REDUCE_SEMS, landing, SGUW_ROWS)

        def class_rows_of(owner, first, n):
            return pl.ds((owner // 2) * WIN_ROWS + first - owner * WIN_ROWS, n)

        def to_sibling(piece):
            k, _, owner, first, n = piece
            at = class_rows_of(owner, first, n)
            return pltpu.make_async_remote_copy(src_ref=sa.at[at, :], dst_ref=ra.at[at, :], send_sem=give_sems.at[k],
                                                recv_sem=take_sems.at[k], device_id=(x, y, 1 - c), device_id_type=MESH)

        def give(piece):
            k, at_step, owner, first, n = piece

            @pl.when(c != owner % 2)
            def _():
                sa[class_rows_of(owner, first, n), :] = chunks[at_step % 2, pl.ds(first % rows, n), :].astype(sa.dtype)
                to_sibling(piece).start()

        def keep(piece):
            k, at_step, owner, first, n = piece
            px, py = owner // 4, (owner // 2) % 2

            @pl.when(c == owner % 2)
            def _():
                to_sibling(piece).wait_recv()
                total = (chunks[at_step % 2, pl.ds(first % rows, n), :]
                         + ra[class_rows_of(owner, first, n), :].astype(F32))
                relation = (x + px - 2 * x * px) + 2 * (y + py - 2 * y * py)

                @pl.when(relation == 0)
                def _():
                    own_ref[pl.ds(first - owner * WIN_ROWS, n), :] = total

                @pl.when(relation != 0)
                def _():
                    at = pl.multiple_of((relation - 1) * WIN_ROWS + first - owner * WIN_ROWS, 16)
                    others_ref[pl.ds(at, n), :] = total.astype(others_ref.dtype)

        pl.when(step == 0)(start)
        pl.when(step == 2)(exchange)

        @pl.when(step == 5)
        def _():
            finish()
            gather[0]()

        pl.when(step == 7)(gather[1])

        @pl.when(step < n_attn)
        def _():
            chunks[step % 2] = _dot(da_ref[...], h_ref[...], TN)

        @pl.when(step >= n_attn)
        def _():
            chunks[step % 2] = _dot(ds_ref[...], h_ref[...], TN)

        for at_step in range(steps):
            @pl.when(step == at_step)
            def _():
                for piece in pieces:
                    if piece[1] == at_step:
                        give(piece)
                    if piece[1] == at_step - 1:
                        keep(piece)

        @pl.when(step == steps - 1)
        def _():
            for piece in pieces:
                if piece[1] == steps - 1:
                    keep(piece)
            for piece in pieces:
                pl.when(c != piece[2] % 2)(to_sibling(piece).wait_send)
            gather[2]()
            sguw_full_ref[...] = landing[...]

    return pl.pallas_call(
        body,
        name="win_grad",
        grid=(steps,),
        in_specs=[pl.BlockSpec((SEQ, rows), lambda i: (0, jnp.minimum(i, n_attn - 1))),
                  pl.BlockSpec((SEQ, rows), lambda i: (0, jnp.maximum(i - n_attn, 0))),
                  _full((SEQ, D_MODEL)), VMEM_SPEC],
        out_specs=(VMEM_SPEC, VMEM_SPEC, _full((N_SGU_HEADS * BLOCK, BLOCK))),
        out_shape=(jax.ShapeDtypeStruct((WIN_ROWS, D_MODEL), F32),
                   jax.ShapeDtypeStruct((3 * WIN_ROWS, D_MODEL), COMM_DTYPE),
                   jax.ShapeDtypeStruct((N_SGU_HEADS * BLOCK, BLOCK), F32)),
        scratch_shapes=([pltpu.VMEM((2, rows, D_MODEL), F32),
                         pltpu.VMEM((class_rows, D_MODEL), COMM_DTYPE), pltpu.VMEM((class_rows, D_MODEL), COMM_DTYPE)]
                        + _reduce_scatter_scratch(SGUW_ROWS, BLOCK, F32)
                        + [pltpu.VMEM((N_SGU_HEADS * BLOCK, BLOCK), F32)]
                        + _dma_sems(REDUCE_SEMS + GATHER_SEMS) + _dma_sems(len(pieces))),
        compiler_params=_params(("arbitrary",), VMEM_LIMIT),
    )(dpa, dps, h, gsguw)


VEC_NORM_G, VEC_B_IN, VEC_SINKS, VEC_LN_G, VEC_LN_B, VEC_B_OUT, VEC_FINAL_G, VEC_LOSS, VEC_SGU_B = 0, 1, 2, 3, 4, 5, 6, 7, 8


def _adamw(w, g, m, v):
    m = ADAM_B1 * m + (1.0 - ADAM_B1) * g
    v = ADAM_B2 * v + (1.0 - ADAM_B2) * (g * g)
    m_hat = m / (1.0 - ADAM_B1 ** ADAM_STEP)
    v_hat = v / (1.0 - ADAM_B2 ** ADAM_STEP)
    delta = -ADAM_LR * (m_hat / (jnp.sqrt(v_hat) + ADAM_EPS) + ADAM_WD * w)
    return delta, m, v


def _adamw_shard(name, g, w, m, v, block_rows):
    def body(g_ref, w_ref, m_ref, v_ref, d_ref, nm_ref, nv_ref):
        d_ref[...], nm_ref[...], nv_ref[...] = _adamw(w_ref[...], g_ref[...], m_ref[...], v_ref[...])

    rows, cols = w.shape
    spec = pl.BlockSpec((block_rows, cols), lambda i: (i, 0))
    return pl.pallas_call(
        body,
        name=name,
        grid=(rows // block_rows,),
        in_specs=[spec] * 4,
        out_specs=(spec,) * 3,
        out_shape=(jax.ShapeDtypeStruct(w.shape, F32),) * 3,
        compiler_params=_params(("arbitrary",)),
    )(g, w, m, v)


VECTOR_SEMS = 4


def _vector_scratch():
    return [pltpu.VMEM((VEC_ROWS, IN_W), F32), pltpu.VMEM((VEC_ROWS, IN_W), F32),
            pltpu.VMEM((4 * VEC_ROWS, IN_W), F32)]


def _all_reduce_vectors(copies, sem0, gng_ref, gba_ref, gbs_ref, gsink_ref, gln_ref, gsgub_ref, vec4_ref, out_ref,
                        vec_ref, ra_vec, slots):
    x, y, c = _place()
    vec_ref[...] = jnp.zeros_like(vec_ref)
    vec_ref[VEC_NORM_G:VEC_NORM_G + 1, 0:D_MODEL] = gng_ref[...]
    vec_ref[VEC_B_IN:VEC_B_IN + 1, 0:ATTN_SECTION] = gba_ref[...]
    vec_ref[VEC_B_IN:VEC_B_IN + 1, ATTN_SECTION:IN_W] = gbs_ref[...]
    vec_ref[VEC_SINKS:VEC_SINKS + 1, 0:LANES] = gsink_ref[...]
    vec_ref[VEC_LN_G:VEC_LN_G + 1, 0:SGU_W] = gln_ref[0:1, :]
    vec_ref[VEC_LN_B:VEC_LN_B + 1, 0:SGU_W] = gln_ref[1:2, :]
    vec_ref[VEC_B_OUT:VEC_B_OUT + 1, 0:D_MODEL] = vec4_ref[2:3, :]
    vec_ref[VEC_FINAL_G:VEC_FINAL_G + 1, 0:D_MODEL] = vec4_ref[1:2, :]
    vec_ref[VEC_LOSS:VEC_LOSS + 1, 0:D_MODEL] = vec4_ref[0:1, :]
    vec_ref[VEC_SGU_B:VEC_SGU_B + N_SGU_HEADS, 0:BLOCK] = gsgub_ref[...]

    to_sibling = copies(sem0, vec_ref, ra_vec, (x, y, 1 - c))
    to_sibling.start()
    to_sibling.wait_recv()

    def chip_slot(place):
        return slots.at[pl.ds(pl.multiple_of((2 * place[0] + place[1]) * VEC_ROWS, 8), VEC_ROWS), :]

    mine = chip_slot((x, y))
    mine[...] = vec_ref[...] + ra_vec[...]
    to_chips = [copies(sem0 + i, mine, mine, (*_chip(rel), c)) for i, rel in enumerate(RELATIONS[1:], start=1)]
    for cp in to_chips:
        cp.start()
    for i, rel in enumerate(RELATIONS[1:], start=1):
        theirs = chip_slot(_chip(rel))
        copies(sem0 + i, theirs, theirs, (x, y, c)).wait_recv()
    out_ref[...] = ((slots[0:VEC_ROWS, :] + slots[VEC_ROWS:2 * VEC_ROWS, :])
                    + slots[2 * VEC_ROWS:3 * VEC_ROWS, :]) + slots[3 * VEC_ROWS:, :]
    to_sibling.wait_send()
    for cp in to_chips:
        cp.wait_send()


def _adamw_replicated(vec, gsguw, weights, m_state, v_state):
    n = len(SMALL)

    def body(*refs):
        vec_ref, gsguw_ref = refs[0], refs[1]
        w_refs, m_refs, v_refs = (refs[2 + k * n:2 + (k + 1) * n] for k in range(3))
        outs = refs[2 + 3 * n:]
        g_refs, d_refs, nm_refs, nv_refs = (outs[k * n:(k + 1) * n] for k in range(4))
        for i, (_, row, shape) in enumerate(SMALL):
            g = gsguw_ref[...] if row is None else vec_ref[row:row + shape[0], 0:shape[1]]
            g_refs[i][...] = g
            d_refs[i][...], nm_refs[i][...], nv_refs[i][...] = _adamw(
                w_refs[i][...], g, m_refs[i][...], v_refs[i][...])

    shapes = tuple(jax.ShapeDtypeStruct(shape, F32) for _, _, shape in SMALL)
    outs = pl.pallas_call(
        body,
        name="adamw_replicated",
        in_specs=[VMEM_SPEC] * (2 + 3 * n),
        out_specs=(VMEM_SPEC,) * (4 * n),
        out_shape=shapes * 4,
    )(vec, gsguw, *weights, *m_state, *v_state)
    return tuple(outs[k * n:(k + 1) * n] for k in range(4))


SMALL = (
    ("norm_g", VEC_NORM_G, (1, D_MODEL)),
    ("b_in", VEC_B_IN, (1, IN_W)),
    ("attn_sinks", VEC_SINKS, (1, N_Q_HEADS)),
    ("sgu_ln_g", VEC_LN_G, (1, SGU_W)),
    ("sgu_ln_b", VEC_LN_B, (1, SGU_W)),
    ("sgu_w", None, (N_SGU_HEADS * BLOCK, BLOCK)),
    ("sgu_b", VEC_SGU_B, (N_SGU_HEADS, BLOCK)),
    ("b_out", VEC_B_OUT, (1, D_MODEL)),
    ("final_norm_g", VEC_FINAL_G, (1, D_MODEL)),
)


def _local_grads(x, target, h, win_t, wout_shard, norm_g, b_in, attn_sinks, sgu_ln_g, sgu_ln_b, sgu_w, sgu_b, b_out,
                 final_g):
    sinks = attn_sinks.reshape(N_Q_HEADS)
    bias_full = jnp.repeat(sgu_b.T, HEAD_DIM, axis=1)
    q, kvx, gates, wout = _in_proj(h, b_in, win_t, wout_shard)
    out, gres, dmix, gwout, vec4 = _mixers_out_proj(sinks, q, kvx, gates, sgu_ln_g, sgu_ln_b, sgu_w, bias_full,
                                                    x, target, wout, b_out, final_g)
    dpa, gsink, gbin_a, dps, gsguw, gsgub, gln, gbin_s, gwout_shard = _mixers_bwd(
        sinks, dmix, q, kvx, out, gates, sgu_ln_g, sgu_ln_b, sgu_w, bias_full, gwout)
    gwin_own, gwin_others, gsguw_sum = _win_grad(dpa, dps, h, gsguw.reshape(N_SGU_HEADS * BLOCK, BLOCK))
    grad_x, gwin_shard, vec = _in_proj_bwd(dpa, dps, win_t, x, norm_g, gres, gwin_own, gwin_others,
                                           (gbin_a, gbin_s, gsink, gln, gsgub, vec4))
    return grad_x, gwin_shard, gwout_shard, gsguw_sum, vec


def kernel(x, norm_g, w_in, b_in, attn_sinks, sgu_ln_g, sgu_ln_b, sgu_w, sgu_b, w_out, b_out, final_norm_g, loss_target, m_norm_g, m_w_in, m_b_in, m_attn_sinks, m_sgu_ln_g, m_sgu_ln_b, m_sgu_w, m_sgu_b, m_w_out, m_b_out, m_final_norm_g, v_norm_g, v_w_in, v_b_in, v_attn_sinks, v_sgu_ln_g, v_sgu_ln_b, v_sgu_w, v_sgu_b, v_w_out, v_b_out, v_final_norm_g):
    given = dict(norm_g=norm_g, b_in=b_in, attn_sinks=attn_sinks, sgu_ln_g=sgu_ln_g, sgu_ln_b=sgu_ln_b,
                 sgu_w=sgu_w, sgu_b=sgu_b, b_out=b_out, final_norm_g=final_norm_g)
    m_given = dict(norm_g=m_norm_g, b_in=m_b_in, attn_sinks=m_attn_sinks, sgu_ln_g=m_sgu_ln_g,
                   sgu_ln_b=m_sgu_ln_b, sgu_w=m_sgu_w, sgu_b=m_sgu_b, b_out=m_b_out, final_norm_g=m_final_norm_g)
    v_given = dict(norm_g=v_norm_g, b_in=v_b_in, attn_sinks=v_attn_sinks, sgu_ln_g=v_sgu_ln_g,
                   sgu_ln_b=v_sgu_ln_b, sgu_w=v_sgu_w, sgu_b=v_sgu_b, b_out=v_b_out, final_norm_g=v_final_norm_g)

    win_t, h = _all_gather_win(w_in[0].T, x[0], norm_g)
    grad_x, gwin_t, gwout, gsguw, vec = _local_grads(
        x[0], loss_target[0], h, win_t, w_out[0], norm_g, b_in, attn_sinks, sgu_ln_g, sgu_ln_b, sgu_w[0], sgu_b[0],
        b_out, final_norm_g.reshape(1, D_MODEL))

    t = lambda a: a[0].T
    d_win, nm_win, nv_win = _adamw_shard("adamw_w_in", gwin_t, t(w_in), t(m_w_in), t(v_w_in), WIN_ROWS // 2)
    d_wout, nm_wout, nv_wout = _adamw_shard("adamw_w_out", gwout, w_out[0], m_w_out[0], v_w_out[0], WOUT_ROWS)
    as_2d = lambda d: [d[name].reshape(shape) for name, _, shape in SMALL]
    loss = vec[VEC_LOSS, 0]
    small = _adamw_replicated(vec, gsguw, as_2d(given), as_2d(m_given), as_2d(v_given))

    def assemble(big_in, big_out, k):
        vals = {name: small[k][i].reshape(given[name].shape) for i, (name, _, _) in enumerate(SMALL)}
        vals["w_in"] = big_in.T[None]
        vals["w_out"] = big_out[None]
        order = ("norm_g", "w_in", "b_in", "attn_sinks", "sgu_ln_g", "sgu_ln_b", "sgu_w", "sgu_b", "w_out",
                 "b_out", "final_norm_g")
        return [vals[name] for name in order]

    return (loss, grad_x[None],
            *assemble(gwin_t, gwout, 0), *assemble(d_win, d_wout, 1),
            *assemble(nm_win, nm_wout, 2), *assemble(nv_win, nv_wout, 3))
```

```python
import functools
import math

import jax
import jax.numpy as jnp
from jax import lax
from jax.experimental import pallas as pl
from jax.experimental.pallas import tpu as pltpu

F32 = jnp.float32
BF16 = jnp.bfloat16
MXU_DTYPE = BF16
COMM_DTYPE = BF16

D_MODEL = 1024
SEQ = 4096
HEAD_DIM = 64
N_Q_HEADS = 8
Q_PER_KV = 4
BLOCK = 128
N_BLOCKS = SEQ // BLOCK
ATTN_W = 512
KV_W = 128
SGU_W = 512
N_SGU_HEADS = 8
IN_W = 2816
NORM_EPS = 1e-5
NEG_INF = -1e30
SCALE = HEAD_DIM ** -0.5
KV0 = ATTN_W
GATE0 = ATTN_W + 2 * KV_W
SGU0 = GATE0 + ATTN_W
ATTN_SECTION = SGU0
SGU_SECTION = IN_W - SGU0

ADAM_LR = 0.001
ADAM_B1 = 0.9
ADAM_B2 = 0.999
ADAM_EPS = 1e-08
ADAM_WD = 0.01
ADAM_STEP = 10

N_DEV = 8
WIN_ROWS = IN_W // N_DEV
WOUT_ROWS = D_MODEL // N_DEV
SGUW_ROWS = N_SGU_HEADS * BLOCK // N_DEV
VEC_ROWS = 16
MESH = pl.DeviceIdType.MESH

LANES = 128
HALF = LANES // 2
N_PAIRS = N_Q_HEADS * HEAD_DIM // LANES
KVX_W = 12 * LANES
TOKEN_TILE = 512
IN_PROJ_BWD_RING = 3
FWD_TOKEN_TILE = 512
ATTN_FWD_AHEAD = 4
FUSED_BLOCKS = 2
SGU_MIX_AFTER_CHAIN = 0
SGU_GATES_AFTER_CHAIN = 1
SGU_GRADS_AFTER_CHAIN = 5
ATTN_BWD_AHEAD = 3
VMEM_LIMIT = 56 * 1024 * 1024

NN = (((1,), (0,)), ((), ()))
NT = (((1,), (1,)), ((), ()))
TN = (((0,), (0,)), ((), ()))


def _dot(a, b, dims=NN):
    return lax.dot_general(a.astype(MXU_DTYPE), b.astype(MXU_DTYPE), dims, preferred_element_type=F32)


def _gelu(x):
    return x * (lax.erf(x * (1.0 / math.sqrt(2.0))) + 1.0) * 0.5


def _gelu_grad(x):
    cdf = (lax.erf(x * (1.0 / math.sqrt(2.0))) + 1.0) * 0.5
    return cdf + x * jnp.exp(-0.5 * x * x) * (1.0 / math.sqrt(2.0 * math.pi))


def _silu_and_grad(z):
    s = jax.nn.sigmoid(z)
    return z * s, s * (1.0 + z * (1.0 - s))


def _params(semantics=None, vmem=None):
    kw = {}
    if semantics is not None:
        kw["dimension_semantics"] = semantics
    if vmem is not None:
        kw["vmem_limit_bytes"] = vmem
    return pltpu.CompilerParams(**kw)


def _full(shape):
    return pl.BlockSpec(shape, lambda *_: (0,) * len(shape))


VMEM_SPEC = pl.BlockSpec(memory_space=pltpu.VMEM)


RELATIONS = ((0, 0), (1, 0), (0, 1), (1, 1))


def _place():
    return lax.axis_index("x"), lax.axis_index("y"), lax.axis_index("c")


def _chip(rel):
    x, y, _ = _place()
    return (1 - x if rel[0] else x, 1 - y if rel[1] else y)


def _block_rows(place, n_rows):
    px, py, pc = place
    return pl.ds(pl.multiple_of((4 * px + 2 * py + pc) * n_rows, 16), n_rows)


class _Copies:
    def __init__(self, send_sems, recv_sems):
        self.send_sems, self.recv_sems = send_sems, recv_sems

    def __call__(self, k, src, dst, to):
        return pltpu.make_async_remote_copy(src_ref=src, dst_ref=dst, send_sem=self.send_sems.at[k],
                                            recv_sem=self.recv_sems.at[k], device_id=to, device_id_type=MESH)


def _gather_plan(copies, sem0, full_ref, n_rows):
    x, y, c = _place()
    me, sibling = (x, y, c), (x, y, 1 - c)
    chips = [_chip(rel) for rel in RELATIONS[1:]]

    def cp(k, block, to):
        rows = full_ref.at[_block_rows(block, n_rows), :]
        return copies(sem0 + k, rows, rows, to)

    first = [cp(0, me, sibling)] + [cp(1 + j, me, (*chip, c)) for j, chip in enumerate(chips)]
    passed = [cp(4 + j, (*chip, c), sibling) for j, chip in enumerate(chips)]

    def start():
        for f in first:
            f.start()

    def forward():
        for j, chip in enumerate(chips):
            cp(1 + j, (*chip, c), me).wait_recv()
            passed[j].start()

    def finish():
        cp(0, sibling, me).wait_recv()
        for j, chip in enumerate(chips):
            cp(4 + j, (*chip, 1 - c), me).wait_recv()
        for f in first + passed:
            f.wait_send()

    return start, forward, finish


GATHER_SEMS = 7


def _reduce_scatter_plan(copies, sem0, part_ref, n_rows, sa, ra, sb, rc, res_ref):
    x, y, c = _place()
    sibling = (x, y, 1 - c)
    n = n_rows
    level1 = copies(sem0, sa, ra, sibling)

    def level2(i):
        slot = pl.ds((i - 1) * n, n)
        return copies(sem0 + i, sb.at[slot, :], rc.at[slot, :], (*_chip(RELATIONS[i]), c))

    def start():
        for i, rel in enumerate(RELATIONS):
            sa[i * n:(i + 1) * n, :] = part_ref[_block_rows((*_chip(rel), 1 - c), n), :].astype(sa.dtype)
        level1.start()

    def exchange():
        level1.wait_recv()
        for i, rel in enumerate(RELATIONS):
            total = part_ref[_block_rows((*_chip(rel), c), n), :] + ra[i * n:(i + 1) * n, :].astype(F32)
            if i == 0:
                res_ref[...] = total
            else:
                sb[(i - 1) * n:i * n, :] = total.astype(sb.dtype)
                level2(i).start()

    def finish():
        acc = res_ref[...]
        for i in range(1, len(RELATIONS)):
            level2(i).wait_recv()
            acc = acc + rc[(i - 1) * n:i * n, :].astype(F32)
        res_ref[...] = acc
        level1.wait_send()
        for i in range(1, len(RELATIONS)):
            level2(i).wait_send()

    return start, exchange, finish


REDUCE_SEMS = 4


def _owner_sums_plan(copies, sem0, own_ref, sb, rc, res_ref):
    _, _, c = _place()
    n = own_ref.shape[0]

    def level2(i):
        slot = pl.ds((i - 1) * n, n)
        return copies(sem0 + i - 1, sb.at[slot, :], rc.at[slot, :], (*_chip(RELATIONS[i]), c))

    def send():
        for i in range(1, len(RELATIONS)):
            level2(i).start()

    def finish():
        acc = own_ref[...]
        for i in range(1, len(RELATIONS)):
            level2(i).wait_recv()
            acc = acc + rc[(i - 1) * n:i * n, :].astype(F32)
        res_ref[...] = acc
        for i in range(1, len(RELATIONS)):
            level2(i).wait_send()

    return send, finish


OWNER_SEMS = 3


def _reduce_scatter_scratch(n_rows, width, dtype):
    return [pltpu.VMEM((4 * n_rows, width), dtype), pltpu.VMEM((4 * n_rows, width), dtype),
            pltpu.VMEM((3 * n_rows, width), dtype), pltpu.VMEM((3 * n_rows, width), dtype)]


def _dma_sems(n):
    return [pltpu.SemaphoreType.DMA((n,)), pltpu.SemaphoreType.DMA((n,))]


def _all_gather_win(win_t_shard, x, norm_g):
    tm = FWD_TOKEN_TILE
    steps = SEQ // tm

    def body(win_ref, x_ref, g_ref, full_ref, h_ref, landing, send_sems, recv_sems):
        step = pl.program_id(0)
        start, forward, finish = _gather_plan(_Copies(send_sems, recv_sems), 0, landing, WIN_ROWS)

        @pl.when(step == 0)
        def _():
            landing[_block_rows(_place(), WIN_ROWS), :] = win_ref[...].astype(COMM_DTYPE)
            start()

        xv = x_ref[...]
        r = lax.rsqrt(jnp.mean(xv * xv, axis=-1, keepdims=True) + NORM_EPS)
        h_ref[...] = ((xv * r) * g_ref[...]).astype(MXU_DTYPE)

        @pl.when(step == steps - 1)
        def _():
            forward()
            finish()
            full_ref[...] = landing[...]

    return pl.pallas_call(
        body,
        name="all_gather_win",
        grid=(steps,),
        in_specs=[VMEM_SPEC, pl.BlockSpec((tm, D_MODEL), lambda i: (i, 0)), _full((1, D_MODEL))],
        out_specs=(_full((IN_W, D_MODEL)), pl.BlockSpec((tm, D_MODEL), lambda i: (i, 0))),
        out_shape=(jax.ShapeDtypeStruct((IN_W, D_MODEL), COMM_DTYPE),
                   jax.ShapeDtypeStruct((SEQ, D_MODEL), MXU_DTYPE)),
        scratch_shapes=[pltpu.VMEM((IN_W, D_MODEL), COMM_DTYPE)] + _dma_sems(GATHER_SEMS),
        compiler_params=_params(("arbitrary",), VMEM_LIMIT),
    )(win_t_shard, x, norm_g)


def _in_proj(h, b_in, win_t, wout_shard):
    tm = FWD_TOKEN_TILE
    steps = SEQ // tm

    def body(h_ref, b_ref, w_ref, wout_ref, q_ref, kvx_ref, gate_ref, wfull_ref, landing, send_sems, recv_sems):
        step = pl.program_id(0)
        start, forward, finish = _gather_plan(_Copies(send_sems, recv_sems), 0, landing, WOUT_ROWS)

        @pl.when(step == 0)
        def _():
            landing[_block_rows(_place(), WOUT_ROWS), :] = wout_ref[...].astype(COMM_DTYPE)
            start()

        pl.when(step == steps // 2)(forward)

        h = h_ref[...]

        def proj(lo, hi):
            return _dot(h, w_ref[lo:hi, :], NT) + b_ref[:, lo:hi]

        qs = proj(0, ATTN_W) * SCALE
        for pair in range(N_PAIRS):
            q_ref[pair] = qs[:, pair * LANES:(pair + 1) * LANES].astype(MXU_DTYPE)
        kv = proj(KV0, GATE0)
        low = lax.broadcasted_iota(jnp.int32, (tm, LANES), 1) < HALF
        for i in range(2):
            t = kv[:, i * LANES:(i + 1) * LANES]
            rot = pltpu.roll(t, HALF, 1)
            variants = (jnp.where(low, t, 0.0), jnp.where(low, 0.0, rot),
                        jnp.where(low, rot, 0.0), jnp.where(low, 0.0, t))
            for j, val in enumerate(variants):
                col = (4 * i + j) * LANES
                kvx_ref[:, col:col + LANES] = val.astype(MXU_DTYPE)
                if i == 1:
                    ones_elsewhere = jnp.where(low == (j % 2 == 0), val, 1.0)
                    kvx_ref[:, col + 4 * LANES:col + 5 * LANES] = ones_elsewhere.astype(MXU_DTYPE)
        for k in range(4):
            gate_ref[k] = proj(GATE0 + k * SGU_W, GATE0 + (k + 1) * SGU_W)

        @pl.when(step == steps - 1)
        def _():
            finish()
            wfull_ref[...] = landing[...]

    return pl.pallas_call(
        body,
        name="in_proj",
        grid=(steps,),
        in_specs=[pl.BlockSpec((tm, D_MODEL), lambda i: (i, 0)),
                  _full((1, IN_W)), _full((IN_W, D_MODEL)), VMEM_SPEC],
        out_specs=(pl.BlockSpec((N_PAIRS, tm, LANES), lambda i: (0, i, 0)),
                   pl.BlockSpec((tm, KVX_W), lambda i: (i, 0)),
                   pl.BlockSpec((4, tm, SGU_W), lambda i: (0, i, 0)),
                   _full((D_MODEL, D_MODEL))),
        out_shape=(jax.ShapeDtypeStruct((N_PAIRS, SEQ, LANES), MXU_DTYPE),
                   jax.ShapeDtypeStruct((SEQ, KVX_W), MXU_DTYPE),
                   jax.ShapeDtypeStruct((4, SEQ, SGU_W), F32),
                   jax.ShapeDtypeStruct((D_MODEL, D_MODEL), COMM_DTYPE)),
        scratch_shapes=[pltpu.VMEM((D_MODEL, D_MODEL), COMM_DTYPE)] + _dma_sems(GATHER_SEMS),
        compiler_params=_params(("arbitrary",), VMEM_LIMIT),
    )(h, b_in, win_t, wout_shard)


def _window_mask(n):
    qi = lax.broadcasted_iota(jnp.int32, (2 * BLOCK, 2 * BLOCK), 0) & (BLOCK - 1)
    p = lax.broadcasted_iota(jnp.int32, (2 * BLOCK, 2 * BLOCK), 1) - BLOCK
    in_window = jnp.logical_and(p <= qi, p > qi - BLOCK)
    return jnp.logical_and(in_window, jnp.logical_or(p >= 0, n > 0))


def _sink_column(sink_ref, g, par):
    return jnp.concatenate([jnp.full((BLOCK, 1), sink_ref[4 * g + par], F32),
                            jnp.full((BLOCK, 1), sink_ref[4 * g + 2 + par], F32)], axis=0)


def _kv_cat(kp_ref, kc_ref, var, with_ones):
    kcol, vcol = var * LANES, (var + (8 if with_ones else 4)) * LANES
    return (jnp.concatenate([kp_ref[:, kcol:kcol + LANES], kc_ref[:, kcol:kcol + LANES]], axis=0),
            jnp.concatenate([kp_ref[:, vcol:vcol + LANES], kc_ref[:, vcol:vcol + LANES]], axis=0))


def _softmax_numerator(s, sink):
    m = jnp.maximum(jnp.max(s, axis=1, keepdims=True), sink)
    return jnp.exp(s - m), m


def _mixers_out_proj(sinks, q, kvx, gates, ln_g, ln_b, sgu_w, bias_full, x, target, wout, b_out, final_g):
    tm = FUSED_BLOCKS * BLOCK
    n_tiles = SEQ // tm

    def body(sink_ref, q_ref, kc_ref, za_ref, us_ref, vs_ref, zs_ref, lng_ref, lnb_ref, w_ref, bias_ref,
             x_ref, t_ref, wout_ref, b_ref, gf_ref,
             out_ref, gres_ref, dmix_ref, gw_ref, vec_ref,
             kp_ref, wm_ref, mixed_next, mixed_cur, out_stage, gb_ref):
        step = pl.program_id(0)

        @pl.when(step == 0)
        def _():
            kp_ref[...] = jnp.zeros_like(kp_ref)
            _mask_sgu_weights(w_ref, wm_ref)
            gw_ref[...] = jnp.zeros_like(gw_ref)
            vec_ref[...] = jnp.zeros_like(vec_ref)
            mixed_cur[...] = jnp.zeros_like(mixed_cur)

        def mixers_block(b, after_chain=()):
            rows = slice(b * BLOCK, (b + 1) * BLOCK)
            kc = kc_ref.at[rows, :]
            u, _, _, vln = _sgu_activations(us_ref[rows, :], vs_ref[rows, :], lng_ref[...], lnb_ref[...])

            valid = _window_mask(step * FUSED_BLOCKS + b)[0:BLOCK]
            chains = [(g, par, i) for g in range(2) for par in range(2) for i in range(2)]
            kv = {(g, par): _kv_cat(kp_ref, kc, 2 * g + par, True) for g in range(2) for par in range(2)}
            scores, outs = {}, {}

            def issue_scores(k):
                g, par, i = chains[k]
                scores[k] = _dot(q_ref[2 * g + i, rows, :], kv[g, par][0], NT)

            ahead = ATTN_FWD_AHEAD
            for k in range(ahead):
                issue_scores(k)
            low = lax.broadcasted_iota(jnp.int32, (BLOCK, LANES), 1) < HALF
            for k, (g, par, i) in enumerate(chains):
                sink = sink_ref[4 * g + 2 * i + par]
                e, m = _softmax_numerator(jnp.where(valid, scores[k], NEG_INF), sink)
                if k + ahead < len(chains):
                    issue_scores(k + ahead)
                o = _dot(e, kv[g, par][1])
                outs[g, par, i] = o / (pltpu.roll(o, HALF, 1) + jnp.exp(sink - m))
                if k == SGU_MIX_AFTER_CHAIN:
                    mixed = _sgu_mix(vln, wm_ref, bias_ref)
                if k % 2 == 0 and k // 2 < len(after_chain):
                    after_chain[k // 2]()
            for pair in range(N_PAIRS):
                g, i = divmod(pair, 2)
                lanes = slice(pair * LANES, (pair + 1) * LANES)
                o = jnp.where(low, outs[g, 0, i], outs[g, 1, i])
                out_stage[pair, rows, :] = o
                gate, _ = _silu_and_grad(za_ref[rows, lanes])
                mixed_next[rows, lanes] = (o * gate).astype(MXU_DTYPE)
            kp_ref[...] = kc[...]
            for pair in range(N_SGU_HEADS // 2):
                cols = slice(pair * LANES, (pair + 1) * LANES)
                gate, _ = _silu_and_grad(zs_ref[rows, cols])
                mixed_next[rows, ATTN_W + pair * LANES:ATTN_W + (pair + 1) * LANES] = (
                    u[:, cols] * mixed[pair] * gate).astype(MXU_DTYPE)

        live = (step > 0).astype(F32)
        quarter = D_MODEL // 4
        columns = [None] * 4

        def project(j):
            def piece():
                columns[j] = _dot(mixed_cur[...], wout_ref[:, j * quarter:(j + 1) * quarter])
            return piece

        half_blocks = FUSED_BLOCKS // 2
        per_block = 4 // half_blocks
        for b in range(half_blocks):
            mixers_block(b, [project(j) for j in range(b * per_block, (b + 1) * per_block)])
        xo = x_ref[...] + jnp.concatenate(columns, axis=1) + b_ref[...]
        r = lax.rsqrt(jnp.mean(xo * xo, axis=-1, keepdims=True) + NORM_EPS)
        xn = xo * r
        gf = gf_ref[...]
        err = xn * gf - t_ref[...]
        loss = 0.5 * jnp.sum(jnp.mean(err * err, axis=-1, keepdims=True), axis=0, keepdims=True)
        dy = err * (1.0 / D_MODEL)
        dxn = dy * gf
        gres = r * (dxn - xn * jnp.mean(dxn * xn, axis=-1, keepdims=True))
        vec_ref[0:1, :] += jnp.broadcast_to(loss * live, (1, D_MODEL))
        vec_ref[1:2, :] += jnp.sum(dy * xn, axis=0, keepdims=True) * live
        vec_ref[2:3, :] += jnp.sum(gres, axis=0, keepdims=True) * live
        gres_ref[...] = gres
        gb_ref[...] = gres.astype(MXU_DTYPE)

        def branch_grad(k):
            def piece():
                dmix_ref[k] = _dot(gb_ref[...], wout_ref[k * ATTN_W:(k + 1) * ATTN_W, :], NT)
            return piece

        def weight_grad(k):
            def piece():
                rows = slice(k * ATTN_W, (k + 1) * ATTN_W)
                gw_ref[rows, :] += _dot(mixed_cur[:, rows], gb_ref[...], TN)
            return piece

        backward = [branch_grad(0), branch_grad(1), weight_grad(0), weight_grad(1)]
        for b in range(half_blocks):
            mixers_block(half_blocks + b, backward[b * per_block:(b + 1) * per_block])

        @pl.when(step < n_tiles)
        def _():
            out_ref[...] = out_stage[...]

        mixed_cur[...] = mixed_next[...]

    ahead_tile = lambda i: jnp.minimum(i, n_tiles - 1)
    behind_tile = lambda i: jnp.maximum(i - 1, 0)
    blk = lambda w: pl.BlockSpec((tm, w), lambda i: (ahead_tile(i), 0))
    tiles = pl.BlockSpec((N_PAIRS, tm, LANES), lambda i: (0, ahead_tile(i), 0))
    gate = lambda k: pl.BlockSpec((None, tm, SGU_W), lambda i: (k, ahead_tile(i), 0))
    behind = lambda w: pl.BlockSpec((tm, w), lambda i: (behind_tile(i), 0))
    return pl.pallas_call(
        body,
        name="mixers_out_proj",
        grid=(n_tiles + 1,),
        in_specs=[pl.BlockSpec(memory_space=pltpu.SMEM), tiles, blk(KVX_W), gate(0), gate(1), gate(2), gate(3),
                  _full((1, SGU_W)), _full((1, SGU_W)), _full((N_SGU_HEADS, BLOCK, BLOCK)), _full((BLOCK, SGU_W)),
                  behind(D_MODEL), behind(D_MODEL), _full((D_MODEL, D_MODEL)), _full((1, D_MODEL)),
                  _full((1, D_MODEL))],
        out_specs=(tiles, behind(D_MODEL), pl.BlockSpec((2, tm, ATTN_W), lambda i: (0, behind_tile(i), 0)),
                   _full((D_MODEL, D_MODEL)), _full((8, D_MODEL))),
        out_shape=(jax.ShapeDtypeStruct((N_PAIRS, SEQ, LANES), F32),
                   jax.ShapeDtypeStruct((SEQ, D_MODEL), F32),
                   jax.ShapeDtypeStruct((2, SEQ, ATTN_W), F32),
                   jax.ShapeDtypeStruct((D_MODEL, D_MODEL), F32),
                   jax.ShapeDtypeStruct((8, D_MODEL), F32)),
        scratch_shapes=[pltpu.VMEM((BLOCK, KVX_W), MXU_DTYPE), pltpu.VMEM((N_SGU_HEADS, BLOCK, BLOCK), MXU_DTYPE),
                        pltpu.VMEM((tm, D_MODEL), MXU_DTYPE), pltpu.VMEM((tm, D_MODEL), MXU_DTYPE),
                        pltpu.VMEM((N_PAIRS, tm, LANES), F32), pltpu.VMEM((tm, D_MODEL), MXU_DTYPE)],
        compiler_params=_params(("arbitrary",), VMEM_LIMIT),
    )(sinks, q, kvx, gates, gates, gates, gates, ln_g, ln_b, sgu_w, bias_full, x, target, wout, b_out, final_g)


def _sgu_activations(us, vs, lng, lnb):
    u = _gelu(us)
    vg = _gelu(vs)
    mu = jnp.mean(vg, axis=-1, keepdims=True)
    xc = vg - mu
    rstd = lax.rsqrt(jnp.mean(xc * xc, axis=-1, keepdims=True) + NORM_EPS)
    vhat = xc * rstd
    return u, vhat, rstd, vhat * lng + lnb


def _mask_sgu_weights(w_ref, masked_ref, transposed_ref=None):
    tril = (lax.broadcasted_iota(jnp.int32, (BLOCK, BLOCK), 0)
            >= lax.broadcasted_iota(jnp.int32, (BLOCK, BLOCK), 1))
    for hh in range(N_SGU_HEADS):
        w = jnp.where(tril, w_ref[hh], 0.0)
        masked_ref[hh] = w.astype(MXU_DTYPE)
        if transposed_ref is not None:
            transposed_ref[hh] = w.T.astype(MXU_DTYPE)


def _sgu_mix(vln, masked_w_ref, bias_ref):
    low = lax.broadcasted_iota(jnp.int32, (BLOCK, LANES), 1) < HALF
    mixed = []
    for pair in range(N_SGU_HEADS // 2):
        vp = vln[:, pair * LANES:(pair + 1) * LANES]
        mixed.append(_dot(masked_w_ref[2 * pair], jnp.where(low, vp, 0.0))
                     + _dot(masked_w_ref[2 * pair + 1], jnp.where(low, 0.0, vp))
                     + bias_ref[:, pair * LANES:(pair + 1) * LANES])
    return mixed


def _mixers_bwd(sinks, dmix, q, kvx, out, gates, ln_g, ln_b, sgu_w, bias_full, gwout):
    last = N_BLOCKS - 1

    def body(sink_ref, d_ref, q_ref, kc_ref, o_ref, za_ref, dsg_ref, us_ref, vs_ref, zs_ref, lng_ref, lnb_ref, w_ref,
             bias_ref, gwout_ref,
             dp_ref, gsink_ref, gbin_ref, dps_ref, gw_ref, gb_ref, gln_ref, gbins_ref, wout_shard_ref,
             kp_ref, pend_ref, carry_ref, wm_ref, wt_ref, gbias_ref, sa_w, ra_w, sb_w, rc_w, send_sems, recv_sems):
        n = pl.program_id(0)
        start, exchange, finish = _reduce_scatter_plan(_Copies(send_sems, recv_sems), 0, gwout_ref, WOUT_ROWS,
                                                       sa_w, ra_w, sb_w, rc_w, wout_shard_ref)
        tril = (lax.broadcasted_iota(jnp.int32, (BLOCK, BLOCK), 0)
                >= lax.broadcasted_iota(jnp.int32, (BLOCK, BLOCK), 1))

        @pl.when(n == 0)
        def _():
            gsink_ref[...] = jnp.zeros_like(gsink_ref)
            gbin_ref[...] = jnp.zeros_like(gbin_ref)
            carry_ref[...] = jnp.zeros_like(carry_ref)
            kp_ref[...] = jnp.zeros_like(kp_ref)
            gw_ref[...] = jnp.zeros_like(gw_ref)
            gln_ref[...] = jnp.zeros_like(gln_ref)
            gbins_ref[...] = jnp.zeros_like(gbins_ref)
            gbias_ref[...] = jnp.zeros_like(gbias_ref)
            _mask_sgu_weights(w_ref, wm_ref, wt_ref)
            start()

        pl.when(n == 3)(exchange)
        pl.when(n == 12)(finish)

        @pl.when(n > 0)
        def _():
            dp_ref[:, 0:ATTN_W] = pend_ref[:, 0:ATTN_W]
            dp_ref[:, GATE0:ATTN_SECTION] = pend_ref[:, ATTN_W:]

        @pl.when(n > last)
        def _():
            dp_ref[:, KV0:GATE0] = carry_ref[...].astype(MXU_DTYPE)

        @pl.when(n <= last)
        def _():
            us = us_ref[...]
            vs = vs_ref[...]
            lng = lng_ref[...]
            u, vhat, rstd, vln = _sgu_activations(us, vs, lng, lnb_ref[...])
            low_sgu = lax.broadcasted_iota(jnp.int32, (BLOCK, LANES), 1) < HALF
            sgu = {}

            def sgu_gates():
                mixed = _sgu_mix(vln, wm_ref, bias_ref)
                sgu["du"], sgu["dzs"], sgu["dm"] = [], [], []
                for pair in range(N_SGU_HEADS // 2):
                    cols = slice(pair * LANES, (pair + 1) * LANES)
                    dsg = dsg_ref[:, cols]
                    gate, gate_grad = _silu_and_grad(zs_ref[:, cols])
                    up = u[:, cols]
                    sgu["du"].append(dsg * mixed[pair] * gate)
                    sgu["dzs"].append(dsg * up * mixed[pair] * gate_grad)
                    dmixed = dsg * up * gate
                    gbias_ref[:, cols] += dmixed
                    sgu["dm"].append((jnp.where(low_sgu, dmixed, 0.0).astype(MXU_DTYPE),
                                      jnp.where(low_sgu, 0.0, dmixed).astype(MXU_DTYPE)))

            def sgu_grads():
                dvln_parts = []
                for pair in range(N_SGU_HEADS // 2):
                    dm_lo, dm_hi = sgu["dm"][pair]
                    vp = vln[:, pair * LANES:(pair + 1) * LANES]
                    gw_ref[2 * pair] += _dot(dm_lo, vp, NT)
                    gw_ref[2 * pair + 1] += _dot(dm_hi, vp, NT)
                    dvln_parts.append(_dot(wt_ref[2 * pair], dm_lo) + _dot(wt_ref[2 * pair + 1], dm_hi))
                dvln = jnp.concatenate(dvln_parts, axis=1)
                gln_ref[0:1, :] += jnp.sum(dvln * vhat, axis=0, keepdims=True)
                gln_ref[1:2, :] += jnp.sum(dvln, axis=0, keepdims=True)
                dvhat = dvln * lng
                dvg = rstd * (dvhat - jnp.mean(dvhat, axis=-1, keepdims=True)
                              - vhat * jnp.mean(dvhat * vhat, axis=-1, keepdims=True))
                dus = jnp.concatenate(sgu["du"], axis=1) * _gelu_grad(us)
                dvs = dvg * _gelu_grad(vs)
                dzs = jnp.concatenate(sgu["dzs"], axis=1)
                for k, val in enumerate((dus, dvs, dzs)):
                    dps_ref[:, k * SGU_W:(k + 1) * SGU_W] = val.astype(MXU_DTYPE)
                    gbins_ref[:, k * SGU_W:(k + 1) * SGU_W] += jnp.sum(val, axis=0, keepdims=True)

            valid = _window_mask(n)[0:BLOCK]
            low = lax.broadcasted_iota(jnp.int32, (BLOCK, LANES), 1) < HALF
            low_keys = lax.broadcasted_iota(jnp.int32, (2 * BLOCK, LANES), 1) < HALF
            lane_row = lax.broadcasted_iota(jnp.int32, (1, LANES), 1)
            gsink = jnp.zeros((1, LANES), F32)
            chains = [(g, par, i) for g in range(2) for par in range(2) for i in range(2)]
            kv = {(g, par): _kv_cat(kp_ref, kc_ref, 2 * g + par, False) for g in range(2) for par in range(2)}
            ones_keys = jnp.ones((2 * BLOCK, LANES), MXU_DTYPE)
            half_of_lane = lax.broadcasted_iota(jnp.int32, (LANES, 2 * LANES), 0) // HALF
            half_of_col = lax.broadcasted_iota(jnp.int32, (LANES, 2 * LANES), 1) // LANES
            sum_halves = (half_of_lane == half_of_col).astype(MXU_DTYPE)
            douts, deltas = [], []
            for pair in range(N_PAIRS):
                lanes = slice(pair * LANES, (pair + 1) * LANES)
                dg = d_ref[:, lanes]
                gate, gate_grad = _silu_and_grad(za_ref[:, lanes])
                o = o_ref[pair]
                dout = dg * gate
                dza = dg * o * gate_grad
                douts.append(dout.astype(MXU_DTYPE))
                deltas.append(_dot(dout * o, sum_halves))
                zl = slice(ATTN_W + pair * LANES, ATTN_W + (pair + 1) * LANES)
                pend_ref[:, zl] = dza.astype(MXU_DTYPE)
                gl = slice(GATE0 + pair * LANES, GATE0 + (pair + 1) * LANES)
                gbin_ref[:, gl] += jnp.sum(dza, axis=0, keepdims=True)

            first = {}

            def issue_first(k):
                g, par, i = chains[k]
                first[k] = (_dot(q_ref[2 * g + i], kv[g, par][0], NT), _dot(douts[2 * g + i], kv[g, par][1], NT))

            numerators = {}

            def issue_row_sums(k):
                g, par, i = chains[k]
                sink = sink_ref[4 * g + 2 * i + par]
                e, m = _softmax_numerator(jnp.where(valid, first[k][0], NEG_INF), sink)
                numerators[k] = (e, jnp.exp(sink - m), _dot(e, ones_keys))

            ahead = ATTN_BWD_AHEAD
            for k in range(ahead):
                issue_first(k)
            issue_row_sums(0)
            issue_row_sums(1)
            dqs, dk_parts, dv_parts = {}, {}, {}
            operands = {}

            def issue_last(k):
                g, par, i = chains[k]
                ds, ds_t, p_t = operands.pop(k)
                dq = _dot(ds, kv[g, par][0])
                dqs[g, i] = dq if par == 0 else dqs[g, i] + dq
                dk = _dot(ds_t, q_ref[2 * g + i])
                dv = _dot(p_t, douts[2 * g + i])
                dk_parts[g, par] = dk if i == 0 else dk_parts[g, par] + dk
                dv_parts[g, par] = dv if i == 0 else dv_parts[g, par] + dv

            for k, (g, par, i) in enumerate(chains):
                h = 4 * g + 2 * i + par
                delta = deltas[2 * g + i][:, par * LANES:(par + 1) * LANES]
                e, at_sink, row_sum = numerators[k]
                inv = 1.0 / (row_sum + at_sink)
                p = e * jnp.tile(inv, (1, 2))
                ds = p * (first[k][1] - jnp.tile(delta, (1, 2)))
                ds = ds.astype(MXU_DTYPE)
                operands[k] = (ds, ds.T, p.astype(MXU_DTYPE).T)
                total = jnp.sum(at_sink * inv * delta, axis=0, keepdims=True)
                gsink = jnp.where(lane_row == h, -total, gsink)
                if k + ahead < len(chains):
                    issue_first(k + ahead)
                if k + 2 < len(chains):
                    issue_row_sums(k + 2)
                if k > 0:
                    issue_last(k - 1)
                if k == SGU_GATES_AFTER_CHAIN:
                    sgu_gates()
                if k == SGU_GRADS_AFTER_CHAIN:
                    sgu_grads()
            issue_last(len(chains) - 1)
            for pair in range(N_PAIRS):
                g, i = divmod(pair, 2)
                dq = dqs[g, i] * SCALE
                lanes = slice(pair * LANES, (pair + 1) * LANES)
                pend_ref[:, lanes] = dq.astype(MXU_DTYPE)
                gbin_ref[:, lanes] += jnp.sum(dq, axis=0, keepdims=True)
            gsink_ref[...] += gsink
            for k, parts in enumerate((dk_parts, dv_parts)):
                masked = {key: jnp.where(low_keys if key[1] == 0 else jnp.logical_not(low_keys), val, 0.0)
                          for key, val in parts.items()}
                both = (masked[0, 0] + masked[1, 1]
                        + pltpu.roll(masked[0, 1] + masked[1, 0], HALF, 1))
                lanes = slice(k * KV_W, (k + 1) * KV_W)
                done = carry_ref[:, lanes] + both[0:BLOCK]
                dp_ref[:, KV0 + k * KV_W:KV0 + (k + 1) * KV_W] = done.astype(MXU_DTYPE)
                carry_ref[:, lanes] = both[BLOCK:]
                gbin_ref[:, KV0 + k * KV_W:KV0 + (k + 1) * KV_W] += jnp.sum(both, axis=0, keepdims=True)
            kp_ref[...] = kc_ref[...]

        @pl.when(n == last)
        def _():
            for hh in range(N_SGU_HEADS):
                gw_ref[hh] = jnp.where(tril, gw_ref[hh], 0.0)
            head_of_lane = lax.broadcasted_iota(jnp.int32, (N_SGU_HEADS, SGU_W), 1) // HEAD_DIM
            select = (head_of_lane == lax.broadcasted_iota(jnp.int32, (N_SGU_HEADS, SGU_W), 0)).astype(F32)
            gb_ref[...] = lax.dot_general(select, gbias_ref[...], NT, precision=lax.Precision.HIGHEST,
                                          preferred_element_type=F32)

    at = lambda n: jnp.minimum(n, last)
    blk = lambda w: pl.BlockSpec((BLOCK, w), lambda n: (at(n), 0))
    tiles = pl.BlockSpec((N_PAIRS, BLOCK, LANES), lambda n: (0, at(n), 0))
    section = lambda k: pl.BlockSpec((None, BLOCK, SGU_W), lambda n: (k, at(n), 0))
    return pl.pallas_call(
        body,
        name="mixers_bwd",
        grid=(N_BLOCKS + 1,),
        in_specs=[pl.BlockSpec(memory_space=pltpu.SMEM),
                  section(0),
                  tiles,
                  blk(KVX_W),
                  tiles,
                  section(0),
                  section(1),
                  section(1), section(2), section(3),
                  _full((1, SGU_W)), _full((1, SGU_W)), _full((N_SGU_HEADS, BLOCK, BLOCK)), _full((BLOCK, SGU_W)),
                  VMEM_SPEC],
        out_specs=(pl.BlockSpec((BLOCK, ATTN_SECTION), lambda n: (jnp.maximum(n - 1, 0), 0)),
                   _full((1, LANES)), _full((1, ATTN_SECTION)),
                   pl.BlockSpec((BLOCK, SGU_SECTION), lambda n: (at(n), 0)),
                   _full((N_SGU_HEADS, BLOCK, BLOCK)), _full((N_SGU_HEADS, BLOCK)),
                   _full((8, SGU_W)), _full((1, SGU_SECTION)), VMEM_SPEC),
        out_shape=(jax.ShapeDtypeStruct((SEQ, ATTN_SECTION), MXU_DTYPE),
                   jax.ShapeDtypeStruct((1, LANES), F32),
                   jax.ShapeDtypeStruct((1, ATTN_SECTION), F32),
                   jax.ShapeDtypeStruct((SEQ, SGU_SECTION), MXU_DTYPE),
                   jax.ShapeDtypeStruct((N_SGU_HEADS, BLOCK, BLOCK), F32),
                   jax.ShapeDtypeStruct((N_SGU_HEADS, BLOCK), F32),
                   jax.ShapeDtypeStruct((8, SGU_W), F32),
                   jax.ShapeDtypeStruct((1, SGU_SECTION), F32),
                   jax.ShapeDtypeStruct((WOUT_ROWS, D_MODEL), F32)),
        scratch_shapes=([pltpu.VMEM((BLOCK, KVX_W), MXU_DTYPE),
                         pltpu.VMEM((BLOCK, 2 * ATTN_W), MXU_DTYPE), pltpu.VMEM((BLOCK, 2 * KV_W), F32),
                         pltpu.VMEM((N_SGU_HEADS, BLOCK, BLOCK), MXU_DTYPE),
                         pltpu.VMEM((N_SGU_HEADS, BLOCK, BLOCK), MXU_DTYPE), pltpu.VMEM((BLOCK, SGU_W), F32)]
                        + _reduce_scatter_scratch(WOUT_ROWS, D_MODEL, COMM_DTYPE) + _dma_sems(REDUCE_SEMS)),
        compiler_params=_params(("arbitrary",), VMEM_LIMIT),
    )(sinks, dmix, q, kvx, out, gates, dmix, gates, gates, gates, ln_g, ln_b, sgu_w, bias_full, gwout)


def _in_proj_bwd(dpa, dps, win_t, x, norm_g, gres, gwin_own, gwin_others, vec_parts):
    tm = TOKEN_TILE
    steps = SEQ // tm
    n_parts = len(vec_parts)

    ring = IN_PROJ_BWD_RING

    def body(da_hbm, ds_hbm, w_ref, x_hbm, g_ref, gres_hbm, own_ref, others_ref, *rest):
        part_refs = rest[:n_parts]
        (gx_ref, shard_ref, vec_out_ref, gng_ref, da_buf, ds_buf, x_buf, gres_buf, load_sems,
         rc, vec_ref, ra_vec, slots, send_sems, recv_sems) = rest[n_parts:]
        step = pl.program_id(0)
        copies = _Copies(send_sems, recv_sems)
        send, finish = _owner_sums_plan(copies, 0, own_ref, others_ref, rc, shard_ref)
        streams = ((da_hbm, da_buf), (ds_hbm, ds_buf), (x_hbm, x_buf), (gres_hbm, gres_buf))

        def loads(tile):
            first = tile * tm if isinstance(tile, int) else pl.multiple_of(tile * tm, tm)
            return [pltpu.make_async_copy(hbm.at[pl.ds(first, tm), :], buf.at[tile % ring], load_sems.at[k, tile % ring])
                    for k, (hbm, buf) in enumerate(streams)]

        @pl.when(step == 0)
        def _():
            gng_ref[...] = jnp.zeros_like(gng_ref)
            for tile in range(ring - 1):
                for load in loads(tile):
                    load.start()
            send()

        @pl.when(step + ring - 1 < steps)
        def _():
            for load in loads(step + ring - 1):
                load.start()

        for load in loads(step):
            load.wait()
        slot = step % ring
        dh = _dot(da_buf[slot], w_ref[0:ATTN_SECTION, :]) + _dot(ds_buf[slot], w_ref[ATTN_SECTION:, :])
        xv = x_buf[slot]
        r = lax.rsqrt(jnp.mean(xv * xv, axis=-1, keepdims=True) + NORM_EPS)
        xn = xv * r
        gng_ref[...] += jnp.sum(dh * xn, axis=0, keepdims=True)
        dxn = dh * g_ref[...]
        gx_ref[...] = r * (dxn - xn * jnp.mean(dxn * xn, axis=-1, keepdims=True)) + gres_buf[slot]

        @pl.when(step == steps - 1)
        def _():
            finish()
            _all_reduce_vectors(copies, OWNER_SEMS, gng_ref, *part_refs, vec_out_ref, vec_ref, ra_vec, slots)

    anywhere = pl.BlockSpec(memory_space=pl.ANY)
    return pl.pallas_call(
        body,
        name="in_proj_bwd",
        grid=(steps,),
        in_specs=[anywhere, anywhere, _full((IN_W, D_MODEL)), anywhere,
                  _full((1, D_MODEL)), anywhere, VMEM_SPEC, VMEM_SPEC] + [VMEM_SPEC] * n_parts,
        out_specs=(pl.BlockSpec((tm, D_MODEL), lambda i: (i, 0)), VMEM_SPEC, VMEM_SPEC),
        out_shape=(jax.ShapeDtypeStruct((SEQ, D_MODEL), F32),
                   jax.ShapeDtypeStruct((WIN_ROWS, D_MODEL), F32),
                   jax.ShapeDtypeStruct((VEC_ROWS, IN_W), F32)),
        scratch_shapes=([pltpu.VMEM((1, D_MODEL), F32),
                         pltpu.VMEM((ring, tm, ATTN_SECTION), dpa.dtype), pltpu.VMEM((ring, tm, SGU_SECTION), dps.dtype),
                         pltpu.VMEM((ring, tm, D_MODEL), F32), pltpu.VMEM((ring, tm, D_MODEL), F32),
                         pltpu.SemaphoreType.DMA((4, ring)), pltpu.VMEM((3 * WIN_ROWS, D_MODEL), COMM_DTYPE)]
                        + _vector_scratch() + _dma_sems(OWNER_SEMS + VECTOR_SEMS)),
        compiler_params=_params(("arbitrary",), VMEM_LIMIT),
    )(dpa, dps, win_t, x, norm_g, gres, gwin_own, gwin_others, *vec_parts)


def _win_grad_pieces(rows):
    pieces = []
    for step in range(IN_W // rows):
        for owner in range(N_DEV):
            lo, hi = max(step * rows, owner * WIN_ROWS), min((step + 1) * rows, (owner + 1) * WIN_ROWS)
            if lo < hi:
                pieces.append((len(pieces), step, owner, lo, hi - lo))
    return pieces


def _win_grad(dpa, dps, h, gsguw):
    rows = 256
    n_attn = ATTN_SECTION // rows
    steps = IN_W // rows
    pieces = _win_grad_pieces(rows)
    class_rows = (N_DEV // 2) * WIN_ROWS

    def body(da_ref, ds_ref, h_ref, gsguw_ref, own_ref, others_ref, sguw_full_ref,
             chunks, sa, ra, sa_s, ra_s, sb_s, rc_s, landing, send_sems, recv_sems, give_sems, take_sems):
        step = pl.program_id(0)
        x, y, c = _place()
        copies = _Copies(send_sems, recv_sems)
        own_sguw = landing.at[_block_rows((x, y, c), SGUW_ROWS), :]
        start, exchange, finish = _reduce_scatter_plan(copies, 0, gsguw_ref, SGUW_ROWS, sa_s, ra_s, sb_s, rc_s,
                                                       own_sguw)
        gather = _gather_plan(copies, REDUCE_SEMS, landing, SGUW_ROWS)

        def class_rows_of(owner, first, n):
            return pl.ds((owner // 2) * WIN_ROWS + first - owner * WIN_ROWS, n)

        def to_sibling(piece):
            k, _, owner, first, n = piece
            at = class_rows_of(owner, first, n)
            return pltpu.make_async_remote_copy(src_ref=sa.at[at, :], dst_ref=ra.at[at, :], send_sem=give_sems.at[k],
                                                recv_sem=take_sems.at[k], device_id=(x, y, 1 - c), device_id_type=MESH)

        def give(piece):
            k, at_step, owner, first, n = piece

            @pl.when(c != owner % 2)
            def _():
                sa[class_rows_of(owner, first, n), :] = chunks[at_step % 2, pl.ds(first % rows, n), :].astype(sa.dtype)
                to_sibling(piece).start()

        def keep(piece):
            k, at_step, owner, first, n = piece
            px, py = owner // 4, (owner // 2) % 2

            @pl.when(c == owner % 2)
            def _():
                to_sibling(piece).wait_recv()
                total = (chunks[at_step % 2, pl.ds(first % rows, n), :]
                         + ra[class_rows_of(owner, first, n), :].astype(F32))
                relation = (x + px - 2 * x * px) + 2 * (y + py - 2 * y * py)

                @pl.when(relation == 0)
                def _():
                    own_ref[pl.ds(first - owner * WIN_ROWS, n), :] = total

                @pl.when(relation != 0)
                def _():
                    at = pl.multiple_of((relation - 1) * WIN_ROWS + first - owner * WIN_ROWS, 16)
                    others_ref[pl.ds(at, n), :] = total.astype(others_ref.dtype)

        pl.when(step == 0)(start)
        pl.when(step == 2)(exchange)

        @pl.when(step == 5)
        def _():
            finish()
            gather[0]()

        pl.when(step == 7)(gather[1])

        @pl.when(step < n_attn)
        def _():
            chunks[step % 2] = _dot(da_ref[...], h_ref[...], TN)

        @pl.when(step >= n_attn)
        def _():
            chunks[step % 2] = _dot(ds_ref[...], h_ref[...], TN)

        for at_step in range(steps):
            @pl.when(step == at_step)
            def _():
                for piece in pieces:
                    if piece[1] == at_step:
                        give(piece)
                    if piece[1] == at_step - 1:
                        keep(piece)

        @pl.when(step == steps - 1)
        def _():
            for piece in pieces:
                if piece[1] == steps - 1:
                    keep(piece)
            for piece in pieces:
                pl.when(c != piece[2] % 2)(to_sibling(piece).wait_send)
            gather[2]()
            sguw_full_ref[...] = landing[...]

    return pl.pallas_call(
        body,
        name="win_grad",
        grid=(steps,),
        in_specs=[pl.BlockSpec((SEQ, rows), lambda i: (0, jnp.minimum(i, n_attn - 1))),
                  pl.BlockSpec((SEQ, rows), lambda i: (0, jnp.maximum(i - n_attn, 0))),
                  _full((SEQ, D_MODEL)), VMEM_SPEC],
        out_specs=(VMEM_SPEC, VMEM_SPEC, _full((N_SGU_HEADS * BLOCK, BLOCK))),
        out_shape=(jax.ShapeDtypeStruct((WIN_ROWS, D_MODEL), F32),
                   jax.ShapeDtypeStruct((3 * WIN_ROWS, D_MODEL), COMM_DTYPE),
                   jax.ShapeDtypeStruct((N_SGU_HEADS * BLOCK, BLOCK), F32)),
        scratch_shapes=([pltpu.VMEM((2, rows, D_MODEL), F32),
                         pltpu.VMEM((class_rows, D_MODEL), COMM_DTYPE), pltpu.VMEM((class_rows, D_MODEL), COMM_DTYPE)]
                        + _reduce_scatter_scratch(SGUW_ROWS, BLOCK, F32)
                        + [pltpu.VMEM((N_SGU_HEADS * BLOCK, BLOCK), F32)]
                        + _dma_sems(REDUCE_SEMS + GATHER_SEMS) + _dma_sems(len(pieces))),
        compiler_params=_params(("arbitrary",), VMEM_LIMIT),
    )(dpa, dps, h, gsguw)


VEC_NORM_G, VEC_B_IN, VEC_SINKS, VEC_LN_G, VEC_LN_B, VEC_B_OUT, VEC_FINAL_G, VEC_LOSS, VEC_SGU_B = 0, 1, 2, 3, 4, 5, 6, 7, 8


def _adamw(w, g, m, v):
    m = ADAM_B1 * m + (1.0 - ADAM_B1) * g
    v = ADAM_B2 * v + (1.0 - ADAM_B2) * (g * g)
    m_hat = m / (1.0 - ADAM_B1 ** ADAM_STEP)
    v_hat = v / (1.0 - ADAM_B2 ** ADAM_STEP)
    delta = -ADAM_LR * (m_hat / (jnp.sqrt(v_hat) + ADAM_EPS) + ADAM_WD * w)
    return delta, m, v


def _adamw_shard(name, g, w, m, v, block_rows):
    def body(g_ref, w_ref, m_ref, v_ref, d_ref, nm_ref, nv_ref):
        d_ref[...], nm_ref[...], nv_ref[...] = _adamw(w_ref[...], g_ref[...], m_ref[...], v_ref[...])

    rows, cols = w.shape
    spec = pl.BlockSpec((block_rows, cols), lambda i: (i, 0))
    return pl.pallas_call(
        body,
        name=name,
        grid=(rows // block_rows,),
        in_specs=[spec] * 4,
        out_specs=(spec,) * 3,
        out_shape=(jax.ShapeDtypeStruct(w.shape, F32),) * 3,
        compiler_params=_params(("arbitrary",)),
    )(g, w, m, v)


VECTOR_SEMS = 4


def _vector_scratch():
    return [pltpu.VMEM((VEC_ROWS, IN_W), F32), pltpu.VMEM((VEC_ROWS, IN_W), F32),
            pltpu.VMEM((4 * VEC_ROWS, IN_W), F32)]


def _all_reduce_vectors(copies, sem0, gng_ref, gba_ref, gbs_ref, gsink_ref, gln_ref, gsgub_ref, vec4_ref, out_ref,
                        vec_ref, ra_vec, slots):
    x, y, c = _place()
    vec_ref[...] = jnp.zeros_like(vec_ref)
    vec_ref[VEC_NORM_G:VEC_NORM_G + 1, 0:D_MODEL] = gng_ref[...]
    vec_ref[VEC_B_IN:VEC_B_IN + 1, 0:ATTN_SECTION] = gba_ref[...]
    vec_ref[VEC_B_IN:VEC_B_IN + 1, ATTN_SECTION:IN_W] = gbs_ref[...]
    vec_ref[VEC_SINKS:VEC_SINKS + 1, 0:LANES] = gsink_ref[...]
    vec_ref[VEC_LN_G:VEC_LN_G + 1, 0:SGU_W] = gln_ref[0:1, :]
    vec_ref[VEC_LN_B:VEC_LN_B + 1, 0:SGU_W] = gln_ref[1:2, :]
    vec_ref[VEC_B_OUT:VEC_B_OUT + 1, 0:D_MODEL] = vec4_ref[2:3, :]
    vec_ref[VEC_FINAL_G:VEC_FINAL_G + 1, 0:D_MODEL] = vec4_ref[1:2, :]
    vec_ref[VEC_LOSS:VEC_LOSS + 1, 0:D_MODEL] = vec4_ref[0:1, :]
    vec_ref[VEC_SGU_B:VEC_SGU_B + N_SGU_HEADS, 0:BLOCK] = gsgub_ref[...]

    to_sibling = copies(sem0, vec_ref, ra_vec, (x, y, 1 - c))
    to_sibling.start()
    to_sibling.wait_recv()

    def chip_slot(place):
        return slots.at[pl.ds(pl.multiple_of((2 * place[0] + place[1]) * VEC_ROWS, 8), VEC_ROWS), :]

    mine = chip_slot((x, y))
    mine[...] = vec_ref[...] + ra_vec[...]
    to_chips = [copies(sem0 + i, mine, mine, (*_chip(rel), c)) for i, rel in enumerate(RELATIONS[1:], start=1)]
    for cp in to_chips:
        cp.start()
    for i, rel in enumerate(RELATIONS[1:], start=1):
        theirs = chip_slot(_chip(rel))
        copies(sem0 + i, theirs, theirs, (x, y, c)).wait_recv()
    out_ref[...] = ((slots[0:VEC_ROWS, :] + slots[VEC_ROWS:2 * VEC_ROWS, :])
                    + slots[2 * VEC_ROWS:3 * VEC_ROWS, :]) + slots[3 * VEC_ROWS:, :]
    to_sibling.wait_send()
    for cp in to_chips:
        cp.wait_send()


def _adamw_replicated(vec, gsguw, weights, m_state, v_state):
    n = len(SMALL)

    def body(*refs):
        vec_ref, gsguw_ref = refs[0], refs[1]
        w_refs, m_refs, v_refs = (refs[2 + k * n:2 + (k + 1) * n] for k in range(3))
        outs = refs[2 + 3 * n:]
        g_refs, d_refs, nm_refs, nv_refs = (outs[k * n:(k + 1) * n] for k in range(4))
        for i, (_, row, shape) in enumerate(SMALL):
            g = gsguw_ref[...] if row is None else vec_ref[row:row + shape[0], 0:shape[1]]
            g_refs[i][...] = g
            d_refs[i][...], nm_refs[i][...], nv_refs[i][...] = _adamw(
                w_refs[i][...], g, m_refs[i][...], v_refs[i][...])

    shapes = tuple(jax.ShapeDtypeStruct(shape, F32) for _, _, shape in SMALL)
    outs = pl.pallas_call(
        body,
        name="adamw_replicated",
        in_specs=[VMEM_SPEC] * (2 + 3 * n),
        out_specs=(VMEM_SPEC,) * (4 * n),
        out_shape=shapes * 4,
    )(vec, gsguw, *weights, *m_state, *v_state)
    return tuple(outs[k * n:(k + 1) * n] for k in range(4))


SMALL = (
    ("norm_g", VEC_NORM_G, (1, D_MODEL)),
    ("b_in", VEC_B_IN, (1, IN_W)),
    ("attn_sinks", VEC_SINKS, (1, N_Q_HEADS)),
    ("sgu_ln_g", VEC_LN_G, (1, SGU_W)),
    ("sgu_ln_b", VEC_LN_B, (1, SGU_W)),
    ("sgu_w", None, (N_SGU_HEADS * BLOCK, BLOCK)),
    ("sgu_b", VEC_SGU_B, (N_SGU_HEADS, BLOCK)),
    ("b_out", VEC_B_OUT, (1, D_MODEL)),
    ("final_norm_g", VEC_FINAL_G, (1, D_MODEL)),
)


def _local_grads(x, target, h, win_t, wout_shard, norm_g, b_in, attn_sinks, sgu_ln_g, sgu_ln_b, sgu_w, sgu_b, b_out,
                 final_g):
    sinks = attn_sinks.reshape(N_Q_HEADS)
    bias_full = jnp.repeat(sgu_b.T, HEAD_DIM, axis=1)
    q, kvx, gates, wout = _in_proj(h, b_in, win_t, wout_shard)
    out, gres, dmix, gwout, vec4 = _mixers_out_proj(sinks, q, kvx, gates, sgu_ln_g, sgu_ln_b, sgu_w, bias_full,
                                                    x, target, wout, b_out, final_g)
    dpa, gsink, gbin_a, dps, gsguw, gsgub, gln, gbin_s, gwout_shard = _mixers_bwd(
        sinks, dmix, q, kvx, out, gates, sgu_ln_g, sgu_ln_b, sgu_w, bias_full, gwout)
    gwin_own, gwin_others, gsguw_sum = _win_grad(dpa, dps, h, gsguw.reshape(N_SGU_HEADS * BLOCK, BLOCK))
    grad_x, gwin_shard, vec = _in_proj_bwd(dpa, dps, win_t, x, norm_g, gres, gwin_own, gwin_others,
                                           (gbin_a, gbin_s, gsink, gln, gsgub, vec4))
    return grad_x, gwin_shard, gwout_shard, gsguw_sum, vec


def kernel(x, norm_g, w_in, b_in, attn_sinks, sgu_ln_g, sgu_ln_b, sgu_w, sgu_b, w_out, b_out, final_norm_g, loss_target, m_norm_g, m_w_in, m_b_in, m_attn_sinks, m_sgu_ln_g, m_sgu_ln_b, m_sgu_w, m_sgu_b, m_w_out, m_b_out, m_final_norm_g, v_norm_g, v_w_in, v_b_in, v_attn_sinks, v_sgu_ln_g, v_sgu_ln_b, v_sgu_w, v_sgu_b, v_w_out, v_b_out, v_final_norm_g):
    given = dict(norm_g=norm_g, b_in=b_in, attn_sinks=attn_sinks, sgu_ln_g=sgu_ln_g, sgu_ln_b=sgu_ln_b,
                 sgu_w=sgu_w, sgu_b=sgu_b, b_out=b_out, final_norm_g=final_norm_g)
    m_given = dict(norm_g=m_norm_g, b_in=m_b_in, attn_sinks=m_attn_sinks, sgu_ln_g=m_sgu_ln_g,
                   sgu_ln_b=m_sgu_ln_b, sgu_w=m_sgu_w, sgu_b=m_sgu_b, b_out=m_b_out, final_norm_g=m_final_norm_g)
    v_given = dict(norm_g=v_norm_g, b_in=v_b_in, attn_sinks=v_attn_sinks, sgu_ln_g=v_sgu_ln_g,
                   sgu_ln_b=v_sgu_ln_b, sgu_w=v_sgu_w, sgu_b=v_sgu_b, b_out=v_b_out, final_norm_g=v_final_norm_g)

    win_t, h = _all_gather_win(w_in[0].T, x[0], norm_g)
    grad_x, gwin_t, gwout, gsguw, vec = _local_grads(
        x[0], loss_target[0], h, win_t, w_out[0], norm_g, b_in, attn_sinks, sgu_ln_g, sgu_ln_b, sgu_w[0], sgu_b[0],
        b_out, final_norm_g.reshape(1, D_MODEL))

    t = lambda a: a[0].T
    d_win, nm_win, nv_win = _adamw_shard("adamw_w_in", gwin_t, t(w_in), t(m_w_in), t(v_w_in), WIN_ROWS // 2)
    d_wout, nm_wout, nv_wout = _adamw_shard("adamw_w_out", gwout, w_out[0], m_w_out[0], v_w_out[0], WOUT_ROWS)
    as_2d = lambda d: [d[name].reshape(shape) for name, _, shape in SMALL]
    loss = vec[VEC_LOSS, 0]
    small = _adamw_replicated(vec, gsguw, as_2d(given), as_2d(m_given), as_2d(v_given))

    def assemble(big_in, big_out, k):
        vals = {name: small[k][i].reshape(given[name].shape) for i, (name, _, _) in enumerate(SMALL)}
        vals["w_in"] = big_in.T[None]
        vals["w_out"] = big_out[None]
        order = ("norm_g", "w_in", "b_in", "attn_sinks", "sgu_ln_g", "sgu_ln_b", "sgu_w", "sgu_b", "w_out",
                 "b_out", "final_norm_g")
        return [vals[name] for name in order]

    return (loss, grad_x[None],
            *assemble(gwin_t, gwout, 0), *assemble(d_win, d_wout, 1),
            *assemble(nm_win, nm_wout, 2), *assemble(nv_win, nv_wout, 3))
```

```python
import functools
import math

import jax
import jax.numpy as jnp
from jax import lax
from jax.experimental import pallas as pl
from jax.experimental.pallas import tpu as pltpu

F32 = jnp.float32
BF16 = jnp.bfloat16
MXU_DTYPE = BF16
COMM_DTYPE = BF16

D_MODEL = 1024
SEQ = 4096
HEAD_DIM = 64
N_Q_HEADS = 8
Q_PER_KV = 4
BLOCK = 128
N_BLOCKS = SEQ // BLOCK
ATTN_W = 512
KV_W = 128
SGU_W = 512
N_SGU_HEADS = 8
IN_W = 2816
NORM_EPS = 1e-5
NEG_INF = -1e30
SCALE = HEAD_DIM ** -0.5
KV0 = ATTN_W
GATE0 = ATTN_W + 2 * KV_W
SGU0 = GATE0 + ATTN_W
ATTN_SECTION = SGU0
SGU_SECTION = IN_W - SGU0

ADAM_LR = 0.001
ADAM_B1 = 0.9
ADAM_B2 = 0.999
ADAM_EPS = 1e-08
ADAM_WD = 0.01
ADAM_STEP = 10

N_DEV = 8
WIN_ROWS = IN_W // N_DEV
WOUT_ROWS = D_MODEL // N_DEV
SGUW_ROWS = N_SGU_HEADS * BLOCK // N_DEV
VEC_ROWS = 16
MESH = pl.DeviceIdType.MESH

LANES = 128
HALF = LANES // 2
N_PAIRS = N_Q_HEADS * HEAD_DIM // LANES
KVX_W = 12 * LANES
TOKEN_TILE = 512
FWD_TOKEN_TILE = 512
ATTN_FWD_AHEAD = 4
FUSED_BLOCKS = 2
SGU_MIX_AFTER_CHAIN = 0
SGU_GATES_AFTER_CHAIN = 1
SGU_GRADS_AFTER_CHAIN = 5
ATTN_BWD_AHEAD = 3
VMEM_LIMIT = 56 * 1024 * 1024

NN = (((1,), (0,)), ((), ()))
NT = (((1,), (1,)), ((), ()))
TN = (((0,), (0,)), ((), ()))


def _dot(a, b, dims=NN):
    return lax.dot_general(a.astype(MXU_DTYPE), b.astype(MXU_DTYPE), dims, preferred_element_type=F32)


def _gelu(x):
    return x * (lax.erf(x * (1.0 / math.sqrt(2.0))) + 1.0) * 0.5


def _gelu_grad(x):
    cdf = (lax.erf(x * (1.0 / math.sqrt(2.0))) + 1.0) * 0.5
    return cdf + x * jnp.exp(-0.5 * x * x) * (1.0 / math.sqrt(2.0 * math.pi))


def _silu_and_grad(z):
    s = jax.nn.sigmoid(z)
    return z * s, s * (1.0 + z * (1.0 - s))


def _params(semantics=None, vmem=None):
    kw = {}
    if semantics is not None:
        kw["dimension_semantics"] = semantics
    if vmem is not None:
        kw["vmem_limit_bytes"] = vmem
    return pltpu.CompilerParams(**kw)


def _full(shape):
    return pl.BlockSpec(shape, lambda *_: (0,) * len(shape))


VMEM_SPEC = pl.BlockSpec(memory_space=pltpu.VMEM)


RELATIONS = ((0, 0), (1, 0), (0, 1), (1, 1))


def _place():
    return lax.axis_index("x"), lax.axis_index("y"), lax.axis_index("c")


def _chip(rel):
    x, y, _ = _place()
    return (1 - x if rel[0] else x, 1 - y if rel[1] else y)


def _block_rows(place, n_rows):
    px, py, pc = place
    return pl.ds(pl.multiple_of((4 * px + 2 * py + pc) * n_rows, 16), n_rows)


class _Copies:
    def __init__(self, send_sems, recv_sems):
        self.send_sems, self.recv_sems = send_sems, recv_sems

    def __call__(self, k, src, dst, to):
        return pltpu.make_async_remote_copy(src_ref=src, dst_ref=dst, send_sem=self.send_sems.at[k],
                                            recv_sem=self.recv_sems.at[k], device_id=to, device_id_type=MESH)


def _gather_plan(copies, sem0, full_ref, n_rows):
    x, y, c = _place()
    me, sibling = (x, y, c), (x, y, 1 - c)
    chips = [_chip(rel) for rel in RELATIONS[1:]]

    def cp(k, block, to):
        rows = full_ref.at[_block_rows(block, n_rows), :]
        return copies(sem0 + k, rows, rows, to)

    first = [cp(0, me, sibling)] + [cp(1 + j, me, (*chip, c)) for j, chip in enumerate(chips)]
    passed = [cp(4 + j, (*chip, c), sibling) for j, chip in enumerate(chips)]

    def start():
        for f in first:
            f.start()

    def forward():
        for j, chip in enumerate(chips):
            cp(1 + j, (*chip, c), me).wait_recv()
            passed[j].start()

    def finish():
        cp(0, sibling, me).wait_recv()
        for j, chip in enumerate(chips):
            cp(4 + j, (*chip, 1 - c), me).wait_recv()
        for f in first + passed:
            f.wait_send()

    return start, forward, finish


GATHER_SEMS = 7


def _reduce_scatter_plan(copies, sem0, part_ref, n_rows, sa, ra, sb, rc, res_ref):
    x, y, c = _place()
    sibling = (x, y, 1 - c)
    n = n_rows
    level1 = copies(sem0, sa, ra, sibling)

    def level2(i):
        slot = pl.ds((i - 1) * n, n)
        return copies(sem0 + i, sb.at[slot, :], rc.at[slot, :], (*_chip(RELATIONS[i]), c))

    def start():
        for i, rel in enumerate(RELATIONS):
            sa[i * n:(i + 1) * n, :] = part_ref[_block_rows((*_chip(rel), 1 - c), n), :].astype(sa.dtype)
        level1.start()

    def exchange():
        level1.wait_recv()
        for i, rel in enumerate(RELATIONS):
            total = part_ref[_block_rows((*_chip(rel), c), n), :] + ra[i * n:(i + 1) * n, :].astype(F32)
            if i == 0:
                res_ref[...] = total
            else:
                sb[(i - 1) * n:i * n, :] = total.astype(sb.dtype)
                level2(i).start()

    def finish():
        acc = res_ref[...]
        for i in range(1, len(RELATIONS)):
            level2(i).wait_recv()
            acc = acc + rc[(i - 1) * n:i * n, :].astype(F32)
        res_ref[...] = acc
        level1.wait_send()
        for i in range(1, len(RELATIONS)):
            level2(i).wait_send()

    return start, exchange, finish


REDUCE_SEMS = 4


def _owner_sums_plan(copies, sem0, own_ref, sb, rc, res_ref):
    _, _, c = _place()
    n = own_ref.shape[0]

    def level2(i):
        slot = pl.ds((i - 1) * n, n)
        return copies(sem0 + i - 1, sb.at[slot, :], rc.at[slot, :], (*_chip(RELATIONS[i]), c))

    def send():
        for i in range(1, len(RELATIONS)):
            level2(i).start()

    def finish():
        acc = own_ref[...]
        for i in range(1, len(RELATIONS)):
            level2(i).wait_recv()
            acc = acc + rc[(i - 1) * n:i * n, :].astype(F32)
        res_ref[...] = acc
        for i in range(1, len(RELATIONS)):
            level2(i).wait_send()

    return send, finish


OWNER_SEMS = 3


def _reduce_scatter_scratch(n_rows, width, dtype):
    return [pltpu.VMEM((4 * n_rows, width), dtype), pltpu.VMEM((4 * n_rows, width), dtype),
            pltpu.VMEM((3 * n_rows, width), dtype), pltpu.VMEM((3 * n_rows, width), dtype)]


def _dma_sems(n):
    return [pltpu.SemaphoreType.DMA((n,)), pltpu.SemaphoreType.DMA((n,))]


def _all_gather_win(win_t_shard, x, norm_g):
    tm = FWD_TOKEN_TILE
    steps = SEQ // tm

    def body(win_ref, x_ref, g_ref, full_ref, h_ref, landing, send_sems, recv_sems):
        step = pl.program_id(0)
        start, forward, finish = _gather_plan(_Copies(send_sems, recv_sems), 0, landing, WIN_ROWS)

        @pl.when(step == 0)
        def _():
            landing[_block_rows(_place(), WIN_ROWS), :] = win_ref[...].astype(COMM_DTYPE)
            start()

        xv = x_ref[...]
        r = lax.rsqrt(jnp.mean(xv * xv, axis=-1, keepdims=True) + NORM_EPS)
        h_ref[...] = ((xv * r) * g_ref[...]).astype(MXU_DTYPE)

        @pl.when(step == steps - 1)
        def _():
            forward()
            finish()
            full_ref[...] = landing[...]

    return pl.pallas_call(
        body,
        name="all_gather_win",
        grid=(steps,),
        in_specs=[VMEM_SPEC, pl.BlockSpec((tm, D_MODEL), lambda i: (i, 0)), _full((1, D_MODEL))],
        out_specs=(_full((IN_W, D_MODEL)), pl.BlockSpec((tm, D_MODEL), lambda i: (i, 0))),
        out_shape=(jax.ShapeDtypeStruct((IN_W, D_MODEL), COMM_DTYPE),
                   jax.ShapeDtypeStruct((SEQ, D_MODEL), MXU_DTYPE)),
        scratch_shapes=[pltpu.VMEM((IN_W, D_MODEL), COMM_DTYPE)] + _dma_sems(GATHER_SEMS),
        compiler_params=_params(("arbitrary",), VMEM_LIMIT),
    )(win_t_shard, x, norm_g)


def _in_proj(h, b_in, win_t, wout_shard):
    tm = FWD_TOKEN_TILE
    steps = SEQ // tm

    def body(h_ref, b_ref, w_ref, wout_ref, q_ref, kvx_ref, gate_ref, wfull_ref, landing, send_sems, recv_sems):
        step = pl.program_id(0)
        start, forward, finish = _gather_plan(_Copies(send_sems, recv_sems), 0, landing, WOUT_ROWS)

        @pl.when(step == 0)
        def _():
            landing[_block_rows(_place(), WOUT_ROWS), :] = wout_ref[...].astype(COMM_DTYPE)
            start()

        pl.when(step == steps // 2)(forward)

        h = h_ref[...]

        def proj(lo, hi):
            return _dot(h, w_ref[lo:hi, :], NT) + b_ref[:, lo:hi]

        qs = proj(0, ATTN_W) * SCALE
        for pair in range(N_PAIRS):
            q_ref[pair] = qs[:, pair * LANES:(pair + 1) * LANES].astype(MXU_DTYPE)
        kv = proj(KV0, GATE0)
        low = lax.broadcasted_iota(jnp.int32, (tm, LANES), 1) < HALF
        for i in range(2):
            t = kv[:, i * LANES:(i + 1) * LANES]
            rot = pltpu.roll(t, HALF, 1)
            variants = (jnp.where(low, t, 0.0), jnp.where(low, 0.0, rot),
                        jnp.where(low, rot, 0.0), jnp.where(low, 0.0, t))
            for j, val in enumerate(variants):
                col = (4 * i + j) * LANES
                kvx_ref[:, col:col + LANES] = val.astype(MXU_DTYPE)
                if i == 1:
                    ones_elsewhere = jnp.where(low == (j % 2 == 0), val, 1.0)
                    kvx_ref[:, col + 4 * LANES:col + 5 * LANES] = ones_elsewhere.astype(MXU_DTYPE)
        for k in range(4):
            gate_ref[k] = proj(GATE0 + k * SGU_W, GATE0 + (k + 1) * SGU_W)

        @pl.when(step == steps - 1)
        def _():
            finish()
            wfull_ref[...] = landing[...]

    return pl.pallas_call(
        body,
        name="in_proj",
        grid=(steps,),
        in_specs=[pl.BlockSpec((tm, D_MODEL), lambda i: (i, 0)),
                  _full((1, IN_W)), _full((IN_W, D_MODEL)), VMEM_SPEC],
        out_specs=(pl.BlockSpec((N_PAIRS, tm, LANES), lambda i: (0, i, 0)),
                   pl.BlockSpec((tm, KVX_W), lambda i: (i, 0)),
                   pl.BlockSpec((4, tm, SGU_W), lambda i: (0, i, 0)),
                   _full((D_MODEL, D_MODEL))),
        out_shape=(jax.ShapeDtypeStruct((N_PAIRS, SEQ, LANES), MXU_DTYPE),
                   jax.ShapeDtypeStruct((SEQ, KVX_W), MXU_DTYPE),
                   jax.ShapeDtypeStruct((4, SEQ, SGU_W), F32),
                   jax.ShapeDtypeStruct((D_MODEL, D_MODEL), COMM_DTYPE)),
        scratch_shapes=[pltpu.VMEM((D_MODEL, D_MODEL), COMM_DTYPE)] + _dma_sems(GATHER_SEMS),
        compiler_params=_params(("arbitrary",), VMEM_LIMIT),
    )(h, b_in, win_t, wout_shard)


def _window_mask(n):
    qi = lax.broadcasted_iota(jnp.int32, (2 * BLOCK, 2 * BLOCK), 0) & (BLOCK - 1)
    p = lax.broadcasted_iota(jnp.int32, (2 * BLOCK, 2 * BLOCK), 1) - BLOCK
    in_window = jnp.logical_and(p <= qi, p > qi - BLOCK)
    return jnp.logical_and(in_window, jnp.logical_or(p >= 0, n > 0))


def _sink_column(sink_ref, g, par):
    return jnp.concatenate([jnp.full((BLOCK, 1), sink_ref[4 * g + par], F32),
                            jnp.full((BLOCK, 1), sink_ref[4 * g + 2 + par], F32)], axis=0)


def _kv_cat(kp_ref, kc_ref, var, with_ones):
    kcol, vcol = var * LANES, (var + (8 if with_ones else 4)) * LANES
    return (jnp.concatenate([kp_ref[:, kcol:kcol + LANES], kc_ref[:, kcol:kcol + LANES]], axis=0),
            jnp.concatenate([kp_ref[:, vcol:vcol + LANES], kc_ref[:, vcol:vcol + LANES]], axis=0))


def _softmax_numerator(s, sink):
    m = jnp.maximum(jnp.max(s, axis=1, keepdims=True), sink)
    return jnp.exp(s - m), m


def _mixers_out_proj(sinks, q, kvx, gates, ln_g, ln_b, sgu_w, bias_full, x, target, wout, b_out, final_g):
    tm = FUSED_BLOCKS * BLOCK
    n_tiles = SEQ // tm

    def body(sink_ref, q_ref, kc_ref, za_ref, us_ref, vs_ref, zs_ref, lng_ref, lnb_ref, w_ref, bias_ref,
             x_ref, t_ref, wout_ref, b_ref, gf_ref,
             out_ref, gres_ref, dmix_ref, gw_ref, vec_ref,
             kp_ref, wm_ref, mixed_next, mixed_cur, out_stage, gb_ref):
        step = pl.program_id(0)

        @pl.when(step == 0)
        def _():
            kp_ref[...] = jnp.zeros_like(kp_ref)
            _mask_sgu_weights(w_ref, wm_ref)
            gw_ref[...] = jnp.zeros_like(gw_ref)
            vec_ref[...] = jnp.zeros_like(vec_ref)
            mixed_cur[...] = jnp.zeros_like(mixed_cur)

        def mixers_block(b, after_chain=()):
            rows = slice(b * BLOCK, (b + 1) * BLOCK)
            kc = kc_ref.at[rows, :]
            u, _, _, vln = _sgu_activations(us_ref[rows, :], vs_ref[rows, :], lng_ref[...], lnb_ref[...])

            valid = _window_mask(step * FUSED_BLOCKS + b)[0:BLOCK]
            chains = [(g, par, i) for g in range(2) for par in range(2) for i in range(2)]
            kv = {(g, par): _kv_cat(kp_ref, kc, 2 * g + par, True) for g in range(2) for par in range(2)}
            scores, outs = {}, {}

            def issue_scores(k):
                g, par, i = chains[k]
                scores[k] = _dot(q_ref[2 * g + i, rows, :], kv[g, par][0], NT)

            ahead = ATTN_FWD_AHEAD
            for k in range(ahead):
                issue_scores(k)
            low = lax.broadcasted_iota(jnp.int32, (BLOCK, LANES), 1) < HALF
            for k, (g, par, i) in enumerate(chains):
                sink = sink_ref[4 * g + 2 * i + par]
                e, m = _softmax_numerator(jnp.where(valid, scores[k], NEG_INF), sink)
                if k + ahead < len(chains):
                    issue_scores(k + ahead)
                o = _dot(e, kv[g, par][1])
                outs[g, par, i] = o / (pltpu.roll(o, HALF, 1) + jnp.exp(sink - m))
                if k == SGU_MIX_AFTER_CHAIN:
                    mixed = _sgu_mix(vln, wm_ref, bias_ref)
                if k % 2 == 0 and k // 2 < len(after_chain):
                    after_chain[k // 2]()
            for pair in range(N_PAIRS):
                g, i = divmod(pair, 2)
                lanes = slice(pair * LANES, (pair + 1) * LANES)
                o = jnp.where(low, outs[g, 0, i], outs[g, 1, i])
                out_stage[pair, rows, :] = o
                gate, _ = _silu_and_grad(za_ref[rows, lanes])
                mixed_next[rows, lanes] = (o * gate).astype(MXU_DTYPE)
            kp_ref[...] = kc[...]
            for pair in range(N_SGU_HEADS // 2):
                cols = slice(pair * LANES, (pair + 1) * LANES)
                gate, _ = _silu_and_grad(zs_ref[rows, cols])
                mixed_next[rows, ATTN_W + pair * LANES:ATTN_W + (pair + 1) * LANES] = (
                    u[:, cols] * mixed[pair] * gate).astype(MXU_DTYPE)

        live = (step > 0).astype(F32)
        quarter = D_MODEL // 4
        columns = [None] * 4

        def project(j):
            def piece():
                columns[j] = _dot(mixed_cur[...], wout_ref[:, j * quarter:(j + 1) * quarter])
            return piece

        half_blocks = FUSED_BLOCKS // 2
        per_block = 4 // half_blocks
        for b in range(half_blocks):
            mixers_block(b, [project(j) for j in range(b * per_block, (b + 1) * per_block)])
        xo = x_ref[...] + jnp.concatenate(columns, axis=1) + b_ref[...]
        r = lax.rsqrt(jnp.mean(xo * xo, axis=-1, keepdims=True) + NORM_EPS)
        xn = xo * r
        gf = gf_ref[...]
        err = xn * gf - t_ref[...]
        loss = 0.5 * jnp.sum(jnp.mean(err * err, axis=-1, keepdims=True), axis=0, keepdims=True)
        dy = err * (1.0 / D_MODEL)
        dxn = dy * gf
        gres = r * (dxn - xn * jnp.mean(dxn * xn, axis=-1, keepdims=True))
        vec_ref[0:1, :] += jnp.broadcast_to(loss * live, (1, D_MODEL))
        vec_ref[1:2, :] += jnp.sum(dy * xn, axis=0, keepdims=True) * live
        vec_ref[2:3, :] += jnp.sum(gres, axis=0, keepdims=True) * live
        gres_ref[...] = gres
        gb_ref[...] = gres.astype(MXU_DTYPE)

        def branch_grad(k):
            def piece():
                dmix_ref[k] = _dot(gb_ref[...], wout_ref[k * ATTN_W:(k + 1) * ATTN_W, :], NT)
            return piece

        def weight_grad(k):
            def piece():
                rows = slice(k * ATTN_W, (k + 1) * ATTN_W)
                gw_ref[rows, :] += _dot(mixed_cur[:, rows], gb_ref[...], TN)
            return piece

        backward = [branch_grad(0), branch_grad(1), weight_grad(0), weight_grad(1)]
        for b in range(half_blocks):
            mixers_block(half_blocks + b, backward[b * per_block:(b + 1) * per_block])

        @pl.when(step < n_tiles)
        def _():
            out_ref[...] = out_stage[...]

        mixed_cur[...] = mixed_next[...]

    ahead_tile = lambda i: jnp.minimum(i, n_tiles - 1)
    behind_tile = lambda i: jnp.maximum(i - 1, 0)
    blk = lambda w: pl.BlockSpec((tm, w), lambda i: (ahead_tile(i), 0))
    tiles = pl.BlockSpec((N_PAIRS, tm, LANES), lambda i: (0, ahead_tile(i), 0))
    gate = lambda k: pl.BlockSpec((None, tm, SGU_W), lambda i: (k, ahead_tile(i), 0))
    behind = lambda w: pl.BlockSpec((tm, w), lambda i: (behind_tile(i), 0))
    return pl.pallas_call(
        body,
        name="mixers_out_proj",
        grid=(n_tiles + 1,),
        in_specs=[pl.BlockSpec(memory_space=pltpu.SMEM), tiles, blk(KVX_W), gate(0), gate(1), gate(2), gate(3),
                  _full((1, SGU_W)), _full((1, SGU_W)), _full((N_SGU_HEADS, BLOCK, BLOCK)), _full((BLOCK, SGU_W)),
                  behind(D_MODEL), behind(D_MODEL), _full((D_MODEL, D_MODEL)), _full((1, D_MODEL)),
                  _full((1, D_MODEL))],
        out_specs=(tiles, behind(D_MODEL), pl.BlockSpec((2, tm, ATTN_W), lambda i: (0, behind_tile(i), 0)),
                   _full((D_MODEL, D_MODEL)), _full((8, D_MODEL))),
        out_shape=(jax.ShapeDtypeStruct((N_PAIRS, SEQ, LANES), F32),
                   jax.ShapeDtypeStruct((SEQ, D_MODEL), F32),
                   jax.ShapeDtypeStruct((2, SEQ, ATTN_W), F32),
                   jax.ShapeDtypeStruct((D_MODEL, D_MODEL), F32),
                   jax.ShapeDtypeStruct((8, D_MODEL), F32)),
        scratch_shapes=[pltpu.VMEM((BLOCK, KVX_W), MXU_DTYPE), pltpu.VMEM((N_SGU_HEADS, BLOCK, BLOCK), MXU_DTYPE),
                        pltpu.VMEM((tm, D_MODEL), MXU_DTYPE), pltpu.VMEM((tm, D_MODEL), MXU_DTYPE),
                        pltpu.VMEM((N_PAIRS, tm, LANES), F32), pltpu.VMEM((tm, D_MODEL), MXU_DTYPE)],
        compiler_params=_params(("arbitrary",), VMEM_LIMIT),
    )(sinks, q, kvx, gates, gates, gates, gates, ln_g, ln_b, sgu_w, bias_full, x, target, wout, b_out, final_g)


def _sgu_activations(us, vs, lng, lnb):
    u = _gelu(us)
    vg = _gelu(vs)
    mu = jnp.mean(vg, axis=-1, keepdims=True)
    xc = vg - mu
    rstd = lax.rsqrt(jnp.mean(xc * xc, axis=-1, keepdims=True) + NORM_EPS)
    vhat = xc * rstd
    return u, vhat, rstd, vhat * lng + lnb


def _mask_sgu_weights(w_ref, masked_ref, transposed_ref=None):
    tril = (lax.broadcasted_iota(jnp.int32, (BLOCK, BLOCK), 0)
            >= lax.broadcasted_iota(jnp.int32, (BLOCK, BLOCK), 1))
    for hh in range(N_SGU_HEADS):
        w = jnp.where(tril, w_ref[hh], 0.0)
        masked_ref[hh] = w.astype(MXU_DTYPE)
        if transposed_ref is not None:
            transposed_ref[hh] = w.T.astype(MXU_DTYPE)


def _sgu_mix(vln, masked_w_ref, bias_ref):
    low = lax.broadcasted_iota(jnp.int32, (BLOCK, LANES), 1) < HALF
    mixed = []
    for pair in range(N_SGU_HEADS // 2):
        vp = vln[:, pair * LANES:(pair + 1) * LANES]
        mixed.append(_dot(masked_w_ref[2 * pair], jnp.where(low, vp, 0.0))
                     + _dot(masked_w_ref[2 * pair + 1], jnp.where(low, 0.0, vp))
                     + bias_ref[:, pair * LANES:(pair + 1) * LANES])
    return mixed


def _mixers_bwd(sinks, dmix, q, kvx, out, gates, ln_g, ln_b, sgu_w, bias_full, gwout):
    last = N_BLOCKS - 1

    def body(sink_ref, d_ref, q_ref, kc_ref, o_ref, za_ref, dsg_ref, us_ref, vs_ref, zs_ref, lng_ref, lnb_ref, w_ref,
             bias_ref, gwout_ref,
             dp_ref, gsink_ref, gbin_ref, dps_ref, gw_ref, gb_ref, gln_ref, gbins_ref, wout_shard_ref,
             kp_ref, pend_ref, carry_ref, wm_ref, wt_ref, gbias_ref, sa_w, ra_w, sb_w, rc_w, send_sems, recv_sems):
        n = pl.program_id(0)
        start, exchange, finish = _reduce_scatter_plan(_Copies(send_sems, recv_sems), 0, gwout_ref, WOUT_ROWS,
                                                       sa_w, ra_w, sb_w, rc_w, wout_shard_ref)
        tril = (lax.broadcasted_iota(jnp.int32, (BLOCK, BLOCK), 0)
                >= lax.broadcasted_iota(jnp.int32, (BLOCK, BLOCK), 1))

        @pl.when(n == 0)
        def _():
            gsink_ref[...] = jnp.zeros_like(gsink_ref)
            gbin_ref[...] = jnp.zeros_like(gbin_ref)
            carry_ref[...] = jnp.zeros_like(carry_ref)
            kp_ref[...] = jnp.zeros_like(kp_ref)
            gw_ref[...] = jnp.zeros_like(gw_ref)
            gln_ref[...] = jnp.zeros_like(gln_ref)
            gbins_ref[...] = jnp.zeros_like(gbins_ref)
            gbias_ref[...] = jnp.zeros_like(gbias_ref)
            _mask_sgu_weights(w_ref, wm_ref, wt_ref)
            start()

        pl.when(n == 3)(exchange)
        pl.when(n == 12)(finish)

        @pl.when(n > 0)
        def _():
            dp_ref[:, 0:ATTN_W] = pend_ref[:, 0:ATTN_W]
            dp_ref[:, GATE0:ATTN_SECTION] = pend_ref[:, ATTN_W:]

        @pl.when(n > last)
        def _():
            dp_ref[:, KV0:GATE0] = carry_ref[...].astype(MXU_DTYPE)

        @pl.when(n <= last)
        def _():
            us = us_ref[...]
            vs = vs_ref[...]
            lng = lng_ref[...]
            u, vhat, rstd, vln = _sgu_activations(us, vs, lng, lnb_ref[...])
            low_sgu = lax.broadcasted_iota(jnp.int32, (BLOCK, LANES), 1) < HALF
            sgu = {}

            def sgu_gates():
                mixed = _sgu_mix(vln, wm_ref, bias_ref)
                sgu["du"], sgu["dzs"], sgu["dm"] = [], [], []
                for pair in range(N_SGU_HEADS // 2):
                    cols = slice(pair * LANES, (pair + 1) * LANES)
                    dsg = dsg_ref[:, cols]
                    gate, gate_grad = _silu_and_grad(zs_ref[:, cols])
                    up = u[:, cols]
                    sgu["du"].append(dsg * mixed[pair] * gate)
                    sgu["dzs"].append(dsg * up * mixed[pair] * gate_grad)
                    dmixed = dsg * up * gate
                    gbias_ref[:, cols] += dmixed
                    sgu["dm"].append((jnp.where(low_sgu, dmixed, 0.0).astype(MXU_DTYPE),
                                      jnp.where(low_sgu, 0.0, dmixed).astype(MXU_DTYPE)))

            def sgu_grads():
                dvln_parts = []
                for pair in range(N_SGU_HEADS // 2):
                    dm_lo, dm_hi = sgu["dm"][pair]
                    vp = vln[:, pair * LANES:(pair + 1) * LANES]
                    gw_ref[2 * pair] += _dot(dm_lo, vp, NT)
                    gw_ref[2 * pair + 1] += _dot(dm_hi, vp, NT)
                    dvln_parts.append(_dot(wt_ref[2 * pair], dm_lo) + _dot(wt_ref[2 * pair + 1], dm_hi))
                dvln = jnp.concatenate(dvln_parts, axis=1)
                gln_ref[0:1, :] += jnp.sum(dvln * vhat, axis=0, keepdims=True)
                gln_ref[1:2, :] += jnp.sum(dvln, axis=0, keepdims=True)
                dvhat = dvln * lng
                dvg = rstd * (dvhat - jnp.mean(dvhat, axis=-1, keepdims=True)
                              - vhat * jnp.mean(dvhat * vhat, axis=-1, keepdims=True))
                dus = jnp.concatenate(sgu["du"], axis=1) * _gelu_grad(us)
                dvs = dvg * _gelu_grad(vs)
                dzs = jnp.concatenate(sgu["dzs"], axis=1)
                for k, val in enumerate((dus, dvs, dzs)):
                    dps_ref[:, k * SGU_W:(k + 1) * SGU_W] = val.astype(MXU_DTYPE)
                    gbins_ref[:, k * SGU_W:(k + 1) * SGU_W] += jnp.sum(val, axis=0, keepdims=True)

            valid = _window_mask(n)[0:BLOCK]
            low = lax.broadcasted_iota(jnp.int32, (BLOCK, LANES), 1) < HALF
            low_keys = lax.broadcasted_iota(jnp.int32, (2 * BLOCK, LANES), 1) < HALF
            lane_row = lax.broadcasted_iota(jnp.int32, (1, LANES), 1)
            gsink = jnp.zeros((1, LANES), F32)
            chains = [(g, par, i) for g in range(2) for par in range(2) for i in range(2)]
            kv = {(g, par): _kv_cat(kp_ref, kc_ref, 2 * g + par, False) for g in range(2) for par in range(2)}
            ones_keys = jnp.ones((2 * BLOCK, LANES), MXU_DTYPE)
            half_of_lane = lax.broadcasted_iota(jnp.int32, (LANES, 2 * LANES), 0) // HALF
            half_of_col = lax.broadcasted_iota(jnp.int32, (LANES, 2 * LANES), 1) // LANES
            sum_halves = (half_of_lane == half_of_col).astype(MXU_DTYPE)
            douts, deltas = [], []
            for pair in range(N_PAIRS):
                lanes = slice(pair * LANES, (pair + 1) * LANES)
                dg = d_ref[:, lanes]
                gate, gate_grad = _silu_and_grad(za_ref[:, lanes])
                o = o_ref[pair]
                dout = dg * gate
                dza = dg * o * gate_grad
                douts.append(dout.astype(MXU_DTYPE))
                deltas.append(_dot(dout * o, sum_halves))
                zl = slice(ATTN_W + pair * LANES, ATTN_W + (pair + 1) * LANES)
                pend_ref[:, zl] = dza.astype(MXU_DTYPE)
                gl = slice(GATE0 + pair * LANES, GATE0 + (pair + 1) * LANES)
                gbin_ref[:, gl] += jnp.sum(dza, axis=0, keepdims=True)

            first = {}

            def issue_first(k):
                g, par, i = chains[k]
                first[k] = (_dot(q_ref[2 * g + i], kv[g, par][0], NT), _dot(douts[2 * g + i], kv[g, par][1], NT))

            numerators = {}

            def issue_row_sums(k):
                g, par, i = chains[k]
                sink = sink_ref[4 * g + 2 * i + par]
                e, m = _softmax_numerator(jnp.where(valid, first[k][0], NEG_INF), sink)
                numerators[k] = (e, jnp.exp(sink - m), _dot(e, ones_keys))

            ahead = ATTN_BWD_AHEAD
            for k in range(ahead):
                issue_first(k)
            issue_row_sums(0)
            issue_row_sums(1)
            dqs, dk_parts, dv_parts = {}, {}, {}
            operands = {}

            def issue_last(k):
                g, par, i = chains[k]
                ds, ds_t, p_t = operands.pop(k)
                dq = _dot(ds, kv[g, par][0])
                dqs[g, i] = dq if par == 0 else dqs[g, i] + dq
                dk = _dot(ds_t, q_ref[2 * g + i])
                dv = _dot(p_t, douts[2 * g + i])
                dk_parts[g, par] = dk if i == 0 else dk_parts[g, par] + dk
                dv_parts[g, par] = dv if i == 0 else dv_parts[g, par] + dv

            for k, (g, par, i) in enumerate(chains):
                h = 4 * g + 2 * i + par
                delta = deltas[2 * g + i][:, par * LANES:(par + 1) * LANES]
                e, at_sink, row_sum = numerators[k]
                inv = 1.0 / (row_sum + at_sink)
                p = e * jnp.tile(inv, (1, 2))
                ds = p * (first[k][1] - jnp.tile(delta, (1, 2)))
                ds = ds.astype(MXU_DTYPE)
                operands[k] = (ds, ds.T, p.astype(MXU_DTYPE).T)
                total = jnp.sum(at_sink * inv * delta, axis=0, keepdims=True)
                gsink = jnp.where(lane_row == h, -total, gsink)
                if k + ahead < len(chains):
                    issue_first(k + ahead)
                if k + 2 < len(chains):
                    issue_row_sums(k + 2)
                if k > 0:
                    issue_last(k - 1)
                if k == SGU_GATES_AFTER_CHAIN:
                    sgu_gates()
                if k == SGU_GRADS_AFTER_CHAIN:
                    sgu_grads()
            issue_last(len(chains) - 1)
            for pair in range(N_PAIRS):
                g, i = divmod(pair, 2)
                dq = dqs[g, i] * SCALE
                lanes = slice(pair * LANES, (pair + 1) * LANES)
                pend_ref[:, lanes] = dq.astype(MXU_DTYPE)
                gbin_ref[:, lanes] += jnp.sum(dq, axis=0, keepdims=True)
            gsink_ref[...] += gsink
            for k, parts in enumerate((dk_parts, dv_parts)):
                masked = {key: jnp.where(low_keys if key[1] == 0 else jnp.logical_not(low_keys), val, 0.0)
                          for key, val in parts.items()}
                both = (masked[0, 0] + masked[1, 1]
                        + pltpu.roll(masked[0, 1] + masked[1, 0], HALF, 1))
                lanes = slice(k * KV_W, (k + 1) * KV_W)
                done = carry_ref[:, lanes] + both[0:BLOCK]
                dp_ref[:, KV0 + k * KV_W:KV0 + (k + 1) * KV_W] = done.astype(MXU_DTYPE)
                carry_ref[:, lanes] = both[BLOCK:]
                gbin_ref[:, KV0 + k * KV_W:KV0 + (k + 1) * KV_W] += jnp.sum(both, axis=0, keepdims=True)
            kp_ref[...] = kc_ref[...]

        @pl.when(n == last)
        def _():
            for hh in range(N_SGU_HEADS):
                gw_ref[hh] = jnp.where(tril, gw_ref[hh], 0.0)
            head_of_lane = lax.broadcasted_iota(jnp.int32, (N_SGU_HEADS, SGU_W), 1) // HEAD_DIM
            select = (head_of_lane == lax.broadcasted_iota(jnp.int32, (N_SGU_HEADS, SGU_W), 0)).astype(F32)
            gb_ref[...] = lax.dot_general(select, gbias_ref[...], NT, precision=lax.Precision.HIGHEST,
                                          preferred_element_type=F32)

    at = lambda n: jnp.minimum(n, last)
    blk = lambda w: pl.BlockSpec((BLOCK, w), lambda n: (at(n), 0))
    tiles = pl.BlockSpec((N_PAIRS, BLOCK, LANES), lambda n: (0, at(n), 0))
    section = lambda k: pl.BlockSpec((None, BLOCK, SGU_W), lambda n: (k, at(n), 0))
    return pl.pallas_call(
        body,
        name="mixers_bwd",
        grid=(N_BLOCKS + 1,),
        in_specs=[pl.BlockSpec(memory_space=pltpu.SMEM),
                  section(0),
                  tiles,
                  blk(KVX_W),
                  tiles,
                  section(0),
                  section(1),
                  section(1), section(2), section(3),
                  _full((1, SGU_W)), _full((1, SGU_W)), _full((N_SGU_HEADS, BLOCK, BLOCK)), _full((BLOCK, SGU_W)),
                  VMEM_SPEC],
        out_specs=(pl.BlockSpec((BLOCK, ATTN_SECTION), lambda n: (jnp.maximum(n - 1, 0), 0)),
                   _full((1, LANES)), _full((1, ATTN_SECTION)),
                   pl.BlockSpec((BLOCK, SGU_SECTION), lambda n: (at(n), 0)),
                   _full((N_SGU_HEADS, BLOCK, BLOCK)), _full((N_SGU_HEADS, BLOCK)),
                   _full((8, SGU_W)), _full((1, SGU_SECTION)), VMEM_SPEC),
        out_shape=(jax.ShapeDtypeStruct((SEQ, ATTN_SECTION), MXU_DTYPE),
                   jax.ShapeDtypeStruct((1, LANES), F32),
                   jax.ShapeDtypeStruct((1, ATTN_SECTION), F32),
                   jax.ShapeDtypeStruct((SEQ, SGU_SECTION), MXU_DTYPE),
                   jax.ShapeDtypeStruct((N_SGU_HEADS, BLOCK, BLOCK), F32),
                   jax.ShapeDtypeStruct((N_SGU_HEADS, BLOCK), F32),
                   jax.ShapeDtypeStruct((8, SGU_W), F32),
                   jax.ShapeDtypeStruct((1, SGU_SECTION), F32),
                   jax.ShapeDtypeStruct((WOUT_ROWS, D_MODEL), F32)),
        scratch_shapes=([pltpu.VMEM((BLOCK, KVX_W), MXU_DTYPE),
                         pltpu.VMEM((BLOCK, 2 * ATTN_W), MXU_DTYPE), pltpu.VMEM((BLOCK, 2 * KV_W), F32),
                         pltpu.VMEM((N_SGU_HEADS, BLOCK, BLOCK), MXU_DTYPE),
                         pltpu.VMEM((N_SGU_HEADS, BLOCK, BLOCK), MXU_DTYPE), pltpu.VMEM((BLOCK, SGU_W), F32)]
                        + _reduce_scatter_scratch(WOUT_ROWS, D_MODEL, COMM_DTYPE) + _dma_sems(REDUCE_SEMS)),
        compiler_params=_params(("arbitrary",), VMEM_LIMIT),
    )(sinks, dmix, q, kvx, out, gates, dmix, gates, gates, gates, ln_g, ln_b, sgu_w, bias_full, gwout)


def _in_proj_bwd(dpa, dps, win_t, x, norm_g, gres, gwin_own, gwin_others, vec_parts):
    tm = TOKEN_TILE
    steps = SEQ // tm
    n_parts = len(vec_parts)

    def body(da_ref, ds_ref, w_ref, x_ref, g_ref, gres_ref, own_ref, others_ref, *rest):
        part_refs = rest[:n_parts]
        gx_ref, shard_ref, vec_out_ref, gng_ref, rc, vec_ref, ra_vec, slots, send_sems, recv_sems = rest[n_parts:]
        step = pl.program_id(0)
        copies = _Copies(send_sems, recv_sems)
        send, finish = _owner_sums_plan(copies, 0, own_ref, others_ref, rc, shard_ref)

        @pl.when(step == 0)
        def _():
            gng_ref[...] = jnp.zeros_like(gng_ref)
            send()

        dh = _dot(da_ref[...], w_ref[0:ATTN_SECTION, :]) + _dot(ds_ref[...], w_ref[ATTN_SECTION:, :])
        xv = x_ref[...]
        r = lax.rsqrt(jnp.mean(xv * xv, axis=-1, keepdims=True) + NORM_EPS)
        xn = xv * r
        gng_ref[...] += jnp.sum(dh * xn, axis=0, keepdims=True)
        dxn = dh * g_ref[...]
        gx_ref[...] = r * (dxn - xn * jnp.mean(dxn * xn, axis=-1, keepdims=True)) + gres_ref[...]

        @pl.when(step == steps - 1)
        def _():
            finish()
            _all_reduce_vectors(copies, OWNER_SEMS, gng_ref, *part_refs, vec_out_ref, vec_ref, ra_vec, slots)

    tile = lambda w: pl.BlockSpec((tm, w), lambda i: (i, 0))
    return pl.pallas_call(
        body,
        name="in_proj_bwd",
        grid=(steps,),
        in_specs=[tile(ATTN_SECTION), tile(SGU_SECTION), _full((IN_W, D_MODEL)), tile(D_MODEL),
                  _full((1, D_MODEL)), tile(D_MODEL), VMEM_SPEC, VMEM_SPEC] + [VMEM_SPEC] * n_parts,
        out_specs=(tile(D_MODEL), VMEM_SPEC, VMEM_SPEC),
        out_shape=(jax.ShapeDtypeStruct((SEQ, D_MODEL), F32),
                   jax.ShapeDtypeStruct((WIN_ROWS, D_MODEL), F32),
                   jax.ShapeDtypeStruct((VEC_ROWS, IN_W), F32)),
        scratch_shapes=([pltpu.VMEM((1, D_MODEL), F32), pltpu.VMEM((3 * WIN_ROWS, D_MODEL), COMM_DTYPE)]
                        + _vector_scratch() + _dma_sems(OWNER_SEMS + VECTOR_SEMS)),
        input_output_aliases={5: 0},
        compiler_params=_params(("arbitrary",), VMEM_LIMIT),
    )(dpa, dps, win_t, x, norm_g, gres, gwin_own, gwin_others, *vec_parts)


def _win_grad_pieces(rows):
    pieces = []
    for step in range(IN_W // rows):
        for owner in range(N_DEV):
            lo, hi = max(step * rows, owner * WIN_ROWS), min((step + 1) * rows, (owner + 1) * WIN_ROWS)
            if lo < hi:
                pieces.append((len(pieces), step, owner, lo, hi - lo))
    return pieces


def _win_grad(dpa, dps, h, gsguw):
    rows = 256
    n_attn = ATTN_SECTION // rows
    steps = IN_W // rows
    pieces = _win_grad_pieces(rows)
    class_rows = (N_DEV // 2) * WIN_ROWS

    def body(da_ref, ds_ref, h_ref, gsguw_ref, own_ref, others_ref, sguw_full_ref,
             chunks, sa, ra, sa_s, ra_s, sb_s, rc_s, landing, send_sems, recv_sems, give_sems, take_sems):
        step = pl.program_id(0)
        x, y, c = _place()
        copies = _Copies(send_sems, recv_sems)
        own_sguw = landing.at[_block_rows((x, y, c), SGUW_ROWS), :]
        start, exchange, finish = _reduce_scatter_plan(copies, 0, gsguw_ref, SGUW_ROWS, sa_s, ra_s, sb_s, rc_s,
                                                       own_sguw)
        gather = _gather_plan(copies, REDUCE_SEMS, landing, SGUW_ROWS)

        def class_rows_of(owner, first, n):
            return pl.ds((owner // 2) * WIN_ROWS + first - owner * WIN_ROWS, n)

        def to_sibling(piece):
            k, _, owner, first, n = piece
            at = class_rows_of(owner, first, n)
            return pltpu.make_async_remote_copy(src_ref=sa.at[at, :], dst_ref=ra.at[at, :], send_sem=give_sems.at[k],
                                                recv_sem=take_sems.at[k], device_id=(x, y, 1 - c), device_id_type=MESH)

        def give(piece):
            k, at_step, owner, first, n = piece

            @pl.when(c != owner % 2)
            def _():
                sa[class_rows_of(owner, first, n), :] = chunks[at_step % 2, pl.ds(first % rows, n), :].astype(sa.dtype)
                to_sibling(piece).start()

        def keep(piece):
            k, at_step, owner, first, n = piece
            px, py = owner // 4, (owner // 2) % 2

            @pl.when(c == owner % 2)
            def _():
                to_sibling(piece).wait_recv()
                total = (chunks[at_step % 2, pl.ds(first % rows, n), :]
                         + ra[class_rows_of(owner, first, n), :].astype(F32))
                relation = (x + px - 2 * x * px) + 2 * (y + py - 2 * y * py)

                @pl.when(relation == 0)
                def _():
                    own_ref[pl.ds(first - owner * WIN_ROWS, n), :] = total

                @pl.when(relation != 0)
                def _():
                    at = pl.multiple_of((relation - 1) * WIN_ROWS + first - owner * WIN_ROWS, 16)
                    others_ref[pl.ds(at, n), :] = total.astype(others_ref.dtype)

        pl.when(step == 0)(start)
        pl.when(step == 2)(exchange)

        @pl.when(step == 5)
        def _():
            finish()
            gather[0]()

        pl.when(step == 7)(gather[1])

        @pl.when(step < n_attn)
        def _():
            chunks[step % 2] = _dot(da_ref[...], h_ref[...], TN)

        @pl.when(step >= n_attn)
        def _():
            chunks[step % 2] = _dot(ds_ref[...], h_ref[...], TN)

        for at_step in range(steps):
            @pl.when(step == at_step)
            def _():
                for piece in pieces:
                    if piece[1] == at_step:
                        give(piece)
                    if piece[1] == at_step - 1:
                        keep(piece)

        @pl.when(step == steps - 1)
        def _():
            for piece in pieces:
                if piece[1] == steps - 1:
                    keep(piece)
            for piece in pieces:
                pl.when(c != piece[2] % 2)(to_sibling(piece).wait_send)
            gather[2]()
            sguw_full_ref[...] = landing[...]

    return pl.pallas_call(
        body,
        name="win_grad",
        grid=(steps,),
        in_specs=[pl.BlockSpec((SEQ, rows), lambda i: (0, jnp.minimum(i, n_attn - 1))),
                  pl.BlockSpec((SEQ, rows), lambda i: (0, jnp.maximum(i - n_attn, 0))),
                  _full((SEQ, D_MODEL)), VMEM_SPEC],
        out_specs=(VMEM_SPEC, VMEM_SPEC, _full((N_SGU_HEADS * BLOCK, BLOCK))),
        out_shape=(jax.ShapeDtypeStruct((WIN_ROWS, D_MODEL), F32),
                   jax.ShapeDtypeStruct((3 * WIN_ROWS, D_MODEL), COMM_DTYPE),
                   jax.ShapeDtypeStruct((N_SGU_HEADS * BLOCK, BLOCK), F32)),
        scratch_shapes=([pltpu.VMEM((2, rows, D_MODEL), F32),
                         pltpu.VMEM((class_rows, D_MODEL), COMM_DTYPE), pltpu.VMEM((class_rows, D_MODEL), COMM_DTYPE)]
                        + _reduce_scatter_scratch(SGUW_ROWS, BLOCK, F32)
                        + [pltpu.VMEM((N_SGU_HEADS * BLOCK, BLOCK), F32)]
                        + _dma_sems(REDUCE_SEMS + GATHER_SEMS) + _dma_sems(len(pieces))),
        compiler_params=_params(("arbitrary",), VMEM_LIMIT),
    )(dpa, dps, h, gsguw)


VEC_NORM_G, VEC_B_IN, VEC_SINKS, VEC_LN_G, VEC_LN_B, VEC_B_OUT, VEC_FINAL_G, VEC_LOSS, VEC_SGU_B = 0, 1, 2, 3, 4, 5, 6, 7, 8


def _adamw(w, g, m, v):
    m = ADAM_B1 * m + (1.0 - ADAM_B1) * g
    v = ADAM_B2 * v + (1.0 - ADAM_B2) * (g * g)
    m_hat = m / (1.0 - ADAM_B1 ** ADAM_STEP)
    v_hat = v / (1.0 - ADAM_B2 ** ADAM_STEP)
    delta = -ADAM_LR * (m_hat / (jnp.sqrt(v_hat) + ADAM_EPS) + ADAM_WD * w)
    return delta, m, v


def _adamw_shard(name, g, w, m, v, block_rows, carried=()):
    def body(g_ref, w_ref, m_ref, v_ref, *rest):
        d_ref, nm_ref, nv_ref = rest[len(carried):len(carried) + 3]
        d_ref[...], nm_ref[...], nv_ref[...] = _adamw(w_ref[...], g_ref[...], m_ref[...], v_ref[...])

    rows, cols = w.shape
    spec = pl.BlockSpec((block_rows, cols), lambda i: (i, 0))
    anywhere = pl.BlockSpec(memory_space=pl.ANY)
    return pl.pallas_call(
        body,
        name=name,
        grid=(rows // block_rows,),
        in_specs=[spec] * 4 + [anywhere] * len(carried),
        out_specs=(spec,) * 3 + (anywhere,) * len(carried),
        out_shape=(jax.ShapeDtypeStruct(w.shape, F32),) * 3 + tuple(
            jax.ShapeDtypeStruct(a.shape, a.dtype) for a in carried),
        input_output_aliases={4 + k: 3 + k for k in range(len(carried))},
        compiler_params=_params(("arbitrary",)),
    )(g, w, m, v, *carried)


VECTOR_SEMS = 4


def _vector_scratch():
    return [pltpu.VMEM((VEC_ROWS, IN_W), F32), pltpu.VMEM((VEC_ROWS, IN_W), F32),
            pltpu.VMEM((4 * VEC_ROWS, IN_W), F32)]


def _all_reduce_vectors(copies, sem0, gng_ref, gba_ref, gbs_ref, gsink_ref, gln_ref, gsgub_ref, vec4_ref, out_ref,
                        vec_ref, ra_vec, slots):
    x, y, c = _place()
    vec_ref[...] = jnp.zeros_like(vec_ref)
    vec_ref[VEC_NORM_G:VEC_NORM_G + 1, 0:D_MODEL] = gng_ref[...]
    vec_ref[VEC_B_IN:VEC_B_IN + 1, 0:ATTN_SECTION] = gba_ref[...]
    vec_ref[VEC_B_IN:VEC_B_IN + 1, ATTN_SECTION:IN_W] = gbs_ref[...]
    vec_ref[VEC_SINKS:VEC_SINKS + 1, 0:LANES] = gsink_ref[...]
    vec_ref[VEC_LN_G:VEC_LN_G + 1, 0:SGU_W] = gln_ref[0:1, :]
    vec_ref[VEC_LN_B:VEC_LN_B + 1, 0:SGU_W] = gln_ref[1:2, :]
    vec_ref[VEC_B_OUT:VEC_B_OUT + 1, 0:D_MODEL] = vec4_ref[2:3, :]
    vec_ref[VEC_FINAL_G:VEC_FINAL_G + 1, 0:D_MODEL] = vec4_ref[1:2, :]
    vec_ref[VEC_LOSS:VEC_LOSS + 1, 0:D_MODEL] = vec4_ref[0:1, :]
    vec_ref[VEC_SGU_B:VEC_SGU_B + N_SGU_HEADS, 0:BLOCK] = gsgub_ref[...]

    to_sibling = copies(sem0, vec_ref, ra_vec, (x, y, 1 - c))
    to_sibling.start()
    to_sibling.wait_recv()

    def chip_slot(place):
        return slots.at[pl.ds(pl.multiple_of((2 * place[0] + place[1]) * VEC_ROWS, 8), VEC_ROWS), :]

    mine = chip_slot((x, y))
    mine[...] = vec_ref[...] + ra_vec[...]
    to_chips = [copies(sem0 + i, mine, mine, (*_chip(rel), c)) for i, rel in enumerate(RELATIONS[1:], start=1)]
    for cp in to_chips:
        cp.start()
    for i, rel in enumerate(RELATIONS[1:], start=1):
        theirs = chip_slot(_chip(rel))
        copies(sem0 + i, theirs, theirs, (x, y, c)).wait_recv()
    out_ref[...] = ((slots[0:VEC_ROWS, :] + slots[VEC_ROWS:2 * VEC_ROWS, :])
                    + slots[2 * VEC_ROWS:3 * VEC_ROWS, :]) + slots[3 * VEC_ROWS:, :]
    to_sibling.wait_send()
    for cp in to_chips:
        cp.wait_send()


def _adamw_replicated(vec, gsguw, weights, m_state, v_state):
    n = len(SMALL)

    def body(*refs):
        vec_ref, gsguw_ref = refs[0], refs[1]
        w_refs, m_refs, v_refs = (refs[2 + k * n:2 + (k + 1) * n] for k in range(3))
        outs = refs[2 + 3 * n:]
        g_refs, d_refs, nm_refs, nv_refs = (outs[k * n:(k + 1) * n] for k in range(4))
        for i, (_, row, shape) in enumerate(SMALL):
            g = gsguw_ref[...] if row is None else vec_ref[row:row + shape[0], 0:shape[1]]
            g_refs[i][...] = g
            d_refs[i][...], nm_refs[i][...], nv_refs[i][...] = _adamw(
                w_refs[i][...], g, m_refs[i][...], v_refs[i][...])

    shapes = tuple(jax.ShapeDtypeStruct(shape, F32) for _, _, shape in SMALL)
    outs = pl.pallas_call(
        body,
        name="adamw_replicated",
        in_specs=[VMEM_SPEC] * (2 + 3 * n),
        out_specs=(VMEM_SPEC,) * (4 * n),
        out_shape=shapes * 4,
    )(vec, gsguw, *weights, *m_state, *v_state)
    return tuple(outs[k * n:(k + 1) * n] for k in range(4))


SMALL = (
    ("norm_g", VEC_NORM_G, (1, D_MODEL)),
    ("b_in", VEC_B_IN, (1, IN_W)),
    ("attn_sinks", VEC_SINKS, (1, N_Q_HEADS)),
    ("sgu_ln_g", VEC_LN_G, (1, SGU_W)),
    ("sgu_ln_b", VEC_LN_B, (1, SGU_W)),
    ("sgu_w", None, (N_SGU_HEADS * BLOCK, BLOCK)),
    ("sgu_b", VEC_SGU_B, (N_SGU_HEADS, BLOCK)),
    ("b_out", VEC_B_OUT, (1, D_MODEL)),
    ("final_norm_g", VEC_FINAL_G, (1, D_MODEL)),
)


def _local_grads(x, target, h, win_t, wout_shard, norm_g, b_in, attn_sinks, sgu_ln_g, sgu_ln_b, sgu_w, sgu_b, b_out,
                 final_g):
    sinks = attn_sinks.reshape(N_Q_HEADS)
    bias_full = jnp.repeat(sgu_b.T, HEAD_DIM, axis=1)
    q, kvx, gates, wout = _in_proj(h, b_in, win_t, wout_shard)
    out, gres, dmix, gwout, vec4 = _mixers_out_proj(sinks, q, kvx, gates, sgu_ln_g, sgu_ln_b, sgu_w, bias_full,
                                                    x, target, wout, b_out, final_g)
    dpa, gsink, gbin_a, dps, gsguw, gsgub, gln, gbin_s, gwout_shard = _mixers_bwd(
        sinks, dmix, q, kvx, out, gates, sgu_ln_g, sgu_ln_b, sgu_w, bias_full, gwout)
    gwin_own, gwin_others, gsguw_sum = _win_grad(dpa, dps, h, gsguw.reshape(N_SGU_HEADS * BLOCK, BLOCK))
    grad_x, gwin_shard, vec = _in_proj_bwd(dpa, dps, win_t, x, norm_g, gres, gwin_own, gwin_others,
                                           (gbin_a, gbin_s, gsink, gln, gsgub, vec4))
    return grad_x, gwin_shard, gwout_shard, gsguw_sum, vec


def kernel(x, norm_g, w_in, b_in, attn_sinks, sgu_ln_g, sgu_ln_b, sgu_w, sgu_b, w_out, b_out, final_norm_g, loss_target, m_norm_g, m_w_in, m_b_in, m_attn_sinks, m_sgu_ln_g, m_sgu_ln_b, m_sgu_w, m_sgu_b, m_w_out, m_b_out, m_final_norm_g, v_norm_g, v_w_in, v_b_in, v_attn_sinks, v_sgu_ln_g, v_sgu_ln_b, v_sgu_w, v_sgu_b, v_w_out, v_b_out, v_final_norm_g):
    given = dict(norm_g=norm_g, b_in=b_in, attn_sinks=attn_sinks, sgu_ln_g=sgu_ln_g, sgu_ln_b=sgu_ln_b,
                 sgu_w=sgu_w, sgu_b=sgu_b, b_out=b_out, final_norm_g=final_norm_g)
    m_given = dict(norm_g=m_norm_g, b_in=m_b_in, attn_sinks=m_attn_sinks, sgu_ln_g=m_sgu_ln_g,
                   sgu_ln_b=m_sgu_ln_b, sgu_w=m_sgu_w, sgu_b=m_sgu_b, b_out=m_b_out, final_norm_g=m_final_norm_g)
    v_given = dict(norm_g=v_norm_g, b_in=v_b_in, attn_sinks=v_attn_sinks, sgu_ln_g=v_sgu_ln_g,
                   sgu_ln_b=v_sgu_ln_b, sgu_w=v_sgu_w, sgu_b=v_sgu_b, b_out=v_b_out, final_norm_g=v_final_norm_g)

    win_t, h = _all_gather_win(w_in[0].T, x[0], norm_g)
    grad_x, gwin_t, gwout, gsguw, vec = _local_grads(
        x[0], loss_target[0], h, win_t, w_out[0], norm_g, b_in, attn_sinks, sgu_ln_g, sgu_ln_b, sgu_w[0], sgu_b[0],
        b_out, final_norm_g.reshape(1, D_MODEL))

    t = lambda a: a[0].T
    d_win, nm_win, nv_win, grad_x = _adamw_shard("adamw_w_in", gwin_t, t(w_in), t(m_w_in), t(v_w_in), WIN_ROWS // 2,
                                                 carried=(grad_x,))
    d_wout, nm_wout, nv_wout = _adamw_shard("adamw_w_out", gwout, w_out[0], m_w_out[0], v_w_out[0], WOUT_ROWS)
    as_2d = lambda d: [d[name].reshape(shape) for name, _, shape in SMALL]
    loss = vec[VEC_LOSS, 0]
    small = _adamw_replicated(vec, gsguw, as_2d(given), as_2d(m_given), as_2d(v_given))

    def assemble(big_in, big_out, k):
        vals = {name: small[k][i].reshape(given[name].shape) for i, (name, _, _) in enumerate(SMALL)}
        vals["w_in"] = big_in.T[None]
        vals["w_out"] = big_out[None]
        order = ("norm_g", "w_in", "b_in", "attn_sinks", "sgu_ln_g", "sgu_ln_b", "sgu_w", "sgu_b", "w_out",
                 "b_out", "final_norm_g")
        return [vals[name] for name in order]

    return (loss, grad_x[None],
            *assemble(gwin_t, gwout, 0), *assemble(d_win, d_wout, 1),
            *assemble(nm_win, nm_wout, 2), *assemble(nv_win, nv_wout, 3))
```

```python
import functools
import math

import jax
import jax.numpy as jnp
from jax import lax
from jax.experimental import pallas as pl
from jax.experimental.pallas import tpu as pltpu

F32 = jnp.float32
BF16 = jnp.bfloat16
MXU_DTYPE = BF16
COMM_DTYPE = BF16

D_MODEL = 1024
SEQ = 4096
HEAD_DIM = 64
N_Q_HEADS = 8
Q_PER_KV = 4
BLOCK = 128
N_BLOCKS = SEQ // BLOCK
ATTN_W = 512
KV_W = 128
SGU_W = 512
N_SGU_HEADS = 8
IN_W = 2816
NORM_EPS = 1e-5
NEG_INF = -1e30
SCALE = HEAD_DIM ** -0.5
KV0 = ATTN_W
GATE0 = ATTN_W + 2 * KV_W
SGU0 = GATE0 + ATTN_W
ATTN_SECTION = SGU0
SGU_SECTION = IN_W - SGU0

ADAM_LR = 0.001
ADAM_B1 = 0.9
ADAM_B2 = 0.999
ADAM_EPS = 1e-08
ADAM_WD = 0.01
ADAM_STEP = 10

N_DEV = 8
WIN_ROWS = IN_W // N_DEV
WOUT_ROWS = D_MODEL // N_DEV
SGUW_ROWS = N_SGU_HEADS * BLOCK // N_DEV
VEC_ROWS = 16
MESH = pl.DeviceIdType.MESH

LANES = 128
HALF = LANES // 2
N_PAIRS = N_Q_HEADS * HEAD_DIM // LANES
KVX_W = 12 * LANES
TOKEN_TILE = 512
FWD_TOKEN_TILE = 512
ATTN_FWD_AHEAD = 4
FUSED_BLOCKS = 2
SGU_MIX_AFTER_CHAIN = 0
SGU_GATES_AFTER_CHAIN = 1
SGU_GRADS_AFTER_CHAIN = 5
ATTN_BWD_AHEAD = 3
VMEM_LIMIT = 56 * 1024 * 1024

NN = (((1,), (0,)), ((), ()))
NT = (((1,), (1,)), ((), ()))
TN = (((0,), (0,)), ((), ()))


def _dot(a, b, dims=NN):
    return lax.dot_general(a.astype(MXU_DTYPE), b.astype(MXU_DTYPE), dims, preferred_element_type=F32)


def _gelu(x):
    return x * (lax.erf(x * (1.0 / math.sqrt(2.0))) + 1.0) * 0.5


def _gelu_grad(x):
    cdf = (lax.erf(x * (1.0 / math.sqrt(2.0))) + 1.0) * 0.5
    return cdf + x * jnp.exp(-0.5 * x * x) * (1.0 / math.sqrt(2.0 * math.pi))


def _silu_and_grad(z):
    s = jax.nn.sigmoid(z)
    return z * s, s * (1.0 + z * (1.0 - s))


def _params(semantics=None, vmem=None):
    kw = {}
    if semantics is not None:
        kw["dimension_semantics"] = semantics
    if vmem is not None:
        kw["vmem_limit_bytes"] = vmem
    return pltpu.CompilerParams(**kw)


def _full(shape):
    return pl.BlockSpec(shape, lambda *_: (0,) * len(shape))


VMEM_SPEC = pl.BlockSpec(memory_space=pltpu.VMEM)


RELATIONS = ((0, 0), (1, 0), (0, 1), (1, 1))


def _place():
    return lax.axis_index("x"), lax.axis_index("y"), lax.axis_index("c")


def _chip(rel):
    x, y, _ = _place()
    return (1 - x if rel[0] else x, 1 - y if rel[1] else y)


def _block_rows(place, n_rows):
    px, py, pc = place
    return pl.ds(pl.multiple_of((4 * px + 2 * py + pc) * n_rows, 16), n_rows)


class _Copies:
    def __init__(self, send_sems, recv_sems):
        self.send_sems, self.recv_sems = send_sems, recv_sems

    def __call__(self, k, src, dst, to):
        return pltpu.make_async_remote_copy(src_ref=src, dst_ref=dst, send_sem=self.send_sems.at[k],
                                            recv_sem=self.recv_sems.at[k], device_id=to, device_id_type=MESH)


def _gather_plan(copies, sem0, full_ref, n_rows):
    x, y, c = _place()
    me, sibling = (x, y, c), (x, y, 1 - c)
    chips = [_chip(rel) for rel in RELATIONS[1:]]

    def cp(k, block, to):
        rows = full_ref.at[_block_rows(block, n_rows), :]
        return copies(sem0 + k, rows, rows, to)

    first = [cp(0, me, sibling)] + [cp(1 + j, me, (*chip, c)) for j, chip in enumerate(chips)]
    passed = [cp(4 + j, (*chip, c), sibling) for j, chip in enumerate(chips)]

    def start():
        for f in first:
            f.start()

    def forward():
        for j, chip in enumerate(chips):
            cp(1 + j, (*chip, c), me).wait_recv()
            passed[j].start()

    def finish():
        cp(0, sibling, me).wait_recv()
        for j, chip in enumerate(chips):
            cp(4 + j, (*chip, 1 - c), me).wait_recv()
        for f in first + passed:
            f.wait_send()

    return start, forward, finish


GATHER_SEMS = 7


def _reduce_scatter_plan(copies, sem0, part_ref, n_rows, sa, ra, sb, rc, res_ref):
    x, y, c = _place()
    sibling = (x, y, 1 - c)
    n = n_rows
    level1 = copies(sem0, sa, ra, sibling)

    def level2(i):
        slot = pl.ds((i - 1) * n, n)
        return copies(sem0 + i, sb.at[slot, :], rc.at[slot, :], (*_chip(RELATIONS[i]), c))

    def start():
        for i, rel in enumerate(RELATIONS):
            sa[i * n:(i + 1) * n, :] = part_ref[_block_rows((*_chip(rel), 1 - c), n), :].astype(sa.dtype)
        level1.start()

    def exchange():
        level1.wait_recv()
        for i, rel in enumerate(RELATIONS):
            total = part_ref[_block_rows((*_chip(rel), c), n), :] + ra[i * n:(i + 1) * n, :].astype(F32)
            if i == 0:
                res_ref[...] = total
            else:
                sb[(i - 1) * n:i * n, :] = total.astype(sb.dtype)
                level2(i).start()

    def finish():
        acc = res_ref[...]
        for i in range(1, len(RELATIONS)):
            level2(i).wait_recv()
            acc = acc + rc[(i - 1) * n:i * n, :].astype(F32)
        res_ref[...] = acc
        level1.wait_send()
        for i in range(1, len(RELATIONS)):
            level2(i).wait_send()

    return start, exchange, finish


REDUCE_SEMS = 4


def _owner_sums_plan(copies, sem0, own_ref, sb, rc, res_ref):
    _, _, c = _place()
    n = own_ref.shape[0]

    def level2(i):
        slot = pl.ds((i - 1) * n, n)
        return copies(sem0 + i - 1, sb.at[slot, :], rc.at[slot, :], (*_chip(RELATIONS[i]), c))

    def send():
        for i in range(1, len(RELATIONS)):
            level2(i).start()

    def finish():
        acc = own_ref[...]
        for i in range(1, len(RELATIONS)):
            level2(i).wait_recv()
            acc = acc + rc[(i - 1) * n:i * n, :].astype(F32)
        res_ref[...] = acc
        for i in range(1, len(RELATIONS)):
            level2(i).wait_send()

    return send, finish


OWNER_SEMS = 3


def _reduce_scatter_scratch(n_rows, width, dtype):
    return [pltpu.VMEM((4 * n_rows, width), dtype), pltpu.VMEM((4 * n_rows, width), dtype),
            pltpu.VMEM((3 * n_rows, width), dtype), pltpu.VMEM((3 * n_rows, width), dtype)]


def _dma_sems(n):
    return [pltpu.SemaphoreType.DMA((n,)), pltpu.SemaphoreType.DMA((n,))]


def _all_gather_win(win_t_shard, x, norm_g):
    tm = FWD_TOKEN_TILE
    steps = SEQ // tm

    def body(win_ref, x_ref, g_ref, full_ref, h_ref, landing, send_sems, recv_sems):
        step = pl.program_id(0)
        start, forward, finish = _gather_plan(_Copies(send_sems, recv_sems), 0, landing, WIN_ROWS)

        @pl.when(step == 0)
        def _():
            landing[_block_rows(_place(), WIN_ROWS), :] = win_ref[...].astype(COMM_DTYPE)
            start()

        xv = x_ref[...]
        r = lax.rsqrt(jnp.mean(xv * xv, axis=-1, keepdims=True) + NORM_EPS)
        h_ref[...] = ((xv * r) * g_ref[...]).astype(MXU_DTYPE)

        @pl.when(step == steps - 1)
        def _():
            forward()
            finish()
            full_ref[...] = landing[...]

    return pl.pallas_call(
        body,
        name="all_gather_win",
        grid=(steps,),
        in_specs=[VMEM_SPEC, pl.BlockSpec((tm, D_MODEL), lambda i: (i, 0)), _full((1, D_MODEL))],
        out_specs=(_full((IN_W, D_MODEL)), pl.BlockSpec((tm, D_MODEL), lambda i: (i, 0))),
        out_shape=(jax.ShapeDtypeStruct((IN_W, D_MODEL), COMM_DTYPE),
                   jax.ShapeDtypeStruct((SEQ, D_MODEL), MXU_DTYPE)),
        scratch_shapes=[pltpu.VMEM((IN_W, D_MODEL), COMM_DTYPE)] + _dma_sems(GATHER_SEMS),
        compiler_params=_params(("arbitrary",), VMEM_LIMIT),
    )(win_t_shard, x, norm_g)


def _in_proj(h, b_in, win_t, wout_shard):
    tm = FWD_TOKEN_TILE
    steps = SEQ // tm

    def body(h_ref, b_ref, w_ref, wout_ref, q_ref, kvx_ref, gate_ref, wfull_ref, landing, send_sems, recv_sems):
        step = pl.program_id(0)
        start, forward, finish = _gather_plan(_Copies(send_sems, recv_sems), 0, landing, WOUT_ROWS)

        @pl.when(step == 0)
        def _():
            landing[_block_rows(_place(), WOUT_ROWS), :] = wout_ref[...].astype(COMM_DTYPE)
            start()

        pl.when(step == steps // 2)(forward)

        h = h_ref[...]

        def proj(lo, hi):
            return _dot(h, w_ref[lo:hi, :], NT) + b_ref[:, lo:hi]

        qs = proj(0, ATTN_W) * SCALE
        for pair in range(N_PAIRS):
            q_ref[pair] = qs[:, pair * LANES:(pair + 1) * LANES].astype(MXU_DTYPE)
        kv = proj(KV0, GATE0)
        low = lax.broadcasted_iota(jnp.int32, (tm, LANES), 1) < HALF
        for i in range(2):
            t = kv[:, i * LANES:(i + 1) * LANES]
            rot = pltpu.roll(t, HALF, 1)
            variants = (jnp.where(low, t, 0.0), jnp.where(low, 0.0, rot),
                        jnp.where(low, rot, 0.0), jnp.where(low, 0.0, t))
            for j, val in enumerate(variants):
                col = (4 * i + j) * LANES
                kvx_ref[:, col:col + LANES] = val.astype(MXU_DTYPE)
                if i == 1:
                    ones_elsewhere = jnp.where(low == (j % 2 == 0), val, 1.0)
                    kvx_ref[:, col + 4 * LANES:col + 5 * LANES] = ones_elsewhere.astype(MXU_DTYPE)
        for k in range(4):
            gate_ref[k] = proj(GATE0 + k * SGU_W, GATE0 + (k + 1) * SGU_W)

        @pl.when(step == steps - 1)
        def _():
            finish()
            wfull_ref[...] = landing[...]

    return pl.pallas_call(
        body,
        name="in_proj",
        grid=(steps,),
        in_specs=[pl.BlockSpec((tm, D_MODEL), lambda i: (i, 0)),
                  _full((1, IN_W)), _full((IN_W, D_MODEL)), VMEM_SPEC],
        out_specs=(pl.BlockSpec((N_PAIRS, tm, LANES), lambda i: (0, i, 0)),
                   pl.BlockSpec((tm, KVX_W), lambda i: (i, 0)),
                   pl.BlockSpec((4, tm, SGU_W), lambda i: (0, i, 0)),
                   _full((D_MODEL, D_MODEL))),
        out_shape=(jax.ShapeDtypeStruct((N_PAIRS, SEQ, LANES), MXU_DTYPE),
                   jax.ShapeDtypeStruct((SEQ, KVX_W), MXU_DTYPE),
                   jax.ShapeDtypeStruct((4, SEQ, SGU_W), F32),
                   jax.ShapeDtypeStruct((D_MODEL, D_MODEL), COMM_DTYPE)),
        scratch_shapes=[pltpu.VMEM((D_MODEL, D_MODEL), COMM_DTYPE)] + _dma_sems(GATHER_SEMS),
        compiler_params=_params(("arbitrary",), VMEM_LIMIT),
    )(h, b_in, win_t, wout_shard)


def _window_mask(n):
    qi = lax.broadcasted_iota(jnp.int32, (2 * BLOCK, 2 * BLOCK), 0) & (BLOCK - 1)
    p = lax.broadcasted_iota(jnp.int32, (2 * BLOCK, 2 * BLOCK), 1) - BLOCK
    in_window = jnp.logical_and(p <= qi, p > qi - BLOCK)
    return jnp.logical_and(in_window, jnp.logical_or(p >= 0, n > 0))


def _sink_column(sink_ref, g, par):
    return jnp.concatenate([jnp.full((BLOCK, 1), sink_ref[4 * g + par], F32),
                            jnp.full((BLOCK, 1), sink_ref[4 * g + 2 + par], F32)], axis=0)


def _kv_cat(kp_ref, kc_ref, var, with_ones):
    kcol, vcol = var * LANES, (var + (8 if with_ones else 4)) * LANES
    return (jnp.concatenate([kp_ref[:, kcol:kcol + LANES], kc_ref[:, kcol:kcol + LANES]], axis=0),
            jnp.concatenate([kp_ref[:, vcol:vcol + LANES], kc_ref[:, vcol:vcol + LANES]], axis=0))


def _softmax_numerator(s, sink):
    m = jnp.maximum(jnp.max(s, axis=1, keepdims=True), sink)
    return jnp.exp(s - m), m


def _mixers_out_proj(sinks, q, kvx, gates, ln_g, ln_b, sgu_w, bias_full, x, target, wout, b_out, final_g):
    tm = FUSED_BLOCKS * BLOCK
    n_tiles = SEQ // tm

    def body(sink_ref, q_ref, kc_ref, za_ref, us_ref, vs_ref, zs_ref, lng_ref, lnb_ref, w_ref, bias_ref,
             x_ref, t_ref, wout_ref, b_ref, gf_ref,
             out_ref, gres_ref, dmix_ref, gw_ref, vec_ref,
             kp_ref, wm_ref, mixed_next, mixed_cur, out_stage, gb_ref):
        step = pl.program_id(0)

        @pl.when(step == 0)
        def _():
            kp_ref[...] = jnp.zeros_like(kp_ref)
            _mask_sgu_weights(w_ref, wm_ref)
            gw_ref[...] = jnp.zeros_like(gw_ref)
            vec_ref[...] = jnp.zeros_like(vec_ref)
            mixed_cur[...] = jnp.zeros_like(mixed_cur)

        def mixers_block(b, after_chain=()):
            rows = slice(b * BLOCK, (b + 1) * BLOCK)
            kc = kc_ref.at[rows, :]
            u, _, _, vln = _sgu_activations(us_ref[rows, :], vs_ref[rows, :], lng_ref[...], lnb_ref[...])

            valid = _window_mask(step * FUSED_BLOCKS + b)[0:BLOCK]
            chains = [(g, par, i) for g in range(2) for par in range(2) for i in range(2)]
            kv = {(g, par): _kv_cat(kp_ref, kc, 2 * g + par, True) for g in range(2) for par in range(2)}
            scores, outs = {}, {}

            def issue_scores(k):
                g, par, i = chains[k]
                scores[k] = _dot(q_ref[2 * g + i, rows, :], kv[g, par][0], NT)

            ahead = ATTN_FWD_AHEAD
            for k in range(ahead):
                issue_scores(k)
            low = lax.broadcasted_iota(jnp.int32, (BLOCK, LANES), 1) < HALF
            for k, (g, par, i) in enumerate(chains):
                sink = sink_ref[4 * g + 2 * i + par]
                e, m = _softmax_numerator(jnp.where(valid, scores[k], NEG_INF), sink)
                if k + ahead < len(chains):
                    issue_scores(k + ahead)
                o = _dot(e, kv[g, par][1])
                outs[g, par, i] = o / (pltpu.roll(o, HALF, 1) + jnp.exp(sink - m))
                if k == SGU_MIX_AFTER_CHAIN:
                    mixed = _sgu_mix(vln, wm_ref, bias_ref)
                if k % 2 == 0 and k // 2 < len(after_chain):
                    after_chain[k // 2]()
            for pair in range(N_PAIRS):
                g, i = divmod(pair, 2)
                lanes = slice(pair * LANES, (pair + 1) * LANES)
                o = jnp.where(low, outs[g, 0, i], outs[g, 1, i])
                out_stage[pair, rows, :] = o
                gate, _ = _silu_and_grad(za_ref[rows, lanes])
                mixed_next[rows, lanes] = (o * gate).astype(MXU_DTYPE)
            kp_ref[...] = kc[...]
            for pair in range(N_SGU_HEADS // 2):
                cols = slice(pair * LANES, (pair + 1) * LANES)
                gate, _ = _silu_and_grad(zs_ref[rows, cols])
                mixed_next[rows, ATTN_W + pair * LANES:ATTN_W + (pair + 1) * LANES] = (
                    u[:, cols] * mixed[pair] * gate).astype(MXU_DTYPE)

        live = (step > 0).astype(F32)
        quarter = D_MODEL // 4
        columns = [None] * 4

        def project(j):
            def piece():
                columns[j] = _dot(mixed_cur[...], wout_ref[:, j * quarter:(j + 1) * quarter])
            return piece

        half_blocks = FUSED_BLOCKS // 2
        per_block = 4 // half_blocks
        for b in range(half_blocks):
            mixers_block(b, [project(j) for j in range(b * per_block, (b + 1) * per_block)])
        xo = x_ref[...] + jnp.concatenate(columns, axis=1) + b_ref[...]
        r = lax.rsqrt(jnp.mean(xo * xo, axis=-1, keepdims=True) + NORM_EPS)
        xn = xo * r
        gf = gf_ref[...]
        err = xn * gf - t_ref[...]
        loss = 0.5 * jnp.sum(jnp.mean(err * err, axis=-1, keepdims=True), axis=0, keepdims=True)
        dy = err * (1.0 / D_MODEL)
        dxn = dy * gf
        gres = r * (dxn - xn * jnp.mean(dxn * xn, axis=-1, keepdims=True))
        vec_ref[0:1, :] += jnp.broadcast_to(loss * live, (1, D_MODEL))
        vec_ref[1:2, :] += jnp.sum(dy * xn, axis=0, keepdims=True) * live
        vec_ref[2:3, :] += jnp.sum(gres, axis=0, keepdims=True) * live
        gres_ref[...] = gres
        gb_ref[...] = gres.astype(MXU_DTYPE)

        def branch_grad(k):
            def piece():
                dmix_ref[k] = _dot(gb_ref[...], wout_ref[k * ATTN_W:(k + 1) * ATTN_W, :], NT)
            return piece

        def weight_grad(k):
            def piece():
                rows = slice(k * ATTN_W, (k + 1) * ATTN_W)
                gw_ref[rows, :] += _dot(mixed_cur[:, rows], gb_ref[...], TN)
            return piece

        backward = [branch_grad(0), branch_grad(1), weight_grad(0), weight_grad(1)]
        for b in range(half_blocks):
            mixers_block(half_blocks + b, backward[b * per_block:(b + 1) * per_block])

        @pl.when(step < n_tiles)
        def _():
            out_ref[...] = out_stage[...]

        mixed_cur[...] = mixed_next[...]

    ahead_tile = lambda i: jnp.minimum(i, n_tiles - 1)
    behind_tile = lambda i: jnp.maximum(i - 1, 0)
    blk = lambda w: pl.BlockSpec((tm, w), lambda i: (ahead_tile(i), 0))
    tiles = pl.BlockSpec((N_PAIRS, tm, LANES), lambda i: (0, ahead_tile(i), 0))
    gate = lambda k: pl.BlockSpec((None, tm, SGU_W), lambda i: (k, ahead_tile(i), 0))
    behind = lambda w: pl.BlockSpec((tm, w), lambda i: (behind_tile(i), 0))
    return pl.pallas_call(
        body,
        name="mixers_out_proj",
        grid=(n_tiles + 1,),
        in_specs=[pl.BlockSpec(memory_space=pltpu.SMEM), tiles, blk(KVX_W), gate(0), gate(1), gate(2), gate(3),
                  _full((1, SGU_W)), _full((1, SGU_W)), _full((N_SGU_HEADS, BLOCK, BLOCK)), _full((BLOCK, SGU_W)),
                  behind(D_MODEL), behind(D_MODEL), _full((D_MODEL, D_MODEL)), _full((1, D_MODEL)),
                  _full((1, D_MODEL))],
        out_specs=(tiles, behind(D_MODEL), pl.BlockSpec((2, tm, ATTN_W), lambda i: (0, behind_tile(i), 0)),
                   _full((D_MODEL, D_MODEL)), _full((8, D_MODEL))),
        out_shape=(jax.ShapeDtypeStruct((N_PAIRS, SEQ, LANES), F32),
                   jax.ShapeDtypeStruct((SEQ, D_MODEL), F32),
                   jax.ShapeDtypeStruct((2, SEQ, ATTN_W), F32),
                   jax.ShapeDtypeStruct((D_MODEL, D_MODEL), F32),
                   jax.ShapeDtypeStruct((8, D_MODEL), F32)),
        scratch_shapes=[pltpu.VMEM((BLOCK, KVX_W), MXU_DTYPE), pltpu.VMEM((N_SGU_HEADS, BLOCK, BLOCK), MXU_DTYPE),
                        pltpu.VMEM((tm, D_MODEL), MXU_DTYPE), pltpu.VMEM((tm, D_MODEL), MXU_DTYPE),
                        pltpu.VMEM((N_PAIRS, tm, LANES), F32), pltpu.VMEM((tm, D_MODEL), MXU_DTYPE)],
        compiler_params=_params(("arbitrary",), VMEM_LIMIT),
    )(sinks, q, kvx, gates, gates, gates, gates, ln_g, ln_b, sgu_w, bias_full, x, target, wout, b_out, final_g)


def _sgu_activations(us, vs, lng, lnb):
    u = _gelu(us)
    vg = _gelu(vs)
    mu = jnp.mean(vg, axis=-1, keepdims=True)
    xc = vg - mu
    rstd = lax.rsqrt(jnp.mean(xc * xc, axis=-1, keepdims=True) + NORM_EPS)
    vhat = xc * rstd
    return u, vhat, rstd, vhat * lng + lnb


def _mask_sgu_weights(w_ref, masked_ref, transposed_ref=None):
    tril = (lax.broadcasted_iota(jnp.int32, (BLOCK, BLOCK), 0)
            >= lax.broadcasted_iota(jnp.int32, (BLOCK, BLOCK), 1))
    for hh in range(N_SGU_HEADS):
        w = jnp.where(tril, w_ref[hh], 0.0)
        masked_ref[hh] = w.astype(MXU_DTYPE)
        if transposed_ref is not None:
            transposed_ref[hh] = w.T.astype(MXU_DTYPE)


def _sgu_mix(vln, masked_w_ref, bias_ref):
    low = lax.broadcasted_iota(jnp.int32, (BLOCK, LANES), 1) < HALF
    mixed = []
    for pair in range(N_SGU_HEADS // 2):
        vp = vln[:, pair * LANES:(pair + 1) * LANES]
        mixed.append(_dot(masked_w_ref[2 * pair], jnp.where(low, vp, 0.0))
                     + _dot(masked_w_ref[2 * pair + 1], jnp.where(low, 0.0, vp))
                     + bias_ref[:, pair * LANES:(pair + 1) * LANES])
    return mixed


def _mixers_bwd(sinks, dmix, q, kvx, out, gates, ln_g, ln_b, sgu_w, bias_full, gwout):
    last = N_BLOCKS - 1

    def body(sink_ref, d_ref, q_ref, kc_ref, o_ref, za_ref, dsg_ref, us_ref, vs_ref, zs_ref, lng_ref, lnb_ref, w_ref,
             bias_ref, gwout_ref,
             dp_ref, gsink_ref, gbin_ref, dps_ref, gw_ref, gb_ref, gln_ref, gbins_ref, wout_shard_ref,
             kp_ref, pend_ref, carry_ref, wm_ref, wt_ref, gbias_ref, sa_w, ra_w, sb_w, rc_w, send_sems, recv_sems):
        n = pl.program_id(0)
        start, exchange, finish = _reduce_scatter_plan(_Copies(send_sems, recv_sems), 0, gwout_ref, WOUT_ROWS,
                                                       sa_w, ra_w, sb_w, rc_w, wout_shard_ref)
        tril = (lax.broadcasted_iota(jnp.int32, (BLOCK, BLOCK), 0)
                >= lax.broadcasted_iota(jnp.int32, (BLOCK, BLOCK), 1))

        @pl.when(n == 0)
        def _():
            gsink_ref[...] = jnp.zeros_like(gsink_ref)
            gbin_ref[...] = jnp.zeros_like(gbin_ref)
            carry_ref[...] = jnp.zeros_like(carry_ref)
            kp_ref[...] = jnp.zeros_like(kp_ref)
            gw_ref[...] = jnp.zeros_like(gw_ref)
            gln_ref[...] = jnp.zeros_like(gln_ref)
            gbins_ref[...] = jnp.zeros_like(gbins_ref)
            gbias_ref[...] = jnp.zeros_like(gbias_ref)
            _mask_sgu_weights(w_ref, wm_ref, wt_ref)
            start()

        pl.when(n == 3)(exchange)
        pl.when(n == 12)(finish)

        @pl.when(n > 0)
        def _():
            dp_ref[:, 0:ATTN_W] = pend_ref[:, 0:ATTN_W]
            dp_ref[:, GATE0:ATTN_SECTION] = pend_ref[:, ATTN_W:]

        @pl.when(n > last)
        def _():
            dp_ref[:, KV0:GATE0] = carry_ref[...].astype(MXU_DTYPE)

        @pl.when(n <= last)
        def _():
            us = us_ref[...]
            vs = vs_ref[...]
            lng = lng_ref[...]
            u, vhat, rstd, vln = _sgu_activations(us, vs, lng, lnb_ref[...])
            low_sgu = lax.broadcasted_iota(jnp.int32, (BLOCK, LANES), 1) < HALF
            sgu = {}

            def sgu_gates():
                mixed = _sgu_mix(vln, wm_ref, bias_ref)
                sgu["du"], sgu["dzs"], sgu["dm"] = [], [], []
                for pair in range(N_SGU_HEADS // 2):
                    cols = slice(pair * LANES, (pair + 1) * LANES)
                    dsg = dsg_ref[:, cols]
                    gate, gate_grad = _silu_and_grad(zs_ref[:, cols])
                    up = u[:, cols]
                    sgu["du"].append(dsg * mixed[pair] * gate)
                    sgu["dzs"].append(dsg * up * mixed[pair] * gate_grad)
                    dmixed = dsg * up * gate
                    gbias_ref[:, cols] += dmixed
                    sgu["dm"].append((jnp.where(low_sgu, dmixed, 0.0).astype(MXU_DTYPE),
                                      jnp.where(low_sgu, 0.0, dmixed).astype(MXU_DTYPE)))

            def sgu_grads():
                dvln_parts = []
                for pair in range(N_SGU_HEADS // 2):
                    dm_lo, dm_hi = sgu["dm"][pair]
                    vp = vln[:, pair * LANES:(pair + 1) * LANES]
                    gw_ref[2 * pair] += _dot(dm_lo, vp, NT)
                    gw_ref[2 * pair + 1] += _dot(dm_hi, vp, NT)
                    dvln_parts.append(_dot(wt_ref[2 * pair], dm_lo) + _dot(wt_ref[2 * pair + 1], dm_hi))
                dvln = jnp.concatenate(dvln_parts, axis=1)
                gln_ref[0:1, :] += jnp.sum(dvln * vhat, axis=0, keepdims=True)
                gln_ref[1:2, :] += jnp.sum(dvln, axis=0, keepdims=True)
                dvhat = dvln * lng
                dvg = rstd * (dvhat - jnp.mean(dvhat, axis=-1, keepdims=True)
                              - vhat * jnp.mean(dvhat * vhat, axis=-1, keepdims=True))
                dus = jnp.concatenate(sgu["du"], axis=1) * _gelu_grad(us)
                dvs = dvg * _gelu_grad(vs)
                dzs = jnp.concatenate(sgu["dzs"], axis=1)
                for k, val in enumerate((dus, dvs, dzs)):
                    dps_ref[:, k * SGU_W:(k + 1) * SGU_W] = val.astype(MXU_DTYPE)
                    gbins_ref[:, k * SGU_W:(k + 1) * SGU_W] += jnp.sum(val, axis=0, keepdims=True)

            valid = _window_mask(n)[0:BLOCK]
            low = lax.broadcasted_iota(jnp.int32, (BLOCK, LANES), 1) < HALF
            low_keys = lax.broadcasted_iota(jnp.int32, (2 * BLOCK, LANES), 1) < HALF
            lane_row = lax.broadcasted_iota(jnp.int32, (1, LANES), 1)
            gsink = jnp.zeros((1, LANES), F32)
            chains = [(g, par, i) for g in range(2) for par in range(2) for i in range(2)]
            kv = {(g, par): _kv_cat(kp_ref, kc_ref, 2 * g + par, False) for g in range(2) for par in range(2)}
            ones_keys = jnp.ones((2 * BLOCK, LANES), MXU_DTYPE)
            half_of_lane = lax.broadcasted_iota(jnp.int32, (LANES, 2 * LANES), 0) // HALF
            half_of_col = lax.broadcasted_iota(jnp.int32, (LANES, 2 * LANES), 1) // LANES
            sum_halves = (half_of_lane == half_of_col).astype(MXU_DTYPE)
            douts, deltas = [], []
            for pair in range(N_PAIRS):
                lanes = slice(pair * LANES, (pair + 1) * LANES)
                dg = d_ref[:, lanes]
                gate, gate_grad = _silu_and_grad(za_ref[:, lanes])
                o = o_ref[pair]
                dout = dg * gate
                dza = dg * o * gate_grad
                douts.append(dout.astype(MXU_DTYPE))
                deltas.append(_dot(dout * o, sum_halves))
                zl = slice(ATTN_W + pair * LANES, ATTN_W + (pair + 1) * LANES)
                pend_ref[:, zl] = dza.astype(MXU_DTYPE)
                gl = slice(GATE0 + pair * LANES, GATE0 + (pair + 1) * LANES)
                gbin_ref[:, gl] += jnp.sum(dza, axis=0, keepdims=True)

            first = {}

            def issue_first(k):
                g, par, i = chains[k]
                first[k] = (_dot(q_ref[2 * g + i], kv[g, par][0], NT), _dot(douts[2 * g + i], kv[g, par][1], NT))

            numerators = {}

            def issue_row_sums(k):
                g, par, i = chains[k]
                sink = sink_ref[4 * g + 2 * i + par]
                e, m = _softmax_numerator(jnp.where(valid, first[k][0], NEG_INF), sink)
                numerators[k] = (e, jnp.exp(sink - m), _dot(e, ones_keys))

            ahead = ATTN_BWD_AHEAD
            for k in range(ahead):
                issue_first(k)
            issue_row_sums(0)
            issue_row_sums(1)
            dqs, dk_parts, dv_parts = {}, {}, {}
            operands = {}

            def issue_last(k):
                g, par, i = chains[k]
                ds, ds_t, p_t = operands.pop(k)
                dq = _dot(ds, kv[g, par][0])
                dqs[g, i] = dq if par == 0 else dqs[g, i] + dq
                dk = _dot(ds_t, q_ref[2 * g + i])
                dv = _dot(p_t, douts[2 * g + i])
                dk_parts[g, par] = dk if i == 0 else dk_parts[g, par] + dk
                dv_parts[g, par] = dv if i == 0 else dv_parts[g, par] + dv

            for k, (g, par, i) in enumerate(chains):
                h = 4 * g + 2 * i + par
                delta = deltas[2 * g + i][:, par * LANES:(par + 1) * LANES]
                e, at_sink, row_sum = numerators[k]
                inv = 1.0 / (row_sum + at_sink)
                p = e * jnp.tile(inv, (1, 2))
                ds = p * (first[k][1] - jnp.tile(delta, (1, 2)))
                ds = ds.astype(MXU_DTYPE)
                operands[k] = (ds, ds.T, p.astype(MXU_DTYPE).T)
                total = jnp.sum(at_sink * inv * delta, axis=0, keepdims=True)
                gsink = jnp.where(lane_row == h, -total, gsink)
                if k + ahead < len(chains):
                    issue_first(k + ahead)
                if k + 2 < len(chains):
                    issue_row_sums(k + 2)
                if k > 0:
                    issue_last(k - 1)
                if k == SGU_GATES_AFTER_CHAIN:
                    sgu_gates()
                if k == SGU_GRADS_AFTER_CHAIN:
                    sgu_grads()
            issue_last(len(chains) - 1)
            for pair in range(N_PAIRS):
                g, i = divmod(pair, 2)
                dq = dqs[g, i] * SCALE
                lanes = slice(pair * LANES, (pair + 1) * LANES)
                pend_ref[:, lanes] = dq.astype(MXU_DTYPE)
                gbin_ref[:, lanes] += jnp.sum(dq, axis=0, keepdims=True)
            gsink_ref[...] += gsink
            for k, parts in enumerate((dk_parts, dv_parts)):
                masked = {key: jnp.where(low_keys if key[1] == 0 else jnp.logical_not(low_keys), val, 0.0)
                          for key, val in parts.items()}
                both = (masked[0, 0] + masked[1, 1]
                        + pltpu.roll(masked[0, 1] + masked[1, 0], HALF, 1))
                lanes = slice(k * KV_W, (k + 1) * KV_W)
                done = carry_ref[:, lanes] + both[0:BLOCK]
                dp_ref[:, KV0 + k * KV_W:KV0 + (k + 1) * KV_W] = done.astype(MXU_DTYPE)
                carry_ref[:, lanes] = both[BLOCK:]
                gbin_ref[:, KV0 + k * KV_W:KV0 + (k + 1) * KV_W] += jnp.sum(both, axis=0, keepdims=True)
            kp_ref[...] = kc_ref[...]

        @pl.when(n == last)
        def _():
            for hh in range(N_SGU_HEADS):
                gw_ref[hh] = jnp.where(tril, gw_ref[hh], 0.0)
            head_of_lane = lax.broadcasted_iota(jnp.int32, (N_SGU_HEADS, SGU_W), 1) // HEAD_DIM
            select = (head_of_lane == lax.broadcasted_iota(jnp.int32, (N_SGU_HEADS, SGU_W), 0)).astype(F32)
            gb_ref[...] = lax.dot_general(select, gbias_ref[...], NT, precision=lax.Precision.HIGHEST,
                                          preferred_element_type=F32)

    at = lambda n: jnp.minimum(n, last)
    blk = lambda w: pl.BlockSpec((BLOCK, w), lambda n: (at(n), 0))
    tiles = pl.BlockSpec((N_PAIRS, BLOCK, LANES), lambda n: (0, at(n), 0))
    section = lambda k: pl.BlockSpec((None, BLOCK, SGU_W), lambda n: (k, at(n), 0))
    return pl.pallas_call(
        body,
        name="mixers_bwd",
        grid=(N_BLOCKS + 1,),
        in_specs=[pl.BlockSpec(memory_space=pltpu.SMEM),
                  section(0),
                  tiles,
                  blk(KVX_W),
                  tiles,
                  section(0),
                  section(1),
                  section(1), section(2), section(3),
                  _full((1, SGU_W)), _full((1, SGU_W)), _full((N_SGU_HEADS, BLOCK, BLOCK)), _full((BLOCK, SGU_W)),
                  VMEM_SPEC],
        out_specs=(pl.BlockSpec((BLOCK, ATTN_SECTION), lambda n: (jnp.maximum(n - 1, 0), 0)),
                   _full((1, LANES)), _full((1, ATTN_SECTION)),
                   pl.BlockSpec((BLOCK, SGU_SECTION), lambda n: (at(n), 0)),
                   _full((N_SGU_HEADS, BLOCK, BLOCK)), _full((N_SGU_HEADS, BLOCK)),
                   _full((8, SGU_W)), _full((1, SGU_SECTION)), VMEM_SPEC),
        out_shape=(jax.ShapeDtypeStruct((SEQ, ATTN_SECTION), MXU_DTYPE),
                   jax.ShapeDtypeStruct((1, LANES), F32),
                   jax.ShapeDtypeStruct((1, ATTN_SECTION), F32),
                   jax.ShapeDtypeStruct((SEQ, SGU_SECTION), MXU_DTYPE),
                   jax.ShapeDtypeStruct((N_SGU_HEADS, BLOCK, BLOCK), F32),
                   jax.ShapeDtypeStruct((N_SGU_HEADS, BLOCK), F32),
                   jax.ShapeDtypeStruct((8, SGU_W), F32),
                   jax.ShapeDtypeStruct((1, SGU_SECTION), F32),
                   jax.ShapeDtypeStruct((WOUT_ROWS, D_MODEL), F32)),
        scratch_shapes=([pltpu.VMEM((BLOCK, KVX_W), MXU_DTYPE),
                         pltpu.VMEM((BLOCK, 2 * ATTN_W), MXU_DTYPE), pltpu.VMEM((BLOCK, 2 * KV_W), F32),
                         pltpu.VMEM((N_SGU_HEADS, BLOCK, BLOCK), MXU_DTYPE),
                         pltpu.VMEM((N_SGU_HEADS, BLOCK, BLOCK), MXU_DTYPE), pltpu.VMEM((BLOCK, SGU_W), F32)]
                        + _reduce_scatter_scratch(WOUT_ROWS, D_MODEL, COMM_DTYPE) + _dma_sems(REDUCE_SEMS)),
        compiler_params=_params(("arbitrary",), VMEM_LIMIT),
    )(sinks, dmix, q, kvx, out, gates, dmix, gates, gates, gates, ln_g, ln_b, sgu_w, bias_full, gwout)


def _in_proj_bwd(dpa, dps, win_t, x, norm_g, gres, gwin_own, gwin_others, vec_parts):
    tm = TOKEN_TILE
    steps = SEQ // tm
    n_parts = len(vec_parts)

    def body(da_ref, ds_ref, w_ref, x_ref, g_ref, gres_ref, own_ref, others_ref, *rest):
        part_refs = rest[:n_parts]
        gx_ref, shard_ref, vec_out_ref, gng_ref, rc, vec_ref, ra_vec, slots, send_sems, recv_sems = rest[n_parts:]
        step = pl.program_id(0)
        copies = _Copies(send_sems, recv_sems)
        send, finish = _owner_sums_plan(copies, 0, own_ref, others_ref, rc, shard_ref)

        @pl.when(step == 0)
        def _():
            gng_ref[...] = jnp.zeros_like(gng_ref)
            send()

        dh = _dot(da_ref[...], w_ref[0:ATTN_SECTION, :]) + _dot(ds_ref[...], w_ref[ATTN_SECTION:, :])
        xv = x_ref[...]
        r = lax.rsqrt(jnp.mean(xv * xv, axis=-1, keepdims=True) + NORM_EPS)
        xn = xv * r
        gng_ref[...] += jnp.sum(dh * xn, axis=0, keepdims=True)
        dxn = dh * g_ref[...]
        gx_ref[...] = r * (dxn - xn * jnp.mean(dxn * xn, axis=-1, keepdims=True)) + gres_ref[...]

        @pl.when(step == steps - 1)
        def _():
            finish()
            _all_reduce_vectors(copies, OWNER_SEMS, gng_ref, *part_refs, vec_out_ref, vec_ref, ra_vec, slots)

    tile = lambda w: pl.BlockSpec((tm, w), lambda i: (i, 0))
    return pl.pallas_call(
        body,
        name="in_proj_bwd",
        grid=(steps,),
        in_specs=[tile(ATTN_SECTION), tile(SGU_SECTION), _full((IN_W, D_MODEL)), tile(D_MODEL),
                  _full((1, D_MODEL)), tile(D_MODEL), VMEM_SPEC,
                  pl.BlockSpec(memory_space=pl.ANY)]
                 + [VMEM_SPEC] * n_parts,
        out_specs=(tile(D_MODEL), VMEM_SPEC, VMEM_SPEC),
        out_shape=(jax.ShapeDtypeStruct((SEQ, D_MODEL), F32),
                   jax.ShapeDtypeStruct((WIN_ROWS, D_MODEL), F32),
                   jax.ShapeDtypeStruct((VEC_ROWS, IN_W), F32)),
        scratch_shapes=([pltpu.VMEM((1, D_MODEL), F32), pltpu.VMEM((3 * WIN_ROWS, D_MODEL), COMM_DTYPE)]
                        + _vector_scratch() + _dma_sems(OWNER_SEMS + VECTOR_SEMS)),
        input_output_aliases={5: 0},
        compiler_params=_params(("arbitrary",), VMEM_LIMIT),
    )(dpa, dps, win_t, x, norm_g, gres, gwin_own, gwin_others, *vec_parts)


def _win_grad_pieces(rows):
    pieces = []
    for step in range(IN_W // rows):
        for owner in range(N_DEV):
            lo, hi = max(step * rows, owner * WIN_ROWS), min((step + 1) * rows, (owner + 1) * WIN_ROWS)
            if lo < hi:
                pieces.append((len(pieces), step, owner, lo, hi - lo))
    return pieces


def _win_grad(dpa, dps, h, gsguw):
    rows = 256
    n_attn = ATTN_SECTION // rows
    steps = IN_W // rows
    pieces = _win_grad_pieces(rows)
    class_rows = (N_DEV // 2) * WIN_ROWS

    def body(da_ref, ds_ref, h_ref, gsguw_ref, own_ref, others_ref, sguw_full_ref,
             chunks, sa, ra, sa_s, ra_s, sb_s, rc_s, landing, send_sems, recv_sems, give_sems, take_sems):
        step = pl.program_id(0)
        x, y, c = _place()
        copies = _Copies(send_sems, recv_sems)
        own_sguw = landing.at[_block_rows((x, y, c), SGUW_ROWS), :]
        start, exchange, finish = _reduce_scatter_plan(copies, 0, gsguw_ref, SGUW_ROWS, sa_s, ra_s, sb_s, rc_s,
                                                       own_sguw)
        gather = _gather_plan(copies, REDUCE_SEMS, landing, SGUW_ROWS)

        def class_rows_of(owner, first, n):
            return pl.ds((owner // 2) * WIN_ROWS + first - owner * WIN_ROWS, n)

        def to_sibling(piece):
            k, _, owner, first, n = piece
            at = class_rows_of(owner, first, n)
            return pltpu.make_async_remote_copy(src_ref=sa.at[at, :], dst_ref=ra.at[at, :], send_sem=give_sems.at[k],
                                                recv_sem=take_sems.at[k], device_id=(x, y, 1 - c), device_id_type=MESH)

        def give(piece):
            k, at_step, owner, first, n = piece

            @pl.when(c != owner % 2)
            def _():
                sa[class_rows_of(owner, first, n), :] = chunks[at_step % 2, pl.ds(first % rows, n), :].astype(sa.dtype)
                to_sibling(piece).start()

        def keep(piece):
            k, at_step, owner, first, n = piece
            px, py = owner // 4, (owner // 2) % 2

            @pl.when(c == owner % 2)
            def _():
                to_sibling(piece).wait_recv()
                total = (chunks[at_step % 2, pl.ds(first % rows, n), :]
                         + ra[class_rows_of(owner, first, n), :].astype(F32))
                relation = (x + px - 2 * x * px) + 2 * (y + py - 2 * y * py)

                @pl.when(relation == 0)
                def _():
                    own_ref[pl.ds(first - owner * WIN_ROWS, n), :] = total

                @pl.when(relation != 0)
                def _():
                    at = pl.multiple_of((relation - 1) * WIN_ROWS + first - owner * WIN_ROWS, 16)
                    others_ref[pl.ds(at, n), :] = total.astype(others_ref.dtype)

        pl.when(step == 0)(start)
        pl.when(step == 2)(exchange)

        @pl.when(step == 5)
        def _():
            finish()
            gather[0]()

        pl.when(step == 7)(gather[1])

        @pl.when(step < n_attn)
        def _():
            chunks[step % 2] = _dot(da_ref[...], h_ref[...], TN)

        @pl.when(step >= n_attn)
        def _():
            chunks[step % 2] = _dot(ds_ref[...], h_ref[...], TN)

        for at_step in range(steps):
            @pl.when(step == at_step)
            def _():
                for piece in pieces:
                    if piece[1] == at_step:
                        give(piece)
                    if piece[1] == at_step - 1:
                        keep(piece)

        @pl.when(step == steps - 1)
        def _():
            for piece in pieces:
                if piece[1] == steps - 1:
                    keep(piece)
            for piece in pieces:
                pl.when(c != piece[2] % 2)(to_sibling(piece).wait_send)
            gather[2]()
            sguw_full_ref[...] = landing[...]

    return pl.pallas_call(
        body,
        name="win_grad",
        grid=(steps,),
        in_specs=[pl.BlockSpec((SEQ, rows), lambda i: (0, jnp.minimum(i, n_attn - 1))),
                  pl.BlockSpec((SEQ, rows), lambda i: (0, jnp.maximum(i - n_attn, 0))),
                  _full((SEQ, D_MODEL)), VMEM_SPEC],
        out_specs=(VMEM_SPEC, VMEM_SPEC, _full((N_SGU_HEADS * BLOCK, BLOCK))),
        out_shape=(jax.ShapeDtypeStruct((WIN_ROWS, D_MODEL), F32),
                   jax.ShapeDtypeStruct((3 * WIN_ROWS, D_MODEL), COMM_DTYPE),
                   jax.ShapeDtypeStruct((N_SGU_HEADS * BLOCK, BLOCK), F32)),
        scratch_shapes=([pltpu.VMEM((2, rows, D_MODEL), F32),
                         pltpu.VMEM((class_rows, D_MODEL), COMM_DTYPE), pltpu.VMEM((class_rows, D_MODEL), COMM_DTYPE)]
                        + _reduce_scatter_scratch(SGUW_ROWS, BLOCK, F32)
                        + [pltpu.VMEM((N_SGU_HEADS * BLOCK, BLOCK), F32)]
                        + _dma_sems(REDUCE_SEMS + GATHER_SEMS) + _dma_sems(len(pieces))),
        compiler_params=_params(("arbitrary",), VMEM_LIMIT),
    )(dpa, dps, h, gsguw)


VEC_NORM_G, VEC_B_IN, VEC_SINKS, VEC_LN_G, VEC_LN_B, VEC_B_OUT, VEC_FINAL_G, VEC_LOSS, VEC_SGU_B = 0, 1, 2, 3, 4, 5, 6, 7, 8


def _adamw(w, g, m, v):
    m = ADAM_B1 * m + (1.0 - ADAM_B1) * g
    v = ADAM_B2 * v + (1.0 - ADAM_B2) * (g * g)
    m_hat = m / (1.0 - ADAM_B1 ** ADAM_STEP)
    v_hat = v / (1.0 - ADAM_B2 ** ADAM_STEP)
    delta = -ADAM_LR * (m_hat / (jnp.sqrt(v_hat) + ADAM_EPS) + ADAM_WD * w)
    return delta, m, v


def _adamw_shard(name, g, w, m, v, block_rows):
    def body(g_ref, w_ref, m_ref, v_ref, d_ref, nm_ref, nv_ref):
        d_ref[...], nm_ref[...], nv_ref[...] = _adamw(w_ref[...], g_ref[...], m_ref[...], v_ref[...])

    rows, cols = w.shape
    spec = pl.BlockSpec((block_rows, cols), lambda i: (i, 0))
    return pl.pallas_call(
        body,
        name=name,
        grid=(rows // block_rows,),
        in_specs=[spec] * 4,
        out_specs=(spec,) * 3,
        out_shape=(jax.ShapeDtypeStruct(w.shape, F32),) * 3,
        compiler_params=_params(("arbitrary",)),
    )(g, w, m, v)


VECTOR_SEMS = 4


def _vector_scratch():
    return [pltpu.VMEM((VEC_ROWS, IN_W), F32), pltpu.VMEM((VEC_ROWS, IN_W), F32),
            pltpu.VMEM((4 * VEC_ROWS, IN_W), F32)]


def _all_reduce_vectors(copies, sem0, gng_ref, gba_ref, gbs_ref, gsink_ref, gln_ref, gsgub_ref, vec4_ref, out_ref,
                        vec_ref, ra_vec, slots):
    x, y, c = _place()
    vec_ref[...] = jnp.zeros_like(vec_ref)
    vec_ref[VEC_NORM_G:VEC_NORM_G + 1, 0:D_MODEL] = gng_ref[...]
    vec_ref[VEC_B_IN:VEC_B_IN + 1, 0:ATTN_SECTION] = gba_ref[...]
    vec_ref[VEC_B_IN:VEC_B_IN + 1, ATTN_SECTION:IN_W] = gbs_ref[...]
    vec_ref[VEC_SINKS:VEC_SINKS + 1, 0:LANES] = gsink_ref[...]
    vec_ref[VEC_LN_G:VEC_LN_G + 1, 0:SGU_W] = gln_ref[0:1, :]
    vec_ref[VEC_LN_B:VEC_LN_B + 1, 0:SGU_W] = gln_ref[1:2, :]
    vec_ref[VEC_B_OUT:VEC_B_OUT + 1, 0:D_MODEL] = vec4_ref[2:3, :]
    vec_ref[VEC_FINAL_G:VEC_FINAL_G + 1, 0:D_MODEL] = vec4_ref[1:2, :]
    vec_ref[VEC_LOSS:VEC_LOSS + 1, 0:D_MODEL] = vec4_ref[0:1, :]
    vec_ref[VEC_SGU_B:VEC_SGU_B + N_SGU_HEADS, 0:BLOCK] = gsgub_ref[...]

    to_sibling = copies(sem0, vec_ref, ra_vec, (x, y, 1 - c))
    to_sibling.start()
    to_sibling.wait_recv()

    def chip_slot(place):
        return slots.at[pl.ds(pl.multiple_of((2 * place[0] + place[1]) * VEC_ROWS, 8), VEC_ROWS), :]

    mine = chip_slot((x, y))
    mine[...] = vec_ref[...] + ra_vec[...]
    to_chips = [copies(sem0 + i, mine, mine, (*_chip(rel), c)) for i, rel in enumerate(RELATIONS[1:], start=1)]
    for cp in to_chips:
        cp.start()
    for i, rel in enumerate(RELATIONS[1:], start=1):
        theirs = chip_slot(_chip(rel))
        copies(sem0 + i, theirs, theirs, (x, y, c)).wait_recv()
    out_ref[...] = ((slots[0:VEC_ROWS, :] + slots[VEC_ROWS:2 * VEC_ROWS, :])
                    + slots[2 * VEC_ROWS:3 * VEC_ROWS, :]) + slots[3 * VEC_ROWS:, :]
    to_sibling.wait_send()
    for cp in to_chips:
        cp.wait_send()


def _adamw_replicated(vec, gsguw, weights, m_state, v_state):
    n = len(SMALL)

    def body(*refs):
        vec_ref, gsguw_ref = refs[0], refs[1]
        w_refs, m_refs, v_refs = (refs[2 + k * n:2 + (k + 1) * n] for k in range(3))
        outs = refs[2 + 3 * n:]
        g_refs, d_refs, nm_refs, nv_refs = (outs[k * n:(k + 1) * n] for k in range(4))
        for i, (_, row, shape) in enumerate(SMALL):
            g = gsguw_ref[...] if row is None else vec_ref[row:row + shape[0], 0:shape[1]]
            g_refs[i][...] = g
            d_refs[i][...], nm_refs[i][...], nv_refs[i][...] = _adamw(
                w_refs[i][...], g, m_refs[i][...], v_refs[i][...])

    shapes = tuple(jax.ShapeDtypeStruct(shape, F32) for _, _, shape in SMALL)
    outs = pl.pallas_call(
        body,
        name="adamw_replicated",
        in_specs=[VMEM_SPEC] * (2 + 3 * n),
        out_specs=(VMEM_SPEC,) * (4 * n),
        out_shape=shapes * 4,
    )(vec, gsguw, *weights, *m_state, *v_state)
    return tuple(outs[k * n:(k + 1) * n] for k in range(4))


SMALL = (
    ("norm_g", VEC_NORM_G, (1, D_MODEL)),
    ("b_in", VEC_B_IN, (1, IN_W)),
    ("attn_sinks", VEC_SINKS, (1, N_Q_HEADS)),
    ("sgu_ln_g", VEC_LN_G, (1, SGU_W)),
    ("sgu_ln_b", VEC_LN_B, (1, SGU_W)),
    ("sgu_w", None, (N_SGU_HEADS * BLOCK, BLOCK)),
    ("sgu_b", VEC_SGU_B, (N_SGU_HEADS, BLOCK)),
    ("b_out", VEC_B_OUT, (1, D_MODEL)),
    ("final_norm_g", VEC_FINAL_G, (1, D_MODEL)),
)


def _local_grads(x, target, h, win_t, wout_shard, norm_g, b_in, attn_sinks, sgu_ln_g, sgu_ln_b, sgu_w, sgu_b, b_out,
                 final_g):
    sinks = attn_sinks.reshape(N_Q_HEADS)
    bias_full = jnp.repeat(sgu_b.T, HEAD_DIM, axis=1)
    q, kvx, gates, wout = _in_proj(h, b_in, win_t, wout_shard)
    out, gres, dmix, gwout, vec4 = _mixers_out_proj(sinks, q, kvx, gates, sgu_ln_g, sgu_ln_b, sgu_w, bias_full,
                                                    x, target, wout, b_out, final_g)
    dpa, gsink, gbin_a, dps, gsguw, gsgub, gln, gbin_s, gwout_shard = _mixers_bwd(
        sinks, dmix, q, kvx, out, gates, sgu_ln_g, sgu_ln_b, sgu_w, bias_full, gwout)
    gwin_own, gwin_others, gsguw_sum = _win_grad(dpa, dps, h, gsguw.reshape(N_SGU_HEADS * BLOCK, BLOCK))
    grad_x, gwin_shard, vec = _in_proj_bwd(dpa, dps, win_t, x, norm_g, gres, gwin_own, gwin_others,
                                           (gbin_a, gbin_s, gsink, gln, gsgub, vec4))
    return grad_x, gwin_shard, gwout_shard, gsguw_sum, vec


def kernel(x, norm_g, w_in, b_in, attn_sinks, sgu_ln_g, sgu_ln_b, sgu_w, sgu_b, w_out, b_out, final_norm_g, loss_target, m_norm_g, m_w_in, m_b_in, m_attn_sinks, m_sgu_ln_g, m_sgu_ln_b, m_sgu_w, m_sgu_b, m_w_out, m_b_out, m_final_norm_g, v_norm_g, v_w_in, v_b_in, v_attn_sinks, v_sgu_ln_g, v_sgu_ln_b, v_sgu_w, v_sgu_b, v_w_out, v_b_out, v_final_norm_g):
    given = dict(norm_g=norm_g, b_in=b_in, attn_sinks=attn_sinks, sgu_ln_g=sgu_ln_g, sgu_ln_b=sgu_ln_b,
                 sgu_w=sgu_w, sgu_b=sgu_b, b_out=b_out, final_norm_g=final_norm_g)
    m_given = dict(norm_g=m_norm_g, b_in=m_b_in, attn_sinks=m_attn_sinks, sgu_ln_g=m_sgu_ln_g,
                   sgu_ln_b=m_sgu_ln_b, sgu_w=m_sgu_w, sgu_b=m_sgu_b, b_out=m_b_out, final_norm_g=m_final_norm_g)
    v_given = dict(norm_g=v_norm_g, b_in=v_b_in, attn_sinks=v_attn_sinks, sgu_ln_g=v_sgu_ln_g,
                   sgu_ln_b=v_sgu_ln_b, sgu_w=v_sgu_w, sgu_b=v_sgu_b, b_out=v_b_out, final_norm_g=v_final_norm_g)

    win_t, h = _all_gather_win(w_in[0].T, x[0], norm_g)
    grad_x, gwin_t, gwout, gsguw, vec = _local_grads(
        x[0], loss_target[0], h, win_t, w_out[0], norm_g, b_in, attn_sinks, sgu_ln_g, sgu_ln_b, sgu_w[0], sgu_b[0],
        b_out, final_norm_g.reshape(1, D_MODEL))

    t = lambda a: a[0].T
    d_win, nm_win, nv_win = _adamw_shard("adamw_w_in", gwin_t, t(w_in), t(m_w_in), t(v_w_in), WIN_ROWS // 2)
    d_wout, nm_wout, nv_wout = _adamw_shard("adamw_w_out", gwout, w_out[0], m_w_out[0], v_w_out[0], WOUT_ROWS)
    as_2d = lambda d: [d[name].reshape(shape) for name, _, shape in SMALL]
    loss = vec[VEC_LOSS, 0]
    small = _adamw_replicated(vec, gsguw, as_2d(given), as_2d(m_given), as_2d(v_given))

    def assemble(big_in, big_out, k):
        vals = {name: small[k][i].reshape(given[name].shape) for i, (name, _, _) in enumerate(SMALL)}
        vals["w_in"] = big_in.T[None]
        vals["w_out"] = big_out[None]
        order = ("norm_g", "w_in", "b_in", "attn_sinks", "sgu_ln_g", "sgu_ln_b", "sgu_w", "sgu_b", "w_out",
                 "b_out", "final_norm_g")
        return [vals[name] for name in order]

    return (loss, grad_x[None],
            *assemble(gwin_t, gwout, 0), *assemble(d_win, d_wout, 1),
            *assemble(nm_win, nm_wout, 2), *assemble(nv_win, nv_wout, 3))
```

```python
import functools
import math

import jax
import jax.numpy as jnp
from jax import lax
from jax.experimental import pallas as pl
from jax.experimental.pallas import tpu as pltpu

F32 = jnp.float32
BF16 = jnp.bfloat16
MXU_DTYPE = BF16
COMM_DTYPE = BF16

D_MODEL = 1024
SEQ = 4096
HEAD_DIM = 64
N_Q_HEADS = 8
Q_PER_KV = 4
BLOCK = 128
N_BLOCKS = SEQ // BLOCK
ATTN_W = 512
KV_W = 128
SGU_W = 512
N_SGU_HEADS = 8
IN_W = 2816
NORM_EPS = 1e-5
NEG_INF = -1e30
SCALE = HEAD_DIM ** -0.5
KV0 = ATTN_W
GATE0 = ATTN_W + 2 * KV_W
SGU0 = GATE0 + ATTN_W
ATTN_SECTION = SGU0
SGU_SECTION = IN_W - SGU0

ADAM_LR = 0.001
ADAM_B1 = 0.9
ADAM_B2 = 0.999
ADAM_EPS = 1e-08
ADAM_WD = 0.01
ADAM_STEP = 10

N_DEV = 8
WIN_ROWS = IN_W // N_DEV
WOUT_ROWS = D_MODEL // N_DEV
SGUW_ROWS = N_SGU_HEADS * BLOCK // N_DEV
VEC_ROWS = 16
MESH = pl.DeviceIdType.MESH

LANES = 128
HALF = LANES // 2
N_PAIRS = N_Q_HEADS * HEAD_DIM // LANES
KVX_W = 12 * LANES
TOKEN_TILE = 512
FWD_TOKEN_TILE = 512
ATTN_FWD_AHEAD = 4
FUSED_BLOCKS = 2
SGU_MIX_AFTER_CHAIN = 0
SGU_GATES_AFTER_CHAIN = 1
SGU_GRADS_AFTER_CHAIN = 5
ATTN_BWD_AHEAD = 3
VMEM_LIMIT = 56 * 1024 * 1024

NN = (((1,), (0,)), ((), ()))
NT = (((1,), (1,)), ((), ()))
TN = (((0,), (0,)), ((), ()))


def _dot(a, b, dims=NN):
    return lax.dot_general(a.astype(MXU_DTYPE), b.astype(MXU_DTYPE), dims, preferred_element_type=F32)


def _gelu(x):
    return x * (lax.erf(x * (1.0 / math.sqrt(2.0))) + 1.0) * 0.5


def _gelu_grad(x):
    cdf = (lax.erf(x * (1.0 / math.sqrt(2.0))) + 1.0) * 0.5
    return cdf + x * jnp.exp(-0.5 * x * x) * (1.0 / math.sqrt(2.0 * math.pi))


def _silu_and_grad(z):
    s = jax.nn.sigmoid(z)
    return z * s, s * (1.0 + z * (1.0 - s))


def _params(semantics=None, vmem=None):
    kw = {}
    if semantics is not None:
        kw["dimension_semantics"] = semantics
    if vmem is not None:
        kw["vmem_limit_bytes"] = vmem
    return pltpu.CompilerParams(**kw)


def _full(shape):
    return pl.BlockSpec(shape, lambda *_: (0,) * len(shape))


VMEM_SPEC = pl.BlockSpec(memory_space=pltpu.VMEM)


RELATIONS = ((0, 0), (1, 0), (0, 1), (1, 1))


def _place():
    return lax.axis_index("x"), lax.axis_index("y"), lax.axis_index("c")


def _chip(rel):
    x, y, _ = _place()
    return (1 - x if rel[0] else x, 1 - y if rel[1] else y)


def _block_rows(place, n_rows):
    px, py, pc = place
    return pl.ds(pl.multiple_of((4 * px + 2 * py + pc) * n_rows, 16), n_rows)


class _Copies:
    def __init__(self, send_sems, recv_sems):
        self.send_sems, self.recv_sems = send_sems, recv_sems

    def __call__(self, k, src, dst, to):
        return pltpu.make_async_remote_copy(src_ref=src, dst_ref=dst, send_sem=self.send_sems.at[k],
                                            recv_sem=self.recv_sems.at[k], device_id=to, device_id_type=MESH)


def _gather_plan(copies, sem0, full_ref, n_rows):
    x, y, c = _place()
    me, sibling = (x, y, c), (x, y, 1 - c)
    chips = [_chip(rel) for rel in RELATIONS[1:]]

    def cp(k, block, to):
        rows = full_ref.at[_block_rows(block, n_rows), :]
        return copies(sem0 + k, rows, rows, to)

    first = [cp(0, me, sibling)] + [cp(1 + j, me, (*chip, c)) for j, chip in enumerate(chips)]
    passed = [cp(4 + j, (*chip, c), sibling) for j, chip in enumerate(chips)]

    def start():
        for f in first:
            f.start()

    def forward():
        for j, chip in enumerate(chips):
            cp(1 + j, (*chip, c), me).wait_recv()
            passed[j].start()

    def finish():
        cp(0, sibling, me).wait_recv()
        for j, chip in enumerate(chips):
            cp(4 + j, (*chip, 1 - c), me).wait_recv()
        for f in first + passed:
            f.wait_send()

    return start, forward, finish


GATHER_SEMS = 7


def _reduce_scatter_plan(copies, sem0, part_ref, n_rows, sa, ra, sb, rc, res_ref):
    x, y, c = _place()
    sibling = (x, y, 1 - c)
    n = n_rows
    level1 = copies(sem0, sa, ra, sibling)

    def level2(i):
        slot = pl.ds((i - 1) * n, n)
        return copies(sem0 + i, sb.at[slot, :], rc.at[slot, :], (*_chip(RELATIONS[i]), c))

    def start():
        for i, rel in enumerate(RELATIONS):
            sa[i * n:(i + 1) * n, :] = part_ref[_block_rows((*_chip(rel), 1 - c), n), :].astype(sa.dtype)
        level1.start()

    def exchange():
        level1.wait_recv()
        for i, rel in enumerate(RELATIONS):
            total = part_ref[_block_rows((*_chip(rel), c), n), :] + ra[i * n:(i + 1) * n, :].astype(F32)
            if i == 0:
                res_ref[...] = total
            else:
                sb[(i - 1) * n:i * n, :] = total.astype(sb.dtype)
                level2(i).start()

    def finish():
        acc = res_ref[...]
        for i in range(1, len(RELATIONS)):
            level2(i).wait_recv()
            acc = acc + rc[(i - 1) * n:i * n, :].astype(F32)
        res_ref[...] = acc
        level1.wait_send()
        for i in range(1, len(RELATIONS)):
            level2(i).wait_send()

    return start, exchange, finish


REDUCE_SEMS = 4


def _owner_sums_plan(copies, sem0, own_ref, sb, rc, res_ref):
    _, _, c = _place()
    n = own_ref.shape[0]

    def level2(i):
        slot = pl.ds((i - 1) * n, n)
        return copies(sem0 + i - 1, sb.at[slot, :], rc.at[slot, :], (*_chip(RELATIONS[i]), c))

    def send():
        for i in reversed(range(1, len(RELATIONS))):
            level2(i).start()

    def finish():
        acc = own_ref[...]
        for i in range(1, len(RELATIONS)):
            level2(i).wait_recv()
            acc = acc + rc[(i - 1) * n:i * n, :].astype(F32)
        res_ref[...] = acc
        for i in range(1, len(RELATIONS)):
            level2(i).wait_send()

    return send, finish


OWNER_SEMS = 3


def _reduce_scatter_scratch(n_rows, width, dtype):
    return [pltpu.VMEM((4 * n_rows, width), dtype), pltpu.VMEM((4 * n_rows, width), dtype),
            pltpu.VMEM((3 * n_rows, width), dtype), pltpu.VMEM((3 * n_rows, width), dtype)]


def _dma_sems(n):
    return [pltpu.SemaphoreType.DMA((n,)), pltpu.SemaphoreType.DMA((n,))]


def _all_gather_win(win_t_shard, x, norm_g):
    tm = FWD_TOKEN_TILE
    steps = SEQ // tm

    def body(win_ref, x_ref, g_ref, full_ref, h_ref, landing, send_sems, recv_sems):
        step = pl.program_id(0)
        start, forward, finish = _gather_plan(_Copies(send_sems, recv_sems), 0, landing, WIN_ROWS)

        @pl.when(step == 0)
        def _():
            landing[_block_rows(_place(), WIN_ROWS), :] = win_ref[...].astype(COMM_DTYPE)
            start()

        xv = x_ref[...]
        r = lax.rsqrt(jnp.mean(xv * xv, axis=-1, keepdims=True) + NORM_EPS)
        h_ref[...] = ((xv * r) * g_ref[...]).astype(MXU_DTYPE)

        @pl.when(step == steps - 1)
        def _():
            forward()
            finish()
            full_ref[...] = landing[...]

    return pl.pallas_call(
        body,
        name="all_gather_win",
        grid=(steps,),
        in_specs=[VMEM_SPEC, pl.BlockSpec((tm, D_MODEL), lambda i: (i, 0)), _full((1, D_MODEL))],
        out_specs=(_full((IN_W, D_MODEL)), pl.BlockSpec((tm, D_MODEL), lambda i: (i, 0))),
        out_shape=(jax.ShapeDtypeStruct((IN_W, D_MODEL), COMM_DTYPE),
                   jax.ShapeDtypeStruct((SEQ, D_MODEL), MXU_DTYPE)),
        scratch_shapes=[pltpu.VMEM((IN_W, D_MODEL), COMM_DTYPE)] + _dma_sems(GATHER_SEMS),
        compiler_params=_params(("arbitrary",), VMEM_LIMIT),
    )(win_t_shard, x, norm_g)


def _in_proj(h, b_in, win_t, wout_shard):
    tm = FWD_TOKEN_TILE
    steps = SEQ // tm

    def body(h_ref, b_ref, w_ref, wout_ref, q_ref, kvx_ref, gate_ref, wfull_ref, landing, send_sems, recv_sems):
        step = pl.program_id(0)
        start, forward, finish = _gather_plan(_Copies(send_sems, recv_sems), 0, landing, WOUT_ROWS)

        @pl.when(step == 0)
        def _():
            landing[_block_rows(_place(), WOUT_ROWS), :] = wout_ref[...].astype(COMM_DTYPE)
            start()

        pl.when(step == steps // 2)(forward)

        h = h_ref[...]

        def proj(lo, hi):
            return _dot(h, w_ref[lo:hi, :], NT) + b_ref[:, lo:hi]

        qs = proj(0, ATTN_W) * SCALE
        for pair in range(N_PAIRS):
            q_ref[pair] = qs[:, pair * LANES:(pair + 1) * LANES].astype(MXU_DTYPE)
        kv = proj(KV0, GATE0)
        low = lax.broadcasted_iota(jnp.int32, (tm, LANES), 1) < HALF
        for i in range(2):
            t = kv[:, i * LANES:(i + 1) * LANES]
            rot = pltpu.roll(t, HALF, 1)
            variants = (jnp.where(low, t, 0.0), jnp.where(low, 0.0, rot),
                        jnp.where(low, rot, 0.0), jnp.where(low, 0.0, t))
            for j, val in enumerate(variants):
                col = (4 * i + j) * LANES
                kvx_ref[:, col:col + LANES] = val.astype(MXU_DTYPE)
                if i == 1:
                    ones_elsewhere = jnp.where(low == (j % 2 == 0), val, 1.0)
                    kvx_ref[:, col + 4 * LANES:col + 5 * LANES] = ones_elsewhere.astype(MXU_DTYPE)
        for k in range(4):
            gate_ref[k] = proj(GATE0 + k * SGU_W, GATE0 + (k + 1) * SGU_W)

        @pl.when(step == steps - 1)
        def _():
            finish()
            wfull_ref[...] = landing[...]

    return pl.pallas_call(
        body,
        name="in_proj",
        grid=(steps,),
        in_specs=[pl.BlockSpec((tm, D_MODEL), lambda i: (i, 0)),
                  _full((1, IN_W)), _full((IN_W, D_MODEL)), VMEM_SPEC],
        out_specs=(pl.BlockSpec((N_PAIRS, tm, LANES), lambda i: (0, i, 0)),
                   pl.BlockSpec((tm, KVX_W), lambda i: (i, 0)),
                   pl.BlockSpec((4, tm, SGU_W), lambda i: (0, i, 0)),
                   _full((D_MODEL, D_MODEL))),
        out_shape=(jax.ShapeDtypeStruct((N_PAIRS, SEQ, LANES), MXU_DTYPE),
                   jax.ShapeDtypeStruct((SEQ, KVX_W), MXU_DTYPE),
                   jax.ShapeDtypeStruct((4, SEQ, SGU_W), F32),
                   jax.ShapeDtypeStruct((D_MODEL, D_MODEL), COMM_DTYPE)),
        scratch_shapes=[pltpu.VMEM((D_MODEL, D_MODEL), COMM_DTYPE)] + _dma_sems(GATHER_SEMS),
        compiler_params=_params(("arbitrary",), VMEM_LIMIT),
    )(h, b_in, win_t, wout_shard)


def _window_mask(n):
    qi = lax.broadcasted_iota(jnp.int32, (2 * BLOCK, 2 * BLOCK), 0) & (BLOCK - 1)
    p = lax.broadcasted_iota(jnp.int32, (2 * BLOCK, 2 * BLOCK), 1) - BLOCK
    in_window = jnp.logical_and(p <= qi, p > qi - BLOCK)
    return jnp.logical_and(in_window, jnp.logical_or(p >= 0, n > 0))


def _sink_column(sink_ref, g, par):
    return jnp.concatenate([jnp.full((BLOCK, 1), sink_ref[4 * g + par], F32),
                            jnp.full((BLOCK, 1), sink_ref[4 * g + 2 + par], F32)], axis=0)


def _kv_cat(kp_ref, kc_ref, var, with_ones):
    kcol, vcol = var * LANES, (var + (8 if with_ones else 4)) * LANES
    return (jnp.concatenate([kp_ref[:, kcol:kcol + LANES], kc_ref[:, kcol:kcol + LANES]], axis=0),
            jnp.concatenate([kp_ref[:, vcol:vcol + LANES], kc_ref[:, vcol:vcol + LANES]], axis=0))


def _softmax_numerator(s, sink):
    m = jnp.maximum(jnp.max(s, axis=1, keepdims=True), sink)
    return jnp.exp(s - m), m


def _mixers_out_proj(sinks, q, kvx, gates, ln_g, ln_b, sgu_w, bias_full, x, target, wout, b_out, final_g):
    tm = FUSED_BLOCKS * BLOCK
    n_tiles = SEQ // tm

    def body(sink_ref, q_ref, kc_ref, za_ref, us_ref, vs_ref, zs_ref, lng_ref, lnb_ref, w_ref, bias_ref,
             x_ref, t_ref, wout_ref, b_ref, gf_ref,
             out_ref, gres_ref, dmix_ref, gw_ref, vec_ref,
             kp_ref, wm_ref, mixed_next, mixed_cur, out_stage, gb_ref):
        step = pl.program_id(0)

        @pl.when(step == 0)
        def _():
            kp_ref[...] = jnp.zeros_like(kp_ref)
            _mask_sgu_weights(w_ref, wm_ref)
            gw_ref[...] = jnp.zeros_like(gw_ref)
            vec_ref[...] = jnp.zeros_like(vec_ref)
            mixed_cur[...] = jnp.zeros_like(mixed_cur)

        def mixers_block(b, after_chain=()):
            rows = slice(b * BLOCK, (b + 1) * BLOCK)
            kc = kc_ref.at[rows, :]
            u, _, _, vln = _sgu_activations(us_ref[rows, :], vs_ref[rows, :], lng_ref[...], lnb_ref[...])

            valid = _window_mask(step * FUSED_BLOCKS + b)[0:BLOCK]
            chains = [(g, par, i) for g in range(2) for par in range(2) for i in range(2)]
            kv = {(g, par): _kv_cat(kp_ref, kc, 2 * g + par, True) for g in range(2) for par in range(2)}
            scores, outs = {}, {}

            def issue_scores(k):
                g, par, i = chains[k]
                scores[k] = _dot(q_ref[2 * g + i, rows, :], kv[g, par][0], NT)

            ahead = ATTN_FWD_AHEAD
            for k in range(ahead):
                issue_scores(k)
            low = lax.broadcasted_iota(jnp.int32, (BLOCK, LANES), 1) < HALF
            for k, (g, par, i) in enumerate(chains):
                sink = sink_ref[4 * g + 2 * i + par]
                e, m = _softmax_numerator(jnp.where(valid, scores[k], NEG_INF), sink)
                if k + ahead < len(chains):
                    issue_scores(k + ahead)
                o = _dot(e, kv[g, par][1])
                outs[g, par, i] = o / (pltpu.roll(o, HALF, 1) + jnp.exp(sink - m))
                if k == SGU_MIX_AFTER_CHAIN:
                    mixed = _sgu_mix(vln, wm_ref, bias_ref)
                if k % 2 == 0 and k // 2 < len(after_chain):
                    after_chain[k // 2]()
            for pair in range(N_PAIRS):
                g, i = divmod(pair, 2)
                lanes = slice(pair * LANES, (pair + 1) * LANES)
                o = jnp.where(low, outs[g, 0, i], outs[g, 1, i])
                out_stage[pair, rows, :] = o
                gate, _ = _silu_and_grad(za_ref[rows, lanes])
                mixed_next[rows, lanes] = (o * gate).astype(MXU_DTYPE)
            kp_ref[...] = kc[...]
            for pair in range(N_SGU_HEADS // 2):
                cols = slice(pair * LANES, (pair + 1) * LANES)
                gate, _ = _silu_and_grad(zs_ref[rows, cols])
                mixed_next[rows, ATTN_W + pair * LANES:ATTN_W + (pair + 1) * LANES] = (
                    u[:, cols] * mixed[pair] * gate).astype(MXU_DTYPE)

        live = (step > 0).astype(F32)
        quarter = D_MODEL // 4
        columns = [None] * 4

        def project(j):
            def piece():
                columns[j] = _dot(mixed_cur[...], wout_ref[:, j * quarter:(j + 1) * quarter])
            return piece

        half_blocks = FUSED_BLOCKS // 2
        per_block = 4 // half_blocks
        for b in range(half_blocks):
            mixers_block(b, [project(j) for j in range(b * per_block, (b + 1) * per_block)])
        xo = x_ref[...] + jnp.concatenate(columns, axis=1) + b_ref[...]
        r = lax.rsqrt(jnp.mean(xo * xo, axis=-1, keepdims=True) + NORM_EPS)
        xn = xo * r
        gf = gf_ref[...]
        err = xn * gf - t_ref[...]
        loss = 0.5 * jnp.sum(jnp.mean(err * err, axis=-1, keepdims=True), axis=0, keepdims=True)
        dy = err * (1.0 / D_MODEL)
        dxn = dy * gf
        gres = r * (dxn - xn * jnp.mean(dxn * xn, axis=-1, keepdims=True))
        vec_ref[0:1, :] += jnp.broadcast_to(loss * live, (1, D_MODEL))
        vec_ref[1:2, :] += jnp.sum(dy * xn, axis=0, keepdims=True) * live
        vec_ref[2:3, :] += jnp.sum(gres, axis=0, keepdims=True) * live
        gres_ref[...] = gres
        gb_ref[...] = gres.astype(MXU_DTYPE)

        def branch_grad(k):
            def piece():
                dmix_ref[k] = _dot(gb_ref[...], wout_ref[k * ATTN_W:(k + 1) * ATTN_W, :], NT)
            return piece

        def weight_grad(k):
            def piece():
                rows = slice(k * ATTN_W, (k + 1) * ATTN_W)
                gw_ref[rows, :] += _dot(mixed_cur[:, rows], gb_ref[...], TN)
            return piece

        backward = [branch_grad(0), branch_grad(1), weight_grad(0), weight_grad(1)]
        for b in range(half_blocks):
            mixers_block(half_blocks + b, backward[b * per_block:(b + 1) * per_block])

        @pl.when(step < n_tiles)
        def _():
            out_ref[...] = out_stage[...]

        mixed_cur[...] = mixed_next[...]

    ahead_tile = lambda i: jnp.minimum(i, n_tiles - 1)
    behind_tile = lambda i: jnp.maximum(i - 1, 0)
    blk = lambda w: pl.BlockSpec((tm, w), lambda i: (ahead_tile(i), 0))
    tiles = pl.BlockSpec((N_PAIRS, tm, LANES), lambda i: (0, ahead_tile(i), 0))
    gate = lambda k: pl.BlockSpec((None, tm, SGU_W), lambda i: (k, ahead_tile(i), 0))
    behind = lambda w: pl.BlockSpec((tm, w), lambda i: (behind_tile(i), 0))
    return pl.pallas_call(
        body,
        name="mixers_out_proj",
        grid=(n_tiles + 1,),
        in_specs=[pl.BlockSpec(memory_space=pltpu.SMEM), tiles, blk(KVX_W), gate(0), gate(1), gate(2), gate(3),
                  _full((1, SGU_W)), _full((1, SGU_W)), _full((N_SGU_HEADS, BLOCK, BLOCK)), _full((BLOCK, SGU_W)),
                  behind(D_MODEL), behind(D_MODEL), _full((D_MODEL, D_MODEL)), _full((1, D_MODEL)),
                  _full((1, D_MODEL))],
        out_specs=(tiles, behind(D_MODEL), pl.BlockSpec((2, tm, ATTN_W), lambda i: (0, behind_tile(i), 0)),
                   _full((D_MODEL, D_MODEL)), _full((8, D_MODEL))),
        out_shape=(jax.ShapeDtypeStruct((N_PAIRS, SEQ, LANES), F32),
                   jax.ShapeDtypeStruct((SEQ, D_MODEL), F32),
                   jax.ShapeDtypeStruct((2, SEQ, ATTN_W), F32),
                   jax.ShapeDtypeStruct((D_MODEL, D_MODEL), F32),
                   jax.ShapeDtypeStruct((8, D_MODEL), F32)),
        scratch_shapes=[pltpu.VMEM((BLOCK, KVX_W), MXU_DTYPE), pltpu.VMEM((N_SGU_HEADS, BLOCK, BLOCK), MXU_DTYPE),
                        pltpu.VMEM((tm, D_MODEL), MXU_DTYPE), pltpu.VMEM((tm, D_MODEL), MXU_DTYPE),
                        pltpu.VMEM((N_PAIRS, tm, LANES), F32), pltpu.VMEM((tm, D_MODEL), MXU_DTYPE)],
        compiler_params=_params(("arbitrary",), VMEM_LIMIT),
    )(sinks, q, kvx, gates, gates, gates, gates, ln_g, ln_b, sgu_w, bias_full, x, target, wout, b_out, final_g)


def _sgu_activations(us, vs, lng, lnb):
    u = _gelu(us)
    vg = _gelu(vs)
    mu = jnp.mean(vg, axis=-1, keepdims=True)
    xc = vg - mu
    rstd = lax.rsqrt(jnp.mean(xc * xc, axis=-1, keepdims=True) + NORM_EPS)
    vhat = xc * rstd
    return u, vhat, rstd, vhat * lng + lnb


def _mask_sgu_weights(w_ref, masked_ref, transposed_ref=None):
    tril = (lax.broadcasted_iota(jnp.int32, (BLOCK, BLOCK), 0)
            >= lax.broadcasted_iota(jnp.int32, (BLOCK, BLOCK), 1))
    for hh in range(N_SGU_HEADS):
        w = jnp.where(tril, w_ref[hh], 0.0)
        masked_ref[hh] = w.astype(MXU_DTYPE)
        if transposed_ref is not None:
            transposed_ref[hh] = w.T.astype(MXU_DTYPE)


def _sgu_mix(vln, masked_w_ref, bias_ref):
    low = lax.broadcasted_iota(jnp.int32, (BLOCK, LANES), 1) < HALF
    mixed = []
    for pair in range(N_SGU_HEADS // 2):
        vp = vln[:, pair * LANES:(pair + 1) * LANES]
        mixed.append(_dot(masked_w_ref[2 * pair], jnp.where(low, vp, 0.0))
                     + _dot(masked_w_ref[2 * pair + 1], jnp.where(low, 0.0, vp))
                     + bias_ref[:, pair * LANES:(pair + 1) * LANES])
    return mixed


def _mixers_bwd(sinks, dmix, q, kvx, out, gates, ln_g, ln_b, sgu_w, bias_full, gwout):
    last = N_BLOCKS - 1

    def body(sink_ref, d_ref, q_ref, kc_ref, o_ref, za_ref, dsg_ref, us_ref, vs_ref, zs_ref, lng_ref, lnb_ref, w_ref,
             bias_ref, gwout_ref,
             dp_ref, gsink_ref, gbin_ref, dps_ref, gw_ref, gb_ref, gln_ref, gbins_ref, wout_shard_ref,
             kp_ref, pend_ref, carry_ref, wm_ref, wt_ref, gbias_ref, sa_w, ra_w, sb_w, rc_w, send_sems, recv_sems):
        n = pl.program_id(0)
        start, exchange, finish = _reduce_scatter_plan(_Copies(send_sems, recv_sems), 0, gwout_ref, WOUT_ROWS,
                                                       sa_w, ra_w, sb_w, rc_w, wout_shard_ref)
        tril = (lax.broadcasted_iota(jnp.int32, (BLOCK, BLOCK), 0)
                >= lax.broadcasted_iota(jnp.int32, (BLOCK, BLOCK), 1))

        @pl.when(n == 0)
        def _():
            gsink_ref[...] = jnp.zeros_like(gsink_ref)
            gbin_ref[...] = jnp.zeros_like(gbin_ref)
            carry_ref[...] = jnp.zeros_like(carry_ref)
            kp_ref[...] = jnp.zeros_like(kp_ref)
            gw_ref[...] = jnp.zeros_like(gw_ref)
            gln_ref[...] = jnp.zeros_like(gln_ref)
            gbins_ref[...] = jnp.zeros_like(gbins_ref)
            gbias_ref[...] = jnp.zeros_like(gbias_ref)
            _mask_sgu_weights(w_ref, wm_ref, wt_ref)
            start()

        pl.when(n == 3)(exchange)
        pl.when(n == 12)(finish)

        @pl.when(n > 0)
        def _():
            dp_ref[:, 0:ATTN_W] = pend_ref[:, 0:ATTN_W]
            dp_ref[:, GATE0:ATTN_SECTION] = pend_ref[:, ATTN_W:]

        @pl.when(n > last)
        def _():
            dp_ref[:, KV0:GATE0] = carry_ref[...].astype(MXU_DTYPE)

        @pl.when(n <= last)
        def _():
            us = us_ref[...]
            vs = vs_ref[...]
            lng = lng_ref[...]
            u, vhat, rstd, vln = _sgu_activations(us, vs, lng, lnb_ref[...])
            low_sgu = lax.broadcasted_iota(jnp.int32, (BLOCK, LANES), 1) < HALF
            sgu = {}

            def sgu_gates():
                mixed = _sgu_mix(vln, wm_ref, bias_ref)
                sgu["du"], sgu["dzs"], sgu["dm"] = [], [], []
                for pair in range(N_SGU_HEADS // 2):
                    cols = slice(pair * LANES, (pair + 1) * LANES)
                    dsg = dsg_ref[:, cols]
                    gate, gate_grad = _silu_and_grad(zs_ref[:, cols])
                    up = u[:, cols]
                    sgu["du"].append(dsg * mixed[pair] * gate)
                    sgu["dzs"].append(dsg * up * mixed[pair] * gate_grad)
                    dmixed = dsg * up * gate
                    gbias_ref[:, cols] += dmixed
                    sgu["dm"].append((jnp.where(low_sgu, dmixed, 0.0).astype(MXU_DTYPE),
                                      jnp.where(low_sgu, 0.0, dmixed).astype(MXU_DTYPE)))

            def sgu_grads():
                dvln_parts = []
                for pair in range(N_SGU_HEADS // 2):
                    dm_lo, dm_hi = sgu["dm"][pair]
                    vp = vln[:, pair * LANES:(pair + 1) * LANES]
                    gw_ref[2 * pair] += _dot(dm_lo, vp, NT)
                    gw_ref[2 * pair + 1] += _dot(dm_hi, vp, NT)
                    dvln_parts.append(_dot(wt_ref[2 * pair], dm_lo) + _dot(wt_ref[2 * pair + 1], dm_hi))
                dvln = jnp.concatenate(dvln_parts, axis=1)
                gln_ref[0:1, :] += jnp.sum(dvln * vhat, axis=0, keepdims=True)
                gln_ref[1:2, :] += jnp.sum(dvln, axis=0, keepdims=True)
                dvhat = dvln * lng
                dvg = rstd * (dvhat - jnp.mean(dvhat, axis=-1, keepdims=True)
                              - vhat * jnp.mean(dvhat * vhat, axis=-1, keepdims=True))
                dus = jnp.concatenate(sgu["du"], axis=1) * _gelu_grad(us)
                dvs = dvg * _gelu_grad(vs)
                dzs = jnp.concatenate(sgu["dzs"], axis=1)
                for k, val in enumerate((dus, dvs, dzs)):
                    dps_ref[:, k * SGU_W:(k + 1) * SGU_W] = val.astype(MXU_DTYPE)
                    gbins_ref[:, k * SGU_W:(k + 1) * SGU_W] += jnp.sum(val, axis=0, keepdims=True)

            valid = _window_mask(n)[0:BLOCK]
            low = lax.broadcasted_iota(jnp.int32, (BLOCK, LANES), 1) < HALF
            low_keys = lax.broadcasted_iota(jnp.int32, (2 * BLOCK, LANES), 1) < HALF
            lane_row = lax.broadcasted_iota(jnp.int32, (1, LANES), 1)
            gsink = jnp.zeros((1, LANES), F32)
            chains = [(g, par, i) for g in range(2) for par in range(2) for i in range(2)]
            kv = {(g, par): _kv_cat(kp_ref, kc_ref, 2 * g + par, False) for g in range(2) for par in range(2)}
            ones_keys = jnp.ones((2 * BLOCK, LANES), MXU_DTYPE)
            half_of_lane = lax.broadcasted_iota(jnp.int32, (LANES, 2 * LANES), 0) // HALF
            half_of_col = lax.broadcasted_iota(jnp.int32, (LANES, 2 * LANES), 1) // LANES
            sum_halves = (half_of_lane == half_of_col).astype(MXU_DTYPE)
            douts, deltas = [], []
            for pair in range(N_PAIRS):
                lanes = slice(pair * LANES, (pair + 1) * LANES)
                dg = d_ref[:, lanes]
                gate, gate_grad = _silu_and_grad(za_ref[:, lanes])
                o = o_ref[pair]
                dout = dg * gate
                dza = dg * o * gate_grad
                douts.append(dout.astype(MXU_DTYPE))
                deltas.append(_dot(dout * o, sum_halves))
                zl = slice(ATTN_W + pair * LANES, ATTN_W + (pair + 1) * LANES)
                pend_ref[:, zl] = dza.astype(MXU_DTYPE)
                gl = slice(GATE0 + pair * LANES, GATE0 + (pair + 1) * LANES)
                gbin_ref[:, gl] += jnp.sum(dza, axis=0, keepdims=True)

            first = {}

            def issue_first(k):
                g, par, i = chains[k]
                first[k] = (_dot(q_ref[2 * g + i], kv[g, par][0], NT), _dot(douts[2 * g + i], kv[g, par][1], NT))

            numerators = {}

            def issue_row_sums(k):
                g, par, i = chains[k]
                sink = sink_ref[4 * g + 2 * i + par]
                e, m = _softmax_numerator(jnp.where(valid, first[k][0], NEG_INF), sink)
                numerators[k] = (e, jnp.exp(sink - m), _dot(e, ones_keys))

            ahead = ATTN_BWD_AHEAD
            for k in range(ahead):
                issue_first(k)
            issue_row_sums(0)
            issue_row_sums(1)
            dqs, dk_parts, dv_parts = {}, {}, {}
            operands = {}

            def issue_last(k):
                g, par, i = chains[k]
                ds, ds_t, p_t = operands.pop(k)
                dq = _dot(ds, kv[g, par][0])
                dqs[g, i] = dq if par == 0 else dqs[g, i] + dq
                dk = _dot(ds_t, q_ref[2 * g + i])
                dv = _dot(p_t, douts[2 * g + i])
                dk_parts[g, par] = dk if i == 0 else dk_parts[g, par] + dk
                dv_parts[g, par] = dv if i == 0 else dv_parts[g, par] + dv

            for k, (g, par, i) in enumerate(chains):
                h = 4 * g + 2 * i + par
                delta = deltas[2 * g + i][:, par * LANES:(par + 1) * LANES]
                e, at_sink, row_sum = numerators[k]
                inv = 1.0 / (row_sum + at_sink)
                p = e * jnp.tile(inv, (1, 2))
                ds = p * (first[k][1] - jnp.tile(delta, (1, 2)))
                ds = ds.astype(MXU_DTYPE)
                operands[k] = (ds, ds.T, p.astype(MXU_DTYPE).T)
                total = jnp.sum(at_sink * inv * delta, axis=0, keepdims=True)
                gsink = jnp.where(lane_row == h, -total, gsink)
                if k + ahead < len(chains):
                    issue_first(k + ahead)
                if k + 2 < len(chains):
                    issue_row_sums(k + 2)
                if k > 0:
                    issue_last(k - 1)
                if k == SGU_GATES_AFTER_CHAIN:
                    sgu_gates()
                if k == SGU_GRADS_AFTER_CHAIN:
                    sgu_grads()
            issue_last(len(chains) - 1)
            for pair in range(N_PAIRS):
                g, i = divmod(pair, 2)
                dq = dqs[g, i] * SCALE
                lanes = slice(pair * LANES, (pair + 1) * LANES)
                pend_ref[:, lanes] = dq.astype(MXU_DTYPE)
                gbin_ref[:, lanes] += jnp.sum(dq, axis=0, keepdims=True)
            gsink_ref[...] += gsink
            for k, parts in enumerate((dk_parts, dv_parts)):
                masked = {key: jnp.where(low_keys if key[1] == 0 else jnp.logical_not(low_keys), val, 0.0)
                          for key, val in parts.items()}
                both = (masked[0, 0] + masked[1, 1]
                        + pltpu.roll(masked[0, 1] + masked[1, 0], HALF, 1))
                lanes = slice(k * KV_W, (k + 1) * KV_W)
                done = carry_ref[:, lanes] + both[0:BLOCK]
                dp_ref[:, KV0 + k * KV_W:KV0 + (k + 1) * KV_W] = done.astype(MXU_DTYPE)
                carry_ref[:, lanes] = both[BLOCK:]
                gbin_ref[:, KV0 + k * KV_W:KV0 + (k + 1) * KV_W] += jnp.sum(both, axis=0, keepdims=True)
            kp_ref[...] = kc_ref[...]

        @pl.when(n == last)
        def _():
            for hh in range(N_SGU_HEADS):
                gw_ref[hh] = jnp.where(tril, gw_ref[hh], 0.0)
            head_of_lane = lax.broadcasted_iota(jnp.int32, (N_SGU_HEADS, SGU_W), 1) // HEAD_DIM
            select = (head_of_lane == lax.broadcasted_iota(jnp.int32, (N_SGU_HEADS, SGU_W), 0)).astype(F32)
            gb_ref[...] = lax.dot_general(select, gbias_ref[...], NT, precision=lax.Precision.HIGHEST,
                                          preferred_element_type=F32)

    at = lambda n: jnp.minimum(n, last)
    blk = lambda w: pl.BlockSpec((BLOCK, w), lambda n: (at(n), 0))
    tiles = pl.BlockSpec((N_PAIRS, BLOCK, LANES), lambda n: (0, at(n), 0))
    section = lambda k: pl.BlockSpec((None, BLOCK, SGU_W), lambda n: (k, at(n), 0))
    return pl.pallas_call(
        body,
        name="mixers_bwd",
        grid=(N_BLOCKS + 1,),
        in_specs=[pl.BlockSpec(memory_space=pltpu.SMEM),
                  section(0),
                  tiles,
                  blk(KVX_W),
                  tiles,
                  section(0),
                  section(1),
                  section(1), section(2), section(3),
                  _full((1, SGU_W)), _full((1, SGU_W)), _full((N_SGU_HEADS, BLOCK, BLOCK)), _full((BLOCK, SGU_W)),
                  VMEM_SPEC],
        out_specs=(pl.BlockSpec((BLOCK, ATTN_SECTION), lambda n: (jnp.maximum(n - 1, 0), 0)),
                   _full((1, LANES)), _full((1, ATTN_SECTION)),
                   pl.BlockSpec((BLOCK, SGU_SECTION), lambda n: (at(n), 0)),
                   _full((N_SGU_HEADS, BLOCK, BLOCK)), _full((N_SGU_HEADS, BLOCK)),
                   _full((8, SGU_W)), _full((1, SGU_SECTION)), VMEM_SPEC),
        out_shape=(jax.ShapeDtypeStruct((SEQ, ATTN_SECTION), MXU_DTYPE),
                   jax.ShapeDtypeStruct((1, LANES), F32),
                   jax.ShapeDtypeStruct((1, ATTN_SECTION), F32),
                   jax.ShapeDtypeStruct((SEQ, SGU_SECTION), MXU_DTYPE),
                   jax.ShapeDtypeStruct((N_SGU_HEADS, BLOCK, BLOCK), F32),
                   jax.ShapeDtypeStruct((N_SGU_HEADS, BLOCK), F32),
                   jax.ShapeDtypeStruct((8, SGU_W), F32),
                   jax.ShapeDtypeStruct((1, SGU_SECTION), F32),
                   jax.ShapeDtypeStruct((WOUT_ROWS, D_MODEL), F32)),
        scratch_shapes=([pltpu.VMEM((BLOCK, KVX_W), MXU_DTYPE),
                         pltpu.VMEM((BLOCK, 2 * ATTN_W), MXU_DTYPE), pltpu.VMEM((BLOCK, 2 * KV_W), F32),
                         pltpu.VMEM((N_SGU_HEADS, BLOCK, BLOCK), MXU_DTYPE),
                         pltpu.VMEM((N_SGU_HEADS, BLOCK, BLOCK), MXU_DTYPE), pltpu.VMEM((BLOCK, SGU_W), F32)]
                        + _reduce_scatter_scratch(WOUT_ROWS, D_MODEL, COMM_DTYPE) + _dma_sems(REDUCE_SEMS)),
        compiler_params=_params(("arbitrary",), VMEM_LIMIT),
    )(sinks, dmix, q, kvx, out, gates, dmix, gates, gates, gates, ln_g, ln_b, sgu_w, bias_full, gwout)


def _in_proj_bwd(dpa, dps, win_t, x, norm_g, gres, gwin_own, gwin_others, vec_parts):
    tm = TOKEN_TILE
    steps = SEQ // tm
    n_parts = len(vec_parts)

    def body(da_ref, ds_ref, w_ref, x_ref, g_ref, gres_ref, own_ref, others_ref, *rest):
        part_refs = rest[:n_parts]
        gx_ref, shard_ref, vec_out_ref, gng_ref, rc, vec_ref, ra_vec, slots, send_sems, recv_sems = rest[n_parts:]
        step = pl.program_id(0)
        copies = _Copies(send_sems, recv_sems)
        send, finish = _owner_sums_plan(copies, 0, own_ref, others_ref, rc, shard_ref)

        @pl.when(step == 0)
        def _():
            gng_ref[...] = jnp.zeros_like(gng_ref)
            send()

        dh = _dot(da_ref[...], w_ref[0:ATTN_SECTION, :]) + _dot(ds_ref[...], w_ref[ATTN_SECTION:, :])
        xv = x_ref[...]
        r = lax.rsqrt(jnp.mean(xv * xv, axis=-1, keepdims=True) + NORM_EPS)
        xn = xv * r
        gng_ref[...] += jnp.sum(dh * xn, axis=0, keepdims=True)
        dxn = dh * g_ref[...]
        gx_ref[...] = r * (dxn - xn * jnp.mean(dxn * xn, axis=-1, keepdims=True)) + gres_ref[...]

        @pl.when(step == steps - 1)
        def _():
            finish()
            _all_reduce_vectors(copies, OWNER_SEMS, gng_ref, *part_refs, vec_out_ref, vec_ref, ra_vec, slots)

    tile = lambda w: pl.BlockSpec((tm, w), lambda i: (i, 0))
    return pl.pallas_call(
        body,
        name="in_proj_bwd",
        grid=(steps,),
        in_specs=[tile(ATTN_SECTION), tile(SGU_SECTION), _full((IN_W, D_MODEL)), tile(D_MODEL),
                  _full((1, D_MODEL)), tile(D_MODEL), VMEM_SPEC, VMEM_SPEC] + [VMEM_SPEC] * n_parts,
        out_specs=(tile(D_MODEL), VMEM_SPEC, VMEM_SPEC),
        out_shape=(jax.ShapeDtypeStruct((SEQ, D_MODEL), F32),
                   jax.ShapeDtypeStruct((WIN_ROWS, D_MODEL), F32),
                   jax.ShapeDtypeStruct((VEC_ROWS, IN_W), F32)),
        scratch_shapes=([pltpu.VMEM((1, D_MODEL), F32), pltpu.VMEM((3 * WIN_ROWS, D_MODEL), COMM_DTYPE)]
                        + _vector_scratch() + _dma_sems(OWNER_SEMS + VECTOR_SEMS)),
        input_output_aliases={5: 0},
        compiler_params=_params(("arbitrary",), VMEM_LIMIT),
    )(dpa, dps, win_t, x, norm_g, gres, gwin_own, gwin_others, *vec_parts)


def _win_grad_pieces(rows):
    pieces = []
    for step in range(IN_W // rows):
        for owner in range(N_DEV):
            lo, hi = max(step * rows, owner * WIN_ROWS), min((step + 1) * rows, (owner + 1) * WIN_ROWS)
            if lo < hi:
                pieces.append((len(pieces), step, owner, lo, hi - lo))
    return pieces


def _win_grad(dpa, dps, h, gsguw):
    rows = 256
    n_attn = ATTN_SECTION // rows
    steps = IN_W // rows
    pieces = _win_grad_pieces(rows)
    class_rows = (N_DEV // 2) * WIN_ROWS

    def body(da_ref, ds_ref, h_ref, gsguw_ref, own_ref, others_ref, sguw_full_ref,
             chunks, sa, ra, sa_s, ra_s, sb_s, rc_s, landing, send_sems, recv_sems, give_sems, take_sems):
        step = pl.program_id(0)
        x, y, c = _place()
        copies = _Copies(send_sems, recv_sems)
        own_sguw = landing.at[_block_rows((x, y, c), SGUW_ROWS), :]
        start, exchange, finish = _reduce_scatter_plan(copies, 0, gsguw_ref, SGUW_ROWS, sa_s, ra_s, sb_s, rc_s,
                                                       own_sguw)
        gather = _gather_plan(copies, REDUCE_SEMS, landing, SGUW_ROWS)

        def class_rows_of(owner, first, n):
            return pl.ds((owner // 2) * WIN_ROWS + first - owner * WIN_ROWS, n)

        def to_sibling(piece):
            k, _, owner, first, n = piece
            at = class_rows_of(owner, first, n)
            return pltpu.make_async_remote_copy(src_ref=sa.at[at, :], dst_ref=ra.at[at, :], send_sem=give_sems.at[k],
                                                recv_sem=take_sems.at[k], device_id=(x, y, 1 - c), device_id_type=MESH)

        def give(piece):
            k, at_step, owner, first, n = piece

            @pl.when(c != owner % 2)
            def _():
                sa[class_rows_of(owner, first, n), :] = chunks[at_step % 2, pl.ds(first % rows, n), :].astype(sa.dtype)
                to_sibling(piece).start()

        def keep(piece):
            k, at_step, owner, first, n = piece
            px, py = owner // 4, (owner // 2) % 2

            @pl.when(c == owner % 2)
            def _():
                to_sibling(piece).wait_recv()
                total = (chunks[at_step % 2, pl.ds(first % rows, n), :]
                         + ra[class_rows_of(owner, first, n), :].astype(F32))
                relation = (x + px - 2 * x * px) + 2 * (y + py - 2 * y * py)

                @pl.when(relation == 0)
                def _():
                    own_ref[pl.ds(first - owner * WIN_ROWS, n), :] = total

                @pl.when(relation != 0)
                def _():
                    at = pl.multiple_of((relation - 1) * WIN_ROWS + first - owner * WIN_ROWS, 16)
                    others_ref[pl.ds(at, n), :] = total.astype(others_ref.dtype)

        pl.when(step == 0)(start)
        pl.when(step == 2)(exchange)

        @pl.when(step == 5)
        def _():
            finish()
            gather[0]()

        pl.when(step == 7)(gather[1])

        @pl.when(step < n_attn)
        def _():
            chunks[step % 2] = _dot(da_ref[...], h_ref[...], TN)

        @pl.when(step >= n_attn)
        def _():
            chunks[step % 2] = _dot(ds_ref[...], h_ref[...], TN)

        for at_step in range(steps):
            @pl.when(step == at_step)
            def _():
                for piece in pieces:
                    if piece[1] == at_step:
                        give(piece)
                    if piece[1] == at_step - 1:
                        keep(piece)

        @pl.when(step == steps - 1)
        def _():
            for piece in pieces:
                if piece[1] == steps - 1:
                    keep(piece)
            for piece in pieces:
                pl.when(c != piece[2] % 2)(to_sibling(piece).wait_send)
            gather[2]()
            sguw_full_ref[...] = landing[...]

    return pl.pallas_call(
        body,
        name="win_grad",
        grid=(steps,),
        in_specs=[pl.BlockSpec((SEQ, rows), lambda i: (0, jnp.minimum(i, n_attn - 1))),
                  pl.BlockSpec((SEQ, rows), lambda i: (0, jnp.maximum(i - n_attn, 0))),
                  _full((SEQ, D_MODEL)), VMEM_SPEC],
        out_specs=(VMEM_SPEC, VMEM_SPEC, _full((N_SGU_HEADS * BLOCK, BLOCK))),
        out_shape=(jax.ShapeDtypeStruct((WIN_ROWS, D_MODEL), F32),
                   jax.ShapeDtypeStruct((3 * WIN_ROWS, D_MODEL), COMM_DTYPE),
                   jax.ShapeDtypeStruct((N_SGU_HEADS * BLOCK, BLOCK), F32)),
        scratch_shapes=([pltpu.VMEM((2, rows, D_MODEL), F32),
                         pltpu.VMEM((class_rows, D_MODEL), COMM_DTYPE), pltpu.VMEM((class_rows, D_MODEL), COMM_DTYPE)]
                        + _reduce_scatter_scratch(SGUW_ROWS, BLOCK, F32)
                        + [pltpu.VMEM((N_SGU_HEADS * BLOCK, BLOCK), F32)]
                        + _dma_sems(REDUCE_SEMS + GATHER_SEMS) + _dma_sems(len(pieces))),
        compiler_params=_params(("arbitrary",), VMEM_LIMIT),
    )(dpa, dps, h, gsguw)


VEC_NORM_G, VEC_B_IN, VEC_SINKS, VEC_LN_G, VEC_LN_B, VEC_B_OUT, VEC_FINAL_G, VEC_LOSS, VEC_SGU_B = 0, 1, 2, 3, 4, 5, 6, 7, 8


def _adamw(w, g, m, v):
    m = ADAM_B1 * m + (1.0 - ADAM_B1) * g
    v = ADAM_B2 * v + (1.0 - ADAM_B2) * (g * g)
    m_hat = m / (1.0 - ADAM_B1 ** ADAM_STEP)
    v_hat = v / (1.0 - ADAM_B2 ** ADAM_STEP)
    delta = -ADAM_LR * (m_hat / (jnp.sqrt(v_hat) + ADAM_EPS) + ADAM_WD * w)
    return delta, m, v


def _adamw_shard(name, g, w, m, v, block_rows):
    def body(g_ref, w_ref, m_ref, v_ref, d_ref, nm_ref, nv_ref):
        d_ref[...], nm_ref[...], nv_ref[...] = _adamw(w_ref[...], g_ref[...], m_ref[...], v_ref[...])

    rows, cols = w.shape
    spec = pl.BlockSpec((block_rows, cols), lambda i: (i, 0))
    return pl.pallas_call(
        body,
        name=name,
        grid=(rows // block_rows,),
        in_specs=[spec] * 4,
        out_specs=(spec,) * 3,
        out_shape=(jax.ShapeDtypeStruct(w.shape, F32),) * 3,
        compiler_params=_params(("arbitrary",)),
    )(g, w, m, v)


VECTOR_SEMS = 4


def _vector_scratch():
    return [pltpu.VMEM((VEC_ROWS, IN_W), F32), pltpu.VMEM((VEC_ROWS, IN_W), F32),
            pltpu.VMEM((4 * VEC_ROWS, IN_W), F32)]


def _all_reduce_vectors(copies, sem0, gng_ref, gba_ref, gbs_ref, gsink_ref, gln_ref, gsgub_ref, vec4_ref, out_ref,
                        vec_ref, ra_vec, slots):
    x, y, c = _place()
    vec_ref[...] = jnp.zeros_like(vec_ref)
    vec_ref[VEC_NORM_G:VEC_NORM_G + 1, 0:D_MODEL] = gng_ref[...]
    vec_ref[VEC_B_IN:VEC_B_IN + 1, 0:ATTN_SECTION] = gba_ref[...]
    vec_ref[VEC_B_IN:VEC_B_IN + 1, ATTN_SECTION:IN_W] = gbs_ref[...]
    vec_ref[VEC_SINKS:VEC_SINKS + 1, 0:LANES] = gsink_ref[...]
    vec_ref[VEC_LN_G:VEC_LN_G + 1, 0:SGU_W] = gln_ref[0:1, :]
    vec_ref[VEC_LN_B:VEC_LN_B + 1, 0:SGU_W] = gln_ref[1:2, :]
    vec_ref[VEC_B_OUT:VEC_B_OUT + 1, 0:D_MODEL] = vec4_ref[2:3, :]
    vec_ref[VEC_FINAL_G:VEC_FINAL_G + 1, 0:D_MODEL] = vec4_ref[1:2, :]
    vec_ref[VEC_LOSS:VEC_LOSS + 1, 0:D_MODEL] = vec4_ref[0:1, :]
    vec_ref[VEC_SGU_B:VEC_SGU_B + N_SGU_HEADS, 0:BLOCK] = gsgub_ref[...]

    to_sibling = copies(sem0, vec_ref, ra_vec, (x, y, 1 - c))
    to_sibling.start()
    to_sibling.wait_recv()

    def chip_slot(place):
        return slots.at[pl.ds(pl.multiple_of((2 * place[0] + place[1]) * VEC_ROWS, 8), VEC_ROWS), :]

    mine = chip_slot((x, y))
    mine[...] = vec_ref[...] + ra_vec[...]
    to_chips = [copies(sem0 + i, mine, mine, (*_chip(rel), c)) for i, rel in enumerate(RELATIONS[1:], start=1)]
    for cp in to_chips:
        cp.start()
    for i, rel in enumerate(RELATIONS[1:], start=1):
        theirs = chip_slot(_chip(rel))
        copies(sem0 + i, theirs, theirs, (x, y, c)).wait_recv()
    out_ref[...] = ((slots[0:VEC_ROWS, :] + slots[VEC_ROWS:2 * VEC_ROWS, :])
                    + slots[2 * VEC_ROWS:3 * VEC_ROWS, :]) + slots[3 * VEC_ROWS:, :]
    to_sibling.wait_send()
    for cp in to_chips:
        cp.wait_send()


def _adamw_replicated(vec, gsguw, weights, m_state, v_state):
    n = len(SMALL)

    def body(*refs):
        vec_ref, gsguw_ref = refs[0], refs[1]
        w_refs, m_refs, v_refs = (refs[2 + k * n:2 + (k + 1) * n] for k in range(3))
        outs = refs[2 + 3 * n:]
        g_refs, d_refs, nm_refs, nv_refs = (outs[k * n:(k + 1) * n] for k in range(4))
        for i, (_, row, shape) in enumerate(SMALL):
            g = gsguw_ref[...] if row is None else vec_ref[row:row + shape[0], 0:shape[1]]
            g_refs[i][...] = g
            d_refs[i][...], nm_refs[i][...], nv_refs[i][...] = _adamw(
                w_refs[i][...], g, m_refs[i][...], v_refs[i][...])

    shapes = tuple(jax.ShapeDtypeStruct(shape, F32) for _, _, shape in SMALL)
    outs = pl.pallas_call(
        body,
        name="adamw_replicated",
        in_specs=[VMEM_SPEC] * (2 + 3 * n),
        out_specs=(VMEM_SPEC,) * (4 * n),
        out_shape=shapes * 4,
    )(vec, gsguw, *weights, *m_state, *v_state)
    return tuple(outs[k * n:(k + 1) * n] for k in range(4))


SMALL = (
    ("norm_g", VEC_NORM_G, (1, D_MODEL)),
    ("b_in", VEC_B_IN, (1, IN_W)),
    ("attn_sinks", VEC_SINKS, (1, N_Q_HEADS)),
    ("sgu_ln_g", VEC_LN_G, (1, SGU_W)),
    ("sgu_ln_b", VEC_LN_B, (1, SGU_W)),
    ("sgu_w", None, (N_SGU_HEADS * BLOCK, BLOCK)),
    ("sgu_b", VEC_SGU_B, (N_SGU_HEADS, BLOCK)),
    ("b_out", VEC_B_OUT, (1, D_MODEL)),
    ("final_norm_g", VEC_FINAL_G, (1, D_MODEL)),
)


def _local_grads(x, target, h, win_t, wout_shard, norm_g, b_in, attn_sinks, sgu_ln_g, sgu_ln_b, sgu_w, sgu_b, b_out,
                 final_g):
    sinks = attn_sinks.reshape(N_Q_HEADS)
    bias_full = jnp.repeat(sgu_b.T, HEAD_DIM, axis=1)
    q, kvx, gates, wout = _in_proj(h, b_in, win_t, wout_shard)
    out, gres, dmix, gwout, vec4 = _mixers_out_proj(sinks, q, kvx, gates, sgu_ln_g, sgu_ln_b, sgu_w, bias_full,
                                                    x, target, wout, b_out, final_g)
    dpa, gsink, gbin_a, dps, gsguw, gsgub, gln, gbin_s, gwout_shard = _mixers_bwd(
        sinks, dmix, q, kvx, out, gates, sgu_ln_g, sgu_ln_b, sgu_w, bias_full, gwout)
    gwin_own, gwin_others, gsguw_sum = _win_grad(dpa, dps, h, gsguw.reshape(N_SGU_HEADS * BLOCK, BLOCK))
    grad_x, gwin_shard, vec = _in_proj_bwd(dpa, dps, win_t, x, norm_g, gres, gwin_own, gwin_others,
                                           (gbin_a, gbin_s, gsink, gln, gsgub, vec4))
    return grad_x, gwin_shard, gwout_shard, gsguw_sum, vec


def kernel(x, norm_g, w_in, b_in, attn_sinks, sgu_ln_g, sgu_ln_b, sgu_w, sgu_b, w_out, b_out, final_norm_g, loss_target, m_norm_g, m_w_in, m_b_in, m_attn_sinks, m_sgu_ln_g, m_sgu_ln_b, m_sgu_w, m_sgu_b, m_w_out, m_b_out, m_final_norm_g, v_norm_g, v_w_in, v_b_in, v_attn_sinks, v_sgu_ln_g, v_sgu_ln_b, v_sgu_w, v_sgu_b, v_w_out, v_b_out, v_final_norm_g):
    given = dict(norm_g=norm_g, b_in=b_in, attn_sinks=attn_sinks, sgu_ln_g=sgu_ln_g, sgu_ln_b=sgu_ln_b,
                 sgu_w=sgu_w, sgu_b=sgu_b, b_out=b_out, final_norm_g=final_norm_g)
    m_given = dict(norm_g=m_norm_g, b_in=m_b_in, attn_sinks=m_attn_sinks, sgu_ln_g=m_sgu_ln_g,
                   sgu_ln_b=m_sgu_ln_b, sgu_w=m_sgu_w, sgu_b=m_sgu_b, b_out=m_b_out, final_norm_g=m_final_norm_g)
    v_given = dict(norm_g=v_norm_g, b_in=v_b_in, attn_sinks=v_attn_sinks, sgu_ln_g=v_sgu_ln_g,
                   sgu_ln_b=v_sgu_ln_b, sgu_w=v_sgu_w, sgu_b=v_sgu_b, b_out=v_b_out, final_norm_g=v_final_norm_g)

    win_t, h = _all_gather_win(w_in[0].T, x[0], norm_g)
    grad_x, gwin_t, gwout, gsguw, vec = _local_grads(
        x[0], loss_target[0], h, win_t, w_out[0], norm_g, b_in, attn_sinks, sgu_ln_g, sgu_ln_b, sgu_w[0], sgu_b[0],
        b_out, final_norm_g.reshape(1, D_MODEL))

    t = lambda a: a[0].T
    d_win, nm_win, nv_win = _adamw_shard("adamw_w_in", gwin_t, t(w_in), t(m_w_in), t(v_w_in), WIN_ROWS // 2)
    d_wout, nm_wout, nv_wout = _adamw_shard("adamw_w_out", gwout, w_out[0], m_w_out[0], v_w_out[0], WOUT_ROWS)
    as_2d = lambda d: [d[name].reshape(shape) for name, _, shape in SMALL]
    loss = vec[VEC_LOSS, 0]
    small = _adamw_replicated(vec, gsguw, as_2d(given), as_2d(m_given), as_2d(v_given))

    def assemble(big_in, big_out, k):
        vals = {name: small[k][i].reshape(given[name].shape) for i, (name, _, _) in enumerate(SMALL)}
        vals["w_in"] = big_in.T[None]
        vals["w_out"] = big_out[None]
        order = ("norm_g", "w_in", "b_in", "attn_sinks", "sgu_ln_g", "sgu_ln_b", "sgu_w", "sgu_b", "w_out",
                 "b_out", "final_norm_g")
        return [vals[name] for name in order]

    return (loss, grad_x[None],
            *assemble(gwin_t, gwout, 0), *assemble(d_win, d_wout, 1),
            *assemble(nm_win, nm_wout, 2), *assemble(nv_win, nv_wout, 3))
```
